```python
import jax, jax.numpy as jnp
from jax import lax
import numpy as np

D_MODEL = 1024
BATCH = 8
SEQ = 8192
DEPTH = 2

N_META = 16
MIX_WIDTH = D_MODEL
ATTN_WIDTH = MIX_WIDTH // 2
CONV_WIDTH = MIX_WIDTH - ATTN_WIDTH
HEAD_DIM = 64
N_Q_HEADS = ATTN_WIDTH // HEAD_DIM
N_KV_HEADS = 2
GROUP = N_Q_HEADS // N_KV_HEADS
KV_WIDTH = N_KV_HEADS * HEAD_DIM
CONV_GROUPS = 8
CONV_K = 3
WINDOW = 128
BLOCK = 128
LEAD_PAD = BLOCK - N_META
ROPE_THETA = 500000.0
ROT_DIM = HEAD_DIM // 4
D_FF = 4 * D_MODEL
IN_WIDTH = ATTN_WIDTH + 2 * KV_WIDTH + 3 * CONV_WIDTH
EPS = 1e-6

kernel_name = 'hymba_swa_sink_shortconv_sandwich'


def rmsnorm(x, g):
    x32 = x.astype(jnp.float32)
    y = x32 * lax.rsqrt(jnp.mean(x32 * x32, axis=-1, keepdims=True) + EPS)
    return y.astype(x.dtype) * g.astype(x.dtype)


def rope_tables(n_pos):
    pos = jnp.arange(n_pos, dtype=jnp.float32)
    inv_freq = jnp.power(jnp.float32(ROPE_THETA), -jnp.arange(0, ROT_DIM, 2, dtype=jnp.float32) / ROT_DIM)
    ang = pos[:, None] * inv_freq[None, :]
    return jnp.cos(ang), jnp.sin(ang)


def partial_rope(t, cos, sin):
    half = ROT_DIM // 2
    t32 = t[..., :ROT_DIM].astype(jnp.float32)
    t1, t2 = t32[..., :half], t32[..., half:]
    c, s = cos[None, :, None, :], sin[None, :, None, :]
    rot = jnp.concatenate([t1 * c - t2 * s, t2 * c + t1 * s], axis=-1).astype(t.dtype)
    return jnp.concatenate([rot, t[..., ROT_DIM:]], axis=-1)


def sliding_window_gqa_sinks(q, k, v, sink):
    bsz, L = q.shape[0], q.shape[1]
    pad = ((0, 0), (LEAD_PAD, 0), (0, 0), (0, 0))
    q, k, v = jnp.pad(q, pad), jnp.pad(k, pad), jnp.pad(v, pad)
    Lp = L + LEAD_PAD
    nb = Lp // BLOCK
    qb = q.reshape(bsz, nb, BLOCK, N_KV_HEADS, GROUP, HEAD_DIM)

    def band(t):
        tb = t.reshape(bsz, nb, BLOCK, N_KV_HEADS, HEAD_DIM)
        prev = jnp.pad(tb, ((0, 0), (1, 0), (0, 0), (0, 0), (0, 0)))[:, :-1]
        return jnp.concatenate([prev, tb], axis=2)

    kw, vw = band(k), band(v)
    s = jnp.einsum('bnqhgd,bnkhd->bnhgqk', qb, kw,
                   preferred_element_type=jnp.float32) * (HEAD_DIM ** -0.5)
    blk = jnp.arange(nb)[:, None, None]
    qpos = blk * BLOCK + jnp.arange(BLOCK)[None, :, None]
    kpos = (blk - 1) * BLOCK + jnp.arange(2 * BLOCK)[None, None, :]
    mask = (kpos <= qpos) & (qpos - kpos < WINDOW) & (kpos >= LEAD_PAD)
    s = jnp.where(mask[None, :, None, None], s, -jnp.inf)
    sk = sink.astype(jnp.float32).reshape(1, 1, N_KV_HEADS, GROUP, 1, 1)
    m = jnp.maximum(jnp.max(s, axis=-1, keepdims=True), sk)
    e = jnp.exp(s - m)
    p = e / (jnp.sum(e, axis=-1, keepdims=True) + jnp.exp(sk - m))
    o = jnp.einsum('bnhgqk,bnkhd->bnqhgd', p.astype(v.dtype), vw)
    return o.reshape(bsz, Lp, N_Q_HEADS * HEAD_DIM)[:, LEAD_PAD:]


def short_gated_conv(b_gate, c_gate, h, w):
    u = c_gate * h
    y = lax.conv_general_dilated(u, w[:, None, :].astype(u.dtype), window_strides=(1,),
                                 padding=[(CONV_K - 1, 0)],
                                 dimension_numbers=('NWC', 'WIO', 'NWC'),
                                 feature_group_count=CONV_WIDTH)
    return b_gate * y


def _fwd_setup_inputs(seed: int = 0) -> dict:
    key = jax.random.key(seed)
    ks = jax.random.split(key, 16)
    f32 = jnp.float32

    def nrm(k, shape, scale):
        return jax.random.normal(k, shape, f32) * scale

    def gain(k, shape):
        return 1.0 + 0.05 * jax.random.normal(k, shape, f32)

    return {
        'x': nrm(ks[0], (BATCH, SEQ, D_MODEL), 1.0),
        'meta_tokens': nrm(ks[1], (N_META, D_MODEL), 1.0),
        'mix_pre_g': gain(ks[2], (DEPTH, D_MODEL)),
        'w_in': nrm(ks[3], (DEPTH, D_MODEL, IN_WIDTH), D_MODEL ** -0.5),
        'conv_w': nrm(ks[4], (DEPTH, CONV_K, CONV_WIDTH), CONV_K ** -0.5),
        'sinks': nrm(ks[5], (DEPTH, N_Q_HEADS), 0.5),
        'attn_out_g': gain(ks[6], (DEPTH, ATTN_WIDTH)),
        'conv_out_g': gain(ks[7], (DEPTH, CONV_WIDTH)),
        'w_out': nrm(ks[8], (DEPTH, MIX_WIDTH, D_MODEL), MIX_WIDTH ** -0.5),
        'mix_post_g': gain(ks[9], (DEPTH, D_MODEL)),
        'mlp_pre_g': gain(ks[10], (DEPTH, D_MODEL)),
        'w_up': nrm(ks[11], (DEPTH, D_MODEL, D_FF), D_MODEL ** -0.5),
        'w_down': nrm(ks[12], (DEPTH, D_FF, D_MODEL), D_FF ** -0.5),
        'mlp_post_g': gain(ks[13], (DEPTH, D_MODEL)),
    }


def _fwd_reference(x, meta_tokens, mix_pre_g, w_in, conv_w, sinks, attn_out_g, conv_out_g,
              w_out, mix_post_g, mlp_pre_g, w_up, w_down, mlp_post_g):
    bsz = x.shape[0]
    meta = jnp.broadcast_to(meta_tokens[None].astype(x.dtype), (bsz, N_META, D_MODEL))
    h = jnp.concatenate([meta, x], axis=1)
    L = h.shape[1]
    cos, sin = rope_tables(L)
    s_q = ATTN_WIDTH
    s_k = s_q + KV_WIDTH
    s_v = s_k + KV_WIDTH
    s_b = s_v + CONV_WIDTH
    s_c = s_b + CONV_WIDTH
    for l in range(DEPTH):
        a = rmsnorm(h, mix_pre_g[l])
        proj = a @ w_in[l]
        q = proj[..., :s_q].reshape(bsz, L, N_Q_HEADS, HEAD_DIM)
        k = proj[..., s_q:s_k].reshape(bsz, L, N_KV_HEADS, HEAD_DIM)
        v = proj[..., s_k:s_v].reshape(bsz, L, N_KV_HEADS, HEAD_DIM)
        b_gate = proj[..., s_v:s_b]
        c_gate = proj[..., s_b:s_c]
        hc = proj[..., s_c:]
        q = partial_rope(q, cos, sin)
        k = partial_rope(k, cos, sin)
        y_attn = sliding_window_gqa_sinks(q, k, v, sinks[l])
        y_conv = short_gated_conv(b_gate, c_gate, hc, conv_w[l])
        y = jnp.concatenate([rmsnorm(y_attn, attn_out_g[l]),
                             rmsnorm(y_conv, conv_out_g[l])], axis=-1)
        h = h + rmsnorm(y @ w_out[l], mix_post_g[l])
        a = rmsnorm(h, mlp_pre_g[l])
        f = jnp.square(jax.nn.relu(a @ w_up[l])) @ w_down[l]
        h = h + rmsnorm(f, mlp_post_g[l])
    return h[:, N_META:]


import jax as _jax
import jax.numpy as _jnp

TWIN_FORMAT = 'train_step'
FWD_PARAMS = ['x', 'meta_tokens', 'mix_pre_g', 'w_in', 'conv_w', 'sinks', 'attn_out_g', 'conv_out_g', 'w_out', 'mix_post_g', 'mlp_pre_g', 'w_up', 'w_down', 'mlp_post_g']
TWIN_WEIGHTS = ['meta_tokens', 'mix_pre_g', 'w_in', 'conv_w', 'sinks', 'attn_out_g', 'conv_out_g', 'w_out', 'mix_post_g', 'mlp_pre_g', 'w_up', 'w_down', 'mlp_post_g']
TWIN_DIFF_INPUT = 'x'
TWIN_INPUTS = ['x', 'meta_tokens', 'mix_pre_g', 'w_in', 'conv_w', 'sinks', 'attn_out_g', 'conv_out_g', 'w_out', 'mix_post_g', 'mlp_pre_g', 'w_up', 'w_down', 'mlp_post_g', 'loss_target', 'm_meta_tokens', 'm_mix_pre_g', 'm_w_in', 'm_conv_w', 'm_sinks', 'm_attn_out_g', 'm_conv_out_g', 'm_w_out', 'm_mix_post_g', 'm_mlp_pre_g', 'm_w_up', 'm_w_down', 'm_mlp_post_g', 'v_meta_tokens', 'v_mix_pre_g', 'v_w_in', 'v_conv_w', 'v_sinks', 'v_attn_out_g', 'v_conv_out_g', 'v_w_out', 'v_mix_post_g', 'v_mlp_pre_g', 'v_w_up', 'v_w_down', 'v_mlp_post_g']
TWIN_OUTPUTS = ['loss', 'grad_x', 'grad_meta_tokens', 'grad_mix_pre_g', 'grad_w_in', 'grad_conv_w', 'grad_sinks', 'grad_attn_out_g', 'grad_conv_out_g', 'grad_w_out', 'grad_mix_post_g', 'grad_mlp_pre_g', 'grad_w_up', 'grad_w_down', 'grad_mlp_post_g', 'delta_meta_tokens', 'delta_mix_pre_g', 'delta_w_in', 'delta_conv_w', 'delta_sinks', 'delta_attn_out_g', 'delta_conv_out_g', 'delta_w_out', 'delta_mix_post_g', 'delta_mlp_pre_g', 'delta_w_up', 'delta_w_down', 'delta_mlp_post_g', 'new_m_meta_tokens', 'new_m_mix_pre_g', 'new_m_w_in', 'new_m_conv_w', 'new_m_sinks', 'new_m_attn_out_g', 'new_m_conv_out_g', 'new_m_w_out', 'new_m_mix_post_g', 'new_m_mlp_pre_g', 'new_m_w_up', 'new_m_w_down', 'new_m_mlp_post_g', 'new_v_meta_tokens', 'new_v_mix_pre_g', 'new_v_w_in', 'new_v_conv_w', 'new_v_sinks', 'new_v_attn_out_g', 'new_v_conv_out_g', 'new_v_w_out', 'new_v_mix_post_g', 'new_v_mlp_pre_g', 'new_v_w_up', 'new_v_w_down', 'new_v_mlp_post_g']
TWIN_LEAF_KINDS = {'loss': 'loss', 'grad_x': 'grad_x', 'grad_meta_tokens': 'grad_w', 'grad_mix_pre_g': 'grad_w', 'grad_w_in': 'grad_w', 'grad_conv_w': 'grad_w', 'grad_sinks': 'grad_w', 'grad_attn_out_g': 'grad_w', 'grad_conv_out_g': 'grad_w', 'grad_w_out': 'grad_w', 'grad_mix_post_g': 'grad_w', 'grad_mlp_pre_g': 'grad_w', 'grad_w_up': 'grad_w', 'grad_w_down': 'grad_w', 'grad_mlp_post_g': 'grad_w', 'delta_meta_tokens': 'delta_w', 'delta_mix_pre_g': 'delta_w', 'delta_w_in': 'delta_w', 'delta_conv_w': 'delta_w', 'delta_sinks': 'delta_w', 'delta_attn_out_g': 'delta_w', 'delta_conv_out_g': 'delta_w', 'delta_w_out': 'delta_w', 'delta_mix_post_g': 'delta_w', 'delta_mlp_pre_g': 'delta_w', 'delta_w_up': 'delta_w', 'delta_w_down': 'delta_w', 'delta_mlp_post_g': 'delta_w', 'new_m_meta_tokens': 'new_m', 'new_m_mix_pre_g': 'new_m', 'new_m_w_in': 'new_m', 'new_m_conv_w': 'new_m', 'new_m_sinks': 'new_m', 'new_m_attn_out_g': 'new_m', 'new_m_conv_out_g': 'new_m', 'new_m_w_out': 'new_m', 'new_m_mix_post_g': 'new_m', 'new_m_mlp_pre_g': 'new_m', 'new_m_w_up': 'new_m', 'new_m_w_down': 'new_m', 'new_m_mlp_post_g': 'new_m', 'new_v_meta_tokens': 'new_v', 'new_v_mix_pre_g': 'new_v', 'new_v_w_in': 'new_v', 'new_v_conv_w': 'new_v', 'new_v_sinks': 'new_v', 'new_v_attn_out_g': 'new_v', 'new_v_conv_out_g': 'new_v', 'new_v_w_out': 'new_v', 'new_v_mix_post_g': 'new_v', 'new_v_mlp_pre_g': 'new_v', 'new_v_w_up': 'new_v', 'new_v_w_down': 'new_v', 'new_v_mlp_post_g': 'new_v'}


def _forward(args):
    return _fwd_reference(*[args[k] for k in FWD_PARAMS])


def _output_shape():
    def fwd():
        inp = _fwd_setup_inputs(0)
        return _fwd_reference(*[inp[k] for k in FWD_PARAMS])
    out = _jax.eval_shape(fwd)
    return out.shape, out.dtype

N_MICROBATCH = 1
ADAM_LR = 0.001
ADAM_B1 = 0.9
ADAM_B2 = 0.999
ADAM_EPS = 1e-08
ADAM_WD = 0.01
ADAM_STEP = 10
PER_EXAMPLE_BATCH_AXIS = {'x': 0, 'loss_target': 0}
SHARED_INPUTS = []
_WEIGHT_DTYPES = {'meta_tokens': _jnp.float32, 'mix_pre_g': _jnp.float32, 'w_in': _jnp.float32, 'conv_w': _jnp.float32, 'sinks': _jnp.float32, 'attn_out_g': _jnp.float32, 'conv_out_g': _jnp.float32, 'w_out': _jnp.float32, 'mix_post_g': _jnp.float32, 'mlp_pre_g': _jnp.float32, 'w_up': _jnp.float32, 'w_down': _jnp.float32, 'mlp_post_g': _jnp.float32}
MOMENT_SCALE = {'meta_tokens': 6.440397e-01, 'mix_pre_g': 9.676429e+00, 'w_in': 7.047943e+00, 'conv_w': 1.133291e+00, 'sinks': 1.087662e+00, 'attn_out_g': 1.851339e+01, 'conv_out_g': 2.187946e+00, 'w_out': 1.099210e+01, 'mix_post_g': 6.438494e+01, 'mlp_pre_g': 5.169029e+00, 'w_up': 2.541839e+00, 'w_down': 2.199799e+01, 'mlp_post_g': 7.026380e+01}


def _to_microbatches(a, axis):
    t = _jnp.moveaxis(a, axis, 0)
    t = t.reshape((N_MICROBATCH, t.shape[0] // N_MICROBATCH) + t.shape[1:])
    return _jnp.moveaxis(t, 1, axis + 1)


def setup_inputs(seed: int = 0) -> dict:
    inp = _fwd_setup_inputs(seed)
    key = _jax.random.fold_in(_jax.random.key(seed), 7919)
    shape, _ = _output_shape()
    out = dict(inp)
    out["loss_target"] = _jax.random.normal(_jax.random.fold_in(key, 0), shape, _jnp.float32)
    for i, name in enumerate(TWIN_WEIGHTS):
        w = inp[name].astype(_jnp.float32)
        if MOMENT_SCALE is None:
            s = _jnp.sqrt(_jnp.mean(_jnp.square(w)) + 1e-30)
        else:
            s = MOMENT_SCALE[name]
        km, kv = _jax.random.split(_jax.random.fold_in(key, i + 1))
        out[name] = w
        out["m_" + name] = s * _jax.random.normal(km, w.shape, _jnp.float32)
        out["v_" + name] = (s * s) * _jax.random.uniform(kv, w.shape, _jnp.float32, 0.5, 1.5)
    if N_MICROBATCH > 1:
        for name, axis in PER_EXAMPLE_BATCH_AXIS.items():
            out[name] = _to_microbatches(out[name], axis)
    return {'x': out['x'], 'meta_tokens': out['meta_tokens'], 'mix_pre_g': out['mix_pre_g'], 'w_in': out['w_in'], 'conv_w': out['conv_w'], 'sinks': out['sinks'], 'attn_out_g': out['attn_out_g'], 'conv_out_g': out['conv_out_g'], 'w_out': out['w_out'], 'mix_post_g': out['mix_post_g'], 'mlp_pre_g': out['mlp_pre_g'], 'w_up': out['w_up'], 'w_down': out['w_down'], 'mlp_post_g': out['mlp_post_g'], 'loss_target': out['loss_target'], 'm_meta_tokens': out['m_meta_tokens'], 'm_mix_pre_g': out['m_mix_pre_g'], 'm_w_in': out['m_w_in'], 'm_conv_w': out['m_conv_w'], 'm_sinks': out['m_sinks'], 'm_attn_out_g': out['m_attn_out_g'], 'm_conv_out_g': out['m_conv_out_g'], 'm_w_out': out['m_w_out'], 'm_mix_post_g': out['m_mix_post_g'], 'm_mlp_pre_g': out['m_mlp_pre_g'], 'm_w_up': out['m_w_up'], 'm_w_down': out['m_w_down'], 'm_mlp_post_g': out['m_mlp_post_g'], 'v_meta_tokens': out['v_meta_tokens'], 'v_mix_pre_g': out['v_mix_pre_g'], 'v_w_in': out['v_w_in'], 'v_conv_w': out['v_conv_w'], 'v_sinks': out['v_sinks'], 'v_attn_out_g': out['v_attn_out_g'], 'v_conv_out_g': out['v_conv_out_g'], 'v_w_out': out['v_w_out'], 'v_mix_post_g': out['v_mix_post_g'], 'v_mlp_pre_g': out['v_mlp_pre_g'], 'v_w_up': out['v_w_up'], 'v_w_down': out['v_w_down'], 'v_mlp_post_g': out['v_mlp_post_g']}


def _loss(weights, diff, rest, loss_target):
    with _jax.named_scope("forward"):
        args = {**rest, TWIN_DIFF_INPUT: diff, **{k: w.astype(_WEIGHT_DTYPES[k]) for k, w in weights.items()}}
        y = _forward(args)
    with _jax.named_scope("loss_head"):
        err = _jnp.square(y.astype(_jnp.float32) - loss_target)
        return 0.5 * _jnp.sum(_jnp.mean(err, axis=-1)) if err.ndim else 0.5 * err


def _adamw(w, g, m, v):
    m = ADAM_B1 * m + (1.0 - ADAM_B1) * g
    v = ADAM_B2 * v + (1.0 - ADAM_B2) * _jnp.square(g)
    m_hat = m / (1.0 - ADAM_B1 ** ADAM_STEP)
    v_hat = v / (1.0 - ADAM_B2 ** ADAM_STEP)
    delta = -ADAM_LR * (m_hat / (_jnp.sqrt(v_hat) + ADAM_EPS) + ADAM_WD * w)
    return delta, m, v


def reference(x, meta_tokens, mix_pre_g, w_in, conv_w, sinks, attn_out_g, conv_out_g, w_out, mix_post_g, mlp_pre_g, w_up, w_down, mlp_post_g, loss_target, m_meta_tokens, m_mix_pre_g, m_w_in, m_conv_w, m_sinks, m_attn_out_g, m_conv_out_g, m_w_out, m_mix_post_g, m_mlp_pre_g, m_w_up, m_w_down, m_mlp_post_g, v_meta_tokens, v_mix_pre_g, v_w_in, v_conv_w, v_sinks, v_attn_out_g, v_conv_out_g, v_w_out, v_mix_post_g, v_mlp_pre_g, v_w_up, v_w_down, v_mlp_post_g):
    given = dict(x=x, meta_tokens=meta_tokens, mix_pre_g=mix_pre_g, w_in=w_in, conv_w=conv_w, sinks=sinks, attn_out_g=attn_out_g, conv_out_g=conv_out_g, w_out=w_out, mix_post_g=mix_post_g, mlp_pre_g=mlp_pre_g, w_up=w_up, w_down=w_down, mlp_post_g=mlp_post_g, loss_target=loss_target, m_meta_tokens=m_meta_tokens, m_mix_pre_g=m_mix_pre_g, m_w_in=m_w_in, m_conv_w=m_conv_w, m_sinks=m_sinks, m_attn_out_g=m_attn_out_g, m_conv_out_g=m_conv_out_g, m_w_out=m_w_out, m_mix_post_g=m_mix_post_g, m_mlp_pre_g=m_mlp_pre_g, m_w_up=m_w_up, m_w_down=m_w_down, m_mlp_post_g=m_mlp_post_g, v_meta_tokens=v_meta_tokens, v_mix_pre_g=v_mix_pre_g, v_w_in=v_w_in, v_conv_w=v_conv_w, v_sinks=v_sinks, v_attn_out_g=v_attn_out_g, v_conv_out_g=v_conv_out_g, v_w_out=v_w_out, v_mix_post_g=v_mix_post_g, v_mlp_pre_g=v_mlp_pre_g, v_w_up=v_w_up, v_w_down=v_w_down, v_mlp_post_g=v_mlp_post_g)
    weights = {n: given[n] for n in TWIN_WEIGHTS}
    shared = {n: given[n] for n in SHARED_INPUTS}
    per_example = {n: given[n] for n in ['x']}
    grad_fn = _jax.value_and_grad(_loss, argnums=(0, 1))

    def one_microbatch(ex, loss_target):
        ex = dict(ex)
        diff = ex.pop(TWIN_DIFF_INPUT)
        return grad_fn(weights, diff, {**shared, **ex}, loss_target)

    if N_MICROBATCH == 1:
        loss, (grad_w, grad_x) = one_microbatch(per_example, given["loss_target"])
    else:
        def body(carry, xs):
            loss_sum, grad_sum = carry
            l_k, (gw_k, gx_k) = one_microbatch(xs[0], xs[1])
            with _jax.named_scope("update"):
                return (loss_sum + l_k, _jax.tree.map(_jnp.add, grad_sum, gw_k)), gx_k

        init = (_jnp.zeros((), _jnp.float32), _jax.tree.map(_jnp.zeros_like, weights))
        (loss, grad_w), grad_x = _jax.lax.scan(body, init, (per_example, given["loss_target"]))
    with _jax.named_scope("update"):
        delta_w, new_m, new_v = {}, {}, {}
        for n in TWIN_WEIGHTS:
            delta_w[n], new_m[n], new_v[n] = _adamw(weights[n], grad_w[n], given["m_" + n], given["v_" + n])
    return (loss, grad_x, *[grad_w[n] for n in TWIN_WEIGHTS], *[delta_w[n] for n in TWIN_WEIGHTS],
            *[new_m[n] for n in TWIN_WEIGHTS], *[new_v[n] for n in TWIN_WEIGHTS])
```

```python
import functools

import jax
import jax.numpy as jnp
from jax import lax
from jax.experimental import pallas as pl
from jax.experimental.pallas import tpu as pltpu

F32 = jnp.float32
BF16 = jnp.bfloat16

D_MODEL = 1024
DEPTH = 2
N_META = 16
ATTN_W = 512
CONV_W = 512
HEAD_DIM = 64
N_Q_HEADS = 8
N_KV_HEADS = 2
GROUP = N_Q_HEADS // N_KV_HEADS
KV_W = N_KV_HEADS * HEAD_DIM
CONV_K = 3
BLOCK = 128
LEAD_PAD = BLOCK - N_META
ROPE_THETA = 500000.0
ROT_DIM = HEAD_DIM // 4
ROT_HALF = ROT_DIM // 2
D_FF = 4 * D_MODEL
IN_W = ATTN_W + 2 * KV_W + 3 * CONV_W
QKV_W = ATTN_W + 2 * KV_W
EPS = 1e-6
SCALE = HEAD_DIM ** -0.5
FF_CHUNK = 1024
N_CHIPS = 4
N_DEV = 8

ADAM_LR = 0.001
ADAM_B1 = 0.9
ADAM_B2 = 0.999
ADAM_EPS = 1e-08
ADAM_WD = 0.01
ADAM_STEP = 10

V7X_VMEM_LIMIT = 56 * 1024 * 1024
SMALL_ROWS = 32

MESH = pl.DeviceIdType.MESH


def _params(*sem):
    return pltpu.CompilerParams(dimension_semantics=sem, vmem_limit_bytes=V7X_VMEM_LIMIT)


def _row_tile(t, most):
    nb = t // BLOCK
    for b in range(most // BLOCK, 0, -1):
        if nb % b == 0:
            return b * BLOCK
    return BLOCK


def _rms(x, g):
    r = lax.rsqrt(jnp.mean(x * x, axis=-1, keepdims=True) + EPS)
    return x * r * g


def _rms_bwd(dy, x, g):
    r = lax.rsqrt(jnp.mean(x * x, axis=-1, keepdims=True) + EPS)
    xh = x * r
    dg = jnp.sum(dy * xh, axis=0, keepdims=True)
    dxh = dy * g
    dx = r * (dxh - xh * jnp.mean(dxh * xh, axis=-1, keepdims=True))
    return dx, dg


def _rope(x, cos, sa, sb):
    n = x.shape[-1]
    return x * cos + pltpu.roll(x, n - ROT_HALF, 1) * sa + pltpu.roll(x, ROT_HALF, 1) * sb


def _rope_bwd(dy, cos, sa, sb):
    n = dy.shape[-1]
    return dy * cos + pltpu.roll(dy * sa, ROT_HALF, 1) + pltpu.roll(dy * sb, n - ROT_HALF, 1)


def _rope_tables(t):
    pos = jnp.arange(t, dtype=F32) - LEAD_PAD
    inv_freq = jnp.power(jnp.float32(ROPE_THETA), -jnp.arange(0, ROT_DIM, 2, dtype=F32) / ROT_DIM)
    ang = pos[:, None] * inv_freq[None, :]
    cos, sin = jnp.cos(ang), jnp.sin(ang)
    rest = HEAD_DIM - ROT_DIM
    zeros = jnp.zeros((t, ROT_HALF), F32)
    c64 = jnp.concatenate([cos, cos, jnp.ones((t, rest), F32)], axis=1)
    a64 = jnp.concatenate([-sin, zeros, jnp.zeros((t, rest), F32)], axis=1)
    b64 = jnp.concatenate([zeros, sin, jnp.zeros((t, rest), F32)], axis=1)
    return tuple(jnp.concatenate([m, m], axis=1) for m in (c64, a64, b64))


def _in_proj(h, g, w, tabs, tm):
    t = h.shape[0]

    def body(h_ref, g_ref, w_ref, c_ref, sa_ref, sb_ref, a_ref, q_ref, k_ref, v_ref, b_ref, cg_ref, hc_ref):
        a = _rms(h_ref[...], g_ref[...]).astype(BF16)
        a_ref[...] = a
        p = jnp.dot(a, w_ref[...], preferred_element_type=F32)
        cos, sa, sb = c_ref[...], sa_ref[...], sb_ref[...]
        rep = ATTN_W // (2 * HEAD_DIM)
        q = _rope(p[:, :ATTN_W], jnp.tile(cos, (1, rep)), jnp.tile(sa, (1, rep)), jnp.tile(sb, (1, rep)))
        q_ref[...] = q.astype(BF16)
        k_ref[...] = _rope(p[:, ATTN_W:ATTN_W + KV_W], cos, sa, sb).astype(BF16)
        v_ref[...] = p[:, ATTN_W + KV_W:QKV_W].astype(BF16)
        b_ref[...] = p[:, QKV_W:QKV_W + CONV_W]
        cg_ref[...] = p[:, QKV_W + CONV_W:QKV_W + 2 * CONV_W]
        hc_ref[...] = p[:, QKV_W + 2 * CONV_W:]

    row = lambda n: pl.BlockSpec((tm, n), lambda i: (i, 0))
    full = lambda a: pl.BlockSpec(a.shape, lambda i: (0, 0))
    return pl.pallas_call(
        body, name="in_proj", grid=(t // tm,),
        in_specs=[row(D_MODEL), full(g), full(w), row(2 * HEAD_DIM), row(2 * HEAD_DIM), row(2 * HEAD_DIM)],
        out_specs=[row(D_MODEL), row(ATTN_W), row(KV_W), row(KV_W), row(CONV_W), row(CONV_W), row(CONV_W)],
        out_shape=[jax.ShapeDtypeStruct((t, D_MODEL), BF16), jax.ShapeDtypeStruct((t, ATTN_W), BF16),
                   jax.ShapeDtypeStruct((t, KV_W), BF16), jax.ShapeDtypeStruct((t, KV_W), BF16),
                   jax.ShapeDtypeStruct((t, CONV_W), F32), jax.ShapeDtypeStruct((t, CONV_W), F32),
                   jax.ShapeDtypeStruct((t, CONV_W), F32)],
        compiler_params=_params("parallel"),
    )(h, g, w, *tabs)


def _attn_mask(i):
    shape = (GROUP * BLOCK, 2 * BLOCK)
    r = lax.broadcasted_iota(jnp.int32, shape, 0) & (BLOCK - 1)
    c = lax.broadcasted_iota(jnp.int32, shape, 1)
    return (c > r) & (c <= r + BLOCK) & (c + (i - 1) * BLOCK >= LEAD_PAD)


def _attn_probs(q4, kg, sk, mask):
    s = lax.dot_general(q4, kg, (((1,), (1,)), ((), ())), preferred_element_type=F32) * SCALE
    s = jnp.where(mask, s, -jnp.inf)
    m = jnp.maximum(jnp.max(s, axis=-1, keepdims=True), sk)
    e = jnp.exp(s - m)
    es = jnp.exp(sk - m)
    den = jnp.sum(e, axis=-1, keepdims=True) + es
    return e / den, es / den


def _stack_heads(ref, g):
    return jnp.concatenate([ref[:, (GROUP * g + j) * HEAD_DIM:(GROUP * g + j + 1) * HEAD_DIM] for j in range(GROUP)],
                           axis=0)


def _sink_column(s_ref, g):
    return jnp.concatenate([jnp.full((BLOCK, 1), s_ref[GROUP * g + j], F32) for j in range(GROUP)], axis=0)


def _two_blocks(ref, i):
    prev = jnp.maximum(i - 1, 0)
    return jnp.concatenate([ref[pl.ds(pl.multiple_of(prev * BLOCK, BLOCK), BLOCK), :],
                            ref[pl.ds(pl.multiple_of(i * BLOCK, BLOCK), BLOCK), :]], axis=0)


def _attn_fwd(q, k, v, sinks):
    t = q.shape[0]

    def body(s_ref, q_ref, k_ref, v_ref, o_ref):
        i = pl.program_id(0)
        kc, vc = _two_blocks(k_ref, i), _two_blocks(v_ref, i)
        mask = _attn_mask(i)
        for g in range(N_KV_HEADS):
            lanes = slice(g * HEAD_DIM, (g + 1) * HEAD_DIM)
            p, _ = _attn_probs(_stack_heads(q_ref, g), kc[:, lanes], _sink_column(s_ref, g), mask)
            o4 = jnp.dot(p.astype(BF16), vc[:, lanes], preferred_element_type=F32)
            for j in range(GROUP):
                hh = GROUP * g + j
                o_ref[:, hh * HEAD_DIM:(hh + 1) * HEAD_DIM] = o4[j * BLOCK:(j + 1) * BLOCK]

    whole = pl.BlockSpec((t, KV_W), lambda i: (0, 0))
    return pl.pallas_call(
        body, name="attn_fwd", grid=(t // BLOCK,),
        in_specs=[pl.BlockSpec(memory_space=pltpu.SMEM), pl.BlockSpec((BLOCK, ATTN_W), lambda i: (i, 0)), whole, whole],
        out_specs=pl.BlockSpec((BLOCK, ATTN_W), lambda i: (i, 0)),
        out_shape=jax.ShapeDtypeStruct((t, ATTN_W), F32),
        compiler_params=_params("parallel"),
    )(sinks, q, k, v)


def _shift_rows(u, halo, n):
    r = pltpu.roll(u, n, 0)
    hr = pltpu.roll(halo, n, 0)
    idx = lax.broadcasted_iota(jnp.int32, hr.shape, 0)
    return jnp.concatenate([jnp.where(idx < n, hr, r[:8]), r[8:]], axis=0)


def _advance_rows(u, halo, n):
    rows = u.shape[0]
    r = pltpu.roll(u, rows - n, 0)
    hr = pltpu.roll(halo, 8 - n, 0)
    idx = lax.broadcasted_iota(jnp.int32, hr.shape, 0)
    return jnp.concatenate([r[:rows - 8], jnp.where(idx >= 8 - n, hr, r[rows - 8:])], axis=0)


def _mix_out(h, o, b, c, hc, cw, ga, gc, w, gp, tm):
    t = h.shape[0]

    def body(h_ref, o_ref, b_ref, c_ref, hc_ref, cw_ref, ga_ref, gc_ref, w_ref, gp_ref, h1_ref, y_ref, z_ref, halo):
        @pl.when(pl.program_id(0) == 0)
        def _():
            halo[...] = jnp.zeros_like(halo)

        u = c_ref[...] * hc_ref[...]
        cv = cw_ref[0:1, :] * _shift_rows(u, halo[...], 2) + cw_ref[1:2, :] * _shift_rows(u, halo[...], 1) \
            + cw_ref[2:3, :] * u
        halo[...] = u[tm - 8:]
        yc = b_ref[...] * cv
        y = jnp.concatenate([_rms(o_ref[...], ga_ref[...]), _rms(yc, gc_ref[...])], axis=1).astype(BF16)
        y_ref[...] = y
        z = jnp.dot(y, w_ref[...], preferred_element_type=F32)
        z_ref[...] = z
        h1_ref[...] = h_ref[...] + _rms(z, gp_ref[...])

    row = lambda n: pl.BlockSpec((tm, n), lambda i: (i, 0))
    full = lambda a: pl.BlockSpec(a.shape, lambda i: (0, 0))
    return pl.pallas_call(
        body, name="mix_out", grid=(t // tm,),
        in_specs=[row(D_MODEL), row(ATTN_W), row(CONV_W), row(CONV_W), row(CONV_W), full(cw), full(ga), full(gc),
                  full(w), full(gp)],
        out_specs=[row(D_MODEL), row(D_MODEL), row(D_MODEL)],
        out_shape=[jax.ShapeDtypeStruct((t, D_MODEL), F32), jax.ShapeDtypeStruct((t, D_MODEL), BF16),
                   jax.ShapeDtypeStruct((t, D_MODEL), F32)],
        scratch_shapes=[pltpu.VMEM((8, CONV_W), F32)],
        compiler_params=_params("arbitrary"),
    )(h, o, b, c, hc, cw, ga, gc, w, gp)


def _mlp(h1, g1, wu, wd, g2, tm):
    t = h1.shape[0]
    nj = D_FF // FF_CHUNK

    def body(h1_ref, g1_ref, wu_ref, wd_ref, g2_ref, h2_ref, a2_ref, act_ref, f_ref, acc):
        j = pl.program_id(1)

        @pl.when(j == 0)
        def _():
            a2_ref[...] = _rms(h1_ref[...], g1_ref[...]).astype(BF16)

        up = jnp.dot(a2_ref[...], wu_ref[...], preferred_element_type=F32)
        act = jnp.square(jnp.maximum(up, 0.0)).astype(BF16)
        act_ref[...] = act
        part = jnp.dot(act, wd_ref[...], preferred_element_type=F32)

        @pl.when(j == 0)
        def _():
            acc[...] = part

        @pl.when(j > 0)
        def _():
            acc[...] += part

        @pl.when(j == nj - 1)
        def _():
            f = acc[...]
            f_ref[...] = f
            h2_ref[...] = h1_ref[...] + _rms(f, g2_ref[...])

    row = pl.BlockSpec((tm, D_MODEL), lambda i, j: (i, 0))
    vec = pl.BlockSpec((1, D_MODEL), lambda i, j: (0, 0))
    return pl.pallas_call(
        body, name="mlp", grid=(t // tm, nj),
        in_specs=[row, vec, pl.BlockSpec((D_MODEL, FF_CHUNK), lambda i, j: (0, j)),
                  pl.BlockSpec((FF_CHUNK, D_MODEL), lambda i, j: (j, 0)), vec],
        out_specs=[row, row, pl.BlockSpec((tm, FF_CHUNK), lambda i, j: (i, j)), row],
        out_shape=[jax.ShapeDtypeStruct((t, D_MODEL), F32), jax.ShapeDtypeStruct((t, D_MODEL), BF16),
                   jax.ShapeDtypeStruct((t, D_FF), BF16), jax.ShapeDtypeStruct((t, D_MODEL), F32)],
        scratch_shapes=[pltpu.VMEM((tm, D_MODEL), F32)],
        compiler_params=_params("parallel", "arbitrary"),
    )(h1, g1, wu, wd, g2)


def _loss_head(h, target):
    t = h.shape[0]

    def body(h_ref, t_ref, loss_ref, dh_ref):
        i = pl.program_id(0)

        @pl.when(i == 0)
        def _():
            loss_ref[...] = jnp.zeros_like(loss_ref)
            dh_ref[...] = jnp.zeros_like(dh_ref)

        @pl.when(i > 0)
        def _():
            err = h_ref[...] - t_ref[...]
            dh_ref[...] = err * (1.0 / D_MODEL)
            loss_ref[...] += jnp.sum(err * err) * (0.5 / D_MODEL)

    return pl.pallas_call(
        body, name="loss_head", grid=(t // BLOCK,),
        in_specs=[pl.BlockSpec((BLOCK, D_MODEL), lambda i: (i, 0)),
                  pl.BlockSpec((BLOCK, D_MODEL), lambda i: (jnp.maximum(i - 1, 0), 0))],
        out_specs=[pl.BlockSpec((8, 128), lambda i: (0, 0)), pl.BlockSpec((BLOCK, D_MODEL), lambda i: (i, 0))],
        out_shape=[jax.ShapeDtypeStruct((8, 128), F32), jax.ShapeDtypeStruct((t, D_MODEL), F32)],
        compiler_params=_params("arbitrary"),
    )(h, target)


def _mlp_bwd(dh2, f, g2, act, wd, wu, h1, g1, tm):
    t = dh2.shape[0]
    nj = D_FF // FF_CHUNK

    def body(dh2_ref, f_ref, g2_ref, act_ref, wd_ref, wu_ref, h1_ref, g1_ref, dh1_ref, df_ref, dup_ref, dg2_ref,
             dg1_ref, acc):
        i, j = pl.program_id(0), pl.program_id(1)

        @pl.when((i == 0) & (j == 0))
        def _():
            dg2_ref[...] = jnp.zeros_like(dg2_ref)
            dg1_ref[...] = jnp.zeros_like(dg1_ref)

        @pl.when(j == 0)
        def _():
            df, dg = _rms_bwd(dh2_ref[...], f_ref[...], g2_ref[...])
            df_ref[...] = df.astype(BF16)
            dg2_ref[...] += dg

        dact = lax.dot_general(df_ref[...], wd_ref[...], (((1,), (1,)), ((), ())), preferred_element_type=F32)
        dup = (dact * (2.0 * jnp.sqrt(act_ref[...].astype(F32)))).astype(BF16)
        dup_ref[...] = dup
        part = lax.dot_general(dup, wu_ref[...], (((1,), (1,)), ((), ())), preferred_element_type=F32)

        @pl.when(j == 0)
        def _():
            acc[...] = part

        @pl.when(j > 0)
        def _():
            acc[...] += part

        @pl.when(j == nj - 1)
        def _():
            dx, dg = _rms_bwd(acc[...], h1_ref[...], g1_ref[...])
            dh1_ref[...] = dh2_ref[...] + dx
            dg1_ref[...] += dg

    row = pl.BlockSpec((tm, D_MODEL), lambda i, j: (i, 0))
    vec = pl.BlockSpec((1, D_MODEL), lambda i, j: (0, 0))
    chunk = pl.BlockSpec((tm, FF_CHUNK), lambda i, j: (i, j))
    return pl.pallas_call(
        body, name="mlp_bwd", grid=(t // tm, nj),
        in_specs=[row, row, vec, chunk, pl.BlockSpec((FF_CHUNK, D_MODEL), lambda i, j: (j, 0)),
                  pl.BlockSpec((D_MODEL, FF_CHUNK), lambda i, j: (0, j)), row, vec],
        out_specs=[row, row, chunk, vec, vec],
        out_shape=[jax.ShapeDtypeStruct((t, D_MODEL), F32), jax.ShapeDtypeStruct((t, D_MODEL), BF16),
                   jax.ShapeDtypeStruct((t, D_FF), BF16), jax.ShapeDtypeStruct((1, D_MODEL), F32),
                   jax.ShapeDtypeStruct((1, D_MODEL), F32)],
        scratch_shapes=[pltpu.VMEM((tm, D_MODEL), F32)],
        compiler_params=_params("arbitrary", "arbitrary"),
    )(dh2, f, g2, act, wd, wu, h1, g1)


def _weight_grad(x, y, tm, name):
    t, k = x.shape
    n = y.shape[1]
    tk = min(k, 1024)
    tn = n if n <= 1024 else (1152 if n % 1152 == 0 else 1024)

    def body(x_ref, y_ref, o_ref):
        @pl.when(pl.program_id(2) == 0)
        def _():
            o_ref[...] = jnp.zeros_like(o_ref)

        o_ref[...] += lax.dot_general(x_ref[...], y_ref[...], (((0,), (0,)), ((), ())), preferred_element_type=F32)

    return pl.pallas_call(
        body, name=name, grid=(k // tk, n // tn, t // tm),
        in_specs=[pl.BlockSpec((tm, tk), lambda a, b, r: (r, a)), pl.BlockSpec((tm, tn), lambda a, b, r: (r, b))],
        out_specs=pl.BlockSpec((tk, tn), lambda a, b, r: (a, b)),
        out_shape=jax.ShapeDtypeStruct((k, n), F32),
        compiler_params=_params("parallel", "parallel", "arbitrary"),
    )(x, y)


def _mix_out_bwd(dh1, z, gp, w, o, b, c, hc, cw, ga, gc, tm):
    t = dh1.shape[0]
    nt = t // tm
    per8 = tm // 8

    def body(dh1_ref, z_ref, gp_ref, w_ref, o_ref, b_ref, c_ref, hc_ref, cp_ref, hp_ref, cw_ref, ga_ref, gc_ref,
             dz_ref, do_ref, dbch_ref, dgp_ref, dga_ref, dgc_ref, dcw_ref, halo):
        i = pl.program_id(0)

        @pl.when(i == 0)
        def _():
            halo[...] = jnp.zeros_like(halo)
            dgp_ref[...] = jnp.zeros_like(dgp_ref)
            dga_ref[...] = jnp.zeros_like(dga_ref)
            dgc_ref[...] = jnp.zeros_like(dgc_ref)
            dcw_ref[...] = jnp.zeros_like(dcw_ref)

        dz, dgp = _rms_bwd(dh1_ref[...], z_ref[...], gp_ref[...])
        dgp_ref[...] += dgp
        dz = dz.astype(BF16)
        dz_ref[...] = dz
        dy = lax.dot_general(dz, w_ref[...], (((1,), (1,)), ((), ())), preferred_element_type=F32)
        do, dga = _rms_bwd(dy[:, :ATTN_W], o_ref[...], ga_ref[...])
        do_ref[...] = do
        dga_ref[...] += dga

        u = c_ref[...] * hc_ref[...]
        first = i == nt - 1
        u_before = jnp.where(first, 0.0, cp_ref[...] * hp_ref[...])
        u1 = _shift_rows(u, u_before, 1)
        u2 = _shift_rows(u, u_before, 2)
        cv = cw_ref[0:1, :] * u2 + cw_ref[1:2, :] * u1 + cw_ref[2:3, :] * u
        bb = b_ref[...]
        dyc, dgc = _rms_bwd(dy[:, ATTN_W:], bb * cv, gc_ref[...])
        dgc_ref[...] += dgc
        dcv = dyc * bb
        d1 = _advance_rows(dcv, halo[...], 1)
        d2 = _advance_rows(dcv, halo[...], 2)
        halo[...] = dcv[:8]
        du = cw_ref[2:3, :] * dcv + cw_ref[1:2, :] * d1 + cw_ref[0:1, :] * d2
        dbch_ref[...] = jnp.concatenate([dyc * cv, du * hc_ref[...], du * c_ref[...]], axis=1).astype(BF16)
        dcw_ref[...] += jnp.concatenate([jnp.sum(dcv * u2, axis=0, keepdims=True),
                                         jnp.sum(dcv * u1, axis=0, keepdims=True),
                                         jnp.sum(dcv * u, axis=0, keepdims=True)], axis=0)

    row = lambda n: pl.BlockSpec((tm, n), lambda i: (nt - 1 - i, 0))
    before = pl.BlockSpec((8, CONV_W), lambda i: (jnp.maximum((nt - 1 - i) * per8 - 1, 0), 0))
    full = lambda a: pl.BlockSpec(a.shape, lambda i: (0, 0))
    vec = lambda n: pl.BlockSpec((1, n), lambda i: (0, 0))
    return pl.pallas_call(
        body, name="mix_out_bwd", grid=(nt,),
        in_specs=[row(D_MODEL), row(D_MODEL), full(gp), full(w), row(ATTN_W), row(CONV_W), row(CONV_W), row(CONV_W),
                  before, before, full(cw), full(ga), full(gc)],
        out_specs=[row(D_MODEL), row(ATTN_W), row(3 * CONV_W), vec(D_MODEL), vec(ATTN_W), vec(CONV_W),
                   pl.BlockSpec((CONV_K, CONV_W), lambda i: (0, 0))],
        out_shape=[jax.ShapeDtypeStruct((t, D_MODEL), BF16), jax.ShapeDtypeStruct((t, ATTN_W), F32),
                   jax.ShapeDtypeStruct((t, 3 * CONV_W), BF16), jax.ShapeDtypeStruct((1, D_MODEL), F32),
                   jax.ShapeDtypeStruct((1, ATTN_W), F32), jax.ShapeDtypeStruct((1, CONV_W), F32),
                   jax.ShapeDtypeStruct((CONV_K, CONV_W), F32)],
        scratch_shapes=[pltpu.VMEM((8, CONV_W), F32)],
        compiler_params=_params("arbitrary"),
    )(dh1, z, gp, w, o, b, c, hc, c, hc, cw, ga, gc)


def _attn_bwd(q, k, v, o, do, sinks):
    t = q.shape[0]

    def body(s_ref, q_ref, k_ref, v_ref, o_ref, do_ref, dq_ref, dk_ref, dv_ref, ds_ref):
        i = pl.program_id(0)

        @pl.when(i == 0)
        def _():
            dk_ref[...] = jnp.zeros_like(dk_ref)
            dv_ref[...] = jnp.zeros_like(dv_ref)
            ds_ref[...] = jnp.zeros_like(ds_ref)

        kc, vc = _two_blocks(k_ref, i), _two_blocks(v_ref, i)
        mask = _attn_mask(i)
        cur = pl.ds(pl.multiple_of(i * BLOCK, BLOCK), BLOCK)
        prev = pl.ds(pl.multiple_of(jnp.maximum(i - 1, 0) * BLOCK, BLOCK), BLOCK)
        for g in range(N_KV_HEADS):
            lanes = slice(g * HEAD_DIM, (g + 1) * HEAD_DIM)
            q4 = _stack_heads(q_ref, g)
            p, ps = _attn_probs(q4, kc[:, lanes], _sink_column(s_ref, g), mask)
            o4, do4 = _stack_heads(o_ref, g), _stack_heads(do_ref, g)
            do4b = do4.astype(BF16)
            dp = lax.dot_general(do4b, vc[:, lanes], (((1,), (1,)), ((), ())), preferred_element_type=F32)
            drow = jnp.sum(do4 * o4, axis=-1, keepdims=True)
            ds = (p * (dp - drow)).astype(BF16)
            dsink = ps * drow
            dq4 = jnp.dot(ds, kc[:, lanes], preferred_element_type=F32) * SCALE
            dkg = lax.dot_general(ds, q4, (((0,), (0,)), ((), ())), preferred_element_type=F32) * SCALE
            dvg = lax.dot_general(p.astype(BF16), do4b, (((0,), (0,)), ((), ())), preferred_element_type=F32)
            dk_ref[cur, lanes] += dkg[BLOCK:]
            dv_ref[cur, lanes] += dvg[BLOCK:]

            @pl.when(i > 0)
            def _():
                dk_ref[prev, lanes] += dkg[:BLOCK]
                dv_ref[prev, lanes] += dvg[:BLOCK]

            for j in range(GROUP):
                hh = GROUP * g + j
                dq_ref[:, hh * HEAD_DIM:(hh + 1) * HEAD_DIM] = dq4[j * BLOCK:(j + 1) * BLOCK]
                ds_ref[hh:hh + 1, :] -= jnp.sum(dsink[j * BLOCK:(j + 1) * BLOCK])

    whole = pl.BlockSpec((t, KV_W), lambda i: (0, 0))
    blk = pl.BlockSpec((BLOCK, ATTN_W), lambda i: (i, 0))
    return pl.pallas_call(
        body, name="attn_bwd", grid=(t // BLOCK,),
        in_specs=[pl.BlockSpec(memory_space=pltpu.SMEM), blk, whole, whole, blk, blk],
        out_specs=[blk, whole, whole, pl.BlockSpec((N_Q_HEADS, 128), lambda i: (0, 0))],
        out_shape=[jax.ShapeDtypeStruct((t, ATTN_W), F32), jax.ShapeDtypeStruct((t, KV_W), F32),
                   jax.ShapeDtypeStruct((t, KV_W), F32), jax.ShapeDtypeStruct((N_Q_HEADS, 128), F32)],
        compiler_params=_params("arbitrary"),
    )(sinks, q, k, v, o, do)


def _in_proj_bwd(dq, dk, dv, dbch, w, dh1, h, g, tabs, tm):
    t = h.shape[0]

    def body(dq_ref, dk_ref, dv_ref, dbch_ref, w_ref, dh1_ref, h_ref, g_ref, c_ref, sa_ref, sb_ref, dh_ref, dp_ref,
             dg_ref):
        @pl.when(pl.program_id(0) == 0)
        def _():
            dg_ref[...] = jnp.zeros_like(dg_ref)

        cos, sa, sb = c_ref[...], sa_ref[...], sb_ref[...]
        rep = ATTN_W // (2 * HEAD_DIM)
        dqr = _rope_bwd(dq_ref[...], jnp.tile(cos, (1, rep)), jnp.tile(sa, (1, rep)), jnp.tile(sb, (1, rep)))
        dkr = _rope_bwd(dk_ref[...], cos, sa, sb)
        dp = jnp.concatenate([dqr.astype(BF16), dkr.astype(BF16), dv_ref[...].astype(BF16), dbch_ref[...]], axis=1)
        dp_ref[...] = dp
        da = lax.dot_general(dp, w_ref[...], (((1,), (1,)), ((), ())), preferred_element_type=F32)
        dx, dg = _rms_bwd(da, h_ref[...], g_ref[...])
        dh_ref[...] = dh1_ref[...] + dx
        dg_ref[...] += dg

    row = lambda n: pl.BlockSpec((tm, n), lambda i: (i, 0))
    full = lambda a: pl.BlockSpec(a.shape, lambda i: (0, 0))
    return pl.pallas_call(
        body, name="in_proj_bwd", grid=(t // tm,),
        in_specs=[row(ATTN_W), row(KV_W), row(KV_W), row(3 * CONV_W), full(w), row(D_MODEL), row(D_MODEL), full(g),
                  row(2 * HEAD_DIM), row(2 * HEAD_DIM), row(2 * HEAD_DIM)],
        out_specs=[row(D_MODEL), row(IN_W), pl.BlockSpec((1, D_MODEL), lambda i: (0, 0))],
        out_shape=[jax.ShapeDtypeStruct((t, D_MODEL), F32), jax.ShapeDtypeStruct((t, IN_W), BF16),
                   jax.ShapeDtypeStruct((1, D_MODEL), F32)],
        compiler_params=_params("arbitrary"),
    )(dq, dk, dv, dbch, w, dh1, h, g, *tabs)


def _local_step(h0, target, p):
    t = h0.shape[0]
    tm = _row_tile(t, 640)
    ts = _row_tile(t, 320)
    tabs = _rope_tables(t)
    saved = []
    h = h0
    for l in range(DEPTH):
        a, q, k, v, b, c, hc = _in_proj(h, p["mix_pre_g"][l], p["w_in"][l], tabs, ts)
        o = _attn_fwd(q, k, v, p["sinks"][l])
        h1, y, z = _mix_out(h, o, b, c, hc, p["conv_w"][l], p["attn_out_g"][l], p["conv_out_g"][l], p["w_out"][l],
                            p["mix_post_g"][l], ts)
        h2, a2, act, f = _mlp(h1, p["mlp_pre_g"][l], p["w_up"][l], p["w_down"][l], p["mlp_post_g"][l], tm)
        saved.append((h, a, q, k, v, b, c, hc, o, h1, y, z, a2, act, f))
        h = h2
    loss_tile, dh = _loss_head(h, target)

    names = ("w_in", "w_out", "w_up", "w_down", "mix_pre_g", "mix_post_g", "mlp_pre_g", "mlp_post_g", "attn_out_g",
             "conv_out_g", "conv_w", "sinks")
    grads = {n: [None] * DEPTH for n in names}
    for l in reversed(range(DEPTH)):
        h_in, a, q, k, v, b, c, hc, o, h1, y, z, a2, act, f = saved[l]
        dh1, df, dup, dg2, dg1 = _mlp_bwd(dh, f, p["mlp_post_g"][l], act, p["w_down"][l], p["w_up"][l], h1,
                                          p["mlp_pre_g"][l], tm)
        grads["w_down"][l] = _weight_grad(act, df, tm, "grad_w_down")
        grads["w_up"][l] = _weight_grad(a2, dup, tm, "grad_w_up")
        dz, do, dbch, dgp, dga, dgc, dcw = _mix_out_bwd(dh1, z, p["mix_post_g"][l], p["w_out"][l], o, b, c, hc,
                                                        p["conv_w"][l], p["attn_out_g"][l], p["conv_out_g"][l], ts)
        grads["w_out"][l] = _weight_grad(y, dz, tm, "grad_w_out")
        dq, dk, dv, dsink = _attn_bwd(q, k, v, o, do, p["sinks"][l])
        dh, dproj, dgi = _in_proj_bwd(dq, dk, dv, dbch, p["w_in"][l], dh1, h_in, p["mix_pre_g"][l], tabs, ts)
        grads["w_in"][l] = _weight_grad(a, dproj, tm, "grad_w_in")
        grads["mlp_post_g"][l], grads["mlp_pre_g"][l], grads["mix_post_g"][l] = dg2, dg1, dgp
        grads["attn_out_g"][l], grads["conv_out_g"][l], grads["conv_w"][l] = dga, dgc, dcw
        grads["mix_pre_g"][l], grads["sinks"][l] = dgi, dsink[:, 0]
    return loss_tile, dh, grads


def _place():
    return lax.axis_index("x"), lax.axis_index("y"), lax.axis_index("c")


def _other_chips(x, y):
    return [(1 - x, y), (x, 1 - y), (1 - x, 1 - y)]


def _gather_chips(arrays):
    n = len(arrays)

    def body(*refs):
        ins, outs = refs[:n], refs[n:2 * n]
        send_sems, recv_sems, local_sems = refs[2 * n:]
        x, y, c = _place()
        chips = _other_chips(x, y)
        me = 2 * x + y
        started = []
        for a in range(n):
            mine = pltpu.make_async_copy(ins[a], outs[a].at[me], local_sems.at[a])
            mine.start()
            started.append(mine)
        sends = []
        for a in range(n):
            for j, (px, py) in enumerate(chips):
                cp = pltpu.make_async_remote_copy(src_ref=ins[a], dst_ref=outs[a].at[me], send_sem=send_sems.at[3 * a + j],
                                                  recv_sem=recv_sems.at[3 * a + j], device_id=(px, py, c),
                                                  device_id_type=MESH)
                cp.start()
                sends.append(cp)
        for a in range(n):
            for j, (px, py) in enumerate(chips):
                pltpu.make_async_remote_copy(src_ref=ins[a], dst_ref=outs[a].at[2 * px + py],
                                             send_sem=send_sems.at[3 * a + j], recv_sem=recv_sems.at[3 * a + j],
                                             device_id=(px, py, c), device_id_type=MESH).wait_recv()
        for cp in sends:
            cp.wait_send()
        for cp in started:
            cp.wait()

    any_spec = pl.BlockSpec(memory_space=pl.ANY)
    return pl.pallas_call(
        body, name="gather_chips",
        in_specs=[any_spec] * n, out_specs=[any_spec] * n,
        out_shape=[jax.ShapeDtypeStruct((N_CHIPS,) + a.shape, a.dtype) for a in arrays],
        scratch_shapes=[pltpu.SemaphoreType.DMA((3 * n,)), pltpu.SemaphoreType.DMA((3 * n,)),
                        pltpu.SemaphoreType.DMA((n,))],
    )(*arrays)


def _swap_halves(arrays, axis):
    n = len(arrays)

    def half(ref, which, a):
        size = arrays[a].shape[axis] // 2
        idx = [slice(None)] * axis + [pl.ds(which * size, size)]
        return ref.at[tuple(idx)]

    def body(*refs):
        ins, outs = refs[:n], refs[n:2 * n]
        send_sems, recv_sems = refs[2 * n:]
        x, y, c = _place()
        sends = []
        for a in range(n):
            cp = pltpu.make_async_remote_copy(src_ref=half(ins[a], 1 - c, a), dst_ref=outs[a], send_sem=send_sems.at[a],
                                              recv_sem=recv_sems.at[a], device_id=(x, y, 1 - c), device_id_type=MESH)
            cp.start()
            sends.append(cp)
        for cp in sends:
            cp.wait()

    def out_shape(a):
        s = list(a.shape)
        s[axis] //= 2
        return jax.ShapeDtypeStruct(tuple(s), a.dtype)

    any_spec = pl.BlockSpec(memory_space=pl.ANY)
    return pl.pallas_call(
        body, name="swap_halves",
        in_specs=[any_spec] * n, out_specs=[any_spec] * n, out_shape=[out_shape(a) for a in arrays],
        scratch_shapes=[pltpu.SemaphoreType.DMA((n,)), pltpu.SemaphoreType.DMA((n,))],
    )(*arrays)


def _add_half(g, r):
    rows, cols = g.shape[2], g.shape[3]
    hr = rows // 2
    tr = min(hr, 256)
    per = hr // tr
    first = (lax.axis_index("c") * per).astype(jnp.int32).reshape(1)

    def body(first_ref, g_ref, r_ref, o_ref):
        o_ref[...] = g_ref[...] + r_ref[...]

    blk = pl.BlockSpec((None, None, tr, cols), lambda s, l, i, first_ref: (s, l, i, 0))
    return pl.pallas_call(
        body, name="add_half",
        grid_spec=pltpu.PrefetchScalarGridSpec(
            num_scalar_prefetch=1, grid=(N_CHIPS, DEPTH, per),
            in_specs=[pl.BlockSpec((None, None, tr, cols), lambda s, l, i, first_ref: (s, l, i + first_ref[0], 0)), blk],
            out_specs=blk),
        out_shape=jax.ShapeDtypeStruct(r.shape, F32),
        compiler_params=_params("parallel", "parallel", "parallel"),
    )(first, g, r)


def _scatter_chips(arrays):
    n = len(arrays)

    def body(*refs):
        ins, outs = refs[:n], refs[n:2 * n]
        send_sems, recv_sems, local_sems = refs[2 * n:]
        x, y, c = _place()
        chips = _other_chips(x, y)
        me = 2 * x + y
        started, sends = [], []
        for a in range(n):
            mine = pltpu.make_async_copy(ins[a].at[me], outs[a].at[me], local_sems.at[a])
            mine.start()
            started.append(mine)
        for a in range(n):
            for j, (px, py) in enumerate(chips):
                cp = pltpu.make_async_remote_copy(src_ref=ins[a].at[2 * px + py], dst_ref=outs[a].at[me],
                                                  send_sem=send_sems.at[3 * a + j], recv_sem=recv_sems.at[3 * a + j],
                                                  device_id=(px, py, c), device_id_type=MESH)
                cp.start()
                sends.append(cp)
        for a in range(n):
            for j, (px, py) in enumerate(chips):
                pltpu.make_async_remote_copy(src_ref=ins[a].at[me], dst_ref=outs[a].at[2 * px + py],
                                             send_sem=send_sems.at[3 * a + j], recv_sem=recv_sems.at[3 * a + j],
                                             device_id=(px, py, c), device_id_type=MESH).wait_recv()
        for cp in sends:
            cp.wait_send()
        for cp in started:
            cp.wait()

    any_spec = pl.BlockSpec(memory_space=pl.ANY)
    return pl.pallas_call(
        body, name="scatter_chips",
        in_specs=[any_spec] * n, out_specs=[any_spec] * n,
        out_shape=[jax.ShapeDtypeStruct(a.shape, a.dtype) for a in arrays],
        scratch_shapes=[pltpu.SemaphoreType.DMA((3 * n,)), pltpu.SemaphoreType.DMA((3 * n,)),
                        pltpu.SemaphoreType.DMA((n,))],
    )(*arrays)


def _sum_chips(q):
    rows, cols = q.shape[2], q.shape[3]
    tr = min(rows, 256)

    def body(q_ref, o_ref):
        o_ref[...] = ((q_ref[0] + q_ref[1]) + q_ref[2]) + q_ref[3]

    return pl.pallas_call(
        body, name="sum_chips", grid=(DEPTH, rows // tr),
        in_specs=[pl.BlockSpec((N_CHIPS, None, tr, cols), lambda l, i: (0, l, i, 0))],
        out_specs=pl.BlockSpec((None, tr, cols), lambda l, i: (l, i, 0)),
        out_shape=jax.ShapeDtypeStruct(q.shape[1:], F32),
        compiler_params=_params("parallel", "parallel"),
    )(q)


def _join_halves(arrays):
    n = len(arrays)

    def body(*refs):
        ins, outs = refs[:n], refs[n:2 * n]
        send_sems, recv_sems, local_sems = refs[2 * n:]
        x, y, c = _place()
        started = []
        for a in range(n):
            hr = arrays[a].shape[1]
            mine_rows = outs[a].at[:, pl.ds(c * hr, hr)]
            local = pltpu.make_async_copy(ins[a], mine_rows, local_sems.at[a])
            local.start()
            cp = pltpu.make_async_remote_copy(src_ref=ins[a], dst_ref=mine_rows, send_sem=send_sems.at[a],
                                              recv_sem=recv_sems.at[a], device_id=(x, y, 1 - c), device_id_type=MESH)
            cp.start()
            started.append((local, cp))
        for a, (local, cp) in enumerate(started):
            hr = arrays[a].shape[1]
            pltpu.make_async_remote_copy(src_ref=ins[a], dst_ref=outs[a].at[:, pl.ds((1 - c) * hr, hr)],
                                         send_sem=send_sems.at[a], recv_sem=recv_sems.at[a], device_id=(x, y, 1 - c),
                                         device_id_type=MESH).wait_recv()
            cp.wait_send()
            local.wait()

    any_spec = pl.BlockSpec(memory_space=pl.ANY)
    return pl.pallas_call(
        body, name="join_halves",
        in_specs=[any_spec] * n, out_specs=[any_spec] * n,
        out_shape=[jax.ShapeDtypeStruct((a.shape[0], 2 * a.shape[1], a.shape[2]), a.dtype) for a in arrays],
        scratch_shapes=[pltpu.SemaphoreType.DMA((n,)), pltpu.SemaphoreType.DMA((n,)), pltpu.SemaphoreType.DMA((n,))],
    )(*arrays)


def _sum_devices(packed):
    def body(p_ref, o_ref, land, send_sems, recv_sems):
        x, y, c = _place()
        me = 4 * x + 2 * y + c
        land[me] = p_ref[...]
        sends = []
        for k in range(1, N_DEV):
            px, py, pc = x ^ (k >> 2), y ^ ((k >> 1) & 1), c ^ (k & 1)
            cp = pltpu.make_async_remote_copy(src_ref=p_ref, dst_ref=land.at[me], send_sem=send_sems.at[k - 1],
                                              recv_sem=recv_sems.at[k - 1], device_id=(px, py, pc), device_id_type=MESH)
            cp.start()
            sends.append(cp)
        for k in range(1, N_DEV):
            px, py, pc = x ^ (k >> 2), y ^ ((k >> 1) & 1), c ^ (k & 1)
            pltpu.make_async_remote_copy(src_ref=p_ref, dst_ref=land.at[4 * px + 2 * py + pc],
                                         send_sem=send_sems.at[k - 1], recv_sem=recv_sems.at[k - 1],
                                         device_id=(px, py, pc), device_id_type=MESH).wait_recv()
        for cp in sends:
            cp.wait_send()
        total = land[0]
        for d in range(1, N_DEV):
            total = total + land[d]
        o_ref[...] = total

    vm = pl.BlockSpec(memory_space=pltpu.VMEM)
    return pl.pallas_call(
        body, name="sum_devices", in_specs=[vm], out_specs=vm,
        out_shape=jax.ShapeDtypeStruct(packed.shape, F32),
        scratch_shapes=[pltpu.VMEM((N_DEV,) + packed.shape, F32), pltpu.SemaphoreType.DMA((N_DEV - 1,)),
                        pltpu.SemaphoreType.DMA((N_DEV - 1,))],
    )(packed)


def _adamw_math(w, g, m, v):
    m = ADAM_B1 * m + (1.0 - ADAM_B1) * g
    v = ADAM_B2 * v + (1.0 - ADAM_B2) * jnp.square(g)
    m_hat = m / (1.0 - ADAM_B1 ** ADAM_STEP)
    v_hat = v / (1.0 - ADAM_B2 ** ADAM_STEP)
    delta = -ADAM_LR * (m_hat / (jnp.sqrt(v_hat) + ADAM_EPS) + ADAM_WD * w)
    return delta, m, v


def _adamw_large(w, g, m, v):
    depth, rows, cols = w.shape
    tr = min(rows, 256)

    def body(w_ref, g_ref, m_ref, v_ref, d_ref, nm_ref, nv_ref):
        d_ref[...], nm_ref[...], nv_ref[...] = _adamw_math(w_ref[...], g_ref[...], m_ref[...], v_ref[...])

    blk = pl.BlockSpec((None, tr, cols), lambda l, i: (l, i, 0))
    return pl.pallas_call(
        body, name="adamw_large", grid=(depth, rows // tr), in_specs=[blk] * 4, out_specs=[blk] * 3,
        out_shape=[jax.ShapeDtypeStruct(w.shape, F32)] * 3,
        compiler_params=_params("parallel", "parallel"),
    )(w, g, m, v)


def _adamw_small(ws, gs, ms, vs):
    n = len(ws)

    def body(*refs):
        w_r, g_r, m_r, v_r = refs[:n], refs[n:2 * n], refs[2 * n:3 * n], refs[3 * n:4 * n]
        d_r, nm_r, nv_r = refs[4 * n:5 * n], refs[5 * n:6 * n], refs[6 * n:]
        for a in range(n):
            d_r[a][...], nm_r[a][...], nv_r[a][...] = _adamw_math(w_r[a][...], g_r[a][...], m_r[a][...], v_r[a][...])

    vm = pl.BlockSpec(memory_space=pltpu.VMEM)
    outs = pl.pallas_call(
        body, name="adamw_small", in_specs=[vm] * (4 * n), out_specs=[vm] * (3 * n),
        out_shape=[jax.ShapeDtypeStruct(w.shape, F32) for w in ws] * 3,
    )(*ws, *gs, *ms, *vs)
    return outs[:n], outs[n:2 * n], outs[2 * n:]


_LARGE = ("w_in", "w_out", "w_up", "w_down")
_SMALL = ("meta_tokens", "mix_pre_g", "conv_w", "sinks", "attn_out_g", "conv_out_g", "mix_post_g", "mlp_pre_g",
          "mlp_post_g")
_ORDER = ("meta_tokens", "mix_pre_g", "w_in", "conv_w", "sinks", "attn_out_g", "conv_out_g", "w_out", "mix_post_g",
          "mlp_pre_g", "w_up", "w_down", "mlp_post_g")


def _whole_weights(w, chip_gathered):
    g_in, g_out, g_up, g_down, g_conv, g_meta = chip_gathered
    depth = g_in.shape[1]
    whole = {
        "w_in": [jnp.transpose(g_in[:, l], (1, 0, 2)).reshape(D_MODEL, IN_W) for l in range(depth)],
        "w_out": [g_out[:, l].reshape(D_MODEL, D_MODEL) for l in range(depth)],
        "w_up": [jnp.transpose(g_up[:, l], (1, 0, 2)).reshape(D_MODEL, D_FF) for l in range(depth)],
        "w_down": [g_down[:, l].reshape(D_FF, D_MODEL) for l in range(depth)],
        "conv_w": [jnp.transpose(g_conv[:, l], (1, 0, 2)).reshape(CONV_K, CONV_W) for l in range(depth)],
        "sinks": [w["sinks"][l] for l in range(depth)],
    }
    for n in ("mix_pre_g", "attn_out_g", "conv_out_g", "mix_post_g", "mlp_pre_g", "mlp_post_g"):
        whole[n] = [w[n][l][None, :] for l in range(depth)]
    meta = jnp.transpose(g_meta, (1, 0, 2)).reshape(N_META, D_MODEL)
    return whole, meta


def _by_chip(per_layer, cols_cut):
    out = []
    for g in per_layer:
        rows, cols = g.shape
        if cols_cut:
            out.append(jnp.transpose(g.reshape(rows, N_CHIPS, cols // N_CHIPS), (1, 0, 2)))
        else:
            out.append(g.reshape(N_CHIPS, rows // N_CHIPS, cols))
    return jnp.stack(out, axis=1)


def _pad_cols(a, n=D_MODEL):
    return jnp.pad(a, ((0, 0), (0, n - a.shape[1])))


def kernel(x, meta_tokens, mix_pre_g, w_in, conv_w, sinks, attn_out_g, conv_out_g, w_out, mix_post_g, mlp_pre_g, w_up, w_down, mlp_post_g, loss_target, m_meta_tokens, m_mix_pre_g, m_w_in, m_conv_w, m_sinks, m_attn_out_g, m_conv_out_g, m_w_out, m_mix_post_g, m_mlp_pre_g, m_w_up, m_w_down, m_mlp_post_g, v_meta_tokens, v_mix_pre_g, v_w_in, v_conv_w, v_sinks, v_attn_out_g, v_conv_out_g, v_w_out, v_mix_post_g, v_mlp_pre_g, v_w_up, v_w_down, v_mlp_post_g):
    w = dict(meta_tokens=meta_tokens, mix_pre_g=mix_pre_g, w_in=w_in, conv_w=conv_w, sinks=sinks,
             attn_out_g=attn_out_g, conv_out_g=conv_out_g, w_out=w_out, mix_post_g=mix_post_g, mlp_pre_g=mlp_pre_g,
             w_up=w_up, w_down=w_down, mlp_post_g=mlp_post_g)
    m = dict(meta_tokens=m_meta_tokens, mix_pre_g=m_mix_pre_g, w_in=m_w_in, conv_w=m_conv_w, sinks=m_sinks,
             attn_out_g=m_attn_out_g, conv_out_g=m_conv_out_g, w_out=m_w_out, mix_post_g=m_mix_post_g,
             mlp_pre_g=m_mlp_pre_g, w_up=m_w_up, w_down=m_w_down, mlp_post_g=m_mlp_post_g)
    v = dict(meta_tokens=v_meta_tokens, mix_pre_g=v_mix_pre_g, w_in=v_w_in, conv_w=v_conv_w, sinks=v_sinks,
             attn_out_g=v_attn_out_g, conv_out_g=v_conv_out_g, w_out=v_w_out, mix_post_g=v_mix_post_g,
             mlp_pre_g=v_mlp_pre_g, w_up=v_w_up, w_down=v_w_down, mlp_post_g=v_mlp_post_g)
    chip = 2 * lax.axis_index("x") + lax.axis_index("y")
    seq = x.shape[1]

    gathered = _gather_chips([w["w_in"].astype(BF16), w["w_out"].astype(BF16), w["w_up"].astype(BF16),
                              w["w_down"].astype(BF16), w["conv_w"], w["meta_tokens"]])
    whole, meta = _whole_weights(w, gathered)

    h0 = jnp.concatenate([jnp.zeros((LEAD_PAD, D_MODEL), F32), meta, x[0]], axis=0)
    loss_tile, dh0, grads = _local_step(h0, loss_target[0], whole)
    loss = lax.psum(loss_tile[0, 0], ("x", "y", "c"))
    grad_x = dh0[BLOCK:][None]

    cols_cut = {"w_in": True, "w_out": False, "w_up": True, "w_down": False}
    mine = [_by_chip(grads[n], cols_cut[n]) for n in _LARGE]
    theirs = _swap_halves(mine, axis=2)
    chip_sums = [_add_half(g, r) for g, r in zip(mine, theirs)]
    arrived = _scatter_chips(chip_sums)
    halves = [_sum_chips(q) for q in arrived]
    reduced = dict(zip(_LARGE, _join_halves(halves)))

    rows = [dh0[LEAD_PAD:BLOCK]]
    for n in ("mix_pre_g", "mix_post_g", "mlp_pre_g", "mlp_post_g"):
        rows += grads[n]
    rows += [jnp.concatenate([grads["attn_out_g"][l], grads["conv_out_g"][l]], axis=1) for l in range(DEPTH)]
    rows.append(jnp.concatenate(grads["conv_w"], axis=1))
    rows.append(_pad_cols(jnp.concatenate(grads["sinks"])[None, :]))
    packed = jnp.concatenate(rows, axis=0)
    packed = jnp.pad(packed, ((0, SMALL_ROWS - packed.shape[0]), (0, 0)))
    total = _sum_devices(packed)
    r0 = N_META
    small = {
        "meta_tokens": lax.dynamic_slice(total[:N_META], (0, chip * (D_MODEL // N_CHIPS)), (N_META, D_MODEL // N_CHIPS)),
        "mix_pre_g": total[r0:r0 + 2], "mix_post_g": total[r0 + 2:r0 + 4], "mlp_pre_g": total[r0 + 4:r0 + 6],
        "mlp_post_g": total[r0 + 6:r0 + 8],
        "attn_out_g": total[r0 + 8:r0 + 10, :ATTN_W], "conv_out_g": total[r0 + 8:r0 + 10, ATTN_W:],
        "conv_w": lax.dynamic_slice(total[r0 + 10:r0 + 13].reshape(CONV_K, DEPTH, CONV_W).transpose(1, 0, 2),
                                    (0, 0, chip * (CONV_W // N_CHIPS)), (DEPTH, CONV_K, CONV_W // N_CHIPS)),
        "sinks": total[r0 + 13, :DEPTH * N_Q_HEADS].reshape(DEPTH, N_Q_HEADS),
    }

    grad, delta, new_m, new_v = {}, {}, {}, {}
    for n in _LARGE:
        grad[n] = reduced[n]
        delta[n], new_m[n], new_v[n] = _adamw_large(w[n], reduced[n], m[n], v[n])
    ds, nms, nvs = _adamw_small([w[n] for n in _SMALL], [small[n] for n in _SMALL], [m[n] for n in _SMALL],
                                [v[n] for n in _SMALL])
    for i, n in enumerate(_SMALL):
        grad[n], delta[n], new_m[n], new_v[n] = small[n], ds[i], nms[i], nvs[i]
    return (loss, grad_x, *[grad[n] for n in _ORDER], *[delta[n] for n in _ORDER], *[new_m[n] for n in _ORDER],
            *[new_v[n] for n in _ORDER])
```

```python
import functools

import jax
import jax.numpy as jnp
from jax import lax
from jax.experimental import pallas as pl
from jax.experimental.pallas import tpu as pltpu

F32 = jnp.float32
BF16 = jnp.bfloat16

D_MODEL = 1024
DEPTH = 2
N_META = 16
ATTN_W = 512
CONV_W = 512
HEAD_DIM = 64
N_Q_HEADS = 8
N_KV_HEADS = 2
GROUP = N_Q_HEADS // N_KV_HEADS
KV_W = N_KV_HEADS * HEAD_DIM
CONV_K = 3
BLOCK = 128
LEAD_PAD = BLOCK - N_META
ROPE_THETA = 500000.0
ROT_DIM = HEAD_DIM // 4
ROT_HALF = ROT_DIM // 2
D_FF = 4 * D_MODEL
IN_W = ATTN_W + 2 * KV_W + 3 * CONV_W
QKV_W = ATTN_W + 2 * KV_W
EPS = 1e-6
SCALE = HEAD_DIM ** -0.5
FF_CHUNK = 1024
N_CHIPS = 4
N_DEV = 8

ADAM_LR = 0.001
ADAM_B1 = 0.9
ADAM_B2 = 0.999
ADAM_EPS = 1e-08
ADAM_WD = 0.01
ADAM_STEP = 10

V7X_VMEM_LIMIT = 56 * 1024 * 1024
SMALL_ROWS = 32

MESH = pl.DeviceIdType.MESH


def _params(*sem):
    return pltpu.CompilerParams(dimension_semantics=sem, vmem_limit_bytes=V7X_VMEM_LIMIT)


def _row_tile(t, most):
    nb = t // BLOCK
    for b in range(most // BLOCK, 0, -1):
        if nb % b == 0:
            return b * BLOCK
    return BLOCK


def _rms(x, g):
    r = lax.rsqrt(jnp.mean(x * x, axis=-1, keepdims=True) + EPS)
    return x * r * g


def _rms_bwd(dy, x, g):
    r = lax.rsqrt(jnp.mean(x * x, axis=-1, keepdims=True) + EPS)
    xh = x * r
    dg = jnp.sum(dy * xh, axis=0, keepdims=True)
    dxh = dy * g
    dx = r * (dxh - xh * jnp.mean(dxh * xh, axis=-1, keepdims=True))
    return dx, dg


def _rope(x, cos, sa, sb):
    n = x.shape[-1]
    return x * cos + pltpu.roll(x, n - ROT_HALF, 1) * sa + pltpu.roll(x, ROT_HALF, 1) * sb


def _rope_bwd(dy, cos, sa, sb):
    n = dy.shape[-1]
    return dy * cos + pltpu.roll(dy * sa, ROT_HALF, 1) + pltpu.roll(dy * sb, n - ROT_HALF, 1)


def _rope_tables(t):
    pos = lax.broadcasted_iota(jnp.int32, (t, 2 * HEAD_DIM), 0).astype(F32) - LEAD_PAD
    dim = lax.broadcasted_iota(jnp.int32, (t, 2 * HEAD_DIM), 1) % HEAD_DIM
    pair = (dim % ROT_HALF).astype(F32)
    inv_freq = jnp.power(jnp.float32(ROPE_THETA), -(2.0 * pair) / ROT_DIM)
    ang = pos * inv_freq
    cos, sin = jnp.cos(ang), jnp.sin(ang)
    return (jnp.where(dim < ROT_DIM, cos, 1.0), jnp.where(dim < ROT_HALF, -sin, 0.0),
            jnp.where((dim >= ROT_HALF) & (dim < ROT_DIM), sin, 0.0))


def _in_proj(h, g, w, tabs, tm):
    t = h.shape[0]

    def body(h_ref, g_ref, w_ref, c_ref, sa_ref, sb_ref, a_ref, q_ref, k_ref, v_ref, b_ref, cg_ref, hc_ref):
        a = _rms(h_ref[...], g_ref[...]).astype(BF16)
        a_ref[...] = a
        p = jnp.dot(a, w_ref[...], preferred_element_type=F32)
        cos, sa, sb = c_ref[...], sa_ref[...], sb_ref[...]
        rep = ATTN_W // (2 * HEAD_DIM)
        q = _rope(p[:, :ATTN_W], jnp.tile(cos, (1, rep)), jnp.tile(sa, (1, rep)), jnp.tile(sb, (1, rep)))
        q_ref[...] = (q * SCALE).astype(BF16)
        k_ref[...] = _rope(p[:, ATTN_W:ATTN_W + KV_W], cos, sa, sb).astype(BF16)
        v_ref[...] = p[:, ATTN_W + KV_W:QKV_W].astype(BF16)
        b_ref[...] = p[:, QKV_W:QKV_W + CONV_W]
        cg_ref[...] = p[:, QKV_W + CONV_W:QKV_W + 2 * CONV_W]
        hc_ref[...] = p[:, QKV_W + 2 * CONV_W:]

    row = lambda n: pl.BlockSpec((tm, n), lambda i: (i, 0))
    full = lambda a: pl.BlockSpec(a.shape, lambda i: (0, 0))
    return pl.pallas_call(
        body, name="in_proj", grid=(t // tm,),
        in_specs=[row(D_MODEL), full(g), full(w), row(2 * HEAD_DIM), row(2 * HEAD_DIM), row(2 * HEAD_DIM)],
        out_specs=[row(D_MODEL), row(ATTN_W), row(KV_W), row(KV_W), row(CONV_W), row(CONV_W), row(CONV_W)],
        out_shape=[jax.ShapeDtypeStruct((t, D_MODEL), BF16), jax.ShapeDtypeStruct((t, ATTN_W), BF16),
                   jax.ShapeDtypeStruct((t, KV_W), BF16), jax.ShapeDtypeStruct((t, KV_W), BF16),
                   jax.ShapeDtypeStruct((t, CONV_W), F32), jax.ShapeDtypeStruct((t, CONV_W), F32),
                   jax.ShapeDtypeStruct((t, CONV_W), F32)],
        compiler_params=_params("parallel"),
    )(h, g, w, *tabs)


def _attn_bias():
    r = lax.broadcasted_iota(jnp.int32, (3, BLOCK, 2 * BLOCK), 1)
    c = lax.broadcasted_iota(jnp.int32, (3, BLOCK, 2 * BLOCK), 2)
    i = lax.broadcasted_iota(jnp.int32, (3, BLOCK, 2 * BLOCK), 0)
    ok = (c > r) & (c <= r + BLOCK) & (c + (i - 1) * BLOCK >= LEAD_PAD)
    return jnp.where(ok, 0.0, -jnp.inf).astype(F32)


def _attn_probs(q4, kg, sk, bias4):
    s = lax.dot_general(q4, kg, (((1,), (1,)), ((), ())), preferred_element_type=F32) + bias4
    m = jnp.maximum(jnp.max(s, axis=-1, keepdims=True), sk)
    e = jnp.exp(s - m)
    es = jnp.exp(sk - m)
    rden = 1.0 / (jnp.sum(e, axis=-1, keepdims=True) + es)
    return e * rden, es * rden


def _stack_heads(ref, rows, g):
    return jnp.concatenate(
        [ref[rows, (GROUP * g + j) * HEAD_DIM:(GROUP * g + j + 1) * HEAD_DIM] for j in range(GROUP)], axis=0)


def _sink_column(s_ref, g):
    return jnp.concatenate([jnp.full((BLOCK, 1), s_ref[GROUP * g + j], F32) for j in range(GROUP)], axis=0)


def _two_blocks(ref, i):
    prev = jnp.maximum(i - 1, 0)
    return jnp.concatenate([ref[pl.ds(pl.multiple_of(prev * BLOCK, BLOCK), BLOCK), :],
                            ref[pl.ds(pl.multiple_of(i * BLOCK, BLOCK), BLOCK), :]], axis=0)


def _block_bias(bias_ref, i):
    b = bias_ref[jnp.minimum(i, 2)]
    return jnp.concatenate([b] * GROUP, axis=0)


def _attn_fwd(q, k, v, bias, sinks, tm):
    t = q.shape[0]
    per_step = tm // BLOCK

    def body(s_ref, q_ref, k_ref, v_ref, bias_ref, o_ref):
        for b in range(per_step):
            i = pl.program_id(0) * per_step + b
            rows = slice(b * BLOCK, (b + 1) * BLOCK)
            kc, vc = _two_blocks(k_ref, i), _two_blocks(v_ref, i)
            bias4 = _block_bias(bias_ref, i)
            for g in range(N_KV_HEADS):
                lanes = slice(g * HEAD_DIM, (g + 1) * HEAD_DIM)
                p, _ = _attn_probs(_stack_heads(q_ref, rows, g), kc[:, lanes], _sink_column(s_ref, g), bias4)
                o4 = jnp.dot(p.astype(BF16), vc[:, lanes], preferred_element_type=F32)
                for j in range(GROUP):
                    hh = GROUP * g + j
                    o_ref[rows, hh * HEAD_DIM:(hh + 1) * HEAD_DIM] = o4[j * BLOCK:(j + 1) * BLOCK]

    whole = pl.BlockSpec((t, KV_W), lambda i: (0, 0))
    return pl.pallas_call(
        body, name="attn_fwd", grid=(t // tm,),
        in_specs=[pl.BlockSpec(memory_space=pltpu.SMEM), pl.BlockSpec((tm, ATTN_W), lambda i: (i, 0)), whole, whole,
                  pl.BlockSpec(bias.shape, lambda i: (0, 0, 0))],
        out_specs=pl.BlockSpec((tm, ATTN_W), lambda i: (i, 0)),
        out_shape=jax.ShapeDtypeStruct((t, ATTN_W), F32),
        compiler_params=_params("parallel"),
    )(sinks, q, k, v, bias)


def _shift_rows(u, halo, n):
    r = pltpu.roll(u, n, 0)
    hr = pltpu.roll(halo, n, 0)
    idx = lax.broadcasted_iota(jnp.int32, hr.shape, 0)
    return jnp.concatenate([jnp.where(idx < n, hr, r[:8]), r[8:]], axis=0)


def _advance_rows(u, halo, n):
    rows = u.shape[0]
    r = pltpu.roll(u, rows - n, 0)
    hr = pltpu.roll(halo, 8 - n, 0)
    idx = lax.broadcasted_iota(jnp.int32, hr.shape, 0)
    return jnp.concatenate([r[:rows - 8], jnp.where(idx >= 8 - n, hr, r[rows - 8:])], axis=0)


def _mix_out(h, o, b, c, hc, cw, ga, gc, w, gp, tm):
    t = h.shape[0]

    def body(h_ref, o_ref, b_ref, c_ref, hc_ref, cw_ref, ga_ref, gc_ref, w_ref, gp_ref, h1_ref, y_ref, z_ref, halo):
        @pl.when(pl.program_id(0) == 0)
        def _():
            halo[...] = jnp.zeros_like(halo)

        u = c_ref[...] * hc_ref[...]
        cv = cw_ref[0:1, :] * _shift_rows(u, halo[...], 2) + cw_ref[1:2, :] * _shift_rows(u, halo[...], 1) \
            + cw_ref[2:3, :] * u
        halo[...] = u[tm - 8:]
        yc = b_ref[...] * cv
        y = jnp.concatenate([_rms(o_ref[...], ga_ref[...]), _rms(yc, gc_ref[...])], axis=1).astype(BF16)
        y_ref[...] = y
        z = jnp.dot(y, w_ref[...], preferred_element_type=F32)
        z_ref[...] = z
        h1_ref[...] = h_ref[...] + _rms(z, gp_ref[...])

    row = lambda n: pl.BlockSpec((tm, n), lambda i: (i, 0))
    full = lambda a: pl.BlockSpec(a.shape, lambda i: (0, 0))
    return pl.pallas_call(
        body, name="mix_out", grid=(t // tm,),
        in_specs=[row(D_MODEL), row(ATTN_W), row(CONV_W), row(CONV_W), row(CONV_W), full(cw), full(ga), full(gc),
                  full(w), full(gp)],
        out_specs=[row(D_MODEL), row(D_MODEL), row(D_MODEL)],
        out_shape=[jax.ShapeDtypeStruct((t, D_MODEL), F32), jax.ShapeDtypeStruct((t, D_MODEL), BF16),
                   jax.ShapeDtypeStruct((t, D_MODEL), F32)],
        scratch_shapes=[pltpu.VMEM((8, CONV_W), F32)],
        compiler_params=_params("arbitrary"),
    )(h, o, b, c, hc, cw, ga, gc, w, gp)


def _mlp(h1, g1, wu, wd, g2, tm):
    t = h1.shape[0]
    nj = D_FF // FF_CHUNK

    def body(h1_ref, g1_ref, wu_ref, wd_ref, g2_ref, h2_ref, a2_ref, act_ref, f_ref, acc):
        j = pl.program_id(1)

        @pl.when(j == 0)
        def _():
            a2_ref[...] = _rms(h1_ref[...], g1_ref[...]).astype(BF16)

        up = jnp.dot(a2_ref[...], wu_ref[...], preferred_element_type=F32)
        act = jnp.square(jnp.maximum(up, 0.0)).astype(BF16)
        act_ref[...] = act
        part = jnp.dot(act, wd_ref[...], preferred_element_type=F32)

        @pl.when(j == 0)
        def _():
            acc[...] = part

        @pl.when(j > 0)
        def _():
            acc[...] += part

        @pl.when(j == nj - 1)
        def _():
            f = acc[...]
            f_ref[...] = f
            h2_ref[...] = h1_ref[...] + _rms(f, g2_ref[...])

    row = pl.BlockSpec((tm, D_MODEL), lambda i, j: (i, 0))
    vec = pl.BlockSpec((1, D_MODEL), lambda i, j: (0, 0))
    return pl.pallas_call(
        body, name="mlp", grid=(t // tm, nj),
        in_specs=[row, vec, pl.BlockSpec((D_MODEL, FF_CHUNK), lambda i, j: (0, j)),
                  pl.BlockSpec((FF_CHUNK, D_MODEL), lambda i, j: (j, 0)), vec],
        out_specs=[row, row, pl.BlockSpec((tm, FF_CHUNK), lambda i, j: (i, j)), row],
        out_shape=[jax.ShapeDtypeStruct((t, D_MODEL), F32), jax.ShapeDtypeStruct((t, D_MODEL), BF16),
                   jax.ShapeDtypeStruct((t, D_FF), BF16), jax.ShapeDtypeStruct((t, D_MODEL), F32)],
        scratch_shapes=[pltpu.VMEM((tm, D_MODEL), F32)],
        compiler_params=_params("parallel", "arbitrary"),
    )(h1, g1, wu, wd, g2)


def _loss_head(h, target, tm):
    t = h.shape[0]
    per_step = tm // BLOCK

    def body(h_ref, *rest):
        t_refs, (loss_ref, dh_ref) = rest[:per_step], rest[per_step:]
        i = pl.program_id(0)

        @pl.when(i == 0)
        def _():
            loss_ref[...] = jnp.zeros_like(loss_ref)

        total = jnp.zeros((), F32)
        for b in range(per_step):
            rows = slice(b * BLOCK, (b + 1) * BLOCK)
            err = h_ref[rows, :] - t_refs[b][...]
            if b == 0:
                err = jnp.where(i == 0, 0.0, err)
            dh_ref[rows, :] = err * (1.0 / D_MODEL)
            total = total + jnp.sum(err * err)
        loss_ref[...] += total * (0.5 / D_MODEL)

    def target_block(b):
        return pl.BlockSpec((BLOCK, D_MODEL), lambda i: (jnp.maximum(i * per_step + b - 1, 0), 0))

    return pl.pallas_call(
        body, name="loss_head", grid=(t // tm,),
        in_specs=[pl.BlockSpec((tm, D_MODEL), lambda i: (i, 0))] + [target_block(b) for b in range(per_step)],
        out_specs=[pl.BlockSpec((8, 128), lambda i: (0, 0)), pl.BlockSpec((tm, D_MODEL), lambda i: (i, 0))],
        out_shape=[jax.ShapeDtypeStruct((8, 128), F32), jax.ShapeDtypeStruct((t, D_MODEL), F32)],
        compiler_params=_params("arbitrary"),
    )(h, *([target] * per_step))


def _mlp_bwd(dh2, f, g2, act, wd, wu, h1, g1, tm):
    t = dh2.shape[0]
    nj = D_FF // FF_CHUNK

    def body(dh2_ref, f_ref, g2_ref, act_ref, wd_ref, wu_ref, h1_ref, g1_ref, dh1_ref, df_ref, dup_ref, dg2_ref,
             dg1_ref, acc):
        i, j = pl.program_id(0), pl.program_id(1)

        @pl.when((i == 0) & (j == 0))
        def _():
            dg2_ref[...] = jnp.zeros_like(dg2_ref)
            dg1_ref[...] = jnp.zeros_like(dg1_ref)

        @pl.when(j == 0)
        def _():
            df, dg = _rms_bwd(dh2_ref[...], f_ref[...], g2_ref[...])
            df_ref[...] = df.astype(BF16)
            dg2_ref[...] += dg

        dact = lax.dot_general(df_ref[...], wd_ref[...], (((1,), (1,)), ((), ())), preferred_element_type=F32)
        dup = (dact * (2.0 * jnp.sqrt(act_ref[...].astype(F32)))).astype(BF16)
        dup_ref[...] = dup
        part = lax.dot_general(dup, wu_ref[...], (((1,), (1,)), ((), ())), preferred_element_type=F32)

        @pl.when(j == 0)
        def _():
            acc[...] = part

        @pl.when(j > 0)
        def _():
            acc[...] += part

        @pl.when(j == nj - 1)
        def _():
            dx, dg = _rms_bwd(acc[...], h1_ref[...], g1_ref[...])
            dh1_ref[...] = dh2_ref[...] + dx
            dg1_ref[...] += dg

    row = pl.BlockSpec((tm, D_MODEL), lambda i, j: (i, 0))
    vec = pl.BlockSpec((1, D_MODEL), lambda i, j: (0, 0))
    chunk = pl.BlockSpec((tm, FF_CHUNK), lambda i, j: (i, j))
    return pl.pallas_call(
        body, name="mlp_bwd", grid=(t // tm, nj),
        in_specs=[row, row, vec, chunk, pl.BlockSpec((FF_CHUNK, D_MODEL), lambda i, j: (j, 0)),
                  pl.BlockSpec((D_MODEL, FF_CHUNK), lambda i, j: (0, j)), row, vec],
        out_specs=[row, row, chunk, vec, vec],
        out_shape=[jax.ShapeDtypeStruct((t, D_MODEL), F32), jax.ShapeDtypeStruct((t, D_MODEL), BF16),
                   jax.ShapeDtypeStruct((t, D_FF), BF16), jax.ShapeDtypeStruct((1, D_MODEL), F32),
                   jax.ShapeDtypeStruct((1, D_MODEL), F32)],
        scratch_shapes=[pltpu.VMEM((tm, D_MODEL), F32)],
        compiler_params=_params("arbitrary", "arbitrary"),
    )(dh2, f, g2, act, wd, wu, h1, g1)


def _weight_grad(x, y, tm, name):
    t, k = x.shape
    n = y.shape[1]
    tk = min(k, 1024)
    tn = n if n <= 1024 else (1152 if n % 1152 == 0 else 1024)

    def body(x_ref, y_ref, o_ref):
        @pl.when(pl.program_id(2) == 0)
        def _():
            o_ref[...] = jnp.zeros_like(o_ref)

        o_ref[...] += lax.dot_general(x_ref[...], y_ref[...], (((0,), (0,)), ((), ())), preferred_element_type=F32)

    return pl.pallas_call(
        body, name=name, grid=(k // tk, n // tn, t // tm),
        in_specs=[pl.BlockSpec((tm, tk), lambda a, b, r: (r, a)), pl.BlockSpec((tm, tn), lambda a, b, r: (r, b))],
        out_specs=pl.BlockSpec((tk, tn), lambda a, b, r: (a, b)),
        out_shape=jax.ShapeDtypeStruct((k, n), F32),
        compiler_params=_params("parallel", "parallel", "arbitrary"),
    )(x, y)


def _mix_out_bwd(dh1, z, gp, w, o, b, c, hc, cw, ga, gc, tm):
    t = dh1.shape[0]
    nt = t // tm
    per8 = tm // 8

    def body(dh1_ref, z_ref, gp_ref, w_ref, o_ref, b_ref, c_ref, hc_ref, cp_ref, hp_ref, cw_ref, ga_ref, gc_ref,
             dz_ref, do_ref, dbch_ref, dgp_ref, dga_ref, dgc_ref, dcw_ref, halo):
        i = pl.program_id(0)

        @pl.when(i == 0)
        def _():
            halo[...] = jnp.zeros_like(halo)
            dgp_ref[...] = jnp.zeros_like(dgp_ref)
            dga_ref[...] = jnp.zeros_like(dga_ref)
            dgc_ref[...] = jnp.zeros_like(dgc_ref)
            dcw_ref[...] = jnp.zeros_like(dcw_ref)

        dz, dgp = _rms_bwd(dh1_ref[...], z_ref[...], gp_ref[...])
        dgp_ref[...] += dgp
        dz = dz.astype(BF16)
        dz_ref[...] = dz
        dy = lax.dot_general(dz, w_ref[...], (((1,), (1,)), ((), ())), preferred_element_type=F32)
        do, dga = _rms_bwd(dy[:, :ATTN_W], o_ref[...], ga_ref[...])
        do_ref[...] = do
        dga_ref[...] += dga

        u = c_ref[...] * hc_ref[...]
        first = i == nt - 1
        u_before = jnp.where(first, 0.0, cp_ref[...] * hp_ref[...])
        u1 = _shift_rows(u, u_before, 1)
        u2 = _shift_rows(u, u_before, 2)
        cv = cw_ref[0:1, :] * u2 + cw_ref[1:2, :] * u1 + cw_ref[2:3, :] * u
        bb = b_ref[...]
        dyc, dgc = _rms_bwd(dy[:, ATTN_W:], bb * cv, gc_ref[...])
        dgc_ref[...] += dgc
        dcv = dyc * bb
        d1 = _advance_rows(dcv, halo[...], 1)
        d2 = _advance_rows(dcv, halo[...], 2)
        halo[...] = dcv[:8]
        du = cw_ref[2:3, :] * dcv + cw_ref[1:2, :] * d1 + cw_ref[0:1, :] * d2
        dbch_ref[...] = jnp.concatenate([dyc * cv, du * hc_ref[...], du * c_ref[...]], axis=1).astype(BF16)
        dcw_ref[...] += jnp.concatenate([jnp.sum(dcv * u2, axis=0, keepdims=True),
                                         jnp.sum(dcv * u1, axis=0, keepdims=True),
                                         jnp.sum(dcv * u, axis=0, keepdims=True)], axis=0)

    row = lambda n: pl.BlockSpec((tm, n), lambda i: (nt - 1 - i, 0))
    before = pl.BlockSpec((8, CONV_W), lambda i: (jnp.maximum((nt - 1 - i) * per8 - 1, 0), 0))
    full = lambda a: pl.BlockSpec(a.shape, lambda i: (0, 0))
    vec = lambda n: pl.BlockSpec((1, n), lambda i: (0, 0))
    return pl.pallas_call(
        body, name="mix_out_bwd", grid=(nt,),
        in_specs=[row(D_MODEL), row(D_MODEL), full(gp), full(w), row(ATTN_W), row(CONV_W), row(CONV_W), row(CONV_W),
                  before, before, full(cw), full(ga), full(gc)],
        out_specs=[row(D_MODEL), row(ATTN_W), row(3 * CONV_W), vec(D_MODEL), vec(ATTN_W), vec(CONV_W),
                   pl.BlockSpec((CONV_K, CONV_W), lambda i: (0, 0))],
        out_shape=[jax.ShapeDtypeStruct((t, D_MODEL), BF16), jax.ShapeDtypeStruct((t, ATTN_W), F32),
                   jax.ShapeDtypeStruct((t, 3 * CONV_W), BF16), jax.ShapeDtypeStruct((1, D_MODEL), F32),
                   jax.ShapeDtypeStruct((1, ATTN_W), F32), jax.ShapeDtypeStruct((1, CONV_W), F32),
                   jax.ShapeDtypeStruct((CONV_K, CONV_W), F32)],
        scratch_shapes=[pltpu.VMEM((8, CONV_W), F32)],
        compiler_params=_params("arbitrary"),
    )(dh1, z, gp, w, o, b, c, hc, c, hc, cw, ga, gc)


def _attn_bwd(q, k, v, o, do, bias, sinks, tm):
    t = q.shape[0]
    per_step = tm // BLOCK

    def body(s_ref, q_ref, k_ref, v_ref, o_ref, do_ref, bias_ref, dq_ref, dk_ref, dv_ref, ds_ref):
        step = pl.program_id(0)

        @pl.when(step == 0)
        def _():
            ds_ref[...] = jnp.zeros_like(ds_ref)

        dsink = [jnp.zeros((GROUP * BLOCK, 1), F32) for _ in range(N_KV_HEADS)]
        ahead = None
        for b in range(per_step):
            i = step * per_step + b
            rows = slice(b * BLOCK, (b + 1) * BLOCK)
            kc, vc = _two_blocks(k_ref, i), _two_blocks(v_ref, i)
            bias4 = _block_bias(bias_ref, i)
            dkg, dvg = [], []
            for g in range(N_KV_HEADS):
                lanes = slice(g * HEAD_DIM, (g + 1) * HEAD_DIM)
                q4 = _stack_heads(q_ref, rows, g)
                p, ps = _attn_probs(q4, kc[:, lanes], _sink_column(s_ref, g), bias4)
                o4, do4 = _stack_heads(o_ref, rows, g), _stack_heads(do_ref, rows, g)
                do4b = do4.astype(BF16)
                dp = lax.dot_general(do4b, vc[:, lanes], (((1,), (1,)), ((), ())), preferred_element_type=F32)
                drow = jnp.sum(do4 * o4, axis=-1, keepdims=True)
                ds = (p * (dp - drow)).astype(BF16)
                dsink[g] = dsink[g] + ps * drow
                dq4 = jnp.dot(ds, kc[:, lanes], preferred_element_type=F32) * SCALE
                dkg.append(lax.dot_general(ds, q4, (((0,), (0,)), ((), ())), preferred_element_type=F32))
                dvg.append(lax.dot_general(p.astype(BF16), do4b, (((0,), (0,)), ((), ())),
                                           preferred_element_type=F32))
                for j in range(GROUP):
                    hh = GROUP * g + j
                    dq_ref[rows, hh * HEAD_DIM:(hh + 1) * HEAD_DIM] = dq4[j * BLOCK:(j + 1) * BLOCK]
            dkb, dvb = jnp.concatenate(dkg, axis=1), jnp.concatenate(dvg, axis=1)
            if b == 0:
                @pl.when(step > 0)
                def _():
                    before = pl.ds(pl.multiple_of((i - 1) * BLOCK, BLOCK), BLOCK)
                    dk_ref[before, :] += dkb[:BLOCK]
                    dv_ref[before, :] += dvb[:BLOCK]
            else:
                at = pl.ds(pl.multiple_of((i - 1) * BLOCK, BLOCK), BLOCK)
                dk_ref[at, :] = ahead[0] + dkb[:BLOCK]
                dv_ref[at, :] = ahead[1] + dvb[:BLOCK]
            ahead = (dkb[BLOCK:], dvb[BLOCK:])
        last = pl.ds(pl.multiple_of(((step + 1) * per_step - 1) * BLOCK, BLOCK), BLOCK)
        dk_ref[last, :] = ahead[0]
        dv_ref[last, :] = ahead[1]
        for g in range(N_KV_HEADS):
            for j in range(GROUP):
                hh = GROUP * g + j
                ds_ref[hh:hh + 1, :] -= jnp.sum(dsink[g][j * BLOCK:(j + 1) * BLOCK])

    whole = pl.BlockSpec((t, KV_W), lambda i: (0, 0))
    blk = pl.BlockSpec((tm, ATTN_W), lambda i: (i, 0))
    return pl.pallas_call(
        body, name="attn_bwd", grid=(t // tm,),
        in_specs=[pl.BlockSpec(memory_space=pltpu.SMEM), blk, whole, whole, blk, blk,
                  pl.BlockSpec(bias.shape, lambda i: (0, 0, 0))],
        out_specs=[blk, whole, whole, pl.BlockSpec((N_Q_HEADS, 128), lambda i: (0, 0))],
        out_shape=[jax.ShapeDtypeStruct((t, ATTN_W), F32), jax.ShapeDtypeStruct((t, KV_W), F32),
                   jax.ShapeDtypeStruct((t, KV_W), F32), jax.ShapeDtypeStruct((N_Q_HEADS, 128), F32)],
        compiler_params=_params("arbitrary"),
    )(sinks, q, k, v, o, do, bias)


def _in_proj_bwd(dq, dk, dv, dbch, w, dh1, h, g, tabs, tm):
    t = h.shape[0]

    def body(dq_ref, dk_ref, dv_ref, dbch_ref, w_ref, dh1_ref, h_ref, g_ref, c_ref, sa_ref, sb_ref, dh_ref, dp_ref,
             dg_ref):
        @pl.when(pl.program_id(0) == 0)
        def _():
            dg_ref[...] = jnp.zeros_like(dg_ref)

        cos, sa, sb = c_ref[...], sa_ref[...], sb_ref[...]
        rep = ATTN_W // (2 * HEAD_DIM)
        dqr = _rope_bwd(dq_ref[...], jnp.tile(cos, (1, rep)), jnp.tile(sa, (1, rep)), jnp.tile(sb, (1, rep)))
        dkr = _rope_bwd(dk_ref[...], cos, sa, sb)
        dp = jnp.concatenate([dqr.astype(BF16), dkr.astype(BF16), dv_ref[...].astype(BF16), dbch_ref[...]], axis=1)
        dp_ref[...] = dp
        da = lax.dot_general(dp, w_ref[...], (((1,), (1,)), ((), ())), preferred_element_type=F32)
        dx, dg = _rms_bwd(da, h_ref[...], g_ref[...])
        dh_ref[...] = dh1_ref[...] + dx
        dg_ref[...] += dg

    row = lambda n: pl.BlockSpec((tm, n), lambda i: (i, 0))
    full = lambda a: pl.BlockSpec(a.shape, lambda i: (0, 0))
    return pl.pallas_call(
        body, name="in_proj_bwd", grid=(t // tm,),
        in_specs=[row(ATTN_W), row(KV_W), row(KV_W), row(3 * CONV_W), full(w), row(D_MODEL), row(D_MODEL), full(g),
                  row(2 * HEAD_DIM), row(2 * HEAD_DIM), row(2 * HEAD_DIM)],
        out_specs=[row(D_MODEL), row(IN_W), pl.BlockSpec((1, D_MODEL), lambda i: (0, 0))],
        out_shape=[jax.ShapeDtypeStruct((t, D_MODEL), F32), jax.ShapeDtypeStruct((t, IN_W), BF16),
                   jax.ShapeDtypeStruct((1, D_MODEL), F32)],
        compiler_params=_params("arbitrary"),
    )(dq, dk, dv, dbch, w, dh1, h, g, *tabs)


def _local_step(h0, target, p):
    t = h0.shape[0]
    tm = _row_tile(t, 640)
    ts = _row_tile(t, 320)
    tabs = _rope_tables(t)
    bias = _attn_bias()
    saved = []
    h = h0
    for l in range(DEPTH):
        a, q, k, v, b, c, hc = _in_proj(h, p["mix_pre_g"][l], p["w_in"][l], tabs, ts)
        o = _attn_fwd(q, k, v, bias, p["sinks"][l], tm)
        h1, y, z = _mix_out(h, o, b, c, hc, p["conv_w"][l], p["attn_out_g"][l], p["conv_out_g"][l], p["w_out"][l],
                            p["mix_post_g"][l], ts)
        h2, a2, act, f = _mlp(h1, p["mlp_pre_g"][l], p["w_up"][l], p["w_down"][l], p["mlp_post_g"][l], tm)
        saved.append((h, a, q, k, v, b, c, hc, o, h1, y, z, a2, act, f))
        h = h2
    loss_tile, dh = _loss_head(h, target, tm)

    names = ("w_in", "w_out", "w_up", "w_down", "mix_pre_g", "mix_post_g", "mlp_pre_g", "mlp_post_g", "attn_out_g",
             "conv_out_g", "conv_w", "sinks")
    grads = {n: [None] * DEPTH for n in names}
    for l in reversed(range(DEPTH)):
        h_in, a, q, k, v, b, c, hc, o, h1, y, z, a2, act, f = saved[l]
        dh1, df, dup, dg2, dg1 = _mlp_bwd(dh, f, p["mlp_post_g"][l], act, p["w_down"][l], p["w_up"][l], h1,
                                          p["mlp_pre_g"][l], tm)
        grads["w_down"][l] = _weight_grad(act, df, tm, "grad_w_down")
        grads["w_up"][l] = _weight_grad(a2, dup, tm, "grad_w_up")
        dz, do, dbch, dgp, dga, dgc, dcw = _mix_out_bwd(dh1, z, p["mix_post_g"][l], p["w_out"][l], o, b, c, hc,
                                                        p["conv_w"][l], p["attn_out_g"][l], p["conv_out_g"][l], ts)
        grads["w_out"][l] = _weight_grad(y, dz, tm, "grad_w_out")
        dq, dk, dv, dsink = _attn_bwd(q, k, v, o, do, bias, p["sinks"][l], tm)
        dh, dproj, dgi = _in_proj_bwd(dq, dk, dv, dbch, p["w_in"][l], dh1, h_in, p["mix_pre_g"][l], tabs, ts)
        grads["w_in"][l] = _weight_grad(a, dproj, tm, "grad_w_in")
        grads["mlp_post_g"][l], grads["mlp_pre_g"][l], grads["mix_post_g"][l] = dg2, dg1, dgp
        grads["attn_out_g"][l], grads["conv_out_g"][l], grads["conv_w"][l] = dga, dgc, dcw
        grads["mix_pre_g"][l], grads["sinks"][l] = dgi, dsink[:, 0]
    return loss_tile, dh, grads


def _place():
    return lax.axis_index("x"), lax.axis_index("y"), lax.axis_index("c")


def _other_chips(x, y):
    return [(1 - x, y), (x, 1 - y), (1 - x, 1 - y)]


def _gather_chips(arrays):
    n = len(arrays)

    def body(*refs):
        ins, outs = refs[:n], refs[n:2 * n]
        send_sems, recv_sems, local_sems = refs[2 * n:]
        x, y, c = _place()
        chips = _other_chips(x, y)
        me = 2 * x + y
        started = []
        for a in range(n):
            mine = pltpu.make_async_copy(ins[a], outs[a].at[me], local_sems.at[a])
            mine.start()
            started.append(mine)
        sends = []
        for a in range(n):
            for j, (px, py) in enumerate(chips):
                cp = pltpu.make_async_remote_copy(src_ref=ins[a], dst_ref=outs[a].at[me], send_sem=send_sems.at[3 * a + j],
                                                  recv_sem=recv_sems.at[3 * a + j], device_id=(px, py, c),
                                                  device_id_type=MESH)
                cp.start()
                sends.append(cp)
        for a in range(n):
            for j, (px, py) in enumerate(chips):
                pltpu.make_async_remote_copy(src_ref=ins[a], dst_ref=outs[a].at[2 * px + py],
                                             send_sem=send_sems.at[3 * a + j], recv_sem=recv_sems.at[3 * a + j],
                                             device_id=(px, py, c), device_id_type=MESH).wait_recv()
        for cp in sends:
            cp.wait_send()
        for cp in started:
            cp.wait()

    any_spec = pl.BlockSpec(memory_space=pl.ANY)
    return pl.pallas_call(
        body, name="gather_chips",
        in_specs=[any_spec] * n, out_specs=[any_spec] * n,
        out_shape=[jax.ShapeDtypeStruct((N_CHIPS,) + a.shape, a.dtype) for a in arrays],
        scratch_shapes=[pltpu.SemaphoreType.DMA((3 * n,)), pltpu.SemaphoreType.DMA((3 * n,)),
                        pltpu.SemaphoreType.DMA((n,))],
    )(*arrays)


def _swap_halves(arrays, axis):
    n = len(arrays)

    def half(ref, which, a):
        size = arrays[a].shape[axis] // 2
        idx = [slice(None)] * axis + [pl.ds(which * size, size)]
        return ref.at[tuple(idx)]

    def body(*refs):
        ins, outs = refs[:n], refs[n:2 * n]
        send_sems, recv_sems = refs[2 * n:]
        x, y, c = _place()
        sends = []
        for a in range(n):
            cp = pltpu.make_async_remote_copy(src_ref=half(ins[a], 1 - c, a), dst_ref=outs[a], send_sem=send_sems.at[a],
                                              recv_sem=recv_sems.at[a], device_id=(x, y, 1 - c), device_id_type=MESH)
            cp.start()
            sends.append(cp)
        for cp in sends:
            cp.wait()

    def out_shape(a):
        s = list(a.shape)
        s[axis] //= 2
        return jax.ShapeDtypeStruct(tuple(s), a.dtype)

    any_spec = pl.BlockSpec(memory_space=pl.ANY)
    return pl.pallas_call(
        body, name="swap_halves",
        in_specs=[any_spec] * n, out_specs=[any_spec] * n, out_shape=[out_shape(a) for a in arrays],
        scratch_shapes=[pltpu.SemaphoreType.DMA((n,)), pltpu.SemaphoreType.DMA((n,))],
    )(*arrays)


def _add_half(g, r):
    rows, cols = g.shape[2], g.shape[3]
    hr = rows // 2
    tr = min(hr, 256)
    per = hr // tr
    first = (lax.axis_index("c") * per).astype(jnp.int32).reshape(1)

    def body(first_ref, g_ref, r_ref, o_ref):
        o_ref[...] = (g_ref[...] + r_ref[...]).astype(BF16)

    blk = pl.BlockSpec((None, None, tr, cols), lambda s, l, i, first_ref: (s, l, i, 0))
    return pl.pallas_call(
        body, name="add_half",
        grid_spec=pltpu.PrefetchScalarGridSpec(
            num_scalar_prefetch=1, grid=(N_CHIPS, DEPTH, per),
            in_specs=[pl.BlockSpec((None, None, tr, cols), lambda s, l, i, first_ref: (s, l, i + first_ref[0], 0)), blk],
            out_specs=blk),
        out_shape=jax.ShapeDtypeStruct(r.shape, BF16),
        compiler_params=_params("parallel", "parallel", "parallel"),
    )(first, g, r)


def _scatter_chips(arrays):
    n = len(arrays)

    def body(*refs):
        ins, outs = refs[:n], refs[n:2 * n]
        send_sems, recv_sems, local_sems = refs[2 * n:]
        x, y, c = _place()
        chips = _other_chips(x, y)
        me = 2 * x + y
        started, sends = [], []
        for a in range(n):
            mine = pltpu.make_async_copy(ins[a].at[me], outs[a].at[me], local_sems.at[a])
            mine.start()
            started.append(mine)
        for a in range(n):
            for j, (px, py) in enumerate(chips):
                cp = pltpu.make_async_remote_copy(src_ref=ins[a].at[2 * px + py], dst_ref=outs[a].at[me],
                                                  send_sem=send_sems.at[3 * a + j], recv_sem=recv_sems.at[3 * a + j],
                                                  device_id=(px, py, c), device_id_type=MESH)
                cp.start()
                sends.append(cp)
        for a in range(n):
            for j, (px, py) in enumerate(chips):
                pltpu.make_async_remote_copy(src_ref=ins[a].at[me], dst_ref=outs[a].at[2 * px + py],
                                             send_sem=send_sems.at[3 * a + j], recv_sem=recv_sems.at[3 * a + j],
                                             device_id=(px, py, c), device_id_type=MESH).wait_recv()
        for cp in sends:
            cp.wait_send()
        for cp in started:
            cp.wait()

    any_spec = pl.BlockSpec(memory_space=pl.ANY)
    return pl.pallas_call(
        body, name="scatter_chips",
        in_specs=[any_spec] * n, out_specs=[any_spec] * n,
        out_shape=[jax.ShapeDtypeStruct(a.shape, a.dtype) for a in arrays],
        scratch_shapes=[pltpu.SemaphoreType.DMA((3 * n,)), pltpu.SemaphoreType.DMA((3 * n,)),
                        pltpu.SemaphoreType.DMA((n,))],
    )(*arrays)


def _sum_chips(q):
    rows, cols = q.shape[2], q.shape[3]
    tr = min(rows, 256)

    def body(q_ref, o_ref):
        part = [q_ref[s].astype(F32) for s in range(N_CHIPS)]
        o_ref[...] = ((part[0] + part[1]) + part[2]) + part[3]

    return pl.pallas_call(
        body, name="sum_chips", grid=(DEPTH, rows // tr),
        in_specs=[pl.BlockSpec((N_CHIPS, None, tr, cols), lambda l, i: (0, l, i, 0))],
        out_specs=pl.BlockSpec((None, tr, cols), lambda l, i: (l, i, 0)),
        out_shape=jax.ShapeDtypeStruct(q.shape[1:], F32),
        compiler_params=_params("parallel", "parallel"),
    )(q)


def _join_halves(arrays):
    n = len(arrays)

    def body(*refs):
        ins, outs = refs[:n], refs[n:2 * n]
        send_sems, recv_sems, local_sems = refs[2 * n:]
        x, y, c = _place()
        started = []
        for a in range(n):
            hr = arrays[a].shape[1]
            mine_rows = outs[a].at[:, pl.ds(c * hr, hr)]
            local = pltpu.make_async_copy(ins[a], mine_rows, local_sems.at[a])
            local.start()
            cp = pltpu.make_async_remote_copy(src_ref=ins[a], dst_ref=mine_rows, send_sem=send_sems.at[a],
                                              recv_sem=recv_sems.at[a], device_id=(x, y, 1 - c), device_id_type=MESH)
            cp.start()
            started.append((local, cp))
        for a, (local, cp) in enumerate(started):
            hr = arrays[a].shape[1]
            pltpu.make_async_remote_copy(src_ref=ins[a], dst_ref=outs[a].at[:, pl.ds((1 - c) * hr, hr)],
                                         send_sem=send_sems.at[a], recv_sem=recv_sems.at[a], device_id=(x, y, 1 - c),
                                         device_id_type=MESH).wait_recv()
            cp.wait_send()
            local.wait()

    any_spec = pl.BlockSpec(memory_space=pl.ANY)
    return pl.pallas_call(
        body, name="join_halves",
        in_specs=[any_spec] * n, out_specs=[any_spec] * n,
        out_shape=[jax.ShapeDtypeStruct((a.shape[0], 2 * a.shape[1], a.shape[2]), a.dtype) for a in arrays],
        scratch_shapes=[pltpu.SemaphoreType.DMA((n,)), pltpu.SemaphoreType.DMA((n,)), pltpu.SemaphoreType.DMA((n,))],
    )(*arrays)


def _sum_devices(packed):
    def body(p_ref, o_ref, land, send_sems, recv_sems):
        x, y, c = _place()
        me = 4 * x + 2 * y + c
        land[me] = p_ref[...]
        sends = []
        for k in range(1, N_DEV):
            px, py, pc = x ^ (k >> 2), y ^ ((k >> 1) & 1), c ^ (k & 1)
            cp = pltpu.make_async_remote_copy(src_ref=p_ref, dst_ref=land.at[me], send_sem=send_sems.at[k - 1],
                                              recv_sem=recv_sems.at[k - 1], device_id=(px, py, pc), device_id_type=MESH)
            cp.start()
            sends.append(cp)
        for k in range(1, N_DEV):
            px, py, pc = x ^ (k >> 2), y ^ ((k >> 1) & 1), c ^ (k & 1)
            pltpu.make_async_remote_copy(src_ref=p_ref, dst_ref=land.at[4 * px + 2 * py + pc],
                                         send_sem=send_sems.at[k - 1], recv_sem=recv_sems.at[k - 1],
                                         device_id=(px, py, pc), device_id_type=MESH).wait_recv()
        for cp in sends:
            cp.wait_send()
        total = land[0]
        for d in range(1, N_DEV):
            total = total + land[d]
        o_ref[...] = total

    vm = pl.BlockSpec(memory_space=pltpu.VMEM)
    return pl.pallas_call(
        body, name="sum_devices", in_specs=[vm], out_specs=vm,
        out_shape=jax.ShapeDtypeStruct(packed.shape, F32),
        scratch_shapes=[pltpu.VMEM((N_DEV,) + packed.shape, F32), pltpu.SemaphoreType.DMA((N_DEV - 1,)),
                        pltpu.SemaphoreType.DMA((N_DEV - 1,))],
    )(packed)


def _adamw_math(w, g, m, v):
    m = ADAM_B1 * m + (1.0 - ADAM_B1) * g
    v = ADAM_B2 * v + (1.0 - ADAM_B2) * jnp.square(g)
    m_hat = m / (1.0 - ADAM_B1 ** ADAM_STEP)
    v_hat = v / (1.0 - ADAM_B2 ** ADAM_STEP)
    delta = -ADAM_LR * (m_hat / (jnp.sqrt(v_hat) + ADAM_EPS) + ADAM_WD * w)
    return delta, m, v


def _adamw_large(w, g, m, v):
    depth, rows, cols = w.shape
    tr = min(rows, 256)

    def body(w_ref, g_ref, m_ref, v_ref, d_ref, nm_ref, nv_ref):
        d_ref[...], nm_ref[...], nv_ref[...] = _adamw_math(w_ref[...], g_ref[...], m_ref[...], v_ref[...])

    blk = pl.BlockSpec((None, tr, cols), lambda l, i: (l, i, 0))
    return pl.pallas_call(
        body, name="adamw_large", grid=(depth, rows // tr), in_specs=[blk] * 4, out_specs=[blk] * 3,
        out_shape=[jax.ShapeDtypeStruct(w.shape, F32)] * 3,
        compiler_params=_params("parallel", "parallel"),
    )(w, g, m, v)


def _adamw_small(ws, gs, ms, vs):
    n = len(ws)

    def body(*refs):
        w_r, g_r, m_r, v_r = refs[:n], refs[n:2 * n], refs[2 * n:3 * n], refs[3 * n:4 * n]
        d_r, nm_r, nv_r = refs[4 * n:5 * n], refs[5 * n:6 * n], refs[6 * n:]
        for a in range(n):
            d_r[a][...], nm_r[a][...], nv_r[a][...] = _adamw_math(w_r[a][...], g_r[a][...], m_r[a][...], v_r[a][...])

    vm = pl.BlockSpec(memory_space=pltpu.VMEM)
    outs = pl.pallas_call(
        body, name="adamw_small", in_specs=[vm] * (4 * n), out_specs=[vm] * (3 * n),
        out_shape=[jax.ShapeDtypeStruct(w.shape, F32) for w in ws] * 3,
    )(*ws, *gs, *ms, *vs)
    return outs[:n], outs[n:2 * n], outs[2 * n:]


_LARGE = ("w_in", "w_out", "w_up", "w_down")
_SMALL = ("meta_tokens", "mix_pre_g", "conv_w", "sinks", "attn_out_g", "conv_out_g", "mix_post_g", "mlp_pre_g",
          "mlp_post_g")
_ORDER = ("meta_tokens", "mix_pre_g", "w_in", "conv_w", "sinks", "attn_out_g", "conv_out_g", "w_out", "mix_post_g",
          "mlp_pre_g", "w_up", "w_down", "mlp_post_g")


def _whole_weights(w, chip_gathered):
    g_in, g_out, g_up, g_down, g_conv, g_meta = chip_gathered
    depth = g_in.shape[1]
    whole = {
        "w_in": [jnp.transpose(g_in[:, l], (1, 0, 2)).reshape(D_MODEL, IN_W) for l in range(depth)],
        "w_out": [g_out[:, l].reshape(D_MODEL, D_MODEL) for l in range(depth)],
        "w_up": [jnp.transpose(g_up[:, l], (1, 0, 2)).reshape(D_MODEL, D_FF) for l in range(depth)],
        "w_down": [g_down[:, l].reshape(D_FF, D_MODEL) for l in range(depth)],
        "conv_w": [jnp.transpose(g_conv[:, l], (1, 0, 2)).reshape(CONV_K, CONV_W) for l in range(depth)],
        "sinks": [w["sinks"][l] for l in range(depth)],
    }
    for n in ("mix_pre_g", "attn_out_g", "conv_out_g", "mix_post_g", "mlp_pre_g", "mlp_post_g"):
        whole[n] = [w[n][l][None, :] for l in range(depth)]
    meta = jnp.transpose(g_meta, (1, 0, 2)).reshape(N_META, D_MODEL)
    return whole, meta


def _by_chip(per_layer, cols_cut):
    out = []
    for g in per_layer:
        rows, cols = g.shape
        if cols_cut:
            out.append(jnp.transpose(g.reshape(rows, N_CHIPS, cols // N_CHIPS), (1, 0, 2)))
        else:
            out.append(g.reshape(N_CHIPS, rows // N_CHIPS, cols))
    return jnp.stack(out, axis=1)


def _pad_cols(a, n=D_MODEL):
    return jnp.pad(a, ((0, 0), (0, n - a.shape[1])))


def kernel(x, meta_tokens, mix_pre_g, w_in, conv_w, sinks, attn_out_g, conv_out_g, w_out, mix_post_g, mlp_pre_g, w_up, w_down, mlp_post_g, loss_target, m_meta_tokens, m_mix_pre_g, m_w_in, m_conv_w, m_sinks, m_attn_out_g, m_conv_out_g, m_w_out, m_mix_post_g, m_mlp_pre_g, m_w_up, m_w_down, m_mlp_post_g, v_meta_tokens, v_mix_pre_g, v_w_in, v_conv_w, v_sinks, v_attn_out_g, v_conv_out_g, v_w_out, v_mix_post_g, v_mlp_pre_g, v_w_up, v_w_down, v_mlp_post_g):
    w = dict(meta_tokens=meta_tokens, mix_pre_g=mix_pre_g, w_in=w_in, conv_w=conv_w, sinks=sinks,
             attn_out_g=attn_out_g, conv_out_g=conv_out_g, w_out=w_out, mix_post_g=mix_post_g, mlp_pre_g=mlp_pre_g,
             w_up=w_up, w_down=w_down, mlp_post_g=mlp_post_g)
    m = dict(meta_tokens=m_meta_tokens, mix_pre_g=m_mix_pre_g, w_in=m_w_in, conv_w=m_conv_w, sinks=m_sinks,
             attn_out_g=m_attn_out_g, conv_out_g=m_conv_out_g, w_out=m_w_out, mix_post_g=m_mix_post_g,
             mlp_pre_g=m_mlp_pre_g, w_up=m_w_up, w_down=m_w_down, mlp_post_g=m_mlp_post_g)
    v = dict(meta_tokens=v_meta_tokens, mix_pre_g=v_mix_pre_g, w_in=v_w_in, conv_w=v_conv_w, sinks=v_sinks,
             attn_out_g=v_attn_out_g, conv_out_g=v_conv_out_g, w_out=v_w_out, mix_post_g=v_mix_post_g,
             mlp_pre_g=v_mlp_pre_g, w_up=v_w_up, w_down=v_w_down, mlp_post_g=v_mlp_post_g)
    chip = 2 * lax.axis_index("x") + lax.axis_index("y")
    seq = x.shape[1]

    gathered = _gather_chips([w["w_in"].astype(BF16), w["w_out"].astype(BF16), w["w_up"].astype(BF16),
                              w["w_down"].astype(BF16), w["conv_w"], w["meta_tokens"]])
    whole, meta = _whole_weights(w, gathered)

    h0 = jnp.concatenate([jnp.zeros((LEAD_PAD, D_MODEL), F32), meta, x[0]], axis=0)
    loss_tile, dh0, grads = _local_step(h0, loss_target[0], whole)
    loss = lax.psum(loss_tile[0, 0], ("x", "y", "c"))
    grad_x = dh0[BLOCK:][None]

    cols_cut = {"w_in": True, "w_out": False, "w_up": True, "w_down": False}
    mine = [_by_chip(grads[n], cols_cut[n]) for n in _LARGE]
    theirs = _swap_halves(mine, axis=2)
    chip_sums = [_add_half(g, r) for g, r in zip(mine, theirs)]
    arrived = _scatter_chips(chip_sums)
    halves = [_sum_chips(q) for q in arrived]
    reduced = dict(zip(_LARGE, _join_halves(halves)))

    rows = [dh0[LEAD_PAD:BLOCK]]
    for n in ("mix_pre_g", "mix_post_g", "mlp_pre_g", "mlp_post_g"):
        rows += grads[n]
    rows += [jnp.concatenate([grads["attn_out_g"][l], grads["conv_out_g"][l]], axis=1) for l in range(DEPTH)]
    rows.append(jnp.concatenate(grads["conv_w"], axis=1))
    rows.append(_pad_cols(jnp.concatenate(grads["sinks"])[None, :]))
    packed = jnp.concatenate(rows, axis=0)
    packed = jnp.pad(packed, ((0, SMALL_ROWS - packed.shape[0]), (0, 0)))
    total = _sum_devices(packed)
    r0 = N_META
    small = {
        "meta_tokens": lax.dynamic_slice(total[:N_META], (0, chip * (D_MODEL // N_CHIPS)), (N_META, D_MODEL // N_CHIPS)),
        "mix_pre_g": total[r0:r0 + 2], "mix_post_g": total[r0 + 2:r0 + 4], "mlp_pre_g": total[r0 + 4:r0 + 6],
        "mlp_post_g": total[r0 + 6:r0 + 8],
        "attn_out_g": total[r0 + 8:r0 + 10, :ATTN_W], "conv_out_g": total[r0 + 8:r0 + 10, ATTN_W:],
        "conv_w": lax.dynamic_slice(total[r0 + 10:r0 + 13].reshape(CONV_K, DEPTH, CONV_W).transpose(1, 0, 2),
                                    (0, 0, chip * (CONV_W // N_CHIPS)), (DEPTH, CONV_K, CONV_W // N_CHIPS)),
        "sinks": total[r0 + 13, :DEPTH * N_Q_HEADS].reshape(DEPTH, N_Q_HEADS),
    }

    grad, delta, new_m, new_v = {}, {}, {}, {}
    for n in _LARGE:
        grad[n] = reduced[n]
        delta[n], new_m[n], new_v[n] = _adamw_large(w[n], reduced[n], m[n], v[n])
    ds, nms, nvs = _adamw_small([w[n] for n in _SMALL], [small[n] for n in _SMALL], [m[n] for n in _SMALL],
                                [v[n] for n in _SMALL])
    for i, n in enumerate(_SMALL):
        grad[n], delta[n], new_m[n], new_v[n] = small[n], ds[i], nms[i], nvs[i]
    return (loss, grad_x, *[grad[n] for n in _ORDER], *[delta[n] for n in _ORDER], *[new_m[n] for n in _ORDER],
            *[new_v[n] for n in _ORDER])
```

```python
import functools

import jax
import jax.numpy as jnp
from jax import lax
from jax.experimental import pallas as pl
from jax.experimental.pallas import tpu as pltpu

F32 = jnp.float32
BF16 = jnp.bfloat16

D_MODEL = 1024
DEPTH = 2
N_META = 16
ATTN_W = 512
CONV_W = 512
HEAD_DIM = 64
N_Q_HEADS = 8
N_KV_HEADS = 2
GROUP = N_Q_HEADS // N_KV_HEADS
KV_W = N_KV_HEADS * HEAD_DIM
CONV_K = 3
BLOCK = 128
LEAD_PAD = BLOCK - N_META
ROPE_THETA = 500000.0
ROT_DIM = HEAD_DIM // 4
ROT_HALF = ROT_DIM // 2
D_FF = 4 * D_MODEL
IN_W = ATTN_W + 2 * KV_W + 3 * CONV_W
QKV_W = ATTN_W + 2 * KV_W
EPS = 1e-6
SCALE = HEAD_DIM ** -0.5
FF_CHUNK = 1024
N_CHIPS = 4
N_DEV = 8

ADAM_LR = 0.001
ADAM_B1 = 0.9
ADAM_B2 = 0.999
ADAM_EPS = 1e-08
ADAM_WD = 0.01
ADAM_STEP = 10

V7X_VMEM_LIMIT = 56 * 1024 * 1024
SMALL_ROWS = 32

MESH = pl.DeviceIdType.MESH


def _params(*sem):
    return pltpu.CompilerParams(dimension_semantics=sem, vmem_limit_bytes=V7X_VMEM_LIMIT)


def _row_tile(t, most):
    nb = t // BLOCK
    for b in range(most // BLOCK, 0, -1):
        if nb % b == 0:
            return b * BLOCK
    return BLOCK


def _rms(x, g):
    r = lax.rsqrt(jnp.mean(x * x, axis=-1, keepdims=True) + EPS)
    return x * r * g


def _rms_bwd(dy, x, g):
    r = lax.rsqrt(jnp.mean(x * x, axis=-1, keepdims=True) + EPS)
    xh = x * r
    dg = jnp.sum(dy * xh, axis=0, keepdims=True)
    dxh = dy * g
    dx = r * (dxh - xh * jnp.mean(dxh * xh, axis=-1, keepdims=True))
    return dx, dg


def _rope(x, cos, sa, sb):
    n = x.shape[-1]
    return x * cos + pltpu.roll(x, n - ROT_HALF, 1) * sa + pltpu.roll(x, ROT_HALF, 1) * sb


def _rope_bwd(dy, cos, sa, sb):
    n = dy.shape[-1]
    return dy * cos + pltpu.roll(dy * sa, ROT_HALF, 1) + pltpu.roll(dy * sb, n - ROT_HALF, 1)


def _rope_tables(t):
    pos = lax.broadcasted_iota(jnp.int32, (t, 2 * HEAD_DIM), 0).astype(F32) - LEAD_PAD
    dim = lax.broadcasted_iota(jnp.int32, (t, 2 * HEAD_DIM), 1) % HEAD_DIM
    pair = (dim % ROT_HALF).astype(F32)
    inv_freq = jnp.power(jnp.float32(ROPE_THETA), -(2.0 * pair) / ROT_DIM)
    ang = pos * inv_freq
    cos, sin = jnp.cos(ang), jnp.sin(ang)
    return (jnp.where(dim < ROT_DIM, cos, 1.0), jnp.where(dim < ROT_HALF, -sin, 0.0),
            jnp.where((dim >= ROT_HALF) & (dim < ROT_DIM), sin, 0.0))


def _in_proj(h, g, w, tabs, tm):
    t = h.shape[0]

    def body(h_ref, g_ref, w_ref, c_ref, sa_ref, sb_ref, a_ref, q_ref, k_ref, v_ref, b_ref, cg_ref, hc_ref):
        a = _rms(h_ref[...], g_ref[...]).astype(BF16)
        a_ref[...] = a
        p = jnp.dot(a, w_ref[...], preferred_element_type=F32)
        cos, sa, sb = c_ref[...], sa_ref[...], sb_ref[...]
        rep = ATTN_W // (2 * HEAD_DIM)
        q = _rope(p[:, :ATTN_W], jnp.tile(cos, (1, rep)), jnp.tile(sa, (1, rep)), jnp.tile(sb, (1, rep)))
        q_ref[...] = (q * SCALE).astype(BF16)
        k_ref[...] = _rope(p[:, ATTN_W:ATTN_W + KV_W], cos, sa, sb).astype(BF16)
        v_ref[...] = p[:, ATTN_W + KV_W:QKV_W].astype(BF16)
        b_ref[...] = p[:, QKV_W:QKV_W + CONV_W]
        cg_ref[...] = p[:, QKV_W + CONV_W:QKV_W + 2 * CONV_W]
        hc_ref[...] = p[:, QKV_W + 2 * CONV_W:]

    row = lambda n: pl.BlockSpec((tm, n), lambda i: (i, 0))
    full = lambda a: pl.BlockSpec(a.shape, lambda i: (0, 0))
    return pl.pallas_call(
        body, name="in_proj", grid=(t // tm,),
        in_specs=[row(D_MODEL), full(g), full(w), row(2 * HEAD_DIM), row(2 * HEAD_DIM), row(2 * HEAD_DIM)],
        out_specs=[row(D_MODEL), row(ATTN_W), row(KV_W), row(KV_W), row(CONV_W), row(CONV_W), row(CONV_W)],
        out_shape=[jax.ShapeDtypeStruct((t, D_MODEL), BF16), jax.ShapeDtypeStruct((t, ATTN_W), BF16),
                   jax.ShapeDtypeStruct((t, KV_W), BF16), jax.ShapeDtypeStruct((t, KV_W), BF16),
                   jax.ShapeDtypeStruct((t, CONV_W), F32), jax.ShapeDtypeStruct((t, CONV_W), F32),
                   jax.ShapeDtypeStruct((t, CONV_W), F32)],
        compiler_params=_params("parallel"),
    )(h, g, w, *tabs)


def _attn_bias():
    r = lax.broadcasted_iota(jnp.int32, (3, BLOCK, 2 * BLOCK), 1)
    c = lax.broadcasted_iota(jnp.int32, (3, BLOCK, 2 * BLOCK), 2)
    i = lax.broadcasted_iota(jnp.int32, (3, BLOCK, 2 * BLOCK), 0)
    ok = (c > r) & (c <= r + BLOCK) & (c + (i - 1) * BLOCK >= LEAD_PAD)
    return jnp.where(ok, 0.0, -jnp.inf).astype(F32)


def _attn_probs(q4, kg, sk, bias4):
    s = lax.dot_general(q4, kg, (((1,), (1,)), ((), ())), preferred_element_type=F32) + bias4
    m = jnp.maximum(jnp.max(s, axis=-1, keepdims=True), sk)
    e = jnp.exp(s - m)
    es = jnp.exp(sk - m)
    rden = 1.0 / (jnp.sum(e, axis=-1, keepdims=True) + es)
    return e * rden, es * rden


def _stack_heads(ref, rows, g):
    return jnp.concatenate(
        [ref[rows, (GROUP * g + j) * HEAD_DIM:(GROUP * g + j + 1) * HEAD_DIM] for j in range(GROUP)], axis=0)


def _sink_column(s_ref, g):
    return jnp.concatenate([jnp.full((BLOCK, 1), s_ref[GROUP * g + j], F32) for j in range(GROUP)], axis=0)


def _two_blocks(ref, i):
    prev = jnp.maximum(i - 1, 0)
    return jnp.concatenate([ref[pl.ds(pl.multiple_of(prev * BLOCK, BLOCK), BLOCK), :],
                            ref[pl.ds(pl.multiple_of(i * BLOCK, BLOCK), BLOCK), :]], axis=0)


def _block_bias(bias_ref, i):
    b = bias_ref[jnp.minimum(i, 2)]
    return jnp.concatenate([b] * GROUP, axis=0)


def _attn_fwd(q, k, v, bias, sinks, tm):
    t = q.shape[0]
    per_step = tm // BLOCK

    def body(s_ref, q_ref, k_ref, v_ref, bias_ref, o_ref):
        for b in range(per_step):
            i = pl.program_id(0) * per_step + b
            rows = slice(b * BLOCK, (b + 1) * BLOCK)
            kc, vc = _two_blocks(k_ref, i), _two_blocks(v_ref, i)
            bias4 = _block_bias(bias_ref, i)
            for g in range(N_KV_HEADS):
                lanes = slice(g * HEAD_DIM, (g + 1) * HEAD_DIM)
                p, _ = _attn_probs(_stack_heads(q_ref, rows, g), kc[:, lanes], _sink_column(s_ref, g), bias4)
                o4 = jnp.dot(p.astype(BF16), vc[:, lanes], preferred_element_type=F32)
                for j in range(GROUP):
                    hh = GROUP * g + j
                    o_ref[rows, hh * HEAD_DIM:(hh + 1) * HEAD_DIM] = o4[j * BLOCK:(j + 1) * BLOCK]

    whole = pl.BlockSpec((t, KV_W), lambda i: (0, 0))
    return pl.pallas_call(
        body, name="attn_fwd", grid=(t // tm,),
        in_specs=[pl.BlockSpec(memory_space=pltpu.SMEM), pl.BlockSpec((tm, ATTN_W), lambda i: (i, 0)), whole, whole,
                  pl.BlockSpec(bias.shape, lambda i: (0, 0, 0))],
        out_specs=pl.BlockSpec((tm, ATTN_W), lambda i: (i, 0)),
        out_shape=jax.ShapeDtypeStruct((t, ATTN_W), F32),
        compiler_params=_params("parallel"),
    )(sinks, q, k, v, bias)


def _shift_rows(u, halo, n):
    r = pltpu.roll(u, n, 0)
    hr = pltpu.roll(halo, n, 0)
    idx = lax.broadcasted_iota(jnp.int32, hr.shape, 0)
    return jnp.concatenate([jnp.where(idx < n, hr, r[:8]), r[8:]], axis=0)


def _advance_rows(u, halo, n):
    rows = u.shape[0]
    r = pltpu.roll(u, rows - n, 0)
    hr = pltpu.roll(halo, 8 - n, 0)
    idx = lax.broadcasted_iota(jnp.int32, hr.shape, 0)
    return jnp.concatenate([r[:rows - 8], jnp.where(idx >= 8 - n, hr, r[rows - 8:])], axis=0)


def _mix_out(h, o, b, c, hc, cw, ga, gc, w, gp, tm):
    t = h.shape[0]

    def body(h_ref, o_ref, b_ref, c_ref, hc_ref, cw_ref, ga_ref, gc_ref, w_ref, gp_ref, h1_ref, y_ref, z_ref, halo):
        @pl.when(pl.program_id(0) == 0)
        def _():
            halo[...] = jnp.zeros_like(halo)

        u = c_ref[...] * hc_ref[...]
        cv = cw_ref[0:1, :] * _shift_rows(u, halo[...], 2) + cw_ref[1:2, :] * _shift_rows(u, halo[...], 1) \
            + cw_ref[2:3, :] * u
        halo[...] = u[tm - 8:]
        yc = b_ref[...] * cv
        y = jnp.concatenate([_rms(o_ref[...], ga_ref[...]), _rms(yc, gc_ref[...])], axis=1).astype(BF16)
        y_ref[...] = y
        z = jnp.dot(y, w_ref[...].reshape(D_MODEL, D_MODEL), preferred_element_type=F32)
        z_ref[...] = z
        h1_ref[...] = h_ref[...] + _rms(z, gp_ref[...])

    row = lambda n: pl.BlockSpec((tm, n), lambda i: (i, 0))
    full = lambda a: pl.BlockSpec(a.shape, lambda i: (0,) * a.ndim)
    return pl.pallas_call(
        body, name="mix_out", grid=(t // tm,),
        in_specs=[row(D_MODEL), row(ATTN_W), row(CONV_W), row(CONV_W), row(CONV_W), full(cw), full(ga), full(gc),
                  full(w), full(gp)],
        out_specs=[row(D_MODEL), row(D_MODEL), row(D_MODEL)],
        out_shape=[jax.ShapeDtypeStruct((t, D_MODEL), F32), jax.ShapeDtypeStruct((t, D_MODEL), BF16),
                   jax.ShapeDtypeStruct((t, D_MODEL), F32)],
        scratch_shapes=[pltpu.VMEM((8, CONV_W), F32)],
        compiler_params=_params("arbitrary"),
    )(h, o, b, c, hc, cw, ga, gc, w, gp)


def _mlp(h1, g1, wu, wd, g2, tm):
    t = h1.shape[0]
    nj = D_FF // FF_CHUNK

    def body(h1_ref, g1_ref, wu_ref, wd_ref, g2_ref, h2_ref, a2_ref, act_ref, f_ref, acc):
        j = pl.program_id(1)

        @pl.when(j == 0)
        def _():
            a2_ref[...] = _rms(h1_ref[...], g1_ref[...]).astype(BF16)

        up = jnp.dot(a2_ref[...], wu_ref[...], preferred_element_type=F32)
        act = jnp.square(jnp.maximum(up, 0.0)).astype(BF16)
        act_ref[...] = act
        part = jnp.dot(act, wd_ref[...], preferred_element_type=F32)

        @pl.when(j == 0)
        def _():
            acc[...] = part

        @pl.when(j > 0)
        def _():
            acc[...] += part

        @pl.when(j == nj - 1)
        def _():
            f = acc[...]
            f_ref[...] = f
            h2_ref[...] = h1_ref[...] + _rms(f, g2_ref[...])

    row = pl.BlockSpec((tm, D_MODEL), lambda i, j: (i, 0))
    vec = pl.BlockSpec((1, D_MODEL), lambda i, j: (0, 0))
    quarter = pl.BlockSpec((None, D_MODEL, FF_CHUNK), lambda i, j: (j, 0, 0))
    return pl.pallas_call(
        body, name="mlp", grid=(t // tm, nj),
        in_specs=[row, vec, quarter, quarter, vec],
        out_specs=[row, row, pl.BlockSpec((tm, FF_CHUNK), lambda i, j: (i, j)), row],
        out_shape=[jax.ShapeDtypeStruct((t, D_MODEL), F32), jax.ShapeDtypeStruct((t, D_MODEL), BF16),
                   jax.ShapeDtypeStruct((t, D_FF), BF16), jax.ShapeDtypeStruct((t, D_MODEL), F32)],
        scratch_shapes=[pltpu.VMEM((tm, D_MODEL), F32)],
        compiler_params=_params("parallel", "arbitrary"),
    )(h1, g1, wu, wd, g2)


def _loss_head(h, target, tm):
    t = h.shape[0]
    per_step = tm // BLOCK

    def body(h_ref, *rest):
        t_refs, (loss_ref, dh_ref) = rest[:per_step], rest[per_step:]
        i = pl.program_id(0)

        @pl.when(i == 0)
        def _():
            loss_ref[...] = jnp.zeros_like(loss_ref)

        total = jnp.zeros((), F32)
        for b in range(per_step):
            rows = slice(b * BLOCK, (b + 1) * BLOCK)
            err = h_ref[rows, :] - t_refs[b][...]
            if b == 0:
                err = jnp.where(i == 0, 0.0, err)
            dh_ref[rows, :] = err * (1.0 / D_MODEL)
            total = total + jnp.sum(err * err)
        loss_ref[...] += total * (0.5 / D_MODEL)

    def target_block(b):
        return pl.BlockSpec((BLOCK, D_MODEL), lambda i: (jnp.maximum(i * per_step + b - 1, 0), 0))

    return pl.pallas_call(
        body, name="loss_head", grid=(t // tm,),
        in_specs=[pl.BlockSpec((tm, D_MODEL), lambda i: (i, 0))] + [target_block(b) for b in range(per_step)],
        out_specs=[pl.BlockSpec((8, 128), lambda i: (0, 0)), pl.BlockSpec((tm, D_MODEL), lambda i: (i, 0))],
        out_shape=[jax.ShapeDtypeStruct((8, 128), F32), jax.ShapeDtypeStruct((t, D_MODEL), F32)],
        compiler_params=_params("arbitrary"),
    )(h, *([target] * per_step))


def _mlp_bwd(dh2, f, g2, act, wd, wu, h1, g1, tm):
    t = dh2.shape[0]
    nj = D_FF // FF_CHUNK

    def body(dh2_ref, f_ref, g2_ref, act_ref, wd_ref, wu_ref, h1_ref, g1_ref, dh1_ref, df_ref, dup_ref, dg2_ref,
             dg1_ref, acc):
        i, j = pl.program_id(0), pl.program_id(1)

        @pl.when((i == 0) & (j == 0))
        def _():
            dg2_ref[...] = jnp.zeros_like(dg2_ref)
            dg1_ref[...] = jnp.zeros_like(dg1_ref)

        @pl.when(j == 0)
        def _():
            df, dg = _rms_bwd(dh2_ref[...], f_ref[...], g2_ref[...])
            df_ref[...] = df.astype(BF16)
            dg2_ref[...] += dg

        dact = lax.dot_general(df_ref[...], wd_ref[...], (((1,), (1,)), ((), ())), preferred_element_type=F32)
        dup = (dact * (2.0 * jnp.sqrt(act_ref[...].astype(F32)))).astype(BF16)
        dup_ref[...] = dup
        part = lax.dot_general(dup, wu_ref[...], (((1,), (1,)), ((), ())), preferred_element_type=F32)

        @pl.when(j == 0)
        def _():
            acc[...] = part

        @pl.when(j > 0)
        def _():
            acc[...] += part

        @pl.when(j == nj - 1)
        def _():
            dx, dg = _rms_bwd(acc[...], h1_ref[...], g1_ref[...])
            dh1_ref[...] = dh2_ref[...] + dx
            dg1_ref[...] += dg

    row = pl.BlockSpec((tm, D_MODEL), lambda i, j: (i, 0))
    vec = pl.BlockSpec((1, D_MODEL), lambda i, j: (0, 0))
    chunk = pl.BlockSpec((tm, FF_CHUNK), lambda i, j: (i, j))
    quarter = pl.BlockSpec((None, D_MODEL, FF_CHUNK), lambda i, j: (j, 0, 0))
    return pl.pallas_call(
        body, name="mlp_bwd", grid=(t // tm, nj),
        in_specs=[row, row, vec, chunk, quarter, quarter, row, vec],
        out_specs=[row, row, chunk, vec, vec],
        out_shape=[jax.ShapeDtypeStruct((t, D_MODEL), F32), jax.ShapeDtypeStruct((t, D_MODEL), BF16),
                   jax.ShapeDtypeStruct((t, D_FF), BF16), jax.ShapeDtypeStruct((1, D_MODEL), F32),
                   jax.ShapeDtypeStruct((1, D_MODEL), F32)],
        scratch_shapes=[pltpu.VMEM((tm, D_MODEL), F32)],
        compiler_params=_params("arbitrary", "arbitrary"),
    )(dh2, f, g2, act, wd, wu, h1, g1)


def _weight_grad(x, y, tm, name):
    t, k = x.shape
    n = y.shape[1]
    tk = tn = FF_CHUNK

    def body(x_ref, y_ref, o_ref):
        @pl.when(pl.program_id(2) == 0)
        def _():
            o_ref[...] = jnp.zeros_like(o_ref)

        o_ref[...] += lax.dot_general(x_ref[...], y_ref[...], (((0,), (0,)), ((), ())), preferred_element_type=F32)

    return pl.pallas_call(
        body, name=name, grid=(k // tk, n // tn, t // tm),
        in_specs=[pl.BlockSpec((tm, tk), lambda a, b, r: (r, a)), pl.BlockSpec((tm, tn), lambda a, b, r: (r, b))],
        out_specs=pl.BlockSpec((None, None, tk, tn), lambda a, b, r: (a, b, 0, 0)),
        out_shape=jax.ShapeDtypeStruct((k // tk, n // tn, tk, tn), F32),
        compiler_params=_params("parallel", "parallel", "arbitrary"),
    )(x, y)


def _weight_grad_in(a, dproj, tm):
    t = a.shape[0]
    nt = t // tm
    qw = IN_W // N_CHIPS

    def body(a_ref, d_ref, o_ref, acc):
        r = pl.program_id(0)
        part = lax.dot_general(a_ref[...], d_ref[...], (((0,), (0,)), ((), ())), preferred_element_type=F32)

        @pl.when(r == 0)
        def _():
            acc[...] = part

        @pl.when(r > 0)
        def _():
            acc[...] += part

        @pl.when(r == nt - 1)
        def _():
            for s in range(N_CHIPS):
                o_ref[s] = acc[:, s * qw:(s + 1) * qw]

    return pl.pallas_call(
        body, name="grad_w_in", grid=(nt,),
        in_specs=[pl.BlockSpec((tm, D_MODEL), lambda r: (r, 0)), pl.BlockSpec((tm, IN_W), lambda r: (r, 0))],
        out_specs=pl.BlockSpec((N_CHIPS, D_MODEL, qw), lambda r: (0, 0, 0)),
        out_shape=jax.ShapeDtypeStruct((N_CHIPS, D_MODEL, qw), F32),
        scratch_shapes=[pltpu.VMEM((D_MODEL, IN_W), F32)],
        compiler_params=_params("arbitrary"),
    )(a, dproj)


def _mix_out_bwd(dh1, z, gp, w, o, b, c, hc, cw, ga, gc, tm):
    t = dh1.shape[0]
    nt = t // tm
    per8 = tm // 8

    def body(dh1_ref, z_ref, gp_ref, w_ref, o_ref, b_ref, c_ref, hc_ref, cp_ref, hp_ref, cw_ref, ga_ref, gc_ref,
             dz_ref, do_ref, dbch_ref, dgp_ref, dga_ref, dgc_ref, dcw_ref, halo):
        i = pl.program_id(0)

        @pl.when(i == 0)
        def _():
            halo[...] = jnp.zeros_like(halo)
            dgp_ref[...] = jnp.zeros_like(dgp_ref)
            dga_ref[...] = jnp.zeros_like(dga_ref)
            dgc_ref[...] = jnp.zeros_like(dgc_ref)
            dcw_ref[...] = jnp.zeros_like(dcw_ref)

        dz, dgp = _rms_bwd(dh1_ref[...], z_ref[...], gp_ref[...])
        dgp_ref[...] += dgp
        dz = dz.astype(BF16)
        dz_ref[...] = dz
        dy = lax.dot_general(dz, w_ref[...].reshape(D_MODEL, D_MODEL), (((1,), (1,)), ((), ())),
                             preferred_element_type=F32)
        do, dga = _rms_bwd(dy[:, :ATTN_W], o_ref[...], ga_ref[...])
        do_ref[...] = do
        dga_ref[...] += dga

        u = c_ref[...] * hc_ref[...]
        first = i == nt - 1
        u_before = jnp.where(first, 0.0, cp_ref[...] * hp_ref[...])
        u1 = _shift_rows(u, u_before, 1)
        u2 = _shift_rows(u, u_before, 2)
        cv = cw_ref[0:1, :] * u2 + cw_ref[1:2, :] * u1 + cw_ref[2:3, :] * u
        bb = b_ref[...]
        dyc, dgc = _rms_bwd(dy[:, ATTN_W:], bb * cv, gc_ref[...])
        dgc_ref[...] += dgc
        dcv = dyc * bb
        d1 = _advance_rows(dcv, halo[...], 1)
        d2 = _advance_rows(dcv, halo[...], 2)
        halo[...] = dcv[:8]
        du = cw_ref[2:3, :] * dcv + cw_ref[1:2, :] * d1 + cw_ref[0:1, :] * d2
        dbch_ref[...] = jnp.concatenate([dyc * cv, du * hc_ref[...], du * c_ref[...]], axis=1).astype(BF16)
        dcw_ref[...] += jnp.concatenate([jnp.sum(dcv * u2, axis=0, keepdims=True),
                                         jnp.sum(dcv * u1, axis=0, keepdims=True),
                                         jnp.sum(dcv * u, axis=0, keepdims=True)], axis=0)

    row = lambda n: pl.BlockSpec((tm, n), lambda i: (nt - 1 - i, 0))
    before = pl.BlockSpec((8, CONV_W), lambda i: (jnp.maximum((nt - 1 - i) * per8 - 1, 0), 0))
    full = lambda a: pl.BlockSpec(a.shape, lambda i: (0,) * a.ndim)
    vec = lambda n: pl.BlockSpec((1, n), lambda i: (0, 0))
    return pl.pallas_call(
        body, name="mix_out_bwd", grid=(nt,),
        in_specs=[row(D_MODEL), row(D_MODEL), full(gp), full(w), row(ATTN_W), row(CONV_W), row(CONV_W), row(CONV_W),
                  before, before, full(cw), full(ga), full(gc)],
        out_specs=[row(D_MODEL), row(ATTN_W), row(3 * CONV_W), vec(D_MODEL), vec(ATTN_W), vec(CONV_W),
                   pl.BlockSpec((CONV_K, CONV_W), lambda i: (0, 0))],
        out_shape=[jax.ShapeDtypeStruct((t, D_MODEL), BF16), jax.ShapeDtypeStruct((t, ATTN_W), F32),
                   jax.ShapeDtypeStruct((t, 3 * CONV_W), BF16), jax.ShapeDtypeStruct((1, D_MODEL), F32),
                   jax.ShapeDtypeStruct((1, ATTN_W), F32), jax.ShapeDtypeStruct((1, CONV_W), F32),
                   jax.ShapeDtypeStruct((CONV_K, CONV_W), F32)],
        scratch_shapes=[pltpu.VMEM((8, CONV_W), F32)],
        compiler_params=_params("arbitrary"),
    )(dh1, z, gp, w, o, b, c, hc, c, hc, cw, ga, gc)


def _attn_bwd(q, k, v, o, do, bias, sinks, tm):
    t = q.shape[0]
    per_step = tm // BLOCK

    def body(s_ref, q_ref, k_ref, v_ref, o_ref, do_ref, bias_ref, dq_ref, dk_ref, dv_ref, ds_ref):
        step = pl.program_id(0)

        @pl.when(step == 0)
        def _():
            ds_ref[...] = jnp.zeros_like(ds_ref)

        dsink = [jnp.zeros((GROUP * BLOCK, 1), F32) for _ in range(N_KV_HEADS)]
        ahead = None
        for b in range(per_step):
            i = step * per_step + b
            rows = slice(b * BLOCK, (b + 1) * BLOCK)
            kc, vc = _two_blocks(k_ref, i), _two_blocks(v_ref, i)
            bias4 = _block_bias(bias_ref, i)
            dkg, dvg = [], []
            for g in range(N_KV_HEADS):
                lanes = slice(g * HEAD_DIM, (g + 1) * HEAD_DIM)
                q4 = _stack_heads(q_ref, rows, g)
                p, ps = _attn_probs(q4, kc[:, lanes], _sink_column(s_ref, g), bias4)
                o4, do4 = _stack_heads(o_ref, rows, g), _stack_heads(do_ref, rows, g)
                do4b = do4.astype(BF16)
                dp = lax.dot_general(do4b, vc[:, lanes], (((1,), (1,)), ((), ())), preferred_element_type=F32)
                drow = jnp.sum(do4 * o4, axis=-1, keepdims=True)
                ds = (p * (dp - drow)).astype(BF16)
                dsink[g] = dsink[g] + ps * drow
                dq4 = jnp.dot(ds, kc[:, lanes], preferred_element_type=F32) * SCALE
                dkg.append(lax.dot_general(ds, q4, (((0,), (0,)), ((), ())), preferred_element_type=F32))
                dvg.append(lax.dot_general(p.astype(BF16), do4b, (((0,), (0,)), ((), ())),
                                           preferred_element_type=F32))
                for j in range(GROUP):
                    hh = GROUP * g + j
                    dq_ref[rows, hh * HEAD_DIM:(hh + 1) * HEAD_DIM] = dq4[j * BLOCK:(j + 1) * BLOCK]
            dkb, dvb = jnp.concatenate(dkg, axis=1), jnp.concatenate(dvg, axis=1)
            if b == 0:
                @pl.when(step > 0)
                def _():
                    before = pl.ds(pl.multiple_of((i - 1) * BLOCK, BLOCK), BLOCK)
                    dk_ref[before, :] += dkb[:BLOCK]
                    dv_ref[before, :] += dvb[:BLOCK]
            else:
                at = pl.ds(pl.multiple_of((i - 1) * BLOCK, BLOCK), BLOCK)
                dk_ref[at, :] = ahead[0] + dkb[:BLOCK]
                dv_ref[at, :] = ahead[1] + dvb[:BLOCK]
            ahead = (dkb[BLOCK:], dvb[BLOCK:])
        last = pl.ds(pl.multiple_of(((step + 1) * per_step - 1) * BLOCK, BLOCK), BLOCK)
        dk_ref[last, :] = ahead[0]
        dv_ref[last, :] = ahead[1]
        for g in range(N_KV_HEADS):
            for j in range(GROUP):
                hh = GROUP * g + j
                ds_ref[hh:hh + 1, :] -= jnp.sum(dsink[g][j * BLOCK:(j + 1) * BLOCK])

    whole = pl.BlockSpec((t, KV_W), lambda i: (0, 0))
    blk = pl.BlockSpec((tm, ATTN_W), lambda i: (i, 0))
    return pl.pallas_call(
        body, name="attn_bwd", grid=(t // tm,),
        in_specs=[pl.BlockSpec(memory_space=pltpu.SMEM), blk, whole, whole, blk, blk,
                  pl.BlockSpec(bias.shape, lambda i: (0, 0, 0))],
        out_specs=[blk, whole, whole, pl.BlockSpec((N_Q_HEADS, 128), lambda i: (0, 0))],
        out_shape=[jax.ShapeDtypeStruct((t, ATTN_W), F32), jax.ShapeDtypeStruct((t, KV_W), F32),
                   jax.ShapeDtypeStruct((t, KV_W), F32), jax.ShapeDtypeStruct((N_Q_HEADS, 128), F32)],
        compiler_params=_params("arbitrary"),
    )(sinks, q, k, v, o, do, bias)


def _in_proj_bwd(dq, dk, dv, dbch, w, dh1, h, g, tabs, tm):
    t = h.shape[0]

    def body(dq_ref, dk_ref, dv_ref, dbch_ref, w_ref, dh1_ref, h_ref, g_ref, c_ref, sa_ref, sb_ref, dh_ref, dp_ref,
             dg_ref):
        @pl.when(pl.program_id(0) == 0)
        def _():
            dg_ref[...] = jnp.zeros_like(dg_ref)

        cos, sa, sb = c_ref[...], sa_ref[...], sb_ref[...]
        rep = ATTN_W // (2 * HEAD_DIM)
        dqr = _rope_bwd(dq_ref[...], jnp.tile(cos, (1, rep)), jnp.tile(sa, (1, rep)), jnp.tile(sb, (1, rep)))
        dkr = _rope_bwd(dk_ref[...], cos, sa, sb)
        dp = jnp.concatenate([dqr.astype(BF16), dkr.astype(BF16), dv_ref[...].astype(BF16), dbch_ref[...]], axis=1)
        dp_ref[...] = dp
        da = lax.dot_general(dp, w_ref[...], (((1,), (1,)), ((), ())), preferred_element_type=F32)
        dx, dg = _rms_bwd(da, h_ref[...], g_ref[...])
        dh_ref[...] = dh1_ref[...] + dx
        dg_ref[...] += dg

    row = lambda n: pl.BlockSpec((tm, n), lambda i: (i, 0))
    full = lambda a: pl.BlockSpec(a.shape, lambda i: (0, 0))
    return pl.pallas_call(
        body, name="in_proj_bwd", grid=(t // tm,),
        in_specs=[row(ATTN_W), row(KV_W), row(KV_W), row(3 * CONV_W), full(w), row(D_MODEL), row(D_MODEL), full(g),
                  row(2 * HEAD_DIM), row(2 * HEAD_DIM), row(2 * HEAD_DIM)],
        out_specs=[row(D_MODEL), row(IN_W), pl.BlockSpec((1, D_MODEL), lambda i: (0, 0))],
        out_shape=[jax.ShapeDtypeStruct((t, D_MODEL), F32), jax.ShapeDtypeStruct((t, IN_W), BF16),
                   jax.ShapeDtypeStruct((1, D_MODEL), F32)],
        compiler_params=_params("arbitrary"),
    )(dq, dk, dv, dbch, w, dh1, h, g, *tabs)


class _Tiles:
    def __init__(self, t):
        self.tm = _row_tile(t, 640)
        self.ts = _row_tile(t, 320)
        self.tabs = _rope_tables(t)
        self.bias = _attn_bias()


def _mixer_fwd(h, p, tl):
    a, q, k, v, b, c, hc = _in_proj(h, p["mix_pre_g"], p["w_in"], tl.tabs, tl.ts)
    o = _attn_fwd(q, k, v, tl.bias, p["sinks"], tl.tm)
    return (h, a, q, k, v, b, c, hc, o)


def _rest_fwd(mixed, p, tl):
    h, a, q, k, v, b, c, hc, o = mixed
    h1, y, z = _mix_out(h, o, b, c, hc, p["conv_w"], p["attn_out_g"], p["conv_out_g"], p["w_out"], p["mix_post_g"],
                        tl.ts)
    h2, a2, act, f = _mlp(h1, p["mlp_pre_g"], p["w_up"], p["w_down"], p["mlp_post_g"], tl.tm)
    return h2, mixed + (h1, y, z, a2, act, f)


def _mlp_part_bwd(dh, saved, p, tl):
    h1, a2, act, f = saved[9], saved[12], saved[13], saved[14]
    dh1, df, dup, dg2, dg1 = _mlp_bwd(dh, f, p["mlp_post_g"], act, p["w_down"], p["w_up"], h1, p["mlp_pre_g"], tl.tm)
    g = {"w_down": _weight_grad(act, df, tl.tm, "grad_w_down").reshape(N_CHIPS, FF_CHUNK, D_MODEL),
         "w_up": _weight_grad(a2, dup, tl.tm, "grad_w_up").reshape(N_CHIPS, D_MODEL, FF_CHUNK),
         "mlp_post_g": dg2, "mlp_pre_g": dg1}
    return dh1, g


def _mix_out_part_bwd(dh1, saved, p, tl):
    b, c, hc, o, y, z = saved[5], saved[6], saved[7], saved[8], saved[10], saved[11]
    dz, do, dbch, dgp, dga, dgc, dcw = _mix_out_bwd(dh1, z, p["mix_post_g"], p["w_out"], o, b, c, hc, p["conv_w"],
                                                    p["attn_out_g"], p["conv_out_g"], tl.ts)
    g = {"w_out": _weight_grad(y, dz, tl.tm, "grad_w_out").reshape(N_CHIPS, D_MODEL // N_CHIPS, D_MODEL),
         "mix_post_g": dgp, "attn_out_g": dga, "conv_out_g": dgc, "conv_w": dcw}
    return (dh1, do, dbch), g


def _attn_in_part_bwd(carry, saved, p, tl):
    dh1, do, dbch = carry
    h_in, a, q, k, v, o = saved[0], saved[1], saved[2], saved[3], saved[4], saved[8]
    dq, dk, dv, dsink = _attn_bwd(q, k, v, o, do, tl.bias, p["sinks"], tl.tm)
    dh, dproj, dgi = _in_proj_bwd(dq, dk, dv, dbch, p["w_in"], dh1, h_in, p["mix_pre_g"], tl.tabs, tl.ts)
    return dh, {"w_in": _weight_grad_in(a, dproj, tl.ts), "mix_pre_g": dgi, "sinks": dsink[:, 0]}


def _place():
    return lax.axis_index("x"), lax.axis_index("y"), lax.axis_index("c")


def _other_chips(x, y):
    return [(1 - x, y), (x, 1 - y), (1 - x, 1 - y)]


_HBM = pl.BlockSpec(memory_space=pltpu.HBM)
_SEM = pl.BlockSpec(memory_space=pltpu.SEMAPHORE)
_EFFECT = pltpu.SideEffectType.DATAFLOW_SIDE_EFFECTING


class _Exchange:
    def __init__(self, name, bufs, plan, n):
        self.name, self.plan, nb = name, plan, len(bufs)

        def body(*refs):
            send, recv, token = refs[nb], refs[nb + 1], refs[-1]
            for k, (src, dst, target, _) in enumerate(plan(refs[:nb])):
                pltpu.make_async_remote_copy(src_ref=src, dst_ref=dst, send_sem=send.at[k], recv_sem=recv.at[k],
                                             device_id=target, device_id_type=MESH).start()
            token[...] = jnp.zeros_like(token)

        outs = pl.pallas_call(
            body, name=name + "_start",
            out_shape=(pltpu.SemaphoreType.DMA((n,)), pltpu.SemaphoreType.DMA((n,)),
                       *[pltpu.HBM(b.shape, b.dtype) for b in bufs], jax.ShapeDtypeStruct((8, 128), F32)),
            in_specs=[_HBM] * nb, out_specs=(_SEM, _SEM, *[_HBM] * nb, pl.BlockSpec(memory_space=pltpu.VMEM)),
            input_output_aliases={i: 2 + i for i in range(nb)},
            compiler_params=pltpu.CompilerParams(has_side_effects=_EFFECT),
        )(*[pltpu.with_memory_space_constraint(b, pltpu.HBM) for b in bufs])
        self.send, self.recv, self.bufs, self.token = outs[0], outs[1], list(outs[2:2 + nb]), outs[-1]

    def wait(self, after):
        plan, nb = self.plan, len(self.bufs)

        def body(*refs):
            send, recv = refs[nb], refs[nb + 1]
            for k, (src, _, target, land) in enumerate(plan(refs[:nb])):
                cp = pltpu.make_async_remote_copy(src_ref=src, dst_ref=land, send_sem=send.at[k], recv_sem=recv.at[k],
                                                  device_id=target, device_id_type=MESH)
                cp.wait_send()
                cp.wait_recv()

        outs = pl.pallas_call(
            body, name=self.name + "_wait", out_shape=[pltpu.HBM(b.shape, b.dtype) for b in self.bufs],
            in_specs=[_HBM] * nb + [_SEM, _SEM, pl.BlockSpec(memory_space=pl.ANY)], out_specs=[_HBM] * nb,
            input_output_aliases={i: i for i in range(nb)},
            compiler_params=pltpu.CompilerParams(has_side_effects=_EFFECT),
        )(*self.bufs, self.send, self.recv, after)
        return list(outs)


def _behind(x, token):
    return lax.optimization_barrier((x, token))[0]


def _gather_plan(n):
    def plan(refs):
        x, y, c = _place()
        me = 2 * x + y
        return [(refs[a].at[me], refs[a].at[me], (px, py, c), refs[a].at[2 * px + py])
                for a in range(n) for px, py in _other_chips(x, y)]

    return plan


def _swap_plan(n, half_rows):
    def plan(refs):
        x, y, c = _place()
        out = []
        for a in range(n):
            hr = half_rows[a]
            out.append((refs[a].at[:, pl.ds((1 - c) * hr, hr)], refs[n + a], (x, y, 1 - c), refs[n + a]))
        return out

    return plan


def _scatter_plan(n):
    def plan(refs):
        x, y, c = _place()
        return [(refs[a].at[2 * px + py], refs[n + a].at[k], (px, py, c), refs[n + a].at[k])
                for a in range(n) for k, (px, py) in enumerate(_other_chips(x, y))]

    return plan


def _join_plan(n):
    def plan(refs):
        x, y, c = _place()
        return [(refs[a].at[c], refs[a].at[c], (x, y, 1 - c), refs[a].at[1 - c]) for a in range(n)]

    return plan


def _add_half(g, r):
    rows, cols = g.shape[1], g.shape[2]
    hr = rows // 2
    tr = min(hr, 256)
    per = hr // tr
    first = (lax.axis_index("c") * per).astype(jnp.int32).reshape(1)

    def body(first_ref, g_ref, r_ref, o_ref):
        o_ref[...] = (g_ref[...] + r_ref[...]).astype(BF16)

    blk = pl.BlockSpec((None, tr, cols), lambda s, i, first_ref: (s, i, 0))
    return pl.pallas_call(
        body, name="add_half",
        grid_spec=pltpu.PrefetchScalarGridSpec(
            num_scalar_prefetch=1, grid=(N_CHIPS, per),
            in_specs=[pl.BlockSpec((None, tr, cols), lambda s, i, first_ref: (s, i + first_ref[0], 0)), blk],
            out_specs=blk),
        out_shape=jax.ShapeDtypeStruct(r.shape, BF16),
        compiler_params=_params("parallel", "parallel"),
    )(first, g, r)


def _sum_chips(s, q):
    rows, cols = s.shape[1], s.shape[2]
    tr = min(rows, 256)
    x, y, c = _place()
    where = jnp.stack([2 * x + y, c]).astype(jnp.int32)

    def body(where_ref, s_ref, q_ref, o_ref):
        part = [q_ref[k].astype(F32) for k in range(N_CHIPS - 1)]
        o_ref[...] = ((s_ref[...].astype(F32) + part[0]) + part[1]) + part[2]

    return pl.pallas_call(
        body, name="sum_chips",
        grid_spec=pltpu.PrefetchScalarGridSpec(
            num_scalar_prefetch=1, grid=(rows // tr,),
            in_specs=[pl.BlockSpec((None, tr, cols), lambda i, where_ref: (where_ref[0], i, 0)),
                      pl.BlockSpec((N_CHIPS - 1, tr, cols), lambda i, where_ref: (0, i, 0))],
            out_specs=pl.BlockSpec((None, tr, cols), lambda i, where_ref: (where_ref[1], i, 0))),
        out_shape=jax.ShapeDtypeStruct((2, rows, cols), F32),
        compiler_params=_params("parallel"),
    )(where, s, q)


def _sum_devices(packed):
    def body(p_ref, o_ref, land, send_sems, recv_sems):
        x, y, c = _place()
        me = 4 * x + 2 * y + c
        land[me] = p_ref[...]
        sends = []
        for k in range(1, N_DEV):
            px, py, pc = x ^ (k >> 2), y ^ ((k >> 1) & 1), c ^ (k & 1)
            cp = pltpu.make_async_remote_copy(src_ref=p_ref, dst_ref=land.at[me], send_sem=send_sems.at[k - 1],
                                              recv_sem=recv_sems.at[k - 1], device_id=(px, py, pc), device_id_type=MESH)
            cp.start()
            sends.append(cp)
        for k in range(1, N_DEV):
            px, py, pc = x ^ (k >> 2), y ^ ((k >> 1) & 1), c ^ (k & 1)
            pltpu.make_async_remote_copy(src_ref=p_ref, dst_ref=land.at[4 * px + 2 * py + pc],
                                         send_sem=send_sems.at[k - 1], recv_sem=recv_sems.at[k - 1],
                                         device_id=(px, py, pc), device_id_type=MESH).wait_recv()
        for cp in sends:
            cp.wait_send()
        total = land[0]
        for d in range(1, N_DEV):
            total = total + land[d]
        o_ref[...] = total

    vm = pl.BlockSpec(memory_space=pltpu.VMEM)
    return pl.pallas_call(
        body, name="sum_devices", in_specs=[vm], out_specs=vm,
        out_shape=jax.ShapeDtypeStruct(packed.shape, F32),
        scratch_shapes=[pltpu.VMEM((N_DEV,) + packed.shape, F32), pltpu.SemaphoreType.DMA((N_DEV - 1,)),
                        pltpu.SemaphoreType.DMA((N_DEV - 1,))],
    )(packed)


def _adamw_math(w, g, m, v):
    m = ADAM_B1 * m + (1.0 - ADAM_B1) * g
    v = ADAM_B2 * v + (1.0 - ADAM_B2) * jnp.square(g)
    m_hat = m / (1.0 - ADAM_B1 ** ADAM_STEP)
    v_hat = v / (1.0 - ADAM_B2 ** ADAM_STEP)
    delta = -ADAM_LR * (m_hat / (jnp.sqrt(v_hat) + ADAM_EPS) + ADAM_WD * w)
    return delta, m, v


def _adamw_large(layer, w, halves, m, v, other):
    _, rows, cols = w.shape
    tr = min(rows // 2, 256)
    per = rows // 2 // tr

    def body(w_ref, g_ref, m_ref, v_ref, *rest):
        g_out, d_ref, nm_ref, nv_ref = rest[-4:]
        g = g_ref[...]
        g_out[...] = g
        d_ref[...], nm_ref[...], nv_ref[...] = _adamw_math(w_ref[...], g, m_ref[...], v_ref[...])

    blk = pl.BlockSpec((None, tr, cols), lambda i: (layer, i, 0))
    half = pl.BlockSpec((None, tr, cols), lambda i: (i // per, i % per, 0))
    kept = [] if other is None else list(other)
    return pl.pallas_call(
        body, name="adamw_large", grid=(rows // tr,),
        in_specs=[blk, half, blk, blk] + [pl.BlockSpec(memory_space=pl.ANY)] * len(kept), out_specs=[blk] * 4,
        out_shape=[jax.ShapeDtypeStruct(w.shape, F32)] * 4,
        input_output_aliases={4 + k: k for k in range(len(kept))},
        compiler_params=_params("parallel"),
    )(w, halves, m, v, *kept)


def _adamw_small(ws, gs, ms, vs):
    n = len(ws)

    def body(*refs):
        w_r, g_r, m_r, v_r = refs[:n], refs[n:2 * n], refs[2 * n:3 * n], refs[3 * n:4 * n]
        d_r, nm_r, nv_r = refs[4 * n:5 * n], refs[5 * n:6 * n], refs[6 * n:]
        for a in range(n):
            d_r[a][...], nm_r[a][...], nv_r[a][...] = _adamw_math(w_r[a][...], g_r[a][...], m_r[a][...], v_r[a][...])

    vm = pl.BlockSpec(memory_space=pltpu.VMEM)
    outs = pl.pallas_call(
        body, name="adamw_small", in_specs=[vm] * (4 * n), out_specs=[vm] * (3 * n),
        out_shape=[jax.ShapeDtypeStruct(w.shape, F32) for w in ws] * 3,
    )(*ws, *gs, *ms, *vs)
    return outs[:n], outs[n:2 * n], outs[2 * n:]


_LARGE = ("w_in", "w_out", "w_up", "w_down")
_SMALL = ("meta_tokens", "mix_pre_g", "conv_w", "sinks", "attn_out_g", "conv_out_g", "mix_post_g", "mlp_pre_g",
          "mlp_post_g")
_ORDER = ("meta_tokens", "mix_pre_g", "w_in", "conv_w", "sinks", "attn_out_g", "conv_out_g", "w_out", "mix_post_g",
          "mlp_pre_g", "w_up", "w_down", "mlp_post_g")


class _Reduce:
    def __init__(self, name, grads):
        self.name, self.n = name, len(grads)
        half_rows = [g.shape[1] // 2 for g in grads]
        zones = [lax.empty((N_CHIPS, hr, g.shape[2]), F32) for g, hr in zip(grads, half_rows)]
        self.exchange = _Exchange(name + "_swap", list(grads) + zones, _swap_plan(self.n, half_rows), self.n)

    @property
    def token(self):
        return self.exchange.token

    def scatter(self, after):
        bufs = self.exchange.wait(after)
        sums = [_add_half(g, r) for g, r in zip(bufs[:self.n], bufs[self.n:])]
        zones = [lax.empty((N_CHIPS - 1,) + s.shape[1:], BF16) for s in sums]
        self.exchange = _Exchange(self.name + "_scatter", sums + zones, _scatter_plan(self.n), 3 * self.n)

    def join(self, after):
        bufs = self.exchange.wait(after)
        halves = [_sum_chips(s, q) for s, q in zip(bufs[:self.n], bufs[self.n:])]
        self.exchange = _Exchange(self.name + "_join", halves, _join_plan(self.n), self.n)

    def done(self, after):
        return self.exchange.wait(after)


def _pad_cols(a, n=D_MODEL):
    return jnp.pad(a, ((0, 0), (0, n - a.shape[1])))


def kernel(x, meta_tokens, mix_pre_g, w_in, conv_w, sinks, attn_out_g, conv_out_g, w_out, mix_post_g, mlp_pre_g, w_up, w_down, mlp_post_g, loss_target, m_meta_tokens, m_mix_pre_g, m_w_in, m_conv_w, m_sinks, m_attn_out_g, m_conv_out_g, m_w_out, m_mix_post_g, m_mlp_pre_g, m_w_up, m_w_down, m_mlp_post_g, v_meta_tokens, v_mix_pre_g, v_w_in, v_conv_w, v_sinks, v_attn_out_g, v_conv_out_g, v_w_out, v_mix_post_g, v_mlp_pre_g, v_w_up, v_w_down, v_mlp_post_g):
    w = dict(meta_tokens=meta_tokens, mix_pre_g=mix_pre_g, w_in=w_in, conv_w=conv_w, sinks=sinks,
             attn_out_g=attn_out_g, conv_out_g=conv_out_g, w_out=w_out, mix_post_g=mix_post_g, mlp_pre_g=mlp_pre_g,
             w_up=w_up, w_down=w_down, mlp_post_g=mlp_post_g)
    m = dict(meta_tokens=m_meta_tokens, mix_pre_g=m_mix_pre_g, w_in=m_w_in, conv_w=m_conv_w, sinks=m_sinks,
             attn_out_g=m_attn_out_g, conv_out_g=m_conv_out_g, w_out=m_w_out, mix_post_g=m_mix_post_g,
             mlp_pre_g=m_mlp_pre_g, w_up=m_w_up, w_down=m_w_down, mlp_post_g=m_mlp_post_g)
    v = dict(meta_tokens=v_meta_tokens, mix_pre_g=v_mix_pre_g, w_in=v_w_in, conv_w=v_conv_w, sinks=v_sinks,
             attn_out_g=v_attn_out_g, conv_out_g=v_conv_out_g, w_out=v_w_out, mix_post_g=v_mix_post_g,
             mlp_pre_g=v_mlp_pre_g, w_up=v_w_up, w_down=v_w_down, mlp_post_g=v_mlp_post_g)
    chip = 2 * lax.axis_index("x") + lax.axis_index("y")
    tl = _Tiles(x.shape[1] + BLOCK)

    def zone(quarter):
        return lax.dynamic_update_slice(lax.empty((N_CHIPS,) + quarter.shape, quarter.dtype), quarter[None],
                                        (chip,) + (0,) * quarter.ndim)

    zones = {n: [zone(w[n][l].astype(BF16)) for l in range(DEPTH)] for n in _LARGE}
    first = _Exchange("gather_first", [zones["w_in"][0], zone(w["conv_w"]), zone(w["meta_tokens"])], _gather_plan(3), 9)
    rest = _Exchange("gather_rest", _behind([zones[n][0] for n in ("w_out", "w_up", "w_down")], first.token),
                     _gather_plan(3), 9)
    second = _Exchange("gather_second", _behind([zones[n][1] for n in _LARGE], rest.token), _gather_plan(4), 12)

    def whole_in(quarters):
        return jnp.transpose(quarters, (1, 0, 2)).reshape(D_MODEL, IN_W)

    q_in, q_conv, q_meta = first.wait(second.token)
    conv_whole = jnp.transpose(q_conv, (1, 2, 0, 3)).reshape(DEPTH, CONV_K, CONV_W)
    meta = jnp.transpose(q_meta, (1, 0, 2)).reshape(N_META, D_MODEL)
    p = [{"conv_w": conv_whole[l], "sinks": w["sinks"][l]} for l in range(DEPTH)]
    for l in range(DEPTH):
        for n in ("mix_pre_g", "attn_out_g", "conv_out_g", "mix_post_g", "mlp_pre_g", "mlp_post_g"):
            p[l][n] = w[n][l][None, :]

    h = jnp.concatenate([jnp.zeros((LEAD_PAD, D_MODEL), F32), meta, x[0]], axis=0)
    p[0]["w_in"] = whole_in(q_in)
    mixed = _mixer_fwd(h, p[0], tl)
    p[0]["w_out"], p[0]["w_up"], p[0]["w_down"] = rest.wait(mixed[-1])
    h, saved0 = _rest_fwd(mixed, p[0], tl)
    q_in, p[1]["w_out"], p[1]["w_up"], p[1]["w_down"] = second.wait(h)
    p[1]["w_in"] = whole_in(q_in)
    h, saved1 = _rest_fwd(_mixer_fwd(h, p[1], tl), p[1], tl)
    loss_tile, dh = _loss_head(h, loss_target[0], tl.tm)
    loss = lax.psum(loss_tile[0, 0], ("x", "y", "c"))

    def adamw(layer, halves, other):
        return {n: _adamw_large(layer, w[n], halves[n], m[n], v[n], None if other is None else other[n])
                for n in halves}

    dh1, g1 = _mlp_part_bwd(dh, saved1, p[1], tl)
    carry, gm = _mix_out_part_bwd(dh1, saved1, p[1], tl)
    dh, gi = _attn_in_part_bwd(carry, saved1, p[1], tl)
    g1.update(gm, **gi)
    red1 = _Reduce("reduce1", [g1[n] for n in _LARGE])
    dh1, g0 = _mlp_part_bwd(_behind(dh, red1.token), saved0, p[0], tl)
    red1.scatter(dh1)
    red0a = _Reduce("reduce0a", _behind([g0["w_up"], g0["w_down"]], red1.token))
    carry, gm = _mix_out_part_bwd(_behind(dh1, red0a.token), saved0, p[0], tl)
    red1.join(carry[1])
    red0a.scatter(_behind(carry[1], red1.token))
    dh0, gi = _attn_in_part_bwd(_behind(carry, red0a.token), saved0, p[0], tl)
    g0.update(gm, **gi)
    red0b = _Reduce("reduce0b", [g0["w_in"], g0["w_out"]])
    done1 = adamw(1, dict(zip(_LARGE, red1.done(red0b.token))), None)
    red0a.join(done1["w_down"][0])
    red0b.scatter(red0a.token)
    halves0 = dict(zip(("w_up", "w_down"), red0a.done(red0b.token)))
    done0 = adamw(0, halves0, done1)
    red0b.join(done0["w_down"][0])
    done0.update(adamw(0, dict(zip(("w_in", "w_out"), red0b.done(red0b.token))), done1))
    grad_x = dh0[BLOCK:][None]
    grads = {n: [g0[n], g1[n]] for n in g0 if n not in _LARGE}

    rows = [dh0[LEAD_PAD:BLOCK]]
    for n in ("mix_pre_g", "mix_post_g", "mlp_pre_g", "mlp_post_g"):
        rows += grads[n]
    rows += [jnp.concatenate([grads["attn_out_g"][l], grads["conv_out_g"][l]], axis=1) for l in range(DEPTH)]
    rows.append(jnp.concatenate(grads["conv_w"], axis=1))
    rows.append(_pad_cols(jnp.concatenate(grads["sinks"])[None, :]))
    packed = jnp.concatenate(rows, axis=0)
    packed = jnp.pad(packed, ((0, SMALL_ROWS - packed.shape[0]), (0, 0)))
    total = _sum_devices(packed)
    r0 = N_META
    small = {
        "meta_tokens": lax.dynamic_slice(total[:N_META], (0, chip * (D_MODEL // N_CHIPS)), (N_META, D_MODEL // N_CHIPS)),
        "mix_pre_g": total[r0:r0 + 2], "mix_post_g": total[r0 + 2:r0 + 4], "mlp_pre_g": total[r0 + 4:r0 + 6],
        "mlp_post_g": total[r0 + 6:r0 + 8],
        "attn_out_g": total[r0 + 8:r0 + 10, :ATTN_W], "conv_out_g": total[r0 + 8:r0 + 10, ATTN_W:],
        "conv_w": lax.dynamic_slice(total[r0 + 10:r0 + 13].reshape(CONV_K, DEPTH, CONV_W).transpose(1, 0, 2),
                                    (0, 0, chip * (CONV_W // N_CHIPS)), (DEPTH, CONV_K, CONV_W // N_CHIPS)),
        "sinks": total[r0 + 13, :DEPTH * N_Q_HEADS].reshape(DEPTH, N_Q_HEADS),
    }

    grad, delta, new_m, new_v = {}, {}, {}, {}
    for n in _LARGE:
        grad[n], delta[n], new_m[n], new_v[n] = done0[n]
    ds, nms, nvs = _adamw_small([w[n] for n in _SMALL], [small[n] for n in _SMALL], [m[n] for n in _SMALL],
                                [v[n] for n in _SMALL])
    for i, n in enumerate(_SMALL):
        grad[n], delta[n], new_m[n], new_v[n] = small[n], ds[i], nms[i], nvs[i]
    return (loss, grad_x, *[grad[n] for n in _ORDER], *[delta[n] for n in _ORDER], *[new_m[n] for n in _ORDER],
            *[new_v[n] for n in _ORDER])
```

```python
import functools

import jax
import jax.numpy as jnp
from jax import lax
from jax.experimental import pallas as pl
from jax.experimental.pallas import tpu as pltpu

F32 = jnp.float32
BF16 = jnp.bfloat16

D_MODEL = 1024
DEPTH = 2
N_META = 16
ATTN_W = 512
CONV_W = 512
HEAD_DIM = 64
N_Q_HEADS = 8
N_KV_HEADS = 2
GROUP = N_Q_HEADS // N_KV_HEADS
KV_W = N_KV_HEADS * HEAD_DIM
CONV_K = 3
BLOCK = 128
LEAD_PAD = BLOCK - N_META
ROPE_THETA = 500000.0
ROT_DIM = HEAD_DIM // 4
ROT_HALF = ROT_DIM // 2
D_FF = 4 * D_MODEL
IN_W = ATTN_W + 2 * KV_W + 3 * CONV_W
QKV_W = ATTN_W + 2 * KV_W
EPS = 1e-6
SCALE = HEAD_DIM ** -0.5
FF_CHUNK = 1024
N_CHIPS = 4
N_DEV = 8

ADAM_LR = 0.001
ADAM_B1 = 0.9
ADAM_B2 = 0.999
ADAM_EPS = 1e-08
ADAM_WD = 0.01
ADAM_STEP = 10

V7X_VMEM_LIMIT = 56 * 1024 * 1024
SMALL_ROWS = 32

MESH = pl.DeviceIdType.MESH


def _params(*sem):
    return pltpu.CompilerParams(dimension_semantics=sem, vmem_limit_bytes=V7X_VMEM_LIMIT)


def _row_tile(t, most):
    nb = t // BLOCK
    for b in range(most // BLOCK, 0, -1):
        if nb % b == 0:
            return b * BLOCK
    return BLOCK


def _behind(body, deps):
    n = len(deps)

    def wrapped(*refs):
        body(*refs[n:])

    return wrapped, [pl.BlockSpec(memory_space=pl.ANY)] * n


def _rms(x, g):
    r = lax.rsqrt(jnp.mean(x * x, axis=-1, keepdims=True) + EPS)
    return x * r * g


def _rms_bwd(dy, x, g):
    r = lax.rsqrt(jnp.mean(x * x, axis=-1, keepdims=True) + EPS)
    xh = x * r
    dg = jnp.sum(dy * xh, axis=0, keepdims=True)
    dxh = dy * g
    dx = r * (dxh - xh * jnp.mean(dxh * xh, axis=-1, keepdims=True))
    return dx, dg


def _rope(x, cos, sa, sb):
    n = x.shape[-1]
    return x * cos + pltpu.roll(x, n - ROT_HALF, 1) * sa + pltpu.roll(x, ROT_HALF, 1) * sb


def _rope_bwd(dy, cos, sa, sb):
    n = dy.shape[-1]
    return dy * cos + pltpu.roll(dy * sa, ROT_HALF, 1) + pltpu.roll(dy * sb, n - ROT_HALF, 1)


def _rope_tables(t):
    pos = lax.broadcasted_iota(jnp.int32, (t, 2 * HEAD_DIM), 0).astype(F32) - LEAD_PAD
    dim = lax.broadcasted_iota(jnp.int32, (t, 2 * HEAD_DIM), 1) % HEAD_DIM
    pair = (dim % ROT_HALF).astype(F32)
    inv_freq = jnp.power(jnp.float32(ROPE_THETA), -(2.0 * pair) / ROT_DIM)
    ang = pos * inv_freq
    cos, sin = jnp.cos(ang), jnp.sin(ang)
    return (jnp.where(dim < ROT_DIM, cos, 1.0), jnp.where(dim < ROT_HALF, -sin, 0.0),
            jnp.where((dim >= ROT_HALF) & (dim < ROT_DIM), sin, 0.0))


def _in_proj(h, g, w, tabs, tm):
    t = h.shape[0]

    def body(h_ref, g_ref, w_ref, c_ref, sa_ref, sb_ref, a_ref, q_ref, k_ref, v_ref, b_ref, cg_ref, hc_ref):
        a = _rms(h_ref[...], g_ref[...]).astype(BF16)
        a_ref[...] = a
        p = jnp.dot(a, w_ref[...], preferred_element_type=F32)
        cos, sa, sb = c_ref[...], sa_ref[...], sb_ref[...]
        rep = ATTN_W // (2 * HEAD_DIM)
        q = _rope(p[:, :ATTN_W], jnp.tile(cos, (1, rep)), jnp.tile(sa, (1, rep)), jnp.tile(sb, (1, rep)))
        q_ref[...] = (q * SCALE).astype(BF16)
        k_ref[...] = _rope(p[:, ATTN_W:ATTN_W + KV_W], cos, sa, sb).astype(BF16)
        v_ref[...] = p[:, ATTN_W + KV_W:QKV_W].astype(BF16)
        b_ref[...] = p[:, QKV_W:QKV_W + CONV_W]
        cg_ref[...] = p[:, QKV_W + CONV_W:QKV_W + 2 * CONV_W]
        hc_ref[...] = p[:, QKV_W + 2 * CONV_W:]

    row = lambda n: pl.BlockSpec((tm, n), lambda i: (i, 0))
    full = lambda a: pl.BlockSpec(a.shape, lambda i: (0, 0))
    return pl.pallas_call(
        body, name="in_proj", grid=(t // tm,),
        in_specs=[row(D_MODEL), full(g), full(w), row(2 * HEAD_DIM), row(2 * HEAD_DIM), row(2 * HEAD_DIM)],
        out_specs=[row(D_MODEL), row(ATTN_W), row(KV_W), row(KV_W), row(CONV_W), row(CONV_W), row(CONV_W)],
        out_shape=[jax.ShapeDtypeStruct((t, D_MODEL), BF16), jax.ShapeDtypeStruct((t, ATTN_W), BF16),
                   jax.ShapeDtypeStruct((t, KV_W), BF16), jax.ShapeDtypeStruct((t, KV_W), BF16),
                   jax.ShapeDtypeStruct((t, CONV_W), F32), jax.ShapeDtypeStruct((t, CONV_W), F32),
                   jax.ShapeDtypeStruct((t, CONV_W), F32)],
        compiler_params=_params("parallel"),
    )(h, g, w, *tabs)


def _attn_bias():
    r = lax.broadcasted_iota(jnp.int32, (3, BLOCK, 2 * BLOCK), 1)
    c = lax.broadcasted_iota(jnp.int32, (3, BLOCK, 2 * BLOCK), 2)
    i = lax.broadcasted_iota(jnp.int32, (3, BLOCK, 2 * BLOCK), 0)
    ok = (c > r) & (c <= r + BLOCK) & (c + (i - 1) * BLOCK >= LEAD_PAD)
    return jnp.where(ok, 0.0, -jnp.inf).astype(F32)


def _attn_probs(q4, kg, sk, bias4):
    s = lax.dot_general(q4, kg, (((1,), (1,)), ((), ())), preferred_element_type=F32) + bias4
    m = jnp.maximum(jnp.max(s, axis=-1, keepdims=True), sk)
    e = jnp.exp(s - m)
    es = jnp.exp(sk - m)
    rden = 1.0 / (jnp.sum(e, axis=-1, keepdims=True) + es)
    return e * rden, es * rden


def _stack_heads(ref, rows, g):
    return jnp.concatenate(
        [ref[rows, (GROUP * g + j) * HEAD_DIM:(GROUP * g + j + 1) * HEAD_DIM] for j in range(GROUP)], axis=0)


def _sink_column(s_ref, g):
    return jnp.concatenate([jnp.full((BLOCK, 1), s_ref[GROUP * g + j], F32) for j in range(GROUP)], axis=0)


def _two_blocks(ref, i):
    prev = jnp.maximum(i - 1, 0)
    return jnp.concatenate([ref[pl.ds(pl.multiple_of(prev * BLOCK, BLOCK), BLOCK), :],
                            ref[pl.ds(pl.multiple_of(i * BLOCK, BLOCK), BLOCK), :]], axis=0)


def _block_bias(bias_ref, i):
    b = bias_ref[jnp.minimum(i, 2)]
    return jnp.concatenate([b] * GROUP, axis=0)


def _attn_fwd(q, k, v, bias, sinks, tm):
    t = q.shape[0]
    per_step = tm // BLOCK

    def body(s_ref, q_ref, k_ref, v_ref, bias_ref, o_ref):
        for b in range(per_step):
            i = pl.program_id(0) * per_step + b
            rows = slice(b * BLOCK, (b + 1) * BLOCK)
            kc, vc = _two_blocks(k_ref, i), _two_blocks(v_ref, i)
            bias4 = _block_bias(bias_ref, i)
            for g in range(N_KV_HEADS):
                lanes = slice(g * HEAD_DIM, (g + 1) * HEAD_DIM)
                p, _ = _attn_probs(_stack_heads(q_ref, rows, g), kc[:, lanes], _sink_column(s_ref, g), bias4)
                o4 = jnp.dot(p.astype(BF16), vc[:, lanes], preferred_element_type=F32)
                for j in range(GROUP):
                    hh = GROUP * g + j
                    o_ref[rows, hh * HEAD_DIM:(hh + 1) * HEAD_DIM] = o4[j * BLOCK:(j + 1) * BLOCK]

    whole = pl.BlockSpec((t, KV_W), lambda i: (0, 0))
    return pl.pallas_call(
        body, name="attn_fwd", grid=(t // tm,),
        in_specs=[pl.BlockSpec(memory_space=pltpu.SMEM), pl.BlockSpec((tm, ATTN_W), lambda i: (i, 0)), whole, whole,
                  pl.BlockSpec(bias.shape, lambda i: (0, 0, 0))],
        out_specs=pl.BlockSpec((tm, ATTN_W), lambda i: (i, 0)),
        out_shape=jax.ShapeDtypeStruct((t, ATTN_W), F32),
        compiler_params=_params("parallel"),
    )(sinks, q, k, v, bias)


def _shift_rows(u, halo, n):
    r = pltpu.roll(u, n, 0)
    hr = pltpu.roll(halo, n, 0)
    idx = lax.broadcasted_iota(jnp.int32, hr.shape, 0)
    return jnp.concatenate([jnp.where(idx < n, hr, r[:8]), r[8:]], axis=0)


def _advance_rows(u, halo, n):
    rows = u.shape[0]
    r = pltpu.roll(u, rows - n, 0)
    hr = pltpu.roll(halo, 8 - n, 0)
    idx = lax.broadcasted_iota(jnp.int32, hr.shape, 0)
    return jnp.concatenate([r[:rows - 8], jnp.where(idx >= 8 - n, hr, r[rows - 8:])], axis=0)


def _mix_out(h, o, b, c, hc, cw, ga, gc, w, gp, tm, deps=()):
    t = h.shape[0]

    def body(h_ref, o_ref, b_ref, c_ref, hc_ref, cw_ref, ga_ref, gc_ref, w_ref, gp_ref, h1_ref, y_ref, z_ref, halo):
        @pl.when(pl.program_id(0) == 0)
        def _():
            halo[...] = jnp.zeros_like(halo)

        u = c_ref[...] * hc_ref[...]
        cv = cw_ref[0:1, :] * _shift_rows(u, halo[...], 2) + cw_ref[1:2, :] * _shift_rows(u, halo[...], 1) \
            + cw_ref[2:3, :] * u
        halo[...] = u[tm - 8:]
        yc = b_ref[...] * cv
        y = jnp.concatenate([_rms(o_ref[...], ga_ref[...]), _rms(yc, gc_ref[...])], axis=1).astype(BF16)
        y_ref[...] = y
        z = jnp.dot(y, w_ref[...].reshape(D_MODEL, D_MODEL), preferred_element_type=F32)
        z_ref[...] = z
        h1_ref[...] = h_ref[...] + _rms(z, gp_ref[...])

    row = lambda n: pl.BlockSpec((tm, n), lambda i: (i, 0))
    full = lambda a: pl.BlockSpec(a.shape, lambda i: (0,) * a.ndim)
    body, dep_specs = _behind(body, deps)
    return pl.pallas_call(
        body, name="mix_out", grid=(t // tm,),
        in_specs=dep_specs + [row(D_MODEL), row(ATTN_W), row(CONV_W), row(CONV_W), row(CONV_W), full(cw), full(ga),
                              full(gc), full(w), full(gp)],
        out_specs=[row(D_MODEL), row(D_MODEL), row(D_MODEL)],
        out_shape=[jax.ShapeDtypeStruct((t, D_MODEL), F32), jax.ShapeDtypeStruct((t, D_MODEL), BF16),
                   jax.ShapeDtypeStruct((t, D_MODEL), F32)],
        scratch_shapes=[pltpu.VMEM((8, CONV_W), F32)],
        compiler_params=_params("arbitrary"),
    )(*deps, h, o, b, c, hc, cw, ga, gc, w, gp)


def _mlp(h1, g1, wu, wd, g2, tm):
    t = h1.shape[0]
    nj = D_FF // FF_CHUNK

    def body(h1_ref, g1_ref, wu_ref, wd_ref, g2_ref, h2_ref, a2_ref, act_ref, f_ref, acc):
        j = pl.program_id(1)

        @pl.when(j == 0)
        def _():
            a2_ref[...] = _rms(h1_ref[...], g1_ref[...]).astype(BF16)

        up = jnp.dot(a2_ref[...], wu_ref[...], preferred_element_type=F32)
        act = jnp.square(jnp.maximum(up, 0.0)).astype(BF16)
        act_ref[...] = act
        part = jnp.dot(act, wd_ref[...], preferred_element_type=F32)

        @pl.when(j == 0)
        def _():
            acc[...] = part

        @pl.when(j > 0)
        def _():
            acc[...] += part

        @pl.when(j == nj - 1)
        def _():
            f = acc[...]
            f_ref[...] = f
            h2_ref[...] = h1_ref[...] + _rms(f, g2_ref[...])

    row = pl.BlockSpec((tm, D_MODEL), lambda i, j: (i, 0))
    vec = pl.BlockSpec((1, D_MODEL), lambda i, j: (0, 0))
    quarter = pl.BlockSpec((None, D_MODEL, FF_CHUNK), lambda i, j: (j, 0, 0))
    return pl.pallas_call(
        body, name="mlp", grid=(t // tm, nj),
        in_specs=[row, vec, quarter, quarter, vec],
        out_specs=[row, row, pl.BlockSpec((tm, FF_CHUNK), lambda i, j: (i, j)), row],
        out_shape=[jax.ShapeDtypeStruct((t, D_MODEL), F32), jax.ShapeDtypeStruct((t, D_MODEL), BF16),
                   jax.ShapeDtypeStruct((t, D_FF), BF16), jax.ShapeDtypeStruct((t, D_MODEL), F32)],
        scratch_shapes=[pltpu.VMEM((tm, D_MODEL), F32)],
        compiler_params=_params("parallel", "arbitrary"),
    )(h1, g1, wu, wd, g2)


def _loss_head(h, target, tm):
    t = h.shape[0]
    per_step = tm // BLOCK

    def body(h_ref, *rest):
        t_refs, (loss_ref, dh_ref) = rest[:per_step], rest[per_step:]
        i = pl.program_id(0)

        @pl.when(i == 0)
        def _():
            loss_ref[...] = jnp.zeros_like(loss_ref)

        total = jnp.zeros((), F32)
        for b in range(per_step):
            rows = slice(b * BLOCK, (b + 1) * BLOCK)
            err = h_ref[rows, :] - t_refs[b][...]
            if b == 0:
                err = jnp.where(i == 0, 0.0, err)
            dh_ref[rows, :] = err * (1.0 / D_MODEL)
            total = total + jnp.sum(err * err)
        loss_ref[...] += total * (0.5 / D_MODEL)

    def target_block(b):
        return pl.BlockSpec((BLOCK, D_MODEL), lambda i: (jnp.maximum(i * per_step + b - 1, 0), 0))

    return pl.pallas_call(
        body, name="loss_head", grid=(t // tm,),
        in_specs=[pl.BlockSpec((tm, D_MODEL), lambda i: (i, 0))] + [target_block(b) for b in range(per_step)],
        out_specs=[pl.BlockSpec((8, 128), lambda i: (0, 0)), pl.BlockSpec((tm, D_MODEL), lambda i: (i, 0))],
        out_shape=[jax.ShapeDtypeStruct((8, 128), F32), jax.ShapeDtypeStruct((t, D_MODEL), F32)],
        compiler_params=_params("arbitrary"),
    )(h, *([target] * per_step))


def _mlp_bwd(dh2, f, g2, act, wd, wu, h1, g1, tm, deps=()):
    t = dh2.shape[0]
    nj = D_FF // FF_CHUNK

    def body(dh2_ref, f_ref, g2_ref, act_ref, wd_ref, wu_ref, h1_ref, g1_ref, dh1_ref, df_ref, dup_ref, dg2_ref,
             dg1_ref, acc):
        i, j = pl.program_id(0), pl.program_id(1)

        @pl.when((i == 0) & (j == 0))
        def _():
            dg2_ref[...] = jnp.zeros_like(dg2_ref)
            dg1_ref[...] = jnp.zeros_like(dg1_ref)

        @pl.when(j == 0)
        def _():
            df, dg = _rms_bwd(dh2_ref[...], f_ref[...], g2_ref[...])
            df_ref[...] = df.astype(BF16)
            dg2_ref[...] += dg

        dact = lax.dot_general(df_ref[...], wd_ref[...], (((1,), (1,)), ((), ())), preferred_element_type=F32)
        dup = (dact * (2.0 * jnp.sqrt(act_ref[...].astype(F32)))).astype(BF16)
        dup_ref[...] = dup
        part = lax.dot_general(dup, wu_ref[...], (((1,), (1,)), ((), ())), preferred_element_type=F32)

        @pl.when(j == 0)
        def _():
            acc[...] = part

        @pl.when(j > 0)
        def _():
            acc[...] += part

        @pl.when(j == nj - 1)
        def _():
            dx, dg = _rms_bwd(acc[...], h1_ref[...], g1_ref[...])
            dh1_ref[...] = dh2_ref[...] + dx
            dg1_ref[...] += dg

    row = pl.BlockSpec((tm, D_MODEL), lambda i, j: (i, 0))
    vec = pl.BlockSpec((1, D_MODEL), lambda i, j: (0, 0))
    chunk = pl.BlockSpec((tm, FF_CHUNK), lambda i, j: (i, j))
    quarter = pl.BlockSpec((None, D_MODEL, FF_CHUNK), lambda i, j: (j, 0, 0))
    body, dep_specs = _behind(body, deps)
    return pl.pallas_call(
        body, name="mlp_bwd", grid=(t // tm, nj),
        in_specs=dep_specs + [row, row, vec, chunk, quarter, quarter, row, vec],
        out_specs=[row, row, chunk, vec, vec],
        out_shape=[jax.ShapeDtypeStruct((t, D_MODEL), F32), jax.ShapeDtypeStruct((t, D_MODEL), BF16),
                   jax.ShapeDtypeStruct((t, D_FF), BF16), jax.ShapeDtypeStruct((1, D_MODEL), F32),
                   jax.ShapeDtypeStruct((1, D_MODEL), F32)],
        scratch_shapes=[pltpu.VMEM((tm, D_MODEL), F32)],
        compiler_params=_params("arbitrary", "arbitrary"),
    )(*deps, dh2, f, g2, act, wd, wu, h1, g1)


def _weight_grad(x, y, tm, name):
    t, k = x.shape
    n = y.shape[1]
    tk = tn = FF_CHUNK

    def body(x_ref, y_ref, o_ref):
        @pl.when(pl.program_id(2) == 0)
        def _():
            o_ref[...] = jnp.zeros_like(o_ref)

        o_ref[...] += lax.dot_general(x_ref[...], y_ref[...], (((0,), (0,)), ((), ())), preferred_element_type=F32)

    return pl.pallas_call(
        body, name=name, grid=(k // tk, n // tn, t // tm),
        in_specs=[pl.BlockSpec((tm, tk), lambda a, b, r: (r, a)), pl.BlockSpec((tm, tn), lambda a, b, r: (r, b))],
        out_specs=pl.BlockSpec((None, None, tk, tn), lambda a, b, r: (a, b, 0, 0)),
        out_shape=jax.ShapeDtypeStruct((k // tk, n // tn, tk, tn), F32),
        compiler_params=_params("parallel", "parallel", "arbitrary"),
    )(x, y)


def _weight_grad_in(a, dproj, tm):
    t = a.shape[0]
    nt = t // tm
    qw = IN_W // N_CHIPS

    def body(a_ref, d_ref, o_ref, acc):
        r = pl.program_id(0)

        @pl.when(r == 0)
        def _():
            acc[...] = jnp.zeros_like(acc)

        acc[...] += lax.dot_general(a_ref[...], d_ref[...], (((0,), (0,)), ((), ())), preferred_element_type=F32)

        @pl.when(r == nt - 1)
        def _():
            for s in range(N_CHIPS):
                o_ref[s] = acc[:, s * qw:(s + 1) * qw]

    return pl.pallas_call(
        body, name="grad_w_in", grid=(nt,),
        in_specs=[pl.BlockSpec((tm, D_MODEL), lambda r: (r, 0)), pl.BlockSpec((tm, IN_W), lambda r: (r, 0))],
        out_specs=pl.BlockSpec(memory_space=pltpu.VMEM),
        out_shape=jax.ShapeDtypeStruct((N_CHIPS, D_MODEL, qw), F32),
        scratch_shapes=[pltpu.VMEM((D_MODEL, IN_W), F32)],
        compiler_params=_params("arbitrary"),
    )(a, dproj)


def _mix_out_bwd(dh1, z, gp, w, o, b, c, hc, cw, ga, gc, tm, deps=()):
    t = dh1.shape[0]
    nt = t // tm
    per8 = tm // 8

    def body(dh1_ref, z_ref, gp_ref, w_ref, o_ref, b_ref, c_ref, hc_ref, cp_ref, hp_ref, cw_ref, ga_ref, gc_ref,
             dz_ref, do_ref, dbch_ref, dgp_ref, dga_ref, dgc_ref, dcw_ref, halo):
        i = pl.program_id(0)

        @pl.when(i == 0)
        def _():
            halo[...] = jnp.zeros_like(halo)
            dgp_ref[...] = jnp.zeros_like(dgp_ref)
            dga_ref[...] = jnp.zeros_like(dga_ref)
            dgc_ref[...] = jnp.zeros_like(dgc_ref)
            dcw_ref[...] = jnp.zeros_like(dcw_ref)

        dz, dgp = _rms_bwd(dh1_ref[...], z_ref[...], gp_ref[...])
        dgp_ref[...] += dgp
        dz = dz.astype(BF16)
        dz_ref[...] = dz
        dy = lax.dot_general(dz, w_ref[...].reshape(D_MODEL, D_MODEL), (((1,), (1,)), ((), ())),
                             preferred_element_type=F32)
        do, dga = _rms_bwd(dy[:, :ATTN_W], o_ref[...], ga_ref[...])
        do_ref[...] = do
        dga_ref[...] += dga

        u = c_ref[...] * hc_ref[...]
        first = i == nt - 1
        u_before = jnp.where(first, 0.0, cp_ref[...] * hp_ref[...])
        u1 = _shift_rows(u, u_before, 1)
        u2 = _shift_rows(u, u_before, 2)
        cv = cw_ref[0:1, :] * u2 + cw_ref[1:2, :] * u1 + cw_ref[2:3, :] * u
        bb = b_ref[...]
        dyc, dgc = _rms_bwd(dy[:, ATTN_W:], bb * cv, gc_ref[...])
        dgc_ref[...] += dgc
        dcv = dyc * bb
        d1 = _advance_rows(dcv, halo[...], 1)
        d2 = _advance_rows(dcv, halo[...], 2)
        halo[...] = dcv[:8]
        du = cw_ref[2:3, :] * dcv + cw_ref[1:2, :] * d1 + cw_ref[0:1, :] * d2
        dbch_ref[...] = jnp.concatenate([dyc * cv, du * hc_ref[...], du * c_ref[...]], axis=1).astype(BF16)
        dcw_ref[...] += jnp.concatenate([jnp.sum(dcv * u2, axis=0, keepdims=True),
                                         jnp.sum(dcv * u1, axis=0, keepdims=True),
                                         jnp.sum(dcv * u, axis=0, keepdims=True)], axis=0)

    row = lambda n: pl.BlockSpec((tm, n), lambda i: (nt - 1 - i, 0))
    before = pl.BlockSpec((8, CONV_W), lambda i: (jnp.maximum((nt - 1 - i) * per8 - 1, 0), 0))
    full = lambda a: pl.BlockSpec(a.shape, lambda i: (0,) * a.ndim)
    vec = lambda n: pl.BlockSpec((1, n), lambda i: (0, 0))
    body, dep_specs = _behind(body, deps)
    return pl.pallas_call(
        body, name="mix_out_bwd", grid=(nt,),
        in_specs=dep_specs + [row(D_MODEL), row(D_MODEL), full(gp), full(w), row(ATTN_W), row(CONV_W), row(CONV_W),
                              row(CONV_W), before, before, full(cw), full(ga), full(gc)],
        out_specs=[row(D_MODEL), row(ATTN_W), row(3 * CONV_W), vec(D_MODEL), vec(ATTN_W), vec(CONV_W),
                   pl.BlockSpec((CONV_K, CONV_W), lambda i: (0, 0))],
        out_shape=[jax.ShapeDtypeStruct((t, D_MODEL), BF16), jax.ShapeDtypeStruct((t, ATTN_W), F32),
                   jax.ShapeDtypeStruct((t, 3 * CONV_W), BF16), jax.ShapeDtypeStruct((1, D_MODEL), F32),
                   jax.ShapeDtypeStruct((1, ATTN_W), F32), jax.ShapeDtypeStruct((1, CONV_W), F32),
                   jax.ShapeDtypeStruct((CONV_K, CONV_W), F32)],
        scratch_shapes=[pltpu.VMEM((8, CONV_W), F32)],
        compiler_params=_params("arbitrary"),
    )(*deps, dh1, z, gp, w, o, b, c, hc, c, hc, cw, ga, gc)


def _attn_bwd(q, k, v, o, do, bias, sinks, tm, deps=()):
    t = q.shape[0]
    per_step = tm // BLOCK

    def body(s_ref, q_ref, k_ref, v_ref, o_ref, do_ref, bias_ref, dq_ref, dk_ref, dv_ref, ds_ref):
        step = pl.program_id(0)

        @pl.when(step == 0)
        def _():
            ds_ref[...] = jnp.zeros_like(ds_ref)

        dsink = [jnp.zeros((GROUP * BLOCK, 1), F32) for _ in range(N_KV_HEADS)]
        ahead = None
        for b in range(per_step):
            i = step * per_step + b
            rows = slice(b * BLOCK, (b + 1) * BLOCK)
            kc, vc = _two_blocks(k_ref, i), _two_blocks(v_ref, i)
            bias4 = _block_bias(bias_ref, i)
            dkg, dvg = [], []
            for g in range(N_KV_HEADS):
                lanes = slice(g * HEAD_DIM, (g + 1) * HEAD_DIM)
                q4 = _stack_heads(q_ref, rows, g)
                p, ps = _attn_probs(q4, kc[:, lanes], _sink_column(s_ref, g), bias4)
                o4, do4 = _stack_heads(o_ref, rows, g), _stack_heads(do_ref, rows, g)
                do4b = do4.astype(BF16)
                dp = lax.dot_general(do4b, vc[:, lanes], (((1,), (1,)), ((), ())), preferred_element_type=F32)
                drow = jnp.sum(do4 * o4, axis=-1, keepdims=True)
                ds = (p * (dp - drow)).astype(BF16)
                dsink[g] = dsink[g] + ps * drow
                dq4 = jnp.dot(ds, kc[:, lanes], preferred_element_type=F32) * SCALE
                dkg.append(lax.dot_general(ds, q4, (((0,), (0,)), ((), ())), preferred_element_type=F32))
                dvg.append(lax.dot_general(p.astype(BF16), do4b, (((0,), (0,)), ((), ())),
                                           preferred_element_type=F32))
                for j in range(GROUP):
                    hh = GROUP * g + j
                    dq_ref[rows, hh * HEAD_DIM:(hh + 1) * HEAD_DIM] = dq4[j * BLOCK:(j + 1) * BLOCK]
            dkb, dvb = jnp.concatenate(dkg, axis=1), jnp.concatenate(dvg, axis=1)
            if b == 0:
                @pl.when(step > 0)
                def _():
                    before = pl.ds(pl.multiple_of((i - 1) * BLOCK, BLOCK), BLOCK)
                    dk_ref[before, :] += dkb[:BLOCK]
                    dv_ref[before, :] += dvb[:BLOCK]
            else:
                at = pl.ds(pl.multiple_of((i - 1) * BLOCK, BLOCK), BLOCK)
                dk_ref[at, :] = ahead[0] + dkb[:BLOCK]
                dv_ref[at, :] = ahead[1] + dvb[:BLOCK]
            ahead = (dkb[BLOCK:], dvb[BLOCK:])
        last = pl.ds(pl.multiple_of(((step + 1) * per_step - 1) * BLOCK, BLOCK), BLOCK)
        dk_ref[last, :] = ahead[0]
        dv_ref[last, :] = ahead[1]
        for g in range(N_KV_HEADS):
            for j in range(GROUP):
                hh = GROUP * g + j
                ds_ref[hh:hh + 1, :] -= jnp.sum(dsink[g][j * BLOCK:(j + 1) * BLOCK])

    whole = pl.BlockSpec((t, KV_W), lambda i: (0, 0))
    blk = pl.BlockSpec((tm, ATTN_W), lambda i: (i, 0))
    body, dep_specs = _behind(body, deps)
    return pl.pallas_call(
        body, name="attn_bwd", grid=(t // tm,),
        in_specs=dep_specs + [pl.BlockSpec(memory_space=pltpu.SMEM), blk, whole, whole, blk, blk,
                              pl.BlockSpec(bias.shape, lambda i: (0, 0, 0))],
        out_specs=[blk, whole, whole, pl.BlockSpec((N_Q_HEADS, 128), lambda i: (0, 0))],
        out_shape=[jax.ShapeDtypeStruct((t, ATTN_W), F32), jax.ShapeDtypeStruct((t, KV_W), F32),
                   jax.ShapeDtypeStruct((t, KV_W), F32), jax.ShapeDtypeStruct((N_Q_HEADS, 128), F32)],
        compiler_params=_params("arbitrary"),
    )(*deps, sinks, q, k, v, o, do, bias)


def _in_proj_bwd(dq, dk, dv, dbch, w, dh1, h, g, tabs, tm):
    t = h.shape[0]

    def body(dq_ref, dk_ref, dv_ref, dbch_ref, w_ref, dh1_ref, h_ref, g_ref, c_ref, sa_ref, sb_ref, dh_ref, dp_ref,
             dg_ref):
        @pl.when(pl.program_id(0) == 0)
        def _():
            dg_ref[...] = jnp.zeros_like(dg_ref)

        cos, sa, sb = c_ref[...], sa_ref[...], sb_ref[...]
        rep = ATTN_W // (2 * HEAD_DIM)
        dqr = _rope_bwd(dq_ref[...], jnp.tile(cos, (1, rep)), jnp.tile(sa, (1, rep)), jnp.tile(sb, (1, rep)))
        dkr = _rope_bwd(dk_ref[...], cos, sa, sb)
        dp = jnp.concatenate([dqr.astype(BF16), dkr.astype(BF16), dv_ref[...].astype(BF16), dbch_ref[...]], axis=1)
        dp_ref[...] = dp
        da = lax.dot_general(dp, w_ref[...], (((1,), (1,)), ((), ())), preferred_element_type=F32)
        dx, dg = _rms_bwd(da, h_ref[...], g_ref[...])
        dh_ref[...] = dh1_ref[...] + dx
        dg_ref[...] += dg

    row = lambda n: pl.BlockSpec((tm, n), lambda i: (i, 0))
    full = lambda a: pl.BlockSpec(a.shape, lambda i: (0, 0))
    return pl.pallas_call(
        body, name="in_proj_bwd", grid=(t // tm,),
        in_specs=[row(ATTN_W), row(KV_W), row(KV_W), row(3 * CONV_W), full(w), row(D_MODEL), row(D_MODEL), full(g),
                  row(2 * HEAD_DIM), row(2 * HEAD_DIM), row(2 * HEAD_DIM)],
        out_specs=[row(D_MODEL), row(IN_W), pl.BlockSpec((1, D_MODEL), lambda i: (0, 0))],
        out_shape=[jax.ShapeDtypeStruct((t, D_MODEL), F32), jax.ShapeDtypeStruct((t, IN_W), BF16),
                   jax.ShapeDtypeStruct((1, D_MODEL), F32)],
        compiler_params=_params("arbitrary"),
    )(dq, dk, dv, dbch, w, dh1, h, g, *tabs)


class _Tiles:
    def __init__(self, t):
        self.tm = _row_tile(t, 640)
        self.ts = _row_tile(t, 320)
        self.tabs = _rope_tables(t)
        self.bias = _attn_bias()


def _mixer_fwd(h, p, tl):
    a, q, k, v, b, c, hc = _in_proj(h, p["mix_pre_g"], p["w_in"], tl.tabs, tl.ts)
    o = _attn_fwd(q, k, v, tl.bias, p["sinks"], tl.tm)
    return (h, a, q, k, v, b, c, hc, o)


def _out_fwd(mixed, p, tl, deps=()):
    h, a, q, k, v, b, c, hc, o = mixed
    h1, y, z = _mix_out(h, o, b, c, hc, p["conv_w"], p["attn_out_g"], p["conv_out_g"], p["w_out"], p["mix_post_g"],
                        tl.ts, deps)
    return h1, mixed + (h1, y, z)


def _mlp_fwd(h1, saved, p, tl):
    h2, a2, act, f = _mlp(h1, p["mlp_pre_g"], p["w_up"], p["w_down"], p["mlp_post_g"], tl.tm)
    return h2, saved + (a2, act, f)


def _mlp_part_bwd(dh, saved, p, tl, deps=()):
    h1, a2, act, f = saved[9], saved[12], saved[13], saved[14]
    dh1, df, dup, dg2, dg1 = _mlp_bwd(dh, f, p["mlp_post_g"], act, p["w_down"], p["w_up"], h1, p["mlp_pre_g"], tl.tm,
                                      deps)
    g = {"w_down": _weight_grad(act, df, tl.tm, "grad_w_down").reshape(N_CHIPS, FF_CHUNK, D_MODEL),
         "w_up": _weight_grad(a2, dup, tl.tm, "grad_w_up").reshape(N_CHIPS, D_MODEL, FF_CHUNK),
         "mlp_post_g": dg2, "mlp_pre_g": dg1}
    return dh1, g


def _mix_out_part_bwd(dh1, saved, p, tl, deps=()):
    b, c, hc, o, y, z = saved[5], saved[6], saved[7], saved[8], saved[10], saved[11]
    dz, do, dbch, dgp, dga, dgc, dcw = _mix_out_bwd(dh1, z, p["mix_post_g"], p["w_out"], o, b, c, hc, p["conv_w"],
                                                    p["attn_out_g"], p["conv_out_g"], tl.ts, deps)
    g = {"w_out": _weight_grad(y, dz, tl.tm, "grad_w_out").reshape(N_CHIPS, D_MODEL // N_CHIPS, D_MODEL),
         "mix_post_g": dgp, "attn_out_g": dga, "conv_out_g": dgc, "conv_w": dcw}
    return (dh1, do, dbch), g


def _attn_in_part_bwd(carry, saved, p, tl, deps=()):
    dh1, do, dbch = carry
    h_in, a, q, k, v, o = saved[0], saved[1], saved[2], saved[3], saved[4], saved[8]
    dq, dk, dv, dsink = _attn_bwd(q, k, v, o, do, tl.bias, p["sinks"], tl.tm, deps)
    dh, dproj, dgi = _in_proj_bwd(dq, dk, dv, dbch, p["w_in"], dh1, h_in, p["mix_pre_g"], tl.tabs, tl.ts)
    return dh, {"w_in": _weight_grad_in(a, dproj, tl.tm), "mix_pre_g": dgi, "sinks": dsink[:, 0]}


def _place():
    return lax.axis_index("x"), lax.axis_index("y"), lax.axis_index("c")


def _other_chips(x, y):
    return [(1 - x, y), (x, 1 - y), (1 - x, 1 - y)]


_HBM = pl.BlockSpec(memory_space=pltpu.HBM)
_SEM = pl.BlockSpec(memory_space=pltpu.SEMAPHORE)
_EFFECT = pltpu.SideEffectType.DATAFLOW_SIDE_EFFECTING


class _Exchange:
    def __init__(self, name, bufs, plan, n, after=()):
        self.name, self.plan, nb = name, plan, len(bufs)
        n_in = nb + len(after)

        def body(*refs):
            send, recv, token = refs[n_in], refs[n_in + 1], refs[-1]
            for k, (src, dst, target, _) in enumerate(plan(refs[:nb])):
                pltpu.make_async_remote_copy(src_ref=src, dst_ref=dst, send_sem=send.at[k], recv_sem=recv.at[k],
                                             device_id=target, device_id_type=MESH).start()
            token[...] = jnp.zeros_like(token)

        outs = pl.pallas_call(
            body, name=name + "_start",
            out_shape=(pltpu.SemaphoreType.DMA((n,)), pltpu.SemaphoreType.DMA((n,)),
                       *[pltpu.HBM(b.shape, b.dtype) for b in bufs], jax.ShapeDtypeStruct((8, 128), F32)),
            in_specs=[_HBM] * nb + [pl.BlockSpec(memory_space=pl.ANY)] * len(after),
            out_specs=(_SEM, _SEM, *[_HBM] * nb, pl.BlockSpec(memory_space=pltpu.VMEM)),
            input_output_aliases={i: 2 + i for i in range(nb)},
            compiler_params=pltpu.CompilerParams(has_side_effects=_EFFECT),
        )(*[pltpu.with_memory_space_constraint(b, pltpu.HBM) for b in bufs], *after)
        self.send, self.recv, self.bufs, self.token = outs[0], outs[1], list(outs[2:2 + nb]), outs[-1]

    def wait(self, after):
        plan, nb = self.plan, len(self.bufs)

        def body(*refs):
            send, recv = refs[nb], refs[nb + 1]
            for k, (src, _, target, land) in enumerate(plan(refs[:nb])):
                cp = pltpu.make_async_remote_copy(src_ref=src, dst_ref=land, send_sem=send.at[k], recv_sem=recv.at[k],
                                                  device_id=target, device_id_type=MESH)
                cp.wait_send()
                cp.wait_recv()

        outs = pl.pallas_call(
            body, name=self.name + "_wait", out_shape=[pltpu.HBM(b.shape, b.dtype) for b in self.bufs],
            in_specs=[_HBM] * nb + [_SEM, _SEM, pl.BlockSpec(memory_space=pl.ANY)], out_specs=[_HBM] * nb,
            input_output_aliases={i: i for i in range(nb)},
            compiler_params=pltpu.CompilerParams(has_side_effects=_EFFECT),
        )(*self.bufs, self.send, self.recv, after)
        return list(outs)


def _gather_plan(n):
    def plan(refs):
        x, y, c = _place()
        me = 2 * x + y
        return [(refs[a].at[me], refs[a].at[me], (px, py, c), refs[a].at[2 * px + py])
                for a in range(n) for px, py in _other_chips(x, y)]

    return plan


def _swap_plan(n, half_rows):
    def plan(refs):
        x, y, c = _place()
        out = []
        for a in range(n):
            hr = half_rows[a]
            out.append((refs[a].at[:, pl.ds((1 - c) * hr, hr)], refs[n + a], (x, y, 1 - c), refs[n + a]))
        return out

    return plan


def _scatter_plan(n):
    def plan(refs):
        x, y, c = _place()
        return [(refs[a].at[2 * px + py], refs[n + a].at[k], (px, py, c), refs[n + a].at[k])
                for a in range(n) for k, (px, py) in enumerate(_other_chips(x, y))]

    return plan


def _join_plan(n):
    def plan(refs):
        x, y, c = _place()
        return [(refs[a].at[c], refs[a].at[c], (x, y, 1 - c), refs[a].at[1 - c]) for a in range(n)]

    return plan


def _add_half(g, r):
    rows, cols = g.shape[1], g.shape[2]
    hr = rows // 2
    tr = min(hr, 256)
    per = hr // tr
    first = (lax.axis_index("c") * per).astype(jnp.int32).reshape(1)

    def body(first_ref, g_ref, r_ref, o_ref):
        o_ref[...] = (g_ref[...] + r_ref[...]).astype(BF16)

    blk = pl.BlockSpec((None, tr, cols), lambda s, i, first_ref: (s, i, 0))
    return pl.pallas_call(
        body, name="add_half",
        grid_spec=pltpu.PrefetchScalarGridSpec(
            num_scalar_prefetch=1, grid=(N_CHIPS, per),
            in_specs=[pl.BlockSpec((None, tr, cols), lambda s, i, first_ref: (s, i + first_ref[0], 0)), blk],
            out_specs=blk),
        out_shape=jax.ShapeDtypeStruct(r.shape, BF16),
        compiler_params=_params("parallel", "parallel"),
    )(first, g, r)


def _sum_chips(s, q):
    rows, cols = s.shape[1], s.shape[2]
    tr = min(rows, 256)
    x, y, c = _place()
    where = jnp.stack([2 * x + y, c]).astype(jnp.int32)

    def body(where_ref, s_ref, q_ref, o_ref):
        part = [q_ref[k].astype(F32) for k in range(N_CHIPS - 1)]
        o_ref[...] = ((s_ref[...].astype(F32) + part[0]) + part[1]) + part[2]

    return pl.pallas_call(
        body, name="sum_chips",
        grid_spec=pltpu.PrefetchScalarGridSpec(
            num_scalar_prefetch=1, grid=(rows // tr,),
            in_specs=[pl.BlockSpec((None, tr, cols), lambda i, where_ref: (where_ref[0], i, 0)),
                      pl.BlockSpec((N_CHIPS - 1, tr, cols), lambda i, where_ref: (0, i, 0))],
            out_specs=pl.BlockSpec((None, tr, cols), lambda i, where_ref: (where_ref[1], i, 0))),
        out_shape=jax.ShapeDtypeStruct((2, rows, cols), F32),
        compiler_params=_params("parallel"),
    )(where, s, q)


def _sum_devices(packed):
    def body(p_ref, o_ref, land, send_sems, recv_sems):
        x, y, c = _place()
        me = 4 * x + 2 * y + c
        land[me] = p_ref[...]
        sends = []
        for k in range(1, N_DEV):
            px, py, pc = x ^ (k >> 2), y ^ ((k >> 1) & 1), c ^ (k & 1)
            cp = pltpu.make_async_remote_copy(src_ref=p_ref, dst_ref=land.at[me], send_sem=send_sems.at[k - 1],
                                              recv_sem=recv_sems.at[k - 1], device_id=(px, py, pc), device_id_type=MESH)
            cp.start()
            sends.append(cp)
        for k in range(1, N_DEV):
            px, py, pc = x ^ (k >> 2), y ^ ((k >> 1) & 1), c ^ (k & 1)
            pltpu.make_async_remote_copy(src_ref=p_ref, dst_ref=land.at[4 * px + 2 * py + pc],
                                         send_sem=send_sems.at[k - 1], recv_sem=recv_sems.at[k - 1],
                                         device_id=(px, py, pc), device_id_type=MESH).wait_recv()
        for cp in sends:
            cp.wait_send()
        total = land[0]
        for d in range(1, N_DEV):
            total = total + land[d]
        o_ref[...] = total

    vm = pl.BlockSpec(memory_space=pltpu.VMEM)
    return pl.pallas_call(
        body, name="sum_devices", in_specs=[vm], out_specs=vm,
        out_shape=jax.ShapeDtypeStruct(packed.shape, F32),
        scratch_shapes=[pltpu.VMEM((N_DEV,) + packed.shape, F32), pltpu.SemaphoreType.DMA((N_DEV - 1,)),
                        pltpu.SemaphoreType.DMA((N_DEV - 1,))],
    )(packed)


def _adamw_math(w, g, m, v):
    m = ADAM_B1 * m + (1.0 - ADAM_B1) * g
    v = ADAM_B2 * v + (1.0 - ADAM_B2) * jnp.square(g)
    m_hat = m / (1.0 - ADAM_B1 ** ADAM_STEP)
    v_hat = v / (1.0 - ADAM_B2 ** ADAM_STEP)
    delta = -ADAM_LR * (m_hat / (jnp.sqrt(v_hat) + ADAM_EPS) + ADAM_WD * w)
    return delta, m, v


def _adamw_large(layer, w, halves, m, v, other):
    _, rows, cols = w.shape
    tr = min(rows // 2, 256)
    per = rows // 2 // tr

    def body(w_ref, g_ref, m_ref, v_ref, *rest):
        g_out, d_ref, nm_ref, nv_ref = rest[-4:]
        g = g_ref[...]
        g_out[...] = g
        d_ref[...], nm_ref[...], nv_ref[...] = _adamw_math(w_ref[...], g, m_ref[...], v_ref[...])

    blk = pl.BlockSpec((None, tr, cols), lambda i: (layer, i, 0))
    half = pl.BlockSpec((None, tr, cols), lambda i: (i // per, i % per, 0))
    kept = [] if other is None else list(other)
    return pl.pallas_call(
        body, name="adamw_large", grid=(rows // tr,),
        in_specs=[blk, half, blk, blk] + [pl.BlockSpec(memory_space=pl.ANY)] * len(kept), out_specs=[blk] * 4,
        out_shape=[jax.ShapeDtypeStruct(w.shape, F32)] * 4,
        input_output_aliases={4 + k: k for k in range(len(kept))},
        compiler_params=_params("parallel"),
    )(w, halves, m, v, *kept)


def _adamw_small(ws, gs, ms, vs):
    n = len(ws)

    def body(*refs):
        w_r, g_r, m_r, v_r = refs[:n], refs[n:2 * n], refs[2 * n:3 * n], refs[3 * n:4 * n]
        d_r, nm_r, nv_r = refs[4 * n:5 * n], refs[5 * n:6 * n], refs[6 * n:]
        for a in range(n):
            d_r[a][...], nm_r[a][...], nv_r[a][...] = _adamw_math(w_r[a][...], g_r[a][...], m_r[a][...], v_r[a][...])

    vm = pl.BlockSpec(memory_space=pltpu.VMEM)
    outs = pl.pallas_call(
        body, name="adamw_small", in_specs=[vm] * (4 * n), out_specs=[vm] * (3 * n),
        out_shape=[jax.ShapeDtypeStruct(w.shape, F32) for w in ws] * 3,
    )(*ws, *gs, *ms, *vs)
    return outs[:n], outs[n:2 * n], outs[2 * n:]


_LARGE = ("w_in", "w_out", "w_up", "w_down")
_SMALL = ("meta_tokens", "mix_pre_g", "conv_w", "sinks", "attn_out_g", "conv_out_g", "mix_post_g", "mlp_pre_g",
          "mlp_post_g")
_ORDER = ("meta_tokens", "mix_pre_g", "w_in", "conv_w", "sinks", "attn_out_g", "conv_out_g", "w_out", "mix_post_g",
          "mlp_pre_g", "w_up", "w_down", "mlp_post_g")


class _Reduce:
    def __init__(self, name, grads, after=()):
        self.name, self.n = name, len(grads)
        half_rows = [g.shape[1] // 2 for g in grads]
        zones = [lax.empty((N_CHIPS, hr, g.shape[2]), F32) for g, hr in zip(grads, half_rows)]
        self.exchange = _Exchange(name + "_swap", list(grads) + zones, _swap_plan(self.n, half_rows), self.n, after)

    @property
    def token(self):
        return self.exchange.token

    def scatter(self, after):
        bufs = self.exchange.wait(after)
        sums = [_add_half(g, r) for g, r in zip(bufs[:self.n], bufs[self.n:])]
        zones = [lax.empty((N_CHIPS - 1,) + s.shape[1:], BF16) for s in sums]
        self.exchange = _Exchange(self.name + "_scatter", sums + zones, _scatter_plan(self.n), 3 * self.n)

    def join(self, after):
        bufs = self.exchange.wait(after)
        halves = [_sum_chips(s, q) for s, q in zip(bufs[:self.n], bufs[self.n:])]
        self.exchange = _Exchange(self.name + "_join", halves, _join_plan(self.n), self.n)

    def done(self, after):
        return self.exchange.wait(after)


def _pad_cols(a, n=D_MODEL):
    return jnp.pad(a, ((0, 0), (0, n - a.shape[1])))


def kernel(x, meta_tokens, mix_pre_g, w_in, conv_w, sinks, attn_out_g, conv_out_g, w_out, mix_post_g, mlp_pre_g, w_up, w_down, mlp_post_g, loss_target, m_meta_tokens, m_mix_pre_g, m_w_in, m_conv_w, m_sinks, m_attn_out_g, m_conv_out_g, m_w_out, m_mix_post_g, m_mlp_pre_g, m_w_up, m_w_down, m_mlp_post_g, v_meta_tokens, v_mix_pre_g, v_w_in, v_conv_w, v_sinks, v_attn_out_g, v_conv_out_g, v_w_out, v_mix_post_g, v_mlp_pre_g, v_w_up, v_w_down, v_mlp_post_g):
    w = dict(meta_tokens=meta_tokens, mix_pre_g=mix_pre_g, w_in=w_in, conv_w=conv_w, sinks=sinks,
             attn_out_g=attn_out_g, conv_out_g=conv_out_g, w_out=w_out, mix_post_g=mix_post_g, mlp_pre_g=mlp_pre_g,
             w_up=w_up, w_down=w_down, mlp_post_g=mlp_post_g)
    m = dict(meta_tokens=m_meta_tokens, mix_pre_g=m_mix_pre_g, w_in=m_w_in, conv_w=m_conv_w, sinks=m_sinks,
             attn_out_g=m_attn_out_g, conv_out_g=m_conv_out_g, w_out=m_w_out, mix_post_g=m_mix_post_g,
             mlp_pre_g=m_mlp_pre_g, w_up=m_w_up, w_down=m_w_down, mlp_post_g=m_mlp_post_g)
    v = dict(meta_tokens=v_meta_tokens, mix_pre_g=v_mix_pre_g, w_in=v_w_in, conv_w=v_conv_w, sinks=v_sinks,
             attn_out_g=v_attn_out_g, conv_out_g=v_conv_out_g, w_out=v_w_out, mix_post_g=v_mix_post_g,
             mlp_pre_g=v_mlp_pre_g, w_up=v_w_up, w_down=v_w_down, mlp_post_g=v_mlp_post_g)
    chip = 2 * lax.axis_index("x") + lax.axis_index("y")
    tl = _Tiles(x.shape[1] + BLOCK)

    def zone(quarter):
        return lax.dynamic_update_slice(lax.empty((N_CHIPS,) + quarter.shape, quarter.dtype), quarter[None],
                                        (chip,) + (0,) * quarter.ndim)

    zones = {n: [zone(w[n][l].astype(BF16)) for l in range(DEPTH)] for n in _LARGE}
    first = _Exchange("gather_first", [zones["w_in"][0], zone(w["conv_w"]), zone(w["meta_tokens"])], _gather_plan(3), 9)
    rest = _Exchange("gather_rest", [zones[n][0] for n in ("w_out", "w_up", "w_down")], _gather_plan(3), 9,
                     [first.token])

    def whole_in(quarters):
        return jnp.transpose(quarters, (1, 0, 2)).reshape(D_MODEL, IN_W)

    q_in, q_conv, q_meta = first.wait(rest.token)
    conv_whole = jnp.transpose(q_conv, (1, 2, 0, 3)).reshape(DEPTH, CONV_K, CONV_W)
    meta = jnp.transpose(q_meta, (1, 0, 2)).reshape(N_META, D_MODEL)
    p = [{"conv_w": conv_whole[l], "sinks": w["sinks"][l]} for l in range(DEPTH)]
    for l in range(DEPTH):
        for n in ("mix_pre_g", "attn_out_g", "conv_out_g", "mix_post_g", "mlp_pre_g", "mlp_post_g"):
            p[l][n] = w[n][l][None, :]

    h = jnp.concatenate([jnp.zeros((LEAD_PAD, D_MODEL), F32), meta, x[0]], axis=0)
    p[0]["w_in"] = whole_in(q_in)
    mixed = _mixer_fwd(h, p[0], tl)
    second = _Exchange("gather_second", [zones["w_in"][1], zones["w_out"][1]], _gather_plan(2), 6, [mixed[-1]])
    second_mlp = _Exchange("gather_second_mlp", [zones["w_up"][1], zones["w_down"][1]], _gather_plan(2), 6,
                           [second.token])
    p[0]["w_out"], p[0]["w_up"], p[0]["w_down"] = rest.wait(second_mlp.token)
    h1, saved0 = _out_fwd(mixed, p[0], tl)
    h, saved0 = _mlp_fwd(h1, saved0, p[0], tl)
    q_in, p[1]["w_out"] = second.wait(h)
    p[1]["w_in"] = whole_in(q_in)
    h1, saved1 = _out_fwd(_mixer_fwd(h, p[1], tl), p[1], tl)
    p[1]["w_up"], p[1]["w_down"] = second_mlp.wait(h1)
    h, saved1 = _mlp_fwd(h1, saved1, p[1], tl)
    loss_tile, dh = _loss_head(h, loss_target[0], tl.tm)
    loss = lax.psum(loss_tile[0, 0], ("x", "y", "c"))

    def adamw(layer, halves, other):
        return {n: _adamw_large(layer, w[n], halves[n], m[n], v[n], None if other is None else other[n])
                for n in halves}

    dh1, g1 = _mlp_part_bwd(dh, saved1, p[1], tl)
    carry, gm = _mix_out_part_bwd(dh1, saved1, p[1], tl)
    dh, gi = _attn_in_part_bwd(carry, saved1, p[1], tl)
    g1.update(gm, **gi)
    red1 = _Reduce("reduce1", [g1[n] for n in _LARGE])
    dh1, g0 = _mlp_part_bwd(dh, saved0, p[0], tl, [red1.token])
    red1.scatter(dh1)
    red0a = _Reduce("reduce0a", [g0["w_up"], g0["w_down"]], [red1.token])
    carry, gm = _mix_out_part_bwd(dh1, saved0, p[0], tl, [red0a.token])
    red1.join(carry[1])
    red0a.scatter(red1.token)
    dh0, gi = _attn_in_part_bwd(carry, saved0, p[0], tl, [red0a.token])
    g0.update(gm, **gi)
    red0b = _Reduce("reduce0b", [g0["w_in"], g0["w_out"]])
    done1 = adamw(1, dict(zip(_LARGE, red1.done(red0b.token))), None)
    red0a.join(done1["w_down"][0])
    red0b.scatter(red0a.token)
    halves0 = dict(zip(("w_up", "w_down"), red0a.done(red0b.token)))
    done0 = adamw(0, halves0, done1)
    red0b.join(done0["w_down"][0])
    done0.update(adamw(0, dict(zip(("w_in", "w_out"), red0b.done(red0b.token))), done1))
    grad_x = dh0[BLOCK:][None]
    grads = {n: [g0[n], g1[n]] for n in g0 if n not in _LARGE}

    rows = [dh0[LEAD_PAD:BLOCK]]
    for n in ("mix_pre_g", "mix_post_g", "mlp_pre_g", "mlp_post_g"):
        rows += grads[n]
    rows += [jnp.concatenate([grads["attn_out_g"][l], grads["conv_out_g"][l]], axis=1) for l in range(DEPTH)]
    rows.append(jnp.concatenate(grads["conv_w"], axis=1))
    rows.append(_pad_cols(jnp.concatenate(grads["sinks"])[None, :]))
    packed = jnp.concatenate(rows, axis=0)
    packed = jnp.pad(packed, ((0, SMALL_ROWS - packed.shape[0]), (0, 0)))
    total = _sum_devices(packed)
    r0 = N_META
    small = {
        "meta_tokens": lax.dynamic_slice(total[:N_META], (0, chip * (D_MODEL // N_CHIPS)), (N_META, D_MODEL // N_CHIPS)),
        "mix_pre_g": total[r0:r0 + 2], "mix_post_g": total[r0 + 2:r0 + 4], "mlp_pre_g": total[r0 + 4:r0 + 6],
        "mlp_post_g": total[r0 + 6:r0 + 8],
        "attn_out_g": total[r0 + 8:r0 + 10, :ATTN_W], "conv_out_g": total[r0 + 8:r0 + 10, ATTN_W:],
        "conv_w": lax.dynamic_slice(total[r0 + 10:r0 + 13].reshape(CONV_K, DEPTH, CONV_W).transpose(1, 0, 2),
                                    (0, 0, chip * (CONV_W // N_CHIPS)), (DEPTH, CONV_K, CONV_W // N_CHIPS)),
        "sinks": total[r0 + 13, :DEPTH * N_Q_HEADS].reshape(DEPTH, N_Q_HEADS),
    }

    grad, delta, new_m, new_v = {}, {}, {}, {}
    for n in _LARGE:
        grad[n], delta[n], new_m[n], new_v[n] = done0[n]
    ds, nms, nvs = _adamw_small([w[n] for n in _SMALL], [small[n] for n in _SMALL], [m[n] for n in _SMALL],
                                [v[n] for n in _SMALL])
    for i, n in enumerate(_SMALL):
        grad[n], delta[n], new_m[n], new_v[n] = small[n], ds[i], nms[i], nvs[i]
    return (loss, grad_x, *[grad[n] for n in _ORDER], *[delta[n] for n in _ORDER], *[new_m[n] for n in _ORDER],
            *[new_v[n] for n in _ORDER])
```

```python
import functools

import jax
import jax.numpy as jnp
from jax import lax
from jax.experimental import pallas as pl
from jax.experimental.pallas import tpu as pltpu

F32 = jnp.float32
BF16 = jnp.bfloat16

D_MODEL = 1024
DEPTH = 2
N_META = 16
ATTN_W = 512
CONV_W = 512
HEAD_DIM = 64
N_Q_HEADS = 8
N_KV_HEADS = 2
GROUP = N_Q_HEADS // N_KV_HEADS
KV_W = N_KV_HEADS * HEAD_DIM
CONV_K = 3
BLOCK = 128
LEAD_PAD = BLOCK - N_META
ROPE_THETA = 500000.0
ROT_DIM = HEAD_DIM // 4
ROT_HALF = ROT_DIM // 2
D_FF = 4 * D_MODEL
IN_W = ATTN_W + 2 * KV_W + 3 * CONV_W
QKV_W = ATTN_W + 2 * KV_W
EPS = 1e-6
SCALE = HEAD_DIM ** -0.5
FF_CHUNK = 1024
N_CHIPS = 4
N_DEV = 8

ADAM_LR = 0.001
ADAM_B1 = 0.9
ADAM_B2 = 0.999
ADAM_EPS = 1e-08
ADAM_WD = 0.01
ADAM_STEP = 10

V7X_VMEM_LIMIT = 56 * 1024 * 1024
SMALL_ROWS = 32

MESH = pl.DeviceIdType.MESH


def _params(*sem):
    return pltpu.CompilerParams(dimension_semantics=sem, vmem_limit_bytes=V7X_VMEM_LIMIT)


def _row_tile(t, most):
    nb = t // BLOCK
    for b in range(most // BLOCK, 0, -1):
        if nb % b == 0:
            return b * BLOCK
    return BLOCK


def _behind(body, deps):
    n = len(deps)

    def wrapped(*refs):
        body(*refs[n:])

    return wrapped, [pl.BlockSpec(memory_space=pl.ANY)] * n


def _rms(x, g):
    r = lax.rsqrt(jnp.mean(x * x, axis=-1, keepdims=True) + EPS)
    return x * r * g


def _rms_bwd(dy, x, g):
    r = lax.rsqrt(jnp.mean(x * x, axis=-1, keepdims=True) + EPS)
    xh = x * r
    dg = jnp.sum(dy * xh, axis=0, keepdims=True)
    dxh = dy * g
    dx = r * (dxh - xh * jnp.mean(dxh * xh, axis=-1, keepdims=True))
    return dx, dg


def _rope(x, cos, sa, sb):
    n = x.shape[-1]
    return x * cos + pltpu.roll(x, n - ROT_HALF, 1) * sa + pltpu.roll(x, ROT_HALF, 1) * sb


def _rope_bwd(dy, cos, sa, sb):
    n = dy.shape[-1]
    return dy * cos + pltpu.roll(dy * sa, ROT_HALF, 1) + pltpu.roll(dy * sb, n - ROT_HALF, 1)


def _rope_tables(t):
    pos = lax.broadcasted_iota(jnp.int32, (t, 2 * HEAD_DIM), 0).astype(F32) - LEAD_PAD
    dim = lax.broadcasted_iota(jnp.int32, (t, 2 * HEAD_DIM), 1) % HEAD_DIM
    pair = (dim % ROT_HALF).astype(F32)
    inv_freq = jnp.power(jnp.float32(ROPE_THETA), -(2.0 * pair) / ROT_DIM)
    ang = pos * inv_freq
    cos, sin = jnp.cos(ang), jnp.sin(ang)
    return (jnp.where(dim < ROT_DIM, cos, 1.0), jnp.where(dim < ROT_HALF, -sin, 0.0),
            jnp.where((dim >= ROT_HALF) & (dim < ROT_DIM), sin, 0.0))


def _in_proj(h, g, w, tabs, tm):
    t = h.shape[0]

    def body(h_ref, g_ref, w_ref, c_ref, sa_ref, sb_ref, a_ref, q_ref, k_ref, v_ref, b_ref, cg_ref, hc_ref):
        a = _rms(h_ref[...], g_ref[...]).astype(BF16)
        a_ref[...] = a
        p = jnp.dot(a, w_ref[...], preferred_element_type=F32)
        cos, sa, sb = c_ref[...], sa_ref[...], sb_ref[...]
        rep = ATTN_W // (2 * HEAD_DIM)
        q = _rope(p[:, :ATTN_W], jnp.tile(cos, (1, rep)), jnp.tile(sa, (1, rep)), jnp.tile(sb, (1, rep)))
        q_ref[...] = (q * SCALE).astype(BF16)
        k_ref[...] = _rope(p[:, ATTN_W:ATTN_W + KV_W], cos, sa, sb).astype(BF16)
        v_ref[...] = p[:, ATTN_W + KV_W:QKV_W].astype(BF16)
        b_ref[...] = p[:, QKV_W:QKV_W + CONV_W]
        cg_ref[...] = p[:, QKV_W + CONV_W:QKV_W + 2 * CONV_W]
        hc_ref[...] = p[:, QKV_W + 2 * CONV_W:]

    row = lambda n: pl.BlockSpec((tm, n), lambda i: (i, 0))
    full = lambda a: pl.BlockSpec(a.shape, lambda i: (0, 0))
    return pl.pallas_call(
        body, name="in_proj", grid=(t // tm,),
        in_specs=[row(D_MODEL), full(g), full(w), row(2 * HEAD_DIM), row(2 * HEAD_DIM), row(2 * HEAD_DIM)],
        out_specs=[row(D_MODEL), row(ATTN_W), row(KV_W), row(KV_W), row(CONV_W), row(CONV_W), row(CONV_W)],
        out_shape=[jax.ShapeDtypeStruct((t, D_MODEL), BF16), jax.ShapeDtypeStruct((t, ATTN_W), BF16),
                   jax.ShapeDtypeStruct((t, KV_W), BF16), jax.ShapeDtypeStruct((t, KV_W), BF16),
                   jax.ShapeDtypeStruct((t, CONV_W), F32), jax.ShapeDtypeStruct((t, CONV_W), F32),
                   jax.ShapeDtypeStruct((t, CONV_W), F32)],
        compiler_params=_params("parallel"),
    )(h, g, w, *tabs)


def _attn_bias():
    r = lax.broadcasted_iota(jnp.int32, (3, BLOCK, 2 * BLOCK), 1)
    c = lax.broadcasted_iota(jnp.int32, (3, BLOCK, 2 * BLOCK), 2)
    i = lax.broadcasted_iota(jnp.int32, (3, BLOCK, 2 * BLOCK), 0)
    ok = (c > r) & (c <= r + BLOCK) & (c + (i - 1) * BLOCK >= LEAD_PAD)
    return jnp.where(ok, 0.0, -jnp.inf).astype(F32)


def _attn_probs(qh, kg, sk, bias):
    s = lax.dot_general(qh, kg, (((1,), (1,)), ((), ())), preferred_element_type=F32) + bias
    m = jnp.maximum(jnp.max(s, axis=-1, keepdims=True), sk)
    e = jnp.exp(s - m)
    es = jnp.exp(sk - m)
    rden = 1.0 / (jnp.sum(e, axis=-1, keepdims=True) + es)
    return e * rden, es * rden


def _head(hh):
    return slice(hh * HEAD_DIM, (hh + 1) * HEAD_DIM)


def _two_blocks(ref, i):
    prev = jnp.maximum(i - 1, 0)
    return jnp.concatenate([ref[pl.ds(pl.multiple_of(prev * BLOCK, BLOCK), BLOCK), :],
                            ref[pl.ds(pl.multiple_of(i * BLOCK, BLOCK), BLOCK), :]], axis=0)


def _attn_fwd(q, k, v, bias, sinks, tm):
    t = q.shape[0]
    per_step = tm // BLOCK

    def body(s_ref, q_ref, k_ref, v_ref, bias_ref, o_ref):
        for b in range(per_step):
            i = pl.program_id(0) * per_step + b
            rows = slice(b * BLOCK, (b + 1) * BLOCK)
            kc, vc = _two_blocks(k_ref, i), _two_blocks(v_ref, i)
            bias_i = bias_ref[jnp.minimum(i, 2)]
            for hh in range(N_Q_HEADS):
                g = hh // GROUP
                p, _ = _attn_probs(q_ref[rows, _head(hh)], kc[:, _head(g)], s_ref[hh], bias_i)
                o_ref[rows, _head(hh)] = jnp.dot(p.astype(BF16), vc[:, _head(g)], preferred_element_type=F32)

    whole = pl.BlockSpec((t, KV_W), lambda i: (0, 0))
    return pl.pallas_call(
        body, name="attn_fwd", grid=(t // tm,),
        in_specs=[pl.BlockSpec(memory_space=pltpu.SMEM), pl.BlockSpec((tm, ATTN_W), lambda i: (i, 0)), whole, whole,
                  pl.BlockSpec(bias.shape, lambda i: (0, 0, 0))],
        out_specs=pl.BlockSpec((tm, ATTN_W), lambda i: (i, 0)),
        out_shape=jax.ShapeDtypeStruct((t, ATTN_W), F32),
        compiler_params=_params("parallel"),
    )(sinks, q, k, v, bias)


def _shift_rows(u, halo, n):
    r = pltpu.roll(u, n, 0)
    hr = pltpu.roll(halo, n, 0)
    idx = lax.broadcasted_iota(jnp.int32, hr.shape, 0)
    return jnp.concatenate([jnp.where(idx < n, hr, r[:8]), r[8:]], axis=0)


def _advance_rows(u, halo, n):
    rows = u.shape[0]
    r = pltpu.roll(u, rows - n, 0)
    hr = pltpu.roll(halo, 8 - n, 0)
    idx = lax.broadcasted_iota(jnp.int32, hr.shape, 0)
    return jnp.concatenate([r[:rows - 8], jnp.where(idx >= 8 - n, hr, r[rows - 8:])], axis=0)


def _mix_out(h, o, b, c, hc, cw, ga, gc, w, gp, tm, deps=()):
    t = h.shape[0]

    def body(h_ref, o_ref, b_ref, c_ref, hc_ref, cw_ref, ga_ref, gc_ref, w_ref, gp_ref, h1_ref, y_ref, z_ref, halo):
        @pl.when(pl.program_id(0) == 0)
        def _():
            halo[...] = jnp.zeros_like(halo)

        u = c_ref[...] * hc_ref[...]
        cv = cw_ref[0:1, :] * _shift_rows(u, halo[...], 2) + cw_ref[1:2, :] * _shift_rows(u, halo[...], 1) \
            + cw_ref[2:3, :] * u
        halo[...] = u[tm - 8:]
        yc = b_ref[...] * cv
        y = jnp.concatenate([_rms(o_ref[...], ga_ref[...]), _rms(yc, gc_ref[...])], axis=1).astype(BF16)
        y_ref[...] = y
        z = jnp.dot(y, w_ref[...].reshape(D_MODEL, D_MODEL), preferred_element_type=F32)
        z_ref[...] = z
        h1_ref[...] = h_ref[...] + _rms(z, gp_ref[...])

    row = lambda n: pl.BlockSpec((tm, n), lambda i: (i, 0))
    full = lambda a: pl.BlockSpec(a.shape, lambda i: (0,) * a.ndim)
    body, dep_specs = _behind(body, deps)
    return pl.pallas_call(
        body, name="mix_out", grid=(t // tm,),
        in_specs=dep_specs + [row(D_MODEL), row(ATTN_W), row(CONV_W), row(CONV_W), row(CONV_W), full(cw), full(ga),
                              full(gc), full(w), full(gp)],
        out_specs=[row(D_MODEL), row(D_MODEL), row(D_MODEL)],
        out_shape=[jax.ShapeDtypeStruct((t, D_MODEL), F32), jax.ShapeDtypeStruct((t, D_MODEL), BF16),
                   jax.ShapeDtypeStruct((t, D_MODEL), F32)],
        scratch_shapes=[pltpu.VMEM((8, CONV_W), F32)],
        compiler_params=_params("arbitrary"),
    )(*deps, h, o, b, c, hc, cw, ga, gc, w, gp)


def _mlp(h1, g1, wu, wd, g2, tm):
    t = h1.shape[0]
    nj = D_FF // FF_CHUNK

    def body(h1_ref, g1_ref, wu_ref, wd_ref, g2_ref, h2_ref, a2_ref, act_ref, f_ref, acc):
        j = pl.program_id(1)

        @pl.when(j == 0)
        def _():
            a2_ref[...] = _rms(h1_ref[...], g1_ref[...]).astype(BF16)

        up = jnp.dot(a2_ref[...], wu_ref[...], preferred_element_type=F32)
        act = jnp.square(jnp.maximum(up, 0.0)).astype(BF16)
        act_ref[...] = act
        part = jnp.dot(act, wd_ref[...], preferred_element_type=F32)

        @pl.when(j == 0)
        def _():
            acc[...] = part

        @pl.when(j > 0)
        def _():
            acc[...] += part

        @pl.when(j == nj - 1)
        def _():
            f = acc[...]
            f_ref[...] = f
            h2_ref[...] = h1_ref[...] + _rms(f, g2_ref[...])

    row = pl.BlockSpec((tm, D_MODEL), lambda i, j: (i, 0))
    vec = pl.BlockSpec((1, D_MODEL), lambda i, j: (0, 0))
    quarter = pl.BlockSpec((None, D_MODEL, FF_CHUNK), lambda i, j: (j, 0, 0))
    return pl.pallas_call(
        body, name="mlp", grid=(t // tm, nj),
        in_specs=[row, vec, quarter, quarter, vec],
        out_specs=[row, row, pl.BlockSpec((tm, FF_CHUNK), lambda i, j: (i, j)), row],
        out_shape=[jax.ShapeDtypeStruct((t, D_MODEL), F32), jax.ShapeDtypeStruct((t, D_MODEL), BF16),
                   jax.ShapeDtypeStruct((t, D_FF), BF16), jax.ShapeDtypeStruct((t, D_MODEL), F32)],
        scratch_shapes=[pltpu.VMEM((tm, D_MODEL), F32)],
        compiler_params=_params("parallel", "arbitrary"),
    )(h1, g1, wu, wd, g2)


def _loss_head(h, target, tm):
    t = h.shape[0]
    per_step = tm // BLOCK

    def body(h_ref, *rest):
        t_refs, (loss_ref, dh_ref) = rest[:per_step], rest[per_step:]
        i = pl.program_id(0)

        @pl.when(i == 0)
        def _():
            loss_ref[...] = jnp.zeros_like(loss_ref)

        total = jnp.zeros((), F32)
        for b in range(per_step):
            rows = slice(b * BLOCK, (b + 1) * BLOCK)
            err = h_ref[rows, :] - t_refs[b][...]
            if b == 0:
                err = jnp.where(i == 0, 0.0, err)
            dh_ref[rows, :] = err * (1.0 / D_MODEL)
            total = total + jnp.sum(err * err)
        loss_ref[...] += total * (0.5 / D_MODEL)

    def target_block(b):
        return pl.BlockSpec((BLOCK, D_MODEL), lambda i: (jnp.maximum(i * per_step + b - 1, 0), 0))

    return pl.pallas_call(
        body, name="loss_head", grid=(t // tm,),
        in_specs=[pl.BlockSpec((tm, D_MODEL), lambda i: (i, 0))] + [target_block(b) for b in range(per_step)],
        out_specs=[pl.BlockSpec((8, 128), lambda i: (0, 0)), pl.BlockSpec((tm, D_MODEL), lambda i: (i, 0))],
        out_shape=[jax.ShapeDtypeStruct((8, 128), F32), jax.ShapeDtypeStruct((t, D_MODEL), F32)],
        compiler_params=_params("arbitrary"),
    )(h, *([target] * per_step))


def _mlp_bwd(dh2, f, g2, act, wd, wu, h1, g1, tm, deps=()):
    t = dh2.shape[0]
    nj = D_FF // FF_CHUNK

    def body(dh2_ref, f_ref, g2_ref, act_ref, wd_ref, wu_ref, h1_ref, g1_ref, dh1_ref, df_ref, dup_ref, dg2_ref,
             dg1_ref, acc):
        i, j = pl.program_id(0), pl.program_id(1)

        @pl.when((i == 0) & (j == 0))
        def _():
            dg2_ref[...] = jnp.zeros_like(dg2_ref)
            dg1_ref[...] = jnp.zeros_like(dg1_ref)

        @pl.when(j == 0)
        def _():
            df, dg = _rms_bwd(dh2_ref[...], f_ref[...], g2_ref[...])
            df_ref[...] = df.astype(BF16)
            dg2_ref[...] += dg

        dact = lax.dot_general(df_ref[...], wd_ref[...], (((1,), (1,)), ((), ())), preferred_element_type=F32)
        dup = (dact * (2.0 * jnp.sqrt(act_ref[...].astype(F32)))).astype(BF16)
        dup_ref[...] = dup
        part = lax.dot_general(dup, wu_ref[...], (((1,), (1,)), ((), ())), preferred_element_type=F32)

        @pl.when(j == 0)
        def _():
            acc[...] = part

        @pl.when(j > 0)
        def _():
            acc[...] += part

        @pl.when(j == nj - 1)
        def _():
            dx, dg = _rms_bwd(acc[...], h1_ref[...], g1_ref[...])
            dh1_ref[...] = dh2_ref[...] + dx
            dg1_ref[...] += dg

    row = pl.BlockSpec((tm, D_MODEL), lambda i, j: (i, 0))
    vec = pl.BlockSpec((1, D_MODEL), lambda i, j: (0, 0))
    chunk = pl.BlockSpec((tm, FF_CHUNK), lambda i, j: (i, j))
    quarter = pl.BlockSpec((None, D_MODEL, FF_CHUNK), lambda i, j: (j, 0, 0))
    body, dep_specs = _behind(body, deps)
    return pl.pallas_call(
        body, name="mlp_bwd", grid=(t // tm, nj),
        in_specs=dep_specs + [row, row, vec, chunk, quarter, quarter, row, vec],
        out_specs=[row, row, chunk, vec, vec],
        out_shape=[jax.ShapeDtypeStruct((t, D_MODEL), F32), jax.ShapeDtypeStruct((t, D_MODEL), BF16),
                   jax.ShapeDtypeStruct((t, D_FF), BF16), jax.ShapeDtypeStruct((1, D_MODEL), F32),
                   jax.ShapeDtypeStruct((1, D_MODEL), F32)],
        scratch_shapes=[pltpu.VMEM((tm, D_MODEL), F32)],
        compiler_params=_params("arbitrary", "arbitrary"),
    )(*deps, dh2, f, g2, act, wd, wu, h1, g1)


def _weight_grad(x, y, tm, name):
    t, k = x.shape
    n = y.shape[1]
    tk = tn = FF_CHUNK

    def body(x_ref, y_ref, o_ref):
        @pl.when(pl.program_id(2) == 0)
        def _():
            o_ref[...] = jnp.zeros_like(o_ref)

        o_ref[...] += lax.dot_general(x_ref[...], y_ref[...], (((0,), (0,)), ((), ())), preferred_element_type=F32)

    return pl.pallas_call(
        body, name=name, grid=(k // tk, n // tn, t // tm),
        in_specs=[pl.BlockSpec((tm, tk), lambda a, b, r: (r, a)), pl.BlockSpec((tm, tn), lambda a, b, r: (r, b))],
        out_specs=pl.BlockSpec((None, None, tk, tn), lambda a, b, r: (a, b, 0, 0)),
        out_shape=jax.ShapeDtypeStruct((k // tk, n // tn, tk, tn), F32),
        compiler_params=_params("parallel", "parallel", "arbitrary"),
    )(x, y)


def _weight_grad_in(a, dproj, tm):
    t = a.shape[0]
    nt = t // tm
    qw = IN_W // N_CHIPS

    def body(a_ref, d_ref, o_ref, acc):
        r = pl.program_id(0)

        @pl.when(r == 0)
        def _():
            acc[...] = jnp.zeros_like(acc)

        acc[...] += lax.dot_general(a_ref[...], d_ref[...], (((0,), (0,)), ((), ())), preferred_element_type=F32)

        @pl.when(r == nt - 1)
        def _():
            for s in range(N_CHIPS):
                o_ref[s] = acc[:, s * qw:(s + 1) * qw]

    return pl.pallas_call(
        body, name="grad_w_in", grid=(nt,),
        in_specs=[pl.BlockSpec((tm, D_MODEL), lambda r: (r, 0)), pl.BlockSpec((tm, IN_W), lambda r: (r, 0))],
        out_specs=pl.BlockSpec(memory_space=pltpu.VMEM),
        out_shape=jax.ShapeDtypeStruct((N_CHIPS, D_MODEL, qw), F32),
        scratch_shapes=[pltpu.VMEM((D_MODEL, IN_W), F32)],
        compiler_params=_params("arbitrary"),
    )(a, dproj)


def _mix_out_bwd(dh1, z, gp, w, o, b, c, hc, cw, ga, gc, tm, deps=()):
    t = dh1.shape[0]
    nt = t // tm
    per8 = tm // 8

    def body(dh1_ref, z_ref, gp_ref, w_ref, o_ref, b_ref, c_ref, hc_ref, cp_ref, hp_ref, cw_ref, ga_ref, gc_ref,
             dz_ref, do_ref, dbch_ref, dgp_ref, dga_ref, dgc_ref, dcw_ref, halo):
        i = pl.program_id(0)

        @pl.when(i == 0)
        def _():
            halo[...] = jnp.zeros_like(halo)
            dgp_ref[...] = jnp.zeros_like(dgp_ref)
            dga_ref[...] = jnp.zeros_like(dga_ref)
            dgc_ref[...] = jnp.zeros_like(dgc_ref)
            dcw_ref[...] = jnp.zeros_like(dcw_ref)

        dz, dgp = _rms_bwd(dh1_ref[...], z_ref[...], gp_ref[...])
        dgp_ref[...] += dgp
        dz = dz.astype(BF16)
        dz_ref[...] = dz
        dy = lax.dot_general(dz, w_ref[...].reshape(D_MODEL, D_MODEL), (((1,), (1,)), ((), ())),
                             preferred_element_type=F32)
        do, dga = _rms_bwd(dy[:, :ATTN_W], o_ref[...], ga_ref[...])
        do_ref[...] = do
        dga_ref[...] += dga

        u = c_ref[...] * hc_ref[...]
        first = i == nt - 1
        u_before = jnp.where(first, 0.0, cp_ref[...] * hp_ref[...])
        u1 = _shift_rows(u, u_before, 1)
        u2 = _shift_rows(u, u_before, 2)
        cv = cw_ref[0:1, :] * u2 + cw_ref[1:2, :] * u1 + cw_ref[2:3, :] * u
        bb = b_ref[...]
        dyc, dgc = _rms_bwd(dy[:, ATTN_W:], bb * cv, gc_ref[...])
        dgc_ref[...] += dgc
        dcv = dyc * bb
        d1 = _advance_rows(dcv, halo[...], 1)
        d2 = _advance_rows(dcv, halo[...], 2)
        halo[...] = dcv[:8]
        du = cw_ref[2:3, :] * dcv + cw_ref[1:2, :] * d1 + cw_ref[0:1, :] * d2
        dbch_ref[...] = jnp.concatenate([dyc * cv, du * hc_ref[...], du * c_ref[...]], axis=1).astype(BF16)
        dcw_ref[...] += jnp.concatenate([jnp.sum(dcv * u2, axis=0, keepdims=True),
                                         jnp.sum(dcv * u1, axis=0, keepdims=True),
                                         jnp.sum(dcv * u, axis=0, keepdims=True)], axis=0)

    row = lambda n: pl.BlockSpec((tm, n), lambda i: (nt - 1 - i, 0))
    before = pl.BlockSpec((8, CONV_W), lambda i: (jnp.maximum((nt - 1 - i) * per8 - 1, 0), 0))
    full = lambda a: pl.BlockSpec(a.shape, lambda i: (0,) * a.ndim)
    vec = lambda n: pl.BlockSpec((1, n), lambda i: (0, 0))
    body, dep_specs = _behind(body, deps)
    return pl.pallas_call(
        body, name="mix_out_bwd", grid=(nt,),
        in_specs=dep_specs + [row(D_MODEL), row(D_MODEL), full(gp), full(w), row(ATTN_W), row(CONV_W), row(CONV_W),
                              row(CONV_W), before, before, full(cw), full(ga), full(gc)],
        out_specs=[row(D_MODEL), row(ATTN_W), row(3 * CONV_W), vec(D_MODEL), vec(ATTN_W), vec(CONV_W),
                   pl.BlockSpec((CONV_K, CONV_W), lambda i: (0, 0))],
        out_shape=[jax.ShapeDtypeStruct((t, D_MODEL), BF16), jax.ShapeDtypeStruct((t, ATTN_W), F32),
                   jax.ShapeDtypeStruct((t, 3 * CONV_W), BF16), jax.ShapeDtypeStruct((1, D_MODEL), F32),
                   jax.ShapeDtypeStruct((1, ATTN_W), F32), jax.ShapeDtypeStruct((1, CONV_W), F32),
                   jax.ShapeDtypeStruct((CONV_K, CONV_W), F32)],
        scratch_shapes=[pltpu.VMEM((8, CONV_W), F32)],
        compiler_params=_params("arbitrary"),
    )(*deps, dh1, z, gp, w, o, b, c, hc, c, hc, cw, ga, gc)


def _attn_bwd(q, k, v, o, do, bias, sinks, tm, deps=()):
    t = q.shape[0]
    per_step = tm // BLOCK

    def body(s_ref, q_ref, k_ref, v_ref, o_ref, do_ref, bias_ref, dq_ref, dk_ref, dv_ref, ds_ref):
        step = pl.program_id(0)

        @pl.when(step == 0)
        def _():
            ds_ref[...] = jnp.zeros_like(ds_ref)

        dsink = [jnp.zeros((BLOCK, 1), F32) for _ in range(N_Q_HEADS)]
        ahead = None
        for b in range(per_step):
            i = step * per_step + b
            rows = slice(b * BLOCK, (b + 1) * BLOCK)
            kc, vc = _two_blocks(k_ref, i), _two_blocks(v_ref, i)
            bias_i = bias_ref[jnp.minimum(i, 2)]
            dkg, dvg = [], []
            for g in range(N_KV_HEADS):
                kg, vg = kc[:, _head(g)], vc[:, _head(g)]
                qs, ps_, dss, dos = [], [], [], []
                for hh in range(GROUP * g, GROUP * (g + 1)):
                    qh = q_ref[rows, _head(hh)]
                    p, ps = _attn_probs(qh, kg, s_ref[hh], bias_i)
                    doh = do_ref[rows, _head(hh)]
                    dohb = doh.astype(BF16)
                    dp = lax.dot_general(dohb, vg, (((1,), (1,)), ((), ())), preferred_element_type=F32)
                    drow = jnp.sum(doh * o_ref[rows, _head(hh)], axis=-1, keepdims=True)
                    ds = (p * (dp - drow)).astype(BF16)
                    dsink[hh] = dsink[hh] + ps * drow
                    dq_ref[rows, _head(hh)] = jnp.dot(ds, kg, preferred_element_type=F32) * SCALE
                    qs.append(qh)
                    ps_.append(p.astype(BF16))
                    dss.append(ds)
                    dos.append(dohb)
                dkg.append(lax.dot_general(jnp.concatenate(dss, axis=0), jnp.concatenate(qs, axis=0),
                                           (((0,), (0,)), ((), ())), preferred_element_type=F32))
                dvg.append(lax.dot_general(jnp.concatenate(ps_, axis=0), jnp.concatenate(dos, axis=0),
                                           (((0,), (0,)), ((), ())), preferred_element_type=F32))
            dkb, dvb = jnp.concatenate(dkg, axis=1), jnp.concatenate(dvg, axis=1)
            if b == 0:
                @pl.when(step > 0)
                def _():
                    before = pl.ds(pl.multiple_of((i - 1) * BLOCK, BLOCK), BLOCK)
                    dk_ref[before, :] += dkb[:BLOCK]
                    dv_ref[before, :] += dvb[:BLOCK]
            else:
                at = pl.ds(pl.multiple_of((i - 1) * BLOCK, BLOCK), BLOCK)
                dk_ref[at, :] = ahead[0] + dkb[:BLOCK]
                dv_ref[at, :] = ahead[1] + dvb[:BLOCK]
            ahead = (dkb[BLOCK:], dvb[BLOCK:])
        last = pl.ds(pl.multiple_of(((step + 1) * per_step - 1) * BLOCK, BLOCK), BLOCK)
        dk_ref[last, :] = ahead[0]
        dv_ref[last, :] = ahead[1]
        for hh in range(N_Q_HEADS):
            ds_ref[hh:hh + 1, :] -= jnp.sum(dsink[hh])

    whole = pl.BlockSpec((t, KV_W), lambda i: (0, 0))
    blk = pl.BlockSpec((tm, ATTN_W), lambda i: (i, 0))
    body, dep_specs = _behind(body, deps)
    return pl.pallas_call(
        body, name="attn_bwd", grid=(t // tm,),
        in_specs=dep_specs + [pl.BlockSpec(memory_space=pltpu.SMEM), blk, whole, whole, blk, blk,
                              pl.BlockSpec(bias.shape, lambda i: (0, 0, 0))],
        out_specs=[blk, whole, whole, pl.BlockSpec((N_Q_HEADS, 128), lambda i: (0, 0))],
        out_shape=[jax.ShapeDtypeStruct((t, ATTN_W), F32), jax.ShapeDtypeStruct((t, KV_W), F32),
                   jax.ShapeDtypeStruct((t, KV_W), F32), jax.ShapeDtypeStruct((N_Q_HEADS, 128), F32)],
        compiler_params=_params("arbitrary"),
    )(*deps, sinks, q, k, v, o, do, bias)


def _in_proj_bwd(dq, dk, dv, dbch, w, dh1, h, g, tabs, tm):
    t = h.shape[0]

    def body(dq_ref, dk_ref, dv_ref, dbch_ref, w_ref, dh1_ref, h_ref, g_ref, c_ref, sa_ref, sb_ref, dh_ref, dp_ref,
             dg_ref):
        @pl.when(pl.program_id(0) == 0)
        def _():
            dg_ref[...] = jnp.zeros_like(dg_ref)

        cos, sa, sb = c_ref[...], sa_ref[...], sb_ref[...]
        rep = ATTN_W // (2 * HEAD_DIM)
        dqr = _rope_bwd(dq_ref[...], jnp.tile(cos, (1, rep)), jnp.tile(sa, (1, rep)), jnp.tile(sb, (1, rep)))
        dkr = _rope_bwd(dk_ref[...], cos, sa, sb)
        dp = jnp.concatenate([dqr.astype(BF16), dkr.astype(BF16), dv_ref[...].astype(BF16), dbch_ref[...]], axis=1)
        dp_ref[...] = dp
        da = lax.dot_general(dp, w_ref[...], (((1,), (1,)), ((), ())), preferred_element_type=F32)
        dx, dg = _rms_bwd(da, h_ref[...], g_ref[...])
        dh_ref[...] = dh1_ref[...] + dx
        dg_ref[...] += dg

    row = lambda n: pl.BlockSpec((tm, n), lambda i: (i, 0))
    full = lambda a: pl.BlockSpec(a.shape, lambda i: (0, 0))
    return pl.pallas_call(
        body, name="in_proj_bwd", grid=(t // tm,),
        in_specs=[row(ATTN_W), row(KV_W), row(KV_W), row(3 * CONV_W), full(w), row(D_MODEL), row(D_MODEL), full(g),
                  row(2 * HEAD_DIM), row(2 * HEAD_DIM), row(2 * HEAD_DIM)],
        out_specs=[row(D_MODEL), row(IN_W), pl.BlockSpec((1, D_MODEL), lambda i: (0, 0))],
        out_shape=[jax.ShapeDtypeStruct((t, D_MODEL), F32), jax.ShapeDtypeStruct((t, IN_W), BF16),
                   jax.ShapeDtypeStruct((1, D_MODEL), F32)],
        compiler_params=_params("arbitrary"),
    )(dq, dk, dv, dbch, w, dh1, h, g, *tabs)


class _Tiles:
    def __init__(self, t):
        self.tm = _row_tile(t, 640)
        self.ts = _row_tile(t, 320)
        self.tabs = _rope_tables(t)
        self.bias = _attn_bias()


def _mixer_fwd(h, p, tl):
    a, q, k, v, b, c, hc = _in_proj(h, p["mix_pre_g"], p["w_in"], tl.tabs, tl.ts)
    o = _attn_fwd(q, k, v, tl.bias, p["sinks"], tl.tm)
    return (h, a, q, k, v, b, c, hc, o)


def _out_fwd(mixed, p, tl, deps=()):
    h, a, q, k, v, b, c, hc, o = mixed
    h1, y, z = _mix_out(h, o, b, c, hc, p["conv_w"], p["attn_out_g"], p["conv_out_g"], p["w_out"], p["mix_post_g"],
                        tl.ts, deps)
    return h1, mixed + (h1, y, z)


def _mlp_fwd(h1, saved, p, tl):
    h2, a2, act, f = _mlp(h1, p["mlp_pre_g"], p["w_up"], p["w_down"], p["mlp_post_g"], tl.tm)
    return h2, saved + (a2, act, f)


def _mlp_part_bwd(dh, saved, p, tl, deps=()):
    h1, a2, act, f = saved[9], saved[12], saved[13], saved[14]
    dh1, df, dup, dg2, dg1 = _mlp_bwd(dh, f, p["mlp_post_g"], act, p["w_down"], p["w_up"], h1, p["mlp_pre_g"], tl.tm,
                                      deps)
    g = {"w_down": _weight_grad(act, df, tl.tm, "grad_w_down").reshape(N_CHIPS, FF_CHUNK, D_MODEL),
         "w_up": _weight_grad(a2, dup, tl.tm, "grad_w_up").reshape(N_CHIPS, D_MODEL, FF_CHUNK),
         "mlp_post_g": dg2, "mlp_pre_g": dg1}
    return dh1, g


def _mix_out_part_bwd(dh1, saved, p, tl, deps=()):
    b, c, hc, o, y, z = saved[5], saved[6], saved[7], saved[8], saved[10], saved[11]
    dz, do, dbch, dgp, dga, dgc, dcw = _mix_out_bwd(dh1, z, p["mix_post_g"], p["w_out"], o, b, c, hc, p["conv_w"],
                                                    p["attn_out_g"], p["conv_out_g"], tl.ts, deps)
    g = {"w_out": _weight_grad(y, dz, tl.tm, "grad_w_out").reshape(N_CHIPS, D_MODEL // N_CHIPS, D_MODEL),
         "mix_post_g": dgp, "attn_out_g": dga, "conv_out_g": dgc, "conv_w": dcw}
    return (dh1, do, dbch), g


def _attn_in_part_bwd(carry, saved, p, tl, deps=()):
    dh1, do, dbch = carry
    h_in, a, q, k, v, o = saved[0], saved[1], saved[2], saved[3], saved[4], saved[8]
    dq, dk, dv, dsink = _attn_bwd(q, k, v, o, do, tl.bias, p["sinks"], tl.tm, deps)
    dh, dproj, dgi = _in_proj_bwd(dq, dk, dv, dbch, p["w_in"], dh1, h_in, p["mix_pre_g"], tl.tabs, tl.ts)
    return dh, {"w_in": _weight_grad_in(a, dproj, tl.tm), "mix_pre_g": dgi, "sinks": dsink[:, 0]}


def _place():
    return lax.axis_index("x"), lax.axis_index("y"), lax.axis_index("c")


def _other_chips(x, y):
    return [(1 - x, y), (x, 1 - y), (1 - x, 1 - y)]


_HBM = pl.BlockSpec(memory_space=pltpu.HBM)
_SEM = pl.BlockSpec(memory_space=pltpu.SEMAPHORE)
_EFFECT = pltpu.SideEffectType.DATAFLOW_SIDE_EFFECTING


class _Exchange:
    def __init__(self, name, bufs, plan, n, after=()):
        self.name, self.plan, nb = name, plan, len(bufs)
        n_in = nb + len(after)

        def body(*refs):
            send, recv, token = refs[n_in], refs[n_in + 1], refs[-1]
            for k, (src, dst, target, _) in enumerate(plan(refs[:nb])):
                pltpu.make_async_remote_copy(src_ref=src, dst_ref=dst, send_sem=send.at[k], recv_sem=recv.at[k],
                                             device_id=target, device_id_type=MESH).start()
            token[...] = jnp.zeros_like(token)

        outs = pl.pallas_call(
            body, name=name + "_start",
            out_shape=(pltpu.SemaphoreType.DMA((n,)), pltpu.SemaphoreType.DMA((n,)),
                       *[pltpu.HBM(b.shape, b.dtype) for b in bufs], jax.ShapeDtypeStruct((8, 128), F32)),
            in_specs=[_HBM] * nb + [pl.BlockSpec(memory_space=pl.ANY)] * len(after),
            out_specs=(_SEM, _SEM, *[_HBM] * nb, pl.BlockSpec(memory_space=pltpu.VMEM)),
            input_output_aliases={i: 2 + i for i in range(nb)},
            compiler_params=pltpu.CompilerParams(has_side_effects=_EFFECT),
        )(*[pltpu.with_memory_space_constraint(b, pltpu.HBM) for b in bufs], *after)
        self.send, self.recv, self.bufs, self.token = outs[0], outs[1], list(outs[2:2 + nb]), outs[-1]

    def wait(self, after):
        plan, nb = self.plan, len(self.bufs)

        def body(*refs):
            send, recv = refs[nb], refs[nb + 1]
            for k, (src, _, target, land) in enumerate(plan(refs[:nb])):
                cp = pltpu.make_async_remote_copy(src_ref=src, dst_ref=land, send_sem=send.at[k], recv_sem=recv.at[k],
                                                  device_id=target, device_id_type=MESH)
                cp.wait_send()
                cp.wait_recv()

        outs = pl.pallas_call(
            body, name=self.name + "_wait", out_shape=[pltpu.HBM(b.shape, b.dtype) for b in self.bufs],
            in_specs=[_HBM] * nb + [_SEM, _SEM, pl.BlockSpec(memory_space=pl.ANY)], out_specs=[_HBM] * nb,
            input_output_aliases={i: i for i in range(nb)},
            compiler_params=pltpu.CompilerParams(has_side_effects=_EFFECT),
        )(*self.bufs, self.send, self.recv, after)
        return list(outs)


def _gather_plan(n):
    def plan(refs):
        x, y, c = _place()
        me = 2 * x + y
        return [(refs[a].at[me], refs[a].at[me], (px, py, c), refs[a].at[2 * px + py])
                for a in range(n) for px, py in _other_chips(x, y)]

    return plan


def _swap_plan(n, half_rows):
    def plan(refs):
        x, y, c = _place()
        out = []
        for a in range(n):
            hr = half_rows[a]
            out.append((refs[a].at[:, pl.ds((1 - c) * hr, hr)], refs[n + a], (x, y, 1 - c), refs[n + a]))
        return out

    return plan


def _scatter_plan(n):
    def plan(refs):
        x, y, c = _place()
        return [(refs[a].at[2 * px + py], refs[n + a].at[k], (px, py, c), refs[n + a].at[k])
                for a in range(n) for k, (px, py) in enumerate(_other_chips(x, y))]

    return plan


def _join_plan(n):
    def plan(refs):
        x, y, c = _place()
        return [(refs[a].at[c], refs[a].at[c], (x, y, 1 - c), refs[a].at[1 - c]) for a in range(n)]

    return plan


def _add_half(g, r):
    rows, cols = g.shape[1], g.shape[2]
    hr = rows // 2
    tr = min(hr, 256)
    per = hr // tr
    first = (lax.axis_index("c") * per).astype(jnp.int32).reshape(1)

    def body(first_ref, g_ref, r_ref, o_ref):
        o_ref[...] = (g_ref[...] + r_ref[...]).astype(BF16)

    blk = pl.BlockSpec((None, tr, cols), lambda s, i, first_ref: (s, i, 0))
    return pl.pallas_call(
        body, name="add_half",
        grid_spec=pltpu.PrefetchScalarGridSpec(
            num_scalar_prefetch=1, grid=(N_CHIPS, per),
            in_specs=[pl.BlockSpec((None, tr, cols), lambda s, i, first_ref: (s, i + first_ref[0], 0)), blk],
            out_specs=blk),
        out_shape=jax.ShapeDtypeStruct(r.shape, BF16),
        compiler_params=_params("parallel", "parallel"),
    )(first, g, r)


def _sum_chips(s, q):
    rows, cols = s.shape[1], s.shape[2]
    tr = min(rows, 256)
    x, y, c = _place()
    where = jnp.stack([2 * x + y, c]).astype(jnp.int32)

    def body(where_ref, s_ref, q_ref, o_ref):
        part = [q_ref[k].astype(F32) for k in range(N_CHIPS - 1)]
        o_ref[...] = ((s_ref[...].astype(F32) + part[0]) + part[1]) + part[2]

    return pl.pallas_call(
        body, name="sum_chips",
        grid_spec=pltpu.PrefetchScalarGridSpec(
            num_scalar_prefetch=1, grid=(rows // tr,),
            in_specs=[pl.BlockSpec((None, tr, cols), lambda i, where_ref: (where_ref[0], i, 0)),
                      pl.BlockSpec((N_CHIPS - 1, tr, cols), lambda i, where_ref: (0, i, 0))],
            out_specs=pl.BlockSpec((None, tr, cols), lambda i, where_ref: (where_ref[1], i, 0))),
        out_shape=jax.ShapeDtypeStruct((2, rows, cols), F32),
        compiler_params=_params("parallel"),
    )(where, s, q)


def _sum_devices(packed):
    def body(p_ref, o_ref, land, send_sems, recv_sems):
        x, y, c = _place()
        me = 4 * x + 2 * y + c
        land[me] = p_ref[...]
        sends = []
        for k in range(1, N_DEV):
            px, py, pc = x ^ (k >> 2), y ^ ((k >> 1) & 1), c ^ (k & 1)
            cp = pltpu.make_async_remote_copy(src_ref=p_ref, dst_ref=land.at[me], send_sem=send_sems.at[k - 1],
                                              recv_sem=recv_sems.at[k - 1], device_id=(px, py, pc), device_id_type=MESH)
            cp.start()
            sends.append(cp)
        for k in range(1, N_DEV):
            px, py, pc = x ^ (k >> 2), y ^ ((k >> 1) & 1), c ^ (k & 1)
            pltpu.make_async_remote_copy(src_ref=p_ref, dst_ref=land.at[4 * px + 2 * py + pc],
                                         send_sem=send_sems.at[k - 1], recv_sem=recv_sems.at[k - 1],
                                         device_id=(px, py, pc), device_id_type=MESH).wait_recv()
        for cp in sends:
            cp.wait_send()
        total = land[0]
        for d in range(1, N_DEV):
            total = total + land[d]
        o_ref[...] = total

    vm = pl.BlockSpec(memory_space=pltpu.VMEM)
    return pl.pallas_call(
        body, name="sum_devices", in_specs=[vm], out_specs=vm,
        out_shape=jax.ShapeDtypeStruct(packed.shape, F32),
        scratch_shapes=[pltpu.VMEM((N_DEV,) + packed.shape, F32), pltpu.SemaphoreType.DMA((N_DEV - 1,)),
                        pltpu.SemaphoreType.DMA((N_DEV - 1,))],
    )(packed)


def _adamw_math(w, g, m, v):
    m = ADAM_B1 * m + (1.0 - ADAM_B1) * g
    v = ADAM_B2 * v + (1.0 - ADAM_B2) * jnp.square(g)
    m_hat = m / (1.0 - ADAM_B1 ** ADAM_STEP)
    v_hat = v / (1.0 - ADAM_B2 ** ADAM_STEP)
    delta = -ADAM_LR * (m_hat / (jnp.sqrt(v_hat) + ADAM_EPS) + ADAM_WD * w)
    return delta, m, v


def _adamw_large(layer, w, halves, m, v, other):
    _, rows, cols = w.shape
    tr = min(rows // 2, 256)
    per = rows // 2 // tr

    def body(w_ref, g_ref, m_ref, v_ref, *rest):
        g_out, d_ref, nm_ref, nv_ref = rest[-4:]
        g = g_ref[...]
        g_out[...] = g
        d_ref[...], nm_ref[...], nv_ref[...] = _adamw_math(w_ref[...], g, m_ref[...], v_ref[...])

    blk = pl.BlockSpec((None, tr, cols), lambda i: (layer, i, 0))
    half = pl.BlockSpec((None, tr, cols), lambda i: (i // per, i % per, 0))
    kept = [] if other is None else list(other)
    return pl.pallas_call(
        body, name="adamw_large", grid=(rows // tr,),
        in_specs=[blk, half, blk, blk] + [pl.BlockSpec(memory_space=pl.ANY)] * len(kept), out_specs=[blk] * 4,
        out_shape=[jax.ShapeDtypeStruct(w.shape, F32)] * 4,
        input_output_aliases={4 + k: k for k in range(len(kept))},
        compiler_params=_params("parallel"),
    )(w, halves, m, v, *kept)


def _adamw_small(ws, gs, ms, vs):
    n = len(ws)

    def body(*refs):
        w_r, g_r, m_r, v_r = refs[:n], refs[n:2 * n], refs[2 * n:3 * n], refs[3 * n:4 * n]
        d_r, nm_r, nv_r = refs[4 * n:5 * n], refs[5 * n:6 * n], refs[6 * n:]
        for a in range(n):
            d_r[a][...], nm_r[a][...], nv_r[a][...] = _adamw_math(w_r[a][...], g_r[a][...], m_r[a][...], v_r[a][...])

    vm = pl.BlockSpec(memory_space=pltpu.VMEM)
    outs = pl.pallas_call(
        body, name="adamw_small", in_specs=[vm] * (4 * n), out_specs=[vm] * (3 * n),
        out_shape=[jax.ShapeDtypeStruct(w.shape, F32) for w in ws] * 3,
    )(*ws, *gs, *ms, *vs)
    return outs[:n], outs[n:2 * n], outs[2 * n:]


_LARGE = ("w_in", "w_out", "w_up", "w_down")
_SMALL = ("meta_tokens", "mix_pre_g", "conv_w", "sinks", "attn_out_g", "conv_out_g", "mix_post_g", "mlp_pre_g",
          "mlp_post_g")
_ORDER = ("meta_tokens", "mix_pre_g", "w_in", "conv_w", "sinks", "attn_out_g", "conv_out_g", "w_out", "mix_post_g",
          "mlp_pre_g", "w_up", "w_down", "mlp_post_g")


class _Reduce:
    def __init__(self, name, grads, after=()):
        self.name, self.n = name, len(grads)
        half_rows = [g.shape[1] // 2 for g in grads]
        zones = [lax.empty((N_CHIPS, hr, g.shape[2]), F32) for g, hr in zip(grads, half_rows)]
        self.exchange = _Exchange(name + "_swap", list(grads) + zones, _swap_plan(self.n, half_rows), self.n, after)

    @property
    def token(self):
        return self.exchange.token

    def scatter(self, after):
        bufs = self.exchange.wait(after)
        sums = [_add_half(g, r) for g, r in zip(bufs[:self.n], bufs[self.n:])]
        zones = [lax.empty((N_CHIPS - 1,) + s.shape[1:], BF16) for s in sums]
        self.exchange = _Exchange(self.name + "_scatter", sums + zones, _scatter_plan(self.n), 3 * self.n)

    def join(self, after):
        bufs = self.exchange.wait(after)
        halves = [_sum_chips(s, q) for s, q in zip(bufs[:self.n], bufs[self.n:])]
        self.exchange = _Exchange(self.name + "_join", halves, _join_plan(self.n), self.n)

    def done(self, after):
        return self.exchange.wait(after)


def _pad_cols(a, n=D_MODEL):
    return jnp.pad(a, ((0, 0), (0, n - a.shape[1])))


def kernel(x, meta_tokens, mix_pre_g, w_in, conv_w, sinks, attn_out_g, conv_out_g, w_out, mix_post_g, mlp_pre_g, w_up, w_down, mlp_post_g, loss_target, m_meta_tokens, m_mix_pre_g, m_w_in, m_conv_w, m_sinks, m_attn_out_g, m_conv_out_g, m_w_out, m_mix_post_g, m_mlp_pre_g, m_w_up, m_w_down, m_mlp_post_g, v_meta_tokens, v_mix_pre_g, v_w_in, v_conv_w, v_sinks, v_attn_out_g, v_conv_out_g, v_w_out, v_mix_post_g, v_mlp_pre_g, v_w_up, v_w_down, v_mlp_post_g):
    w = dict(meta_tokens=meta_tokens, mix_pre_g=mix_pre_g, w_in=w_in, conv_w=conv_w, sinks=sinks,
             attn_out_g=attn_out_g, conv_out_g=conv_out_g, w_out=w_out, mix_post_g=mix_post_g, mlp_pre_g=mlp_pre_g,
             w_up=w_up, w_down=w_down, mlp_post_g=mlp_post_g)
    m = dict(meta_tokens=m_meta_tokens, mix_pre_g=m_mix_pre_g, w_in=m_w_in, conv_w=m_conv_w, sinks=m_sinks,
             attn_out_g=m_attn_out_g, conv_out_g=m_conv_out_g, w_out=m_w_out, mix_post_g=m_mix_post_g,
             mlp_pre_g=m_mlp_pre_g, w_up=m_w_up, w_down=m_w_down, mlp_post_g=m_mlp_post_g)
    v = dict(meta_tokens=v_meta_tokens, mix_pre_g=v_mix_pre_g, w_in=v_w_in, conv_w=v_conv_w, sinks=v_sinks,
             attn_out_g=v_attn_out_g, conv_out_g=v_conv_out_g, w_out=v_w_out, mix_post_g=v_mix_post_g,
             mlp_pre_g=v_mlp_pre_g, w_up=v_w_up, w_down=v_w_down, mlp_post_g=v_mlp_post_g)
    chip = 2 * lax.axis_index("x") + lax.axis_index("y")
    tl = _Tiles(x.shape[1] + BLOCK)

    def zone(quarter):
        return lax.dynamic_update_slice(lax.empty((N_CHIPS,) + quarter.shape, quarter.dtype), quarter[None],
                                        (chip,) + (0,) * quarter.ndim)

    zones = {n: [zone(w[n][l].astype(BF16)) for l in range(DEPTH)] for n in _LARGE}
    first = _Exchange("gather_first", [zones["w_in"][0], zone(w["conv_w"]), zone(w["meta_tokens"])], _gather_plan(3), 9)
    rest = _Exchange("gather_rest", [zones[n][0] for n in ("w_out", "w_up", "w_down")], _gather_plan(3), 9,
                     [first.token])

    def whole_in(quarters):
        return jnp.transpose(quarters, (1, 0, 2)).reshape(D_MODEL, IN_W)

    q_in, q_conv, q_meta = first.wait(rest.token)
    conv_whole = jnp.transpose(q_conv, (1, 2, 0, 3)).reshape(DEPTH, CONV_K, CONV_W)
    meta = jnp.transpose(q_meta, (1, 0, 2)).reshape(N_META, D_MODEL)
    p = [{"conv_w": conv_whole[l], "sinks": w["sinks"][l]} for l in range(DEPTH)]
    for l in range(DEPTH):
        for n in ("mix_pre_g", "attn_out_g", "conv_out_g", "mix_post_g", "mlp_pre_g", "mlp_post_g"):
            p[l][n] = w[n][l][None, :]

    h = jnp.concatenate([jnp.zeros((LEAD_PAD, D_MODEL), F32), meta, x[0]], axis=0)
    p[0]["w_in"] = whole_in(q_in)
    mixed = _mixer_fwd(h, p[0], tl)
    second = _Exchange("gather_second", [zones["w_in"][1], zones["w_out"][1]], _gather_plan(2), 6, [mixed[-1]])
    second_mlp = _Exchange("gather_second_mlp", [zones["w_up"][1], zones["w_down"][1]], _gather_plan(2), 6,
                           [second.token])
    p[0]["w_out"], p[0]["w_up"], p[0]["w_down"] = rest.wait(second_mlp.token)
    h1, saved0 = _out_fwd(mixed, p[0], tl)
    h, saved0 = _mlp_fwd(h1, saved0, p[0], tl)
    q_in, p[1]["w_out"] = second.wait(h)
    p[1]["w_in"] = whole_in(q_in)
    h1, saved1 = _out_fwd(_mixer_fwd(h, p[1], tl), p[1], tl)
    p[1]["w_up"], p[1]["w_down"] = second_mlp.wait(h1)
    h, saved1 = _mlp_fwd(h1, saved1, p[1], tl)
    loss_tile, dh = _loss_head(h, loss_target[0], tl.tm)
    loss = lax.psum(loss_tile[0, 0], ("x", "y", "c"))

    def adamw(layer, halves, other):
        return {n: _adamw_large(layer, w[n], halves[n], m[n], v[n], None if other is None else other[n])
                for n in halves}

    dh1, g1 = _mlp_part_bwd(dh, saved1, p[1], tl)
    carry, gm = _mix_out_part_bwd(dh1, saved1, p[1], tl)
    dh, gi = _attn_in_part_bwd(carry, saved1, p[1], tl)
    g1.update(gm, **gi)
    red1 = _Reduce("reduce1", [g1[n] for n in _LARGE])
    dh1, g0 = _mlp_part_bwd(dh, saved0, p[0], tl, [red1.token])
    red1.scatter(dh1)
    red0a = _Reduce("reduce0a", [g0["w_up"], g0["w_down"]], [red1.token])
    carry, gm = _mix_out_part_bwd(dh1, saved0, p[0], tl, [red0a.token])
    red1.join(carry[1])
    red0a.scatter(red1.token)
    dh0, gi = _attn_in_part_bwd(carry, saved0, p[0], tl, [red0a.token])
    g0.update(gm, **gi)
    red0b = _Reduce("reduce0b", [g0["w_in"], g0["w_out"]])
    done1 = adamw(1, dict(zip(_LARGE, red1.done(red0b.token))), None)
    red0a.join(done1["w_down"][0])
    red0b.scatter(red0a.token)
    halves0 = dict(zip(("w_up", "w_down"), red0a.done(red0b.token)))
    done0 = adamw(0, halves0, done1)
    red0b.join(done0["w_down"][0])
    done0.update(adamw(0, dict(zip(("w_in", "w_out"), red0b.done(red0b.token))), done1))
    grad_x = dh0[BLOCK:][None]
    grads = {n: [g0[n], g1[n]] for n in g0 if n not in _LARGE}

    rows = [dh0[LEAD_PAD:BLOCK]]
    for n in ("mix_pre_g", "mix_post_g", "mlp_pre_g", "mlp_post_g"):
        rows += grads[n]
    rows += [jnp.concatenate([grads["attn_out_g"][l], grads["conv_out_g"][l]], axis=1) for l in range(DEPTH)]
    rows.append(jnp.concatenate(grads["conv_w"], axis=1))
    rows.append(_pad_cols(jnp.concatenate(grads["sinks"])[None, :]))
    packed = jnp.concatenate(rows, axis=0)
    packed = jnp.pad(packed, ((0, SMALL_ROWS - packed.shape[0]), (0, 0)))
    total = _sum_devices(packed)
    r0 = N_META
    small = {
        "meta_tokens": lax.dynamic_slice(total[:N_META], (0, chip * (D_MODEL // N_CHIPS)), (N_META, D_MODEL // N_CHIPS)),
        "mix_pre_g": total[r0:r0 + 2], "mix_post_g": total[r0 + 2:r0 + 4], "mlp_pre_g": total[r0 + 4:r0 + 6],
        "mlp_post_g": total[r0 + 6:r0 + 8],
        "attn_out_g": total[r0 + 8:r0 + 10, :ATTN_W], "conv_out_g": total[r0 + 8:r0 + 10, ATTN_W:],
        "conv_w": lax.dynamic_slice(total[r0 + 10:r0 + 13].reshape(CONV_K, DEPTH, CONV_W).transpose(1, 0, 2),
                                    (0, 0, chip * (CONV_W // N_CHIPS)), (DEPTH, CONV_K, CONV_W // N_CHIPS)),
        "sinks": total[r0 + 13, :DEPTH * N_Q_HEADS].reshape(DEPTH, N_Q_HEADS),
    }

    grad, delta, new_m, new_v = {}, {}, {}, {}
    for n in _LARGE:
        grad[n], delta[n], new_m[n], new_v[n] = done0[n]
    ds, nms, nvs = _adamw_small([w[n] for n in _SMALL], [small[n] for n in _SMALL], [m[n] for n in _SMALL],
                                [v[n] for n in _SMALL])
    for i, n in enumerate(_SMALL):
        grad[n], delta[n], new_m[n], new_v[n] = small[n], ds[i], nms[i], nvs[i]
    return (loss, grad_x, *[grad[n] for n in _ORDER], *[delta[n] for n in _ORDER], *[new_m[n] for n in _ORDER],
            *[new_v[n] for n in _ORDER])
```

```python
import functools

import jax
import jax.numpy as jnp
from jax import lax
from jax.experimental import pallas as pl
from jax.experimental.pallas import tpu as pltpu

F32 = jnp.float32
BF16 = jnp.bfloat16

D_MODEL = 1024
DEPTH = 2
N_META = 16
ATTN_W = 512
CONV_W = 512
HEAD_DIM = 64
N_Q_HEADS = 8
N_KV_HEADS = 2
GROUP = N_Q_HEADS // N_KV_HEADS
KV_W = N_KV_HEADS * HEAD_DIM
CONV_K = 3
BLOCK = 128
LEAD_PAD = BLOCK - N_META
ROPE_THETA = 500000.0
ROT_DIM = HEAD_DIM // 4
ROT_HALF = ROT_DIM // 2
D_FF = 4 * D_MODEL
IN_W = ATTN_W + 2 * KV_W + 3 * CONV_W
QKV_W = ATTN_W + 2 * KV_W
EPS = 1e-6
SCALE = HEAD_DIM ** -0.5
FF_CHUNK = 1024
N_CHIPS = 4
N_DEV = 8

ADAM_LR = 0.001
ADAM_B1 = 0.9
ADAM_B2 = 0.999
ADAM_EPS = 1e-08
ADAM_WD = 0.01
ADAM_STEP = 10

V7X_VMEM_LIMIT = 56 * 1024 * 1024
SMALL_ROWS = 32

MESH = pl.DeviceIdType.MESH


def _params(*sem):
    return pltpu.CompilerParams(dimension_semantics=sem, vmem_limit_bytes=V7X_VMEM_LIMIT)


def _row_tile(t, most):
    nb = t // BLOCK
    for b in range(most // BLOCK, 0, -1):
        if nb % b == 0:
            return b * BLOCK
    return BLOCK


def _behind(body, deps):
    n = len(deps)

    def wrapped(*refs):
        body(*refs[n:])

    return wrapped, [pl.BlockSpec(memory_space=pl.ANY)] * n


def _rms(x, g):
    r = lax.rsqrt(jnp.mean(x * x, axis=-1, keepdims=True) + EPS)
    return x * r * g


def _rms_bwd(dy, x, g):
    r = lax.rsqrt(jnp.mean(x * x, axis=-1, keepdims=True) + EPS)
    xh = x * r
    dg = jnp.sum(dy * xh, axis=0, keepdims=True)
    dxh = dy * g
    dx = r * (dxh - xh * jnp.mean(dxh * xh, axis=-1, keepdims=True))
    return dx, dg


def _rope(x, cos, sa, sb):
    n = x.shape[-1]
    return x * cos + pltpu.roll(x, n - ROT_HALF, 1) * sa + pltpu.roll(x, ROT_HALF, 1) * sb


def _rope_bwd(dy, cos, sa, sb):
    n = dy.shape[-1]
    return dy * cos + pltpu.roll(dy * sa, ROT_HALF, 1) + pltpu.roll(dy * sb, n - ROT_HALF, 1)


def _rope_tables(t):
    pos = lax.broadcasted_iota(jnp.int32, (t, 2 * HEAD_DIM), 0).astype(F32) - LEAD_PAD
    dim = lax.broadcasted_iota(jnp.int32, (t, 2 * HEAD_DIM), 1) % HEAD_DIM
    pair = (dim % ROT_HALF).astype(F32)
    inv_freq = jnp.power(jnp.float32(ROPE_THETA), -(2.0 * pair) / ROT_DIM)
    ang = pos * inv_freq
    cos, sin = jnp.cos(ang), jnp.sin(ang)
    return (jnp.where(dim < ROT_DIM, cos, 1.0), jnp.where(dim < ROT_HALF, -sin, 0.0),
            jnp.where((dim >= ROT_HALF) & (dim < ROT_DIM), sin, 0.0))


def _in_proj(h, g, w, tabs, tm):
    t = h.shape[0]

    def body(h_ref, g_ref, w_ref, c_ref, sa_ref, sb_ref, a_ref, q_ref, k_ref, v_ref, b_ref, cg_ref, hc_ref):
        a = _rms(h_ref[...], g_ref[...]).astype(BF16)
        a_ref[...] = a
        p = jnp.dot(a, w_ref[...], preferred_element_type=F32)
        cos, sa, sb = c_ref[...], sa_ref[...], sb_ref[...]
        rep = ATTN_W // (2 * HEAD_DIM)
        q = _rope(p[:, :ATTN_W], jnp.tile(cos, (1, rep)), jnp.tile(sa, (1, rep)), jnp.tile(sb, (1, rep)))
        q_ref[...] = (q * SCALE).astype(BF16)
        k_ref[...] = _rope(p[:, ATTN_W:ATTN_W + KV_W], cos, sa, sb).astype(BF16)
        v_ref[...] = p[:, ATTN_W + KV_W:QKV_W].astype(BF16)
        b_ref[...] = p[:, QKV_W:QKV_W + CONV_W]
        cg_ref[...] = p[:, QKV_W + CONV_W:QKV_W + 2 * CONV_W]
        hc_ref[...] = p[:, QKV_W + 2 * CONV_W:]

    row = lambda n: pl.BlockSpec((tm, n), lambda i: (i, 0))
    full = lambda a: pl.BlockSpec(a.shape, lambda i: (0, 0))
    return pl.pallas_call(
        body, name="in_proj", grid=(t // tm,),
        in_specs=[row(D_MODEL), full(g), full(w), row(2 * HEAD_DIM), row(2 * HEAD_DIM), row(2 * HEAD_DIM)],
        out_specs=[row(D_MODEL), row(ATTN_W), row(KV_W), row(KV_W), row(CONV_W), row(CONV_W), row(CONV_W)],
        out_shape=[jax.ShapeDtypeStruct((t, D_MODEL), BF16), jax.ShapeDtypeStruct((t, ATTN_W), BF16),
                   jax.ShapeDtypeStruct((t, KV_W), BF16), jax.ShapeDtypeStruct((t, KV_W), BF16),
                   jax.ShapeDtypeStruct((t, CONV_W), F32), jax.ShapeDtypeStruct((t, CONV_W), F32),
                   jax.ShapeDtypeStruct((t, CONV_W), F32)],
        compiler_params=_params("parallel"),
    )(h, g, w, *tabs)


def _attn_bias():
    r = lax.broadcasted_iota(jnp.int32, (3, BLOCK, 2 * BLOCK), 1)
    c = lax.broadcasted_iota(jnp.int32, (3, BLOCK, 2 * BLOCK), 2)
    i = lax.broadcasted_iota(jnp.int32, (3, BLOCK, 2 * BLOCK), 0)
    ok = (c > r) & (c <= r + BLOCK) & (c + (i - 1) * BLOCK >= LEAD_PAD)
    return jnp.where(ok, 0.0, -jnp.inf).astype(F32)


def _attn_scores(qh, kg, bias):
    return lax.dot_general(qh, kg, (((1,), (1,)), ((), ())), preferred_element_type=F32) + bias


def _attn_probs(s, sk):
    m = jnp.maximum(jnp.max(s, axis=-1, keepdims=True), sk)
    e = jnp.exp(s - m)
    es = jnp.exp(sk - m)
    rden = 1.0 / (jnp.sum(e, axis=-1, keepdims=True) + es)
    return e * rden, es * rden


def _head(hh):
    return slice(hh * HEAD_DIM, (hh + 1) * HEAD_DIM)


def _two_blocks(ref, i):
    prev = jnp.maximum(i - 1, 0)
    return jnp.concatenate([ref[pl.ds(pl.multiple_of(prev * BLOCK, BLOCK), BLOCK), :],
                            ref[pl.ds(pl.multiple_of(i * BLOCK, BLOCK), BLOCK), :]], axis=0)


def _attn_fwd(q, k, v, bias, sinks, tm):
    t = q.shape[0]
    per_step = tm // BLOCK
    heads = range(N_Q_HEADS)

    def body(s_ref, q_ref, k_ref, v_ref, bias_ref, o_ref):
        for b in range(per_step):
            i = pl.program_id(0) * per_step + b
            rows = slice(b * BLOCK, (b + 1) * BLOCK)
            kc, vc = _two_blocks(k_ref, i), _two_blocks(v_ref, i)
            bias_i = bias_ref[jnp.minimum(i, 2)]
            scores = [_attn_scores(q_ref[rows, _head(hh)], kc[:, _head(hh // GROUP)], bias_i) for hh in heads]
            probs = [_attn_probs(scores[hh], s_ref[hh])[0].astype(BF16) for hh in heads]
            for hh in heads:
                o_ref[rows, _head(hh)] = jnp.dot(probs[hh], vc[:, _head(hh // GROUP)], preferred_element_type=F32)

    whole = pl.BlockSpec((t, KV_W), lambda i: (0, 0))
    return pl.pallas_call(
        body, name="attn_fwd", grid=(t // tm,),
        in_specs=[pl.BlockSpec(memory_space=pltpu.SMEM), pl.BlockSpec((tm, ATTN_W), lambda i: (i, 0)), whole, whole,
                  pl.BlockSpec(bias.shape, lambda i: (0, 0, 0))],
        out_specs=pl.BlockSpec((tm, ATTN_W), lambda i: (i, 0)),
        out_shape=jax.ShapeDtypeStruct((t, ATTN_W), F32),
        compiler_params=_params("parallel"),
    )(sinks, q, k, v, bias)


def _shift_rows(u, halo, n):
    r = pltpu.roll(u, n, 0)
    hr = pltpu.roll(halo, n, 0)
    idx = lax.broadcasted_iota(jnp.int32, hr.shape, 0)
    return jnp.concatenate([jnp.where(idx < n, hr, r[:8]), r[8:]], axis=0)


def _advance_rows(u, halo, n):
    rows = u.shape[0]
    r = pltpu.roll(u, rows - n, 0)
    hr = pltpu.roll(halo, 8 - n, 0)
    idx = lax.broadcasted_iota(jnp.int32, hr.shape, 0)
    return jnp.concatenate([r[:rows - 8], jnp.where(idx >= 8 - n, hr, r[rows - 8:])], axis=0)


def _mix_out(h, o, b, c, hc, cw, ga, gc, w, gp, tm, deps=()):
    t = h.shape[0]

    def body(h_ref, o_ref, b_ref, c_ref, hc_ref, cw_ref, ga_ref, gc_ref, w_ref, gp_ref, h1_ref, y_ref, z_ref, halo):
        @pl.when(pl.program_id(0) == 0)
        def _():
            halo[...] = jnp.zeros_like(halo)

        u = c_ref[...] * hc_ref[...]
        cv = cw_ref[0:1, :] * _shift_rows(u, halo[...], 2) + cw_ref[1:2, :] * _shift_rows(u, halo[...], 1) \
            + cw_ref[2:3, :] * u
        halo[...] = u[tm - 8:]
        yc = b_ref[...] * cv
        y = jnp.concatenate([_rms(o_ref[...], ga_ref[...]), _rms(yc, gc_ref[...])], axis=1).astype(BF16)
        y_ref[...] = y
        z = jnp.dot(y, w_ref[...].reshape(D_MODEL, D_MODEL), preferred_element_type=F32)
        z_ref[...] = z
        h1_ref[...] = h_ref[...] + _rms(z, gp_ref[...])

    row = lambda n: pl.BlockSpec((tm, n), lambda i: (i, 0))
    full = lambda a: pl.BlockSpec(a.shape, lambda i: (0,) * a.ndim)
    body, dep_specs = _behind(body, deps)
    return pl.pallas_call(
        body, name="mix_out", grid=(t // tm,),
        in_specs=dep_specs + [row(D_MODEL), row(ATTN_W), row(CONV_W), row(CONV_W), row(CONV_W), full(cw), full(ga),
                              full(gc), full(w), full(gp)],
        out_specs=[row(D_MODEL), row(D_MODEL), row(D_MODEL)],
        out_shape=[jax.ShapeDtypeStruct((t, D_MODEL), F32), jax.ShapeDtypeStruct((t, D_MODEL), BF16),
                   jax.ShapeDtypeStruct((t, D_MODEL), F32)],
        scratch_shapes=[pltpu.VMEM((8, CONV_W), F32)],
        compiler_params=_params("arbitrary"),
    )(*deps, h, o, b, c, hc, cw, ga, gc, w, gp)


def _mlp(h1, g1, wu, wd, g2, tm):
    t = h1.shape[0]
    nj = D_FF // FF_CHUNK

    def body(h1_ref, g1_ref, wu_ref, wd_ref, g2_ref, h2_ref, a2_ref, act_ref, f_ref, acc):
        j = pl.program_id(1)

        @pl.when(j == 0)
        def _():
            a2_ref[...] = _rms(h1_ref[...], g1_ref[...]).astype(BF16)

        up = jnp.dot(a2_ref[...], wu_ref[...], preferred_element_type=F32)
        act = jnp.square(jnp.maximum(up, 0.0)).astype(BF16)
        act_ref[...] = act
        part = jnp.dot(act, wd_ref[...], preferred_element_type=F32)

        @pl.when(j == 0)
        def _():
            acc[...] = part

        @pl.when(j > 0)
        def _():
            acc[...] += part

        @pl.when(j == nj - 1)
        def _():
            f = acc[...]
            f_ref[...] = f
            h2_ref[...] = h1_ref[...] + _rms(f, g2_ref[...])

    row = pl.BlockSpec((tm, D_MODEL), lambda i, j: (i, 0))
    vec = pl.BlockSpec((1, D_MODEL), lambda i, j: (0, 0))
    quarter = pl.BlockSpec((None, D_MODEL, FF_CHUNK), lambda i, j: (j, 0, 0))
    return pl.pallas_call(
        body, name="mlp", grid=(t // tm, nj),
        in_specs=[row, vec, quarter, quarter, vec],
        out_specs=[row, row, pl.BlockSpec((tm, FF_CHUNK), lambda i, j: (i, j)), row],
        out_shape=[jax.ShapeDtypeStruct((t, D_MODEL), F32), jax.ShapeDtypeStruct((t, D_MODEL), BF16),
                   jax.ShapeDtypeStruct((t, D_FF), BF16), jax.ShapeDtypeStruct((t, D_MODEL), F32)],
        scratch_shapes=[pltpu.VMEM((tm, D_MODEL), F32)],
        compiler_params=_params("parallel", "arbitrary"),
    )(h1, g1, wu, wd, g2)


def _loss_head(h, target, tm):
    t = h.shape[0]
    per_step = tm // BLOCK

    def body(h_ref, *rest):
        t_refs, (loss_ref, dh_ref) = rest[:per_step], rest[per_step:]
        i = pl.program_id(0)

        @pl.when(i == 0)
        def _():
            loss_ref[...] = jnp.zeros_like(loss_ref)

        total = jnp.zeros((), F32)
        for b in range(per_step):
            rows = slice(b * BLOCK, (b + 1) * BLOCK)
            err = h_ref[rows, :] - t_refs[b][...]
            if b == 0:
                err = jnp.where(i == 0, 0.0, err)
            dh_ref[rows, :] = err * (1.0 / D_MODEL)
            total = total + jnp.sum(err * err)
        loss_ref[...] += total * (0.5 / D_MODEL)

    def target_block(b):
        return pl.BlockSpec((BLOCK, D_MODEL), lambda i: (jnp.maximum(i * per_step + b - 1, 0), 0))

    return pl.pallas_call(
        body, name="loss_head", grid=(t // tm,),
        in_specs=[pl.BlockSpec((tm, D_MODEL), lambda i: (i, 0))] + [target_block(b) for b in range(per_step)],
        out_specs=[pl.BlockSpec((8, 128), lambda i: (0, 0)), pl.BlockSpec((tm, D_MODEL), lambda i: (i, 0))],
        out_shape=[jax.ShapeDtypeStruct((8, 128), F32), jax.ShapeDtypeStruct((t, D_MODEL), F32)],
        compiler_params=_params("arbitrary"),
    )(h, *([target] * per_step))


def _mlp_bwd(dh2, f, g2, act, wd, wu, h1, g1, tm, deps=()):
    t = dh2.shape[0]
    nj = D_FF // FF_CHUNK

    def body(dh2_ref, f_ref, g2_ref, act_ref, wd_ref, wu_ref, h1_ref, g1_ref, dh1_ref, df_ref, dup_ref, dg2_ref,
             dg1_ref, acc):
        i, j = pl.program_id(0), pl.program_id(1)

        @pl.when((i == 0) & (j == 0))
        def _():
            dg2_ref[...] = jnp.zeros_like(dg2_ref)
            dg1_ref[...] = jnp.zeros_like(dg1_ref)

        @pl.when(j == 0)
        def _():
            df, dg = _rms_bwd(dh2_ref[...], f_ref[...], g2_ref[...])
            df_ref[...] = df.astype(BF16)
            dg2_ref[...] += dg

        dact = lax.dot_general(df_ref[...], wd_ref[...], (((1,), (1,)), ((), ())), preferred_element_type=F32)
        dup = (dact * (2.0 * jnp.sqrt(act_ref[...].astype(F32)))).astype(BF16)
        dup_ref[...] = dup
        part = lax.dot_general(dup, wu_ref[...], (((1,), (1,)), ((), ())), preferred_element_type=F32)

        @pl.when(j == 0)
        def _():
            acc[...] = part

        @pl.when(j > 0)
        def _():
            acc[...] += part

        @pl.when(j == nj - 1)
        def _():
            dx, dg = _rms_bwd(acc[...], h1_ref[...], g1_ref[...])
            dh1_ref[...] = dh2_ref[...] + dx
            dg1_ref[...] += dg

    row = pl.BlockSpec((tm, D_MODEL), lambda i, j: (i, 0))
    vec = pl.BlockSpec((1, D_MODEL), lambda i, j: (0, 0))
    chunk = pl.BlockSpec((tm, FF_CHUNK), lambda i, j: (i, j))
    quarter = pl.BlockSpec((None, D_MODEL, FF_CHUNK), lambda i, j: (j, 0, 0))
    body, dep_specs = _behind(body, deps)
    return pl.pallas_call(
        body, name="mlp_bwd", grid=(t // tm, nj),
        in_specs=dep_specs + [row, row, vec, chunk, quarter, quarter, row, vec],
        out_specs=[row, row, chunk, vec, vec],
        out_shape=[jax.ShapeDtypeStruct((t, D_MODEL), F32), jax.ShapeDtypeStruct((t, D_MODEL), BF16),
                   jax.ShapeDtypeStruct((t, D_FF), BF16), jax.ShapeDtypeStruct((1, D_MODEL), F32),
                   jax.ShapeDtypeStruct((1, D_MODEL), F32)],
        scratch_shapes=[pltpu.VMEM((tm, D_MODEL), F32)],
        compiler_params=_params("arbitrary", "arbitrary"),
    )(*deps, dh2, f, g2, act, wd, wu, h1, g1)


def _weight_grad(x, y, tm, name):
    t, k = x.shape
    n = y.shape[1]
    tk = tn = FF_CHUNK

    def body(x_ref, y_ref, o_ref):
        @pl.when(pl.program_id(2) == 0)
        def _():
            o_ref[...] = jnp.zeros_like(o_ref)

        o_ref[...] += lax.dot_general(x_ref[...], y_ref[...], (((0,), (0,)), ((), ())), preferred_element_type=F32)

    return pl.pallas_call(
        body, name=name, grid=(k // tk, n // tn, t // tm),
        in_specs=[pl.BlockSpec((tm, tk), lambda a, b, r: (r, a)), pl.BlockSpec((tm, tn), lambda a, b, r: (r, b))],
        out_specs=pl.BlockSpec((None, None, tk, tn), lambda a, b, r: (a, b, 0, 0)),
        out_shape=jax.ShapeDtypeStruct((k // tk, n // tn, tk, tn), F32),
        compiler_params=_params("parallel", "parallel", "arbitrary"),
    )(x, y)


def _weight_grad_in(a, dproj, tm):
    t = a.shape[0]
    nt = t // tm
    qw = IN_W // N_CHIPS

    def body(a_ref, d_ref, o_ref, acc):
        r = pl.program_id(0)

        @pl.when(r == 0)
        def _():
            acc[...] = jnp.zeros_like(acc)

        acc[...] += lax.dot_general(a_ref[...], d_ref[...], (((0,), (0,)), ((), ())), preferred_element_type=F32)

        @pl.when(r == nt - 1)
        def _():
            for s in range(N_CHIPS):
                o_ref[s] = acc[:, s * qw:(s + 1) * qw]

    return pl.pallas_call(
        body, name="grad_w_in", grid=(nt,),
        in_specs=[pl.BlockSpec((tm, D_MODEL), lambda r: (r, 0)), pl.BlockSpec((tm, IN_W), lambda r: (r, 0))],
        out_specs=pl.BlockSpec(memory_space=pltpu.VMEM),
        out_shape=jax.ShapeDtypeStruct((N_CHIPS, D_MODEL, qw), F32),
        scratch_shapes=[pltpu.VMEM((D_MODEL, IN_W), F32)],
        compiler_params=_params("arbitrary"),
    )(a, dproj)


def _mix_out_bwd(dh1, z, gp, w, o, b, c, hc, cw, ga, gc, tm, deps=()):
    t = dh1.shape[0]
    nt = t // tm
    per8 = tm // 8

    def body(dh1_ref, z_ref, gp_ref, w_ref, o_ref, b_ref, c_ref, hc_ref, cp_ref, hp_ref, cw_ref, ga_ref, gc_ref,
             dz_ref, do_ref, dbch_ref, dgp_ref, dga_ref, dgc_ref, dcw_ref, halo):
        i = pl.program_id(0)

        @pl.when(i == 0)
        def _():
            halo[...] = jnp.zeros_like(halo)
            dgp_ref[...] = jnp.zeros_like(dgp_ref)
            dga_ref[...] = jnp.zeros_like(dga_ref)
            dgc_ref[...] = jnp.zeros_like(dgc_ref)
            dcw_ref[...] = jnp.zeros_like(dcw_ref)

        dz, dgp = _rms_bwd(dh1_ref[...], z_ref[...], gp_ref[...])
        dgp_ref[...] += dgp
        dz = dz.astype(BF16)
        dz_ref[...] = dz
        dy = lax.dot_general(dz, w_ref[...].reshape(D_MODEL, D_MODEL), (((1,), (1,)), ((), ())),
                             preferred_element_type=F32)
        do, dga = _rms_bwd(dy[:, :ATTN_W], o_ref[...], ga_ref[...])
        do_ref[...] = do
        dga_ref[...] += dga

        u = c_ref[...] * hc_ref[...]
        first = i == nt - 1
        u_before = jnp.where(first, 0.0, cp_ref[...] * hp_ref[...])
        u1 = _shift_rows(u, u_before, 1)
        u2 = _shift_rows(u, u_before, 2)
        cv = cw_ref[0:1, :] * u2 + cw_ref[1:2, :] * u1 + cw_ref[2:3, :] * u
        bb = b_ref[...]
        dyc, dgc = _rms_bwd(dy[:, ATTN_W:], bb * cv, gc_ref[...])
        dgc_ref[...] += dgc
        dcv = dyc * bb
        d1 = _advance_rows(dcv, halo[...], 1)
        d2 = _advance_rows(dcv, halo[...], 2)
        halo[...] = dcv[:8]
        du = cw_ref[2:3, :] * dcv + cw_ref[1:2, :] * d1 + cw_ref[0:1, :] * d2
        dbch_ref[...] = jnp.concatenate([dyc * cv, du * hc_ref[...], du * c_ref[...]], axis=1).astype(BF16)
        dcw_ref[...] += jnp.concatenate([jnp.sum(dcv * u2, axis=0, keepdims=True),
                                         jnp.sum(dcv * u1, axis=0, keepdims=True),
                                         jnp.sum(dcv * u, axis=0, keepdims=True)], axis=0)

    row = lambda n: pl.BlockSpec((tm, n), lambda i: (nt - 1 - i, 0))
    before = pl.BlockSpec((8, CONV_W), lambda i: (jnp.maximum((nt - 1 - i) * per8 - 1, 0), 0))
    full = lambda a: pl.BlockSpec(a.shape, lambda i: (0,) * a.ndim)
    vec = lambda n: pl.BlockSpec((1, n), lambda i: (0, 0))
    body, dep_specs = _behind(body, deps)
    return pl.pallas_call(
        body, name="mix_out_bwd", grid=(nt,),
        in_specs=dep_specs + [row(D_MODEL), row(D_MODEL), full(gp), full(w), row(ATTN_W), row(CONV_W), row(CONV_W),
                              row(CONV_W), before, before, full(cw), full(ga), full(gc)],
        out_specs=[row(D_MODEL), row(ATTN_W), row(3 * CONV_W), vec(D_MODEL), vec(ATTN_W), vec(CONV_W),
                   pl.BlockSpec((CONV_K, CONV_W), lambda i: (0, 0))],
        out_shape=[jax.ShapeDtypeStruct((t, D_MODEL), BF16), jax.ShapeDtypeStruct((t, ATTN_W), F32),
                   jax.ShapeDtypeStruct((t, 3 * CONV_W), BF16), jax.ShapeDtypeStruct((1, D_MODEL), F32),
                   jax.ShapeDtypeStruct((1, ATTN_W), F32), jax.ShapeDtypeStruct((1, CONV_W), F32),
                   jax.ShapeDtypeStruct((CONV_K, CONV_W), F32)],
        scratch_shapes=[pltpu.VMEM((8, CONV_W), F32)],
        compiler_params=_params("arbitrary"),
    )(*deps, dh1, z, gp, w, o, b, c, hc, c, hc, cw, ga, gc)


def _attn_bwd(q, k, v, o, do, bias, sinks, tm, deps=()):
    t = q.shape[0]
    per_step = tm // BLOCK

    def body(s_ref, q_ref, k_ref, v_ref, o_ref, do_ref, bias_ref, dq_ref, dk_ref, dv_ref, ds_ref):
        step = pl.program_id(0)

        @pl.when(step == 0)
        def _():
            ds_ref[...] = jnp.zeros_like(ds_ref)

        heads = range(N_Q_HEADS)

        def first_matmuls(b):
            i = step * per_step + b
            rows = slice(b * BLOCK, (b + 1) * BLOCK)
            kc, vc = _two_blocks(k_ref, i), _two_blocks(v_ref, i)
            bias_i = bias_ref[jnp.minimum(i, 2)]
            kgs = [kc[:, _head(g)] for g in range(N_KV_HEADS)]
            vgs = [vc[:, _head(g)] for g in range(N_KV_HEADS)]
            qs = [q_ref[rows, _head(hh)] for hh in heads]
            dos = [do_ref[rows, _head(hh)] for hh in heads]
            dosb = [d.astype(BF16) for d in dos]
            scores = [_attn_scores(qs[hh], kgs[hh // GROUP], bias_i) for hh in heads]
            dps = [lax.dot_general(dosb[hh], vgs[hh // GROUP], (((1,), (1,)), ((), ())), preferred_element_type=F32)
                   for hh in heads]
            return kgs, qs, dos, dosb, scores, dps

        dsink = [jnp.zeros((BLOCK, 1), F32) for _ in range(N_Q_HEADS)]
        ahead = None
        for b in range(per_step):
            i = step * per_step + b
            rows = slice(b * BLOCK, (b + 1) * BLOCK)
            kgs, qs, dos, dosb, scores, dps = first_matmuls(b)
            ps, dss = [], []
            for hh in heads:
                p, share = _attn_probs(scores[hh], s_ref[hh])
                drow = jnp.sum(dos[hh] * o_ref[rows, _head(hh)], axis=-1, keepdims=True)
                dss.append((p * (dps[hh] - drow)).astype(BF16))
                ps.append(p.astype(BF16))
                dsink[hh] = dsink[hh] + share * drow
            for hh in heads:
                dq_ref[rows, _head(hh)] = jnp.dot(dss[hh], kgs[hh // GROUP], preferred_element_type=F32) * SCALE
            groups = [slice(GROUP * g, GROUP * (g + 1)) for g in range(N_KV_HEADS)]
            dkg = [lax.dot_general(jnp.concatenate(dss[gr], axis=0), jnp.concatenate(qs[gr], axis=0),
                                   (((0,), (0,)), ((), ())), preferred_element_type=F32) for gr in groups]
            dvg = [lax.dot_general(jnp.concatenate(ps[gr], axis=0), jnp.concatenate(dosb[gr], axis=0),
                                   (((0,), (0,)), ((), ())), preferred_element_type=F32) for gr in groups]
            dkb, dvb = jnp.concatenate(dkg, axis=1), jnp.concatenate(dvg, axis=1)
            if b == 0:
                @pl.when(step > 0)
                def _():
                    before = pl.ds(pl.multiple_of((i - 1) * BLOCK, BLOCK), BLOCK)
                    dk_ref[before, :] += dkb[:BLOCK]
                    dv_ref[before, :] += dvb[:BLOCK]
            else:
                at = pl.ds(pl.multiple_of((i - 1) * BLOCK, BLOCK), BLOCK)
                dk_ref[at, :] = ahead[0] + dkb[:BLOCK]
                dv_ref[at, :] = ahead[1] + dvb[:BLOCK]
            ahead = (dkb[BLOCK:], dvb[BLOCK:])
        last = pl.ds(pl.multiple_of(((step + 1) * per_step - 1) * BLOCK, BLOCK), BLOCK)
        dk_ref[last, :] = ahead[0]
        dv_ref[last, :] = ahead[1]
        for hh in range(N_Q_HEADS):
            ds_ref[hh:hh + 1, :] -= jnp.sum(dsink[hh])

    whole = pl.BlockSpec((t, KV_W), lambda i: (0, 0))
    blk = pl.BlockSpec((tm, ATTN_W), lambda i: (i, 0))
    body, dep_specs = _behind(body, deps)
    return pl.pallas_call(
        body, name="attn_bwd", grid=(t // tm,),
        in_specs=dep_specs + [pl.BlockSpec(memory_space=pltpu.SMEM), blk, whole, whole, blk, blk,
                              pl.BlockSpec(bias.shape, lambda i: (0, 0, 0))],
        out_specs=[blk, whole, whole, pl.BlockSpec((N_Q_HEADS, 128), lambda i: (0, 0))],
        out_shape=[jax.ShapeDtypeStruct((t, ATTN_W), F32), jax.ShapeDtypeStruct((t, KV_W), F32),
                   jax.ShapeDtypeStruct((t, KV_W), F32), jax.ShapeDtypeStruct((N_Q_HEADS, 128), F32)],
        compiler_params=_params("arbitrary"),
    )(*deps, sinks, q, k, v, o, do, bias)


def _in_proj_bwd(dq, dk, dv, dbch, w, dh1, h, g, tabs, tm):
    t = h.shape[0]

    def body(dq_ref, dk_ref, dv_ref, dbch_ref, w_ref, dh1_ref, h_ref, g_ref, c_ref, sa_ref, sb_ref, dh_ref, dp_ref,
             dg_ref):
        @pl.when(pl.program_id(0) == 0)
        def _():
            dg_ref[...] = jnp.zeros_like(dg_ref)

        cos, sa, sb = c_ref[...], sa_ref[...], sb_ref[...]
        rep = ATTN_W // (2 * HEAD_DIM)
        dqr = _rope_bwd(dq_ref[...], jnp.tile(cos, (1, rep)), jnp.tile(sa, (1, rep)), jnp.tile(sb, (1, rep)))
        dkr = _rope_bwd(dk_ref[...], cos, sa, sb)
        dp = jnp.concatenate([dqr.astype(BF16), dkr.astype(BF16), dv_ref[...].astype(BF16), dbch_ref[...]], axis=1)
        dp_ref[...] = dp
        da = lax.dot_general(dp, w_ref[...], (((1,), (1,)), ((), ())), preferred_element_type=F32)
        dx, dg = _rms_bwd(da, h_ref[...], g_ref[...])
        dh_ref[...] = dh1_ref[...] + dx
        dg_ref[...] += dg

    row = lambda n: pl.BlockSpec((tm, n), lambda i: (i, 0))
    full = lambda a: pl.BlockSpec(a.shape, lambda i: (0, 0))
    return pl.pallas_call(
        body, name="in_proj_bwd", grid=(t // tm,),
        in_specs=[row(ATTN_W), row(KV_W), row(KV_W), row(3 * CONV_W), full(w), row(D_MODEL), row(D_MODEL), full(g),
                  row(2 * HEAD_DIM), row(2 * HEAD_DIM), row(2 * HEAD_DIM)],
        out_specs=[row(D_MODEL), row(IN_W), pl.BlockSpec((1, D_MODEL), lambda i: (0, 0))],
        out_shape=[jax.ShapeDtypeStruct((t, D_MODEL), F32), jax.ShapeDtypeStruct((t, IN_W), BF16),
                   jax.ShapeDtypeStruct((1, D_MODEL), F32)],
        compiler_params=_params("arbitrary"),
    )(dq, dk, dv, dbch, w, dh1, h, g, *tabs)


class _Tiles:
    def __init__(self, t):
        self.tm = _row_tile(t, 640)
        self.ts = _row_tile(t, 320)
        self.tabs = _rope_tables(t)
        self.bias = _attn_bias()


def _mixer_fwd(h, p, tl):
    a, q, k, v, b, c, hc = _in_proj(h, p["mix_pre_g"], p["w_in"], tl.tabs, tl.ts)
    o = _attn_fwd(q, k, v, tl.bias, p["sinks"], tl.tm)
    return (h, a, q, k, v, b, c, hc, o)


def _out_fwd(mixed, p, tl, deps=()):
    h, a, q, k, v, b, c, hc, o = mixed
    h1, y, z = _mix_out(h, o, b, c, hc, p["conv_w"], p["attn_out_g"], p["conv_out_g"], p["w_out"], p["mix_post_g"],
                        tl.ts, deps)
    return h1, mixed + (h1, y, z)


def _mlp_fwd(h1, saved, p, tl):
    h2, a2, act, f = _mlp(h1, p["mlp_pre_g"], p["w_up"], p["w_down"], p["mlp_post_g"], tl.tm)
    return h2, saved + (a2, act, f)


def _mlp_part_bwd(dh, saved, p, tl, deps=()):
    h1, a2, act, f = saved[9], saved[12], saved[13], saved[14]
    dh1, df, dup, dg2, dg1 = _mlp_bwd(dh, f, p["mlp_post_g"], act, p["w_down"], p["w_up"], h1, p["mlp_pre_g"], tl.tm,
                                      deps)
    g = {"w_down": _weight_grad(act, df, tl.tm, "grad_w_down").reshape(N_CHIPS, FF_CHUNK, D_MODEL),
         "w_up": _weight_grad(a2, dup, tl.tm, "grad_w_up").reshape(N_CHIPS, D_MODEL, FF_CHUNK),
         "mlp_post_g": dg2, "mlp_pre_g": dg1}
    return dh1, g


def _mix_out_part_bwd(dh1, saved, p, tl, deps=()):
    b, c, hc, o, y, z = saved[5], saved[6], saved[7], saved[8], saved[10], saved[11]
    dz, do, dbch, dgp, dga, dgc, dcw = _mix_out_bwd(dh1, z, p["mix_post_g"], p["w_out"], o, b, c, hc, p["conv_w"],
                                                    p["attn_out_g"], p["conv_out_g"], tl.ts, deps)
    g = {"w_out": _weight_grad(y, dz, tl.tm, "grad_w_out").reshape(N_CHIPS, D_MODEL // N_CHIPS, D_MODEL),
         "mix_post_g": dgp, "attn_out_g": dga, "conv_out_g": dgc, "conv_w": dcw}
    return (dh1, do, dbch), g


def _attn_in_part_bwd(carry, saved, p, tl, deps=()):
    dh1, do, dbch = carry
    h_in, a, q, k, v, o = saved[0], saved[1], saved[2], saved[3], saved[4], saved[8]
    dq, dk, dv, dsink = _attn_bwd(q, k, v, o, do, tl.bias, p["sinks"], tl.tm, deps)
    dh, dproj, dgi = _in_proj_bwd(dq, dk, dv, dbch, p["w_in"], dh1, h_in, p["mix_pre_g"], tl.tabs, tl.ts)
    return dh, {"w_in": _weight_grad_in(a, dproj, tl.tm), "mix_pre_g": dgi, "sinks": dsink[:, 0]}


def _place():
    return lax.axis_index("x"), lax.axis_index("y"), lax.axis_index("c")


def _other_chips(x, y):
    return [(1 - x, y), (x, 1 - y), (1 - x, 1 - y)]


_HBM = pl.BlockSpec(memory_space=pltpu.HBM)
_SEM = pl.BlockSpec(memory_space=pltpu.SEMAPHORE)
_EFFECT = pltpu.SideEffectType.DATAFLOW_SIDE_EFFECTING


class _Exchange:
    def __init__(self, name, bufs, plan, n, after=()):
        self.name, self.plan, nb = name, plan, len(bufs)
        n_in = nb + len(after)

        def body(*refs):
            send, recv, token = refs[n_in], refs[n_in + 1], refs[-1]
            for k, (src, dst, target, _) in enumerate(plan(refs[:nb])):
                pltpu.make_async_remote_copy(src_ref=src, dst_ref=dst, send_sem=send.at[k], recv_sem=recv.at[k],
                                             device_id=target, device_id_type=MESH).start()
            token[...] = jnp.zeros_like(token)

        outs = pl.pallas_call(
            body, name=name + "_start",
            out_shape=(pltpu.SemaphoreType.DMA((n,)), pltpu.SemaphoreType.DMA((n,)),
                       *[pltpu.HBM(b.shape, b.dtype) for b in bufs], jax.ShapeDtypeStruct((8, 128), F32)),
            in_specs=[_HBM] * nb + [pl.BlockSpec(memory_space=pl.ANY)] * len(after),
            out_specs=(_SEM, _SEM, *[_HBM] * nb, pl.BlockSpec(memory_space=pltpu.VMEM)),
            input_output_aliases={i: 2 + i for i in range(nb)},
            compiler_params=pltpu.CompilerParams(has_side_effects=_EFFECT),
        )(*[pltpu.with_memory_space_constraint(b, pltpu.HBM) for b in bufs], *after)
        self.send, self.recv, self.bufs, self.token = outs[0], outs[1], list(outs[2:2 + nb]), outs[-1]

    def wait(self, after):
        plan, nb = self.plan, len(self.bufs)

        def body(*refs):
            send, recv = refs[nb], refs[nb + 1]
            for k, (src, _, target, land) in enumerate(plan(refs[:nb])):
                cp = pltpu.make_async_remote_copy(src_ref=src, dst_ref=land, send_sem=send.at[k], recv_sem=recv.at[k],
                                                  device_id=target, device_id_type=MESH)
                cp.wait_send()
                cp.wait_recv()

        outs = pl.pallas_call(
            body, name=self.name + "_wait", out_shape=[pltpu.HBM(b.shape, b.dtype) for b in self.bufs],
            in_specs=[_HBM] * nb + [_SEM, _SEM, pl.BlockSpec(memory_space=pl.ANY)], out_specs=[_HBM] * nb,
            input_output_aliases={i: i for i in range(nb)},
            compiler_params=pltpu.CompilerParams(has_side_effects=_EFFECT),
        )(*self.bufs, self.send, self.recv, after)
        return list(outs)


def _gather_plan(n):
    def plan(refs):
        x, y, c = _place()
        me = 2 * x + y
        return [(refs[a].at[me], refs[a].at[me], (px, py, c), refs[a].at[2 * px + py])
                for a in range(n) for px, py in _other_chips(x, y)]

    return plan


def _swap_plan(n, half_rows):
    def plan(refs):
        x, y, c = _place()
        out = []
        for a in range(n):
            hr = half_rows[a]
            out.append((refs[a].at[:, pl.ds((1 - c) * hr, hr)], refs[n + a], (x, y, 1 - c), refs[n + a]))
        return out

    return plan


def _scatter_plan(n):
    def plan(refs):
        x, y, c = _place()
        return [(refs[a].at[2 * px + py], refs[n + a].at[k], (px, py, c), refs[n + a].at[k])
                for a in range(n) for k, (px, py) in enumerate(_other_chips(x, y))]

    return plan


def _join_plan(n):
    def plan(refs):
        x, y, c = _place()
        return [(refs[a].at[c], refs[a].at[c], (x, y, 1 - c), refs[a].at[1 - c]) for a in range(n)]

    return plan


def _add_half(g, r):
    rows, cols = g.shape[1], g.shape[2]
    hr = rows // 2
    tr = min(hr, 256)
    per = hr // tr
    first = (lax.axis_index("c") * per).astype(jnp.int32).reshape(1)

    def body(first_ref, g_ref, r_ref, o_ref):
        o_ref[...] = (g_ref[...] + r_ref[...]).astype(BF16)

    blk = pl.BlockSpec((None, tr, cols), lambda s, i, first_ref: (s, i, 0))
    return pl.pallas_call(
        body, name="add_half",
        grid_spec=pltpu.PrefetchScalarGridSpec(
            num_scalar_prefetch=1, grid=(N_CHIPS, per),
            in_specs=[pl.BlockSpec((None, tr, cols), lambda s, i, first_ref: (s, i + first_ref[0], 0)), blk],
            out_specs=blk),
        out_shape=jax.ShapeDtypeStruct(r.shape, BF16),
        compiler_params=_params("parallel", "parallel"),
    )(first, g, r)


def _sum_chips(s, q):
    rows, cols = s.shape[1], s.shape[2]
    tr = min(rows, 256)
    x, y, c = _place()
    where = jnp.stack([2 * x + y, c]).astype(jnp.int32)

    def body(where_ref, s_ref, q_ref, o_ref):
        part = [q_ref[k].astype(F32) for k in range(N_CHIPS - 1)]
        o_ref[...] = ((s_ref[...].astype(F32) + part[0]) + part[1]) + part[2]

    return pl.pallas_call(
        body, name="sum_chips",
        grid_spec=pltpu.PrefetchScalarGridSpec(
            num_scalar_prefetch=1, grid=(rows // tr,),
            in_specs=[pl.BlockSpec((None, tr, cols), lambda i, where_ref: (where_ref[0], i, 0)),
                      pl.BlockSpec((N_CHIPS - 1, tr, cols), lambda i, where_ref: (0, i, 0))],
            out_specs=pl.BlockSpec((None, tr, cols), lambda i, where_ref: (where_ref[1], i, 0))),
        out_shape=jax.ShapeDtypeStruct((2, rows, cols), F32),
        compiler_params=_params("parallel"),
    )(where, s, q)


def _sum_devices(packed):
    def body(p_ref, o_ref, land, send_sems, recv_sems):
        x, y, c = _place()
        me = 4 * x + 2 * y + c
        land[me] = p_ref[...]
        sends = []
        for k in range(1, N_DEV):
            px, py, pc = x ^ (k >> 2), y ^ ((k >> 1) & 1), c ^ (k & 1)
            cp = pltpu.make_async_remote_copy(src_ref=p_ref, dst_ref=land.at[me], send_sem=send_sems.at[k - 1],
                                              recv_sem=recv_sems.at[k - 1], device_id=(px, py, pc), device_id_type=MESH)
            cp.start()
            sends.append(cp)
        for k in range(1, N_DEV):
            px, py, pc = x ^ (k >> 2), y ^ ((k >> 1) & 1), c ^ (k & 1)
            pltpu.make_async_remote_copy(src_ref=p_ref, dst_ref=land.at[4 * px + 2 * py + pc],
                                         send_sem=send_sems.at[k - 1], recv_sem=recv_sems.at[k - 1],
                                         device_id=(px, py, pc), device_id_type=MESH).wait_recv()
        for cp in sends:
            cp.wait_send()
        total = land[0]
        for d in range(1, N_DEV):
            total = total + land[d]
        o_ref[...] = total

    vm = pl.BlockSpec(memory_space=pltpu.VMEM)
    return pl.pallas_call(
        body, name="sum_devices", in_specs=[vm], out_specs=vm,
        out_shape=jax.ShapeDtypeStruct(packed.shape, F32),
        scratch_shapes=[pltpu.VMEM((N_DEV,) + packed.shape, F32), pltpu.SemaphoreType.DMA((N_DEV - 1,)),
                        pltpu.SemaphoreType.DMA((N_DEV - 1,))],
    )(packed)


def _adamw_math(w, g, m, v):
    m = ADAM_B1 * m + (1.0 - ADAM_B1) * g
    v = ADAM_B2 * v + (1.0 - ADAM_B2) * jnp.square(g)
    m_hat = m / (1.0 - ADAM_B1 ** ADAM_STEP)
    v_hat = v / (1.0 - ADAM_B2 ** ADAM_STEP)
    delta = -ADAM_LR * (m_hat / (jnp.sqrt(v_hat) + ADAM_EPS) + ADAM_WD * w)
    return delta, m, v


def _adamw_large(layer, w, halves, m, v, other):
    _, rows, cols = w.shape
    tr = min(rows // 2, 256)
    per = rows // 2 // tr

    def body(w_ref, g_ref, m_ref, v_ref, *rest):
        g_out, d_ref, nm_ref, nv_ref = rest[-4:]
        g = g_ref[...]
        g_out[...] = g
        d_ref[...], nm_ref[...], nv_ref[...] = _adamw_math(w_ref[...], g, m_ref[...], v_ref[...])

    blk = pl.BlockSpec((None, tr, cols), lambda i: (layer, i, 0))
    half = pl.BlockSpec((None, tr, cols), lambda i: (i // per, i % per, 0))
    kept = [] if other is None else list(other)
    return pl.pallas_call(
        body, name="adamw_large", grid=(rows // tr,),
        in_specs=[blk, half, blk, blk] + [pl.BlockSpec(memory_space=pl.ANY)] * len(kept), out_specs=[blk] * 4,
        out_shape=[jax.ShapeDtypeStruct(w.shape, F32)] * 4,
        input_output_aliases={4 + k: k for k in range(len(kept))},
        compiler_params=_params("parallel"),
    )(w, halves, m, v, *kept)


def _adamw_small(ws, gs, ms, vs):
    n = len(ws)

    def body(*refs):
        w_r, g_r, m_r, v_r = refs[:n], refs[n:2 * n], refs[2 * n:3 * n], refs[3 * n:4 * n]
        d_r, nm_r, nv_r = refs[4 * n:5 * n], refs[5 * n:6 * n], refs[6 * n:]
        for a in range(n):
            d_r[a][...], nm_r[a][...], nv_r[a][...] = _adamw_math(w_r[a][...], g_r[a][...], m_r[a][...], v_r[a][...])

    vm = pl.BlockSpec(memory_space=pltpu.VMEM)
    outs = pl.pallas_call(
        body, name="adamw_small", in_specs=[vm] * (4 * n), out_specs=[vm] * (3 * n),
        out_shape=[jax.ShapeDtypeStruct(w.shape, F32) for w in ws] * 3,
    )(*ws, *gs, *ms, *vs)
    return outs[:n], outs[n:2 * n], outs[2 * n:]


_LARGE = ("w_in", "w_out", "w_up", "w_down")
_SMALL = ("meta_tokens", "mix_pre_g", "conv_w", "sinks", "attn_out_g", "conv_out_g", "mix_post_g", "mlp_pre_g",
          "mlp_post_g")
_ORDER = ("meta_tokens", "mix_pre_g", "w_in", "conv_w", "sinks", "attn_out_g", "conv_out_g", "w_out", "mix_post_g",
          "mlp_pre_g", "w_up", "w_down", "mlp_post_g")


class _Reduce:
    def __init__(self, name, grads, after=()):
        self.name, self.n = name, len(grads)
        half_rows = [g.shape[1] // 2 for g in grads]
        zones = [lax.empty((N_CHIPS, hr, g.shape[2]), F32) for g, hr in zip(grads, half_rows)]
        self.exchange = _Exchange(name + "_swap", list(grads) + zones, _swap_plan(self.n, half_rows), self.n, after)

    @property
    def token(self):
        return self.exchange.token

    def scatter(self, after):
        bufs = self.exchange.wait(after)
        sums = [_add_half(g, r) for g, r in zip(bufs[:self.n], bufs[self.n:])]
        zones = [lax.empty((N_CHIPS - 1,) + s.shape[1:], BF16) for s in sums]
        self.exchange = _Exchange(self.name + "_scatter", sums + zones, _scatter_plan(self.n), 3 * self.n)

    def join(self, after):
        bufs = self.exchange.wait(after)
        halves = [_sum_chips(s, q) for s, q in zip(bufs[:self.n], bufs[self.n:])]
        self.exchange = _Exchange(self.name + "_join", halves, _join_plan(self.n), self.n)

    def done(self, after):
        return self.exchange.wait(after)


def _pad_cols(a, n=D_MODEL):
    return jnp.pad(a, ((0, 0), (0, n - a.shape[1])))


def kernel(x, meta_tokens, mix_pre_g, w_in, conv_w, sinks, attn_out_g, conv_out_g, w_out, mix_post_g, mlp_pre_g, w_up, w_down, mlp_post_g, loss_target, m_meta_tokens, m_mix_pre_g, m_w_in, m_conv_w, m_sinks, m_attn_out_g, m_conv_out_g, m_w_out, m_mix_post_g, m_mlp_pre_g, m_w_up, m_w_down, m_mlp_post_g, v_meta_tokens, v_mix_pre_g, v_w_in, v_conv_w, v_sinks, v_attn_out_g, v_conv_out_g, v_w_out, v_mix_post_g, v_mlp_pre_g, v_w_up, v_w_down, v_mlp_post_g):
    w = dict(meta_tokens=meta_tokens, mix_pre_g=mix_pre_g, w_in=w_in, conv_w=conv_w, sinks=sinks,
             attn_out_g=attn_out_g, conv_out_g=conv_out_g, w_out=w_out, mix_post_g=mix_post_g, mlp_pre_g=mlp_pre_g,
             w_up=w_up, w_down=w_down, mlp_post_g=mlp_post_g)
    m = dict(meta_tokens=m_meta_tokens, mix_pre_g=m_mix_pre_g, w_in=m_w_in, conv_w=m_conv_w, sinks=m_sinks,
             attn_out_g=m_attn_out_g, conv_out_g=m_conv_out_g, w_out=m_w_out, mix_post_g=m_mix_post_g,
             mlp_pre_g=m_mlp_pre_g, w_up=m_w_up, w_down=m_w_down, mlp_post_g=m_mlp_post_g)
    v = dict(meta_tokens=v_meta_tokens, mix_pre_g=v_mix_pre_g, w_in=v_w_in, conv_w=v_conv_w, sinks=v_sinks,
             attn_out_g=v_attn_out_g, conv_out_g=v_conv_out_g, w_out=v_w_out, mix_post_g=v_mix_post_g,
             mlp_pre_g=v_mlp_pre_g, w_up=v_w_up, w_down=v_w_down, mlp_post_g=v_mlp_post_g)
    chip = 2 * lax.axis_index("x") + lax.axis_index("y")
    tl = _Tiles(x.shape[1] + BLOCK)

    def zone(quarter):
        return lax.dynamic_update_slice(lax.empty((N_CHIPS,) + quarter.shape, quarter.dtype), quarter[None],
                                        (chip,) + (0,) * quarter.ndim)

    zones = {n: [zone(w[n][l].astype(BF16)) for l in range(DEPTH)] for n in _LARGE}
    first = _Exchange("gather_first", [zones["w_in"][0], zone(w["conv_w"]), zone(w["meta_tokens"])], _gather_plan(3), 9)
    rest = _Exchange("gather_rest", [zones[n][0] for n in ("w_out", "w_up", "w_down")], _gather_plan(3), 9,
                     [first.token])

    def whole_in(quarters):
        return jnp.transpose(quarters, (1, 0, 2)).reshape(D_MODEL, IN_W)

    q_in, q_conv, q_meta = first.wait(rest.token)
    conv_whole = jnp.transpose(q_conv, (1, 2, 0, 3)).reshape(DEPTH, CONV_K, CONV_W)
    meta = jnp.transpose(q_meta, (1, 0, 2)).reshape(N_META, D_MODEL)
    p = [{"conv_w": conv_whole[l], "sinks": w["sinks"][l]} for l in range(DEPTH)]
    for l in range(DEPTH):
        for n in ("mix_pre_g", "attn_out_g", "conv_out_g", "mix_post_g", "mlp_pre_g", "mlp_post_g"):
            p[l][n] = w[n][l][None, :]

    h = jnp.concatenate([jnp.zeros((LEAD_PAD, D_MODEL), F32), meta, x[0]], axis=0)
    p[0]["w_in"] = whole_in(q_in)
    mixed = _mixer_fwd(h, p[0], tl)
    second = _Exchange("gather_second", [zones["w_in"][1], zones["w_out"][1]], _gather_plan(2), 6, [mixed[-1]])
    second_mlp = _Exchange("gather_second_mlp", [zones["w_up"][1], zones["w_down"][1]], _gather_plan(2), 6,
                           [second.token])
    p[0]["w_out"], p[0]["w_up"], p[0]["w_down"] = rest.wait(second_mlp.token)
    h1, saved0 = _out_fwd(mixed, p[0], tl)
    h, saved0 = _mlp_fwd(h1, saved0, p[0], tl)
    q_in, p[1]["w_out"] = second.wait(h)
    p[1]["w_in"] = whole_in(q_in)
    h1, saved1 = _out_fwd(_mixer_fwd(h, p[1], tl), p[1], tl)
    p[1]["w_up"], p[1]["w_down"] = second_mlp.wait(h1)
    h, saved1 = _mlp_fwd(h1, saved1, p[1], tl)
    loss_tile, dh = _loss_head(h, loss_target[0], tl.tm)
    loss = lax.psum(loss_tile[0, 0], ("x", "y", "c"))

    def adamw(layer, halves, other):
        return {n: _adamw_large(layer, w[n], halves[n], m[n], v[n], None if other is None else other[n])
                for n in halves}

    dh1, g1 = _mlp_part_bwd(dh, saved1, p[1], tl)
    carry, gm = _mix_out_part_bwd(dh1, saved1, p[1], tl)
    dh, gi = _attn_in_part_bwd(carry, saved1, p[1], tl)
    g1.update(gm, **gi)
    red1 = _Reduce("reduce1", [g1[n] for n in _LARGE])
    dh1, g0 = _mlp_part_bwd(dh, saved0, p[0], tl, [red1.token])
    red1.scatter(dh1)
    red0a = _Reduce("reduce0a", [g0["w_up"], g0["w_down"]], [red1.token])
    carry, gm = _mix_out_part_bwd(dh1, saved0, p[0], tl, [red0a.token])
    red1.join(carry[1])
    red0a.scatter(red1.token)
    dh0, gi = _attn_in_part_bwd(carry, saved0, p[0], tl, [red0a.token])
    g0.update(gm, **gi)
    red0b = _Reduce("reduce0b", [g0["w_in"], g0["w_out"]])
    done1 = adamw(1, dict(zip(_LARGE, red1.done(red0b.token))), None)
    red0a.join(done1["w_down"][0])
    red0b.scatter(red0a.token)
    halves0 = dict(zip(("w_up", "w_down"), red0a.done(red0b.token)))
    done0 = adamw(0, halves0, done1)
    red0b.join(done0["w_down"][0])
    done0.update(adamw(0, dict(zip(("w_in", "w_out"), red0b.done(red0b.token))), done1))
    grad_x = dh0[BLOCK:][None]
    grads = {n: [g0[n], g1[n]] for n in g0 if n not in _LARGE}

    rows = [dh0[LEAD_PAD:BLOCK]]
    for n in ("mix_pre_g", "mix_post_g", "mlp_pre_g", "mlp_post_g"):
        rows += grads[n]
    rows += [jnp.concatenate([grads["attn_out_g"][l], grads["conv_out_g"][l]], axis=1) for l in range(DEPTH)]
    rows.append(jnp.concatenate(grads["conv_w"], axis=1))
    rows.append(_pad_cols(jnp.concatenate(grads["sinks"])[None, :]))
    packed = jnp.concatenate(rows, axis=0)
    packed = jnp.pad(packed, ((0, SMALL_ROWS - packed.shape[0]), (0, 0)))
    total = _sum_devices(packed)
    r0 = N_META
    small = {
        "meta_tokens": lax.dynamic_slice(total[:N_META], (0, chip * (D_MODEL // N_CHIPS)), (N_META, D_MODEL // N_CHIPS)),
        "mix_pre_g": total[r0:r0 + 2], "mix_post_g": total[r0 + 2:r0 + 4], "mlp_pre_g": total[r0 + 4:r0 + 6],
        "mlp_post_g": total[r0 + 6:r0 + 8],
        "attn_out_g": total[r0 + 8:r0 + 10, :ATTN_W], "conv_out_g": total[r0 + 8:r0 + 10, ATTN_W:],
        "conv_w": lax.dynamic_slice(total[r0 + 10:r0 + 13].reshape(CONV_K, DEPTH, CONV_W).transpose(1, 0, 2),
                                    (0, 0, chip * (CONV_W // N_CHIPS)), (DEPTH, CONV_K, CONV_W // N_CHIPS)),
        "sinks": total[r0 + 13, :DEPTH * N_Q_HEADS].reshape(DEPTH, N_Q_HEADS),
    }

    grad, delta, new_m, new_v = {}, {}, {}, {}
    for n in _LARGE:
        grad[n], delta[n], new_m[n], new_v[n] = done0[n]
    ds, nms, nvs = _adamw_small([w[n] for n in _SMALL], [small[n] for n in _SMALL], [m[n] for n in _SMALL],
                                [v[n] for n in _SMALL])
    for i, n in enumerate(_SMALL):
        grad[n], delta[n], new_m[n], new_v[n] = small[n], ds[i], nms[i], nvs[i]
    return (loss, grad_x, *[grad[n] for n in _ORDER], *[delta[n] for n in _ORDER], *[new_m[n] for n in _ORDER],
            *[new_v[n] for n in _ORDER])
```

```python
import functools

import jax
import jax.numpy as jnp
from jax import lax
from jax.experimental import pallas as pl
from jax.experimental.pallas import tpu as pltpu

F32 = jnp.float32
BF16 = jnp.bfloat16

D_MODEL = 1024
DEPTH = 2
N_META = 16
ATTN_W = 512
CONV_W = 512
HEAD_DIM = 64
N_Q_HEADS = 8
N_KV_HEADS = 2
GROUP = N_Q_HEADS // N_KV_HEADS
KV_W = N_KV_HEADS * HEAD_DIM
CONV_K = 3
BLOCK = 128
LEAD_PAD = BLOCK - N_META
ROPE_THETA = 500000.0
ROT_DIM = HEAD_DIM // 4
ROT_HALF = ROT_DIM // 2
D_FF = 4 * D_MODEL
IN_W = ATTN_W + 2 * KV_W + 3 * CONV_W
QKV_W = ATTN_W + 2 * KV_W
EPS = 1e-6
SCALE = HEAD_DIM ** -0.5
FF_CHUNK = 1024
N_CHIPS = 4
N_DEV = 8

ADAM_LR = 0.001
ADAM_B1 = 0.9
ADAM_B2 = 0.999
ADAM_EPS = 1e-08
ADAM_WD = 0.01
ADAM_STEP = 10

V7X_VMEM_LIMIT = 56 * 1024 * 1024
SMALL_ROWS = 32

MESH = pl.DeviceIdType.MESH


def _params(*sem):
    return pltpu.CompilerParams(dimension_semantics=sem, vmem_limit_bytes=V7X_VMEM_LIMIT)


def _row_tile(t, most):
    nb = t // BLOCK
    for b in range(most // BLOCK, 0, -1):
        if nb % b == 0:
            return b * BLOCK
    return BLOCK


def _behind(body, deps):
    n = len(deps)

    def wrapped(*refs):
        body(*refs[n:])

    return wrapped, [pl.BlockSpec(memory_space=pl.ANY)] * n


def _rms(x, g):
    r = lax.rsqrt(jnp.mean(x * x, axis=-1, keepdims=True) + EPS)
    return x * r * g


def _rms_bwd(dy, x, g):
    r = lax.rsqrt(jnp.mean(x * x, axis=-1, keepdims=True) + EPS)
    xh = x * r
    dg = jnp.sum(dy * xh, axis=0, keepdims=True)
    dxh = dy * g
    dx = r * (dxh - xh * jnp.mean(dxh * xh, axis=-1, keepdims=True))
    return dx, dg


def _rope(x, cos, sa, sb):
    n = x.shape[-1]
    return x * cos + pltpu.roll(x, n - ROT_HALF, 1) * sa + pltpu.roll(x, ROT_HALF, 1) * sb


def _rope_bwd(dy, cos, sa, sb):
    n = dy.shape[-1]
    return dy * cos + pltpu.roll(dy * sa, ROT_HALF, 1) + pltpu.roll(dy * sb, n - ROT_HALF, 1)


def _rope_tables(t):
    pos = lax.broadcasted_iota(jnp.int32, (t, 2 * HEAD_DIM), 0).astype(F32) - LEAD_PAD
    dim = lax.broadcasted_iota(jnp.int32, (t, 2 * HEAD_DIM), 1) % HEAD_DIM
    pair = (dim % ROT_HALF).astype(F32)
    inv_freq = jnp.power(jnp.float32(ROPE_THETA), -(2.0 * pair) / ROT_DIM)
    ang = pos * inv_freq
    cos, sin = jnp.cos(ang), jnp.sin(ang)
    return (jnp.where(dim < ROT_DIM, cos, 1.0), jnp.where(dim < ROT_HALF, -sin, 0.0),
            jnp.where((dim >= ROT_HALF) & (dim < ROT_DIM), sin, 0.0))


def _in_proj(h, g, w, tabs, tm):
    t = h.shape[0]

    def body(h_ref, g_ref, w_ref, c_ref, sa_ref, sb_ref, a_ref, q_ref, k_ref, v_ref, b_ref, cg_ref, hc_ref):
        a = _rms(h_ref[...], g_ref[...]).astype(BF16)
        a_ref[...] = a
        p = jnp.dot(a, w_ref[...], preferred_element_type=F32)
        cos, sa, sb = c_ref[...], sa_ref[...], sb_ref[...]
        rep = ATTN_W // (2 * HEAD_DIM)
        q = _rope(p[:, :ATTN_W], jnp.tile(cos, (1, rep)), jnp.tile(sa, (1, rep)), jnp.tile(sb, (1, rep)))
        q_ref[...] = (q * SCALE).astype(BF16)
        k_ref[...] = _rope(p[:, ATTN_W:ATTN_W + KV_W], cos, sa, sb).astype(BF16)
        v_ref[...] = p[:, ATTN_W + KV_W:QKV_W].astype(BF16)
        b_ref[...] = p[:, QKV_W:QKV_W + CONV_W]
        cg_ref[...] = p[:, QKV_W + CONV_W:QKV_W + 2 * CONV_W]
        hc_ref[...] = p[:, QKV_W + 2 * CONV_W:]

    row = lambda n: pl.BlockSpec((tm, n), lambda i: (i, 0))
    full = lambda a: pl.BlockSpec(a.shape, lambda i: (0, 0))
    return pl.pallas_call(
        body, name="in_proj", grid=(t // tm,),
        in_specs=[row(D_MODEL), full(g), full(w), row(2 * HEAD_DIM), row(2 * HEAD_DIM), row(2 * HEAD_DIM)],
        out_specs=[row(D_MODEL), row(ATTN_W), row(KV_W), row(KV_W), row(CONV_W), row(CONV_W), row(CONV_W)],
        out_shape=[jax.ShapeDtypeStruct((t, D_MODEL), BF16), jax.ShapeDtypeStruct((t, ATTN_W), BF16),
                   jax.ShapeDtypeStruct((t, KV_W), BF16), jax.ShapeDtypeStruct((t, KV_W), BF16),
                   jax.ShapeDtypeStruct((t, CONV_W), F32), jax.ShapeDtypeStruct((t, CONV_W), F32),
                   jax.ShapeDtypeStruct((t, CONV_W), F32)],
        compiler_params=_params("parallel"),
    )(h, g, w, *tabs)


def _attn_bias():
    r = lax.broadcasted_iota(jnp.int32, (3, BLOCK, 2 * BLOCK), 1)
    c = lax.broadcasted_iota(jnp.int32, (3, BLOCK, 2 * BLOCK), 2)
    i = lax.broadcasted_iota(jnp.int32, (3, BLOCK, 2 * BLOCK), 0)
    ok = (c > r) & (c <= r + BLOCK) & (c + (i - 1) * BLOCK >= LEAD_PAD)
    return jnp.where(ok, 0.0, -jnp.inf).astype(F32)


def _attn_scores(qh, kg, bias):
    return lax.dot_general(qh, kg, (((1,), (1,)), ((), ())), preferred_element_type=F32) + bias


def _attn_probs(s, sk):
    m = jnp.maximum(jnp.max(s, axis=-1, keepdims=True), sk)
    e = jnp.exp(s - m)
    es = jnp.exp(sk - m)
    rden = 1.0 / (jnp.sum(e, axis=-1, keepdims=True) + es)
    return e * rden, es * rden


def _head(hh):
    return slice(hh * HEAD_DIM, (hh + 1) * HEAD_DIM)


def _two_blocks(ref, i):
    prev = jnp.maximum(i - 1, 0)
    return jnp.concatenate([ref[pl.ds(pl.multiple_of(prev * BLOCK, BLOCK), BLOCK), :],
                            ref[pl.ds(pl.multiple_of(i * BLOCK, BLOCK), BLOCK), :]], axis=0)


def _attn_fwd(q, k, v, bias, sinks, tm):
    t = q.shape[0]
    per_step = tm // BLOCK
    heads = range(N_Q_HEADS)

    def body(s_ref, q_ref, k_ref, v_ref, bias_ref, o_ref):
        for b in range(per_step):
            i = pl.program_id(0) * per_step + b
            rows = slice(b * BLOCK, (b + 1) * BLOCK)
            kc, vc = _two_blocks(k_ref, i), _two_blocks(v_ref, i)
            bias_i = bias_ref[jnp.minimum(i, 2)]
            scores = [_attn_scores(q_ref[rows, _head(hh)], kc[:, _head(hh // GROUP)], bias_i) for hh in heads]
            probs = [_attn_probs(scores[hh], s_ref[hh])[0].astype(BF16) for hh in heads]
            for hh in heads:
                o_ref[rows, _head(hh)] = jnp.dot(probs[hh], vc[:, _head(hh // GROUP)], preferred_element_type=F32)

    whole = pl.BlockSpec((t, KV_W), lambda i: (0, 0))
    return pl.pallas_call(
        body, name="attn_fwd", grid=(t // tm,),
        in_specs=[pl.BlockSpec(memory_space=pltpu.SMEM), pl.BlockSpec((tm, ATTN_W), lambda i: (i, 0)), whole, whole,
                  pl.BlockSpec(bias.shape, lambda i: (0, 0, 0))],
        out_specs=pl.BlockSpec((tm, ATTN_W), lambda i: (i, 0)),
        out_shape=jax.ShapeDtypeStruct((t, ATTN_W), F32),
        compiler_params=_params("parallel"),
    )(sinks, q, k, v, bias)


def _shift_rows(u, halo, n):
    r = pltpu.roll(u, n, 0)
    hr = pltpu.roll(halo, n, 0)
    idx = lax.broadcasted_iota(jnp.int32, hr.shape, 0)
    return jnp.concatenate([jnp.where(idx < n, hr, r[:8]), r[8:]], axis=0)


def _advance_rows(u, halo, n):
    rows = u.shape[0]
    r = pltpu.roll(u, rows - n, 0)
    hr = pltpu.roll(halo, 8 - n, 0)
    idx = lax.broadcasted_iota(jnp.int32, hr.shape, 0)
    return jnp.concatenate([r[:rows - 8], jnp.where(idx >= 8 - n, hr, r[rows - 8:])], axis=0)


def _mix_out(h, o, b, c, hc, cw, ga, gc, w, gp, tm, deps=()):
    t = h.shape[0]

    def body(h_ref, o_ref, b_ref, c_ref, hc_ref, cw_ref, ga_ref, gc_ref, w_ref, gp_ref, h1_ref, y_ref, z_ref, halo):
        @pl.when(pl.program_id(0) == 0)
        def _():
            halo[...] = jnp.zeros_like(halo)

        u = c_ref[...] * hc_ref[...]
        cv = cw_ref[0:1, :] * _shift_rows(u, halo[...], 2) + cw_ref[1:2, :] * _shift_rows(u, halo[...], 1) \
            + cw_ref[2:3, :] * u
        halo[...] = u[tm - 8:]
        yc = b_ref[...] * cv
        y = jnp.concatenate([_rms(o_ref[...], ga_ref[...]), _rms(yc, gc_ref[...])], axis=1).astype(BF16)
        y_ref[...] = y
        z = jnp.dot(y, w_ref[...].reshape(D_MODEL, D_MODEL), preferred_element_type=F32)
        z_ref[...] = z
        h1_ref[...] = h_ref[...] + _rms(z, gp_ref[...])

    row = lambda n: pl.BlockSpec((tm, n), lambda i: (i, 0))
    full = lambda a: pl.BlockSpec(a.shape, lambda i: (0,) * a.ndim)
    body, dep_specs = _behind(body, deps)
    return pl.pallas_call(
        body, name="mix_out", grid=(t // tm,),
        in_specs=dep_specs + [row(D_MODEL), row(ATTN_W), row(CONV_W), row(CONV_W), row(CONV_W), full(cw), full(ga),
                              full(gc), full(w), full(gp)],
        out_specs=[row(D_MODEL), row(D_MODEL), row(D_MODEL)],
        out_shape=[jax.ShapeDtypeStruct((t, D_MODEL), F32), jax.ShapeDtypeStruct((t, D_MODEL), BF16),
                   jax.ShapeDtypeStruct((t, D_MODEL), F32)],
        scratch_shapes=[pltpu.VMEM((8, CONV_W), F32)],
        compiler_params=_params("arbitrary"),
    )(*deps, h, o, b, c, hc, cw, ga, gc, w, gp)


def _mlp(h1, g1, wu, wd, g2, tm):
    t = h1.shape[0]
    nj = D_FF // FF_CHUNK

    def body(h1_ref, g1_ref, wu_ref, wd_ref, g2_ref, h2_ref, a2_ref, act_ref, f_ref, acc):
        j = pl.program_id(1)

        @pl.when(j == 0)
        def _():
            a2_ref[...] = _rms(h1_ref[...], g1_ref[...]).astype(BF16)

        up = jnp.dot(a2_ref[...], wu_ref[...], preferred_element_type=F32)
        act = jnp.square(jnp.maximum(up, 0.0)).astype(BF16)
        act_ref[...] = act
        part = jnp.dot(act, wd_ref[...], preferred_element_type=F32)

        @pl.when(j == 0)
        def _():
            acc[...] = part

        @pl.when(j > 0)
        def _():
            acc[...] += part

        @pl.when(j == nj - 1)
        def _():
            f = acc[...]
            f_ref[...] = f
            h2_ref[...] = h1_ref[...] + _rms(f, g2_ref[...])

    row = pl.BlockSpec((tm, D_MODEL), lambda i, j: (i, 0))
    vec = pl.BlockSpec((1, D_MODEL), lambda i, j: (0, 0))
    quarter = pl.BlockSpec((None, D_MODEL, FF_CHUNK), lambda i, j: (j, 0, 0))
    return pl.pallas_call(
        body, name="mlp", grid=(t // tm, nj),
        in_specs=[row, vec, quarter, quarter, vec],
        out_specs=[row, row, pl.BlockSpec((tm, FF_CHUNK), lambda i, j: (i, j)), row],
        out_shape=[jax.ShapeDtypeStruct((t, D_MODEL), F32), jax.ShapeDtypeStruct((t, D_MODEL), BF16),
                   jax.ShapeDtypeStruct((t, D_FF), BF16), jax.ShapeDtypeStruct((t, D_MODEL), F32)],
        scratch_shapes=[pltpu.VMEM((tm, D_MODEL), F32)],
        compiler_params=_params("parallel", "arbitrary"),
    )(h1, g1, wu, wd, g2)


def _loss_head(h, target, tm):
    t = h.shape[0]
    per_step = tm // BLOCK

    def body(h_ref, *rest):
        t_refs, (loss_ref, dh_ref) = rest[:per_step], rest[per_step:]
        i = pl.program_id(0)

        @pl.when(i == 0)
        def _():
            loss_ref[...] = jnp.zeros_like(loss_ref)

        total = jnp.zeros((), F32)
        for b in range(per_step):
            rows = slice(b * BLOCK, (b + 1) * BLOCK)
            err = h_ref[rows, :] - t_refs[b][...]
            if b == 0:
                err = jnp.where(i == 0, 0.0, err)
            dh_ref[rows, :] = err * (1.0 / D_MODEL)
            total = total + jnp.sum(err * err)
        loss_ref[...] += total * (0.5 / D_MODEL)

    def target_block(b):
        return pl.BlockSpec((BLOCK, D_MODEL), lambda i: (jnp.maximum(i * per_step + b - 1, 0), 0))

    return pl.pallas_call(
        body, name="loss_head", grid=(t // tm,),
        in_specs=[pl.BlockSpec((tm, D_MODEL), lambda i: (i, 0))] + [target_block(b) for b in range(per_step)],
        out_specs=[pl.BlockSpec((8, 128), lambda i: (0, 0)), pl.BlockSpec((tm, D_MODEL), lambda i: (i, 0))],
        out_shape=[jax.ShapeDtypeStruct((8, 128), F32), jax.ShapeDtypeStruct((t, D_MODEL), F32)],
        compiler_params=_params("arbitrary"),
    )(h, *([target] * per_step))


def _mlp_bwd(dh2, f, g2, act, wd, wu, h1, g1, tm, deps=()):
    t = dh2.shape[0]
    nj = D_FF // FF_CHUNK

    def body(dh2_ref, f_ref, g2_ref, act_ref, wd_ref, wu_ref, h1_ref, g1_ref, dh1_ref, df_ref, dup_ref, dg2_ref,
             dg1_ref, acc):
        i, j = pl.program_id(0), pl.program_id(1)

        @pl.when((i == 0) & (j == 0))
        def _():
            dg2_ref[...] = jnp.zeros_like(dg2_ref)
            dg1_ref[...] = jnp.zeros_like(dg1_ref)

        @pl.when(j == 0)
        def _():
            df, dg = _rms_bwd(dh2_ref[...], f_ref[...], g2_ref[...])
            df_ref[...] = df.astype(BF16)
            dg2_ref[...] += dg

        dact = lax.dot_general(df_ref[...], wd_ref[...], (((1,), (1,)), ((), ())), preferred_element_type=F32)
        dup = (dact * (2.0 * jnp.sqrt(act_ref[...].astype(F32)))).astype(BF16)
        dup_ref[...] = dup
        part = lax.dot_general(dup, wu_ref[...], (((1,), (1,)), ((), ())), preferred_element_type=F32)

        @pl.when(j == 0)
        def _():
            acc[...] = part

        @pl.when(j > 0)
        def _():
            acc[...] += part

        @pl.when(j == nj - 1)
        def _():
            dx, dg = _rms_bwd(acc[...], h1_ref[...], g1_ref[...])
            dh1_ref[...] = dh2_ref[...] + dx
            dg1_ref[...] += dg

    row = pl.BlockSpec((tm, D_MODEL), lambda i, j: (i, 0))
    vec = pl.BlockSpec((1, D_MODEL), lambda i, j: (0, 0))
    chunk = pl.BlockSpec((tm, FF_CHUNK), lambda i, j: (i, j))
    quarter = pl.BlockSpec((None, D_MODEL, FF_CHUNK), lambda i, j: (j, 0, 0))
    body, dep_specs = _behind(body, deps)
    return pl.pallas_call(
        body, name="mlp_bwd", grid=(t // tm, nj),
        in_specs=dep_specs + [row, row, vec, chunk, quarter, quarter, row, vec],
        out_specs=[row, row, chunk, vec, vec],
        out_shape=[jax.ShapeDtypeStruct((t, D_MODEL), F32), jax.ShapeDtypeStruct((t, D_MODEL), BF16),
                   jax.ShapeDtypeStruct((t, D_FF), BF16), jax.ShapeDtypeStruct((1, D_MODEL), F32),
                   jax.ShapeDtypeStruct((1, D_MODEL), F32)],
        scratch_shapes=[pltpu.VMEM((tm, D_MODEL), F32)],
        compiler_params=_params("arbitrary", "arbitrary"),
    )(*deps, dh2, f, g2, act, wd, wu, h1, g1)


def _row_split(t):
    tile = min(t, 1024)
    return tile, t // tile, t % tile


def _row_split_specs(t, cols, col_of):
    tile, whole, rest = _row_split(t)
    specs = [pl.BlockSpec((tile, cols), lambda *g: (jnp.minimum(g[-1], whole - 1), col_of(*g[:-1])))]
    if rest:
        specs.append(pl.BlockSpec((rest, cols), lambda *g: (whole * tile // rest, col_of(*g[:-1]))))
    return specs


def _weight_grad(x, y, name):
    t, k = x.shape
    n = y.shape[1]
    tk = tn = FF_CHUNK
    _, whole, rest = _row_split(t)
    steps = whole + bool(rest)

    def body(*refs):
        o_ref, ob_ref, r = refs[-2], refs[-1], pl.program_id(2)

        @pl.when(r == 0)
        def _():
            o_ref[...] = jnp.zeros_like(o_ref)

        def add(x_ref, y_ref):
            o_ref[...] += lax.dot_general(x_ref[...], y_ref[...], (((0,), (0,)), ((), ())),
                                          preferred_element_type=F32)

        if rest:
            pl.when(r < whole)(lambda: add(refs[0], refs[2]))
            pl.when(r == whole)(lambda: add(refs[1], refs[3]))
        else:
            add(refs[0], refs[1])

        @pl.when(r == steps - 1)
        def _():
            ob_ref[...] = o_ref[...].astype(BF16)

    tile = pl.BlockSpec((None, None, tk, tn), lambda a, b, r: (a, b, 0, 0))
    return pl.pallas_call(
        body, name=name, grid=(k // tk, n // tn, steps),
        in_specs=_row_split_specs(t, tk, lambda a, b: a) + _row_split_specs(t, tn, lambda a, b: b),
        out_specs=[tile, tile],
        out_shape=[jax.ShapeDtypeStruct((k // tk, n // tn, tk, tn), F32),
                   jax.ShapeDtypeStruct((k // tk, n // tn, tk, tn), BF16)],
        compiler_params=_params("parallel", "parallel", "arbitrary"),
    )(*([x] * (1 + bool(rest))), *([y] * (1 + bool(rest))))


def _weight_grad_in(a, dproj):
    t = a.shape[0]
    _, whole, rest = _row_split(t)
    qw = IN_W // N_CHIPS

    def body(*refs):
        o_ref, ob_ref, acc, r = refs[-3], refs[-2], refs[-1], pl.program_id(0)

        @pl.when(r == 0)
        def _():
            acc[...] = jnp.zeros_like(acc)

        def add(a_ref, d_ref):
            acc[...] += lax.dot_general(a_ref[...], d_ref[...], (((0,), (0,)), ((), ())),
                                        preferred_element_type=F32)

        if rest:
            pl.when(r < whole)(lambda: add(refs[0], refs[2]))
            pl.when(r == whole)(lambda: add(refs[1], refs[3]))
        else:
            add(refs[0], refs[1])

        @pl.when(r == whole + bool(rest) - 1)
        def _():
            for s in range(N_CHIPS):
                quarter = acc[:, s * qw:(s + 1) * qw]
                o_ref[s] = quarter
                ob_ref[s] = quarter.astype(BF16)

    vm = pl.BlockSpec(memory_space=pltpu.VMEM)
    return pl.pallas_call(
        body, name="grad_w_in", grid=(whole + bool(rest),),
        in_specs=_row_split_specs(t, D_MODEL, lambda: 0) + _row_split_specs(t, IN_W, lambda: 0),
        out_specs=[vm, vm],
        out_shape=[jax.ShapeDtypeStruct((N_CHIPS, D_MODEL, qw), F32),
                   jax.ShapeDtypeStruct((N_CHIPS, D_MODEL, qw), BF16)],
        scratch_shapes=[pltpu.VMEM((D_MODEL, IN_W), F32)],
        compiler_params=_params("arbitrary"),
    )(*([a] * (1 + bool(rest))), *([dproj] * (1 + bool(rest))))


def _mix_out_bwd(dh1, z, gp, w, o, b, c, hc, cw, ga, gc, tm, deps=()):
    t = dh1.shape[0]
    nt = t // tm
    per8 = tm // 8

    def body(dh1_ref, z_ref, gp_ref, w_ref, o_ref, b_ref, c_ref, hc_ref, cp_ref, hp_ref, cw_ref, ga_ref, gc_ref,
             dz_ref, do_ref, dbch_ref, dgp_ref, dga_ref, dgc_ref, dcw_ref, halo):
        i = pl.program_id(0)

        @pl.when(i == 0)
        def _():
            halo[...] = jnp.zeros_like(halo)
            dgp_ref[...] = jnp.zeros_like(dgp_ref)
            dga_ref[...] = jnp.zeros_like(dga_ref)
            dgc_ref[...] = jnp.zeros_like(dgc_ref)
            dcw_ref[...] = jnp.zeros_like(dcw_ref)

        dz, dgp = _rms_bwd(dh1_ref[...], z_ref[...], gp_ref[...])
        dgp_ref[...] += dgp
        dz = dz.astype(BF16)
        dz_ref[...] = dz
        dy = lax.dot_general(dz, w_ref[...].reshape(D_MODEL, D_MODEL), (((1,), (1,)), ((), ())),
                             preferred_element_type=F32)
        do, dga = _rms_bwd(dy[:, :ATTN_W], o_ref[...], ga_ref[...])
        do_ref[...] = do
        dga_ref[...] += dga

        u = c_ref[...] * hc_ref[...]
        first = i == nt - 1
        u_before = jnp.where(first, 0.0, cp_ref[...] * hp_ref[...])
        u1 = _shift_rows(u, u_before, 1)
        u2 = _shift_rows(u, u_before, 2)
        cv = cw_ref[0:1, :] * u2 + cw_ref[1:2, :] * u1 + cw_ref[2:3, :] * u
        bb = b_ref[...]
        dyc, dgc = _rms_bwd(dy[:, ATTN_W:], bb * cv, gc_ref[...])
        dgc_ref[...] += dgc
        dcv = dyc * bb
        d1 = _advance_rows(dcv, halo[...], 1)
        d2 = _advance_rows(dcv, halo[...], 2)
        halo[...] = dcv[:8]
        du = cw_ref[2:3, :] * dcv + cw_ref[1:2, :] * d1 + cw_ref[0:1, :] * d2
        dbch_ref[...] = jnp.concatenate([dyc * cv, du * hc_ref[...], du * c_ref[...]], axis=1).astype(BF16)
        dcw_ref[...] += jnp.concatenate([jnp.sum(dcv * u2, axis=0, keepdims=True),
                                         jnp.sum(dcv * u1, axis=0, keepdims=True),
                                         jnp.sum(dcv * u, axis=0, keepdims=True)], axis=0)

    row = lambda n: pl.BlockSpec((tm, n), lambda i: (nt - 1 - i, 0))
    before = pl.BlockSpec((8, CONV_W), lambda i: (jnp.maximum((nt - 1 - i) * per8 - 1, 0), 0))
    full = lambda a: pl.BlockSpec(a.shape, lambda i: (0,) * a.ndim)
    vec = lambda n: pl.BlockSpec((1, n), lambda i: (0, 0))
    body, dep_specs = _behind(body, deps)
    return pl.pallas_call(
        body, name="mix_out_bwd", grid=(nt,),
        in_specs=dep_specs + [row(D_MODEL), row(D_MODEL), full(gp), full(w), row(ATTN_W), row(CONV_W), row(CONV_W),
                              row(CONV_W), before, before, full(cw), full(ga), full(gc)],
        out_specs=[row(D_MODEL), row(ATTN_W), row(3 * CONV_W), vec(D_MODEL), vec(ATTN_W), vec(CONV_W),
                   pl.BlockSpec((CONV_K, CONV_W), lambda i: (0, 0))],
        out_shape=[jax.ShapeDtypeStruct((t, D_MODEL), BF16), jax.ShapeDtypeStruct((t, ATTN_W), F32),
                   jax.ShapeDtypeStruct((t, 3 * CONV_W), BF16), jax.ShapeDtypeStruct((1, D_MODEL), F32),
                   jax.ShapeDtypeStruct((1, ATTN_W), F32), jax.ShapeDtypeStruct((1, CONV_W), F32),
                   jax.ShapeDtypeStruct((CONV_K, CONV_W), F32)],
        scratch_shapes=[pltpu.VMEM((8, CONV_W), F32)],
        compiler_params=_params("arbitrary"),
    )(*deps, dh1, z, gp, w, o, b, c, hc, c, hc, cw, ga, gc)


def _attn_bwd(q, k, v, o, do, bias, sinks, tm, deps=()):
    t = q.shape[0]
    per_step = tm // BLOCK

    def body(s_ref, q_ref, k_ref, v_ref, o_ref, do_ref, bias_ref, dq_ref, dk_ref, dv_ref, ds_ref):
        step = pl.program_id(0)

        @pl.when(step == 0)
        def _():
            ds_ref[...] = jnp.zeros_like(ds_ref)

        heads = range(N_Q_HEADS)

        def first_matmuls(b):
            i = step * per_step + b
            rows = slice(b * BLOCK, (b + 1) * BLOCK)
            kc, vc = _two_blocks(k_ref, i), _two_blocks(v_ref, i)
            bias_i = bias_ref[jnp.minimum(i, 2)]
            kgs = [kc[:, _head(g)] for g in range(N_KV_HEADS)]
            vgs = [vc[:, _head(g)] for g in range(N_KV_HEADS)]
            qs = [q_ref[rows, _head(hh)] for hh in heads]
            dos = [do_ref[rows, _head(hh)] for hh in heads]
            dosb = [d.astype(BF16) for d in dos]
            scores = [_attn_scores(qs[hh], kgs[hh // GROUP], bias_i) for hh in heads]
            dps = [lax.dot_general(dosb[hh], vgs[hh // GROUP], (((1,), (1,)), ((), ())), preferred_element_type=F32)
                   for hh in heads]
            return kgs, qs, dos, dosb, scores, dps

        dsink = [jnp.zeros((BLOCK, 1), F32) for _ in range(N_Q_HEADS)]
        ahead = None
        for b in range(per_step):
            i = step * per_step + b
            rows = slice(b * BLOCK, (b + 1) * BLOCK)
            kgs, qs, dos, dosb, scores, dps = first_matmuls(b)
            ps, dss = [], []
            for hh in heads:
                p, share = _attn_probs(scores[hh], s_ref[hh])
                drow = jnp.sum(dos[hh] * o_ref[rows, _head(hh)], axis=-1, keepdims=True)
                dss.append((p * (dps[hh] - drow)).astype(BF16))
                ps.append(p.astype(BF16))
                dsink[hh] = dsink[hh] + share * drow
            for hh in heads:
                dq_ref[rows, _head(hh)] = jnp.dot(dss[hh], kgs[hh // GROUP], preferred_element_type=F32) * SCALE
            groups = [slice(GROUP * g, GROUP * (g + 1)) for g in range(N_KV_HEADS)]
            dkg = [lax.dot_general(jnp.concatenate(dss[gr], axis=0), jnp.concatenate(qs[gr], axis=0),
                                   (((0,), (0,)), ((), ())), preferred_element_type=F32) for gr in groups]
            dvg = [lax.dot_general(jnp.concatenate(ps[gr], axis=0), jnp.concatenate(dosb[gr], axis=0),
                                   (((0,), (0,)), ((), ())), preferred_element_type=F32) for gr in groups]
            dkb, dvb = jnp.concatenate(dkg, axis=1), jnp.concatenate(dvg, axis=1)
            if b == 0:
                @pl.when(step > 0)
                def _():
                    before = pl.ds(pl.multiple_of((i - 1) * BLOCK, BLOCK), BLOCK)
                    dk_ref[before, :] += dkb[:BLOCK]
                    dv_ref[before, :] += dvb[:BLOCK]
            else:
                at = pl.ds(pl.multiple_of((i - 1) * BLOCK, BLOCK), BLOCK)
                dk_ref[at, :] = ahead[0] + dkb[:BLOCK]
                dv_ref[at, :] = ahead[1] + dvb[:BLOCK]
            ahead = (dkb[BLOCK:], dvb[BLOCK:])
        last = pl.ds(pl.multiple_of(((step + 1) * per_step - 1) * BLOCK, BLOCK), BLOCK)
        dk_ref[last, :] = ahead[0]
        dv_ref[last, :] = ahead[1]
        for hh in range(N_Q_HEADS):
            ds_ref[hh:hh + 1, :] -= jnp.sum(dsink[hh])

    whole = pl.BlockSpec((t, KV_W), lambda i: (0, 0))
    blk = pl.BlockSpec((tm, ATTN_W), lambda i: (i, 0))
    body, dep_specs = _behind(body, deps)
    return pl.pallas_call(
        body, name="attn_bwd", grid=(t // tm,),
        in_specs=dep_specs + [pl.BlockSpec(memory_space=pltpu.SMEM), blk, whole, whole, blk, blk,
                              pl.BlockSpec(bias.shape, lambda i: (0, 0, 0))],
        out_specs=[blk, whole, whole, pl.BlockSpec((N_Q_HEADS, 128), lambda i: (0, 0))],
        out_shape=[jax.ShapeDtypeStruct((t, ATTN_W), F32), jax.ShapeDtypeStruct((t, KV_W), F32),
                   jax.ShapeDtypeStruct((t, KV_W), F32), jax.ShapeDtypeStruct((N_Q_HEADS, 128), F32)],
        compiler_params=_params("arbitrary"),
    )(*deps, sinks, q, k, v, o, do, bias)


def _in_proj_bwd(dq, dk, dv, dbch, w, dh1, h, g, tabs, tm):
    t = h.shape[0]

    def body(dq_ref, dk_ref, dv_ref, dbch_ref, w_ref, dh1_ref, h_ref, g_ref, c_ref, sa_ref, sb_ref, dh_ref, dp_ref,
             dg_ref):
        @pl.when(pl.program_id(0) == 0)
        def _():
            dg_ref[...] = jnp.zeros_like(dg_ref)

        cos, sa, sb = c_ref[...], sa_ref[...], sb_ref[...]
        rep = ATTN_W // (2 * HEAD_DIM)
        dqr = _rope_bwd(dq_ref[...], jnp.tile(cos, (1, rep)), jnp.tile(sa, (1, rep)), jnp.tile(sb, (1, rep)))
        dkr = _rope_bwd(dk_ref[...], cos, sa, sb)
        dp = jnp.concatenate([dqr.astype(BF16), dkr.astype(BF16), dv_ref[...].astype(BF16), dbch_ref[...]], axis=1)
        dp_ref[...] = dp
        da = lax.dot_general(dp, w_ref[...], (((1,), (1,)), ((), ())), preferred_element_type=F32)
        dx, dg = _rms_bwd(da, h_ref[...], g_ref[...])
        dh_ref[...] = dh1_ref[...] + dx
        dg_ref[...] += dg

    row = lambda n: pl.BlockSpec((tm, n), lambda i: (i, 0))
    full = lambda a: pl.BlockSpec(a.shape, lambda i: (0, 0))
    return pl.pallas_call(
        body, name="in_proj_bwd", grid=(t // tm,),
        in_specs=[row(ATTN_W), row(KV_W), row(KV_W), row(3 * CONV_W), full(w), row(D_MODEL), row(D_MODEL), full(g),
                  row(2 * HEAD_DIM), row(2 * HEAD_DIM), row(2 * HEAD_DIM)],
        out_specs=[row(D_MODEL), row(IN_W), pl.BlockSpec((1, D_MODEL), lambda i: (0, 0))],
        out_shape=[jax.ShapeDtypeStruct((t, D_MODEL), F32), jax.ShapeDtypeStruct((t, IN_W), BF16),
                   jax.ShapeDtypeStruct((1, D_MODEL), F32)],
        compiler_params=_params("arbitrary"),
    )(dq, dk, dv, dbch, w, dh1, h, g, *tabs)


class _Tiles:
    def __init__(self, t):
        self.tm = _row_tile(t, 640)
        self.ts = _row_tile(t, 320)
        self.tabs = _rope_tables(t)
        self.bias = _attn_bias()


def _mixer_fwd(h, p, tl):
    a, q, k, v, b, c, hc = _in_proj(h, p["mix_pre_g"], p["w_in"], tl.tabs, tl.ts)
    o = _attn_fwd(q, k, v, tl.bias, p["sinks"], tl.tm)
    return (h, a, q, k, v, b, c, hc, o)


def _out_fwd(mixed, p, tl, deps=()):
    h, a, q, k, v, b, c, hc, o = mixed
    h1, y, z = _mix_out(h, o, b, c, hc, p["conv_w"], p["attn_out_g"], p["conv_out_g"], p["w_out"], p["mix_post_g"],
                        tl.ts, deps)
    return h1, mixed + (h1, y, z)


def _mlp_fwd(h1, saved, p, tl):
    h2, a2, act, f = _mlp(h1, p["mlp_pre_g"], p["w_up"], p["w_down"], p["mlp_post_g"], tl.tm)
    return h2, saved + (a2, act, f)


def _mlp_part_bwd(dh, saved, p, tl, deps=()):
    h1, a2, act, f = saved[9], saved[12], saved[13], saved[14]
    dh1, df, dup, dg2, dg1 = _mlp_bwd(dh, f, p["mlp_post_g"], act, p["w_down"], p["w_up"], h1, p["mlp_pre_g"], tl.tm,
                                      deps)
    g = {"w_down": [d.reshape(N_CHIPS, FF_CHUNK, D_MODEL) for d in _weight_grad(act, df, "grad_w_down")],
         "w_up": [d.reshape(N_CHIPS, D_MODEL, FF_CHUNK) for d in _weight_grad(a2, dup, "grad_w_up")],
         "mlp_post_g": dg2, "mlp_pre_g": dg1}
    return dh1, g


def _mix_out_part_bwd(dh1, saved, p, tl, deps=()):
    b, c, hc, o, y, z = saved[5], saved[6], saved[7], saved[8], saved[10], saved[11]
    dz, do, dbch, dgp, dga, dgc, dcw = _mix_out_bwd(dh1, z, p["mix_post_g"], p["w_out"], o, b, c, hc, p["conv_w"],
                                                    p["attn_out_g"], p["conv_out_g"], tl.ts, deps)
    g = {"w_out": [d.reshape(N_CHIPS, D_MODEL // N_CHIPS, D_MODEL) for d in _weight_grad(y, dz, "grad_w_out")],
         "mix_post_g": dgp, "attn_out_g": dga, "conv_out_g": dgc, "conv_w": dcw}
    return (dh1, do, dbch), g


def _attn_in_part_bwd(carry, saved, p, tl, deps=()):
    dh1, do, dbch = carry
    h_in, a, q, k, v, o = saved[0], saved[1], saved[2], saved[3], saved[4], saved[8]
    dq, dk, dv, dsink = _attn_bwd(q, k, v, o, do, tl.bias, p["sinks"], tl.tm, deps)
    dh, dproj, dgi = _in_proj_bwd(dq, dk, dv, dbch, p["w_in"], dh1, h_in, p["mix_pre_g"], tl.tabs, tl.ts)
    return dh, {"w_in": list(_weight_grad_in(a, dproj)), "mix_pre_g": dgi, "sinks": dsink[:, 0]}


def _place():
    return lax.axis_index("x"), lax.axis_index("y"), lax.axis_index("c")


def _other_chips(x, y):
    return [(1 - x, y), (x, 1 - y), (1 - x, 1 - y)]


_HBM = pl.BlockSpec(memory_space=pltpu.HBM)
_SEM = pl.BlockSpec(memory_space=pltpu.SEMAPHORE)
_EFFECT = pltpu.SideEffectType.DATAFLOW_SIDE_EFFECTING


class _Exchange:
    def __init__(self, name, bufs, plan, n, after=()):
        self.name, self.plan, nb = name, plan, len(bufs)
        n_in = nb + len(after)

        def body(*refs):
            send, recv, token = refs[n_in], refs[n_in + 1], refs[-1]
            for k, (src, dst, target, _) in enumerate(plan(refs[:nb])):
                pltpu.make_async_remote_copy(src_ref=src, dst_ref=dst, send_sem=send.at[k], recv_sem=recv.at[k],
                                             device_id=target, device_id_type=MESH).start()
            token[...] = jnp.zeros_like(token)

        outs = pl.pallas_call(
            body, name=name + "_start",
            out_shape=(pltpu.SemaphoreType.DMA((n,)), pltpu.SemaphoreType.DMA((n,)),
                       *[pltpu.HBM(b.shape, b.dtype) for b in bufs], jax.ShapeDtypeStruct((8, 128), F32)),
            in_specs=[_HBM] * nb + [pl.BlockSpec(memory_space=pl.ANY)] * len(after),
            out_specs=(_SEM, _SEM, *[_HBM] * nb, pl.BlockSpec(memory_space=pltpu.VMEM)),
            input_output_aliases={i: 2 + i for i in range(nb)},
            compiler_params=pltpu.CompilerParams(has_side_effects=_EFFECT),
        )(*[pltpu.with_memory_space_constraint(b, pltpu.HBM) for b in bufs], *after)
        self.send, self.recv, self.bufs, self.token = outs[0], outs[1], list(outs[2:2 + nb]), outs[-1]

    def wait(self, *after):
        plan, nb = self.plan, len(self.bufs)

        def body(*refs):
            send, recv = refs[nb], refs[nb + 1]
            for k, (src, _, target, land) in enumerate(plan(refs[:nb])):
                cp = pltpu.make_async_remote_copy(src_ref=src, dst_ref=land, send_sem=send.at[k], recv_sem=recv.at[k],
                                                  device_id=target, device_id_type=MESH)
                cp.wait_send()
                cp.wait_recv()

        outs = pl.pallas_call(
            body, name=self.name + "_wait", out_shape=[pltpu.HBM(b.shape, b.dtype) for b in self.bufs],
            in_specs=[_HBM] * nb + [_SEM, _SEM] + [pl.BlockSpec(memory_space=pl.ANY)] * len(after),
            out_specs=[_HBM] * nb, input_output_aliases={i: i for i in range(nb)},
            compiler_params=pltpu.CompilerParams(has_side_effects=_EFFECT),
        )(*self.bufs, self.send, self.recv, *after)
        return list(outs)


def _gather_plan(n):
    def plan(refs):
        x, y, c = _place()
        me = 2 * x + y
        return [(refs[a].at[me], refs[a].at[me], (px, py, c), refs[a].at[2 * px + py])
                for a in range(n) for px, py in _other_chips(x, y)]

    return plan


def _peers():
    x, y, c = _place()
    return [(k - 1, (x ^ (k >> 2), y ^ ((k >> 1) & 1), c ^ (k & 1))) for k in range(1, N_DEV)]


def _scatter_plan(n, half_rows):
    def plan(refs):
        out = []
        for a in range(n):
            hr = half_rows[a]
            for k, (px, py, pc) in _peers():
                out.append((refs[a].at[2 * px + py, pl.ds(pc * hr, hr)], refs[n + a].at[k], (px, py, pc),
                            refs[n + a].at[k]))
        return out

    return plan


def _join_plan(n):
    def plan(refs):
        x, y, c = _place()
        return [(refs[a].at[c], refs[a].at[c], (x, y, 1 - c), refs[a].at[1 - c]) for a in range(n)]

    return plan


def _sum_parts(g, q):
    rows, cols = g.shape[1], g.shape[2]
    hr = rows // 2
    tr = min(hr, 256)
    per = hr // tr
    x, y, c = _place()
    where = jnp.stack([2 * x + y, c]).astype(jnp.int32)

    def body(where_ref, g_ref, q_ref, o_ref):
        total = g_ref[...]
        for k in range(N_DEV - 1):
            total = total + q_ref[k].astype(F32)
        o_ref[...] = total

    return pl.pallas_call(
        body, name="sum_parts",
        grid_spec=pltpu.PrefetchScalarGridSpec(
            num_scalar_prefetch=1, grid=(per,),
            in_specs=[pl.BlockSpec((None, tr, cols), lambda i, where_ref: (where_ref[0], where_ref[1] * per + i, 0)),
                      pl.BlockSpec((N_DEV - 1, tr, cols), lambda i, where_ref: (0, i, 0))],
            out_specs=pl.BlockSpec((None, tr, cols), lambda i, where_ref: (where_ref[1], i, 0))),
        out_shape=jax.ShapeDtypeStruct((2, hr, cols), F32),
        compiler_params=_params("parallel"),
    )(where, g, q)


def _sum_devices(packed):
    def body(p_ref, o_ref, land, send_sems, recv_sems):
        x, y, c = _place()
        me = 4 * x + 2 * y + c
        land[me] = p_ref[...]
        sends = []
        for k in range(1, N_DEV):
            px, py, pc = x ^ (k >> 2), y ^ ((k >> 1) & 1), c ^ (k & 1)
            cp = pltpu.make_async_remote_copy(src_ref=p_ref, dst_ref=land.at[me], send_sem=send_sems.at[k - 1],
                                              recv_sem=recv_sems.at[k - 1], device_id=(px, py, pc), device_id_type=MESH)
            cp.start()
            sends.append(cp)
        for k in range(1, N_DEV):
            px, py, pc = x ^ (k >> 2), y ^ ((k >> 1) & 1), c ^ (k & 1)
            pltpu.make_async_remote_copy(src_ref=p_ref, dst_ref=land.at[4 * px + 2 * py + pc],
                                         send_sem=send_sems.at[k - 1], recv_sem=recv_sems.at[k - 1],
                                         device_id=(px, py, pc), device_id_type=MESH).wait_recv()
        for cp in sends:
            cp.wait_send()
        total = land[0]
        for d in range(1, N_DEV):
            total = total + land[d]
        o_ref[...] = total

    vm = pl.BlockSpec(memory_space=pltpu.VMEM)
    return pl.pallas_call(
        body, name="sum_devices", in_specs=[vm], out_specs=vm,
        out_shape=jax.ShapeDtypeStruct(packed.shape, F32),
        scratch_shapes=[pltpu.VMEM((N_DEV,) + packed.shape, F32), pltpu.SemaphoreType.DMA((N_DEV - 1,)),
                        pltpu.SemaphoreType.DMA((N_DEV - 1,))],
    )(packed)


def _adamw_math(w, g, m, v):
    m = ADAM_B1 * m + (1.0 - ADAM_B1) * g
    v = ADAM_B2 * v + (1.0 - ADAM_B2) * jnp.square(g)
    m_hat = m / (1.0 - ADAM_B1 ** ADAM_STEP)
    v_hat = v / (1.0 - ADAM_B2 ** ADAM_STEP)
    delta = -ADAM_LR * (m_hat / (jnp.sqrt(v_hat) + ADAM_EPS) + ADAM_WD * w)
    return delta, m, v


def _adamw_large(layer, w, halves, m, v, other):
    _, rows, cols = w.shape
    tr = min(rows // 2, 256)
    per = rows // 2 // tr

    def body(w_ref, g_ref, m_ref, v_ref, *rest):
        g_out, d_ref, nm_ref, nv_ref = rest[-4:]
        g = g_ref[...]
        g_out[...] = g
        d_ref[...], nm_ref[...], nv_ref[...] = _adamw_math(w_ref[...], g, m_ref[...], v_ref[...])

    blk = pl.BlockSpec((None, tr, cols), lambda i: (layer, i, 0))
    half = pl.BlockSpec((None, tr, cols), lambda i: (i // per, i % per, 0))
    kept = [] if other is None else list(other)
    return pl.pallas_call(
        body, name="adamw_large", grid=(rows // tr,),
        in_specs=[blk, half, blk, blk] + [pl.BlockSpec(memory_space=pl.ANY)] * len(kept), out_specs=[blk] * 4,
        out_shape=[jax.ShapeDtypeStruct(w.shape, F32)] * 4,
        input_output_aliases={4 + k: k for k in range(len(kept))},
        compiler_params=_params("parallel"),
    )(w, halves, m, v, *kept)


def _adamw_small(ws, gs, ms, vs):
    n = len(ws)

    def body(*refs):
        w_r, g_r, m_r, v_r = refs[:n], refs[n:2 * n], refs[2 * n:3 * n], refs[3 * n:4 * n]
        d_r, nm_r, nv_r = refs[4 * n:5 * n], refs[5 * n:6 * n], refs[6 * n:]
        for a in range(n):
            d_r[a][...], nm_r[a][...], nv_r[a][...] = _adamw_math(w_r[a][...], g_r[a][...], m_r[a][...], v_r[a][...])

    vm = pl.BlockSpec(memory_space=pltpu.VMEM)
    outs = pl.pallas_call(
        body, name="adamw_small", in_specs=[vm] * (4 * n), out_specs=[vm] * (3 * n),
        out_shape=[jax.ShapeDtypeStruct(w.shape, F32) for w in ws] * 3,
    )(*ws, *gs, *ms, *vs)
    return outs[:n], outs[n:2 * n], outs[2 * n:]


_LARGE = ("w_in", "w_out", "w_up", "w_down")
_SMALL = ("meta_tokens", "mix_pre_g", "conv_w", "sinks", "attn_out_g", "conv_out_g", "mix_post_g", "mlp_pre_g",
          "mlp_post_g")
_ORDER = ("meta_tokens", "mix_pre_g", "w_in", "conv_w", "sinks", "attn_out_g", "conv_out_g", "w_out", "mix_post_g",
          "mlp_pre_g", "w_up", "w_down", "mlp_post_g")


class _Reduce:
    def __init__(self, name, grads, after=()):
        self.name, self.n = name, len(grads)
        self.own = [g for g, _ in grads]
        half_rows = [g.shape[1] // 2 for g in self.own]
        zones = [lax.empty((N_DEV - 1, hr, g.shape[2]), BF16) for g, hr in zip(self.own, half_rows)]
        self.exchange = _Exchange(name + "_scatter", [b for _, b in grads] + zones, _scatter_plan(self.n, half_rows),
                                  (N_DEV - 1) * self.n, after)

    @property
    def token(self):
        return self.exchange.token

    def join(self, after):
        bufs = self.exchange.wait(after)
        halves = [_sum_parts(g, q) for g, q in zip(self.own, bufs[self.n:])]
        self.exchange = _Exchange(self.name + "_join", halves, _join_plan(self.n), self.n)

    def done(self, after):
        return self.exchange.wait(after)


def _pad_cols(a, n=D_MODEL):
    return jnp.pad(a, ((0, 0), (0, n - a.shape[1])))


def kernel(x, meta_tokens, mix_pre_g, w_in, conv_w, sinks, attn_out_g, conv_out_g, w_out, mix_post_g, mlp_pre_g, w_up, w_down, mlp_post_g, loss_target, m_meta_tokens, m_mix_pre_g, m_w_in, m_conv_w, m_sinks, m_attn_out_g, m_conv_out_g, m_w_out, m_mix_post_g, m_mlp_pre_g, m_w_up, m_w_down, m_mlp_post_g, v_meta_tokens, v_mix_pre_g, v_w_in, v_conv_w, v_sinks, v_attn_out_g, v_conv_out_g, v_w_out, v_mix_post_g, v_mlp_pre_g, v_w_up, v_w_down, v_mlp_post_g):
    w = dict(meta_tokens=meta_tokens, mix_pre_g=mix_pre_g, w_in=w_in, conv_w=conv_w, sinks=sinks,
             attn_out_g=attn_out_g, conv_out_g=conv_out_g, w_out=w_out, mix_post_g=mix_post_g, mlp_pre_g=mlp_pre_g,
             w_up=w_up, w_down=w_down, mlp_post_g=mlp_post_g)
    m = dict(meta_tokens=m_meta_tokens, mix_pre_g=m_mix_pre_g, w_in=m_w_in, conv_w=m_conv_w, sinks=m_sinks,
             attn_out_g=m_attn_out_g, conv_out_g=m_conv_out_g, w_out=m_w_out, mix_post_g=m_mix_post_g,
             mlp_pre_g=m_mlp_pre_g, w_up=m_w_up, w_down=m_w_down, mlp_post_g=m_mlp_post_g)
    v = dict(meta_tokens=v_meta_tokens, mix_pre_g=v_mix_pre_g, w_in=v_w_in, conv_w=v_conv_w, sinks=v_sinks,
             attn_out_g=v_attn_out_g, conv_out_g=v_conv_out_g, w_out=v_w_out, mix_post_g=v_mix_post_g,
             mlp_pre_g=v_mlp_pre_g, w_up=v_w_up, w_down=v_w_down, mlp_post_g=v_mlp_post_g)
    chip = 2 * lax.axis_index("x") + lax.axis_index("y")
    tl = _Tiles(x.shape[1] + BLOCK)

    def zone(quarter):
        return lax.dynamic_update_slice(lax.empty((N_CHIPS,) + quarter.shape, quarter.dtype), quarter[None],
                                        (chip,) + (0,) * quarter.ndim)

    zones = {n: [zone(w[n][l].astype(BF16)) for l in range(DEPTH)] for n in _LARGE}
    first = _Exchange("gather_first", [zones["w_in"][0], zone(w["conv_w"]), zone(w["meta_tokens"])], _gather_plan(3), 9)
    rest = _Exchange("gather_rest", [zones[n][0] for n in ("w_out", "w_up", "w_down")], _gather_plan(3), 9,
                     [first.token])

    def whole_in(quarters):
        return jnp.transpose(quarters, (1, 0, 2)).reshape(D_MODEL, IN_W)

    q_in, q_conv, q_meta = first.wait(rest.token, *tl.tabs, tl.bias)
    conv_whole = jnp.transpose(q_conv, (1, 2, 0, 3)).reshape(DEPTH, CONV_K, CONV_W)
    meta = jnp.transpose(q_meta, (1, 0, 2)).reshape(N_META, D_MODEL)
    p = [{"conv_w": conv_whole[l], "sinks": w["sinks"][l]} for l in range(DEPTH)]
    for l in range(DEPTH):
        for n in ("mix_pre_g", "attn_out_g", "conv_out_g", "mix_post_g", "mlp_pre_g", "mlp_post_g"):
            p[l][n] = w[n][l][None, :]

    h = jnp.concatenate([jnp.zeros((LEAD_PAD, D_MODEL), F32), meta, x[0]], axis=0)
    p[0]["w_in"] = whole_in(q_in)
    mixed = _mixer_fwd(h, p[0], tl)
    second = _Exchange("gather_second", [zones["w_in"][1], zones["w_out"][1]], _gather_plan(2), 6, [mixed[-1]])
    second_mlp = _Exchange("gather_second_mlp", [zones["w_up"][1], zones["w_down"][1]], _gather_plan(2), 6,
                           [second.token])
    p[0]["w_out"], p[0]["w_up"], p[0]["w_down"] = rest.wait(second_mlp.token)
    h1, saved0 = _out_fwd(mixed, p[0], tl)
    h, saved0 = _mlp_fwd(h1, saved0, p[0], tl)
    q_in, p[1]["w_out"] = second.wait(h)
    p[1]["w_in"] = whole_in(q_in)
    h1, saved1 = _out_fwd(_mixer_fwd(h, p[1], tl), p[1], tl)
    p[1]["w_up"], p[1]["w_down"] = second_mlp.wait(h1)
    h, saved1 = _mlp_fwd(h1, saved1, p[1], tl)
    loss_tile, dh = _loss_head(h, loss_target[0], tl.tm)
    loss = lax.psum(loss_tile[0, 0], ("x", "y", "c"))

    def adamw(layer, halves, other):
        return {n: _adamw_large(layer, w[n], halves[n], m[n], v[n], None if other is None else other[n])
                for n in halves}

    dh1, g1 = _mlp_part_bwd(dh, saved1, p[1], tl)
    carry, gm = _mix_out_part_bwd(dh1, saved1, p[1], tl)
    dh, gi = _attn_in_part_bwd(carry, saved1, p[1], tl)
    g1.update(gm, **gi)
    red1 = _Reduce("reduce1", [g1[n] for n in _LARGE])
    dh1, g0 = _mlp_part_bwd(dh, saved0, p[0], tl, [red1.token])
    red1.join(g0["w_down"][0])
    carry, gm = _mix_out_part_bwd(dh1, saved0, p[0], tl, [red1.token])
    first0 = ("w_up", "w_down", "w_out")
    g0.update(gm)
    red0a = _Reduce("reduce0a", [g0[n] for n in first0])
    dh0, gi = _attn_in_part_bwd(carry, saved0, p[0], tl, [red0a.token])
    g0.update(gi)
    red0b = _Reduce("reduce0b", [g0["w_in"]])
    done1 = adamw(1, dict(zip(_LARGE, red1.done(red0b.token))), None)
    red0a.join(done1["w_down"][0])
    red0b.join(red0a.token)
    done0 = adamw(0, dict(zip(first0, red0a.done(red0b.token))), done1)
    done0.update(adamw(0, {"w_in": red0b.done(done0["w_down"][0])[0]}, done1))
    grad_x = dh0[BLOCK:][None]
    grads = {n: [g0[n], g1[n]] for n in g0 if n not in _LARGE}

    rows = [dh0[LEAD_PAD:BLOCK]]
    for n in ("mix_pre_g", "mix_post_g", "mlp_pre_g", "mlp_post_g"):
        rows += grads[n]
    rows += [jnp.concatenate([grads["attn_out_g"][l], grads["conv_out_g"][l]], axis=1) for l in range(DEPTH)]
    rows.append(jnp.concatenate(grads["conv_w"], axis=1))
    rows.append(_pad_cols(jnp.concatenate(grads["sinks"])[None, :]))
    packed = jnp.concatenate(rows, axis=0)
    packed = jnp.pad(packed, ((0, SMALL_ROWS - packed.shape[0]), (0, 0)))
    total = _sum_devices(packed)
    r0 = N_META
    small = {
        "meta_tokens": lax.dynamic_slice(total[:N_META], (0, chip * (D_MODEL // N_CHIPS)), (N_META, D_MODEL // N_CHIPS)),
        "mix_pre_g": total[r0:r0 + 2], "mix_post_g": total[r0 + 2:r0 + 4], "mlp_pre_g": total[r0 + 4:r0 + 6],
        "mlp_post_g": total[r0 + 6:r0 + 8],
        "attn_out_g": total[r0 + 8:r0 + 10, :ATTN_W], "conv_out_g": total[r0 + 8:r0 + 10, ATTN_W:],
        "conv_w": lax.dynamic_slice(total[r0 + 10:r0 + 13].reshape(CONV_K, DEPTH, CONV_W).transpose(1, 0, 2),
                                    (0, 0, chip * (CONV_W // N_CHIPS)), (DEPTH, CONV_K, CONV_W // N_CHIPS)),
        "sinks": total[r0 + 13, :DEPTH * N_Q_HEADS].reshape(DEPTH, N_Q_HEADS),
    }

    grad, delta, new_m, new_v = {}, {}, {}, {}
    for n in _LARGE:
        grad[n], delta[n], new_m[n], new_v[n] = done0[n]
    ds, nms, nvs = _adamw_small([w[n] for n in _SMALL], [small[n] for n in _SMALL], [m[n] for n in _SMALL],
                                [v[n] for n in _SMALL])
    for i, n in enumerate(_SMALL):
        grad[n], delta[n], new_m[n], new_v[n] = small[n], ds[i], nms[i], nvs[i]
    return (loss, grad_x, *[grad[n] for n in _ORDER], *[delta[n] for n in _ORDER], *[new_m[n] for n in _ORDER],
            *[new_v[n] for n in _ORDER])
```

```python
import functools

import jax
import jax.numpy as jnp
from jax import lax
from jax.experimental import pallas as pl
from jax.experimental.pallas import tpu as pltpu

F32 = jnp.float32
BF16 = jnp.bfloat16

D_MODEL = 1024
DEPTH = 2
N_META = 16
ATTN_W = 512
CONV_W = 512
HEAD_DIM = 64
N_Q_HEADS = 8
N_KV_HEADS = 2
GROUP = N_Q_HEADS // N_KV_HEADS
KV_W = N_KV_HEADS * HEAD_DIM
CONV_K = 3
BLOCK = 128
LEAD_PAD = BLOCK - N_META
ROPE_THETA = 500000.0
ROT_DIM = HEAD_DIM // 4
ROT_HALF = ROT_DIM // 2
D_FF = 4 * D_MODEL
IN_W = ATTN_W + 2 * KV_W + 3 * CONV_W
QKV_W = ATTN_W + 2 * KV_W
EPS = 1e-6
SCALE = HEAD_DIM ** -0.5
FF_CHUNK = 1024
N_CHIPS = 4
N_DEV = 8

ADAM_LR = 0.001
ADAM_B1 = 0.9
ADAM_B2 = 0.999
ADAM_EPS = 1e-08
ADAM_WD = 0.01
ADAM_STEP = 10

V7X_VMEM_LIMIT = 56 * 1024 * 1024
SMALL_ROWS = 32

MESH = pl.DeviceIdType.MESH


def _params(*sem):
    return pltpu.CompilerParams(dimension_semantics=sem, vmem_limit_bytes=V7X_VMEM_LIMIT)


def _row_tile(t, most):
    nb = t // BLOCK
    for b in range(most // BLOCK, 0, -1):
        if nb % b == 0:
            return b * BLOCK
    return BLOCK


def _behind(body, deps):
    n = len(deps)

    def wrapped(*refs):
        body(*refs[n:])

    return wrapped, [pl.BlockSpec(memory_space=pl.ANY)] * n


def _rms(x, g):
    r = lax.rsqrt(jnp.mean(x * x, axis=-1, keepdims=True) + EPS)
    return x * r * g


def _rms_bwd(dy, x, g):
    r = lax.rsqrt(jnp.mean(x * x, axis=-1, keepdims=True) + EPS)
    xh = x * r
    dg = jnp.sum(dy * xh, axis=0, keepdims=True)
    dxh = dy * g
    dx = r * (dxh - xh * jnp.mean(dxh * xh, axis=-1, keepdims=True))
    return dx, dg


def _rope(x, cos, sa, sb):
    n = x.shape[-1]
    return x * cos + pltpu.roll(x, n - ROT_HALF, 1) * sa + pltpu.roll(x, ROT_HALF, 1) * sb


def _rope_bwd(dy, cos, sa, sb):
    n = dy.shape[-1]
    return dy * cos + pltpu.roll(dy * sa, ROT_HALF, 1) + pltpu.roll(dy * sb, n - ROT_HALF, 1)


def _rope_tables(t):
    pos = lax.broadcasted_iota(jnp.int32, (t, 2 * HEAD_DIM), 0).astype(F32) - LEAD_PAD
    dim = lax.broadcasted_iota(jnp.int32, (t, 2 * HEAD_DIM), 1) % HEAD_DIM
    pair = (dim % ROT_HALF).astype(F32)
    inv_freq = jnp.power(jnp.float32(ROPE_THETA), -(2.0 * pair) / ROT_DIM)
    ang = pos * inv_freq
    cos, sin = jnp.cos(ang), jnp.sin(ang)
    return (jnp.where(dim < ROT_DIM, cos, 1.0), jnp.where(dim < ROT_HALF, -sin, 0.0),
            jnp.where((dim >= ROT_HALF) & (dim < ROT_DIM), sin, 0.0))


def _in_proj(h, g, w, tabs, tm):
    t = h.shape[0]

    def body(h_ref, g_ref, w_ref, c_ref, sa_ref, sb_ref, a_ref, q_ref, k_ref, v_ref, b_ref, cg_ref, hc_ref):
        a = _rms(h_ref[...], g_ref[...]).astype(BF16)
        a_ref[...] = a
        p = jnp.dot(a, w_ref[...], preferred_element_type=F32)
        cos, sa, sb = c_ref[...], sa_ref[...], sb_ref[...]
        rep = ATTN_W // (2 * HEAD_DIM)
        q = _rope(p[:, :ATTN_W], jnp.tile(cos, (1, rep)), jnp.tile(sa, (1, rep)), jnp.tile(sb, (1, rep)))
        q_ref[...] = (q * SCALE).astype(BF16)
        k_ref[...] = _rope(p[:, ATTN_W:ATTN_W + KV_W], cos, sa, sb).astype(BF16)
        v_ref[...] = p[:, ATTN_W + KV_W:QKV_W].astype(BF16)
        b_ref[...] = p[:, QKV_W:QKV_W + CONV_W]
        cg_ref[...] = p[:, QKV_W + CONV_W:QKV_W + 2 * CONV_W]
        hc_ref[...] = p[:, QKV_W + 2 * CONV_W:]

    row = lambda n: pl.BlockSpec((tm, n), lambda i: (i, 0))
    full = lambda a: pl.BlockSpec(a.shape, lambda i: (0, 0))
    return pl.pallas_call(
        body, name="in_proj", grid=(t // tm,),
        in_specs=[row(D_MODEL), full(g), full(w), row(2 * HEAD_DIM), row(2 * HEAD_DIM), row(2 * HEAD_DIM)],
        out_specs=[row(D_MODEL), row(ATTN_W), row(KV_W), row(KV_W), row(CONV_W), row(CONV_W), row(CONV_W)],
        out_shape=[jax.ShapeDtypeStruct((t, D_MODEL), BF16), jax.ShapeDtypeStruct((t, ATTN_W), BF16),
                   jax.ShapeDtypeStruct((t, KV_W), BF16), jax.ShapeDtypeStruct((t, KV_W), BF16),
                   jax.ShapeDtypeStruct((t, CONV_W), F32), jax.ShapeDtypeStruct((t, CONV_W), F32),
                   jax.ShapeDtypeStruct((t, CONV_W), F32)],
        compiler_params=_params("parallel"),
    )(h, g, w, *tabs)


def _attn_bias():
    r = lax.broadcasted_iota(jnp.int32, (3, BLOCK, 2 * BLOCK), 1)
    c = lax.broadcasted_iota(jnp.int32, (3, BLOCK, 2 * BLOCK), 2)
    i = lax.broadcasted_iota(jnp.int32, (3, BLOCK, 2 * BLOCK), 0)
    ok = (c > r) & (c <= r + BLOCK) & (c + (i - 1) * BLOCK >= LEAD_PAD)
    return jnp.where(ok, 0.0, -jnp.inf).astype(F32)


def _attn_scores(qh, kg, bias):
    return lax.dot_general(qh, kg, (((1,), (1,)), ((), ())), preferred_element_type=F32) + bias


def _attn_probs(s, sk):
    m = jnp.maximum(jnp.max(s, axis=-1, keepdims=True), sk)
    e = jnp.exp(s - m)
    es = jnp.exp(sk - m)
    rden = 1.0 / (jnp.sum(e, axis=-1, keepdims=True) + es)
    return e * rden, es * rden


def _head(hh):
    return slice(hh * HEAD_DIM, (hh + 1) * HEAD_DIM)


def _two_blocks(ref, i):
    prev = jnp.maximum(i - 1, 0)
    return jnp.concatenate([ref[pl.ds(pl.multiple_of(prev * BLOCK, BLOCK), BLOCK), :],
                            ref[pl.ds(pl.multiple_of(i * BLOCK, BLOCK), BLOCK), :]], axis=0)


def _attn_fwd(q, k, v, bias, sinks, tm):
    t = q.shape[0]
    per_step = tm // BLOCK
    heads = range(N_Q_HEADS)

    def body(s_ref, q_ref, k_ref, v_ref, bias_ref, o_ref):
        for b in range(per_step):
            i = pl.program_id(0) * per_step + b
            rows = slice(b * BLOCK, (b + 1) * BLOCK)
            kc, vc = _two_blocks(k_ref, i), _two_blocks(v_ref, i)
            bias_i = bias_ref[jnp.minimum(i, 2)]
            scores = [_attn_scores(q_ref[rows, _head(hh)], kc[:, _head(hh // GROUP)], bias_i) for hh in heads]
            probs = [_attn_probs(scores[hh], s_ref[hh])[0].astype(BF16) for hh in heads]
            for hh in heads:
                o_ref[rows, _head(hh)] = jnp.dot(probs[hh], vc[:, _head(hh // GROUP)], preferred_element_type=F32)

    whole = pl.BlockSpec((t, KV_W), lambda i: (0, 0))
    return pl.pallas_call(
        body, name="attn_fwd", grid=(t // tm,),
        in_specs=[pl.BlockSpec(memory_space=pltpu.SMEM), pl.BlockSpec((tm, ATTN_W), lambda i: (i, 0)), whole, whole,
                  pl.BlockSpec(bias.shape, lambda i: (0, 0, 0))],
        out_specs=pl.BlockSpec((tm, ATTN_W), lambda i: (i, 0)),
        out_shape=jax.ShapeDtypeStruct((t, ATTN_W), F32),
        compiler_params=_params("parallel"),
    )(sinks, q, k, v, bias)


def _shift_rows(u, halo, n):
    r = pltpu.roll(u, n, 0)
    hr = pltpu.roll(halo, n, 0)
    idx = lax.broadcasted_iota(jnp.int32, hr.shape, 0)
    return jnp.concatenate([jnp.where(idx < n, hr, r[:8]), r[8:]], axis=0)


def _advance_rows(u, halo, n):
    rows = u.shape[0]
    r = pltpu.roll(u, rows - n, 0)
    hr = pltpu.roll(halo, 8 - n, 0)
    idx = lax.broadcasted_iota(jnp.int32, hr.shape, 0)
    return jnp.concatenate([r[:rows - 8], jnp.where(idx >= 8 - n, hr, r[rows - 8:])], axis=0)


def _mix_out(h, o, b, c, hc, cw, ga, gc, w, gp, tm, deps=()):
    t = h.shape[0]

    def body(h_ref, o_ref, b_ref, c_ref, hc_ref, cw_ref, ga_ref, gc_ref, w_ref, gp_ref, h1_ref, y_ref, z_ref, halo):
        @pl.when(pl.program_id(0) == 0)
        def _():
            halo[...] = jnp.zeros_like(halo)

        u = c_ref[...] * hc_ref[...]
        cv = cw_ref[0:1, :] * _shift_rows(u, halo[...], 2) + cw_ref[1:2, :] * _shift_rows(u, halo[...], 1) \
            + cw_ref[2:3, :] * u
        halo[...] = u[tm - 8:]
        yc = b_ref[...] * cv
        y = jnp.concatenate([_rms(o_ref[...], ga_ref[...]), _rms(yc, gc_ref[...])], axis=1).astype(BF16)
        y_ref[...] = y
        z = jnp.dot(y, w_ref[...].reshape(D_MODEL, D_MODEL), preferred_element_type=F32)
        z_ref[...] = z
        h1_ref[...] = h_ref[...] + _rms(z, gp_ref[...])

    row = lambda n: pl.BlockSpec((tm, n), lambda i: (i, 0))
    full = lambda a: pl.BlockSpec(a.shape, lambda i: (0,) * a.ndim)
    body, dep_specs = _behind(body, deps)
    return pl.pallas_call(
        body, name="mix_out", grid=(t // tm,),
        in_specs=dep_specs + [row(D_MODEL), row(ATTN_W), row(CONV_W), row(CONV_W), row(CONV_W), full(cw), full(ga),
                              full(gc), full(w), full(gp)],
        out_specs=[row(D_MODEL), row(D_MODEL), row(D_MODEL)],
        out_shape=[jax.ShapeDtypeStruct((t, D_MODEL), F32), jax.ShapeDtypeStruct((t, D_MODEL), BF16),
                   jax.ShapeDtypeStruct((t, D_MODEL), F32)],
        scratch_shapes=[pltpu.VMEM((8, CONV_W), F32)],
        compiler_params=_params("arbitrary"),
    )(*deps, h, o, b, c, hc, cw, ga, gc, w, gp)


def _mlp(h1, g1, wu, wd, g2, tm):
    t = h1.shape[0]
    nj = D_FF // FF_CHUNK

    def body(h1_ref, g1_ref, wu_ref, wd_ref, g2_ref, h2_ref, a2_ref, act_ref, f_ref, acc):
        j = pl.program_id(1)

        @pl.when(j == 0)
        def _():
            a2_ref[...] = _rms(h1_ref[...], g1_ref[...]).astype(BF16)

        up = jnp.dot(a2_ref[...], wu_ref[...], preferred_element_type=F32)
        act = jnp.square(jnp.maximum(up, 0.0)).astype(BF16)
        act_ref[...] = act
        part = jnp.dot(act, wd_ref[...], preferred_element_type=F32)

        @pl.when(j == 0)
        def _():
            acc[...] = part

        @pl.when(j > 0)
        def _():
            acc[...] += part

        @pl.when(j == nj - 1)
        def _():
            f = acc[...]
            f_ref[...] = f
            h2_ref[...] = h1_ref[...] + _rms(f, g2_ref[...])

    row = pl.BlockSpec((tm, D_MODEL), lambda i, j: (i, 0))
    vec = pl.BlockSpec((1, D_MODEL), lambda i, j: (0, 0))
    quarter = pl.BlockSpec((None, D_MODEL, FF_CHUNK), lambda i, j: (j, 0, 0))
    return pl.pallas_call(
        body, name="mlp", grid=(t // tm, nj),
        in_specs=[row, vec, quarter, quarter, vec],
        out_specs=[row, row, pl.BlockSpec((tm, FF_CHUNK), lambda i, j: (i, j)), row],
        out_shape=[jax.ShapeDtypeStruct((t, D_MODEL), F32), jax.ShapeDtypeStruct((t, D_MODEL), BF16),
                   jax.ShapeDtypeStruct((t, D_FF), BF16), jax.ShapeDtypeStruct((t, D_MODEL), F32)],
        scratch_shapes=[pltpu.VMEM((tm, D_MODEL), F32)],
        compiler_params=_params("parallel", "arbitrary"),
    )(h1, g1, wu, wd, g2)


def _loss_head(h, target, tm):
    t = h.shape[0]
    per_step = tm // BLOCK

    def body(h_ref, *rest):
        t_refs, (loss_ref, dh_ref) = rest[:per_step], rest[per_step:]
        i = pl.program_id(0)

        @pl.when(i == 0)
        def _():
            loss_ref[...] = jnp.zeros_like(loss_ref)

        total = jnp.zeros((), F32)
        for b in range(per_step):
            rows = slice(b * BLOCK, (b + 1) * BLOCK)
            err = h_ref[rows, :] - t_refs[b][...]
            if b == 0:
                err = jnp.where(i == 0, 0.0, err)
            dh_ref[rows, :] = err * (1.0 / D_MODEL)
            total = total + jnp.sum(err * err)
        loss_ref[...] += total * (0.5 / D_MODEL)

    def target_block(b):
        return pl.BlockSpec((BLOCK, D_MODEL), lambda i: (jnp.maximum(i * per_step + b - 1, 0), 0))

    return pl.pallas_call(
        body, name="loss_head", grid=(t // tm,),
        in_specs=[pl.BlockSpec((tm, D_MODEL), lambda i: (i, 0))] + [target_block(b) for b in range(per_step)],
        out_specs=[pl.BlockSpec((8, 128), lambda i: (0, 0)), pl.BlockSpec((tm, D_MODEL), lambda i: (i, 0))],
        out_shape=[jax.ShapeDtypeStruct((8, 128), F32), jax.ShapeDtypeStruct((t, D_MODEL), F32)],
        compiler_params=_params("arbitrary"),
    )(h, *([target] * per_step))


def _mlp_bwd(dh2, f, g2, act, wd, wu, h1, g1, tm, deps=()):
    t = dh2.shape[0]
    nj = D_FF // FF_CHUNK

    def body(dh2_ref, f_ref, g2_ref, act_ref, wd_ref, wu_ref, h1_ref, g1_ref, dh1_ref, df_ref, dup_ref, dg2_ref,
             dg1_ref, acc):
        i, j = pl.program_id(0), pl.program_id(1)

        @pl.when((i == 0) & (j == 0))
        def _():
            dg2_ref[...] = jnp.zeros_like(dg2_ref)
            dg1_ref[...] = jnp.zeros_like(dg1_ref)

        @pl.when(j == 0)
        def _():
            df, dg = _rms_bwd(dh2_ref[...], f_ref[...], g2_ref[...])
            df_ref[...] = df.astype(BF16)
            dg2_ref[...] += dg

        dact = lax.dot_general(df_ref[...], wd_ref[...], (((1,), (1,)), ((), ())), preferred_element_type=F32)
        dup = (dact * (2.0 * jnp.sqrt(act_ref[...].astype(F32)))).astype(BF16)
        dup_ref[...] = dup
        part = lax.dot_general(dup, wu_ref[...], (((1,), (1,)), ((), ())), preferred_element_type=F32)

        @pl.when(j == 0)
        def _():
            acc[...] = part

        @pl.when(j > 0)
        def _():
            acc[...] += part

        @pl.when(j == nj - 1)
        def _():
            dx, dg = _rms_bwd(acc[...], h1_ref[...], g1_ref[...])
            dh1_ref[...] = dh2_ref[...] + dx
            dg1_ref[...] += dg

    row = pl.BlockSpec((tm, D_MODEL), lambda i, j: (i, 0))
    vec = pl.BlockSpec((1, D_MODEL), lambda i, j: (0, 0))
    chunk = pl.BlockSpec((tm, FF_CHUNK), lambda i, j: (i, j))
    quarter = pl.BlockSpec((None, D_MODEL, FF_CHUNK), lambda i, j: (j, 0, 0))
    body, dep_specs = _behind(body, deps)
    return pl.pallas_call(
        body, name="mlp_bwd", grid=(t // tm, nj),
        in_specs=dep_specs + [row, row, vec, chunk, quarter, quarter, row, vec],
        out_specs=[row, row, chunk, vec, vec],
        out_shape=[jax.ShapeDtypeStruct((t, D_MODEL), F32), jax.ShapeDtypeStruct((t, D_MODEL), BF16),
                   jax.ShapeDtypeStruct((t, D_FF), BF16), jax.ShapeDtypeStruct((1, D_MODEL), F32),
                   jax.ShapeDtypeStruct((1, D_MODEL), F32)],
        scratch_shapes=[pltpu.VMEM((tm, D_MODEL), F32)],
        compiler_params=_params("arbitrary", "arbitrary"),
    )(*deps, dh2, f, g2, act, wd, wu, h1, g1)


def _row_split(t):
    tile = min(t, 1024)
    return tile, t // tile, t % tile


def _row_split_specs(t, cols, col_of):
    tile, whole, rest = _row_split(t)
    specs = [pl.BlockSpec((tile, cols), lambda *g: (jnp.minimum(g[-1], whole - 1), col_of(*g[:-1])))]
    if rest:
        specs.append(pl.BlockSpec((rest, cols), lambda *g: (whole * tile // rest, col_of(*g[:-1]))))
    return specs


def _weight_grad(x, y, name):
    t, k = x.shape
    n = y.shape[1]
    tk = tn = FF_CHUNK
    _, whole, rest = _row_split(t)
    steps = whole + bool(rest)

    def body(*refs):
        o_ref, ob_ref, r = refs[-2], refs[-1], pl.program_id(2)

        @pl.when(r == 0)
        def _():
            o_ref[...] = jnp.zeros_like(o_ref)

        def add(x_ref, y_ref):
            o_ref[...] += lax.dot_general(x_ref[...], y_ref[...], (((0,), (0,)), ((), ())),
                                          preferred_element_type=F32)

        if rest:
            pl.when(r < whole)(lambda: add(refs[0], refs[2]))
            pl.when(r == whole)(lambda: add(refs[1], refs[3]))
        else:
            add(refs[0], refs[1])

        @pl.when(r == steps - 1)
        def _():
            ob_ref[...] = o_ref[...].astype(BF16)

    tile = pl.BlockSpec((None, None, tk, tn), lambda a, b, r: (a, b, 0, 0))
    return pl.pallas_call(
        body, name=name, grid=(k // tk, n // tn, steps),
        in_specs=_row_split_specs(t, tk, lambda a, b: a) + _row_split_specs(t, tn, lambda a, b: b),
        out_specs=[tile, tile],
        out_shape=[jax.ShapeDtypeStruct((k // tk, n // tn, tk, tn), F32),
                   jax.ShapeDtypeStruct((k // tk, n // tn, tk, tn), BF16)],
        compiler_params=_params("parallel", "parallel", "arbitrary"),
    )(*([x] * (1 + bool(rest))), *([y] * (1 + bool(rest))))


def _weight_grad_in(a, dproj):
    t = a.shape[0]
    _, whole, rest = _row_split(t)
    qw = IN_W // N_CHIPS

    def body(*refs):
        o_ref, ob_ref, acc, r = refs[-3], refs[-2], refs[-1], pl.program_id(0)

        @pl.when(r == 0)
        def _():
            acc[...] = jnp.zeros_like(acc)

        def add(a_ref, d_ref):
            acc[...] += lax.dot_general(a_ref[...], d_ref[...], (((0,), (0,)), ((), ())),
                                        preferred_element_type=F32)

        if rest:
            pl.when(r < whole)(lambda: add(refs[0], refs[2]))
            pl.when(r == whole)(lambda: add(refs[1], refs[3]))
        else:
            add(refs[0], refs[1])

        @pl.when(r == whole + bool(rest) - 1)
        def _():
            for s in range(N_CHIPS):
                quarter = acc[:, s * qw:(s + 1) * qw]
                o_ref[s] = quarter
                ob_ref[s] = quarter.astype(BF16)

    vm = pl.BlockSpec(memory_space=pltpu.VMEM)
    return pl.pallas_call(
        body, name="grad_w_in", grid=(whole + bool(rest),),
        in_specs=_row_split_specs(t, D_MODEL, lambda: 0) + _row_split_specs(t, IN_W, lambda: 0),
        out_specs=[vm, vm],
        out_shape=[jax.ShapeDtypeStruct((N_CHIPS, D_MODEL, qw), F32),
                   jax.ShapeDtypeStruct((N_CHIPS, D_MODEL, qw), BF16)],
        scratch_shapes=[pltpu.VMEM((D_MODEL, IN_W), F32)],
        compiler_params=_params("arbitrary"),
    )(*([a] * (1 + bool(rest))), *([dproj] * (1 + bool(rest))))


def _mix_out_bwd(dh1, z, gp, w, o, b, c, hc, cw, ga, gc, tm, deps=()):
    t = dh1.shape[0]
    nt = t // tm
    per8 = tm // 8

    def body(dh1_ref, z_ref, gp_ref, w_ref, o_ref, b_ref, c_ref, hc_ref, cp_ref, hp_ref, cw_ref, ga_ref, gc_ref,
             dz_ref, do_ref, dbch_ref, dgp_ref, dga_ref, dgc_ref, dcw_ref, halo):
        i = pl.program_id(0)

        @pl.when(i == 0)
        def _():
            halo[...] = jnp.zeros_like(halo)
            dgp_ref[...] = jnp.zeros_like(dgp_ref)
            dga_ref[...] = jnp.zeros_like(dga_ref)
            dgc_ref[...] = jnp.zeros_like(dgc_ref)
            dcw_ref[...] = jnp.zeros_like(dcw_ref)

        dz, dgp = _rms_bwd(dh1_ref[...], z_ref[...], gp_ref[...])
        dgp_ref[...] += dgp
        dz = dz.astype(BF16)
        dz_ref[...] = dz
        dy = lax.dot_general(dz, w_ref[...].reshape(D_MODEL, D_MODEL), (((1,), (1,)), ((), ())),
                             preferred_element_type=F32)
        do, dga = _rms_bwd(dy[:, :ATTN_W], o_ref[...], ga_ref[...])
        do_ref[...] = do
        dga_ref[...] += dga

        u = c_ref[...] * hc_ref[...]
        first = i == nt - 1
        u_before = jnp.where(first, 0.0, cp_ref[...] * hp_ref[...])
        u1 = _shift_rows(u, u_before, 1)
        u2 = _shift_rows(u, u_before, 2)
        cv = cw_ref[0:1, :] * u2 + cw_ref[1:2, :] * u1 + cw_ref[2:3, :] * u
        bb = b_ref[...]
        dyc, dgc = _rms_bwd(dy[:, ATTN_W:], bb * cv, gc_ref[...])
        dgc_ref[...] += dgc
        dcv = dyc * bb
        d1 = _advance_rows(dcv, halo[...], 1)
        d2 = _advance_rows(dcv, halo[...], 2)
        halo[...] = dcv[:8]
        du = cw_ref[2:3, :] * dcv + cw_ref[1:2, :] * d1 + cw_ref[0:1, :] * d2
        dbch_ref[...] = jnp.concatenate([dyc * cv, du * hc_ref[...], du * c_ref[...]], axis=1).astype(BF16)
        dcw_ref[...] += jnp.concatenate([jnp.sum(dcv * u2, axis=0, keepdims=True),
                                         jnp.sum(dcv * u1, axis=0, keepdims=True),
                                         jnp.sum(dcv * u, axis=0, keepdims=True)], axis=0)

    row = lambda n: pl.BlockSpec((tm, n), lambda i: (nt - 1 - i, 0))
    before = pl.BlockSpec((8, CONV_W), lambda i: (jnp.maximum((nt - 1 - i) * per8 - 1, 0), 0))
    full = lambda a: pl.BlockSpec(a.shape, lambda i: (0,) * a.ndim)
    vec = lambda n: pl.BlockSpec((1, n), lambda i: (0, 0))
    body, dep_specs = _behind(body, deps)
    return pl.pallas_call(
        body, name="mix_out_bwd", grid=(nt,),
        in_specs=dep_specs + [row(D_MODEL), row(D_MODEL), full(gp), full(w), row(ATTN_W), row(CONV_W), row(CONV_W),
                              row(CONV_W), before, before, full(cw), full(ga), full(gc)],
        out_specs=[row(D_MODEL), row(ATTN_W), row(3 * CONV_W), vec(D_MODEL), vec(ATTN_W), vec(CONV_W),
                   pl.BlockSpec((CONV_K, CONV_W), lambda i: (0, 0))],
        out_shape=[jax.ShapeDtypeStruct((t, D_MODEL), BF16), jax.ShapeDtypeStruct((t, ATTN_W), F32),
                   jax.ShapeDtypeStruct((t, 3 * CONV_W), BF16), jax.ShapeDtypeStruct((1, D_MODEL), F32),
                   jax.ShapeDtypeStruct((1, ATTN_W), F32), jax.ShapeDtypeStruct((1, CONV_W), F32),
                   jax.ShapeDtypeStruct((CONV_K, CONV_W), F32)],
        scratch_shapes=[pltpu.VMEM((8, CONV_W), F32)],
        compiler_params=_params("arbitrary"),
    )(*deps, dh1, z, gp, w, o, b, c, hc, c, hc, cw, ga, gc)


def _attn_bwd(q, k, v, o, do, bias, sinks, tm, deps=()):
    t = q.shape[0]
    per_step = tm // BLOCK

    def body(s_ref, q_ref, k_ref, v_ref, o_ref, do_ref, bias_ref, dq_ref, dk_ref, dv_ref, ds_ref):
        step = pl.program_id(0)

        @pl.when(step == 0)
        def _():
            ds_ref[...] = jnp.zeros_like(ds_ref)

        heads = range(N_Q_HEADS)

        def first_matmuls(b):
            i = step * per_step + b
            rows = slice(b * BLOCK, (b + 1) * BLOCK)
            kc, vc = _two_blocks(k_ref, i), _two_blocks(v_ref, i)
            bias_i = bias_ref[jnp.minimum(i, 2)]
            kgs = [kc[:, _head(g)] for g in range(N_KV_HEADS)]
            vgs = [vc[:, _head(g)] for g in range(N_KV_HEADS)]
            qs = [q_ref[rows, _head(hh)] for hh in heads]
            dos = [do_ref[rows, _head(hh)] for hh in heads]
            dosb = [d.astype(BF16) for d in dos]
            scores = [_attn_scores(qs[hh], kgs[hh // GROUP], bias_i) for hh in heads]
            dps = [lax.dot_general(dosb[hh], vgs[hh // GROUP], (((1,), (1,)), ((), ())), preferred_element_type=F32)
                   for hh in heads]
            return kgs, qs, dos, dosb, scores, dps

        dsink = [jnp.zeros((BLOCK, 1), F32) for _ in range(N_Q_HEADS)]
        ahead = None
        for b in range(per_step):
            i = step * per_step + b
            rows = slice(b * BLOCK, (b + 1) * BLOCK)
            kgs, qs, dos, dosb, scores, dps = first_matmuls(b)
            ps, dss = [], []
            for hh in heads:
                p, share = _attn_probs(scores[hh], s_ref[hh])
                drow = jnp.sum(dos[hh] * o_ref[rows, _head(hh)], axis=-1, keepdims=True)
                dss.append((p * (dps[hh] - drow)).astype(BF16))
                ps.append(p.astype(BF16))
                dsink[hh] = dsink[hh] + share * drow
            for hh in heads:
                dq_ref[rows, _head(hh)] = jnp.dot(dss[hh], kgs[hh // GROUP], preferred_element_type=F32) * SCALE
            groups = [slice(GROUP * g, GROUP * (g + 1)) for g in range(N_KV_HEADS)]
            dkg = [lax.dot_general(jnp.concatenate(dss[gr], axis=0), jnp.concatenate(qs[gr], axis=0),
                                   (((0,), (0,)), ((), ())), preferred_element_type=F32) for gr in groups]
            dvg = [lax.dot_general(jnp.concatenate(ps[gr], axis=0), jnp.concatenate(dosb[gr], axis=0),
                                   (((0,), (0,)), ((), ())), preferred_element_type=F32) for gr in groups]
            dkb, dvb = jnp.concatenate(dkg, axis=1), jnp.concatenate(dvg, axis=1)
            if b == 0:
                @pl.when(step > 0)
                def _():
                    before = pl.ds(pl.multiple_of((i - 1) * BLOCK, BLOCK), BLOCK)
                    dk_ref[before, :] += dkb[:BLOCK]
                    dv_ref[before, :] += dvb[:BLOCK]
            else:
                at = pl.ds(pl.multiple_of((i - 1) * BLOCK, BLOCK), BLOCK)
                dk_ref[at, :] = ahead[0] + dkb[:BLOCK]
                dv_ref[at, :] = ahead[1] + dvb[:BLOCK]
            ahead = (dkb[BLOCK:], dvb[BLOCK:])
        last = pl.ds(pl.multiple_of(((step + 1) * per_step - 1) * BLOCK, BLOCK), BLOCK)
        dk_ref[last, :] = ahead[0]
        dv_ref[last, :] = ahead[1]
        for hh in range(N_Q_HEADS):
            ds_ref[hh:hh + 1, :] -= jnp.sum(dsink[hh])

    whole = pl.BlockSpec((t, KV_W), lambda i: (0, 0))
    blk = pl.BlockSpec((tm, ATTN_W), lambda i: (i, 0))
    body, dep_specs = _behind(body, deps)
    return pl.pallas_call(
        body, name="attn_bwd", grid=(t // tm,),
        in_specs=dep_specs + [pl.BlockSpec(memory_space=pltpu.SMEM), blk, whole, whole, blk, blk,
                              pl.BlockSpec(bias.shape, lambda i: (0, 0, 0))],
        out_specs=[blk, whole, whole, pl.BlockSpec((N_Q_HEADS, 128), lambda i: (0, 0))],
        out_shape=[jax.ShapeDtypeStruct((t, ATTN_W), F32), jax.ShapeDtypeStruct((t, KV_W), F32),
                   jax.ShapeDtypeStruct((t, KV_W), F32), jax.ShapeDtypeStruct((N_Q_HEADS, 128), F32)],
        compiler_params=_params("arbitrary"),
    )(*deps, sinks, q, k, v, o, do, bias)


def _in_proj_bwd(dq, dk, dv, dbch, w, dh1, h, g, tabs, tm):
    t = h.shape[0]

    def body(dq_ref, dk_ref, dv_ref, dbch_ref, w_ref, dh1_ref, h_ref, g_ref, c_ref, sa_ref, sb_ref, dh_ref, dp_ref,
             dg_ref):
        @pl.when(pl.program_id(0) == 0)
        def _():
            dg_ref[...] = jnp.zeros_like(dg_ref)

        cos, sa, sb = c_ref[...], sa_ref[...], sb_ref[...]
        rep = ATTN_W // (2 * HEAD_DIM)
        dqr = _rope_bwd(dq_ref[...], jnp.tile(cos, (1, rep)), jnp.tile(sa, (1, rep)), jnp.tile(sb, (1, rep)))
        dkr = _rope_bwd(dk_ref[...], cos, sa, sb)
        dp = jnp.concatenate([dqr.astype(BF16), dkr.astype(BF16), dv_ref[...].astype(BF16), dbch_ref[...]], axis=1)
        dp_ref[...] = dp
        da = lax.dot_general(dp, w_ref[...], (((1,), (1,)), ((), ())), preferred_element_type=F32)
        dx, dg = _rms_bwd(da, h_ref[...], g_ref[...])
        dh_ref[...] = dh1_ref[...] + dx
        dg_ref[...] += dg

    row = lambda n: pl.BlockSpec((tm, n), lambda i: (i, 0))
    full = lambda a: pl.BlockSpec(a.shape, lambda i: (0, 0))
    return pl.pallas_call(
        body, name="in_proj_bwd", grid=(t // tm,),
        in_specs=[row(ATTN_W), row(KV_W), row(KV_W), row(3 * CONV_W), full(w), row(D_MODEL), row(D_MODEL), full(g),
                  row(2 * HEAD_DIM), row(2 * HEAD_DIM), row(2 * HEAD_DIM)],
        out_specs=[row(D_MODEL), row(IN_W), pl.BlockSpec((1, D_MODEL), lambda i: (0, 0))],
        out_shape=[jax.ShapeDtypeStruct((t, D_MODEL), F32), jax.ShapeDtypeStruct((t, IN_W), BF16),
                   jax.ShapeDtypeStruct((1, D_MODEL), F32)],
        compiler_params=_params("arbitrary"),
    )(dq, dk, dv, dbch, w, dh1, h, g, *tabs)


class _Tiles:
    def __init__(self, t):
        self.tm = _row_tile(t, 640)
        self.ts = self.tm
        self.tabs = _rope_tables(t)
        self.bias = _attn_bias()


def _mixer_fwd(h, p, tl):
    a, q, k, v, b, c, hc = _in_proj(h, p["mix_pre_g"], p["w_in"], tl.tabs, tl.ts)
    o = _attn_fwd(q, k, v, tl.bias, p["sinks"], tl.tm)
    return (h, a, q, k, v, b, c, hc, o)


def _out_fwd(mixed, p, tl, deps=()):
    h, a, q, k, v, b, c, hc, o = mixed
    h1, y, z = _mix_out(h, o, b, c, hc, p["conv_w"], p["attn_out_g"], p["conv_out_g"], p["w_out"], p["mix_post_g"],
                        tl.ts, deps)
    return h1, mixed + (h1, y, z)


def _mlp_fwd(h1, saved, p, tl):
    h2, a2, act, f = _mlp(h1, p["mlp_pre_g"], p["w_up"], p["w_down"], p["mlp_post_g"], tl.tm)
    return h2, saved + (a2, act, f)


def _mlp_part_bwd(dh, saved, p, tl, deps=()):
    h1, a2, act, f = saved[9], saved[12], saved[13], saved[14]
    dh1, df, dup, dg2, dg1 = _mlp_bwd(dh, f, p["mlp_post_g"], act, p["w_down"], p["w_up"], h1, p["mlp_pre_g"], tl.tm,
                                      deps)
    g = {"w_down": [d.reshape(N_CHIPS, FF_CHUNK, D_MODEL) for d in _weight_grad(act, df, "grad_w_down")],
         "w_up": [d.reshape(N_CHIPS, D_MODEL, FF_CHUNK) for d in _weight_grad(a2, dup, "grad_w_up")],
         "mlp_post_g": dg2, "mlp_pre_g": dg1}
    return dh1, g


def _mix_out_part_bwd(dh1, saved, p, tl, deps=()):
    b, c, hc, o, y, z = saved[5], saved[6], saved[7], saved[8], saved[10], saved[11]
    dz, do, dbch, dgp, dga, dgc, dcw = _mix_out_bwd(dh1, z, p["mix_post_g"], p["w_out"], o, b, c, hc, p["conv_w"],
                                                    p["attn_out_g"], p["conv_out_g"], tl.ts, deps)
    g = {"w_out": [d.reshape(N_CHIPS, D_MODEL // N_CHIPS, D_MODEL) for d in _weight_grad(y, dz, "grad_w_out")],
         "mix_post_g": dgp, "attn_out_g": dga, "conv_out_g": dgc, "conv_w": dcw}
    return (dh1, do, dbch), g


def _attn_in_part_bwd(carry, saved, p, tl, deps=()):
    dh1, do, dbch = carry
    h_in, a, q, k, v, o = saved[0], saved[1], saved[2], saved[3], saved[4], saved[8]
    dq, dk, dv, dsink = _attn_bwd(q, k, v, o, do, tl.bias, p["sinks"], tl.tm, deps)
    dh, dproj, dgi = _in_proj_bwd(dq, dk, dv, dbch, p["w_in"], dh1, h_in, p["mix_pre_g"], tl.tabs, tl.ts)
    return dh, {"w_in": list(_weight_grad_in(a, dproj)), "mix_pre_g": dgi, "sinks": dsink[:, 0]}


def _place():
    return lax.axis_index("x"), lax.axis_index("y"), lax.axis_index("c")


def _other_chips(x, y):
    return [(1 - x, y), (x, 1 - y), (1 - x, 1 - y)]


_HBM = pl.BlockSpec(memory_space=pltpu.HBM)
_SEM = pl.BlockSpec(memory_space=pltpu.SEMAPHORE)
_EFFECT = pltpu.SideEffectType.DATAFLOW_SIDE_EFFECTING


class _Exchange:
    def __init__(self, name, bufs, plan, n, after=()):
        self.name, self.plan, nb = name, plan, len(bufs)
        n_in = nb + len(after)

        def body(*refs):
            send, recv, token = refs[n_in], refs[n_in + 1], refs[-1]
            for k, (src, dst, target, _) in enumerate(plan(refs[:nb])):
                pltpu.make_async_remote_copy(src_ref=src, dst_ref=dst, send_sem=send.at[k], recv_sem=recv.at[k],
                                             device_id=target, device_id_type=MESH).start()
            token[...] = jnp.zeros_like(token)

        outs = pl.pallas_call(
            body, name=name + "_start",
            out_shape=(pltpu.SemaphoreType.DMA((n,)), pltpu.SemaphoreType.DMA((n,)),
                       *[pltpu.HBM(b.shape, b.dtype) for b in bufs], jax.ShapeDtypeStruct((8, 128), F32)),
            in_specs=[_HBM] * nb + [pl.BlockSpec(memory_space=pl.ANY)] * len(after),
            out_specs=(_SEM, _SEM, *[_HBM] * nb, pl.BlockSpec(memory_space=pltpu.VMEM)),
            input_output_aliases={i: 2 + i for i in range(nb)},
            compiler_params=pltpu.CompilerParams(has_side_effects=_EFFECT),
        )(*[pltpu.with_memory_space_constraint(b, pltpu.HBM) for b in bufs], *after)
        self.send, self.recv, self.bufs, self.token = outs[0], outs[1], list(outs[2:2 + nb]), outs[-1]

    def wait(self, *after):
        plan, nb = self.plan, len(self.bufs)

        def body(*refs):
            send, recv = refs[nb], refs[nb + 1]
            for k, (src, _, target, land) in enumerate(plan(refs[:nb])):
                cp = pltpu.make_async_remote_copy(src_ref=src, dst_ref=land, send_sem=send.at[k], recv_sem=recv.at[k],
                                                  device_id=target, device_id_type=MESH)
                cp.wait_send()
                cp.wait_recv()

        outs = pl.pallas_call(
            body, name=self.name + "_wait", out_shape=[pltpu.HBM(b.shape, b.dtype) for b in self.bufs],
            in_specs=[_HBM] * nb + [_SEM, _SEM] + [pl.BlockSpec(memory_space=pl.ANY)] * len(after),
            out_specs=[_HBM] * nb, input_output_aliases={i: i for i in range(nb)},
            compiler_params=pltpu.CompilerParams(has_side_effects=_EFFECT),
        )(*self.bufs, self.send, self.recv, *after)
        return list(outs)


def _gather_plan(n):
    def plan(refs):
        x, y, c = _place()
        me = 2 * x + y
        return [(refs[a].at[me], refs[a].at[me], (px, py, c), refs[a].at[2 * px + py])
                for a in range(n) for px, py in _other_chips(x, y)]

    return plan


def _peers():
    x, y, c = _place()
    return [(k - 1, (x ^ (k >> 2), y ^ ((k >> 1) & 1), c ^ (k & 1))) for k in range(1, N_DEV)]


def _scatter_plan(n, half_rows):
    def plan(refs):
        out = []
        for a in range(n):
            hr = half_rows[a]
            for k, (px, py, pc) in _peers():
                out.append((refs[a].at[2 * px + py, pl.ds(pc * hr, hr)], refs[n + a].at[k], (px, py, pc),
                            refs[n + a].at[k]))
        return out

    return plan


def _join_plan(n):
    def plan(refs):
        x, y, c = _place()
        return [(refs[a].at[c], refs[a].at[c], (x, y, 1 - c), refs[a].at[1 - c]) for a in range(n)]

    return plan


def _sum_parts(g, q):
    rows, cols = g.shape[1], g.shape[2]
    hr = rows // 2
    tr = min(hr, 256)
    per = hr // tr
    x, y, c = _place()
    where = jnp.stack([2 * x + y, c]).astype(jnp.int32)

    def body(where_ref, g_ref, q_ref, o_ref):
        total = g_ref[...]
        for k in range(N_DEV - 1):
            total = total + q_ref[k].astype(F32)
        o_ref[...] = total

    return pl.pallas_call(
        body, name="sum_parts",
        grid_spec=pltpu.PrefetchScalarGridSpec(
            num_scalar_prefetch=1, grid=(per,),
            in_specs=[pl.BlockSpec((None, tr, cols), lambda i, where_ref: (where_ref[0], where_ref[1] * per + i, 0)),
                      pl.BlockSpec((N_DEV - 1, tr, cols), lambda i, where_ref: (0, i, 0))],
            out_specs=pl.BlockSpec((None, tr, cols), lambda i, where_ref: (where_ref[1], i, 0))),
        out_shape=jax.ShapeDtypeStruct((2, hr, cols), F32),
        compiler_params=_params("parallel"),
    )(where, g, q)


def _sum_devices(packed):
    def body(p_ref, o_ref, land, send_sems, recv_sems):
        x, y, c = _place()
        me = 4 * x + 2 * y + c
        land[me] = p_ref[...]
        sends = []
        for k in range(1, N_DEV):
            px, py, pc = x ^ (k >> 2), y ^ ((k >> 1) & 1), c ^ (k & 1)
            cp = pltpu.make_async_remote_copy(src_ref=p_ref, dst_ref=land.at[me], send_sem=send_sems.at[k - 1],
                                              recv_sem=recv_sems.at[k - 1], device_id=(px, py, pc), device_id_type=MESH)
            cp.start()
            sends.append(cp)
        for k in range(1, N_DEV):
            px, py, pc = x ^ (k >> 2), y ^ ((k >> 1) & 1), c ^ (k & 1)
            pltpu.make_async_remote_copy(src_ref=p_ref, dst_ref=land.at[4 * px + 2 * py + pc],
                                         send_sem=send_sems.at[k - 1], recv_sem=recv_sems.at[k - 1],
                                         device_id=(px, py, pc), device_id_type=MESH).wait_recv()
        for cp in sends:
            cp.wait_send()
        total = land[0]
        for d in range(1, N_DEV):
            total = total + land[d]
        o_ref[...] = total

    vm = pl.BlockSpec(memory_space=pltpu.VMEM)
    return pl.pallas_call(
        body, name="sum_devices", in_specs=[vm], out_specs=vm,
        out_shape=jax.ShapeDtypeStruct(packed.shape, F32),
        scratch_shapes=[pltpu.VMEM((N_DEV,) + packed.shape, F32), pltpu.SemaphoreType.DMA((N_DEV - 1,)),
                        pltpu.SemaphoreType.DMA((N_DEV - 1,))],
    )(packed)


def _adamw_math(w, g, m, v):
    m = ADAM_B1 * m + (1.0 - ADAM_B1) * g
    v = ADAM_B2 * v + (1.0 - ADAM_B2) * jnp.square(g)
    m_hat = m / (1.0 - ADAM_B1 ** ADAM_STEP)
    v_hat = v / (1.0 - ADAM_B2 ** ADAM_STEP)
    delta = -ADAM_LR * (m_hat / (jnp.sqrt(v_hat) + ADAM_EPS) + ADAM_WD * w)
    return delta, m, v


def _adamw_large(layer, w, halves, m, v, other):
    _, rows, cols = w.shape
    tr = min(rows // 2, 256)
    per = rows // 2 // tr

    def body(w_ref, g_ref, m_ref, v_ref, *rest):
        g_out, d_ref, nm_ref, nv_ref = rest[-4:]
        g = g_ref[...]
        g_out[...] = g
        d_ref[...], nm_ref[...], nv_ref[...] = _adamw_math(w_ref[...], g, m_ref[...], v_ref[...])

    blk = pl.BlockSpec((None, tr, cols), lambda i: (layer, i, 0))
    half = pl.BlockSpec((None, tr, cols), lambda i: (i // per, i % per, 0))
    kept = [] if other is None else list(other)
    return pl.pallas_call(
        body, name="adamw_large", grid=(rows // tr,),
        in_specs=[blk, half, blk, blk] + [pl.BlockSpec(memory_space=pl.ANY)] * len(kept), out_specs=[blk] * 4,
        out_shape=[jax.ShapeDtypeStruct(w.shape, F32)] * 4,
        input_output_aliases={4 + k: k for k in range(len(kept))},
        compiler_params=_params("parallel"),
    )(w, halves, m, v, *kept)


def _adamw_small(ws, gs, ms, vs):
    n = len(ws)

    def body(*refs):
        w_r, g_r, m_r, v_r = refs[:n], refs[n:2 * n], refs[2 * n:3 * n], refs[3 * n:4 * n]
        d_r, nm_r, nv_r = refs[4 * n:5 * n], refs[5 * n:6 * n], refs[6 * n:]
        for a in range(n):
            d_r[a][...], nm_r[a][...], nv_r[a][...] = _adamw_math(w_r[a][...], g_r[a][...], m_r[a][...], v_r[a][...])

    vm = pl.BlockSpec(memory_space=pltpu.VMEM)
    outs = pl.pallas_call(
        body, name="adamw_small", in_specs=[vm] * (4 * n), out_specs=[vm] * (3 * n),
        out_shape=[jax.ShapeDtypeStruct(w.shape, F32) for w in ws] * 3,
    )(*ws, *gs, *ms, *vs)
    return outs[:n], outs[n:2 * n], outs[2 * n:]


_LARGE = ("w_in", "w_out", "w_up", "w_down")
_SMALL = ("meta_tokens", "mix_pre_g", "conv_w", "sinks", "attn_out_g", "conv_out_g", "mix_post_g", "mlp_pre_g",
          "mlp_post_g")
_ORDER = ("meta_tokens", "mix_pre_g", "w_in", "conv_w", "sinks", "attn_out_g", "conv_out_g", "w_out", "mix_post_g",
          "mlp_pre_g", "w_up", "w_down", "mlp_post_g")


class _Reduce:
    def __init__(self, name, grads, after=()):
        self.name, self.n = name, len(grads)
        self.own = [g for g, _ in grads]
        half_rows = [g.shape[1] // 2 for g in self.own]
        zones = [lax.empty((N_DEV - 1, hr, g.shape[2]), BF16) for g, hr in zip(self.own, half_rows)]
        self.exchange = _Exchange(name + "_scatter", [b for _, b in grads] + zones, _scatter_plan(self.n, half_rows),
                                  (N_DEV - 1) * self.n, after)

    @property
    def token(self):
        return self.exchange.token

    def join(self, *after):
        bufs = self.exchange.wait(*after)
        halves = [_sum_parts(g, q) for g, q in zip(self.own, bufs[self.n:])]
        self.exchange = _Exchange(self.name + "_join", halves, _join_plan(self.n), self.n)

    def done(self, *after):
        return self.exchange.wait(*after)


def _pad_cols(a, n=D_MODEL):
    return jnp.pad(a, ((0, 0), (0, n - a.shape[1])))


def kernel(x, meta_tokens, mix_pre_g, w_in, conv_w, sinks, attn_out_g, conv_out_g, w_out, mix_post_g, mlp_pre_g, w_up, w_down, mlp_post_g, loss_target, m_meta_tokens, m_mix_pre_g, m_w_in, m_conv_w, m_sinks, m_attn_out_g, m_conv_out_g, m_w_out, m_mix_post_g, m_mlp_pre_g, m_w_up, m_w_down, m_mlp_post_g, v_meta_tokens, v_mix_pre_g, v_w_in, v_conv_w, v_sinks, v_attn_out_g, v_conv_out_g, v_w_out, v_mix_post_g, v_mlp_pre_g, v_w_up, v_w_down, v_mlp_post_g):
    w = dict(meta_tokens=meta_tokens, mix_pre_g=mix_pre_g, w_in=w_in, conv_w=conv_w, sinks=sinks,
             attn_out_g=attn_out_g, conv_out_g=conv_out_g, w_out=w_out, mix_post_g=mix_post_g, mlp_pre_g=mlp_pre_g,
             w_up=w_up, w_down=w_down, mlp_post_g=mlp_post_g)
    m = dict(meta_tokens=m_meta_tokens, mix_pre_g=m_mix_pre_g, w_in=m_w_in, conv_w=m_conv_w, sinks=m_sinks,
             attn_out_g=m_attn_out_g, conv_out_g=m_conv_out_g, w_out=m_w_out, mix_post_g=m_mix_post_g,
             mlp_pre_g=m_mlp_pre_g, w_up=m_w_up, w_down=m_w_down, mlp_post_g=m_mlp_post_g)
    v = dict(meta_tokens=v_meta_tokens, mix_pre_g=v_mix_pre_g, w_in=v_w_in, conv_w=v_conv_w, sinks=v_sinks,
             attn_out_g=v_attn_out_g, conv_out_g=v_conv_out_g, w_out=v_w_out, mix_post_g=v_mix_post_g,
             mlp_pre_g=v_mlp_pre_g, w_up=v_w_up, w_down=v_w_down, mlp_post_g=v_mlp_post_g)
    chip = 2 * lax.axis_index("x") + lax.axis_index("y")
    tl = _Tiles(x.shape[1] + BLOCK)

    def zone(quarter):
        return lax.dynamic_update_slice(lax.empty((N_CHIPS,) + quarter.shape, quarter.dtype), quarter[None],
                                        (chip,) + (0,) * quarter.ndim)

    zones = {n: [zone(w[n][l].astype(BF16)) for l in range(DEPTH)] for n in _LARGE}
    first = _Exchange("gather_first", [zones["w_in"][0], zone(w["conv_w"]), zone(w["meta_tokens"])], _gather_plan(3), 9)
    rest = _Exchange("gather_rest", [zones[n][0] for n in ("w_out", "w_up", "w_down")], _gather_plan(3), 9,
                     [first.token])

    def whole_in(quarters):
        return jnp.transpose(quarters, (1, 0, 2)).reshape(D_MODEL, IN_W)

    h = jnp.concatenate([jnp.zeros((BLOCK, D_MODEL), F32), x[0]], axis=0)
    q_in, q_conv, q_meta = first.wait(rest.token, *tl.tabs, tl.bias, h)
    conv_whole = jnp.transpose(q_conv, (1, 2, 0, 3)).reshape(DEPTH, CONV_K, CONV_W)
    meta = jnp.transpose(q_meta, (1, 0, 2)).reshape(N_META, D_MODEL)
    p = [{"conv_w": conv_whole[l], "sinks": w["sinks"][l]} for l in range(DEPTH)]
    for l in range(DEPTH):
        for n in ("mix_pre_g", "attn_out_g", "conv_out_g", "mix_post_g", "mlp_pre_g", "mlp_post_g"):
            p[l][n] = w[n][l][None, :]

    h = lax.dynamic_update_slice(h, meta, (LEAD_PAD, 0))
    p[0]["w_in"] = whole_in(q_in)
    mixed = _mixer_fwd(h, p[0], tl)
    second = _Exchange("gather_second", [zones["w_in"][1], zones["w_out"][1]], _gather_plan(2), 6, [mixed[-1]])
    second_mlp = _Exchange("gather_second_mlp", [zones["w_up"][1], zones["w_down"][1]], _gather_plan(2), 6,
                           [second.token])
    p[0]["w_out"], p[0]["w_up"], p[0]["w_down"] = rest.wait(second_mlp.token)
    h1, saved0 = _out_fwd(mixed, p[0], tl)
    h, saved0 = _mlp_fwd(h1, saved0, p[0], tl)
    q_in, p[1]["w_out"] = second.wait(h)
    p[1]["w_in"] = whole_in(q_in)
    h1, saved1 = _out_fwd(_mixer_fwd(h, p[1], tl), p[1], tl)
    p[1]["w_up"], p[1]["w_down"] = second_mlp.wait(h1)
    h, saved1 = _mlp_fwd(h1, saved1, p[1], tl)
    loss_tile, dh = _loss_head(h, loss_target[0], tl.tm)
    loss = lax.psum(loss_tile[0, 0], ("x", "y", "c"))

    def adamw(layer, halves, other):
        return {n: _adamw_large(layer, w[n], halves[n], m[n], v[n], None if other is None else other[n])
                for n in halves}

    dh1, g1 = _mlp_part_bwd(dh, saved1, p[1], tl)
    carry, gm = _mix_out_part_bwd(dh1, saved1, p[1], tl)
    dh, gi = _attn_in_part_bwd(carry, saved1, p[1], tl)
    g1.update(gm, **gi)
    red1 = _Reduce("reduce1", [g1[n] for n in _LARGE])
    dh1, g0 = _mlp_part_bwd(dh, saved0, p[0], tl, [red1.token])
    red1.join(g0["w_down"][0])
    carry, gm = _mix_out_part_bwd(dh1, saved0, p[0], tl, [red1.token])
    first0 = ("w_up", "w_down", "w_out")
    g0.update(gm)
    red0a = _Reduce("reduce0a", [g0[n] for n in first0])
    dh0, gi = _attn_in_part_bwd(carry, saved0, p[0], tl, [red0a.token])
    g0.update(gi)
    red0b = _Reduce("reduce0b", [g0["w_in"]])
    grad_x = dh0[BLOCK:][None]
    grads = {n: [g0[n], g1[n]] for n in g0 if n not in _LARGE}

    rows = [dh0[LEAD_PAD:BLOCK]]
    for n in ("mix_pre_g", "mix_post_g", "mlp_pre_g", "mlp_post_g"):
        rows += grads[n]
    rows += [jnp.concatenate([grads["attn_out_g"][l], grads["conv_out_g"][l]], axis=1) for l in range(DEPTH)]
    rows.append(jnp.concatenate(grads["conv_w"], axis=1))
    rows.append(_pad_cols(jnp.concatenate(grads["sinks"])[None, :]))
    packed = jnp.concatenate(rows, axis=0)
    packed = jnp.pad(packed, ((0, SMALL_ROWS - packed.shape[0]), (0, 0)))
    total = _sum_devices(packed)
    r0 = N_META
    small = {
        "meta_tokens": lax.dynamic_slice(total[:N_META], (0, chip * (D_MODEL // N_CHIPS)), (N_META, D_MODEL // N_CHIPS)),
        "mix_pre_g": total[r0:r0 + 2], "mix_post_g": total[r0 + 2:r0 + 4], "mlp_pre_g": total[r0 + 4:r0 + 6],
        "mlp_post_g": total[r0 + 6:r0 + 8],
        "attn_out_g": total[r0 + 8:r0 + 10, :ATTN_W], "conv_out_g": total[r0 + 8:r0 + 10, ATTN_W:],
        "conv_w": lax.dynamic_slice(total[r0 + 10:r0 + 13].reshape(CONV_K, DEPTH, CONV_W).transpose(1, 0, 2),
                                    (0, 0, chip * (CONV_W // N_CHIPS)), (DEPTH, CONV_K, CONV_W // N_CHIPS)),
        "sinks": total[r0 + 13, :DEPTH * N_Q_HEADS].reshape(DEPTH, N_Q_HEADS),
    }

    ds, nms, nvs = _adamw_small([w[n] for n in _SMALL], [small[n] for n in _SMALL], [m[n] for n in _SMALL],
                                [v[n] for n in _SMALL])
    done1 = adamw(1, dict(zip(_LARGE, red1.done(red0b.token))), None)
    red0a.join(ds[0], grad_x, *[done1[n][0] for n in _LARGE])
    red0b.join(red0a.token)
    done0 = adamw(0, dict(zip(first0, red0a.done(red0b.token))), done1)
    done0.update(adamw(0, {"w_in": red0b.done(done0["w_down"][0])[0]}, done1))
    grad, delta, new_m, new_v = {}, {}, {}, {}
    for n in _LARGE:
        grad[n], delta[n], new_m[n], new_v[n] = done0[n]
    for i, n in enumerate(_SMALL):
        grad[n], delta[n], new_m[n], new_v[n] = small[n], ds[i], nms[i], nvs[i]
    return (loss, grad_x, *[grad[n] for n in _ORDER], *[delta[n] for n in _ORDER], *[new_m[n] for n in _ORDER],
            *[new_v[n] for n in _ORDER])
```

```python
import functools

import jax
import jax.numpy as jnp
from jax import lax
from jax.experimental import pallas as pl
from jax.experimental.pallas import tpu as pltpu

F32 = jnp.float32
BF16 = jnp.bfloat16

D_MODEL = 1024
DEPTH = 2
N_META = 16
ATTN_W = 512
CONV_W = 512
HEAD_DIM = 64
N_Q_HEADS = 8
N_KV_HEADS = 2
GROUP = N_Q_HEADS // N_KV_HEADS
KV_W = N_KV_HEADS * HEAD_DIM
CONV_K = 3
BLOCK = 128
LEAD_PAD = BLOCK - N_META
ROPE_THETA = 500000.0
ROT_DIM = HEAD_DIM // 4
ROT_HALF = ROT_DIM // 2
D_FF = 4 * D_MODEL
IN_W = ATTN_W + 2 * KV_W + 3 * CONV_W
QKV_W = ATTN_W + 2 * KV_W
EPS = 1e-6
SCALE = HEAD_DIM ** -0.5
FF_CHUNK = 1024
N_CHIPS = 4
N_DEV = 8

ADAM_LR = 0.001
ADAM_B1 = 0.9
ADAM_B2 = 0.999
ADAM_EPS = 1e-08
ADAM_WD = 0.01
ADAM_STEP = 10

V7X_VMEM_LIMIT = 56 * 1024 * 1024
SMALL_ROWS = 32

MESH = pl.DeviceIdType.MESH


def _params(*sem):
    return pltpu.CompilerParams(dimension_semantics=sem, vmem_limit_bytes=V7X_VMEM_LIMIT)


def _block_rows(n):
    return max(r for r in range(16, min(n, 256) + 1, 16) if n % r == 0)


def _row_tile(t, most):
    nb = t // BLOCK
    for b in range(most // BLOCK, 0, -1):
        if nb % b == 0:
            return b * BLOCK
    return BLOCK


def _behind(body, deps):
    n = len(deps)

    def wrapped(*refs):
        body(*refs[n:])

    return wrapped, [pl.BlockSpec(memory_space=pl.ANY)] * n


def _rms(x, g):
    r = lax.rsqrt(jnp.mean(x * x, axis=-1, keepdims=True) + EPS)
    return x * r * g


def _rms_bwd(dy, x, g):
    r = lax.rsqrt(jnp.mean(x * x, axis=-1, keepdims=True) + EPS)
    xh = x * r
    dg = jnp.sum(dy * xh, axis=0, keepdims=True)
    dxh = dy * g
    dx = r * (dxh - xh * jnp.mean(dxh * xh, axis=-1, keepdims=True))
    return dx, dg


def _rope(x, cos, sa, sb):
    n = x.shape[-1]
    return x * cos + pltpu.roll(x, n - ROT_HALF, 1) * sa + pltpu.roll(x, ROT_HALF, 1) * sb


def _rope_bwd(dy, cos, sa, sb):
    n = dy.shape[-1]
    return dy * cos + pltpu.roll(dy * sa, ROT_HALF, 1) + pltpu.roll(dy * sb, n - ROT_HALF, 1)


def _rope_tables(t):
    pos = lax.broadcasted_iota(jnp.int32, (t, 2 * HEAD_DIM), 0).astype(F32) - LEAD_PAD
    dim = lax.broadcasted_iota(jnp.int32, (t, 2 * HEAD_DIM), 1) % HEAD_DIM
    pair = (dim % ROT_HALF).astype(F32)
    inv_freq = jnp.power(jnp.float32(ROPE_THETA), -(2.0 * pair) / ROT_DIM)
    ang = pos * inv_freq
    cos, sin = jnp.cos(ang), jnp.sin(ang)
    return (jnp.where(dim < ROT_DIM, cos, 1.0), jnp.where(dim < ROT_HALF, -sin, 0.0),
            jnp.where((dim >= ROT_HALF) & (dim < ROT_DIM), sin, 0.0))


def _in_proj(h, g, w, tabs, tm):
    t = h.shape[0]

    def body(h_ref, g_ref, w_ref, c_ref, sa_ref, sb_ref, a_ref, q_ref, k_ref, v_ref, b_ref, cg_ref, hc_ref):
        a = _rms(h_ref[...], g_ref[...]).astype(BF16)
        a_ref[...] = a
        p = lax.dot_general(a, w_ref[...], (((1,), (1,)), ((), ())), preferred_element_type=F32)
        cos, sa, sb = c_ref[...], sa_ref[...], sb_ref[...]
        rep = ATTN_W // (2 * HEAD_DIM)
        q = _rope(p[:, :ATTN_W], jnp.tile(cos, (1, rep)), jnp.tile(sa, (1, rep)), jnp.tile(sb, (1, rep)))
        q_ref[...] = (q * SCALE).astype(BF16)
        k_ref[...] = _rope(p[:, ATTN_W:ATTN_W + KV_W], cos, sa, sb).astype(BF16)
        v_ref[...] = p[:, ATTN_W + KV_W:QKV_W].astype(BF16)
        b_ref[...] = p[:, QKV_W:QKV_W + CONV_W]
        cg_ref[...] = p[:, QKV_W + CONV_W:QKV_W + 2 * CONV_W]
        hc_ref[...] = p[:, QKV_W + 2 * CONV_W:]

    row = lambda n: pl.BlockSpec((tm, n), lambda i: (i, 0))
    full = lambda a: pl.BlockSpec(a.shape, lambda i: (0, 0))
    return pl.pallas_call(
        body, name="in_proj", grid=(t // tm,),
        in_specs=[row(D_MODEL), full(g), full(w), row(2 * HEAD_DIM), row(2 * HEAD_DIM), row(2 * HEAD_DIM)],
        out_specs=[row(D_MODEL), row(ATTN_W), row(KV_W), row(KV_W), row(CONV_W), row(CONV_W), row(CONV_W)],
        out_shape=[jax.ShapeDtypeStruct((t, D_MODEL), BF16), jax.ShapeDtypeStruct((t, ATTN_W), BF16),
                   jax.ShapeDtypeStruct((t, KV_W), BF16), jax.ShapeDtypeStruct((t, KV_W), BF16),
                   jax.ShapeDtypeStruct((t, CONV_W), F32), jax.ShapeDtypeStruct((t, CONV_W), F32),
                   jax.ShapeDtypeStruct((t, CONV_W), F32)],
        compiler_params=_params("parallel"),
    )(h, g, w, *tabs)


def _attn_bias():
    r = lax.broadcasted_iota(jnp.int32, (3, BLOCK, 2 * BLOCK), 1)
    c = lax.broadcasted_iota(jnp.int32, (3, BLOCK, 2 * BLOCK), 2)
    i = lax.broadcasted_iota(jnp.int32, (3, BLOCK, 2 * BLOCK), 0)
    ok = (c > r) & (c <= r + BLOCK) & (c + (i - 1) * BLOCK >= LEAD_PAD)
    return jnp.where(ok, 0.0, -jnp.inf).astype(F32)


def _attn_scores(qh, kg, bias):
    return lax.dot_general(qh, kg, (((1,), (1,)), ((), ())), preferred_element_type=F32) + bias


def _attn_probs(s, sk):
    m = jnp.maximum(jnp.max(s, axis=-1, keepdims=True), sk)
    e = jnp.exp(s - m)
    es = jnp.exp(sk - m)
    rden = 1.0 / (jnp.sum(e, axis=-1, keepdims=True) + es)
    return e * rden, es * rden


def _head(hh):
    return slice(hh * HEAD_DIM, (hh + 1) * HEAD_DIM)


def _two_blocks(ref, i):
    prev = jnp.maximum(i - 1, 0)
    return jnp.concatenate([ref[pl.ds(pl.multiple_of(prev * BLOCK, BLOCK), BLOCK), :],
                            ref[pl.ds(pl.multiple_of(i * BLOCK, BLOCK), BLOCK), :]], axis=0)


def _attn_fwd(q, k, v, bias, sinks, tm):
    t = q.shape[0]
    per_step = tm // BLOCK
    heads = range(N_Q_HEADS)

    def body(s_ref, q_ref, k_ref, v_ref, bias_ref, o_ref):
        for b in range(per_step):
            i = pl.program_id(0) * per_step + b
            rows = slice(b * BLOCK, (b + 1) * BLOCK)
            kc, vc = _two_blocks(k_ref, i), _two_blocks(v_ref, i)
            bias_i = bias_ref[jnp.minimum(i, 2)]
            scores = [_attn_scores(q_ref[rows, _head(hh)], kc[:, _head(hh // GROUP)], bias_i) for hh in heads]
            probs = [_attn_probs(scores[hh], s_ref[hh])[0].astype(BF16) for hh in heads]
            for hh in heads:
                o_ref[rows, _head(hh)] = jnp.dot(probs[hh], vc[:, _head(hh // GROUP)], preferred_element_type=F32)

    whole = pl.BlockSpec((t, KV_W), lambda i: (0, 0))
    return pl.pallas_call(
        body, name="attn_fwd", grid=(t // tm,),
        in_specs=[pl.BlockSpec(memory_space=pltpu.SMEM), pl.BlockSpec((tm, ATTN_W), lambda i: (i, 0)), whole, whole,
                  pl.BlockSpec(bias.shape, lambda i: (0, 0, 0))],
        out_specs=pl.BlockSpec((tm, ATTN_W), lambda i: (i, 0)),
        out_shape=jax.ShapeDtypeStruct((t, ATTN_W), F32),
        compiler_params=_params("parallel"),
    )(sinks, q, k, v, bias)


def _shift_rows(u, halo, n):
    r = pltpu.roll(u, n, 0)
    hr = pltpu.roll(halo, n, 0)
    idx = lax.broadcasted_iota(jnp.int32, hr.shape, 0)
    return jnp.concatenate([jnp.where(idx < n, hr, r[:8]), r[8:]], axis=0)


def _advance_rows(u, halo, n):
    rows = u.shape[0]
    r = pltpu.roll(u, rows - n, 0)
    hr = pltpu.roll(halo, 8 - n, 0)
    idx = lax.broadcasted_iota(jnp.int32, hr.shape, 0)
    return jnp.concatenate([r[:rows - 8], jnp.where(idx >= 8 - n, hr, r[rows - 8:])], axis=0)


def _mix_out(h, o, b, c, hc, cw, ga, gc, w, gp, tm, deps=()):
    t = h.shape[0]

    def body(h_ref, o_ref, b_ref, c_ref, hc_ref, cw_ref, ga_ref, gc_ref, w_ref, gp_ref, h1_ref, y_ref, z_ref, halo):
        @pl.when(pl.program_id(0) == 0)
        def _():
            halo[...] = jnp.zeros_like(halo)

        u = c_ref[...] * hc_ref[...]
        cv = cw_ref[0:1, :] * _shift_rows(u, halo[...], 2) + cw_ref[1:2, :] * _shift_rows(u, halo[...], 1) \
            + cw_ref[2:3, :] * u
        halo[...] = u[tm - 8:]
        yc = b_ref[...] * cv
        y = jnp.concatenate([_rms(o_ref[...], ga_ref[...]), _rms(yc, gc_ref[...])], axis=1).astype(BF16)
        y_ref[...] = y
        z = jnp.dot(y, w_ref[...].reshape(D_MODEL, D_MODEL), preferred_element_type=F32)
        z_ref[...] = z
        h1_ref[...] = h_ref[...] + _rms(z, gp_ref[...])

    row = lambda n: pl.BlockSpec((tm, n), lambda i: (i, 0))
    full = lambda a: pl.BlockSpec(a.shape, lambda i: (0,) * a.ndim)
    body, dep_specs = _behind(body, deps)
    return pl.pallas_call(
        body, name="mix_out", grid=(t // tm,),
        in_specs=dep_specs + [row(D_MODEL), row(ATTN_W), row(CONV_W), row(CONV_W), row(CONV_W), full(cw), full(ga),
                              full(gc), full(w), full(gp)],
        out_specs=[row(D_MODEL), row(D_MODEL), row(D_MODEL)],
        out_shape=[jax.ShapeDtypeStruct((t, D_MODEL), F32), jax.ShapeDtypeStruct((t, D_MODEL), BF16),
                   jax.ShapeDtypeStruct((t, D_MODEL), F32)],
        scratch_shapes=[pltpu.VMEM((8, CONV_W), F32)],
        compiler_params=_params("arbitrary"),
    )(*deps, h, o, b, c, hc, cw, ga, gc, w, gp)


def _mlp(h1, g1, wu, wd, g2, tm):
    t = h1.shape[0]
    nj = D_FF // FF_CHUNK

    def body(h1_ref, g1_ref, wu_ref, wd_ref, g2_ref, h2_ref, a2_ref, act_ref, f_ref, acc):
        j = pl.program_id(1)

        @pl.when(j == 0)
        def _():
            a2_ref[...] = _rms(h1_ref[...], g1_ref[...]).astype(BF16)

        up = jnp.dot(a2_ref[...], wu_ref[...], preferred_element_type=F32)
        act = jnp.square(jnp.maximum(up, 0.0)).astype(BF16)
        act_ref[...] = act
        part = jnp.dot(act, wd_ref[...], preferred_element_type=F32)

        @pl.when(j == 0)
        def _():
            acc[...] = part

        @pl.when(j > 0)
        def _():
            acc[...] += part

        @pl.when(j == nj - 1)
        def _():
            f = acc[...]
            f_ref[...] = f
            h2_ref[...] = h1_ref[...] + _rms(f, g2_ref[...])

    row = pl.BlockSpec((tm, D_MODEL), lambda i, j: (i, 0))
    vec = pl.BlockSpec((1, D_MODEL), lambda i, j: (0, 0))
    quarter = pl.BlockSpec((None, D_MODEL, FF_CHUNK), lambda i, j: (j, 0, 0))
    return pl.pallas_call(
        body, name="mlp", grid=(t // tm, nj),
        in_specs=[row, vec, quarter, quarter, vec],
        out_specs=[row, row, pl.BlockSpec((tm, FF_CHUNK), lambda i, j: (i, j)), row],
        out_shape=[jax.ShapeDtypeStruct((t, D_MODEL), F32), jax.ShapeDtypeStruct((t, D_MODEL), BF16),
                   jax.ShapeDtypeStruct((t, D_FF), BF16), jax.ShapeDtypeStruct((t, D_MODEL), F32)],
        scratch_shapes=[pltpu.VMEM((tm, D_MODEL), F32)],
        compiler_params=_params("parallel", "arbitrary"),
    )(h1, g1, wu, wd, g2)


def _loss_head(h, target, tm):
    t = h.shape[0]
    per_step = tm // BLOCK

    def body(h_ref, *rest):
        t_refs, (loss_ref, dh_ref) = rest[:per_step], rest[per_step:]
        i = pl.program_id(0)

        @pl.when(i == 0)
        def _():
            loss_ref[...] = jnp.zeros_like(loss_ref)

        total = jnp.zeros((), F32)
        for b in range(per_step):
            rows = slice(b * BLOCK, (b + 1) * BLOCK)
            err = h_ref[rows, :] - t_refs[b][...]
            if b == 0:
                err = jnp.where(i == 0, 0.0, err)
            dh_ref[rows, :] = err * (1.0 / D_MODEL)
            total = total + jnp.sum(err * err)
        loss_ref[...] += total * (0.5 / D_MODEL)

    def target_block(b):
        return pl.BlockSpec((BLOCK, D_MODEL), lambda i: (jnp.maximum(i * per_step + b - 1, 0), 0))

    return pl.pallas_call(
        body, name="loss_head", grid=(t // tm,),
        in_specs=[pl.BlockSpec((tm, D_MODEL), lambda i: (i, 0))] + [target_block(b) for b in range(per_step)],
        out_specs=[pl.BlockSpec((8, 128), lambda i: (0, 0)), pl.BlockSpec((tm, D_MODEL), lambda i: (i, 0))],
        out_shape=[jax.ShapeDtypeStruct((8, 128), F32), jax.ShapeDtypeStruct((t, D_MODEL), F32)],
        compiler_params=_params("arbitrary"),
    )(h, *([target] * per_step))


def _mlp_bwd(dh2, f, g2, act, wd, wu, h1, g1, tm, deps=()):
    t = dh2.shape[0]
    nj = D_FF // FF_CHUNK

    def body(dh2_ref, f_ref, g2_ref, act_ref, wd_ref, wu_ref, h1_ref, g1_ref, dh1_ref, df_ref, dup_ref, dg2_ref,
             dg1_ref, acc):
        i, j = pl.program_id(0), pl.program_id(1)

        @pl.when((i == 0) & (j == 0))
        def _():
            dg2_ref[...] = jnp.zeros_like(dg2_ref)
            dg1_ref[...] = jnp.zeros_like(dg1_ref)

        @pl.when(j == 0)
        def _():
            df, dg = _rms_bwd(dh2_ref[...], f_ref[...], g2_ref[...])
            df_ref[...] = df.astype(BF16)
            dg2_ref[...] += dg

        dact = lax.dot_general(df_ref[...], wd_ref[...], (((1,), (1,)), ((), ())), preferred_element_type=F32)
        dup = (dact * (2.0 * jnp.sqrt(act_ref[...].astype(F32)))).astype(BF16)
        dup_ref[...] = dup
        part = lax.dot_general(dup, wu_ref[...], (((1,), (1,)), ((), ())), preferred_element_type=F32)

        @pl.when(j == 0)
        def _():
            acc[...] = part

        @pl.when(j > 0)
        def _():
            acc[...] += part

        @pl.when(j == nj - 1)
        def _():
            dx, dg = _rms_bwd(acc[...], h1_ref[...], g1_ref[...])
            dh1_ref[...] = dh2_ref[...] + dx
            dg1_ref[...] += dg

    row = pl.BlockSpec((tm, D_MODEL), lambda i, j: (i, 0))
    vec = pl.BlockSpec((1, D_MODEL), lambda i, j: (0, 0))
    chunk = pl.BlockSpec((tm, FF_CHUNK), lambda i, j: (i, j))
    quarter = pl.BlockSpec((None, D_MODEL, FF_CHUNK), lambda i, j: (j, 0, 0))
    body, dep_specs = _behind(body, deps)
    return pl.pallas_call(
        body, name="mlp_bwd", grid=(t // tm, nj),
        in_specs=dep_specs + [row, row, vec, chunk, quarter, quarter, row, vec],
        out_specs=[row, row, chunk, vec, vec],
        out_shape=[jax.ShapeDtypeStruct((t, D_MODEL), F32), jax.ShapeDtypeStruct((t, D_MODEL), BF16),
                   jax.ShapeDtypeStruct((t, D_FF), BF16), jax.ShapeDtypeStruct((1, D_MODEL), F32),
                   jax.ShapeDtypeStruct((1, D_MODEL), F32)],
        scratch_shapes=[pltpu.VMEM((tm, D_MODEL), F32)],
        compiler_params=_params("arbitrary", "arbitrary"),
    )(*deps, dh2, f, g2, act, wd, wu, h1, g1)


def _row_split(t):
    tile = min(t, 1024)
    return tile, t // tile, t % tile


def _row_split_specs(t, cols, col_of):
    tile, whole, rest = _row_split(t)
    specs = [pl.BlockSpec((tile, cols), lambda *g: (jnp.minimum(g[-1], whole - 1), col_of(*g[:-1])))]
    if rest:
        specs.append(pl.BlockSpec((rest, cols), lambda *g: (whole * tile // rest, col_of(*g[:-1]))))
    return specs


def _weight_grad(x, y, name):
    t, k = x.shape
    n = y.shape[1]
    tn = FF_CHUNK
    tk = FF_CHUNK if k % FF_CHUNK == 0 else k
    _, whole, rest = _row_split(t)
    steps = whole + bool(rest)
    one_tile = k == tk and n == tn

    def body(*refs):
        o_ref, ob_ref, r = refs[-2], refs[-1], pl.program_id(2)
        if one_tile:
            o_ref, ob_ref = o_ref.at[0, 0], ob_ref.at[0, 0]

        @pl.when(r == 0)
        def _():
            o_ref[...] = jnp.zeros_like(o_ref)

        def add(x_ref, y_ref):
            o_ref[...] += lax.dot_general(x_ref[...], y_ref[...], (((0,), (0,)), ((), ())),
                                          preferred_element_type=F32)

        if rest:
            pl.when(r < whole)(lambda: add(refs[0], refs[2]))
            pl.when(r == whole)(lambda: add(refs[1], refs[3]))
        else:
            add(refs[0], refs[1])

        @pl.when(r == steps - 1)
        def _():
            ob_ref[...] = o_ref[...].astype(BF16)

    tile = pl.BlockSpec((None, None, tk, tn), lambda a, b, r: (a, b, 0, 0))
    if one_tile:
        tile = pl.BlockSpec(memory_space=pltpu.VMEM)
    return pl.pallas_call(
        body, name=name, grid=(k // tk, n // tn, steps),
        in_specs=_row_split_specs(t, tk, lambda a, b: a) + _row_split_specs(t, tn, lambda a, b: b),
        out_specs=[tile, tile],
        out_shape=[jax.ShapeDtypeStruct((k // tk, n // tn, tk, tn), F32),
                   jax.ShapeDtypeStruct((k // tk, n // tn, tk, tn), BF16)],
        compiler_params=_params("parallel", "parallel", "arbitrary"),
    )(*([x] * (1 + bool(rest))), *([y] * (1 + bool(rest))))


def _mix_out_bwd(dh1, z, gp, w, o, b, c, hc, cw, ga, gc, tm, deps=()):
    t = dh1.shape[0]
    nt = t // tm
    per8 = tm // 8

    def body(dh1_ref, z_ref, gp_ref, w_ref, o_ref, b_ref, c_ref, hc_ref, cp_ref, hp_ref, cw_ref, ga_ref, gc_ref,
             dz_ref, do_ref, dbch_ref, dgp_ref, dga_ref, dgc_ref, dcw_ref, halo):
        i = pl.program_id(0)

        @pl.when(i == 0)
        def _():
            halo[...] = jnp.zeros_like(halo)
            dgp_ref[...] = jnp.zeros_like(dgp_ref)
            dga_ref[...] = jnp.zeros_like(dga_ref)
            dgc_ref[...] = jnp.zeros_like(dgc_ref)
            dcw_ref[...] = jnp.zeros_like(dcw_ref)

        dz, dgp = _rms_bwd(dh1_ref[...], z_ref[...], gp_ref[...])
        dgp_ref[...] += dgp
        dz = dz.astype(BF16)
        dz_ref[...] = dz
        dy = lax.dot_general(dz, w_ref[...].reshape(D_MODEL, D_MODEL), (((1,), (1,)), ((), ())),
                             preferred_element_type=F32)
        do, dga = _rms_bwd(dy[:, :ATTN_W], o_ref[...], ga_ref[...])
        do_ref[...] = do
        dga_ref[...] += dga

        u = c_ref[...] * hc_ref[...]
        first = i == nt - 1
        u_before = jnp.where(first, 0.0, cp_ref[...] * hp_ref[...])
        u1 = _shift_rows(u, u_before, 1)
        u2 = _shift_rows(u, u_before, 2)
        cv = cw_ref[0:1, :] * u2 + cw_ref[1:2, :] * u1 + cw_ref[2:3, :] * u
        bb = b_ref[...]
        dyc, dgc = _rms_bwd(dy[:, ATTN_W:], bb * cv, gc_ref[...])
        dgc_ref[...] += dgc
        dcv = dyc * bb
        d1 = _advance_rows(dcv, halo[...], 1)
        d2 = _advance_rows(dcv, halo[...], 2)
        halo[...] = dcv[:8]
        du = cw_ref[2:3, :] * dcv + cw_ref[1:2, :] * d1 + cw_ref[0:1, :] * d2
        dbch_ref[...] = jnp.concatenate([dyc * cv, du * hc_ref[...], du * c_ref[...]], axis=1).astype(BF16)
        dcw_ref[...] += jnp.concatenate([jnp.sum(dcv * u2, axis=0, keepdims=True),
                                         jnp.sum(dcv * u1, axis=0, keepdims=True),
                                         jnp.sum(dcv * u, axis=0, keepdims=True)], axis=0)

    row = lambda n: pl.BlockSpec((tm, n), lambda i: (nt - 1 - i, 0))
    before = pl.BlockSpec((8, CONV_W), lambda i: (jnp.maximum((nt - 1 - i) * per8 - 1, 0), 0))
    full = lambda a: pl.BlockSpec(a.shape, lambda i: (0,) * a.ndim)
    vec = lambda n: pl.BlockSpec((1, n), lambda i: (0, 0))
    body, dep_specs = _behind(body, deps)
    return pl.pallas_call(
        body, name="mix_out_bwd", grid=(nt,),
        in_specs=dep_specs + [row(D_MODEL), row(D_MODEL), full(gp), full(w), row(ATTN_W), row(CONV_W), row(CONV_W),
                              row(CONV_W), before, before, full(cw), full(ga), full(gc)],
        out_specs=[row(D_MODEL), row(ATTN_W), row(3 * CONV_W), vec(D_MODEL), vec(ATTN_W), vec(CONV_W),
                   pl.BlockSpec((CONV_K, CONV_W), lambda i: (0, 0))],
        out_shape=[jax.ShapeDtypeStruct((t, D_MODEL), BF16), jax.ShapeDtypeStruct((t, ATTN_W), F32),
                   jax.ShapeDtypeStruct((t, 3 * CONV_W), BF16), jax.ShapeDtypeStruct((1, D_MODEL), F32),
                   jax.ShapeDtypeStruct((1, ATTN_W), F32), jax.ShapeDtypeStruct((1, CONV_W), F32),
                   jax.ShapeDtypeStruct((CONV_K, CONV_W), F32)],
        scratch_shapes=[pltpu.VMEM((8, CONV_W), F32)],
        compiler_params=_params("arbitrary"),
    )(*deps, dh1, z, gp, w, o, b, c, hc, c, hc, cw, ga, gc)


def _attn_bwd(q, k, v, o, do, bias, sinks, tm, deps=()):
    t = q.shape[0]
    per_step = tm // BLOCK

    def body(s_ref, q_ref, k_ref, v_ref, o_ref, do_ref, bias_ref, dq_ref, dk_ref, dv_ref, ds_ref):
        step = pl.program_id(0)

        @pl.when(step == 0)
        def _():
            ds_ref[...] = jnp.zeros_like(ds_ref)

        heads = range(N_Q_HEADS)

        def first_matmuls(b):
            i = step * per_step + b
            rows = slice(b * BLOCK, (b + 1) * BLOCK)
            kc, vc = _two_blocks(k_ref, i), _two_blocks(v_ref, i)
            bias_i = bias_ref[jnp.minimum(i, 2)]
            kgs = [kc[:, _head(g)] for g in range(N_KV_HEADS)]
            vgs = [vc[:, _head(g)] for g in range(N_KV_HEADS)]
            qs = [q_ref[rows, _head(hh)] for hh in heads]
            dos = [do_ref[rows, _head(hh)] for hh in heads]
            dosb = [d.astype(BF16) for d in dos]
            scores = [_attn_scores(qs[hh], kgs[hh // GROUP], bias_i) for hh in heads]
            dps = [lax.dot_general(dosb[hh], vgs[hh // GROUP], (((1,), (1,)), ((), ())), preferred_element_type=F32)
                   for hh in heads]
            return kgs, qs, dos, dosb, scores, dps

        dsink = [jnp.zeros((BLOCK, 1), F32) for _ in range(N_Q_HEADS)]
        ahead = None
        for b in range(per_step):
            i = step * per_step + b
            rows = slice(b * BLOCK, (b + 1) * BLOCK)
            kgs, qs, dos, dosb, scores, dps = first_matmuls(b)
            ps, dss = [], []
            for hh in heads:
                p, share = _attn_probs(scores[hh], s_ref[hh])
                drow = jnp.sum(dos[hh] * o_ref[rows, _head(hh)], axis=-1, keepdims=True)
                dss.append((p * (dps[hh] - drow)).astype(BF16))
                ps.append(p.astype(BF16))
                dsink[hh] = dsink[hh] + share * drow
            for hh in heads:
                dq_ref[rows, _head(hh)] = jnp.dot(dss[hh], kgs[hh // GROUP], preferred_element_type=F32) * SCALE
            groups = [slice(GROUP * g, GROUP * (g + 1)) for g in range(N_KV_HEADS)]
            dkg = [lax.dot_general(jnp.concatenate(dss[gr], axis=0), jnp.concatenate(qs[gr], axis=0),
                                   (((0,), (0,)), ((), ())), preferred_element_type=F32) for gr in groups]
            dvg = [lax.dot_general(jnp.concatenate(ps[gr], axis=0), jnp.concatenate(dosb[gr], axis=0),
                                   (((0,), (0,)), ((), ())), preferred_element_type=F32) for gr in groups]
            dkb, dvb = jnp.concatenate(dkg, axis=1), jnp.concatenate(dvg, axis=1)
            if b == 0:
                @pl.when(step > 0)
                def _():
                    before = pl.ds(pl.multiple_of((i - 1) * BLOCK, BLOCK), BLOCK)
                    dk_ref[before, :] += dkb[:BLOCK]
                    dv_ref[before, :] += dvb[:BLOCK]
            else:
                at = pl.ds(pl.multiple_of((i - 1) * BLOCK, BLOCK), BLOCK)
                dk_ref[at, :] = ahead[0] + dkb[:BLOCK]
                dv_ref[at, :] = ahead[1] + dvb[:BLOCK]
            ahead = (dkb[BLOCK:], dvb[BLOCK:])
        last = pl.ds(pl.multiple_of(((step + 1) * per_step - 1) * BLOCK, BLOCK), BLOCK)
        dk_ref[last, :] = ahead[0]
        dv_ref[last, :] = ahead[1]
        for hh in range(N_Q_HEADS):
            ds_ref[hh:hh + 1, :] -= jnp.sum(dsink[hh])

    whole = pl.BlockSpec((t, KV_W), lambda i: (0, 0))
    blk = pl.BlockSpec((tm, ATTN_W), lambda i: (i, 0))
    body, dep_specs = _behind(body, deps)
    return pl.pallas_call(
        body, name="attn_bwd", grid=(t // tm,),
        in_specs=dep_specs + [pl.BlockSpec(memory_space=pltpu.SMEM), blk, whole, whole, blk, blk,
                              pl.BlockSpec(bias.shape, lambda i: (0, 0, 0))],
        out_specs=[blk, whole, whole, pl.BlockSpec((N_Q_HEADS, 128), lambda i: (0, 0))],
        out_shape=[jax.ShapeDtypeStruct((t, ATTN_W), F32), jax.ShapeDtypeStruct((t, KV_W), F32),
                   jax.ShapeDtypeStruct((t, KV_W), F32), jax.ShapeDtypeStruct((N_Q_HEADS, 128), F32)],
        compiler_params=_params("arbitrary"),
    )(*deps, sinks, q, k, v, o, do, bias)


def _in_proj_bwd(dq, dk, dv, dbch, w, dh1, h, g, tabs, tm):
    t = h.shape[0]

    def body(dq_ref, dk_ref, dv_ref, dbch_ref, w_ref, dh1_ref, h_ref, g_ref, c_ref, sa_ref, sb_ref, dh_ref, dp_ref,
             dg_ref):
        @pl.when(pl.program_id(0) == 0)
        def _():
            dg_ref[...] = jnp.zeros_like(dg_ref)

        cos, sa, sb = c_ref[...], sa_ref[...], sb_ref[...]
        rep = ATTN_W // (2 * HEAD_DIM)
        dqr = _rope_bwd(dq_ref[...], jnp.tile(cos, (1, rep)), jnp.tile(sa, (1, rep)), jnp.tile(sb, (1, rep)))
        dkr = _rope_bwd(dk_ref[...], cos, sa, sb)
        dp = jnp.concatenate([dqr.astype(BF16), dkr.astype(BF16), dv_ref[...].astype(BF16), dbch_ref[...]], axis=1)
        dp_ref[...] = dp
        da = jnp.dot(dp, w_ref[...], preferred_element_type=F32)
        dx, dg = _rms_bwd(da, h_ref[...], g_ref[...])
        dh_ref[...] = dh1_ref[...] + dx
        dg_ref[...] += dg

    row = lambda n: pl.BlockSpec((tm, n), lambda i: (i, 0))
    full = lambda a: pl.BlockSpec(a.shape, lambda i: (0, 0))
    return pl.pallas_call(
        body, name="in_proj_bwd", grid=(t // tm,),
        in_specs=[row(ATTN_W), row(KV_W), row(KV_W), row(3 * CONV_W), full(w), row(D_MODEL), row(D_MODEL), full(g),
                  row(2 * HEAD_DIM), row(2 * HEAD_DIM), row(2 * HEAD_DIM)],
        out_specs=[row(D_MODEL), row(IN_W), pl.BlockSpec((1, D_MODEL), lambda i: (0, 0))],
        out_shape=[jax.ShapeDtypeStruct((t, D_MODEL), F32), jax.ShapeDtypeStruct((t, IN_W), BF16),
                   jax.ShapeDtypeStruct((1, D_MODEL), F32)],
        compiler_params=_params("arbitrary"),
    )(dq, dk, dv, dbch, w, dh1, h, g, *tabs)


class _Tiles:
    def __init__(self, t):
        self.tm = _row_tile(t, 640)
        self.ts = self.tm
        self.tabs = _rope_tables(t)
        self.bias = _attn_bias()


def _mixer_fwd(h, p, tl):
    a, q, k, v, b, c, hc = _in_proj(h, p["mix_pre_g"], p["w_in"], tl.tabs, tl.ts)
    o = _attn_fwd(q, k, v, tl.bias, p["sinks"], tl.tm)
    return (h, a, q, k, v, b, c, hc, o)


def _out_fwd(mixed, p, tl, deps=()):
    h, a, q, k, v, b, c, hc, o = mixed
    h1, y, z = _mix_out(h, o, b, c, hc, p["conv_w"], p["attn_out_g"], p["conv_out_g"], p["w_out"], p["mix_post_g"],
                        tl.ts, deps)
    return h1, mixed + (h1, y, z)


def _mlp_fwd(h1, saved, p, tl):
    h2, a2, act, f = _mlp(h1, p["mlp_pre_g"], p["w_up"], p["w_down"], p["mlp_post_g"], tl.tm)
    return h2, saved + (a2, act, f)


def _mlp_part_bwd(dh, saved, p, tl, deps=()):
    h1, a2, act, f = saved[9], saved[12], saved[13], saved[14]
    dh1, df, dup, dg2, dg1 = _mlp_bwd(dh, f, p["mlp_post_g"], act, p["w_down"], p["w_up"], h1, p["mlp_pre_g"], tl.tm,
                                      deps)
    g = {"w_down": [d.reshape(N_CHIPS, FF_CHUNK, D_MODEL) for d in _weight_grad(act, df, "grad_w_down")],
         "w_up": [d.reshape(N_CHIPS, D_MODEL, FF_CHUNK) for d in _weight_grad(a2, dup, "grad_w_up")],
         "mlp_post_g": dg2, "mlp_pre_g": dg1}
    return dh1, g


def _mix_out_part_bwd(dh1, saved, p, tl, deps=()):
    b, c, hc, o, y, z = saved[5], saved[6], saved[7], saved[8], saved[10], saved[11]
    dz, do, dbch, dgp, dga, dgc, dcw = _mix_out_bwd(dh1, z, p["mix_post_g"], p["w_out"], o, b, c, hc, p["conv_w"],
                                                    p["attn_out_g"], p["conv_out_g"], tl.ts, deps)
    g = {"w_out": [d.reshape(N_CHIPS, D_MODEL // N_CHIPS, D_MODEL) for d in _weight_grad(y, dz, "grad_w_out")],
         "mix_post_g": dgp, "attn_out_g": dga, "conv_out_g": dgc, "conv_w": dcw}
    return (dh1, do, dbch), g


def _attn_in_part_bwd(carry, saved, p, tl, deps=()):
    dh1, do, dbch = carry
    h_in, a, q, k, v, o = saved[0], saved[1], saved[2], saved[3], saved[4], saved[8]
    dq, dk, dv, dsink = _attn_bwd(q, k, v, o, do, tl.bias, p["sinks"], tl.tm, deps)
    dh, dproj, dgi = _in_proj_bwd(dq, dk, dv, dbch, p["w_in"], dh1, h_in, p["mix_pre_g"], tl.tabs, tl.ts)
    g_in = [d.reshape(N_CHIPS, IN_W // N_CHIPS, D_MODEL) for d in _weight_grad(dproj, a, "grad_w_in")]
    return dh, {"w_in": g_in, "mix_pre_g": dgi, "sinks": dsink[:, 0]}


def _place():
    return lax.axis_index("x"), lax.axis_index("y"), lax.axis_index("c")


def _other_chips(x, y):
    return [(1 - x, y), (x, 1 - y), (1 - x, 1 - y)]


_HBM = pl.BlockSpec(memory_space=pltpu.HBM)
_SEM = pl.BlockSpec(memory_space=pltpu.SEMAPHORE)
_EFFECT = pltpu.SideEffectType.DATAFLOW_SIDE_EFFECTING


class _Exchange:
    def __init__(self, name, bufs, plan, n, after=()):
        self.name, self.plan, nb = name, plan, len(bufs)
        n_in = nb + len(after)

        def body(*refs):
            send, recv, token = refs[n_in], refs[n_in + 1], refs[-1]
            for k, (src, dst, target, _) in enumerate(plan(refs[:nb])):
                pltpu.make_async_remote_copy(src_ref=src, dst_ref=dst, send_sem=send.at[k], recv_sem=recv.at[k],
                                             device_id=target, device_id_type=MESH).start()
            token[...] = jnp.zeros_like(token)

        outs = pl.pallas_call(
            body, name=name + "_start",
            out_shape=(pltpu.SemaphoreType.DMA((n,)), pltpu.SemaphoreType.DMA((n,)),
                       *[pltpu.HBM(b.shape, b.dtype) for b in bufs], jax.ShapeDtypeStruct((8, 128), F32)),
            in_specs=[_HBM] * nb + [pl.BlockSpec(memory_space=pl.ANY)] * len(after),
            out_specs=(_SEM, _SEM, *[_HBM] * nb, pl.BlockSpec(memory_space=pltpu.VMEM)),
            input_output_aliases={i: 2 + i for i in range(nb)},
            compiler_params=pltpu.CompilerParams(has_side_effects=_EFFECT),
        )(*[pltpu.with_memory_space_constraint(b, pltpu.HBM) for b in bufs], *after)
        self.send, self.recv, self.bufs, self.token = outs[0], outs[1], list(outs[2:2 + nb]), outs[-1]

    def wait(self, *after):
        plan, nb = self.plan, len(self.bufs)

        def body(*refs):
            send, recv = refs[nb], refs[nb + 1]
            for k, (src, _, target, land) in enumerate(plan(refs[:nb])):
                cp = pltpu.make_async_remote_copy(src_ref=src, dst_ref=land, send_sem=send.at[k], recv_sem=recv.at[k],
                                                  device_id=target, device_id_type=MESH)
                cp.wait_send()
                cp.wait_recv()

        outs = pl.pallas_call(
            body, name=self.name + "_wait", out_shape=[pltpu.HBM(b.shape, b.dtype) for b in self.bufs],
            in_specs=[_HBM] * nb + [_SEM, _SEM] + [pl.BlockSpec(memory_space=pl.ANY)] * len(after),
            out_specs=[_HBM] * nb, input_output_aliases={i: i for i in range(nb)},
            compiler_params=pltpu.CompilerParams(has_side_effects=_EFFECT),
        )(*self.bufs, self.send, self.recv, *after)
        return list(outs)


def _gather_plan(n):
    def plan(refs):
        x, y, c = _place()
        me = 2 * x + y
        return [(refs[a].at[me], refs[a].at[me], (px, py, c), refs[a].at[2 * px + py])
                for a in range(n) for px, py in _other_chips(x, y)]

    return plan


def _peers():
    x, y, c = _place()
    return [(k - 1, (x ^ (k >> 2), y ^ ((k >> 1) & 1), c ^ (k & 1))) for k in range(1, N_DEV)]


def _scatter_plan(n, half_rows):
    def plan(refs):
        out = []
        for a in range(n):
            hr = half_rows[a]
            for k, (px, py, pc) in _peers():
                out.append((refs[a].at[2 * px + py, pl.ds(pc * hr, hr)], refs[n + a].at[k], (px, py, pc),
                            refs[n + a].at[k]))
        return out

    return plan


def _join_plan(n):
    def plan(refs):
        x, y, c = _place()
        return [(refs[a].at[c], refs[a].at[c], (x, y, 1 - c), refs[a].at[1 - c]) for a in range(n)]

    return plan


def _sum_parts(g, q):
    rows, cols = g.shape[1], g.shape[2]
    hr = rows // 2
    tr = _block_rows(hr)
    per = hr // tr
    x, y, c = _place()
    where = jnp.stack([2 * x + y, c]).astype(jnp.int32)

    def body(where_ref, g_ref, q_ref, o_ref):
        total = g_ref[...]
        for k in range(N_DEV - 1):
            total = total + q_ref[k].astype(F32)
        o_ref[...] = total

    return pl.pallas_call(
        body, name="sum_parts",
        grid_spec=pltpu.PrefetchScalarGridSpec(
            num_scalar_prefetch=1, grid=(per,),
            in_specs=[pl.BlockSpec((None, tr, cols), lambda i, where_ref: (where_ref[0], where_ref[1] * per + i, 0)),
                      pl.BlockSpec((N_DEV - 1, tr, cols), lambda i, where_ref: (0, i, 0))],
            out_specs=pl.BlockSpec((None, tr, cols), lambda i, where_ref: (where_ref[1], i, 0))),
        out_shape=jax.ShapeDtypeStruct((2, hr, cols), F32),
        compiler_params=_params("parallel"),
    )(where, g, q)


def _sum_devices(packed):
    def body(p_ref, o_ref, land, send_sems, recv_sems):
        x, y, c = _place()
        me = 4 * x + 2 * y + c
        land[me] = p_ref[...]
        sends = []
        for k in range(1, N_DEV):
            px, py, pc = x ^ (k >> 2), y ^ ((k >> 1) & 1), c ^ (k & 1)
            cp = pltpu.make_async_remote_copy(src_ref=p_ref, dst_ref=land.at[me], send_sem=send_sems.at[k - 1],
                                              recv_sem=recv_sems.at[k - 1], device_id=(px, py, pc), device_id_type=MESH)
            cp.start()
            sends.append(cp)
        for k in range(1, N_DEV):
            px, py, pc = x ^ (k >> 2), y ^ ((k >> 1) & 1), c ^ (k & 1)
            pltpu.make_async_remote_copy(src_ref=p_ref, dst_ref=land.at[4 * px + 2 * py + pc],
                                         send_sem=send_sems.at[k - 1], recv_sem=recv_sems.at[k - 1],
                                         device_id=(px, py, pc), device_id_type=MESH).wait_recv()
        for cp in sends:
            cp.wait_send()
        total = land[0]
        for d in range(1, N_DEV):
            total = total + land[d]
        o_ref[...] = total

    vm = pl.BlockSpec(memory_space=pltpu.VMEM)
    return pl.pallas_call(
        body, name="sum_devices", in_specs=[vm], out_specs=vm,
        out_shape=jax.ShapeDtypeStruct(packed.shape, F32),
        scratch_shapes=[pltpu.VMEM((N_DEV,) + packed.shape, F32), pltpu.SemaphoreType.DMA((N_DEV - 1,)),
                        pltpu.SemaphoreType.DMA((N_DEV - 1,))],
    )(packed)


def _adamw_math(w, g, m, v):
    m = ADAM_B1 * m + (1.0 - ADAM_B1) * g
    v = ADAM_B2 * v + (1.0 - ADAM_B2) * jnp.square(g)
    m_hat = m / (1.0 - ADAM_B1 ** ADAM_STEP)
    v_hat = v / (1.0 - ADAM_B2 ** ADAM_STEP)
    delta = -ADAM_LR * (m_hat / (jnp.sqrt(v_hat) + ADAM_EPS) + ADAM_WD * w)
    return delta, m, v


def _adamw_large(layer, w, halves, m, v, other):
    _, rows, cols = w.shape
    tr = _block_rows(rows // 2)
    per = rows // 2 // tr

    def body(w_ref, g_ref, m_ref, v_ref, *rest):
        g_out, d_ref, nm_ref, nv_ref = rest[-4:]
        g = g_ref[...]
        g_out[...] = g
        d_ref[...], nm_ref[...], nv_ref[...] = _adamw_math(w_ref[...], g, m_ref[...], v_ref[...])

    blk = pl.BlockSpec((None, tr, cols), lambda i: (layer, i, 0))
    half = pl.BlockSpec((None, tr, cols), lambda i: (i // per, i % per, 0))
    kept = [] if other is None else list(other)
    return pl.pallas_call(
        body, name="adamw_large", grid=(rows // tr,),
        in_specs=[blk, half, blk, blk] + [pl.BlockSpec(memory_space=pl.ANY)] * len(kept), out_specs=[blk] * 4,
        out_shape=[jax.ShapeDtypeStruct(w.shape, F32)] * 4,
        input_output_aliases={4 + k: k for k in range(len(kept))},
        compiler_params=_params("parallel"),
    )(w, halves, m, v, *kept)


def _adamw_small(ws, gs, ms, vs):
    n = len(ws)

    def body(*refs):
        w_r, g_r, m_r, v_r = refs[:n], refs[n:2 * n], refs[2 * n:3 * n], refs[3 * n:4 * n]
        d_r, nm_r, nv_r = refs[4 * n:5 * n], refs[5 * n:6 * n], refs[6 * n:]
        for a in range(n):
            d_r[a][...], nm_r[a][...], nv_r[a][...] = _adamw_math(w_r[a][...], g_r[a][...], m_r[a][...], v_r[a][...])

    vm = pl.BlockSpec(memory_space=pltpu.VMEM)
    outs = pl.pallas_call(
        body, name="adamw_small", in_specs=[vm] * (4 * n), out_specs=[vm] * (3 * n),
        out_shape=[jax.ShapeDtypeStruct(w.shape, F32) for w in ws] * 3,
    )(*ws, *gs, *ms, *vs)
    return outs[:n], outs[n:2 * n], outs[2 * n:]


_LARGE = ("w_in", "w_out", "w_up", "w_down")
_SMALL = ("meta_tokens", "mix_pre_g", "conv_w", "sinks", "attn_out_g", "conv_out_g", "mix_post_g", "mlp_pre_g",
          "mlp_post_g")
_ORDER = ("meta_tokens", "mix_pre_g", "w_in", "conv_w", "sinks", "attn_out_g", "conv_out_g", "w_out", "mix_post_g",
          "mlp_pre_g", "w_up", "w_down", "mlp_post_g")


class _Reduce:
    def __init__(self, name, grads, after=()):
        self.name, self.n = name, len(grads)
        self.own = [g for g, _ in grads]
        half_rows = [g.shape[1] // 2 for g in self.own]
        zones = [lax.empty((N_DEV - 1, hr, g.shape[2]), BF16) for g, hr in zip(self.own, half_rows)]
        self.exchange = _Exchange(name + "_scatter", [b for _, b in grads] + zones, _scatter_plan(self.n, half_rows),
                                  (N_DEV - 1) * self.n, after)

    @property
    def token(self):
        return self.exchange.token

    def join(self, *after):
        bufs = self.exchange.wait(*after)
        halves = [_sum_parts(g, q) for g, q in zip(self.own, bufs[self.n:])]
        self.exchange = _Exchange(self.name + "_join", halves, _join_plan(self.n), self.n)

    def done(self, *after):
        return self.exchange.wait(*after)


def _pad_cols(a, n=D_MODEL):
    return jnp.pad(a, ((0, 0), (0, n - a.shape[1])))


def kernel(x, meta_tokens, mix_pre_g, w_in, conv_w, sinks, attn_out_g, conv_out_g, w_out, mix_post_g, mlp_pre_g, w_up, w_down, mlp_post_g, loss_target, m_meta_tokens, m_mix_pre_g, m_w_in, m_conv_w, m_sinks, m_attn_out_g, m_conv_out_g, m_w_out, m_mix_post_g, m_mlp_pre_g, m_w_up, m_w_down, m_mlp_post_g, v_meta_tokens, v_mix_pre_g, v_w_in, v_conv_w, v_sinks, v_attn_out_g, v_conv_out_g, v_w_out, v_mix_post_g, v_mlp_pre_g, v_w_up, v_w_down, v_mlp_post_g):
    w = dict(meta_tokens=meta_tokens, mix_pre_g=mix_pre_g, w_in=w_in, conv_w=conv_w, sinks=sinks,
             attn_out_g=attn_out_g, conv_out_g=conv_out_g, w_out=w_out, mix_post_g=mix_post_g, mlp_pre_g=mlp_pre_g,
             w_up=w_up, w_down=w_down, mlp_post_g=mlp_post_g)
    m = dict(meta_tokens=m_meta_tokens, mix_pre_g=m_mix_pre_g, w_in=m_w_in, conv_w=m_conv_w, sinks=m_sinks,
             attn_out_g=m_attn_out_g, conv_out_g=m_conv_out_g, w_out=m_w_out, mix_post_g=m_mix_post_g,
             mlp_pre_g=m_mlp_pre_g, w_up=m_w_up, w_down=m_w_down, mlp_post_g=m_mlp_post_g)
    v = dict(meta_tokens=v_meta_tokens, mix_pre_g=v_mix_pre_g, w_in=v_w_in, conv_w=v_conv_w, sinks=v_sinks,
             attn_out_g=v_attn_out_g, conv_out_g=v_conv_out_g, w_out=v_w_out, mix_post_g=v_mix_post_g,
             mlp_pre_g=v_mlp_pre_g, w_up=v_w_up, w_down=v_w_down, mlp_post_g=v_mlp_post_g)
    chip = 2 * lax.axis_index("x") + lax.axis_index("y")
    tl = _Tiles(x.shape[1] + BLOCK)

    def zone(quarter):
        return lax.dynamic_update_slice(lax.empty((N_CHIPS,) + quarter.shape, quarter.dtype), quarter[None],
                                        (chip,) + (0,) * quarter.ndim)

    w, m, v = ({**d, "w_in": jnp.swapaxes(d["w_in"], 1, 2)} for d in (w, m, v))
    zones = {n: [zone(w[n][l].astype(BF16)) for l in range(DEPTH)] for n in _LARGE}
    first = _Exchange("gather_first", [zones["w_in"][0], zone(w["conv_w"]), zone(w["meta_tokens"])], _gather_plan(3), 9)
    out0 = _Exchange("gather_out", [zones["w_out"][0]], _gather_plan(1), 3, [first.token])
    rest = _Exchange("gather_rest", [zones[n][0] for n in ("w_up", "w_down")], _gather_plan(2), 6, [out0.token])

    def whole_in(quarters):
        return quarters.reshape(IN_W, D_MODEL)

    h = jnp.concatenate([jnp.zeros((BLOCK, D_MODEL), F32), x[0]], axis=0)
    q_in, q_conv, q_meta = first.wait(rest.token, *tl.tabs, tl.bias, h)
    conv_whole = jnp.transpose(q_conv, (1, 2, 0, 3)).reshape(DEPTH, CONV_K, CONV_W)
    meta = jnp.transpose(q_meta, (1, 0, 2)).reshape(N_META, D_MODEL)
    p = [{"conv_w": conv_whole[l], "sinks": w["sinks"][l]} for l in range(DEPTH)]
    for l in range(DEPTH):
        for n in ("mix_pre_g", "attn_out_g", "conv_out_g", "mix_post_g", "mlp_pre_g", "mlp_post_g"):
            p[l][n] = w[n][l][None, :]

    h = lax.dynamic_update_slice(h, meta, (LEAD_PAD, 0))
    p[0]["w_in"] = whole_in(q_in)
    mixed = _mixer_fwd(h, p[0], tl)
    second = _Exchange("gather_second", [zones["w_in"][1], zones["w_out"][1]], _gather_plan(2), 6, [mixed[-1]])
    second_mlp = _Exchange("gather_second_mlp", [zones["w_up"][1], zones["w_down"][1]], _gather_plan(2), 6,
                           [second.token])
    p[0]["w_out"], = out0.wait(second_mlp.token)
    h1, saved0 = _out_fwd(mixed, p[0], tl)
    p[0]["w_up"], p[0]["w_down"] = rest.wait(h1)
    h, saved0 = _mlp_fwd(h1, saved0, p[0], tl)
    q_in, p[1]["w_out"] = second.wait(h)
    p[1]["w_in"] = whole_in(q_in)
    h1, saved1 = _out_fwd(_mixer_fwd(h, p[1], tl), p[1], tl)
    p[1]["w_up"], p[1]["w_down"] = second_mlp.wait(h1)
    h, saved1 = _mlp_fwd(h1, saved1, p[1], tl)
    loss_tile, dh = _loss_head(h, loss_target[0], tl.tm)
    loss = lax.psum(loss_tile[0, 0], ("x", "y", "c"))

    def adamw(layer, halves, other):
        return {n: _adamw_large(layer, w[n], halves[n], m[n], v[n], None if other is None else other[n])
                for n in halves}

    dh1, g1 = _mlp_part_bwd(dh, saved1, p[1], tl)
    carry, gm = _mix_out_part_bwd(dh1, saved1, p[1], tl)
    dh, gi = _attn_in_part_bwd(carry, saved1, p[1], tl)
    g1.update(gm, **gi)
    red1 = _Reduce("reduce1", [g1[n] for n in _LARGE])
    dh1, g0 = _mlp_part_bwd(dh, saved0, p[0], tl, [red1.token])
    red1.join(g0["w_down"][0])
    carry, gm = _mix_out_part_bwd(dh1, saved0, p[0], tl, [red1.token])
    first0 = ("w_up", "w_down", "w_out")
    g0.update(gm)
    red0a = _Reduce("reduce0a", [g0[n] for n in first0])
    dh0, gi = _attn_in_part_bwd(carry, saved0, p[0], tl, [red0a.token])
    g0.update(gi)
    red0b = _Reduce("reduce0b", [g0["w_in"]])
    grad_x = dh0[BLOCK:][None]
    grads = {n: [g0[n], g1[n]] for n in g0 if n not in _LARGE}

    rows = [dh0[LEAD_PAD:BLOCK]]
    for n in ("mix_pre_g", "mix_post_g", "mlp_pre_g", "mlp_post_g"):
        rows += grads[n]
    rows += [jnp.concatenate([grads["attn_out_g"][l], grads["conv_out_g"][l]], axis=1) for l in range(DEPTH)]
    rows.append(jnp.concatenate(grads["conv_w"], axis=1))
    rows.append(_pad_cols(jnp.concatenate(grads["sinks"])[None, :]))
    packed = jnp.concatenate(rows, axis=0)
    packed = jnp.pad(packed, ((0, SMALL_ROWS - packed.shape[0]), (0, 0)))
    total = _sum_devices(packed)
    r0 = N_META
    small = {
        "meta_tokens": lax.dynamic_slice(total[:N_META], (0, chip * (D_MODEL // N_CHIPS)), (N_META, D_MODEL // N_CHIPS)),
        "mix_pre_g": total[r0:r0 + 2], "mix_post_g": total[r0 + 2:r0 + 4], "mlp_pre_g": total[r0 + 4:r0 + 6],
        "mlp_post_g": total[r0 + 6:r0 + 8],
        "attn_out_g": total[r0 + 8:r0 + 10, :ATTN_W], "conv_out_g": total[r0 + 8:r0 + 10, ATTN_W:],
        "conv_w": lax.dynamic_slice(total[r0 + 10:r0 + 13].reshape(CONV_K, DEPTH, CONV_W).transpose(1, 0, 2),
                                    (0, 0, chip * (CONV_W // N_CHIPS)), (DEPTH, CONV_K, CONV_W // N_CHIPS)),
        "sinks": total[r0 + 13, :DEPTH * N_Q_HEADS].reshape(DEPTH, N_Q_HEADS),
    }

    ds, nms, nvs = _adamw_small([w[n] for n in _SMALL], [small[n] for n in _SMALL], [m[n] for n in _SMALL],
                                [v[n] for n in _SMALL])
    done1 = adamw(1, dict(zip(_LARGE, red1.done(red0b.token))), None)
    red0a.join(ds[0], grad_x, *[done1[n][0] for n in _LARGE])
    red0b.join(red0a.token)
    done0 = adamw(0, dict(zip(first0, red0a.done(red0b.token))), done1)
    done0.update(adamw(0, {"w_in": red0b.done(done0["w_down"][0])[0]}, done1))
    grad, delta, new_m, new_v = {}, {}, {}, {}
    for n in _LARGE:
        grad[n], delta[n], new_m[n], new_v[n] = done0[n]
    for d in (grad, delta, new_m, new_v):
        d["w_in"] = jnp.swapaxes(d["w_in"], 1, 2)
    for i, n in enumerate(_SMALL):
        grad[n], delta[n], new_m[n], new_v[n] = small[n], ds[i], nms[i], nvs[i]
    return (loss, grad_x, *[grad[n] for n in _ORDER], *[delta[n] for n in _ORDER], *[new_m[n] for n in _ORDER],
            *[new_v[n] for n in _ORDER])
```

```python
import functools

import jax
import jax.numpy as jnp
from jax import lax
from jax.experimental import pallas as pl
from jax.experimental.pallas import tpu as pltpu

F32 = jnp.float32
BF16 = jnp.bfloat16

D_MODEL = 1024
DEPTH = 2
N_META = 16
ATTN_W = 512
CONV_W = 512
HEAD_DIM = 64
N_Q_HEADS = 8
N_KV_HEADS = 2
GROUP = N_Q_HEADS // N_KV_HEADS
KV_W = N_KV_HEADS * HEAD_DIM
CONV_K = 3
BLOCK = 128
LEAD_PAD = BLOCK - N_META
ROPE_THETA = 500000.0
ROT_DIM = HEAD_DIM // 4
ROT_HALF = ROT_DIM // 2
D_FF = 4 * D_MODEL
IN_W = ATTN_W + 2 * KV_W + 3 * CONV_W
QKV_W = ATTN_W + 2 * KV_W
EPS = 1e-6
SCALE = HEAD_DIM ** -0.5
FF_CHUNK = 1024
N_CHIPS = 4
N_DEV = 8

ADAM_LR = 0.001
ADAM_B1 = 0.9
ADAM_B2 = 0.999
ADAM_EPS = 1e-08
ADAM_WD = 0.01
ADAM_STEP = 10

V7X_VMEM_LIMIT = 56 * 1024 * 1024
SMALL_ROWS = 32

MESH = pl.DeviceIdType.MESH


def _params(*sem):
    return pltpu.CompilerParams(dimension_semantics=sem, vmem_limit_bytes=V7X_VMEM_LIMIT)


def _block_rows(n):
    return max(r for r in range(16, min(n, 256) + 1, 16) if n % r == 0)


def _row_tile(t, most):
    nb = t // BLOCK
    for b in range(most // BLOCK, 0, -1):
        if nb % b == 0:
            return b * BLOCK
    return BLOCK


def _behind(body, deps):
    n = len(deps)

    def wrapped(*refs):
        body(*refs[n:])

    return wrapped, [pl.BlockSpec(memory_space=pl.ANY)] * n


def _rms(x, g):
    r = lax.rsqrt(jnp.mean(x * x, axis=-1, keepdims=True) + EPS)
    return x * r * g


def _rms_bwd(dy, x, g):
    r = lax.rsqrt(jnp.mean(x * x, axis=-1, keepdims=True) + EPS)
    xh = x * r
    dg = jnp.sum(dy * xh, axis=0, keepdims=True)
    dxh = dy * g
    dx = r * (dxh - xh * jnp.mean(dxh * xh, axis=-1, keepdims=True))
    return dx, dg


def _rope(x, cos, sa, sb):
    n = x.shape[-1]
    return x * cos + pltpu.roll(x, n - ROT_HALF, 1) * sa + pltpu.roll(x, ROT_HALF, 1) * sb


def _rope_bwd(dy, cos, sa, sb):
    n = dy.shape[-1]
    return dy * cos + pltpu.roll(dy * sa, ROT_HALF, 1) + pltpu.roll(dy * sb, n - ROT_HALF, 1)


def _rope_tables(t):
    pos = lax.broadcasted_iota(jnp.int32, (t, 2 * HEAD_DIM), 0).astype(F32) - LEAD_PAD
    dim = lax.broadcasted_iota(jnp.int32, (t, 2 * HEAD_DIM), 1) % HEAD_DIM
    pair = (dim % ROT_HALF).astype(F32)
    inv_freq = jnp.power(jnp.float32(ROPE_THETA), -(2.0 * pair) / ROT_DIM)
    ang = pos * inv_freq
    cos, sin = jnp.cos(ang), jnp.sin(ang)
    return (jnp.where(dim < ROT_DIM, cos, 1.0), jnp.where(dim < ROT_HALF, -sin, 0.0),
            jnp.where((dim >= ROT_HALF) & (dim < ROT_DIM), sin, 0.0))


def _in_proj(h, g, w, tabs, tm):
    t = h.shape[0]

    def body(h_ref, g_ref, w_ref, c_ref, sa_ref, sb_ref, a_ref, q_ref, k_ref, v_ref, b_ref, cg_ref, hc_ref):
        a = _rms(h_ref[...], g_ref[...]).astype(BF16)
        a_ref[...] = a
        p = lax.dot_general(a, w_ref[...], (((1,), (1,)), ((), ())), preferred_element_type=F32)
        cos, sa, sb = c_ref[...], sa_ref[...], sb_ref[...]
        rep = ATTN_W // (2 * HEAD_DIM)
        q = _rope(p[:, :ATTN_W], jnp.tile(cos, (1, rep)), jnp.tile(sa, (1, rep)), jnp.tile(sb, (1, rep)))
        q_ref[...] = (q * SCALE).astype(BF16)
        k_ref[...] = _rope(p[:, ATTN_W:ATTN_W + KV_W], cos, sa, sb).astype(BF16)
        v_ref[...] = p[:, ATTN_W + KV_W:QKV_W].astype(BF16)
        b_ref[...] = p[:, QKV_W:QKV_W + CONV_W]
        cg_ref[...] = p[:, QKV_W + CONV_W:QKV_W + 2 * CONV_W]
        hc_ref[...] = p[:, QKV_W + 2 * CONV_W:]

    row = lambda n: pl.BlockSpec((tm, n), lambda i: (i, 0))
    full = lambda a: pl.BlockSpec(a.shape, lambda i: (0, 0))
    return pl.pallas_call(
        body, name="in_proj", grid=(t // tm,),
        in_specs=[row(D_MODEL), full(g), full(w), row(2 * HEAD_DIM), row(2 * HEAD_DIM), row(2 * HEAD_DIM)],
        out_specs=[row(D_MODEL), row(ATTN_W), row(KV_W), row(KV_W), row(CONV_W), row(CONV_W), row(CONV_W)],
        out_shape=[jax.ShapeDtypeStruct((t, D_MODEL), BF16), jax.ShapeDtypeStruct((t, ATTN_W), BF16),
                   jax.ShapeDtypeStruct((t, KV_W), BF16), jax.ShapeDtypeStruct((t, KV_W), BF16),
                   jax.ShapeDtypeStruct((t, CONV_W), F32), jax.ShapeDtypeStruct((t, CONV_W), F32),
                   jax.ShapeDtypeStruct((t, CONV_W), F32)],
        compiler_params=_params("parallel"),
    )(h, g, w, *tabs)


def _attn_bias():
    r = lax.broadcasted_iota(jnp.int32, (3, BLOCK, 2 * BLOCK), 1)
    c = lax.broadcasted_iota(jnp.int32, (3, BLOCK, 2 * BLOCK), 2)
    i = lax.broadcasted_iota(jnp.int32, (3, BLOCK, 2 * BLOCK), 0)
    ok = (c > r) & (c <= r + BLOCK) & (c + (i - 1) * BLOCK >= LEAD_PAD)
    return jnp.where(ok, 0.0, -jnp.inf).astype(F32)


def _attn_scores(qh, kg, bias):
    return lax.dot_general(qh, kg, (((1,), (1,)), ((), ())), preferred_element_type=F32) + bias


def _attn_probs(s, sk):
    m = jnp.maximum(jnp.max(s, axis=-1, keepdims=True), sk)
    e = jnp.exp(s - m)
    es = jnp.exp(sk - m)
    rden = 1.0 / (jnp.sum(e, axis=-1, keepdims=True) + es)
    return e * rden, es * rden


def _head(hh):
    return slice(hh * HEAD_DIM, (hh + 1) * HEAD_DIM)


def _two_blocks(ref, i):
    prev = jnp.maximum(i - 1, 0)
    return jnp.concatenate([ref[pl.ds(pl.multiple_of(prev * BLOCK, BLOCK), BLOCK), :],
                            ref[pl.ds(pl.multiple_of(i * BLOCK, BLOCK), BLOCK), :]], axis=0)


def _attn_fwd(q, k, v, bias, sinks, tm):
    t = q.shape[0]
    per_step = tm // BLOCK
    heads = range(N_Q_HEADS)

    def body(s_ref, q_ref, k_ref, v_ref, bias_ref, o_ref):
        for b in range(per_step):
            i = pl.program_id(0) * per_step + b
            rows = slice(b * BLOCK, (b + 1) * BLOCK)
            kc, vc = _two_blocks(k_ref, i), _two_blocks(v_ref, i)
            bias_i = bias_ref[jnp.minimum(i, 2)]
            scores = [_attn_scores(q_ref[rows, _head(hh)], kc[:, _head(hh // GROUP)], bias_i) for hh in heads]
            probs = [_attn_probs(scores[hh], s_ref[hh])[0].astype(BF16) for hh in heads]
            for hh in heads:
                o_ref[rows, _head(hh)] = jnp.dot(probs[hh], vc[:, _head(hh // GROUP)], preferred_element_type=F32)

    whole = pl.BlockSpec((t, KV_W), lambda i: (0, 0))
    return pl.pallas_call(
        body, name="attn_fwd", grid=(t // tm,),
        in_specs=[pl.BlockSpec(memory_space=pltpu.SMEM), pl.BlockSpec((tm, ATTN_W), lambda i: (i, 0)), whole, whole,
                  pl.BlockSpec(bias.shape, lambda i: (0, 0, 0))],
        out_specs=pl.BlockSpec((tm, ATTN_W), lambda i: (i, 0)),
        out_shape=jax.ShapeDtypeStruct((t, ATTN_W), F32),
        compiler_params=_params("parallel"),
    )(sinks, q, k, v, bias)


def _shift_rows(u, halo, n):
    r = pltpu.roll(u, n, 0)
    hr = pltpu.roll(halo, n, 0)
    idx = lax.broadcasted_iota(jnp.int32, hr.shape, 0)
    return jnp.concatenate([jnp.where(idx < n, hr, r[:8]), r[8:]], axis=0)


def _advance_rows(u, halo, n):
    rows = u.shape[0]
    r = pltpu.roll(u, rows - n, 0)
    hr = pltpu.roll(halo, 8 - n, 0)
    idx = lax.broadcasted_iota(jnp.int32, hr.shape, 0)
    return jnp.concatenate([r[:rows - 8], jnp.where(idx >= 8 - n, hr, r[rows - 8:])], axis=0)


def _mix_out(h, o, b, c, hc, cw, ga, gc, w, gp, tm, deps=()):
    t = h.shape[0]

    def body(h_ref, o_ref, b_ref, c_ref, hc_ref, cw_ref, ga_ref, gc_ref, w_ref, gp_ref, h1_ref, y_ref, z_ref, halo):
        @pl.when(pl.program_id(0) == 0)
        def _():
            halo[...] = jnp.zeros_like(halo)

        u = c_ref[...] * hc_ref[...]
        cv = cw_ref[0:1, :] * _shift_rows(u, halo[...], 2) + cw_ref[1:2, :] * _shift_rows(u, halo[...], 1) \
            + cw_ref[2:3, :] * u
        halo[...] = u[tm - 8:]
        yc = b_ref[...] * cv
        y = jnp.concatenate([_rms(o_ref[...], ga_ref[...]), _rms(yc, gc_ref[...])], axis=1).astype(BF16)
        y_ref[...] = y
        z = jnp.dot(y, w_ref[...].reshape(D_MODEL, D_MODEL), preferred_element_type=F32)
        z_ref[...] = z
        h1_ref[...] = h_ref[...] + _rms(z, gp_ref[...])

    row = lambda n: pl.BlockSpec((tm, n), lambda i: (i, 0))
    full = lambda a: pl.BlockSpec(a.shape, lambda i: (0,) * a.ndim)
    body, dep_specs = _behind(body, deps)
    return pl.pallas_call(
        body, name="mix_out", grid=(t // tm,),
        in_specs=dep_specs + [row(D_MODEL), row(ATTN_W), row(CONV_W), row(CONV_W), row(CONV_W), full(cw), full(ga),
                              full(gc), full(w), full(gp)],
        out_specs=[row(D_MODEL), row(D_MODEL), row(D_MODEL)],
        out_shape=[jax.ShapeDtypeStruct((t, D_MODEL), F32), jax.ShapeDtypeStruct((t, D_MODEL), BF16),
                   jax.ShapeDtypeStruct((t, D_MODEL), F32)],
        scratch_shapes=[pltpu.VMEM((8, CONV_W), F32)],
        compiler_params=_params("arbitrary"),
    )(*deps, h, o, b, c, hc, cw, ga, gc, w, gp)


def _mlp(h1, g1, wu, wd, g2, tm):
    t = h1.shape[0]
    nj = D_FF // FF_CHUNK

    def body(h1_ref, g1_ref, wu_ref, wd_ref, g2_ref, h2_ref, a2_ref, act_ref, f_ref, acc):
        j = pl.program_id(1)

        @pl.when(j == 0)
        def _():
            a2_ref[...] = _rms(h1_ref[...], g1_ref[...]).astype(BF16)

        up = jnp.dot(a2_ref[...], wu_ref[j], preferred_element_type=F32)
        act = jnp.square(jnp.maximum(up, 0.0)).astype(BF16)
        act_ref[...] = act
        part = jnp.dot(act, wd_ref[j], preferred_element_type=F32)

        @pl.when(j == 0)
        def _():
            acc[...] = part

        @pl.when(j > 0)
        def _():
            acc[...] += part

        @pl.when(j == nj - 1)
        def _():
            f = acc[...]
            f_ref[...] = f
            h2_ref[...] = h1_ref[...] + _rms(f, g2_ref[...])

    row = pl.BlockSpec((tm, D_MODEL), lambda i, j: (i, 0))
    vec = pl.BlockSpec((1, D_MODEL), lambda i, j: (0, 0))
    resident = pl.BlockSpec(memory_space=pltpu.VMEM)
    return pl.pallas_call(
        body, name="mlp", grid=(t // tm, nj),
        in_specs=[row, vec, resident, resident, vec],
        out_specs=[row, row, pl.BlockSpec((tm, FF_CHUNK), lambda i, j: (i, j)), row],
        out_shape=[jax.ShapeDtypeStruct((t, D_MODEL), F32), jax.ShapeDtypeStruct((t, D_MODEL), BF16),
                   jax.ShapeDtypeStruct((t, D_FF), BF16), jax.ShapeDtypeStruct((t, D_MODEL), F32)],
        scratch_shapes=[pltpu.VMEM((tm, D_MODEL), F32)],
        compiler_params=_params("parallel", "arbitrary"),
    )(h1, g1, wu, wd, g2)


def _loss_head(h, target, tm):
    t = h.shape[0]
    per_step = tm // BLOCK

    def body(h_ref, *rest):
        t_refs, (loss_ref, dh_ref) = rest[:per_step], rest[per_step:]
        i = pl.program_id(0)

        @pl.when(i == 0)
        def _():
            loss_ref[...] = jnp.zeros_like(loss_ref)

        total = jnp.zeros((), F32)
        for b in range(per_step):
            rows = slice(b * BLOCK, (b + 1) * BLOCK)
            err = h_ref[rows, :] - t_refs[b][...]
            if b == 0:
                err = jnp.where(i == 0, 0.0, err)
            dh_ref[rows, :] = err * (1.0 / D_MODEL)
            total = total + jnp.sum(err * err)
        loss_ref[...] += total * (0.5 / D_MODEL)

    def target_block(b):
        return pl.BlockSpec((BLOCK, D_MODEL), lambda i: (jnp.maximum(i * per_step + b - 1, 0), 0))

    return pl.pallas_call(
        body, name="loss_head", grid=(t // tm,),
        in_specs=[pl.BlockSpec((tm, D_MODEL), lambda i: (i, 0))] + [target_block(b) for b in range(per_step)],
        out_specs=[pl.BlockSpec((8, 128), lambda i: (0, 0)), pl.BlockSpec((tm, D_MODEL), lambda i: (i, 0))],
        out_shape=[jax.ShapeDtypeStruct((8, 128), F32), jax.ShapeDtypeStruct((t, D_MODEL), F32)],
        compiler_params=_params("arbitrary"),
    )(h, *([target] * per_step))


def _mlp_bwd(dh2, f, g2, act, wd, wu, h1, g1, tm, deps=()):
    t = dh2.shape[0]
    nj = D_FF // FF_CHUNK

    def body(dh2_ref, f_ref, g2_ref, act_ref, wd_ref, wu_ref, h1_ref, g1_ref, dh1_ref, df_ref, dup_ref, dg2_ref,
             dg1_ref, acc):
        i, j = pl.program_id(0), pl.program_id(1)

        @pl.when((i == 0) & (j == 0))
        def _():
            dg2_ref[...] = jnp.zeros_like(dg2_ref)
            dg1_ref[...] = jnp.zeros_like(dg1_ref)

        @pl.when(j == 0)
        def _():
            df, dg = _rms_bwd(dh2_ref[...], f_ref[...], g2_ref[...])
            df_ref[...] = df.astype(BF16)
            dg2_ref[...] += dg

        dact = lax.dot_general(df_ref[...], wd_ref[j], (((1,), (1,)), ((), ())), preferred_element_type=F32)
        dup = (dact * (2.0 * jnp.sqrt(act_ref[...].astype(F32)))).astype(BF16)
        dup_ref[...] = dup
        part = lax.dot_general(dup, wu_ref[j], (((1,), (1,)), ((), ())), preferred_element_type=F32)

        @pl.when(j == 0)
        def _():
            acc[...] = part

        @pl.when(j > 0)
        def _():
            acc[...] += part

        @pl.when(j == nj - 1)
        def _():
            dx, dg = _rms_bwd(acc[...], h1_ref[...], g1_ref[...])
            dh1_ref[...] = dh2_ref[...] + dx
            dg1_ref[...] += dg

    row = pl.BlockSpec((tm, D_MODEL), lambda i, j: (i, 0))
    vec = pl.BlockSpec((1, D_MODEL), lambda i, j: (0, 0))
    chunk = pl.BlockSpec((tm, FF_CHUNK), lambda i, j: (i, j))
    resident = pl.BlockSpec(memory_space=pltpu.VMEM)
    body, dep_specs = _behind(body, deps)
    return pl.pallas_call(
        body, name="mlp_bwd", grid=(t // tm, nj),
        in_specs=dep_specs + [row, row, vec, chunk, resident, resident, row, vec],
        out_specs=[row, row, chunk, vec, vec],
        out_shape=[jax.ShapeDtypeStruct((t, D_MODEL), F32), jax.ShapeDtypeStruct((t, D_MODEL), BF16),
                   jax.ShapeDtypeStruct((t, D_FF), BF16), jax.ShapeDtypeStruct((1, D_MODEL), F32),
                   jax.ShapeDtypeStruct((1, D_MODEL), F32)],
        scratch_shapes=[pltpu.VMEM((tm, D_MODEL), F32)],
        compiler_params=_params("arbitrary", "arbitrary"),
    )(*deps, dh2, f, g2, act, wd, wu, h1, g1)


def _row_split(t):
    tile = min(t, 1024)
    return tile, t // tile, t % tile


def _row_split_specs(t, cols, col_of):
    tile, whole, rest = _row_split(t)
    specs = [pl.BlockSpec((tile, cols), lambda *g: (jnp.minimum(g[-1], whole - 1), col_of(*g[:-1])))]
    if rest:
        specs.append(pl.BlockSpec((rest, cols), lambda *g: (whole * tile // rest, col_of(*g[:-1]))))
    return specs


def _weight_grad(x, y, name):
    t, k = x.shape
    n = y.shape[1]
    tn = FF_CHUNK
    tk = FF_CHUNK if k % FF_CHUNK == 0 else k
    _, whole, rest = _row_split(t)
    steps = whole + bool(rest)
    one_tile = k == tk and n == tn

    def body(*refs):
        o_ref, ob_ref, r = refs[-2], refs[-1], pl.program_id(2)
        if one_tile:
            o_ref, ob_ref = o_ref.at[0, 0], ob_ref.at[0, 0]

        @pl.when(r == 0)
        def _():
            o_ref[...] = jnp.zeros_like(o_ref)

        def add(x_ref, y_ref):
            o_ref[...] += lax.dot_general(x_ref[...], y_ref[...], (((0,), (0,)), ((), ())),
                                          preferred_element_type=F32)

        if rest:
            pl.when(r < whole)(lambda: add(refs[0], refs[2]))
            pl.when(r == whole)(lambda: add(refs[1], refs[3]))
        else:
            add(refs[0], refs[1])

        @pl.when(r == steps - 1)
        def _():
            ob_ref[...] = o_ref[...].astype(BF16)

    tile = pl.BlockSpec((None, None, tk, tn), lambda a, b, r: (a, b, 0, 0))
    if one_tile:
        tile = pl.BlockSpec(memory_space=pltpu.VMEM)
    return pl.pallas_call(
        body, name=name, grid=(k // tk, n // tn, steps),
        in_specs=_row_split_specs(t, tk, lambda a, b: a) + _row_split_specs(t, tn, lambda a, b: b),
        out_specs=[tile, tile],
        out_shape=[jax.ShapeDtypeStruct((k // tk, n // tn, tk, tn), F32),
                   jax.ShapeDtypeStruct((k // tk, n // tn, tk, tn), BF16)],
        compiler_params=_params("parallel", "parallel", "arbitrary"),
    )(*([x] * (1 + bool(rest))), *([y] * (1 + bool(rest))))


def _mix_out_bwd(dh1, z, gp, w, o, b, c, hc, cw, ga, gc, tm, deps=()):
    t = dh1.shape[0]
    nt = t // tm
    per8 = tm // 8

    def body(dh1_ref, z_ref, gp_ref, w_ref, o_ref, b_ref, c_ref, hc_ref, cp_ref, hp_ref, cw_ref, ga_ref, gc_ref,
             dz_ref, do_ref, dbch_ref, dgp_ref, dga_ref, dgc_ref, dcw_ref, halo):
        i = pl.program_id(0)

        @pl.when(i == 0)
        def _():
            halo[...] = jnp.zeros_like(halo)
            dgp_ref[...] = jnp.zeros_like(dgp_ref)
            dga_ref[...] = jnp.zeros_like(dga_ref)
            dgc_ref[...] = jnp.zeros_like(dgc_ref)
            dcw_ref[...] = jnp.zeros_like(dcw_ref)

        dz, dgp = _rms_bwd(dh1_ref[...], z_ref[...], gp_ref[...])
        dgp_ref[...] += dgp
        dz = dz.astype(BF16)
        dz_ref[...] = dz
        dy = lax.dot_general(dz, w_ref[...].reshape(D_MODEL, D_MODEL), (((1,), (1,)), ((), ())),
                             preferred_element_type=F32)
        do, dga = _rms_bwd(dy[:, :ATTN_W], o_ref[...], ga_ref[...])
        do_ref[...] = do
        dga_ref[...] += dga

        u = c_ref[...] * hc_ref[...]
        first = i == nt - 1
        u_before = jnp.where(first, 0.0, cp_ref[...] * hp_ref[...])
        u1 = _shift_rows(u, u_before, 1)
        u2 = _shift_rows(u, u_before, 2)
        cv = cw_ref[0:1, :] * u2 + cw_ref[1:2, :] * u1 + cw_ref[2:3, :] * u
        bb = b_ref[...]
        dyc, dgc = _rms_bwd(dy[:, ATTN_W:], bb * cv, gc_ref[...])
        dgc_ref[...] += dgc
        dcv = dyc * bb
        d1 = _advance_rows(dcv, halo[...], 1)
        d2 = _advance_rows(dcv, halo[...], 2)
        halo[...] = dcv[:8]
        du = cw_ref[2:3, :] * dcv + cw_ref[1:2, :] * d1 + cw_ref[0:1, :] * d2
        dbch_ref[...] = jnp.concatenate([dyc * cv, du * hc_ref[...], du * c_ref[...]], axis=1).astype(BF16)
        dcw_ref[...] += jnp.concatenate([jnp.sum(dcv * u2, axis=0, keepdims=True),
                                         jnp.sum(dcv * u1, axis=0, keepdims=True),
                                         jnp.sum(dcv * u, axis=0, keepdims=True)], axis=0)

    row = lambda n: pl.BlockSpec((tm, n), lambda i: (nt - 1 - i, 0))
    before = pl.BlockSpec((8, CONV_W), lambda i: (jnp.maximum((nt - 1 - i) * per8 - 1, 0), 0))
    full = lambda a: pl.BlockSpec(a.shape, lambda i: (0,) * a.ndim)
    vec = lambda n: pl.BlockSpec((1, n), lambda i: (0, 0))
    body, dep_specs = _behind(body, deps)
    return pl.pallas_call(
        body, name="mix_out_bwd", grid=(nt,),
        in_specs=dep_specs + [row(D_MODEL), row(D_MODEL), full(gp), full(w), row(ATTN_W), row(CONV_W), row(CONV_W),
                              row(CONV_W), before, before, full(cw), full(ga), full(gc)],
        out_specs=[row(D_MODEL), row(ATTN_W), row(3 * CONV_W), vec(D_MODEL), vec(ATTN_W), vec(CONV_W),
                   pl.BlockSpec((CONV_K, CONV_W), lambda i: (0, 0))],
        out_shape=[jax.ShapeDtypeStruct((t, D_MODEL), BF16), jax.ShapeDtypeStruct((t, ATTN_W), F32),
                   jax.ShapeDtypeStruct((t, 3 * CONV_W), BF16), jax.ShapeDtypeStruct((1, D_MODEL), F32),
                   jax.ShapeDtypeStruct((1, ATTN_W), F32), jax.ShapeDtypeStruct((1, CONV_W), F32),
                   jax.ShapeDtypeStruct((CONV_K, CONV_W), F32)],
        scratch_shapes=[pltpu.VMEM((8, CONV_W), F32)],
        compiler_params=_params("arbitrary"),
    )(*deps, dh1, z, gp, w, o, b, c, hc, c, hc, cw, ga, gc)


def _attn_bwd(q, k, v, o, do, bias, sinks, tm, deps=()):
    t = q.shape[0]
    per_step = tm // BLOCK

    def body(s_ref, q_ref, k_ref, v_ref, o_ref, do_ref, bias_ref, dq_ref, dk_ref, dv_ref, ds_ref):
        step = pl.program_id(0)

        @pl.when(step == 0)
        def _():
            ds_ref[...] = jnp.zeros_like(ds_ref)

        heads = range(N_Q_HEADS)

        def first_matmuls(b):
            i = step * per_step + b
            rows = slice(b * BLOCK, (b + 1) * BLOCK)
            kc, vc = _two_blocks(k_ref, i), _two_blocks(v_ref, i)
            bias_i = bias_ref[jnp.minimum(i, 2)]
            kgs = [kc[:, _head(g)] for g in range(N_KV_HEADS)]
            vgs = [vc[:, _head(g)] for g in range(N_KV_HEADS)]
            qs = [q_ref[rows, _head(hh)] for hh in heads]
            dos = [do_ref[rows, _head(hh)] for hh in heads]
            dosb = [d.astype(BF16) for d in dos]
            scores = [_attn_scores(qs[hh], kgs[hh // GROUP], bias_i) for hh in heads]
            dps = [lax.dot_general(dosb[hh], vgs[hh // GROUP], (((1,), (1,)), ((), ())), preferred_element_type=F32)
                   for hh in heads]
            return kgs, qs, dos, dosb, scores, dps

        dsink = [jnp.zeros((BLOCK, 1), F32) for _ in range(N_Q_HEADS)]
        ahead = None
        for b in range(per_step):
            i = step * per_step + b
            rows = slice(b * BLOCK, (b + 1) * BLOCK)
            kgs, qs, dos, dosb, scores, dps = first_matmuls(b)
            ps, dss = [], []
            for hh in heads:
                p, share = _attn_probs(scores[hh], s_ref[hh])
                drow = jnp.sum(dos[hh] * o_ref[rows, _head(hh)], axis=-1, keepdims=True)
                dss.append((p * (dps[hh] - drow)).astype(BF16))
                ps.append(p.astype(BF16))
                dsink[hh] = dsink[hh] + share * drow
            for hh in heads:
                dq_ref[rows, _head(hh)] = jnp.dot(dss[hh], kgs[hh // GROUP], preferred_element_type=F32) * SCALE
            groups = [slice(GROUP * g, GROUP * (g + 1)) for g in range(N_KV_HEADS)]
            dkg = [lax.dot_general(jnp.concatenate(dss[gr], axis=0), jnp.concatenate(qs[gr], axis=0),
                                   (((0,), (0,)), ((), ())), preferred_element_type=F32) for gr in groups]
            dvg = [lax.dot_general(jnp.concatenate(ps[gr], axis=0), jnp.concatenate(dosb[gr], axis=0),
                                   (((0,), (0,)), ((), ())), preferred_element_type=F32) for gr in groups]
            dkb, dvb = jnp.concatenate(dkg, axis=1), jnp.concatenate(dvg, axis=1)
            if b == 0:
                @pl.when(step > 0)
                def _():
                    before = pl.ds(pl.multiple_of((i - 1) * BLOCK, BLOCK), BLOCK)
                    dk_ref[before, :] += dkb[:BLOCK]
                    dv_ref[before, :] += dvb[:BLOCK]
            else:
                at = pl.ds(pl.multiple_of((i - 1) * BLOCK, BLOCK), BLOCK)
                dk_ref[at, :] = ahead[0] + dkb[:BLOCK]
                dv_ref[at, :] = ahead[1] + dvb[:BLOCK]
            ahead = (dkb[BLOCK:], dvb[BLOCK:])
        last = pl.ds(pl.multiple_of(((step + 1) * per_step - 1) * BLOCK, BLOCK), BLOCK)
        dk_ref[last, :] = ahead[0]
        dv_ref[last, :] = ahead[1]
        for hh in range(N_Q_HEADS):
            ds_ref[hh:hh + 1, :] -= jnp.sum(dsink[hh])

    whole = pl.BlockSpec((t, KV_W), lambda i: (0, 0))
    blk = pl.BlockSpec((tm, ATTN_W), lambda i: (i, 0))
    body, dep_specs = _behind(body, deps)
    return pl.pallas_call(
        body, name="attn_bwd", grid=(t // tm,),
        in_specs=dep_specs + [pl.BlockSpec(memory_space=pltpu.SMEM), blk, whole, whole, blk, blk,
                              pl.BlockSpec(bias.shape, lambda i: (0, 0, 0))],
        out_specs=[blk, whole, whole, pl.BlockSpec((N_Q_HEADS, 128), lambda i: (0, 0))],
        out_shape=[jax.ShapeDtypeStruct((t, ATTN_W), F32), jax.ShapeDtypeStruct((t, KV_W), F32),
                   jax.ShapeDtypeStruct((t, KV_W), F32), jax.ShapeDtypeStruct((N_Q_HEADS, 128), F32)],
        compiler_params=_params("arbitrary"),
    )(*deps, sinks, q, k, v, o, do, bias)


def _in_proj_bwd(dq, dk, dv, dbch, w, dh1, h, g, tabs, tm):
    t = h.shape[0]

    def body(dq_ref, dk_ref, dv_ref, dbch_ref, w_ref, dh1_ref, h_ref, g_ref, c_ref, sa_ref, sb_ref, dh_ref, dp_ref,
             dg_ref):
        @pl.when(pl.program_id(0) == 0)
        def _():
            dg_ref[...] = jnp.zeros_like(dg_ref)

        cos, sa, sb = c_ref[...], sa_ref[...], sb_ref[...]
        rep = ATTN_W // (2 * HEAD_DIM)
        dqr = _rope_bwd(dq_ref[...], jnp.tile(cos, (1, rep)), jnp.tile(sa, (1, rep)), jnp.tile(sb, (1, rep)))
        dkr = _rope_bwd(dk_ref[...], cos, sa, sb)
        dp = jnp.concatenate([dqr.astype(BF16), dkr.astype(BF16), dv_ref[...].astype(BF16), dbch_ref[...]], axis=1)
        dp_ref[...] = dp
        da = jnp.dot(dp, w_ref[...], preferred_element_type=F32)
        dx, dg = _rms_bwd(da, h_ref[...], g_ref[...])
        dh_ref[...] = dh1_ref[...] + dx
        dg_ref[...] += dg

    row = lambda n: pl.BlockSpec((tm, n), lambda i: (i, 0))
    full = lambda a: pl.BlockSpec(a.shape, lambda i: (0, 0))
    return pl.pallas_call(
        body, name="in_proj_bwd", grid=(t // tm,),
        in_specs=[row(ATTN_W), row(KV_W), row(KV_W), row(3 * CONV_W), full(w), row(D_MODEL), row(D_MODEL), full(g),
                  row(2 * HEAD_DIM), row(2 * HEAD_DIM), row(2 * HEAD_DIM)],
        out_specs=[row(D_MODEL), row(IN_W), pl.BlockSpec((1, D_MODEL), lambda i: (0, 0))],
        out_shape=[jax.ShapeDtypeStruct((t, D_MODEL), F32), jax.ShapeDtypeStruct((t, IN_W), BF16),
                   jax.ShapeDtypeStruct((1, D_MODEL), F32)],
        compiler_params=_params("arbitrary"),
    )(dq, dk, dv, dbch, w, dh1, h, g, *tabs)


class _Tiles:
    def __init__(self, t):
        self.tm = _row_tile(t, 640)
        self.ts = self.tm
        self.tabs = _rope_tables(t)
        self.bias = _attn_bias()


def _mixer_fwd(h, p, tl):
    a, q, k, v, b, c, hc = _in_proj(h, p["mix_pre_g"], p["w_in"], tl.tabs, tl.ts)
    o = _attn_fwd(q, k, v, tl.bias, p["sinks"], tl.tm)
    return (h, a, q, k, v, b, c, hc, o)


def _out_fwd(mixed, p, tl, deps=()):
    h, a, q, k, v, b, c, hc, o = mixed
    h1, y, z = _mix_out(h, o, b, c, hc, p["conv_w"], p["attn_out_g"], p["conv_out_g"], p["w_out"], p["mix_post_g"],
                        tl.ts, deps)
    return h1, mixed + (h1, y, z)


def _mlp_fwd(h1, saved, p, tl):
    h2, a2, act, f = _mlp(h1, p["mlp_pre_g"], p["w_up"], p["w_down"], p["mlp_post_g"], tl.tm)
    return h2, saved + (a2, act, f)


def _mlp_part_bwd(dh, saved, p, tl, deps=()):
    h1, a2, act, f = saved[9], saved[12], saved[13], saved[14]
    dh1, df, dup, dg2, dg1 = _mlp_bwd(dh, f, p["mlp_post_g"], act, p["w_down"], p["w_up"], h1, p["mlp_pre_g"], tl.tm,
                                      deps)
    g = {"w_down": [d.reshape(N_CHIPS, FF_CHUNK, D_MODEL) for d in _weight_grad(act, df, "grad_w_down")],
         "w_up": [d.reshape(N_CHIPS, D_MODEL, FF_CHUNK) for d in _weight_grad(a2, dup, "grad_w_up")],
         "mlp_post_g": dg2, "mlp_pre_g": dg1}
    return dh1, g


def _mix_out_part_bwd(dh1, saved, p, tl, deps=()):
    b, c, hc, o, y, z = saved[5], saved[6], saved[7], saved[8], saved[10], saved[11]
    dz, do, dbch, dgp, dga, dgc, dcw = _mix_out_bwd(dh1, z, p["mix_post_g"], p["w_out"], o, b, c, hc, p["conv_w"],
                                                    p["attn_out_g"], p["conv_out_g"], tl.ts, deps)
    g = {"w_out": [d.reshape(N_CHIPS, D_MODEL // N_CHIPS, D_MODEL) for d in _weight_grad(y, dz, "grad_w_out")],
         "mix_post_g": dgp, "attn_out_g": dga, "conv_out_g": dgc, "conv_w": dcw}
    return (dh1, do, dbch), g


def _attn_in_part_bwd(carry, saved, p, tl, deps=()):
    dh1, do, dbch = carry
    h_in, a, q, k, v, o = saved[0], saved[1], saved[2], saved[3], saved[4], saved[8]
    dq, dk, dv, dsink = _attn_bwd(q, k, v, o, do, tl.bias, p["sinks"], tl.tm, deps)
    dh, dproj, dgi = _in_proj_bwd(dq, dk, dv, dbch, p["w_in"], dh1, h_in, p["mix_pre_g"], tl.tabs, tl.ts)
    g_in = [d.reshape(N_CHIPS, IN_W // N_CHIPS, D_MODEL) for d in _weight_grad(dproj, a, "grad_w_in")]
    return dh, {"w_in": g_in, "mix_pre_g": dgi, "sinks": dsink[:, 0]}


def _place():
    return lax.axis_index("x"), lax.axis_index("y"), lax.axis_index("c")


def _other_chips(x, y):
    return [(1 - x, y), (x, 1 - y), (1 - x, 1 - y)]


_HBM = pl.BlockSpec(memory_space=pltpu.HBM)
_SEM = pl.BlockSpec(memory_space=pltpu.SEMAPHORE)
_EFFECT = pltpu.SideEffectType.DATAFLOW_SIDE_EFFECTING


class _Exchange:
    def __init__(self, name, bufs, plan, n, after=()):
        self.name, self.plan, nb = name, plan, len(bufs)
        n_in = nb + len(after)

        def body(*refs):
            send, recv, token = refs[n_in], refs[n_in + 1], refs[-1]
            for k, (src, dst, target, _) in enumerate(plan(refs[:nb])):
                pltpu.make_async_remote_copy(src_ref=src, dst_ref=dst, send_sem=send.at[k], recv_sem=recv.at[k],
                                             device_id=target, device_id_type=MESH).start()
            token[...] = jnp.zeros_like(token)

        outs = pl.pallas_call(
            body, name=name + "_start",
            out_shape=(pltpu.SemaphoreType.DMA((n,)), pltpu.SemaphoreType.DMA((n,)),
                       *[pltpu.HBM(b.shape, b.dtype) for b in bufs], jax.ShapeDtypeStruct((8, 128), F32)),
            in_specs=[_HBM] * nb + [pl.BlockSpec(memory_space=pl.ANY)] * len(after),
            out_specs=(_SEM, _SEM, *[_HBM] * nb, pl.BlockSpec(memory_space=pltpu.VMEM)),
            input_output_aliases={i: 2 + i for i in range(nb)},
            compiler_params=pltpu.CompilerParams(has_side_effects=_EFFECT),
        )(*[pltpu.with_memory_space_constraint(b, pltpu.HBM) for b in bufs], *after)
        self.send, self.recv, self.bufs, self.token = outs[0], outs[1], list(outs[2:2 + nb]), outs[-1]

    def wait(self, *after):
        plan, nb = self.plan, len(self.bufs)

        def body(*refs):
            send, recv = refs[nb], refs[nb + 1]
            for k, (src, _, target, land) in enumerate(plan(refs[:nb])):
                cp = pltpu.make_async_remote_copy(src_ref=src, dst_ref=land, send_sem=send.at[k], recv_sem=recv.at[k],
                                                  device_id=target, device_id_type=MESH)
                cp.wait_send()
                cp.wait_recv()

        outs = pl.pallas_call(
            body, name=self.name + "_wait", out_shape=[pltpu.HBM(b.shape, b.dtype) for b in self.bufs],
            in_specs=[_HBM] * nb + [_SEM, _SEM] + [pl.BlockSpec(memory_space=pl.ANY)] * len(after),
            out_specs=[_HBM] * nb, input_output_aliases={i: i for i in range(nb)},
            compiler_params=pltpu.CompilerParams(has_side_effects=_EFFECT),
        )(*self.bufs, self.send, self.recv, *after)
        return list(outs)


def _gather_plan(n):
    def plan(refs):
        x, y, c = _place()
        me = 2 * x + y
        return [(refs[a].at[me], refs[a].at[me], (px, py, c), refs[a].at[2 * px + py])
                for a in range(n) for px, py in _other_chips(x, y)]

    return plan


def _peers():
    x, y, c = _place()
    return [(k - 1, (x ^ (k >> 2), y ^ ((k >> 1) & 1), c ^ (k & 1))) for k in range(1, N_DEV)]


def _scatter_plan(n, half_rows):
    def plan(refs):
        out = []
        for a in range(n):
            hr = half_rows[a]
            for k, (px, py, pc) in _peers():
                out.append((refs[a].at[2 * px + py, pl.ds(pc * hr, hr)], refs[n + a].at[k], (px, py, pc),
                            refs[n + a].at[k]))
        return out

    return plan


def _join_plan(n):
    def plan(refs):
        x, y, c = _place()
        return [(refs[a].at[c], refs[a].at[c], (x, y, 1 - c), refs[a].at[1 - c]) for a in range(n)]

    return plan


def _sum_parts(g, q):
    rows, cols = g.shape[1], g.shape[2]
    hr = rows // 2
    tr = _block_rows(hr)
    per = hr // tr
    x, y, c = _place()
    where = jnp.stack([2 * x + y, c]).astype(jnp.int32)

    def body(where_ref, g_ref, q_ref, o_ref):
        total = g_ref[...]
        for k in range(N_DEV - 1):
            total = total + q_ref[k].astype(F32)
        o_ref[...] = total

    return pl.pallas_call(
        body, name="sum_parts",
        grid_spec=pltpu.PrefetchScalarGridSpec(
            num_scalar_prefetch=1, grid=(per,),
            in_specs=[pl.BlockSpec((None, tr, cols), lambda i, where_ref: (where_ref[0], where_ref[1] * per + i, 0)),
                      pl.BlockSpec((N_DEV - 1, tr, cols), lambda i, where_ref: (0, i, 0))],
            out_specs=pl.BlockSpec((None, tr, cols), lambda i, where_ref: (where_ref[1], i, 0))),
        out_shape=jax.ShapeDtypeStruct((2, hr, cols), F32),
        compiler_params=_params("parallel"),
    )(where, g, q)


def _sum_devices(packed):
    def body(p_ref, o_ref, land, send_sems, recv_sems):
        x, y, c = _place()
        me = 4 * x + 2 * y + c
        land[me] = p_ref[...]
        sends = []
        for k in range(1, N_DEV):
            px, py, pc = x ^ (k >> 2), y ^ ((k >> 1) & 1), c ^ (k & 1)
            cp = pltpu.make_async_remote_copy(src_ref=p_ref, dst_ref=land.at[me], send_sem=send_sems.at[k - 1],
                                              recv_sem=recv_sems.at[k - 1], device_id=(px, py, pc), device_id_type=MESH)
            cp.start()
            sends.append(cp)
        for k in range(1, N_DEV):
            px, py, pc = x ^ (k >> 2), y ^ ((k >> 1) & 1), c ^ (k & 1)
            pltpu.make_async_remote_copy(src_ref=p_ref, dst_ref=land.at[4 * px + 2 * py + pc],
                                         send_sem=send_sems.at[k - 1], recv_sem=recv_sems.at[k - 1],
                                         device_id=(px, py, pc), device_id_type=MESH).wait_recv()
        for cp in sends:
            cp.wait_send()
        total = land[0]
        for d in range(1, N_DEV):
            total = total + land[d]
        o_ref[...] = total

    vm = pl.BlockSpec(memory_space=pltpu.VMEM)
    return pl.pallas_call(
        body, name="sum_devices", in_specs=[vm], out_specs=vm,
        out_shape=jax.ShapeDtypeStruct(packed.shape, F32),
        scratch_shapes=[pltpu.VMEM((N_DEV,) + packed.shape, F32), pltpu.SemaphoreType.DMA((N_DEV - 1,)),
                        pltpu.SemaphoreType.DMA((N_DEV - 1,))],
    )(packed)


def _adamw_math(w, g, m, v):
    m = ADAM_B1 * m + (1.0 - ADAM_B1) * g
    v = ADAM_B2 * v + (1.0 - ADAM_B2) * jnp.square(g)
    m_hat = m / (1.0 - ADAM_B1 ** ADAM_STEP)
    v_hat = v / (1.0 - ADAM_B2 ** ADAM_STEP)
    delta = -ADAM_LR * (m_hat / (jnp.sqrt(v_hat) + ADAM_EPS) + ADAM_WD * w)
    return delta, m, v


def _adamw_large(layer, w, halves, m, v, other):
    _, rows, cols = w.shape
    tr = _block_rows(rows // 2)
    per = rows // 2 // tr

    def body(w_ref, g_ref, m_ref, v_ref, *rest):
        g_out, d_ref, nm_ref, nv_ref = rest[-4:]
        g = g_ref[...]
        g_out[...] = g
        d_ref[...], nm_ref[...], nv_ref[...] = _adamw_math(w_ref[...], g, m_ref[...], v_ref[...])

    blk = pl.BlockSpec((None, tr, cols), lambda i: (layer, i, 0))
    half = pl.BlockSpec((None, tr, cols), lambda i: (i // per, i % per, 0))
    kept = [] if other is None else list(other)
    return pl.pallas_call(
        body, name="adamw_large", grid=(rows // tr,),
        in_specs=[blk, half, blk, blk] + [pl.BlockSpec(memory_space=pl.ANY)] * len(kept), out_specs=[blk] * 4,
        out_shape=[jax.ShapeDtypeStruct(w.shape, F32)] * 4,
        input_output_aliases={4 + k: k for k in range(len(kept))},
        compiler_params=_params("parallel"),
    )(w, halves, m, v, *kept)


def _adamw_small(ws, gs, ms, vs):
    n = len(ws)

    def body(*refs):
        w_r, g_r, m_r, v_r = refs[:n], refs[n:2 * n], refs[2 * n:3 * n], refs[3 * n:4 * n]
        d_r, nm_r, nv_r = refs[4 * n:5 * n], refs[5 * n:6 * n], refs[6 * n:]
        for a in range(n):
            d_r[a][...], nm_r[a][...], nv_r[a][...] = _adamw_math(w_r[a][...], g_r[a][...], m_r[a][...], v_r[a][...])

    vm = pl.BlockSpec(memory_space=pltpu.VMEM)
    outs = pl.pallas_call(
        body, name="adamw_small", in_specs=[vm] * (4 * n), out_specs=[vm] * (3 * n),
        out_shape=[jax.ShapeDtypeStruct(w.shape, F32) for w in ws] * 3,
    )(*ws, *gs, *ms, *vs)
    return outs[:n], outs[n:2 * n], outs[2 * n:]


_LARGE = ("w_in", "w_out", "w_up", "w_down")
_SMALL = ("meta_tokens", "mix_pre_g", "conv_w", "sinks", "attn_out_g", "conv_out_g", "mix_post_g", "mlp_pre_g",
          "mlp_post_g")
_ORDER = ("meta_tokens", "mix_pre_g", "w_in", "conv_w", "sinks", "attn_out_g", "conv_out_g", "w_out", "mix_post_g",
          "mlp_pre_g", "w_up", "w_down", "mlp_post_g")


class _Reduce:
    def __init__(self, name, grads, after=()):
        self.name, self.n = name, len(grads)
        self.own = [g for g, _ in grads]
        half_rows = [g.shape[1] // 2 for g in self.own]
        zones = [lax.empty((N_DEV - 1, hr, g.shape[2]), BF16) for g, hr in zip(self.own, half_rows)]
        self.exchange = _Exchange(name + "_scatter", [b for _, b in grads] + zones, _scatter_plan(self.n, half_rows),
                                  (N_DEV - 1) * self.n, after)

    @property
    def token(self):
        return self.exchange.token

    def join(self, *after):
        bufs = self.exchange.wait(*after)
        halves = [_sum_parts(g, q) for g, q in zip(self.own, bufs[self.n:])]
        self.exchange = _Exchange(self.name + "_join", halves, _join_plan(self.n), self.n)

    def done(self, *after):
        return self.exchange.wait(*after)


def _pad_cols(a, n=D_MODEL):
    return jnp.pad(a, ((0, 0), (0, n - a.shape[1])))


def kernel(x, meta_tokens, mix_pre_g, w_in, conv_w, sinks, attn_out_g, conv_out_g, w_out, mix_post_g, mlp_pre_g, w_up, w_down, mlp_post_g, loss_target, m_meta_tokens, m_mix_pre_g, m_w_in, m_conv_w, m_sinks, m_attn_out_g, m_conv_out_g, m_w_out, m_mix_post_g, m_mlp_pre_g, m_w_up, m_w_down, m_mlp_post_g, v_meta_tokens, v_mix_pre_g, v_w_in, v_conv_w, v_sinks, v_attn_out_g, v_conv_out_g, v_w_out, v_mix_post_g, v_mlp_pre_g, v_w_up, v_w_down, v_mlp_post_g):
    w = dict(meta_tokens=meta_tokens, mix_pre_g=mix_pre_g, w_in=w_in, conv_w=conv_w, sinks=sinks,
             attn_out_g=attn_out_g, conv_out_g=conv_out_g, w_out=w_out, mix_post_g=mix_post_g, mlp_pre_g=mlp_pre_g,
             w_up=w_up, w_down=w_down, mlp_post_g=mlp_post_g)
    m = dict(meta_tokens=m_meta_tokens, mix_pre_g=m_mix_pre_g, w_in=m_w_in, conv_w=m_conv_w, sinks=m_sinks,
             attn_out_g=m_attn_out_g, conv_out_g=m_conv_out_g, w_out=m_w_out, mix_post_g=m_mix_post_g,
             mlp_pre_g=m_mlp_pre_g, w_up=m_w_up, w_down=m_w_down, mlp_post_g=m_mlp_post_g)
    v = dict(meta_tokens=v_meta_tokens, mix_pre_g=v_mix_pre_g, w_in=v_w_in, conv_w=v_conv_w, sinks=v_sinks,
             attn_out_g=v_attn_out_g, conv_out_g=v_conv_out_g, w_out=v_w_out, mix_post_g=v_mix_post_g,
             mlp_pre_g=v_mlp_pre_g, w_up=v_w_up, w_down=v_w_down, mlp_post_g=v_mlp_post_g)
    chip = 2 * lax.axis_index("x") + lax.axis_index("y")
    tl = _Tiles(x.shape[1] + BLOCK)

    def zone(quarter):
        return lax.dynamic_update_slice(lax.empty((N_CHIPS,) + quarter.shape, quarter.dtype), quarter[None],
                                        (chip,) + (0,) * quarter.ndim)

    w, m, v = ({**d, "w_in": jnp.swapaxes(d["w_in"], 1, 2)} for d in (w, m, v))
    zones = {n: [zone(w[n][l].astype(BF16)) for l in range(DEPTH)] for n in _LARGE}
    first = _Exchange("gather_first", [zones["w_in"][0], zone(w["conv_w"]), zone(w["meta_tokens"])], _gather_plan(3), 9)
    out0 = _Exchange("gather_out", [zones["w_out"][0]], _gather_plan(1), 3, [first.token])
    rest = _Exchange("gather_rest", [zones[n][0] for n in ("w_up", "w_down")], _gather_plan(2), 6, [out0.token])

    def whole_in(quarters):
        return quarters.reshape(IN_W, D_MODEL)

    h = jnp.concatenate([jnp.zeros((BLOCK, D_MODEL), F32), x[0]], axis=0)
    q_in, q_conv, q_meta = first.wait(rest.token, *tl.tabs, tl.bias, h)
    conv_whole = jnp.transpose(q_conv, (1, 2, 0, 3)).reshape(DEPTH, CONV_K, CONV_W)
    meta = jnp.transpose(q_meta, (1, 0, 2)).reshape(N_META, D_MODEL)
    p = [{"conv_w": conv_whole[l], "sinks": w["sinks"][l]} for l in range(DEPTH)]
    for l in range(DEPTH):
        for n in ("mix_pre_g", "attn_out_g", "conv_out_g", "mix_post_g", "mlp_pre_g", "mlp_post_g"):
            p[l][n] = w[n][l][None, :]

    h = lax.dynamic_update_slice(h, meta, (LEAD_PAD, 0))
    p[0]["w_in"] = whole_in(q_in)
    mixed = _mixer_fwd(h, p[0], tl)
    second = _Exchange("gather_second", [zones["w_in"][1], zones["w_out"][1]], _gather_plan(2), 6, [mixed[-1]])
    second_mlp = _Exchange("gather_second_mlp", [zones["w_up"][1], zones["w_down"][1]], _gather_plan(2), 6,
                           [second.token])
    p[0]["w_out"], = out0.wait(second_mlp.token)
    h1, saved0 = _out_fwd(mixed, p[0], tl)
    p[0]["w_up"], p[0]["w_down"] = rest.wait(h1)
    h, saved0 = _mlp_fwd(h1, saved0, p[0], tl)
    q_in, p[1]["w_out"] = second.wait(h)
    p[1]["w_in"] = whole_in(q_in)
    h1, saved1 = _out_fwd(_mixer_fwd(h, p[1], tl), p[1], tl)
    p[1]["w_up"], p[1]["w_down"] = second_mlp.wait(h1)
    h, saved1 = _mlp_fwd(h1, saved1, p[1], tl)
    loss_tile, dh = _loss_head(h, loss_target[0], tl.tm)
    loss = lax.psum(loss_tile[0, 0], ("x", "y", "c"))

    def adamw(layer, halves, other):
        return {n: _adamw_large(layer, w[n], halves[n], m[n], v[n], None if other is None else other[n])
                for n in halves}

    dh1, g1 = _mlp_part_bwd(dh, saved1, p[1], tl)
    carry, gm = _mix_out_part_bwd(dh1, saved1, p[1], tl)
    dh, gi = _attn_in_part_bwd(carry, saved1, p[1], tl)
    g1.update(gm, **gi)
    red1 = _Reduce("reduce1", [g1[n] for n in _LARGE])
    dh1, g0 = _mlp_part_bwd(dh, saved0, p[0], tl, [red1.token])
    red1.join(g0["w_down"][0])
    carry, gm = _mix_out_part_bwd(dh1, saved0, p[0], tl, [red1.token])
    first0 = ("w_up", "w_down", "w_out")
    g0.update(gm)
    red0a = _Reduce("reduce0a", [g0[n] for n in first0])
    dh0, gi = _attn_in_part_bwd(carry, saved0, p[0], tl, [red0a.token])
    g0.update(gi)
    red0b = _Reduce("reduce0b", [g0["w_in"]])
    grad_x = dh0[BLOCK:][None]
    grads = {n: [g0[n], g1[n]] for n in g0 if n not in _LARGE}

    rows = [dh0[LEAD_PAD:BLOCK]]
    for n in ("mix_pre_g", "mix_post_g", "mlp_pre_g", "mlp_post_g"):
        rows += grads[n]
    rows += [jnp.concatenate([grads["attn_out_g"][l], grads["conv_out_g"][l]], axis=1) for l in range(DEPTH)]
    rows.append(jnp.concatenate(grads["conv_w"], axis=1))
    rows.append(_pad_cols(jnp.concatenate(grads["sinks"])[None, :]))
    packed = jnp.concatenate(rows, axis=0)
    packed = jnp.pad(packed, ((0, SMALL_ROWS - packed.shape[0]), (0, 0)))
    total = _sum_devices(packed)
    r0 = N_META
    small = {
        "meta_tokens": lax.dynamic_slice(total[:N_META], (0, chip * (D_MODEL // N_CHIPS)), (N_META, D_MODEL // N_CHIPS)),
        "mix_pre_g": total[r0:r0 + 2], "mix_post_g": total[r0 + 2:r0 + 4], "mlp_pre_g": total[r0 + 4:r0 + 6],
        "mlp_post_g": total[r0 + 6:r0 + 8],
        "attn_out_g": total[r0 + 8:r0 + 10, :ATTN_W], "conv_out_g": total[r0 + 8:r0 + 10, ATTN_W:],
        "conv_w": lax.dynamic_slice(total[r0 + 10:r0 + 13].reshape(CONV_K, DEPTH, CONV_W).transpose(1, 0, 2),
                                    (0, 0, chip * (CONV_W // N_CHIPS)), (DEPTH, CONV_K, CONV_W // N_CHIPS)),
        "sinks": total[r0 + 13, :DEPTH * N_Q_HEADS].reshape(DEPTH, N_Q_HEADS),
    }

    ds, nms, nvs = _adamw_small([w[n] for n in _SMALL], [small[n] for n in _SMALL], [m[n] for n in _SMALL],
                                [v[n] for n in _SMALL])
    done1 = adamw(1, dict(zip(_LARGE, red1.done(red0b.token))), None)
    red0a.join(ds[0], grad_x, *[done1[n][0] for n in _LARGE])
    red0b.join(red0a.token)
    done0 = adamw(0, dict(zip(first0, red0a.done(red0b.token))), done1)
    done0.update(adamw(0, {"w_in": red0b.done(done0["w_down"][0])[0]}, done1))
    grad, delta, new_m, new_v = {}, {}, {}, {}
    for n in _LARGE:
        grad[n], delta[n], new_m[n], new_v[n] = done0[n]
    for d in (grad, delta, new_m, new_v):
        d["w_in"] = jnp.swapaxes(d["w_in"], 1, 2)
    for i, n in enumerate(_SMALL):
        grad[n], delta[n], new_m[n], new_v[n] = small[n], ds[i], nms[i], nvs[i]
    return (loss, grad_x, *[grad[n] for n in _ORDER], *[delta[n] for n in _ORDER], *[new_m[n] for n in _ORDER],
            *[new_v[n] for n in _ORDER])
```

```python
import functools

import jax
import jax.numpy as jnp
from jax import lax
from jax.experimental import pallas as pl
from jax.experimental.pallas import tpu as pltpu

F32 = jnp.float32
BF16 = jnp.bfloat16

D_MODEL = 1024
DEPTH = 2
N_META = 16
ATTN_W = 512
CONV_W = 512
HEAD_DIM = 64
N_Q_HEADS = 8
N_KV_HEADS = 2
GROUP = N_Q_HEADS // N_KV_HEADS
KV_W = N_KV_HEADS * HEAD_DIM
CONV_K = 3
BLOCK = 128
LEAD_PAD = BLOCK - N_META
ROPE_THETA = 500000.0
ROT_DIM = HEAD_DIM // 4
ROT_HALF = ROT_DIM // 2
D_FF = 4 * D_MODEL
IN_W = ATTN_W + 2 * KV_W + 3 * CONV_W
QKV_W = ATTN_W + 2 * KV_W
EPS = 1e-6
SCALE = HEAD_DIM ** -0.5
FF_CHUNK = 1024
N_CHIPS = 4
N_DEV = 8

ADAM_LR = 0.001
ADAM_B1 = 0.9
ADAM_B2 = 0.999
ADAM_EPS = 1e-08
ADAM_WD = 0.01
ADAM_STEP = 10

V7X_VMEM_LIMIT = 56 * 1024 * 1024
SMALL_ROWS = 32

MESH = pl.DeviceIdType.MESH


def _params(*sem):
    return pltpu.CompilerParams(dimension_semantics=sem, vmem_limit_bytes=V7X_VMEM_LIMIT)


def _block_rows(n):
    return max(r for r in range(16, min(n, 256) + 1, 16) if n % r == 0)


def _row_tile(t, most):
    nb = t // BLOCK
    for b in range(most // BLOCK, 0, -1):
        if nb % b == 0:
            return b * BLOCK
    return BLOCK


def _behind(body, deps):
    n = len(deps)

    def wrapped(*refs):
        body(*refs[n:])

    return wrapped, [pl.BlockSpec(memory_space=pl.ANY)] * n


def _rms(x, g):
    r = lax.rsqrt(jnp.mean(x * x, axis=-1, keepdims=True) + EPS)
    return x * r * g


def _rms_bwd(dy, x, g):
    r = lax.rsqrt(jnp.mean(x * x, axis=-1, keepdims=True) + EPS)
    xh = x * r
    dg = jnp.sum(dy * xh, axis=0, keepdims=True)
    dxh = dy * g
    dx = r * (dxh - xh * jnp.mean(dxh * xh, axis=-1, keepdims=True))
    return dx, dg


def _rope(x, cos, sa, sb):
    n = x.shape[-1]
    return x * cos + pltpu.roll(x, n - ROT_HALF, 1) * sa + pltpu.roll(x, ROT_HALF, 1) * sb


def _rope_bwd(dy, cos, sa, sb):
    n = dy.shape[-1]
    return dy * cos + pltpu.roll(dy * sa, ROT_HALF, 1) + pltpu.roll(dy * sb, n - ROT_HALF, 1)


def _rope_tables(t):
    pos = lax.broadcasted_iota(jnp.int32, (t, ROT_HALF), 0).astype(F32) - LEAD_PAD
    pair = lax.broadcasted_iota(jnp.int32, (t, ROT_HALF), 1).astype(F32)
    inv_freq = jnp.power(jnp.float32(ROPE_THETA), -(2.0 * pair) / ROT_DIM)
    ang = pos * inv_freq
    cos, sin = lax.optimization_barrier((jnp.cos(ang), jnp.sin(ang)))
    spread = (1, 2 * HEAD_DIM // ROT_HALF)
    cos, sin = jnp.tile(cos, spread), jnp.tile(sin, spread)
    dim = lax.broadcasted_iota(jnp.int32, (t, 2 * HEAD_DIM), 1) % HEAD_DIM
    return (jnp.where(dim < ROT_DIM, cos, 1.0), jnp.where(dim < ROT_HALF, -sin, 0.0),
            jnp.where((dim >= ROT_HALF) & (dim < ROT_DIM), sin, 0.0))


def _in_proj(h, g, w, tabs, tm):
    t = h.shape[0]

    def body(h_ref, g_ref, w_ref, c_ref, sa_ref, sb_ref, a_ref, q_ref, k_ref, v_ref, b_ref, cg_ref, hc_ref):
        a = _rms(h_ref[...], g_ref[...]).astype(BF16)
        a_ref[...] = a
        p = lax.dot_general(a, w_ref[...], (((1,), (1,)), ((), ())), preferred_element_type=F32)
        cos, sa, sb = c_ref[...], sa_ref[...], sb_ref[...]
        rep = ATTN_W // (2 * HEAD_DIM)
        q = _rope(p[:, :ATTN_W], jnp.tile(cos, (1, rep)), jnp.tile(sa, (1, rep)), jnp.tile(sb, (1, rep)))
        q_ref[...] = (q * SCALE).astype(BF16)
        k_ref[...] = _rope(p[:, ATTN_W:ATTN_W + KV_W], cos, sa, sb).astype(BF16)
        v_ref[...] = p[:, ATTN_W + KV_W:QKV_W].astype(BF16)
        b_ref[...] = p[:, QKV_W:QKV_W + CONV_W]
        cg_ref[...] = p[:, QKV_W + CONV_W:QKV_W + 2 * CONV_W]
        hc_ref[...] = p[:, QKV_W + 2 * CONV_W:]

    row = lambda n: pl.BlockSpec((tm, n), lambda i: (i, 0))
    full = lambda a: pl.BlockSpec(a.shape, lambda i: (0, 0))
    return pl.pallas_call(
        body, name="in_proj", grid=(t // tm,),
        in_specs=[row(D_MODEL), full(g), full(w), row(2 * HEAD_DIM), row(2 * HEAD_DIM), row(2 * HEAD_DIM)],
        out_specs=[row(D_MODEL), row(ATTN_W), row(KV_W), row(KV_W), row(CONV_W), row(CONV_W), row(CONV_W)],
        out_shape=[jax.ShapeDtypeStruct((t, D_MODEL), BF16), jax.ShapeDtypeStruct((t, ATTN_W), BF16),
                   jax.ShapeDtypeStruct((t, KV_W), BF16), jax.ShapeDtypeStruct((t, KV_W), BF16),
                   jax.ShapeDtypeStruct((t, CONV_W), F32), jax.ShapeDtypeStruct((t, CONV_W), F32),
                   jax.ShapeDtypeStruct((t, CONV_W), F32)],
        compiler_params=_params("parallel"),
    )(h, g, w, *tabs)


def _attn_bias():
    r = lax.broadcasted_iota(jnp.int32, (3, BLOCK, 2 * BLOCK), 1)
    c = lax.broadcasted_iota(jnp.int32, (3, BLOCK, 2 * BLOCK), 2)
    i = lax.broadcasted_iota(jnp.int32, (3, BLOCK, 2 * BLOCK), 0)
    ok = (c > r) & (c <= r + BLOCK) & (c + (i - 1) * BLOCK >= LEAD_PAD)
    return jnp.where(ok, 0.0, -jnp.inf).astype(F32)


def _attn_scores(qh, kg, bias):
    return lax.dot_general(qh, kg, (((1,), (1,)), ((), ())), preferred_element_type=F32) + bias


def _attn_probs(s, sk):
    m = jnp.maximum(jnp.max(s, axis=-1, keepdims=True), sk)
    e = jnp.exp(s - m)
    es = jnp.exp(sk - m)
    rden = 1.0 / (jnp.sum(e, axis=-1, keepdims=True) + es)
    return e * rden, es * rden


def _head(hh):
    return slice(hh * HEAD_DIM, (hh + 1) * HEAD_DIM)


def _two_blocks(ref, i):
    prev = jnp.maximum(i - 1, 0)
    return jnp.concatenate([ref[pl.ds(pl.multiple_of(prev * BLOCK, BLOCK), BLOCK), :],
                            ref[pl.ds(pl.multiple_of(i * BLOCK, BLOCK), BLOCK), :]], axis=0)


def _attn_fwd(q, k, v, bias, sinks, tm):
    t = q.shape[0]
    per_step = tm // BLOCK
    heads = range(N_Q_HEADS)

    def body(s_ref, q_ref, k_ref, v_ref, bias_ref, o_ref):
        for b in range(per_step):
            i = pl.program_id(0) * per_step + b
            rows = slice(b * BLOCK, (b + 1) * BLOCK)
            kc, vc = _two_blocks(k_ref, i), _two_blocks(v_ref, i)
            bias_i = bias_ref[jnp.minimum(i, 2)]
            scores = [_attn_scores(q_ref[rows, _head(hh)], kc[:, _head(hh // GROUP)], bias_i) for hh in heads]
            probs = [_attn_probs(scores[hh], s_ref[hh])[0].astype(BF16) for hh in heads]
            for hh in heads:
                o_ref[rows, _head(hh)] = jnp.dot(probs[hh], vc[:, _head(hh // GROUP)], preferred_element_type=F32)

    whole = pl.BlockSpec((t, KV_W), lambda i: (0, 0))
    return pl.pallas_call(
        body, name="attn_fwd", grid=(t // tm,),
        in_specs=[pl.BlockSpec(memory_space=pltpu.SMEM), pl.BlockSpec((tm, ATTN_W), lambda i: (i, 0)), whole, whole,
                  pl.BlockSpec(bias.shape, lambda i: (0, 0, 0))],
        out_specs=pl.BlockSpec((tm, ATTN_W), lambda i: (i, 0)),
        out_shape=jax.ShapeDtypeStruct((t, ATTN_W), F32),
        compiler_params=_params("parallel"),
    )(sinks, q, k, v, bias)


def _shift_rows(u, halo, n):
    r = pltpu.roll(u, n, 0)
    hr = pltpu.roll(halo, n, 0)
    idx = lax.broadcasted_iota(jnp.int32, hr.shape, 0)
    return jnp.concatenate([jnp.where(idx < n, hr, r[:8]), r[8:]], axis=0)


def _advance_rows(u, halo, n):
    rows = u.shape[0]
    r = pltpu.roll(u, rows - n, 0)
    hr = pltpu.roll(halo, 8 - n, 0)
    idx = lax.broadcasted_iota(jnp.int32, hr.shape, 0)
    return jnp.concatenate([r[:rows - 8], jnp.where(idx >= 8 - n, hr, r[rows - 8:])], axis=0)


def _mix_out(h, o, b, c, hc, cw, ga, gc, w, gp, tm, deps=()):
    t = h.shape[0]

    def body(h_ref, o_ref, b_ref, c_ref, hc_ref, cw_ref, ga_ref, gc_ref, w_ref, gp_ref, h1_ref, y_ref, z_ref, halo):
        @pl.when(pl.program_id(0) == 0)
        def _():
            halo[...] = jnp.zeros_like(halo)

        u = c_ref[...] * hc_ref[...]
        cv = cw_ref[0:1, :] * _shift_rows(u, halo[...], 2) + cw_ref[1:2, :] * _shift_rows(u, halo[...], 1) \
            + cw_ref[2:3, :] * u
        halo[...] = u[tm - 8:]
        yc = b_ref[...] * cv
        y = jnp.concatenate([_rms(o_ref[...], ga_ref[...]), _rms(yc, gc_ref[...])], axis=1).astype(BF16)
        y_ref[...] = y
        z = jnp.dot(y, w_ref[...].reshape(D_MODEL, D_MODEL), preferred_element_type=F32)
        z_ref[...] = z
        h1_ref[...] = h_ref[...] + _rms(z, gp_ref[...])

    row = lambda n: pl.BlockSpec((tm, n), lambda i: (i, 0))
    full = lambda a: pl.BlockSpec(a.shape, lambda i: (0,) * a.ndim)
    body, dep_specs = _behind(body, deps)
    return pl.pallas_call(
        body, name="mix_out", grid=(t // tm,),
        in_specs=dep_specs + [row(D_MODEL), row(ATTN_W), row(CONV_W), row(CONV_W), row(CONV_W), full(cw), full(ga),
                              full(gc), full(w), full(gp)],
        out_specs=[row(D_MODEL), row(D_MODEL), row(D_MODEL)],
        out_shape=[jax.ShapeDtypeStruct((t, D_MODEL), F32), jax.ShapeDtypeStruct((t, D_MODEL), BF16),
                   jax.ShapeDtypeStruct((t, D_MODEL), F32)],
        scratch_shapes=[pltpu.VMEM((8, CONV_W), F32)],
        compiler_params=_params("arbitrary"),
    )(*deps, h, o, b, c, hc, cw, ga, gc, w, gp)


def _mlp(h1, g1, wu, wd, g2, tm):
    t = h1.shape[0]
    nj = D_FF // FF_CHUNK

    def body(h1_ref, g1_ref, wu_ref, wd_ref, g2_ref, h2_ref, a2_ref, act_ref, f_ref):
        a2 = _rms(h1_ref[...], g1_ref[...]).astype(BF16)
        a2_ref[...] = a2
        f = None
        for j in range(nj):
            up = jnp.dot(a2, wu_ref[j], preferred_element_type=F32)
            act = jnp.square(jnp.maximum(up, 0.0)).astype(BF16)
            act_ref[:, j * FF_CHUNK:(j + 1) * FF_CHUNK] = act
            part = jnp.dot(act, wd_ref[j], preferred_element_type=F32)
            f = part if f is None else f + part
        f_ref[...] = f
        h2_ref[...] = h1_ref[...] + _rms(f, g2_ref[...])

    row = pl.BlockSpec((tm, D_MODEL), lambda i: (i, 0))
    vec = pl.BlockSpec((1, D_MODEL), lambda i: (0, 0))
    resident = pl.BlockSpec(memory_space=pltpu.VMEM)
    return pl.pallas_call(
        body, name="mlp", grid=(t // tm,),
        in_specs=[row, vec, resident, resident, vec],
        out_specs=[row, row, pl.BlockSpec((tm, D_FF), lambda i: (i, 0)), row],
        out_shape=[jax.ShapeDtypeStruct((t, D_MODEL), F32), jax.ShapeDtypeStruct((t, D_MODEL), BF16),
                   jax.ShapeDtypeStruct((t, D_FF), BF16), jax.ShapeDtypeStruct((t, D_MODEL), F32)],
        compiler_params=_params("parallel"),
    )(h1, g1, wu, wd, g2)


def _loss_head(h, target, tm):
    t = h.shape[0]
    per_step = tm // BLOCK

    def body(h_ref, *rest):
        t_refs, (loss_ref, dh_ref) = rest[:per_step], rest[per_step:]
        i = pl.program_id(0)

        @pl.when(i == 0)
        def _():
            loss_ref[...] = jnp.zeros_like(loss_ref)

        total = jnp.zeros((), F32)
        for b in range(per_step):
            rows = slice(b * BLOCK, (b + 1) * BLOCK)
            err = h_ref[rows, :] - t_refs[b][...]
            if b == 0:
                err = jnp.where(i == 0, 0.0, err)
            dh_ref[rows, :] = err * (1.0 / D_MODEL)
            total = total + jnp.sum(err * err)
        loss_ref[...] += total * (0.5 / D_MODEL)

    def target_block(b):
        return pl.BlockSpec((BLOCK, D_MODEL), lambda i: (jnp.maximum(i * per_step + b - 1, 0), 0))

    return pl.pallas_call(
        body, name="loss_head", grid=(t // tm,),
        in_specs=[pl.BlockSpec((tm, D_MODEL), lambda i: (i, 0))] + [target_block(b) for b in range(per_step)],
        out_specs=[pl.BlockSpec((8, 128), lambda i: (0, 0)), pl.BlockSpec((tm, D_MODEL), lambda i: (i, 0))],
        out_shape=[jax.ShapeDtypeStruct((8, 128), F32), jax.ShapeDtypeStruct((t, D_MODEL), F32)],
        compiler_params=_params("arbitrary"),
    )(h, *([target] * per_step))


def _mlp_bwd(dh2, f, g2, act, wd, wu, h1, g1, tm, deps=()):
    t = dh2.shape[0]
    nj = D_FF // FF_CHUNK

    def body(dh2_ref, f_ref, g2_ref, act_ref, wd_ref, wu_ref, h1_ref, g1_ref, dh1_ref, df_ref, dup_ref, dg2_ref,
             dg1_ref, acc):
        i, j = pl.program_id(0), pl.program_id(1)

        @pl.when((i == 0) & (j == 0))
        def _():
            dg2_ref[...] = jnp.zeros_like(dg2_ref)
            dg1_ref[...] = jnp.zeros_like(dg1_ref)

        @pl.when(j == 0)
        def _():
            df, dg = _rms_bwd(dh2_ref[...], f_ref[...], g2_ref[...])
            df_ref[...] = df.astype(BF16)
            dg2_ref[...] += dg

        dact = lax.dot_general(df_ref[...], wd_ref[j], (((1,), (1,)), ((), ())), preferred_element_type=F32)
        dup = (dact * (2.0 * jnp.sqrt(act_ref[...].astype(F32)))).astype(BF16)
        dup_ref[...] = dup
        part = lax.dot_general(dup, wu_ref[j], (((1,), (1,)), ((), ())), preferred_element_type=F32)

        @pl.when(j == 0)
        def _():
            acc[...] = part

        @pl.when(j > 0)
        def _():
            acc[...] += part

        @pl.when(j == nj - 1)
        def _():
            dx, dg = _rms_bwd(acc[...], h1_ref[...], g1_ref[...])
            dh1_ref[...] = dh2_ref[...] + dx
            dg1_ref[...] += dg

    row = pl.BlockSpec((tm, D_MODEL), lambda i, j: (i, 0))
    vec = pl.BlockSpec((1, D_MODEL), lambda i, j: (0, 0))
    chunk = pl.BlockSpec((tm, FF_CHUNK), lambda i, j: (i, j))
    resident = pl.BlockSpec(memory_space=pltpu.VMEM)
    body, dep_specs = _behind(body, deps)
    return pl.pallas_call(
        body, name="mlp_bwd", grid=(t // tm, nj),
        in_specs=dep_specs + [row, row, vec, chunk, resident, resident, row, vec],
        out_specs=[row, row, chunk, vec, vec],
        out_shape=[jax.ShapeDtypeStruct((t, D_MODEL), F32), jax.ShapeDtypeStruct((t, D_MODEL), BF16),
                   jax.ShapeDtypeStruct((t, D_FF), BF16), jax.ShapeDtypeStruct((1, D_MODEL), F32),
                   jax.ShapeDtypeStruct((1, D_MODEL), F32)],
        scratch_shapes=[pltpu.VMEM((tm, D_MODEL), F32)],
        compiler_params=_params("arbitrary", "arbitrary"),
    )(*deps, dh2, f, g2, act, wd, wu, h1, g1)


def _row_split(t):
    tile = min(t, 1024)
    return tile, t // tile, t % tile


def _row_split_specs(t, cols, col_of):
    tile, whole, rest = _row_split(t)
    specs = [pl.BlockSpec((tile, cols), lambda *g: (jnp.minimum(g[-1], whole - 1), col_of(*g[:-1])))]
    if rest:
        specs.append(pl.BlockSpec((rest, cols), lambda *g: (whole * tile // rest, col_of(*g[:-1]))))
    return specs


def _weight_grad(x, y, name):
    t, k = x.shape
    n = y.shape[1]
    tn = FF_CHUNK
    tk = FF_CHUNK if k % FF_CHUNK == 0 else k
    _, whole, rest = _row_split(t)
    steps = whole + bool(rest)
    one_tile = k == tk and n == tn

    def body(*refs):
        o_ref, ob_ref, r = refs[-2], refs[-1], pl.program_id(2)
        if one_tile:
            o_ref, ob_ref = o_ref.at[0, 0], ob_ref.at[0, 0]

        @pl.when(r == 0)
        def _():
            o_ref[...] = jnp.zeros_like(o_ref)

        def add(x_ref, y_ref):
            o_ref[...] += lax.dot_general(x_ref[...], y_ref[...], (((0,), (0,)), ((), ())),
                                          preferred_element_type=F32)

        if rest:
            pl.when(r < whole)(lambda: add(refs[0], refs[2]))
            pl.when(r == whole)(lambda: add(refs[1], refs[3]))
        else:
            add(refs[0], refs[1])

        @pl.when(r == steps - 1)
        def _():
            ob_ref[...] = o_ref[...].astype(BF16)

    tile = pl.BlockSpec((None, None, tk, tn), lambda a, b, r: (a, b, 0, 0))
    if one_tile:
        tile = pl.BlockSpec(memory_space=pltpu.VMEM)
    return pl.pallas_call(
        body, name=name, grid=(k // tk, n // tn, steps),
        in_specs=_row_split_specs(t, tk, lambda a, b: a) + _row_split_specs(t, tn, lambda a, b: b),
        out_specs=[tile, tile],
        out_shape=[jax.ShapeDtypeStruct((k // tk, n // tn, tk, tn), F32),
                   jax.ShapeDtypeStruct((k // tk, n // tn, tk, tn), BF16)],
        compiler_params=_params("parallel", "parallel", "arbitrary"),
    )(*([x] * (1 + bool(rest))), *([y] * (1 + bool(rest))))


def _mix_out_bwd(dh1, z, gp, w, o, b, c, hc, cw, ga, gc, tm, deps=()):
    t = dh1.shape[0]
    nt = t // tm
    per8 = tm // 8

    def body(dh1_ref, z_ref, gp_ref, w_ref, o_ref, b_ref, c_ref, hc_ref, cp_ref, hp_ref, cw_ref, ga_ref, gc_ref,
             dz_ref, do_ref, dbch_ref, dgp_ref, dga_ref, dgc_ref, dcw_ref, halo):
        i = pl.program_id(0)

        @pl.when(i == 0)
        def _():
            halo[...] = jnp.zeros_like(halo)
            dgp_ref[...] = jnp.zeros_like(dgp_ref)
            dga_ref[...] = jnp.zeros_like(dga_ref)
            dgc_ref[...] = jnp.zeros_like(dgc_ref)
            dcw_ref[...] = jnp.zeros_like(dcw_ref)

        dz, dgp = _rms_bwd(dh1_ref[...], z_ref[...], gp_ref[...])
        dgp_ref[...] += dgp
        dz = dz.astype(BF16)
        dz_ref[...] = dz
        dy = lax.dot_general(dz, w_ref[...].reshape(D_MODEL, D_MODEL), (((1,), (1,)), ((), ())),
                             preferred_element_type=F32)
        do, dga = _rms_bwd(dy[:, :ATTN_W], o_ref[...], ga_ref[...])
        do_ref[...] = do
        dga_ref[...] += dga

        u = c_ref[...] * hc_ref[...]
        first = i == nt - 1
        u_before = jnp.where(first, 0.0, cp_ref[...] * hp_ref[...])
        u1 = _shift_rows(u, u_before, 1)
        u2 = _shift_rows(u, u_before, 2)
        cv = cw_ref[0:1, :] * u2 + cw_ref[1:2, :] * u1 + cw_ref[2:3, :] * u
        bb = b_ref[...]
        dyc, dgc = _rms_bwd(dy[:, ATTN_W:], bb * cv, gc_ref[...])
        dgc_ref[...] += dgc
        dcv = dyc * bb
        d1 = _advance_rows(dcv, halo[...], 1)
        d2 = _advance_rows(dcv, halo[...], 2)
        halo[...] = dcv[:8]
        du = cw_ref[2:3, :] * dcv + cw_ref[1:2, :] * d1 + cw_ref[0:1, :] * d2
        dbch_ref[...] = jnp.concatenate([dyc * cv, du * hc_ref[...], du * c_ref[...]], axis=1).astype(BF16)
        dcw_ref[...] += jnp.concatenate([jnp.sum(dcv * u2, axis=0, keepdims=True),
                                         jnp.sum(dcv * u1, axis=0, keepdims=True),
                                         jnp.sum(dcv * u, axis=0, keepdims=True)], axis=0)

    row = lambda n: pl.BlockSpec((tm, n), lambda i: (nt - 1 - i, 0))
    before = pl.BlockSpec((8, CONV_W), lambda i: (jnp.maximum((nt - 1 - i) * per8 - 1, 0), 0))
    full = lambda a: pl.BlockSpec(a.shape, lambda i: (0,) * a.ndim)
    vec = lambda n: pl.BlockSpec((1, n), lambda i: (0, 0))
    body, dep_specs = _behind(body, deps)
    return pl.pallas_call(
        body, name="mix_out_bwd", grid=(nt,),
        in_specs=dep_specs + [row(D_MODEL), row(D_MODEL), full(gp), full(w), row(ATTN_W), row(CONV_W), row(CONV_W),
                              row(CONV_W), before, before, full(cw), full(ga), full(gc)],
        out_specs=[row(D_MODEL), row(ATTN_W), row(3 * CONV_W), vec(D_MODEL), vec(ATTN_W), vec(CONV_W),
                   pl.BlockSpec((CONV_K, CONV_W), lambda i: (0, 0))],
        out_shape=[jax.ShapeDtypeStruct((t, D_MODEL), BF16), jax.ShapeDtypeStruct((t, ATTN_W), F32),
                   jax.ShapeDtypeStruct((t, 3 * CONV_W), BF16), jax.ShapeDtypeStruct((1, D_MODEL), F32),
                   jax.ShapeDtypeStruct((1, ATTN_W), F32), jax.ShapeDtypeStruct((1, CONV_W), F32),
                   jax.ShapeDtypeStruct((CONV_K, CONV_W), F32)],
        scratch_shapes=[pltpu.VMEM((8, CONV_W), F32)],
        compiler_params=_params("arbitrary"),
    )(*deps, dh1, z, gp, w, o, b, c, hc, c, hc, cw, ga, gc)


def _attn_bwd(q, k, v, o, do, bias, sinks, tm, deps=()):
    t = q.shape[0]
    per_step = tm // BLOCK

    def body(s_ref, q_ref, k_ref, v_ref, o_ref, do_ref, bias_ref, dq_ref, dk_ref, dv_ref, ds_ref):
        step = pl.program_id(0)

        @pl.when(step == 0)
        def _():
            ds_ref[...] = jnp.zeros_like(ds_ref)

        heads = range(N_Q_HEADS)

        def first_matmuls(b):
            i = step * per_step + b
            rows = slice(b * BLOCK, (b + 1) * BLOCK)
            kc, vc = _two_blocks(k_ref, i), _two_blocks(v_ref, i)
            bias_i = bias_ref[jnp.minimum(i, 2)]
            kgs = [kc[:, _head(g)] for g in range(N_KV_HEADS)]
            vgs = [vc[:, _head(g)] for g in range(N_KV_HEADS)]
            qs = [q_ref[rows, _head(hh)] for hh in heads]
            dos = [do_ref[rows, _head(hh)] for hh in heads]
            dosb = [d.astype(BF16) for d in dos]
            scores = [_attn_scores(qs[hh], kgs[hh // GROUP], bias_i) for hh in heads]
            dps = [lax.dot_general(dosb[hh], vgs[hh // GROUP], (((1,), (1,)), ((), ())), preferred_element_type=F32)
                   for hh in heads]
            return kgs, qs, dos, dosb, scores, dps

        dsink = [jnp.zeros((BLOCK, 1), F32) for _ in range(N_Q_HEADS)]
        ahead = None
        for b in range(per_step):
            i = step * per_step + b
            rows = slice(b * BLOCK, (b + 1) * BLOCK)
            kgs, qs, dos, dosb, scores, dps = first_matmuls(b)
            ps, dss = [], []
            for hh in heads:
                p, share = _attn_probs(scores[hh], s_ref[hh])
                drow = jnp.sum(dos[hh] * o_ref[rows, _head(hh)], axis=-1, keepdims=True)
                dss.append((p * (dps[hh] - drow)).astype(BF16))
                ps.append(p.astype(BF16))
                dsink[hh] = dsink[hh] + share * drow
            for hh in heads:
                dq_ref[rows, _head(hh)] = jnp.dot(dss[hh], kgs[hh // GROUP], preferred_element_type=F32) * SCALE
            groups = [slice(GROUP * g, GROUP * (g + 1)) for g in range(N_KV_HEADS)]
            dkg = [lax.dot_general(jnp.concatenate(dss[gr], axis=0), jnp.concatenate(qs[gr], axis=0),
                                   (((0,), (0,)), ((), ())), preferred_element_type=F32) for gr in groups]
            dvg = [lax.dot_general(jnp.concatenate(ps[gr], axis=0), jnp.concatenate(dosb[gr], axis=0),
                                   (((0,), (0,)), ((), ())), preferred_element_type=F32) for gr in groups]
            dkb, dvb = jnp.concatenate(dkg, axis=1), jnp.concatenate(dvg, axis=1)
            if b == 0:
                @pl.when(step > 0)
                def _():
                    before = pl.ds(pl.multiple_of((i - 1) * BLOCK, BLOCK), BLOCK)
                    dk_ref[before, :] += dkb[:BLOCK]
                    dv_ref[before, :] += dvb[:BLOCK]
            else:
                at = pl.ds(pl.multiple_of((i - 1) * BLOCK, BLOCK), BLOCK)
                dk_ref[at, :] = ahead[0] + dkb[:BLOCK]
                dv_ref[at, :] = ahead[1] + dvb[:BLOCK]
            ahead = (dkb[BLOCK:], dvb[BLOCK:])
        last = pl.ds(pl.multiple_of(((step + 1) * per_step - 1) * BLOCK, BLOCK), BLOCK)
        dk_ref[last, :] = ahead[0]
        dv_ref[last, :] = ahead[1]
        for hh in range(N_Q_HEADS):
            ds_ref[hh:hh + 1, :] -= jnp.sum(dsink[hh])

    whole = pl.BlockSpec((t, KV_W), lambda i: (0, 0))
    blk = pl.BlockSpec((tm, ATTN_W), lambda i: (i, 0))
    body, dep_specs = _behind(body, deps)
    return pl.pallas_call(
        body, name="attn_bwd", grid=(t // tm,),
        in_specs=dep_specs + [pl.BlockSpec(memory_space=pltpu.SMEM), blk, whole, whole, blk, blk,
                              pl.BlockSpec(bias.shape, lambda i: (0, 0, 0))],
        out_specs=[blk, whole, whole, pl.BlockSpec((N_Q_HEADS, 128), lambda i: (0, 0))],
        out_shape=[jax.ShapeDtypeStruct((t, ATTN_W), F32), jax.ShapeDtypeStruct((t, KV_W), F32),
                   jax.ShapeDtypeStruct((t, KV_W), F32), jax.ShapeDtypeStruct((N_Q_HEADS, 128), F32)],
        compiler_params=_params("arbitrary"),
    )(*deps, sinks, q, k, v, o, do, bias)


def _in_proj_bwd(dq, dk, dv, dbch, w, dh1, h, g, tabs, tm):
    t = h.shape[0]

    def body(dq_ref, dk_ref, dv_ref, dbch_ref, w_ref, dh1_ref, h_ref, g_ref, c_ref, sa_ref, sb_ref, dh_ref, dp_ref,
             dg_ref):
        @pl.when(pl.program_id(0) == 0)
        def _():
            dg_ref[...] = jnp.zeros_like(dg_ref)

        cos, sa, sb = c_ref[...], sa_ref[...], sb_ref[...]
        rep = ATTN_W // (2 * HEAD_DIM)
        dqr = _rope_bwd(dq_ref[...], jnp.tile(cos, (1, rep)), jnp.tile(sa, (1, rep)), jnp.tile(sb, (1, rep)))
        dkr = _rope_bwd(dk_ref[...], cos, sa, sb)
        dp = jnp.concatenate([dqr.astype(BF16), dkr.astype(BF16), dv_ref[...].astype(BF16), dbch_ref[...]], axis=1)
        dp_ref[...] = dp
        da = jnp.dot(dp, w_ref[...], preferred_element_type=F32)
        dx, dg = _rms_bwd(da, h_ref[...], g_ref[...])
        dh_ref[...] = dh1_ref[...] + dx
        dg_ref[...] += dg

    row = lambda n: pl.BlockSpec((tm, n), lambda i: (i, 0))
    full = lambda a: pl.BlockSpec(a.shape, lambda i: (0, 0))
    return pl.pallas_call(
        body, name="in_proj_bwd", grid=(t // tm,),
        in_specs=[row(ATTN_W), row(KV_W), row(KV_W), row(3 * CONV_W), full(w), row(D_MODEL), row(D_MODEL), full(g),
                  row(2 * HEAD_DIM), row(2 * HEAD_DIM), row(2 * HEAD_DIM)],
        out_specs=[row(D_MODEL), row(IN_W), pl.BlockSpec((1, D_MODEL), lambda i: (0, 0))],
        out_shape=[jax.ShapeDtypeStruct((t, D_MODEL), F32), jax.ShapeDtypeStruct((t, IN_W), BF16),
                   jax.ShapeDtypeStruct((1, D_MODEL), F32)],
        compiler_params=_params("arbitrary"),
    )(dq, dk, dv, dbch, w, dh1, h, g, *tabs)


class _Tiles:
    def __init__(self, t):
        self.tm = _row_tile(t, 640)
        self.ts = self.tm
        self.tabs = _rope_tables(t)
        self.bias = _attn_bias()


def _mixer_fwd(h, p, tl):
    a, q, k, v, b, c, hc = _in_proj(h, p["mix_pre_g"], p["w_in"], tl.tabs, tl.ts)
    o = _attn_fwd(q, k, v, tl.bias, p["sinks"], tl.tm)
    return (h, a, q, k, v, b, c, hc, o)


def _out_fwd(mixed, p, tl, deps=()):
    h, a, q, k, v, b, c, hc, o = mixed
    h1, y, z = _mix_out(h, o, b, c, hc, p["conv_w"], p["attn_out_g"], p["conv_out_g"], p["w_out"], p["mix_post_g"],
                        tl.ts, deps)
    return h1, mixed + (h1, y, z)


def _mlp_fwd(h1, saved, p, tl):
    h2, a2, act, f = _mlp(h1, p["mlp_pre_g"], p["w_up"], p["w_down"], p["mlp_post_g"], tl.tm)
    return h2, saved + (a2, act, f)


def _mlp_part_bwd(dh, saved, p, tl, deps=()):
    h1, a2, act, f = saved[9], saved[12], saved[13], saved[14]
    dh1, df, dup, dg2, dg1 = _mlp_bwd(dh, f, p["mlp_post_g"], act, p["w_down"], p["w_up"], h1, p["mlp_pre_g"], tl.tm,
                                      deps)
    g = {"w_down": [d.reshape(N_CHIPS, FF_CHUNK, D_MODEL) for d in _weight_grad(act, df, "grad_w_down")],
         "w_up": [d.reshape(N_CHIPS, D_MODEL, FF_CHUNK) for d in _weight_grad(a2, dup, "grad_w_up")],
         "mlp_post_g": dg2, "mlp_pre_g": dg1}
    return dh1, g


def _mix_out_part_bwd(dh1, saved, p, tl, deps=()):
    b, c, hc, o, y, z = saved[5], saved[6], saved[7], saved[8], saved[10], saved[11]
    dz, do, dbch, dgp, dga, dgc, dcw = _mix_out_bwd(dh1, z, p["mix_post_g"], p["w_out"], o, b, c, hc, p["conv_w"],
                                                    p["attn_out_g"], p["conv_out_g"], tl.ts, deps)
    g = {"w_out": [d.reshape(N_CHIPS, D_MODEL // N_CHIPS, D_MODEL) for d in _weight_grad(y, dz, "grad_w_out")],
         "mix_post_g": dgp, "attn_out_g": dga, "conv_out_g": dgc, "conv_w": dcw}
    return (dh1, do, dbch), g


def _attn_in_part_bwd(carry, saved, p, tl, deps=()):
    dh1, do, dbch = carry
    h_in, a, q, k, v, o = saved[0], saved[1], saved[2], saved[3], saved[4], saved[8]
    dq, dk, dv, dsink = _attn_bwd(q, k, v, o, do, tl.bias, p["sinks"], tl.tm, deps)
    dh, dproj, dgi = _in_proj_bwd(dq, dk, dv, dbch, p["w_in"], dh1, h_in, p["mix_pre_g"], tl.tabs, tl.ts)
    g_in = [d.reshape(N_CHIPS, IN_W // N_CHIPS, D_MODEL) for d in _weight_grad(dproj, a, "grad_w_in")]
    return dh, {"w_in": g_in, "mix_pre_g": dgi, "sinks": dsink[:, 0]}


def _place():
    return lax.axis_index("x"), lax.axis_index("y"), lax.axis_index("c")


def _other_chips(x, y):
    return [(1 - x, y), (x, 1 - y), (1 - x, 1 - y)]


_HBM = pl.BlockSpec(memory_space=pltpu.HBM)
_SEM = pl.BlockSpec(memory_space=pltpu.SEMAPHORE)
_EFFECT = pltpu.SideEffectType.DATAFLOW_SIDE_EFFECTING


class _Exchange:
    def __init__(self, name, bufs, plan, n, after=()):
        self.name, self.plan, nb = name, plan, len(bufs)
        n_in = nb + len(after)

        def body(*refs):
            send, recv, token = refs[n_in], refs[n_in + 1], refs[-1]
            for k, (src, dst, target, _) in enumerate(plan(refs[:nb])):
                pltpu.make_async_remote_copy(src_ref=src, dst_ref=dst, send_sem=send.at[k], recv_sem=recv.at[k],
                                             device_id=target, device_id_type=MESH).start()
            token[...] = jnp.zeros_like(token)

        outs = pl.pallas_call(
            body, name=name + "_start",
            out_shape=(pltpu.SemaphoreType.DMA((n,)), pltpu.SemaphoreType.DMA((n,)),
                       *[pltpu.HBM(b.shape, b.dtype) for b in bufs], jax.ShapeDtypeStruct((8, 128), F32)),
            in_specs=[_HBM] * nb + [pl.BlockSpec(memory_space=pl.ANY)] * len(after),
            out_specs=(_SEM, _SEM, *[_HBM] * nb, pl.BlockSpec(memory_space=pltpu.VMEM)),
            input_output_aliases={i: 2 + i for i in range(nb)},
            compiler_params=pltpu.CompilerParams(has_side_effects=_EFFECT),
        )(*[pltpu.with_memory_space_constraint(b, pltpu.HBM) for b in bufs], *after)
        self.send, self.recv, self.bufs, self.token = outs[0], outs[1], list(outs[2:2 + nb]), outs[-1]

    def wait(self, *after):
        plan, nb = self.plan, len(self.bufs)

        def body(*refs):
            send, recv = refs[nb], refs[nb + 1]
            for k, (src, _, target, land) in enumerate(plan(refs[:nb])):
                cp = pltpu.make_async_remote_copy(src_ref=src, dst_ref=land, send_sem=send.at[k], recv_sem=recv.at[k],
                                                  device_id=target, device_id_type=MESH)
                cp.wait_send()
                cp.wait_recv()

        outs = pl.pallas_call(
            body, name=self.name + "_wait", out_shape=[pltpu.HBM(b.shape, b.dtype) for b in self.bufs],
            in_specs=[_HBM] * nb + [_SEM, _SEM] + [pl.BlockSpec(memory_space=pl.ANY)] * len(after),
            out_specs=[_HBM] * nb, input_output_aliases={i: i for i in range(nb)},
            compiler_params=pltpu.CompilerParams(has_side_effects=_EFFECT),
        )(*self.bufs, self.send, self.recv, *after)
        return list(outs)


def _gather_plan(n):
    def plan(refs):
        x, y, c = _place()
        me = 2 * x + y
        return [(refs[a].at[me], refs[a].at[me], (px, py, c), refs[a].at[2 * px + py])
                for a in range(n) for px, py in _other_chips(x, y)]

    return plan


def _peers():
    x, y, c = _place()
    return [(k - 1, (x ^ (k >> 2), y ^ ((k >> 1) & 1), c ^ (k & 1))) for k in range(1, N_DEV)]


def _scatter_plan(n, half_rows):
    def plan(refs):
        out = []
        for a in range(n):
            hr = half_rows[a]
            for k, (px, py, pc) in _peers():
                out.append((refs[a].at[2 * px + py, pl.ds(pc * hr, hr)], refs[n + a].at[k], (px, py, pc),
                            refs[n + a].at[k]))
        return out

    return plan


def _join_plan(n):
    def plan(refs):
        x, y, c = _place()
        return [(refs[a].at[c], refs[a].at[c], (x, y, 1 - c), refs[a].at[1 - c]) for a in range(n)]

    return plan


def _sum_parts(g, q):
    rows, cols = g.shape[1], g.shape[2]
    hr = rows // 2
    tr = _block_rows(hr)
    per = hr // tr
    x, y, c = _place()
    where = jnp.stack([2 * x + y, c]).astype(jnp.int32)

    def body(where_ref, g_ref, q_ref, o_ref):
        total = g_ref[...]
        for k in range(N_DEV - 1):
            total = total + q_ref[k].astype(F32)
        o_ref[...] = total

    return pl.pallas_call(
        body, name="sum_parts",
        grid_spec=pltpu.PrefetchScalarGridSpec(
            num_scalar_prefetch=1, grid=(per,),
            in_specs=[pl.BlockSpec((None, tr, cols), lambda i, where_ref: (where_ref[0], where_ref[1] * per + i, 0)),
                      pl.BlockSpec((N_DEV - 1, tr, cols), lambda i, where_ref: (0, i, 0))],
            out_specs=pl.BlockSpec((None, tr, cols), lambda i, where_ref: (where_ref[1], i, 0))),
        out_shape=jax.ShapeDtypeStruct((2, hr, cols), F32),
        compiler_params=_params("parallel"),
    )(where, g, q)


def _sum_devices(packed):
    def body(p_ref, o_ref, land, send_sems, recv_sems):
        x, y, c = _place()
        me = 4 * x + 2 * y + c
        land[me] = p_ref[...]
        sends = []
        for k in range(1, N_DEV):
            px, py, pc = x ^ (k >> 2), y ^ ((k >> 1) & 1), c ^ (k & 1)
            cp = pltpu.make_async_remote_copy(src_ref=p_ref, dst_ref=land.at[me], send_sem=send_sems.at[k - 1],
                                              recv_sem=recv_sems.at[k - 1], device_id=(px, py, pc), device_id_type=MESH)
            cp.start()
            sends.append(cp)
        for k in range(1, N_DEV):
            px, py, pc = x ^ (k >> 2), y ^ ((k >> 1) & 1), c ^ (k & 1)
            pltpu.make_async_remote_copy(src_ref=p_ref, dst_ref=land.at[4 * px + 2 * py + pc],
                                         send_sem=send_sems.at[k - 1], recv_sem=recv_sems.at[k - 1],
                                         device_id=(px, py, pc), device_id_type=MESH).wait_recv()
        for cp in sends:
            cp.wait_send()
        total = land[0]
        for d in range(1, N_DEV):
            total = total + land[d]
        o_ref[...] = total

    vm = pl.BlockSpec(memory_space=pltpu.VMEM)
    return pl.pallas_call(
        body, name="sum_devices", in_specs=[vm], out_specs=vm,
        out_shape=jax.ShapeDtypeStruct(packed.shape, F32),
        scratch_shapes=[pltpu.VMEM((N_DEV,) + packed.shape, F32), pltpu.SemaphoreType.DMA((N_DEV - 1,)),
                        pltpu.SemaphoreType.DMA((N_DEV - 1,))],
    )(packed)


def _adamw_math(w, g, m, v):
    m = ADAM_B1 * m + (1.0 - ADAM_B1) * g
    v = ADAM_B2 * v + (1.0 - ADAM_B2) * jnp.square(g)
    m_hat = m / (1.0 - ADAM_B1 ** ADAM_STEP)
    v_hat = v / (1.0 - ADAM_B2 ** ADAM_STEP)
    delta = -ADAM_LR * (m_hat / (jnp.sqrt(v_hat) + ADAM_EPS) + ADAM_WD * w)
    return delta, m, v


def _adamw_large(layer, w, halves, m, v, other):
    _, rows, cols = w.shape
    tr = _block_rows(rows // 2)
    per = rows // 2 // tr

    def body(w_ref, g_ref, m_ref, v_ref, *rest):
        g_out, d_ref, nm_ref, nv_ref = rest[-4:]
        g = g_ref[...]
        g_out[...] = g
        d_ref[...], nm_ref[...], nv_ref[...] = _adamw_math(w_ref[...], g, m_ref[...], v_ref[...])

    blk = pl.BlockSpec((None, tr, cols), lambda i: (layer, i, 0))
    half = pl.BlockSpec((None, tr, cols), lambda i: (i // per, i % per, 0))
    kept = [] if other is None else list(other)
    return pl.pallas_call(
        body, name="adamw_large", grid=(rows // tr,),
        in_specs=[blk, half, blk, blk] + [pl.BlockSpec(memory_space=pl.ANY)] * len(kept), out_specs=[blk] * 4,
        out_shape=[jax.ShapeDtypeStruct(w.shape, F32)] * 4,
        input_output_aliases={4 + k: k for k in range(len(kept))},
        compiler_params=_params("parallel"),
    )(w, halves, m, v, *kept)


def _adamw_small(ws, gs, ms, vs):
    n = len(ws)

    def body(*refs):
        w_r, g_r, m_r, v_r = refs[:n], refs[n:2 * n], refs[2 * n:3 * n], refs[3 * n:4 * n]
        d_r, nm_r, nv_r = refs[4 * n:5 * n], refs[5 * n:6 * n], refs[6 * n:]
        for a in range(n):
            d_r[a][...], nm_r[a][...], nv_r[a][...] = _adamw_math(w_r[a][...], g_r[a][...], m_r[a][...], v_r[a][...])

    vm = pl.BlockSpec(memory_space=pltpu.VMEM)
    outs = pl.pallas_call(
        body, name="adamw_small", in_specs=[vm] * (4 * n), out_specs=[vm] * (3 * n),
        out_shape=[jax.ShapeDtypeStruct(w.shape, F32) for w in ws] * 3,
    )(*ws, *gs, *ms, *vs)
    return outs[:n], outs[n:2 * n], outs[2 * n:]


_LARGE = ("w_in", "w_out", "w_up", "w_down")
_SMALL = ("meta_tokens", "mix_pre_g", "conv_w", "sinks", "attn_out_g", "conv_out_g", "mix_post_g", "mlp_pre_g",
          "mlp_post_g")
_ORDER = ("meta_tokens", "mix_pre_g", "w_in", "conv_w", "sinks", "attn_out_g", "conv_out_g", "w_out", "mix_post_g",
          "mlp_pre_g", "w_up", "w_down", "mlp_post_g")


class _Reduce:
    def __init__(self, name, grads, after=()):
        self.name, self.n = name, len(grads)
        self.own = [g for g, _ in grads]
        half_rows = [g.shape[1] // 2 for g in self.own]
        zones = [lax.empty((N_DEV - 1, hr, g.shape[2]), BF16) for g, hr in zip(self.own, half_rows)]
        self.exchange = _Exchange(name + "_scatter", [b for _, b in grads] + zones, _scatter_plan(self.n, half_rows),
                                  (N_DEV - 1) * self.n, after)

    @property
    def token(self):
        return self.exchange.token

    def join(self, *after):
        bufs = self.exchange.wait(*after)
        halves = [_sum_parts(g, q) for g, q in zip(self.own, bufs[self.n:])]
        self.exchange = _Exchange(self.name + "_join", halves, _join_plan(self.n), self.n)

    def done(self, *after):
        return self.exchange.wait(*after)


def _pad_cols(a, n=D_MODEL):
    return jnp.pad(a, ((0, 0), (0, n - a.shape[1])))


def kernel(x, meta_tokens, mix_pre_g, w_in, conv_w, sinks, attn_out_g, conv_out_g, w_out, mix_post_g, mlp_pre_g, w_up, w_down, mlp_post_g, loss_target, m_meta_tokens, m_mix_pre_g, m_w_in, m_conv_w, m_sinks, m_attn_out_g, m_conv_out_g, m_w_out, m_mix_post_g, m_mlp_pre_g, m_w_up, m_w_down, m_mlp_post_g, v_meta_tokens, v_mix_pre_g, v_w_in, v_conv_w, v_sinks, v_attn_out_g, v_conv_out_g, v_w_out, v_mix_post_g, v_mlp_pre_g, v_w_up, v_w_down, v_mlp_post_g):
    w = dict(meta_tokens=meta_tokens, mix_pre_g=mix_pre_g, w_in=w_in, conv_w=conv_w, sinks=sinks,
             attn_out_g=attn_out_g, conv_out_g=conv_out_g, w_out=w_out, mix_post_g=mix_post_g, mlp_pre_g=mlp_pre_g,
             w_up=w_up, w_down=w_down, mlp_post_g=mlp_post_g)
    m = dict(meta_tokens=m_meta_tokens, mix_pre_g=m_mix_pre_g, w_in=m_w_in, conv_w=m_conv_w, sinks=m_sinks,
             attn_out_g=m_attn_out_g, conv_out_g=m_conv_out_g, w_out=m_w_out, mix_post_g=m_mix_post_g,
             mlp_pre_g=m_mlp_pre_g, w_up=m_w_up, w_down=m_w_down, mlp_post_g=m_mlp_post_g)
    v = dict(meta_tokens=v_meta_tokens, mix_pre_g=v_mix_pre_g, w_in=v_w_in, conv_w=v_conv_w, sinks=v_sinks,
             attn_out_g=v_attn_out_g, conv_out_g=v_conv_out_g, w_out=v_w_out, mix_post_g=v_mix_post_g,
             mlp_pre_g=v_mlp_pre_g, w_up=v_w_up, w_down=v_w_down, mlp_post_g=v_mlp_post_g)
    chip = 2 * lax.axis_index("x") + lax.axis_index("y")
    tl = _Tiles(x.shape[1] + BLOCK)

    def zone(quarter):
        return lax.dynamic_update_slice(lax.empty((N_CHIPS,) + quarter.shape, quarter.dtype), quarter[None],
                                        (chip,) + (0,) * quarter.ndim)

    w, m, v = ({**d, "w_in": jnp.swapaxes(d["w_in"], 1, 2)} for d in (w, m, v))
    zones = {n: [zone(w[n][l].astype(BF16)) for l in range(DEPTH)] for n in _LARGE}
    first = _Exchange("gather_first", [zones["w_in"][0], zone(w["conv_w"]), zone(w["meta_tokens"])], _gather_plan(3), 9)
    out0 = _Exchange("gather_out", [zones["w_out"][0]], _gather_plan(1), 3, [first.token])
    rest = _Exchange("gather_rest", [zones[n][0] for n in ("w_up", "w_down")], _gather_plan(2), 6, [out0.token])

    def whole_in(quarters):
        return quarters.reshape(IN_W, D_MODEL)

    h = jnp.concatenate([jnp.zeros((BLOCK, D_MODEL), F32), x[0]], axis=0)
    q_in, q_conv, q_meta = first.wait(rest.token, *tl.tabs, tl.bias, h)
    conv_whole = jnp.transpose(q_conv, (1, 2, 0, 3)).reshape(DEPTH, CONV_K, CONV_W)
    meta = jnp.transpose(q_meta, (1, 0, 2)).reshape(N_META, D_MODEL)
    p = [{"conv_w": conv_whole[l], "sinks": w["sinks"][l]} for l in range(DEPTH)]
    for l in range(DEPTH):
        for n in ("mix_pre_g", "attn_out_g", "conv_out_g", "mix_post_g", "mlp_pre_g", "mlp_post_g"):
            p[l][n] = w[n][l][None, :]

    h = lax.dynamic_update_slice(h, meta, (LEAD_PAD, 0))
    p[0]["w_in"] = whole_in(q_in)
    mixed = _mixer_fwd(h, p[0], tl)
    second = _Exchange("gather_second", [zones["w_in"][1], zones["w_out"][1]], _gather_plan(2), 6, [mixed[-1]])
    second_mlp = _Exchange("gather_second_mlp", [zones["w_up"][1], zones["w_down"][1]], _gather_plan(2), 6,
                           [second.token])
    p[0]["w_out"], = out0.wait(second_mlp.token)
    h1, saved0 = _out_fwd(mixed, p[0], tl)
    p[0]["w_up"], p[0]["w_down"] = rest.wait(h1)
    h, saved0 = _mlp_fwd(h1, saved0, p[0], tl)
    q_in, p[1]["w_out"] = second.wait(h)
    p[1]["w_in"] = whole_in(q_in)
    h1, saved1 = _out_fwd(_mixer_fwd(h, p[1], tl), p[1], tl)
    p[1]["w_up"], p[1]["w_down"] = second_mlp.wait(h1)
    h, saved1 = _mlp_fwd(h1, saved1, p[1], tl)
    loss_tile, dh = _loss_head(h, loss_target[0], tl.tm)

    def adamw(layer, halves, other):
        return {n: _adamw_large(layer, w[n], halves[n], m[n], v[n], None if other is None else other[n])
                for n in halves}

    dh1, g1 = _mlp_part_bwd(dh, saved1, p[1], tl)
    carry, gm = _mix_out_part_bwd(dh1, saved1, p[1], tl)
    dh, gi = _attn_in_part_bwd(carry, saved1, p[1], tl)
    g1.update(gm, **gi)
    red1 = _Reduce("reduce1", [g1[n] for n in _LARGE])
    dh1, g0 = _mlp_part_bwd(dh, saved0, p[0], tl, [red1.token])
    red1.join(g0["w_down"][0])
    carry, gm = _mix_out_part_bwd(dh1, saved0, p[0], tl, [red1.token])
    first0 = ("w_up", "w_down", "w_out")
    g0.update(gm)
    red0a = _Reduce("reduce0a", [g0[n] for n in first0])
    dh0, gi = _attn_in_part_bwd(carry, saved0, p[0], tl, [red0a.token])
    g0.update(gi)
    red0b = _Reduce("reduce0b", [g0["w_in"]])
    grad_x = dh0[BLOCK:][None]
    grads = {n: [g0[n], g1[n]] for n in g0 if n not in _LARGE}

    rows = [dh0[LEAD_PAD:BLOCK]]
    for n in ("mix_pre_g", "mix_post_g", "mlp_pre_g", "mlp_post_g"):
        rows += grads[n]
    rows += [jnp.concatenate([grads["attn_out_g"][l], grads["conv_out_g"][l]], axis=1) for l in range(DEPTH)]
    rows.append(jnp.concatenate(grads["conv_w"], axis=1))
    rows.append(_pad_cols(jnp.concatenate(grads["sinks"])[None, :]))
    rows.append(_pad_cols(loss_tile[:1]))
    packed = jnp.concatenate(rows, axis=0)
    packed = jnp.pad(packed, ((0, SMALL_ROWS - packed.shape[0]), (0, 0)))
    total = _sum_devices(packed)
    r0 = N_META
    small = {
        "meta_tokens": lax.dynamic_slice(total[:N_META], (0, chip * (D_MODEL // N_CHIPS)), (N_META, D_MODEL // N_CHIPS)),
        "mix_pre_g": total[r0:r0 + 2], "mix_post_g": total[r0 + 2:r0 + 4], "mlp_pre_g": total[r0 + 4:r0 + 6],
        "mlp_post_g": total[r0 + 6:r0 + 8],
        "attn_out_g": total[r0 + 8:r0 + 10, :ATTN_W], "conv_out_g": total[r0 + 8:r0 + 10, ATTN_W:],
        "conv_w": lax.dynamic_slice(total[r0 + 10:r0 + 13].reshape(CONV_K, DEPTH, CONV_W).transpose(1, 0, 2),
                                    (0, 0, chip * (CONV_W // N_CHIPS)), (DEPTH, CONV_K, CONV_W // N_CHIPS)),
        "sinks": total[r0 + 13, :DEPTH * N_Q_HEADS].reshape(DEPTH, N_Q_HEADS),
    }
    loss = total[r0 + 14, 0]

    ds, nms, nvs = _adamw_small([w[n] for n in _SMALL], [small[n] for n in _SMALL], [m[n] for n in _SMALL],
                                [v[n] for n in _SMALL])
    done1 = adamw(1, dict(zip(_LARGE, red1.done(red0b.token))), None)
    red0a.join(ds[0], grad_x, *[done1[n][0] for n in _LARGE])
    red0b.join(red0a.token)
    done0 = adamw(0, dict(zip(first0, red0a.done(red0b.token))), done1)
    done0.update(adamw(0, {"w_in": red0b.done(done0["w_down"][0])[0]}, done1))
    grad, delta, new_m, new_v = {}, {}, {}, {}
    for n in _LARGE:
        grad[n], delta[n], new_m[n], new_v[n] = done0[n]
    for d in (grad, delta, new_m, new_v):
        d["w_in"] = jnp.swapaxes(d["w_in"], 1, 2)
    for i, n in enumerate(_SMALL):
        grad[n], delta[n], new_m[n], new_v[n] = small[n], ds[i], nms[i], nvs[i]
    return (loss, grad_x, *[grad[n] for n in _ORDER], *[delta[n] for n in _ORDER], *[new_m[n] for n in _ORDER],
            *[new_v[n] for n in _ORDER])
```

```python
import functools

import jax
import jax.numpy as jnp
from jax import lax
from jax.experimental import pallas as pl
from jax.experimental.pallas import tpu as pltpu

F32 = jnp.float32
BF16 = jnp.bfloat16

D_MODEL = 1024
DEPTH = 2
N_META = 16
ATTN_W = 512
CONV_W = 512
HEAD_DIM = 64
N_Q_HEADS = 8
N_KV_HEADS = 2
GROUP = N_Q_HEADS // N_KV_HEADS
KV_W = N_KV_HEADS * HEAD_DIM
CONV_K = 3
BLOCK = 128
LEAD_PAD = BLOCK - N_META
ROPE_THETA = 500000.0
ROT_DIM = HEAD_DIM // 4
ROT_HALF = ROT_DIM // 2
D_FF = 4 * D_MODEL
IN_W = ATTN_W + 2 * KV_W + 3 * CONV_W
QKV_W = ATTN_W + 2 * KV_W
EPS = 1e-6
SCALE = HEAD_DIM ** -0.5
FF_CHUNK = 1024
N_CHIPS = 4
N_DEV = 8

ADAM_LR = 0.001
ADAM_B1 = 0.9
ADAM_B2 = 0.999
ADAM_EPS = 1e-08
ADAM_WD = 0.01
ADAM_STEP = 10

V7X_VMEM_LIMIT = 56 * 1024 * 1024
SMALL_ROWS = 32

MESH = pl.DeviceIdType.MESH


def _params(*sem):
    return pltpu.CompilerParams(dimension_semantics=sem, vmem_limit_bytes=V7X_VMEM_LIMIT)


def _block_rows(n):
    return max(r for r in range(16, min(n, 256) + 1, 16) if n % r == 0)


def _row_tile(t, most):
    nb = t // BLOCK
    for b in range(most // BLOCK, 0, -1):
        if nb % b == 0:
            return b * BLOCK
    return BLOCK


def _behind(body, deps):
    n = len(deps)

    def wrapped(*refs):
        body(*refs[n:])

    return wrapped, [pl.BlockSpec(memory_space=pl.ANY)] * n


def _rms(x, g):
    r = lax.rsqrt(jnp.mean(x * x, axis=-1, keepdims=True) + EPS)
    return x * r * g


def _rms_bwd(dy, x, g):
    r = lax.rsqrt(jnp.mean(x * x, axis=-1, keepdims=True) + EPS)
    xh = x * r
    dg = jnp.sum(dy * xh, axis=0, keepdims=True)
    dxh = dy * g
    dx = r * (dxh - xh * jnp.mean(dxh * xh, axis=-1, keepdims=True))
    return dx, dg


def _rope(x, cos, sa, sb):
    n = x.shape[-1]
    return x * cos + pltpu.roll(x, n - ROT_HALF, 1) * sa + pltpu.roll(x, ROT_HALF, 1) * sb


def _rope_bwd(dy, cos, sa, sb):
    n = dy.shape[-1]
    return dy * cos + pltpu.roll(dy * sa, ROT_HALF, 1) + pltpu.roll(dy * sb, n - ROT_HALF, 1)


def _rope_tables(t):
    pos = lax.broadcasted_iota(jnp.int32, (t, ROT_HALF), 0).astype(F32) - LEAD_PAD
    pair = lax.broadcasted_iota(jnp.int32, (t, ROT_HALF), 1).astype(F32)
    inv_freq = jnp.power(jnp.float32(ROPE_THETA), -(2.0 * pair) / ROT_DIM)
    ang = pos * inv_freq
    cos, sin = lax.optimization_barrier((jnp.cos(ang), jnp.sin(ang)))
    spread = (1, 2 * HEAD_DIM // ROT_HALF)
    cos, sin = jnp.tile(cos, spread), jnp.tile(sin, spread)
    dim = lax.broadcasted_iota(jnp.int32, (t, 2 * HEAD_DIM), 1) % HEAD_DIM
    return (jnp.where(dim < ROT_DIM, cos, 1.0), jnp.where(dim < ROT_HALF, -sin, 0.0),
            jnp.where((dim >= ROT_HALF) & (dim < ROT_DIM), sin, 0.0))


def _in_proj(h, g, w, tabs, tm):
    t = h.shape[0]

    def body(h_ref, g_ref, w_ref, c_ref, sa_ref, sb_ref, a_ref, q_ref, k_ref, v_ref, b_ref, cg_ref, hc_ref):
        a = _rms(h_ref[...], g_ref[...]).astype(BF16)
        a_ref[...] = a
        p = lax.dot_general(a, w_ref[...], (((1,), (1,)), ((), ())), preferred_element_type=F32)
        cos, sa, sb = c_ref[...], sa_ref[...], sb_ref[...]
        rep = ATTN_W // (2 * HEAD_DIM)
        q = _rope(p[:, :ATTN_W], jnp.tile(cos, (1, rep)), jnp.tile(sa, (1, rep)), jnp.tile(sb, (1, rep)))
        q_ref[...] = (q * SCALE).astype(BF16)
        k_ref[...] = _rope(p[:, ATTN_W:ATTN_W + KV_W], cos, sa, sb).astype(BF16)
        v_ref[...] = p[:, ATTN_W + KV_W:QKV_W].astype(BF16)
        b_ref[...] = p[:, QKV_W:QKV_W + CONV_W]
        cg_ref[...] = p[:, QKV_W + CONV_W:QKV_W + 2 * CONV_W]
        hc_ref[...] = p[:, QKV_W + 2 * CONV_W:]

    row = lambda n: pl.BlockSpec((tm, n), lambda i: (i, 0))
    full = lambda a: pl.BlockSpec(a.shape, lambda i: (0, 0))
    return pl.pallas_call(
        body, name="in_proj", grid=(t // tm,),
        in_specs=[row(D_MODEL), full(g), full(w), row(2 * HEAD_DIM), row(2 * HEAD_DIM), row(2 * HEAD_DIM)],
        out_specs=[row(D_MODEL), row(ATTN_W), row(KV_W), row(KV_W), row(CONV_W), row(CONV_W), row(CONV_W)],
        out_shape=[jax.ShapeDtypeStruct((t, D_MODEL), BF16), jax.ShapeDtypeStruct((t, ATTN_W), BF16),
                   jax.ShapeDtypeStruct((t, KV_W), BF16), jax.ShapeDtypeStruct((t, KV_W), BF16),
                   jax.ShapeDtypeStruct((t, CONV_W), F32), jax.ShapeDtypeStruct((t, CONV_W), F32),
                   jax.ShapeDtypeStruct((t, CONV_W), F32)],
        compiler_params=_params("parallel"),
    )(h, g, w, *tabs)


def _attn_bias():
    r = lax.broadcasted_iota(jnp.int32, (3, BLOCK, 2 * BLOCK), 1)
    c = lax.broadcasted_iota(jnp.int32, (3, BLOCK, 2 * BLOCK), 2)
    i = lax.broadcasted_iota(jnp.int32, (3, BLOCK, 2 * BLOCK), 0)
    ok = (c > r) & (c <= r + BLOCK) & (c + (i - 1) * BLOCK >= LEAD_PAD)
    return jnp.where(ok, 0.0, -jnp.inf).astype(F32)


def _attn_scores(qh, kg, bias):
    return lax.dot_general(qh, kg, (((1,), (1,)), ((), ())), preferred_element_type=F32) + bias


def _attn_probs(s, sk):
    m = jnp.maximum(jnp.max(s, axis=-1, keepdims=True), sk)
    e = jnp.exp(s - m)
    es = jnp.exp(sk - m)
    rden = 1.0 / (jnp.sum(e, axis=-1, keepdims=True) + es)
    return e * rden, es * rden


def _head(hh):
    return slice(hh * HEAD_DIM, (hh + 1) * HEAD_DIM)


def _two_blocks(ref, i):
    prev = jnp.maximum(i - 1, 0)
    return jnp.concatenate([ref[pl.ds(pl.multiple_of(prev * BLOCK, BLOCK), BLOCK), :],
                            ref[pl.ds(pl.multiple_of(i * BLOCK, BLOCK), BLOCK), :]], axis=0)


def _attn_fwd(q, k, v, bias, sinks, tm):
    t = q.shape[0]
    per_step = tm // BLOCK
    heads = range(N_Q_HEADS)

    def body(s_ref, q_ref, k_ref, v_ref, bias_ref, o_ref):
        for b in range(per_step):
            i = pl.program_id(0) * per_step + b
            rows = slice(b * BLOCK, (b + 1) * BLOCK)
            kc, vc = _two_blocks(k_ref, i), _two_blocks(v_ref, i)
            bias_i = bias_ref[jnp.minimum(i, 2)]
            scores = [_attn_scores(q_ref[rows, _head(hh)], kc[:, _head(hh // GROUP)], bias_i) for hh in heads]
            probs = [_attn_probs(scores[hh], s_ref[hh])[0].astype(BF16) for hh in heads]
            for hh in heads:
                o_ref[rows, _head(hh)] = jnp.dot(probs[hh], vc[:, _head(hh // GROUP)], preferred_element_type=F32)

    whole = pl.BlockSpec((t, KV_W), lambda i: (0, 0))
    return pl.pallas_call(
        body, name="attn_fwd", grid=(t // tm,),
        in_specs=[pl.BlockSpec(memory_space=pltpu.SMEM), pl.BlockSpec((tm, ATTN_W), lambda i: (i, 0)), whole, whole,
                  pl.BlockSpec(bias.shape, lambda i: (0, 0, 0))],
        out_specs=pl.BlockSpec((tm, ATTN_W), lambda i: (i, 0)),
        out_shape=jax.ShapeDtypeStruct((t, ATTN_W), F32),
        compiler_params=_params("parallel"),
    )(sinks, q, k, v, bias)


def _shift_rows(u, halo, n):
    r = pltpu.roll(u, n, 0)
    hr = pltpu.roll(halo, n, 0)
    idx = lax.broadcasted_iota(jnp.int32, hr.shape, 0)
    return jnp.concatenate([jnp.where(idx < n, hr, r[:8]), r[8:]], axis=0)


def _advance_rows(u, halo, n):
    rows = u.shape[0]
    r = pltpu.roll(u, rows - n, 0)
    hr = pltpu.roll(halo, 8 - n, 0)
    idx = lax.broadcasted_iota(jnp.int32, hr.shape, 0)
    return jnp.concatenate([r[:rows - 8], jnp.where(idx >= 8 - n, hr, r[rows - 8:])], axis=0)


def _mix_out(h, o, b, c, hc, cw, ga, gc, w, gp, tm, deps=()):
    t = h.shape[0]

    def body(h_ref, o_ref, b_ref, c_ref, hc_ref, cw_ref, ga_ref, gc_ref, w_ref, gp_ref, h1_ref, y_ref, z_ref, halo):
        @pl.when(pl.program_id(0) == 0)
        def _():
            halo[...] = jnp.zeros_like(halo)

        u = c_ref[...] * hc_ref[...]
        cv = cw_ref[0:1, :] * _shift_rows(u, halo[...], 2) + cw_ref[1:2, :] * _shift_rows(u, halo[...], 1) \
            + cw_ref[2:3, :] * u
        halo[...] = u[tm - 8:]
        yc = b_ref[...] * cv
        y = jnp.concatenate([_rms(o_ref[...], ga_ref[...]), _rms(yc, gc_ref[...])], axis=1).astype(BF16)
        y_ref[...] = y
        z = jnp.dot(y, w_ref[...].reshape(D_MODEL, D_MODEL), preferred_element_type=F32)
        z_ref[...] = z
        h1_ref[...] = h_ref[...] + _rms(z, gp_ref[...])

    row = lambda n: pl.BlockSpec((tm, n), lambda i: (i, 0))
    full = lambda a: pl.BlockSpec(a.shape, lambda i: (0,) * a.ndim)
    body, dep_specs = _behind(body, deps)
    return pl.pallas_call(
        body, name="mix_out", grid=(t // tm,),
        in_specs=dep_specs + [row(D_MODEL), row(ATTN_W), row(CONV_W), row(CONV_W), row(CONV_W), full(cw), full(ga),
                              full(gc), full(w), full(gp)],
        out_specs=[row(D_MODEL), row(D_MODEL), row(D_MODEL)],
        out_shape=[jax.ShapeDtypeStruct((t, D_MODEL), F32), jax.ShapeDtypeStruct((t, D_MODEL), BF16),
                   jax.ShapeDtypeStruct((t, D_MODEL), F32)],
        scratch_shapes=[pltpu.VMEM((8, CONV_W), F32)],
        compiler_params=_params("arbitrary"),
    )(*deps, h, o, b, c, hc, cw, ga, gc, w, gp)


def _mlp(h1, g1, wu, wd, g2, tm):
    t = h1.shape[0]
    nj = D_FF // FF_CHUNK

    def body(h1_ref, g1_ref, wu_ref, wd_ref, g2_ref, h2_ref, a2_ref, act_ref, f_ref):
        a2 = _rms(h1_ref[...], g1_ref[...]).astype(BF16)
        a2_ref[...] = a2
        f = None
        for j in range(nj):
            up = jnp.dot(a2, wu_ref[j], preferred_element_type=F32)
            act = jnp.square(jnp.maximum(up, 0.0)).astype(BF16)
            act_ref[:, j * FF_CHUNK:(j + 1) * FF_CHUNK] = act
            part = jnp.dot(act, wd_ref[j], preferred_element_type=F32)
            f = part if f is None else f + part
        f_ref[...] = f
        h2_ref[...] = h1_ref[...] + _rms(f, g2_ref[...])

    row = pl.BlockSpec((tm, D_MODEL), lambda i: (i, 0))
    vec = pl.BlockSpec((1, D_MODEL), lambda i: (0, 0))
    resident = pl.BlockSpec(memory_space=pltpu.VMEM)
    return pl.pallas_call(
        body, name="mlp", grid=(t // tm,),
        in_specs=[row, vec, resident, resident, vec],
        out_specs=[row, row, pl.BlockSpec((tm, D_FF), lambda i: (i, 0)), row],
        out_shape=[jax.ShapeDtypeStruct((t, D_MODEL), F32), jax.ShapeDtypeStruct((t, D_MODEL), BF16),
                   jax.ShapeDtypeStruct((t, D_FF), BF16), jax.ShapeDtypeStruct((t, D_MODEL), F32)],
        compiler_params=_params("parallel"),
    )(h1, g1, wu, wd, g2)


def _loss_head(h, target, tm):
    t = h.shape[0]
    per_step = tm // BLOCK

    def body(h_ref, *rest):
        t_refs, (loss_ref, dh_ref) = rest[:per_step], rest[per_step:]
        i = pl.program_id(0)

        @pl.when(i == 0)
        def _():
            loss_ref[...] = jnp.zeros_like(loss_ref)

        total = jnp.zeros((), F32)
        for b in range(per_step):
            rows = slice(b * BLOCK, (b + 1) * BLOCK)
            err = h_ref[rows, :] - t_refs[b][...]
            if b == 0:
                err = jnp.where(i == 0, 0.0, err)
            dh_ref[rows, :] = err * (1.0 / D_MODEL)
            total = total + jnp.sum(err * err)
        loss_ref[...] += total * (0.5 / D_MODEL)

    def target_block(b):
        return pl.BlockSpec((BLOCK, D_MODEL), lambda i: (jnp.maximum(i * per_step + b - 1, 0), 0))

    return pl.pallas_call(
        body, name="loss_head", grid=(t // tm,),
        in_specs=[pl.BlockSpec((tm, D_MODEL), lambda i: (i, 0))] + [target_block(b) for b in range(per_step)],
        out_specs=[pl.BlockSpec((8, 128), lambda i: (0, 0)), pl.BlockSpec((tm, D_MODEL), lambda i: (i, 0))],
        out_shape=[jax.ShapeDtypeStruct((8, 128), F32), jax.ShapeDtypeStruct((t, D_MODEL), F32)],
        compiler_params=_params("arbitrary"),
    )(h, *([target] * per_step))


def _mlp_bwd_hidden(dh2, f, g2, act, wd, tm, deps=()):
    t = dh2.shape[0]
    nj = D_FF // FF_CHUNK

    def body(dh2_ref, f_ref, g2_ref, act_ref, wd_ref, df_ref, dup_ref, dg2_ref):
        @pl.when(pl.program_id(0) == 0)
        def _():
            dg2_ref[...] = jnp.zeros_like(dg2_ref)

        df, dg = _rms_bwd(dh2_ref[...], f_ref[...], g2_ref[...])
        dg2_ref[...] += dg
        df = df.astype(BF16)
        df_ref[...] = df
        for j in range(nj):
            cols = slice(j * FF_CHUNK, (j + 1) * FF_CHUNK)
            dact = lax.dot_general(df, wd_ref[j], (((1,), (1,)), ((), ())), preferred_element_type=F32)
            dup_ref[:, cols] = (dact * (2.0 * jnp.sqrt(act_ref[:, cols].astype(F32)))).astype(BF16)

    row = pl.BlockSpec((tm, D_MODEL), lambda i: (i, 0))
    wide = pl.BlockSpec((tm, D_FF), lambda i: (i, 0))
    vec = pl.BlockSpec((1, D_MODEL), lambda i: (0, 0))
    body, dep_specs = _behind(body, deps)
    return pl.pallas_call(
        body, name="mlp_bwd_hidden", grid=(t // tm,),
        in_specs=dep_specs + [row, row, vec, wide, pl.BlockSpec(memory_space=pltpu.VMEM)],
        out_specs=[row, wide, vec],
        out_shape=[jax.ShapeDtypeStruct((t, D_MODEL), BF16), jax.ShapeDtypeStruct((t, D_FF), BF16),
                   jax.ShapeDtypeStruct((1, D_MODEL), F32)],
        compiler_params=_params("arbitrary"),
    )(*deps, dh2, f, g2, act, wd)


def _mlp_bwd_input(dup, wu, h1, g1, dh2, tm):
    t = dh2.shape[0]
    nj = D_FF // FF_CHUNK

    def body(dup_ref, wu_ref, h1_ref, g1_ref, dh2_ref, dh1_ref, dg1_ref):
        @pl.when(pl.program_id(0) == 0)
        def _():
            dg1_ref[...] = jnp.zeros_like(dg1_ref)

        da2 = None
        for j in range(nj):
            part = lax.dot_general(dup_ref[:, j * FF_CHUNK:(j + 1) * FF_CHUNK], wu_ref[j], (((1,), (1,)), ((), ())),
                                   preferred_element_type=F32)
            da2 = part if da2 is None else da2 + part
        dx, dg = _rms_bwd(da2, h1_ref[...], g1_ref[...])
        dh1_ref[...] = dh2_ref[...] + dx
        dg1_ref[...] += dg

    row = pl.BlockSpec((tm, D_MODEL), lambda i: (i, 0))
    vec = pl.BlockSpec((1, D_MODEL), lambda i: (0, 0))
    return pl.pallas_call(
        body, name="mlp_bwd_input", grid=(t // tm,),
        in_specs=[pl.BlockSpec((tm, D_FF), lambda i: (i, 0)), pl.BlockSpec(memory_space=pltpu.VMEM), row, vec, row],
        out_specs=[row, vec],
        out_shape=[jax.ShapeDtypeStruct((t, D_MODEL), F32), jax.ShapeDtypeStruct((1, D_MODEL), F32)],
        compiler_params=_params("arbitrary"),
    )(dup, wu, h1, g1, dh2)


def _row_split(t):
    tile = min(t, 1024)
    return tile, t // tile, t % tile


def _row_split_specs(t, cols, col_of):
    tile, whole, rest = _row_split(t)
    specs = [pl.BlockSpec((tile, cols), lambda *g: (jnp.minimum(g[-1], whole - 1), col_of(*g[:-1])))]
    if rest:
        specs.append(pl.BlockSpec((rest, cols), lambda *g: (whole * tile // rest, col_of(*g[:-1]))))
    return specs


def _weight_grad(x, y, name):
    t, k = x.shape
    n = y.shape[1]
    tn = FF_CHUNK
    tk = FF_CHUNK if k % FF_CHUNK == 0 else k
    _, whole, rest = _row_split(t)
    steps = whole + bool(rest)
    one_tile = k == tk and n == tn

    def body(*refs):
        o_ref, ob_ref, r = refs[-2], refs[-1], pl.program_id(2)
        if one_tile:
            o_ref, ob_ref = o_ref.at[0, 0], ob_ref.at[0, 0]

        @pl.when(r == 0)
        def _():
            o_ref[...] = jnp.zeros_like(o_ref)

        def add(x_ref, y_ref):
            o_ref[...] += lax.dot_general(x_ref[...], y_ref[...], (((0,), (0,)), ((), ())),
                                          preferred_element_type=F32)

        if rest:
            pl.when(r < whole)(lambda: add(refs[0], refs[2]))
            pl.when(r == whole)(lambda: add(refs[1], refs[3]))
        else:
            add(refs[0], refs[1])

        @pl.when(r == steps - 1)
        def _():
            ob_ref[...] = o_ref[...].astype(BF16)

    tile = pl.BlockSpec((None, None, tk, tn), lambda a, b, r: (a, b, 0, 0))
    if one_tile:
        tile = pl.BlockSpec(memory_space=pltpu.VMEM)
    return pl.pallas_call(
        body, name=name, grid=(k // tk, n // tn, steps),
        in_specs=_row_split_specs(t, tk, lambda a, b: a) + _row_split_specs(t, tn, lambda a, b: b),
        out_specs=[tile, tile],
        out_shape=[jax.ShapeDtypeStruct((k // tk, n // tn, tk, tn), F32),
                   jax.ShapeDtypeStruct((k // tk, n // tn, tk, tn), BF16)],
        compiler_params=_params("parallel", "parallel", "arbitrary"),
    )(*([x] * (1 + bool(rest))), *([y] * (1 + bool(rest))))


def _mix_out_bwd(dh1, z, gp, w, o, b, c, hc, cw, ga, gc, tm, deps=()):
    t = dh1.shape[0]
    nt = t // tm
    per8 = tm // 8

    def body(dh1_ref, z_ref, gp_ref, w_ref, o_ref, b_ref, c_ref, hc_ref, cp_ref, hp_ref, cw_ref, ga_ref, gc_ref,
             dz_ref, do_ref, dbch_ref, dgp_ref, dga_ref, dgc_ref, dcw_ref, halo):
        i = pl.program_id(0)

        @pl.when(i == 0)
        def _():
            halo[...] = jnp.zeros_like(halo)
            dgp_ref[...] = jnp.zeros_like(dgp_ref)
            dga_ref[...] = jnp.zeros_like(dga_ref)
            dgc_ref[...] = jnp.zeros_like(dgc_ref)
            dcw_ref[...] = jnp.zeros_like(dcw_ref)

        dz, dgp = _rms_bwd(dh1_ref[...], z_ref[...], gp_ref[...])
        dgp_ref[...] += dgp
        dz = dz.astype(BF16)
        dz_ref[...] = dz
        dy = lax.dot_general(dz, w_ref[...].reshape(D_MODEL, D_MODEL), (((1,), (1,)), ((), ())),
                             preferred_element_type=F32)
        do, dga = _rms_bwd(dy[:, :ATTN_W], o_ref[...], ga_ref[...])
        do_ref[...] = do
        dga_ref[...] += dga

        u = c_ref[...] * hc_ref[...]
        first = i == nt - 1
        u_before = jnp.where(first, 0.0, cp_ref[...] * hp_ref[...])
        u1 = _shift_rows(u, u_before, 1)
        u2 = _shift_rows(u, u_before, 2)
        cv = cw_ref[0:1, :] * u2 + cw_ref[1:2, :] * u1 + cw_ref[2:3, :] * u
        bb = b_ref[...]
        dyc, dgc = _rms_bwd(dy[:, ATTN_W:], bb * cv, gc_ref[...])
        dgc_ref[...] += dgc
        dcv = dyc * bb
        d1 = _advance_rows(dcv, halo[...], 1)
        d2 = _advance_rows(dcv, halo[...], 2)
        halo[...] = dcv[:8]
        du = cw_ref[2:3, :] * dcv + cw_ref[1:2, :] * d1 + cw_ref[0:1, :] * d2
        dbch_ref[...] = jnp.concatenate([dyc * cv, du * hc_ref[...], du * c_ref[...]], axis=1).astype(BF16)
        dcw_ref[...] += jnp.concatenate([jnp.sum(dcv * u2, axis=0, keepdims=True),
                                         jnp.sum(dcv * u1, axis=0, keepdims=True),
                                         jnp.sum(dcv * u, axis=0, keepdims=True)], axis=0)

    row = lambda n: pl.BlockSpec((tm, n), lambda i: (nt - 1 - i, 0))
    before = pl.BlockSpec((8, CONV_W), lambda i: (jnp.maximum((nt - 1 - i) * per8 - 1, 0), 0))
    full = lambda a: pl.BlockSpec(a.shape, lambda i: (0,) * a.ndim)
    vec = lambda n: pl.BlockSpec((1, n), lambda i: (0, 0))
    body, dep_specs = _behind(body, deps)
    return pl.pallas_call(
        body, name="mix_out_bwd", grid=(nt,),
        in_specs=dep_specs + [row(D_MODEL), row(D_MODEL), full(gp), full(w), row(ATTN_W), row(CONV_W), row(CONV_W),
                              row(CONV_W), before, before, full(cw), full(ga), full(gc)],
        out_specs=[row(D_MODEL), row(ATTN_W), row(3 * CONV_W), vec(D_MODEL), vec(ATTN_W), vec(CONV_W),
                   pl.BlockSpec((CONV_K, CONV_W), lambda i: (0, 0))],
        out_shape=[jax.ShapeDtypeStruct((t, D_MODEL), BF16), jax.ShapeDtypeStruct((t, ATTN_W), F32),
                   jax.ShapeDtypeStruct((t, 3 * CONV_W), BF16), jax.ShapeDtypeStruct((1, D_MODEL), F32),
                   jax.ShapeDtypeStruct((1, ATTN_W), F32), jax.ShapeDtypeStruct((1, CONV_W), F32),
                   jax.ShapeDtypeStruct((CONV_K, CONV_W), F32)],
        scratch_shapes=[pltpu.VMEM((8, CONV_W), F32)],
        compiler_params=_params("arbitrary"),
    )(*deps, dh1, z, gp, w, o, b, c, hc, c, hc, cw, ga, gc)


def _attn_bwd(q, k, v, o, do, bias, sinks, tm, deps=()):
    t = q.shape[0]
    per_step = tm // BLOCK

    def body(s_ref, q_ref, k_ref, v_ref, o_ref, do_ref, bias_ref, dq_ref, dk_ref, dv_ref, ds_ref):
        step = pl.program_id(0)

        @pl.when(step == 0)
        def _():
            ds_ref[...] = jnp.zeros_like(ds_ref)

        heads = range(N_Q_HEADS)

        def first_matmuls(b):
            i = step * per_step + b
            rows = slice(b * BLOCK, (b + 1) * BLOCK)
            kc, vc = _two_blocks(k_ref, i), _two_blocks(v_ref, i)
            bias_i = bias_ref[jnp.minimum(i, 2)]
            kgs = [kc[:, _head(g)] for g in range(N_KV_HEADS)]
            vgs = [vc[:, _head(g)] for g in range(N_KV_HEADS)]
            qs = [q_ref[rows, _head(hh)] for hh in heads]
            dos = [do_ref[rows, _head(hh)] for hh in heads]
            dosb = [d.astype(BF16) for d in dos]
            scores = [_attn_scores(qs[hh], kgs[hh // GROUP], bias_i) for hh in heads]
            dps = [lax.dot_general(dosb[hh], vgs[hh // GROUP], (((1,), (1,)), ((), ())), preferred_element_type=F32)
                   for hh in heads]
            return kgs, qs, dos, dosb, scores, dps

        dsink = [jnp.zeros((BLOCK, 1), F32) for _ in range(N_Q_HEADS)]
        ahead = None
        for b in range(per_step):
            i = step * per_step + b
            rows = slice(b * BLOCK, (b + 1) * BLOCK)
            kgs, qs, dos, dosb, scores, dps = first_matmuls(b)
            ps, dss = [], []
            for hh in heads:
                p, share = _attn_probs(scores[hh], s_ref[hh])
                drow = jnp.sum(dos[hh] * o_ref[rows, _head(hh)], axis=-1, keepdims=True)
                dss.append((p * (dps[hh] - drow)).astype(BF16))
                ps.append(p.astype(BF16))
                dsink[hh] = dsink[hh] + share * drow
            for hh in heads:
                dq_ref[rows, _head(hh)] = jnp.dot(dss[hh], kgs[hh // GROUP], preferred_element_type=F32) * SCALE
            groups = [slice(GROUP * g, GROUP * (g + 1)) for g in range(N_KV_HEADS)]
            dkg = [lax.dot_general(jnp.concatenate(dss[gr], axis=0), jnp.concatenate(qs[gr], axis=0),
                                   (((0,), (0,)), ((), ())), preferred_element_type=F32) for gr in groups]
            dvg = [lax.dot_general(jnp.concatenate(ps[gr], axis=0), jnp.concatenate(dosb[gr], axis=0),
                                   (((0,), (0,)), ((), ())), preferred_element_type=F32) for gr in groups]
            dkb, dvb = jnp.concatenate(dkg, axis=1), jnp.concatenate(dvg, axis=1)
            if b == 0:
                @pl.when(step > 0)
                def _():
                    before = pl.ds(pl.multiple_of((i - 1) * BLOCK, BLOCK), BLOCK)
                    dk_ref[before, :] += dkb[:BLOCK]
                    dv_ref[before, :] += dvb[:BLOCK]
            else:
                at = pl.ds(pl.multiple_of((i - 1) * BLOCK, BLOCK), BLOCK)
                dk_ref[at, :] = ahead[0] + dkb[:BLOCK]
                dv_ref[at, :] = ahead[1] + dvb[:BLOCK]
            ahead = (dkb[BLOCK:], dvb[BLOCK:])
        last = pl.ds(pl.multiple_of(((step + 1) * per_step - 1) * BLOCK, BLOCK), BLOCK)
        dk_ref[last, :] = ahead[0]
        dv_ref[last, :] = ahead[1]
        for hh in range(N_Q_HEADS):
            ds_ref[hh:hh + 1, :] -= jnp.sum(dsink[hh])

    whole = pl.BlockSpec((t, KV_W), lambda i: (0, 0))
    blk = pl.BlockSpec((tm, ATTN_W), lambda i: (i, 0))
    body, dep_specs = _behind(body, deps)
    return pl.pallas_call(
        body, name="attn_bwd", grid=(t // tm,),
        in_specs=dep_specs + [pl.BlockSpec(memory_space=pltpu.SMEM), blk, whole, whole, blk, blk,
                              pl.BlockSpec(bias.shape, lambda i: (0, 0, 0))],
        out_specs=[blk, whole, whole, pl.BlockSpec((N_Q_HEADS, 128), lambda i: (0, 0))],
        out_shape=[jax.ShapeDtypeStruct((t, ATTN_W), F32), jax.ShapeDtypeStruct((t, KV_W), F32),
                   jax.ShapeDtypeStruct((t, KV_W), F32), jax.ShapeDtypeStruct((N_Q_HEADS, 128), F32)],
        compiler_params=_params("arbitrary"),
    )(*deps, sinks, q, k, v, o, do, bias)


def _in_proj_bwd(dq, dk, dv, dbch, w, dh1, h, g, tabs, tm):
    t = h.shape[0]

    def body(dq_ref, dk_ref, dv_ref, dbch_ref, w_ref, dh1_ref, h_ref, g_ref, c_ref, sa_ref, sb_ref, dh_ref, dp_ref,
             dg_ref):
        @pl.when(pl.program_id(0) == 0)
        def _():
            dg_ref[...] = jnp.zeros_like(dg_ref)

        cos, sa, sb = c_ref[...], sa_ref[...], sb_ref[...]
        rep = ATTN_W // (2 * HEAD_DIM)
        dqr = _rope_bwd(dq_ref[...], jnp.tile(cos, (1, rep)), jnp.tile(sa, (1, rep)), jnp.tile(sb, (1, rep)))
        dkr = _rope_bwd(dk_ref[...], cos, sa, sb)
        dp = jnp.concatenate([dqr.astype(BF16), dkr.astype(BF16), dv_ref[...].astype(BF16), dbch_ref[...]], axis=1)
        dp_ref[...] = dp
        da = jnp.dot(dp, w_ref[...], preferred_element_type=F32)
        dx, dg = _rms_bwd(da, h_ref[...], g_ref[...])
        dh_ref[...] = dh1_ref[...] + dx
        dg_ref[...] += dg

    row = lambda n: pl.BlockSpec((tm, n), lambda i: (i, 0))
    full = lambda a: pl.BlockSpec(a.shape, lambda i: (0, 0))
    return pl.pallas_call(
        body, name="in_proj_bwd", grid=(t // tm,),
        in_specs=[row(ATTN_W), row(KV_W), row(KV_W), row(3 * CONV_W), full(w), row(D_MODEL), row(D_MODEL), full(g),
                  row(2 * HEAD_DIM), row(2 * HEAD_DIM), row(2 * HEAD_DIM)],
        out_specs=[row(D_MODEL), row(IN_W), pl.BlockSpec((1, D_MODEL), lambda i: (0, 0))],
        out_shape=[jax.ShapeDtypeStruct((t, D_MODEL), F32), jax.ShapeDtypeStruct((t, IN_W), BF16),
                   jax.ShapeDtypeStruct((1, D_MODEL), F32)],
        compiler_params=_params("arbitrary"),
    )(dq, dk, dv, dbch, w, dh1, h, g, *tabs)


class _Tiles:
    def __init__(self, t):
        self.tm = _row_tile(t, 640)
        self.ts = self.tm
        self.tabs = _rope_tables(t)
        self.bias = _attn_bias()


def _mixer_fwd(h, p, tl):
    a, q, k, v, b, c, hc = _in_proj(h, p["mix_pre_g"], p["w_in"], tl.tabs, tl.ts)
    o = _attn_fwd(q, k, v, tl.bias, p["sinks"], tl.tm)
    return (h, a, q, k, v, b, c, hc, o)


def _out_fwd(mixed, p, tl, deps=()):
    h, a, q, k, v, b, c, hc, o = mixed
    h1, y, z = _mix_out(h, o, b, c, hc, p["conv_w"], p["attn_out_g"], p["conv_out_g"], p["w_out"], p["mix_post_g"],
                        tl.ts, deps)
    return h1, mixed + (h1, y, z)


def _mlp_fwd(h1, saved, p, tl):
    h2, a2, act, f = _mlp(h1, p["mlp_pre_g"], p["w_up"], p["w_down"], p["mlp_post_g"], tl.tm)
    return h2, saved + (a2, act, f)


def _mlp_part_bwd(dh, saved, p, tl, deps=()):
    h1, a2, act, f = saved[9], saved[12], saved[13], saved[14]
    df, dup, dg2 = _mlp_bwd_hidden(dh, f, p["mlp_post_g"], act, p["w_down"], tl.tm, deps)
    dh1, dg1 = _mlp_bwd_input(dup, p["w_up"], h1, p["mlp_pre_g"], dh, tl.tm)
    g = {"w_down": [d.reshape(N_CHIPS, FF_CHUNK, D_MODEL) for d in _weight_grad(act, df, "grad_w_down")],
         "w_up": [d.reshape(N_CHIPS, D_MODEL, FF_CHUNK) for d in _weight_grad(a2, dup, "grad_w_up")],
         "mlp_post_g": dg2, "mlp_pre_g": dg1}
    return dh1, g


def _mix_out_part_bwd(dh1, saved, p, tl, deps=()):
    b, c, hc, o, y, z = saved[5], saved[6], saved[7], saved[8], saved[10], saved[11]
    dz, do, dbch, dgp, dga, dgc, dcw = _mix_out_bwd(dh1, z, p["mix_post_g"], p["w_out"], o, b, c, hc, p["conv_w"],
                                                    p["attn_out_g"], p["conv_out_g"], tl.ts, deps)
    g = {"w_out": [d.reshape(N_CHIPS, D_MODEL // N_CHIPS, D_MODEL) for d in _weight_grad(y, dz, "grad_w_out")],
         "mix_post_g": dgp, "attn_out_g": dga, "conv_out_g": dgc, "conv_w": dcw}
    return (dh1, do, dbch), g


def _attn_in_part_bwd(carry, saved, p, tl, deps=()):
    dh1, do, dbch = carry
    h_in, a, q, k, v, o = saved[0], saved[1], saved[2], saved[3], saved[4], saved[8]
    dq, dk, dv, dsink = _attn_bwd(q, k, v, o, do, tl.bias, p["sinks"], tl.tm, deps)
    dh, dproj, dgi = _in_proj_bwd(dq, dk, dv, dbch, p["w_in"], dh1, h_in, p["mix_pre_g"], tl.tabs, tl.ts)
    g_in = [d.reshape(N_CHIPS, IN_W // N_CHIPS, D_MODEL) for d in _weight_grad(dproj, a, "grad_w_in")]
    return dh, {"w_in": g_in, "mix_pre_g": dgi, "sinks": dsink[:, 0]}


def _place():
    return lax.axis_index("x"), lax.axis_index("y"), lax.axis_index("c")


def _other_chips(x, y):
    return [(1 - x, y), (x, 1 - y), (1 - x, 1 - y)]


_HBM = pl.BlockSpec(memory_space=pltpu.HBM)
_SEM = pl.BlockSpec(memory_space=pltpu.SEMAPHORE)
_EFFECT = pltpu.SideEffectType.DATAFLOW_SIDE_EFFECTING


class _Exchange:
    def __init__(self, name, bufs, plan, n, after=()):
        self.name, self.plan, nb = name, plan, len(bufs)
        n_in = nb + len(after)

        def body(*refs):
            send, recv, token = refs[n_in], refs[n_in + 1], refs[-1]
            for k, (src, dst, target, _) in enumerate(plan(refs[:nb])):
                pltpu.make_async_remote_copy(src_ref=src, dst_ref=dst, send_sem=send.at[k], recv_sem=recv.at[k],
                                             device_id=target, device_id_type=MESH).start()
            token[...] = jnp.zeros_like(token)

        outs = pl.pallas_call(
            body, name=name + "_start",
            out_shape=(pltpu.SemaphoreType.DMA((n,)), pltpu.SemaphoreType.DMA((n,)),
                       *[pltpu.HBM(b.shape, b.dtype) for b in bufs], jax.ShapeDtypeStruct((8, 128), F32)),
            in_specs=[_HBM] * nb + [pl.BlockSpec(memory_space=pl.ANY)] * len(after),
            out_specs=(_SEM, _SEM, *[_HBM] * nb, pl.BlockSpec(memory_space=pltpu.VMEM)),
            input_output_aliases={i: 2 + i for i in range(nb)},
            compiler_params=pltpu.CompilerParams(has_side_effects=_EFFECT),
        )(*[pltpu.with_memory_space_constraint(b, pltpu.HBM) for b in bufs], *after)
        self.send, self.recv, self.bufs, self.token = outs[0], outs[1], list(outs[2:2 + nb]), outs[-1]

    def wait(self, *after):
        plan, nb = self.plan, len(self.bufs)

        def body(*refs):
            send, recv = refs[nb], refs[nb + 1]
            for k, (src, _, target, land) in enumerate(plan(refs[:nb])):
                cp = pltpu.make_async_remote_copy(src_ref=src, dst_ref=land, send_sem=send.at[k], recv_sem=recv.at[k],
                                                  device_id=target, device_id_type=MESH)
                cp.wait_send()
                cp.wait_recv()

        outs = pl.pallas_call(
            body, name=self.name + "_wait", out_shape=[pltpu.HBM(b.shape, b.dtype) for b in self.bufs],
            in_specs=[_HBM] * nb + [_SEM, _SEM] + [pl.BlockSpec(memory_space=pl.ANY)] * len(after),
            out_specs=[_HBM] * nb, input_output_aliases={i: i for i in range(nb)},
            compiler_params=pltpu.CompilerParams(has_side_effects=_EFFECT),
        )(*self.bufs, self.send, self.recv, *after)
        return list(outs)


def _gather_plan(n):
    def plan(refs):
        x, y, c = _place()
        me = 2 * x + y
        return [(refs[a].at[me], refs[a].at[me], (px, py, c), refs[a].at[2 * px + py])
                for a in range(n) for px, py in _other_chips(x, y)]

    return plan


def _peers():
    x, y, c = _place()
    return [(k - 1, (x ^ (k >> 2), y ^ ((k >> 1) & 1), c ^ (k & 1))) for k in range(1, N_DEV)]


def _scatter_plan(n, half_rows):
    def plan(refs):
        out = []
        for a in range(n):
            hr = half_rows[a]
            for k, (px, py, pc) in _peers():
                out.append((refs[a].at[2 * px + py, pl.ds(pc * hr, hr)], refs[n + a].at[k], (px, py, pc),
                            refs[n + a].at[k]))
        return out

    return plan


def _join_plan(n):
    def plan(refs):
        x, y, c = _place()
        return [(refs[a].at[c], refs[a].at[c], (x, y, 1 - c), refs[a].at[1 - c]) for a in range(n)]

    return plan


def _sum_parts(g, q):
    rows, cols = g.shape[1], g.shape[2]
    hr = rows // 2
    tr = _block_rows(hr)
    per = hr // tr
    x, y, c = _place()
    where = jnp.stack([2 * x + y, c]).astype(jnp.int32)

    def body(where_ref, g_ref, q_ref, o_ref):
        total = g_ref[...]
        for k in range(N_DEV - 1):
            total = total + q_ref[k].astype(F32)
        o_ref[...] = total

    return pl.pallas_call(
        body, name="sum_parts",
        grid_spec=pltpu.PrefetchScalarGridSpec(
            num_scalar_prefetch=1, grid=(per,),
            in_specs=[pl.BlockSpec((None, tr, cols), lambda i, where_ref: (where_ref[0], where_ref[1] * per + i, 0)),
                      pl.BlockSpec((N_DEV - 1, tr, cols), lambda i, where_ref: (0, i, 0))],
            out_specs=pl.BlockSpec((None, tr, cols), lambda i, where_ref: (where_ref[1], i, 0))),
        out_shape=jax.ShapeDtypeStruct((2, hr, cols), F32),
        compiler_params=_params("parallel"),
    )(where, g, q)


def _sum_devices(packed):
    def body(p_ref, o_ref, land, send_sems, recv_sems):
        x, y, c = _place()
        me = 4 * x + 2 * y + c
        land[me] = p_ref[...]
        sends = []
        for k in range(1, N_DEV):
            px, py, pc = x ^ (k >> 2), y ^ ((k >> 1) & 1), c ^ (k & 1)
            cp = pltpu.make_async_remote_copy(src_ref=p_ref, dst_ref=land.at[me], send_sem=send_sems.at[k - 1],
                                              recv_sem=recv_sems.at[k - 1], device_id=(px, py, pc), device_id_type=MESH)
            cp.start()
            sends.append(cp)
        for k in range(1, N_DEV):
            px, py, pc = x ^ (k >> 2), y ^ ((k >> 1) & 1), c ^ (k & 1)
            pltpu.make_async_remote_copy(src_ref=p_ref, dst_ref=land.at[4 * px + 2 * py + pc],
                                         send_sem=send_sems.at[k - 1], recv_sem=recv_sems.at[k - 1],
                                         device_id=(px, py, pc), device_id_type=MESH).wait_recv()
        for cp in sends:
            cp.wait_send()
        total = land[0]
        for d in range(1, N_DEV):
            total = total + land[d]
        o_ref[...] = total

    vm = pl.BlockSpec(memory_space=pltpu.VMEM)
    return pl.pallas_call(
        body, name="sum_devices", in_specs=[vm], out_specs=vm,
        out_shape=jax.ShapeDtypeStruct(packed.shape, F32),
        scratch_shapes=[pltpu.VMEM((N_DEV,) + packed.shape, F32), pltpu.SemaphoreType.DMA((N_DEV - 1,)),
                        pltpu.SemaphoreType.DMA((N_DEV - 1,))],
    )(packed)


def _adamw_math(w, g, m, v):
    m = ADAM_B1 * m + (1.0 - ADAM_B1) * g
    v = ADAM_B2 * v + (1.0 - ADAM_B2) * jnp.square(g)
    m_hat = m / (1.0 - ADAM_B1 ** ADAM_STEP)
    v_hat = v / (1.0 - ADAM_B2 ** ADAM_STEP)
    delta = -ADAM_LR * (m_hat / (jnp.sqrt(v_hat) + ADAM_EPS) + ADAM_WD * w)
    return delta, m, v


def _adamw_large(layer, w, halves, m, v, other):
    _, rows, cols = w.shape
    tr = _block_rows(rows // 2)
    per = rows // 2 // tr

    def body(w_ref, g_ref, m_ref, v_ref, *rest):
        g_out, d_ref, nm_ref, nv_ref = rest[-4:]
        g = g_ref[...]
        g_out[...] = g
        d_ref[...], nm_ref[...], nv_ref[...] = _adamw_math(w_ref[...], g, m_ref[...], v_ref[...])

    blk = pl.BlockSpec((None, tr, cols), lambda i: (layer, i, 0))
    half = pl.BlockSpec((None, tr, cols), lambda i: (i // per, i % per, 0))
    kept = [] if other is None else list(other)
    return pl.pallas_call(
        body, name="adamw_large", grid=(rows // tr,),
        in_specs=[blk, half, blk, blk] + [pl.BlockSpec(memory_space=pl.ANY)] * len(kept), out_specs=[blk] * 4,
        out_shape=[jax.ShapeDtypeStruct(w.shape, F32)] * 4,
        input_output_aliases={4 + k: k for k in range(len(kept))},
        compiler_params=_params("parallel"),
    )(w, halves, m, v, *kept)


def _adamw_small(ws, gs, ms, vs):
    n = len(ws)

    def body(*refs):
        w_r, g_r, m_r, v_r = refs[:n], refs[n:2 * n], refs[2 * n:3 * n], refs[3 * n:4 * n]
        d_r, nm_r, nv_r = refs[4 * n:5 * n], refs[5 * n:6 * n], refs[6 * n:]
        for a in range(n):
            d_r[a][...], nm_r[a][...], nv_r[a][...] = _adamw_math(w_r[a][...], g_r[a][...], m_r[a][...], v_r[a][...])

    vm = pl.BlockSpec(memory_space=pltpu.VMEM)
    outs = pl.pallas_call(
        body, name="adamw_small", in_specs=[vm] * (4 * n), out_specs=[vm] * (3 * n),
        out_shape=[jax.ShapeDtypeStruct(w.shape, F32) for w in ws] * 3,
    )(*ws, *gs, *ms, *vs)
    return outs[:n], outs[n:2 * n], outs[2 * n:]


_LARGE = ("w_in", "w_out", "w_up", "w_down")
_SMALL = ("meta_tokens", "mix_pre_g", "conv_w", "sinks", "attn_out_g", "conv_out_g", "mix_post_g", "mlp_pre_g",
          "mlp_post_g")
_ORDER = ("meta_tokens", "mix_pre_g", "w_in", "conv_w", "sinks", "attn_out_g", "conv_out_g", "w_out", "mix_post_g",
          "mlp_pre_g", "w_up", "w_down", "mlp_post_g")


class _Reduce:
    def __init__(self, name, grads, after=()):
        self.name, self.n = name, len(grads)
        self.own = [g for g, _ in grads]
        half_rows = [g.shape[1] // 2 for g in self.own]
        zones = [lax.empty((N_DEV - 1, hr, g.shape[2]), BF16) for g, hr in zip(self.own, half_rows)]
        self.exchange = _Exchange(name + "_scatter", [b for _, b in grads] + zones, _scatter_plan(self.n, half_rows),
                                  (N_DEV - 1) * self.n, after)

    @property
    def token(self):
        return self.exchange.token

    def join(self, *after):
        bufs = self.exchange.wait(*after)
        halves = [_sum_parts(g, q) for g, q in zip(self.own, bufs[self.n:])]
        self.exchange = _Exchange(self.name + "_join", halves, _join_plan(self.n), self.n)

    def done(self, *after):
        return self.exchange.wait(*after)


def _pad_cols(a, n=D_MODEL):
    return jnp.pad(a, ((0, 0), (0, n - a.shape[1])))


def kernel(x, meta_tokens, mix_pre_g, w_in, conv_w, sinks, attn_out_g, conv_out_g, w_out, mix_post_g, mlp_pre_g, w_up, w_down, mlp_post_g, loss_target, m_meta_tokens, m_mix_pre_g, m_w_in, m_conv_w, m_sinks, m_attn_out_g, m_conv_out_g, m_w_out, m_mix_post_g, m_mlp_pre_g, m_w_up, m_w_down, m_mlp_post_g, v_meta_tokens, v_mix_pre_g, v_w_in, v_conv_w, v_sinks, v_attn_out_g, v_conv_out_g, v_w_out, v_mix_post_g, v_mlp_pre_g, v_w_up, v_w_down, v_mlp_post_g):
    w = dict(meta_tokens=meta_tokens, mix_pre_g=mix_pre_g, w_in=w_in, conv_w=conv_w, sinks=sinks,
             attn_out_g=attn_out_g, conv_out_g=conv_out_g, w_out=w_out, mix_post_g=mix_post_g, mlp_pre_g=mlp_pre_g,
             w_up=w_up, w_down=w_down, mlp_post_g=mlp_post_g)
    m = dict(meta_tokens=m_meta_tokens, mix_pre_g=m_mix_pre_g, w_in=m_w_in, conv_w=m_conv_w, sinks=m_sinks,
             attn_out_g=m_attn_out_g, conv_out_g=m_conv_out_g, w_out=m_w_out, mix_post_g=m_mix_post_g,
             mlp_pre_g=m_mlp_pre_g, w_up=m_w_up, w_down=m_w_down, mlp_post_g=m_mlp_post_g)
    v = dict(meta_tokens=v_meta_tokens, mix_pre_g=v_mix_pre_g, w_in=v_w_in, conv_w=v_conv_w, sinks=v_sinks,
             attn_out_g=v_attn_out_g, conv_out_g=v_conv_out_g, w_out=v_w_out, mix_post_g=v_mix_post_g,
             mlp_pre_g=v_mlp_pre_g, w_up=v_w_up, w_down=v_w_down, mlp_post_g=v_mlp_post_g)
    chip = 2 * lax.axis_index("x") + lax.axis_index("y")
    tl = _Tiles(x.shape[1] + BLOCK)

    def zone(quarter):
        return lax.dynamic_update_slice(lax.empty((N_CHIPS,) + quarter.shape, quarter.dtype), quarter[None],
                                        (chip,) + (0,) * quarter.ndim)

    w, m, v = ({**d, "w_in": jnp.swapaxes(d["w_in"], 1, 2)} for d in (w, m, v))
    zones = {n: [zone(w[n][l].astype(BF16)) for l in range(DEPTH)] for n in _LARGE}
    first = _Exchange("gather_first", [zones["w_in"][0], zone(w["conv_w"]), zone(w["meta_tokens"])], _gather_plan(3), 9)
    out0 = _Exchange("gather_out", [zones["w_out"][0]], _gather_plan(1), 3, [first.token])
    rest = _Exchange("gather_rest", [zones[n][0] for n in ("w_up", "w_down")], _gather_plan(2), 6, [out0.token])

    def whole_in(quarters):
        return quarters.reshape(IN_W, D_MODEL)

    h = jnp.concatenate([jnp.zeros((BLOCK, D_MODEL), F32), x[0]], axis=0)
    q_in, q_conv, q_meta = first.wait(rest.token, *tl.tabs, tl.bias, h)
    conv_whole = jnp.transpose(q_conv, (1, 2, 0, 3)).reshape(DEPTH, CONV_K, CONV_W)
    meta = jnp.transpose(q_meta, (1, 0, 2)).reshape(N_META, D_MODEL)
    p = [{"conv_w": conv_whole[l], "sinks": w["sinks"][l]} for l in range(DEPTH)]
    for l in range(DEPTH):
        for n in ("mix_pre_g", "attn_out_g", "conv_out_g", "mix_post_g", "mlp_pre_g", "mlp_post_g"):
            p[l][n] = w[n][l][None, :]

    h = lax.dynamic_update_slice(h, meta, (LEAD_PAD, 0))
    p[0]["w_in"] = whole_in(q_in)
    mixed = _mixer_fwd(h, p[0], tl)
    second = _Exchange("gather_second", [zones["w_in"][1], zones["w_out"][1]], _gather_plan(2), 6, [mixed[-1]])
    second_mlp = _Exchange("gather_second_mlp", [zones["w_up"][1], zones["w_down"][1]], _gather_plan(2), 6,
                           [second.token])
    p[0]["w_out"], = out0.wait(second_mlp.token)
    h1, saved0 = _out_fwd(mixed, p[0], tl)
    p[0]["w_up"], p[0]["w_down"] = rest.wait(h1)
    h, saved0 = _mlp_fwd(h1, saved0, p[0], tl)
    q_in, p[1]["w_out"] = second.wait(h)
    p[1]["w_in"] = whole_in(q_in)
    h1, saved1 = _out_fwd(_mixer_fwd(h, p[1], tl), p[1], tl)
    p[1]["w_up"], p[1]["w_down"] = second_mlp.wait(h1)
    h, saved1 = _mlp_fwd(h1, saved1, p[1], tl)
    loss_tile, dh = _loss_head(h, loss_target[0], tl.tm)

    def adamw(layer, halves, other):
        return {n: _adamw_large(layer, w[n], halves[n], m[n], v[n], None if other is None else other[n])
                for n in halves}

    dh1, g1 = _mlp_part_bwd(dh, saved1, p[1], tl)
    carry, gm = _mix_out_part_bwd(dh1, saved1, p[1], tl)
    dh, gi = _attn_in_part_bwd(carry, saved1, p[1], tl)
    g1.update(gm, **gi)
    red1 = _Reduce("reduce1", [g1[n] for n in _LARGE])
    dh1, g0 = _mlp_part_bwd(dh, saved0, p[0], tl, [red1.token])
    red1.join(g0["w_down"][0])
    carry, gm = _mix_out_part_bwd(dh1, saved0, p[0], tl, [red1.token])
    first0 = ("w_up", "w_down", "w_out")
    g0.update(gm)
    red0a = _Reduce("reduce0a", [g0[n] for n in first0])
    dh0, gi = _attn_in_part_bwd(carry, saved0, p[0], tl, [red0a.token])
    g0.update(gi)
    red0b = _Reduce("reduce0b", [g0["w_in"]])
    grad_x = dh0[BLOCK:][None]
    grads = {n: [g0[n], g1[n]] for n in g0 if n not in _LARGE}

    rows = [dh0[LEAD_PAD:BLOCK]]
    for n in ("mix_pre_g", "mix_post_g", "mlp_pre_g", "mlp_post_g"):
        rows += grads[n]
    rows += [jnp.concatenate([grads["attn_out_g"][l], grads["conv_out_g"][l]], axis=1) for l in range(DEPTH)]
    rows.append(jnp.concatenate(grads["conv_w"], axis=1))
    rows.append(_pad_cols(jnp.concatenate(grads["sinks"])[None, :]))
    rows.append(_pad_cols(loss_tile[:1]))
    packed = jnp.concatenate(rows, axis=0)
    packed = jnp.pad(packed, ((0, SMALL_ROWS - packed.shape[0]), (0, 0)))
    total = _sum_devices(packed)
    r0 = N_META
    small = {
        "meta_tokens": lax.dynamic_slice(total[:N_META], (0, chip * (D_MODEL // N_CHIPS)), (N_META, D_MODEL // N_CHIPS)),
        "mix_pre_g": total[r0:r0 + 2], "mix_post_g": total[r0 + 2:r0 + 4], "mlp_pre_g": total[r0 + 4:r0 + 6],
        "mlp_post_g": total[r0 + 6:r0 + 8],
        "attn_out_g": total[r0 + 8:r0 + 10, :ATTN_W], "conv_out_g": total[r0 + 8:r0 + 10, ATTN_W:],
        "conv_w": lax.dynamic_slice(total[r0 + 10:r0 + 13].reshape(CONV_K, DEPTH, CONV_W).transpose(1, 0, 2),
                                    (0, 0, chip * (CONV_W // N_CHIPS)), (DEPTH, CONV_K, CONV_W // N_CHIPS)),
        "sinks": total[r0 + 13, :DEPTH * N_Q_HEADS].reshape(DEPTH, N_Q_HEADS),
    }
    loss = total[r0 + 14, 0]

    ds, nms, nvs = _adamw_small([w[n] for n in _SMALL], [small[n] for n in _SMALL], [m[n] for n in _SMALL],
                                [v[n] for n in _SMALL])
    done1 = adamw(1, dict(zip(_LARGE, red1.done(red0b.token))), None)
    red0a.join(ds[0], grad_x, *[done1[n][0] for n in _LARGE])
    red0b.join(red0a.token)
    done0 = adamw(0, dict(zip(first0, red0a.done(red0b.token))), done1)
    done0.update(adamw(0, {"w_in": red0b.done(done0["w_down"][0])[0]}, done1))
    grad, delta, new_m, new_v = {}, {}, {}, {}
    for n in _LARGE:
        grad[n], delta[n], new_m[n], new_v[n] = done0[n]
    for d in (grad, delta, new_m, new_v):
        d["w_in"] = jnp.swapaxes(d["w_in"], 1, 2)
    for i, n in enumerate(_SMALL):
        grad[n], delta[n], new_m[n], new_v[n] = small[n], ds[i], nms[i], nvs[i]
    return (loss, grad_x, *[grad[n] for n in _ORDER], *[delta[n] for n in _ORDER], *[new_m[n] for n in _ORDER],
            *[new_v[n] for n in _ORDER])
```

```python
import functools

import jax
import jax.numpy as jnp
from jax import lax
from jax.experimental import pallas as pl
from jax.experimental.pallas import tpu as pltpu

F32 = jnp.float32
BF16 = jnp.bfloat16

D_MODEL = 1024
DEPTH = 2
N_META = 16
ATTN_W = 512
CONV_W = 512
HEAD_DIM = 64
N_Q_HEADS = 8
N_KV_HEADS = 2
GROUP = N_Q_HEADS // N_KV_HEADS
KV_W = N_KV_HEADS * HEAD_DIM
CONV_K = 3
BLOCK = 128
LEAD_PAD = BLOCK - N_META
ROPE_THETA = 500000.0
ROT_DIM = HEAD_DIM // 4
ROT_HALF = ROT_DIM // 2
D_FF = 4 * D_MODEL
IN_W = ATTN_W + 2 * KV_W + 3 * CONV_W
QKV_W = ATTN_W + 2 * KV_W
EPS = 1e-6
SCALE = HEAD_DIM ** -0.5
FF_CHUNK = 1024
N_CHIPS = 4
N_DEV = 8

ADAM_LR = 0.001
ADAM_B1 = 0.9
ADAM_B2 = 0.999
ADAM_EPS = 1e-08
ADAM_WD = 0.01
ADAM_STEP = 10

V7X_VMEM_LIMIT = 56 * 1024 * 1024
SMALL_ROWS = 32

MESH = pl.DeviceIdType.MESH


def _params(*sem):
    return pltpu.CompilerParams(dimension_semantics=sem, vmem_limit_bytes=V7X_VMEM_LIMIT)


def _block_rows(n):
    return max(r for r in range(16, min(n, 256) + 1, 16) if n % r == 0)


def _row_tile(t, most):
    nb = t // BLOCK
    for b in range(most // BLOCK, 0, -1):
        if nb % b == 0:
            return b * BLOCK
    return BLOCK


def _behind(body, deps):
    n = len(deps)

    def wrapped(*refs):
        body(*refs[n:])

    return wrapped, [pl.BlockSpec(memory_space=pl.ANY)] * n


def _rms(x, g):
    r = lax.rsqrt(jnp.mean(x * x, axis=-1, keepdims=True) + EPS)
    return x * r * g


def _rms_bwd(dy, x, g):
    r = lax.rsqrt(jnp.mean(x * x, axis=-1, keepdims=True) + EPS)
    xh = x * r
    dg = jnp.sum(dy * xh, axis=0, keepdims=True)
    dxh = dy * g
    dx = r * (dxh - xh * jnp.mean(dxh * xh, axis=-1, keepdims=True))
    return dx, dg


def _rope(x, cos, sa, sb):
    n = x.shape[-1]
    return x * cos + pltpu.roll(x, n - ROT_HALF, 1) * sa + pltpu.roll(x, ROT_HALF, 1) * sb


def _rope_bwd(dy, cos, sa, sb):
    n = dy.shape[-1]
    return dy * cos + pltpu.roll(dy * sa, ROT_HALF, 1) + pltpu.roll(dy * sb, n - ROT_HALF, 1)


def _rope_tables(t):
    pos = lax.broadcasted_iota(jnp.int32, (t, ROT_HALF), 0).astype(F32) - LEAD_PAD
    pair = lax.broadcasted_iota(jnp.int32, (t, ROT_HALF), 1).astype(F32)
    inv_freq = jnp.power(jnp.float32(ROPE_THETA), -(2.0 * pair) / ROT_DIM)
    ang = pos * inv_freq
    cos, sin = lax.optimization_barrier((jnp.cos(ang), jnp.sin(ang)))
    spread = (1, 2 * HEAD_DIM // ROT_HALF)
    cos, sin = jnp.tile(cos, spread), jnp.tile(sin, spread)
    dim = lax.broadcasted_iota(jnp.int32, (t, 2 * HEAD_DIM), 1) % HEAD_DIM
    return (jnp.where(dim < ROT_DIM, cos, 1.0), jnp.where(dim < ROT_HALF, -sin, 0.0),
            jnp.where((dim >= ROT_HALF) & (dim < ROT_DIM), sin, 0.0))


def _in_proj(h, g, w, tabs, tm):
    t = h.shape[0]

    def body(h_ref, g_ref, w_ref, c_ref, sa_ref, sb_ref, a_ref, q_ref, k_ref, v_ref, b_ref, cg_ref, hc_ref):
        a = _rms(h_ref[...], g_ref[...]).astype(BF16)
        a_ref[...] = a
        p = lax.dot_general(a, w_ref[...], (((1,), (1,)), ((), ())), preferred_element_type=F32)
        cos, sa, sb = c_ref[...], sa_ref[...], sb_ref[...]
        rep = ATTN_W // (2 * HEAD_DIM)
        q = _rope(p[:, :ATTN_W], jnp.tile(cos, (1, rep)), jnp.tile(sa, (1, rep)), jnp.tile(sb, (1, rep)))
        q_ref[...] = (q * SCALE).astype(BF16)
        k_ref[...] = _rope(p[:, ATTN_W:ATTN_W + KV_W], cos, sa, sb).astype(BF16)
        v_ref[...] = p[:, ATTN_W + KV_W:QKV_W].astype(BF16)
        b_ref[...] = p[:, QKV_W:QKV_W + CONV_W].astype(BF16)
        cg_ref[...] = p[:, QKV_W + CONV_W:QKV_W + 2 * CONV_W].astype(BF16)
        hc_ref[...] = p[:, QKV_W + 2 * CONV_W:].astype(BF16)

    row = lambda n: pl.BlockSpec((tm, n), lambda i: (i, 0))
    full = lambda a: pl.BlockSpec(a.shape, lambda i: (0, 0))
    return pl.pallas_call(
        body, name="in_proj", grid=(t // tm,),
        in_specs=[row(D_MODEL), full(g), full(w), row(2 * HEAD_DIM), row(2 * HEAD_DIM), row(2 * HEAD_DIM)],
        out_specs=[row(D_MODEL), row(ATTN_W), row(KV_W), row(KV_W), row(CONV_W), row(CONV_W), row(CONV_W)],
        out_shape=[jax.ShapeDtypeStruct((t, D_MODEL), BF16), jax.ShapeDtypeStruct((t, ATTN_W), BF16),
                   jax.ShapeDtypeStruct((t, KV_W), BF16), jax.ShapeDtypeStruct((t, KV_W), BF16),
                   jax.ShapeDtypeStruct((t, CONV_W), BF16), jax.ShapeDtypeStruct((t, CONV_W), BF16),
                   jax.ShapeDtypeStruct((t, CONV_W), BF16)],
        compiler_params=_params("parallel"),
    )(h, g, w, *tabs)


def _attn_bias():
    r = lax.broadcasted_iota(jnp.int32, (3, BLOCK, 2 * BLOCK), 1)
    c = lax.broadcasted_iota(jnp.int32, (3, BLOCK, 2 * BLOCK), 2)
    i = lax.broadcasted_iota(jnp.int32, (3, BLOCK, 2 * BLOCK), 0)
    ok = (c > r) & (c <= r + BLOCK) & (c + (i - 1) * BLOCK >= LEAD_PAD)
    return jnp.where(ok, 0.0, -jnp.inf).astype(F32)


def _attn_scores(qh, kg, bias):
    return lax.dot_general(qh, kg, (((1,), (1,)), ((), ())), preferred_element_type=F32) + bias


def _attn_probs(s, sk):
    m = jnp.maximum(jnp.max(s, axis=-1, keepdims=True), sk)
    e = jnp.exp(s - m)
    es = jnp.exp(sk - m)
    rden = 1.0 / (jnp.sum(e, axis=-1, keepdims=True) + es)
    return e * rden, es * rden


def _head(hh):
    return slice(hh * HEAD_DIM, (hh + 1) * HEAD_DIM)


def _two_blocks(ref, i):
    prev = jnp.maximum(i - 1, 0)
    return jnp.concatenate([ref[pl.ds(pl.multiple_of(prev * BLOCK, BLOCK), BLOCK), :],
                            ref[pl.ds(pl.multiple_of(i * BLOCK, BLOCK), BLOCK), :]], axis=0)


def _attn_fwd(q, k, v, bias, sinks, tm):
    t = q.shape[0]
    per_step = tm // BLOCK
    heads = range(N_Q_HEADS)

    def body(s_ref, q_ref, k_ref, v_ref, bias_ref, o_ref):
        for b in range(per_step):
            i = pl.program_id(0) * per_step + b
            rows = slice(b * BLOCK, (b + 1) * BLOCK)
            kc, vc = _two_blocks(k_ref, i), _two_blocks(v_ref, i)
            bias_i = bias_ref[jnp.minimum(i, 2)]
            scores = [_attn_scores(q_ref[rows, _head(hh)], kc[:, _head(hh // GROUP)], bias_i) for hh in heads]
            probs = [_attn_probs(scores[hh], s_ref[hh])[0].astype(BF16) for hh in heads]
            for hh in heads:
                o_ref[rows, _head(hh)] = jnp.dot(probs[hh], vc[:, _head(hh // GROUP)],
                                                 preferred_element_type=F32).astype(BF16)

    whole = pl.BlockSpec((t, KV_W), lambda i: (0, 0))
    return pl.pallas_call(
        body, name="attn_fwd", grid=(t // tm,),
        in_specs=[pl.BlockSpec(memory_space=pltpu.SMEM), pl.BlockSpec((tm, ATTN_W), lambda i: (i, 0)), whole, whole,
                  pl.BlockSpec(bias.shape, lambda i: (0, 0, 0))],
        out_specs=pl.BlockSpec((tm, ATTN_W), lambda i: (i, 0)),
        out_shape=jax.ShapeDtypeStruct((t, ATTN_W), BF16),
        compiler_params=_params("parallel"),
    )(sinks, q, k, v, bias)


def _shift_rows(u, halo, n):
    r = pltpu.roll(u, n, 0)
    hr = pltpu.roll(halo, n, 0)
    idx = lax.broadcasted_iota(jnp.int32, hr.shape, 0)
    return jnp.concatenate([jnp.where(idx < n, hr, r[:8]), r[8:]], axis=0)


def _advance_rows(u, halo, n):
    rows = u.shape[0]
    r = pltpu.roll(u, rows - n, 0)
    hr = pltpu.roll(halo, 8 - n, 0)
    idx = lax.broadcasted_iota(jnp.int32, hr.shape, 0)
    return jnp.concatenate([r[:rows - 8], jnp.where(idx >= 8 - n, hr, r[rows - 8:])], axis=0)


def _mix_out(h, o, b, c, hc, cw, ga, gc, w, gp, tm, deps=()):
    t = h.shape[0]

    def body(h_ref, o_ref, b_ref, c_ref, hc_ref, cw_ref, ga_ref, gc_ref, w_ref, gp_ref, h1_ref, y_ref, z_ref, halo):
        @pl.when(pl.program_id(0) == 0)
        def _():
            halo[...] = jnp.zeros_like(halo)

        u = c_ref[...].astype(F32) * hc_ref[...].astype(F32)
        cv = cw_ref[0:1, :] * _shift_rows(u, halo[...], 2) + cw_ref[1:2, :] * _shift_rows(u, halo[...], 1) \
            + cw_ref[2:3, :] * u
        halo[...] = u[tm - 8:]
        yc = b_ref[...].astype(F32) * cv
        y = jnp.concatenate([_rms(o_ref[...].astype(F32), ga_ref[...]), _rms(yc, gc_ref[...])], axis=1).astype(BF16)
        y_ref[...] = y
        z = jnp.dot(y, w_ref[...].reshape(D_MODEL, D_MODEL), preferred_element_type=F32)
        z_ref[...] = z
        h1_ref[...] = h_ref[...] + _rms(z, gp_ref[...])

    row = lambda n: pl.BlockSpec((tm, n), lambda i: (i, 0))
    full = lambda a: pl.BlockSpec(a.shape, lambda i: (0,) * a.ndim)
    body, dep_specs = _behind(body, deps)
    return pl.pallas_call(
        body, name="mix_out", grid=(t // tm,),
        in_specs=dep_specs + [row(D_MODEL), row(ATTN_W), row(CONV_W), row(CONV_W), row(CONV_W), full(cw), full(ga),
                              full(gc), full(w), full(gp)],
        out_specs=[row(D_MODEL), row(D_MODEL), row(D_MODEL)],
        out_shape=[jax.ShapeDtypeStruct((t, D_MODEL), F32), jax.ShapeDtypeStruct((t, D_MODEL), BF16),
                   jax.ShapeDtypeStruct((t, D_MODEL), F32)],
        scratch_shapes=[pltpu.VMEM((8, CONV_W), F32)],
        compiler_params=_params("arbitrary"),
    )(*deps, h, o, b, c, hc, cw, ga, gc, w, gp)


def _mlp(h1, g1, wu, wd, g2, tm):
    t = h1.shape[0]
    nj = D_FF // FF_CHUNK

    def body(h1_ref, g1_ref, wu_ref, wd_ref, g2_ref, h2_ref, a2_ref, act_ref, f_ref):
        a2 = _rms(h1_ref[...], g1_ref[...]).astype(BF16)
        a2_ref[...] = a2
        f = None
        for j in range(nj):
            up = jnp.dot(a2, wu_ref[j], preferred_element_type=F32)
            act = jnp.square(jnp.maximum(up, 0.0)).astype(BF16)
            act_ref[:, j * FF_CHUNK:(j + 1) * FF_CHUNK] = act
            part = jnp.dot(act, wd_ref[j], preferred_element_type=F32)
            f = part if f is None else f + part
        f_ref[...] = f
        h2_ref[...] = h1_ref[...] + _rms(f, g2_ref[...])

    row = pl.BlockSpec((tm, D_MODEL), lambda i: (i, 0))
    vec = pl.BlockSpec((1, D_MODEL), lambda i: (0, 0))
    resident = pl.BlockSpec(memory_space=pltpu.VMEM)
    return pl.pallas_call(
        body, name="mlp", grid=(t // tm,),
        in_specs=[row, vec, resident, resident, vec],
        out_specs=[row, row, pl.BlockSpec((tm, D_FF), lambda i: (i, 0)), row],
        out_shape=[jax.ShapeDtypeStruct((t, D_MODEL), F32), jax.ShapeDtypeStruct((t, D_MODEL), BF16),
                   jax.ShapeDtypeStruct((t, D_FF), BF16), jax.ShapeDtypeStruct((t, D_MODEL), F32)],
        compiler_params=_params("parallel"),
    )(h1, g1, wu, wd, g2)


def _loss_head(h, target, tm):
    t = h.shape[0]
    per_step = tm // BLOCK

    def body(h_ref, *rest):
        t_refs, (loss_ref, dh_ref) = rest[:per_step], rest[per_step:]
        i = pl.program_id(0)

        @pl.when(i == 0)
        def _():
            loss_ref[...] = jnp.zeros_like(loss_ref)

        total = jnp.zeros((), F32)
        for b in range(per_step):
            rows = slice(b * BLOCK, (b + 1) * BLOCK)
            err = h_ref[rows, :] - t_refs[b][...]
            if b == 0:
                err = jnp.where(i == 0, 0.0, err)
            dh_ref[rows, :] = err * (1.0 / D_MODEL)
            total = total + jnp.sum(err * err)
        loss_ref[...] += total * (0.5 / D_MODEL)

    def target_block(b):
        return pl.BlockSpec((BLOCK, D_MODEL), lambda i: (jnp.maximum(i * per_step + b - 1, 0), 0))

    return pl.pallas_call(
        body, name="loss_head", grid=(t // tm,),
        in_specs=[pl.BlockSpec((tm, D_MODEL), lambda i: (i, 0))] + [target_block(b) for b in range(per_step)],
        out_specs=[pl.BlockSpec((8, 128), lambda i: (0, 0)), pl.BlockSpec((tm, D_MODEL), lambda i: (i, 0))],
        out_shape=[jax.ShapeDtypeStruct((8, 128), F32), jax.ShapeDtypeStruct((t, D_MODEL), F32)],
        compiler_params=_params("arbitrary"),
    )(h, *([target] * per_step))


def _mlp_bwd_hidden(dh2, f, g2, act, wd, tm, deps=()):
    t = dh2.shape[0]
    nj = D_FF // FF_CHUNK

    def body(dh2_ref, f_ref, g2_ref, act_ref, wd_ref, df_ref, dup_ref, dg2_ref):
        @pl.when(pl.program_id(0) == 0)
        def _():
            dg2_ref[...] = jnp.zeros_like(dg2_ref)

        df, dg = _rms_bwd(dh2_ref[...], f_ref[...], g2_ref[...])
        dg2_ref[...] += dg
        df = df.astype(BF16)
        df_ref[...] = df
        for j in range(nj):
            cols = slice(j * FF_CHUNK, (j + 1) * FF_CHUNK)
            dact = lax.dot_general(df, wd_ref[j], (((1,), (1,)), ((), ())), preferred_element_type=F32)
            dup_ref[:, cols] = (dact * (2.0 * jnp.sqrt(act_ref[:, cols].astype(F32)))).astype(BF16)

    row = pl.BlockSpec((tm, D_MODEL), lambda i: (i, 0))
    wide = pl.BlockSpec((tm, D_FF), lambda i: (i, 0))
    vec = pl.BlockSpec((1, D_MODEL), lambda i: (0, 0))
    body, dep_specs = _behind(body, deps)
    return pl.pallas_call(
        body, name="mlp_bwd_hidden", grid=(t // tm,),
        in_specs=dep_specs + [row, row, vec, wide, pl.BlockSpec(memory_space=pltpu.VMEM)],
        out_specs=[row, wide, vec],
        out_shape=[jax.ShapeDtypeStruct((t, D_MODEL), BF16), jax.ShapeDtypeStruct((t, D_FF), BF16),
                   jax.ShapeDtypeStruct((1, D_MODEL), F32)],
        compiler_params=_params("arbitrary"),
    )(*deps, dh2, f, g2, act, wd)


def _mlp_bwd_input(dup, wu, h1, g1, dh2, tm):
    t = dh2.shape[0]
    nj = D_FF // FF_CHUNK

    def body(dup_ref, wu_ref, h1_ref, g1_ref, dh2_ref, dh1_ref, dg1_ref):
        @pl.when(pl.program_id(0) == 0)
        def _():
            dg1_ref[...] = jnp.zeros_like(dg1_ref)

        da2 = None
        for j in range(nj):
            part = lax.dot_general(dup_ref[:, j * FF_CHUNK:(j + 1) * FF_CHUNK], wu_ref[j], (((1,), (1,)), ((), ())),
                                   preferred_element_type=F32)
            da2 = part if da2 is None else da2 + part
        dx, dg = _rms_bwd(da2, h1_ref[...], g1_ref[...])
        dh1_ref[...] = dh2_ref[...] + dx
        dg1_ref[...] += dg

    row = pl.BlockSpec((tm, D_MODEL), lambda i: (i, 0))
    vec = pl.BlockSpec((1, D_MODEL), lambda i: (0, 0))
    return pl.pallas_call(
        body, name="mlp_bwd_input", grid=(t // tm,),
        in_specs=[pl.BlockSpec((tm, D_FF), lambda i: (i, 0)), pl.BlockSpec(memory_space=pltpu.VMEM), row, vec, row],
        out_specs=[row, vec],
        out_shape=[jax.ShapeDtypeStruct((t, D_MODEL), F32), jax.ShapeDtypeStruct((1, D_MODEL), F32)],
        compiler_params=_params("arbitrary"),
    )(dup, wu, h1, g1, dh2)


def _row_split(t):
    tile = min(t, 1024)
    return tile, t // tile, t % tile


def _row_split_specs(t, cols, col_of):
    tile, whole, rest = _row_split(t)
    specs = [pl.BlockSpec((tile, cols), lambda *g: (jnp.minimum(g[-1], whole - 1), col_of(*g[:-1])))]
    if rest:
        specs.append(pl.BlockSpec((rest, cols), lambda *g: (whole * tile // rest, col_of(*g[:-1]))))
    return specs


def _weight_grad(x, y, name):
    t, k = x.shape
    n = y.shape[1]
    tn = FF_CHUNK
    tk = FF_CHUNK if k % FF_CHUNK == 0 else k
    _, whole, rest = _row_split(t)
    steps = whole + bool(rest)
    one_tile = k == tk and n == tn

    def body(*refs):
        o_ref, ob_ref, r = refs[-2], refs[-1], pl.program_id(2)
        if one_tile:
            o_ref, ob_ref = o_ref.at[0, 0], ob_ref.at[0, 0]

        @pl.when(r == 0)
        def _():
            o_ref[...] = jnp.zeros_like(o_ref)

        def add(x_ref, y_ref):
            o_ref[...] += lax.dot_general(x_ref[...], y_ref[...], (((0,), (0,)), ((), ())),
                                          preferred_element_type=F32)

        if rest:
            pl.when(r < whole)(lambda: add(refs[0], refs[2]))
            pl.when(r == whole)(lambda: add(refs[1], refs[3]))
        else:
            add(refs[0], refs[1])

        @pl.when(r == steps - 1)
        def _():
            ob_ref[...] = o_ref[...].astype(BF16)

    tile = pl.BlockSpec((None, None, tk, tn), lambda a, b, r: (a, b, 0, 0))
    if one_tile:
        tile = pl.BlockSpec(memory_space=pltpu.VMEM)
    return pl.pallas_call(
        body, name=name, grid=(k // tk, n // tn, steps),
        in_specs=_row_split_specs(t, tk, lambda a, b: a) + _row_split_specs(t, tn, lambda a, b: b),
        out_specs=[tile, tile],
        out_shape=[jax.ShapeDtypeStruct((k // tk, n // tn, tk, tn), F32),
                   jax.ShapeDtypeStruct((k // tk, n // tn, tk, tn), BF16)],
        compiler_params=_params("parallel", "parallel", "arbitrary"),
    )(*([x] * (1 + bool(rest))), *([y] * (1 + bool(rest))))


def _mix_out_bwd(dh1, z, gp, w, o, b, c, hc, cw, ga, gc, tm, deps=()):
    t = dh1.shape[0]
    nt = t // tm
    per16 = tm // 16

    def body(dh1_ref, z_ref, gp_ref, w_ref, o_ref, b_ref, c_ref, hc_ref, cp_ref, hp_ref, cw_ref, ga_ref, gc_ref,
             dz_ref, do_ref, dbch_ref, dgp_ref, dga_ref, dgc_ref, dcw_ref, halo):
        i = pl.program_id(0)

        @pl.when(i == 0)
        def _():
            halo[...] = jnp.zeros_like(halo)
            dgp_ref[...] = jnp.zeros_like(dgp_ref)
            dga_ref[...] = jnp.zeros_like(dga_ref)
            dgc_ref[...] = jnp.zeros_like(dgc_ref)
            dcw_ref[...] = jnp.zeros_like(dcw_ref)

        dz, dgp = _rms_bwd(dh1_ref[...], z_ref[...], gp_ref[...])
        dgp_ref[...] += dgp
        dz = dz.astype(BF16)
        dz_ref[...] = dz
        dy = lax.dot_general(dz, w_ref[...].reshape(D_MODEL, D_MODEL), (((1,), (1,)), ((), ())),
                             preferred_element_type=F32)
        do, dga = _rms_bwd(dy[:, :ATTN_W], o_ref[...].astype(F32), ga_ref[...])
        do_ref[...] = do.astype(BF16)
        dga_ref[...] += dga

        cc, hh = c_ref[...].astype(F32), hc_ref[...].astype(F32)
        u = cc * hh
        first = i == nt - 1
        u_before = jnp.where(first, 0.0, (cp_ref[...].astype(F32) * hp_ref[...].astype(F32))[8:])
        u1 = _shift_rows(u, u_before, 1)
        u2 = _shift_rows(u, u_before, 2)
        cv = cw_ref[0:1, :] * u2 + cw_ref[1:2, :] * u1 + cw_ref[2:3, :] * u
        bb = b_ref[...].astype(F32)
        dyc, dgc = _rms_bwd(dy[:, ATTN_W:], bb * cv, gc_ref[...])
        dgc_ref[...] += dgc
        dcv = dyc * bb
        d1 = _advance_rows(dcv, halo[...], 1)
        d2 = _advance_rows(dcv, halo[...], 2)
        halo[...] = dcv[:8]
        du = cw_ref[2:3, :] * dcv + cw_ref[1:2, :] * d1 + cw_ref[0:1, :] * d2
        dbch_ref[...] = jnp.concatenate([dyc * cv, du * hh, du * cc], axis=1).astype(BF16)
        dcw_ref[...] += jnp.concatenate([jnp.sum(dcv * u2, axis=0, keepdims=True),
                                         jnp.sum(dcv * u1, axis=0, keepdims=True),
                                         jnp.sum(dcv * u, axis=0, keepdims=True)], axis=0)

    row = lambda n: pl.BlockSpec((tm, n), lambda i: (nt - 1 - i, 0))
    before = pl.BlockSpec((16, CONV_W), lambda i: (jnp.maximum((nt - 1 - i) * per16 - 1, 0), 0))
    full = lambda a: pl.BlockSpec(a.shape, lambda i: (0,) * a.ndim)
    vec = lambda n: pl.BlockSpec((1, n), lambda i: (0, 0))
    body, dep_specs = _behind(body, deps)
    return pl.pallas_call(
        body, name="mix_out_bwd", grid=(nt,),
        in_specs=dep_specs + [row(D_MODEL), row(D_MODEL), full(gp), full(w), row(ATTN_W), row(CONV_W), row(CONV_W),
                              row(CONV_W), before, before, full(cw), full(ga), full(gc)],
        out_specs=[row(D_MODEL), row(ATTN_W), row(3 * CONV_W), vec(D_MODEL), vec(ATTN_W), vec(CONV_W),
                   pl.BlockSpec((CONV_K, CONV_W), lambda i: (0, 0))],
        out_shape=[jax.ShapeDtypeStruct((t, D_MODEL), BF16), jax.ShapeDtypeStruct((t, ATTN_W), BF16),
                   jax.ShapeDtypeStruct((t, 3 * CONV_W), BF16), jax.ShapeDtypeStruct((1, D_MODEL), F32),
                   jax.ShapeDtypeStruct((1, ATTN_W), F32), jax.ShapeDtypeStruct((1, CONV_W), F32),
                   jax.ShapeDtypeStruct((CONV_K, CONV_W), F32)],
        scratch_shapes=[pltpu.VMEM((8, CONV_W), F32)],
        compiler_params=_params("arbitrary"),
    )(*deps, dh1, z, gp, w, o, b, c, hc, c, hc, cw, ga, gc)


def _attn_bwd(q, k, v, o, do, bias, sinks, tm, deps=()):
    t = q.shape[0]
    per_step = tm // BLOCK

    def body(s_ref, q_ref, k_ref, v_ref, o_ref, do_ref, bias_ref, dq_ref, dk_ref, dv_ref, ds_ref):
        step = pl.program_id(0)

        @pl.when(step == 0)
        def _():
            ds_ref[...] = jnp.zeros_like(ds_ref)

        heads = range(N_Q_HEADS)

        def first_matmuls(b):
            i = step * per_step + b
            rows = slice(b * BLOCK, (b + 1) * BLOCK)
            kc, vc = _two_blocks(k_ref, i), _two_blocks(v_ref, i)
            bias_i = bias_ref[jnp.minimum(i, 2)]
            kgs = [kc[:, _head(g)] for g in range(N_KV_HEADS)]
            vgs = [vc[:, _head(g)] for g in range(N_KV_HEADS)]
            qs = [q_ref[rows, _head(hh)] for hh in heads]
            dosb = [do_ref[rows, _head(hh)] for hh in heads]
            dos = [d.astype(F32) for d in dosb]
            scores = [_attn_scores(qs[hh], kgs[hh // GROUP], bias_i) for hh in heads]
            dps = [lax.dot_general(dosb[hh], vgs[hh // GROUP], (((1,), (1,)), ((), ())), preferred_element_type=F32)
                   for hh in heads]
            return kgs, qs, dos, dosb, scores, dps

        dsink = [jnp.zeros((BLOCK, 1), F32) for _ in range(N_Q_HEADS)]
        ahead = None
        for b in range(per_step):
            i = step * per_step + b
            rows = slice(b * BLOCK, (b + 1) * BLOCK)
            kgs, qs, dos, dosb, scores, dps = first_matmuls(b)
            ps, dss = [], []
            for hh in heads:
                p, share = _attn_probs(scores[hh], s_ref[hh])
                drow = jnp.sum(dos[hh] * o_ref[rows, _head(hh)].astype(F32), axis=-1, keepdims=True)
                dss.append((p * (dps[hh] - drow)).astype(BF16))
                ps.append(p.astype(BF16))
                dsink[hh] = dsink[hh] + share * drow
            for hh in heads:
                dq_ref[rows, _head(hh)] = (jnp.dot(dss[hh], kgs[hh // GROUP], preferred_element_type=F32)
                                           * SCALE).astype(BF16)
            groups = [slice(GROUP * g, GROUP * (g + 1)) for g in range(N_KV_HEADS)]
            dkg = [lax.dot_general(jnp.concatenate(dss[gr], axis=0), jnp.concatenate(qs[gr], axis=0),
                                   (((0,), (0,)), ((), ())), preferred_element_type=F32) for gr in groups]
            dvg = [lax.dot_general(jnp.concatenate(ps[gr], axis=0), jnp.concatenate(dosb[gr], axis=0),
                                   (((0,), (0,)), ((), ())), preferred_element_type=F32) for gr in groups]
            dkb, dvb = jnp.concatenate(dkg, axis=1), jnp.concatenate(dvg, axis=1)
            if b == 0:
                @pl.when(step > 0)
                def _():
                    before = pl.ds(pl.multiple_of((i - 1) * BLOCK, BLOCK), BLOCK)
                    dk_ref[before, :] += dkb[:BLOCK]
                    dv_ref[before, :] += dvb[:BLOCK]
            else:
                at = pl.ds(pl.multiple_of((i - 1) * BLOCK, BLOCK), BLOCK)
                dk_ref[at, :] = ahead[0] + dkb[:BLOCK]
                dv_ref[at, :] = ahead[1] + dvb[:BLOCK]
            ahead = (dkb[BLOCK:], dvb[BLOCK:])
        last = pl.ds(pl.multiple_of(((step + 1) * per_step - 1) * BLOCK, BLOCK), BLOCK)
        dk_ref[last, :] = ahead[0]
        dv_ref[last, :] = ahead[1]
        for hh in range(N_Q_HEADS):
            ds_ref[hh:hh + 1, :] -= jnp.sum(dsink[hh])

    whole = pl.BlockSpec((t, KV_W), lambda i: (0, 0))
    blk = pl.BlockSpec((tm, ATTN_W), lambda i: (i, 0))
    body, dep_specs = _behind(body, deps)
    return pl.pallas_call(
        body, name="attn_bwd", grid=(t // tm,),
        in_specs=dep_specs + [pl.BlockSpec(memory_space=pltpu.SMEM), blk, whole, whole, blk, blk,
                              pl.BlockSpec(bias.shape, lambda i: (0, 0, 0))],
        out_specs=[blk, whole, whole, pl.BlockSpec((N_Q_HEADS, 128), lambda i: (0, 0))],
        out_shape=[jax.ShapeDtypeStruct((t, ATTN_W), BF16), jax.ShapeDtypeStruct((t, KV_W), F32),
                   jax.ShapeDtypeStruct((t, KV_W), F32), jax.ShapeDtypeStruct((N_Q_HEADS, 128), F32)],
        compiler_params=_params("arbitrary"),
    )(*deps, sinks, q, k, v, o, do, bias)


def _in_proj_bwd(dq, dk, dv, dbch, w, dh1, h, g, tabs, tm):
    t = h.shape[0]

    def body(dq_ref, dk_ref, dv_ref, dbch_ref, w_ref, dh1_ref, h_ref, g_ref, c_ref, sa_ref, sb_ref, dh_ref, dp_ref,
             dg_ref):
        @pl.when(pl.program_id(0) == 0)
        def _():
            dg_ref[...] = jnp.zeros_like(dg_ref)

        cos, sa, sb = c_ref[...], sa_ref[...], sb_ref[...]
        rep = ATTN_W // (2 * HEAD_DIM)
        dqr = _rope_bwd(dq_ref[...].astype(F32), jnp.tile(cos, (1, rep)), jnp.tile(sa, (1, rep)),
                        jnp.tile(sb, (1, rep)))
        dkr = _rope_bwd(dk_ref[...], cos, sa, sb)
        dp = jnp.concatenate([dqr.astype(BF16), dkr.astype(BF16), dv_ref[...].astype(BF16), dbch_ref[...]], axis=1)
        dp_ref[...] = dp
        da = jnp.dot(dp, w_ref[...], preferred_element_type=F32)
        dx, dg = _rms_bwd(da, h_ref[...], g_ref[...])
        dh_ref[...] = dh1_ref[...] + dx
        dg_ref[...] += dg

    row = lambda n: pl.BlockSpec((tm, n), lambda i: (i, 0))
    full = lambda a: pl.BlockSpec(a.shape, lambda i: (0, 0))
    return pl.pallas_call(
        body, name="in_proj_bwd", grid=(t // tm,),
        in_specs=[row(ATTN_W), row(KV_W), row(KV_W), row(3 * CONV_W), full(w), row(D_MODEL), row(D_MODEL), full(g),
                  row(2 * HEAD_DIM), row(2 * HEAD_DIM), row(2 * HEAD_DIM)],
        out_specs=[row(D_MODEL), row(IN_W), pl.BlockSpec((1, D_MODEL), lambda i: (0, 0))],
        out_shape=[jax.ShapeDtypeStruct((t, D_MODEL), F32), jax.ShapeDtypeStruct((t, IN_W), BF16),
                   jax.ShapeDtypeStruct((1, D_MODEL), F32)],
        compiler_params=_params("arbitrary"),
    )(dq, dk, dv, dbch, w, dh1, h, g, *tabs)


class _Tiles:
    def __init__(self, t):
        self.tm = _row_tile(t, 640)
        self.ts = self.tm
        self.tabs = _rope_tables(t)
        self.bias = _attn_bias()


def _mixer_fwd(h, p, tl):
    a, q, k, v, b, c, hc = _in_proj(h, p["mix_pre_g"], p["w_in"], tl.tabs, tl.ts)
    o = _attn_fwd(q, k, v, tl.bias, p["sinks"], tl.tm)
    return (h, a, q, k, v, b, c, hc, o)


def _out_fwd(mixed, p, tl, deps=()):
    h, a, q, k, v, b, c, hc, o = mixed
    h1, y, z = _mix_out(h, o, b, c, hc, p["conv_w"], p["attn_out_g"], p["conv_out_g"], p["w_out"], p["mix_post_g"],
                        tl.ts, deps)
    return h1, mixed + (h1, y, z)


def _mlp_fwd(h1, saved, p, tl):
    h2, a2, act, f = _mlp(h1, p["mlp_pre_g"], p["w_up"], p["w_down"], p["mlp_post_g"], tl.tm)
    return h2, saved + (a2, act, f)


def _mlp_part_bwd(dh, saved, p, tl, deps=()):
    h1, a2, act, f = saved[9], saved[12], saved[13], saved[14]
    df, dup, dg2 = _mlp_bwd_hidden(dh, f, p["mlp_post_g"], act, p["w_down"], tl.tm, deps)
    dh1, dg1 = _mlp_bwd_input(dup, p["w_up"], h1, p["mlp_pre_g"], dh, tl.tm)
    g = {"w_down": [d.reshape(N_CHIPS, FF_CHUNK, D_MODEL) for d in _weight_grad(act, df, "grad_w_down")],
         "w_up": [d.reshape(N_CHIPS, D_MODEL, FF_CHUNK) for d in _weight_grad(a2, dup, "grad_w_up")],
         "mlp_post_g": dg2, "mlp_pre_g": dg1}
    return dh1, g


def _mix_out_part_bwd(dh1, saved, p, tl, deps=()):
    b, c, hc, o, y, z = saved[5], saved[6], saved[7], saved[8], saved[10], saved[11]
    dz, do, dbch, dgp, dga, dgc, dcw = _mix_out_bwd(dh1, z, p["mix_post_g"], p["w_out"], o, b, c, hc, p["conv_w"],
                                                    p["attn_out_g"], p["conv_out_g"], tl.ts, deps)
    g = {"w_out": [d.reshape(N_CHIPS, D_MODEL // N_CHIPS, D_MODEL) for d in _weight_grad(y, dz, "grad_w_out")],
         "mix_post_g": dgp, "attn_out_g": dga, "conv_out_g": dgc, "conv_w": dcw}
    return (dh1, do, dbch), g


def _attn_in_part_bwd(carry, saved, p, tl, deps=()):
    dh1, do, dbch = carry
    h_in, a, q, k, v, o = saved[0], saved[1], saved[2], saved[3], saved[4], saved[8]
    dq, dk, dv, dsink = _attn_bwd(q, k, v, o, do, tl.bias, p["sinks"], tl.tm, deps)
    dh, dproj, dgi = _in_proj_bwd(dq, dk, dv, dbch, p["w_in"], dh1, h_in, p["mix_pre_g"], tl.tabs, tl.ts)
    g_in = [d.reshape(N_CHIPS, IN_W // N_CHIPS, D_MODEL) for d in _weight_grad(dproj, a, "grad_w_in")]
    return dh, {"w_in": g_in, "mix_pre_g": dgi, "sinks": dsink[:, 0]}


def _place():
    return lax.axis_index("x"), lax.axis_index("y"), lax.axis_index("c")


def _other_chips(x, y):
    return [(1 - x, y), (x, 1 - y), (1 - x, 1 - y)]


_HBM = pl.BlockSpec(memory_space=pltpu.HBM)
_SEM = pl.BlockSpec(memory_space=pltpu.SEMAPHORE)
_EFFECT = pltpu.SideEffectType.DATAFLOW_SIDE_EFFECTING


class _Exchange:
    def __init__(self, name, bufs, plan, n, after=()):
        self.name, self.plan, nb = name, plan, len(bufs)
        n_in = nb + len(after)

        def body(*refs):
            send, recv, token = refs[n_in], refs[n_in + 1], refs[-1]
            for k, (src, dst, target, _) in enumerate(plan(refs[:nb])):
                pltpu.make_async_remote_copy(src_ref=src, dst_ref=dst, send_sem=send.at[k], recv_sem=recv.at[k],
                                             device_id=target, device_id_type=MESH).start()
            token[...] = jnp.zeros_like(token)

        outs = pl.pallas_call(
            body, name=name + "_start",
            out_shape=(pltpu.SemaphoreType.DMA((n,)), pltpu.SemaphoreType.DMA((n,)),
                       *[pltpu.HBM(b.shape, b.dtype) for b in bufs], jax.ShapeDtypeStruct((8, 128), F32)),
            in_specs=[_HBM] * nb + [pl.BlockSpec(memory_space=pl.ANY)] * len(after),
            out_specs=(_SEM, _SEM, *[_HBM] * nb, pl.BlockSpec(memory_space=pltpu.VMEM)),
            input_output_aliases={i: 2 + i for i in range(nb)},
            compiler_params=pltpu.CompilerParams(has_side_effects=_EFFECT),
        )(*[pltpu.with_memory_space_constraint(b, pltpu.HBM) for b in bufs], *after)
        self.send, self.recv, self.bufs, self.token = outs[0], outs[1], list(outs[2:2 + nb]), outs[-1]

    def wait(self, *after):
        plan, nb = self.plan, len(self.bufs)

        def body(*refs):
            send, recv = refs[nb], refs[nb + 1]
            for k, (src, _, target, land) in enumerate(plan(refs[:nb])):
                cp = pltpu.make_async_remote_copy(src_ref=src, dst_ref=land, send_sem=send.at[k], recv_sem=recv.at[k],
                                                  device_id=target, device_id_type=MESH)
                cp.wait_send()
                cp.wait_recv()

        outs = pl.pallas_call(
            body, name=self.name + "_wait", out_shape=[pltpu.HBM(b.shape, b.dtype) for b in self.bufs],
            in_specs=[_HBM] * nb + [_SEM, _SEM] + [pl.BlockSpec(memory_space=pl.ANY)] * len(after),
            out_specs=[_HBM] * nb, input_output_aliases={i: i for i in range(nb)},
            compiler_params=pltpu.CompilerParams(has_side_effects=_EFFECT),
        )(*self.bufs, self.send, self.recv, *after)
        return list(outs)


def _gather_plan(n):
    def plan(refs):
        x, y, c = _place()
        me = 2 * x + y
        return [(refs[a].at[me], refs[a].at[me], (px, py, c), refs[a].at[2 * px + py])
                for a in range(n) for px, py in _other_chips(x, y)]

    return plan


def _peers():
    x, y, c = _place()
    return [(k - 1, (x ^ (k >> 2), y ^ ((k >> 1) & 1), c ^ (k & 1))) for k in range(1, N_DEV)]


def _scatter_plan(n, half_rows):
    def plan(refs):
        out = []
        for a in range(n):
            hr = half_rows[a]
            for k, (px, py, pc) in _peers():
                out.append((refs[a].at[2 * px + py, pl.ds(pc * hr, hr)], refs[n + a].at[k], (px, py, pc),
                            refs[n + a].at[k]))
        return out

    return plan


def _join_plan(n):
    def plan(refs):
        x, y, c = _place()
        return [(refs[a].at[c], refs[a].at[c], (x, y, 1 - c), refs[a].at[1 - c]) for a in range(n)]

    return plan


def _sum_parts(g, q):
    rows, cols = g.shape[1], g.shape[2]
    hr = rows // 2
    tr = _block_rows(hr)
    per = hr // tr
    x, y, c = _place()
    where = jnp.stack([2 * x + y, c]).astype(jnp.int32)

    def body(where_ref, g_ref, q_ref, o_ref):
        total = g_ref[...]
        for k in range(N_DEV - 1):
            total = total + q_ref[k].astype(F32)
        o_ref[...] = total

    return pl.pallas_call(
        body, name="sum_parts",
        grid_spec=pltpu.PrefetchScalarGridSpec(
            num_scalar_prefetch=1, grid=(per,),
            in_specs=[pl.BlockSpec((None, tr, cols), lambda i, where_ref: (where_ref[0], where_ref[1] * per + i, 0)),
                      pl.BlockSpec((N_DEV - 1, tr, cols), lambda i, where_ref: (0, i, 0))],
            out_specs=pl.BlockSpec((None, tr, cols), lambda i, where_ref: (where_ref[1], i, 0))),
        out_shape=jax.ShapeDtypeStruct((2, hr, cols), F32),
        compiler_params=_params("parallel"),
    )(where, g, q)


def _sum_devices(packed):
    def body(p_ref, o_ref, land, send_sems, recv_sems):
        x, y, c = _place()
        me = 4 * x + 2 * y + c
        land[me] = p_ref[...]
        sends = []
        for k in range(1, N_DEV):
            px, py, pc = x ^ (k >> 2), y ^ ((k >> 1) & 1), c ^ (k & 1)
            cp = pltpu.make_async_remote_copy(src_ref=p_ref, dst_ref=land.at[me], send_sem=send_sems.at[k - 1],
                                              recv_sem=recv_sems.at[k - 1], device_id=(px, py, pc), device_id_type=MESH)
            cp.start()
            sends.append(cp)
        for k in range(1, N_DEV):
            px, py, pc = x ^ (k >> 2), y ^ ((k >> 1) & 1), c ^ (k & 1)
            pltpu.make_async_remote_copy(src_ref=p_ref, dst_ref=land.at[4 * px + 2 * py + pc],
                                         send_sem=send_sems.at[k - 1], recv_sem=recv_sems.at[k - 1],
                                         device_id=(px, py, pc), device_id_type=MESH).wait_recv()
        for cp in sends:
            cp.wait_send()
        total = land[0]
        for d in range(1, N_DEV):
            total = total + land[d]
        o_ref[...] = total

    vm = pl.BlockSpec(memory_space=pltpu.VMEM)
    return pl.pallas_call(
        body, name="sum_devices", in_specs=[vm], out_specs=vm,
        out_shape=jax.ShapeDtypeStruct(packed.shape, F32),
        scratch_shapes=[pltpu.VMEM((N_DEV,) + packed.shape, F32), pltpu.SemaphoreType.DMA((N_DEV - 1,)),
                        pltpu.SemaphoreType.DMA((N_DEV - 1,))],
    )(packed)


def _adamw_math(w, g, m, v):
    m = ADAM_B1 * m + (1.0 - ADAM_B1) * g
    v = ADAM_B2 * v + (1.0 - ADAM_B2) * jnp.square(g)
    m_hat = m / (1.0 - ADAM_B1 ** ADAM_STEP)
    v_hat = v / (1.0 - ADAM_B2 ** ADAM_STEP)
    delta = -ADAM_LR * (m_hat / (jnp.sqrt(v_hat) + ADAM_EPS) + ADAM_WD * w)
    return delta, m, v


def _adamw_large(layer, w, halves, m, v, other):
    _, rows, cols = w.shape
    tr = _block_rows(rows // 2)
    per = rows // 2 // tr

    def body(w_ref, g_ref, m_ref, v_ref, *rest):
        g_out, d_ref, nm_ref, nv_ref = rest[-4:]
        g = g_ref[...]
        g_out[...] = g
        d_ref[...], nm_ref[...], nv_ref[...] = _adamw_math(w_ref[...], g, m_ref[...], v_ref[...])

    blk = pl.BlockSpec((None, tr, cols), lambda i: (layer, i, 0))
    half = pl.BlockSpec((None, tr, cols), lambda i: (i // per, i % per, 0))
    kept = [] if other is None else list(other)
    return pl.pallas_call(
        body, name="adamw_large", grid=(rows // tr,),
        in_specs=[blk, half, blk, blk] + [pl.BlockSpec(memory_space=pl.ANY)] * len(kept), out_specs=[blk] * 4,
        out_shape=[jax.ShapeDtypeStruct(w.shape, F32)] * 4,
        input_output_aliases={4 + k: k for k in range(len(kept))},
        compiler_params=_params("parallel"),
    )(w, halves, m, v, *kept)


def _adamw_small(ws, gs, ms, vs):
    n = len(ws)

    def body(*refs):
        w_r, g_r, m_r, v_r = refs[:n], refs[n:2 * n], refs[2 * n:3 * n], refs[3 * n:4 * n]
        d_r, nm_r, nv_r = refs[4 * n:5 * n], refs[5 * n:6 * n], refs[6 * n:]
        for a in range(n):
            d_r[a][...], nm_r[a][...], nv_r[a][...] = _adamw_math(w_r[a][...], g_r[a][...], m_r[a][...], v_r[a][...])

    vm = pl.BlockSpec(memory_space=pltpu.VMEM)
    outs = pl.pallas_call(
        body, name="adamw_small", in_specs=[vm] * (4 * n), out_specs=[vm] * (3 * n),
        out_shape=[jax.ShapeDtypeStruct(w.shape, F32) for w in ws] * 3,
    )(*ws, *gs, *ms, *vs)
    return outs[:n], outs[n:2 * n], outs[2 * n:]


_LARGE = ("w_in", "w_out", "w_up", "w_down")
_SMALL = ("meta_tokens", "mix_pre_g", "conv_w", "sinks", "attn_out_g", "conv_out_g", "mix_post_g", "mlp_pre_g",
          "mlp_post_g")
_ORDER = ("meta_tokens", "mix_pre_g", "w_in", "conv_w", "sinks", "attn_out_g", "conv_out_g", "w_out", "mix_post_g",
          "mlp_pre_g", "w_up", "w_down", "mlp_post_g")


class _Reduce:
    def __init__(self, name, grads, after=()):
        self.name, self.n = name, len(grads)
        self.own = [g for g, _ in grads]
        half_rows = [g.shape[1] // 2 for g in self.own]
        zones = [lax.empty((N_DEV - 1, hr, g.shape[2]), BF16) for g, hr in zip(self.own, half_rows)]
        self.exchange = _Exchange(name + "_scatter", [b for _, b in grads] + zones, _scatter_plan(self.n, half_rows),
                                  (N_DEV - 1) * self.n, after)

    @property
    def token(self):
        return self.exchange.token

    def join(self, *after):
        bufs = self.exchange.wait(*after)
        halves = [_sum_parts(g, q) for g, q in zip(self.own, bufs[self.n:])]
        self.exchange = _Exchange(self.name + "_join", halves, _join_plan(self.n), self.n)

    def done(self, *after):
        return self.exchange.wait(*after)


def _pad_cols(a, n=D_MODEL):
    return jnp.pad(a, ((0, 0), (0, n - a.shape[1])))


def kernel(x, meta_tokens, mix_pre_g, w_in, conv_w, sinks, attn_out_g, conv_out_g, w_out, mix_post_g, mlp_pre_g, w_up, w_down, mlp_post_g, loss_target, m_meta_tokens, m_mix_pre_g, m_w_in, m_conv_w, m_sinks, m_attn_out_g, m_conv_out_g, m_w_out, m_mix_post_g, m_mlp_pre_g, m_w_up, m_w_down, m_mlp_post_g, v_meta_tokens, v_mix_pre_g, v_w_in, v_conv_w, v_sinks, v_attn_out_g, v_conv_out_g, v_w_out, v_mix_post_g, v_mlp_pre_g, v_w_up, v_w_down, v_mlp_post_g):
    w = dict(meta_tokens=meta_tokens, mix_pre_g=mix_pre_g, w_in=w_in, conv_w=conv_w, sinks=sinks,
             attn_out_g=attn_out_g, conv_out_g=conv_out_g, w_out=w_out, mix_post_g=mix_post_g, mlp_pre_g=mlp_pre_g,
             w_up=w_up, w_down=w_down, mlp_post_g=mlp_post_g)
    m = dict(meta_tokens=m_meta_tokens, mix_pre_g=m_mix_pre_g, w_in=m_w_in, conv_w=m_conv_w, sinks=m_sinks,
             attn_out_g=m_attn_out_g, conv_out_g=m_conv_out_g, w_out=m_w_out, mix_post_g=m_mix_post_g,
             mlp_pre_g=m_mlp_pre_g, w_up=m_w_up, w_down=m_w_down, mlp_post_g=m_mlp_post_g)
    v = dict(meta_tokens=v_meta_tokens, mix_pre_g=v_mix_pre_g, w_in=v_w_in, conv_w=v_conv_w, sinks=v_sinks,
             attn_out_g=v_attn_out_g, conv_out_g=v_conv_out_g, w_out=v_w_out, mix_post_g=v_mix_post_g,
             mlp_pre_g=v_mlp_pre_g, w_up=v_w_up, w_down=v_w_down, mlp_post_g=v_mlp_post_g)
    chip = 2 * lax.axis_index("x") + lax.axis_index("y")
    tl = _Tiles(x.shape[1] + BLOCK)

    def zone(quarter):
        return lax.dynamic_update_slice(lax.empty((N_CHIPS,) + quarter.shape, quarter.dtype), quarter[None],
                                        (chip,) + (0,) * quarter.ndim)

    w, m, v = ({**d, "w_in": jnp.swapaxes(d["w_in"], 1, 2)} for d in (w, m, v))
    zones = {n: [zone(w[n][l].astype(BF16)) for l in range(DEPTH)] for n in _LARGE}
    first = _Exchange("gather_first", [zones["w_in"][0], zone(w["conv_w"]), zone(w["meta_tokens"])], _gather_plan(3), 9)
    out0 = _Exchange("gather_out", [zones["w_out"][0]], _gather_plan(1), 3, [first.token])
    rest = _Exchange("gather_rest", [zones[n][0] for n in ("w_up", "w_down")], _gather_plan(2), 6, [out0.token])

    def whole_in(quarters):
        return quarters.reshape(IN_W, D_MODEL)

    h = jnp.concatenate([jnp.zeros((BLOCK, D_MODEL), F32), x[0]], axis=0)
    q_in, q_conv, q_meta = first.wait(rest.token, *tl.tabs, tl.bias, h)
    conv_whole = jnp.transpose(q_conv, (1, 2, 0, 3)).reshape(DEPTH, CONV_K, CONV_W)
    meta = jnp.transpose(q_meta, (1, 0, 2)).reshape(N_META, D_MODEL)
    p = [{"conv_w": conv_whole[l], "sinks": w["sinks"][l]} for l in range(DEPTH)]
    for l in range(DEPTH):
        for n in ("mix_pre_g", "attn_out_g", "conv_out_g", "mix_post_g", "mlp_pre_g", "mlp_post_g"):
            p[l][n] = w[n][l][None, :]

    h = lax.dynamic_update_slice(h, meta, (LEAD_PAD, 0))
    p[0]["w_in"] = whole_in(q_in)
    mixed = _mixer_fwd(h, p[0], tl)
    second = _Exchange("gather_second", [zones["w_in"][1], zones["w_out"][1]], _gather_plan(2), 6, [mixed[-1]])
    second_mlp = _Exchange("gather_second_mlp", [zones["w_up"][1], zones["w_down"][1]], _gather_plan(2), 6,
                           [second.token])
    p[0]["w_out"], = out0.wait(second_mlp.token)
    h1, saved0 = _out_fwd(mixed, p[0], tl)
    p[0]["w_up"], p[0]["w_down"] = rest.wait(h1)
    h, saved0 = _mlp_fwd(h1, saved0, p[0], tl)
    q_in, p[1]["w_out"] = second.wait(h)
    p[1]["w_in"] = whole_in(q_in)
    h1, saved1 = _out_fwd(_mixer_fwd(h, p[1], tl), p[1], tl)
    p[1]["w_up"], p[1]["w_down"] = second_mlp.wait(h1)
    h, saved1 = _mlp_fwd(h1, saved1, p[1], tl)
    loss_tile, dh = _loss_head(h, loss_target[0], tl.tm)

    def adamw(layer, halves, other):
        return {n: _adamw_large(layer, w[n], halves[n], m[n], v[n], None if other is None else other[n])
                for n in halves}

    dh1, g1 = _mlp_part_bwd(dh, saved1, p[1], tl)
    carry, gm = _mix_out_part_bwd(dh1, saved1, p[1], tl)
    dh, gi = _attn_in_part_bwd(carry, saved1, p[1], tl)
    g1.update(gm, **gi)
    red1 = _Reduce("reduce1", [g1[n] for n in _LARGE])
    dh1, g0 = _mlp_part_bwd(dh, saved0, p[0], tl, [red1.token])
    red1.join(g0["w_down"][0])
    carry, gm = _mix_out_part_bwd(dh1, saved0, p[0], tl, [red1.token])
    first0 = ("w_up", "w_down", "w_out")
    g0.update(gm)
    red0a = _Reduce("reduce0a", [g0[n] for n in first0])
    dh0, gi = _attn_in_part_bwd(carry, saved0, p[0], tl, [red0a.token])
    g0.update(gi)
    red0b = _Reduce("reduce0b", [g0["w_in"]])
    grad_x = dh0[BLOCK:][None]
    grads = {n: [g0[n], g1[n]] for n in g0 if n not in _LARGE}

    rows = [dh0[LEAD_PAD:BLOCK]]
    for n in ("mix_pre_g", "mix_post_g", "mlp_pre_g", "mlp_post_g"):
        rows += grads[n]
    rows += [jnp.concatenate([grads["attn_out_g"][l], grads["conv_out_g"][l]], axis=1) for l in range(DEPTH)]
    rows.append(jnp.concatenate(grads["conv_w"], axis=1))
    rows.append(_pad_cols(jnp.concatenate(grads["sinks"])[None, :]))
    rows.append(_pad_cols(loss_tile[:1]))
    packed = jnp.concatenate(rows, axis=0)
    packed = jnp.pad(packed, ((0, SMALL_ROWS - packed.shape[0]), (0, 0)))
    total = _sum_devices(packed)
    r0 = N_META
    small = {
        "meta_tokens": lax.dynamic_slice(total[:N_META], (0, chip * (D_MODEL // N_CHIPS)), (N_META, D_MODEL // N_CHIPS)),
        "mix_pre_g": total[r0:r0 + 2], "mix_post_g": total[r0 + 2:r0 + 4], "mlp_pre_g": total[r0 + 4:r0 + 6],
        "mlp_post_g": total[r0 + 6:r0 + 8],
        "attn_out_g": total[r0 + 8:r0 + 10, :ATTN_W], "conv_out_g": total[r0 + 8:r0 + 10, ATTN_W:],
        "conv_w": lax.dynamic_slice(total[r0 + 10:r0 + 13].reshape(CONV_K, DEPTH, CONV_W).transpose(1, 0, 2),
                                    (0, 0, chip * (CONV_W // N_CHIPS)), (DEPTH, CONV_K, CONV_W // N_CHIPS)),
        "sinks": total[r0 + 13, :DEPTH * N_Q_HEADS].reshape(DEPTH, N_Q_HEADS),
    }
    loss = total[r0 + 14, 0]

    ds, nms, nvs = _adamw_small([w[n] for n in _SMALL], [small[n] for n in _SMALL], [m[n] for n in _SMALL],
                                [v[n] for n in _SMALL])
    done1 = adamw(1, dict(zip(_LARGE, red1.done(red0b.token))), None)
    red0a.join(ds[0], grad_x, *[done1[n][0] for n in _LARGE])
    red0b.join(red0a.token)
    done0 = adamw(0, dict(zip(first0, red0a.done(red0b.token))), done1)
    done0.update(adamw(0, {"w_in": red0b.done(done0["w_down"][0])[0]}, done1))
    grad, delta, new_m, new_v = {}, {}, {}, {}
    for n in _LARGE:
        grad[n], delta[n], new_m[n], new_v[n] = done0[n]
    for d in (grad, delta, new_m, new_v):
        d["w_in"] = jnp.swapaxes(d["w_in"], 1, 2)
    for i, n in enumerate(_SMALL):
        grad[n], delta[n], new_m[n], new_v[n] = small[n], ds[i], nms[i], nvs[i]
    return (loss, grad_x, *[grad[n] for n in _ORDER], *[delta[n] for n in _ORDER], *[new_m[n] for n in _ORDER],
            *[new_v[n] for n in _ORDER])
```

```python
import functools

import jax
import jax.numpy as jnp
from jax import lax
from jax.experimental import pallas as pl
from jax.experimental.pallas import tpu as pltpu

F32 = jnp.float32
BF16 = jnp.bfloat16

D_MODEL = 1024
DEPTH = 2
N_META = 16
ATTN_W = 512
CONV_W = 512
HEAD_DIM = 64
N_Q_HEADS = 8
N_KV_HEADS = 2
GROUP = N_Q_HEADS // N_KV_HEADS
KV_W = N_KV_HEADS * HEAD_DIM
CONV_K = 3
BLOCK = 128
LEAD_PAD = BLOCK - N_META
ROPE_THETA = 500000.0
ROT_DIM = HEAD_DIM // 4
ROT_HALF = ROT_DIM // 2
D_FF = 4 * D_MODEL
IN_W = ATTN_W + 2 * KV_W + 3 * CONV_W
QKV_W = ATTN_W + 2 * KV_W
EPS = 1e-6
SCALE = HEAD_DIM ** -0.5
FF_CHUNK = 1024
N_CHIPS = 4
N_DEV = 8

ADAM_LR = 0.001
ADAM_B1 = 0.9
ADAM_B2 = 0.999
ADAM_EPS = 1e-08
ADAM_WD = 0.01
ADAM_STEP = 10

V7X_VMEM_LIMIT = 60 * 1024 * 1024
SMALL_ROWS = 32

MESH = pl.DeviceIdType.MESH


def _params(*sem):
    return pltpu.CompilerParams(dimension_semantics=sem, vmem_limit_bytes=V7X_VMEM_LIMIT)


def _block_rows(n):
    return max(r for r in range(16, min(n, 256) + 1, 16) if n % r == 0)


def _row_tile(t, most):
    nb = t // BLOCK
    for b in range(most // BLOCK, 0, -1):
        if nb % b == 0:
            return b * BLOCK
    return BLOCK


def _behind(body, deps):
    n = len(deps)

    def wrapped(*refs):
        body(*refs[n:])

    return wrapped, [pl.BlockSpec(memory_space=pl.ANY)] * n


def _rms(x, g):
    r = lax.rsqrt(jnp.mean(x * x, axis=-1, keepdims=True) + EPS)
    return x * r * g


def _rms_bwd(dy, x, g):
    r = lax.rsqrt(jnp.mean(x * x, axis=-1, keepdims=True) + EPS)
    xh = x * r
    dg = jnp.sum(dy * xh, axis=0, keepdims=True)
    dxh = dy * g
    dx = r * (dxh - xh * jnp.mean(dxh * xh, axis=-1, keepdims=True))
    return dx, dg


def _rope(x, cos, sa, sb):
    n = x.shape[-1]
    return x * cos + pltpu.roll(x, n - ROT_HALF, 1) * sa + pltpu.roll(x, ROT_HALF, 1) * sb


def _rope_bwd(dy, cos, sa, sb):
    n = dy.shape[-1]
    return dy * cos + pltpu.roll(dy * sa, ROT_HALF, 1) + pltpu.roll(dy * sb, n - ROT_HALF, 1)


def _rope_tables(t):
    pos = lax.broadcasted_iota(jnp.int32, (t, ROT_HALF), 0).astype(F32) - LEAD_PAD
    pair = lax.broadcasted_iota(jnp.int32, (t, ROT_HALF), 1).astype(F32)
    inv_freq = jnp.power(jnp.float32(ROPE_THETA), -(2.0 * pair) / ROT_DIM)
    ang = pos * inv_freq
    cos, sin = lax.optimization_barrier((jnp.cos(ang), jnp.sin(ang)))
    spread = (1, 2 * HEAD_DIM // ROT_HALF)
    cos, sin = jnp.tile(cos, spread), jnp.tile(sin, spread)
    dim = lax.broadcasted_iota(jnp.int32, (t, 2 * HEAD_DIM), 1) % HEAD_DIM
    return (jnp.where(dim < ROT_DIM, cos, 1.0), jnp.where(dim < ROT_HALF, -sin, 0.0),
            jnp.where((dim >= ROT_HALF) & (dim < ROT_DIM), sin, 0.0))


def _in_proj(h, g, w, tabs, tm):
    t = h.shape[0]

    def body(h_ref, g_ref, w_ref, c_ref, sa_ref, sb_ref, a_ref, q_ref, k_ref, v_ref, b_ref, cg_ref, hc_ref):
        a = _rms(h_ref[...], g_ref[...]).astype(BF16)
        a_ref[...] = a
        p = lax.dot_general(a, w_ref[...], (((1,), (1,)), ((), ())), preferred_element_type=F32)
        cos, sa, sb = c_ref[...], sa_ref[...], sb_ref[...]
        rep = ATTN_W // (2 * HEAD_DIM)
        q = _rope(p[:, :ATTN_W], jnp.tile(cos, (1, rep)), jnp.tile(sa, (1, rep)), jnp.tile(sb, (1, rep)))
        q_ref[...] = (q * SCALE).astype(BF16)
        k_ref[...] = _rope(p[:, ATTN_W:ATTN_W + KV_W], cos, sa, sb).astype(BF16)
        v_ref[...] = p[:, ATTN_W + KV_W:QKV_W].astype(BF16)
        b_ref[...] = p[:, QKV_W:QKV_W + CONV_W].astype(BF16)
        cg_ref[...] = p[:, QKV_W + CONV_W:QKV_W + 2 * CONV_W].astype(BF16)
        hc_ref[...] = p[:, QKV_W + 2 * CONV_W:].astype(BF16)

    row = lambda n: pl.BlockSpec((tm, n), lambda i: (i, 0))
    full = lambda a: pl.BlockSpec(a.shape, lambda i: (0, 0))
    return pl.pallas_call(
        body, name="in_proj", grid=(t // tm,),
        in_specs=[row(D_MODEL), full(g), full(w), row(2 * HEAD_DIM), row(2 * HEAD_DIM), row(2 * HEAD_DIM)],
        out_specs=[row(D_MODEL), row(ATTN_W), row(KV_W), row(KV_W), row(CONV_W), row(CONV_W), row(CONV_W)],
        out_shape=[jax.ShapeDtypeStruct((t, D_MODEL), BF16), jax.ShapeDtypeStruct((t, ATTN_W), BF16),
                   jax.ShapeDtypeStruct((t, KV_W), BF16), jax.ShapeDtypeStruct((t, KV_W), BF16),
                   jax.ShapeDtypeStruct((t, CONV_W), BF16), jax.ShapeDtypeStruct((t, CONV_W), BF16),
                   jax.ShapeDtypeStruct((t, CONV_W), BF16)],
        compiler_params=_params("parallel"),
    )(h, g, w, *tabs)


def _attn_bias():
    r = lax.broadcasted_iota(jnp.int32, (3, BLOCK, 2 * BLOCK), 1)
    c = lax.broadcasted_iota(jnp.int32, (3, BLOCK, 2 * BLOCK), 2)
    i = lax.broadcasted_iota(jnp.int32, (3, BLOCK, 2 * BLOCK), 0)
    ok = (c > r) & (c <= r + BLOCK) & (c + (i - 1) * BLOCK >= LEAD_PAD)
    return jnp.where(ok, 0.0, -jnp.inf).astype(F32)


def _attn_scores(qh, kg, bias):
    return lax.dot_general(qh, kg, (((1,), (1,)), ((), ())), preferred_element_type=F32) + bias


def _attn_probs(s, sk):
    m = jnp.maximum(jnp.max(s, axis=-1, keepdims=True), sk)
    e = jnp.exp(s - m)
    es = jnp.exp(sk - m)
    rden = 1.0 / (jnp.sum(e, axis=-1, keepdims=True) + es)
    return e * rden, es * rden


def _head(hh):
    return slice(hh * HEAD_DIM, (hh + 1) * HEAD_DIM)


def _two_blocks(ref, i):
    prev = jnp.maximum(i - 1, 0)
    return jnp.concatenate([ref[pl.ds(pl.multiple_of(prev * BLOCK, BLOCK), BLOCK), :],
                            ref[pl.ds(pl.multiple_of(i * BLOCK, BLOCK), BLOCK), :]], axis=0)


def _attn_fwd(q, k, v, bias, sinks, tm):
    t = q.shape[0]
    per_step = tm // BLOCK
    heads = range(N_Q_HEADS)

    def body(s_ref, q_ref, k_ref, v_ref, bias_ref, o_ref):
        for b in range(per_step):
            i = pl.program_id(0) * per_step + b
            rows = slice(b * BLOCK, (b + 1) * BLOCK)
            kc, vc = _two_blocks(k_ref, i), _two_blocks(v_ref, i)
            bias_i = bias_ref[jnp.minimum(i, 2)]
            scores = [_attn_scores(q_ref[rows, _head(hh)], kc[:, _head(hh // GROUP)], bias_i) for hh in heads]
            probs = [_attn_probs(scores[hh], s_ref[hh])[0].astype(BF16) for hh in heads]
            for hh in heads:
                o_ref[rows, _head(hh)] = jnp.dot(probs[hh], vc[:, _head(hh // GROUP)],
                                                 preferred_element_type=F32).astype(BF16)

    whole = pl.BlockSpec((t, KV_W), lambda i: (0, 0))
    return pl.pallas_call(
        body, name="attn_fwd", grid=(t // tm,),
        in_specs=[pl.BlockSpec(memory_space=pltpu.SMEM), pl.BlockSpec((tm, ATTN_W), lambda i: (i, 0)), whole, whole,
                  pl.BlockSpec(bias.shape, lambda i: (0, 0, 0))],
        out_specs=pl.BlockSpec((tm, ATTN_W), lambda i: (i, 0)),
        out_shape=jax.ShapeDtypeStruct((t, ATTN_W), BF16),
        compiler_params=_params("parallel"),
    )(sinks, q, k, v, bias)


def _shift_rows(u, halo, n):
    r = pltpu.roll(u, n, 0)
    hr = pltpu.roll(halo, n, 0)
    idx = lax.broadcasted_iota(jnp.int32, hr.shape, 0)
    return jnp.concatenate([jnp.where(idx < n, hr, r[:8]), r[8:]], axis=0)


def _advance_rows(u, halo, n):
    rows = u.shape[0]
    r = pltpu.roll(u, rows - n, 0)
    hr = pltpu.roll(halo, 8 - n, 0)
    idx = lax.broadcasted_iota(jnp.int32, hr.shape, 0)
    return jnp.concatenate([r[:rows - 8], jnp.where(idx >= 8 - n, hr, r[rows - 8:])], axis=0)


def _mix_out(h, o, b, c, hc, cw, ga, gc, w, gp, tm, deps=()):
    t = h.shape[0]

    def body(h_ref, o_ref, b_ref, c_ref, hc_ref, cw_ref, ga_ref, gc_ref, w_ref, gp_ref, h1_ref, y_ref, z_ref, halo):
        @pl.when(pl.program_id(0) == 0)
        def _():
            halo[...] = jnp.zeros_like(halo)

        u = c_ref[...].astype(F32) * hc_ref[...].astype(F32)
        cv = cw_ref[0:1, :] * _shift_rows(u, halo[...], 2) + cw_ref[1:2, :] * _shift_rows(u, halo[...], 1) \
            + cw_ref[2:3, :] * u
        halo[...] = u[tm - 8:]
        yc = b_ref[...].astype(F32) * cv
        y = jnp.concatenate([_rms(o_ref[...].astype(F32), ga_ref[...]), _rms(yc, gc_ref[...])], axis=1).astype(BF16)
        y_ref[...] = y
        z = jnp.dot(y, w_ref[...].reshape(D_MODEL, D_MODEL), preferred_element_type=F32)
        z_ref[...] = z
        h1_ref[...] = h_ref[...] + _rms(z, gp_ref[...])

    row = lambda n: pl.BlockSpec((tm, n), lambda i: (i, 0))
    full = lambda a: pl.BlockSpec(a.shape, lambda i: (0,) * a.ndim)
    body, dep_specs = _behind(body, deps)
    return pl.pallas_call(
        body, name="mix_out", grid=(t // tm,),
        in_specs=dep_specs + [row(D_MODEL), row(ATTN_W), row(CONV_W), row(CONV_W), row(CONV_W), full(cw), full(ga),
                              full(gc), full(w), full(gp)],
        out_specs=[row(D_MODEL), row(D_MODEL), row(D_MODEL)],
        out_shape=[jax.ShapeDtypeStruct((t, D_MODEL), F32), jax.ShapeDtypeStruct((t, D_MODEL), BF16),
                   jax.ShapeDtypeStruct((t, D_MODEL), F32)],
        scratch_shapes=[pltpu.VMEM((8, CONV_W), F32)],
        compiler_params=_params("arbitrary"),
    )(*deps, h, o, b, c, hc, cw, ga, gc, w, gp)


def _mlp(h1, g1, wu, wd, g2, tm, target=None):
    t = h1.shape[0]
    nj = D_FF // FF_CHUNK
    per_step = tm // BLOCK if target is not None else 0

    def body(h1_ref, g1_ref, wu_ref, wd_ref, g2_ref, *rest):
        t_refs, outs = rest[:per_step], rest[per_step:]
        a2_ref, slope_ref, f_ref = outs[-3:]
        a2 = _rms(h1_ref[...], g1_ref[...]).astype(BF16)
        a2_ref[...] = a2
        f = None
        for j in range(nj):
            up = jnp.dot(a2, wu_ref[j], preferred_element_type=F32)
            r = jnp.maximum(up, 0.0)
            slope_ref[:, j * FF_CHUNK:(j + 1) * FF_CHUNK] = (r + r).astype(BF16)
            part = jnp.dot((r * r).astype(BF16), wd_ref[j], preferred_element_type=F32)
            f = part if f is None else f + part
        f_ref[...] = f
        h2 = h1_ref[...] + _rms(f, g2_ref[...])
        if target is None:
            outs[0][...] = h2
            return
        loss_ref, dh_ref = outs[:2]
        i = pl.program_id(0)

        @pl.when(i == 0)
        def _():
            loss_ref[...] = jnp.zeros_like(loss_ref)

        total = jnp.zeros((), F32)
        for b in range(per_step):
            rows = slice(b * BLOCK, (b + 1) * BLOCK)
            err = h2[rows] - t_refs[b][...]
            if b == 0:
                err = jnp.where(i == 0, 0.0, err)
            dh_ref[rows, :] = err * (1.0 / D_MODEL)
            total = total + jnp.sum(err * err)
        loss_ref[...] += total * (0.5 / D_MODEL)

    def target_block(b):
        return pl.BlockSpec((BLOCK, D_MODEL), lambda i: (jnp.maximum(i * per_step + b - 1, 0), 0))

    row = pl.BlockSpec((tm, D_MODEL), lambda i: (i, 0))
    vec = pl.BlockSpec((1, D_MODEL), lambda i: (0, 0))
    resident = pl.BlockSpec(memory_space=pltpu.VMEM)
    first_specs, first_shapes = [row], [jax.ShapeDtypeStruct((t, D_MODEL), F32)]
    if target is not None:
        first_specs = [pl.BlockSpec((8, 128), lambda i: (0, 0)), row]
        first_shapes = [jax.ShapeDtypeStruct((8, 128), F32), jax.ShapeDtypeStruct((t, D_MODEL), F32)]
    outs = pl.pallas_call(
        body, name="mlp", grid=(t // tm,),
        in_specs=[row, vec, resident, resident, vec] + [target_block(b) for b in range(per_step)],
        out_specs=first_specs + [row, pl.BlockSpec((tm, D_FF), lambda i: (i, 0)), row],
        out_shape=first_shapes + [jax.ShapeDtypeStruct((t, D_MODEL), BF16), jax.ShapeDtypeStruct((t, D_FF), BF16),
                                  jax.ShapeDtypeStruct((t, D_MODEL), F32)],
        compiler_params=_params("parallel" if target is None else "arbitrary"),
    )(h1, g1, wu, wd, g2, *([target] * per_step))
    return (outs[0] if target is None else tuple(outs[:2]),) + tuple(outs[-3:])


def _mlp_bwd_hidden(dh2, f, g2, slope, wd, tm, deps=()):
    t = dh2.shape[0]
    nj = D_FF // FF_CHUNK

    def body(dh2_ref, f_ref, g2_ref, slope_ref, wd_ref, df_ref, dup_ref, dg2_ref):
        @pl.when(pl.program_id(0) == 0)
        def _():
            dg2_ref[...] = jnp.zeros_like(dg2_ref)

        df, dg = _rms_bwd(dh2_ref[...], f_ref[...], g2_ref[...])
        dg2_ref[...] += dg
        df = df.astype(BF16)
        df_ref[...] = df
        for j in range(nj):
            cols = slice(j * FF_CHUNK, (j + 1) * FF_CHUNK)
            dact = lax.dot_general(df, wd_ref[j], (((1,), (1,)), ((), ())), preferred_element_type=F32)
            dup_ref[:, cols] = (dact * slope_ref[:, cols].astype(F32)).astype(BF16)

    row = pl.BlockSpec((tm, D_MODEL), lambda i: (i, 0))
    wide = pl.BlockSpec((tm, D_FF), lambda i: (i, 0))
    vec = pl.BlockSpec((1, D_MODEL), lambda i: (0, 0))
    body, dep_specs = _behind(body, deps)
    return pl.pallas_call(
        body, name="mlp_bwd_hidden", grid=(t // tm,),
        in_specs=dep_specs + [row, row, vec, wide, pl.BlockSpec(memory_space=pltpu.VMEM)],
        out_specs=[row, wide, vec],
        out_shape=[jax.ShapeDtypeStruct((t, D_MODEL), BF16), jax.ShapeDtypeStruct((t, D_FF), BF16),
                   jax.ShapeDtypeStruct((1, D_MODEL), F32)],
        compiler_params=_params("arbitrary"),
    )(*deps, dh2, f, g2, slope, wd)


def _mlp_bwd_input(dup, wu, h1, g1, dh2, tm):
    t = dh2.shape[0]
    nj = D_FF // FF_CHUNK

    def body(dup_ref, wu_ref, h1_ref, g1_ref, dh2_ref, dh1_ref, dg1_ref):
        @pl.when(pl.program_id(0) == 0)
        def _():
            dg1_ref[...] = jnp.zeros_like(dg1_ref)

        da2 = None
        for j in range(nj):
            part = lax.dot_general(dup_ref[:, j * FF_CHUNK:(j + 1) * FF_CHUNK], wu_ref[j], (((1,), (1,)), ((), ())),
                                   preferred_element_type=F32)
            da2 = part if da2 is None else da2 + part
        dx, dg = _rms_bwd(da2, h1_ref[...], g1_ref[...])
        dh1_ref[...] = dh2_ref[...] + dx
        dg1_ref[...] += dg

    row = pl.BlockSpec((tm, D_MODEL), lambda i: (i, 0))
    vec = pl.BlockSpec((1, D_MODEL), lambda i: (0, 0))
    return pl.pallas_call(
        body, name="mlp_bwd_input", grid=(t // tm,),
        in_specs=[pl.BlockSpec((tm, D_FF), lambda i: (i, 0)), pl.BlockSpec(memory_space=pltpu.VMEM), row, vec, row],
        out_specs=[row, vec],
        out_shape=[jax.ShapeDtypeStruct((t, D_MODEL), F32), jax.ShapeDtypeStruct((1, D_MODEL), F32)],
        compiler_params=_params("arbitrary"),
    )(dup, wu, h1, g1, dh2)


def _row_split(t):
    tile = min(t, 1024)
    return tile, t // tile, t % tile


def _row_split_specs(t, cols, col_of):
    tile, whole, rest = _row_split(t)
    specs = [pl.BlockSpec((tile, cols), lambda *g: (jnp.minimum(g[-1], whole - 1), col_of(*g[:-1])))]
    if rest:
        specs.append(pl.BlockSpec((rest, cols), lambda *g: (whole * tile // rest, col_of(*g[:-1]))))
    return specs


def _weight_grad(x, y, name, x_is_slope=False):
    t, k = x.shape
    n = y.shape[1]
    tn = FF_CHUNK
    tk = FF_CHUNK if k % FF_CHUNK == 0 else k
    _, whole, rest = _row_split(t)
    steps = whole + bool(rest)
    one_tile = k == tk and n == tn

    def body(*refs):
        o_ref, ob_ref, r = refs[-2], refs[-1], pl.program_id(2)
        if one_tile:
            o_ref, ob_ref = o_ref.at[0, 0], ob_ref.at[0, 0]

        @pl.when(r == 0)
        def _():
            o_ref[...] = jnp.zeros_like(o_ref)

        def add(x_ref, y_ref):
            xv = x_ref[...]
            if x_is_slope:
                xv = xv.astype(F32)
                xv = (xv * xv * 0.25).astype(BF16)
            o_ref[...] += lax.dot_general(xv, y_ref[...], (((0,), (0,)), ((), ())), preferred_element_type=F32)

        if rest:
            pl.when(r < whole)(lambda: add(refs[0], refs[2]))
            pl.when(r == whole)(lambda: add(refs[1], refs[3]))
        else:
            add(refs[0], refs[1])

        @pl.when(r == steps - 1)
        def _():
            ob_ref[...] = o_ref[...].astype(BF16)

    tile = pl.BlockSpec((None, None, tk, tn), lambda a, b, r: (a, b, 0, 0))
    if one_tile:
        tile = pl.BlockSpec(memory_space=pltpu.VMEM)
    return pl.pallas_call(
        body, name=name, grid=(k // tk, n // tn, steps),
        in_specs=_row_split_specs(t, tk, lambda a, b: a) + _row_split_specs(t, tn, lambda a, b: b),
        out_specs=[tile, tile],
        out_shape=[jax.ShapeDtypeStruct((k // tk, n // tn, tk, tn), F32),
                   jax.ShapeDtypeStruct((k // tk, n // tn, tk, tn), BF16)],
        compiler_params=_params("parallel", "parallel", "arbitrary"),
    )(*([x] * (1 + bool(rest))), *([y] * (1 + bool(rest))))


def _mix_out_bwd(dh1, z, gp, w, o, b, c, hc, cw, ga, gc, tm, deps=()):
    t = dh1.shape[0]
    nt = t // tm
    per16 = tm // 16

    def body(dh1_ref, z_ref, gp_ref, w_ref, o_ref, b_ref, c_ref, hc_ref, cp_ref, hp_ref, cw_ref, ga_ref, gc_ref,
             dz_ref, do_ref, dbch_ref, dgp_ref, dga_ref, dgc_ref, dcw_ref, halo):
        i = pl.program_id(0)

        @pl.when(i == 0)
        def _():
            halo[...] = jnp.zeros_like(halo)
            dgp_ref[...] = jnp.zeros_like(dgp_ref)
            dga_ref[...] = jnp.zeros_like(dga_ref)
            dgc_ref[...] = jnp.zeros_like(dgc_ref)
            dcw_ref[...] = jnp.zeros_like(dcw_ref)

        dz, dgp = _rms_bwd(dh1_ref[...], z_ref[...], gp_ref[...])
        dgp_ref[...] += dgp
        dz = dz.astype(BF16)
        dz_ref[...] = dz
        dy = lax.dot_general(dz, w_ref[...].reshape(D_MODEL, D_MODEL), (((1,), (1,)), ((), ())),
                             preferred_element_type=F32)
        do, dga = _rms_bwd(dy[:, :ATTN_W], o_ref[...].astype(F32), ga_ref[...])
        do_ref[...] = do.astype(BF16)
        dga_ref[...] += dga

        cc, hh = c_ref[...].astype(F32), hc_ref[...].astype(F32)
        u = cc * hh
        first = i == nt - 1
        u_before = jnp.where(first, 0.0, (cp_ref[...].astype(F32) * hp_ref[...].astype(F32))[8:])
        u1 = _shift_rows(u, u_before, 1)
        u2 = _shift_rows(u, u_before, 2)
        cv = cw_ref[0:1, :] * u2 + cw_ref[1:2, :] * u1 + cw_ref[2:3, :] * u
        bb = b_ref[...].astype(F32)
        dyc, dgc = _rms_bwd(dy[:, ATTN_W:], bb * cv, gc_ref[...])
        dgc_ref[...] += dgc
        dcv = dyc * bb
        d1 = _advance_rows(dcv, halo[...], 1)
        d2 = _advance_rows(dcv, halo[...], 2)
        halo[...] = dcv[:8]
        du = cw_ref[2:3, :] * dcv + cw_ref[1:2, :] * d1 + cw_ref[0:1, :] * d2
        dbch_ref[...] = jnp.concatenate([dyc * cv, du * hh, du * cc], axis=1).astype(BF16)
        dcw_ref[...] += jnp.concatenate([jnp.sum(dcv * u2, axis=0, keepdims=True),
                                         jnp.sum(dcv * u1, axis=0, keepdims=True),
                                         jnp.sum(dcv * u, axis=0, keepdims=True)], axis=0)

    row = lambda n: pl.BlockSpec((tm, n), lambda i: (nt - 1 - i, 0))
    before = pl.BlockSpec((16, CONV_W), lambda i: (jnp.maximum((nt - 1 - i) * per16 - 1, 0), 0))
    full = lambda a: pl.BlockSpec(a.shape, lambda i: (0,) * a.ndim)
    vec = lambda n: pl.BlockSpec((1, n), lambda i: (0, 0))
    body, dep_specs = _behind(body, deps)
    return pl.pallas_call(
        body, name="mix_out_bwd", grid=(nt,),
        in_specs=dep_specs + [row(D_MODEL), row(D_MODEL), full(gp), full(w), row(ATTN_W), row(CONV_W), row(CONV_W),
                              row(CONV_W), before, before, full(cw), full(ga), full(gc)],
        out_specs=[row(D_MODEL), row(ATTN_W), row(3 * CONV_W), vec(D_MODEL), vec(ATTN_W), vec(CONV_W),
                   pl.BlockSpec((CONV_K, CONV_W), lambda i: (0, 0))],
        out_shape=[jax.ShapeDtypeStruct((t, D_MODEL), BF16), jax.ShapeDtypeStruct((t, ATTN_W), BF16),
                   jax.ShapeDtypeStruct((t, 3 * CONV_W), BF16), jax.ShapeDtypeStruct((1, D_MODEL), F32),
                   jax.ShapeDtypeStruct((1, ATTN_W), F32), jax.ShapeDtypeStruct((1, CONV_W), F32),
                   jax.ShapeDtypeStruct((CONV_K, CONV_W), F32)],
        scratch_shapes=[pltpu.VMEM((8, CONV_W), F32)],
        compiler_params=_params("arbitrary"),
    )(*deps, dh1, z, gp, w, o, b, c, hc, c, hc, cw, ga, gc)


def _attn_bwd(q, k, v, o, do, bias, sinks, tm, deps=()):
    t = q.shape[0]
    per_step = tm // BLOCK

    def body(s_ref, q_ref, k_ref, v_ref, o_ref, do_ref, bias_ref, dq_ref, dk_ref, dv_ref, ds_ref):
        step = pl.program_id(0)

        @pl.when(step == 0)
        def _():
            ds_ref[...] = jnp.zeros_like(ds_ref)

        heads = range(N_Q_HEADS)

        def first_matmuls(b):
            i = step * per_step + b
            rows = slice(b * BLOCK, (b + 1) * BLOCK)
            kc, vc = _two_blocks(k_ref, i), _two_blocks(v_ref, i)
            bias_i = bias_ref[jnp.minimum(i, 2)]
            kgs = [kc[:, _head(g)] for g in range(N_KV_HEADS)]
            vgs = [vc[:, _head(g)] for g in range(N_KV_HEADS)]
            qs = [q_ref[rows, _head(hh)] for hh in heads]
            dosb = [do_ref[rows, _head(hh)] for hh in heads]
            dos = [d.astype(F32) for d in dosb]
            scores = [_attn_scores(qs[hh], kgs[hh // GROUP], bias_i) for hh in heads]
            dps = [lax.dot_general(dosb[hh], vgs[hh // GROUP], (((1,), (1,)), ((), ())), preferred_element_type=F32)
                   for hh in heads]
            return kgs, qs, dos, dosb, scores, dps

        dsink = [jnp.zeros((BLOCK, 1), F32) for _ in range(N_Q_HEADS)]
        ahead = None
        for b in range(per_step):
            i = step * per_step + b
            rows = slice(b * BLOCK, (b + 1) * BLOCK)
            kgs, qs, dos, dosb, scores, dps = first_matmuls(b)
            ps, dss = [], []
            for hh in heads:
                p, share = _attn_probs(scores[hh], s_ref[hh])
                drow = jnp.sum(dos[hh] * o_ref[rows, _head(hh)].astype(F32), axis=-1, keepdims=True)
                dss.append((p * (dps[hh] - drow)).astype(BF16))
                ps.append(p.astype(BF16))
                dsink[hh] = dsink[hh] + share * drow
            for hh in heads:
                dq_ref[rows, _head(hh)] = (jnp.dot(dss[hh], kgs[hh // GROUP], preferred_element_type=F32)
                                           * SCALE).astype(BF16)
            groups = [slice(GROUP * g, GROUP * (g + 1)) for g in range(N_KV_HEADS)]
            dkg = [lax.dot_general(jnp.concatenate(dss[gr], axis=0), jnp.concatenate(qs[gr], axis=0),
                                   (((0,), (0,)), ((), ())), preferred_element_type=F32) for gr in groups]
            dvg = [lax.dot_general(jnp.concatenate(ps[gr], axis=0), jnp.concatenate(dosb[gr], axis=0),
                                   (((0,), (0,)), ((), ())), preferred_element_type=F32) for gr in groups]
            dkb, dvb = jnp.concatenate(dkg, axis=1), jnp.concatenate(dvg, axis=1)
            if b == 0:
                @pl.when(step > 0)
                def _():
                    before = pl.ds(pl.multiple_of((i - 1) * BLOCK, BLOCK), BLOCK)
                    dk_ref[before, :] += dkb[:BLOCK]
                    dv_ref[before, :] += dvb[:BLOCK]
            else:
                at = pl.ds(pl.multiple_of((i - 1) * BLOCK, BLOCK), BLOCK)
                dk_ref[at, :] = ahead[0] + dkb[:BLOCK]
                dv_ref[at, :] = ahead[1] + dvb[:BLOCK]
            ahead = (dkb[BLOCK:], dvb[BLOCK:])
        last = pl.ds(pl.multiple_of(((step + 1) * per_step - 1) * BLOCK, BLOCK), BLOCK)
        dk_ref[last, :] = ahead[0]
        dv_ref[last, :] = ahead[1]
        for hh in range(N_Q_HEADS):
            ds_ref[hh:hh + 1, :] -= jnp.sum(dsink[hh])

    whole = pl.BlockSpec((t, KV_W), lambda i: (0, 0))
    blk = pl.BlockSpec((tm, ATTN_W), lambda i: (i, 0))
    body, dep_specs = _behind(body, deps)
    return pl.pallas_call(
        body, name="attn_bwd", grid=(t // tm,),
        in_specs=dep_specs + [pl.BlockSpec(memory_space=pltpu.SMEM), blk, whole, whole, blk, blk,
                              pl.BlockSpec(bias.shape, lambda i: (0, 0, 0))],
        out_specs=[blk, whole, whole, pl.BlockSpec((N_Q_HEADS, 128), lambda i: (0, 0))],
        out_shape=[jax.ShapeDtypeStruct((t, ATTN_W), BF16), jax.ShapeDtypeStruct((t, KV_W), F32),
                   jax.ShapeDtypeStruct((t, KV_W), F32), jax.ShapeDtypeStruct((N_Q_HEADS, 128), F32)],
        compiler_params=_params("arbitrary"),
    )(*deps, sinks, q, k, v, o, do, bias)


def _in_proj_bwd(dq, dk, dv, dbch, w, dh1, h, g, tabs, tm):
    t = h.shape[0]

    def body(dq_ref, dk_ref, dv_ref, dbch_ref, w_ref, dh1_ref, h_ref, g_ref, c_ref, sa_ref, sb_ref, dh_ref, dp_ref,
             dg_ref):
        @pl.when(pl.program_id(0) == 0)
        def _():
            dg_ref[...] = jnp.zeros_like(dg_ref)

        cos, sa, sb = c_ref[...], sa_ref[...], sb_ref[...]
        rep = ATTN_W // (2 * HEAD_DIM)
        dqr = _rope_bwd(dq_ref[...].astype(F32), jnp.tile(cos, (1, rep)), jnp.tile(sa, (1, rep)),
                        jnp.tile(sb, (1, rep)))
        dkr = _rope_bwd(dk_ref[...], cos, sa, sb)
        dp = jnp.concatenate([dqr.astype(BF16), dkr.astype(BF16), dv_ref[...].astype(BF16), dbch_ref[...]], axis=1)
        dp_ref[...] = dp
        da = jnp.dot(dp, w_ref[...], preferred_element_type=F32)
        dx, dg = _rms_bwd(da, h_ref[...], g_ref[...])
        dh_ref[...] = dh1_ref[...] + dx
        dg_ref[...] += dg

    row = lambda n: pl.BlockSpec((tm, n), lambda i: (i, 0))
    full = lambda a: pl.BlockSpec(a.shape, lambda i: (0, 0))
    return pl.pallas_call(
        body, name="in_proj_bwd", grid=(t // tm,),
        in_specs=[row(ATTN_W), row(KV_W), row(KV_W), row(3 * CONV_W), full(w), row(D_MODEL), row(D_MODEL), full(g),
                  row(2 * HEAD_DIM), row(2 * HEAD_DIM), row(2 * HEAD_DIM)],
        out_specs=[row(D_MODEL), row(IN_W), pl.BlockSpec((1, D_MODEL), lambda i: (0, 0))],
        out_shape=[jax.ShapeDtypeStruct((t, D_MODEL), F32), jax.ShapeDtypeStruct((t, IN_W), BF16),
                   jax.ShapeDtypeStruct((1, D_MODEL), F32)],
        compiler_params=_params("arbitrary"),
    )(dq, dk, dv, dbch, w, dh1, h, g, *tabs)


class _Tiles:
    def __init__(self, t):
        self.tm = _row_tile(t, 640)
        self.ts = self.tm
        self.tabs = _rope_tables(t)
        self.bias = _attn_bias()


def _mixer_fwd(h, p, tl):
    a, q, k, v, b, c, hc = _in_proj(h, p["mix_pre_g"], p["w_in"], tl.tabs, tl.ts)
    o = _attn_fwd(q, k, v, tl.bias, p["sinks"], tl.tm)
    return (h, a, q, k, v, b, c, hc, o)


def _out_fwd(mixed, p, tl, deps=()):
    h, a, q, k, v, b, c, hc, o = mixed
    h1, y, z = _mix_out(h, o, b, c, hc, p["conv_w"], p["attn_out_g"], p["conv_out_g"], p["w_out"], p["mix_post_g"],
                        tl.ts, deps)
    return h1, mixed + (h1, y, z)


def _mlp_fwd(h1, saved, p, tl, target=None):
    h2, a2, slope, f = _mlp(h1, p["mlp_pre_g"], p["w_up"], p["w_down"], p["mlp_post_g"], tl.tm, target)
    return h2, saved + (a2, slope, f)


def _mlp_part_bwd(dh, saved, p, tl, deps=()):
    h1, a2, slope, f = saved[9], saved[12], saved[13], saved[14]
    df, dup, dg2 = _mlp_bwd_hidden(dh, f, p["mlp_post_g"], slope, p["w_down"], tl.tm, deps)
    dh1, dg1 = _mlp_bwd_input(dup, p["w_up"], h1, p["mlp_pre_g"], dh, tl.tm)
    g = {"w_down": [d.reshape(N_CHIPS, FF_CHUNK, D_MODEL)
                    for d in _weight_grad(slope, df, "grad_w_down", x_is_slope=True)],
         "w_up": [d.reshape(N_CHIPS, D_MODEL, FF_CHUNK) for d in _weight_grad(a2, dup, "grad_w_up")],
         "mlp_post_g": dg2, "mlp_pre_g": dg1}
    return dh1, g


def _mix_out_part_bwd(dh1, saved, p, tl, deps=()):
    b, c, hc, o, y, z = saved[5], saved[6], saved[7], saved[8], saved[10], saved[11]
    dz, do, dbch, dgp, dga, dgc, dcw = _mix_out_bwd(dh1, z, p["mix_post_g"], p["w_out"], o, b, c, hc, p["conv_w"],
                                                    p["attn_out_g"], p["conv_out_g"], tl.ts, deps)
    g = {"w_out": [d.reshape(N_CHIPS, D_MODEL // N_CHIPS, D_MODEL) for d in _weight_grad(y, dz, "grad_w_out")],
         "mix_post_g": dgp, "attn_out_g": dga, "conv_out_g": dgc, "conv_w": dcw}
    return (dh1, do, dbch), g


def _attn_in_part_bwd(carry, saved, p, tl, deps=()):
    dh1, do, dbch = carry
    h_in, a, q, k, v, o = saved[0], saved[1], saved[2], saved[3], saved[4], saved[8]
    dq, dk, dv, dsink = _attn_bwd(q, k, v, o, do, tl.bias, p["sinks"], tl.tm, deps)
    dh, dproj, dgi = _in_proj_bwd(dq, dk, dv, dbch, p["w_in"], dh1, h_in, p["mix_pre_g"], tl.tabs, tl.ts)
    g_in = [d.reshape(N_CHIPS, IN_W // N_CHIPS, D_MODEL) for d in _weight_grad(dproj, a, "grad_w_in")]
    return dh, {"w_in": g_in, "mix_pre_g": dgi, "sinks": dsink[:, 0]}


def _place():
    return lax.axis_index("x"), lax.axis_index("y"), lax.axis_index("c")


def _other_chips(x, y):
    return [(1 - x, y), (x, 1 - y), (1 - x, 1 - y)]


_HBM = pl.BlockSpec(memory_space=pltpu.HBM)
_SEM = pl.BlockSpec(memory_space=pltpu.SEMAPHORE)
_EFFECT = pltpu.SideEffectType.DATAFLOW_SIDE_EFFECTING


class _Exchange:
    def __init__(self, name, bufs, plan, n, after=()):
        self.name, self.plan, nb = name, plan, len(bufs)
        n_in = nb + len(after)

        def body(*refs):
            send, recv, token = refs[n_in], refs[n_in + 1], refs[-1]
            for k, (src, dst, target, _) in enumerate(plan(refs[:nb])):
                pltpu.make_async_remote_copy(src_ref=src, dst_ref=dst, send_sem=send.at[k], recv_sem=recv.at[k],
                                             device_id=target, device_id_type=MESH).start()
            token[...] = jnp.zeros_like(token)

        outs = pl.pallas_call(
            body, name=name + "_start",
            out_shape=(pltpu.SemaphoreType.DMA((n,)), pltpu.SemaphoreType.DMA((n,)),
                       *[pltpu.HBM(b.shape, b.dtype) for b in bufs], jax.ShapeDtypeStruct((8, 128), F32)),
            in_specs=[_HBM] * nb + [pl.BlockSpec(memory_space=pl.ANY)] * len(after),
            out_specs=(_SEM, _SEM, *[_HBM] * nb, pl.BlockSpec(memory_space=pltpu.VMEM)),
            input_output_aliases={i: 2 + i for i in range(nb)},
            compiler_params=pltpu.CompilerParams(has_side_effects=_EFFECT),
        )(*[pltpu.with_memory_space_constraint(b, pltpu.HBM) for b in bufs], *after)
        self.send, self.recv, self.bufs, self.token = outs[0], outs[1], list(outs[2:2 + nb]), outs[-1]

    def wait(self, *after):
        plan, nb = self.plan, len(self.bufs)

        def body(*refs):
            send, recv = refs[nb], refs[nb + 1]
            for k, (src, _, target, land) in enumerate(plan(refs[:nb])):
                cp = pltpu.make_async_remote_copy(src_ref=src, dst_ref=land, send_sem=send.at[k], recv_sem=recv.at[k],
                                                  device_id=target, device_id_type=MESH)
                cp.wait_send()
                cp.wait_recv()

        outs = pl.pallas_call(
            body, name=self.name + "_wait", out_shape=[pltpu.HBM(b.shape, b.dtype) for b in self.bufs],
            in_specs=[_HBM] * nb + [_SEM, _SEM] + [pl.BlockSpec(memory_space=pl.ANY)] * len(after),
            out_specs=[_HBM] * nb, input_output_aliases={i: i for i in range(nb)},
            compiler_params=pltpu.CompilerParams(has_side_effects=_EFFECT),
        )(*self.bufs, self.send, self.recv, *after)
        return list(outs)


def _gather_plan(n):
    def plan(refs):
        x, y, c = _place()
        me = 2 * x + y
        return [(refs[a].at[me], refs[a].at[me], (px, py, c), refs[a].at[2 * px + py])
                for a in range(n) for px, py in _other_chips(x, y)]

    return plan


def _peers():
    x, y, c = _place()
    return [(k - 1, (x ^ (k >> 2), y ^ ((k >> 1) & 1), c ^ (k & 1))) for k in range(1, N_DEV)]


def _scatter_plan(n, half_rows):
    def plan(refs):
        out = []
        for a in range(n):
            hr = half_rows[a]
            for k, (px, py, pc) in _peers():
                out.append((refs[a].at[2 * px + py, pl.ds(pc * hr, hr)], refs[n + a].at[k], (px, py, pc),
                            refs[n + a].at[k]))
        return out

    return plan


def _join_plan(n):
    def plan(refs):
        x, y, c = _place()
        return [(refs[a].at[c], refs[a].at[c], (x, y, 1 - c), refs[a].at[1 - c]) for a in range(n)]

    return plan


def _sum_parts(g, q):
    rows, cols = g.shape[1], g.shape[2]
    hr = rows // 2
    tr = _block_rows(hr)
    per = hr // tr
    x, y, c = _place()
    where = jnp.stack([2 * x + y, c]).astype(jnp.int32)

    def body(where_ref, g_ref, q_ref, o_ref):
        total = g_ref[...]
        for k in range(N_DEV - 1):
            total = total + q_ref[k].astype(F32)
        o_ref[...] = total

    return pl.pallas_call(
        body, name="sum_parts",
        grid_spec=pltpu.PrefetchScalarGridSpec(
            num_scalar_prefetch=1, grid=(per,),
            in_specs=[pl.BlockSpec((None, tr, cols), lambda i, where_ref: (where_ref[0], where_ref[1] * per + i, 0)),
                      pl.BlockSpec((N_DEV - 1, tr, cols), lambda i, where_ref: (0, i, 0))],
            out_specs=pl.BlockSpec((None, tr, cols), lambda i, where_ref: (where_ref[1], i, 0))),
        out_shape=jax.ShapeDtypeStruct((2, hr, cols), F32),
        compiler_params=_params("parallel"),
    )(where, g, q)


def _sum_devices(packed):
    def body(p_ref, o_ref, land, send_sems, recv_sems):
        x, y, c = _place()
        me = 4 * x + 2 * y + c
        land[me] = p_ref[...]
        sends = []
        for k in range(1, N_DEV):
            px, py, pc = x ^ (k >> 2), y ^ ((k >> 1) & 1), c ^ (k & 1)
            cp = pltpu.make_async_remote_copy(src_ref=p_ref, dst_ref=land.at[me], send_sem=send_sems.at[k - 1],
                                              recv_sem=recv_sems.at[k - 1], device_id=(px, py, pc), device_id_type=MESH)
            cp.start()
            sends.append(cp)
        for k in range(1, N_DEV):
            px, py, pc = x ^ (k >> 2), y ^ ((k >> 1) & 1), c ^ (k & 1)
            pltpu.make_async_remote_copy(src_ref=p_ref, dst_ref=land.at[4 * px + 2 * py + pc],
                                         send_sem=send_sems.at[k - 1], recv_sem=recv_sems.at[k - 1],
                                         device_id=(px, py, pc), device_id_type=MESH).wait_recv()
        for cp in sends:
            cp.wait_send()
        total = land[0]
        for d in range(1, N_DEV):
            total = total + land[d]
        o_ref[...] = total

    vm = pl.BlockSpec(memory_space=pltpu.VMEM)
    return pl.pallas_call(
        body, name="sum_devices", in_specs=[vm], out_specs=vm,
        out_shape=jax.ShapeDtypeStruct(packed.shape, F32),
        scratch_shapes=[pltpu.VMEM((N_DEV,) + packed.shape, F32), pltpu.SemaphoreType.DMA((N_DEV - 1,)),
                        pltpu.SemaphoreType.DMA((N_DEV - 1,))],
    )(packed)


def _adamw_math(w, g, m, v):
    m = ADAM_B1 * m + (1.0 - ADAM_B1) * g
    v = ADAM_B2 * v + (1.0 - ADAM_B2) * jnp.square(g)
    m_hat = m / (1.0 - ADAM_B1 ** ADAM_STEP)
    v_hat = v / (1.0 - ADAM_B2 ** ADAM_STEP)
    delta = -ADAM_LR * (m_hat / (jnp.sqrt(v_hat) + ADAM_EPS) + ADAM_WD * w)
    return delta, m, v


def _adamw_large(layer, w, halves, m, v, other):
    _, rows, cols = w.shape
    tr = _block_rows(rows // 2)
    per = rows // 2 // tr

    def body(w_ref, g_ref, m_ref, v_ref, *rest):
        g_out, d_ref, nm_ref, nv_ref = rest[-4:]
        g = g_ref[...]
        g_out[...] = g
        d_ref[...], nm_ref[...], nv_ref[...] = _adamw_math(w_ref[...], g, m_ref[...], v_ref[...])

    blk = pl.BlockSpec((None, tr, cols), lambda i: (layer, i, 0))
    half = pl.BlockSpec((None, tr, cols), lambda i: (i // per, i % per, 0))
    kept = [] if other is None else list(other)
    return pl.pallas_call(
        body, name="adamw_large", grid=(rows // tr,),
        in_specs=[blk, half, blk, blk] + [pl.BlockSpec(memory_space=pl.ANY)] * len(kept), out_specs=[blk] * 4,
        out_shape=[jax.ShapeDtypeStruct(w.shape, F32)] * 4,
        input_output_aliases={4 + k: k for k in range(len(kept))},
        compiler_params=_params("parallel"),
    )(w, halves, m, v, *kept)


def _adamw_small(ws, gs, ms, vs):
    n = len(ws)

    def body(*refs):
        w_r, g_r, m_r, v_r = refs[:n], refs[n:2 * n], refs[2 * n:3 * n], refs[3 * n:4 * n]
        d_r, nm_r, nv_r = refs[4 * n:5 * n], refs[5 * n:6 * n], refs[6 * n:]
        for a in range(n):
            d_r[a][...], nm_r[a][...], nv_r[a][...] = _adamw_math(w_r[a][...], g_r[a][...], m_r[a][...], v_r[a][...])

    vm = pl.BlockSpec(memory_space=pltpu.VMEM)
    outs = pl.pallas_call(
        body, name="adamw_small", in_specs=[vm] * (4 * n), out_specs=[vm] * (3 * n),
        out_shape=[jax.ShapeDtypeStruct(w.shape, F32) for w in ws] * 3,
    )(*ws, *gs, *ms, *vs)
    return outs[:n], outs[n:2 * n], outs[2 * n:]


_LARGE = ("w_in", "w_out", "w_up", "w_down")
_SMALL = ("meta_tokens", "mix_pre_g", "conv_w", "sinks", "attn_out_g", "conv_out_g", "mix_post_g", "mlp_pre_g",
          "mlp_post_g")
_ORDER = ("meta_tokens", "mix_pre_g", "w_in", "conv_w", "sinks", "attn_out_g", "conv_out_g", "w_out", "mix_post_g",
          "mlp_pre_g", "w_up", "w_down", "mlp_post_g")


class _Reduce:
    def __init__(self, name, grads, after=()):
        self.name, self.n = name, len(grads)
        self.own = [g for g, _ in grads]
        half_rows = [g.shape[1] // 2 for g in self.own]
        zones = [lax.empty((N_DEV - 1, hr, g.shape[2]), BF16) for g, hr in zip(self.own, half_rows)]
        self.exchange = _Exchange(name + "_scatter", [b for _, b in grads] + zones, _scatter_plan(self.n, half_rows),
                                  (N_DEV - 1) * self.n, after)

    @property
    def token(self):
        return self.exchange.token

    def join(self, *after):
        bufs = self.exchange.wait(*after)
        halves = [_sum_parts(g, q) for g, q in zip(self.own, bufs[self.n:])]
        self.exchange = _Exchange(self.name + "_join", halves, _join_plan(self.n), self.n)

    def done(self, *after):
        return self.exchange.wait(*after)


def _pad_cols(a, n=D_MODEL):
    return jnp.pad(a, ((0, 0), (0, n - a.shape[1])))


def kernel(x, meta_tokens, mix_pre_g, w_in, conv_w, sinks, attn_out_g, conv_out_g, w_out, mix_post_g, mlp_pre_g, w_up, w_down, mlp_post_g, loss_target, m_meta_tokens, m_mix_pre_g, m_w_in, m_conv_w, m_sinks, m_attn_out_g, m_conv_out_g, m_w_out, m_mix_post_g, m_mlp_pre_g, m_w_up, m_w_down, m_mlp_post_g, v_meta_tokens, v_mix_pre_g, v_w_in, v_conv_w, v_sinks, v_attn_out_g, v_conv_out_g, v_w_out, v_mix_post_g, v_mlp_pre_g, v_w_up, v_w_down, v_mlp_post_g):
    w = dict(meta_tokens=meta_tokens, mix_pre_g=mix_pre_g, w_in=w_in, conv_w=conv_w, sinks=sinks,
             attn_out_g=attn_out_g, conv_out_g=conv_out_g, w_out=w_out, mix_post_g=mix_post_g, mlp_pre_g=mlp_pre_g,
             w_up=w_up, w_down=w_down, mlp_post_g=mlp_post_g)
    m = dict(meta_tokens=m_meta_tokens, mix_pre_g=m_mix_pre_g, w_in=m_w_in, conv_w=m_conv_w, sinks=m_sinks,
             attn_out_g=m_attn_out_g, conv_out_g=m_conv_out_g, w_out=m_w_out, mix_post_g=m_mix_post_g,
             mlp_pre_g=m_mlp_pre_g, w_up=m_w_up, w_down=m_w_down, mlp_post_g=m_mlp_post_g)
    v = dict(meta_tokens=v_meta_tokens, mix_pre_g=v_mix_pre_g, w_in=v_w_in, conv_w=v_conv_w, sinks=v_sinks,
             attn_out_g=v_attn_out_g, conv_out_g=v_conv_out_g, w_out=v_w_out, mix_post_g=v_mix_post_g,
             mlp_pre_g=v_mlp_pre_g, w_up=v_w_up, w_down=v_w_down, mlp_post_g=v_mlp_post_g)
    chip = 2 * lax.axis_index("x") + lax.axis_index("y")
    tl = _Tiles(x.shape[1] + BLOCK)

    def zone(quarter):
        return lax.dynamic_update_slice(lax.empty((N_CHIPS,) + quarter.shape, quarter.dtype), quarter[None],
                                        (chip,) + (0,) * quarter.ndim)

    w, m, v = ({**d, "w_in": jnp.swapaxes(d["w_in"], 1, 2)} for d in (w, m, v))
    zones = {n: [zone(w[n][l].astype(BF16)) for l in range(DEPTH)] for n in _LARGE}
    first = _Exchange("gather_first", [zones["w_in"][0], zone(w["conv_w"]), zone(w["meta_tokens"])], _gather_plan(3), 9)
    out0 = _Exchange("gather_out", [zones["w_out"][0]], _gather_plan(1), 3, [first.token])
    rest = _Exchange("gather_rest", [zones[n][0] for n in ("w_up", "w_down")], _gather_plan(2), 6, [out0.token])

    def whole_in(quarters):
        return quarters.reshape(IN_W, D_MODEL)

    h = jnp.concatenate([jnp.zeros((BLOCK, D_MODEL), F32), x[0]], axis=0)
    q_in, q_conv, q_meta = first.wait(rest.token, *tl.tabs, tl.bias, h)
    conv_whole = jnp.transpose(q_conv, (1, 2, 0, 3)).reshape(DEPTH, CONV_K, CONV_W)
    meta = jnp.transpose(q_meta, (1, 0, 2)).reshape(N_META, D_MODEL)
    p = [{"conv_w": conv_whole[l], "sinks": w["sinks"][l]} for l in range(DEPTH)]
    for l in range(DEPTH):
        for n in ("mix_pre_g", "attn_out_g", "conv_out_g", "mix_post_g", "mlp_pre_g", "mlp_post_g"):
            p[l][n] = w[n][l][None, :]

    h = lax.dynamic_update_slice(h, meta, (LEAD_PAD, 0))
    p[0]["w_in"] = whole_in(q_in)
    mixed = _mixer_fwd(h, p[0], tl)
    second = _Exchange("gather_second", [zones["w_in"][1], zones["w_out"][1]], _gather_plan(2), 6, [mixed[-1]])
    second_mlp = _Exchange("gather_second_mlp", [zones["w_up"][1], zones["w_down"][1]], _gather_plan(2), 6,
                           [second.token])
    p[0]["w_out"], = out0.wait(second_mlp.token)
    h1, saved0 = _out_fwd(mixed, p[0], tl)
    p[0]["w_up"], p[0]["w_down"] = rest.wait(h1)
    h, saved0 = _mlp_fwd(h1, saved0, p[0], tl)
    q_in, p[1]["w_out"] = second.wait(h)
    p[1]["w_in"] = whole_in(q_in)
    h1, saved1 = _out_fwd(_mixer_fwd(h, p[1], tl), p[1], tl)
    p[1]["w_up"], p[1]["w_down"] = second_mlp.wait(h1)
    (loss_tile, dh), saved1 = _mlp_fwd(h1, saved1, p[1], tl, loss_target[0])

    def adamw(layer, halves, other):
        return {n: _adamw_large(layer, w[n], halves[n], m[n], v[n], None if other is None else other[n])
                for n in halves}

    dh1, g1 = _mlp_part_bwd(dh, saved1, p[1], tl)
    carry, gm = _mix_out_part_bwd(dh1, saved1, p[1], tl)
    dh, gi = _attn_in_part_bwd(carry, saved1, p[1], tl)
    g1.update(gm, **gi)
    red1 = _Reduce("reduce1", [g1[n] for n in _LARGE])
    dh1, g0 = _mlp_part_bwd(dh, saved0, p[0], tl, [red1.token])
    red1.join(g0["w_down"][0])
    carry, gm = _mix_out_part_bwd(dh1, saved0, p[0], tl, [red1.token])
    first0 = ("w_up", "w_down", "w_out")
    g0.update(gm)
    red0a = _Reduce("reduce0a", [g0[n] for n in first0])
    dh0, gi = _attn_in_part_bwd(carry, saved0, p[0], tl, [red0a.token])
    g0.update(gi)
    red0b = _Reduce("reduce0b", [g0["w_in"]])
    grad_x = dh0[BLOCK:][None]
    grads = {n: [g0[n], g1[n]] for n in g0 if n not in _LARGE}

    rows = [dh0[LEAD_PAD:BLOCK]]
    for n in ("mix_pre_g", "mix_post_g", "mlp_pre_g", "mlp_post_g"):
        rows += grads[n]
    rows += [jnp.concatenate([grads["attn_out_g"][l], grads["conv_out_g"][l]], axis=1) for l in range(DEPTH)]
    rows.append(jnp.concatenate(grads["conv_w"], axis=1))
    rows.append(_pad_cols(jnp.concatenate(grads["sinks"])[None, :]))
    rows.append(_pad_cols(loss_tile[:1]))
    packed = jnp.concatenate(rows, axis=0)
    packed = jnp.pad(packed, ((0, SMALL_ROWS - packed.shape[0]), (0, 0)))
    total = _sum_devices(packed)
    r0 = N_META
    small = {
        "meta_tokens": lax.dynamic_slice(total[:N_META], (0, chip * (D_MODEL // N_CHIPS)), (N_META, D_MODEL // N_CHIPS)),
        "mix_pre_g": total[r0:r0 + 2], "mix_post_g": total[r0 + 2:r0 + 4], "mlp_pre_g": total[r0 + 4:r0 + 6],
        "mlp_post_g": total[r0 + 6:r0 + 8],
        "attn_out_g": total[r0 + 8:r0 + 10, :ATTN_W], "conv_out_g": total[r0 + 8:r0 + 10, ATTN_W:],
        "conv_w": lax.dynamic_slice(total[r0 + 10:r0 + 13].reshape(CONV_K, DEPTH, CONV_W).transpose(1, 0, 2),
                                    (0, 0, chip * (CONV_W // N_CHIPS)), (DEPTH, CONV_K, CONV_W // N_CHIPS)),
        "sinks": total[r0 + 13, :DEPTH * N_Q_HEADS].reshape(DEPTH, N_Q_HEADS),
    }
    loss = total[r0 + 14, 0]

    ds, nms, nvs = _adamw_small([w[n] for n in _SMALL], [small[n] for n in _SMALL], [m[n] for n in _SMALL],
                                [v[n] for n in _SMALL])
    done1 = adamw(1, dict(zip(_LARGE, red1.done(red0b.token))), None)
    red0a.join(ds[0], grad_x, *[done1[n][0] for n in _LARGE])
    red0b.join(red0a.token)
    done0 = adamw(0, dict(zip(first0, red0a.done(red0b.token))), done1)
    done0.update(adamw(0, {"w_in": red0b.done(done0["w_down"][0])[0]}, done1))
    grad, delta, new_m, new_v = {}, {}, {}, {}
    for n in _LARGE:
        grad[n], delta[n], new_m[n], new_v[n] = done0[n]
    for d in (grad, delta, new_m, new_v):
        d["w_in"] = jnp.swapaxes(d["w_in"], 1, 2)
    for i, n in enumerate(_SMALL):
        grad[n], delta[n], new_m[n], new_v[n] = small[n], ds[i], nms[i], nvs[i]
    return (loss, grad_x, *[grad[n] for n in _ORDER], *[delta[n] for n in _ORDER], *[new_m[n] for n in _ORDER],
            *[new_v[n] for n in _ORDER])
```

```python
import functools

import jax
import jax.numpy as jnp
from jax import lax
from jax.experimental import pallas as pl
from jax.experimental.pallas import tpu as pltpu

F32 = jnp.float32
BF16 = jnp.bfloat16

D_MODEL = 1024
DEPTH = 2
N_META = 16
ATTN_W = 512
CONV_W = 512
HEAD_DIM = 64
N_Q_HEADS = 8
N_KV_HEADS = 2
GROUP = N_Q_HEADS // N_KV_HEADS
KV_W = N_KV_HEADS * HEAD_DIM
CONV_K = 3
BLOCK = 128
LEAD_PAD = BLOCK - N_META
ROPE_THETA = 500000.0
ROT_DIM = HEAD_DIM // 4
ROT_HALF = ROT_DIM // 2
D_FF = 4 * D_MODEL
IN_W = ATTN_W + 2 * KV_W + 3 * CONV_W
QKV_W = ATTN_W + 2 * KV_W
EPS = 1e-6
SCALE = HEAD_DIM ** -0.5
FF_CHUNK = 1024
N_CHIPS = 4
N_DEV = 8

ADAM_LR = 0.001
ADAM_B1 = 0.9
ADAM_B2 = 0.999
ADAM_EPS = 1e-08
ADAM_WD = 0.01
ADAM_STEP = 10

V7X_VMEM_LIMIT = 60 * 1024 * 1024
SMALL_ROWS = 32

MESH = pl.DeviceIdType.MESH


def _params(*sem):
    return pltpu.CompilerParams(dimension_semantics=sem, vmem_limit_bytes=V7X_VMEM_LIMIT)


def _block_rows(n):
    return max(r for r in range(16, min(n, 256) + 1, 16) if n % r == 0)


def _row_tile(t, most):
    nb = t // BLOCK
    for b in range(most // BLOCK, 0, -1):
        if nb % b == 0:
            return b * BLOCK
    return BLOCK


def _behind(body, deps):
    n = len(deps)

    def wrapped(*refs):
        body(*refs[n:])

    return wrapped, [pl.BlockSpec(memory_space=pl.ANY)] * n


def _rms(x, g):
    r = lax.rsqrt(jnp.mean(x * x, axis=-1, keepdims=True) + EPS)
    return x * r * g


def _rms_bwd(dy, x, g):
    r = lax.rsqrt(jnp.mean(x * x, axis=-1, keepdims=True) + EPS)
    xh = x * r
    dg = jnp.sum(dy * xh, axis=0, keepdims=True)
    dxh = dy * g
    dx = r * (dxh - xh * jnp.mean(dxh * xh, axis=-1, keepdims=True))
    return dx, dg


def _rope(x, cos, sa, sb):
    n = x.shape[-1]
    return x * cos + pltpu.roll(x, n - ROT_HALF, 1) * sa + pltpu.roll(x, ROT_HALF, 1) * sb


def _rope_bwd(dy, cos, sa, sb):
    n = dy.shape[-1]
    return dy * cos + pltpu.roll(dy * sa, ROT_HALF, 1) + pltpu.roll(dy * sb, n - ROT_HALF, 1)


def _rope_tables(t):
    pos = lax.broadcasted_iota(jnp.int32, (t, ROT_HALF), 0).astype(F32) - LEAD_PAD
    pair = lax.broadcasted_iota(jnp.int32, (t, ROT_HALF), 1).astype(F32)
    inv_freq = jnp.power(jnp.float32(ROPE_THETA), -(2.0 * pair) / ROT_DIM)
    ang = pos * inv_freq
    cos, sin = lax.optimization_barrier((jnp.cos(ang), jnp.sin(ang)))
    spread = (1, 2 * HEAD_DIM // ROT_HALF)
    cos, sin = jnp.tile(cos, spread), jnp.tile(sin, spread)
    dim = lax.broadcasted_iota(jnp.int32, (t, 2 * HEAD_DIM), 1) % HEAD_DIM
    return (jnp.where(dim < ROT_DIM, cos, 1.0), jnp.where(dim < ROT_HALF, -sin, 0.0),
            jnp.where((dim >= ROT_HALF) & (dim < ROT_DIM), sin, 0.0))


def _in_proj(h, g, w, tabs, tm):
    t = h.shape[0]

    def body(h_ref, g_ref, w_ref, c_ref, sa_ref, sb_ref, a_ref, q_ref, k_ref, v_ref, b_ref, cg_ref, hc_ref):
        a = _rms(h_ref[...], g_ref[...]).astype(BF16)
        a_ref[...] = a
        p = lax.dot_general(a, w_ref[...], (((1,), (1,)), ((), ())), preferred_element_type=F32)
        cos, sa, sb = c_ref[...], sa_ref[...], sb_ref[...]
        rep = ATTN_W // (2 * HEAD_DIM)
        q = _rope(p[:, :ATTN_W], jnp.tile(cos, (1, rep)), jnp.tile(sa, (1, rep)), jnp.tile(sb, (1, rep)))
        q_ref[...] = (q * SCALE).astype(BF16)
        k_ref[...] = _rope(p[:, ATTN_W:ATTN_W + KV_W], cos, sa, sb).astype(BF16)
        v_ref[...] = p[:, ATTN_W + KV_W:QKV_W].astype(BF16)
        b_ref[...] = p[:, QKV_W:QKV_W + CONV_W].astype(BF16)
        cg_ref[...] = p[:, QKV_W + CONV_W:QKV_W + 2 * CONV_W].astype(BF16)
        hc_ref[...] = p[:, QKV_W + 2 * CONV_W:].astype(BF16)

    row = lambda n: pl.BlockSpec((tm, n), lambda i: (i, 0))
    full = lambda a: pl.BlockSpec(a.shape, lambda i: (0, 0))
    return pl.pallas_call(
        body, name="in_proj", grid=(t // tm,),
        in_specs=[row(D_MODEL), full(g), full(w), row(2 * HEAD_DIM), row(2 * HEAD_DIM), row(2 * HEAD_DIM)],
        out_specs=[row(D_MODEL), row(ATTN_W), row(KV_W), row(KV_W), row(CONV_W), row(CONV_W), row(CONV_W)],
        out_shape=[jax.ShapeDtypeStruct((t, D_MODEL), BF16), jax.ShapeDtypeStruct((t, ATTN_W), BF16),
                   jax.ShapeDtypeStruct((t, KV_W), BF16), jax.ShapeDtypeStruct((t, KV_W), BF16),
                   jax.ShapeDtypeStruct((t, CONV_W), BF16), jax.ShapeDtypeStruct((t, CONV_W), BF16),
                   jax.ShapeDtypeStruct((t, CONV_W), BF16)],
        compiler_params=_params("parallel"),
    )(h, g, w, *tabs)


def _attn_bias():
    r = lax.broadcasted_iota(jnp.int32, (3, BLOCK, 2 * BLOCK), 1)
    c = lax.broadcasted_iota(jnp.int32, (3, BLOCK, 2 * BLOCK), 2)
    i = lax.broadcasted_iota(jnp.int32, (3, BLOCK, 2 * BLOCK), 0)
    ok = (c > r) & (c <= r + BLOCK) & (c + (i - 1) * BLOCK >= LEAD_PAD)
    return jnp.where(ok, 0.0, -jnp.inf).astype(F32)


def _attn_scores(qh, kg, bias):
    return lax.dot_general(qh, kg, (((1,), (1,)), ((), ())), preferred_element_type=F32) + bias


def _attn_probs(s, sk):
    m = jnp.maximum(jnp.max(s, axis=-1, keepdims=True), sk)
    e = jnp.exp(s - m)
    es = jnp.exp(sk - m)
    rden = 1.0 / (jnp.sum(e, axis=-1, keepdims=True) + es)
    return e * rden, es * rden


def _head(hh):
    return slice(hh * HEAD_DIM, (hh + 1) * HEAD_DIM)


def _two_blocks(ref, i):
    prev = jnp.maximum(i - 1, 0)
    return jnp.concatenate([ref[pl.ds(pl.multiple_of(prev * BLOCK, BLOCK), BLOCK), :],
                            ref[pl.ds(pl.multiple_of(i * BLOCK, BLOCK), BLOCK), :]], axis=0)


def _attn_fwd(q, k, v, bias, sinks, tm):
    t = q.shape[0]
    per_step = tm // BLOCK
    heads = range(N_Q_HEADS)

    def body(s_ref, q_ref, k_ref, v_ref, bias_ref, o_ref):
        for b in range(per_step):
            i = pl.program_id(0) * per_step + b
            rows = slice(b * BLOCK, (b + 1) * BLOCK)
            kc, vc = _two_blocks(k_ref, i), _two_blocks(v_ref, i)
            bias_i = bias_ref[jnp.minimum(i, 2)]
            scores = [_attn_scores(q_ref[rows, _head(hh)], kc[:, _head(hh // GROUP)], bias_i) for hh in heads]
            probs = [_attn_probs(scores[hh], s_ref[hh])[0].astype(BF16) for hh in heads]
            for hh in heads:
                o_ref[rows, _head(hh)] = jnp.dot(probs[hh], vc[:, _head(hh // GROUP)],
                                                 preferred_element_type=F32).astype(BF16)

    whole = pl.BlockSpec((t, KV_W), lambda i: (0, 0))
    return pl.pallas_call(
        body, name="attn_fwd", grid=(t // tm,),
        in_specs=[pl.BlockSpec(memory_space=pltpu.SMEM), pl.BlockSpec((tm, ATTN_W), lambda i: (i, 0)), whole, whole,
                  pl.BlockSpec(bias.shape, lambda i: (0, 0, 0))],
        out_specs=pl.BlockSpec((tm, ATTN_W), lambda i: (i, 0)),
        out_shape=jax.ShapeDtypeStruct((t, ATTN_W), BF16),
        compiler_params=_params("parallel"),
    )(sinks, q, k, v, bias)


def _shift_rows(u, halo, n):
    r = pltpu.roll(u, n, 0)
    hr = pltpu.roll(halo, n, 0)
    idx = lax.broadcasted_iota(jnp.int32, hr.shape, 0)
    return jnp.concatenate([jnp.where(idx < n, hr, r[:8]), r[8:]], axis=0)


def _advance_rows(u, halo, n):
    rows = u.shape[0]
    r = pltpu.roll(u, rows - n, 0)
    hr = pltpu.roll(halo, 8 - n, 0)
    idx = lax.broadcasted_iota(jnp.int32, hr.shape, 0)
    return jnp.concatenate([r[:rows - 8], jnp.where(idx >= 8 - n, hr, r[rows - 8:])], axis=0)


def _mix_out(h, o, b, c, hc, cw, ga, gc, w, gp, tm, deps=()):
    t = h.shape[0]

    def body(h_ref, o_ref, b_ref, c_ref, hc_ref, cw_ref, ga_ref, gc_ref, w_ref, gp_ref, h1_ref, y_ref, z_ref, halo):
        @pl.when(pl.program_id(0) == 0)
        def _():
            halo[...] = jnp.zeros_like(halo)

        u = c_ref[...].astype(F32) * hc_ref[...].astype(F32)
        cv = cw_ref[0:1, :] * _shift_rows(u, halo[...], 2) + cw_ref[1:2, :] * _shift_rows(u, halo[...], 1) \
            + cw_ref[2:3, :] * u
        halo[...] = u[tm - 8:]
        yc = b_ref[...].astype(F32) * cv
        y = jnp.concatenate([_rms(o_ref[...].astype(F32), ga_ref[...]), _rms(yc, gc_ref[...])], axis=1).astype(BF16)
        y_ref[...] = y
        z = jnp.dot(y, w_ref[...].reshape(D_MODEL, D_MODEL), preferred_element_type=F32)
        z_ref[...] = z
        h1_ref[...] = h_ref[...] + _rms(z, gp_ref[...])

    row = lambda n: pl.BlockSpec((tm, n), lambda i: (i, 0))
    full = lambda a: pl.BlockSpec(a.shape, lambda i: (0,) * a.ndim)
    body, dep_specs = _behind(body, deps)
    return pl.pallas_call(
        body, name="mix_out", grid=(t // tm,),
        in_specs=dep_specs + [row(D_MODEL), row(ATTN_W), row(CONV_W), row(CONV_W), row(CONV_W), full(cw), full(ga),
                              full(gc), full(w), full(gp)],
        out_specs=[row(D_MODEL), row(D_MODEL), row(D_MODEL)],
        out_shape=[jax.ShapeDtypeStruct((t, D_MODEL), F32), jax.ShapeDtypeStruct((t, D_MODEL), BF16),
                   jax.ShapeDtypeStruct((t, D_MODEL), F32)],
        scratch_shapes=[pltpu.VMEM((8, CONV_W), F32)],
        compiler_params=_params("arbitrary"),
    )(*deps, h, o, b, c, hc, cw, ga, gc, w, gp)


def _mlp(h1, g1, wu, wd, g2, tm, target=None):
    t = h1.shape[0]
    nj = D_FF // FF_CHUNK
    per_step = tm // BLOCK if target is not None else 0

    def body(h1_ref, g1_ref, wu_ref, wd_ref, g2_ref, *rest):
        t_refs, outs = rest[:per_step], rest[per_step:]
        a2_ref, slope_ref, f_ref = outs[-3:]
        a2 = _rms(h1_ref[...], g1_ref[...]).astype(BF16)
        a2_ref[...] = a2
        f = None
        for j in range(nj):
            up = jnp.dot(a2, wu_ref[j], preferred_element_type=F32)
            r = jnp.maximum(up, 0.0)
            slope_ref[:, j * FF_CHUNK:(j + 1) * FF_CHUNK] = (r + r).astype(BF16)
            part = jnp.dot((r * r).astype(BF16), wd_ref[j], preferred_element_type=F32)
            f = part if f is None else f + part
        f_ref[...] = f
        h2 = h1_ref[...] + _rms(f, g2_ref[...])
        if target is None:
            outs[0][...] = h2
            return
        loss_ref, dh_ref = outs[:2]
        i = pl.program_id(0)

        @pl.when(i == 0)
        def _():
            loss_ref[...] = jnp.zeros_like(loss_ref)

        total = jnp.zeros((), F32)
        for b in range(per_step):
            rows = slice(b * BLOCK, (b + 1) * BLOCK)
            err = h2[rows] - t_refs[b][...]
            if b == 0:
                err = jnp.where(i == 0, 0.0, err)
            dh_ref[rows, :] = err * (1.0 / D_MODEL)
            total = total + jnp.sum(err * err)
        loss_ref[...] += total * (0.5 / D_MODEL)

    def target_block(b):
        return pl.BlockSpec((BLOCK, D_MODEL), lambda i: (jnp.maximum(i * per_step + b - 1, 0), 0))

    row = pl.BlockSpec((tm, D_MODEL), lambda i: (i, 0))
    vec = pl.BlockSpec((1, D_MODEL), lambda i: (0, 0))
    resident = pl.BlockSpec(memory_space=pltpu.VMEM)
    first_specs, first_shapes = [row], [jax.ShapeDtypeStruct((t, D_MODEL), F32)]
    if target is not None:
        first_specs = [pl.BlockSpec((8, 128), lambda i: (0, 0)), row]
        first_shapes = [jax.ShapeDtypeStruct((8, 128), F32), jax.ShapeDtypeStruct((t, D_MODEL), F32)]
    outs = pl.pallas_call(
        body, name="mlp", grid=(t // tm,),
        in_specs=[row, vec, resident, resident, vec] + [target_block(b) for b in range(per_step)],
        out_specs=first_specs + [row, pl.BlockSpec((tm, D_FF), lambda i: (i, 0)), row],
        out_shape=first_shapes + [jax.ShapeDtypeStruct((t, D_MODEL), BF16), jax.ShapeDtypeStruct((t, D_FF), BF16),
                                  jax.ShapeDtypeStruct((t, D_MODEL), F32)],
        compiler_params=_params("parallel" if target is None else "arbitrary"),
    )(h1, g1, wu, wd, g2, *([target] * per_step))
    return (outs[0] if target is None else tuple(outs[:2]),) + tuple(outs[-3:])


def _mlp_bwd_hidden(dh2, f, g2, slope, wd, tm, deps=()):
    t = dh2.shape[0]
    nj = D_FF // FF_CHUNK

    def body(dh2_ref, f_ref, g2_ref, slope_ref, wd_ref, df_ref, dup_ref, dg2_ref):
        @pl.when(pl.program_id(0) == 0)
        def _():
            dg2_ref[...] = jnp.zeros_like(dg2_ref)

        df, dg = _rms_bwd(dh2_ref[...], f_ref[...], g2_ref[...])
        dg2_ref[...] += dg
        df = df.astype(BF16)
        df_ref[...] = df
        for j in range(nj):
            cols = slice(j * FF_CHUNK, (j + 1) * FF_CHUNK)
            dact = lax.dot_general(df, wd_ref[j], (((1,), (1,)), ((), ())), preferred_element_type=F32)
            dup_ref[:, cols] = (dact * slope_ref[:, cols].astype(F32)).astype(BF16)

    row = pl.BlockSpec((tm, D_MODEL), lambda i: (i, 0))
    wide = pl.BlockSpec((tm, D_FF), lambda i: (i, 0))
    vec = pl.BlockSpec((1, D_MODEL), lambda i: (0, 0))
    body, dep_specs = _behind(body, deps)
    return pl.pallas_call(
        body, name="mlp_bwd_hidden", grid=(t // tm,),
        in_specs=dep_specs + [row, row, vec, wide, pl.BlockSpec(memory_space=pltpu.VMEM)],
        out_specs=[row, wide, vec],
        out_shape=[jax.ShapeDtypeStruct((t, D_MODEL), BF16), jax.ShapeDtypeStruct((t, D_FF), BF16),
                   jax.ShapeDtypeStruct((1, D_MODEL), F32)],
        compiler_params=_params("arbitrary"),
    )(*deps, dh2, f, g2, slope, wd)


def _mlp_bwd_input(dup, wu, h1, g1, dh2, tm):
    t = dh2.shape[0]
    nj = D_FF // FF_CHUNK

    def body(dup_ref, wu_ref, h1_ref, g1_ref, dh2_ref, dh1_ref, dg1_ref):
        @pl.when(pl.program_id(0) == 0)
        def _():
            dg1_ref[...] = jnp.zeros_like(dg1_ref)

        da2 = None
        for j in range(nj):
            part = lax.dot_general(dup_ref[:, j * FF_CHUNK:(j + 1) * FF_CHUNK], wu_ref[j], (((1,), (1,)), ((), ())),
                                   preferred_element_type=F32)
            da2 = part if da2 is None else da2 + part
        dx, dg = _rms_bwd(da2, h1_ref[...], g1_ref[...])
        dh1_ref[...] = dh2_ref[...] + dx
        dg1_ref[...] += dg

    row = pl.BlockSpec((tm, D_MODEL), lambda i: (i, 0))
    vec = pl.BlockSpec((1, D_MODEL), lambda i: (0, 0))
    return pl.pallas_call(
        body, name="mlp_bwd_input", grid=(t // tm,),
        in_specs=[pl.BlockSpec((tm, D_FF), lambda i: (i, 0)), pl.BlockSpec(memory_space=pltpu.VMEM), row, vec, row],
        out_specs=[row, vec],
        out_shape=[jax.ShapeDtypeStruct((t, D_MODEL), F32), jax.ShapeDtypeStruct((1, D_MODEL), F32)],
        compiler_params=_params("arbitrary"),
    )(dup, wu, h1, g1, dh2)


def _row_split(t):
    tile = min(t, 1024)
    return tile, t // tile, t % tile


def _row_split_specs(t, cols):
    tile, whole, rest = _row_split(t)
    specs = [pl.BlockSpec((tile, cols), lambda r: (jnp.minimum(r, whole - 1), 0))]
    if rest:
        specs.append(pl.BlockSpec((rest, cols), lambda r: (whole * tile // rest, 0)))
    return specs


def _weight_grad(x, y, name, x_is_slope=False):
    t, k = x.shape
    n = y.shape[1]
    tn = FF_CHUNK
    tk = FF_CHUNK if k % FF_CHUNK == 0 else k
    _, whole, rest = _row_split(t)
    steps = whole + bool(rest)

    def body(*refs):
        o_ref, ob_ref, r = refs[-2], refs[-1], pl.program_id(0)

        @pl.when(r == 0)
        def _():
            o_ref[...] = jnp.zeros_like(o_ref)

        def add(x_ref, y_ref):
            for a in range(k // tk):
                xv = x_ref[:, a * tk:(a + 1) * tk]
                if x_is_slope:
                    xv = xv.astype(F32)
                    xv = (xv * xv * 0.25).astype(BF16)
                for b in range(n // tn):
                    o_ref[a, b] += lax.dot_general(xv, y_ref[:, b * tn:(b + 1) * tn], (((0,), (0,)), ((), ())),
                                                   preferred_element_type=F32)

        if rest:
            pl.when(r < whole)(lambda: add(refs[0], refs[2]))
            pl.when(r == whole)(lambda: add(refs[1], refs[3]))
        else:
            add(refs[0], refs[1])

        @pl.when(r == steps - 1)
        def _():
            ob_ref[...] = o_ref[...].astype(BF16)

    vm = pl.BlockSpec(memory_space=pltpu.VMEM)
    return pl.pallas_call(
        body, name=name, grid=(steps,),
        in_specs=_row_split_specs(t, k) + _row_split_specs(t, n), out_specs=[vm, vm],
        out_shape=[jax.ShapeDtypeStruct((k // tk, n // tn, tk, tn), F32),
                   jax.ShapeDtypeStruct((k // tk, n // tn, tk, tn), BF16)],
        compiler_params=_params("arbitrary"),
    )(*([x] * (1 + bool(rest))), *([y] * (1 + bool(rest))))


def _mix_out_bwd(dh1, z, gp, w, o, b, c, hc, cw, ga, gc, tm, deps=()):
    t = dh1.shape[0]
    nt = t // tm
    per16 = tm // 16

    def body(dh1_ref, z_ref, gp_ref, w_ref, o_ref, b_ref, c_ref, hc_ref, cp_ref, hp_ref, cw_ref, ga_ref, gc_ref,
             dz_ref, do_ref, dbch_ref, dgp_ref, dga_ref, dgc_ref, dcw_ref, halo):
        i = pl.program_id(0)

        @pl.when(i == 0)
        def _():
            halo[...] = jnp.zeros_like(halo)
            dgp_ref[...] = jnp.zeros_like(dgp_ref)
            dga_ref[...] = jnp.zeros_like(dga_ref)
            dgc_ref[...] = jnp.zeros_like(dgc_ref)
            dcw_ref[...] = jnp.zeros_like(dcw_ref)

        dz, dgp = _rms_bwd(dh1_ref[...], z_ref[...], gp_ref[...])
        dgp_ref[...] += dgp
        dz = dz.astype(BF16)
        dz_ref[...] = dz
        dy = lax.dot_general(dz, w_ref[...].reshape(D_MODEL, D_MODEL), (((1,), (1,)), ((), ())),
                             preferred_element_type=F32)
        do, dga = _rms_bwd(dy[:, :ATTN_W], o_ref[...].astype(F32), ga_ref[...])
        do_ref[...] = do.astype(BF16)
        dga_ref[...] += dga

        cc, hh = c_ref[...].astype(F32), hc_ref[...].astype(F32)
        u = cc * hh
        first = i == nt - 1
        u_before = jnp.where(first, 0.0, (cp_ref[...].astype(F32) * hp_ref[...].astype(F32))[8:])
        u1 = _shift_rows(u, u_before, 1)
        u2 = _shift_rows(u, u_before, 2)
        cv = cw_ref[0:1, :] * u2 + cw_ref[1:2, :] * u1 + cw_ref[2:3, :] * u
        bb = b_ref[...].astype(F32)
        dyc, dgc = _rms_bwd(dy[:, ATTN_W:], bb * cv, gc_ref[...])
        dgc_ref[...] += dgc
        dcv = dyc * bb
        d1 = _advance_rows(dcv, halo[...], 1)
        d2 = _advance_rows(dcv, halo[...], 2)
        halo[...] = dcv[:8]
        du = cw_ref[2:3, :] * dcv + cw_ref[1:2, :] * d1 + cw_ref[0:1, :] * d2
        dbch_ref[...] = jnp.concatenate([dyc * cv, du * hh, du * cc], axis=1).astype(BF16)
        dcw_ref[...] += jnp.concatenate([jnp.sum(dcv * u2, axis=0, keepdims=True),
                                         jnp.sum(dcv * u1, axis=0, keepdims=True),
                                         jnp.sum(dcv * u, axis=0, keepdims=True)], axis=0)

    row = lambda n: pl.BlockSpec((tm, n), lambda i: (nt - 1 - i, 0))
    before = pl.BlockSpec((16, CONV_W), lambda i: (jnp.maximum((nt - 1 - i) * per16 - 1, 0), 0))
    full = lambda a: pl.BlockSpec(a.shape, lambda i: (0,) * a.ndim)
    vec = lambda n: pl.BlockSpec((1, n), lambda i: (0, 0))
    body, dep_specs = _behind(body, deps)
    return pl.pallas_call(
        body, name="mix_out_bwd", grid=(nt,),
        in_specs=dep_specs + [row(D_MODEL), row(D_MODEL), full(gp), full(w), row(ATTN_W), row(CONV_W), row(CONV_W),
                              row(CONV_W), before, before, full(cw), full(ga), full(gc)],
        out_specs=[row(D_MODEL), row(ATTN_W), row(3 * CONV_W), vec(D_MODEL), vec(ATTN_W), vec(CONV_W),
                   pl.BlockSpec((CONV_K, CONV_W), lambda i: (0, 0))],
        out_shape=[jax.ShapeDtypeStruct((t, D_MODEL), BF16), jax.ShapeDtypeStruct((t, ATTN_W), BF16),
                   jax.ShapeDtypeStruct((t, 3 * CONV_W), BF16), jax.ShapeDtypeStruct((1, D_MODEL), F32),
                   jax.ShapeDtypeStruct((1, ATTN_W), F32), jax.ShapeDtypeStruct((1, CONV_W), F32),
                   jax.ShapeDtypeStruct((CONV_K, CONV_W), F32)],
        scratch_shapes=[pltpu.VMEM((8, CONV_W), F32)],
        compiler_params=_params("arbitrary"),
    )(*deps, dh1, z, gp, w, o, b, c, hc, c, hc, cw, ga, gc)


def _attn_bwd(q, k, v, o, do, bias, sinks, tm, deps=()):
    t = q.shape[0]
    per_step = tm // BLOCK

    def body(s_ref, q_ref, k_ref, v_ref, o_ref, do_ref, bias_ref, dq_ref, dk_ref, dv_ref, ds_ref):
        step = pl.program_id(0)

        @pl.when(step == 0)
        def _():
            ds_ref[...] = jnp.zeros_like(ds_ref)

        heads = range(N_Q_HEADS)

        def first_matmuls(b):
            i = step * per_step + b
            rows = slice(b * BLOCK, (b + 1) * BLOCK)
            kc, vc = _two_blocks(k_ref, i), _two_blocks(v_ref, i)
            bias_i = bias_ref[jnp.minimum(i, 2)]
            kgs = [kc[:, _head(g)] for g in range(N_KV_HEADS)]
            vgs = [vc[:, _head(g)] for g in range(N_KV_HEADS)]
            qs = [q_ref[rows, _head(hh)] for hh in heads]
            dosb = [do_ref[rows, _head(hh)] for hh in heads]
            dos = [d.astype(F32) for d in dosb]
            scores = [_attn_scores(qs[hh], kgs[hh // GROUP], bias_i) for hh in heads]
            dps = [lax.dot_general(dosb[hh], vgs[hh // GROUP], (((1,), (1,)), ((), ())), preferred_element_type=F32)
                   for hh in heads]
            return kgs, qs, dos, dosb, scores, dps

        dsink = [jnp.zeros((BLOCK, 1), F32) for _ in range(N_Q_HEADS)]
        ahead = None
        for b in range(per_step):
            i = step * per_step + b
            rows = slice(b * BLOCK, (b + 1) * BLOCK)
            kgs, qs, dos, dosb, scores, dps = first_matmuls(b)
            ps, dss = [], []
            for hh in heads:
                p, share = _attn_probs(scores[hh], s_ref[hh])
                drow = jnp.sum(dos[hh] * o_ref[rows, _head(hh)].astype(F32), axis=-1, keepdims=True)
                dss.append((p * (dps[hh] - drow)).astype(BF16))
                ps.append(p.astype(BF16))
                dsink[hh] = dsink[hh] + share * drow
            for hh in heads:
                dq_ref[rows, _head(hh)] = (jnp.dot(dss[hh], kgs[hh // GROUP], preferred_element_type=F32)
                                           * SCALE).astype(BF16)
            groups = [slice(GROUP * g, GROUP * (g + 1)) for g in range(N_KV_HEADS)]
            dkg = [lax.dot_general(jnp.concatenate(dss[gr], axis=0), jnp.concatenate(qs[gr], axis=0),
                                   (((0,), (0,)), ((), ())), preferred_element_type=F32) for gr in groups]
            dvg = [lax.dot_general(jnp.concatenate(ps[gr], axis=0), jnp.concatenate(dosb[gr], axis=0),
                                   (((0,), (0,)), ((), ())), preferred_element_type=F32) for gr in groups]
            dkb, dvb = jnp.concatenate(dkg, axis=1), jnp.concatenate(dvg, axis=1)
            if b == 0:
                @pl.when(step > 0)
                def _():
                    before = pl.ds(pl.multiple_of((i - 1) * BLOCK, BLOCK), BLOCK)
                    dk_ref[before, :] += dkb[:BLOCK]
                    dv_ref[before, :] += dvb[:BLOCK]
            else:
                at = pl.ds(pl.multiple_of((i - 1) * BLOCK, BLOCK), BLOCK)
                dk_ref[at, :] = ahead[0] + dkb[:BLOCK]
                dv_ref[at, :] = ahead[1] + dvb[:BLOCK]
            ahead = (dkb[BLOCK:], dvb[BLOCK:])
        last = pl.ds(pl.multiple_of(((step + 1) * per_step - 1) * BLOCK, BLOCK), BLOCK)
        dk_ref[last, :] = ahead[0]
        dv_ref[last, :] = ahead[1]
        for hh in range(N_Q_HEADS):
            ds_ref[hh:hh + 1, :] -= jnp.sum(dsink[hh])

    whole = pl.BlockSpec((t, KV_W), lambda i: (0, 0))
    blk = pl.BlockSpec((tm, ATTN_W), lambda i: (i, 0))
    body, dep_specs = _behind(body, deps)
    return pl.pallas_call(
        body, name="attn_bwd", grid=(t // tm,),
        in_specs=dep_specs + [pl.BlockSpec(memory_space=pltpu.SMEM), blk, whole, whole, blk, blk,
                              pl.BlockSpec(bias.shape, lambda i: (0, 0, 0))],
        out_specs=[blk, whole, whole, pl.BlockSpec((N_Q_HEADS, 128), lambda i: (0, 0))],
        out_shape=[jax.ShapeDtypeStruct((t, ATTN_W), BF16), jax.ShapeDtypeStruct((t, KV_W), F32),
                   jax.ShapeDtypeStruct((t, KV_W), F32), jax.ShapeDtypeStruct((N_Q_HEADS, 128), F32)],
        compiler_params=_params("arbitrary"),
    )(*deps, sinks, q, k, v, o, do, bias)


def _in_proj_bwd(dq, dk, dv, dbch, w, dh1, h, g, tabs, tm):
    t = h.shape[0]

    def body(dq_ref, dk_ref, dv_ref, dbch_ref, w_ref, dh1_ref, h_ref, g_ref, c_ref, sa_ref, sb_ref, dh_ref, dp_ref,
             dg_ref):
        @pl.when(pl.program_id(0) == 0)
        def _():
            dg_ref[...] = jnp.zeros_like(dg_ref)

        cos, sa, sb = c_ref[...], sa_ref[...], sb_ref[...]
        rep = ATTN_W // (2 * HEAD_DIM)
        dqr = _rope_bwd(dq_ref[...].astype(F32), jnp.tile(cos, (1, rep)), jnp.tile(sa, (1, rep)),
                        jnp.tile(sb, (1, rep)))
        dkr = _rope_bwd(dk_ref[...], cos, sa, sb)
        dp = jnp.concatenate([dqr.astype(BF16), dkr.astype(BF16), dv_ref[...].astype(BF16), dbch_ref[...]], axis=1)
        dp_ref[...] = dp
        da = jnp.dot(dp, w_ref[...], preferred_element_type=F32)
        dx, dg = _rms_bwd(da, h_ref[...], g_ref[...])
        dh_ref[...] = dh1_ref[...] + dx
        dg_ref[...] += dg

    row = lambda n: pl.BlockSpec((tm, n), lambda i: (i, 0))
    full = lambda a: pl.BlockSpec(a.shape, lambda i: (0, 0))
    return pl.pallas_call(
        body, name="in_proj_bwd", grid=(t // tm,),
        in_specs=[row(ATTN_W), row(KV_W), row(KV_W), row(3 * CONV_W), full(w), row(D_MODEL), row(D_MODEL), full(g),
                  row(2 * HEAD_DIM), row(2 * HEAD_DIM), row(2 * HEAD_DIM)],
        out_specs=[row(D_MODEL), row(IN_W), pl.BlockSpec((1, D_MODEL), lambda i: (0, 0))],
        out_shape=[jax.ShapeDtypeStruct((t, D_MODEL), F32), jax.ShapeDtypeStruct((t, IN_W), BF16),
                   jax.ShapeDtypeStruct((1, D_MODEL), F32)],
        compiler_params=_params("arbitrary"),
    )(dq, dk, dv, dbch, w, dh1, h, g, *tabs)


class _Tiles:
    def __init__(self, t):
        self.tm = _row_tile(t, 640)
        self.ts = self.tm
        self.tabs = _rope_tables(t)
        self.bias = _attn_bias()


def _mixer_fwd(h, p, tl):
    a, q, k, v, b, c, hc = _in_proj(h, p["mix_pre_g"], p["w_in"], tl.tabs, tl.ts)
    o = _attn_fwd(q, k, v, tl.bias, p["sinks"], tl.tm)
    return (h, a, q, k, v, b, c, hc, o)


def _out_fwd(mixed, p, tl, deps=()):
    h, a, q, k, v, b, c, hc, o = mixed
    h1, y, z = _mix_out(h, o, b, c, hc, p["conv_w"], p["attn_out_g"], p["conv_out_g"], p["w_out"], p["mix_post_g"],
                        tl.ts, deps)
    return h1, mixed + (h1, y, z)


def _mlp_fwd(h1, saved, p, tl, target=None):
    h2, a2, slope, f = _mlp(h1, p["mlp_pre_g"], p["w_up"], p["w_down"], p["mlp_post_g"], tl.tm, target)
    return h2, saved + (a2, slope, f)


def _mlp_part_bwd(dh, saved, p, tl, deps=()):
    h1, a2, slope, f = saved[9], saved[12], saved[13], saved[14]
    df, dup, dg2 = _mlp_bwd_hidden(dh, f, p["mlp_post_g"], slope, p["w_down"], tl.tm, deps)
    dh1, dg1 = _mlp_bwd_input(dup, p["w_up"], h1, p["mlp_pre_g"], dh, tl.tm)
    g = {"w_down": [d.reshape(N_CHIPS, FF_CHUNK, D_MODEL)
                    for d in _weight_grad(slope, df, "grad_w_down", x_is_slope=True)],
         "w_up": [d.reshape(N_CHIPS, D_MODEL, FF_CHUNK) for d in _weight_grad(a2, dup, "grad_w_up")],
         "mlp_post_g": dg2, "mlp_pre_g": dg1}
    return dh1, g


def _mix_out_part_bwd(dh1, saved, p, tl, deps=()):
    b, c, hc, o, y, z = saved[5], saved[6], saved[7], saved[8], saved[10], saved[11]
    dz, do, dbch, dgp, dga, dgc, dcw = _mix_out_bwd(dh1, z, p["mix_post_g"], p["w_out"], o, b, c, hc, p["conv_w"],
                                                    p["attn_out_g"], p["conv_out_g"], tl.ts, deps)
    g = {"w_out": [d.reshape(N_CHIPS, D_MODEL // N_CHIPS, D_MODEL) for d in _weight_grad(y, dz, "grad_w_out")],
         "mix_post_g": dgp, "attn_out_g": dga, "conv_out_g": dgc, "conv_w": dcw}
    return (dh1, do, dbch), g


def _attn_in_part_bwd(carry, saved, p, tl, deps=()):
    dh1, do, dbch = carry
    h_in, a, q, k, v, o = saved[0], saved[1], saved[2], saved[3], saved[4], saved[8]
    dq, dk, dv, dsink = _attn_bwd(q, k, v, o, do, tl.bias, p["sinks"], tl.tm, deps)
    dh, dproj, dgi = _in_proj_bwd(dq, dk, dv, dbch, p["w_in"], dh1, h_in, p["mix_pre_g"], tl.tabs, tl.ts)
    g_in = [d.reshape(N_CHIPS, IN_W // N_CHIPS, D_MODEL) for d in _weight_grad(dproj, a, "grad_w_in")]
    return dh, {"w_in": g_in, "mix_pre_g": dgi, "sinks": dsink[:, 0]}


def _place():
    return lax.axis_index("x"), lax.axis_index("y"), lax.axis_index("c")


def _other_chips(x, y):
    return [(1 - x, y), (x, 1 - y), (1 - x, 1 - y)]


_HBM = pl.BlockSpec(memory_space=pltpu.HBM)
_SEM = pl.BlockSpec(memory_space=pltpu.SEMAPHORE)
_EFFECT = pltpu.SideEffectType.DATAFLOW_SIDE_EFFECTING


class _Exchange:
    def __init__(self, name, bufs, plan, n, after=()):
        self.name, self.plan, nb = name, plan, len(bufs)
        n_in = nb + len(after)

        def body(*refs):
            send, recv, token = refs[n_in], refs[n_in + 1], refs[-1]
            for k, (src, dst, target, _) in enumerate(plan(refs[:nb])):
                pltpu.make_async_remote_copy(src_ref=src, dst_ref=dst, send_sem=send.at[k], recv_sem=recv.at[k],
                                             device_id=target, device_id_type=MESH).start()
            token[...] = jnp.zeros_like(token)

        outs = pl.pallas_call(
            body, name=name + "_start",
            out_shape=(pltpu.SemaphoreType.DMA((n,)), pltpu.SemaphoreType.DMA((n,)),
                       *[pltpu.HBM(b.shape, b.dtype) for b in bufs], jax.ShapeDtypeStruct((8, 128), F32)),
            in_specs=[_HBM] * nb + [pl.BlockSpec(memory_space=pl.ANY)] * len(after),
            out_specs=(_SEM, _SEM, *[_HBM] * nb, pl.BlockSpec(memory_space=pltpu.VMEM)),
            input_output_aliases={i: 2 + i for i in range(nb)},
            compiler_params=pltpu.CompilerParams(has_side_effects=_EFFECT),
        )(*[pltpu.with_memory_space_constraint(b, pltpu.HBM) for b in bufs], *after)
        self.send, self.recv, self.bufs, self.token = outs[0], outs[1], list(outs[2:2 + nb]), outs[-1]

    def wait(self, *after):
        plan, nb = self.plan, len(self.bufs)

        def body(*refs):
            send, recv = refs[nb], refs[nb + 1]
            for k, (src, _, target, land) in enumerate(plan(refs[:nb])):
                cp = pltpu.make_async_remote_copy(src_ref=src, dst_ref=land, send_sem=send.at[k], recv_sem=recv.at[k],
                                                  device_id=target, device_id_type=MESH)
                cp.wait_send()
                cp.wait_recv()

        outs = pl.pallas_call(
            body, name=self.name + "_wait", out_shape=[pltpu.HBM(b.shape, b.dtype) for b in self.bufs],
            in_specs=[_HBM] * nb + [_SEM, _SEM] + [pl.BlockSpec(memory_space=pl.ANY)] * len(after),
            out_specs=[_HBM] * nb, input_output_aliases={i: i for i in range(nb)},
            compiler_params=pltpu.CompilerParams(has_side_effects=_EFFECT),
        )(*self.bufs, self.send, self.recv, *after)
        return list(outs)


def _gather_plan(n):
    def plan(refs):
        x, y, c = _place()
        me = 2 * x + y
        return [(refs[a].at[me], refs[a].at[me], (px, py, c), refs[a].at[2 * px + py])
                for a in range(n) for px, py in _other_chips(x, y)]

    return plan


def _peers():
    x, y, c = _place()
    return [(k - 1, (x ^ (k >> 2), y ^ ((k >> 1) & 1), c ^ (k & 1))) for k in range(1, N_DEV)]


def _scatter_plan(n, half_rows):
    def plan(refs):
        out = []
        for a in range(n):
            hr = half_rows[a]
            for k, (px, py, pc) in _peers():
                out.append((refs[a].at[2 * px + py, pl.ds(pc * hr, hr)], refs[n + a].at[k], (px, py, pc),
                            refs[n + a].at[k]))
        return out

    return plan


def _join_plan(n):
    def plan(refs):
        x, y, c = _place()
        return [(refs[a].at[c], refs[a].at[c], (x, y, 1 - c), refs[a].at[1 - c]) for a in range(n)]

    return plan


def _sum_parts(g, q):
    rows, cols = g.shape[1], g.shape[2]
    hr = rows // 2
    tr = _block_rows(hr)
    per = hr // tr
    x, y, c = _place()
    where = jnp.stack([2 * x + y, c]).astype(jnp.int32)

    def body(where_ref, g_ref, q_ref, o_ref):
        total = g_ref[...]
        for k in range(N_DEV - 1):
            total = total + q_ref[k].astype(F32)
        o_ref[...] = total

    return pl.pallas_call(
        body, name="sum_parts",
        grid_spec=pltpu.PrefetchScalarGridSpec(
            num_scalar_prefetch=1, grid=(per,),
            in_specs=[pl.BlockSpec((None, tr, cols), lambda i, where_ref: (where_ref[0], where_ref[1] * per + i, 0)),
                      pl.BlockSpec((N_DEV - 1, tr, cols), lambda i, where_ref: (0, i, 0))],
            out_specs=pl.BlockSpec((None, tr, cols), lambda i, where_ref: (where_ref[1], i, 0))),
        out_shape=jax.ShapeDtypeStruct((2, hr, cols), F32),
        compiler_params=_params("parallel"),
    )(where, g, q)


def _sum_devices(packed):
    def body(p_ref, o_ref, land, send_sems, recv_sems):
        x, y, c = _place()
        me = 4 * x + 2 * y + c
        land[me] = p_ref[...]
        sends = []
        for k in range(1, N_DEV):
            px, py, pc = x ^ (k >> 2), y ^ ((k >> 1) & 1), c ^ (k & 1)
            cp = pltpu.make_async_remote_copy(src_ref=p_ref, dst_ref=land.at[me], send_sem=send_sems.at[k - 1],
                                              recv_sem=recv_sems.at[k - 1], device_id=(px, py, pc), device_id_type=MESH)
            cp.start()
            sends.append(cp)
        for k in range(1, N_DEV):
            px, py, pc = x ^ (k >> 2), y ^ ((k >> 1) & 1), c ^ (k & 1)
            pltpu.make_async_remote_copy(src_ref=p_ref, dst_ref=land.at[4 * px + 2 * py + pc],
                                         send_sem=send_sems.at[k - 1], recv_sem=recv_sems.at[k - 1],
                                         device_id=(px, py, pc), device_id_type=MESH).wait_recv()
        for cp in sends:
            cp.wait_send()
        total = land[0]
        for d in range(1, N_DEV):
            total = total + land[d]
        o_ref[...] = total

    vm = pl.BlockSpec(memory_space=pltpu.VMEM)
    return pl.pallas_call(
        body, name="sum_devices", in_specs=[vm], out_specs=vm,
        out_shape=jax.ShapeDtypeStruct(packed.shape, F32),
        scratch_shapes=[pltpu.VMEM((N_DEV,) + packed.shape, F32), pltpu.SemaphoreType.DMA((N_DEV - 1,)),
                        pltpu.SemaphoreType.DMA((N_DEV - 1,))],
    )(packed)


def _adamw_math(w, g, m, v):
    m = ADAM_B1 * m + (1.0 - ADAM_B1) * g
    v = ADAM_B2 * v + (1.0 - ADAM_B2) * jnp.square(g)
    m_hat = m / (1.0 - ADAM_B1 ** ADAM_STEP)
    v_hat = v / (1.0 - ADAM_B2 ** ADAM_STEP)
    delta = -ADAM_LR * (m_hat / (jnp.sqrt(v_hat) + ADAM_EPS) + ADAM_WD * w)
    return delta, m, v


def _adamw_large(layer, w, halves, m, v, other):
    _, rows, cols = w.shape
    tr = _block_rows(rows // 2)
    per = rows // 2 // tr

    def body(w_ref, g_ref, m_ref, v_ref, *rest):
        g_out, d_ref, nm_ref, nv_ref = rest[-4:]
        g = g_ref[...]
        g_out[...] = g
        d_ref[...], nm_ref[...], nv_ref[...] = _adamw_math(w_ref[...], g, m_ref[...], v_ref[...])

    blk = pl.BlockSpec((None, tr, cols), lambda i: (layer, i, 0))
    half = pl.BlockSpec((None, tr, cols), lambda i: (i // per, i % per, 0))
    kept = [] if other is None else list(other)
    return pl.pallas_call(
        body, name="adamw_large", grid=(rows // tr,),
        in_specs=[blk, half, blk, blk] + [pl.BlockSpec(memory_space=pl.ANY)] * len(kept), out_specs=[blk] * 4,
        out_shape=[jax.ShapeDtypeStruct(w.shape, F32)] * 4,
        input_output_aliases={4 + k: k for k in range(len(kept))},
        compiler_params=_params("parallel"),
    )(w, halves, m, v, *kept)


def _adamw_small(ws, gs, ms, vs):
    n = len(ws)

    def body(*refs):
        w_r, g_r, m_r, v_r = refs[:n], refs[n:2 * n], refs[2 * n:3 * n], refs[3 * n:4 * n]
        d_r, nm_r, nv_r = refs[4 * n:5 * n], refs[5 * n:6 * n], refs[6 * n:]
        for a in range(n):
            d_r[a][...], nm_r[a][...], nv_r[a][...] = _adamw_math(w_r[a][...], g_r[a][...], m_r[a][...], v_r[a][...])

    vm = pl.BlockSpec(memory_space=pltpu.VMEM)
    outs = pl.pallas_call(
        body, name="adamw_small", in_specs=[vm] * (4 * n), out_specs=[vm] * (3 * n),
        out_shape=[jax.ShapeDtypeStruct(w.shape, F32) for w in ws] * 3,
    )(*ws, *gs, *ms, *vs)
    return outs[:n], outs[n:2 * n], outs[2 * n:]


_LARGE = ("w_in", "w_out", "w_up", "w_down")
_SMALL = ("meta_tokens", "mix_pre_g", "conv_w", "sinks", "attn_out_g", "conv_out_g", "mix_post_g", "mlp_pre_g",
          "mlp_post_g")
_ORDER = ("meta_tokens", "mix_pre_g", "w_in", "conv_w", "sinks", "attn_out_g", "conv_out_g", "w_out", "mix_post_g",
          "mlp_pre_g", "w_up", "w_down", "mlp_post_g")


class _Reduce:
    def __init__(self, name, grads, after=()):
        self.name, self.n = name, len(grads)
        self.own = [g for g, _ in grads]
        half_rows = [g.shape[1] // 2 for g in self.own]
        zones = [lax.empty((N_DEV - 1, hr, g.shape[2]), BF16) for g, hr in zip(self.own, half_rows)]
        self.exchange = _Exchange(name + "_scatter", [b for _, b in grads] + zones, _scatter_plan(self.n, half_rows),
                                  (N_DEV - 1) * self.n, after)

    @property
    def token(self):
        return self.exchange.token

    def join(self, *after):
        bufs = self.exchange.wait(*after)
        halves = [_sum_parts(g, q) for g, q in zip(self.own, bufs[self.n:])]
        self.exchange = _Exchange(self.name + "_join", halves, _join_plan(self.n), self.n)

    def done(self, *after):
        return self.exchange.wait(*after)


def _pad_cols(a, n=D_MODEL):
    return jnp.pad(a, ((0, 0), (0, n - a.shape[1])))


def kernel(x, meta_tokens, mix_pre_g, w_in, conv_w, sinks, attn_out_g, conv_out_g, w_out, mix_post_g, mlp_pre_g, w_up, w_down, mlp_post_g, loss_target, m_meta_tokens, m_mix_pre_g, m_w_in, m_conv_w, m_sinks, m_attn_out_g, m_conv_out_g, m_w_out, m_mix_post_g, m_mlp_pre_g, m_w_up, m_w_down, m_mlp_post_g, v_meta_tokens, v_mix_pre_g, v_w_in, v_conv_w, v_sinks, v_attn_out_g, v_conv_out_g, v_w_out, v_mix_post_g, v_mlp_pre_g, v_w_up, v_w_down, v_mlp_post_g):
    w = dict(meta_tokens=meta_tokens, mix_pre_g=mix_pre_g, w_in=w_in, conv_w=conv_w, sinks=sinks,
             attn_out_g=attn_out_g, conv_out_g=conv_out_g, w_out=w_out, mix_post_g=mix_post_g, mlp_pre_g=mlp_pre_g,
             w_up=w_up, w_down=w_down, mlp_post_g=mlp_post_g)
    m = dict(meta_tokens=m_meta_tokens, mix_pre_g=m_mix_pre_g, w_in=m_w_in, conv_w=m_conv_w, sinks=m_sinks,
             attn_out_g=m_attn_out_g, conv_out_g=m_conv_out_g, w_out=m_w_out, mix_post_g=m_mix_post_g,
             mlp_pre_g=m_mlp_pre_g, w_up=m_w_up, w_down=m_w_down, mlp_post_g=m_mlp_post_g)
    v = dict(meta_tokens=v_meta_tokens, mix_pre_g=v_mix_pre_g, w_in=v_w_in, conv_w=v_conv_w, sinks=v_sinks,
             attn_out_g=v_attn_out_g, conv_out_g=v_conv_out_g, w_out=v_w_out, mix_post_g=v_mix_post_g,
             mlp_pre_g=v_mlp_pre_g, w_up=v_w_up, w_down=v_w_down, mlp_post_g=v_mlp_post_g)
    chip = 2 * lax.axis_index("x") + lax.axis_index("y")
    tl = _Tiles(x.shape[1] + BLOCK)

    def zone(quarter):
        return lax.dynamic_update_slice(lax.empty((N_CHIPS,) + quarter.shape, quarter.dtype), quarter[None],
                                        (chip,) + (0,) * quarter.ndim)

    w, m, v = ({**d, "w_in": jnp.swapaxes(d["w_in"], 1, 2)} for d in (w, m, v))
    zones = {n: [zone(w[n][l].astype(BF16)) for l in range(DEPTH)] for n in _LARGE}
    first = _Exchange("gather_first", [zones["w_in"][0], zone(w["conv_w"]), zone(w["meta_tokens"])], _gather_plan(3), 9)
    out0 = _Exchange("gather_out", [zones["w_out"][0]], _gather_plan(1), 3, [first.token])
    rest = _Exchange("gather_rest", [zones[n][0] for n in ("w_up", "w_down")], _gather_plan(2), 6, [out0.token])

    def whole_in(quarters):
        return quarters.reshape(IN_W, D_MODEL)

    h = jnp.concatenate([jnp.zeros((BLOCK, D_MODEL), F32), x[0]], axis=0)
    q_in, q_conv, q_meta = first.wait(rest.token, *tl.tabs, tl.bias, h)
    conv_whole = jnp.transpose(q_conv, (1, 2, 0, 3)).reshape(DEPTH, CONV_K, CONV_W)
    meta = jnp.transpose(q_meta, (1, 0, 2)).reshape(N_META, D_MODEL)
    p = [{"conv_w": conv_whole[l], "sinks": w["sinks"][l]} for l in range(DEPTH)]
    for l in range(DEPTH):
        for n in ("mix_pre_g", "attn_out_g", "conv_out_g", "mix_post_g", "mlp_pre_g", "mlp_post_g"):
            p[l][n] = w[n][l][None, :]

    h = lax.dynamic_update_slice(h, meta, (LEAD_PAD, 0))
    p[0]["w_in"] = whole_in(q_in)
    mixed = _mixer_fwd(h, p[0], tl)
    second = _Exchange("gather_second", [zones["w_in"][1], zones["w_out"][1]], _gather_plan(2), 6, [mixed[-1]])
    second_mlp = _Exchange("gather_second_mlp", [zones["w_up"][1], zones["w_down"][1]], _gather_plan(2), 6,
                           [second.token])
    p[0]["w_out"], = out0.wait(second_mlp.token)
    h1, saved0 = _out_fwd(mixed, p[0], tl)
    p[0]["w_up"], p[0]["w_down"] = rest.wait(h1)
    h, saved0 = _mlp_fwd(h1, saved0, p[0], tl)
    q_in, p[1]["w_out"] = second.wait(h)
    p[1]["w_in"] = whole_in(q_in)
    h1, saved1 = _out_fwd(_mixer_fwd(h, p[1], tl), p[1], tl)
    p[1]["w_up"], p[1]["w_down"] = second_mlp.wait(h1)
    (loss_tile, dh), saved1 = _mlp_fwd(h1, saved1, p[1], tl, loss_target[0])

    def adamw(layer, halves, other):
        return {n: _adamw_large(layer, w[n], halves[n], m[n], v[n], None if other is None else other[n])
                for n in halves}

    dh1, g1 = _mlp_part_bwd(dh, saved1, p[1], tl)
    carry, gm = _mix_out_part_bwd(dh1, saved1, p[1], tl)
    dh, gi = _attn_in_part_bwd(carry, saved1, p[1], tl)
    g1.update(gm, **gi)
    red1 = _Reduce("reduce1", [g1[n] for n in _LARGE])
    dh1, g0 = _mlp_part_bwd(dh, saved0, p[0], tl, [red1.token])
    red1.join(g0["w_down"][0])
    carry, gm = _mix_out_part_bwd(dh1, saved0, p[0], tl, [red1.token])
    first0 = ("w_up", "w_down", "w_out")
    g0.update(gm)
    red0a = _Reduce("reduce0a", [g0[n] for n in first0])
    dh0, gi = _attn_in_part_bwd(carry, saved0, p[0], tl, [red0a.token])
    g0.update(gi)
    red0b = _Reduce("reduce0b", [g0["w_in"]])
    grad_x = dh0[BLOCK:][None]
    grads = {n: [g0[n], g1[n]] for n in g0 if n not in _LARGE}

    rows = [dh0[LEAD_PAD:BLOCK]]
    for n in ("mix_pre_g", "mix_post_g", "mlp_pre_g", "mlp_post_g"):
        rows += grads[n]
    rows += [jnp.concatenate([grads["attn_out_g"][l], grads["conv_out_g"][l]], axis=1) for l in range(DEPTH)]
    rows.append(jnp.concatenate(grads["conv_w"], axis=1))
    rows.append(_pad_cols(jnp.concatenate(grads["sinks"])[None, :]))
    rows.append(_pad_cols(loss_tile[:1]))
    packed = jnp.concatenate(rows, axis=0)
    packed = jnp.pad(packed, ((0, SMALL_ROWS - packed.shape[0]), (0, 0)))
    total = _sum_devices(packed)
    r0 = N_META
    small = {
        "meta_tokens": lax.dynamic_slice(total[:N_META], (0, chip * (D_MODEL // N_CHIPS)), (N_META, D_MODEL // N_CHIPS)),
        "mix_pre_g": total[r0:r0 + 2], "mix_post_g": total[r0 + 2:r0 + 4], "mlp_pre_g": total[r0 + 4:r0 + 6],
        "mlp_post_g": total[r0 + 6:r0 + 8],
        "attn_out_g": total[r0 + 8:r0 + 10, :ATTN_W], "conv_out_g": total[r0 + 8:r0 + 10, ATTN_W:],
        "conv_w": lax.dynamic_slice(total[r0 + 10:r0 + 13].reshape(CONV_K, DEPTH, CONV_W).transpose(1, 0, 2),
                                    (0, 0, chip * (CONV_W // N_CHIPS)), (DEPTH, CONV_K, CONV_W // N_CHIPS)),
        "sinks": total[r0 + 13, :DEPTH * N_Q_HEADS].reshape(DEPTH, N_Q_HEADS),
    }
    loss = total[r0 + 14, 0]

    ds, nms, nvs = _adamw_small([w[n] for n in _SMALL], [small[n] for n in _SMALL], [m[n] for n in _SMALL],
                                [v[n] for n in _SMALL])
    done1 = adamw(1, dict(zip(_LARGE, red1.done(red0b.token))), None)
    red0a.join(ds[0], grad_x, *[done1[n][0] for n in _LARGE])
    red0b.join(red0a.token)
    done0 = adamw(0, dict(zip(first0, red0a.done(red0b.token))), done1)
    done0.update(adamw(0, {"w_in": red0b.done(done0["w_down"][0])[0]}, done1))
    grad, delta, new_m, new_v = {}, {}, {}, {}
    for n in _LARGE:
        grad[n], delta[n], new_m[n], new_v[n] = done0[n]
    for d in (grad, delta, new_m, new_v):
        d["w_in"] = jnp.swapaxes(d["w_in"], 1, 2)
    for i, n in enumerate(_SMALL):
        grad[n], delta[n], new_m[n], new_v[n] = small[n], ds[i], nms[i], nvs[i]
    return (loss, grad_x, *[grad[n] for n in _ORDER], *[delta[n] for n in _ORDER], *[new_m[n] for n in _ORDER],
            *[new_v[n] for n in _ORDER])
```

```python
import functools

import jax
import jax.numpy as jnp
from jax import lax
from jax.experimental import pallas as pl
from jax.experimental.pallas import tpu as pltpu

F32 = jnp.float32
BF16 = jnp.bfloat16

D_MODEL = 1024
DEPTH = 2
N_META = 16
ATTN_W = 512
CONV_W = 512
HEAD_DIM = 64
N_Q_HEADS = 8
N_KV_HEADS = 2
GROUP = N_Q_HEADS // N_KV_HEADS
KV_W = N_KV_HEADS * HEAD_DIM
CONV_K = 3
BLOCK = 128
LEAD_PAD = BLOCK - N_META
ROPE_THETA = 500000.0
ROT_DIM = HEAD_DIM // 4
ROT_HALF = ROT_DIM // 2
D_FF = 4 * D_MODEL
IN_W = ATTN_W + 2 * KV_W + 3 * CONV_W
QKV_W = ATTN_W + 2 * KV_W
EPS = 1e-6
SCALE = HEAD_DIM ** -0.5
FF_CHUNK = 1024
N_CHIPS = 4
N_DEV = 8

ADAM_LR = 0.001
ADAM_B1 = 0.9
ADAM_B2 = 0.999
ADAM_EPS = 1e-08
ADAM_WD = 0.01
ADAM_STEP = 10

V7X_VMEM_LIMIT = 60 * 1024 * 1024
SMALL_ROWS = 32

MESH = pl.DeviceIdType.MESH


def _params(*sem):
    return pltpu.CompilerParams(dimension_semantics=sem, vmem_limit_bytes=V7X_VMEM_LIMIT)


def _block_rows(n):
    return max(r for r in range(16, min(n, 256) + 1, 16) if n % r == 0)


def _row_tile(t, most):
    nb = t // BLOCK
    for b in range(most // BLOCK, 0, -1):
        if nb % b == 0:
            return b * BLOCK
    return BLOCK


def _behind(body, deps):
    n = len(deps)

    def wrapped(*refs):
        body(*refs[n:])

    return wrapped, [pl.BlockSpec(memory_space=pl.ANY)] * n


def _rms(x, g):
    r = lax.rsqrt(jnp.mean(x * x, axis=-1, keepdims=True) + EPS)
    return x * r * g


def _rms_bwd(dy, x, g):
    r = lax.rsqrt(jnp.mean(x * x, axis=-1, keepdims=True) + EPS)
    xh = x * r
    dg = jnp.sum(dy * xh, axis=0, keepdims=True)
    dxh = dy * g
    dx = r * (dxh - xh * jnp.mean(dxh * xh, axis=-1, keepdims=True))
    return dx, dg


def _rope(x, cos, sa, sb):
    n = x.shape[-1]
    return x * cos + pltpu.roll(x, n - ROT_HALF, 1) * sa + pltpu.roll(x, ROT_HALF, 1) * sb


def _rope_bwd(dy, cos, sa, sb):
    n = dy.shape[-1]
    return dy * cos + pltpu.roll(dy * sa, ROT_HALF, 1) + pltpu.roll(dy * sb, n - ROT_HALF, 1)


def _rope_tables(t):
    pos = lax.broadcasted_iota(jnp.int32, (t, ROT_HALF), 0).astype(F32) - LEAD_PAD
    pair = lax.broadcasted_iota(jnp.int32, (t, ROT_HALF), 1).astype(F32)
    inv_freq = jnp.power(jnp.float32(ROPE_THETA), -(2.0 * pair) / ROT_DIM)
    ang = pos * inv_freq
    cos, sin = lax.optimization_barrier((jnp.cos(ang), jnp.sin(ang)))
    spread = (1, 2 * HEAD_DIM // ROT_HALF)
    cos, sin = jnp.tile(cos, spread), jnp.tile(sin, spread)
    dim = lax.broadcasted_iota(jnp.int32, (t, 2 * HEAD_DIM), 1) % HEAD_DIM
    return (jnp.where(dim < ROT_DIM, cos, 1.0), jnp.where(dim < ROT_HALF, -sin, 0.0),
            jnp.where((dim >= ROT_HALF) & (dim < ROT_DIM), sin, 0.0))


def _in_proj(h, g, w, tabs, tm):
    t = h.shape[0]

    def body(h_ref, g_ref, w_ref, c_ref, sa_ref, sb_ref, a_ref, q_ref, k_ref, v_ref, b_ref, cg_ref, hc_ref):
        a = _rms(h_ref[...], g_ref[...]).astype(BF16)
        a_ref[...] = a
        p = lax.dot_general(a, w_ref[...], (((1,), (1,)), ((), ())), preferred_element_type=F32)
        cos, sa, sb = c_ref[...], sa_ref[...], sb_ref[...]
        rep = ATTN_W // (2 * HEAD_DIM)
        q = _rope(p[:, :ATTN_W], jnp.tile(cos, (1, rep)), jnp.tile(sa, (1, rep)), jnp.tile(sb, (1, rep)))
        q_ref[...] = (q * SCALE).astype(BF16)
        k_ref[...] = _rope(p[:, ATTN_W:ATTN_W + KV_W], cos, sa, sb).astype(BF16)
        v_ref[...] = p[:, ATTN_W + KV_W:QKV_W].astype(BF16)
        b_ref[...] = p[:, QKV_W:QKV_W + CONV_W].astype(BF16)
        cg_ref[...] = p[:, QKV_W + CONV_W:QKV_W + 2 * CONV_W].astype(BF16)
        hc_ref[...] = p[:, QKV_W + 2 * CONV_W:].astype(BF16)

    row = lambda n: pl.BlockSpec((tm, n), lambda i: (i, 0))
    full = lambda a: pl.BlockSpec(a.shape, lambda i: (0, 0))
    return pl.pallas_call(
        body, name="in_proj", grid=(t // tm,),
        in_specs=[row(D_MODEL), full(g), full(w), row(2 * HEAD_DIM), row(2 * HEAD_DIM), row(2 * HEAD_DIM)],
        out_specs=[row(D_MODEL), row(ATTN_W), row(KV_W), row(KV_W), row(CONV_W), row(CONV_W), row(CONV_W)],
        out_shape=[jax.ShapeDtypeStruct((t, D_MODEL), BF16), jax.ShapeDtypeStruct((t, ATTN_W), BF16),
                   jax.ShapeDtypeStruct((t, KV_W), BF16), jax.ShapeDtypeStruct((t, KV_W), BF16),
                   jax.ShapeDtypeStruct((t, CONV_W), BF16), jax.ShapeDtypeStruct((t, CONV_W), BF16),
                   jax.ShapeDtypeStruct((t, CONV_W), BF16)],
        compiler_params=_params("parallel"),
    )(h, g, w, *tabs)


def _attn_bias():
    r = lax.broadcasted_iota(jnp.int32, (3, BLOCK, 2 * BLOCK), 1)
    c = lax.broadcasted_iota(jnp.int32, (3, BLOCK, 2 * BLOCK), 2)
    i = lax.broadcasted_iota(jnp.int32, (3, BLOCK, 2 * BLOCK), 0)
    ok = (c > r) & (c <= r + BLOCK) & (c + (i - 1) * BLOCK >= LEAD_PAD)
    return jnp.where(ok, 0.0, -jnp.inf).astype(F32)


def _attn_scores(qh, kg, bias):
    return lax.dot_general(qh, kg, (((1,), (1,)), ((), ())), preferred_element_type=F32) + bias


def _attn_probs(s, sk):
    m = jnp.maximum(jnp.max(s, axis=-1, keepdims=True), sk)
    e = jnp.exp(s - m)
    es = jnp.exp(sk - m)
    rden = 1.0 / (jnp.sum(e, axis=-1, keepdims=True) + es)
    return e * rden, es * rden


def _head(hh):
    return slice(hh * HEAD_DIM, (hh + 1) * HEAD_DIM)


def _two_blocks(ref, i):
    prev = jnp.maximum(i - 1, 0)
    return jnp.concatenate([ref[pl.ds(pl.multiple_of(prev * BLOCK, BLOCK), BLOCK), :],
                            ref[pl.ds(pl.multiple_of(i * BLOCK, BLOCK), BLOCK), :]], axis=0)


def _attn_fwd(q, k, v, bias, sinks, tm):
    t = q.shape[0]
    per_step = tm // BLOCK
    heads = range(N_Q_HEADS)

    def body(s_ref, q_ref, k_ref, v_ref, bias_ref, o_ref):
        for b in range(per_step):
            i = pl.program_id(0) * per_step + b
            rows = slice(b * BLOCK, (b + 1) * BLOCK)
            kc, vc = _two_blocks(k_ref, i), _two_blocks(v_ref, i)
            bias_i = bias_ref[jnp.minimum(i, 2)]
            scores = [_attn_scores(q_ref[rows, _head(hh)], kc[:, _head(hh // GROUP)], bias_i) for hh in heads]
            probs = [_attn_probs(scores[hh], s_ref[hh])[0].astype(BF16) for hh in heads]
            for hh in heads:
                o_ref[rows, _head(hh)] = jnp.dot(probs[hh], vc[:, _head(hh // GROUP)],
                                                 preferred_element_type=F32).astype(BF16)

    whole = pl.BlockSpec((t, KV_W), lambda i: (0, 0))
    return pl.pallas_call(
        body, name="attn_fwd", grid=(t // tm,),
        in_specs=[pl.BlockSpec(memory_space=pltpu.SMEM), pl.BlockSpec((tm, ATTN_W), lambda i: (i, 0)), whole, whole,
                  pl.BlockSpec(bias.shape, lambda i: (0, 0, 0))],
        out_specs=pl.BlockSpec((tm, ATTN_W), lambda i: (i, 0)),
        out_shape=jax.ShapeDtypeStruct((t, ATTN_W), BF16),
        compiler_params=_params("parallel"),
    )(sinks, q, k, v, bias)


def _shift_rows(u, halo, n):
    r = pltpu.roll(u, n, 0)
    hr = pltpu.roll(halo, n, 0)
    idx = lax.broadcasted_iota(jnp.int32, hr.shape, 0)
    return jnp.concatenate([jnp.where(idx < n, hr, r[:8]), r[8:]], axis=0)


def _advance_rows(u, halo, n):
    rows = u.shape[0]
    r = pltpu.roll(u, rows - n, 0)
    hr = pltpu.roll(halo, 8 - n, 0)
    idx = lax.broadcasted_iota(jnp.int32, hr.shape, 0)
    return jnp.concatenate([r[:rows - 8], jnp.where(idx >= 8 - n, hr, r[rows - 8:])], axis=0)


def _mix_out(h, o, b, c, hc, cw, ga, gc, w, gp, tm, deps=()):
    t = h.shape[0]

    def body(h_ref, o_ref, b_ref, c_ref, hc_ref, cw_ref, ga_ref, gc_ref, w_ref, gp_ref, h1_ref, y_ref, z_ref, halo):
        @pl.when(pl.program_id(0) == 0)
        def _():
            halo[...] = jnp.zeros_like(halo)

        u = c_ref[...].astype(F32) * hc_ref[...].astype(F32)
        cv = cw_ref[0:1, :] * _shift_rows(u, halo[...], 2) + cw_ref[1:2, :] * _shift_rows(u, halo[...], 1) \
            + cw_ref[2:3, :] * u
        halo[...] = u[tm - 8:]
        yc = b_ref[...].astype(F32) * cv
        y = jnp.concatenate([_rms(o_ref[...].astype(F32), ga_ref[...]), _rms(yc, gc_ref[...])], axis=1).astype(BF16)
        y_ref[...] = y
        z = jnp.dot(y, w_ref[...].reshape(D_MODEL, D_MODEL), preferred_element_type=F32)
        z_ref[...] = z
        h1_ref[...] = h_ref[...] + _rms(z, gp_ref[...])

    row = lambda n: pl.BlockSpec((tm, n), lambda i: (i, 0))
    full = lambda a: pl.BlockSpec(a.shape, lambda i: (0,) * a.ndim)
    body, dep_specs = _behind(body, deps)
    return pl.pallas_call(
        body, name="mix_out", grid=(t // tm,),
        in_specs=dep_specs + [row(D_MODEL), row(ATTN_W), row(CONV_W), row(CONV_W), row(CONV_W), full(cw), full(ga),
                              full(gc), full(w), full(gp)],
        out_specs=[row(D_MODEL), row(D_MODEL), row(D_MODEL)],
        out_shape=[jax.ShapeDtypeStruct((t, D_MODEL), F32), jax.ShapeDtypeStruct((t, D_MODEL), BF16),
                   jax.ShapeDtypeStruct((t, D_MODEL), F32)],
        scratch_shapes=[pltpu.VMEM((8, CONV_W), F32)],
        compiler_params=_params("arbitrary"),
    )(*deps, h, o, b, c, hc, cw, ga, gc, w, gp)


def _mlp(h1, g1, wu, wd, g2, tm, target=None):
    t = h1.shape[0]
    nj = D_FF // FF_CHUNK
    per_step = tm // BLOCK if target is not None else 0

    def body(h1_ref, g1_ref, wu_ref, wd_ref, g2_ref, *rest):
        t_refs, outs = rest[:per_step], rest[per_step:]
        a2_ref, slope_ref, f_ref = outs[-3:]
        a2 = _rms(h1_ref[...], g1_ref[...]).astype(BF16)
        a2_ref[...] = a2
        f = None
        for j in range(nj):
            up = jnp.dot(a2, wu_ref[j], preferred_element_type=F32)
            r = jnp.maximum(up, 0.0)
            slope_ref[:, j * FF_CHUNK:(j + 1) * FF_CHUNK] = (r + r).astype(BF16)
            part = jnp.dot((r * r).astype(BF16), wd_ref[j], preferred_element_type=F32)
            f = part if f is None else f + part
        f_ref[...] = f
        h2 = h1_ref[...] + _rms(f, g2_ref[...])
        if target is None:
            outs[0][...] = h2
            return
        loss_ref, dh_ref = outs[:2]
        i = pl.program_id(0)

        @pl.when(i == 0)
        def _():
            loss_ref[...] = jnp.zeros_like(loss_ref)

        total = jnp.zeros((), F32)
        for b in range(per_step):
            rows = slice(b * BLOCK, (b + 1) * BLOCK)
            err = h2[rows] - t_refs[b][...]
            if b == 0:
                err = jnp.where(i == 0, 0.0, err)
            dh_ref[rows, :] = err * (1.0 / D_MODEL)
            total = total + jnp.sum(err * err)
        loss_ref[...] += total * (0.5 / D_MODEL)

    def target_block(b):
        return pl.BlockSpec((BLOCK, D_MODEL), lambda i: (jnp.maximum(i * per_step + b - 1, 0), 0))

    row = pl.BlockSpec((tm, D_MODEL), lambda i: (i, 0))
    vec = pl.BlockSpec((1, D_MODEL), lambda i: (0, 0))
    resident = pl.BlockSpec(memory_space=pltpu.VMEM)
    first_specs, first_shapes = [row], [jax.ShapeDtypeStruct((t, D_MODEL), F32)]
    if target is not None:
        first_specs = [pl.BlockSpec((8, 128), lambda i: (0, 0)), row]
        first_shapes = [jax.ShapeDtypeStruct((8, 128), F32), jax.ShapeDtypeStruct((t, D_MODEL), F32)]
    outs = pl.pallas_call(
        body, name="mlp", grid=(t // tm,),
        in_specs=[row, vec, resident, resident, vec] + [target_block(b) for b in range(per_step)],
        out_specs=first_specs + [row, pl.BlockSpec((tm, D_FF), lambda i: (i, 0)), row],
        out_shape=first_shapes + [jax.ShapeDtypeStruct((t, D_MODEL), BF16), jax.ShapeDtypeStruct((t, D_FF), BF16),
                                  jax.ShapeDtypeStruct((t, D_MODEL), F32)],
        compiler_params=_params("parallel" if target is None else "arbitrary"),
    )(h1, g1, wu, wd, g2, *([target] * per_step))
    return (outs[0] if target is None else tuple(outs[:2]),) + tuple(outs[-3:])


def _mlp_bwd_hidden(dh2, f, g2, slope, wd, tm, deps=()):
    t = dh2.shape[0]
    nj = D_FF // FF_CHUNK

    def body(dh2_ref, f_ref, g2_ref, slope_ref, wd_ref, df_ref, dup_ref, dg2_ref):
        @pl.when(pl.program_id(0) == 0)
        def _():
            dg2_ref[...] = jnp.zeros_like(dg2_ref)

        df, dg = _rms_bwd(dh2_ref[...], f_ref[...], g2_ref[...])
        dg2_ref[...] += dg
        df = df.astype(BF16)
        df_ref[...] = df
        for j in range(nj):
            cols = slice(j * FF_CHUNK, (j + 1) * FF_CHUNK)
            dact = lax.dot_general(df, wd_ref[j], (((1,), (1,)), ((), ())), preferred_element_type=F32)
            dup_ref[:, cols] = (dact * slope_ref[:, cols].astype(F32)).astype(BF16)

    row = pl.BlockSpec((tm, D_MODEL), lambda i: (i, 0))
    wide = pl.BlockSpec((tm, D_FF), lambda i: (i, 0))
    vec = pl.BlockSpec((1, D_MODEL), lambda i: (0, 0))
    body, dep_specs = _behind(body, deps)
    return pl.pallas_call(
        body, name="mlp_bwd_hidden", grid=(t // tm,),
        in_specs=dep_specs + [row, row, vec, wide, pl.BlockSpec(memory_space=pltpu.VMEM)],
        out_specs=[row, wide, vec],
        out_shape=[jax.ShapeDtypeStruct((t, D_MODEL), BF16), jax.ShapeDtypeStruct((t, D_FF), BF16),
                   jax.ShapeDtypeStruct((1, D_MODEL), F32)],
        compiler_params=_params("arbitrary"),
    )(*deps, dh2, f, g2, slope, wd)


def _mlp_bwd_input(dup, wu, h1, g1, dh2, tm):
    t = dh2.shape[0]
    nj = D_FF // FF_CHUNK

    def body(dup_ref, wu_ref, h1_ref, g1_ref, dh2_ref, dh1_ref, dg1_ref):
        @pl.when(pl.program_id(0) == 0)
        def _():
            dg1_ref[...] = jnp.zeros_like(dg1_ref)

        da2 = None
        for j in range(nj):
            part = lax.dot_general(dup_ref[:, j * FF_CHUNK:(j + 1) * FF_CHUNK], wu_ref[j], (((1,), (1,)), ((), ())),
                                   preferred_element_type=F32)
            da2 = part if da2 is None else da2 + part
        dx, dg = _rms_bwd(da2, h1_ref[...], g1_ref[...])
        dh1_ref[...] = dh2_ref[...] + dx
        dg1_ref[...] += dg

    row = pl.BlockSpec((tm, D_MODEL), lambda i: (i, 0))
    vec = pl.BlockSpec((1, D_MODEL), lambda i: (0, 0))
    return pl.pallas_call(
        body, name="mlp_bwd_input", grid=(t // tm,),
        in_specs=[pl.BlockSpec((tm, D_FF), lambda i: (i, 0)), pl.BlockSpec(memory_space=pltpu.VMEM), row, vec, row],
        out_specs=[row, vec],
        out_shape=[jax.ShapeDtypeStruct((t, D_MODEL), F32), jax.ShapeDtypeStruct((1, D_MODEL), F32)],
        compiler_params=_params("arbitrary"),
    )(dup, wu, h1, g1, dh2)


def _row_split(t):
    tile = min(t, 1024)
    return tile, t // tile, t % tile


def _row_split_specs(t, cols):
    tile, whole, rest = _row_split(t)
    specs = [pl.BlockSpec((tile, cols), lambda r: (jnp.minimum(r, whole - 1), 0))]
    if rest:
        specs.append(pl.BlockSpec((rest, cols), lambda r: (whole * tile // rest, 0)))
    return specs


def _weight_grad(x, y, name, x_is_slope=False):
    t, k = x.shape
    n = y.shape[1]
    tn = FF_CHUNK
    tk = FF_CHUNK if k % FF_CHUNK == 0 else k
    _, whole, rest = _row_split(t)
    steps = whole + bool(rest)

    def body(*refs):
        o_ref, ob_ref, r = refs[-2], refs[-1], pl.program_id(0)

        @pl.when(r == 0)
        def _():
            o_ref[...] = jnp.zeros_like(o_ref)

        def add(x_ref, y_ref):
            for a in range(k // tk):
                xv = x_ref[:, a * tk:(a + 1) * tk]
                if x_is_slope:
                    xv = xv.astype(F32)
                    xv = (xv * xv * 0.25).astype(BF16)
                for b in range(n // tn):
                    o_ref[a, b] += lax.dot_general(xv, y_ref[:, b * tn:(b + 1) * tn], (((0,), (0,)), ((), ())),
                                                   preferred_element_type=F32)

        if rest:
            pl.when(r < whole)(lambda: add(refs[0], refs[2]))
            pl.when(r == whole)(lambda: add(refs[1], refs[3]))
        else:
            add(refs[0], refs[1])

        @pl.when(r == steps - 1)
        def _():
            ob_ref[...] = o_ref[...].astype(BF16)

    vm = pl.BlockSpec(memory_space=pltpu.VMEM)
    return pl.pallas_call(
        body, name=name, grid=(steps,),
        in_specs=_row_split_specs(t, k) + _row_split_specs(t, n), out_specs=[vm, vm],
        out_shape=[jax.ShapeDtypeStruct((k // tk, n // tn, tk, tn), F32),
                   jax.ShapeDtypeStruct((k // tk, n // tn, tk, tn), BF16)],
        compiler_params=_params("arbitrary"),
    )(*([x] * (1 + bool(rest))), *([y] * (1 + bool(rest))))


def _mix_out_bwd(dh1, z, gp, w, o, b, c, hc, cw, ga, gc, tm, deps=()):
    t = dh1.shape[0]
    nt = t // tm
    per16 = tm // 16

    def body(dh1_ref, z_ref, gp_ref, w_ref, o_ref, b_ref, c_ref, hc_ref, cp_ref, hp_ref, cw_ref, ga_ref, gc_ref,
             dz_ref, do_ref, dbch_ref, dgp_ref, dga_ref, dgc_ref, dcw_ref, halo):
        i = pl.program_id(0)

        @pl.when(i == 0)
        def _():
            halo[...] = jnp.zeros_like(halo)
            dgp_ref[...] = jnp.zeros_like(dgp_ref)
            dga_ref[...] = jnp.zeros_like(dga_ref)
            dgc_ref[...] = jnp.zeros_like(dgc_ref)
            dcw_ref[...] = jnp.zeros_like(dcw_ref)

        dz, dgp = _rms_bwd(dh1_ref[...], z_ref[...], gp_ref[...])
        dgp_ref[...] += dgp
        dz = dz.astype(BF16)
        dz_ref[...] = dz
        dy = lax.dot_general(dz, w_ref[...].reshape(D_MODEL, D_MODEL), (((1,), (1,)), ((), ())),
                             preferred_element_type=F32)
        do, dga = _rms_bwd(dy[:, :ATTN_W], o_ref[...].astype(F32), ga_ref[...])
        do_ref[...] = do.astype(BF16)
        dga_ref[...] += dga

        cc, hh = c_ref[...].astype(F32), hc_ref[...].astype(F32)
        u = cc * hh
        first = i == nt - 1
        u_before = jnp.where(first, 0.0, (cp_ref[...].astype(F32) * hp_ref[...].astype(F32))[8:])
        u1 = _shift_rows(u, u_before, 1)
        u2 = _shift_rows(u, u_before, 2)
        cv = cw_ref[0:1, :] * u2 + cw_ref[1:2, :] * u1 + cw_ref[2:3, :] * u
        bb = b_ref[...].astype(F32)
        dyc, dgc = _rms_bwd(dy[:, ATTN_W:], bb * cv, gc_ref[...])
        dgc_ref[...] += dgc
        dcv = dyc * bb
        d1 = _advance_rows(dcv, halo[...], 1)
        d2 = _advance_rows(dcv, halo[...], 2)
        halo[...] = dcv[:8]
        du = cw_ref[2:3, :] * dcv + cw_ref[1:2, :] * d1 + cw_ref[0:1, :] * d2
        dbch_ref[...] = jnp.concatenate([dyc * cv, du * hh, du * cc], axis=1).astype(BF16)
        dcw_ref[...] += jnp.concatenate([jnp.sum(dcv * u2, axis=0, keepdims=True),
                                         jnp.sum(dcv * u1, axis=0, keepdims=True),
                                         jnp.sum(dcv * u, axis=0, keepdims=True)], axis=0)

    row = lambda n: pl.BlockSpec((tm, n), lambda i: (nt - 1 - i, 0))
    before = pl.BlockSpec((16, CONV_W), lambda i: (jnp.maximum((nt - 1 - i) * per16 - 1, 0), 0))
    full = lambda a: pl.BlockSpec(a.shape, lambda i: (0,) * a.ndim)
    vec = lambda n: pl.BlockSpec((1, n), lambda i: (0, 0))
    body, dep_specs = _behind(body, deps)
    return pl.pallas_call(
        body, name="mix_out_bwd", grid=(nt,),
        in_specs=dep_specs + [row(D_MODEL), row(D_MODEL), full(gp), full(w), row(ATTN_W), row(CONV_W), row(CONV_W),
                              row(CONV_W), before, before, full(cw), full(ga), full(gc)],
        out_specs=[row(D_MODEL), row(ATTN_W), row(3 * CONV_W), vec(D_MODEL), vec(ATTN_W), vec(CONV_W),
                   pl.BlockSpec((CONV_K, CONV_W), lambda i: (0, 0))],
        out_shape=[jax.ShapeDtypeStruct((t, D_MODEL), BF16), jax.ShapeDtypeStruct((t, ATTN_W), BF16),
                   jax.ShapeDtypeStruct((t, 3 * CONV_W), BF16), jax.ShapeDtypeStruct((1, D_MODEL), F32),
                   jax.ShapeDtypeStruct((1, ATTN_W), F32), jax.ShapeDtypeStruct((1, CONV_W), F32),
                   jax.ShapeDtypeStruct((CONV_K, CONV_W), F32)],
        scratch_shapes=[pltpu.VMEM((8, CONV_W), F32)],
        compiler_params=_params("arbitrary"),
    )(*deps, dh1, z, gp, w, o, b, c, hc, c, hc, cw, ga, gc)


def _attn_bwd(q, k, v, o, do, bias, sinks, tm, deps=()):
    t = q.shape[0]
    per_step = tm // BLOCK

    def body(s_ref, q_ref, k_ref, v_ref, o_ref, do_ref, bias_ref, dq_ref, dk_ref, dv_ref, ds_ref):
        step = pl.program_id(0)

        @pl.when(step == 0)
        def _():
            ds_ref[...] = jnp.zeros_like(ds_ref)

        heads = range(N_Q_HEADS)

        def first_matmuls(b):
            i = step * per_step + b
            rows = slice(b * BLOCK, (b + 1) * BLOCK)
            kc, vc = _two_blocks(k_ref, i), _two_blocks(v_ref, i)
            bias_i = bias_ref[jnp.minimum(i, 2)]
            kgs = [kc[:, _head(g)] for g in range(N_KV_HEADS)]
            vgs = [vc[:, _head(g)] for g in range(N_KV_HEADS)]
            qs = [q_ref[rows, _head(hh)] for hh in heads]
            dosb = [do_ref[rows, _head(hh)] for hh in heads]
            dos = [d.astype(F32) for d in dosb]
            scores = [_attn_scores(qs[hh], kgs[hh // GROUP], bias_i) for hh in heads]
            dps = [lax.dot_general(dosb[hh], vgs[hh // GROUP], (((1,), (1,)), ((), ())), preferred_element_type=F32)
                   for hh in heads]
            return kgs, qs, dos, dosb, scores, dps

        dsink = [jnp.zeros((BLOCK, 1), F32) for _ in range(N_Q_HEADS)]
        ahead = None
        for b in range(per_step):
            i = step * per_step + b
            rows = slice(b * BLOCK, (b + 1) * BLOCK)
            kgs, qs, dos, dosb, scores, dps = first_matmuls(b)
            ps, dss = [], []
            for hh in heads:
                p, share = _attn_probs(scores[hh], s_ref[hh])
                drow = jnp.sum(dos[hh] * o_ref[rows, _head(hh)].astype(F32), axis=-1, keepdims=True)
                dss.append((p * (dps[hh] - drow)).astype(BF16))
                ps.append(p.astype(BF16))
                dsink[hh] = dsink[hh] + share * drow
            for hh in heads:
                dq_ref[rows, _head(hh)] = (jnp.dot(dss[hh], kgs[hh // GROUP], preferred_element_type=F32)
                                           * SCALE).astype(BF16)
            groups = [slice(GROUP * g, GROUP * (g + 1)) for g in range(N_KV_HEADS)]
            dkg = [lax.dot_general(jnp.concatenate(dss[gr], axis=0), jnp.concatenate(qs[gr], axis=0),
                                   (((0,), (0,)), ((), ())), preferred_element_type=F32) for gr in groups]
            dvg = [lax.dot_general(jnp.concatenate(ps[gr], axis=0), jnp.concatenate(dosb[gr], axis=0),
                                   (((0,), (0,)), ((), ())), preferred_element_type=F32) for gr in groups]
            dkb, dvb = jnp.concatenate(dkg, axis=1), jnp.concatenate(dvg, axis=1)
            if b == 0:
                @pl.when(step > 0)
                def _():
                    before = pl.ds(pl.multiple_of((i - 1) * BLOCK, BLOCK), BLOCK)
                    dk_ref[before, :] += dkb[:BLOCK]
                    dv_ref[before, :] += dvb[:BLOCK]
            else:
                at = pl.ds(pl.multiple_of((i - 1) * BLOCK, BLOCK), BLOCK)
                dk_ref[at, :] = ahead[0] + dkb[:BLOCK]
                dv_ref[at, :] = ahead[1] + dvb[:BLOCK]
            ahead = (dkb[BLOCK:], dvb[BLOCK:])
        last = pl.ds(pl.multiple_of(((step + 1) * per_step - 1) * BLOCK, BLOCK), BLOCK)
        dk_ref[last, :] = ahead[0]
        dv_ref[last, :] = ahead[1]
        for hh in range(N_Q_HEADS):
            ds_ref[hh:hh + 1, :] -= jnp.sum(dsink[hh])

    whole = pl.BlockSpec((t, KV_W), lambda i: (0, 0))
    blk = pl.BlockSpec((tm, ATTN_W), lambda i: (i, 0))
    body, dep_specs = _behind(body, deps)
    return pl.pallas_call(
        body, name="attn_bwd", grid=(t // tm,),
        in_specs=dep_specs + [pl.BlockSpec(memory_space=pltpu.SMEM), blk, whole, whole, blk, blk,
                              pl.BlockSpec(bias.shape, lambda i: (0, 0, 0))],
        out_specs=[blk, whole, whole, pl.BlockSpec((N_Q_HEADS, 128), lambda i: (0, 0))],
        out_shape=[jax.ShapeDtypeStruct((t, ATTN_W), BF16), jax.ShapeDtypeStruct((t, KV_W), F32),
                   jax.ShapeDtypeStruct((t, KV_W), F32), jax.ShapeDtypeStruct((N_Q_HEADS, 128), F32)],
        compiler_params=_params("arbitrary"),
    )(*deps, sinks, q, k, v, o, do, bias)


def _in_proj_bwd(dq, dk, dv, dbch, w, dh1, h, g, tabs, tm):
    t = h.shape[0]

    def body(dq_ref, dk_ref, dv_ref, dbch_ref, w_ref, dh1_ref, h_ref, g_ref, c_ref, sa_ref, sb_ref, dh_ref, dp_ref,
             dg_ref):
        @pl.when(pl.program_id(0) == 0)
        def _():
            dg_ref[...] = jnp.zeros_like(dg_ref)

        cos, sa, sb = c_ref[...], sa_ref[...], sb_ref[...]
        rep = ATTN_W // (2 * HEAD_DIM)
        dqr = _rope_bwd(dq_ref[...].astype(F32), jnp.tile(cos, (1, rep)), jnp.tile(sa, (1, rep)),
                        jnp.tile(sb, (1, rep)))
        dkr = _rope_bwd(dk_ref[...], cos, sa, sb)
        dp = jnp.concatenate([dqr.astype(BF16), dkr.astype(BF16), dv_ref[...].astype(BF16), dbch_ref[...]], axis=1)
        dp_ref[...] = dp
        da = jnp.dot(dp, w_ref[...], preferred_element_type=F32)
        dx, dg = _rms_bwd(da, h_ref[...], g_ref[...])
        dh_ref[...] = dh1_ref[...] + dx
        dg_ref[...] += dg

    row = lambda n: pl.BlockSpec((tm, n), lambda i: (i, 0))
    full = lambda a: pl.BlockSpec(a.shape, lambda i: (0, 0))
    return pl.pallas_call(
        body, name="in_proj_bwd", grid=(t // tm,),
        in_specs=[row(ATTN_W), row(KV_W), row(KV_W), row(3 * CONV_W), full(w), row(D_MODEL), row(D_MODEL), full(g),
                  row(2 * HEAD_DIM), row(2 * HEAD_DIM), row(2 * HEAD_DIM)],
        out_specs=[row(D_MODEL), row(IN_W), pl.BlockSpec((1, D_MODEL), lambda i: (0, 0))],
        out_shape=[jax.ShapeDtypeStruct((t, D_MODEL), F32), jax.ShapeDtypeStruct((t, IN_W), BF16),
                   jax.ShapeDtypeStruct((1, D_MODEL), F32)],
        compiler_params=_params("arbitrary"),
    )(dq, dk, dv, dbch, w, dh1, h, g, *tabs)


class _Tiles:
    def __init__(self, t):
        self.tm = _row_tile(t, 640)
        self.ts = self.tm
        self.tabs = _rope_tables(t)
        self.bias = _attn_bias()


def _mixer_fwd(h, p, tl):
    a, q, k, v, b, c, hc = _in_proj(h, p["mix_pre_g"], p["w_in"], tl.tabs, tl.ts)
    o = _attn_fwd(q, k, v, tl.bias, p["sinks"], tl.tm)
    return (h, a, q, k, v, b, c, hc, o)


def _out_fwd(mixed, p, tl, deps=()):
    h, a, q, k, v, b, c, hc, o = mixed
    h1, y, z = _mix_out(h, o, b, c, hc, p["conv_w"], p["attn_out_g"], p["conv_out_g"], p["w_out"], p["mix_post_g"],
                        tl.ts, deps)
    return h1, mixed + (h1, y, z)


def _mlp_fwd(h1, saved, p, tl, target=None):
    h2, a2, slope, f = _mlp(h1, p["mlp_pre_g"], p["w_up"], p["w_down"], p["mlp_post_g"], tl.tm, target)
    return h2, saved + (a2, slope, f)


def _mlp_part_bwd(dh, saved, p, tl, deps=()):
    h1, a2, slope, f = saved[9], saved[12], saved[13], saved[14]
    df, dup, dg2 = _mlp_bwd_hidden(dh, f, p["mlp_post_g"], slope, p["w_down"], tl.tm, deps)
    dh1, dg1 = _mlp_bwd_input(dup, p["w_up"], h1, p["mlp_pre_g"], dh, tl.tm)
    g = {"w_down": [d.reshape(N_CHIPS, FF_CHUNK, D_MODEL)
                    for d in _weight_grad(slope, df, "grad_w_down", x_is_slope=True)],
         "w_up": [d.reshape(N_CHIPS, D_MODEL, FF_CHUNK) for d in _weight_grad(a2, dup, "grad_w_up")],
         "mlp_post_g": dg2, "mlp_pre_g": dg1}
    return dh1, g


def _mix_out_part_bwd(dh1, saved, p, tl, deps=()):
    b, c, hc, o, y, z = saved[5], saved[6], saved[7], saved[8], saved[10], saved[11]
    dz, do, dbch, dgp, dga, dgc, dcw = _mix_out_bwd(dh1, z, p["mix_post_g"], p["w_out"], o, b, c, hc, p["conv_w"],
                                                    p["attn_out_g"], p["conv_out_g"], tl.ts, deps)
    g = {"w_out": [d.reshape(N_CHIPS, D_MODEL // N_CHIPS, D_MODEL) for d in _weight_grad(y, dz, "grad_w_out")],
         "mix_post_g": dgp, "attn_out_g": dga, "conv_out_g": dgc, "conv_w": dcw}
    return (dh1, do, dbch), g


def _attn_in_part_bwd(carry, saved, p, tl, deps=()):
    dh1, do, dbch = carry
    h_in, a, q, k, v, o = saved[0], saved[1], saved[2], saved[3], saved[4], saved[8]
    dq, dk, dv, dsink = _attn_bwd(q, k, v, o, do, tl.bias, p["sinks"], tl.tm, deps)
    dh, dproj, dgi = _in_proj_bwd(dq, dk, dv, dbch, p["w_in"], dh1, h_in, p["mix_pre_g"], tl.tabs, tl.ts)
    g_in = [d.reshape(N_CHIPS, IN_W // N_CHIPS, D_MODEL) for d in _weight_grad(dproj, a, "grad_w_in")]
    return dh, {"w_in": g_in, "mix_pre_g": dgi, "sinks": dsink[:, 0]}


def _place():
    return lax.axis_index("x"), lax.axis_index("y"), lax.axis_index("c")


def _other_chips(x, y):
    return [(1 - x, y), (x, 1 - y), (1 - x, 1 - y)]


_HBM = pl.BlockSpec(memory_space=pltpu.HBM)
_SEM = pl.BlockSpec(memory_space=pltpu.SEMAPHORE)
_EFFECT = pltpu.SideEffectType.DATAFLOW_SIDE_EFFECTING


class _Exchange:
    def __init__(self, name, bufs, plan, n, after=()):
        self.name, self.plan, nb = name, plan, len(bufs)
        n_in = nb + len(after)

        def body(*refs):
            send, recv, token = refs[n_in], refs[n_in + 1], refs[-1]
            for k, (src, dst, target, _) in enumerate(plan(refs[:nb])):
                pltpu.make_async_remote_copy(src_ref=src, dst_ref=dst, send_sem=send.at[k], recv_sem=recv.at[k],
                                             device_id=target, device_id_type=MESH).start()
            token[...] = jnp.zeros_like(token)

        outs = pl.pallas_call(
            body, name=name + "_start",
            out_shape=(pltpu.SemaphoreType.DMA((n,)), pltpu.SemaphoreType.DMA((n,)),
                       *[pltpu.HBM(b.shape, b.dtype) for b in bufs], jax.ShapeDtypeStruct((8, 128), F32)),
            in_specs=[_HBM] * nb + [pl.BlockSpec(memory_space=pl.ANY)] * len(after),
            out_specs=(_SEM, _SEM, *[_HBM] * nb, pl.BlockSpec(memory_space=pltpu.VMEM)),
            input_output_aliases={i: 2 + i for i in range(nb)},
            compiler_params=pltpu.CompilerParams(has_side_effects=_EFFECT),
        )(*[pltpu.with_memory_space_constraint(b, pltpu.HBM) for b in bufs], *after)
        self.send, self.recv, self.bufs, self.token = outs[0], outs[1], list(outs[2:2 + nb]), outs[-1]

    def wait(self, *after):
        plan, nb = self.plan, len(self.bufs)

        def body(*refs):
            send, recv = refs[nb], refs[nb + 1]
            for k, (src, _, target, land) in enumerate(plan(refs[:nb])):
                cp = pltpu.make_async_remote_copy(src_ref=src, dst_ref=land, send_sem=send.at[k], recv_sem=recv.at[k],
                                                  device_id=target, device_id_type=MESH)
                cp.wait_send()
                cp.wait_recv()

        outs = pl.pallas_call(
            body, name=self.name + "_wait", out_shape=[pltpu.HBM(b.shape, b.dtype) for b in self.bufs],
            in_specs=[_HBM] * nb + [_SEM, _SEM] + [pl.BlockSpec(memory_space=pl.ANY)] * len(after),
            out_specs=[_HBM] * nb, input_output_aliases={i: i for i in range(nb)},
            compiler_params=pltpu.CompilerParams(has_side_effects=_EFFECT),
        )(*self.bufs, self.send, self.recv, *after)
        return list(outs)


def _gather_plan(n):
    def plan(refs):
        x, y, c = _place()
        me = 2 * x + y
        return [(refs[a].at[me], refs[a].at[me], (px, py, c), refs[a].at[2 * px + py])
                for a in range(n) for px, py in _other_chips(x, y)]

    return plan


def _peers():
    x, y, c = _place()
    return [(k - 1, (x ^ (k >> 2), y ^ ((k >> 1) & 1), c ^ (k & 1))) for k in range(1, N_DEV)]


def _scatter_plan(n, half_rows):
    def plan(refs):
        out = []
        for a in range(n):
            hr = half_rows[a]
            for k, (px, py, pc) in _peers():
                out.append((refs[a].at[2 * px + py, pl.ds(pc * hr, hr)], refs[n + a].at[k], (px, py, pc),
                            refs[n + a].at[k]))
        return out

    return plan


def _join_plan(layer, half_rows):
    def plan(refs):
        x, y, c = _place()
        out = []
        for a, hr in enumerate(half_rows):
            mine = refs[a].at[layer, pl.ds(c * hr, hr)]
            out.append((mine, mine, (x, y, 1 - c), refs[a].at[layer, pl.ds((1 - c) * hr, hr)]))
        return out

    return plan


def _sum_devices(packed):
    def body(p_ref, o_ref, land, send_sems, recv_sems):
        x, y, c = _place()
        me = 4 * x + 2 * y + c
        land[me] = p_ref[...]
        sends = []
        for k in range(1, N_DEV):
            px, py, pc = x ^ (k >> 2), y ^ ((k >> 1) & 1), c ^ (k & 1)
            cp = pltpu.make_async_remote_copy(src_ref=p_ref, dst_ref=land.at[me], send_sem=send_sems.at[k - 1],
                                              recv_sem=recv_sems.at[k - 1], device_id=(px, py, pc), device_id_type=MESH)
            cp.start()
            sends.append(cp)
        for k in range(1, N_DEV):
            px, py, pc = x ^ (k >> 2), y ^ ((k >> 1) & 1), c ^ (k & 1)
            pltpu.make_async_remote_copy(src_ref=p_ref, dst_ref=land.at[4 * px + 2 * py + pc],
                                         send_sem=send_sems.at[k - 1], recv_sem=recv_sems.at[k - 1],
                                         device_id=(px, py, pc), device_id_type=MESH).wait_recv()
        for cp in sends:
            cp.wait_send()
        total = land[0]
        for d in range(1, N_DEV):
            total = total + land[d]
        o_ref[...] = total

    vm = pl.BlockSpec(memory_space=pltpu.VMEM)
    return pl.pallas_call(
        body, name="sum_devices", in_specs=[vm], out_specs=vm,
        out_shape=jax.ShapeDtypeStruct(packed.shape, F32),
        scratch_shapes=[pltpu.VMEM((N_DEV,) + packed.shape, F32), pltpu.SemaphoreType.DMA((N_DEV - 1,)),
                        pltpu.SemaphoreType.DMA((N_DEV - 1,))],
    )(packed)


def _adamw_math(w, g, m, v):
    m = ADAM_B1 * m + (1.0 - ADAM_B1) * g
    v = ADAM_B2 * v + (1.0 - ADAM_B2) * jnp.square(g)
    m_hat = m / (1.0 - ADAM_B1 ** ADAM_STEP)
    v_hat = v / (1.0 - ADAM_B2 ** ADAM_STEP)
    delta = -ADAM_LR * (m_hat / (jnp.sqrt(v_hat) + ADAM_EPS) + ADAM_WD * w)
    return delta, m, v


def _adamw_half(layer, w, m, v, g, q, other):
    _, rows, cols = w.shape
    hr = rows // 2
    tr = _block_rows(hr)
    per = hr // tr
    x, y, c = _place()
    where = jnp.stack([2 * x + y, c * per]).astype(jnp.int32)

    def body(where_ref, w_ref, m_ref, v_ref, g_ref, q_ref, *rest):
        g_out, d_ref, nm_ref, nv_ref = rest[-4:]
        g = g_ref[...]
        for k in range(N_DEV - 1):
            g = g + q_ref[k].astype(F32)
        g_out[...] = g
        d_ref[...], nm_ref[...], nv_ref[...] = _adamw_math(w_ref[...], g, m_ref[...], v_ref[...])

    mine = pl.BlockSpec((None, tr, cols), lambda i, where_ref: (layer, where_ref[1] + i, 0))
    kept = [] if other is None else list(other)
    return pl.pallas_call(
        body, name="adamw_half",
        grid_spec=pltpu.PrefetchScalarGridSpec(
            num_scalar_prefetch=1, grid=(per,),
            in_specs=[mine, mine, mine,
                      pl.BlockSpec((None, tr, cols), lambda i, where_ref: (where_ref[0], where_ref[1] + i, 0)),
                      pl.BlockSpec((N_DEV - 1, tr, cols), lambda i, where_ref: (0, i, 0))]
            + [pl.BlockSpec(memory_space=pl.ANY)] * len(kept),
            out_specs=[mine] * 4),
        out_shape=[jax.ShapeDtypeStruct(w.shape, F32)] * 4,
        input_output_aliases={6 + k: k for k in range(len(kept))},
        compiler_params=_params("parallel"),
    )(where, w, m, v, g, q, *kept)


def _adamw_small(ws, gs, ms, vs):
    n = len(ws)

    def body(*refs):
        w_r, g_r, m_r, v_r = refs[:n], refs[n:2 * n], refs[2 * n:3 * n], refs[3 * n:4 * n]
        d_r, nm_r, nv_r = refs[4 * n:5 * n], refs[5 * n:6 * n], refs[6 * n:]
        for a in range(n):
            d_r[a][...], nm_r[a][...], nv_r[a][...] = _adamw_math(w_r[a][...], g_r[a][...], m_r[a][...], v_r[a][...])

    vm = pl.BlockSpec(memory_space=pltpu.VMEM)
    outs = pl.pallas_call(
        body, name="adamw_small", in_specs=[vm] * (4 * n), out_specs=[vm] * (3 * n),
        out_shape=[jax.ShapeDtypeStruct(w.shape, F32) for w in ws] * 3,
    )(*ws, *gs, *ms, *vs)
    return outs[:n], outs[n:2 * n], outs[2 * n:]


_LARGE = ("w_in", "w_out", "w_up", "w_down")
_SMALL = ("meta_tokens", "mix_pre_g", "conv_w", "sinks", "attn_out_g", "conv_out_g", "mix_post_g", "mlp_pre_g",
          "mlp_post_g")
_ORDER = ("meta_tokens", "mix_pre_g", "w_in", "conv_w", "sinks", "attn_out_g", "conv_out_g", "w_out", "mix_post_g",
          "mlp_pre_g", "w_up", "w_down", "mlp_post_g")


class _Reduce:
    def __init__(self, name, layer, names, grads, after=()):
        self.name, self.layer, self.names = name, layer, names
        self.own = [grads[n][0] for n in names]
        self.half_rows = [g.shape[1] // 2 for g in self.own]
        zones = [lax.empty((N_DEV - 1, hr, g.shape[2]), BF16) for g, hr in zip(self.own, self.half_rows)]
        self.exchange = _Exchange(name + "_scatter", [grads[n][1] for n in names] + zones,
                                  _scatter_plan(len(names), self.half_rows), (N_DEV - 1) * len(names), after)

    @property
    def token(self):
        return self.exchange.token

    def update(self, w, m, v, other, *after):
        parts = self.exchange.wait(*after)[len(self.names):]
        results = []
        for n, g, q in zip(self.names, self.own, parts):
            results += _adamw_half(self.layer, w[n], m[n], v[n], g, q, other.get(n))
        self.exchange = _Exchange(self.name + "_join", results,
                                  _join_plan(self.layer, [hr for hr in self.half_rows for _ in range(4)]), len(results))

    def done(self, *after):
        bufs = self.exchange.wait(*after)
        return {n: bufs[4 * a:4 * a + 4] for a, n in enumerate(self.names)}


def _pad_cols(a, n=D_MODEL):
    return jnp.pad(a, ((0, 0), (0, n - a.shape[1])))


def kernel(x, meta_tokens, mix_pre_g, w_in, conv_w, sinks, attn_out_g, conv_out_g, w_out, mix_post_g, mlp_pre_g, w_up, w_down, mlp_post_g, loss_target, m_meta_tokens, m_mix_pre_g, m_w_in, m_conv_w, m_sinks, m_attn_out_g, m_conv_out_g, m_w_out, m_mix_post_g, m_mlp_pre_g, m_w_up, m_w_down, m_mlp_post_g, v_meta_tokens, v_mix_pre_g, v_w_in, v_conv_w, v_sinks, v_attn_out_g, v_conv_out_g, v_w_out, v_mix_post_g, v_mlp_pre_g, v_w_up, v_w_down, v_mlp_post_g):
    w = dict(meta_tokens=meta_tokens, mix_pre_g=mix_pre_g, w_in=w_in, conv_w=conv_w, sinks=sinks,
             attn_out_g=attn_out_g, conv_out_g=conv_out_g, w_out=w_out, mix_post_g=mix_post_g, mlp_pre_g=mlp_pre_g,
             w_up=w_up, w_down=w_down, mlp_post_g=mlp_post_g)
    m = dict(meta_tokens=m_meta_tokens, mix_pre_g=m_mix_pre_g, w_in=m_w_in, conv_w=m_conv_w, sinks=m_sinks,
             attn_out_g=m_attn_out_g, conv_out_g=m_conv_out_g, w_out=m_w_out, mix_post_g=m_mix_post_g,
             mlp_pre_g=m_mlp_pre_g, w_up=m_w_up, w_down=m_w_down, mlp_post_g=m_mlp_post_g)
    v = dict(meta_tokens=v_meta_tokens, mix_pre_g=v_mix_pre_g, w_in=v_w_in, conv_w=v_conv_w, sinks=v_sinks,
             attn_out_g=v_attn_out_g, conv_out_g=v_conv_out_g, w_out=v_w_out, mix_post_g=v_mix_post_g,
             mlp_pre_g=v_mlp_pre_g, w_up=v_w_up, w_down=v_w_down, mlp_post_g=v_mlp_post_g)
    chip = 2 * lax.axis_index("x") + lax.axis_index("y")
    tl = _Tiles(x.shape[1] + BLOCK)

    def zone(quarter):
        return lax.dynamic_update_slice(lax.empty((N_CHIPS,) + quarter.shape, quarter.dtype), quarter[None],
                                        (chip,) + (0,) * quarter.ndim)

    w, m, v = ({**d, "w_in": jnp.swapaxes(d["w_in"], 1, 2)} for d in (w, m, v))
    zones = {n: [zone(w[n][l].astype(BF16)) for l in range(DEPTH)] for n in _LARGE}
    first = _Exchange("gather_first", [zones["w_in"][0], zone(w["conv_w"]), zone(w["meta_tokens"])], _gather_plan(3), 9)
    out0 = _Exchange("gather_out", [zones["w_out"][0]], _gather_plan(1), 3, [first.token])
    rest = _Exchange("gather_rest", [zones[n][0] for n in ("w_up", "w_down")], _gather_plan(2), 6, [out0.token])

    def whole_in(quarters):
        return quarters.reshape(IN_W, D_MODEL)

    h = jnp.concatenate([jnp.zeros((BLOCK, D_MODEL), F32), x[0]], axis=0)
    q_in, q_conv, q_meta = first.wait(rest.token, *tl.tabs, tl.bias, h)
    conv_whole = jnp.transpose(q_conv, (1, 2, 0, 3)).reshape(DEPTH, CONV_K, CONV_W)
    meta = jnp.transpose(q_meta, (1, 0, 2)).reshape(N_META, D_MODEL)
    p = [{"conv_w": conv_whole[l], "sinks": w["sinks"][l]} for l in range(DEPTH)]
    for l in range(DEPTH):
        for n in ("mix_pre_g", "attn_out_g", "conv_out_g", "mix_post_g", "mlp_pre_g", "mlp_post_g"):
            p[l][n] = w[n][l][None, :]

    h = lax.dynamic_update_slice(h, meta, (LEAD_PAD, 0))
    p[0]["w_in"] = whole_in(q_in)
    mixed = _mixer_fwd(h, p[0], tl)
    second = _Exchange("gather_second", [zones["w_in"][1], zones["w_out"][1]], _gather_plan(2), 6, [mixed[-1]])
    second_mlp = _Exchange("gather_second_mlp", [zones["w_up"][1], zones["w_down"][1]], _gather_plan(2), 6,
                           [second.token])
    p[0]["w_out"], = out0.wait(second_mlp.token)
    h1, saved0 = _out_fwd(mixed, p[0], tl)
    p[0]["w_up"], p[0]["w_down"] = rest.wait(h1)
    h, saved0 = _mlp_fwd(h1, saved0, p[0], tl)
    q_in, p[1]["w_out"] = second.wait(h)
    p[1]["w_in"] = whole_in(q_in)
    h1, saved1 = _out_fwd(_mixer_fwd(h, p[1], tl), p[1], tl)
    p[1]["w_up"], p[1]["w_down"] = second_mlp.wait(h1)
    (loss_tile, dh), saved1 = _mlp_fwd(h1, saved1, p[1], tl, loss_target[0])

    dh1, g1 = _mlp_part_bwd(dh, saved1, p[1], tl)
    carry, gm = _mix_out_part_bwd(dh1, saved1, p[1], tl)
    dh, gi = _attn_in_part_bwd(carry, saved1, p[1], tl)
    g1.update(gm, **gi)
    red1 = _Reduce("reduce1", 1, _LARGE, g1)
    dh1, g0 = _mlp_part_bwd(dh, saved0, p[0], tl, [red1.token])
    red1.update(w, m, v, {}, g0["w_down"][0])
    carry, gm = _mix_out_part_bwd(dh1, saved0, p[0], tl, [red1.token])
    g0.update(gm)
    red0a = _Reduce("reduce0a", 0, ("w_up", "w_down", "w_out"), g0)
    dh0, gi = _attn_in_part_bwd(carry, saved0, p[0], tl, [red0a.token])
    g0.update(gi)
    red0b = _Reduce("reduce0b", 0, ("w_in",), g0)
    grad_x = dh0[BLOCK:][None]
    grads = {n: [g0[n], g1[n]] for n in g0 if n not in _LARGE}

    rows = [dh0[LEAD_PAD:BLOCK]]
    for n in ("mix_pre_g", "mix_post_g", "mlp_pre_g", "mlp_post_g"):
        rows += grads[n]
    rows += [jnp.concatenate([grads["attn_out_g"][l], grads["conv_out_g"][l]], axis=1) for l in range(DEPTH)]
    rows.append(jnp.concatenate(grads["conv_w"], axis=1))
    rows.append(_pad_cols(jnp.concatenate(grads["sinks"])[None, :]))
    rows.append(_pad_cols(loss_tile[:1]))
    packed = jnp.concatenate(rows, axis=0)
    packed = jnp.pad(packed, ((0, SMALL_ROWS - packed.shape[0]), (0, 0)))
    total = _sum_devices(packed)
    r0 = N_META
    small = {
        "meta_tokens": lax.dynamic_slice(total[:N_META], (0, chip * (D_MODEL // N_CHIPS)), (N_META, D_MODEL // N_CHIPS)),
        "mix_pre_g": total[r0:r0 + 2], "mix_post_g": total[r0 + 2:r0 + 4], "mlp_pre_g": total[r0 + 4:r0 + 6],
        "mlp_post_g": total[r0 + 6:r0 + 8],
        "attn_out_g": total[r0 + 8:r0 + 10, :ATTN_W], "conv_out_g": total[r0 + 8:r0 + 10, ATTN_W:],
        "conv_w": lax.dynamic_slice(total[r0 + 10:r0 + 13].reshape(CONV_K, DEPTH, CONV_W).transpose(1, 0, 2),
                                    (0, 0, chip * (CONV_W // N_CHIPS)), (DEPTH, CONV_K, CONV_W // N_CHIPS)),
        "sinks": total[r0 + 13, :DEPTH * N_Q_HEADS].reshape(DEPTH, N_Q_HEADS),
    }
    loss = total[r0 + 14, 0]

    ds, nms, nvs = _adamw_small([w[n] for n in _SMALL], [small[n] for n in _SMALL], [m[n] for n in _SMALL],
                                [v[n] for n in _SMALL])
    done1 = red1.done(red0b.token)
    red0a.update(w, m, v, done1, ds[0], grad_x)
    red0b.update(w, m, v, done1, red0a.token)
    done0 = {**red0a.done(red0b.token), **red0b.done(red0b.token)}
    grad, delta, new_m, new_v = {}, {}, {}, {}
    for n in _LARGE:
        grad[n], delta[n], new_m[n], new_v[n] = done0[n]
    for d in (grad, delta, new_m, new_v):
        d["w_in"] = jnp.swapaxes(d["w_in"], 1, 2)
    for i, n in enumerate(_SMALL):
        grad[n], delta[n], new_m[n], new_v[n] = small[n], ds[i], nms[i], nvs[i]
    return (loss, grad_x, *[grad[n] for n in _ORDER], *[delta[n] for n in _ORDER], *[new_m[n] for n in _ORDER],
            *[new_v[n] for n in _ORDER])
```

```python
import functools

import jax
import jax.numpy as jnp
from jax import lax
from jax.experimental import pallas as pl
from jax.experimental.pallas import tpu as pltpu

F32 = jnp.float32
BF16 = jnp.bfloat16

D_MODEL = 1024
DEPTH = 2
N_META = 16
ATTN_W = 512
CONV_W = 512
HEAD_DIM = 64
N_Q_HEADS = 8
N_KV_HEADS = 2
GROUP = N_Q_HEADS // N_KV_HEADS
KV_W = N_KV_HEADS * HEAD_DIM
CONV_K = 3
BLOCK = 128
LEAD_PAD = BLOCK - N_META
ROPE_THETA = 500000.0
ROT_DIM = HEAD_DIM // 4
ROT_HALF = ROT_DIM // 2
D_FF = 4 * D_MODEL
IN_W = ATTN_W + 2 * KV_W + 3 * CONV_W
QKV_W = ATTN_W + 2 * KV_W
EPS = 1e-6
SCALE = HEAD_DIM ** -0.5
FF_CHUNK = 1024
N_CHIPS = 4
N_DEV = 8

ADAM_LR = 0.001
ADAM_B1 = 0.9
ADAM_B2 = 0.999
ADAM_EPS = 1e-08
ADAM_WD = 0.01
ADAM_STEP = 10

V7X_VMEM_LIMIT = 60 * 1024 * 1024
SMALL_ROWS = 32

MESH = pl.DeviceIdType.MESH


def _params(*sem):
    return pltpu.CompilerParams(dimension_semantics=sem, vmem_limit_bytes=V7X_VMEM_LIMIT)


def _block_rows(n):
    return max(r for r in range(16, min(n, 256) + 1, 16) if n % r == 0)


def _row_tile(t, most):
    nb = t // BLOCK
    for b in range(most // BLOCK, 0, -1):
        if nb % b == 0:
            return b * BLOCK
    return BLOCK


def _behind(body, deps):
    n = len(deps)

    def wrapped(*refs):
        body(*refs[n:])

    return wrapped, [pl.BlockSpec(memory_space=pl.ANY)] * n


def _rms(x, g):
    r = lax.rsqrt(jnp.mean(x * x, axis=-1, keepdims=True) + EPS)
    return x * r * g


def _rms_bwd(dy, x, g):
    r = lax.rsqrt(jnp.mean(x * x, axis=-1, keepdims=True) + EPS)
    xh = x * r
    dg = jnp.sum(dy * xh, axis=0, keepdims=True)
    dxh = dy * g
    dx = r * (dxh - xh * jnp.mean(dxh * xh, axis=-1, keepdims=True))
    return dx, dg


def _rope(x, cos, sa, sb):
    n = x.shape[-1]
    return x * cos + pltpu.roll(x, n - ROT_HALF, 1) * sa + pltpu.roll(x, ROT_HALF, 1) * sb


def _rope_bwd(dy, cos, sa, sb):
    n = dy.shape[-1]
    return dy * cos + pltpu.roll(dy * sa, ROT_HALF, 1) + pltpu.roll(dy * sb, n - ROT_HALF, 1)


def _rope_tables(t):
    pos = lax.broadcasted_iota(jnp.int32, (t, ROT_HALF), 0).astype(F32) - LEAD_PAD
    pair = lax.broadcasted_iota(jnp.int32, (t, ROT_HALF), 1).astype(F32)
    inv_freq = jnp.power(jnp.float32(ROPE_THETA), -(2.0 * pair) / ROT_DIM)
    ang = pos * inv_freq
    cos, sin = lax.optimization_barrier((jnp.cos(ang), jnp.sin(ang)))
    spread = (1, 2 * HEAD_DIM // ROT_HALF)
    cos, sin = jnp.tile(cos, spread), jnp.tile(sin, spread)
    dim = lax.broadcasted_iota(jnp.int32, (t, 2 * HEAD_DIM), 1) % HEAD_DIM
    return (jnp.where(dim < ROT_DIM, cos, 1.0), jnp.where(dim < ROT_HALF, -sin, 0.0),
            jnp.where((dim >= ROT_HALF) & (dim < ROT_DIM), sin, 0.0))


def _in_proj(h, g, w, tabs, tm):
    t = h.shape[0]

    def body(h_ref, g_ref, w_ref, c_ref, sa_ref, sb_ref, a_ref, q_ref, k_ref, v_ref, b_ref, cg_ref, hc_ref):
        a = _rms(h_ref[...], g_ref[...]).astype(BF16)
        a_ref[...] = a
        p = lax.dot_general(a, w_ref[...], (((1,), (1,)), ((), ())), preferred_element_type=F32)
        cos, sa, sb = c_ref[...], sa_ref[...], sb_ref[...]
        rep = ATTN_W // (2 * HEAD_DIM)
        q = _rope(p[:, :ATTN_W], jnp.tile(cos, (1, rep)), jnp.tile(sa, (1, rep)), jnp.tile(sb, (1, rep)))
        q_ref[...] = (q * SCALE).astype(BF16)
        k_ref[...] = _rope(p[:, ATTN_W:ATTN_W + KV_W], cos, sa, sb).astype(BF16)
        v_ref[...] = p[:, ATTN_W + KV_W:QKV_W].astype(BF16)
        b_ref[...] = p[:, QKV_W:QKV_W + CONV_W].astype(BF16)
        cg_ref[...] = p[:, QKV_W + CONV_W:QKV_W + 2 * CONV_W].astype(BF16)
        hc_ref[...] = p[:, QKV_W + 2 * CONV_W:].astype(BF16)

    row = lambda n: pl.BlockSpec((tm, n), lambda i: (i, 0))
    full = lambda a: pl.BlockSpec(a.shape, lambda i: (0, 0))
    return pl.pallas_call(
        body, name="in_proj", grid=(t // tm,),
        in_specs=[row(D_MODEL), full(g), full(w), row(2 * HEAD_DIM), row(2 * HEAD_DIM), row(2 * HEAD_DIM)],
        out_specs=[row(D_MODEL), row(ATTN_W), row(KV_W), row(KV_W), row(CONV_W), row(CONV_W), row(CONV_W)],
        out_shape=[jax.ShapeDtypeStruct((t, D_MODEL), BF16), jax.ShapeDtypeStruct((t, ATTN_W), BF16),
                   jax.ShapeDtypeStruct((t, KV_W), BF16), jax.ShapeDtypeStruct((t, KV_W), BF16),
                   jax.ShapeDtypeStruct((t, CONV_W), BF16), jax.ShapeDtypeStruct((t, CONV_W), BF16),
                   jax.ShapeDtypeStruct((t, CONV_W), BF16)],
        compiler_params=_params("parallel"),
    )(h, g, w, *tabs)


def _attn_bias():
    r = lax.broadcasted_iota(jnp.int32, (3, BLOCK, 2 * BLOCK), 1)
    c = lax.broadcasted_iota(jnp.int32, (3, BLOCK, 2 * BLOCK), 2)
    i = lax.broadcasted_iota(jnp.int32, (3, BLOCK, 2 * BLOCK), 0)
    ok = (c > r) & (c <= r + BLOCK) & (c + (i - 1) * BLOCK >= LEAD_PAD)
    return jnp.where(ok, 0.0, -jnp.inf).astype(F32)


def _attn_scores(qh, kg, bias):
    return lax.dot_general(qh, kg, (((1,), (1,)), ((), ())), preferred_element_type=F32) + bias


def _attn_probs(s, sk):
    m = jnp.maximum(jnp.max(s, axis=-1, keepdims=True), sk)
    e = jnp.exp(s - m)
    es = jnp.exp(sk - m)
    rden = 1.0 / (jnp.sum(e, axis=-1, keepdims=True) + es)
    return e * rden, es * rden


def _head(hh):
    return slice(hh * HEAD_DIM, (hh + 1) * HEAD_DIM)


def _two_blocks(ref, i):
    prev = jnp.maximum(i - 1, 0)
    return jnp.concatenate([ref[pl.ds(pl.multiple_of(prev * BLOCK, BLOCK), BLOCK), :],
                            ref[pl.ds(pl.multiple_of(i * BLOCK, BLOCK), BLOCK), :]], axis=0)


def _attn_fwd(q, k, v, bias, sinks, tm):
    t = q.shape[0]
    per_step = tm // BLOCK
    heads = range(N_Q_HEADS)

    def body(s_ref, q_ref, k_ref, v_ref, bias_ref, o_ref):
        for b in range(per_step):
            i = pl.program_id(0) * per_step + b
            rows = slice(b * BLOCK, (b + 1) * BLOCK)
            kc, vc = _two_blocks(k_ref, i), _two_blocks(v_ref, i)
            bias_i = bias_ref[jnp.minimum(i, 2)]
            scores = [_attn_scores(q_ref[rows, _head(hh)], kc[:, _head(hh // GROUP)], bias_i) for hh in heads]
            probs = [_attn_probs(scores[hh], s_ref[hh])[0].astype(BF16) for hh in heads]
            for hh in heads:
                o_ref[rows, _head(hh)] = jnp.dot(probs[hh], vc[:, _head(hh // GROUP)],
                                                 preferred_element_type=F32).astype(BF16)

    whole = pl.BlockSpec((t, KV_W), lambda i: (0, 0))
    return pl.pallas_call(
        body, name="attn_fwd", grid=(t // tm,),
        in_specs=[pl.BlockSpec(memory_space=pltpu.SMEM), pl.BlockSpec((tm, ATTN_W), lambda i: (i, 0)), whole, whole,
                  pl.BlockSpec(bias.shape, lambda i: (0, 0, 0))],
        out_specs=pl.BlockSpec((tm, ATTN_W), lambda i: (i, 0)),
        out_shape=jax.ShapeDtypeStruct((t, ATTN_W), BF16),
        compiler_params=_params("parallel"),
    )(sinks, q, k, v, bias)


def _shift_rows(u, halo, n):
    r = pltpu.roll(u, n, 0)
    hr = pltpu.roll(halo, n, 0)
    idx = lax.broadcasted_iota(jnp.int32, hr.shape, 0)
    return jnp.concatenate([jnp.where(idx < n, hr, r[:8]), r[8:]], axis=0)


def _advance_rows(u, halo, n):
    rows = u.shape[0]
    r = pltpu.roll(u, rows - n, 0)
    hr = pltpu.roll(halo, 8 - n, 0)
    idx = lax.broadcasted_iota(jnp.int32, hr.shape, 0)
    return jnp.concatenate([r[:rows - 8], jnp.where(idx >= 8 - n, hr, r[rows - 8:])], axis=0)


def _mix_out(h, o, b, c, hc, cw, ga, gc, w, gp, tm, deps=()):
    t = h.shape[0]

    def body(h_ref, o_ref, b_ref, c_ref, hc_ref, cw_ref, ga_ref, gc_ref, w_ref, gp_ref, h1_ref, y_ref, z_ref, halo):
        @pl.when(pl.program_id(0) == 0)
        def _():
            halo[...] = jnp.zeros_like(halo)

        u = c_ref[...].astype(F32) * hc_ref[...].astype(F32)
        cv = cw_ref[0:1, :] * _shift_rows(u, halo[...], 2) + cw_ref[1:2, :] * _shift_rows(u, halo[...], 1) \
            + cw_ref[2:3, :] * u
        halo[...] = u[tm - 8:]
        yc = b_ref[...].astype(F32) * cv
        y = jnp.concatenate([_rms(o_ref[...].astype(F32), ga_ref[...]), _rms(yc, gc_ref[...])], axis=1).astype(BF16)
        y_ref[...] = y
        z = jnp.dot(y, w_ref[...].reshape(D_MODEL, D_MODEL), preferred_element_type=F32)
        z_ref[...] = z
        h1_ref[...] = h_ref[...] + _rms(z, gp_ref[...])

    row = lambda n: pl.BlockSpec((tm, n), lambda i: (i, 0))
    full = lambda a: pl.BlockSpec(a.shape, lambda i: (0,) * a.ndim)
    body, dep_specs = _behind(body, deps)
    return pl.pallas_call(
        body, name="mix_out", grid=(t // tm,),
        in_specs=dep_specs + [row(D_MODEL), row(ATTN_W), row(CONV_W), row(CONV_W), row(CONV_W), full(cw), full(ga),
                              full(gc), full(w), full(gp)],
        out_specs=[row(D_MODEL), row(D_MODEL), row(D_MODEL)],
        out_shape=[jax.ShapeDtypeStruct((t, D_MODEL), F32), jax.ShapeDtypeStruct((t, D_MODEL), BF16),
                   jax.ShapeDtypeStruct((t, D_MODEL), F32)],
        scratch_shapes=[pltpu.VMEM((8, CONV_W), F32)],
        compiler_params=_params("arbitrary"),
    )(*deps, h, o, b, c, hc, cw, ga, gc, w, gp)


def _mlp(h1, g1, wu, wd, g2, tm, target=None):
    t = h1.shape[0]
    nj = D_FF // FF_CHUNK
    per_step = tm // BLOCK if target is not None else 0

    def body(h1_ref, g1_ref, wu_ref, wd_ref, g2_ref, *rest):
        t_refs, outs = rest[:per_step], rest[per_step:]
        a2_ref, slope_ref, f_ref = outs[-3:]
        a2 = _rms(h1_ref[...], g1_ref[...]).astype(BF16)
        a2_ref[...] = a2
        f = None
        for j in range(nj):
            up = jnp.dot(a2, wu_ref[j], preferred_element_type=F32)
            r = jnp.maximum(up, 0.0)
            slope_ref[:, j * FF_CHUNK:(j + 1) * FF_CHUNK] = (r + r).astype(BF16)
            part = jnp.dot((r * r).astype(BF16), wd_ref[j], preferred_element_type=F32)
            f = part if f is None else f + part
        f_ref[...] = f
        h2 = h1_ref[...] + _rms(f, g2_ref[...])
        if target is None:
            outs[0][...] = h2
            return
        loss_ref, dh_ref = outs[:2]
        i = pl.program_id(0)

        @pl.when(i == 0)
        def _():
            loss_ref[...] = jnp.zeros_like(loss_ref)

        total = jnp.zeros((), F32)
        for b in range(per_step):
            rows = slice(b * BLOCK, (b + 1) * BLOCK)
            err = h2[rows] - t_refs[b][...]
            if b == 0:
                err = jnp.where(i == 0, 0.0, err)
            dh_ref[rows, :] = err * (1.0 / D_MODEL)
            total = total + jnp.sum(err * err)
        loss_ref[...] += total * (0.5 / D_MODEL)

    def target_block(b):
        return pl.BlockSpec((BLOCK, D_MODEL), lambda i: (jnp.maximum(i * per_step + b - 1, 0), 0))

    row = pl.BlockSpec((tm, D_MODEL), lambda i: (i, 0))
    vec = pl.BlockSpec((1, D_MODEL), lambda i: (0, 0))
    resident = pl.BlockSpec(memory_space=pltpu.VMEM)
    first_specs, first_shapes = [row], [jax.ShapeDtypeStruct((t, D_MODEL), F32)]
    if target is not None:
        first_specs = [pl.BlockSpec((8, 128), lambda i: (0, 0)), row]
        first_shapes = [jax.ShapeDtypeStruct((8, 128), F32), jax.ShapeDtypeStruct((t, D_MODEL), F32)]
    outs = pl.pallas_call(
        body, name="mlp", grid=(t // tm,),
        in_specs=[row, vec, resident, resident, vec] + [target_block(b) for b in range(per_step)],
        out_specs=first_specs + [row, pl.BlockSpec((tm, D_FF), lambda i: (i, 0)), row],
        out_shape=first_shapes + [jax.ShapeDtypeStruct((t, D_MODEL), BF16), jax.ShapeDtypeStruct((t, D_FF), BF16),
                                  jax.ShapeDtypeStruct((t, D_MODEL), F32)],
        compiler_params=_params("parallel" if target is None else "arbitrary"),
    )(h1, g1, wu, wd, g2, *([target] * per_step))
    return (outs[0] if target is None else tuple(outs[:2]),) + tuple(outs[-3:])


def _mlp_bwd_hidden(dh2, f, g2, slope, wd, tm, deps=()):
    t = dh2.shape[0]
    nj = D_FF // FF_CHUNK

    def body(dh2_ref, f_ref, g2_ref, slope_ref, wd_ref, df_ref, dup_ref, dg2_ref):
        @pl.when(pl.program_id(0) == 0)
        def _():
            dg2_ref[...] = jnp.zeros_like(dg2_ref)

        df, dg = _rms_bwd(dh2_ref[...], f_ref[...], g2_ref[...])
        dg2_ref[...] += dg
        df = df.astype(BF16)
        df_ref[...] = df
        for j in range(nj):
            cols = slice(j * FF_CHUNK, (j + 1) * FF_CHUNK)
            dact = lax.dot_general(df, wd_ref[j], (((1,), (1,)), ((), ())), preferred_element_type=F32)
            dup_ref[:, cols] = (dact * slope_ref[:, cols].astype(F32)).astype(BF16)

    row = pl.BlockSpec((tm, D_MODEL), lambda i: (i, 0))
    wide = pl.BlockSpec((tm, D_FF), lambda i: (i, 0))
    vec = pl.BlockSpec((1, D_MODEL), lambda i: (0, 0))
    body, dep_specs = _behind(body, deps)
    return pl.pallas_call(
        body, name="mlp_bwd_hidden", grid=(t // tm,),
        in_specs=dep_specs + [row, row, vec, wide, pl.BlockSpec(memory_space=pltpu.VMEM)],
        out_specs=[row, wide, vec],
        out_shape=[jax.ShapeDtypeStruct((t, D_MODEL), BF16), jax.ShapeDtypeStruct((t, D_FF), BF16),
                   jax.ShapeDtypeStruct((1, D_MODEL), F32)],
        compiler_params=_params("arbitrary"),
    )(*deps, dh2, f, g2, slope, wd)


def _mlp_bwd_input(dup, wu, h1, g1, dh2, tm):
    t = dh2.shape[0]
    nj = D_FF // FF_CHUNK

    def body(dup_ref, wu_ref, h1_ref, g1_ref, dh2_ref, dh1_ref, dg1_ref):
        @pl.when(pl.program_id(0) == 0)
        def _():
            dg1_ref[...] = jnp.zeros_like(dg1_ref)

        da2 = None
        for j in range(nj):
            part = lax.dot_general(dup_ref[:, j * FF_CHUNK:(j + 1) * FF_CHUNK], wu_ref[j], (((1,), (1,)), ((), ())),
                                   preferred_element_type=F32)
            da2 = part if da2 is None else da2 + part
        dx, dg = _rms_bwd(da2, h1_ref[...], g1_ref[...])
        dh1_ref[...] = dh2_ref[...] + dx
        dg1_ref[...] += dg

    row = pl.BlockSpec((tm, D_MODEL), lambda i: (i, 0))
    vec = pl.BlockSpec((1, D_MODEL), lambda i: (0, 0))
    return pl.pallas_call(
        body, name="mlp_bwd_input", grid=(t // tm,),
        in_specs=[pl.BlockSpec((tm, D_FF), lambda i: (i, 0)), pl.BlockSpec(memory_space=pltpu.VMEM), row, vec, row],
        out_specs=[row, vec],
        out_shape=[jax.ShapeDtypeStruct((t, D_MODEL), F32), jax.ShapeDtypeStruct((1, D_MODEL), F32)],
        compiler_params=_params("arbitrary"),
    )(dup, wu, h1, g1, dh2)


def _row_split(t):
    tile = min(t, 1024)
    return tile, t // tile, t % tile


def _row_split_specs(t, cols):
    tile, whole, rest = _row_split(t)
    specs = [pl.BlockSpec((tile, cols), lambda r: (jnp.minimum(r, whole - 1), 0))]
    if rest:
        specs.append(pl.BlockSpec((rest, cols), lambda r: (whole * tile // rest, 0)))
    return specs


def _weight_grad(x, y, name, x_is_slope=False):
    t, k = x.shape
    n = y.shape[1]
    tn = FF_CHUNK
    tk = FF_CHUNK if k % FF_CHUNK == 0 else k
    _, whole, rest = _row_split(t)
    steps = whole + bool(rest)

    def body(*refs):
        o_ref, ob_ref, r = refs[-2], refs[-1], pl.program_id(0)

        @pl.when(r == 0)
        def _():
            o_ref[...] = jnp.zeros_like(o_ref)

        def add(x_ref, y_ref):
            for a in range(k // tk):
                xv = x_ref[:, a * tk:(a + 1) * tk]
                if x_is_slope:
                    xv = xv.astype(F32)
                    xv = (xv * xv * 0.25).astype(BF16)
                for b in range(n // tn):
                    o_ref[a, b] += lax.dot_general(xv, y_ref[:, b * tn:(b + 1) * tn], (((0,), (0,)), ((), ())),
                                                   preferred_element_type=F32)

        if rest:
            pl.when(r < whole)(lambda: add(refs[0], refs[2]))
            pl.when(r == whole)(lambda: add(refs[1], refs[3]))
        else:
            add(refs[0], refs[1])

        @pl.when(r == steps - 1)
        def _():
            ob_ref[...] = o_ref[...].astype(BF16)

    vm = pl.BlockSpec(memory_space=pltpu.VMEM)
    return pl.pallas_call(
        body, name=name, grid=(steps,),
        in_specs=_row_split_specs(t, k) + _row_split_specs(t, n), out_specs=[vm, vm],
        out_shape=[jax.ShapeDtypeStruct((k // tk, n // tn, tk, tn), F32),
                   jax.ShapeDtypeStruct((k // tk, n // tn, tk, tn), BF16)],
        compiler_params=_params("arbitrary"),
    )(*([x] * (1 + bool(rest))), *([y] * (1 + bool(rest))))


def _mix_out_bwd(dh1, z, gp, w, y, o, b, c, hc, cw, ga, gc, tm, deps=()):
    t = dh1.shape[0]
    nt = t // tm
    per16 = tm // 16

    def body(dh1_ref, z_ref, gp_ref, w_ref, y_ref, o_ref, b_ref, c_ref, hc_ref, cp_ref, hp_ref, cw_ref, ga_ref,
             gc_ref, do_ref, dbch_ref, dgp_ref, dga_ref, dgc_ref, dcw_ref, dw_ref, dwb_ref, halo):
        i = pl.program_id(0)

        @pl.when(i == 0)
        def _():
            halo[...] = jnp.zeros_like(halo)
            dgp_ref[...] = jnp.zeros_like(dgp_ref)
            dga_ref[...] = jnp.zeros_like(dga_ref)
            dgc_ref[...] = jnp.zeros_like(dgc_ref)
            dcw_ref[...] = jnp.zeros_like(dcw_ref)
            dw_ref[...] = jnp.zeros_like(dw_ref)

        dz, dgp = _rms_bwd(dh1_ref[...], z_ref[...], gp_ref[...])
        dgp_ref[...] += dgp
        dz = dz.astype(BF16)
        dw_ref[...] += lax.dot_general(y_ref[...], dz, (((0,), (0,)), ((), ())), preferred_element_type=F32)

        @pl.when(i == nt - 1)
        def _():
            dwb_ref[...] = dw_ref[...].astype(BF16)

        dy = lax.dot_general(dz, w_ref[...].reshape(D_MODEL, D_MODEL), (((1,), (1,)), ((), ())),
                             preferred_element_type=F32)
        do, dga = _rms_bwd(dy[:, :ATTN_W], o_ref[...].astype(F32), ga_ref[...])
        do_ref[...] = do.astype(BF16)
        dga_ref[...] += dga

        cc, hh = c_ref[...].astype(F32), hc_ref[...].astype(F32)
        u = cc * hh
        first = i == nt - 1
        u_before = jnp.where(first, 0.0, (cp_ref[...].astype(F32) * hp_ref[...].astype(F32))[8:])
        u1 = _shift_rows(u, u_before, 1)
        u2 = _shift_rows(u, u_before, 2)
        cv = cw_ref[0:1, :] * u2 + cw_ref[1:2, :] * u1 + cw_ref[2:3, :] * u
        bb = b_ref[...].astype(F32)
        dyc, dgc = _rms_bwd(dy[:, ATTN_W:], bb * cv, gc_ref[...])
        dgc_ref[...] += dgc
        dcv = dyc * bb
        d1 = _advance_rows(dcv, halo[...], 1)
        d2 = _advance_rows(dcv, halo[...], 2)
        halo[...] = dcv[:8]
        du = cw_ref[2:3, :] * dcv + cw_ref[1:2, :] * d1 + cw_ref[0:1, :] * d2
        dbch_ref[...] = jnp.concatenate([dyc * cv, du * hh, du * cc], axis=1).astype(BF16)
        dcw_ref[...] += jnp.concatenate([jnp.sum(dcv * u2, axis=0, keepdims=True),
                                         jnp.sum(dcv * u1, axis=0, keepdims=True),
                                         jnp.sum(dcv * u, axis=0, keepdims=True)], axis=0)

    row = lambda n: pl.BlockSpec((tm, n), lambda i: (nt - 1 - i, 0))
    before = pl.BlockSpec((16, CONV_W), lambda i: (jnp.maximum((nt - 1 - i) * per16 - 1, 0), 0))
    full = lambda a: pl.BlockSpec(a.shape, lambda i: (0,) * a.ndim)
    vec = lambda n: pl.BlockSpec((1, n), lambda i: (0, 0))
    body, dep_specs = _behind(body, deps)
    resident = pl.BlockSpec(memory_space=pltpu.VMEM)
    return pl.pallas_call(
        body, name="mix_out_bwd", grid=(nt,),
        in_specs=dep_specs + [row(D_MODEL), row(D_MODEL), full(gp), full(w), row(D_MODEL), row(ATTN_W), row(CONV_W),
                              row(CONV_W), row(CONV_W), before, before, full(cw), full(ga), full(gc)],
        out_specs=[row(ATTN_W), row(3 * CONV_W), vec(D_MODEL), vec(ATTN_W), vec(CONV_W),
                   pl.BlockSpec((CONV_K, CONV_W), lambda i: (0, 0)), resident, resident],
        out_shape=[jax.ShapeDtypeStruct((t, ATTN_W), BF16), jax.ShapeDtypeStruct((t, 3 * CONV_W), BF16),
                   jax.ShapeDtypeStruct((1, D_MODEL), F32), jax.ShapeDtypeStruct((1, ATTN_W), F32),
                   jax.ShapeDtypeStruct((1, CONV_W), F32), jax.ShapeDtypeStruct((CONV_K, CONV_W), F32),
                   jax.ShapeDtypeStruct((D_MODEL, D_MODEL), F32), jax.ShapeDtypeStruct((D_MODEL, D_MODEL), BF16)],
        scratch_shapes=[pltpu.VMEM((8, CONV_W), F32)],
        compiler_params=_params("arbitrary"),
    )(*deps, dh1, z, gp, w, y, o, b, c, hc, c, hc, cw, ga, gc)


def _attn_bwd(q, k, v, o, do, bias, sinks, tm, deps=()):
    t = q.shape[0]
    per_step = tm // BLOCK

    def body(s_ref, q_ref, k_ref, v_ref, o_ref, do_ref, bias_ref, dq_ref, dk_ref, dv_ref, ds_ref):
        step = pl.program_id(0)

        @pl.when(step == 0)
        def _():
            ds_ref[...] = jnp.zeros_like(ds_ref)

        heads = range(N_Q_HEADS)

        def first_matmuls(b):
            i = step * per_step + b
            rows = slice(b * BLOCK, (b + 1) * BLOCK)
            kc, vc = _two_blocks(k_ref, i), _two_blocks(v_ref, i)
            bias_i = bias_ref[jnp.minimum(i, 2)]
            kgs = [kc[:, _head(g)] for g in range(N_KV_HEADS)]
            vgs = [vc[:, _head(g)] for g in range(N_KV_HEADS)]
            qs = [q_ref[rows, _head(hh)] for hh in heads]
            dosb = [do_ref[rows, _head(hh)] for hh in heads]
            dos = [d.astype(F32) for d in dosb]
            scores = [_attn_scores(qs[hh], kgs[hh // GROUP], bias_i) for hh in heads]
            dps = [lax.dot_general(dosb[hh], vgs[hh // GROUP], (((1,), (1,)), ((), ())), preferred_element_type=F32)
                   for hh in heads]
            return kgs, qs, dos, dosb, scores, dps

        dsink = [jnp.zeros((BLOCK, 1), F32) for _ in range(N_Q_HEADS)]
        ahead = None
        for b in range(per_step):
            i = step * per_step + b
            rows = slice(b * BLOCK, (b + 1) * BLOCK)
            kgs, qs, dos, dosb, scores, dps = first_matmuls(b)
            ps, dss = [], []
            for hh in heads:
                p, share = _attn_probs(scores[hh], s_ref[hh])
                drow = jnp.sum(dos[hh] * o_ref[rows, _head(hh)].astype(F32), axis=-1, keepdims=True)
                dss.append((p * (dps[hh] - drow)).astype(BF16))
                ps.append(p.astype(BF16))
                dsink[hh] = dsink[hh] + share * drow
            for hh in heads:
                dq_ref[rows, _head(hh)] = (jnp.dot(dss[hh], kgs[hh // GROUP], preferred_element_type=F32)
                                           * SCALE).astype(BF16)
            groups = [slice(GROUP * g, GROUP * (g + 1)) for g in range(N_KV_HEADS)]
            dkg = [lax.dot_general(jnp.concatenate(dss[gr], axis=0), jnp.concatenate(qs[gr], axis=0),
                                   (((0,), (0,)), ((), ())), preferred_element_type=F32) for gr in groups]
            dvg = [lax.dot_general(jnp.concatenate(ps[gr], axis=0), jnp.concatenate(dosb[gr], axis=0),
                                   (((0,), (0,)), ((), ())), preferred_element_type=F32) for gr in groups]
            dkb, dvb = jnp.concatenate(dkg, axis=1), jnp.concatenate(dvg, axis=1)
            if b == 0:
                @pl.when(step > 0)
                def _():
                    before = pl.ds(pl.multiple_of((i - 1) * BLOCK, BLOCK), BLOCK)
                    dk_ref[before, :] += dkb[:BLOCK]
                    dv_ref[before, :] += dvb[:BLOCK]
            else:
                at = pl.ds(pl.multiple_of((i - 1) * BLOCK, BLOCK), BLOCK)
                dk_ref[at, :] = ahead[0] + dkb[:BLOCK]
                dv_ref[at, :] = ahead[1] + dvb[:BLOCK]
            ahead = (dkb[BLOCK:], dvb[BLOCK:])
        last = pl.ds(pl.multiple_of(((step + 1) * per_step - 1) * BLOCK, BLOCK), BLOCK)
        dk_ref[last, :] = ahead[0]
        dv_ref[last, :] = ahead[1]
        for hh in range(N_Q_HEADS):
            ds_ref[hh:hh + 1, :] -= jnp.sum(dsink[hh])

    whole = pl.BlockSpec((t, KV_W), lambda i: (0, 0))
    blk = pl.BlockSpec((tm, ATTN_W), lambda i: (i, 0))
    body, dep_specs = _behind(body, deps)
    return pl.pallas_call(
        body, name="attn_bwd", grid=(t // tm,),
        in_specs=dep_specs + [pl.BlockSpec(memory_space=pltpu.SMEM), blk, whole, whole, blk, blk,
                              pl.BlockSpec(bias.shape, lambda i: (0, 0, 0))],
        out_specs=[blk, whole, whole, pl.BlockSpec((N_Q_HEADS, 128), lambda i: (0, 0))],
        out_shape=[jax.ShapeDtypeStruct((t, ATTN_W), BF16), jax.ShapeDtypeStruct((t, KV_W), F32),
                   jax.ShapeDtypeStruct((t, KV_W), F32), jax.ShapeDtypeStruct((N_Q_HEADS, 128), F32)],
        compiler_params=_params("arbitrary"),
    )(*deps, sinks, q, k, v, o, do, bias)


def _in_proj_bwd(dq, dk, dv, dbch, w, dh1, h, g, tabs, tm):
    t = h.shape[0]

    def body(dq_ref, dk_ref, dv_ref, dbch_ref, w_ref, dh1_ref, h_ref, g_ref, c_ref, sa_ref, sb_ref, dh_ref, dp_ref,
             dg_ref):
        @pl.when(pl.program_id(0) == 0)
        def _():
            dg_ref[...] = jnp.zeros_like(dg_ref)

        cos, sa, sb = c_ref[...], sa_ref[...], sb_ref[...]
        rep = ATTN_W // (2 * HEAD_DIM)
        dqr = _rope_bwd(dq_ref[...].astype(F32), jnp.tile(cos, (1, rep)), jnp.tile(sa, (1, rep)),
                        jnp.tile(sb, (1, rep)))
        dkr = _rope_bwd(dk_ref[...], cos, sa, sb)
        dp = jnp.concatenate([dqr.astype(BF16), dkr.astype(BF16), dv_ref[...].astype(BF16), dbch_ref[...]], axis=1)
        dp_ref[...] = dp
        da = jnp.dot(dp, w_ref[...], preferred_element_type=F32)
        dx, dg = _rms_bwd(da, h_ref[...], g_ref[...])
        dh_ref[...] = dh1_ref[...] + dx
        dg_ref[...] += dg

    row = lambda n: pl.BlockSpec((tm, n), lambda i: (i, 0))
    full = lambda a: pl.BlockSpec(a.shape, lambda i: (0, 0))
    return pl.pallas_call(
        body, name="in_proj_bwd", grid=(t // tm,),
        in_specs=[row(ATTN_W), row(KV_W), row(KV_W), row(3 * CONV_W), full(w), row(D_MODEL), row(D_MODEL), full(g),
                  row(2 * HEAD_DIM), row(2 * HEAD_DIM), row(2 * HEAD_DIM)],
        out_specs=[row(D_MODEL), row(IN_W), pl.BlockSpec((1, D_MODEL), lambda i: (0, 0))],
        out_shape=[jax.ShapeDtypeStruct((t, D_MODEL), F32), jax.ShapeDtypeStruct((t, IN_W), BF16),
                   jax.ShapeDtypeStruct((1, D_MODEL), F32)],
        compiler_params=_params("arbitrary"),
    )(dq, dk, dv, dbch, w, dh1, h, g, *tabs)


class _Tiles:
    def __init__(self, t):
        self.tm = _row_tile(t, 640)
        self.ts = self.tm
        self.tabs = _rope_tables(t)
        self.bias = _attn_bias()


def _mixer_fwd(h, p, tl):
    a, q, k, v, b, c, hc = _in_proj(h, p["mix_pre_g"], p["w_in"], tl.tabs, tl.ts)
    o = _attn_fwd(q, k, v, tl.bias, p["sinks"], tl.tm)
    return (h, a, q, k, v, b, c, hc, o)


def _out_fwd(mixed, p, tl, deps=()):
    h, a, q, k, v, b, c, hc, o = mixed
    h1, y, z = _mix_out(h, o, b, c, hc, p["conv_w"], p["attn_out_g"], p["conv_out_g"], p["w_out"], p["mix_post_g"],
                        tl.ts, deps)
    return h1, mixed + (h1, y, z)


def _mlp_fwd(h1, saved, p, tl, target=None):
    h2, a2, slope, f = _mlp(h1, p["mlp_pre_g"], p["w_up"], p["w_down"], p["mlp_post_g"], tl.tm, target)
    return h2, saved + (a2, slope, f)


def _mlp_part_bwd(dh, saved, p, tl, deps=()):
    h1, a2, slope, f = saved[9], saved[12], saved[13], saved[14]
    df, dup, dg2 = _mlp_bwd_hidden(dh, f, p["mlp_post_g"], slope, p["w_down"], tl.tm, deps)
    dh1, dg1 = _mlp_bwd_input(dup, p["w_up"], h1, p["mlp_pre_g"], dh, tl.tm)
    g = {"w_down": [d.reshape(N_CHIPS, FF_CHUNK, D_MODEL)
                    for d in _weight_grad(slope, df, "grad_w_down", x_is_slope=True)],
         "w_up": [d.reshape(N_CHIPS, D_MODEL, FF_CHUNK) for d in _weight_grad(a2, dup, "grad_w_up")],
         "mlp_post_g": dg2, "mlp_pre_g": dg1}
    return dh1, g


def _mix_out_part_bwd(dh1, saved, p, tl, deps=()):
    b, c, hc, o, y, z = saved[5], saved[6], saved[7], saved[8], saved[10], saved[11]
    do, dbch, dgp, dga, dgc, dcw, dw, dwb = _mix_out_bwd(dh1, z, p["mix_post_g"], p["w_out"], y, o, b, c, hc,
                                                         p["conv_w"], p["attn_out_g"], p["conv_out_g"], tl.ts, deps)
    g = {"w_out": [d.reshape(N_CHIPS, D_MODEL // N_CHIPS, D_MODEL) for d in (dw, dwb)],
         "mix_post_g": dgp, "attn_out_g": dga, "conv_out_g": dgc, "conv_w": dcw}
    return (dh1, do, dbch), g


def _attn_in_part_bwd(carry, saved, p, tl, deps=()):
    dh1, do, dbch = carry
    h_in, a, q, k, v, o = saved[0], saved[1], saved[2], saved[3], saved[4], saved[8]
    dq, dk, dv, dsink = _attn_bwd(q, k, v, o, do, tl.bias, p["sinks"], tl.tm, deps)
    dh, dproj, dgi = _in_proj_bwd(dq, dk, dv, dbch, p["w_in"], dh1, h_in, p["mix_pre_g"], tl.tabs, tl.ts)
    g_in = [d.reshape(N_CHIPS, IN_W // N_CHIPS, D_MODEL) for d in _weight_grad(dproj, a, "grad_w_in")]
    return dh, {"w_in": g_in, "mix_pre_g": dgi, "sinks": dsink[:, 0]}


def _place():
    return lax.axis_index("x"), lax.axis_index("y"), lax.axis_index("c")


def _other_chips(x, y):
    return [(1 - x, y), (x, 1 - y), (1 - x, 1 - y)]


_HBM = pl.BlockSpec(memory_space=pltpu.HBM)
_SEM = pl.BlockSpec(memory_space=pltpu.SEMAPHORE)
_EFFECT = pltpu.SideEffectType.DATAFLOW_SIDE_EFFECTING


class _Exchange:
    def __init__(self, name, bufs, plan, n, after=()):
        self.name, self.plan, nb = name, plan, len(bufs)
        n_in = nb + len(after)

        def body(*refs):
            send, recv, token = refs[n_in], refs[n_in + 1], refs[-1]
            for k, (src, dst, target, _) in enumerate(plan(refs[:nb])):
                pltpu.make_async_remote_copy(src_ref=src, dst_ref=dst, send_sem=send.at[k], recv_sem=recv.at[k],
                                             device_id=target, device_id_type=MESH).start()
            token[...] = jnp.zeros_like(token)

        outs = pl.pallas_call(
            body, name=name + "_start",
            out_shape=(pltpu.SemaphoreType.DMA((n,)), pltpu.SemaphoreType.DMA((n,)),
                       *[pltpu.HBM(b.shape, b.dtype) for b in bufs], jax.ShapeDtypeStruct((8, 128), F32)),
            in_specs=[_HBM] * nb + [pl.BlockSpec(memory_space=pl.ANY)] * len(after),
            out_specs=(_SEM, _SEM, *[_HBM] * nb, pl.BlockSpec(memory_space=pltpu.VMEM)),
            input_output_aliases={i: 2 + i for i in range(nb)},
            compiler_params=pltpu.CompilerParams(has_side_effects=_EFFECT),
        )(*[pltpu.with_memory_space_constraint(b, pltpu.HBM) for b in bufs], *after)
        self.send, self.recv, self.bufs, self.token = outs[0], outs[1], list(outs[2:2 + nb]), outs[-1]

    def wait(self, *after):
        plan, nb = self.plan, len(self.bufs)

        def body(*refs):
            send, recv = refs[nb], refs[nb + 1]
            for k, (src, _, target, land) in enumerate(plan(refs[:nb])):
                cp = pltpu.make_async_remote_copy(src_ref=src, dst_ref=land, send_sem=send.at[k], recv_sem=recv.at[k],
                                                  device_id=target, device_id_type=MESH)
                cp.wait_send()
                cp.wait_recv()

        outs = pl.pallas_call(
            body, name=self.name + "_wait", out_shape=[pltpu.HBM(b.shape, b.dtype) for b in self.bufs],
            in_specs=[_HBM] * nb + [_SEM, _SEM] + [pl.BlockSpec(memory_space=pl.ANY)] * len(after),
            out_specs=[_HBM] * nb, input_output_aliases={i: i for i in range(nb)},
            compiler_params=pltpu.CompilerParams(has_side_effects=_EFFECT),
        )(*self.bufs, self.send, self.recv, *after)
        return list(outs)


def _gather_plan(n):
    def plan(refs):
        x, y, c = _place()
        me = 2 * x + y
        return [(refs[a].at[me], refs[a].at[me], (px, py, c), refs[a].at[2 * px + py])
                for a in range(n) for px, py in _other_chips(x, y)]

    return plan


def _peers():
    x, y, c = _place()
    return [(k - 1, (x ^ (k >> 2), y ^ ((k >> 1) & 1), c ^ (k & 1))) for k in range(1, N_DEV)]


def _scatter_plan(n, half_rows):
    def plan(refs):
        out = []
        for a in range(n):
            hr = half_rows[a]
            for k, (px, py, pc) in _peers():
                out.append((refs[a].at[2 * px + py, pl.ds(pc * hr, hr)], refs[n + a].at[k], (px, py, pc),
                            refs[n + a].at[k]))
        return out

    return plan


def _join_plan(n):
    def plan(refs):
        x, y, c = _place()
        return [(refs[a].at[c], refs[a].at[c], (x, y, 1 - c), refs[a].at[1 - c]) for a in range(n)]

    return plan


def _sum_parts(g, q):
    rows, cols = g.shape[1], g.shape[2]
    hr = rows // 2
    tr = _block_rows(hr)
    per = hr // tr
    x, y, c = _place()
    where = jnp.stack([2 * x + y, c]).astype(jnp.int32)

    def body(where_ref, g_ref, q_ref, o_ref):
        total = g_ref[...]
        for k in range(N_DEV - 1):
            total = total + q_ref[k].astype(F32)
        o_ref[...] = total

    return pl.pallas_call(
        body, name="sum_parts",
        grid_spec=pltpu.PrefetchScalarGridSpec(
            num_scalar_prefetch=1, grid=(per,),
            in_specs=[pl.BlockSpec((None, tr, cols), lambda i, where_ref: (where_ref[0], where_ref[1] * per + i, 0)),
                      pl.BlockSpec((N_DEV - 1, tr, cols), lambda i, where_ref: (0, i, 0))],
            out_specs=pl.BlockSpec((None, tr, cols), lambda i, where_ref: (where_ref[1], i, 0))),
        out_shape=jax.ShapeDtypeStruct((2, hr, cols), F32),
        compiler_params=_params("parallel"),
    )(where, g, q)


def _sum_devices(packed):
    def body(p_ref, o_ref, land, send_sems, recv_sems):
        x, y, c = _place()
        me = 4 * x + 2 * y + c
        land[me] = p_ref[...]
        sends = []
        for k in range(1, N_DEV):
            px, py, pc = x ^ (k >> 2), y ^ ((k >> 1) & 1), c ^ (k & 1)
            cp = pltpu.make_async_remote_copy(src_ref=p_ref, dst_ref=land.at[me], send_sem=send_sems.at[k - 1],
                                              recv_sem=recv_sems.at[k - 1], device_id=(px, py, pc), device_id_type=MESH)
            cp.start()
            sends.append(cp)
        for k in range(1, N_DEV):
            px, py, pc = x ^ (k >> 2), y ^ ((k >> 1) & 1), c ^ (k & 1)
            pltpu.make_async_remote_copy(src_ref=p_ref, dst_ref=land.at[4 * px + 2 * py + pc],
                                         send_sem=send_sems.at[k - 1], recv_sem=recv_sems.at[k - 1],
                                         device_id=(px, py, pc), device_id_type=MESH).wait_recv()
        for cp in sends:
            cp.wait_send()
        total = land[0]
        for d in range(1, N_DEV):
            total = total + land[d]
        o_ref[...] = total

    vm = pl.BlockSpec(memory_space=pltpu.VMEM)
    return pl.pallas_call(
        body, name="sum_devices", in_specs=[vm], out_specs=vm,
        out_shape=jax.ShapeDtypeStruct(packed.shape, F32),
        scratch_shapes=[pltpu.VMEM((N_DEV,) + packed.shape, F32), pltpu.SemaphoreType.DMA((N_DEV - 1,)),
                        pltpu.SemaphoreType.DMA((N_DEV - 1,))],
    )(packed)


def _adamw_math(w, g, m, v):
    m = ADAM_B1 * m + (1.0 - ADAM_B1) * g
    v = ADAM_B2 * v + (1.0 - ADAM_B2) * jnp.square(g)
    m_hat = m / (1.0 - ADAM_B1 ** ADAM_STEP)
    v_hat = v / (1.0 - ADAM_B2 ** ADAM_STEP)
    delta = -ADAM_LR * (m_hat / (jnp.sqrt(v_hat) + ADAM_EPS) + ADAM_WD * w)
    return delta, m, v


def _adamw_large(layer, w, halves, m, v, other):
    _, rows, cols = w.shape
    tr = _block_rows(rows // 2)
    per = rows // 2 // tr

    def body(w_ref, g_ref, m_ref, v_ref, *rest):
        g_out, d_ref, nm_ref, nv_ref = rest[-4:]
        g = g_ref[...]
        g_out[...] = g
        d_ref[...], nm_ref[...], nv_ref[...] = _adamw_math(w_ref[...], g, m_ref[...], v_ref[...])

    blk = pl.BlockSpec((None, tr, cols), lambda i: (layer, i, 0))
    half = pl.BlockSpec((None, tr, cols), lambda i: (i // per, i % per, 0))
    kept = [] if other is None else list(other)
    return pl.pallas_call(
        body, name="adamw_large", grid=(rows // tr,),
        in_specs=[blk, half, blk, blk] + [pl.BlockSpec(memory_space=pl.ANY)] * len(kept), out_specs=[blk] * 4,
        out_shape=[jax.ShapeDtypeStruct(w.shape, F32)] * 4,
        input_output_aliases={4 + k: k for k in range(len(kept))},
        compiler_params=_params("parallel"),
    )(w, halves, m, v, *kept)


def _adamw_small(ws, gs, ms, vs):
    n = len(ws)

    def body(*refs):
        w_r, g_r, m_r, v_r = refs[:n], refs[n:2 * n], refs[2 * n:3 * n], refs[3 * n:4 * n]
        d_r, nm_r, nv_r = refs[4 * n:5 * n], refs[5 * n:6 * n], refs[6 * n:]
        for a in range(n):
            d_r[a][...], nm_r[a][...], nv_r[a][...] = _adamw_math(w_r[a][...], g_r[a][...], m_r[a][...], v_r[a][...])

    vm = pl.BlockSpec(memory_space=pltpu.VMEM)
    outs = pl.pallas_call(
        body, name="adamw_small", in_specs=[vm] * (4 * n), out_specs=[vm] * (3 * n),
        out_shape=[jax.ShapeDtypeStruct(w.shape, F32) for w in ws] * 3,
    )(*ws, *gs, *ms, *vs)
    return outs[:n], outs[n:2 * n], outs[2 * n:]


_LARGE = ("w_in", "w_out", "w_up", "w_down")
_SMALL = ("meta_tokens", "mix_pre_g", "conv_w", "sinks", "attn_out_g", "conv_out_g", "mix_post_g", "mlp_pre_g",
          "mlp_post_g")
_ORDER = ("meta_tokens", "mix_pre_g", "w_in", "conv_w", "sinks", "attn_out_g", "conv_out_g", "w_out", "mix_post_g",
          "mlp_pre_g", "w_up", "w_down", "mlp_post_g")


class _Reduce:
    def __init__(self, name, grads, after=()):
        self.name, self.n = name, len(grads)
        self.own = [g for g, _ in grads]
        half_rows = [g.shape[1] // 2 for g in self.own]
        zones = [lax.empty((N_DEV - 1, hr, g.shape[2]), BF16) for g, hr in zip(self.own, half_rows)]
        self.exchange = _Exchange(name + "_scatter", [b for _, b in grads] + zones, _scatter_plan(self.n, half_rows),
                                  (N_DEV - 1) * self.n, after)

    @property
    def token(self):
        return self.exchange.token

    def join(self, *after):
        bufs = self.exchange.wait(*after)
        halves = [_sum_parts(g, q) for g, q in zip(self.own, bufs[self.n:])]
        self.exchange = _Exchange(self.name + "_join", halves, _join_plan(self.n), self.n)

    def done(self, *after):
        return self.exchange.wait(*after)


def _pad_cols(a, n=D_MODEL):
    return jnp.pad(a, ((0, 0), (0, n - a.shape[1])))


def kernel(x, meta_tokens, mix_pre_g, w_in, conv_w, sinks, attn_out_g, conv_out_g, w_out, mix_post_g, mlp_pre_g, w_up, w_down, mlp_post_g, loss_target, m_meta_tokens, m_mix_pre_g, m_w_in, m_conv_w, m_sinks, m_attn_out_g, m_conv_out_g, m_w_out, m_mix_post_g, m_mlp_pre_g, m_w_up, m_w_down, m_mlp_post_g, v_meta_tokens, v_mix_pre_g, v_w_in, v_conv_w, v_sinks, v_attn_out_g, v_conv_out_g, v_w_out, v_mix_post_g, v_mlp_pre_g, v_w_up, v_w_down, v_mlp_post_g):
    w = dict(meta_tokens=meta_tokens, mix_pre_g=mix_pre_g, w_in=w_in, conv_w=conv_w, sinks=sinks,
             attn_out_g=attn_out_g, conv_out_g=conv_out_g, w_out=w_out, mix_post_g=mix_post_g, mlp_pre_g=mlp_pre_g,
             w_up=w_up, w_down=w_down, mlp_post_g=mlp_post_g)
    m = dict(meta_tokens=m_meta_tokens, mix_pre_g=m_mix_pre_g, w_in=m_w_in, conv_w=m_conv_w, sinks=m_sinks,
             attn_out_g=m_attn_out_g, conv_out_g=m_conv_out_g, w_out=m_w_out, mix_post_g=m_mix_post_g,
             mlp_pre_g=m_mlp_pre_g, w_up=m_w_up, w_down=m_w_down, mlp_post_g=m_mlp_post_g)
    v = dict(meta_tokens=v_meta_tokens, mix_pre_g=v_mix_pre_g, w_in=v_w_in, conv_w=v_conv_w, sinks=v_sinks,
             attn_out_g=v_attn_out_g, conv_out_g=v_conv_out_g, w_out=v_w_out, mix_post_g=v_mix_post_g,
             mlp_pre_g=v_mlp_pre_g, w_up=v_w_up, w_down=v_w_down, mlp_post_g=v_mlp_post_g)
    chip = 2 * lax.axis_index("x") + lax.axis_index("y")
    tl = _Tiles(x.shape[1] + BLOCK)

    def zone(quarter):
        return lax.dynamic_update_slice(lax.empty((N_CHIPS,) + quarter.shape, quarter.dtype), quarter[None],
                                        (chip,) + (0,) * quarter.ndim)

    w, m, v = ({**d, "w_in": jnp.swapaxes(d["w_in"], 1, 2)} for d in (w, m, v))
    zones = {n: [zone(w[n][l].astype(BF16)) for l in range(DEPTH)] for n in _LARGE}
    first = _Exchange("gather_first", [zones["w_in"][0], zone(w["conv_w"]), zone(w["meta_tokens"])], _gather_plan(3), 9)
    out0 = _Exchange("gather_out", [zones["w_out"][0]], _gather_plan(1), 3, [first.token])
    rest = _Exchange("gather_rest", [zones[n][0] for n in ("w_up", "w_down")], _gather_plan(2), 6, [out0.token])

    def whole_in(quarters):
        return quarters.reshape(IN_W, D_MODEL)

    h = jnp.concatenate([jnp.zeros((BLOCK, D_MODEL), F32), x[0]], axis=0)
    q_in, q_conv, q_meta = first.wait(rest.token, *tl.tabs, tl.bias, h)
    conv_whole = jnp.transpose(q_conv, (1, 2, 0, 3)).reshape(DEPTH, CONV_K, CONV_W)
    meta = jnp.transpose(q_meta, (1, 0, 2)).reshape(N_META, D_MODEL)
    p = [{"conv_w": conv_whole[l], "sinks": w["sinks"][l]} for l in range(DEPTH)]
    for l in range(DEPTH):
        for n in ("mix_pre_g", "attn_out_g", "conv_out_g", "mix_post_g", "mlp_pre_g", "mlp_post_g"):
            p[l][n] = w[n][l][None, :]

    h = lax.dynamic_update_slice(h, meta, (LEAD_PAD, 0))
    p[0]["w_in"] = whole_in(q_in)
    mixed = _mixer_fwd(h, p[0], tl)
    second = _Exchange("gather_second", [zones["w_in"][1], zones["w_out"][1]], _gather_plan(2), 6, [mixed[-1]])
    second_mlp = _Exchange("gather_second_mlp", [zones["w_up"][1], zones["w_down"][1]], _gather_plan(2), 6,
                           [second.token])
    p[0]["w_out"], = out0.wait(second_mlp.token)
    h1, saved0 = _out_fwd(mixed, p[0], tl)
    p[0]["w_up"], p[0]["w_down"] = rest.wait(h1)
    h, saved0 = _mlp_fwd(h1, saved0, p[0], tl)
    q_in, p[1]["w_out"] = second.wait(h)
    p[1]["w_in"] = whole_in(q_in)
    h1, saved1 = _out_fwd(_mixer_fwd(h, p[1], tl), p[1], tl)
    p[1]["w_up"], p[1]["w_down"] = second_mlp.wait(h1)
    (loss_tile, dh), saved1 = _mlp_fwd(h1, saved1, p[1], tl, loss_target[0])

    def adamw(layer, halves, other):
        return {n: _adamw_large(layer, w[n], halves[n], m[n], v[n], None if other is None else other[n])
                for n in halves}

    dh1, g1 = _mlp_part_bwd(dh, saved1, p[1], tl)
    carry, gm = _mix_out_part_bwd(dh1, saved1, p[1], tl)
    dh, gi = _attn_in_part_bwd(carry, saved1, p[1], tl)
    g1.update(gm, **gi)
    red1 = _Reduce("reduce1", [g1[n] for n in _LARGE])
    dh1, g0 = _mlp_part_bwd(dh, saved0, p[0], tl, [red1.token])
    red1.join(g0["w_down"][0])
    carry, gm = _mix_out_part_bwd(dh1, saved0, p[0], tl, [red1.token])
    first0 = ("w_up", "w_down", "w_out")
    g0.update(gm)
    red0a = _Reduce("reduce0a", [g0[n] for n in first0])
    dh0, gi = _attn_in_part_bwd(carry, saved0, p[0], tl, [red0a.token])
    g0.update(gi)
    red0b = _Reduce("reduce0b", [g0["w_in"]])
    grad_x = dh0[BLOCK:][None]
    grads = {n: [g0[n], g1[n]] for n in g0 if n not in _LARGE}

    rows = [dh0[LEAD_PAD:BLOCK]]
    for n in ("mix_pre_g", "mix_post_g", "mlp_pre_g", "mlp_post_g"):
        rows += grads[n]
    rows += [jnp.concatenate([grads["attn_out_g"][l], grads["conv_out_g"][l]], axis=1) for l in range(DEPTH)]
    rows.append(jnp.concatenate(grads["conv_w"], axis=1))
    rows.append(_pad_cols(jnp.concatenate(grads["sinks"])[None, :]))
    rows.append(_pad_cols(loss_tile[:1]))
    packed = jnp.concatenate(rows, axis=0)
    packed = jnp.pad(packed, ((0, SMALL_ROWS - packed.shape[0]), (0, 0)))
    total = _sum_devices(packed)
    r0 = N_META
    small = {
        "meta_tokens": lax.dynamic_slice(total[:N_META], (0, chip * (D_MODEL // N_CHIPS)), (N_META, D_MODEL // N_CHIPS)),
        "mix_pre_g": total[r0:r0 + 2], "mix_post_g": total[r0 + 2:r0 + 4], "mlp_pre_g": total[r0 + 4:r0 + 6],
        "mlp_post_g": total[r0 + 6:r0 + 8],
        "attn_out_g": total[r0 + 8:r0 + 10, :ATTN_W], "conv_out_g": total[r0 + 8:r0 + 10, ATTN_W:],
        "conv_w": lax.dynamic_slice(total[r0 + 10:r0 + 13].reshape(CONV_K, DEPTH, CONV_W).transpose(1, 0, 2),
                                    (0, 0, chip * (CONV_W // N_CHIPS)), (DEPTH, CONV_K, CONV_W // N_CHIPS)),
        "sinks": total[r0 + 13, :DEPTH * N_Q_HEADS].reshape(DEPTH, N_Q_HEADS),
    }
    loss = total[r0 + 14, 0]

    ds, nms, nvs = _adamw_small([w[n] for n in _SMALL], [small[n] for n in _SMALL], [m[n] for n in _SMALL],
                                [v[n] for n in _SMALL])
    done1 = adamw(1, dict(zip(_LARGE, red1.done(red0b.token))), None)
    red0a.join(ds[0], grad_x, *[done1[n][0] for n in _LARGE])
    red0b.join(red0a.token)
    done0 = adamw(0, dict(zip(first0, red0a.done(red0b.token))), done1)
    done0.update(adamw(0, {"w_in": red0b.done(done0["w_down"][0])[0]}, done1))
    grad, delta, new_m, new_v = {}, {}, {}, {}
    for n in _LARGE:
        grad[n], delta[n], new_m[n], new_v[n] = done0[n]
    for d in (grad, delta, new_m, new_v):
        d["w_in"] = jnp.swapaxes(d["w_in"], 1, 2)
    for i, n in enumerate(_SMALL):
        grad[n], delta[n], new_m[n], new_v[n] = small[n], ds[i], nms[i], nvs[i]
    return (loss, grad_x, *[grad[n] for n in _ORDER], *[delta[n] for n in _ORDER], *[new_m[n] for n in _ORDER],
            *[new_v[n] for n in _ORDER])
```

```python
import functools

import jax
import jax.numpy as jnp
from jax import lax
from jax.experimental import pallas as pl
from jax.experimental.pallas import tpu as pltpu

F32 = jnp.float32
BF16 = jnp.bfloat16

D_MODEL = 1024
DEPTH = 2
N_META = 16
ATTN_W = 512
CONV_W = 512
HEAD_DIM = 64
N_Q_HEADS = 8
N_KV_HEADS = 2
GROUP = N_Q_HEADS // N_KV_HEADS
KV_W = N_KV_HEADS * HEAD_DIM
CONV_K = 3
BLOCK = 128
LEAD_PAD = BLOCK - N_META
ROPE_THETA = 500000.0
ROT_DIM = HEAD_DIM // 4
ROT_HALF = ROT_DIM // 2
D_FF = 4 * D_MODEL
IN_W = ATTN_W + 2 * KV_W + 3 * CONV_W
QKV_W = ATTN_W + 2 * KV_W
EPS = 1e-6
SCALE = HEAD_DIM ** -0.5
FF_CHUNK = 1024
N_CHIPS = 4
N_DEV = 8

ADAM_LR = 0.001
ADAM_B1 = 0.9
ADAM_B2 = 0.999
ADAM_EPS = 1e-08
ADAM_WD = 0.01
ADAM_STEP = 10

V7X_VMEM_LIMIT = 60 * 1024 * 1024
SMALL_ROWS = 32

MESH = pl.DeviceIdType.MESH


def _params(*sem):
    return pltpu.CompilerParams(dimension_semantics=sem, vmem_limit_bytes=V7X_VMEM_LIMIT)


def _block_rows(n):
    return max(r for r in range(16, min(n, 256) + 1, 16) if n % r == 0)


def _row_tile(t, most):
    nb = t // BLOCK
    for b in range(most // BLOCK, 0, -1):
        if nb % b == 0:
            return b * BLOCK
    return BLOCK


def _behind(body, deps):
    n = len(deps)

    def wrapped(*refs):
        body(*refs[n:])

    return wrapped, [pl.BlockSpec(memory_space=pl.ANY)] * n


def _rms(x, g):
    r = lax.rsqrt(jnp.mean(x * x, axis=-1, keepdims=True) + EPS)
    return x * r * g


def _rms_bwd(dy, x, g):
    r = lax.rsqrt(jnp.mean(x * x, axis=-1, keepdims=True) + EPS)
    xh = x * r
    dg = jnp.sum(dy * xh, axis=0, keepdims=True)
    dxh = dy * g
    dx = r * (dxh - xh * jnp.mean(dxh * xh, axis=-1, keepdims=True))
    return dx, dg


def _rope(x, cos, sa, sb):
    n = x.shape[-1]
    return x * cos + pltpu.roll(x, n - ROT_HALF, 1) * sa + pltpu.roll(x, ROT_HALF, 1) * sb


def _rope_bwd(dy, cos, sa, sb):
    n = dy.shape[-1]
    return dy * cos + pltpu.roll(dy * sa, ROT_HALF, 1) + pltpu.roll(dy * sb, n - ROT_HALF, 1)


def _rope_tables(t):
    pos = lax.broadcasted_iota(jnp.int32, (t, ROT_HALF), 0).astype(F32) - LEAD_PAD
    pair = lax.broadcasted_iota(jnp.int32, (t, ROT_HALF), 1).astype(F32)
    inv_freq = jnp.power(jnp.float32(ROPE_THETA), -(2.0 * pair) / ROT_DIM)
    ang = pos * inv_freq
    cos, sin = lax.optimization_barrier((jnp.cos(ang), jnp.sin(ang)))
    spread = (1, 2 * HEAD_DIM // ROT_HALF)
    cos, sin = jnp.tile(cos, spread), jnp.tile(sin, spread)
    dim = lax.broadcasted_iota(jnp.int32, (t, 2 * HEAD_DIM), 1) % HEAD_DIM
    return (jnp.where(dim < ROT_DIM, cos, 1.0), jnp.where(dim < ROT_HALF, -sin, 0.0),
            jnp.where((dim >= ROT_HALF) & (dim < ROT_DIM), sin, 0.0))


def _in_proj(h, g, w, tabs, tm):
    t = h.shape[0]

    def body(h_ref, g_ref, w_ref, c_ref, sa_ref, sb_ref, a_ref, q_ref, k_ref, v_ref, b_ref, cg_ref, hc_ref):
        a = _rms(h_ref[...], g_ref[...]).astype(BF16)
        a_ref[...] = a
        p = lax.dot_general(a, w_ref[...], (((1,), (1,)), ((), ())), preferred_element_type=F32)
        cos, sa, sb = c_ref[...], sa_ref[...], sb_ref[...]
        rep = ATTN_W // (2 * HEAD_DIM)
        q = _rope(p[:, :ATTN_W], jnp.tile(cos, (1, rep)), jnp.tile(sa, (1, rep)), jnp.tile(sb, (1, rep)))
        q_ref[...] = (q * SCALE).astype(BF16)
        k_ref[...] = _rope(p[:, ATTN_W:ATTN_W + KV_W], cos, sa, sb).astype(BF16)
        v_ref[...] = p[:, ATTN_W + KV_W:QKV_W].astype(BF16)
        b_ref[...] = p[:, QKV_W:QKV_W + CONV_W].astype(BF16)
        cg_ref[...] = p[:, QKV_W + CONV_W:QKV_W + 2 * CONV_W].astype(BF16)
        hc_ref[...] = p[:, QKV_W + 2 * CONV_W:].astype(BF16)

    row = lambda n: pl.BlockSpec((tm, n), lambda i: (i, 0))
    full = lambda a: pl.BlockSpec(a.shape, lambda i: (0, 0))
    return pl.pallas_call(
        body, name="in_proj", grid=(t // tm,),
        in_specs=[row(D_MODEL), full(g), full(w), row(2 * HEAD_DIM), row(2 * HEAD_DIM), row(2 * HEAD_DIM)],
        out_specs=[row(D_MODEL), row(ATTN_W), row(KV_W), row(KV_W), row(CONV_W), row(CONV_W), row(CONV_W)],
        out_shape=[jax.ShapeDtypeStruct((t, D_MODEL), BF16), jax.ShapeDtypeStruct((t, ATTN_W), BF16),
                   jax.ShapeDtypeStruct((t, KV_W), BF16), jax.ShapeDtypeStruct((t, KV_W), BF16),
                   jax.ShapeDtypeStruct((t, CONV_W), BF16), jax.ShapeDtypeStruct((t, CONV_W), BF16),
                   jax.ShapeDtypeStruct((t, CONV_W), BF16)],
        compiler_params=_params("parallel"),
    )(h, g, w, *tabs)


def _attn_bias():
    r = lax.broadcasted_iota(jnp.int32, (3, BLOCK, 2 * BLOCK), 1)
    c = lax.broadcasted_iota(jnp.int32, (3, BLOCK, 2 * BLOCK), 2)
    i = lax.broadcasted_iota(jnp.int32, (3, BLOCK, 2 * BLOCK), 0)
    ok = (c > r) & (c <= r + BLOCK) & (c + (i - 1) * BLOCK >= LEAD_PAD)
    return jnp.where(ok, 0.0, -jnp.inf).astype(F32)


def _attn_scores(qh, kg, bias):
    return lax.dot_general(qh, kg, (((1,), (1,)), ((), ())), preferred_element_type=F32) + bias


def _attn_probs(s, sk):
    m = jnp.maximum(jnp.max(s, axis=-1, keepdims=True), sk)
    e = jnp.exp(s - m)
    es = jnp.exp(sk - m)
    rden = 1.0 / (jnp.sum(e, axis=-1, keepdims=True) + es)
    return e * rden, es * rden


def _head(hh):
    return slice(hh * HEAD_DIM, (hh + 1) * HEAD_DIM)


def _two_blocks(ref, i):
    prev = jnp.maximum(i - 1, 0)
    return jnp.concatenate([ref[pl.ds(pl.multiple_of(prev * BLOCK, BLOCK), BLOCK), :],
                            ref[pl.ds(pl.multiple_of(i * BLOCK, BLOCK), BLOCK), :]], axis=0)


def _attn_fwd(q, k, v, bias, sinks, tm):
    t = q.shape[0]
    per_step = tm // BLOCK
    heads = range(N_Q_HEADS)

    def body(s_ref, q_ref, k_ref, v_ref, bias_ref, o_ref):
        for b in range(per_step):
            i = pl.program_id(0) * per_step + b
            rows = slice(b * BLOCK, (b + 1) * BLOCK)
            kc, vc = _two_blocks(k_ref, i), _two_blocks(v_ref, i)
            bias_i = bias_ref[jnp.minimum(i, 2)]
            scores = [_attn_scores(q_ref[rows, _head(hh)], kc[:, _head(hh // GROUP)], bias_i) for hh in heads]
            probs = [_attn_probs(scores[hh], s_ref[hh])[0].astype(BF16) for hh in heads]
            for hh in heads:
                o_ref[rows, _head(hh)] = jnp.dot(probs[hh], vc[:, _head(hh // GROUP)],
                                                 preferred_element_type=F32).astype(BF16)

    whole = pl.BlockSpec((t, KV_W), lambda i: (0, 0))
    return pl.pallas_call(
        body, name="attn_fwd", grid=(t // tm,),
        in_specs=[pl.BlockSpec(memory_space=pltpu.SMEM), pl.BlockSpec((tm, ATTN_W), lambda i: (i, 0)), whole, whole,
                  pl.BlockSpec(bias.shape, lambda i: (0, 0, 0))],
        out_specs=pl.BlockSpec((tm, ATTN_W), lambda i: (i, 0)),
        out_shape=jax.ShapeDtypeStruct((t, ATTN_W), BF16),
        compiler_params=_params("parallel"),
    )(sinks, q, k, v, bias)


def _shift_rows(u, halo, n):
    r = pltpu.roll(u, n, 0)
    hr = pltpu.roll(halo, n, 0)
    idx = lax.broadcasted_iota(jnp.int32, hr.shape, 0)
    return jnp.concatenate([jnp.where(idx < n, hr, r[:8]), r[8:]], axis=0)


def _advance_rows(u, halo, n):
    rows = u.shape[0]
    r = pltpu.roll(u, rows - n, 0)
    hr = pltpu.roll(halo, 8 - n, 0)
    idx = lax.broadcasted_iota(jnp.int32, hr.shape, 0)
    return jnp.concatenate([r[:rows - 8], jnp.where(idx >= 8 - n, hr, r[rows - 8:])], axis=0)


def _mix_out(h, o, b, c, hc, cw, ga, gc, w, gp, tm, deps=()):
    t = h.shape[0]

    def body(h_ref, o_ref, b_ref, c_ref, hc_ref, cw_ref, ga_ref, gc_ref, w_ref, gp_ref, h1_ref, y_ref, z_ref, halo):
        @pl.when(pl.program_id(0) == 0)
        def _():
            halo[...] = jnp.zeros_like(halo)

        u = c_ref[...].astype(F32) * hc_ref[...].astype(F32)
        cv = cw_ref[0:1, :] * _shift_rows(u, halo[...], 2) + cw_ref[1:2, :] * _shift_rows(u, halo[...], 1) \
            + cw_ref[2:3, :] * u
        halo[...] = u[tm - 8:]
        yc = b_ref[...].astype(F32) * cv
        y = jnp.concatenate([_rms(o_ref[...].astype(F32), ga_ref[...]), _rms(yc, gc_ref[...])], axis=1).astype(BF16)
        y_ref[...] = y
        z = jnp.dot(y, w_ref[...].reshape(D_MODEL, D_MODEL), preferred_element_type=F32)
        z_ref[...] = z.astype(BF16)
        h1_ref[...] = h_ref[...] + _rms(z, gp_ref[...])

    row = lambda n: pl.BlockSpec((tm, n), lambda i: (i, 0))
    full = lambda a: pl.BlockSpec(a.shape, lambda i: (0,) * a.ndim)
    body, dep_specs = _behind(body, deps)
    return pl.pallas_call(
        body, name="mix_out", grid=(t // tm,),
        in_specs=dep_specs + [row(D_MODEL), row(ATTN_W), row(CONV_W), row(CONV_W), row(CONV_W), full(cw), full(ga),
                              full(gc), full(w), full(gp)],
        out_specs=[row(D_MODEL), row(D_MODEL), row(D_MODEL)],
        out_shape=[jax.ShapeDtypeStruct((t, D_MODEL), F32), jax.ShapeDtypeStruct((t, D_MODEL), BF16),
                   jax.ShapeDtypeStruct((t, D_MODEL), BF16)],
        scratch_shapes=[pltpu.VMEM((8, CONV_W), F32)],
        compiler_params=_params("arbitrary"),
    )(*deps, h, o, b, c, hc, cw, ga, gc, w, gp)


def _mlp(h1, g1, wu, wd, g2, tm, target=None):
    t = h1.shape[0]
    nj = D_FF // FF_CHUNK
    per_step = tm // BLOCK if target is not None else 0

    def body(h1_ref, g1_ref, wu_ref, wd_ref, g2_ref, *rest):
        t_refs, outs = rest[:per_step], rest[per_step:]
        a2_ref, slope_ref, f_ref = outs[-3:]
        a2 = _rms(h1_ref[...], g1_ref[...]).astype(BF16)
        a2_ref[...] = a2
        f = None
        for j in range(nj):
            up = jnp.dot(a2, wu_ref[j], preferred_element_type=F32)
            r = jnp.maximum(up, 0.0)
            slope_ref[:, j * FF_CHUNK:(j + 1) * FF_CHUNK] = (r + r).astype(BF16)
            part = jnp.dot((r * r).astype(BF16), wd_ref[j], preferred_element_type=F32)
            f = part if f is None else f + part
        f_ref[...] = f
        h2 = h1_ref[...] + _rms(f, g2_ref[...])
        if target is None:
            outs[0][...] = h2
            return
        loss_ref, dh_ref = outs[:2]
        i = pl.program_id(0)

        @pl.when(i == 0)
        def _():
            loss_ref[...] = jnp.zeros_like(loss_ref)

        total = jnp.zeros((), F32)
        for b in range(per_step):
            rows = slice(b * BLOCK, (b + 1) * BLOCK)
            err = h2[rows] - t_refs[b][...]
            if b == 0:
                err = jnp.where(i == 0, 0.0, err)
            dh_ref[rows, :] = err * (1.0 / D_MODEL)
            total = total + jnp.sum(err * err)
        loss_ref[...] += total * (0.5 / D_MODEL)

    def target_block(b):
        return pl.BlockSpec((BLOCK, D_MODEL), lambda i: (jnp.maximum(i * per_step + b - 1, 0), 0))

    row = pl.BlockSpec((tm, D_MODEL), lambda i: (i, 0))
    vec = pl.BlockSpec((1, D_MODEL), lambda i: (0, 0))
    resident = pl.BlockSpec(memory_space=pltpu.VMEM)
    first_specs, first_shapes = [row], [jax.ShapeDtypeStruct((t, D_MODEL), F32)]
    if target is not None:
        first_specs = [pl.BlockSpec((8, 128), lambda i: (0, 0)), row]
        first_shapes = [jax.ShapeDtypeStruct((8, 128), F32), jax.ShapeDtypeStruct((t, D_MODEL), F32)]
    outs = pl.pallas_call(
        body, name="mlp", grid=(t // tm,),
        in_specs=[row, vec, resident, resident, vec] + [target_block(b) for b in range(per_step)],
        out_specs=first_specs + [row, pl.BlockSpec((tm, D_FF), lambda i: (i, 0)), row],
        out_shape=first_shapes + [jax.ShapeDtypeStruct((t, D_MODEL), BF16), jax.ShapeDtypeStruct((t, D_FF), BF16),
                                  jax.ShapeDtypeStruct((t, D_MODEL), F32)],
        compiler_params=_params("parallel" if target is None else "arbitrary"),
    )(h1, g1, wu, wd, g2, *([target] * per_step))
    return (outs[0] if target is None else tuple(outs[:2]),) + tuple(outs[-3:])


def _mlp_bwd_hidden(dh2, f, g2, slope, wd, tm, deps=()):
    t = dh2.shape[0]
    nj = D_FF // FF_CHUNK

    def body(dh2_ref, f_ref, g2_ref, slope_ref, wd_ref, df_ref, dup_ref, dg2_ref):
        @pl.when(pl.program_id(0) == 0)
        def _():
            dg2_ref[...] = jnp.zeros_like(dg2_ref)

        df, dg = _rms_bwd(dh2_ref[...], f_ref[...], g2_ref[...])
        dg2_ref[...] += dg
        df = df.astype(BF16)
        df_ref[...] = df
        for j in range(nj):
            cols = slice(j * FF_CHUNK, (j + 1) * FF_CHUNK)
            dact = lax.dot_general(df, wd_ref[j], (((1,), (1,)), ((), ())), preferred_element_type=F32)
            dup_ref[:, cols] = (dact * slope_ref[:, cols].astype(F32)).astype(BF16)

    row = pl.BlockSpec((tm, D_MODEL), lambda i: (i, 0))
    wide = pl.BlockSpec((tm, D_FF), lambda i: (i, 0))
    vec = pl.BlockSpec((1, D_MODEL), lambda i: (0, 0))
    body, dep_specs = _behind(body, deps)
    return pl.pallas_call(
        body, name="mlp_bwd_hidden", grid=(t // tm,),
        in_specs=dep_specs + [row, row, vec, wide, pl.BlockSpec(memory_space=pltpu.VMEM)],
        out_specs=[row, wide, vec],
        out_shape=[jax.ShapeDtypeStruct((t, D_MODEL), BF16), jax.ShapeDtypeStruct((t, D_FF), BF16),
                   jax.ShapeDtypeStruct((1, D_MODEL), F32)],
        compiler_params=_params("arbitrary"),
    )(*deps, dh2, f, g2, slope, wd)


def _mlp_bwd_input(dup, wu, h1, g1, dh2, tm):
    t = dh2.shape[0]
    nj = D_FF // FF_CHUNK

    def body(dup_ref, wu_ref, h1_ref, g1_ref, dh2_ref, dh1_ref, dg1_ref):
        @pl.when(pl.program_id(0) == 0)
        def _():
            dg1_ref[...] = jnp.zeros_like(dg1_ref)

        da2 = None
        for j in range(nj):
            part = lax.dot_general(dup_ref[:, j * FF_CHUNK:(j + 1) * FF_CHUNK], wu_ref[j], (((1,), (1,)), ((), ())),
                                   preferred_element_type=F32)
            da2 = part if da2 is None else da2 + part
        dx, dg = _rms_bwd(da2, h1_ref[...], g1_ref[...])
        dh1_ref[...] = dh2_ref[...] + dx
        dg1_ref[...] += dg

    row = pl.BlockSpec((tm, D_MODEL), lambda i: (i, 0))
    vec = pl.BlockSpec((1, D_MODEL), lambda i: (0, 0))
    return pl.pallas_call(
        body, name="mlp_bwd_input", grid=(t // tm,),
        in_specs=[pl.BlockSpec((tm, D_FF), lambda i: (i, 0)), pl.BlockSpec(memory_space=pltpu.VMEM), row, vec, row],
        out_specs=[row, vec],
        out_shape=[jax.ShapeDtypeStruct((t, D_MODEL), F32), jax.ShapeDtypeStruct((1, D_MODEL), F32)],
        compiler_params=_params("arbitrary"),
    )(dup, wu, h1, g1, dh2)


def _row_split(t):
    tile = min(t, 1024)
    return tile, t // tile, t % tile


def _row_split_specs(t, cols):
    tile, whole, rest = _row_split(t)
    specs = [pl.BlockSpec((tile, cols), lambda r: (jnp.minimum(r, whole - 1), 0))]
    if rest:
        specs.append(pl.BlockSpec((rest, cols), lambda r: (whole * tile // rest, 0)))
    return specs


def _weight_grad(x, y, name, x_is_slope=False):
    t, k = x.shape
    n = y.shape[1]
    tn = FF_CHUNK
    tk = FF_CHUNK if k % FF_CHUNK == 0 else k
    _, whole, rest = _row_split(t)
    steps = whole + bool(rest)

    def body(*refs):
        o_ref, ob_ref, r = refs[-2], refs[-1], pl.program_id(0)

        @pl.when(r == 0)
        def _():
            o_ref[...] = jnp.zeros_like(o_ref)

        def add(x_ref, y_ref):
            for a in range(k // tk):
                xv = x_ref[:, a * tk:(a + 1) * tk]
                if x_is_slope:
                    xv = xv.astype(F32)
                    xv = (xv * xv * 0.25).astype(BF16)
                for b in range(n // tn):
                    o_ref[a, b] += lax.dot_general(xv, y_ref[:, b * tn:(b + 1) * tn], (((0,), (0,)), ((), ())),
                                                   preferred_element_type=F32)

        if rest:
            pl.when(r < whole)(lambda: add(refs[0], refs[2]))
            pl.when(r == whole)(lambda: add(refs[1], refs[3]))
        else:
            add(refs[0], refs[1])

        @pl.when(r == steps - 1)
        def _():
            ob_ref[...] = o_ref[...].astype(BF16)

    vm = pl.BlockSpec(memory_space=pltpu.VMEM)
    return pl.pallas_call(
        body, name=name, grid=(steps,),
        in_specs=_row_split_specs(t, k) + _row_split_specs(t, n), out_specs=[vm, vm],
        out_shape=[jax.ShapeDtypeStruct((k // tk, n // tn, tk, tn), F32),
                   jax.ShapeDtypeStruct((k // tk, n // tn, tk, tn), BF16)],
        compiler_params=_params("arbitrary"),
    )(*([x] * (1 + bool(rest))), *([y] * (1 + bool(rest))))


def _mix_out_bwd(dh1, z, gp, w, o, b, c, hc, cw, ga, gc, tm, deps=()):
    t = dh1.shape[0]
    nt = t // tm
    per16 = tm // 16

    def body(dh1_ref, z_ref, gp_ref, w_ref, o_ref, b_ref, c_ref, hc_ref, cp_ref, hp_ref, cw_ref, ga_ref, gc_ref,
             dz_ref, do_ref, dbch_ref, dgp_ref, dga_ref, dgc_ref, dcw_ref, halo):
        i = pl.program_id(0)

        @pl.when(i == 0)
        def _():
            halo[...] = jnp.zeros_like(halo)
            dgp_ref[...] = jnp.zeros_like(dgp_ref)
            dga_ref[...] = jnp.zeros_like(dga_ref)
            dgc_ref[...] = jnp.zeros_like(dgc_ref)
            dcw_ref[...] = jnp.zeros_like(dcw_ref)

        dz, dgp = _rms_bwd(dh1_ref[...], z_ref[...].astype(F32), gp_ref[...])
        dgp_ref[...] += dgp
        dz = dz.astype(BF16)
        dz_ref[...] = dz
        dy = lax.dot_general(dz, w_ref[...].reshape(D_MODEL, D_MODEL), (((1,), (1,)), ((), ())),
                             preferred_element_type=F32)
        do, dga = _rms_bwd(dy[:, :ATTN_W], o_ref[...].astype(F32), ga_ref[...])
        do_ref[...] = do.astype(BF16)
        dga_ref[...] += dga

        cc, hh = c_ref[...].astype(F32), hc_ref[...].astype(F32)
        u = cc * hh
        first = i == nt - 1
        u_before = jnp.where(first, 0.0, (cp_ref[...].astype(F32) * hp_ref[...].astype(F32))[8:])
        u1 = _shift_rows(u, u_before, 1)
        u2 = _shift_rows(u, u_before, 2)
        cv = cw_ref[0:1, :] * u2 + cw_ref[1:2, :] * u1 + cw_ref[2:3, :] * u
        bb = b_ref[...].astype(F32)
        dyc, dgc = _rms_bwd(dy[:, ATTN_W:], bb * cv, gc_ref[...])
        dgc_ref[...] += dgc
        dcv = dyc * bb
        d1 = _advance_rows(dcv, halo[...], 1)
        d2 = _advance_rows(dcv, halo[...], 2)
        halo[...] = dcv[:8]
        du = cw_ref[2:3, :] * dcv + cw_ref[1:2, :] * d1 + cw_ref[0:1, :] * d2
        dbch_ref[...] = jnp.concatenate([dyc * cv, du * hh, du * cc], axis=1).astype(BF16)
        dcw_ref[...] += jnp.concatenate([jnp.sum(dcv * u2, axis=0, keepdims=True),
                                         jnp.sum(dcv * u1, axis=0, keepdims=True),
                                         jnp.sum(dcv * u, axis=0, keepdims=True)], axis=0)

    row = lambda n: pl.BlockSpec((tm, n), lambda i: (nt - 1 - i, 0))
    before = pl.BlockSpec((16, CONV_W), lambda i: (jnp.maximum((nt - 1 - i) * per16 - 1, 0), 0))
    full = lambda a: pl.BlockSpec(a.shape, lambda i: (0,) * a.ndim)
    vec = lambda n: pl.BlockSpec((1, n), lambda i: (0, 0))
    body, dep_specs = _behind(body, deps)
    return pl.pallas_call(
        body, name="mix_out_bwd", grid=(nt,),
        in_specs=dep_specs + [row(D_MODEL), row(D_MODEL), full(gp), full(w), row(ATTN_W), row(CONV_W), row(CONV_W),
                              row(CONV_W), before, before, full(cw), full(ga), full(gc)],
        out_specs=[row(D_MODEL), row(ATTN_W), row(3 * CONV_W), vec(D_MODEL), vec(ATTN_W), vec(CONV_W),
                   pl.BlockSpec((CONV_K, CONV_W), lambda i: (0, 0))],
        out_shape=[jax.ShapeDtypeStruct((t, D_MODEL), BF16), jax.ShapeDtypeStruct((t, ATTN_W), BF16),
                   jax.ShapeDtypeStruct((t, 3 * CONV_W), BF16), jax.ShapeDtypeStruct((1, D_MODEL), F32),
                   jax.ShapeDtypeStruct((1, ATTN_W), F32), jax.ShapeDtypeStruct((1, CONV_W), F32),
                   jax.ShapeDtypeStruct((CONV_K, CONV_W), F32)],
        scratch_shapes=[pltpu.VMEM((8, CONV_W), F32)],
        compiler_params=_params("arbitrary"),
    )(*deps, dh1, z, gp, w, o, b, c, hc, c, hc, cw, ga, gc)


def _attn_bwd(q, k, v, o, do, bias, sinks, tm, deps=()):
    t = q.shape[0]
    per_step = tm // BLOCK

    def body(s_ref, q_ref, k_ref, v_ref, o_ref, do_ref, bias_ref, dq_ref, dk_ref, dv_ref, ds_ref):
        step = pl.program_id(0)

        @pl.when(step == 0)
        def _():
            ds_ref[...] = jnp.zeros_like(ds_ref)

        heads = range(N_Q_HEADS)

        def first_matmuls(b):
            i = step * per_step + b
            rows = slice(b * BLOCK, (b + 1) * BLOCK)
            kc, vc = _two_blocks(k_ref, i), _two_blocks(v_ref, i)
            bias_i = bias_ref[jnp.minimum(i, 2)]
            kgs = [kc[:, _head(g)] for g in range(N_KV_HEADS)]
            vgs = [vc[:, _head(g)] for g in range(N_KV_HEADS)]
            qs = [q_ref[rows, _head(hh)] for hh in heads]
            dosb = [do_ref[rows, _head(hh)] for hh in heads]
            dos = [d.astype(F32) for d in dosb]
            scores = [_attn_scores(qs[hh], kgs[hh // GROUP], bias_i) for hh in heads]
            dps = [lax.dot_general(dosb[hh], vgs[hh // GROUP], (((1,), (1,)), ((), ())), preferred_element_type=F32)
                   for hh in heads]
            return kgs, qs, dos, dosb, scores, dps

        dsink = [jnp.zeros((BLOCK, 1), F32) for _ in range(N_Q_HEADS)]
        ahead = None
        for b in range(per_step):
            i = step * per_step + b
            rows = slice(b * BLOCK, (b + 1) * BLOCK)
            kgs, qs, dos, dosb, scores, dps = first_matmuls(b)
            ps, dss = [], []
            for hh in heads:
                p, share = _attn_probs(scores[hh], s_ref[hh])
                drow = jnp.sum(dos[hh] * o_ref[rows, _head(hh)].astype(F32), axis=-1, keepdims=True)
                dss.append((p * (dps[hh] - drow)).astype(BF16))
                ps.append(p.astype(BF16))
                dsink[hh] = dsink[hh] + share * drow
            for hh in heads:
                dq_ref[rows, _head(hh)] = (jnp.dot(dss[hh], kgs[hh // GROUP], preferred_element_type=F32)
                                           * SCALE).astype(BF16)
            groups = [slice(GROUP * g, GROUP * (g + 1)) for g in range(N_KV_HEADS)]
            dkg = [lax.dot_general(jnp.concatenate(dss[gr], axis=0), jnp.concatenate(qs[gr], axis=0),
                                   (((0,), (0,)), ((), ())), preferred_element_type=F32) for gr in groups]
            dvg = [lax.dot_general(jnp.concatenate(ps[gr], axis=0), jnp.concatenate(dosb[gr], axis=0),
                                   (((0,), (0,)), ((), ())), preferred_element_type=F32) for gr in groups]
            dkb, dvb = jnp.concatenate(dkg, axis=1), jnp.concatenate(dvg, axis=1)
            if b == 0:
                @pl.when(step > 0)
                def _():
                    before = pl.ds(pl.multiple_of((i - 1) * BLOCK, BLOCK), BLOCK)
                    dk_ref[before, :] += dkb[:BLOCK]
                    dv_ref[before, :] += dvb[:BLOCK]
            else:
                at = pl.ds(pl.multiple_of((i - 1) * BLOCK, BLOCK), BLOCK)
                dk_ref[at, :] = ahead[0] + dkb[:BLOCK]
                dv_ref[at, :] = ahead[1] + dvb[:BLOCK]
            ahead = (dkb[BLOCK:], dvb[BLOCK:])
        last = pl.ds(pl.multiple_of(((step + 1) * per_step - 1) * BLOCK, BLOCK), BLOCK)
        dk_ref[last, :] = ahead[0]
        dv_ref[last, :] = ahead[1]
        for hh in range(N_Q_HEADS):
            ds_ref[hh:hh + 1, :] -= jnp.sum(dsink[hh])

    whole = pl.BlockSpec((t, KV_W), lambda i: (0, 0))
    blk = pl.BlockSpec((tm, ATTN_W), lambda i: (i, 0))
    body, dep_specs = _behind(body, deps)
    return pl.pallas_call(
        body, name="attn_bwd", grid=(t // tm,),
        in_specs=dep_specs + [pl.BlockSpec(memory_space=pltpu.SMEM), blk, whole, whole, blk, blk,
                              pl.BlockSpec(bias.shape, lambda i: (0, 0, 0))],
        out_specs=[blk, whole, whole, pl.BlockSpec((N_Q_HEADS, 128), lambda i: (0, 0))],
        out_shape=[jax.ShapeDtypeStruct((t, ATTN_W), BF16), jax.ShapeDtypeStruct((t, KV_W), F32),
                   jax.ShapeDtypeStruct((t, KV_W), F32), jax.ShapeDtypeStruct((N_Q_HEADS, 128), F32)],
        compiler_params=_params("arbitrary"),
    )(*deps, sinks, q, k, v, o, do, bias)


def _in_proj_bwd(dq, dk, dv, dbch, w, dh1, h, g, tabs, tm):
    t = h.shape[0]

    def body(dq_ref, dk_ref, dv_ref, dbch_ref, w_ref, dh1_ref, h_ref, g_ref, c_ref, sa_ref, sb_ref, dh_ref, dp_ref,
             dg_ref):
        @pl.when(pl.program_id(0) == 0)
        def _():
            dg_ref[...] = jnp.zeros_like(dg_ref)

        cos, sa, sb = c_ref[...], sa_ref[...], sb_ref[...]
        rep = ATTN_W // (2 * HEAD_DIM)
        dqr = _rope_bwd(dq_ref[...].astype(F32), jnp.tile(cos, (1, rep)), jnp.tile(sa, (1, rep)),
                        jnp.tile(sb, (1, rep)))
        dkr = _rope_bwd(dk_ref[...], cos, sa, sb)
        dp = jnp.concatenate([dqr.astype(BF16), dkr.astype(BF16), dv_ref[...].astype(BF16), dbch_ref[...]], axis=1)
        dp_ref[...] = dp
        da = jnp.dot(dp, w_ref[...], preferred_element_type=F32)
        dx, dg = _rms_bwd(da, h_ref[...], g_ref[...])
        dh_ref[...] = dh1_ref[...] + dx
        dg_ref[...] += dg

    row = lambda n: pl.BlockSpec((tm, n), lambda i: (i, 0))
    full = lambda a: pl.BlockSpec(a.shape, lambda i: (0, 0))
    return pl.pallas_call(
        body, name="in_proj_bwd", grid=(t // tm,),
        in_specs=[row(ATTN_W), row(KV_W), row(KV_W), row(3 * CONV_W), full(w), row(D_MODEL), row(D_MODEL), full(g),
                  row(2 * HEAD_DIM), row(2 * HEAD_DIM), row(2 * HEAD_DIM)],
        out_specs=[row(D_MODEL), row(IN_W), pl.BlockSpec((1, D_MODEL), lambda i: (0, 0))],
        out_shape=[jax.ShapeDtypeStruct((t, D_MODEL), F32), jax.ShapeDtypeStruct((t, IN_W), BF16),
                   jax.ShapeDtypeStruct((1, D_MODEL), F32)],
        compiler_params=_params("arbitrary"),
    )(dq, dk, dv, dbch, w, dh1, h, g, *tabs)


class _Tiles:
    def __init__(self, t):
        self.tm = _row_tile(t, 640)
        self.ts = self.tm
        self.tabs = _rope_tables(t)
        self.bias = _attn_bias()


def _mixer_fwd(h, p, tl):
    a, q, k, v, b, c, hc = _in_proj(h, p["mix_pre_g"], p["w_in"], tl.tabs, tl.ts)
    o = _attn_fwd(q, k, v, tl.bias, p["sinks"], tl.tm)
    return (h, a, q, k, v, b, c, hc, o)


def _out_fwd(mixed, p, tl, deps=()):
    h, a, q, k, v, b, c, hc, o = mixed
    h1, y, z = _mix_out(h, o, b, c, hc, p["conv_w"], p["attn_out_g"], p["conv_out_g"], p["w_out"], p["mix_post_g"],
                        tl.ts, deps)
    return h1, mixed + (h1, y, z)


def _mlp_fwd(h1, saved, p, tl, target=None):
    h2, a2, slope, f = _mlp(h1, p["mlp_pre_g"], p["w_up"], p["w_down"], p["mlp_post_g"], tl.tm, target)
    return h2, saved + (a2, slope, f)


def _mlp_part_bwd(dh, saved, p, tl, deps=()):
    h1, a2, slope, f = saved[9], saved[12], saved[13], saved[14]
    df, dup, dg2 = _mlp_bwd_hidden(dh, f, p["mlp_post_g"], slope, p["w_down"], tl.tm, deps)
    dh1, dg1 = _mlp_bwd_input(dup, p["w_up"], h1, p["mlp_pre_g"], dh, tl.tm)
    g = {"w_down": [d.reshape(N_CHIPS, FF_CHUNK, D_MODEL)
                    for d in _weight_grad(slope, df, "grad_w_down", x_is_slope=True)],
         "w_up": [d.reshape(N_CHIPS, D_MODEL, FF_CHUNK) for d in _weight_grad(a2, dup, "grad_w_up")],
         "mlp_post_g": dg2, "mlp_pre_g": dg1}
    return dh1, g


def _mix_out_part_bwd(dh1, saved, p, tl, deps=()):
    b, c, hc, o, y, z = saved[5], saved[6], saved[7], saved[8], saved[10], saved[11]
    dz, do, dbch, dgp, dga, dgc, dcw = _mix_out_bwd(dh1, z, p["mix_post_g"], p["w_out"], o, b, c, hc, p["conv_w"],
                                                    p["attn_out_g"], p["conv_out_g"], tl.ts, deps)
    g = {"w_out": [d.reshape(N_CHIPS, D_MODEL // N_CHIPS, D_MODEL) for d in _weight_grad(y, dz, "grad_w_out")],
         "mix_post_g": dgp, "attn_out_g": dga, "conv_out_g": dgc, "conv_w": dcw}
    return (dh1, do, dbch), g


def _attn_in_part_bwd(carry, saved, p, tl, deps=()):
    dh1, do, dbch = carry
    h_in, a, q, k, v, o = saved[0], saved[1], saved[2], saved[3], saved[4], saved[8]
    dq, dk, dv, dsink = _attn_bwd(q, k, v, o, do, tl.bias, p["sinks"], tl.tm, deps)
    dh, dproj, dgi = _in_proj_bwd(dq, dk, dv, dbch, p["w_in"], dh1, h_in, p["mix_pre_g"], tl.tabs, tl.ts)
    g_in = [d.reshape(N_CHIPS, IN_W // N_CHIPS, D_MODEL) for d in _weight_grad(dproj, a, "grad_w_in")]
    return dh, {"w_in": g_in, "mix_pre_g": dgi, "sinks": dsink[:, 0]}


def _place():
    return lax.axis_index("x"), lax.axis_index("y"), lax.axis_index("c")


def _other_chips(x, y):
    return [(1 - x, y), (x, 1 - y), (1 - x, 1 - y)]


_HBM = pl.BlockSpec(memory_space=pltpu.HBM)
_SEM = pl.BlockSpec(memory_space=pltpu.SEMAPHORE)
_EFFECT = pltpu.SideEffectType.DATAFLOW_SIDE_EFFECTING


class _Exchange:
    def __init__(self, name, bufs, plan, n, after=()):
        self.name, self.plan, nb = name, plan, len(bufs)
        n_in = nb + len(after)

        def body(*refs):
            send, recv, token = refs[n_in], refs[n_in + 1], refs[-1]
            for k, (src, dst, target, _) in enumerate(plan(refs[:nb])):
                pltpu.make_async_remote_copy(src_ref=src, dst_ref=dst, send_sem=send.at[k], recv_sem=recv.at[k],
                                             device_id=target, device_id_type=MESH).start()
            token[...] = jnp.zeros_like(token)

        outs = pl.pallas_call(
            body, name=name + "_start",
            out_shape=(pltpu.SemaphoreType.DMA((n,)), pltpu.SemaphoreType.DMA((n,)),
                       *[pltpu.HBM(b.shape, b.dtype) for b in bufs], jax.ShapeDtypeStruct((8, 128), F32)),
            in_specs=[_HBM] * nb + [pl.BlockSpec(memory_space=pl.ANY)] * len(after),
            out_specs=(_SEM, _SEM, *[_HBM] * nb, pl.BlockSpec(memory_space=pltpu.VMEM)),
            input_output_aliases={i: 2 + i for i in range(nb)},
            compiler_params=pltpu.CompilerParams(has_side_effects=_EFFECT),
        )(*[pltpu.with_memory_space_constraint(b, pltpu.HBM) for b in bufs], *after)
        self.send, self.recv, self.bufs, self.token = outs[0], outs[1], list(outs[2:2 + nb]), outs[-1]

    def wait(self, *after):
        plan, nb = self.plan, len(self.bufs)

        def body(*refs):
            send, recv = refs[nb], refs[nb + 1]
            for k, (src, _, target, land) in enumerate(plan(refs[:nb])):
                cp = pltpu.make_async_remote_copy(src_ref=src, dst_ref=land, send_sem=send.at[k], recv_sem=recv.at[k],
                                                  device_id=target, device_id_type=MESH)
                cp.wait_send()
                cp.wait_recv()

        outs = pl.pallas_call(
            body, name=self.name + "_wait", out_shape=[pltpu.HBM(b.shape, b.dtype) for b in self.bufs],
            in_specs=[_HBM] * nb + [_SEM, _SEM] + [pl.BlockSpec(memory_space=pl.ANY)] * len(after),
            out_specs=[_HBM] * nb, input_output_aliases={i: i for i in range(nb)},
            compiler_params=pltpu.CompilerParams(has_side_effects=_EFFECT),
        )(*self.bufs, self.send, self.recv, *after)
        return list(outs)


def _gather_plan(n):
    def plan(refs):
        x, y, c = _place()
        me = 2 * x + y
        return [(refs[a].at[me], refs[a].at[me], (px, py, c), refs[a].at[2 * px + py])
                for a in range(n) for px, py in _other_chips(x, y)]

    return plan


def _peers():
    x, y, c = _place()
    return [(k - 1, (x ^ (k >> 2), y ^ ((k >> 1) & 1), c ^ (k & 1))) for k in range(1, N_DEV)]


def _scatter_plan(n, half_rows):
    def plan(refs):
        out = []
        for a in range(n):
            hr = half_rows[a]
            for k, (px, py, pc) in _peers():
                out.append((refs[a].at[2 * px + py, pl.ds(pc * hr, hr)], refs[n + a].at[k], (px, py, pc),
                            refs[n + a].at[k]))
        return out

    return plan


def _join_plan(n):
    def plan(refs):
        x, y, c = _place()
        return [(refs[a].at[c], refs[a].at[c], (x, y, 1 - c), refs[a].at[1 - c]) for a in range(n)]

    return plan


def _sum_parts(g, q):
    rows, cols = g.shape[1], g.shape[2]
    hr = rows // 2
    tr = _block_rows(hr)
    per = hr // tr
    x, y, c = _place()
    where = jnp.stack([2 * x + y, c]).astype(jnp.int32)

    def body(where_ref, g_ref, q_ref, o_ref):
        total = g_ref[...]
        for k in range(N_DEV - 1):
            total = total + q_ref[k].astype(F32)
        o_ref[...] = total

    return pl.pallas_call(
        body, name="sum_parts",
        grid_spec=pltpu.PrefetchScalarGridSpec(
            num_scalar_prefetch=1, grid=(per,),
            in_specs=[pl.BlockSpec((None, tr, cols), lambda i, where_ref: (where_ref[0], where_ref[1] * per + i, 0)),
                      pl.BlockSpec((N_DEV - 1, tr, cols), lambda i, where_ref: (0, i, 0))],
            out_specs=pl.BlockSpec((None, tr, cols), lambda i, where_ref: (where_ref[1], i, 0))),
        out_shape=jax.ShapeDtypeStruct((2, hr, cols), F32),
        compiler_params=_params("parallel"),
    )(where, g, q)


def _sum_devices(packed):
    def body(p_ref, o_ref, land, send_sems, recv_sems):
        x, y, c = _place()
        me = 4 * x + 2 * y + c
        land[me] = p_ref[...]
        sends = []
        for k in range(1, N_DEV):
            px, py, pc = x ^ (k >> 2), y ^ ((k >> 1) & 1), c ^ (k & 1)
            cp = pltpu.make_async_remote_copy(src_ref=p_ref, dst_ref=land.at[me], send_sem=send_sems.at[k - 1],
                                              recv_sem=recv_sems.at[k - 1], device_id=(px, py, pc), device_id_type=MESH)
            cp.start()
            sends.append(cp)
        for k in range(1, N_DEV):
            px, py, pc = x ^ (k >> 2), y ^ ((k >> 1) & 1), c ^ (k & 1)
            pltpu.make_async_remote_copy(src_ref=p_ref, dst_ref=land.at[4 * px + 2 * py + pc],
                                         send_sem=send_sems.at[k - 1], recv_sem=recv_sems.at[k - 1],
                                         device_id=(px, py, pc), device_id_type=MESH).wait_recv()
        for cp in sends:
            cp.wait_send()
        total = land[0]
        for d in range(1, N_DEV):
            total = total + land[d]
        o_ref[...] = total

    vm = pl.BlockSpec(memory_space=pltpu.VMEM)
    return pl.pallas_call(
        body, name="sum_devices", in_specs=[vm], out_specs=vm,
        out_shape=jax.ShapeDtypeStruct(packed.shape, F32),
        scratch_shapes=[pltpu.VMEM((N_DEV,) + packed.shape, F32), pltpu.SemaphoreType.DMA((N_DEV - 1,)),
                        pltpu.SemaphoreType.DMA((N_DEV - 1,))],
    )(packed)


def _adamw_math(w, g, m, v):
    m = ADAM_B1 * m + (1.0 - ADAM_B1) * g
    v = ADAM_B2 * v + (1.0 - ADAM_B2) * jnp.square(g)
    m_hat = m / (1.0 - ADAM_B1 ** ADAM_STEP)
    v_hat = v / (1.0 - ADAM_B2 ** ADAM_STEP)
    delta = -ADAM_LR * (m_hat / (jnp.sqrt(v_hat) + ADAM_EPS) + ADAM_WD * w)
    return delta, m, v


def _adamw_large(layer, w, halves, m, v, other):
    _, rows, cols = w.shape
    tr = _block_rows(rows // 2)
    per = rows // 2 // tr

    def body(w_ref, g_ref, m_ref, v_ref, *rest):
        g_out, d_ref, nm_ref, nv_ref = rest[-4:]
        g = g_ref[...]
        g_out[...] = g
        d_ref[...], nm_ref[...], nv_ref[...] = _adamw_math(w_ref[...], g, m_ref[...], v_ref[...])

    blk = pl.BlockSpec((None, tr, cols), lambda i: (layer, i, 0))
    half = pl.BlockSpec((None, tr, cols), lambda i: (i // per, i % per, 0))
    kept = [] if other is None else list(other)
    return pl.pallas_call(
        body, name="adamw_large", grid=(rows // tr,),
        in_specs=[blk, half, blk, blk] + [pl.BlockSpec(memory_space=pl.ANY)] * len(kept), out_specs=[blk] * 4,
        out_shape=[jax.ShapeDtypeStruct(w.shape, F32)] * 4,
        input_output_aliases={4 + k: k for k in range(len(kept))},
        compiler_params=_params("parallel"),
    )(w, halves, m, v, *kept)


def _adamw_small(ws, gs, ms, vs):
    n = len(ws)

    def body(*refs):
        w_r, g_r, m_r, v_r = refs[:n], refs[n:2 * n], refs[2 * n:3 * n], refs[3 * n:4 * n]
        d_r, nm_r, nv_r = refs[4 * n:5 * n], refs[5 * n:6 * n], refs[6 * n:]
        for a in range(n):
            d_r[a][...], nm_r[a][...], nv_r[a][...] = _adamw_math(w_r[a][...], g_r[a][...], m_r[a][...], v_r[a][...])

    vm = pl.BlockSpec(memory_space=pltpu.VMEM)
    outs = pl.pallas_call(
        body, name="adamw_small", in_specs=[vm] * (4 * n), out_specs=[vm] * (3 * n),
        out_shape=[jax.ShapeDtypeStruct(w.shape, F32) for w in ws] * 3,
    )(*ws, *gs, *ms, *vs)
    return outs[:n], outs[n:2 * n], outs[2 * n:]


_LARGE = ("w_in", "w_out", "w_up", "w_down")
_SMALL = ("meta_tokens", "mix_pre_g", "conv_w", "sinks", "attn_out_g", "conv_out_g", "mix_post_g", "mlp_pre_g",
          "mlp_post_g")
_ORDER = ("meta_tokens", "mix_pre_g", "w_in", "conv_w", "sinks", "attn_out_g", "conv_out_g", "w_out", "mix_post_g",
          "mlp_pre_g", "w_up", "w_down", "mlp_post_g")


class _Reduce:
    def __init__(self, name, grads, after=()):
        self.name, self.n = name, len(grads)
        self.own = [g for g, _ in grads]
        half_rows = [g.shape[1] // 2 for g in self.own]
        zones = [lax.empty((N_DEV - 1, hr, g.shape[2]), BF16) for g, hr in zip(self.own, half_rows)]
        self.exchange = _Exchange(name + "_scatter", [b for _, b in grads] + zones, _scatter_plan(self.n, half_rows),
                                  (N_DEV - 1) * self.n, after)

    @property
    def token(self):
        return self.exchange.token

    def join(self, *after):
        bufs = self.exchange.wait(*after)
        halves = [_sum_parts(g, q) for g, q in zip(self.own, bufs[self.n:])]
        self.exchange = _Exchange(self.name + "_join", halves, _join_plan(self.n), self.n)

    def done(self, *after):
        return self.exchange.wait(*after)


def _pad_cols(a, n=D_MODEL):
    return jnp.pad(a, ((0, 0), (0, n - a.shape[1])))


def kernel(x, meta_tokens, mix_pre_g, w_in, conv_w, sinks, attn_out_g, conv_out_g, w_out, mix_post_g, mlp_pre_g, w_up, w_down, mlp_post_g, loss_target, m_meta_tokens, m_mix_pre_g, m_w_in, m_conv_w, m_sinks, m_attn_out_g, m_conv_out_g, m_w_out, m_mix_post_g, m_mlp_pre_g, m_w_up, m_w_down, m_mlp_post_g, v_meta_tokens, v_mix_pre_g, v_w_in, v_conv_w, v_sinks, v_attn_out_g, v_conv_out_g, v_w_out, v_mix_post_g, v_mlp_pre_g, v_w_up, v_w_down, v_mlp_post_g):
    w = dict(meta_tokens=meta_tokens, mix_pre_g=mix_pre_g, w_in=w_in, conv_w=conv_w, sinks=sinks,
             attn_out_g=attn_out_g, conv_out_g=conv_out_g, w_out=w_out, mix_post_g=mix_post_g, mlp_pre_g=mlp_pre_g,
             w_up=w_up, w_down=w_down, mlp_post_g=mlp_post_g)
    m = dict(meta_tokens=m_meta_tokens, mix_pre_g=m_mix_pre_g, w_in=m_w_in, conv_w=m_conv_w, sinks=m_sinks,
             attn_out_g=m_attn_out_g, conv_out_g=m_conv_out_g, w_out=m_w_out, mix_post_g=m_mix_post_g,
             mlp_pre_g=m_mlp_pre_g, w_up=m_w_up, w_down=m_w_down, mlp_post_g=m_mlp_post_g)
    v = dict(meta_tokens=v_meta_tokens, mix_pre_g=v_mix_pre_g, w_in=v_w_in, conv_w=v_conv_w, sinks=v_sinks,
             attn_out_g=v_attn_out_g, conv_out_g=v_conv_out_g, w_out=v_w_out, mix_post_g=v_mix_post_g,
             mlp_pre_g=v_mlp_pre_g, w_up=v_w_up, w_down=v_w_down, mlp_post_g=v_mlp_post_g)
    chip = 2 * lax.axis_index("x") + lax.axis_index("y")
    tl = _Tiles(x.shape[1] + BLOCK)

    def zone(quarter):
        return lax.dynamic_update_slice(lax.empty((N_CHIPS,) + quarter.shape, quarter.dtype), quarter[None],
                                        (chip,) + (0,) * quarter.ndim)

    w, m, v = ({**d, "w_in": jnp.swapaxes(d["w_in"], 1, 2)} for d in (w, m, v))
    zones = {n: [zone(w[n][l].astype(BF16)) for l in range(DEPTH)] for n in _LARGE}
    first = _Exchange("gather_first", [zones["w_in"][0], zone(w["conv_w"]), zone(w["meta_tokens"])], _gather_plan(3), 9)
    out0 = _Exchange("gather_out", [zones["w_out"][0]], _gather_plan(1), 3, [first.token])
    rest = _Exchange("gather_rest", [zones[n][0] for n in ("w_up", "w_down")], _gather_plan(2), 6, [out0.token])

    def whole_in(quarters):
        return quarters.reshape(IN_W, D_MODEL)

    h = jnp.concatenate([jnp.zeros((BLOCK, D_MODEL), F32), x[0]], axis=0)
    q_in, q_conv, q_meta = first.wait(rest.token, *tl.tabs, tl.bias, h)
    conv_whole = jnp.transpose(q_conv, (1, 2, 0, 3)).reshape(DEPTH, CONV_K, CONV_W)
    meta = jnp.transpose(q_meta, (1, 0, 2)).reshape(N_META, D_MODEL)
    p = [{"conv_w": conv_whole[l], "sinks": w["sinks"][l]} for l in range(DEPTH)]
    for l in range(DEPTH):
        for n in ("mix_pre_g", "attn_out_g", "conv_out_g", "mix_post_g", "mlp_pre_g", "mlp_post_g"):
            p[l][n] = w[n][l][None, :]

    h = lax.dynamic_update_slice(h, meta, (LEAD_PAD, 0))
    p[0]["w_in"] = whole_in(q_in)
    mixed = _mixer_fwd(h, p[0], tl)
    second = _Exchange("gather_second", [zones["w_in"][1], zones["w_out"][1]], _gather_plan(2), 6, [mixed[-1]])
    second_mlp = _Exchange("gather_second_mlp", [zones["w_up"][1], zones["w_down"][1]], _gather_plan(2), 6,
                           [second.token])
    p[0]["w_out"], = out0.wait(second_mlp.token)
    h1, saved0 = _out_fwd(mixed, p[0], tl)
    p[0]["w_up"], p[0]["w_down"] = rest.wait(h1)
    h, saved0 = _mlp_fwd(h1, saved0, p[0], tl)
    q_in, p[1]["w_out"] = second.wait(h)
    p[1]["w_in"] = whole_in(q_in)
    h1, saved1 = _out_fwd(_mixer_fwd(h, p[1], tl), p[1], tl)
    p[1]["w_up"], p[1]["w_down"] = second_mlp.wait(h1)
    (loss_tile, dh), saved1 = _mlp_fwd(h1, saved1, p[1], tl, loss_target[0])

    def adamw(layer, halves, other):
        return {n: _adamw_large(layer, w[n], halves[n], m[n], v[n], None if other is None else other[n])
                for n in halves}

    dh1, g1 = _mlp_part_bwd(dh, saved1, p[1], tl)
    carry, gm = _mix_out_part_bwd(dh1, saved1, p[1], tl)
    dh, gi = _attn_in_part_bwd(carry, saved1, p[1], tl)
    g1.update(gm, **gi)
    red1 = _Reduce("reduce1", [g1[n] for n in _LARGE])
    dh1, g0 = _mlp_part_bwd(dh, saved0, p[0], tl, [red1.token])
    red1.join(g0["w_down"][0])
    carry, gm = _mix_out_part_bwd(dh1, saved0, p[0], tl, [red1.token])
    first0 = ("w_up", "w_down", "w_out")
    g0.update(gm)
    red0a = _Reduce("reduce0a", [g0[n] for n in first0])
    dh0, gi = _attn_in_part_bwd(carry, saved0, p[0], tl, [red0a.token])
    g0.update(gi)
    red0b = _Reduce("reduce0b", [g0["w_in"]])
    grad_x = dh0[BLOCK:][None]
    grads = {n: [g0[n], g1[n]] for n in g0 if n not in _LARGE}

    rows = [dh0[LEAD_PAD:BLOCK]]
    for n in ("mix_pre_g", "mix_post_g", "mlp_pre_g", "mlp_post_g"):
        rows += grads[n]
    rows += [jnp.concatenate([grads["attn_out_g"][l], grads["conv_out_g"][l]], axis=1) for l in range(DEPTH)]
    rows.append(jnp.concatenate(grads["conv_w"], axis=1))
    rows.append(_pad_cols(jnp.concatenate(grads["sinks"])[None, :]))
    rows.append(_pad_cols(loss_tile[:1]))
    packed = jnp.concatenate(rows, axis=0)
    packed = jnp.pad(packed, ((0, SMALL_ROWS - packed.shape[0]), (0, 0)))
    total = _sum_devices(packed)
    r0 = N_META
    small = {
        "meta_tokens": lax.dynamic_slice(total[:N_META], (0, chip * (D_MODEL // N_CHIPS)), (N_META, D_MODEL // N_CHIPS)),
        "mix_pre_g": total[r0:r0 + 2], "mix_post_g": total[r0 + 2:r0 + 4], "mlp_pre_g": total[r0 + 4:r0 + 6],
        "mlp_post_g": total[r0 + 6:r0 + 8],
        "attn_out_g": total[r0 + 8:r0 + 10, :ATTN_W], "conv_out_g": total[r0 + 8:r0 + 10, ATTN_W:],
        "conv_w": lax.dynamic_slice(total[r0 + 10:r0 + 13].reshape(CONV_K, DEPTH, CONV_W).transpose(1, 0, 2),
                                    (0, 0, chip * (CONV_W // N_CHIPS)), (DEPTH, CONV_K, CONV_W // N_CHIPS)),
        "sinks": total[r0 + 13, :DEPTH * N_Q_HEADS].reshape(DEPTH, N_Q_HEADS),
    }
    loss = total[r0 + 14, 0]

    ds, nms, nvs = _adamw_small([w[n] for n in _SMALL], [small[n] for n in _SMALL], [m[n] for n in _SMALL],
                                [v[n] for n in _SMALL])
    done1 = adamw(1, dict(zip(_LARGE, red1.done(red0b.token))), None)
    red0a.join(ds[0], grad_x, *[done1[n][0] for n in _LARGE])
    red0b.join(red0a.token)
    done0 = adamw(0, dict(zip(first0, red0a.done(red0b.token))), done1)
    done0.update(adamw(0, {"w_in": red0b.done(done0["w_down"][0])[0]}, done1))
    grad, delta, new_m, new_v = {}, {}, {}, {}
    for n in _LARGE:
        grad[n], delta[n], new_m[n], new_v[n] = done0[n]
    for d in (grad, delta, new_m, new_v):
        d["w_in"] = jnp.swapaxes(d["w_in"], 1, 2)
    for i, n in enumerate(_SMALL):
        grad[n], delta[n], new_m[n], new_v[n] = small[n], ds[i], nms[i], nvs[i]
    return (loss, grad_x, *[grad[n] for n in _ORDER], *[delta[n] for n in _ORDER], *[new_m[n] for n in _ORDER],
            *[new_v[n] for n in _ORDER])
```

```python
import functools

import jax
import jax.numpy as jnp
from jax import lax
from jax.experimental import pallas as pl
from jax.experimental.pallas import tpu as pltpu

F32 = jnp.float32
BF16 = jnp.bfloat16

D_MODEL = 1024
DEPTH = 2
N_META = 16
ATTN_W = 512
CONV_W = 512
HEAD_DIM = 64
N_Q_HEADS = 8
N_KV_HEADS = 2
GROUP = N_Q_HEADS // N_KV_HEADS
KV_W = N_KV_HEADS * HEAD_DIM
CONV_K = 3
BLOCK = 128
LEAD_PAD = BLOCK - N_META
ROPE_THETA = 500000.0
ROT_DIM = HEAD_DIM // 4
ROT_HALF = ROT_DIM // 2
D_FF = 4 * D_MODEL
IN_W = ATTN_W + 2 * KV_W + 3 * CONV_W
QKV_W = ATTN_W + 2 * KV_W
EPS = 1e-6
SCALE = HEAD_DIM ** -0.5
FF_CHUNK = 1024
N_CHIPS = 4
N_DEV = 8

ADAM_LR = 0.001
ADAM_B1 = 0.9
ADAM_B2 = 0.999
ADAM_EPS = 1e-08
ADAM_WD = 0.01
ADAM_STEP = 10

V7X_VMEM_LIMIT = 60 * 1024 * 1024
SMALL_ROWS = 32

MESH = pl.DeviceIdType.MESH


def _params(*sem):
    return pltpu.CompilerParams(dimension_semantics=sem, vmem_limit_bytes=V7X_VMEM_LIMIT)


def _block_rows(n):
    return max(r for r in range(16, min(n, 256) + 1, 16) if n % r == 0)


def _row_tile(t, most):
    nb = t // BLOCK
    for b in range(most // BLOCK, 0, -1):
        if nb % b == 0:
            return b * BLOCK
    return BLOCK


def _behind(body, deps):
    n = len(deps)

    def wrapped(*refs):
        body(*refs[n:])

    return wrapped, [pl.BlockSpec(memory_space=pl.ANY)] * n


def _rms(x, g):
    r = lax.rsqrt(jnp.mean(x * x, axis=-1, keepdims=True) + EPS)
    return x * r * g


def _rms_bwd(dy, x, g):
    r = lax.rsqrt(jnp.mean(x * x, axis=-1, keepdims=True) + EPS)
    xh = x * r
    dg = jnp.sum(dy * xh, axis=0, keepdims=True)
    dxh = dy * g
    dx = r * (dxh - xh * jnp.mean(dxh * xh, axis=-1, keepdims=True))
    return dx, dg


def _rope(x, cos, sa, sb):
    n = x.shape[-1]
    return x * cos + pltpu.roll(x, n - ROT_HALF, 1) * sa + pltpu.roll(x, ROT_HALF, 1) * sb


def _rope_bwd(dy, cos, sa, sb):
    n = dy.shape[-1]
    return dy * cos + pltpu.roll(dy * sa, ROT_HALF, 1) + pltpu.roll(dy * sb, n - ROT_HALF, 1)


def _rope_tables(t):
    pos = lax.broadcasted_iota(jnp.int32, (t, ROT_HALF), 0).astype(F32) - LEAD_PAD
    pair = lax.broadcasted_iota(jnp.int32, (t, ROT_HALF), 1).astype(F32)
    inv_freq = jnp.power(jnp.float32(ROPE_THETA), -(2.0 * pair) / ROT_DIM)
    ang = pos * inv_freq
    cos, sin = lax.optimization_barrier((jnp.cos(ang), jnp.sin(ang)))
    spread = (1, 2 * HEAD_DIM // ROT_HALF)
    cos, sin = jnp.tile(cos, spread), jnp.tile(sin, spread)
    dim = lax.broadcasted_iota(jnp.int32, (t, 2 * HEAD_DIM), 1) % HEAD_DIM
    return (jnp.where(dim < ROT_DIM, cos, 1.0), jnp.where(dim < ROT_HALF, -sin, 0.0),
            jnp.where((dim >= ROT_HALF) & (dim < ROT_DIM), sin, 0.0))


def _in_proj(h, g, w, tabs, tm):
    t = h.shape[0]

    def body(h_ref, g_ref, w_ref, c_ref, sa_ref, sb_ref, a_ref, q_ref, k_ref, v_ref, b_ref, cg_ref, hc_ref):
        a = _rms(h_ref[...], g_ref[...]).astype(BF16)
        a_ref[...] = a
        p = lax.dot_general(a, w_ref[...], (((1,), (1,)), ((), ())), preferred_element_type=F32)
        cos, sa, sb = c_ref[...], sa_ref[...], sb_ref[...]
        rep = ATTN_W // (2 * HEAD_DIM)
        q = _rope(p[:, :ATTN_W], jnp.tile(cos, (1, rep)), jnp.tile(sa, (1, rep)), jnp.tile(sb, (1, rep)))
        q_ref[...] = (q * SCALE).astype(BF16)
        k_ref[...] = _rope(p[:, ATTN_W:ATTN_W + KV_W], cos, sa, sb).astype(BF16)
        v_ref[...] = p[:, ATTN_W + KV_W:QKV_W].astype(BF16)
        b_ref[...] = p[:, QKV_W:QKV_W + CONV_W].astype(BF16)
        cg_ref[...] = p[:, QKV_W + CONV_W:QKV_W + 2 * CONV_W].astype(BF16)
        hc_ref[...] = p[:, QKV_W + 2 * CONV_W:].astype(BF16)

    row = lambda n: pl.BlockSpec((tm, n), lambda i: (i, 0))
    full = lambda a: pl.BlockSpec(a.shape, lambda i: (0, 0))
    return pl.pallas_call(
        body, name="in_proj", grid=(t // tm,),
        in_specs=[row(D_MODEL), full(g), full(w), row(2 * HEAD_DIM), row(2 * HEAD_DIM), row(2 * HEAD_DIM)],
        out_specs=[row(D_MODEL), row(ATTN_W), row(KV_W), row(KV_W), row(CONV_W), row(CONV_W), row(CONV_W)],
        out_shape=[jax.ShapeDtypeStruct((t, D_MODEL), BF16), jax.ShapeDtypeStruct((t, ATTN_W), BF16),
                   jax.ShapeDtypeStruct((t, KV_W), BF16), jax.ShapeDtypeStruct((t, KV_W), BF16),
                   jax.ShapeDtypeStruct((t, CONV_W), BF16), jax.ShapeDtypeStruct((t, CONV_W), BF16),
                   jax.ShapeDtypeStruct((t, CONV_W), BF16)],
        compiler_params=_params("parallel"),
    )(h, g, w, *tabs)


def _attn_bias():
    r = lax.broadcasted_iota(jnp.int32, (3, BLOCK, 2 * BLOCK), 1)
    c = lax.broadcasted_iota(jnp.int32, (3, BLOCK, 2 * BLOCK), 2)
    i = lax.broadcasted_iota(jnp.int32, (3, BLOCK, 2 * BLOCK), 0)
    ok = (c > r) & (c <= r + BLOCK) & (c + (i - 1) * BLOCK >= LEAD_PAD)
    return jnp.where(ok, 0.0, -jnp.inf).astype(F32)


def _attn_scores(qh, kg, bias):
    return lax.dot_general(qh, kg, (((1,), (1,)), ((), ())), preferred_element_type=F32) + bias


def _attn_probs(s, sk):
    m = jnp.maximum(jnp.max(s, axis=-1, keepdims=True), sk)
    e = jnp.exp(s - m)
    es = jnp.exp(sk - m)
    rden = 1.0 / (jnp.sum(e, axis=-1, keepdims=True) + es)
    return e * rden, es * rden


def _head(hh):
    return slice(hh * HEAD_DIM, (hh + 1) * HEAD_DIM)


def _two_blocks(ref, i):
    prev = jnp.maximum(i - 1, 0)
    return jnp.concatenate([ref[pl.ds(pl.multiple_of(prev * BLOCK, BLOCK), BLOCK), :],
                            ref[pl.ds(pl.multiple_of(i * BLOCK, BLOCK), BLOCK), :]], axis=0)


def _attn_fwd(q, k, v, bias, sinks, tm):
    t = q.shape[0]
    per_step = tm // BLOCK
    heads = range(N_Q_HEADS)

    def body(s_ref, q_ref, k_ref, v_ref, bias_ref, o_ref):
        for b in range(per_step):
            i = pl.program_id(0) * per_step + b
            rows = slice(b * BLOCK, (b + 1) * BLOCK)
            kc, vc = _two_blocks(k_ref, i), _two_blocks(v_ref, i)
            bias_i = bias_ref[jnp.minimum(i, 2)]
            scores = [_attn_scores(q_ref[rows, _head(hh)], kc[:, _head(hh // GROUP)], bias_i) for hh in heads]
            probs = [_attn_probs(scores[hh], s_ref[hh])[0].astype(BF16) for hh in heads]
            for hh in heads:
                o_ref[rows, _head(hh)] = jnp.dot(probs[hh], vc[:, _head(hh // GROUP)],
                                                 preferred_element_type=F32).astype(BF16)

    whole = pl.BlockSpec((t, KV_W), lambda i: (0, 0))
    return pl.pallas_call(
        body, name="attn_fwd", grid=(t // tm,),
        in_specs=[pl.BlockSpec(memory_space=pltpu.SMEM), pl.BlockSpec((tm, ATTN_W), lambda i: (i, 0)), whole, whole,
                  pl.BlockSpec(bias.shape, lambda i: (0, 0, 0))],
        out_specs=pl.BlockSpec((tm, ATTN_W), lambda i: (i, 0)),
        out_shape=jax.ShapeDtypeStruct((t, ATTN_W), BF16),
        compiler_params=_params("parallel"),
    )(sinks, q, k, v, bias)


def _shift_rows(u, halo, n):
    r = pltpu.roll(u, n, 0)
    hr = pltpu.roll(halo, n, 0)
    idx = lax.broadcasted_iota(jnp.int32, hr.shape, 0)
    return jnp.concatenate([jnp.where(idx < n, hr, r[:8]), r[8:]], axis=0)


def _advance_rows(u, halo, n):
    rows = u.shape[0]
    r = pltpu.roll(u, rows - n, 0)
    hr = pltpu.roll(halo, 8 - n, 0)
    idx = lax.broadcasted_iota(jnp.int32, hr.shape, 0)
    return jnp.concatenate([r[:rows - 8], jnp.where(idx >= 8 - n, hr, r[rows - 8:])], axis=0)


def _mix_out(h, o, b, c, hc, cw, ga, gc, w, gp, tm, deps=()):
    t = h.shape[0]

    def body(h_ref, o_ref, b_ref, c_ref, hc_ref, cw_ref, ga_ref, gc_ref, w_ref, gp_ref, h1_ref, y_ref, z_ref, halo):
        @pl.when(pl.program_id(0) == 0)
        def _():
            halo[...] = jnp.zeros_like(halo)

        u = c_ref[...].astype(F32) * hc_ref[...].astype(F32)
        cv = cw_ref[0:1, :] * _shift_rows(u, halo[...], 2) + cw_ref[1:2, :] * _shift_rows(u, halo[...], 1) \
            + cw_ref[2:3, :] * u
        halo[...] = u[tm - 8:]
        yc = b_ref[...].astype(F32) * cv
        y = jnp.concatenate([_rms(o_ref[...].astype(F32), ga_ref[...]), _rms(yc, gc_ref[...])], axis=1).astype(BF16)
        y_ref[...] = y
        z = jnp.dot(y, w_ref[...].reshape(D_MODEL, D_MODEL), preferred_element_type=F32)
        z_ref[...] = z
        h1_ref[...] = h_ref[...] + _rms(z, gp_ref[...])

    row = lambda n: pl.BlockSpec((tm, n), lambda i: (i, 0))
    full = lambda a: pl.BlockSpec(a.shape, lambda i: (0,) * a.ndim)
    body, dep_specs = _behind(body, deps)
    return pl.pallas_call(
        body, name="mix_out", grid=(t // tm,),
        in_specs=dep_specs + [row(D_MODEL), row(ATTN_W), row(CONV_W), row(CONV_W), row(CONV_W), full(cw), full(ga),
                              full(gc), full(w), full(gp)],
        out_specs=[row(D_MODEL), row(D_MODEL), row(D_MODEL)],
        out_shape=[jax.ShapeDtypeStruct((t, D_MODEL), F32), jax.ShapeDtypeStruct((t, D_MODEL), BF16),
                   jax.ShapeDtypeStruct((t, D_MODEL), F32)],
        scratch_shapes=[pltpu.VMEM((8, CONV_W), F32)],
        compiler_params=_params("arbitrary"),
    )(*deps, h, o, b, c, hc, cw, ga, gc, w, gp)


def _mlp(h1, g1, wu, wd, g2, tm, target=None):
    t = h1.shape[0]
    nj = D_FF // FF_CHUNK
    per_step = tm // BLOCK if target is not None else 0

    def body(h1_ref, g1_ref, wu_ref, wd_ref, g2_ref, *rest):
        t_refs, outs = rest[:per_step], rest[per_step:]
        a2_ref, slope_ref, f_ref = outs[-3:]
        a2 = _rms(h1_ref[...], g1_ref[...]).astype(BF16)
        a2_ref[...] = a2
        f = None
        for j in range(nj):
            up = jnp.dot(a2, wu_ref[j], preferred_element_type=F32)
            r = jnp.maximum(up, 0.0)
            slope_ref[:, j * FF_CHUNK:(j + 1) * FF_CHUNK] = (r + r).astype(BF16)
            part = jnp.dot((r * r).astype(BF16), wd_ref[j], preferred_element_type=F32)
            f = part if f is None else f + part
        f_ref[...] = f
        h2 = h1_ref[...] + _rms(f, g2_ref[...])
        if target is None:
            outs[0][...] = h2
            return
        loss_ref, dh_ref = outs[:2]
        i = pl.program_id(0)

        @pl.when(i == 0)
        def _():
            loss_ref[...] = jnp.zeros_like(loss_ref)

        total = jnp.zeros((), F32)
        for b in range(per_step):
            rows = slice(b * BLOCK, (b + 1) * BLOCK)
            err = h2[rows] - t_refs[b][...]
            if b == 0:
                err = jnp.where(i == 0, 0.0, err)
            dh_ref[rows, :] = err * (1.0 / D_MODEL)
            total = total + jnp.sum(err * err)
        loss_ref[...] += total * (0.5 / D_MODEL)

    def target_block(b):
        return pl.BlockSpec((BLOCK, D_MODEL), lambda i: (jnp.maximum(i * per_step + b - 1, 0), 0))

    row = pl.BlockSpec((tm, D_MODEL), lambda i: (i, 0))
    vec = pl.BlockSpec((1, D_MODEL), lambda i: (0, 0))
    resident = pl.BlockSpec(memory_space=pltpu.VMEM)
    first_specs, first_shapes = [row], [jax.ShapeDtypeStruct((t, D_MODEL), F32)]
    if target is not None:
        first_specs = [pl.BlockSpec((8, 128), lambda i: (0, 0)), row]
        first_shapes = [jax.ShapeDtypeStruct((8, 128), F32), jax.ShapeDtypeStruct((t, D_MODEL), F32)]
    outs = pl.pallas_call(
        body, name="mlp", grid=(t // tm,),
        in_specs=[row, vec, resident, resident, vec] + [target_block(b) for b in range(per_step)],
        out_specs=first_specs + [row, pl.BlockSpec((tm, D_FF), lambda i: (i, 0)), row],
        out_shape=first_shapes + [jax.ShapeDtypeStruct((t, D_MODEL), BF16), jax.ShapeDtypeStruct((t, D_FF), BF16),
                                  jax.ShapeDtypeStruct((t, D_MODEL), F32)],
        compiler_params=_params("parallel" if target is None else "arbitrary"),
    )(h1, g1, wu, wd, g2, *([target] * per_step))
    return (outs[0] if target is None else tuple(outs[:2]),) + tuple(outs[-3:])


def _mlp_bwd_hidden(dh2, f, g2, slope, wd, tm, deps=()):
    t = dh2.shape[0]
    nj = D_FF // FF_CHUNK

    def body(dh2_ref, f_ref, g2_ref, slope_ref, wd_ref, df_ref, dup_ref, dg2_ref):
        @pl.when(pl.program_id(0) == 0)
        def _():
            dg2_ref[...] = jnp.zeros_like(dg2_ref)

        df, dg = _rms_bwd(dh2_ref[...], f_ref[...], g2_ref[...])
        dg2_ref[...] += dg
        df = df.astype(BF16)
        df_ref[...] = df
        for j in range(nj):
            cols = slice(j * FF_CHUNK, (j + 1) * FF_CHUNK)
            dact = lax.dot_general(df, wd_ref[j], (((1,), (1,)), ((), ())), preferred_element_type=F32)
            dup_ref[:, cols] = (dact * slope_ref[:, cols].astype(F32)).astype(BF16)

    row = pl.BlockSpec((tm, D_MODEL), lambda i: (i, 0))
    wide = pl.BlockSpec((tm, D_FF), lambda i: (i, 0))
    vec = pl.BlockSpec((1, D_MODEL), lambda i: (0, 0))
    body, dep_specs = _behind(body, deps)
    return pl.pallas_call(
        body, name="mlp_bwd_hidden", grid=(t // tm,),
        in_specs=dep_specs + [row, row, vec, wide, pl.BlockSpec(memory_space=pltpu.VMEM)],
        out_specs=[row, wide, vec],
        out_shape=[jax.ShapeDtypeStruct((t, D_MODEL), BF16), jax.ShapeDtypeStruct((t, D_FF), BF16),
                   jax.ShapeDtypeStruct((1, D_MODEL), F32)],
        compiler_params=_params("arbitrary"),
    )(*deps, dh2, f, g2, slope, wd)


def _mlp_bwd_input(dup, wu, h1, g1, dh2, tm):
    t = dh2.shape[0]
    nj = D_FF // FF_CHUNK

    def body(dup_ref, wu_ref, h1_ref, g1_ref, dh2_ref, dh1_ref, dg1_ref):
        @pl.when(pl.program_id(0) == 0)
        def _():
            dg1_ref[...] = jnp.zeros_like(dg1_ref)

        da2 = None
        for j in range(nj):
            part = lax.dot_general(dup_ref[:, j * FF_CHUNK:(j + 1) * FF_CHUNK], wu_ref[j], (((1,), (1,)), ((), ())),
                                   preferred_element_type=F32)
            da2 = part if da2 is None else da2 + part
        dx, dg = _rms_bwd(da2, h1_ref[...], g1_ref[...])
        dh1_ref[...] = dh2_ref[...] + dx
        dg1_ref[...] += dg

    row = pl.BlockSpec((tm, D_MODEL), lambda i: (i, 0))
    vec = pl.BlockSpec((1, D_MODEL), lambda i: (0, 0))
    return pl.pallas_call(
        body, name="mlp_bwd_input", grid=(t // tm,),
        in_specs=[pl.BlockSpec((tm, D_FF), lambda i: (i, 0)), pl.BlockSpec(memory_space=pltpu.VMEM), row, vec, row],
        out_specs=[row, vec],
        out_shape=[jax.ShapeDtypeStruct((t, D_MODEL), F32), jax.ShapeDtypeStruct((1, D_MODEL), F32)],
        compiler_params=_params("arbitrary"),
    )(dup, wu, h1, g1, dh2)


def _row_split(t):
    tile = min(t, 1024)
    return tile, t // tile, t % tile


def _row_split_specs(t, cols):
    tile, whole, rest = _row_split(t)
    specs = [pl.BlockSpec((tile, cols), lambda r: (jnp.minimum(r, whole - 1), 0))]
    if rest:
        specs.append(pl.BlockSpec((rest, cols), lambda r: (whole * tile // rest, 0)))
    return specs


def _weight_grad(x, y, name, x_is_slope=False):
    t, k = x.shape
    n = y.shape[1]
    tn = FF_CHUNK
    tk = FF_CHUNK if k % FF_CHUNK == 0 else k
    _, whole, rest = _row_split(t)
    steps = whole + bool(rest)

    def body(*refs):
        o_ref, ob_ref, r = refs[-2], refs[-1], pl.program_id(0)

        @pl.when(r == 0)
        def _():
            o_ref[...] = jnp.zeros_like(o_ref)

        def add(x_ref, y_ref):
            for a in range(k // tk):
                xv = x_ref[:, a * tk:(a + 1) * tk]
                if x_is_slope:
                    xv = xv.astype(F32)
                    xv = (xv * xv * 0.25).astype(BF16)
                for b in range(n // tn):
                    o_ref[a, b] += lax.dot_general(xv, y_ref[:, b * tn:(b + 1) * tn], (((0,), (0,)), ((), ())),
                                                   preferred_element_type=F32)

        if rest:
            pl.when(r < whole)(lambda: add(refs[0], refs[2]))
            pl.when(r == whole)(lambda: add(refs[1], refs[3]))
        else:
            add(refs[0], refs[1])

        @pl.when(r == steps - 1)
        def _():
            ob_ref[...] = o_ref[...].astype(BF16)

    vm = pl.BlockSpec(memory_space=pltpu.VMEM)
    return pl.pallas_call(
        body, name=name, grid=(steps,),
        in_specs=_row_split_specs(t, k) + _row_split_specs(t, n), out_specs=[vm, vm],
        out_shape=[jax.ShapeDtypeStruct((k // tk, n // tn, tk, tn), F32),
                   jax.ShapeDtypeStruct((k // tk, n // tn, tk, tn), BF16)],
        compiler_params=_params("arbitrary"),
    )(*([x] * (1 + bool(rest))), *([y] * (1 + bool(rest))))


def _mix_out_bwd(dh1, z, gp, w, o, b, c, hc, cw, ga, gc, tm, deps=()):
    t = dh1.shape[0]
    nt = t // tm
    per16 = tm // 16

    def body(dh1_ref, z_ref, gp_ref, w_ref, o_ref, b_ref, c_ref, hc_ref, cp_ref, hp_ref, cw_ref, ga_ref, gc_ref,
             dz_ref, do_ref, dbch_ref, dgp_ref, dga_ref, dgc_ref, dcw_ref, halo):
        i = pl.program_id(0)

        @pl.when(i == 0)
        def _():
            halo[...] = jnp.zeros_like(halo)
            dgp_ref[...] = jnp.zeros_like(dgp_ref)
            dga_ref[...] = jnp.zeros_like(dga_ref)
            dgc_ref[...] = jnp.zeros_like(dgc_ref)
            dcw_ref[...] = jnp.zeros_like(dcw_ref)

        dz, dgp = _rms_bwd(dh1_ref[...], z_ref[...], gp_ref[...])
        dgp_ref[...] += dgp
        dz = dz.astype(BF16)
        dz_ref[...] = dz
        dy = lax.dot_general(dz, w_ref[...].reshape(D_MODEL, D_MODEL), (((1,), (1,)), ((), ())),
                             preferred_element_type=F32)
        do, dga = _rms_bwd(dy[:, :ATTN_W], o_ref[...].astype(F32), ga_ref[...])
        do_ref[...] = do.astype(BF16)
        dga_ref[...] += dga

        cc, hh = c_ref[...].astype(F32), hc_ref[...].astype(F32)
        u = cc * hh
        first = i == nt - 1
        u_before = jnp.where(first, 0.0, (cp_ref[...].astype(F32) * hp_ref[...].astype(F32))[8:])
        u1 = _shift_rows(u, u_before, 1)
        u2 = _shift_rows(u, u_before, 2)
        cv = cw_ref[0:1, :] * u2 + cw_ref[1:2, :] * u1 + cw_ref[2:3, :] * u
        bb = b_ref[...].astype(F32)
        dyc, dgc = _rms_bwd(dy[:, ATTN_W:], bb * cv, gc_ref[...])
        dgc_ref[...] += dgc
        dcv = dyc * bb
        d1 = _advance_rows(dcv, halo[...], 1)
        d2 = _advance_rows(dcv, halo[...], 2)
        halo[...] = dcv[:8]
        du = cw_ref[2:3, :] * dcv + cw_ref[1:2, :] * d1 + cw_ref[0:1, :] * d2
        dbch_ref[...] = jnp.concatenate([dyc * cv, du * hh, du * cc], axis=1).astype(BF16)
        dcw_ref[...] += jnp.concatenate([jnp.sum(dcv * u2, axis=0, keepdims=True),
                                         jnp.sum(dcv * u1, axis=0, keepdims=True),
                                         jnp.sum(dcv * u, axis=0, keepdims=True)], axis=0)

    row = lambda n: pl.BlockSpec((tm, n), lambda i: (nt - 1 - i, 0))
    before = pl.BlockSpec((16, CONV_W), lambda i: (jnp.maximum((nt - 1 - i) * per16 - 1, 0), 0))
    full = lambda a: pl.BlockSpec(a.shape, lambda i: (0,) * a.ndim)
    vec = lambda n: pl.BlockSpec((1, n), lambda i: (0, 0))
    body, dep_specs = _behind(body, deps)
    return pl.pallas_call(
        body, name="mix_out_bwd", grid=(nt,),
        in_specs=dep_specs + [row(D_MODEL), row(D_MODEL), full(gp), full(w), row(ATTN_W), row(CONV_W), row(CONV_W),
                              row(CONV_W), before, before, full(cw), full(ga), full(gc)],
        out_specs=[row(D_MODEL), row(ATTN_W), row(3 * CONV_W), vec(D_MODEL), vec(ATTN_W), vec(CONV_W),
                   pl.BlockSpec((CONV_K, CONV_W), lambda i: (0, 0))],
        out_shape=[jax.ShapeDtypeStruct((t, D_MODEL), BF16), jax.ShapeDtypeStruct((t, ATTN_W), BF16),
                   jax.ShapeDtypeStruct((t, 3 * CONV_W), BF16), jax.ShapeDtypeStruct((1, D_MODEL), F32),
                   jax.ShapeDtypeStruct((1, ATTN_W), F32), jax.ShapeDtypeStruct((1, CONV_W), F32),
                   jax.ShapeDtypeStruct((CONV_K, CONV_W), F32)],
        scratch_shapes=[pltpu.VMEM((8, CONV_W), F32)],
        compiler_params=_params("arbitrary"),
    )(*deps, dh1, z, gp, w, o, b, c, hc, c, hc, cw, ga, gc)


def _attn_bwd(q, k, v, o, do, bias, sinks, tm, deps=()):
    t = q.shape[0]
    per_step = tm // BLOCK

    def body(s_ref, q_ref, k_ref, v_ref, o_ref, do_ref, bias_ref, dq_ref, dk_ref, dv_ref, ds_ref):
        step = pl.program_id(0)

        @pl.when(step == 0)
        def _():
            ds_ref[...] = jnp.zeros_like(ds_ref)

        heads = range(N_Q_HEADS)

        def first_matmuls(b):
            i = step * per_step + b
            rows = slice(b * BLOCK, (b + 1) * BLOCK)
            kc, vc = _two_blocks(k_ref, i), _two_blocks(v_ref, i)
            bias_i = bias_ref[jnp.minimum(i, 2)]
            kgs = [kc[:, _head(g)] for g in range(N_KV_HEADS)]
            vgs = [vc[:, _head(g)] for g in range(N_KV_HEADS)]
            qs = [q_ref[rows, _head(hh)] for hh in heads]
            dosb = [do_ref[rows, _head(hh)] for hh in heads]
            dos = [d.astype(F32) for d in dosb]
            scores = [_attn_scores(qs[hh], kgs[hh // GROUP], bias_i) for hh in heads]
            dps = [lax.dot_general(dosb[hh], vgs[hh // GROUP], (((1,), (1,)), ((), ())), preferred_element_type=F32)
                   for hh in heads]
            return kgs, qs, dos, dosb, scores, dps

        dsink = [jnp.zeros((BLOCK, 1), F32) for _ in range(N_Q_HEADS)]
        ahead = None
        for b in range(per_step):
            i = step * per_step + b
            rows = slice(b * BLOCK, (b + 1) * BLOCK)
            kgs, qs, dos, dosb, scores, dps = first_matmuls(b)
            ps, dss = [], []
            for hh in heads:
                p, share = _attn_probs(scores[hh], s_ref[hh])
                drow = jnp.sum(dos[hh] * o_ref[rows, _head(hh)].astype(F32), axis=-1, keepdims=True)
                dss.append((p * (dps[hh] - drow)).astype(BF16))
                ps.append(p.astype(BF16))
                dsink[hh] = dsink[hh] + share * drow
            for hh in heads:
                dq_ref[rows, _head(hh)] = (jnp.dot(dss[hh], kgs[hh // GROUP], preferred_element_type=F32)
                                           * SCALE).astype(BF16)
            groups = [slice(GROUP * g, GROUP * (g + 1)) for g in range(N_KV_HEADS)]
            dkg = [lax.dot_general(jnp.concatenate(dss[gr], axis=0), jnp.concatenate(qs[gr], axis=0),
                                   (((0,), (0,)), ((), ())), preferred_element_type=F32) for gr in groups]
            dvg = [lax.dot_general(jnp.concatenate(ps[gr], axis=0), jnp.concatenate(dosb[gr], axis=0),
                                   (((0,), (0,)), ((), ())), preferred_element_type=F32) for gr in groups]
            dkb, dvb = jnp.concatenate(dkg, axis=1), jnp.concatenate(dvg, axis=1)
            if b == 0:
                @pl.when(step > 0)
                def _():
                    before = pl.ds(pl.multiple_of((i - 1) * BLOCK, BLOCK), BLOCK)
                    dk_ref[before, :] += dkb[:BLOCK]
                    dv_ref[before, :] += dvb[:BLOCK]
            else:
                at = pl.ds(pl.multiple_of((i - 1) * BLOCK, BLOCK), BLOCK)
                dk_ref[at, :] = ahead[0] + dkb[:BLOCK]
                dv_ref[at, :] = ahead[1] + dvb[:BLOCK]
            ahead = (dkb[BLOCK:], dvb[BLOCK:])
        last = pl.ds(pl.multiple_of(((step + 1) * per_step - 1) * BLOCK, BLOCK), BLOCK)
        dk_ref[last, :] = ahead[0]
        dv_ref[last, :] = ahead[1]
        for hh in range(N_Q_HEADS):
            ds_ref[hh:hh + 1, :] -= jnp.sum(dsink[hh])

    whole = pl.BlockSpec((t, KV_W), lambda i: (0, 0))
    blk = pl.BlockSpec((tm, ATTN_W), lambda i: (i, 0))
    body, dep_specs = _behind(body, deps)
    return pl.pallas_call(
        body, name="attn_bwd", grid=(t // tm,),
        in_specs=dep_specs + [pl.BlockSpec(memory_space=pltpu.SMEM), blk, whole, whole, blk, blk,
                              pl.BlockSpec(bias.shape, lambda i: (0, 0, 0))],
        out_specs=[blk, whole, whole, pl.BlockSpec((N_Q_HEADS, 128), lambda i: (0, 0))],
        out_shape=[jax.ShapeDtypeStruct((t, ATTN_W), BF16), jax.ShapeDtypeStruct((t, KV_W), F32),
                   jax.ShapeDtypeStruct((t, KV_W), F32), jax.ShapeDtypeStruct((N_Q_HEADS, 128), F32)],
        compiler_params=_params("arbitrary"),
    )(*deps, sinks, q, k, v, o, do, bias)


def _in_proj_bwd(dq, dk, dv, dbch, w, dh1, h, g, tabs, tm, split_lead=False):
    t = h.shape[0]
    nt = t // tm

    def body(dq_ref, dk_ref, dv_ref, dbch_ref, w_ref, dh1_ref, h_ref, g_ref, c_ref, sa_ref, sb_ref, *rest):
        dp_ref, dg_ref = rest[2:4] if split_lead else rest[1:3]
        i = pl.program_id(0)

        @pl.when(i == 0)
        def _():
            dg_ref[...] = jnp.zeros_like(dg_ref)

        cos, sa, sb = c_ref[...], sa_ref[...], sb_ref[...]
        rep = ATTN_W // (2 * HEAD_DIM)
        dqr = _rope_bwd(dq_ref[...].astype(F32), jnp.tile(cos, (1, rep)), jnp.tile(sa, (1, rep)),
                        jnp.tile(sb, (1, rep)))
        dkr = _rope_bwd(dk_ref[...], cos, sa, sb)
        dp = jnp.concatenate([dqr.astype(BF16), dkr.astype(BF16), dv_ref[...].astype(BF16), dbch_ref[...]], axis=1)
        dp_ref[...] = dp
        da = jnp.dot(dp, w_ref[...], preferred_element_type=F32)
        dx, dg = _rms_bwd(da, h_ref[...], g_ref[...])
        dg_ref[...] += dg
        dh = dh1_ref[...] + dx
        if not split_lead:
            rest[0][...] = dh
            return
        lead_ref, seq_ref, stage, sems = rest[0], rest[1], rest[4], rest[5]

        def copy(j, slot, first):
            if first:
                return pltpu.make_async_copy(stage.at[slot, pl.ds(BLOCK, tm - BLOCK)],
                                             seq_ref.at[pl.ds(0, tm - BLOCK)], sems.at[slot])
            return pltpu.make_async_copy(stage.at[slot], seq_ref.at[pl.ds(pl.multiple_of(j * tm - BLOCK, BLOCK), tm)],
                                         sems.at[slot])

        slot = i % 2
        pl.when(i == 2)(lambda: copy(0, slot, True).wait())
        pl.when(i > 2)(lambda: copy(i - 2, slot, False).wait())
        stage[slot] = dh

        @pl.when(i == 0)
        def _():
            lead_ref[...] = dh[:BLOCK]
            copy(0, slot, True).start()

        pl.when(i > 0)(lambda: copy(i, slot, False).start())

        @pl.when(i == nt - 1)
        def _():
            for j in range(max(nt - 2, 0), nt):
                copy(j, j % 2, j == 0).wait()

    row = lambda n: pl.BlockSpec((tm, n), lambda i: (i, 0))
    full = lambda a: pl.BlockSpec(a.shape, lambda i: (0, 0))
    dh_specs, dh_shapes, scratch = [row(D_MODEL)], [jax.ShapeDtypeStruct((t, D_MODEL), F32)], []
    if split_lead:
        dh_specs = [pl.BlockSpec((BLOCK, D_MODEL), lambda i: (0, 0)), pl.BlockSpec(memory_space=pl.ANY)]
        dh_shapes = [jax.ShapeDtypeStruct((BLOCK, D_MODEL), F32), jax.ShapeDtypeStruct((t - BLOCK, D_MODEL), F32)]
        scratch = [pltpu.VMEM((2, tm, D_MODEL), F32), pltpu.SemaphoreType.DMA((2,))]
    outs = pl.pallas_call(
        body, name="in_proj_bwd", grid=(nt,),
        in_specs=[row(ATTN_W), row(KV_W), row(KV_W), row(3 * CONV_W), full(w), row(D_MODEL), row(D_MODEL), full(g),
                  row(2 * HEAD_DIM), row(2 * HEAD_DIM), row(2 * HEAD_DIM)],
        out_specs=dh_specs + [row(IN_W), pl.BlockSpec((1, D_MODEL), lambda i: (0, 0))],
        out_shape=dh_shapes + [jax.ShapeDtypeStruct((t, IN_W), BF16), jax.ShapeDtypeStruct((1, D_MODEL), F32)],
        scratch_shapes=scratch,
        compiler_params=_params("arbitrary"),
    )(dq, dk, dv, dbch, w, dh1, h, g, *tabs)
    return (tuple(outs[:2]) if split_lead else outs[0],) + tuple(outs[-2:])


class _Tiles:
    def __init__(self, t):
        self.tm = _row_tile(t, 640)
        self.ts = self.tm
        self.tabs = _rope_tables(t)
        self.bias = _attn_bias()


def _mixer_fwd(h, p, tl):
    a, q, k, v, b, c, hc = _in_proj(h, p["mix_pre_g"], p["w_in"], tl.tabs, tl.ts)
    o = _attn_fwd(q, k, v, tl.bias, p["sinks"], tl.tm)
    return (h, a, q, k, v, b, c, hc, o)


def _out_fwd(mixed, p, tl, deps=()):
    h, a, q, k, v, b, c, hc, o = mixed
    h1, y, z = _mix_out(h, o, b, c, hc, p["conv_w"], p["attn_out_g"], p["conv_out_g"], p["w_out"], p["mix_post_g"],
                        tl.ts, deps)
    return h1, mixed + (h1, y, z)


def _mlp_fwd(h1, saved, p, tl, target=None):
    h2, a2, slope, f = _mlp(h1, p["mlp_pre_g"], p["w_up"], p["w_down"], p["mlp_post_g"], tl.tm, target)
    return h2, saved + (a2, slope, f)


def _mlp_part_bwd(dh, saved, p, tl, deps=()):
    h1, a2, slope, f = saved[9], saved[12], saved[13], saved[14]
    df, dup, dg2 = _mlp_bwd_hidden(dh, f, p["mlp_post_g"], slope, p["w_down"], tl.tm, deps)
    dh1, dg1 = _mlp_bwd_input(dup, p["w_up"], h1, p["mlp_pre_g"], dh, tl.tm)
    g = {"w_down": [d.reshape(N_CHIPS, FF_CHUNK, D_MODEL)
                    for d in _weight_grad(slope, df, "grad_w_down", x_is_slope=True)],
         "w_up": [d.reshape(N_CHIPS, D_MODEL, FF_CHUNK) for d in _weight_grad(a2, dup, "grad_w_up")],
         "mlp_post_g": dg2, "mlp_pre_g": dg1}
    return dh1, g


def _mix_out_part_bwd(dh1, saved, p, tl, deps=()):
    b, c, hc, o, y, z = saved[5], saved[6], saved[7], saved[8], saved[10], saved[11]
    dz, do, dbch, dgp, dga, dgc, dcw = _mix_out_bwd(dh1, z, p["mix_post_g"], p["w_out"], o, b, c, hc, p["conv_w"],
                                                    p["attn_out_g"], p["conv_out_g"], tl.ts, deps)
    g = {"w_out": [d.reshape(N_CHIPS, D_MODEL // N_CHIPS, D_MODEL) for d in _weight_grad(y, dz, "grad_w_out")],
         "mix_post_g": dgp, "attn_out_g": dga, "conv_out_g": dgc, "conv_w": dcw}
    return (dh1, do, dbch), g


def _attn_in_part_bwd(carry, saved, p, tl, deps=(), split_lead=False):
    dh1, do, dbch = carry
    h_in, a, q, k, v, o = saved[0], saved[1], saved[2], saved[3], saved[4], saved[8]
    dq, dk, dv, dsink = _attn_bwd(q, k, v, o, do, tl.bias, p["sinks"], tl.tm, deps)
    dh, dproj, dgi = _in_proj_bwd(dq, dk, dv, dbch, p["w_in"], dh1, h_in, p["mix_pre_g"], tl.tabs, tl.ts, split_lead)
    g_in = [d.reshape(N_CHIPS, IN_W // N_CHIPS, D_MODEL) for d in _weight_grad(dproj, a, "grad_w_in")]
    return dh, {"w_in": g_in, "mix_pre_g": dgi, "sinks": dsink[:, 0]}


def _place():
    return lax.axis_index("x"), lax.axis_index("y"), lax.axis_index("c")


def _other_chips(x, y):
    return [(1 - x, y), (x, 1 - y), (1 - x, 1 - y)]


_HBM = pl.BlockSpec(memory_space=pltpu.HBM)
_SEM = pl.BlockSpec(memory_space=pltpu.SEMAPHORE)
_EFFECT = pltpu.SideEffectType.DATAFLOW_SIDE_EFFECTING


class _Exchange:
    def __init__(self, name, bufs, plan, n, after=()):
        self.name, self.plan, nb = name, plan, len(bufs)
        n_in = nb + len(after)

        def body(*refs):
            send, recv, token = refs[n_in], refs[n_in + 1], refs[-1]
            for k, (src, dst, target, _) in enumerate(plan(refs[:nb])):
                pltpu.make_async_remote_copy(src_ref=src, dst_ref=dst, send_sem=send.at[k], recv_sem=recv.at[k],
                                             device_id=target, device_id_type=MESH).start()
            token[...] = jnp.zeros_like(token)

        outs = pl.pallas_call(
            body, name=name + "_start",
            out_shape=(pltpu.SemaphoreType.DMA((n,)), pltpu.SemaphoreType.DMA((n,)),
                       *[pltpu.HBM(b.shape, b.dtype) for b in bufs], jax.ShapeDtypeStruct((8, 128), F32)),
            in_specs=[_HBM] * nb + [pl.BlockSpec(memory_space=pl.ANY)] * len(after),
            out_specs=(_SEM, _SEM, *[_HBM] * nb, pl.BlockSpec(memory_space=pltpu.VMEM)),
            input_output_aliases={i: 2 + i for i in range(nb)},
            compiler_params=pltpu.CompilerParams(has_side_effects=_EFFECT),
        )(*[pltpu.with_memory_space_constraint(b, pltpu.HBM) for b in bufs], *after)
        self.send, self.recv, self.bufs, self.token = outs[0], outs[1], list(outs[2:2 + nb]), outs[-1]

    def wait(self, *after):
        plan, nb = self.plan, len(self.bufs)

        def body(*refs):
            send, recv = refs[nb], refs[nb + 1]
            for k, (src, _, target, land) in enumerate(plan(refs[:nb])):
                cp = pltpu.make_async_remote_copy(src_ref=src, dst_ref=land, send_sem=send.at[k], recv_sem=recv.at[k],
                                                  device_id=target, device_id_type=MESH)
                cp.wait_send()
                cp.wait_recv()

        outs = pl.pallas_call(
            body, name=self.name + "_wait", out_shape=[pltpu.HBM(b.shape, b.dtype) for b in self.bufs],
            in_specs=[_HBM] * nb + [_SEM, _SEM] + [pl.BlockSpec(memory_space=pl.ANY)] * len(after),
            out_specs=[_HBM] * nb, input_output_aliases={i: i for i in range(nb)},
            compiler_params=pltpu.CompilerParams(has_side_effects=_EFFECT),
        )(*self.bufs, self.send, self.recv, *after)
        return list(outs)


def _gather_plan(n):
    def plan(refs):
        x, y, c = _place()
        me = 2 * x + y
        return [(refs[a].at[me], refs[a].at[me], (px, py, c), refs[a].at[2 * px + py])
                for a in range(n) for px, py in _other_chips(x, y)]

    return plan


def _peers():
    x, y, c = _place()
    return [(k - 1, (x ^ (k >> 2), y ^ ((k >> 1) & 1), c ^ (k & 1))) for k in range(1, N_DEV)]


def _scatter_plan(n, half_rows):
    def plan(refs):
        out = []
        for a in range(n):
            hr = half_rows[a]
            for k, (px, py, pc) in _peers():
                out.append((refs[a].at[2 * px + py, pl.ds(pc * hr, hr)], refs[n + a].at[k], (px, py, pc),
                            refs[n + a].at[k]))
        return out

    return plan


def _join_plan(n):
    def plan(refs):
        x, y, c = _place()
        return [(refs[a].at[c], refs[a].at[c], (x, y, 1 - c), refs[a].at[1 - c]) for a in range(n)]

    return plan


def _sum_parts(g, q):
    rows, cols = g.shape[1], g.shape[2]
    hr = rows // 2
    tr = _block_rows(hr)
    per = hr // tr
    x, y, c = _place()
    where = jnp.stack([2 * x + y, c]).astype(jnp.int32)

    def body(where_ref, g_ref, q_ref, o_ref):
        total = g_ref[...]
        for k in range(N_DEV - 1):
            total = total + q_ref[k].astype(F32)
        o_ref[...] = total

    return pl.pallas_call(
        body, name="sum_parts",
        grid_spec=pltpu.PrefetchScalarGridSpec(
            num_scalar_prefetch=1, grid=(per,),
            in_specs=[pl.BlockSpec((None, tr, cols), lambda i, where_ref: (where_ref[0], where_ref[1] * per + i, 0)),
                      pl.BlockSpec((N_DEV - 1, tr, cols), lambda i, where_ref: (0, i, 0))],
            out_specs=pl.BlockSpec((None, tr, cols), lambda i, where_ref: (where_ref[1], i, 0))),
        out_shape=jax.ShapeDtypeStruct((2, hr, cols), F32),
        compiler_params=_params("parallel"),
    )(where, g, q)


def _all_plan(refs):
    x, y, c = _place()
    mine = refs[0].at[4 * x + 2 * y + c]
    return [(mine, mine, (px, py, pc), refs[0].at[4 * px + 2 * py + pc]) for _, (px, py, pc) in _peers()]


def _sum_devices(parts):
    def body(p_ref, o_ref):
        total = p_ref[0]
        for d in range(1, N_DEV):
            total = total + p_ref[d]
        o_ref[...] = total

    vm = pl.BlockSpec(memory_space=pltpu.VMEM)
    return pl.pallas_call(body, name="sum_devices", in_specs=[vm], out_specs=vm,
                          out_shape=jax.ShapeDtypeStruct(parts.shape[1:], F32))(parts)


def _adamw_math(w, g, m, v):
    m = ADAM_B1 * m + (1.0 - ADAM_B1) * g
    v = ADAM_B2 * v + (1.0 - ADAM_B2) * jnp.square(g)
    m_hat = m / (1.0 - ADAM_B1 ** ADAM_STEP)
    v_hat = v / (1.0 - ADAM_B2 ** ADAM_STEP)
    delta = -ADAM_LR * (m_hat / (jnp.sqrt(v_hat) + ADAM_EPS) + ADAM_WD * w)
    return delta, m, v


def _adamw_large(layer, w, halves, m, v, other):
    _, rows, cols = w.shape
    tr = _block_rows(rows // 2)
    per = rows // 2 // tr

    def body(w_ref, g_ref, m_ref, v_ref, *rest):
        g_out, d_ref, nm_ref, nv_ref = rest[-4:]
        g = g_ref[...]
        g_out[...] = g
        d_ref[...], nm_ref[...], nv_ref[...] = _adamw_math(w_ref[...], g, m_ref[...], v_ref[...])

    blk = pl.BlockSpec((None, tr, cols), lambda i: (layer, i, 0))
    half = pl.BlockSpec((None, tr, cols), lambda i: (i // per, i % per, 0))
    kept = [] if other is None else list(other)
    return pl.pallas_call(
        body, name="adamw_large", grid=(rows // tr,),
        in_specs=[blk, half, blk, blk] + [pl.BlockSpec(memory_space=pl.ANY)] * len(kept), out_specs=[blk] * 4,
        out_shape=[jax.ShapeDtypeStruct(w.shape, F32)] * 4,
        input_output_aliases={4 + k: k for k in range(len(kept))},
        compiler_params=_params("parallel"),
    )(w, halves, m, v, *kept)


def _adamw_small(ws, gs, ms, vs):
    n = len(ws)

    def body(*refs):
        w_r, g_r, m_r, v_r = refs[:n], refs[n:2 * n], refs[2 * n:3 * n], refs[3 * n:4 * n]
        d_r, nm_r, nv_r = refs[4 * n:5 * n], refs[5 * n:6 * n], refs[6 * n:]
        for a in range(n):
            d_r[a][...], nm_r[a][...], nv_r[a][...] = _adamw_math(w_r[a][...], g_r[a][...], m_r[a][...], v_r[a][...])

    vm = pl.BlockSpec(memory_space=pltpu.VMEM)
    outs = pl.pallas_call(
        body, name="adamw_small", in_specs=[vm] * (4 * n), out_specs=[vm] * (3 * n),
        out_shape=[jax.ShapeDtypeStruct(w.shape, F32) for w in ws] * 3,
    )(*ws, *gs, *ms, *vs)
    return outs[:n], outs[n:2 * n], outs[2 * n:]


_LARGE = ("w_in", "w_out", "w_up", "w_down")
_SMALL = ("meta_tokens", "mix_pre_g", "conv_w", "sinks", "attn_out_g", "conv_out_g", "mix_post_g", "mlp_pre_g",
          "mlp_post_g")
_ORDER = ("meta_tokens", "mix_pre_g", "w_in", "conv_w", "sinks", "attn_out_g", "conv_out_g", "w_out", "mix_post_g",
          "mlp_pre_g", "w_up", "w_down", "mlp_post_g")


class _Reduce:
    def __init__(self, name, grads, after=()):
        self.name, self.n = name, len(grads)
        self.own = [g for g, _ in grads]
        half_rows = [g.shape[1] // 2 for g in self.own]
        zones = [lax.empty((N_DEV - 1, hr, g.shape[2]), BF16) for g, hr in zip(self.own, half_rows)]
        self.exchange = _Exchange(name + "_scatter", [b for _, b in grads] + zones, _scatter_plan(self.n, half_rows),
                                  (N_DEV - 1) * self.n, after)

    @property
    def token(self):
        return self.exchange.token

    def join(self, *after):
        bufs = self.exchange.wait(*after)
        halves = [_sum_parts(g, q) for g, q in zip(self.own, bufs[self.n:])]
        self.exchange = _Exchange(self.name + "_join", halves, _join_plan(self.n), self.n)

    def done(self, *after):
        return self.exchange.wait(*after)


def _pad_cols(a, n=D_MODEL):
    return jnp.pad(a, ((0, 0), (0, n - a.shape[1])))


def kernel(x, meta_tokens, mix_pre_g, w_in, conv_w, sinks, attn_out_g, conv_out_g, w_out, mix_post_g, mlp_pre_g, w_up, w_down, mlp_post_g, loss_target, m_meta_tokens, m_mix_pre_g, m_w_in, m_conv_w, m_sinks, m_attn_out_g, m_conv_out_g, m_w_out, m_mix_post_g, m_mlp_pre_g, m_w_up, m_w_down, m_mlp_post_g, v_meta_tokens, v_mix_pre_g, v_w_in, v_conv_w, v_sinks, v_attn_out_g, v_conv_out_g, v_w_out, v_mix_post_g, v_mlp_pre_g, v_w_up, v_w_down, v_mlp_post_g):
    w = dict(meta_tokens=meta_tokens, mix_pre_g=mix_pre_g, w_in=w_in, conv_w=conv_w, sinks=sinks,
             attn_out_g=attn_out_g, conv_out_g=conv_out_g, w_out=w_out, mix_post_g=mix_post_g, mlp_pre_g=mlp_pre_g,
             w_up=w_up, w_down=w_down, mlp_post_g=mlp_post_g)
    m = dict(meta_tokens=m_meta_tokens, mix_pre_g=m_mix_pre_g, w_in=m_w_in, conv_w=m_conv_w, sinks=m_sinks,
             attn_out_g=m_attn_out_g, conv_out_g=m_conv_out_g, w_out=m_w_out, mix_post_g=m_mix_post_g,
             mlp_pre_g=m_mlp_pre_g, w_up=m_w_up, w_down=m_w_down, mlp_post_g=m_mlp_post_g)
    v = dict(meta_tokens=v_meta_tokens, mix_pre_g=v_mix_pre_g, w_in=v_w_in, conv_w=v_conv_w, sinks=v_sinks,
             attn_out_g=v_attn_out_g, conv_out_g=v_conv_out_g, w_out=v_w_out, mix_post_g=v_mix_post_g,
             mlp_pre_g=v_mlp_pre_g, w_up=v_w_up, w_down=v_w_down, mlp_post_g=v_mlp_post_g)
    chip = 2 * lax.axis_index("x") + lax.axis_index("y")
    tl = _Tiles(x.shape[1] + BLOCK)

    def zone(quarter):
        return lax.dynamic_update_slice(lax.empty((N_CHIPS,) + quarter.shape, quarter.dtype), quarter[None],
                                        (chip,) + (0,) * quarter.ndim)

    w, m, v = ({**d, "w_in": jnp.swapaxes(d["w_in"], 1, 2)} for d in (w, m, v))
    zones = {n: [zone(w[n][l].astype(BF16)) for l in range(DEPTH)] for n in _LARGE}
    first = _Exchange("gather_first", [zones["w_in"][0], zone(w["conv_w"]), zone(w["meta_tokens"])], _gather_plan(3), 9)
    out0 = _Exchange("gather_out", [zones["w_out"][0]], _gather_plan(1), 3, [first.token])
    rest = _Exchange("gather_rest", [zones[n][0] for n in ("w_up", "w_down")], _gather_plan(2), 6, [out0.token])

    def whole_in(quarters):
        return quarters.reshape(IN_W, D_MODEL)

    h = jnp.concatenate([jnp.zeros((BLOCK, D_MODEL), F32), x[0]], axis=0)
    q_in, q_conv, q_meta = first.wait(rest.token, *tl.tabs, tl.bias, h)
    conv_whole = jnp.transpose(q_conv, (1, 2, 0, 3)).reshape(DEPTH, CONV_K, CONV_W)
    meta = jnp.transpose(q_meta, (1, 0, 2)).reshape(N_META, D_MODEL)
    p = [{"conv_w": conv_whole[l], "sinks": w["sinks"][l]} for l in range(DEPTH)]
    for l in range(DEPTH):
        for n in ("mix_pre_g", "attn_out_g", "conv_out_g", "mix_post_g", "mlp_pre_g", "mlp_post_g"):
            p[l][n] = w[n][l][None, :]

    h = lax.dynamic_update_slice(h, meta, (LEAD_PAD, 0))
    p[0]["w_in"] = whole_in(q_in)
    mixed = _mixer_fwd(h, p[0], tl)
    second = _Exchange("gather_second", [zones["w_in"][1], zones["w_out"][1]], _gather_plan(2), 6, [mixed[-1]])
    second_mlp = _Exchange("gather_second_mlp", [zones["w_up"][1], zones["w_down"][1]], _gather_plan(2), 6,
                           [second.token])
    p[0]["w_out"], = out0.wait(second_mlp.token)
    h1, saved0 = _out_fwd(mixed, p[0], tl)
    p[0]["w_up"], p[0]["w_down"] = rest.wait(h1)
    h, saved0 = _mlp_fwd(h1, saved0, p[0], tl)
    q_in, p[1]["w_out"] = second.wait(h)
    p[1]["w_in"] = whole_in(q_in)
    h1, saved1 = _out_fwd(_mixer_fwd(h, p[1], tl), p[1], tl)
    p[1]["w_up"], p[1]["w_down"] = second_mlp.wait(h1)
    (loss_tile, dh), saved1 = _mlp_fwd(h1, saved1, p[1], tl, loss_target[0])

    def adamw(layer, halves, other):
        return {n: _adamw_large(layer, w[n], halves[n], m[n], v[n], None if other is None else other[n])
                for n in halves}

    dh1, g1 = _mlp_part_bwd(dh, saved1, p[1], tl)
    carry, gm = _mix_out_part_bwd(dh1, saved1, p[1], tl)
    dh, gi = _attn_in_part_bwd(carry, saved1, p[1], tl)
    g1.update(gm, **gi)
    red1 = _Reduce("reduce1", [g1[n] for n in _LARGE])
    dh1, g0 = _mlp_part_bwd(dh, saved0, p[0], tl, [red1.token])
    red1.join(g0["w_down"][0])
    carry, gm = _mix_out_part_bwd(dh1, saved0, p[0], tl, [red1.token])
    first0 = ("w_up", "w_down", "w_out")
    g0.update(gm)
    red0a = _Reduce("reduce0a", [g0[n] for n in first0])
    (dlead, dseq), gi = _attn_in_part_bwd(carry, saved0, p[0], tl, [red0a.token], split_lead=True)
    g0.update(gi)
    red0b = _Reduce("reduce0b", [g0["w_in"]])
    grad_x = dseq[None]
    grads = {n: [g0[n], g1[n]] for n in g0 if n not in _LARGE}

    rows = [dlead[LEAD_PAD:]]
    for n in ("mix_pre_g", "mix_post_g", "mlp_pre_g", "mlp_post_g"):
        rows += grads[n]
    rows += [jnp.concatenate([grads["attn_out_g"][l], grads["conv_out_g"][l]], axis=1) for l in range(DEPTH)]
    rows.append(jnp.concatenate(grads["conv_w"], axis=1))
    rows.append(_pad_cols(jnp.concatenate(grads["sinks"])[None, :]))
    rows.append(_pad_cols(loss_tile[:1]))
    packed = jnp.concatenate(rows, axis=0)
    packed = jnp.pad(packed, ((0, SMALL_ROWS - packed.shape[0]), (0, 0)))
    device = 2 * chip + lax.axis_index("c")
    small_parts = _Exchange("gather_small", [lax.dynamic_update_slice(lax.empty((N_DEV,) + packed.shape, F32),
                                                                      packed[None], (device, 0, 0))], _all_plan, N_DEV - 1)
    done1 = adamw(1, dict(zip(_LARGE, red1.done(small_parts.token, red0b.token))), None)
    total = _sum_devices(small_parts.wait(*[done1[n][0] for n in _LARGE])[0])
    r0 = N_META
    small = {
        "meta_tokens": lax.dynamic_slice(total[:N_META], (0, chip * (D_MODEL // N_CHIPS)), (N_META, D_MODEL // N_CHIPS)),
        "mix_pre_g": total[r0:r0 + 2], "mix_post_g": total[r0 + 2:r0 + 4], "mlp_pre_g": total[r0 + 4:r0 + 6],
        "mlp_post_g": total[r0 + 6:r0 + 8],
        "attn_out_g": total[r0 + 8:r0 + 10, :ATTN_W], "conv_out_g": total[r0 + 8:r0 + 10, ATTN_W:],
        "conv_w": lax.dynamic_slice(total[r0 + 10:r0 + 13].reshape(CONV_K, DEPTH, CONV_W).transpose(1, 0, 2),
                                    (0, 0, chip * (CONV_W // N_CHIPS)), (DEPTH, CONV_K, CONV_W // N_CHIPS)),
        "sinks": total[r0 + 13, :DEPTH * N_Q_HEADS].reshape(DEPTH, N_Q_HEADS),
    }
    loss = total[r0 + 14, 0]

    ds, nms, nvs = _adamw_small([w[n] for n in _SMALL], [small[n] for n in _SMALL], [m[n] for n in _SMALL],
                                [v[n] for n in _SMALL])
    red0a.join(ds[0], grad_x)
    red0b.join(red0a.token)
    done0 = adamw(0, dict(zip(first0, red0a.done(red0b.token))), done1)
    done0.update(adamw(0, {"w_in": red0b.done(done0["w_down"][0])[0]}, done1))
    grad, delta, new_m, new_v = {}, {}, {}, {}
    for n in _LARGE:
        grad[n], delta[n], new_m[n], new_v[n] = done0[n]
    for d in (grad, delta, new_m, new_v):
        d["w_in"] = jnp.swapaxes(d["w_in"], 1, 2)
    for i, n in enumerate(_SMALL):
        grad[n], delta[n], new_m[n], new_v[n] = small[n], ds[i], nms[i], nvs[i]
    return (loss, grad_x, *[grad[n] for n in _ORDER], *[delta[n] for n in _ORDER], *[new_m[n] for n in _ORDER],
            *[new_v[n] for n in _ORDER])
```

```python
import functools

import jax
import jax.numpy as jnp
from jax import lax
from jax.experimental import pallas as pl
from jax.experimental.pallas import tpu as pltpu

F32 = jnp.float32
BF16 = jnp.bfloat16

D_MODEL = 1024
DEPTH = 2
N_META = 16
ATTN_W = 512
CONV_W = 512
HEAD_DIM = 64
N_Q_HEADS = 8
N_KV_HEADS = 2
GROUP = N_Q_HEADS // N_KV_HEADS
KV_W = N_KV_HEADS * HEAD_DIM
CONV_K = 3
BLOCK = 128
LEAD_PAD = BLOCK - N_META
ROPE_THETA = 500000.0
ROT_DIM = HEAD_DIM // 4
ROT_HALF = ROT_DIM // 2
D_FF = 4 * D_MODEL
IN_W = ATTN_W + 2 * KV_W + 3 * CONV_W
QKV_W = ATTN_W + 2 * KV_W
EPS = 1e-6
SCALE = HEAD_DIM ** -0.5
FF_CHUNK = 1024
N_CHIPS = 4
N_DEV = 8

ADAM_LR = 0.001
ADAM_B1 = 0.9
ADAM_B2 = 0.999
ADAM_EPS = 1e-08
ADAM_WD = 0.01
ADAM_STEP = 10

V7X_VMEM_LIMIT = 60 * 1024 * 1024
SMALL_ROWS = 32

MESH = pl.DeviceIdType.MESH


def _params(*sem):
    return pltpu.CompilerParams(dimension_semantics=sem, vmem_limit_bytes=V7X_VMEM_LIMIT)


def _block_rows(n):
    return max(r for r in range(16, min(n, 256) + 1, 16) if n % r == 0)


def _row_tile(t, most):
    nb = t // BLOCK
    for b in range(most // BLOCK, 0, -1):
        if nb % b == 0:
            return b * BLOCK
    return BLOCK


def _behind(body, deps):
    n = len(deps)

    def wrapped(*refs):
        body(*refs[n:])

    return wrapped, [pl.BlockSpec(memory_space=pl.ANY)] * n


def _rms(x, g):
    r = lax.rsqrt(jnp.mean(x * x, axis=-1, keepdims=True) + EPS)
    return x * r * g


def _rms_bwd(dy, x, g):
    r = lax.rsqrt(jnp.mean(x * x, axis=-1, keepdims=True) + EPS)
    xh = x * r
    dg = jnp.sum(dy * xh, axis=0, keepdims=True)
    dxh = dy * g
    dx = r * (dxh - xh * jnp.mean(dxh * xh, axis=-1, keepdims=True))
    return dx, dg


def _rope(x, cos, sa, sb):
    n = x.shape[-1]
    return x * cos + pltpu.roll(x, n - ROT_HALF, 1) * sa + pltpu.roll(x, ROT_HALF, 1) * sb


def _rope_bwd(dy, cos, sa, sb):
    n = dy.shape[-1]
    return dy * cos + pltpu.roll(dy * sa, ROT_HALF, 1) + pltpu.roll(dy * sb, n - ROT_HALF, 1)


def _rope_tables(t):
    pos = lax.broadcasted_iota(jnp.int32, (t, ROT_HALF), 0).astype(F32) - LEAD_PAD
    pair = lax.broadcasted_iota(jnp.int32, (t, ROT_HALF), 1).astype(F32)
    inv_freq = jnp.power(jnp.float32(ROPE_THETA), -(2.0 * pair) / ROT_DIM)
    ang = pos * inv_freq
    cos, sin = lax.optimization_barrier((jnp.cos(ang), jnp.sin(ang)))
    spread = (1, 2 * HEAD_DIM // ROT_HALF)
    cos, sin = jnp.tile(cos, spread), jnp.tile(sin, spread)
    dim = lax.broadcasted_iota(jnp.int32, (t, 2 * HEAD_DIM), 1) % HEAD_DIM
    return (jnp.where(dim < ROT_DIM, cos, 1.0), jnp.where(dim < ROT_HALF, -sin, 0.0),
            jnp.where((dim >= ROT_HALF) & (dim < ROT_DIM), sin, 0.0))


def _in_proj(h, g, w, tabs, tm):
    t = h.shape[0]

    def body(h_ref, g_ref, w_ref, c_ref, sa_ref, sb_ref, a_ref, q_ref, k_ref, v_ref, b_ref, cg_ref, hc_ref):
        a = _rms(h_ref[...], g_ref[...]).astype(BF16)
        a_ref[...] = a
        p = lax.dot_general(a, w_ref[...], (((1,), (1,)), ((), ())), preferred_element_type=F32)
        cos, sa, sb = c_ref[...], sa_ref[...], sb_ref[...]
        rep = ATTN_W // (2 * HEAD_DIM)
        q = _rope(p[:, :ATTN_W], jnp.tile(cos, (1, rep)), jnp.tile(sa, (1, rep)), jnp.tile(sb, (1, rep)))
        q_ref[...] = (q * SCALE).astype(BF16)
        k_ref[...] = _rope(p[:, ATTN_W:ATTN_W + KV_W], cos, sa, sb).astype(BF16)
        v_ref[...] = p[:, ATTN_W + KV_W:QKV_W].astype(BF16)
        b_ref[...] = p[:, QKV_W:QKV_W + CONV_W].astype(BF16)
        cg_ref[...] = p[:, QKV_W + CONV_W:QKV_W + 2 * CONV_W].astype(BF16)
        hc_ref[...] = p[:, QKV_W + 2 * CONV_W:].astype(BF16)

    row = lambda n: pl.BlockSpec((tm, n), lambda i: (i, 0))
    full = lambda a: pl.BlockSpec(a.shape, lambda i: (0, 0))
    return pl.pallas_call(
        body, name="in_proj", grid=(t // tm,),
        in_specs=[row(D_MODEL), full(g), full(w), row(2 * HEAD_DIM), row(2 * HEAD_DIM), row(2 * HEAD_DIM)],
        out_specs=[row(D_MODEL), row(ATTN_W), row(KV_W), row(KV_W), row(CONV_W), row(CONV_W), row(CONV_W)],
        out_shape=[jax.ShapeDtypeStruct((t, D_MODEL), BF16), jax.ShapeDtypeStruct((t, ATTN_W), BF16),
                   jax.ShapeDtypeStruct((t, KV_W), BF16), jax.ShapeDtypeStruct((t, KV_W), BF16),
                   jax.ShapeDtypeStruct((t, CONV_W), BF16), jax.ShapeDtypeStruct((t, CONV_W), BF16),
                   jax.ShapeDtypeStruct((t, CONV_W), BF16)],
        compiler_params=_params("parallel"),
    )(h, g, w, *tabs)


def _attn_bias():
    r = lax.broadcasted_iota(jnp.int32, (3, BLOCK, 2 * BLOCK), 1)
    c = lax.broadcasted_iota(jnp.int32, (3, BLOCK, 2 * BLOCK), 2)
    i = lax.broadcasted_iota(jnp.int32, (3, BLOCK, 2 * BLOCK), 0)
    ok = (c > r) & (c <= r + BLOCK) & (c + (i - 1) * BLOCK >= LEAD_PAD)
    return jnp.where(ok, 0.0, -jnp.inf).astype(F32)


def _attn_scores(qh, kg, bias):
    return lax.dot_general(qh, kg, (((1,), (1,)), ((), ())), preferred_element_type=F32) + bias


def _attn_probs(s, sk):
    m = jnp.maximum(jnp.max(s, axis=-1, keepdims=True), sk)
    e = jnp.exp(s - m)
    es = jnp.exp(sk - m)
    rden = 1.0 / (jnp.sum(e, axis=-1, keepdims=True) + es)
    return e * rden, es * rden


def _head(hh):
    return slice(hh * HEAD_DIM, (hh + 1) * HEAD_DIM)


def _two_blocks(ref, i):
    prev = jnp.maximum(i - 1, 0)
    return jnp.concatenate([ref[pl.ds(pl.multiple_of(prev * BLOCK, BLOCK), BLOCK), :],
                            ref[pl.ds(pl.multiple_of(i * BLOCK, BLOCK), BLOCK), :]], axis=0)


def _attn_fwd(q, k, v, bias, sinks, tm):
    t = q.shape[0]
    per_step = tm // BLOCK
    heads = range(N_Q_HEADS)

    def body(s_ref, q_ref, k_ref, v_ref, bias_ref, o_ref):
        for b in range(per_step):
            i = pl.program_id(0) * per_step + b
            rows = slice(b * BLOCK, (b + 1) * BLOCK)
            kc, vc = _two_blocks(k_ref, i), _two_blocks(v_ref, i)
            bias_i = bias_ref[jnp.minimum(i, 2)]
            scores = [_attn_scores(q_ref[rows, _head(hh)], kc[:, _head(hh // GROUP)], bias_i) for hh in heads]
            probs = [_attn_probs(scores[hh], s_ref[hh])[0].astype(BF16) for hh in heads]
            for hh in heads:
                o_ref[rows, _head(hh)] = jnp.dot(probs[hh], vc[:, _head(hh // GROUP)],
                                                 preferred_element_type=F32).astype(BF16)

    whole = pl.BlockSpec((t, KV_W), lambda i: (0, 0))
    return pl.pallas_call(
        body, name="attn_fwd", grid=(t // tm,),
        in_specs=[pl.BlockSpec(memory_space=pltpu.SMEM), pl.BlockSpec((tm, ATTN_W), lambda i: (i, 0)), whole, whole,
                  pl.BlockSpec(bias.shape, lambda i: (0, 0, 0))],
        out_specs=pl.BlockSpec((tm, ATTN_W), lambda i: (i, 0)),
        out_shape=jax.ShapeDtypeStruct((t, ATTN_W), BF16),
        compiler_params=_params("parallel"),
    )(sinks, q, k, v, bias)


def _shift_rows(u, halo, n):
    r = pltpu.roll(u, n, 0)
    hr = pltpu.roll(halo, n, 0)
    idx = lax.broadcasted_iota(jnp.int32, hr.shape, 0)
    return jnp.concatenate([jnp.where(idx < n, hr, r[:8]), r[8:]], axis=0)


def _advance_rows(u, halo, n):
    rows = u.shape[0]
    r = pltpu.roll(u, rows - n, 0)
    hr = pltpu.roll(halo, 8 - n, 0)
    idx = lax.broadcasted_iota(jnp.int32, hr.shape, 0)
    return jnp.concatenate([r[:rows - 8], jnp.where(idx >= 8 - n, hr, r[rows - 8:])], axis=0)


def _mix_out(h, o, b, c, hc, cw, ga, gc, w, gp, tm, deps=()):
    t = h.shape[0]

    def body(h_ref, o_ref, b_ref, c_ref, hc_ref, cw_ref, ga_ref, gc_ref, w_ref, gp_ref, h1_ref, y_ref, z_ref, halo):
        @pl.when(pl.program_id(0) == 0)
        def _():
            halo[...] = jnp.zeros_like(halo)

        u = c_ref[...].astype(F32) * hc_ref[...].astype(F32)
        cv = cw_ref[0:1, :] * _shift_rows(u, halo[...], 2) + cw_ref[1:2, :] * _shift_rows(u, halo[...], 1) \
            + cw_ref[2:3, :] * u
        halo[...] = u[tm - 8:]
        yc = b_ref[...].astype(F32) * cv
        y = jnp.concatenate([_rms(o_ref[...].astype(F32), ga_ref[...]), _rms(yc, gc_ref[...])], axis=1).astype(BF16)
        y_ref[...] = y
        z = jnp.dot(y, w_ref[...].reshape(D_MODEL, D_MODEL), preferred_element_type=F32)
        z_ref[...] = z
        h1_ref[...] = h_ref[...] + _rms(z, gp_ref[...])

    row = lambda n: pl.BlockSpec((tm, n), lambda i: (i, 0))
    full = lambda a: pl.BlockSpec(a.shape, lambda i: (0,) * a.ndim)
    body, dep_specs = _behind(body, deps)
    return pl.pallas_call(
        body, name="mix_out", grid=(t // tm,),
        in_specs=dep_specs + [row(D_MODEL), row(ATTN_W), row(CONV_W), row(CONV_W), row(CONV_W), full(cw), full(ga),
                              full(gc), full(w), full(gp)],
        out_specs=[row(D_MODEL), row(D_MODEL), row(D_MODEL)],
        out_shape=[jax.ShapeDtypeStruct((t, D_MODEL), F32), jax.ShapeDtypeStruct((t, D_MODEL), BF16),
                   jax.ShapeDtypeStruct((t, D_MODEL), F32)],
        scratch_shapes=[pltpu.VMEM((8, CONV_W), F32)],
        compiler_params=_params("arbitrary"),
    )(*deps, h, o, b, c, hc, cw, ga, gc, w, gp)


def _mlp(h1, g1, wu, wd, g2, tm, target=None):
    t = h1.shape[0]
    nj = D_FF // FF_CHUNK
    per_step = tm // BLOCK if target is not None else 0

    def body(h1_ref, g1_ref, wu_ref, wd_ref, g2_ref, *rest):
        t_refs, outs = rest[:per_step], rest[per_step:]
        a2_ref, slope_ref, f_ref = outs[-3:]
        a2 = _rms(h1_ref[...], g1_ref[...]).astype(BF16)
        a2_ref[...] = a2
        f = None
        for j in range(nj):
            up = jnp.dot(a2, wu_ref[j], preferred_element_type=F32)
            r = jnp.maximum(up, 0.0)
            slope_ref[:, j * FF_CHUNK:(j + 1) * FF_CHUNK] = (r + r).astype(BF16)
            part = jnp.dot((r * r).astype(BF16), wd_ref[j], preferred_element_type=F32)
            f = part if f is None else f + part
        f_ref[...] = f
        h2 = h1_ref[...] + _rms(f, g2_ref[...])
        if target is None:
            outs[0][...] = h2
            return
        loss_ref, dh_ref = outs[:2]
        i = pl.program_id(0)

        @pl.when(i == 0)
        def _():
            loss_ref[...] = jnp.zeros_like(loss_ref)

        total = jnp.zeros((), F32)
        for b in range(per_step):
            rows = slice(b * BLOCK, (b + 1) * BLOCK)
            err = h2[rows] - t_refs[b][...]
            if b == 0:
                err = jnp.where(i == 0, 0.0, err)
            dh_ref[rows, :] = err * (1.0 / D_MODEL)
            total = total + jnp.sum(err * err)
        loss_ref[...] += total * (0.5 / D_MODEL)

    def target_block(b):
        return pl.BlockSpec((BLOCK, D_MODEL), lambda i: (jnp.maximum(i * per_step + b - 1, 0), 0))

    row = pl.BlockSpec((tm, D_MODEL), lambda i: (i, 0))
    vec = pl.BlockSpec((1, D_MODEL), lambda i: (0, 0))
    resident = pl.BlockSpec(memory_space=pltpu.VMEM)
    first_specs, first_shapes = [row], [jax.ShapeDtypeStruct((t, D_MODEL), F32)]
    if target is not None:
        first_specs = [pl.BlockSpec((8, 128), lambda i: (0, 0)), row]
        first_shapes = [jax.ShapeDtypeStruct((8, 128), F32), jax.ShapeDtypeStruct((t, D_MODEL), F32)]
    outs = pl.pallas_call(
        body, name="mlp", grid=(t // tm,),
        in_specs=[row, vec, resident, resident, vec] + [target_block(b) for b in range(per_step)],
        out_specs=first_specs + [row, pl.BlockSpec((tm, D_FF), lambda i: (i, 0)), row],
        out_shape=first_shapes + [jax.ShapeDtypeStruct((t, D_MODEL), BF16), jax.ShapeDtypeStruct((t, D_FF), BF16),
                                  jax.ShapeDtypeStruct((t, D_MODEL), F32)],
        compiler_params=_params("parallel" if target is None else "arbitrary"),
    )(h1, g1, wu, wd, g2, *([target] * per_step))
    return (outs[0] if target is None else tuple(outs[:2]),) + tuple(outs[-3:])


def _mlp_bwd_hidden(dh2, f, g2, slope, wd, tm, deps=()):
    t = dh2.shape[0]
    nj = D_FF // FF_CHUNK

    def body(dh2_ref, f_ref, g2_ref, slope_ref, wd_ref, df_ref, dup_ref, dg2_ref):
        @pl.when(pl.program_id(0) == 0)
        def _():
            dg2_ref[...] = jnp.zeros_like(dg2_ref)

        df, dg = _rms_bwd(dh2_ref[...], f_ref[...], g2_ref[...])
        dg2_ref[...] += dg
        df = df.astype(BF16)
        df_ref[...] = df
        for j in range(nj):
            cols = slice(j * FF_CHUNK, (j + 1) * FF_CHUNK)
            dact = lax.dot_general(df, wd_ref[j], (((1,), (1,)), ((), ())), preferred_element_type=F32)
            dup_ref[:, cols] = (dact * slope_ref[:, cols].astype(F32)).astype(BF16)

    row = pl.BlockSpec((tm, D_MODEL), lambda i: (i, 0))
    wide = pl.BlockSpec((tm, D_FF), lambda i: (i, 0))
    vec = pl.BlockSpec((1, D_MODEL), lambda i: (0, 0))
    body, dep_specs = _behind(body, deps)
    return pl.pallas_call(
        body, name="mlp_bwd_hidden", grid=(t // tm,),
        in_specs=dep_specs + [row, row, vec, wide, pl.BlockSpec(memory_space=pltpu.VMEM)],
        out_specs=[row, wide, vec],
        out_shape=[jax.ShapeDtypeStruct((t, D_MODEL), BF16), jax.ShapeDtypeStruct((t, D_FF), BF16),
                   jax.ShapeDtypeStruct((1, D_MODEL), F32)],
        compiler_params=_params("arbitrary"),
    )(*deps, dh2, f, g2, slope, wd)


def _mlp_bwd_input(dup, wu, h1, g1, dh2, tm):
    t = dh2.shape[0]
    nj = D_FF // FF_CHUNK

    def body(dup_ref, wu_ref, h1_ref, g1_ref, dh2_ref, dh1_ref, dg1_ref):
        @pl.when(pl.program_id(0) == 0)
        def _():
            dg1_ref[...] = jnp.zeros_like(dg1_ref)

        da2 = None
        for j in range(nj):
            part = lax.dot_general(dup_ref[:, j * FF_CHUNK:(j + 1) * FF_CHUNK], wu_ref[j], (((1,), (1,)), ((), ())),
                                   preferred_element_type=F32)
            da2 = part if da2 is None else da2 + part
        dx, dg = _rms_bwd(da2, h1_ref[...], g1_ref[...])
        dh1_ref[...] = dh2_ref[...] + dx
        dg1_ref[...] += dg

    row = pl.BlockSpec((tm, D_MODEL), lambda i: (i, 0))
    vec = pl.BlockSpec((1, D_MODEL), lambda i: (0, 0))
    return pl.pallas_call(
        body, name="mlp_bwd_input", grid=(t // tm,),
        in_specs=[pl.BlockSpec((tm, D_FF), lambda i: (i, 0)), pl.BlockSpec(memory_space=pltpu.VMEM), row, vec, row],
        out_specs=[row, vec],
        out_shape=[jax.ShapeDtypeStruct((t, D_MODEL), F32), jax.ShapeDtypeStruct((1, D_MODEL), F32)],
        compiler_params=_params("arbitrary"),
    )(dup, wu, h1, g1, dh2)


def _row_split(t):
    tile = min(t, 1024)
    return tile, t // tile, t % tile


def _row_split_specs(t, cols):
    tile, whole, rest = _row_split(t)
    specs = [pl.BlockSpec((tile, cols), lambda r: (jnp.minimum(r, whole - 1), 0))]
    if rest:
        specs.append(pl.BlockSpec((rest, cols), lambda r: (whole * tile // rest, 0)))
    return specs


def _weight_grad(x, y, name, x_is_slope=False, deps=()):
    t, k = x.shape
    n = y.shape[1]
    tn = FF_CHUNK
    tk = FF_CHUNK if k % FF_CHUNK == 0 else k
    _, whole, rest = _row_split(t)
    steps = whole + bool(rest)

    def body(*refs):
        o_ref, ob_ref, r = refs[-2], refs[-1], pl.program_id(0)

        @pl.when(r == 0)
        def _():
            o_ref[...] = jnp.zeros_like(o_ref)

        def add(x_ref, y_ref):
            for a in range(k // tk):
                xv = x_ref[:, a * tk:(a + 1) * tk]
                if x_is_slope:
                    xv = xv.astype(F32)
                    xv = (xv * xv * 0.25).astype(BF16)
                for b in range(n // tn):
                    o_ref[a, b] += lax.dot_general(xv, y_ref[:, b * tn:(b + 1) * tn], (((0,), (0,)), ((), ())),
                                                   preferred_element_type=F32)

        if rest:
            pl.when(r < whole)(lambda: add(refs[0], refs[2]))
            pl.when(r == whole)(lambda: add(refs[1], refs[3]))
        else:
            add(refs[0], refs[1])

        @pl.when(r == steps - 1)
        def _():
            ob_ref[...] = o_ref[...].astype(BF16)

    vm = pl.BlockSpec(memory_space=pltpu.VMEM)
    body, dep_specs = _behind(body, deps)
    return pl.pallas_call(
        body, name=name, grid=(steps,),
        in_specs=dep_specs + _row_split_specs(t, k) + _row_split_specs(t, n), out_specs=[vm, vm],
        out_shape=[jax.ShapeDtypeStruct((k // tk, n // tn, tk, tn), F32),
                   jax.ShapeDtypeStruct((k // tk, n // tn, tk, tn), BF16)],
        compiler_params=_params("arbitrary"),
    )(*deps, *([x] * (1 + bool(rest))), *([y] * (1 + bool(rest))))


def _mix_out_bwd(dh1, z, gp, w, o, b, c, hc, cw, ga, gc, tm, deps=()):
    t = dh1.shape[0]
    nt = t // tm
    per16 = tm // 16

    def body(dh1_ref, z_ref, gp_ref, w_ref, o_ref, b_ref, c_ref, hc_ref, cp_ref, hp_ref, cw_ref, ga_ref, gc_ref,
             dz_ref, do_ref, dbch_ref, dgp_ref, dga_ref, dgc_ref, dcw_ref, halo):
        i = pl.program_id(0)

        @pl.when(i == 0)
        def _():
            halo[...] = jnp.zeros_like(halo)
            dgp_ref[...] = jnp.zeros_like(dgp_ref)
            dga_ref[...] = jnp.zeros_like(dga_ref)
            dgc_ref[...] = jnp.zeros_like(dgc_ref)
            dcw_ref[...] = jnp.zeros_like(dcw_ref)

        dz, dgp = _rms_bwd(dh1_ref[...], z_ref[...], gp_ref[...])
        dgp_ref[...] += dgp
        dz = dz.astype(BF16)
        dz_ref[...] = dz
        dy = lax.dot_general(dz, w_ref[...].reshape(D_MODEL, D_MODEL), (((1,), (1,)), ((), ())),
                             preferred_element_type=F32)
        do, dga = _rms_bwd(dy[:, :ATTN_W], o_ref[...].astype(F32), ga_ref[...])
        do_ref[...] = do.astype(BF16)
        dga_ref[...] += dga

        cc, hh = c_ref[...].astype(F32), hc_ref[...].astype(F32)
        u = cc * hh
        first = i == nt - 1
        u_before = jnp.where(first, 0.0, (cp_ref[...].astype(F32) * hp_ref[...].astype(F32))[8:])
        u1 = _shift_rows(u, u_before, 1)
        u2 = _shift_rows(u, u_before, 2)
        cv = cw_ref[0:1, :] * u2 + cw_ref[1:2, :] * u1 + cw_ref[2:3, :] * u
        bb = b_ref[...].astype(F32)
        dyc, dgc = _rms_bwd(dy[:, ATTN_W:], bb * cv, gc_ref[...])
        dgc_ref[...] += dgc
        dcv = dyc * bb
        d1 = _advance_rows(dcv, halo[...], 1)
        d2 = _advance_rows(dcv, halo[...], 2)
        halo[...] = dcv[:8]
        du = cw_ref[2:3, :] * dcv + cw_ref[1:2, :] * d1 + cw_ref[0:1, :] * d2
        dbch_ref[...] = jnp.concatenate([dyc * cv, du * hh, du * cc], axis=1).astype(BF16)
        dcw_ref[...] += jnp.concatenate([jnp.sum(dcv * u2, axis=0, keepdims=True),
                                         jnp.sum(dcv * u1, axis=0, keepdims=True),
                                         jnp.sum(dcv * u, axis=0, keepdims=True)], axis=0)

    row = lambda n: pl.BlockSpec((tm, n), lambda i: (nt - 1 - i, 0))
    before = pl.BlockSpec((16, CONV_W), lambda i: (jnp.maximum((nt - 1 - i) * per16 - 1, 0), 0))
    full = lambda a: pl.BlockSpec(a.shape, lambda i: (0,) * a.ndim)
    vec = lambda n: pl.BlockSpec((1, n), lambda i: (0, 0))
    body, dep_specs = _behind(body, deps)
    return pl.pallas_call(
        body, name="mix_out_bwd", grid=(nt,),
        in_specs=dep_specs + [row(D_MODEL), row(D_MODEL), full(gp), full(w), row(ATTN_W), row(CONV_W), row(CONV_W),
                              row(CONV_W), before, before, full(cw), full(ga), full(gc)],
        out_specs=[row(D_MODEL), row(ATTN_W), row(3 * CONV_W), vec(D_MODEL), vec(ATTN_W), vec(CONV_W),
                   pl.BlockSpec((CONV_K, CONV_W), lambda i: (0, 0))],
        out_shape=[jax.ShapeDtypeStruct((t, D_MODEL), BF16), jax.ShapeDtypeStruct((t, ATTN_W), BF16),
                   jax.ShapeDtypeStruct((t, 3 * CONV_W), BF16), jax.ShapeDtypeStruct((1, D_MODEL), F32),
                   jax.ShapeDtypeStruct((1, ATTN_W), F32), jax.ShapeDtypeStruct((1, CONV_W), F32),
                   jax.ShapeDtypeStruct((CONV_K, CONV_W), F32)],
        scratch_shapes=[pltpu.VMEM((8, CONV_W), F32)],
        compiler_params=_params("arbitrary"),
    )(*deps, dh1, z, gp, w, o, b, c, hc, c, hc, cw, ga, gc)


def _attn_bwd(q, k, v, o, do, bias, sinks, tm, deps=()):
    t = q.shape[0]
    per_step = tm // BLOCK

    def body(s_ref, q_ref, k_ref, v_ref, o_ref, do_ref, bias_ref, dq_ref, dk_ref, dv_ref, ds_ref):
        step = pl.program_id(0)

        @pl.when(step == 0)
        def _():
            ds_ref[...] = jnp.zeros_like(ds_ref)

        heads = range(N_Q_HEADS)

        def first_matmuls(b):
            i = step * per_step + b
            rows = slice(b * BLOCK, (b + 1) * BLOCK)
            kc, vc = _two_blocks(k_ref, i), _two_blocks(v_ref, i)
            bias_i = bias_ref[jnp.minimum(i, 2)]
            kgs = [kc[:, _head(g)] for g in range(N_KV_HEADS)]
            vgs = [vc[:, _head(g)] for g in range(N_KV_HEADS)]
            qs = [q_ref[rows, _head(hh)] for hh in heads]
            dosb = [do_ref[rows, _head(hh)] for hh in heads]
            dos = [d.astype(F32) for d in dosb]
            scores = [_attn_scores(qs[hh], kgs[hh // GROUP], bias_i) for hh in heads]
            dps = [lax.dot_general(dosb[hh], vgs[hh // GROUP], (((1,), (1,)), ((), ())), preferred_element_type=F32)
                   for hh in heads]
            return kgs, qs, dos, dosb, scores, dps

        dsink = [jnp.zeros((BLOCK, 1), F32) for _ in range(N_Q_HEADS)]
        ahead = None
        for b in range(per_step):
            i = step * per_step + b
            rows = slice(b * BLOCK, (b + 1) * BLOCK)
            kgs, qs, dos, dosb, scores, dps = first_matmuls(b)
            ps, dss = [], []
            for hh in heads:
                p, share = _attn_probs(scores[hh], s_ref[hh])
                drow = jnp.sum(dos[hh] * o_ref[rows, _head(hh)].astype(F32), axis=-1, keepdims=True)
                dss.append((p * (dps[hh] - drow)).astype(BF16))
                ps.append(p.astype(BF16))
                dsink[hh] = dsink[hh] + share * drow
            for hh in heads:
                dq_ref[rows, _head(hh)] = (jnp.dot(dss[hh], kgs[hh // GROUP], preferred_element_type=F32)
                                           * SCALE).astype(BF16)
            groups = [slice(GROUP * g, GROUP * (g + 1)) for g in range(N_KV_HEADS)]
            dkg = [lax.dot_general(jnp.concatenate(dss[gr], axis=0), jnp.concatenate(qs[gr], axis=0),
                                   (((0,), (0,)), ((), ())), preferred_element_type=F32) for gr in groups]
            dvg = [lax.dot_general(jnp.concatenate(ps[gr], axis=0), jnp.concatenate(dosb[gr], axis=0),
                                   (((0,), (0,)), ((), ())), preferred_element_type=F32) for gr in groups]
            dkb, dvb = jnp.concatenate(dkg, axis=1), jnp.concatenate(dvg, axis=1)
            if b == 0:
                @pl.when(step > 0)
                def _():
                    before = pl.ds(pl.multiple_of((i - 1) * BLOCK, BLOCK), BLOCK)
                    dk_ref[before, :] += dkb[:BLOCK]
                    dv_ref[before, :] += dvb[:BLOCK]
            else:
                at = pl.ds(pl.multiple_of((i - 1) * BLOCK, BLOCK), BLOCK)
                dk_ref[at, :] = ahead[0] + dkb[:BLOCK]
                dv_ref[at, :] = ahead[1] + dvb[:BLOCK]
            ahead = (dkb[BLOCK:], dvb[BLOCK:])
        last = pl.ds(pl.multiple_of(((step + 1) * per_step - 1) * BLOCK, BLOCK), BLOCK)
        dk_ref[last, :] = ahead[0]
        dv_ref[last, :] = ahead[1]
        for hh in range(N_Q_HEADS):
            ds_ref[hh:hh + 1, :] -= jnp.sum(dsink[hh])

    whole = pl.BlockSpec((t, KV_W), lambda i: (0, 0))
    blk = pl.BlockSpec((tm, ATTN_W), lambda i: (i, 0))
    body, dep_specs = _behind(body, deps)
    return pl.pallas_call(
        body, name="attn_bwd", grid=(t // tm,),
        in_specs=dep_specs + [pl.BlockSpec(memory_space=pltpu.SMEM), blk, whole, whole, blk, blk,
                              pl.BlockSpec(bias.shape, lambda i: (0, 0, 0))],
        out_specs=[blk, whole, whole, pl.BlockSpec((N_Q_HEADS, 128), lambda i: (0, 0))],
        out_shape=[jax.ShapeDtypeStruct((t, ATTN_W), BF16), jax.ShapeDtypeStruct((t, KV_W), F32),
                   jax.ShapeDtypeStruct((t, KV_W), F32), jax.ShapeDtypeStruct((N_Q_HEADS, 128), F32)],
        compiler_params=_params("arbitrary"),
    )(*deps, sinks, q, k, v, o, do, bias)


def _in_proj_bwd(dq, dk, dv, dbch, w, dh1, h, g, tabs, tm, split_lead=False):
    t = h.shape[0]
    nt = t // tm

    def body(dq_ref, dk_ref, dv_ref, dbch_ref, w_ref, dh1_ref, h_ref, g_ref, c_ref, sa_ref, sb_ref, *rest):
        dp_ref, dg_ref = rest[2:4] if split_lead else rest[1:3]
        i = pl.program_id(0)

        @pl.when(i == 0)
        def _():
            dg_ref[...] = jnp.zeros_like(dg_ref)

        cos, sa, sb = c_ref[...], sa_ref[...], sb_ref[...]
        rep = ATTN_W // (2 * HEAD_DIM)
        dqr = _rope_bwd(dq_ref[...].astype(F32), jnp.tile(cos, (1, rep)), jnp.tile(sa, (1, rep)),
                        jnp.tile(sb, (1, rep)))
        dkr = _rope_bwd(dk_ref[...], cos, sa, sb)
        dp = jnp.concatenate([dqr.astype(BF16), dkr.astype(BF16), dv_ref[...].astype(BF16), dbch_ref[...]], axis=1)
        dp_ref[...] = dp
        da = jnp.dot(dp, w_ref[...], preferred_element_type=F32)
        dx, dg = _rms_bwd(da, h_ref[...], g_ref[...])
        dg_ref[...] += dg
        dh = dh1_ref[...] + dx
        if not split_lead:
            rest[0][...] = dh
            return
        lead_ref, seq_ref, stage, sems = rest[0], rest[1], rest[4], rest[5]

        def copy(j, slot, first):
            if first:
                return pltpu.make_async_copy(stage.at[slot, pl.ds(BLOCK, tm - BLOCK)],
                                             seq_ref.at[pl.ds(0, tm - BLOCK)], sems.at[slot])
            return pltpu.make_async_copy(stage.at[slot], seq_ref.at[pl.ds(pl.multiple_of(j * tm - BLOCK, BLOCK), tm)],
                                         sems.at[slot])

        slot = i % 2
        pl.when(i == 2)(lambda: copy(0, slot, True).wait())
        pl.when(i > 2)(lambda: copy(i - 2, slot, False).wait())
        stage[slot] = dh

        @pl.when(i == 0)
        def _():
            lead_ref[...] = dh[:BLOCK]
            copy(0, slot, True).start()

        pl.when(i > 0)(lambda: copy(i, slot, False).start())

        @pl.when(i == nt - 1)
        def _():
            for j in range(max(nt - 2, 0), nt):
                copy(j, j % 2, j == 0).wait()

    row = lambda n: pl.BlockSpec((tm, n), lambda i: (i, 0))
    full = lambda a: pl.BlockSpec(a.shape, lambda i: (0, 0))
    dh_specs, dh_shapes, scratch = [row(D_MODEL)], [jax.ShapeDtypeStruct((t, D_MODEL), F32)], []
    if split_lead:
        dh_specs = [pl.BlockSpec((BLOCK, D_MODEL), lambda i: (0, 0)), pl.BlockSpec(memory_space=pl.ANY)]
        dh_shapes = [jax.ShapeDtypeStruct((BLOCK, D_MODEL), F32), jax.ShapeDtypeStruct((t - BLOCK, D_MODEL), F32)]
        scratch = [pltpu.VMEM((2, tm, D_MODEL), F32), pltpu.SemaphoreType.DMA((2,))]
    outs = pl.pallas_call(
        body, name="in_proj_bwd", grid=(nt,),
        in_specs=[row(ATTN_W), row(KV_W), row(KV_W), row(3 * CONV_W), full(w), row(D_MODEL), row(D_MODEL), full(g),
                  row(2 * HEAD_DIM), row(2 * HEAD_DIM), row(2 * HEAD_DIM)],
        out_specs=dh_specs + [row(IN_W), pl.BlockSpec((1, D_MODEL), lambda i: (0, 0))],
        out_shape=dh_shapes + [jax.ShapeDtypeStruct((t, IN_W), BF16), jax.ShapeDtypeStruct((1, D_MODEL), F32)],
        scratch_shapes=scratch,
        compiler_params=_params("arbitrary"),
    )(dq, dk, dv, dbch, w, dh1, h, g, *tabs)
    return (tuple(outs[:2]) if split_lead else outs[0],) + tuple(outs[-2:])


class _Tiles:
    def __init__(self, t):
        self.tm = _row_tile(t, 640)
        self.ts = self.tm
        self.tabs = _rope_tables(t)
        self.bias = _attn_bias()


def _mixer_fwd(h, p, tl):
    a, q, k, v, b, c, hc = _in_proj(h, p["mix_pre_g"], p["w_in"], tl.tabs, tl.ts)
    o = _attn_fwd(q, k, v, tl.bias, p["sinks"], tl.tm)
    return (h, a, q, k, v, b, c, hc, o)


def _out_fwd(mixed, p, tl, deps=()):
    h, a, q, k, v, b, c, hc, o = mixed
    h1, y, z = _mix_out(h, o, b, c, hc, p["conv_w"], p["attn_out_g"], p["conv_out_g"], p["w_out"], p["mix_post_g"],
                        tl.ts, deps)
    return h1, mixed + (h1, y, z)


def _mlp_fwd(h1, saved, p, tl, target=None):
    h2, a2, slope, f = _mlp(h1, p["mlp_pre_g"], p["w_up"], p["w_down"], p["mlp_post_g"], tl.tm, target)
    return h2, saved + (a2, slope, f)


def _mlp_part_bwd(dh, saved, p, tl, deps=()):
    h1, a2, slope, f = saved[9], saved[12], saved[13], saved[14]
    df, dup, dg2 = _mlp_bwd_hidden(dh, f, p["mlp_post_g"], slope, p["w_down"], tl.tm, deps)
    dh1, dg1 = _mlp_bwd_input(dup, p["w_up"], h1, p["mlp_pre_g"], dh, tl.tm)
    g = {"w_down": [d.reshape(N_CHIPS, FF_CHUNK, D_MODEL)
                    for d in _weight_grad(slope, df, "grad_w_down", x_is_slope=True)],
         "w_up": [d.reshape(N_CHIPS, D_MODEL, FF_CHUNK) for d in _weight_grad(a2, dup, "grad_w_up")],
         "mlp_post_g": dg2, "mlp_pre_g": dg1}
    return dh1, g


def _mix_out_part_bwd(dh1, saved, p, tl, deps=()):
    b, c, hc, o, y, z = saved[5], saved[6], saved[7], saved[8], saved[10], saved[11]
    dz, do, dbch, dgp, dga, dgc, dcw = _mix_out_bwd(dh1, z, p["mix_post_g"], p["w_out"], o, b, c, hc, p["conv_w"],
                                                    p["attn_out_g"], p["conv_out_g"], tl.ts, deps)
    g = {"w_out": [d.reshape(N_CHIPS, D_MODEL // N_CHIPS, D_MODEL) for d in _weight_grad(y, dz, "grad_w_out")],
         "mix_post_g": dgp, "attn_out_g": dga, "conv_out_g": dgc, "conv_w": dcw}
    return (dh1, do, dbch), g


def _attn_in_part_bwd(carry, saved, p, tl, deps=(), split_lead=False):
    dh1, do, dbch = carry
    h_in, q, k, v, o = saved[0], saved[2], saved[3], saved[4], saved[8]
    dq, dk, dv, dsink = _attn_bwd(q, k, v, o, do, tl.bias, p["sinks"], tl.tm, deps)
    dh, dproj, dgi = _in_proj_bwd(dq, dk, dv, dbch, p["w_in"], dh1, h_in, p["mix_pre_g"], tl.tabs, tl.ts, split_lead)
    return dh, dproj, {"mix_pre_g": dgi, "sinks": dsink[:, 0]}


def _in_grad(dproj, saved, deps=()):
    return [d.reshape(N_CHIPS, IN_W // N_CHIPS, D_MODEL) for d in _weight_grad(dproj, saved[1], "grad_w_in", deps=deps)]


def _place():
    return lax.axis_index("x"), lax.axis_index("y"), lax.axis_index("c")


def _other_chips(x, y):
    return [(1 - x, y), (x, 1 - y), (1 - x, 1 - y)]


_HBM = pl.BlockSpec(memory_space=pltpu.HBM)
_SEM = pl.BlockSpec(memory_space=pltpu.SEMAPHORE)
_EFFECT = pltpu.SideEffectType.DATAFLOW_SIDE_EFFECTING


class _Exchange:
    def __init__(self, name, bufs, plan, n, after=()):
        self.name, self.plan, nb = name, plan, len(bufs)
        n_in = nb + len(after)

        def body(*refs):
            send, recv, token = refs[n_in], refs[n_in + 1], refs[-1]
            for k, (src, dst, target, _) in enumerate(plan(refs[:nb])):
                pltpu.make_async_remote_copy(src_ref=src, dst_ref=dst, send_sem=send.at[k], recv_sem=recv.at[k],
                                             device_id=target, device_id_type=MESH).start()
            token[...] = jnp.zeros_like(token)

        outs = pl.pallas_call(
            body, name=name + "_start",
            out_shape=(pltpu.SemaphoreType.DMA((n,)), pltpu.SemaphoreType.DMA((n,)),
                       *[pltpu.HBM(b.shape, b.dtype) for b in bufs], jax.ShapeDtypeStruct((8, 128), F32)),
            in_specs=[_HBM] * nb + [pl.BlockSpec(memory_space=pl.ANY)] * len(after),
            out_specs=(_SEM, _SEM, *[_HBM] * nb, pl.BlockSpec(memory_space=pltpu.VMEM)),
            input_output_aliases={i: 2 + i for i in range(nb)},
            compiler_params=pltpu.CompilerParams(has_side_effects=_EFFECT),
        )(*[pltpu.with_memory_space_constraint(b, pltpu.HBM) for b in bufs], *after)
        self.send, self.recv, self.bufs, self.token = outs[0], outs[1], list(outs[2:2 + nb]), outs[-1]

    def wait(self, *after):
        plan, nb = self.plan, len(self.bufs)

        def body(*refs):
            send, recv = refs[nb], refs[nb + 1]
            for k, (src, _, target, land) in enumerate(plan(refs[:nb])):
                cp = pltpu.make_async_remote_copy(src_ref=src, dst_ref=land, send_sem=send.at[k], recv_sem=recv.at[k],
                                                  device_id=target, device_id_type=MESH)
                cp.wait_send()
                cp.wait_recv()

        outs = pl.pallas_call(
            body, name=self.name + "_wait", out_shape=[pltpu.HBM(b.shape, b.dtype) for b in self.bufs],
            in_specs=[_HBM] * nb + [_SEM, _SEM] + [pl.BlockSpec(memory_space=pl.ANY)] * len(after),
            out_specs=[_HBM] * nb, input_output_aliases={i: i for i in range(nb)},
            compiler_params=pltpu.CompilerParams(has_side_effects=_EFFECT),
        )(*self.bufs, self.send, self.recv, *after)
        return list(outs)


def _gather_plan(n):
    def plan(refs):
        x, y, c = _place()
        me = 2 * x + y
        return [(refs[a].at[me], refs[a].at[me], (px, py, c), refs[a].at[2 * px + py])
                for a in range(n) for px, py in _other_chips(x, y)]

    return plan


def _peers():
    x, y, c = _place()
    return [(k - 1, (x ^ (k >> 2), y ^ ((k >> 1) & 1), c ^ (k & 1))) for k in range(1, N_DEV)]


def _scatter_plan(n, half_rows):
    def plan(refs):
        out = []
        for a in range(n):
            hr = half_rows[a]
            for k, (px, py, pc) in _peers():
                out.append((refs[a].at[2 * px + py, pl.ds(pc * hr, hr)], refs[n + a].at[k], (px, py, pc),
                            refs[n + a].at[k]))
        return out

    return plan


def _join_plan(n):
    def plan(refs):
        x, y, c = _place()
        return [(refs[a].at[c], refs[a].at[c], (x, y, 1 - c), refs[a].at[1 - c]) for a in range(n)]

    return plan


def _sum_parts(g, q):
    rows, cols = g.shape[1], g.shape[2]
    hr = rows // 2
    tr = _block_rows(hr)
    per = hr // tr
    x, y, c = _place()
    where = jnp.stack([2 * x + y, c]).astype(jnp.int32)

    def body(where_ref, g_ref, q_ref, o_ref):
        total = g_ref[...]
        for k in range(N_DEV - 1):
            total = total + q_ref[k].astype(F32)
        o_ref[...] = total

    return pl.pallas_call(
        body, name="sum_parts",
        grid_spec=pltpu.PrefetchScalarGridSpec(
            num_scalar_prefetch=1, grid=(per,),
            in_specs=[pl.BlockSpec((None, tr, cols), lambda i, where_ref: (where_ref[0], where_ref[1] * per + i, 0)),
                      pl.BlockSpec((N_DEV - 1, tr, cols), lambda i, where_ref: (0, i, 0))],
            out_specs=pl.BlockSpec((None, tr, cols), lambda i, where_ref: (where_ref[1], i, 0))),
        out_shape=jax.ShapeDtypeStruct((2, hr, cols), F32),
        compiler_params=_params("parallel"),
    )(where, g, q)


def _all_plan(refs):
    x, y, c = _place()
    mine = refs[0].at[4 * x + 2 * y + c]
    return [(mine, mine, (px, py, pc), refs[0].at[4 * px + 2 * py + pc]) for _, (px, py, pc) in _peers()]


def _sum_devices(parts):
    def body(p_ref, o_ref):
        total = p_ref[0]
        for d in range(1, N_DEV):
            total = total + p_ref[d]
        o_ref[...] = total

    vm = pl.BlockSpec(memory_space=pltpu.VMEM)
    return pl.pallas_call(body, name="sum_devices", in_specs=[vm], out_specs=vm,
                          out_shape=jax.ShapeDtypeStruct(parts.shape[1:], F32))(parts)


def _adamw_math(w, g, m, v):
    m = ADAM_B1 * m + (1.0 - ADAM_B1) * g
    v = ADAM_B2 * v + (1.0 - ADAM_B2) * jnp.square(g)
    m_hat = m / (1.0 - ADAM_B1 ** ADAM_STEP)
    v_hat = v / (1.0 - ADAM_B2 ** ADAM_STEP)
    delta = -ADAM_LR * (m_hat / (jnp.sqrt(v_hat) + ADAM_EPS) + ADAM_WD * w)
    return delta, m, v


def _adamw_large(layer, w, halves, m, v, other):
    _, rows, cols = w.shape
    tr = _block_rows(rows // 2)
    per = rows // 2 // tr

    def body(w_ref, g_ref, m_ref, v_ref, *rest):
        g_out, d_ref, nm_ref, nv_ref = rest[-4:]
        g = g_ref[...]
        g_out[...] = g
        d_ref[...], nm_ref[...], nv_ref[...] = _adamw_math(w_ref[...], g, m_ref[...], v_ref[...])

    blk = pl.BlockSpec((None, tr, cols), lambda i: (layer, i, 0))
    half = pl.BlockSpec((None, tr, cols), lambda i: (i // per, i % per, 0))
    kept = [] if other is None else list(other)
    return pl.pallas_call(
        body, name="adamw_large", grid=(rows // tr,),
        in_specs=[blk, half, blk, blk] + [pl.BlockSpec(memory_space=pl.ANY)] * len(kept), out_specs=[blk] * 4,
        out_shape=[jax.ShapeDtypeStruct(w.shape, F32)] * 4,
        input_output_aliases={4 + k: k for k in range(len(kept))},
        compiler_params=_params("parallel"),
    )(w, halves, m, v, *kept)


def _adamw_small(ws, gs, ms, vs):
    n = len(ws)

    def body(*refs):
        w_r, g_r, m_r, v_r = refs[:n], refs[n:2 * n], refs[2 * n:3 * n], refs[3 * n:4 * n]
        d_r, nm_r, nv_r = refs[4 * n:5 * n], refs[5 * n:6 * n], refs[6 * n:]
        for a in range(n):
            d_r[a][...], nm_r[a][...], nv_r[a][...] = _adamw_math(w_r[a][...], g_r[a][...], m_r[a][...], v_r[a][...])

    vm = pl.BlockSpec(memory_space=pltpu.VMEM)
    outs = pl.pallas_call(
        body, name="adamw_small", in_specs=[vm] * (4 * n), out_specs=[vm] * (3 * n),
        out_shape=[jax.ShapeDtypeStruct(w.shape, F32) for w in ws] * 3,
    )(*ws, *gs, *ms, *vs)
    return outs[:n], outs[n:2 * n], outs[2 * n:]


_LARGE = ("w_in", "w_out", "w_up", "w_down")
_SMALL = ("meta_tokens", "mix_pre_g", "conv_w", "sinks", "attn_out_g", "conv_out_g", "mix_post_g", "mlp_pre_g",
          "mlp_post_g")
_ORDER = ("meta_tokens", "mix_pre_g", "w_in", "conv_w", "sinks", "attn_out_g", "conv_out_g", "w_out", "mix_post_g",
          "mlp_pre_g", "w_up", "w_down", "mlp_post_g")


class _Reduce:
    def __init__(self, name, grads, after=()):
        self.name, self.n = name, len(grads)
        self.own = [g for g, _ in grads]
        half_rows = [g.shape[1] // 2 for g in self.own]
        zones = [lax.empty((N_DEV - 1, hr, g.shape[2]), BF16) for g, hr in zip(self.own, half_rows)]
        self.exchange = _Exchange(name + "_scatter", [b for _, b in grads] + zones, _scatter_plan(self.n, half_rows),
                                  (N_DEV - 1) * self.n, after)

    @property
    def token(self):
        return self.exchange.token

    def join(self, *after):
        bufs = self.exchange.wait(*after)
        halves = [_sum_parts(g, q) for g, q in zip(self.own, bufs[self.n:])]
        self.exchange = _Exchange(self.name + "_join", halves, _join_plan(self.n), self.n)

    def done(self, *after):
        return self.exchange.wait(*after)


def _pad_cols(a, n=D_MODEL):
    return jnp.pad(a, ((0, 0), (0, n - a.shape[1])))


def kernel(x, meta_tokens, mix_pre_g, w_in, conv_w, sinks, attn_out_g, conv_out_g, w_out, mix_post_g, mlp_pre_g, w_up, w_down, mlp_post_g, loss_target, m_meta_tokens, m_mix_pre_g, m_w_in, m_conv_w, m_sinks, m_attn_out_g, m_conv_out_g, m_w_out, m_mix_post_g, m_mlp_pre_g, m_w_up, m_w_down, m_mlp_post_g, v_meta_tokens, v_mix_pre_g, v_w_in, v_conv_w, v_sinks, v_attn_out_g, v_conv_out_g, v_w_out, v_mix_post_g, v_mlp_pre_g, v_w_up, v_w_down, v_mlp_post_g):
    w = dict(meta_tokens=meta_tokens, mix_pre_g=mix_pre_g, w_in=w_in, conv_w=conv_w, sinks=sinks,
             attn_out_g=attn_out_g, conv_out_g=conv_out_g, w_out=w_out, mix_post_g=mix_post_g, mlp_pre_g=mlp_pre_g,
             w_up=w_up, w_down=w_down, mlp_post_g=mlp_post_g)
    m = dict(meta_tokens=m_meta_tokens, mix_pre_g=m_mix_pre_g, w_in=m_w_in, conv_w=m_conv_w, sinks=m_sinks,
             attn_out_g=m_attn_out_g, conv_out_g=m_conv_out_g, w_out=m_w_out, mix_post_g=m_mix_post_g,
             mlp_pre_g=m_mlp_pre_g, w_up=m_w_up, w_down=m_w_down, mlp_post_g=m_mlp_post_g)
    v = dict(meta_tokens=v_meta_tokens, mix_pre_g=v_mix_pre_g, w_in=v_w_in, conv_w=v_conv_w, sinks=v_sinks,
             attn_out_g=v_attn_out_g, conv_out_g=v_conv_out_g, w_out=v_w_out, mix_post_g=v_mix_post_g,
             mlp_pre_g=v_mlp_pre_g, w_up=v_w_up, w_down=v_w_down, mlp_post_g=v_mlp_post_g)
    chip = 2 * lax.axis_index("x") + lax.axis_index("y")
    tl = _Tiles(x.shape[1] + BLOCK)

    def zone(quarter):
        return lax.dynamic_update_slice(lax.empty((N_CHIPS,) + quarter.shape, quarter.dtype), quarter[None],
                                        (chip,) + (0,) * quarter.ndim)

    w, m, v = ({**d, "w_in": jnp.swapaxes(d["w_in"], 1, 2)} for d in (w, m, v))
    zones = {n: [zone(w[n][l].astype(BF16)) for l in range(DEPTH)] for n in _LARGE}
    first = _Exchange("gather_first", [zones["w_in"][0], zone(w["conv_w"]), zone(w["meta_tokens"])], _gather_plan(3), 9)
    out0 = _Exchange("gather_out", [zones["w_out"][0]], _gather_plan(1), 3, [first.token])
    rest = _Exchange("gather_rest", [zones[n][0] for n in ("w_up", "w_down")], _gather_plan(2), 6, [out0.token])

    def whole_in(quarters):
        return quarters.reshape(IN_W, D_MODEL)

    h = jnp.concatenate([jnp.zeros((BLOCK, D_MODEL), F32), x[0]], axis=0)
    q_in, q_conv, q_meta = first.wait(rest.token, *tl.tabs, tl.bias, h)
    conv_whole = jnp.transpose(q_conv, (1, 2, 0, 3)).reshape(DEPTH, CONV_K, CONV_W)
    meta = jnp.transpose(q_meta, (1, 0, 2)).reshape(N_META, D_MODEL)
    p = [{"conv_w": conv_whole[l], "sinks": w["sinks"][l]} for l in range(DEPTH)]
    for l in range(DEPTH):
        for n in ("mix_pre_g", "attn_out_g", "conv_out_g", "mix_post_g", "mlp_pre_g", "mlp_post_g"):
            p[l][n] = w[n][l][None, :]

    h = lax.dynamic_update_slice(h, meta, (LEAD_PAD, 0))
    p[0]["w_in"] = whole_in(q_in)
    mixed = _mixer_fwd(h, p[0], tl)
    second = _Exchange("gather_second", [zones["w_in"][1], zones["w_out"][1]], _gather_plan(2), 6, [mixed[-1]])
    second_mlp = _Exchange("gather_second_mlp", [zones["w_up"][1], zones["w_down"][1]], _gather_plan(2), 6,
                           [second.token])
    p[0]["w_out"], = out0.wait(second_mlp.token)
    h1, saved0 = _out_fwd(mixed, p[0], tl)
    p[0]["w_up"], p[0]["w_down"] = rest.wait(h1)
    h, saved0 = _mlp_fwd(h1, saved0, p[0], tl)
    q_in, p[1]["w_out"] = second.wait(h)
    p[1]["w_in"] = whole_in(q_in)
    h1, saved1 = _out_fwd(_mixer_fwd(h, p[1], tl), p[1], tl)
    p[1]["w_up"], p[1]["w_down"] = second_mlp.wait(h1)
    (loss_tile, dh), saved1 = _mlp_fwd(h1, saved1, p[1], tl, loss_target[0])

    def adamw(layer, halves, other):
        return {n: _adamw_large(layer, w[n], halves[n], m[n], v[n], None if other is None else other[n])
                for n in halves}

    dh1, g1 = _mlp_part_bwd(dh, saved1, p[1], tl)
    carry, gm = _mix_out_part_bwd(dh1, saved1, p[1], tl)
    dh, dproj, gi = _attn_in_part_bwd(carry, saved1, p[1], tl)
    g1.update(gm, w_in=_in_grad(dproj, saved1), **gi)
    red1 = _Reduce("reduce1", [g1[n] for n in _LARGE])
    dh1, g0 = _mlp_part_bwd(dh, saved0, p[0], tl, [red1.token])
    red1.join(g0["w_down"][0])
    carry, gm = _mix_out_part_bwd(dh1, saved0, p[0], tl, [red1.token])
    first0 = ("w_up", "w_down", "w_out")
    g0.update(gm)
    red0a = _Reduce("reduce0a", [g0[n] for n in first0])
    (dlead, dseq), dproj, gi = _attn_in_part_bwd(carry, saved0, p[0], tl, [red0a.token], split_lead=True)
    g0.update(gi)
    grad_x = dseq[None]
    grads = {n: [g0[n], g1[n]] for n in g0 if n not in _LARGE}

    rows = [dlead[LEAD_PAD:]]
    for n in ("mix_pre_g", "mix_post_g", "mlp_pre_g", "mlp_post_g"):
        rows += grads[n]
    rows += [jnp.concatenate([grads["attn_out_g"][l], grads["conv_out_g"][l]], axis=1) for l in range(DEPTH)]
    rows.append(jnp.concatenate(grads["conv_w"], axis=1))
    rows.append(_pad_cols(jnp.concatenate(grads["sinks"])[None, :]))
    rows.append(_pad_cols(loss_tile[:1]))
    packed = jnp.concatenate(rows, axis=0)
    packed = jnp.pad(packed, ((0, SMALL_ROWS - packed.shape[0]), (0, 0)))
    device = 2 * chip + lax.axis_index("c")
    small_parts = _Exchange("gather_small", [lax.dynamic_update_slice(lax.empty((N_DEV,) + packed.shape, F32),
                                                                      packed[None], (device, 0, 0))], _all_plan, N_DEV - 1)
    g0["w_in"] = _in_grad(dproj, saved0, [small_parts.token])
    red0b = _Reduce("reduce0b", [g0["w_in"]])
    done1 = adamw(1, dict(zip(_LARGE, red1.done(red0b.token))), None)
    total = _sum_devices(small_parts.wait(*[done1[n][0] for n in _LARGE])[0])
    r0 = N_META
    small = {
        "meta_tokens": lax.dynamic_slice(total[:N_META], (0, chip * (D_MODEL // N_CHIPS)), (N_META, D_MODEL // N_CHIPS)),
        "mix_pre_g": total[r0:r0 + 2], "mix_post_g": total[r0 + 2:r0 + 4], "mlp_pre_g": total[r0 + 4:r0 + 6],
        "mlp_post_g": total[r0 + 6:r0 + 8],
        "attn_out_g": total[r0 + 8:r0 + 10, :ATTN_W], "conv_out_g": total[r0 + 8:r0 + 10, ATTN_W:],
        "conv_w": lax.dynamic_slice(total[r0 + 10:r0 + 13].reshape(CONV_K, DEPTH, CONV_W).transpose(1, 0, 2),
                                    (0, 0, chip * (CONV_W // N_CHIPS)), (DEPTH, CONV_K, CONV_W // N_CHIPS)),
        "sinks": total[r0 + 13, :DEPTH * N_Q_HEADS].reshape(DEPTH, N_Q_HEADS),
    }
    loss = total[r0 + 14, 0]

    ds, nms, nvs = _adamw_small([w[n] for n in _SMALL], [small[n] for n in _SMALL], [m[n] for n in _SMALL],
                                [v[n] for n in _SMALL])
    red0a.join(ds[0], grad_x)
    red0b.join(red0a.token)
    done0 = adamw(0, dict(zip(first0, red0a.done(red0b.token))), done1)
    done0.update(adamw(0, {"w_in": red0b.done(done0["w_down"][0])[0]}, done1))
    grad, delta, new_m, new_v = {}, {}, {}, {}
    for n in _LARGE:
        grad[n], delta[n], new_m[n], new_v[n] = done0[n]
    for d in (grad, delta, new_m, new_v):
        d["w_in"] = jnp.swapaxes(d["w_in"], 1, 2)
    for i, n in enumerate(_SMALL):
        grad[n], delta[n], new_m[n], new_v[n] = small[n], ds[i], nms[i], nvs[i]
    return (loss, grad_x, *[grad[n] for n in _ORDER], *[delta[n] for n in _ORDER], *[new_m[n] for n in _ORDER],
            *[new_v[n] for n in _ORDER])
```

```python
import jax
import jax.numpy as jnp
from jax import lax
from jax.experimental import pallas as pl
from jax.experimental.pallas import tpu as pltpu

F32 = jnp.float32
BF16 = jnp.bfloat16

D_MODEL = 1024
DEPTH = 2
N_META = 16
ATTN_W = 512
CONV_W = 512
HEAD_DIM = 64
N_Q_HEADS = 8
N_KV_HEADS = 2
GROUP = N_Q_HEADS // N_KV_HEADS
KV_W = N_KV_HEADS * HEAD_DIM
CONV_K = 3
BLOCK = 128
LEAD_PAD = BLOCK - N_META
ROPE_THETA = 500000.0
ROT_DIM = HEAD_DIM // 4
ROT_HALF = ROT_DIM // 2
D_FF = 4 * D_MODEL
IN_W = ATTN_W + 2 * KV_W + 3 * CONV_W
QKV_W = ATTN_W + 2 * KV_W
EPS = 1e-6
SCALE = HEAD_DIM ** -0.5
FF_CHUNK = 1024
N_CHIPS = 4
N_DEV = 8

ADAM_LR = 0.001
ADAM_B1 = 0.9
ADAM_B2 = 0.999
ADAM_EPS = 1e-08
ADAM_WD = 0.01
ADAM_STEP = 10

V7X_VMEM_LIMIT = 60 * 1024 * 1024
SMALL_ROWS = 32

MESH = pl.DeviceIdType.MESH


def _params(*sem):
    return pltpu.CompilerParams(dimension_semantics=sem, vmem_limit_bytes=V7X_VMEM_LIMIT)


def _block_rows(n):
    return max(r for r in range(16, min(n, 64) + 1, 16) if n % r == 0)


def _row_tile(t, most):
    nb = t // BLOCK
    for b in range(most // BLOCK, 0, -1):
        if nb % b == 0:
            return b * BLOCK
    return BLOCK


def _behind(body, deps):
    n = len(deps)

    def wrapped(*refs):
        body(*refs[n:])

    return wrapped, [pl.BlockSpec(memory_space=pl.ANY)] * n


def _rms(x, g):
    r = lax.rsqrt(jnp.mean(x * x, axis=-1, keepdims=True) + EPS)
    return x * r * g


def _rms_bwd(dy, x, g):
    r = lax.rsqrt(jnp.mean(x * x, axis=-1, keepdims=True) + EPS)
    xh = x * r
    dg = jnp.sum(dy * xh, axis=0, keepdims=True)
    dxh = dy * g
    dx = r * (dxh - xh * jnp.mean(dxh * xh, axis=-1, keepdims=True))
    return dx, dg


def _rope(x, cos, sa, sb):
    n = x.shape[-1]
    return x * cos + pltpu.roll(x, n - ROT_HALF, 1) * sa + pltpu.roll(x, ROT_HALF, 1) * sb


def _rope_bwd(dy, cos, sa, sb):
    n = dy.shape[-1]
    return dy * cos + pltpu.roll(dy * sa, ROT_HALF, 1) + pltpu.roll(dy * sb, n - ROT_HALF, 1)


def _rope_tables(t):
    pos = lax.broadcasted_iota(jnp.int32, (t, ROT_HALF), 0).astype(F32) - LEAD_PAD
    pair = lax.broadcasted_iota(jnp.int32, (t, ROT_HALF), 1).astype(F32)
    inv_freq = jnp.power(jnp.float32(ROPE_THETA), -(2.0 * pair) / ROT_DIM)
    ang = pos * inv_freq
    cos, sin = lax.optimization_barrier((jnp.cos(ang), jnp.sin(ang)))
    spread = (1, 2 * HEAD_DIM // ROT_HALF)
    cos, sin = jnp.tile(cos, spread), jnp.tile(sin, spread)
    dim = lax.broadcasted_iota(jnp.int32, (t, 2 * HEAD_DIM), 1) % HEAD_DIM
    return (jnp.where(dim < ROT_DIM, cos, 1.0), jnp.where(dim < ROT_HALF, -sin, 0.0),
            jnp.where((dim >= ROT_HALF) & (dim < ROT_DIM), sin, 0.0))


def _in_proj(h, g, w, tabs, tm):
    t = h.shape[0]

    def body(h_ref, g_ref, w_ref, c_ref, sa_ref, sb_ref, a_ref, q_ref, k_ref, v_ref, b_ref, cg_ref, hc_ref):
        a = _rms(h_ref[...], g_ref[...]).astype(BF16)
        a_ref[...] = a
        p = lax.dot_general(a, w_ref[...], (((1,), (1,)), ((), ())), preferred_element_type=F32)
        cos, sa, sb = c_ref[...], sa_ref[...], sb_ref[...]
        rep = ATTN_W // (2 * HEAD_DIM)
        q = _rope(p[:, :ATTN_W], jnp.tile(cos, (1, rep)), jnp.tile(sa, (1, rep)), jnp.tile(sb, (1, rep)))
        q_ref[...] = (q * SCALE).astype(BF16)
        k_ref[...] = _rope(p[:, ATTN_W:ATTN_W + KV_W], cos, sa, sb).astype(BF16)
        v_ref[...] = p[:, ATTN_W + KV_W:QKV_W].astype(BF16)
        b_ref[...] = p[:, QKV_W:QKV_W + CONV_W].astype(BF16)
        cg_ref[...] = p[:, QKV_W + CONV_W:QKV_W + 2 * CONV_W].astype(BF16)
        hc_ref[...] = p[:, QKV_W + 2 * CONV_W:].astype(BF16)

    row = lambda n: pl.BlockSpec((tm, n), lambda i: (i, 0))
    full = lambda a: pl.BlockSpec(a.shape, lambda i: (0, 0))
    return pl.pallas_call(
        body, name="in_proj", grid=(t // tm,),
        in_specs=[row(D_MODEL), full(g), full(w), row(2 * HEAD_DIM), row(2 * HEAD_DIM), row(2 * HEAD_DIM)],
        out_specs=[row(D_MODEL), row(ATTN_W), row(KV_W), row(KV_W), row(CONV_W), row(CONV_W), row(CONV_W)],
        out_shape=[jax.ShapeDtypeStruct((t, D_MODEL), BF16), jax.ShapeDtypeStruct((t, ATTN_W), BF16),
                   jax.ShapeDtypeStruct((t, KV_W), BF16), jax.ShapeDtypeStruct((t, KV_W), BF16),
                   jax.ShapeDtypeStruct((t, CONV_W), BF16), jax.ShapeDtypeStruct((t, CONV_W), BF16),
                   jax.ShapeDtypeStruct((t, CONV_W), BF16)],
        compiler_params=_params("parallel"),
    )(h, g, w, *tabs)


def _attn_bias():
    r = lax.broadcasted_iota(jnp.int32, (3, BLOCK, 2 * BLOCK), 1)
    c = lax.broadcasted_iota(jnp.int32, (3, BLOCK, 2 * BLOCK), 2)
    i = lax.broadcasted_iota(jnp.int32, (3, BLOCK, 2 * BLOCK), 0)
    ok = (c > r) & (c <= r + BLOCK) & (c + (i - 1) * BLOCK >= LEAD_PAD)
    return jnp.where(ok, 0.0, -jnp.inf).astype(F32)


def _attn_scores(qh, kg, bias):
    return lax.dot_general(qh, kg, (((1,), (1,)), ((), ())), preferred_element_type=F32) + bias


def _attn_probs(s, sk):
    m = jnp.maximum(jnp.max(s, axis=-1, keepdims=True), sk)
    e = jnp.exp(s - m)
    es = jnp.exp(sk - m)
    rden = 1.0 / (jnp.sum(e, axis=-1, keepdims=True) + es)
    return e * rden, es * rden


def _head(hh):
    return slice(hh * HEAD_DIM, (hh + 1) * HEAD_DIM)


def _two_blocks(ref, i):
    prev = jnp.maximum(i - 1, 0)
    return jnp.concatenate([ref[pl.ds(pl.multiple_of(prev * BLOCK, BLOCK), BLOCK), :],
                            ref[pl.ds(pl.multiple_of(i * BLOCK, BLOCK), BLOCK), :]], axis=0)


def _attn_fwd(q, k, v, bias, sinks, tm):
    t = q.shape[0]
    per_step = tm // BLOCK
    heads = range(N_Q_HEADS)

    def body(s_ref, q_ref, k_ref, v_ref, bias_ref, o_ref):
        for b in range(per_step):
            i = pl.program_id(0) * per_step + b
            rows = slice(b * BLOCK, (b + 1) * BLOCK)
            kc, vc = _two_blocks(k_ref, i), _two_blocks(v_ref, i)
            bias_i = bias_ref[jnp.minimum(i, 2)]
            scores = [_attn_scores(q_ref[rows, _head(hh)], kc[:, _head(hh // GROUP)], bias_i) for hh in heads]
            probs = [_attn_probs(scores[hh], s_ref[hh])[0].astype(BF16) for hh in heads]
            for hh in heads:
                o_ref[rows, _head(hh)] = jnp.dot(probs[hh], vc[:, _head(hh // GROUP)],
                                                 preferred_element_type=F32).astype(BF16)

    whole = pl.BlockSpec((t, KV_W), lambda i: (0, 0))
    return pl.pallas_call(
        body, name="attn_fwd", grid=(t // tm,),
        in_specs=[pl.BlockSpec(memory_space=pltpu.SMEM), pl.BlockSpec((tm, ATTN_W), lambda i: (i, 0)), whole, whole,
                  pl.BlockSpec(bias.shape, lambda i: (0, 0, 0))],
        out_specs=pl.BlockSpec((tm, ATTN_W), lambda i: (i, 0)),
        out_shape=jax.ShapeDtypeStruct((t, ATTN_W), BF16),
        compiler_params=_params("parallel"),
    )(sinks, q, k, v, bias)


def _shift_rows(u, halo, n):
    r = pltpu.roll(u, n, 0)
    hr = pltpu.roll(halo, n, 0)
    idx = lax.broadcasted_iota(jnp.int32, hr.shape, 0)
    return jnp.concatenate([jnp.where(idx < n, hr, r[:8]), r[8:]], axis=0)


def _advance_rows(u, halo, n):
    rows = u.shape[0]
    r = pltpu.roll(u, rows - n, 0)
    hr = pltpu.roll(halo, 8 - n, 0)
    idx = lax.broadcasted_iota(jnp.int32, hr.shape, 0)
    return jnp.concatenate([r[:rows - 8], jnp.where(idx >= 8 - n, hr, r[rows - 8:])], axis=0)


def _mix_out(h, o, b, c, hc, cw, ga, gc, w, gp, tm, deps=()):
    t = h.shape[0]

    def body(h_ref, o_ref, b_ref, c_ref, hc_ref, cw_ref, ga_ref, gc_ref, w_ref, gp_ref, h1_ref, y_ref, z_ref, halo):
        @pl.when(pl.program_id(0) == 0)
        def _():
            halo[...] = jnp.zeros_like(halo)

        u = c_ref[...].astype(F32) * hc_ref[...].astype(F32)
        cv = cw_ref[0:1, :] * _shift_rows(u, halo[...], 2) + cw_ref[1:2, :] * _shift_rows(u, halo[...], 1) \
            + cw_ref[2:3, :] * u
        halo[...] = u[tm - 8:]
        yc = b_ref[...].astype(F32) * cv
        y = jnp.concatenate([_rms(o_ref[...].astype(F32), ga_ref[...]), _rms(yc, gc_ref[...])], axis=1).astype(BF16)
        y_ref[...] = y
        z = jnp.dot(y, w_ref[...].reshape(D_MODEL, D_MODEL), preferred_element_type=F32)
        z_ref[...] = z
        h1_ref[...] = h_ref[...] + _rms(z, gp_ref[...])

    row = lambda n: pl.BlockSpec((tm, n), lambda i: (i, 0))
    full = lambda a: pl.BlockSpec(a.shape, lambda i: (0,) * a.ndim)
    body, dep_specs = _behind(body, deps)
    return pl.pallas_call(
        body, name="mix_out", grid=(t // tm,),
        in_specs=dep_specs + [row(D_MODEL), row(ATTN_W), row(CONV_W), row(CONV_W), row(CONV_W), full(cw), full(ga),
                              full(gc), full(w), full(gp)],
        out_specs=[row(D_MODEL), row(D_MODEL), row(D_MODEL)],
        out_shape=[jax.ShapeDtypeStruct((t, D_MODEL), F32), jax.ShapeDtypeStruct((t, D_MODEL), BF16),
                   jax.ShapeDtypeStruct((t, D_MODEL), F32)],
        scratch_shapes=[pltpu.VMEM((8, CONV_W), F32)],
        compiler_params=_params("arbitrary"),
    )(*deps, h, o, b, c, hc, cw, ga, gc, w, gp)


def _mlp(h1, g1, wu, wd, g2, tm, target=None):
    t = h1.shape[0]
    nj = D_FF // FF_CHUNK
    per_step = tm // BLOCK if target is not None else 0

    def body(h1_ref, g1_ref, wu_ref, wd_ref, g2_ref, *rest):
        t_refs, outs = rest[:per_step], rest[per_step:]
        a2_ref, slope_ref, f_ref = outs[-3:]
        a2 = _rms(h1_ref[...], g1_ref[...]).astype(BF16)
        a2_ref[...] = a2
        f = None
        for j in range(nj):
            up = jnp.dot(a2, wu_ref[j], preferred_element_type=F32)
            r = jnp.maximum(up, 0.0)
            slope_ref[:, j * FF_CHUNK:(j + 1) * FF_CHUNK] = (r + r).astype(BF16)
            part = jnp.dot((r * r).astype(BF16), wd_ref[j], preferred_element_type=F32)
            f = part if f is None else f + part
        f_ref[...] = f
        h2 = h1_ref[...] + _rms(f, g2_ref[...])
        if target is None:
            outs[0][...] = h2
            return
        loss_ref, dh_ref = outs[:2]
        i = pl.program_id(0)

        @pl.when(i == 0)
        def _():
            loss_ref[...] = jnp.zeros_like(loss_ref)

        total = jnp.zeros((), F32)
        for b in range(per_step):
            rows = slice(b * BLOCK, (b + 1) * BLOCK)
            err = h2[rows] - t_refs[b][...]
            if b == 0:
                err = jnp.where(i == 0, 0.0, err)
            dh_ref[rows, :] = err * (1.0 / D_MODEL)
            total = total + jnp.sum(err * err)
        loss_ref[...] += total * (0.5 / D_MODEL)

    def target_block(b):
        return pl.BlockSpec((BLOCK, D_MODEL), lambda i: (jnp.maximum(i * per_step + b - 1, 0), 0))

    row = pl.BlockSpec((tm, D_MODEL), lambda i: (i, 0))
    vec = pl.BlockSpec((1, D_MODEL), lambda i: (0, 0))
    resident = pl.BlockSpec(memory_space=pltpu.VMEM)
    first_specs, first_shapes = [row], [jax.ShapeDtypeStruct((t, D_MODEL), F32)]
    if target is not None:
        first_specs = [pl.BlockSpec((8, 128), lambda i: (0, 0)), row]
        first_shapes = [jax.ShapeDtypeStruct((8, 128), F32), jax.ShapeDtypeStruct((t, D_MODEL), F32)]
    outs = pl.pallas_call(
        body, name="mlp", grid=(t // tm,),
        in_specs=[row, vec, resident, resident, vec] + [target_block(b) for b in range(per_step)],
        out_specs=first_specs + [row, pl.BlockSpec((tm, D_FF), lambda i: (i, 0)), row],
        out_shape=first_shapes + [jax.ShapeDtypeStruct((t, D_MODEL), BF16), jax.ShapeDtypeStruct((t, D_FF), BF16),
                                  jax.ShapeDtypeStruct((t, D_MODEL), F32)],
        compiler_params=_params("parallel" if target is None else "arbitrary"),
    )(h1, g1, wu, wd, g2, *([target] * per_step))
    return (outs[0] if target is None else tuple(outs[:2]),) + tuple(outs[-3:])


def _mlp_bwd_hidden(dh2, f, g2, slope, wd, tm, deps=()):
    t = dh2.shape[0]
    nj = D_FF // FF_CHUNK

    def body(dh2_ref, f_ref, g2_ref, slope_ref, wd_ref, df_ref, dup_ref, dg2_ref):
        @pl.when(pl.program_id(0) == 0)
        def _():
            dg2_ref[...] = jnp.zeros_like(dg2_ref)

        df, dg = _rms_bwd(dh2_ref[...], f_ref[...], g2_ref[...])
        dg2_ref[...] += dg
        df = df.astype(BF16)
        df_ref[...] = df
        for j in range(nj):
            cols = slice(j * FF_CHUNK, (j + 1) * FF_CHUNK)
            dact = lax.dot_general(df, wd_ref[j], (((1,), (1,)), ((), ())), preferred_element_type=F32)
            dup_ref[:, cols] = (dact * slope_ref[:, cols].astype(F32)).astype(BF16)

    row = pl.BlockSpec((tm, D_MODEL), lambda i: (i, 0))
    wide = pl.BlockSpec((tm, D_FF), lambda i: (i, 0))
    vec = pl.BlockSpec((1, D_MODEL), lambda i: (0, 0))
    body, dep_specs = _behind(body, deps)
    return pl.pallas_call(
        body, name="mlp_bwd_hidden", grid=(t // tm,),
        in_specs=dep_specs + [row, row, vec, wide, pl.BlockSpec(memory_space=pltpu.VMEM)],
        out_specs=[row, wide, vec],
        out_shape=[jax.ShapeDtypeStruct((t, D_MODEL), BF16), jax.ShapeDtypeStruct((t, D_FF), BF16),
                   jax.ShapeDtypeStruct((1, D_MODEL), F32)],
        compiler_params=_params("arbitrary"),
    )(*deps, dh2, f, g2, slope, wd)


def _mlp_bwd_input(dup, wu, h1, g1, dh2, tm):
    t = dh2.shape[0]
    nj = D_FF // FF_CHUNK

    def body(dup_ref, wu_ref, h1_ref, g1_ref, dh2_ref, dh1_ref, dg1_ref):
        @pl.when(pl.program_id(0) == 0)
        def _():
            dg1_ref[...] = jnp.zeros_like(dg1_ref)

        da2 = None
        for j in range(nj):
            part = lax.dot_general(dup_ref[:, j * FF_CHUNK:(j + 1) * FF_CHUNK], wu_ref[j], (((1,), (1,)), ((), ())),
                                   preferred_element_type=F32)
            da2 = part if da2 is None else da2 + part
        dx, dg = _rms_bwd(da2, h1_ref[...], g1_ref[...])
        dh1_ref[...] = dh2_ref[...] + dx
        dg1_ref[...] += dg

    row = pl.BlockSpec((tm, D_MODEL), lambda i: (i, 0))
    vec = pl.BlockSpec((1, D_MODEL), lambda i: (0, 0))
    return pl.pallas_call(
        body, name="mlp_bwd_input", grid=(t // tm,),
        in_specs=[pl.BlockSpec((tm, D_FF), lambda i: (i, 0)), pl.BlockSpec(memory_space=pltpu.VMEM), row, vec, row],
        out_specs=[row, vec],
        out_shape=[jax.ShapeDtypeStruct((t, D_MODEL), F32), jax.ShapeDtypeStruct((1, D_MODEL), F32)],
        compiler_params=_params("arbitrary"),
    )(dup, wu, h1, g1, dh2)


def _row_split(t):
    tile = min(t, 1024)
    return tile, t // tile, t % tile


def _row_split_specs(t, cols):
    tile, whole, rest = _row_split(t)
    specs = [pl.BlockSpec((tile, cols), lambda r: (jnp.minimum(r, whole - 1), 0))]
    if rest:
        specs.append(pl.BlockSpec((rest, cols), lambda r: (whole * tile // rest, 0)))
    return specs


def _weight_grad(x, y, name, x_is_slope=False, deps=()):
    t, k = x.shape
    n = y.shape[1]
    tn = FF_CHUNK
    tk = FF_CHUNK if k % FF_CHUNK == 0 else k
    _, whole, rest = _row_split(t)
    steps = whole + bool(rest)

    def body(*refs):
        o_ref, ob_ref, r = refs[-2], refs[-1], pl.program_id(0)

        @pl.when(r == 0)
        def _():
            o_ref[...] = jnp.zeros_like(o_ref)

        def add(x_ref, y_ref):
            for a in range(k // tk):
                xv = x_ref[:, a * tk:(a + 1) * tk]
                if x_is_slope:
                    xv = xv.astype(F32)
                    xv = (xv * xv * 0.25).astype(BF16)
                for b in range(n // tn):
                    o_ref[a, b] += lax.dot_general(xv, y_ref[:, b * tn:(b + 1) * tn], (((0,), (0,)), ((), ())),
                                                   preferred_element_type=F32)

        if rest:
            pl.when(r < whole)(lambda: add(refs[0], refs[2]))
            pl.when(r == whole)(lambda: add(refs[1], refs[3]))
        else:
            add(refs[0], refs[1])

        @pl.when(r == steps - 1)
        def _():
            ob_ref[...] = o_ref[...].astype(BF16)

    vm = pl.BlockSpec(memory_space=pltpu.VMEM)
    body, dep_specs = _behind(body, deps)
    return pl.pallas_call(
        body, name=name, grid=(steps,),
        in_specs=dep_specs + _row_split_specs(t, k) + _row_split_specs(t, n), out_specs=[vm, vm],
        out_shape=[jax.ShapeDtypeStruct((k // tk, n // tn, tk, tn), F32),
                   jax.ShapeDtypeStruct((k // tk, n // tn, tk, tn), BF16)],
        compiler_params=_params("arbitrary"),
    )(*deps, *([x] * (1 + bool(rest))), *([y] * (1 + bool(rest))))


def _mix_out_bwd(dh1, z, gp, w, o, b, c, hc, cw, ga, gc, tm, deps=()):
    t = dh1.shape[0]
    nt = t // tm
    per16 = tm // 16

    def body(dh1_ref, z_ref, gp_ref, w_ref, o_ref, b_ref, c_ref, hc_ref, cp_ref, hp_ref, cw_ref, ga_ref, gc_ref,
             dz_ref, do_ref, dbch_ref, dgp_ref, dga_ref, dgc_ref, dcw_ref, halo):
        i = pl.program_id(0)

        @pl.when(i == 0)
        def _():
            halo[...] = jnp.zeros_like(halo)
            dgp_ref[...] = jnp.zeros_like(dgp_ref)
            dga_ref[...] = jnp.zeros_like(dga_ref)
            dgc_ref[...] = jnp.zeros_like(dgc_ref)
            dcw_ref[...] = jnp.zeros_like(dcw_ref)

        dz, dgp = _rms_bwd(dh1_ref[...], z_ref[...], gp_ref[...])
        dgp_ref[...] += dgp
        dz = dz.astype(BF16)
        dz_ref[...] = dz
        dy = lax.dot_general(dz, w_ref[...].reshape(D_MODEL, D_MODEL), (((1,), (1,)), ((), ())),
                             preferred_element_type=F32)
        do, dga = _rms_bwd(dy[:, :ATTN_W], o_ref[...].astype(F32), ga_ref[...])
        do_ref[...] = do.astype(BF16)
        dga_ref[...] += dga

        cc, hh = c_ref[...].astype(F32), hc_ref[...].astype(F32)
        u = cc * hh
        first = i == nt - 1
        u_before = jnp.where(first, 0.0, (cp_ref[...].astype(F32) * hp_ref[...].astype(F32))[8:])
        u1 = _shift_rows(u, u_before, 1)
        u2 = _shift_rows(u, u_before, 2)
        cv = cw_ref[0:1, :] * u2 + cw_ref[1:2, :] * u1 + cw_ref[2:3, :] * u
        bb = b_ref[...].astype(F32)
        dyc, dgc = _rms_bwd(dy[:, ATTN_W:], bb * cv, gc_ref[...])
        dgc_ref[...] += dgc
        dcv = dyc * bb
        d1 = _advance_rows(dcv, halo[...], 1)
        d2 = _advance_rows(dcv, halo[...], 2)
        halo[...] = dcv[:8]
        du = cw_ref[2:3, :] * dcv + cw_ref[1:2, :] * d1 + cw_ref[0:1, :] * d2
        dbch_ref[...] = jnp.concatenate([dyc * cv, du * hh, du * cc], axis=1).astype(BF16)
        dcw_ref[...] += jnp.concatenate([jnp.sum(dcv * u2, axis=0, keepdims=True),
                                         jnp.sum(dcv * u1, axis=0, keepdims=True),
                                         jnp.sum(dcv * u, axis=0, keepdims=True)], axis=0)

    row = lambda n: pl.BlockSpec((tm, n), lambda i: (nt - 1 - i, 0))
    before = pl.BlockSpec((16, CONV_W), lambda i: (jnp.maximum((nt - 1 - i) * per16 - 1, 0), 0))
    full = lambda a: pl.BlockSpec(a.shape, lambda i: (0,) * a.ndim)
    vec = lambda n: pl.BlockSpec((1, n), lambda i: (0, 0))
    body, dep_specs = _behind(body, deps)
    return pl.pallas_call(
        body, name="mix_out_bwd", grid=(nt,),
        in_specs=dep_specs + [row(D_MODEL), row(D_MODEL), full(gp), full(w), row(ATTN_W), row(CONV_W), row(CONV_W),
                              row(CONV_W), before, before, full(cw), full(ga), full(gc)],
        out_specs=[row(D_MODEL), row(ATTN_W), row(3 * CONV_W), vec(D_MODEL), vec(ATTN_W), vec(CONV_W),
                   pl.BlockSpec((CONV_K, CONV_W), lambda i: (0, 0))],
        out_shape=[jax.ShapeDtypeStruct((t, D_MODEL), BF16), jax.ShapeDtypeStruct((t, ATTN_W), BF16),
                   jax.ShapeDtypeStruct((t, 3 * CONV_W), BF16), jax.ShapeDtypeStruct((1, D_MODEL), F32),
                   jax.ShapeDtypeStruct((1, ATTN_W), F32), jax.ShapeDtypeStruct((1, CONV_W), F32),
                   jax.ShapeDtypeStruct((CONV_K, CONV_W), F32)],
        scratch_shapes=[pltpu.VMEM((8, CONV_W), F32)],
        compiler_params=_params("arbitrary"),
    )(*deps, dh1, z, gp, w, o, b, c, hc, c, hc, cw, ga, gc)


def _attn_bwd(q, k, v, o, do, bias, sinks, tm, deps=()):
    t = q.shape[0]
    per_step = tm // BLOCK

    def body(s_ref, q_ref, k_ref, v_ref, o_ref, do_ref, bias_ref, dq_ref, dk_ref, dv_ref, ds_ref):
        step = pl.program_id(0)

        @pl.when(step == 0)
        def _():
            ds_ref[...] = jnp.zeros_like(ds_ref)

        heads = range(N_Q_HEADS)

        def first_matmuls(b):
            i = step * per_step + b
            rows = slice(b * BLOCK, (b + 1) * BLOCK)
            kc, vc = _two_blocks(k_ref, i), _two_blocks(v_ref, i)
            bias_i = bias_ref[jnp.minimum(i, 2)]
            kgs = [kc[:, _head(g)] for g in range(N_KV_HEADS)]
            vgs = [vc[:, _head(g)] for g in range(N_KV_HEADS)]
            qs = [q_ref[rows, _head(hh)] for hh in heads]
            dosb = [do_ref[rows, _head(hh)] for hh in heads]
            dos = [d.astype(F32) for d in dosb]
            scores = [_attn_scores(qs[hh], kgs[hh // GROUP], bias_i) for hh in heads]
            dps = [lax.dot_general(dosb[hh], vgs[hh // GROUP], (((1,), (1,)), ((), ())), preferred_element_type=F32)
                   for hh in heads]
            return kgs, qs, dos, dosb, scores, dps

        dsink = [jnp.zeros((BLOCK, 1), F32) for _ in range(N_Q_HEADS)]
        ahead = None
        for b in range(per_step):
            i = step * per_step + b
            rows = slice(b * BLOCK, (b + 1) * BLOCK)
            kgs, qs, dos, dosb, scores, dps = first_matmuls(b)
            ps, dss = [], []
            for hh in heads:
                p, share = _attn_probs(scores[hh], s_ref[hh])
                drow = jnp.sum(dos[hh] * o_ref[rows, _head(hh)].astype(F32), axis=-1, keepdims=True)
                dss.append((p * (dps[hh] - drow)).astype(BF16))
                ps.append(p.astype(BF16))
                dsink[hh] = dsink[hh] + share * drow
            for hh in heads:
                dq_ref[rows, _head(hh)] = (jnp.dot(dss[hh], kgs[hh // GROUP], preferred_element_type=F32)
                                           * SCALE).astype(BF16)
            groups = [slice(GROUP * g, GROUP * (g + 1)) for g in range(N_KV_HEADS)]
            dkg = [lax.dot_general(jnp.concatenate(dss[gr], axis=0), jnp.concatenate(qs[gr], axis=0),
                                   (((0,), (0,)), ((), ())), preferred_element_type=F32) for gr in groups]
            dvg = [lax.dot_general(jnp.concatenate(ps[gr], axis=0), jnp.concatenate(dosb[gr], axis=0),
                                   (((0,), (0,)), ((), ())), preferred_element_type=F32) for gr in groups]
            dkb, dvb = jnp.concatenate(dkg, axis=1), jnp.concatenate(dvg, axis=1)
            if b == 0:
                @pl.when(step > 0)
                def _():
                    before = pl.ds(pl.multiple_of((i - 1) * BLOCK, BLOCK), BLOCK)
                    dk_ref[before, :] += dkb[:BLOCK]
                    dv_ref[before, :] += dvb[:BLOCK]
            else:
                at = pl.ds(pl.multiple_of((i - 1) * BLOCK, BLOCK), BLOCK)
                dk_ref[at, :] = ahead[0] + dkb[:BLOCK]
                dv_ref[at, :] = ahead[1] + dvb[:BLOCK]
            ahead = (dkb[BLOCK:], dvb[BLOCK:])
        last = pl.ds(pl.multiple_of(((step + 1) * per_step - 1) * BLOCK, BLOCK), BLOCK)
        dk_ref[last, :] = ahead[0]
        dv_ref[last, :] = ahead[1]
        for hh in range(N_Q_HEADS):
            ds_ref[hh:hh + 1, :] -= jnp.sum(dsink[hh])

    whole = pl.BlockSpec((t, KV_W), lambda i: (0, 0))
    blk = pl.BlockSpec((tm, ATTN_W), lambda i: (i, 0))
    body, dep_specs = _behind(body, deps)
    return pl.pallas_call(
        body, name="attn_bwd", grid=(t // tm,),
        in_specs=dep_specs + [pl.BlockSpec(memory_space=pltpu.SMEM), blk, whole, whole, blk, blk,
                              pl.BlockSpec(bias.shape, lambda i: (0, 0, 0))],
        out_specs=[blk, whole, whole, pl.BlockSpec((N_Q_HEADS, 128), lambda i: (0, 0))],
        out_shape=[jax.ShapeDtypeStruct((t, ATTN_W), BF16), jax.ShapeDtypeStruct((t, KV_W), F32),
                   jax.ShapeDtypeStruct((t, KV_W), F32), jax.ShapeDtypeStruct((N_Q_HEADS, 128), F32)],
        compiler_params=_params("arbitrary"),
    )(*deps, sinks, q, k, v, o, do, bias)


def _in_proj_bwd(dq, dk, dv, dbch, w, dh1, h, g, tabs, tm, split_lead=False):
    t = h.shape[0]
    nt = t // tm

    def body(dq_ref, dk_ref, dv_ref, dbch_ref, w_ref, dh1_ref, h_ref, g_ref, c_ref, sa_ref, sb_ref, *rest):
        dp_ref, dg_ref = rest[2:4] if split_lead else rest[1:3]
        i = pl.program_id(0)

        @pl.when(i == 0)
        def _():
            dg_ref[...] = jnp.zeros_like(dg_ref)

        cos, sa, sb = c_ref[...], sa_ref[...], sb_ref[...]
        rep = ATTN_W // (2 * HEAD_DIM)
        dqr = _rope_bwd(dq_ref[...].astype(F32), jnp.tile(cos, (1, rep)), jnp.tile(sa, (1, rep)),
                        jnp.tile(sb, (1, rep)))
        dkr = _rope_bwd(dk_ref[...], cos, sa, sb)
        dp = jnp.concatenate([dqr.astype(BF16), dkr.astype(BF16), dv_ref[...].astype(BF16), dbch_ref[...]], axis=1)
        dp_ref[...] = dp
        da = jnp.dot(dp, w_ref[...], preferred_element_type=F32)
        dx, dg = _rms_bwd(da, h_ref[...], g_ref[...])
        dg_ref[...] += dg
        dh = dh1_ref[...] + dx
        if not split_lead:
            rest[0][...] = dh
            return
        lead_ref, seq_ref, stage, sems = rest[0], rest[1], rest[4], rest[5]

        def copy(j, slot, first):
            if first:
                return pltpu.make_async_copy(stage.at[slot, pl.ds(BLOCK, tm - BLOCK)],
                                             seq_ref.at[pl.ds(0, tm - BLOCK)], sems.at[slot])
            return pltpu.make_async_copy(stage.at[slot], seq_ref.at[pl.ds(pl.multiple_of(j * tm - BLOCK, BLOCK), tm)],
                                         sems.at[slot])

        slot = i % 2
        pl.when(i == 2)(lambda: copy(0, slot, True).wait())
        pl.when(i > 2)(lambda: copy(i - 2, slot, False).wait())
        stage[slot] = dh

        @pl.when(i == 0)
        def _():
            lead_ref[...] = dh[:BLOCK]
            copy(0, slot, True).start()

        pl.when(i > 0)(lambda: copy(i, slot, False).start())

        @pl.when(i == nt - 1)
        def _():
            for j in range(max(nt - 2, 0), nt):
                copy(j, j % 2, j == 0).wait()

    row = lambda n: pl.BlockSpec((tm, n), lambda i: (i, 0))
    full = lambda a: pl.BlockSpec(a.shape, lambda i: (0, 0))
    dh_specs, dh_shapes, scratch = [row(D_MODEL)], [jax.ShapeDtypeStruct((t, D_MODEL), F32)], []
    if split_lead:
        dh_specs = [pl.BlockSpec((BLOCK, D_MODEL), lambda i: (0, 0)), pl.BlockSpec(memory_space=pl.ANY)]
        dh_shapes = [jax.ShapeDtypeStruct((BLOCK, D_MODEL), F32), jax.ShapeDtypeStruct((t - BLOCK, D_MODEL), F32)]
        scratch = [pltpu.VMEM((2, tm, D_MODEL), F32), pltpu.SemaphoreType.DMA((2,))]
    outs = pl.pallas_call(
        body, name="in_proj_bwd", grid=(nt,),
        in_specs=[row(ATTN_W), row(KV_W), row(KV_W), row(3 * CONV_W), full(w), row(D_MODEL), row(D_MODEL), full(g),
                  row(2 * HEAD_DIM), row(2 * HEAD_DIM), row(2 * HEAD_DIM)],
        out_specs=dh_specs + [row(IN_W), pl.BlockSpec((1, D_MODEL), lambda i: (0, 0))],
        out_shape=dh_shapes + [jax.ShapeDtypeStruct((t, IN_W), BF16), jax.ShapeDtypeStruct((1, D_MODEL), F32)],
        scratch_shapes=scratch,
        compiler_params=_params("arbitrary"),
    )(dq, dk, dv, dbch, w, dh1, h, g, *tabs)
    return (tuple(outs[:2]) if split_lead else outs[0],) + tuple(outs[-2:])


class _Tiles:
    def __init__(self, t):
        self.tm = _row_tile(t, 640)
        self.ts = self.tm
        self.tabs = _rope_tables(t)
        self.bias = _attn_bias()


def _mixer_fwd(h, p, tl):
    a, q, k, v, b, c, hc = _in_proj(h, p["mix_pre_g"], p["w_in"], tl.tabs, tl.ts)
    o = _attn_fwd(q, k, v, tl.bias, p["sinks"], tl.tm)
    return (h, a, q, k, v, b, c, hc, o)


def _out_fwd(mixed, p, tl, deps=()):
    h, a, q, k, v, b, c, hc, o = mixed
    h1, y, z = _mix_out(h, o, b, c, hc, p["conv_w"], p["attn_out_g"], p["conv_out_g"], p["w_out"], p["mix_post_g"],
                        tl.ts, deps)
    return h1, mixed + (h1, y, z)


def _mlp_fwd(h1, saved, p, tl, target=None):
    h2, a2, slope, f = _mlp(h1, p["mlp_pre_g"], p["w_up"], p["w_down"], p["mlp_post_g"], tl.tm, target)
    return h2, saved + (a2, slope, f)


def _mlp_part_bwd(dh, saved, p, tl, deps=()):
    h1, a2, slope, f = saved[9], saved[12], saved[13], saved[14]
    df, dup, dg2 = _mlp_bwd_hidden(dh, f, p["mlp_post_g"], slope, p["w_down"], tl.tm, deps)
    dh1, dg1 = _mlp_bwd_input(dup, p["w_up"], h1, p["mlp_pre_g"], dh, tl.tm)
    g = {"w_down": [d.reshape(N_CHIPS, FF_CHUNK, D_MODEL)
                    for d in _weight_grad(slope, df, "grad_w_down", x_is_slope=True)],
         "w_up": [d.reshape(N_CHIPS, D_MODEL, FF_CHUNK) for d in _weight_grad(a2, dup, "grad_w_up")],
         "mlp_post_g": dg2, "mlp_pre_g": dg1}
    return dh1, g


def _mix_out_part_bwd(dh1, saved, p, tl, deps=()):
    b, c, hc, o, y, z = saved[5], saved[6], saved[7], saved[8], saved[10], saved[11]
    dz, do, dbch, dgp, dga, dgc, dcw = _mix_out_bwd(dh1, z, p["mix_post_g"], p["w_out"], o, b, c, hc, p["conv_w"],
                                                    p["attn_out_g"], p["conv_out_g"], tl.ts, deps)
    g = {"w_out": [d.reshape(N_CHIPS, D_MODEL // N_CHIPS, D_MODEL) for d in _weight_grad(y, dz, "grad_w_out")],
         "mix_post_g": dgp, "attn_out_g": dga, "conv_out_g": dgc, "conv_w": dcw}
    return (dh1, do, dbch), g


def _attn_in_part_bwd(carry, saved, p, tl, deps=(), split_lead=False):
    dh1, do, dbch = carry
    h_in, q, k, v, o = saved[0], saved[2], saved[3], saved[4], saved[8]
    dq, dk, dv, dsink = _attn_bwd(q, k, v, o, do, tl.bias, p["sinks"], tl.tm, deps)
    dh, dproj, dgi = _in_proj_bwd(dq, dk, dv, dbch, p["w_in"], dh1, h_in, p["mix_pre_g"], tl.tabs, tl.ts, split_lead)
    return dh, dproj, {"mix_pre_g": dgi, "sinks": dsink[:, 0]}


def _in_grad(dproj, saved, deps=()):
    return [d.reshape(N_CHIPS, IN_W // N_CHIPS, D_MODEL) for d in _weight_grad(dproj, saved[1], "grad_w_in", deps=deps)]


def _place():
    return lax.axis_index("x"), lax.axis_index("y"), lax.axis_index("c")


def _other_chips(x, y):
    return [(1 - x, y), (x, 1 - y), (1 - x, 1 - y)]


_HBM = pl.BlockSpec(memory_space=pltpu.HBM)
_SEM = pl.BlockSpec(memory_space=pltpu.SEMAPHORE)
_EFFECT = pltpu.SideEffectType.DATAFLOW_SIDE_EFFECTING


class _Exchange:
    def __init__(self, name, bufs, plan, n, after=()):
        self.name, self.plan, nb = name, plan, len(bufs)
        n_in = nb + len(after)

        def body(*refs):
            send, recv, token = refs[n_in], refs[n_in + 1], refs[-1]
            for k, (src, dst, target, _) in enumerate(plan(refs[:nb])):
                pltpu.make_async_remote_copy(src_ref=src, dst_ref=dst, send_sem=send.at[k], recv_sem=recv.at[k],
                                             device_id=target, device_id_type=MESH).start()
            token[...] = jnp.zeros_like(token)

        outs = pl.pallas_call(
            body, name=name + "_start",
            out_shape=(pltpu.SemaphoreType.DMA((n,)), pltpu.SemaphoreType.DMA((n,)),
                       *[pltpu.HBM(b.shape, b.dtype) for b in bufs], jax.ShapeDtypeStruct((8, 128), F32)),
            in_specs=[_HBM] * nb + [pl.BlockSpec(memory_space=pl.ANY)] * len(after),
            out_specs=(_SEM, _SEM, *[_HBM] * nb, pl.BlockSpec(memory_space=pltpu.VMEM)),
            input_output_aliases={i: 2 + i for i in range(nb)},
            compiler_params=pltpu.CompilerParams(has_side_effects=_EFFECT),
        )(*[pltpu.with_memory_space_constraint(b, pltpu.HBM) for b in bufs], *after)
        self.send, self.recv, self.bufs, self.token = outs[0], outs[1], list(outs[2:2 + nb]), outs[-1]

    def wait(self, *after):
        plan, nb = self.plan, len(self.bufs)

        def body(*refs):
            send, recv = refs[nb], refs[nb + 1]
            for k, (src, _, target, land) in enumerate(plan(refs[:nb])):
                cp = pltpu.make_async_remote_copy(src_ref=src, dst_ref=land, send_sem=send.at[k], recv_sem=recv.at[k],
                                                  device_id=target, device_id_type=MESH)
                cp.wait_send()
                cp.wait_recv()

        outs = pl.pallas_call(
            body, name=self.name + "_wait", out_shape=[pltpu.HBM(b.shape, b.dtype) for b in self.bufs],
            in_specs=[_HBM] * nb + [_SEM, _SEM] + [pl.BlockSpec(memory_space=pl.ANY)] * len(after),
            out_specs=[_HBM] * nb, input_output_aliases={i: i for i in range(nb)},
            compiler_params=pltpu.CompilerParams(has_side_effects=_EFFECT),
        )(*self.bufs, self.send, self.recv, *after)
        return list(outs)


def _gather_plan(n):
    def plan(refs):
        x, y, c = _place()
        me = 2 * x + y
        return [(refs[a].at[me], refs[a].at[me], (px, py, c), refs[a].at[2 * px + py])
                for a in range(n) for px, py in _other_chips(x, y)]

    return plan


def _peers():
    x, y, c = _place()
    return [(k - 1, (x ^ (k >> 2), y ^ ((k >> 1) & 1), c ^ (k & 1))) for k in range(1, N_DEV)]


def _scatter_plan(n, half_rows):
    def plan(refs):
        out = []
        for a in range(n):
            hr = half_rows[a]
            for k, (px, py, pc) in _peers():
                out.append((refs[a].at[2 * px + py, pl.ds(pc * hr, hr)], refs[n + a].at[k], (px, py, pc),
                            refs[n + a].at[k]))
        return out

    return plan


def _join_plan(n):
    def plan(refs):
        x, y, c = _place()
        return [(refs[a].at[c], refs[a].at[c], (x, y, 1 - c), refs[a].at[1 - c]) for a in range(n)]

    return plan


def _sum_parts(gs, qs):
    n = len(gs)
    half_rows = [g.shape[1] // 2 for g in gs]
    tr = [_block_rows(hr) for hr in half_rows]
    per = [hr // t for hr, t in zip(half_rows, tr)]
    x, y, c = _place()
    where = jnp.stack([2 * x + y, c]).astype(jnp.int32)

    def body(where_ref, *refs):
        i = pl.program_id(0)
        for a in range(n):
            g_ref, q_ref, o_ref = refs[a], refs[n + a], refs[2 * n + a]

            @pl.when(i < per[a])
            def _():
                total = g_ref[...]
                for k in range(N_DEV - 1):
                    total = total + q_ref[k].astype(F32)
                o_ref[...] = total

    def at(a, i):
        return jnp.minimum(i, per[a] - 1)

    specs_g = [pl.BlockSpec((None, tr[a], gs[a].shape[2]),
                            lambda i, where_ref, a=a: (where_ref[0], where_ref[1] * per[a] + at(a, i), 0)) for a in range(n)]
    specs_q = [pl.BlockSpec((N_DEV - 1, tr[a], gs[a].shape[2]), lambda i, where_ref, a=a: (0, at(a, i), 0))
               for a in range(n)]
    specs_o = [pl.BlockSpec((None, tr[a], gs[a].shape[2]), lambda i, where_ref, a=a: (where_ref[1], at(a, i), 0))
               for a in range(n)]
    return pl.pallas_call(
        body, name="sum_parts",
        grid_spec=pltpu.PrefetchScalarGridSpec(num_scalar_prefetch=1, grid=(max(per),), in_specs=specs_g + specs_q,
                                               out_specs=specs_o),
        out_shape=[jax.ShapeDtypeStruct((2, hr, g.shape[2]), F32) for g, hr in zip(gs, half_rows)],
        compiler_params=_params("arbitrary"),
    )(where, *gs, *qs)


def _all_plan(refs):
    x, y, c = _place()
    mine = refs[0].at[4 * x + 2 * y + c]
    return [(mine, mine, (px, py, pc), refs[0].at[4 * px + 2 * py + pc]) for _, (px, py, pc) in _peers()]


def _sum_devices(parts):
    def body(p_ref, o_ref):
        total = p_ref[0]
        for d in range(1, N_DEV):
            total = total + p_ref[d]
        o_ref[...] = total

    vm = pl.BlockSpec(memory_space=pltpu.VMEM)
    return pl.pallas_call(body, name="sum_devices", in_specs=[vm], out_specs=vm,
                          out_shape=jax.ShapeDtypeStruct(parts.shape[1:], F32))(parts)


def _adamw_math(w, g, m, v):
    m = ADAM_B1 * m + (1.0 - ADAM_B1) * g
    v = ADAM_B2 * v + (1.0 - ADAM_B2) * jnp.square(g)
    m_hat = m / (1.0 - ADAM_B1 ** ADAM_STEP)
    v_hat = v / (1.0 - ADAM_B2 ** ADAM_STEP)
    delta = -ADAM_LR * (m_hat / (jnp.sqrt(v_hat) + ADAM_EPS) + ADAM_WD * w)
    return delta, m, v


def _adamw_large(layer, ws, halves, ms, vs, others):
    n = len(ws)
    tr = [_block_rows(w.shape[1] // 2) for w in ws]
    per = [w.shape[1] // 2 // t for w, t in zip(ws, tr)]
    kept = [] if others is None else [a for four in others for a in four]

    def body(*refs):
        i = pl.program_id(0)
        outs = refs[4 * n + len(kept):]
        for a in range(n):
            w_ref, g_ref, m_ref, v_ref = refs[a], refs[n + a], refs[2 * n + a], refs[3 * n + a]
            g_out, d_ref, nm_ref, nv_ref = outs[4 * a:4 * a + 4]

            @pl.when(i < 2 * per[a])
            def _():
                g = g_ref[...]
                g_out[...] = g
                d_ref[...], nm_ref[...], nv_ref[...] = _adamw_math(w_ref[...], g, m_ref[...], v_ref[...])

    def at(a, i):
        return jnp.minimum(i, 2 * per[a] - 1)

    blk = [pl.BlockSpec((None, tr[a], ws[a].shape[2]), lambda i, a=a: (layer, at(a, i), 0)) for a in range(n)]
    half = [pl.BlockSpec((None, tr[a], ws[a].shape[2]), lambda i, a=a: (at(a, i) // per[a], at(a, i) % per[a], 0))
            for a in range(n)]
    outs = pl.pallas_call(
        body, name="adamw_large", grid=(2 * max(per),),
        in_specs=blk + half + blk + blk + [pl.BlockSpec(memory_space=pl.ANY)] * len(kept),
        out_specs=[blk[a] for a in range(n) for _ in range(4)],
        out_shape=[jax.ShapeDtypeStruct(w.shape, F32) for w in ws for _ in range(4)],
        input_output_aliases={4 * n + k: k for k in range(len(kept))},
        compiler_params=_params("arbitrary"),
    )(*ws, *halves, *ms, *vs, *kept)
    return [outs[4 * a:4 * a + 4] for a in range(n)]


def _adamw_small(ws, gs, ms, vs):
    n = len(ws)

    def body(*refs):
        w_r, g_r, m_r, v_r = refs[:n], refs[n:2 * n], refs[2 * n:3 * n], refs[3 * n:4 * n]
        d_r, nm_r, nv_r = refs[4 * n:5 * n], refs[5 * n:6 * n], refs[6 * n:]
        for a in range(n):
            d_r[a][...], nm_r[a][...], nv_r[a][...] = _adamw_math(w_r[a][...], g_r[a][...], m_r[a][...], v_r[a][...])

    vm = pl.BlockSpec(memory_space=pltpu.VMEM)
    outs = pl.pallas_call(
        body, name="adamw_small", in_specs=[vm] * (4 * n), out_specs=[vm] * (3 * n),
        out_shape=[jax.ShapeDtypeStruct(w.shape, F32) for w in ws] * 3,
    )(*ws, *gs, *ms, *vs)
    return outs[:n], outs[n:2 * n], outs[2 * n:]


_LARGE = ("w_in", "w_out", "w_up", "w_down")
_SMALL = ("meta_tokens", "mix_pre_g", "conv_w", "sinks", "attn_out_g", "conv_out_g", "mix_post_g", "mlp_pre_g",
          "mlp_post_g")
_ORDER = ("meta_tokens", "mix_pre_g", "w_in", "conv_w", "sinks", "attn_out_g", "conv_out_g", "w_out", "mix_post_g",
          "mlp_pre_g", "w_up", "w_down", "mlp_post_g")


class _Reduce:
    def __init__(self, name, grads, after=()):
        self.name, self.n = name, len(grads)
        self.own = [g for g, _ in grads]
        half_rows = [g.shape[1] // 2 for g in self.own]
        zones = [lax.empty((N_DEV - 1, hr, g.shape[2]), BF16) for g, hr in zip(self.own, half_rows)]
        self.exchange = _Exchange(name + "_scatter", [b for _, b in grads] + zones, _scatter_plan(self.n, half_rows),
                                  (N_DEV - 1) * self.n, after)

    @property
    def token(self):
        return self.exchange.token

    def join(self, *after):
        bufs = self.exchange.wait(*after)
        halves = list(_sum_parts(self.own, bufs[self.n:]))
        self.exchange = _Exchange(self.name + "_join", halves, _join_plan(self.n), self.n)

    def done(self, *after):
        return self.exchange.wait(*after)


def _pad_cols(a, n=D_MODEL):
    return jnp.pad(a, ((0, 0), (0, n - a.shape[1])))


def kernel(x, meta_tokens, mix_pre_g, w_in, conv_w, sinks, attn_out_g, conv_out_g, w_out, mix_post_g, mlp_pre_g, w_up, w_down, mlp_post_g, loss_target, m_meta_tokens, m_mix_pre_g, m_w_in, m_conv_w, m_sinks, m_attn_out_g, m_conv_out_g, m_w_out, m_mix_post_g, m_mlp_pre_g, m_w_up, m_w_down, m_mlp_post_g, v_meta_tokens, v_mix_pre_g, v_w_in, v_conv_w, v_sinks, v_attn_out_g, v_conv_out_g, v_w_out, v_mix_post_g, v_mlp_pre_g, v_w_up, v_w_down, v_mlp_post_g):
    w = dict(meta_tokens=meta_tokens, mix_pre_g=mix_pre_g, w_in=w_in, conv_w=conv_w, sinks=sinks,
             attn_out_g=attn_out_g, conv_out_g=conv_out_g, w_out=w_out, mix_post_g=mix_post_g, mlp_pre_g=mlp_pre_g,
             w_up=w_up, w_down=w_down, mlp_post_g=mlp_post_g)
    m = dict(meta_tokens=m_meta_tokens, mix_pre_g=m_mix_pre_g, w_in=m_w_in, conv_w=m_conv_w, sinks=m_sinks,
             attn_out_g=m_attn_out_g, conv_out_g=m_conv_out_g, w_out=m_w_out, mix_post_g=m_mix_post_g,
             mlp_pre_g=m_mlp_pre_g, w_up=m_w_up, w_down=m_w_down, mlp_post_g=m_mlp_post_g)
    v = dict(meta_tokens=v_meta_tokens, mix_pre_g=v_mix_pre_g, w_in=v_w_in, conv_w=v_conv_w, sinks=v_sinks,
             attn_out_g=v_attn_out_g, conv_out_g=v_conv_out_g, w_out=v_w_out, mix_post_g=v_mix_post_g,
             mlp_pre_g=v_mlp_pre_g, w_up=v_w_up, w_down=v_w_down, mlp_post_g=v_mlp_post_g)
    chip = 2 * lax.axis_index("x") + lax.axis_index("y")
    tl = _Tiles(x.shape[1] + BLOCK)

    def zone(quarter):
        return lax.dynamic_update_slice(lax.empty((N_CHIPS,) + quarter.shape, quarter.dtype), quarter[None],
                                        (chip,) + (0,) * quarter.ndim)

    w, m, v = ({**d, "w_in": jnp.swapaxes(d["w_in"], 1, 2)} for d in (w, m, v))
    zones = {n: [zone(w[n][l].astype(BF16)) for l in range(DEPTH)] for n in _LARGE}
    first = _Exchange("gather_first", [zones["w_in"][0], zone(w["conv_w"]), zone(w["meta_tokens"])], _gather_plan(3), 9)
    out0 = _Exchange("gather_out", [zones["w_out"][0]], _gather_plan(1), 3, [first.token])
    rest = _Exchange("gather_rest", [zones[n][0] for n in ("w_up", "w_down")], _gather_plan(2), 6, [out0.token])

    def whole_in(quarters):
        return quarters.reshape(IN_W, D_MODEL)

    h = jnp.concatenate([jnp.zeros((BLOCK, D_MODEL), F32), x[0]], axis=0)
    q_in, q_conv, q_meta = first.wait(rest.token, *tl.tabs, tl.bias, h)
    conv_whole = jnp.transpose(q_conv, (1, 2, 0, 3)).reshape(DEPTH, CONV_K, CONV_W)
    meta = jnp.transpose(q_meta, (1, 0, 2)).reshape(N_META, D_MODEL)
    p = [{"conv_w": conv_whole[l], "sinks": w["sinks"][l]} for l in range(DEPTH)]
    for l in range(DEPTH):
        for n in ("mix_pre_g", "attn_out_g", "conv_out_g", "mix_post_g", "mlp_pre_g", "mlp_post_g"):
            p[l][n] = w[n][l][None, :]

    h = lax.dynamic_update_slice(h, meta, (LEAD_PAD, 0))
    p[0]["w_in"] = whole_in(q_in)
    mixed = _mixer_fwd(h, p[0], tl)
    second = _Exchange("gather_second", [zones["w_in"][1], zones["w_out"][1]], _gather_plan(2), 6, [mixed[-1]])
    second_mlp = _Exchange("gather_second_mlp", [zones["w_up"][1], zones["w_down"][1]], _gather_plan(2), 6,
                           [second.token])
    p[0]["w_out"], = out0.wait(second_mlp.token)
    h1, saved0 = _out_fwd(mixed, p[0], tl)
    p[0]["w_up"], p[0]["w_down"] = rest.wait(h1)
    h, saved0 = _mlp_fwd(h1, saved0, p[0], tl)
    q_in, p[1]["w_out"] = second.wait(h)
    p[1]["w_in"] = whole_in(q_in)
    h1, saved1 = _out_fwd(_mixer_fwd(h, p[1], tl), p[1], tl)
    p[1]["w_up"], p[1]["w_down"] = second_mlp.wait(h1)
    (loss_tile, dh), saved1 = _mlp_fwd(h1, saved1, p[1], tl, loss_target[0])

    def adamw(layer, halves, other):
        names = list(halves)
        done = _adamw_large(layer, [w[n] for n in names], [halves[n] for n in names], [m[n] for n in names],
                            [v[n] for n in names], None if other is None else [other[n] for n in names])
        return dict(zip(names, done))

    dh1, g1 = _mlp_part_bwd(dh, saved1, p[1], tl)
    carry, gm = _mix_out_part_bwd(dh1, saved1, p[1], tl)
    dh, dproj, gi = _attn_in_part_bwd(carry, saved1, p[1], tl)
    g1.update(gm, w_in=_in_grad(dproj, saved1), **gi)
    red1 = _Reduce("reduce1", [g1[n] for n in _LARGE])
    dh1, g0 = _mlp_part_bwd(dh, saved0, p[0], tl, [red1.token])
    red1.join(g0["w_down"][0])
    carry, gm = _mix_out_part_bwd(dh1, saved0, p[0], tl, [red1.token])
    first0 = ("w_up", "w_down", "w_out")
    g0.update(gm)
    red0a = _Reduce("reduce0a", [g0[n] for n in first0])
    (dlead, dseq), dproj, gi = _attn_in_part_bwd(carry, saved0, p[0], tl, [red0a.token], split_lead=True)
    g0.update(gi)
    grad_x = dseq[None]
    grads = {n: [g0[n], g1[n]] for n in g0 if n not in _LARGE}

    rows = [dlead[LEAD_PAD:]]
    for n in ("mix_pre_g", "mix_post_g", "mlp_pre_g", "mlp_post_g"):
        rows += grads[n]
    rows += [jnp.concatenate([grads["attn_out_g"][l], grads["conv_out_g"][l]], axis=1) for l in range(DEPTH)]
    rows.append(jnp.concatenate(grads["conv_w"], axis=1))
    rows.append(_pad_cols(jnp.concatenate(grads["sinks"])[None, :]))
    rows.append(_pad_cols(loss_tile[:1]))
    packed = jnp.concatenate(rows, axis=0)
    packed = jnp.pad(packed, ((0, SMALL_ROWS - packed.shape[0]), (0, 0)))
    device = 2 * chip + lax.axis_index("c")
    small_parts = _Exchange("gather_small", [lax.dynamic_update_slice(lax.empty((N_DEV,) + packed.shape, F32),
                                                                      packed[None], (device, 0, 0))], _all_plan, N_DEV - 1)
    g0["w_in"] = _in_grad(dproj, saved0, [small_parts.token])
    red0b = _Reduce("reduce0b", [g0["w_in"]])
    done1 = adamw(1, dict(zip(_LARGE, red1.done(red0b.token))), None)
    total = _sum_devices(small_parts.wait(*[done1[n][0] for n in _LARGE])[0])
    r0 = N_META
    small = {
        "meta_tokens": lax.dynamic_slice(total[:N_META], (0, chip * (D_MODEL // N_CHIPS)), (N_META, D_MODEL // N_CHIPS)),
        "mix_pre_g": total[r0:r0 + 2], "mix_post_g": total[r0 + 2:r0 + 4], "mlp_pre_g": total[r0 + 4:r0 + 6],
        "mlp_post_g": total[r0 + 6:r0 + 8],
        "attn_out_g": total[r0 + 8:r0 + 10, :ATTN_W], "conv_out_g": total[r0 + 8:r0 + 10, ATTN_W:],
        "conv_w": lax.dynamic_slice(total[r0 + 10:r0 + 13].reshape(CONV_K, DEPTH, CONV_W).transpose(1, 0, 2),
                                    (0, 0, chip * (CONV_W // N_CHIPS)), (DEPTH, CONV_K, CONV_W // N_CHIPS)),
        "sinks": total[r0 + 13, :DEPTH * N_Q_HEADS].reshape(DEPTH, N_Q_HEADS),
    }
    loss = total[r0 + 14, 0]

    ds, nms, nvs = _adamw_small([w[n] for n in _SMALL], [small[n] for n in _SMALL], [m[n] for n in _SMALL],
                                [v[n] for n in _SMALL])
    red0a.join(ds[0], grad_x)
    red0b.join(red0a.token)
    done0 = adamw(0, dict(zip(first0, red0a.done(red0b.token))), done1)
    done0.update(adamw(0, {"w_in": red0b.done(done0["w_down"][0])[0]}, done1))
    grad, delta, new_m, new_v = {}, {}, {}, {}
    for n in _LARGE:
        grad[n], delta[n], new_m[n], new_v[n] = done0[n]
    for d in (grad, delta, new_m, new_v):
        d["w_in"] = jnp.swapaxes(d["w_in"], 1, 2)
    for i, n in enumerate(_SMALL):
        grad[n], delta[n], new_m[n], new_v[n] = small[n], ds[i], nms[i], nvs[i]
    return (loss, grad_x, *[grad[n] for n in _ORDER], *[delta[n] for n in _ORDER], *[new_m[n] for n in _ORDER],
            *[new_v[n] for n in _ORDER])
```

```python
import jax
import jax.numpy as jnp
from jax import lax
from jax.experimental import pallas as pl
from jax.experimental.pallas import tpu as pltpu

F32 = jnp.float32
BF16 = jnp.bfloat16

D_MODEL = 1024
DEPTH = 2
N_META = 16
ATTN_W = 512
CONV_W = 512
HEAD_DIM = 64
N_Q_HEADS = 8
N_KV_HEADS = 2
GROUP = N_Q_HEADS // N_KV_HEADS
KV_W = N_KV_HEADS * HEAD_DIM
CONV_K = 3
BLOCK = 128
LEAD_PAD = BLOCK - N_META
ROPE_THETA = 500000.0
ROT_DIM = HEAD_DIM // 4
ROT_HALF = ROT_DIM // 2
D_FF = 4 * D_MODEL
IN_W = ATTN_W + 2 * KV_W + 3 * CONV_W
QKV_W = ATTN_W + 2 * KV_W
EPS = 1e-6
SCALE = HEAD_DIM ** -0.5
FF_CHUNK = 1024
N_CHIPS = 4
N_DEV = 8

ADAM_LR = 0.001
ADAM_B1 = 0.9
ADAM_B2 = 0.999
ADAM_EPS = 1e-08
ADAM_WD = 0.01
ADAM_STEP = 10

V7X_VMEM_LIMIT = 60 * 1024 * 1024
SMALL_ROWS = 32

MESH = pl.DeviceIdType.MESH


def _params(*sem):
    return pltpu.CompilerParams(dimension_semantics=sem, vmem_limit_bytes=V7X_VMEM_LIMIT)


def _block_rows(n):
    return max(r for r in range(16, min(n, 64) + 1, 16) if n % r == 0)


def _row_tile(t, most):
    nb = t // BLOCK
    for b in range(most // BLOCK, 0, -1):
        if nb % b == 0:
            return b * BLOCK
    return BLOCK


def _behind(body, deps):
    n = len(deps)

    def wrapped(*refs):
        body(*refs[n:])

    return wrapped, [pl.BlockSpec(memory_space=pl.ANY)] * n


def _rms(x, g):
    r = lax.rsqrt(jnp.mean(x * x, axis=-1, keepdims=True) + EPS)
    return x * r * g


def _rms_bwd(dy, x, g):
    r = lax.rsqrt(jnp.mean(x * x, axis=-1, keepdims=True) + EPS)
    xh = x * r
    dg = jnp.sum(dy * xh, axis=0, keepdims=True)
    dxh = dy * g
    dx = r * (dxh - xh * jnp.mean(dxh * xh, axis=-1, keepdims=True))
    return dx, dg


def _rope(x, cos, sa, sb):
    n = x.shape[-1]
    return x * cos + pltpu.roll(x, n - ROT_HALF, 1) * sa + pltpu.roll(x, ROT_HALF, 1) * sb


def _rope_bwd(dy, cos, sa, sb):
    n = dy.shape[-1]
    return dy * cos + pltpu.roll(dy * sa, ROT_HALF, 1) + pltpu.roll(dy * sb, n - ROT_HALF, 1)


def _rope_tables(t):
    pos = lax.broadcasted_iota(jnp.int32, (t, ROT_HALF), 0).astype(F32) - LEAD_PAD
    pair = lax.broadcasted_iota(jnp.int32, (t, ROT_HALF), 1).astype(F32)
    inv_freq = jnp.power(jnp.float32(ROPE_THETA), -(2.0 * pair) / ROT_DIM)
    ang = pos * inv_freq
    cos, sin = lax.optimization_barrier((jnp.cos(ang), jnp.sin(ang)))
    spread = (1, 2 * HEAD_DIM // ROT_HALF)
    cos, sin = jnp.tile(cos, spread), jnp.tile(sin, spread)
    dim = lax.broadcasted_iota(jnp.int32, (t, 2 * HEAD_DIM), 1) % HEAD_DIM
    return jnp.where(dim < ROT_DIM, cos, 1.0), jnp.where(dim < ROT_DIM, sin, 0.0)


def _rope_factors(cos, sin):
    dim = lax.broadcasted_iota(jnp.int32, sin.shape, 1) % HEAD_DIM
    return cos, jnp.where(dim < ROT_HALF, -sin, 0.0), jnp.where(dim >= ROT_HALF, sin, 0.0)


def _in_proj(h, g, w, tabs, tm, lead=None):
    t = h.shape[0] + (0 if lead is None else BLOCK)
    per_step = 0 if lead is None else tm // BLOCK

    def body(*refs):
        if lead is None:
            x = refs[0][...]
            refs = refs[1:]
        else:
            blocks = [r[...] for r in refs[1:1 + per_step]]
            blocks[0] = jnp.where(pl.program_id(0) == 0, refs[0][...], blocks[0])
            x = jnp.concatenate(blocks, axis=0)
            first_out = 1 + per_step + 4
            refs[first_out][...] = x
            refs = refs[1 + per_step:first_out] + refs[first_out + 1:]
        g_ref, w_ref, c_ref, s_ref, a_ref, q_ref, k_ref, v_ref, b_ref, cg_ref, hc_ref = refs
        a = _rms(x, g_ref[...]).astype(BF16)
        a_ref[...] = a
        p = lax.dot_general(a, w_ref[...], (((1,), (1,)), ((), ())), preferred_element_type=F32)
        cos, sa, sb = _rope_factors(c_ref[...], s_ref[...])
        rep = ATTN_W // (2 * HEAD_DIM)
        q = _rope(p[:, :ATTN_W], jnp.tile(cos, (1, rep)), jnp.tile(sa, (1, rep)), jnp.tile(sb, (1, rep)))
        q_ref[...] = (q * SCALE).astype(BF16)
        k_ref[...] = _rope(p[:, ATTN_W:ATTN_W + KV_W], cos, sa, sb).astype(BF16)
        v_ref[...] = p[:, ATTN_W + KV_W:QKV_W].astype(BF16)
        b_ref[...] = p[:, QKV_W:QKV_W + CONV_W].astype(BF16)
        cg_ref[...] = p[:, QKV_W + CONV_W:QKV_W + 2 * CONV_W].astype(BF16)
        hc_ref[...] = p[:, QKV_W + 2 * CONV_W:].astype(BF16)

    row = lambda n: pl.BlockSpec((tm, n), lambda i: (i, 0))
    full = lambda a: pl.BlockSpec(a.shape, lambda i: (0, 0))

    def sequence_block(b):
        return pl.BlockSpec((BLOCK, D_MODEL), lambda i: (jnp.maximum(i * per_step + b - 1, 0), 0))

    if lead is None:
        first_in, first_args, first_out, first_shape = [row(D_MODEL)], [h], [], []
    else:
        first_in = [full(lead)] + [sequence_block(b) for b in range(per_step)]
        first_args = [lead] + [h] * per_step
        first_out, first_shape = [row(D_MODEL)], [jax.ShapeDtypeStruct((t, D_MODEL), F32)]
    return pl.pallas_call(
        body, name="in_proj", grid=(t // tm,),
        in_specs=first_in + [full(g), full(w), row(2 * HEAD_DIM), row(2 * HEAD_DIM)],
        out_specs=first_out + [row(D_MODEL), row(ATTN_W), row(KV_W), row(KV_W), row(CONV_W), row(CONV_W), row(CONV_W)],
        out_shape=first_shape + [jax.ShapeDtypeStruct((t, D_MODEL), BF16), jax.ShapeDtypeStruct((t, ATTN_W), BF16),
                                 jax.ShapeDtypeStruct((t, KV_W), BF16), jax.ShapeDtypeStruct((t, KV_W), BF16),
                                 jax.ShapeDtypeStruct((t, CONV_W), BF16), jax.ShapeDtypeStruct((t, CONV_W), BF16),
                                 jax.ShapeDtypeStruct((t, CONV_W), BF16)],
        compiler_params=_params("parallel"),
    )(*first_args, g, w, *tabs)


def _attn_bias():
    r = lax.broadcasted_iota(jnp.int32, (3, BLOCK, 2 * BLOCK), 1)
    c = lax.broadcasted_iota(jnp.int32, (3, BLOCK, 2 * BLOCK), 2)
    i = lax.broadcasted_iota(jnp.int32, (3, BLOCK, 2 * BLOCK), 0)
    ok = (c > r) & (c <= r + BLOCK) & (c + (i - 1) * BLOCK >= LEAD_PAD)
    return jnp.where(ok, 0.0, -jnp.inf).astype(F32)


def _attn_scores(qh, kg, bias):
    return lax.dot_general(qh, kg, (((1,), (1,)), ((), ())), preferred_element_type=F32) + bias


def _attn_probs(s, sk):
    m = jnp.maximum(jnp.max(s, axis=-1, keepdims=True), sk)
    e = jnp.exp(s - m)
    es = jnp.exp(sk - m)
    rden = 1.0 / (jnp.sum(e, axis=-1, keepdims=True) + es)
    return e * rden, es * rden


def _head(hh):
    return slice(hh * HEAD_DIM, (hh + 1) * HEAD_DIM)


def _two_blocks(ref, i):
    prev = jnp.maximum(i - 1, 0)
    return jnp.concatenate([ref[pl.ds(pl.multiple_of(prev * BLOCK, BLOCK), BLOCK), :],
                            ref[pl.ds(pl.multiple_of(i * BLOCK, BLOCK), BLOCK), :]], axis=0)


def _attn_fwd(q, k, v, bias, sinks, tm):
    t = q.shape[0]
    per_step = tm // BLOCK
    heads = range(N_Q_HEADS)

    def body(s_ref, q_ref, k_ref, v_ref, bias_ref, o_ref):
        for b in range(per_step):
            i = pl.program_id(0) * per_step + b
            rows = slice(b * BLOCK, (b + 1) * BLOCK)
            kc, vc = _two_blocks(k_ref, i), _two_blocks(v_ref, i)
            bias_i = bias_ref[jnp.minimum(i, 2)]
            scores = [_attn_scores(q_ref[rows, _head(hh)], kc[:, _head(hh // GROUP)], bias_i) for hh in heads]
            probs = [_attn_probs(scores[hh], s_ref[hh])[0].astype(BF16) for hh in heads]
            for hh in heads:
                o_ref[rows, _head(hh)] = jnp.dot(probs[hh], vc[:, _head(hh // GROUP)],
                                                 preferred_element_type=F32).astype(BF16)

    whole = pl.BlockSpec((t, KV_W), lambda i: (0, 0))
    return pl.pallas_call(
        body, name="attn_fwd", grid=(t // tm,),
        in_specs=[pl.BlockSpec(memory_space=pltpu.SMEM), pl.BlockSpec((tm, ATTN_W), lambda i: (i, 0)), whole, whole,
                  pl.BlockSpec(bias.shape, lambda i: (0, 0, 0))],
        out_specs=pl.BlockSpec((tm, ATTN_W), lambda i: (i, 0)),
        out_shape=jax.ShapeDtypeStruct((t, ATTN_W), BF16),
        compiler_params=_params("parallel"),
    )(sinks, q, k, v, bias)


def _shift_rows(u, halo, n):
    r = pltpu.roll(u, n, 0)
    hr = pltpu.roll(halo, n, 0)
    idx = lax.broadcasted_iota(jnp.int32, hr.shape, 0)
    return jnp.concatenate([jnp.where(idx < n, hr, r[:8]), r[8:]], axis=0)


def _advance_rows(u, halo, n):
    rows = u.shape[0]
    r = pltpu.roll(u, rows - n, 0)
    hr = pltpu.roll(halo, 8 - n, 0)
    idx = lax.broadcasted_iota(jnp.int32, hr.shape, 0)
    return jnp.concatenate([r[:rows - 8], jnp.where(idx >= 8 - n, hr, r[rows - 8:])], axis=0)


def _mix_out(h, o, b, c, hc, cw, ga, gc, w, gp, tm, deps=()):
    t = h.shape[0]

    def body(h_ref, o_ref, b_ref, c_ref, hc_ref, cw_ref, ga_ref, gc_ref, w_ref, gp_ref, h1_ref, y_ref, z_ref, halo):
        @pl.when(pl.program_id(0) == 0)
        def _():
            halo[...] = jnp.zeros_like(halo)

        u = c_ref[...].astype(F32) * hc_ref[...].astype(F32)
        cv = cw_ref[0:1, :] * _shift_rows(u, halo[...], 2) + cw_ref[1:2, :] * _shift_rows(u, halo[...], 1) \
            + cw_ref[2:3, :] * u
        halo[...] = u[tm - 8:]
        yc = b_ref[...].astype(F32) * cv
        y = jnp.concatenate([_rms(o_ref[...].astype(F32), ga_ref[...]), _rms(yc, gc_ref[...])], axis=1).astype(BF16)
        y_ref[...] = y
        z = jnp.dot(y, w_ref[...].reshape(D_MODEL, D_MODEL), preferred_element_type=F32)
        z_ref[...] = z
        h1_ref[...] = h_ref[...] + _rms(z, gp_ref[...])

    row = lambda n: pl.BlockSpec((tm, n), lambda i: (i, 0))
    full = lambda a: pl.BlockSpec(a.shape, lambda i: (0,) * a.ndim)
    body, dep_specs = _behind(body, deps)
    return pl.pallas_call(
        body, name="mix_out", grid=(t // tm,),
        in_specs=dep_specs + [row(D_MODEL), row(ATTN_W), row(CONV_W), row(CONV_W), row(CONV_W), full(cw), full(ga),
                              full(gc), full(w), full(gp)],
        out_specs=[row(D_MODEL), row(D_MODEL), row(D_MODEL)],
        out_shape=[jax.ShapeDtypeStruct((t, D_MODEL), F32), jax.ShapeDtypeStruct((t, D_MODEL), BF16),
                   jax.ShapeDtypeStruct((t, D_MODEL), F32)],
        scratch_shapes=[pltpu.VMEM((8, CONV_W), F32)],
        compiler_params=_params("arbitrary"),
    )(*deps, h, o, b, c, hc, cw, ga, gc, w, gp)


def _mlp(h1, g1, wu, wd, g2, tm, target=None):
    t = h1.shape[0]
    nj = D_FF // FF_CHUNK
    per_step = tm // BLOCK if target is not None else 0

    def body(h1_ref, g1_ref, wu_ref, wd_ref, g2_ref, *rest):
        t_refs, outs = rest[:per_step], rest[per_step:]
        a2_ref, slope_ref, f_ref = outs[-3:]
        a2 = _rms(h1_ref[...], g1_ref[...]).astype(BF16)
        a2_ref[...] = a2
        f = None
        for j in range(nj):
            up = jnp.dot(a2, wu_ref[j], preferred_element_type=F32)
            r = jnp.maximum(up, 0.0)
            slope_ref[:, j * FF_CHUNK:(j + 1) * FF_CHUNK] = (r + r).astype(BF16)
            part = jnp.dot((r * r).astype(BF16), wd_ref[j], preferred_element_type=F32)
            f = part if f is None else f + part
        f_ref[...] = f
        h2 = h1_ref[...] + _rms(f, g2_ref[...])
        if target is None:
            outs[0][...] = h2
            return
        loss_ref, dh_ref = outs[:2]
        i = pl.program_id(0)

        @pl.when(i == 0)
        def _():
            loss_ref[...] = jnp.zeros_like(loss_ref)

        total = jnp.zeros((), F32)
        for b in range(per_step):
            rows = slice(b * BLOCK, (b + 1) * BLOCK)
            err = h2[rows] - t_refs[b][...]
            if b == 0:
                err = jnp.where(i == 0, 0.0, err)
            dh_ref[rows, :] = err * (1.0 / D_MODEL)
            total = total + jnp.sum(err * err)
        loss_ref[...] += total * (0.5 / D_MODEL)

    def target_block(b):
        return pl.BlockSpec((BLOCK, D_MODEL), lambda i: (jnp.maximum(i * per_step + b - 1, 0), 0))

    row = pl.BlockSpec((tm, D_MODEL), lambda i: (i, 0))
    vec = pl.BlockSpec((1, D_MODEL), lambda i: (0, 0))
    resident = pl.BlockSpec(memory_space=pltpu.VMEM)
    first_specs, first_shapes = [row], [jax.ShapeDtypeStruct((t, D_MODEL), F32)]
    if target is not None:
        first_specs = [pl.BlockSpec((8, 128), lambda i: (0, 0)), row]
        first_shapes = [jax.ShapeDtypeStruct((8, 128), F32), jax.ShapeDtypeStruct((t, D_MODEL), F32)]
    outs = pl.pallas_call(
        body, name="mlp", grid=(t // tm,),
        in_specs=[row, vec, resident, resident, vec] + [target_block(b) for b in range(per_step)],
        out_specs=first_specs + [row, pl.BlockSpec((tm, D_FF), lambda i: (i, 0)), row],
        out_shape=first_shapes + [jax.ShapeDtypeStruct((t, D_MODEL), BF16), jax.ShapeDtypeStruct((t, D_FF), BF16),
                                  jax.ShapeDtypeStruct((t, D_MODEL), F32)],
        compiler_params=_params("parallel" if target is None else "arbitrary"),
    )(h1, g1, wu, wd, g2, *([target] * per_step))
    return (outs[0] if target is None else tuple(outs[:2]),) + tuple(outs[-3:])


def _mlp_bwd_hidden(dh2, f, g2, slope, wd, tm, deps=()):
    t = dh2.shape[0]
    nj = D_FF // FF_CHUNK

    def body(dh2_ref, f_ref, g2_ref, slope_ref, wd_ref, df_ref, dup_ref, dg2_ref):
        @pl.when(pl.program_id(0) == 0)
        def _():
            dg2_ref[...] = jnp.zeros_like(dg2_ref)

        df, dg = _rms_bwd(dh2_ref[...], f_ref[...], g2_ref[...])
        dg2_ref[...] += dg
        df = df.astype(BF16)
        df_ref[...] = df
        for j in range(nj):
            cols = slice(j * FF_CHUNK, (j + 1) * FF_CHUNK)
            dact = lax.dot_general(df, wd_ref[j], (((1,), (1,)), ((), ())), preferred_element_type=F32)
            dup_ref[:, cols] = (dact * slope_ref[:, cols].astype(F32)).astype(BF16)

    row = pl.BlockSpec((tm, D_MODEL), lambda i: (i, 0))
    wide = pl.BlockSpec((tm, D_FF), lambda i: (i, 0))
    vec = pl.BlockSpec((1, D_MODEL), lambda i: (0, 0))
    body, dep_specs = _behind(body, deps)
    return pl.pallas_call(
        body, name="mlp_bwd_hidden", grid=(t // tm,),
        in_specs=dep_specs + [row, row, vec, wide, pl.BlockSpec(memory_space=pltpu.VMEM)],
        out_specs=[row, wide, vec],
        out_shape=[jax.ShapeDtypeStruct((t, D_MODEL), BF16), jax.ShapeDtypeStruct((t, D_FF), BF16),
                   jax.ShapeDtypeStruct((1, D_MODEL), F32)],
        compiler_params=_params("arbitrary"),
    )(*deps, dh2, f, g2, slope, wd)


def _mlp_bwd_input(dup, wu, h1, g1, dh2, tm):
    t = dh2.shape[0]
    nj = D_FF // FF_CHUNK

    def body(dup_ref, wu_ref, h1_ref, g1_ref, dh2_ref, dh1_ref, dg1_ref):
        @pl.when(pl.program_id(0) == 0)
        def _():
            dg1_ref[...] = jnp.zeros_like(dg1_ref)

        da2 = None
        for j in range(nj):
            part = lax.dot_general(dup_ref[:, j * FF_CHUNK:(j + 1) * FF_CHUNK], wu_ref[j], (((1,), (1,)), ((), ())),
                                   preferred_element_type=F32)
            da2 = part if da2 is None else da2 + part
        dx, dg = _rms_bwd(da2, h1_ref[...], g1_ref[...])
        dh1_ref[...] = dh2_ref[...] + dx
        dg1_ref[...] += dg

    row = pl.BlockSpec((tm, D_MODEL), lambda i: (i, 0))
    vec = pl.BlockSpec((1, D_MODEL), lambda i: (0, 0))
    return pl.pallas_call(
        body, name="mlp_bwd_input", grid=(t // tm,),
        in_specs=[pl.BlockSpec((tm, D_FF), lambda i: (i, 0)), pl.BlockSpec(memory_space=pltpu.VMEM), row, vec, row],
        out_specs=[row, vec],
        out_shape=[jax.ShapeDtypeStruct((t, D_MODEL), F32), jax.ShapeDtypeStruct((1, D_MODEL), F32)],
        compiler_params=_params("arbitrary"),
    )(dup, wu, h1, g1, dh2)


def _row_split(t):
    tile = min(t, 1024)
    return tile, t // tile, t % tile


def _row_split_specs(t, cols):
    tile, whole, rest = _row_split(t)
    specs = [pl.BlockSpec((tile, cols), lambda r: (jnp.minimum(r, whole - 1), 0))]
    if rest:
        specs.append(pl.BlockSpec((rest, cols), lambda r: (whole * tile // rest, 0)))
    return specs


def _weight_grad(x, y, name, x_is_slope=False, deps=()):
    t, k = x.shape
    n = y.shape[1]
    tn = FF_CHUNK
    tk = FF_CHUNK if k % FF_CHUNK == 0 else k
    _, whole, rest = _row_split(t)
    steps = whole + bool(rest)

    def body(*refs):
        o_ref, ob_ref, r = refs[-2], refs[-1], pl.program_id(0)

        @pl.when(r == 0)
        def _():
            o_ref[...] = jnp.zeros_like(o_ref)

        def add(x_ref, y_ref):
            for a in range(k // tk):
                xv = x_ref[:, a * tk:(a + 1) * tk]
                if x_is_slope:
                    xv = xv.astype(F32)
                    xv = (xv * xv * 0.25).astype(BF16)
                for b in range(n // tn):
                    o_ref[a, b] += lax.dot_general(xv, y_ref[:, b * tn:(b + 1) * tn], (((0,), (0,)), ((), ())),
                                                   preferred_element_type=F32)

        if rest:
            pl.when(r < whole)(lambda: add(refs[0], refs[2]))
            pl.when(r == whole)(lambda: add(refs[1], refs[3]))
        else:
            add(refs[0], refs[1])

        @pl.when(r == steps - 1)
        def _():
            ob_ref[...] = o_ref[...].astype(BF16)

    vm = pl.BlockSpec(memory_space=pltpu.VMEM)
    body, dep_specs = _behind(body, deps)
    return pl.pallas_call(
        body, name=name, grid=(steps,),
        in_specs=dep_specs + _row_split_specs(t, k) + _row_split_specs(t, n), out_specs=[vm, vm],
        out_shape=[jax.ShapeDtypeStruct((k // tk, n // tn, tk, tn), F32),
                   jax.ShapeDtypeStruct((k // tk, n // tn, tk, tn), BF16)],
        compiler_params=_params("arbitrary"),
    )(*deps, *([x] * (1 + bool(rest))), *([y] * (1 + bool(rest))))


def _mix_out_bwd(dh1, z, gp, w, o, b, c, hc, cw, ga, gc, tm, deps=()):
    t = dh1.shape[0]
    nt = t // tm
    per16 = tm // 16

    def body(dh1_ref, z_ref, gp_ref, w_ref, o_ref, b_ref, c_ref, hc_ref, cp_ref, hp_ref, cw_ref, ga_ref, gc_ref,
             dz_ref, do_ref, dbch_ref, dgp_ref, dga_ref, dgc_ref, dcw_ref, halo):
        i = pl.program_id(0)

        @pl.when(i == 0)
        def _():
            halo[...] = jnp.zeros_like(halo)
            dgp_ref[...] = jnp.zeros_like(dgp_ref)
            dga_ref[...] = jnp.zeros_like(dga_ref)
            dgc_ref[...] = jnp.zeros_like(dgc_ref)
            dcw_ref[...] = jnp.zeros_like(dcw_ref)

        dz, dgp = _rms_bwd(dh1_ref[...], z_ref[...], gp_ref[...])
        dgp_ref[...] += dgp
        dz = dz.astype(BF16)
        dz_ref[...] = dz
        dy = lax.dot_general(dz, w_ref[...].reshape(D_MODEL, D_MODEL), (((1,), (1,)), ((), ())),
                             preferred_element_type=F32)
        do, dga = _rms_bwd(dy[:, :ATTN_W], o_ref[...].astype(F32), ga_ref[...])
        do_ref[...] = do.astype(BF16)
        dga_ref[...] += dga

        cc, hh = c_ref[...].astype(F32), hc_ref[...].astype(F32)
        u = cc * hh
        first = i == nt - 1
        u_before = jnp.where(first, 0.0, (cp_ref[...].astype(F32) * hp_ref[...].astype(F32))[8:])
        u1 = _shift_rows(u, u_before, 1)
        u2 = _shift_rows(u, u_before, 2)
        cv = cw_ref[0:1, :] * u2 + cw_ref[1:2, :] * u1 + cw_ref[2:3, :] * u
        bb = b_ref[...].astype(F32)
        dyc, dgc = _rms_bwd(dy[:, ATTN_W:], bb * cv, gc_ref[...])
        dgc_ref[...] += dgc
        dcv = dyc * bb
        d1 = _advance_rows(dcv, halo[...], 1)
        d2 = _advance_rows(dcv, halo[...], 2)
        halo[...] = dcv[:8]
        du = cw_ref[2:3, :] * dcv + cw_ref[1:2, :] * d1 + cw_ref[0:1, :] * d2
        dbch_ref[...] = jnp.concatenate([dyc * cv, du * hh, du * cc], axis=1).astype(BF16)
        dcw_ref[...] += jnp.concatenate([jnp.sum(dcv * u2, axis=0, keepdims=True),
                                         jnp.sum(dcv * u1, axis=0, keepdims=True),
                                         jnp.sum(dcv * u, axis=0, keepdims=True)], axis=0)

    row = lambda n: pl.BlockSpec((tm, n), lambda i: (nt - 1 - i, 0))
    before = pl.BlockSpec((16, CONV_W), lambda i: (jnp.maximum((nt - 1 - i) * per16 - 1, 0), 0))
    full = lambda a: pl.BlockSpec(a.shape, lambda i: (0,) * a.ndim)
    vec = lambda n: pl.BlockSpec((1, n), lambda i: (0, 0))
    body, dep_specs = _behind(body, deps)
    return pl.pallas_call(
        body, name="mix_out_bwd", grid=(nt,),
        in_specs=dep_specs + [row(D_MODEL), row(D_MODEL), full(gp), full(w), row(ATTN_W), row(CONV_W), row(CONV_W),
                              row(CONV_W), before, before, full(cw), full(ga), full(gc)],
        out_specs=[row(D_MODEL), row(ATTN_W), row(3 * CONV_W), vec(D_MODEL), vec(ATTN_W), vec(CONV_W),
                   pl.BlockSpec((CONV_K, CONV_W), lambda i: (0, 0))],
        out_shape=[jax.ShapeDtypeStruct((t, D_MODEL), BF16), jax.ShapeDtypeStruct((t, ATTN_W), BF16),
                   jax.ShapeDtypeStruct((t, 3 * CONV_W), BF16), jax.ShapeDtypeStruct((1, D_MODEL), F32),
                   jax.ShapeDtypeStruct((1, ATTN_W), F32), jax.ShapeDtypeStruct((1, CONV_W), F32),
                   jax.ShapeDtypeStruct((CONV_K, CONV_W), F32)],
        scratch_shapes=[pltpu.VMEM((8, CONV_W), F32)],
        compiler_params=_params("arbitrary"),
    )(*deps, dh1, z, gp, w, o, b, c, hc, c, hc, cw, ga, gc)


def _attn_bwd(q, k, v, o, do, bias, sinks, tm, deps=()):
    t = q.shape[0]
    per_step = tm // BLOCK

    def body(s_ref, q_ref, k_ref, v_ref, o_ref, do_ref, bias_ref, dq_ref, dk_ref, dv_ref, ds_ref):
        step = pl.program_id(0)

        @pl.when(step == 0)
        def _():
            ds_ref[...] = jnp.zeros_like(ds_ref)

        heads = range(N_Q_HEADS)

        def first_matmuls(b):
            i = step * per_step + b
            rows = slice(b * BLOCK, (b + 1) * BLOCK)
            kc, vc = _two_blocks(k_ref, i), _two_blocks(v_ref, i)
            bias_i = bias_ref[jnp.minimum(i, 2)]
            kgs = [kc[:, _head(g)] for g in range(N_KV_HEADS)]
            vgs = [vc[:, _head(g)] for g in range(N_KV_HEADS)]
            qs = [q_ref[rows, _head(hh)] for hh in heads]
            dosb = [do_ref[rows, _head(hh)] for hh in heads]
            dos = [d.astype(F32) for d in dosb]
            scores = [_attn_scores(qs[hh], kgs[hh // GROUP], bias_i) for hh in heads]
            dps = [lax.dot_general(dosb[hh], vgs[hh // GROUP], (((1,), (1,)), ((), ())), preferred_element_type=F32)
                   for hh in heads]
            return kgs, qs, dos, dosb, scores, dps

        dsink = [jnp.zeros((BLOCK, 1), F32) for _ in range(N_Q_HEADS)]
        ahead = None
        for b in range(per_step):
            i = step * per_step + b
            rows = slice(b * BLOCK, (b + 1) * BLOCK)
            kgs, qs, dos, dosb, scores, dps = first_matmuls(b)
            ps, dss = [], []
            for hh in heads:
                p, share = _attn_probs(scores[hh], s_ref[hh])
                drow = jnp.sum(dos[hh] * o_ref[rows, _head(hh)].astype(F32), axis=-1, keepdims=True)
                dss.append((p * (dps[hh] - drow)).astype(BF16))
                ps.append(p.astype(BF16))
                dsink[hh] = dsink[hh] + share * drow
            for hh in heads:
                dq_ref[rows, _head(hh)] = (jnp.dot(dss[hh], kgs[hh // GROUP], preferred_element_type=F32)
                                           * SCALE).astype(BF16)
            groups = [slice(GROUP * g, GROUP * (g + 1)) for g in range(N_KV_HEADS)]
            dkg = [lax.dot_general(jnp.concatenate(dss[gr], axis=0), jnp.concatenate(qs[gr], axis=0),
                                   (((0,), (0,)), ((), ())), preferred_element_type=F32) for gr in groups]
            dvg = [lax.dot_general(jnp.concatenate(ps[gr], axis=0), jnp.concatenate(dosb[gr], axis=0),
                                   (((0,), (0,)), ((), ())), preferred_element_type=F32) for gr in groups]
            dkb, dvb = jnp.concatenate(dkg, axis=1), jnp.concatenate(dvg, axis=1)
            if b == 0:
                @pl.when(step > 0)
                def _():
                    before = pl.ds(pl.multiple_of((i - 1) * BLOCK, BLOCK), BLOCK)
                    dk_ref[before, :] += dkb[:BLOCK]
                    dv_ref[before, :] += dvb[:BLOCK]
            else:
                at = pl.ds(pl.multiple_of((i - 1) * BLOCK, BLOCK), BLOCK)
                dk_ref[at, :] = ahead[0] + dkb[:BLOCK]
                dv_ref[at, :] = ahead[1] + dvb[:BLOCK]
            ahead = (dkb[BLOCK:], dvb[BLOCK:])
        last = pl.ds(pl.multiple_of(((step + 1) * per_step - 1) * BLOCK, BLOCK), BLOCK)
        dk_ref[last, :] = ahead[0]
        dv_ref[last, :] = ahead[1]
        for hh in range(N_Q_HEADS):
            ds_ref[hh:hh + 1, :] -= jnp.sum(dsink[hh])

    whole = pl.BlockSpec((t, KV_W), lambda i: (0, 0))
    blk = pl.BlockSpec((tm, ATTN_W), lambda i: (i, 0))
    body, dep_specs = _behind(body, deps)
    return pl.pallas_call(
        body, name="attn_bwd", grid=(t // tm,),
        in_specs=dep_specs + [pl.BlockSpec(memory_space=pltpu.SMEM), blk, whole, whole, blk, blk,
                              pl.BlockSpec(bias.shape, lambda i: (0, 0, 0))],
        out_specs=[blk, whole, whole, pl.BlockSpec((N_Q_HEADS, 128), lambda i: (0, 0))],
        out_shape=[jax.ShapeDtypeStruct((t, ATTN_W), BF16), jax.ShapeDtypeStruct((t, KV_W), F32),
                   jax.ShapeDtypeStruct((t, KV_W), F32), jax.ShapeDtypeStruct((N_Q_HEADS, 128), F32)],
        compiler_params=_params("arbitrary"),
    )(*deps, sinks, q, k, v, o, do, bias)


def _in_proj_bwd(dq, dk, dv, dbch, w, dh1, h, g, tabs, tm, split_lead=False):
    t = h.shape[0]
    nt = t // tm

    def body(dq_ref, dk_ref, dv_ref, dbch_ref, w_ref, dh1_ref, h_ref, g_ref, c_ref, s_ref, *rest):
        dp_ref, dg_ref = rest[2:4] if split_lead else rest[1:3]
        i = pl.program_id(0)

        @pl.when(i == 0)
        def _():
            dg_ref[...] = jnp.zeros_like(dg_ref)

        cos, sa, sb = _rope_factors(c_ref[...], s_ref[...])
        rep = ATTN_W // (2 * HEAD_DIM)
        dqr = _rope_bwd(dq_ref[...].astype(F32), jnp.tile(cos, (1, rep)), jnp.tile(sa, (1, rep)),
                        jnp.tile(sb, (1, rep)))
        dkr = _rope_bwd(dk_ref[...], cos, sa, sb)
        dp = jnp.concatenate([dqr.astype(BF16), dkr.astype(BF16), dv_ref[...].astype(BF16), dbch_ref[...]], axis=1)
        dp_ref[...] = dp
        da = jnp.dot(dp, w_ref[...], preferred_element_type=F32)
        dx, dg = _rms_bwd(da, h_ref[...], g_ref[...])
        dg_ref[...] += dg
        dh = dh1_ref[...] + dx
        if not split_lead:
            rest[0][...] = dh
            return
        lead_ref, seq_ref, stage, sems = rest[0], rest[1], rest[4], rest[5]

        def copy(j, slot, first):
            if first:
                return pltpu.make_async_copy(stage.at[slot, pl.ds(BLOCK, tm - BLOCK)],
                                             seq_ref.at[pl.ds(0, tm - BLOCK)], sems.at[slot])
            return pltpu.make_async_copy(stage.at[slot], seq_ref.at[pl.ds(pl.multiple_of(j * tm - BLOCK, BLOCK), tm)],
                                         sems.at[slot])

        slot = i % 2
        pl.when(i == 2)(lambda: copy(0, slot, True).wait())
        pl.when(i > 2)(lambda: copy(i - 2, slot, False).wait())
        stage[slot] = dh

        @pl.when(i == 0)
        def _():
            lead_ref[...] = dh[:BLOCK]
            copy(0, slot, True).start()

        pl.when(i > 0)(lambda: copy(i, slot, False).start())

        @pl.when(i == nt - 1)
        def _():
            for j in range(max(nt - 2, 0), nt):
                copy(j, j % 2, j == 0).wait()

    row = lambda n: pl.BlockSpec((tm, n), lambda i: (i, 0))
    full = lambda a: pl.BlockSpec(a.shape, lambda i: (0, 0))
    dh_specs, dh_shapes, scratch = [row(D_MODEL)], [jax.ShapeDtypeStruct((t, D_MODEL), F32)], []
    if split_lead:
        dh_specs = [pl.BlockSpec((BLOCK, D_MODEL), lambda i: (0, 0)), pl.BlockSpec(memory_space=pl.ANY)]
        dh_shapes = [jax.ShapeDtypeStruct((BLOCK, D_MODEL), F32), jax.ShapeDtypeStruct((t - BLOCK, D_MODEL), F32)]
        scratch = [pltpu.VMEM((2, tm, D_MODEL), F32), pltpu.SemaphoreType.DMA((2,))]
    outs = pl.pallas_call(
        body, name="in_proj_bwd", grid=(nt,),
        in_specs=[row(ATTN_W), row(KV_W), row(KV_W), row(3 * CONV_W), full(w), row(D_MODEL), row(D_MODEL), full(g),
                  row(2 * HEAD_DIM), row(2 * HEAD_DIM)],
        out_specs=dh_specs + [row(IN_W), pl.BlockSpec((1, D_MODEL), lambda i: (0, 0))],
        out_shape=dh_shapes + [jax.ShapeDtypeStruct((t, IN_W), BF16), jax.ShapeDtypeStruct((1, D_MODEL), F32)],
        scratch_shapes=scratch,
        compiler_params=_params("arbitrary"),
    )(dq, dk, dv, dbch, w, dh1, h, g, *tabs)
    return (tuple(outs[:2]) if split_lead else outs[0],) + tuple(outs[-2:])


class _Tiles:
    def __init__(self, t):
        self.tm = _row_tile(t, 640)
        self.ts = self.tm
        self.tabs = _rope_tables(t)
        self.bias = _attn_bias()


def _mixer_fwd(h, p, tl, lead=None):
    if lead is None:
        a, q, k, v, b, c, hc = _in_proj(h, p["mix_pre_g"], p["w_in"], tl.tabs, tl.ts)
    else:
        h, a, q, k, v, b, c, hc = _in_proj(h, p["mix_pre_g"], p["w_in"], tl.tabs, tl.ts, lead)
    o = _attn_fwd(q, k, v, tl.bias, p["sinks"], tl.tm)
    return (h, a, q, k, v, b, c, hc, o)


def _out_fwd(mixed, p, tl, deps=()):
    h, a, q, k, v, b, c, hc, o = mixed
    h1, y, z = _mix_out(h, o, b, c, hc, p["conv_w"], p["attn_out_g"], p["conv_out_g"], p["w_out"], p["mix_post_g"],
                        tl.ts, deps)
    return h1, mixed + (h1, y, z)


def _mlp_fwd(h1, saved, p, tl, target=None):
    h2, a2, slope, f = _mlp(h1, p["mlp_pre_g"], p["w_up"], p["w_down"], p["mlp_post_g"], tl.tm, target)
    return h2, saved + (a2, slope, f)


def _mlp_part_bwd(dh, saved, p, tl, deps=()):
    h1, a2, slope, f = saved[9], saved[12], saved[13], saved[14]
    df, dup, dg2 = _mlp_bwd_hidden(dh, f, p["mlp_post_g"], slope, p["w_down"], tl.tm, deps)
    dh1, dg1 = _mlp_bwd_input(dup, p["w_up"], h1, p["mlp_pre_g"], dh, tl.tm)
    g = {"w_down": [d.reshape(N_CHIPS, FF_CHUNK, D_MODEL)
                    for d in _weight_grad(slope, df, "grad_w_down", x_is_slope=True)],
         "w_up": [d.reshape(N_CHIPS, D_MODEL, FF_CHUNK) for d in _weight_grad(a2, dup, "grad_w_up")],
         "mlp_post_g": dg2, "mlp_pre_g": dg1}
    return dh1, g


def _mix_out_part_bwd(dh1, saved, p, tl, deps=()):
    b, c, hc, o, y, z = saved[5], saved[6], saved[7], saved[8], saved[10], saved[11]
    dz, do, dbch, dgp, dga, dgc, dcw = _mix_out_bwd(dh1, z, p["mix_post_g"], p["w_out"], o, b, c, hc, p["conv_w"],
                                                    p["attn_out_g"], p["conv_out_g"], tl.ts, deps)
    g = {"w_out": [d.reshape(N_CHIPS, D_MODEL // N_CHIPS, D_MODEL) for d in _weight_grad(y, dz, "grad_w_out")],
         "mix_post_g": dgp, "attn_out_g": dga, "conv_out_g": dgc, "conv_w": dcw}
    return (dh1, do, dbch), g


def _attn_in_part_bwd(carry, saved, p, tl, deps=(), split_lead=False):
    dh1, do, dbch = carry
    h_in, q, k, v, o = saved[0], saved[2], saved[3], saved[4], saved[8]
    dq, dk, dv, dsink = _attn_bwd(q, k, v, o, do, tl.bias, p["sinks"], tl.tm, deps)
    dh, dproj, dgi = _in_proj_bwd(dq, dk, dv, dbch, p["w_in"], dh1, h_in, p["mix_pre_g"], tl.tabs, tl.ts, split_lead)
    return dh, dproj, {"mix_pre_g": dgi, "sinks": dsink[:, 0]}


def _in_grad(dproj, saved, deps=()):
    return [d.reshape(N_CHIPS, IN_W // N_CHIPS, D_MODEL) for d in _weight_grad(dproj, saved[1], "grad_w_in", deps=deps)]


def _place():
    return lax.axis_index("x"), lax.axis_index("y"), lax.axis_index("c")


def _other_chips(x, y):
    return [(1 - x, y), (x, 1 - y), (1 - x, 1 - y)]


_HBM = pl.BlockSpec(memory_space=pltpu.HBM)
_SEM = pl.BlockSpec(memory_space=pltpu.SEMAPHORE)
_EFFECT = pltpu.SideEffectType.DATAFLOW_SIDE_EFFECTING


class _Exchange:
    def __init__(self, name, bufs, plan, n, after=()):
        self.name, self.plan, nb = name, plan, len(bufs)
        n_in = nb + len(after)

        def body(*refs):
            send, recv, token = refs[n_in], refs[n_in + 1], refs[-1]
            for k, (src, dst, target, _) in enumerate(plan(refs[:nb])):
                pltpu.make_async_remote_copy(src_ref=src, dst_ref=dst, send_sem=send.at[k], recv_sem=recv.at[k],
                                             device_id=target, device_id_type=MESH).start()
            token[...] = jnp.zeros_like(token)

        outs = pl.pallas_call(
            body, name=name + "_start",
            out_shape=(pltpu.SemaphoreType.DMA((n,)), pltpu.SemaphoreType.DMA((n,)),
                       *[pltpu.HBM(b.shape, b.dtype) for b in bufs], jax.ShapeDtypeStruct((8, 128), F32)),
            in_specs=[_HBM] * nb + [pl.BlockSpec(memory_space=pl.ANY)] * len(after),
            out_specs=(_SEM, _SEM, *[_HBM] * nb, pl.BlockSpec(memory_space=pltpu.VMEM)),
            input_output_aliases={i: 2 + i for i in range(nb)},
            compiler_params=pltpu.CompilerParams(has_side_effects=_EFFECT),
        )(*[pltpu.with_memory_space_constraint(b, pltpu.HBM) for b in bufs], *after)
        self.send, self.recv, self.bufs, self.token = outs[0], outs[1], list(outs[2:2 + nb]), outs[-1]

    def wait(self, *after):
        plan, nb = self.plan, len(self.bufs)

        def body(*refs):
            send, recv = refs[nb], refs[nb + 1]
            for k, (src, _, target, land) in enumerate(plan(refs[:nb])):
                cp = pltpu.make_async_remote_copy(src_ref=src, dst_ref=land, send_sem=send.at[k], recv_sem=recv.at[k],
                                                  device_id=target, device_id_type=MESH)
                cp.wait_send()
                cp.wait_recv()

        outs = pl.pallas_call(
            body, name=self.name + "_wait", out_shape=[pltpu.HBM(b.shape, b.dtype) for b in self.bufs],
            in_specs=[_HBM] * nb + [_SEM, _SEM] + [pl.BlockSpec(memory_space=pl.ANY)] * len(after),
            out_specs=[_HBM] * nb, input_output_aliases={i: i for i in range(nb)},
            compiler_params=pltpu.CompilerParams(has_side_effects=_EFFECT),
        )(*self.bufs, self.send, self.recv, *after)
        return list(outs)


def _gather_plan(n):
    def plan(refs):
        x, y, c = _place()
        me = 2 * x + y
        return [(refs[a].at[me], refs[a].at[me], (px, py, c), refs[a].at[2 * px + py])
                for a in range(n) for px, py in _other_chips(x, y)]

    return plan


def _peers():
    x, y, c = _place()
    return [(k - 1, (x ^ (k >> 2), y ^ ((k >> 1) & 1), c ^ (k & 1))) for k in range(1, N_DEV)]


def _scatter_plan(n, half_rows):
    def plan(refs):
        out = []
        for a in range(n):
            hr = half_rows[a]
            for k, (px, py, pc) in _peers():
                out.append((refs[a].at[2 * px + py, pl.ds(pc * hr, hr)], refs[n + a].at[k], (px, py, pc),
                            refs[n + a].at[k]))
        return out

    return plan


def _join_plan(n):
    def plan(refs):
        x, y, c = _place()
        return [(refs[a].at[c], refs[a].at[c], (x, y, 1 - c), refs[a].at[1 - c]) for a in range(n)]

    return plan


def _sum_parts(gs, qs):
    n = len(gs)
    half_rows = [g.shape[1] // 2 for g in gs]
    tr = [_block_rows(hr) for hr in half_rows]
    per = [hr // t for hr, t in zip(half_rows, tr)]
    x, y, c = _place()
    where = jnp.stack([2 * x + y, c]).astype(jnp.int32)

    def body(where_ref, *refs):
        i = pl.program_id(0)
        for a in range(n):
            g_ref, q_ref, o_ref = refs[a], refs[n + a], refs[2 * n + a]

            @pl.when(i < per[a])
            def _():
                total = g_ref[...]
                for k in range(N_DEV - 1):
                    total = total + q_ref[k].astype(F32)
                o_ref[...] = total

    def at(a, i):
        return jnp.minimum(i, per[a] - 1)

    specs_g = [pl.BlockSpec((None, tr[a], gs[a].shape[2]),
                            lambda i, where_ref, a=a: (where_ref[0], where_ref[1] * per[a] + at(a, i), 0)) for a in range(n)]
    specs_q = [pl.BlockSpec((N_DEV - 1, tr[a], gs[a].shape[2]), lambda i, where_ref, a=a: (0, at(a, i), 0))
               for a in range(n)]
    specs_o = [pl.BlockSpec((None, tr[a], gs[a].shape[2]), lambda i, where_ref, a=a: (where_ref[1], at(a, i), 0))
               for a in range(n)]
    return pl.pallas_call(
        body, name="sum_parts",
        grid_spec=pltpu.PrefetchScalarGridSpec(num_scalar_prefetch=1, grid=(max(per),), in_specs=specs_g + specs_q,
                                               out_specs=specs_o),
        out_shape=[jax.ShapeDtypeStruct((2, hr, g.shape[2]), F32) for g, hr in zip(gs, half_rows)],
        compiler_params=_params("arbitrary"),
    )(where, *gs, *qs)


def _all_plan(refs):
    x, y, c = _place()
    mine = refs[0].at[4 * x + 2 * y + c]
    return [(mine, mine, (px, py, pc), refs[0].at[4 * px + 2 * py + pc]) for _, (px, py, pc) in _peers()]


def _sum_devices(parts):
    def body(p_ref, o_ref):
        total = p_ref[0]
        for d in range(1, N_DEV):
            total = total + p_ref[d]
        o_ref[...] = total

    vm = pl.BlockSpec(memory_space=pltpu.VMEM)
    return pl.pallas_call(body, name="sum_devices", in_specs=[vm], out_specs=vm,
                          out_shape=jax.ShapeDtypeStruct(parts.shape[1:], F32))(parts)


def _adamw_math(w, g, m, v):
    m = ADAM_B1 * m + (1.0 - ADAM_B1) * g
    v = ADAM_B2 * v + (1.0 - ADAM_B2) * jnp.square(g)
    m_hat = m / (1.0 - ADAM_B1 ** ADAM_STEP)
    v_hat = v / (1.0 - ADAM_B2 ** ADAM_STEP)
    delta = -ADAM_LR * (m_hat / (jnp.sqrt(v_hat) + ADAM_EPS) + ADAM_WD * w)
    return delta, m, v


def _adamw_large(layer, ws, halves, ms, vs, others):
    n = len(ws)
    tr = [_block_rows(w.shape[1] // 2) for w in ws]
    per = [w.shape[1] // 2 // t for w, t in zip(ws, tr)]
    kept = [] if others is None else [a for four in others for a in four]

    def body(*refs):
        i = pl.program_id(0)
        outs = refs[4 * n + len(kept):]
        for a in range(n):
            w_ref, g_ref, m_ref, v_ref = refs[a], refs[n + a], refs[2 * n + a], refs[3 * n + a]
            g_out, d_ref, nm_ref, nv_ref = outs[4 * a:4 * a + 4]

            @pl.when(i < 2 * per[a])
            def _():
                g = g_ref[...]
                g_out[...] = g
                d_ref[...], nm_ref[...], nv_ref[...] = _adamw_math(w_ref[...], g, m_ref[...], v_ref[...])

    def at(a, i):
        return jnp.minimum(i, 2 * per[a] - 1)

    blk = [pl.BlockSpec((None, tr[a], ws[a].shape[2]), lambda i, a=a: (layer, at(a, i), 0)) for a in range(n)]
    half = [pl.BlockSpec((None, tr[a], ws[a].shape[2]), lambda i, a=a: (at(a, i) // per[a], at(a, i) % per[a], 0))
            for a in range(n)]
    outs = pl.pallas_call(
        body, name="adamw_large", grid=(2 * max(per),),
        in_specs=blk + half + blk + blk + [pl.BlockSpec(memory_space=pl.ANY)] * len(kept),
        out_specs=[blk[a] for a in range(n) for _ in range(4)],
        out_shape=[jax.ShapeDtypeStruct(w.shape, F32) for w in ws for _ in range(4)],
        input_output_aliases={4 * n + k: k for k in range(len(kept))},
        compiler_params=_params("arbitrary"),
    )(*ws, *halves, *ms, *vs, *kept)
    return [outs[4 * a:4 * a + 4] for a in range(n)]


def _adamw_small(ws, gs, ms, vs):
    n = len(ws)

    def body(*refs):
        w_r, g_r, m_r, v_r = refs[:n], refs[n:2 * n], refs[2 * n:3 * n], refs[3 * n:4 * n]
        d_r, nm_r, nv_r = refs[4 * n:5 * n], refs[5 * n:6 * n], refs[6 * n:]
        for a in range(n):
            d_r[a][...], nm_r[a][...], nv_r[a][...] = _adamw_math(w_r[a][...], g_r[a][...], m_r[a][...], v_r[a][...])

    vm = pl.BlockSpec(memory_space=pltpu.VMEM)
    outs = pl.pallas_call(
        body, name="adamw_small", in_specs=[vm] * (4 * n), out_specs=[vm] * (3 * n),
        out_shape=[jax.ShapeDtypeStruct(w.shape, F32) for w in ws] * 3,
    )(*ws, *gs, *ms, *vs)
    return outs[:n], outs[n:2 * n], outs[2 * n:]


_LARGE = ("w_in", "w_out", "w_up", "w_down")
_SMALL = ("meta_tokens", "mix_pre_g", "conv_w", "sinks", "attn_out_g", "conv_out_g", "mix_post_g", "mlp_pre_g",
          "mlp_post_g")
_ORDER = ("meta_tokens", "mix_pre_g", "w_in", "conv_w", "sinks", "attn_out_g", "conv_out_g", "w_out", "mix_post_g",
          "mlp_pre_g", "w_up", "w_down", "mlp_post_g")


class _Reduce:
    def __init__(self, name, grads, after=()):
        self.name, self.n = name, len(grads)
        self.own = [g for g, _ in grads]
        half_rows = [g.shape[1] // 2 for g in self.own]
        zones = [lax.empty((N_DEV - 1, hr, g.shape[2]), BF16) for g, hr in zip(self.own, half_rows)]
        self.exchange = _Exchange(name + "_scatter", [b for _, b in grads] + zones, _scatter_plan(self.n, half_rows),
                                  (N_DEV - 1) * self.n, after)

    @property
    def token(self):
        return self.exchange.token

    def join(self, *after):
        bufs = self.exchange.wait(*after)
        halves = list(_sum_parts(self.own, bufs[self.n:]))
        self.exchange = _Exchange(self.name + "_join", halves, _join_plan(self.n), self.n)

    def done(self, *after):
        return self.exchange.wait(*after)


def _pad_cols(a, n=D_MODEL):
    return jnp.pad(a, ((0, 0), (0, n - a.shape[1])))


def kernel(x, meta_tokens, mix_pre_g, w_in, conv_w, sinks, attn_out_g, conv_out_g, w_out, mix_post_g, mlp_pre_g, w_up, w_down, mlp_post_g, loss_target, m_meta_tokens, m_mix_pre_g, m_w_in, m_conv_w, m_sinks, m_attn_out_g, m_conv_out_g, m_w_out, m_mix_post_g, m_mlp_pre_g, m_w_up, m_w_down, m_mlp_post_g, v_meta_tokens, v_mix_pre_g, v_w_in, v_conv_w, v_sinks, v_attn_out_g, v_conv_out_g, v_w_out, v_mix_post_g, v_mlp_pre_g, v_w_up, v_w_down, v_mlp_post_g):
    w = dict(meta_tokens=meta_tokens, mix_pre_g=mix_pre_g, w_in=w_in, conv_w=conv_w, sinks=sinks,
             attn_out_g=attn_out_g, conv_out_g=conv_out_g, w_out=w_out, mix_post_g=mix_post_g, mlp_pre_g=mlp_pre_g,
             w_up=w_up, w_down=w_down, mlp_post_g=mlp_post_g)
    m = dict(meta_tokens=m_meta_tokens, mix_pre_g=m_mix_pre_g, w_in=m_w_in, conv_w=m_conv_w, sinks=m_sinks,
             attn_out_g=m_attn_out_g, conv_out_g=m_conv_out_g, w_out=m_w_out, mix_post_g=m_mix_post_g,
             mlp_pre_g=m_mlp_pre_g, w_up=m_w_up, w_down=m_w_down, mlp_post_g=m_mlp_post_g)
    v = dict(meta_tokens=v_meta_tokens, mix_pre_g=v_mix_pre_g, w_in=v_w_in, conv_w=v_conv_w, sinks=v_sinks,
             attn_out_g=v_attn_out_g, conv_out_g=v_conv_out_g, w_out=v_w_out, mix_post_g=v_mix_post_g,
             mlp_pre_g=v_mlp_pre_g, w_up=v_w_up, w_down=v_w_down, mlp_post_g=v_mlp_post_g)
    chip = 2 * lax.axis_index("x") + lax.axis_index("y")
    tl = _Tiles(x.shape[1] + BLOCK)

    def zone(quarter):
        return lax.dynamic_update_slice(lax.empty((N_CHIPS,) + quarter.shape, quarter.dtype), quarter[None],
                                        (chip,) + (0,) * quarter.ndim)

    w, m, v = ({**d, "w_in": jnp.swapaxes(d["w_in"], 1, 2)} for d in (w, m, v))
    zones = {n: [zone(w[n][l].astype(BF16)) for l in range(DEPTH)] for n in _LARGE}
    first = _Exchange("gather_first", [zones["w_in"][0], zone(w["conv_w"]), zone(w["meta_tokens"])], _gather_plan(3), 9)
    out0 = _Exchange("gather_out", [zones["w_out"][0]], _gather_plan(1), 3, [first.token])
    rest = _Exchange("gather_rest", [zones[n][0] for n in ("w_up", "w_down")], _gather_plan(2), 6, [out0.token])

    def whole_in(quarters):
        return quarters.reshape(IN_W, D_MODEL)

    q_in, q_conv, q_meta = first.wait(rest.token, *tl.tabs, tl.bias)
    conv_whole = jnp.transpose(q_conv, (1, 2, 0, 3)).reshape(DEPTH, CONV_K, CONV_W)
    meta = jnp.transpose(q_meta, (1, 0, 2)).reshape(N_META, D_MODEL)
    p = [{"conv_w": conv_whole[l], "sinks": w["sinks"][l]} for l in range(DEPTH)]
    for l in range(DEPTH):
        for n in ("mix_pre_g", "attn_out_g", "conv_out_g", "mix_post_g", "mlp_pre_g", "mlp_post_g"):
            p[l][n] = w[n][l][None, :]

    lead = jnp.concatenate([jnp.zeros((LEAD_PAD, D_MODEL), F32), meta], axis=0)
    p[0]["w_in"] = whole_in(q_in)
    mixed = _mixer_fwd(x[0], p[0], tl, lead)
    second = _Exchange("gather_second", [zones["w_in"][1], zones["w_out"][1]], _gather_plan(2), 6, [mixed[-1]])
    second_mlp = _Exchange("gather_second_mlp", [zones["w_up"][1], zones["w_down"][1]], _gather_plan(2), 6,
                           [second.token])
    p[0]["w_out"], = out0.wait(second_mlp.token)
    h1, saved0 = _out_fwd(mixed, p[0], tl)
    p[0]["w_up"], p[0]["w_down"] = rest.wait(h1)
    h, saved0 = _mlp_fwd(h1, saved0, p[0], tl)
    q_in, p[1]["w_out"] = second.wait(h)
    p[1]["w_in"] = whole_in(q_in)
    h1, saved1 = _out_fwd(_mixer_fwd(h, p[1], tl), p[1], tl)
    p[1]["w_up"], p[1]["w_down"] = second_mlp.wait(h1)
    (loss_tile, dh), saved1 = _mlp_fwd(h1, saved1, p[1], tl, loss_target[0])

    def adamw(layer, halves, other):
        names = list(halves)
        done = _adamw_large(layer, [w[n] for n in names], [halves[n] for n in names], [m[n] for n in names],
                            [v[n] for n in names], None if other is None else [other[n] for n in names])
        return dict(zip(names, done))

    dh1, g1 = _mlp_part_bwd(dh, saved1, p[1], tl)
    carry, gm = _mix_out_part_bwd(dh1, saved1, p[1], tl)
    dh, dproj, gi = _attn_in_part_bwd(carry, saved1, p[1], tl)
    g1.update(gm, w_in=_in_grad(dproj, saved1), **gi)
    red1 = _Reduce("reduce1", [g1[n] for n in _LARGE])
    dh1, g0 = _mlp_part_bwd(dh, saved0, p[0], tl, [red1.token])
    red1.join(g0["w_down"][0])
    carry, gm = _mix_out_part_bwd(dh1, saved0, p[0], tl, [red1.token])
    first0 = ("w_up", "w_down", "w_out")
    g0.update(gm)
    red0a = _Reduce("reduce0a", [g0[n] for n in first0])
    (dlead, dseq), dproj, gi = _attn_in_part_bwd(carry, saved0, p[0], tl, [red0a.token], split_lead=True)
    g0.update(gi)
    grad_x = dseq[None]
    grads = {n: [g0[n], g1[n]] for n in g0 if n not in _LARGE}

    rows = [dlead[LEAD_PAD:]]
    for n in ("mix_pre_g", "mix_post_g", "mlp_pre_g", "mlp_post_g"):
        rows += grads[n]
    rows += [jnp.concatenate([grads["attn_out_g"][l], grads["conv_out_g"][l]], axis=1) for l in range(DEPTH)]
    rows.append(jnp.concatenate(grads["conv_w"], axis=1))
    rows.append(_pad_cols(jnp.concatenate(grads["sinks"])[None, :]))
    rows.append(_pad_cols(loss_tile[:1]))
    packed = jnp.concatenate(rows, axis=0)
    packed = jnp.pad(packed, ((0, SMALL_ROWS - packed.shape[0]), (0, 0)))
    device = 2 * chip + lax.axis_index("c")
    small_parts = _Exchange("gather_small", [lax.dynamic_update_slice(lax.empty((N_DEV,) + packed.shape, F32),
                                                                      packed[None], (device, 0, 0))], _all_plan, N_DEV - 1)
    g0["w_in"] = _in_grad(dproj, saved0, [small_parts.token])
    red0b = _Reduce("reduce0b", [g0["w_in"]])
    done1 = adamw(1, dict(zip(_LARGE, red1.done(red0b.token))), None)
    total = _sum_devices(small_parts.wait(*[done1[n][0] for n in _LARGE])[0])
    r0 = N_META
    small = {
        "meta_tokens": lax.dynamic_slice(total[:N_META], (0, chip * (D_MODEL // N_CHIPS)), (N_META, D_MODEL // N_CHIPS)),
        "mix_pre_g": total[r0:r0 + 2], "mix_post_g": total[r0 + 2:r0 + 4], "mlp_pre_g": total[r0 + 4:r0 + 6],
        "mlp_post_g": total[r0 + 6:r0 + 8],
        "attn_out_g": total[r0 + 8:r0 + 10, :ATTN_W], "conv_out_g": total[r0 + 8:r0 + 10, ATTN_W:],
        "conv_w": lax.dynamic_slice(total[r0 + 10:r0 + 13].reshape(CONV_K, DEPTH, CONV_W).transpose(1, 0, 2),
                                    (0, 0, chip * (CONV_W // N_CHIPS)), (DEPTH, CONV_K, CONV_W // N_CHIPS)),
        "sinks": total[r0 + 13, :DEPTH * N_Q_HEADS].reshape(DEPTH, N_Q_HEADS),
    }
    loss = total[r0 + 14, 0]

    ds, nms, nvs = _adamw_small([w[n] for n in _SMALL], [small[n] for n in _SMALL], [m[n] for n in _SMALL],
                                [v[n] for n in _SMALL])
    red0a.join(ds[0], grad_x)
    red0b.join(red0a.token)
    done0 = adamw(0, dict(zip(first0, red0a.done(red0b.token))), done1)
    done0.update(adamw(0, {"w_in": red0b.done(done0["w_down"][0])[0]}, done1))
    grad, delta, new_m, new_v = {}, {}, {}, {}
    for n in _LARGE:
        grad[n], delta[n], new_m[n], new_v[n] = done0[n]
    for d in (grad, delta, new_m, new_v):
        d["w_in"] = jnp.swapaxes(d["w_in"], 1, 2)
    for i, n in enumerate(_SMALL):
        grad[n], delta[n], new_m[n], new_v[n] = small[n], ds[i], nms[i], nvs[i]
    return (loss, grad_x, *[grad[n] for n in _ORDER], *[delta[n] for n in _ORDER], *[new_m[n] for n in _ORDER],
            *[new_v[n] for n in _ORDER])
```

```python
import jax
import jax.numpy as jnp
from jax import lax
from jax.experimental import pallas as pl
from jax.experimental.pallas import tpu as pltpu

F32 = jnp.float32
BF16 = jnp.bfloat16

D_MODEL = 1024
DEPTH = 2
N_META = 16
ATTN_W = 512
CONV_W = 512
HEAD_DIM = 64
N_Q_HEADS = 8
N_KV_HEADS = 2
GROUP = N_Q_HEADS // N_KV_HEADS
KV_W = N_KV_HEADS * HEAD_DIM
CONV_K = 3
BLOCK = 128
LEAD_PAD = BLOCK - N_META
ROPE_THETA = 500000.0
ROT_DIM = HEAD_DIM // 4
ROT_HALF = ROT_DIM // 2
D_FF = 4 * D_MODEL
IN_W = ATTN_W + 2 * KV_W + 3 * CONV_W
QKV_W = ATTN_W + 2 * KV_W
EPS = 1e-6
SCALE = HEAD_DIM ** -0.5
FF_CHUNK = 1024
N_CHIPS = 4
N_DEV = 8

ADAM_LR = 0.001
ADAM_B1 = 0.9
ADAM_B2 = 0.999
ADAM_EPS = 1e-08
ADAM_WD = 0.01
ADAM_STEP = 10

V7X_VMEM_LIMIT = 60 * 1024 * 1024
SMALL_ROWS = 32

MESH = pl.DeviceIdType.MESH


def _params(*sem):
    return pltpu.CompilerParams(dimension_semantics=sem, vmem_limit_bytes=V7X_VMEM_LIMIT)


def _block_rows(n):
    return max(r for r in range(16, min(n, 64) + 1, 16) if n % r == 0)


def _row_tile(t, most):
    nb = t // BLOCK
    for b in range(most // BLOCK, 0, -1):
        if nb % b == 0:
            return b * BLOCK
    return BLOCK


def _behind(body, deps):
    n = len(deps)

    def wrapped(*refs):
        body(*refs[n:])

    return wrapped, [pl.BlockSpec(memory_space=pl.ANY)] * n


def _rms(x, g):
    r = lax.rsqrt(jnp.mean(x * x, axis=-1, keepdims=True) + EPS)
    return x * r * g


def _rms_bwd(dy, x, g):
    r = lax.rsqrt(jnp.mean(x * x, axis=-1, keepdims=True) + EPS)
    xh = x * r
    dg = jnp.sum(dy * xh, axis=0, keepdims=True)
    dxh = dy * g
    dx = r * (dxh - xh * jnp.mean(dxh * xh, axis=-1, keepdims=True))
    return dx, dg


def _rope(x, cos, sa, sb):
    n = x.shape[-1]
    return x * cos + pltpu.roll(x, n - ROT_HALF, 1) * sa + pltpu.roll(x, ROT_HALF, 1) * sb


def _rope_bwd(dy, cos, sa, sb):
    n = dy.shape[-1]
    return dy * cos + pltpu.roll(dy * sa, ROT_HALF, 1) + pltpu.roll(dy * sb, n - ROT_HALF, 1)


def _rope_tables(t):
    pos = lax.broadcasted_iota(jnp.int32, (t, ROT_HALF), 0).astype(F32) - LEAD_PAD
    pair = lax.broadcasted_iota(jnp.int32, (t, ROT_HALF), 1).astype(F32)
    inv_freq = jnp.power(jnp.float32(ROPE_THETA), -(2.0 * pair) / ROT_DIM)
    ang = pos * inv_freq
    cos, sin = lax.optimization_barrier((jnp.cos(ang), jnp.sin(ang)))
    spread = (1, 2 * HEAD_DIM // ROT_HALF)
    cos, sin = jnp.tile(cos, spread), jnp.tile(sin, spread)
    dim = lax.broadcasted_iota(jnp.int32, (t, 2 * HEAD_DIM), 1) % HEAD_DIM
    return jnp.where(dim < ROT_DIM, cos, 1.0), jnp.where(dim < ROT_DIM, sin, 0.0)


def _rope_factors(cos, sin):
    dim = lax.broadcasted_iota(jnp.int32, sin.shape, 1) % HEAD_DIM
    return cos, jnp.where(dim < ROT_HALF, -sin, 0.0), jnp.where(dim >= ROT_HALF, sin, 0.0)


def _in_proj(h, g, w, tabs, tm, lead=None):
    t = h.shape[0] + (0 if lead is None else BLOCK)
    per_step = 0 if lead is None else tm // BLOCK

    def body(*refs):
        if lead is None:
            x = refs[0][...]
            refs = refs[1:]
        else:
            blocks = [r[...] for r in refs[1:1 + per_step]]
            blocks[0] = jnp.where(pl.program_id(0) == 0, refs[0][...], blocks[0])
            x = jnp.concatenate(blocks, axis=0)
            first_out = 1 + per_step + 4
            refs[first_out][...] = x
            refs = refs[1 + per_step:first_out] + refs[first_out + 1:]
        g_ref, w_ref, c_ref, s_ref, a_ref, q_ref, k_ref, v_ref, b_ref, cg_ref, hc_ref = refs
        a = _rms(x, g_ref[...]).astype(BF16)
        a_ref[...] = a
        p = lax.dot_general(a, w_ref[...], (((1,), (1,)), ((), ())), preferred_element_type=F32)
        cos, sa, sb = _rope_factors(c_ref[...], s_ref[...])
        rep = ATTN_W // (2 * HEAD_DIM)
        q = _rope(p[:, :ATTN_W], jnp.tile(cos, (1, rep)), jnp.tile(sa, (1, rep)), jnp.tile(sb, (1, rep)))
        q_ref[...] = (q * SCALE).astype(BF16)
        k_ref[...] = _rope(p[:, ATTN_W:ATTN_W + KV_W], cos, sa, sb).astype(BF16)
        v_ref[...] = p[:, ATTN_W + KV_W:QKV_W].astype(BF16)
        b_ref[...] = p[:, QKV_W:QKV_W + CONV_W].astype(BF16)
        cg_ref[...] = p[:, QKV_W + CONV_W:QKV_W + 2 * CONV_W].astype(BF16)
        hc_ref[...] = p[:, QKV_W + 2 * CONV_W:].astype(BF16)

    row = lambda n: pl.BlockSpec((tm, n), lambda i: (i, 0))
    full = lambda a: pl.BlockSpec(a.shape, lambda i: (0, 0))

    def sequence_block(b):
        return pl.BlockSpec((BLOCK, D_MODEL), lambda i: (jnp.maximum(i * per_step + b - 1, 0), 0))

    if lead is None:
        first_in, first_args, first_out, first_shape = [row(D_MODEL)], [h], [], []
    else:
        first_in = [full(lead)] + [sequence_block(b) for b in range(per_step)]
        first_args = [lead] + [h] * per_step
        first_out, first_shape = [row(D_MODEL)], [jax.ShapeDtypeStruct((t, D_MODEL), F32)]
    return pl.pallas_call(
        body, name="in_proj", grid=(t // tm,),
        in_specs=first_in + [full(g), full(w), row(2 * HEAD_DIM), row(2 * HEAD_DIM)],
        out_specs=first_out + [row(D_MODEL), row(ATTN_W), row(KV_W), row(KV_W), row(CONV_W), row(CONV_W), row(CONV_W)],
        out_shape=first_shape + [jax.ShapeDtypeStruct((t, D_MODEL), BF16), jax.ShapeDtypeStruct((t, ATTN_W), BF16),
                                 jax.ShapeDtypeStruct((t, KV_W), BF16), jax.ShapeDtypeStruct((t, KV_W), BF16),
                                 jax.ShapeDtypeStruct((t, CONV_W), BF16), jax.ShapeDtypeStruct((t, CONV_W), BF16),
                                 jax.ShapeDtypeStruct((t, CONV_W), BF16)],
        compiler_params=_params("parallel"),
    )(*first_args, g, w, *tabs)


def _attn_bias():
    r = lax.broadcasted_iota(jnp.int32, (3, BLOCK, 2 * BLOCK), 1)
    c = lax.broadcasted_iota(jnp.int32, (3, BLOCK, 2 * BLOCK), 2)
    i = lax.broadcasted_iota(jnp.int32, (3, BLOCK, 2 * BLOCK), 0)
    ok = (c > r) & (c <= r + BLOCK) & (c + (i - 1) * BLOCK >= LEAD_PAD)
    return jnp.where(ok, 0.0, -jnp.inf).astype(F32)


def _attn_scores(qh, kg, bias):
    return lax.dot_general(qh, kg, (((1,), (1,)), ((), ())), preferred_element_type=F32) + bias


def _attn_probs(s, sk):
    m = jnp.maximum(jnp.max(s, axis=-1, keepdims=True), sk)
    e = jnp.exp(s - m)
    es = jnp.exp(sk - m)
    rden = 1.0 / (jnp.sum(e, axis=-1, keepdims=True) + es)
    return e * rden, es * rden


def _head(hh):
    return slice(hh * HEAD_DIM, (hh + 1) * HEAD_DIM)


def _two_blocks(ref, i):
    prev = jnp.maximum(i - 1, 0)
    return jnp.concatenate([ref[pl.ds(pl.multiple_of(prev * BLOCK, BLOCK), BLOCK), :],
                            ref[pl.ds(pl.multiple_of(i * BLOCK, BLOCK), BLOCK), :]], axis=0)


def _attn_fwd(q, k, v, bias, sinks, tm):
    t = q.shape[0]
    per_step = tm // BLOCK
    heads = range(N_Q_HEADS)

    def body(s_ref, q_ref, k_ref, v_ref, bias_ref, o_ref):
        for b in range(per_step):
            i = pl.program_id(0) * per_step + b
            rows = slice(b * BLOCK, (b + 1) * BLOCK)
            kc, vc = _two_blocks(k_ref, i), _two_blocks(v_ref, i)
            bias_i = bias_ref[jnp.minimum(i, 2)]
            scores = [_attn_scores(q_ref[rows, _head(hh)], kc[:, _head(hh // GROUP)], bias_i) for hh in heads]
            probs = [_attn_probs(scores[hh], s_ref[hh])[0].astype(BF16) for hh in heads]
            for hh in heads:
                o_ref[rows, _head(hh)] = jnp.dot(probs[hh], vc[:, _head(hh // GROUP)],
                                                 preferred_element_type=F32).astype(BF16)

    whole = pl.BlockSpec((t, KV_W), lambda i: (0, 0))
    return pl.pallas_call(
        body, name="attn_fwd", grid=(t // tm,),
        in_specs=[pl.BlockSpec(memory_space=pltpu.SMEM), pl.BlockSpec((tm, ATTN_W), lambda i: (i, 0)), whole, whole,
                  pl.BlockSpec(bias.shape, lambda i: (0, 0, 0))],
        out_specs=pl.BlockSpec((tm, ATTN_W), lambda i: (i, 0)),
        out_shape=jax.ShapeDtypeStruct((t, ATTN_W), BF16),
        compiler_params=_params("parallel"),
    )(sinks, q, k, v, bias)


def _shift_rows(u, halo, n):
    r = pltpu.roll(u, n, 0)
    hr = pltpu.roll(halo, n, 0)
    idx = lax.broadcasted_iota(jnp.int32, hr.shape, 0)
    return jnp.concatenate([jnp.where(idx < n, hr, r[:8]), r[8:]], axis=0)


def _advance_rows(u, halo, n):
    rows = u.shape[0]
    r = pltpu.roll(u, rows - n, 0)
    hr = pltpu.roll(halo, 8 - n, 0)
    idx = lax.broadcasted_iota(jnp.int32, hr.shape, 0)
    return jnp.concatenate([r[:rows - 8], jnp.where(idx >= 8 - n, hr, r[rows - 8:])], axis=0)


def _mix_out(h, o, b, c, hc, cw, ga, gc, w, gp, tm, deps=()):
    t = h.shape[0]

    def body(h_ref, o_ref, b_ref, c_ref, hc_ref, cw_ref, ga_ref, gc_ref, w_ref, gp_ref, h1_ref, y_ref, z_ref, halo):
        @pl.when(pl.program_id(0) == 0)
        def _():
            halo[...] = jnp.zeros_like(halo)

        u = c_ref[...].astype(F32) * hc_ref[...].astype(F32)
        cv = cw_ref[0:1, :] * _shift_rows(u, halo[...], 2) + cw_ref[1:2, :] * _shift_rows(u, halo[...], 1) \
            + cw_ref[2:3, :] * u
        halo[...] = u[tm - 8:]
        yc = b_ref[...].astype(F32) * cv
        y = jnp.concatenate([_rms(o_ref[...].astype(F32), ga_ref[...]), _rms(yc, gc_ref[...])], axis=1).astype(BF16)
        y_ref[...] = y
        z = jnp.dot(y, w_ref[...].reshape(D_MODEL, D_MODEL), preferred_element_type=F32)
        z_ref[...] = z
        h1_ref[...] = h_ref[...] + _rms(z, gp_ref[...])

    row = lambda n: pl.BlockSpec((tm, n), lambda i: (i, 0))
    full = lambda a: pl.BlockSpec(a.shape, lambda i: (0,) * a.ndim)
    body, dep_specs = _behind(body, deps)
    return pl.pallas_call(
        body, name="mix_out", grid=(t // tm,),
        in_specs=dep_specs + [row(D_MODEL), row(ATTN_W), row(CONV_W), row(CONV_W), row(CONV_W), full(cw), full(ga),
                              full(gc), full(w), full(gp)],
        out_specs=[row(D_MODEL), row(D_MODEL), row(D_MODEL)],
        out_shape=[jax.ShapeDtypeStruct((t, D_MODEL), F32), jax.ShapeDtypeStruct((t, D_MODEL), BF16),
                   jax.ShapeDtypeStruct((t, D_MODEL), F32)],
        scratch_shapes=[pltpu.VMEM((8, CONV_W), F32)],
        compiler_params=_params("arbitrary"),
    )(*deps, h, o, b, c, hc, cw, ga, gc, w, gp)


def _mlp(h1, g1, wu, wd, g2, tm, target=None):
    t = h1.shape[0]
    nj = D_FF // FF_CHUNK
    per_step = tm // BLOCK if target is not None else 0

    def body(h1_ref, g1_ref, wu_ref, wd_ref, g2_ref, *rest):
        t_refs, outs = rest[:per_step], rest[per_step:]
        a2_ref, slope_ref, f_ref = outs[-3:]
        a2 = _rms(h1_ref[...], g1_ref[...]).astype(BF16)
        a2_ref[...] = a2
        f = None
        for j in range(nj):
            up = jnp.dot(a2, wu_ref[j], preferred_element_type=F32)
            r = jnp.maximum(up, 0.0)
            slope_ref[:, j * FF_CHUNK:(j + 1) * FF_CHUNK] = (r + r).astype(BF16)
            part = jnp.dot((r * r).astype(BF16), wd_ref[j], preferred_element_type=F32)
            f = part if f is None else f + part
        f_ref[...] = f
        h2 = h1_ref[...] + _rms(f, g2_ref[...])
        if target is None:
            outs[0][...] = h2
            return
        loss_ref, dh_ref = outs[:2]
        i = pl.program_id(0)

        @pl.when(i == 0)
        def _():
            loss_ref[...] = jnp.zeros_like(loss_ref)

        total = jnp.zeros((), F32)
        for b in range(per_step):
            rows = slice(b * BLOCK, (b + 1) * BLOCK)
            err = h2[rows] - t_refs[b][...]
            if b == 0:
                err = jnp.where(i == 0, 0.0, err)
            dh_ref[rows, :] = err * (1.0 / D_MODEL)
            total = total + jnp.sum(err * err)
        loss_ref[...] += total * (0.5 / D_MODEL)

    def target_block(b):
        return pl.BlockSpec((BLOCK, D_MODEL), lambda i: (jnp.maximum(i * per_step + b - 1, 0), 0))

    row = pl.BlockSpec((tm, D_MODEL), lambda i: (i, 0))
    vec = pl.BlockSpec((1, D_MODEL), lambda i: (0, 0))
    resident = pl.BlockSpec(memory_space=pltpu.VMEM)
    first_specs, first_shapes = [row], [jax.ShapeDtypeStruct((t, D_MODEL), F32)]
    if target is not None:
        first_specs = [pl.BlockSpec((8, 128), lambda i: (0, 0)), row]
        first_shapes = [jax.ShapeDtypeStruct((8, 128), F32), jax.ShapeDtypeStruct((t, D_MODEL), F32)]
    outs = pl.pallas_call(
        body, name="mlp", grid=(t // tm,),
        in_specs=[row, vec, resident, resident, vec] + [target_block(b) for b in range(per_step)],
        out_specs=first_specs + [row, pl.BlockSpec((tm, D_FF), lambda i: (i, 0)), row],
        out_shape=first_shapes + [jax.ShapeDtypeStruct((t, D_MODEL), BF16), jax.ShapeDtypeStruct((t, D_FF), BF16),
                                  jax.ShapeDtypeStruct((t, D_MODEL), F32)],
        compiler_params=_params("parallel" if target is None else "arbitrary"),
    )(h1, g1, wu, wd, g2, *([target] * per_step))
    return (outs[0] if target is None else tuple(outs[:2]),) + tuple(outs[-3:])


def _mlp_bwd_hidden(dh2, f, g2, slope, wd, tm, deps=()):
    t = dh2.shape[0]
    nj = D_FF // FF_CHUNK

    def body(dh2_ref, f_ref, g2_ref, slope_ref, wd_ref, df_ref, dup_ref, dg2_ref):
        @pl.when(pl.program_id(0) == 0)
        def _():
            dg2_ref[...] = jnp.zeros_like(dg2_ref)

        df, dg = _rms_bwd(dh2_ref[...], f_ref[...], g2_ref[...])
        dg2_ref[...] += dg
        df = df.astype(BF16)
        df_ref[...] = df
        for j in range(nj):
            cols = slice(j * FF_CHUNK, (j + 1) * FF_CHUNK)
            dact = lax.dot_general(df, wd_ref[j], (((1,), (1,)), ((), ())), preferred_element_type=F32)
            dup_ref[:, cols] = (dact * slope_ref[:, cols].astype(F32)).astype(BF16)

    row = pl.BlockSpec((tm, D_MODEL), lambda i: (i, 0))
    wide = pl.BlockSpec((tm, D_FF), lambda i: (i, 0))
    vec = pl.BlockSpec((1, D_MODEL), lambda i: (0, 0))
    body, dep_specs = _behind(body, deps)
    return pl.pallas_call(
        body, name="mlp_bwd_hidden", grid=(t // tm,),
        in_specs=dep_specs + [row, row, vec, wide, pl.BlockSpec(memory_space=pltpu.VMEM)],
        out_specs=[row, wide, vec],
        out_shape=[jax.ShapeDtypeStruct((t, D_MODEL), BF16), jax.ShapeDtypeStruct((t, D_FF), BF16),
                   jax.ShapeDtypeStruct((1, D_MODEL), F32)],
        compiler_params=_params("arbitrary"),
    )(*deps, dh2, f, g2, slope, wd)


def _mlp_bwd_input(dup, wu, h1, g1, dh2, tm):
    t = dh2.shape[0]
    nj = D_FF // FF_CHUNK

    def body(dup_ref, wu_ref, h1_ref, g1_ref, dh2_ref, dh1_ref, dg1_ref):
        @pl.when(pl.program_id(0) == 0)
        def _():
            dg1_ref[...] = jnp.zeros_like(dg1_ref)

        da2 = None
        for j in range(nj):
            part = lax.dot_general(dup_ref[:, j * FF_CHUNK:(j + 1) * FF_CHUNK], wu_ref[j], (((1,), (1,)), ((), ())),
                                   preferred_element_type=F32)
            da2 = part if da2 is None else da2 + part
        dx, dg = _rms_bwd(da2, h1_ref[...], g1_ref[...])
        dh1_ref[...] = dh2_ref[...] + dx
        dg1_ref[...] += dg

    row = pl.BlockSpec((tm, D_MODEL), lambda i: (i, 0))
    vec = pl.BlockSpec((1, D_MODEL), lambda i: (0, 0))
    return pl.pallas_call(
        body, name="mlp_bwd_input", grid=(t // tm,),
        in_specs=[pl.BlockSpec((tm, D_FF), lambda i: (i, 0)), pl.BlockSpec(memory_space=pltpu.VMEM), row, vec, row],
        out_specs=[row, vec],
        out_shape=[jax.ShapeDtypeStruct((t, D_MODEL), F32), jax.ShapeDtypeStruct((1, D_MODEL), F32)],
        compiler_params=_params("arbitrary"),
    )(dup, wu, h1, g1, dh2)


def _row_split(t):
    tile = min(t, 1024)
    return tile, t // tile, t % tile


def _row_split_specs(t, cols):
    tile, whole, rest = _row_split(t)
    specs = [pl.BlockSpec((tile, cols), lambda r: (jnp.minimum(r, whole - 1), 0))]
    if rest:
        specs.append(pl.BlockSpec((rest, cols), lambda r: (whole * tile // rest, 0)))
    return specs


def _weight_grad(x, y, name, x_is_slope=False, deps=()):
    t, k = x.shape
    n = y.shape[1]
    tn = FF_CHUNK
    tk = FF_CHUNK if k % FF_CHUNK == 0 else k
    _, whole, rest = _row_split(t)
    steps = whole + bool(rest)

    def body(*refs):
        o_ref, ob_ref, r = refs[-2], refs[-1], pl.program_id(0)

        @pl.when(r == 0)
        def _():
            o_ref[...] = jnp.zeros_like(o_ref)

        def add(x_ref, y_ref):
            for a in range(k // tk):
                xv = x_ref[:, a * tk:(a + 1) * tk]
                if x_is_slope:
                    xv = xv.astype(F32)
                    xv = (xv * xv * 0.25).astype(BF16)
                for b in range(n // tn):
                    o_ref[a, b] += lax.dot_general(xv, y_ref[:, b * tn:(b + 1) * tn], (((0,), (0,)), ((), ())),
                                                   preferred_element_type=F32)

        if rest:
            pl.when(r < whole)(lambda: add(refs[0], refs[2]))
            pl.when(r == whole)(lambda: add(refs[1], refs[3]))
        else:
            add(refs[0], refs[1])

        @pl.when(r == steps - 1)
        def _():
            ob_ref[...] = o_ref[...].astype(BF16)

    vm = pl.BlockSpec(memory_space=pltpu.VMEM)
    body, dep_specs = _behind(body, deps)
    return pl.pallas_call(
        body, name=name, grid=(steps,),
        in_specs=dep_specs + _row_split_specs(t, k) + _row_split_specs(t, n), out_specs=[vm, vm],
        out_shape=[jax.ShapeDtypeStruct((k // tk, n // tn, tk, tn), F32),
                   jax.ShapeDtypeStruct((k // tk, n // tn, tk, tn), BF16)],
        compiler_params=_params("arbitrary"),
    )(*deps, *([x] * (1 + bool(rest))), *([y] * (1 + bool(rest))))


def _mix_out_bwd(dh1, z, gp, w, o, b, c, hc, cw, ga, gc, tm, deps=()):
    t = dh1.shape[0]
    nt = t // tm
    per16 = tm // 16

    def body(dh1_ref, z_ref, gp_ref, w_ref, o_ref, b_ref, c_ref, hc_ref, cp_ref, hp_ref, cw_ref, ga_ref, gc_ref,
             dz_ref, do_ref, dbch_ref, dgp_ref, dga_ref, dgc_ref, dcw_ref, halo):
        i = pl.program_id(0)

        @pl.when(i == 0)
        def _():
            halo[...] = jnp.zeros_like(halo)
            dgp_ref[...] = jnp.zeros_like(dgp_ref)
            dga_ref[...] = jnp.zeros_like(dga_ref)
            dgc_ref[...] = jnp.zeros_like(dgc_ref)
            dcw_ref[...] = jnp.zeros_like(dcw_ref)

        dz, dgp = _rms_bwd(dh1_ref[...], z_ref[...], gp_ref[...])
        dgp_ref[...] += dgp
        dz = dz.astype(BF16)
        dz_ref[...] = dz
        dy = lax.dot_general(dz, w_ref[...].reshape(D_MODEL, D_MODEL), (((1,), (1,)), ((), ())),
                             preferred_element_type=F32)
        do, dga = _rms_bwd(dy[:, :ATTN_W], o_ref[...].astype(F32), ga_ref[...])
        do_ref[...] = do.astype(BF16)
        dga_ref[...] += dga

        cc, hh = c_ref[...].astype(F32), hc_ref[...].astype(F32)
        u = cc * hh
        first = i == nt - 1
        u_before = jnp.where(first, 0.0, (cp_ref[...].astype(F32) * hp_ref[...].astype(F32))[8:])
        u1 = _shift_rows(u, u_before, 1)
        u2 = _shift_rows(u, u_before, 2)
        cv = cw_ref[0:1, :] * u2 + cw_ref[1:2, :] * u1 + cw_ref[2:3, :] * u
        bb = b_ref[...].astype(F32)
        dyc, dgc = _rms_bwd(dy[:, ATTN_W:], bb * cv, gc_ref[...])
        dgc_ref[...] += dgc
        dcv = dyc * bb
        d1 = _advance_rows(dcv, halo[...], 1)
        d2 = _advance_rows(dcv, halo[...], 2)
        halo[...] = dcv[:8]
        du = cw_ref[2:3, :] * dcv + cw_ref[1:2, :] * d1 + cw_ref[0:1, :] * d2
        dbch_ref[...] = jnp.concatenate([dyc * cv, du * hh, du * cc], axis=1).astype(BF16)
        dcw_ref[...] += jnp.concatenate([jnp.sum(dcv * u2, axis=0, keepdims=True),
                                         jnp.sum(dcv * u1, axis=0, keepdims=True),
                                         jnp.sum(dcv * u, axis=0, keepdims=True)], axis=0)

    row = lambda n: pl.BlockSpec((tm, n), lambda i: (nt - 1 - i, 0))
    before = pl.BlockSpec((16, CONV_W), lambda i: (jnp.maximum((nt - 1 - i) * per16 - 1, 0), 0))
    full = lambda a: pl.BlockSpec(a.shape, lambda i: (0,) * a.ndim)
    vec = lambda n: pl.BlockSpec((1, n), lambda i: (0, 0))
    body, dep_specs = _behind(body, deps)
    return pl.pallas_call(
        body, name="mix_out_bwd", grid=(nt,),
        in_specs=dep_specs + [row(D_MODEL), row(D_MODEL), full(gp), full(w), row(ATTN_W), row(CONV_W), row(CONV_W),
                              row(CONV_W), before, before, full(cw), full(ga), full(gc)],
        out_specs=[row(D_MODEL), row(ATTN_W), row(3 * CONV_W), vec(D_MODEL), vec(ATTN_W), vec(CONV_W),
                   pl.BlockSpec((CONV_K, CONV_W), lambda i: (0, 0))],
        out_shape=[jax.ShapeDtypeStruct((t, D_MODEL), BF16), jax.ShapeDtypeStruct((t, ATTN_W), BF16),
                   jax.ShapeDtypeStruct((t, 3 * CONV_W), BF16), jax.ShapeDtypeStruct((1, D_MODEL), F32),
                   jax.ShapeDtypeStruct((1, ATTN_W), F32), jax.ShapeDtypeStruct((1, CONV_W), F32),
                   jax.ShapeDtypeStruct((CONV_K, CONV_W), F32)],
        scratch_shapes=[pltpu.VMEM((8, CONV_W), F32)],
        compiler_params=_params("arbitrary"),
    )(*deps, dh1, z, gp, w, o, b, c, hc, c, hc, cw, ga, gc)


def _attn_bwd(q, k, v, o, do, bias, sinks, tm, deps=()):
    t = q.shape[0]
    per_step = tm // BLOCK

    def body(s_ref, q_ref, k_ref, v_ref, o_ref, do_ref, bias_ref, dq_ref, dk_ref, dv_ref, ds_ref):
        step = pl.program_id(0)

        @pl.when(step == 0)
        def _():
            ds_ref[...] = jnp.zeros_like(ds_ref)

        heads = range(N_Q_HEADS)

        def first_matmuls(b):
            i = step * per_step + b
            rows = slice(b * BLOCK, (b + 1) * BLOCK)
            kc, vc = _two_blocks(k_ref, i), _two_blocks(v_ref, i)
            bias_i = bias_ref[jnp.minimum(i, 2)]
            kgs = [kc[:, _head(g)] for g in range(N_KV_HEADS)]
            vgs = [vc[:, _head(g)] for g in range(N_KV_HEADS)]
            qs = [q_ref[rows, _head(hh)] for hh in heads]
            dosb = [do_ref[rows, _head(hh)] for hh in heads]
            dos = [d.astype(F32) for d in dosb]
            scores = [_attn_scores(qs[hh], kgs[hh // GROUP], bias_i) for hh in heads]
            dps = [lax.dot_general(dosb[hh], vgs[hh // GROUP], (((1,), (1,)), ((), ())), preferred_element_type=F32)
                   for hh in heads]
            return kgs, qs, dos, dosb, scores, dps

        dsink = [jnp.zeros((BLOCK, 1), F32) for _ in range(N_Q_HEADS)]
        ahead = None
        for b in range(per_step):
            i = step * per_step + b
            rows = slice(b * BLOCK, (b + 1) * BLOCK)
            kgs, qs, dos, dosb, scores, dps = first_matmuls(b)
            ps, dss = [], []
            for hh in heads:
                p, share = _attn_probs(scores[hh], s_ref[hh])
                drow = jnp.sum(dos[hh] * o_ref[rows, _head(hh)].astype(F32), axis=-1, keepdims=True)
                dss.append((p * (dps[hh] - drow)).astype(BF16))
                ps.append(p.astype(BF16))
                dsink[hh] = dsink[hh] + share * drow
            for hh in heads:
                dq_ref[rows, _head(hh)] = (jnp.dot(dss[hh], kgs[hh // GROUP], preferred_element_type=F32)
                                           * SCALE).astype(BF16)
            groups = [slice(GROUP * g, GROUP * (g + 1)) for g in range(N_KV_HEADS)]
            dkg = [lax.dot_general(jnp.concatenate(dss[gr], axis=0), jnp.concatenate(qs[gr], axis=0),
                                   (((0,), (0,)), ((), ())), preferred_element_type=F32) for gr in groups]
            dvg = [lax.dot_general(jnp.concatenate(ps[gr], axis=0), jnp.concatenate(dosb[gr], axis=0),
                                   (((0,), (0,)), ((), ())), preferred_element_type=F32) for gr in groups]
            dkb, dvb = jnp.concatenate(dkg, axis=1), jnp.concatenate(dvg, axis=1)
            if b == 0:
                @pl.when(step > 0)
                def _():
                    before = pl.ds(pl.multiple_of((i - 1) * BLOCK, BLOCK), BLOCK)
                    dk_ref[before, :] += dkb[:BLOCK]
                    dv_ref[before, :] += dvb[:BLOCK]
            else:
                at = pl.ds(pl.multiple_of((i - 1) * BLOCK, BLOCK), BLOCK)
                dk_ref[at, :] = ahead[0] + dkb[:BLOCK]
                dv_ref[at, :] = ahead[1] + dvb[:BLOCK]
            ahead = (dkb[BLOCK:], dvb[BLOCK:])
        last = pl.ds(pl.multiple_of(((step + 1) * per_step - 1) * BLOCK, BLOCK), BLOCK)
        dk_ref[last, :] = ahead[0]
        dv_ref[last, :] = ahead[1]
        for hh in range(N_Q_HEADS):
            ds_ref[hh:hh + 1, :] -= jnp.sum(dsink[hh])

    whole = pl.BlockSpec((t, KV_W), lambda i: (0, 0))
    blk = pl.BlockSpec((tm, ATTN_W), lambda i: (i, 0))
    body, dep_specs = _behind(body, deps)
    return pl.pallas_call(
        body, name="attn_bwd", grid=(t // tm,),
        in_specs=dep_specs + [pl.BlockSpec(memory_space=pltpu.SMEM), blk, whole, whole, blk, blk,
                              pl.BlockSpec(bias.shape, lambda i: (0, 0, 0))],
        out_specs=[blk, whole, whole, pl.BlockSpec((N_Q_HEADS, 128), lambda i: (0, 0))],
        out_shape=[jax.ShapeDtypeStruct((t, ATTN_W), BF16), jax.ShapeDtypeStruct((t, KV_W), F32),
                   jax.ShapeDtypeStruct((t, KV_W), F32), jax.ShapeDtypeStruct((N_Q_HEADS, 128), F32)],
        compiler_params=_params("arbitrary"),
    )(*deps, sinks, q, k, v, o, do, bias)


def _in_proj_bwd(dq, dk, dv, dbch, w, dh1, h, g, tabs, tm, split_lead=False):
    t = h.shape[0]
    nt = t // tm

    def body(dq_ref, dk_ref, dv_ref, dbch_ref, w_ref, dh1_ref, h_ref, g_ref, c_ref, s_ref, *rest):
        dp_ref, dg_ref = rest[2:4] if split_lead else rest[1:3]
        i = pl.program_id(0)

        @pl.when(i == 0)
        def _():
            dg_ref[...] = jnp.zeros_like(dg_ref)

        cos, sa, sb = _rope_factors(c_ref[...], s_ref[...])
        rep = ATTN_W // (2 * HEAD_DIM)
        dqr = _rope_bwd(dq_ref[...].astype(F32), jnp.tile(cos, (1, rep)), jnp.tile(sa, (1, rep)),
                        jnp.tile(sb, (1, rep)))
        dkr = _rope_bwd(dk_ref[...], cos, sa, sb)
        dp = jnp.concatenate([dqr.astype(BF16), dkr.astype(BF16), dv_ref[...].astype(BF16), dbch_ref[...]], axis=1)
        dp_ref[...] = dp
        da = jnp.dot(dp, w_ref[...], preferred_element_type=F32)
        dx, dg = _rms_bwd(da, h_ref[...], g_ref[...])
        dg_ref[...] += dg
        dh = dh1_ref[...] + dx
        if not split_lead:
            rest[0][...] = dh
            return
        lead_ref, seq_ref, stage, sems = rest[0], rest[1], rest[4], rest[5]

        def copy(j, slot, first):
            if first:
                return pltpu.make_async_copy(stage.at[slot, pl.ds(BLOCK, tm - BLOCK)],
                                             seq_ref.at[pl.ds(0, tm - BLOCK)], sems.at[slot])
            return pltpu.make_async_copy(stage.at[slot], seq_ref.at[pl.ds(pl.multiple_of(j * tm - BLOCK, BLOCK), tm)],
                                         sems.at[slot])

        slot = i % 2
        pl.when(i == 2)(lambda: copy(0, slot, True).wait())
        pl.when(i > 2)(lambda: copy(i - 2, slot, False).wait())
        stage[slot] = dh

        @pl.when(i == 0)
        def _():
            lead_ref[...] = dh[:BLOCK]
            copy(0, slot, True).start()

        pl.when(i > 0)(lambda: copy(i, slot, False).start())

        @pl.when(i == nt - 1)
        def _():
            for j in range(max(nt - 2, 0), nt):
                copy(j, j % 2, j == 0).wait()

    row = lambda n: pl.BlockSpec((tm, n), lambda i: (i, 0))
    full = lambda a: pl.BlockSpec(a.shape, lambda i: (0, 0))
    dh_specs, dh_shapes, scratch = [row(D_MODEL)], [jax.ShapeDtypeStruct((t, D_MODEL), F32)], []
    if split_lead:
        dh_specs = [pl.BlockSpec((BLOCK, D_MODEL), lambda i: (0, 0)), pl.BlockSpec(memory_space=pl.ANY)]
        dh_shapes = [jax.ShapeDtypeStruct((BLOCK, D_MODEL), F32), jax.ShapeDtypeStruct((t - BLOCK, D_MODEL), F32)]
        scratch = [pltpu.VMEM((2, tm, D_MODEL), F32), pltpu.SemaphoreType.DMA((2,))]
    outs = pl.pallas_call(
        body, name="in_proj_bwd", grid=(nt,),
        in_specs=[row(ATTN_W), row(KV_W), row(KV_W), row(3 * CONV_W), full(w), row(D_MODEL), row(D_MODEL), full(g),
                  row(2 * HEAD_DIM), row(2 * HEAD_DIM)],
        out_specs=dh_specs + [row(IN_W), pl.BlockSpec((1, D_MODEL), lambda i: (0, 0))],
        out_shape=dh_shapes + [jax.ShapeDtypeStruct((t, IN_W), BF16), jax.ShapeDtypeStruct((1, D_MODEL), F32)],
        scratch_shapes=scratch,
        compiler_params=_params("arbitrary"),
    )(dq, dk, dv, dbch, w, dh1, h, g, *tabs)
    return (tuple(outs[:2]) if split_lead else outs[0],) + tuple(outs[-2:])


class _Tiles:
    def __init__(self, t):
        self.tm = _row_tile(t, 640)
        self.ts = self.tm
        self.tabs = _rope_tables(t)
        self.bias = _attn_bias()


def _mixer_fwd(h, p, tl, lead=None):
    if lead is None:
        a, q, k, v, b, c, hc = _in_proj(h, p["mix_pre_g"], p["w_in"], tl.tabs, tl.ts)
    else:
        h, a, q, k, v, b, c, hc = _in_proj(h, p["mix_pre_g"], p["w_in"], tl.tabs, tl.ts, lead)
    o = _attn_fwd(q, k, v, tl.bias, p["sinks"], tl.tm)
    return (h, a, q, k, v, b, c, hc, o)


def _out_fwd(mixed, p, tl, deps=()):
    h, a, q, k, v, b, c, hc, o = mixed
    h1, y, z = _mix_out(h, o, b, c, hc, p["conv_w"], p["attn_out_g"], p["conv_out_g"], p["w_out"], p["mix_post_g"],
                        tl.ts, deps)
    return h1, mixed + (h1, y, z)


def _mlp_fwd(h1, saved, p, tl, target=None):
    h2, a2, slope, f = _mlp(h1, p["mlp_pre_g"], p["w_up"], p["w_down"], p["mlp_post_g"], tl.tm, target)
    return h2, saved + (a2, slope, f)


def _mlp_part_bwd(dh, saved, p, tl, deps=()):
    h1, a2, slope, f = saved[9], saved[12], saved[13], saved[14]
    df, dup, dg2 = _mlp_bwd_hidden(dh, f, p["mlp_post_g"], slope, p["w_down"], tl.tm, deps)
    dh1, dg1 = _mlp_bwd_input(dup, p["w_up"], h1, p["mlp_pre_g"], dh, tl.tm)
    g = {"w_down": [d.reshape(N_CHIPS, FF_CHUNK, D_MODEL)
                    for d in _weight_grad(slope, df, "grad_w_down", x_is_slope=True)],
         "w_up": [d.reshape(N_CHIPS, D_MODEL, FF_CHUNK) for d in _weight_grad(a2, dup, "grad_w_up")],
         "mlp_post_g": dg2, "mlp_pre_g": dg1}
    return dh1, g


def _mix_out_part_bwd(dh1, saved, p, tl, deps=()):
    b, c, hc, o, y, z = saved[5], saved[6], saved[7], saved[8], saved[10], saved[11]
    dz, do, dbch, dgp, dga, dgc, dcw = _mix_out_bwd(dh1, z, p["mix_post_g"], p["w_out"], o, b, c, hc, p["conv_w"],
                                                    p["attn_out_g"], p["conv_out_g"], tl.ts, deps)
    g = {"w_out": [d.reshape(N_CHIPS, D_MODEL // N_CHIPS, D_MODEL) for d in _weight_grad(y, dz, "grad_w_out")],
         "mix_post_g": dgp, "attn_out_g": dga, "conv_out_g": dgc, "conv_w": dcw}
    return (dh1, do, dbch), g


def _attn_in_part_bwd(carry, saved, p, tl, deps=(), split_lead=False):
    dh1, do, dbch = carry
    h_in, q, k, v, o = saved[0], saved[2], saved[3], saved[4], saved[8]
    dq, dk, dv, dsink = _attn_bwd(q, k, v, o, do, tl.bias, p["sinks"], tl.tm, deps)
    dh, dproj, dgi = _in_proj_bwd(dq, dk, dv, dbch, p["w_in"], dh1, h_in, p["mix_pre_g"], tl.tabs, tl.ts, split_lead)
    return dh, dproj, {"mix_pre_g": dgi, "sinks": dsink[:, 0]}


def _in_grad(dproj, saved, deps=()):
    return [d.reshape(N_CHIPS, IN_W // N_CHIPS, D_MODEL) for d in _weight_grad(dproj, saved[1], "grad_w_in", deps=deps)]


def _place():
    return lax.axis_index("x"), lax.axis_index("y"), lax.axis_index("c")


def _other_chips(x, y):
    return [(1 - x, y), (x, 1 - y), (1 - x, 1 - y)]


_HBM = pl.BlockSpec(memory_space=pltpu.HBM)
_SEM = pl.BlockSpec(memory_space=pltpu.SEMAPHORE)
_EFFECT = pltpu.SideEffectType.DATAFLOW_SIDE_EFFECTING


class _Exchange:
    def __init__(self, name, bufs, plan, n, after=()):
        self.name, self.plan, nb = name, plan, len(bufs)
        n_in = nb + len(after)

        def body(*refs):
            send, recv, token = refs[n_in], refs[n_in + 1], refs[-1]
            for k, (src, dst, target, _) in enumerate(plan(refs[:nb])):
                pltpu.make_async_remote_copy(src_ref=src, dst_ref=dst, send_sem=send.at[k], recv_sem=recv.at[k],
                                             device_id=target, device_id_type=MESH).start()
            token[...] = jnp.zeros_like(token)

        outs = pl.pallas_call(
            body, name=name + "_start",
            out_shape=(pltpu.SemaphoreType.DMA((n,)), pltpu.SemaphoreType.DMA((n,)),
                       *[pltpu.HBM(b.shape, b.dtype) for b in bufs], jax.ShapeDtypeStruct((8, 128), F32)),
            in_specs=[_HBM] * nb + [pl.BlockSpec(memory_space=pl.ANY)] * len(after),
            out_specs=(_SEM, _SEM, *[_HBM] * nb, pl.BlockSpec(memory_space=pltpu.VMEM)),
            input_output_aliases={i: 2 + i for i in range(nb)},
            compiler_params=pltpu.CompilerParams(has_side_effects=_EFFECT),
        )(*[pltpu.with_memory_space_constraint(b, pltpu.HBM) for b in bufs], *after)
        self.send, self.recv, self.bufs, self.token = outs[0], outs[1], list(outs[2:2 + nb]), outs[-1]

    def wait(self, *after):
        plan, nb = self.plan, len(self.bufs)

        def body(*refs):
            send, recv = refs[nb], refs[nb + 1]
            for k, (src, _, target, land) in enumerate(plan(refs[:nb])):
                cp = pltpu.make_async_remote_copy(src_ref=src, dst_ref=land, send_sem=send.at[k], recv_sem=recv.at[k],
                                                  device_id=target, device_id_type=MESH)
                cp.wait_send()
                cp.wait_recv()

        outs = pl.pallas_call(
            body, name=self.name + "_wait", out_shape=[pltpu.HBM(b.shape, b.dtype) for b in self.bufs],
            in_specs=[_HBM] * nb + [_SEM, _SEM] + [pl.BlockSpec(memory_space=pl.ANY)] * len(after),
            out_specs=[_HBM] * nb, input_output_aliases={i: i for i in range(nb)},
            compiler_params=pltpu.CompilerParams(has_side_effects=_EFFECT),
        )(*self.bufs, self.send, self.recv, *after)
        return list(outs)


def _gather_plan(n):
    def plan(refs):
        x, y, c = _place()
        me = 2 * x + y
        return [(refs[a].at[me], refs[a].at[me], (px, py, c), refs[a].at[2 * px + py])
                for a in range(n) for px, py in _other_chips(x, y)]

    return plan


def _gather_half_plan(n, half_rows):
    def plan(refs):
        x, y, c = _place()
        me = 2 * x + y
        out = []
        for a in range(n):
            rows = pl.ds(c * half_rows[a], half_rows[a])
            out += [(refs[a].at[me, rows], refs[a].at[me, rows], (px, py, c), refs[a].at[2 * px + py, rows])
                    for px, py in _other_chips(x, y)]
        return out

    return plan


def _hand_over_plan(n, half_rows):
    def plan(refs):
        x, y, c = _place()
        out = []
        for a in range(n):
            mine, theirs = pl.ds(c * half_rows[a], half_rows[a]), pl.ds((1 - c) * half_rows[a], half_rows[a])
            for px, py in _other_chips(x, y):
                held = refs[a].at[2 * px + py, mine]
                out.append((held, held, (x, y, 1 - c), refs[a].at[2 * px + py, theirs]))
        return out

    return plan


def _peers():
    x, y, c = _place()
    return [(k - 1, (x ^ (k >> 2), y ^ ((k >> 1) & 1), c ^ (k & 1))) for k in range(1, N_DEV)]


def _scatter_plan(n, half_rows):
    def plan(refs):
        out = []
        for a in range(n):
            hr = half_rows[a]
            for k, (px, py, pc) in _peers():
                out.append((refs[a].at[2 * px + py, pl.ds(pc * hr, hr)], refs[n + a].at[k], (px, py, pc),
                            refs[n + a].at[k]))
        return out

    return plan


def _join_plan(n):
    def plan(refs):
        x, y, c = _place()
        return [(refs[a].at[c], refs[a].at[c], (x, y, 1 - c), refs[a].at[1 - c]) for a in range(n)]

    return plan


def _sum_parts(gs, qs):
    n = len(gs)
    half_rows = [g.shape[1] // 2 for g in gs]
    tr = [_block_rows(hr) for hr in half_rows]
    per = [hr // t for hr, t in zip(half_rows, tr)]
    x, y, c = _place()
    where = jnp.stack([2 * x + y, c]).astype(jnp.int32)

    def body(where_ref, *refs):
        i = pl.program_id(0)
        for a in range(n):
            g_ref, q_ref, o_ref = refs[a], refs[n + a], refs[2 * n + a]

            @pl.when(i < per[a])
            def _():
                total = g_ref[...]
                for k in range(N_DEV - 1):
                    total = total + q_ref[k].astype(F32)
                o_ref[...] = total

    def at(a, i):
        return jnp.minimum(i, per[a] - 1)

    specs_g = [pl.BlockSpec((None, tr[a], gs[a].shape[2]),
                            lambda i, where_ref, a=a: (where_ref[0], where_ref[1] * per[a] + at(a, i), 0)) for a in range(n)]
    specs_q = [pl.BlockSpec((N_DEV - 1, tr[a], gs[a].shape[2]), lambda i, where_ref, a=a: (0, at(a, i), 0))
               for a in range(n)]
    specs_o = [pl.BlockSpec((None, tr[a], gs[a].shape[2]), lambda i, where_ref, a=a: (where_ref[1], at(a, i), 0))
               for a in range(n)]
    return pl.pallas_call(
        body, name="sum_parts",
        grid_spec=pltpu.PrefetchScalarGridSpec(num_scalar_prefetch=1, grid=(max(per),), in_specs=specs_g + specs_q,
                                               out_specs=specs_o),
        out_shape=[jax.ShapeDtypeStruct((2, hr, g.shape[2]), F32) for g, hr in zip(gs, half_rows)],
        compiler_params=_params("arbitrary"),
    )(where, *gs, *qs)


def _all_plan(refs):
    x, y, c = _place()
    mine = refs[0].at[4 * x + 2 * y + c]
    return [(mine, mine, (px, py, pc), refs[0].at[4 * px + 2 * py + pc]) for _, (px, py, pc) in _peers()]


def _sum_devices(parts):
    def body(p_ref, o_ref):
        total = p_ref[0]
        for d in range(1, N_DEV):
            total = total + p_ref[d]
        o_ref[...] = total

    vm = pl.BlockSpec(memory_space=pltpu.VMEM)
    return pl.pallas_call(body, name="sum_devices", in_specs=[vm], out_specs=vm,
                          out_shape=jax.ShapeDtypeStruct(parts.shape[1:], F32))(parts)


def _adamw_math(w, g, m, v):
    m = ADAM_B1 * m + (1.0 - ADAM_B1) * g
    v = ADAM_B2 * v + (1.0 - ADAM_B2) * jnp.square(g)
    m_hat = m / (1.0 - ADAM_B1 ** ADAM_STEP)
    v_hat = v / (1.0 - ADAM_B2 ** ADAM_STEP)
    delta = -ADAM_LR * (m_hat / (jnp.sqrt(v_hat) + ADAM_EPS) + ADAM_WD * w)
    return delta, m, v


def _adamw_large(layer, ws, halves, ms, vs, others):
    n = len(ws)
    tr = [_block_rows(w.shape[1] // 2) for w in ws]
    per = [w.shape[1] // 2 // t for w, t in zip(ws, tr)]
    kept = [] if others is None else [a for four in others for a in four]

    def body(*refs):
        i = pl.program_id(0)
        outs = refs[4 * n + len(kept):]
        for a in range(n):
            w_ref, g_ref, m_ref, v_ref = refs[a], refs[n + a], refs[2 * n + a], refs[3 * n + a]
            g_out, d_ref, nm_ref, nv_ref = outs[4 * a:4 * a + 4]

            @pl.when(i < 2 * per[a])
            def _():
                g = g_ref[...]
                g_out[...] = g
                d_ref[...], nm_ref[...], nv_ref[...] = _adamw_math(w_ref[...], g, m_ref[...], v_ref[...])

    def at(a, i):
        return jnp.minimum(i, 2 * per[a] - 1)

    blk = [pl.BlockSpec((None, tr[a], ws[a].shape[2]), lambda i, a=a: (layer, at(a, i), 0)) for a in range(n)]
    half = [pl.BlockSpec((None, tr[a], ws[a].shape[2]), lambda i, a=a: (at(a, i) // per[a], at(a, i) % per[a], 0))
            for a in range(n)]
    outs = pl.pallas_call(
        body, name="adamw_large", grid=(2 * max(per),),
        in_specs=blk + half + blk + blk + [pl.BlockSpec(memory_space=pl.ANY)] * len(kept),
        out_specs=[blk[a] for a in range(n) for _ in range(4)],
        out_shape=[jax.ShapeDtypeStruct(w.shape, F32) for w in ws for _ in range(4)],
        input_output_aliases={4 * n + k: k for k in range(len(kept))},
        compiler_params=_params("arbitrary"),
    )(*ws, *halves, *ms, *vs, *kept)
    return [outs[4 * a:4 * a + 4] for a in range(n)]


def _adamw_small(ws, gs, ms, vs):
    n = len(ws)

    def body(*refs):
        w_r, g_r, m_r, v_r = refs[:n], refs[n:2 * n], refs[2 * n:3 * n], refs[3 * n:4 * n]
        d_r, nm_r, nv_r = refs[4 * n:5 * n], refs[5 * n:6 * n], refs[6 * n:]
        for a in range(n):
            d_r[a][...], nm_r[a][...], nv_r[a][...] = _adamw_math(w_r[a][...], g_r[a][...], m_r[a][...], v_r[a][...])

    vm = pl.BlockSpec(memory_space=pltpu.VMEM)
    outs = pl.pallas_call(
        body, name="adamw_small", in_specs=[vm] * (4 * n), out_specs=[vm] * (3 * n),
        out_shape=[jax.ShapeDtypeStruct(w.shape, F32) for w in ws] * 3,
    )(*ws, *gs, *ms, *vs)
    return outs[:n], outs[n:2 * n], outs[2 * n:]


_LARGE = ("w_in", "w_out", "w_up", "w_down")
_SMALL = ("meta_tokens", "mix_pre_g", "conv_w", "sinks", "attn_out_g", "conv_out_g", "mix_post_g", "mlp_pre_g",
          "mlp_post_g")
_ORDER = ("meta_tokens", "mix_pre_g", "w_in", "conv_w", "sinks", "attn_out_g", "conv_out_g", "w_out", "mix_post_g",
          "mlp_pre_g", "w_up", "w_down", "mlp_post_g")


class _Reduce:
    def __init__(self, name, grads, after=()):
        self.name, self.n = name, len(grads)
        self.own = [g for g, _ in grads]
        half_rows = [g.shape[1] // 2 for g in self.own]
        zones = [lax.empty((N_DEV - 1, hr, g.shape[2]), BF16) for g, hr in zip(self.own, half_rows)]
        self.exchange = _Exchange(name + "_scatter", [b for _, b in grads] + zones, _scatter_plan(self.n, half_rows),
                                  (N_DEV - 1) * self.n, after)

    @property
    def token(self):
        return self.exchange.token

    def join(self, *after):
        bufs = self.exchange.wait(*after)
        halves = list(_sum_parts(self.own, bufs[self.n:]))
        self.exchange = _Exchange(self.name + "_join", halves, _join_plan(self.n), self.n)

    def done(self, *after):
        return self.exchange.wait(*after)


def _pad_cols(a, n=D_MODEL):
    return jnp.pad(a, ((0, 0), (0, n - a.shape[1])))


def kernel(x, meta_tokens, mix_pre_g, w_in, conv_w, sinks, attn_out_g, conv_out_g, w_out, mix_post_g, mlp_pre_g, w_up, w_down, mlp_post_g, loss_target, m_meta_tokens, m_mix_pre_g, m_w_in, m_conv_w, m_sinks, m_attn_out_g, m_conv_out_g, m_w_out, m_mix_post_g, m_mlp_pre_g, m_w_up, m_w_down, m_mlp_post_g, v_meta_tokens, v_mix_pre_g, v_w_in, v_conv_w, v_sinks, v_attn_out_g, v_conv_out_g, v_w_out, v_mix_post_g, v_mlp_pre_g, v_w_up, v_w_down, v_mlp_post_g):
    w = dict(meta_tokens=meta_tokens, mix_pre_g=mix_pre_g, w_in=w_in, conv_w=conv_w, sinks=sinks,
             attn_out_g=attn_out_g, conv_out_g=conv_out_g, w_out=w_out, mix_post_g=mix_post_g, mlp_pre_g=mlp_pre_g,
             w_up=w_up, w_down=w_down, mlp_post_g=mlp_post_g)
    m = dict(meta_tokens=m_meta_tokens, mix_pre_g=m_mix_pre_g, w_in=m_w_in, conv_w=m_conv_w, sinks=m_sinks,
             attn_out_g=m_attn_out_g, conv_out_g=m_conv_out_g, w_out=m_w_out, mix_post_g=m_mix_post_g,
             mlp_pre_g=m_mlp_pre_g, w_up=m_w_up, w_down=m_w_down, mlp_post_g=m_mlp_post_g)
    v = dict(meta_tokens=v_meta_tokens, mix_pre_g=v_mix_pre_g, w_in=v_w_in, conv_w=v_conv_w, sinks=v_sinks,
             attn_out_g=v_attn_out_g, conv_out_g=v_conv_out_g, w_out=v_w_out, mix_post_g=v_mix_post_g,
             mlp_pre_g=v_mlp_pre_g, w_up=v_w_up, w_down=v_w_down, mlp_post_g=v_mlp_post_g)
    chip = 2 * lax.axis_index("x") + lax.axis_index("y")
    tl = _Tiles(x.shape[1] + BLOCK)

    def zone(quarter):
        return lax.dynamic_update_slice(lax.empty((N_CHIPS,) + quarter.shape, quarter.dtype), quarter[None],
                                        (chip,) + (0,) * quarter.ndim)

    w, m, v = ({**d, "w_in": jnp.swapaxes(d["w_in"], 1, 2)} for d in (w, m, v))
    zones = {n: [zone(w[n][l].astype(BF16)) for l in range(DEPTH)] for n in _LARGE}
    first = _Exchange("gather_first", [zones["w_in"][0], zone(w["conv_w"]), zone(w["meta_tokens"])], _gather_plan(3), 9)
    out0 = _Exchange("gather_out", [zones["w_out"][0]], _gather_plan(1), 3, [first.token])
    mlp_halves = [D_MODEL // 2, FF_CHUNK // 2]
    rest = _Exchange("gather_rest", [zones[n][0] for n in ("w_up", "w_down")], _gather_half_plan(2, mlp_halves), 6,
                     [out0.token])

    def whole_in(quarters):
        return quarters.reshape(IN_W, D_MODEL)

    q_in, q_conv, q_meta = first.wait(rest.token, *tl.tabs, tl.bias)
    conv_whole = jnp.transpose(q_conv, (1, 2, 0, 3)).reshape(DEPTH, CONV_K, CONV_W)
    meta = jnp.transpose(q_meta, (1, 0, 2)).reshape(N_META, D_MODEL)
    p = [{"conv_w": conv_whole[l], "sinks": w["sinks"][l]} for l in range(DEPTH)]
    for l in range(DEPTH):
        for n in ("mix_pre_g", "attn_out_g", "conv_out_g", "mix_post_g", "mlp_pre_g", "mlp_post_g"):
            p[l][n] = w[n][l][None, :]

    lead = jnp.concatenate([jnp.zeros((LEAD_PAD, D_MODEL), F32), meta], axis=0)
    p[0]["w_in"] = whole_in(q_in)
    mixed = _mixer_fwd(x[0], p[0], tl, lead)
    second = _Exchange("gather_second", [zones["w_in"][1], zones["w_out"][1]], _gather_plan(2), 6, [mixed[-1]])
    second_mlp = _Exchange("gather_second_mlp", [zones["w_up"][1], zones["w_down"][1]], _gather_plan(2), 6,
                           [second.token])
    hand_over = _Exchange("hand_over_rest", rest.wait(second_mlp.token), _hand_over_plan(2, mlp_halves), 6)
    p[0]["w_out"], = out0.wait(hand_over.token)
    h1, saved0 = _out_fwd(mixed, p[0], tl)
    p[0]["w_up"], p[0]["w_down"] = hand_over.wait(h1)
    h, saved0 = _mlp_fwd(h1, saved0, p[0], tl)
    q_in, p[1]["w_out"] = second.wait(h)
    p[1]["w_in"] = whole_in(q_in)
    h1, saved1 = _out_fwd(_mixer_fwd(h, p[1], tl), p[1], tl)
    p[1]["w_up"], p[1]["w_down"] = second_mlp.wait(h1)
    (loss_tile, dh), saved1 = _mlp_fwd(h1, saved1, p[1], tl, loss_target[0])

    def adamw(layer, halves, other):
        names = list(halves)
        done = _adamw_large(layer, [w[n] for n in names], [halves[n] for n in names], [m[n] for n in names],
                            [v[n] for n in names], None if other is None else [other[n] for n in names])
        return dict(zip(names, done))

    dh1, g1 = _mlp_part_bwd(dh, saved1, p[1], tl)
    carry, gm = _mix_out_part_bwd(dh1, saved1, p[1], tl)
    dh, dproj, gi = _attn_in_part_bwd(carry, saved1, p[1], tl)
    g1.update(gm, w_in=_in_grad(dproj, saved1), **gi)
    red1 = _Reduce("reduce1", [g1[n] for n in _LARGE])
    dh1, g0 = _mlp_part_bwd(dh, saved0, p[0], tl, [red1.token])
    red1.join(g0["w_down"][0])
    carry, gm = _mix_out_part_bwd(dh1, saved0, p[0], tl, [red1.token])
    first0 = ("w_up", "w_down", "w_out")
    g0.update(gm)
    red0a = _Reduce("reduce0a", [g0[n] for n in first0])
    (dlead, dseq), dproj, gi = _attn_in_part_bwd(carry, saved0, p[0], tl, [red0a.token], split_lead=True)
    g0.update(gi)
    grad_x = dseq[None]
    grads = {n: [g0[n], g1[n]] for n in g0 if n not in _LARGE}

    rows = [dlead[LEAD_PAD:]]
    for n in ("mix_pre_g", "mix_post_g", "mlp_pre_g", "mlp_post_g"):
        rows += grads[n]
    rows += [jnp.concatenate([grads["attn_out_g"][l], grads["conv_out_g"][l]], axis=1) for l in range(DEPTH)]
    rows.append(jnp.concatenate(grads["conv_w"], axis=1))
    rows.append(_pad_cols(jnp.concatenate(grads["sinks"])[None, :]))
    rows.append(_pad_cols(loss_tile[:1]))
    packed = jnp.concatenate(rows, axis=0)
    packed = jnp.pad(packed, ((0, SMALL_ROWS - packed.shape[0]), (0, 0)))
    device = 2 * chip + lax.axis_index("c")
    small_parts = _Exchange("gather_small", [lax.dynamic_update_slice(lax.empty((N_DEV,) + packed.shape, F32),
                                                                      packed[None], (device, 0, 0))], _all_plan, N_DEV - 1)
    g0["w_in"] = _in_grad(dproj, saved0, [small_parts.token])
    red0b = _Reduce("reduce0b", [g0["w_in"]])
    done1 = adamw(1, dict(zip(_LARGE, red1.done(red0b.token))), None)
    total = _sum_devices(small_parts.wait(*[done1[n][0] for n in _LARGE])[0])
    r0 = N_META
    small = {
        "meta_tokens": lax.dynamic_slice(total[:N_META], (0, chip * (D_MODEL // N_CHIPS)), (N_META, D_MODEL // N_CHIPS)),
        "mix_pre_g": total[r0:r0 + 2], "mix_post_g": total[r0 + 2:r0 + 4], "mlp_pre_g": total[r0 + 4:r0 + 6],
        "mlp_post_g": total[r0 + 6:r0 + 8],
        "attn_out_g": total[r0 + 8:r0 + 10, :ATTN_W], "conv_out_g": total[r0 + 8:r0 + 10, ATTN_W:],
        "conv_w": lax.dynamic_slice(total[r0 + 10:r0 + 13].reshape(CONV_K, DEPTH, CONV_W).transpose(1, 0, 2),
                                    (0, 0, chip * (CONV_W // N_CHIPS)), (DEPTH, CONV_K, CONV_W // N_CHIPS)),
        "sinks": total[r0 + 13, :DEPTH * N_Q_HEADS].reshape(DEPTH, N_Q_HEADS),
    }
    loss = total[r0 + 14, 0]

    ds, nms, nvs = _adamw_small([w[n] for n in _SMALL], [small[n] for n in _SMALL], [m[n] for n in _SMALL],
                                [v[n] for n in _SMALL])
    red0a.join(ds[0], grad_x)
    red0b.join(red0a.token)
    done0 = adamw(0, dict(zip(first0, red0a.done(red0b.token))), done1)
    done0.update(adamw(0, {"w_in": red0b.done(done0["w_down"][0])[0]}, done1))
    grad, delta, new_m, new_v = {}, {}, {}, {}
    for n in _LARGE:
        grad[n], delta[n], new_m[n], new_v[n] = done0[n]
    for d in (grad, delta, new_m, new_v):
        d["w_in"] = jnp.swapaxes(d["w_in"], 1, 2)
    for i, n in enumerate(_SMALL):
        grad[n], delta[n], new_m[n], new_v[n] = small[n], ds[i], nms[i], nvs[i]
    return (loss, grad_x, *[grad[n] for n in _ORDER], *[delta[n] for n in _ORDER], *[new_m[n] for n in _ORDER],
            *[new_v[n] for n in _ORDER])
```

```python
import jax
import jax.numpy as jnp
from jax import lax
from jax.experimental import pallas as pl
from jax.experimental.pallas import tpu as pltpu

F32 = jnp.float32
BF16 = jnp.bfloat16

D_MODEL = 1024
DEPTH = 2
N_META = 16
ATTN_W = 512
CONV_W = 512
HEAD_DIM = 64
N_Q_HEADS = 8
N_KV_HEADS = 2
GROUP = N_Q_HEADS // N_KV_HEADS
KV_W = N_KV_HEADS * HEAD_DIM
CONV_K = 3
BLOCK = 128
LEAD_PAD = BLOCK - N_META
ROPE_THETA = 500000.0
ROT_DIM = HEAD_DIM // 4
ROT_HALF = ROT_DIM // 2
D_FF = 4 * D_MODEL
IN_W = ATTN_W + 2 * KV_W + 3 * CONV_W
QKV_W = ATTN_W + 2 * KV_W
EPS = 1e-6
SCALE = HEAD_DIM ** -0.5
FF_CHUNK = 1024
N_CHIPS = 4
N_DEV = 8

ADAM_LR = 0.001
ADAM_B1 = 0.9
ADAM_B2 = 0.999
ADAM_EPS = 1e-08
ADAM_WD = 0.01
ADAM_STEP = 10

V7X_VMEM_LIMIT = 60 * 1024 * 1024
SMALL_ROWS = 32

MESH = pl.DeviceIdType.MESH


def _params(*sem):
    return pltpu.CompilerParams(dimension_semantics=sem, vmem_limit_bytes=V7X_VMEM_LIMIT)


def _block_rows(n):
    return max(r for r in range(16, min(n, 64) + 1, 16) if n % r == 0)


def _row_tile(t, most):
    nb = t // BLOCK
    for b in range(most // BLOCK, 0, -1):
        if nb % b == 0:
            return b * BLOCK
    return BLOCK


def _behind(body, deps):
    n = len(deps)

    def wrapped(*refs):
        body(*refs[n:])

    return wrapped, [pl.BlockSpec(memory_space=pl.ANY)] * n


def _rms(x, g):
    r = lax.rsqrt(jnp.mean(x * x, axis=-1, keepdims=True) + EPS)
    return x * r * g


def _rms_bwd(dy, x, g):
    r = lax.rsqrt(jnp.mean(x * x, axis=-1, keepdims=True) + EPS)
    xh = x * r
    dg = jnp.sum(dy * xh, axis=0, keepdims=True)
    dxh = dy * g
    dx = r * (dxh - xh * jnp.mean(dxh * xh, axis=-1, keepdims=True))
    return dx, dg


def _rope(x, cos, sa, sb):
    n = x.shape[-1]
    return x * cos + pltpu.roll(x, n - ROT_HALF, 1) * sa + pltpu.roll(x, ROT_HALF, 1) * sb


def _rope_bwd(dy, cos, sa, sb):
    n = dy.shape[-1]
    return dy * cos + pltpu.roll(dy * sa, ROT_HALF, 1) + pltpu.roll(dy * sb, n - ROT_HALF, 1)


def _rope_tables(t):
    pos = lax.broadcasted_iota(jnp.int32, (t, ROT_HALF), 0).astype(F32) - LEAD_PAD
    pair = lax.broadcasted_iota(jnp.int32, (t, ROT_HALF), 1).astype(F32)
    inv_freq = jnp.power(jnp.float32(ROPE_THETA), -(2.0 * pair) / ROT_DIM)
    ang = pos * inv_freq
    cos, sin = lax.optimization_barrier((jnp.cos(ang), jnp.sin(ang)))
    spread = (1, 2 * HEAD_DIM // ROT_HALF)
    cos, sin = jnp.tile(cos, spread), jnp.tile(sin, spread)
    dim = lax.broadcasted_iota(jnp.int32, (t, 2 * HEAD_DIM), 1) % HEAD_DIM
    return jnp.where(dim < ROT_DIM, cos, 1.0), jnp.where(dim < ROT_DIM, sin, 0.0)


def _rope_factors(cos, sin):
    dim = lax.broadcasted_iota(jnp.int32, sin.shape, 1) % HEAD_DIM
    return cos, jnp.where(dim < ROT_HALF, -sin, 0.0), jnp.where(dim >= ROT_HALF, sin, 0.0)


def _in_proj(h, g, w, tabs, tm, lead=None):
    t = h.shape[0] + (0 if lead is None else BLOCK)
    per_step = 0 if lead is None else tm // BLOCK

    def body(*refs):
        if lead is None:
            x = refs[0][...]
            refs = refs[1:]
        else:
            blocks = [r[...] for r in refs[1:1 + per_step]]
            blocks[0] = jnp.where(pl.program_id(0) == 0, refs[0][...], blocks[0])
            x = jnp.concatenate(blocks, axis=0)
            first_out = 1 + per_step + 4
            refs[first_out][...] = x
            refs = refs[1 + per_step:first_out] + refs[first_out + 1:]
        g_ref, w_ref, c_ref, s_ref, a_ref, q_ref, k_ref, v_ref, b_ref, cg_ref, hc_ref = refs
        a = _rms(x, g_ref[...]).astype(BF16)
        a_ref[...] = a
        p = lax.dot_general(a, w_ref[...], (((1,), (1,)), ((), ())), preferred_element_type=F32)
        cos, sa, sb = _rope_factors(c_ref[...], s_ref[...])
        rep = ATTN_W // (2 * HEAD_DIM)
        q = _rope(p[:, :ATTN_W], jnp.tile(cos, (1, rep)), jnp.tile(sa, (1, rep)), jnp.tile(sb, (1, rep)))
        q_ref[...] = (q * SCALE).astype(BF16)
        k_ref[...] = _rope(p[:, ATTN_W:ATTN_W + KV_W], cos, sa, sb).astype(BF16)
        v_ref[...] = p[:, ATTN_W + KV_W:QKV_W].astype(BF16)
        b_ref[...] = p[:, QKV_W:QKV_W + CONV_W].astype(BF16)
        cg_ref[...] = p[:, QKV_W + CONV_W:QKV_W + 2 * CONV_W].astype(BF16)
        hc_ref[...] = p[:, QKV_W + 2 * CONV_W:].astype(BF16)

    row = lambda n: pl.BlockSpec((tm, n), lambda i: (i, 0))
    full = lambda a: pl.BlockSpec(a.shape, lambda i: (0, 0))

    def sequence_block(b):
        return pl.BlockSpec((BLOCK, D_MODEL), lambda i: (jnp.maximum(i * per_step + b - 1, 0), 0))

    if lead is None:
        first_in, first_args, first_out, first_shape = [row(D_MODEL)], [h], [], []
    else:
        first_in = [full(lead)] + [sequence_block(b) for b in range(per_step)]
        first_args = [lead] + [h] * per_step
        first_out, first_shape = [row(D_MODEL)], [jax.ShapeDtypeStruct((t, D_MODEL), F32)]
    return pl.pallas_call(
        body, name="in_proj", grid=(t // tm,),
        in_specs=first_in + [full(g), full(w), row(2 * HEAD_DIM), row(2 * HEAD_DIM)],
        out_specs=first_out + [row(D_MODEL), row(ATTN_W), row(KV_W), row(KV_W), row(CONV_W), row(CONV_W), row(CONV_W)],
        out_shape=first_shape + [jax.ShapeDtypeStruct((t, D_MODEL), BF16), jax.ShapeDtypeStruct((t, ATTN_W), BF16),
                                 jax.ShapeDtypeStruct((t, KV_W), BF16), jax.ShapeDtypeStruct((t, KV_W), BF16),
                                 jax.ShapeDtypeStruct((t, CONV_W), BF16), jax.ShapeDtypeStruct((t, CONV_W), BF16),
                                 jax.ShapeDtypeStruct((t, CONV_W), BF16)],
        compiler_params=_params("parallel"),
    )(*first_args, g, w, *tabs)


def _attn_bias():
    r = lax.broadcasted_iota(jnp.int32, (3, BLOCK, 2 * BLOCK), 1)
    c = lax.broadcasted_iota(jnp.int32, (3, BLOCK, 2 * BLOCK), 2)
    i = lax.broadcasted_iota(jnp.int32, (3, BLOCK, 2 * BLOCK), 0)
    ok = (c > r) & (c <= r + BLOCK) & (c + (i - 1) * BLOCK >= LEAD_PAD)
    return jnp.where(ok, 0.0, -jnp.inf).astype(F32)


def _attn_scores(qh, kg, bias):
    return lax.dot_general(qh, kg, (((1,), (1,)), ((), ())), preferred_element_type=F32) + bias


def _attn_probs(s, sk):
    m = jnp.maximum(jnp.max(s, axis=-1, keepdims=True), sk)
    e = jnp.exp(s - m)
    es = jnp.exp(sk - m)
    rden = 1.0 / (jnp.sum(e, axis=-1, keepdims=True) + es)
    return e * rden, es * rden


def _head(hh):
    return slice(hh * HEAD_DIM, (hh + 1) * HEAD_DIM)


def _two_blocks(ref, i):
    prev = jnp.maximum(i - 1, 0)
    return jnp.concatenate([ref[pl.ds(pl.multiple_of(prev * BLOCK, BLOCK), BLOCK), :],
                            ref[pl.ds(pl.multiple_of(i * BLOCK, BLOCK), BLOCK), :]], axis=0)


def _attn_fwd(q, k, v, bias, sinks, tm):
    t = q.shape[0]
    per_step = tm // BLOCK
    heads = range(N_Q_HEADS)

    def body(s_ref, q_ref, k_ref, v_ref, bias_ref, o_ref):
        for b in range(per_step):
            i = pl.program_id(0) * per_step + b
            rows = slice(b * BLOCK, (b + 1) * BLOCK)
            kc, vc = _two_blocks(k_ref, i), _two_blocks(v_ref, i)
            bias_i = bias_ref[jnp.minimum(i, 2)]
            scores = [_attn_scores(q_ref[rows, _head(hh)], kc[:, _head(hh // GROUP)], bias_i) for hh in heads]
            probs = [_attn_probs(scores[hh], s_ref[hh])[0].astype(BF16) for hh in heads]
            for hh in heads:
                o_ref[rows, _head(hh)] = jnp.dot(probs[hh], vc[:, _head(hh // GROUP)],
                                                 preferred_element_type=F32).astype(BF16)

    whole = pl.BlockSpec((t, KV_W), lambda i: (0, 0))
    return pl.pallas_call(
        body, name="attn_fwd", grid=(t // tm,),
        in_specs=[pl.BlockSpec(memory_space=pltpu.SMEM), pl.BlockSpec((tm, ATTN_W), lambda i: (i, 0)), whole, whole,
                  pl.BlockSpec(bias.shape, lambda i: (0, 0, 0))],
        out_specs=pl.BlockSpec((tm, ATTN_W), lambda i: (i, 0)),
        out_shape=jax.ShapeDtypeStruct((t, ATTN_W), BF16),
        compiler_params=_params("parallel"),
    )(sinks, q, k, v, bias)


def _shift_rows(u, halo, n):
    r = pltpu.roll(u, n, 0)
    hr = pltpu.roll(halo, n, 0)
    idx = lax.broadcasted_iota(jnp.int32, hr.shape, 0)
    return jnp.concatenate([jnp.where(idx < n, hr, r[:8]), r[8:]], axis=0)


def _advance_rows(u, halo, n):
    rows = u.shape[0]
    r = pltpu.roll(u, rows - n, 0)
    hr = pltpu.roll(halo, 8 - n, 0)
    idx = lax.broadcasted_iota(jnp.int32, hr.shape, 0)
    return jnp.concatenate([r[:rows - 8], jnp.where(idx >= 8 - n, hr, r[rows - 8:])], axis=0)


def _mix_out(h, o, b, c, hc, cw, ga, gc, w, gp, tm, deps=()):
    t = h.shape[0]

    def body(h_ref, o_ref, b_ref, c_ref, hc_ref, cw_ref, ga_ref, gc_ref, w_ref, gp_ref, h1_ref, y_ref, z_ref, halo):
        @pl.when(pl.program_id(0) == 0)
        def _():
            halo[...] = jnp.zeros_like(halo)

        u = c_ref[...].astype(F32) * hc_ref[...].astype(F32)
        cv = cw_ref[0:1, :] * _shift_rows(u, halo[...], 2) + cw_ref[1:2, :] * _shift_rows(u, halo[...], 1) \
            + cw_ref[2:3, :] * u
        halo[...] = u[tm - 8:]
        yc = b_ref[...].astype(F32) * cv
        y = jnp.concatenate([_rms(o_ref[...].astype(F32), ga_ref[...]), _rms(yc, gc_ref[...])], axis=1).astype(BF16)
        y_ref[...] = y
        z = jnp.dot(y, w_ref[...].reshape(D_MODEL, D_MODEL), preferred_element_type=F32)
        z_ref[...] = z
        h1_ref[...] = h_ref[...] + _rms(z, gp_ref[...])

    row = lambda n: pl.BlockSpec((tm, n), lambda i: (i, 0))
    full = lambda a: pl.BlockSpec(a.shape, lambda i: (0,) * a.ndim)
    body, dep_specs = _behind(body, deps)
    return pl.pallas_call(
        body, name="mix_out", grid=(t // tm,),
        in_specs=dep_specs + [row(D_MODEL), row(ATTN_W), row(CONV_W), row(CONV_W), row(CONV_W), full(cw), full(ga),
                              full(gc), full(w), full(gp)],
        out_specs=[row(D_MODEL), row(D_MODEL), row(D_MODEL)],
        out_shape=[jax.ShapeDtypeStruct((t, D_MODEL), F32), jax.ShapeDtypeStruct((t, D_MODEL), BF16),
                   jax.ShapeDtypeStruct((t, D_MODEL), F32)],
        scratch_shapes=[pltpu.VMEM((8, CONV_W), F32)],
        compiler_params=_params("arbitrary"),
    )(*deps, h, o, b, c, hc, cw, ga, gc, w, gp)


def _mlp(h1, g1, wu, wd, g2, tm, target=None):
    t = h1.shape[0]
    nj = D_FF // FF_CHUNK
    per_step = tm // BLOCK if target is not None else 0

    def body(h1_ref, g1_ref, wu_ref, wd_ref, g2_ref, *rest):
        t_refs, outs = rest[:per_step], rest[per_step:]
        a2_ref, slope_ref = (outs[1], outs[2]) if target is None else (outs[2], outs[3])
        a2 = _rms(h1_ref[...], g1_ref[...]).astype(BF16)
        a2_ref[...] = a2
        f = None
        for j in range(nj):
            up = jnp.dot(a2, wu_ref[j], preferred_element_type=F32)
            r = jnp.maximum(up, 0.0)
            slope_ref[:, j * FF_CHUNK:(j + 1) * FF_CHUNK] = (r + r).astype(BF16)
            part = jnp.dot((r * r).astype(BF16), wd_ref[j], preferred_element_type=F32)
            f = part if f is None else f + part
        h2 = h1_ref[...] + _rms(f, g2_ref[...])
        if target is None:
            outs[0][...] = h2
            outs[3][...] = f
            return
        loss_ref, dh_ref, df_ref, dg2_ref = outs[0], outs[1], outs[4], outs[5]
        i = pl.program_id(0)

        @pl.when(i == 0)
        def _():
            loss_ref[...] = jnp.zeros_like(loss_ref)
            dg2_ref[...] = jnp.zeros_like(dg2_ref)

        total = jnp.zeros((), F32)
        dh2 = []
        for b in range(per_step):
            err = h2[b * BLOCK:(b + 1) * BLOCK] - t_refs[b][...]
            if b == 0:
                err = jnp.where(i == 0, 0.0, err)
            dh2.append(err * (1.0 / D_MODEL))
            total = total + jnp.sum(err * err)
        loss_ref[...] += total * (0.5 / D_MODEL)
        dh2 = jnp.concatenate(dh2, axis=0)
        dh_ref[...] = dh2
        df, dg = _rms_bwd(dh2, f, g2_ref[...])
        df_ref[...] = df.astype(BF16)
        dg2_ref[...] += dg

    def target_block(b):
        return pl.BlockSpec((BLOCK, D_MODEL), lambda i: (jnp.maximum(i * per_step + b - 1, 0), 0))

    row = pl.BlockSpec((tm, D_MODEL), lambda i: (i, 0))
    vec = pl.BlockSpec((1, D_MODEL), lambda i: (0, 0))
    resident = pl.BlockSpec(memory_space=pltpu.VMEM)
    wide = pl.BlockSpec((tm, D_FF), lambda i: (i, 0))
    kept = [jax.ShapeDtypeStruct((t, D_MODEL), BF16), jax.ShapeDtypeStruct((t, D_FF), BF16)]
    if target is None:
        specs = [row, row, wide, row]
        shapes = [jax.ShapeDtypeStruct((t, D_MODEL), F32)] + kept + [jax.ShapeDtypeStruct((t, D_MODEL), F32)]
    else:
        specs = [pl.BlockSpec((8, 128), lambda i: (0, 0)), row, row, wide, row, vec]
        shapes = [jax.ShapeDtypeStruct((8, 128), F32), jax.ShapeDtypeStruct((t, D_MODEL), F32)] + kept \
            + [jax.ShapeDtypeStruct((t, D_MODEL), BF16), jax.ShapeDtypeStruct((1, D_MODEL), F32)]
    outs = pl.pallas_call(
        body, name="mlp", grid=(t // tm,),
        in_specs=[row, vec, resident, resident, vec] + [target_block(b) for b in range(per_step)],
        out_specs=specs, out_shape=shapes,
        compiler_params=_params("parallel" if target is None else "arbitrary"),
    )(h1, g1, wu, wd, g2, *([target] * per_step))
    if target is None:
        return tuple(outs)
    return (tuple(outs[:2]), outs[2], outs[3], tuple(outs[4:]))


def _mlp_bwd_hidden(dh2, f, g2, slope, wd, tm, deps=(), df=None):
    t = slope.shape[0]
    nj = D_FF // FF_CHUNK

    def hidden(df, slope_ref, wd_ref, dup_ref):
        for j in range(nj):
            cols = slice(j * FF_CHUNK, (j + 1) * FF_CHUNK)
            dact = lax.dot_general(df, wd_ref[j], (((1,), (1,)), ((), ())), preferred_element_type=F32)
            dup_ref[:, cols] = (dact * slope_ref[:, cols].astype(F32)).astype(BF16)

    def body(dh2_ref, f_ref, g2_ref, slope_ref, wd_ref, df_ref, dup_ref, dg2_ref):
        @pl.when(pl.program_id(0) == 0)
        def _():
            dg2_ref[...] = jnp.zeros_like(dg2_ref)

        df, dg = _rms_bwd(dh2_ref[...], f_ref[...], g2_ref[...])
        dg2_ref[...] += dg
        df = df.astype(BF16)
        df_ref[...] = df
        hidden(df, slope_ref, wd_ref, dup_ref)

    def body_from_df(df_ref, slope_ref, wd_ref, dup_ref):
        hidden(df_ref[...], slope_ref, wd_ref, dup_ref)

    row = pl.BlockSpec((tm, D_MODEL), lambda i: (i, 0))
    wide = pl.BlockSpec((tm, D_FF), lambda i: (i, 0))
    vec = pl.BlockSpec((1, D_MODEL), lambda i: (0, 0))
    resident = pl.BlockSpec(memory_space=pltpu.VMEM)
    if df is not None:
        body_from_df, dep_specs = _behind(body_from_df, deps)
        return pl.pallas_call(
            body_from_df, name="mlp_bwd_hidden", grid=(t // tm,), in_specs=dep_specs + [row, wide, resident],
            out_specs=wide, out_shape=jax.ShapeDtypeStruct((t, D_FF), BF16), compiler_params=_params("parallel"),
        )(*deps, df, slope, wd)
    body, dep_specs = _behind(body, deps)
    return pl.pallas_call(
        body, name="mlp_bwd_hidden", grid=(t // tm,),
        in_specs=dep_specs + [row, row, vec, wide, resident],
        out_specs=[row, wide, vec],
        out_shape=[jax.ShapeDtypeStruct((t, D_MODEL), BF16), jax.ShapeDtypeStruct((t, D_FF), BF16),
                   jax.ShapeDtypeStruct((1, D_MODEL), F32)],
        compiler_params=_params("arbitrary"),
    )(*deps, dh2, f, g2, slope, wd)


def _mlp_bwd_input(dup, wu, h1, g1, dh2, tm):
    t = dh2.shape[0]
    nj = D_FF // FF_CHUNK

    def body(dup_ref, wu_ref, h1_ref, g1_ref, dh2_ref, dh1_ref, dg1_ref):
        @pl.when(pl.program_id(0) == 0)
        def _():
            dg1_ref[...] = jnp.zeros_like(dg1_ref)

        da2 = None
        for j in range(nj):
            part = lax.dot_general(dup_ref[:, j * FF_CHUNK:(j + 1) * FF_CHUNK], wu_ref[j], (((1,), (1,)), ((), ())),
                                   preferred_element_type=F32)
            da2 = part if da2 is None else da2 + part
        dx, dg = _rms_bwd(da2, h1_ref[...], g1_ref[...])
        dh1_ref[...] = dh2_ref[...] + dx
        dg1_ref[...] += dg

    row = pl.BlockSpec((tm, D_MODEL), lambda i: (i, 0))
    vec = pl.BlockSpec((1, D_MODEL), lambda i: (0, 0))
    return pl.pallas_call(
        body, name="mlp_bwd_input", grid=(t // tm,),
        in_specs=[pl.BlockSpec((tm, D_FF), lambda i: (i, 0)), pl.BlockSpec(memory_space=pltpu.VMEM), row, vec, row],
        out_specs=[row, vec],
        out_shape=[jax.ShapeDtypeStruct((t, D_MODEL), F32), jax.ShapeDtypeStruct((1, D_MODEL), F32)],
        compiler_params=_params("arbitrary"),
    )(dup, wu, h1, g1, dh2)


def _row_split(t):
    tile = min(t, 1024)
    return tile, t // tile, t % tile


def _row_split_specs(t, cols):
    tile, whole, rest = _row_split(t)
    specs = [pl.BlockSpec((tile, cols), lambda r: (jnp.minimum(r, whole - 1), 0))]
    if rest:
        specs.append(pl.BlockSpec((rest, cols), lambda r: (whole * tile // rest, 0)))
    return specs


def _weight_grad(x, y, name, x_is_slope=False, deps=()):
    t, k = x.shape
    n = y.shape[1]
    tn = FF_CHUNK
    tk = FF_CHUNK if k % FF_CHUNK == 0 else k
    _, whole, rest = _row_split(t)
    steps = whole + bool(rest)

    def body(*refs):
        o_ref, ob_ref, r = refs[-2], refs[-1], pl.program_id(0)

        @pl.when(r == 0)
        def _():
            o_ref[...] = jnp.zeros_like(o_ref)

        def add(x_ref, y_ref):
            for a in range(k // tk):
                xv = x_ref[:, a * tk:(a + 1) * tk]
                if x_is_slope:
                    xv = xv.astype(F32)
                    xv = (xv * xv * 0.25).astype(BF16)
                for b in range(n // tn):
                    o_ref[a, b] += lax.dot_general(xv, y_ref[:, b * tn:(b + 1) * tn], (((0,), (0,)), ((), ())),
                                                   preferred_element_type=F32)

        if rest:
            pl.when(r < whole)(lambda: add(refs[0], refs[2]))
            pl.when(r == whole)(lambda: add(refs[1], refs[3]))
        else:
            add(refs[0], refs[1])

        @pl.when(r == steps - 1)
        def _():
            ob_ref[...] = o_ref[...].astype(BF16)

    vm = pl.BlockSpec(memory_space=pltpu.VMEM)
    body, dep_specs = _behind(body, deps)
    return pl.pallas_call(
        body, name=name, grid=(steps,),
        in_specs=dep_specs + _row_split_specs(t, k) + _row_split_specs(t, n), out_specs=[vm, vm],
        out_shape=[jax.ShapeDtypeStruct((k // tk, n // tn, tk, tn), F32),
                   jax.ShapeDtypeStruct((k // tk, n // tn, tk, tn), BF16)],
        compiler_params=_params("arbitrary"),
    )(*deps, *([x] * (1 + bool(rest))), *([y] * (1 + bool(rest))))


def _mix_out_bwd(dh1, z, gp, w, o, b, c, hc, cw, ga, gc, tm, deps=()):
    t = dh1.shape[0]
    nt = t // tm
    per16 = tm // 16

    def body(dh1_ref, z_ref, gp_ref, w_ref, o_ref, b_ref, c_ref, hc_ref, cp_ref, hp_ref, cw_ref, ga_ref, gc_ref,
             dz_ref, do_ref, dbch_ref, dgp_ref, dga_ref, dgc_ref, dcw_ref, halo):
        i = pl.program_id(0)

        @pl.when(i == 0)
        def _():
            halo[...] = jnp.zeros_like(halo)
            dgp_ref[...] = jnp.zeros_like(dgp_ref)
            dga_ref[...] = jnp.zeros_like(dga_ref)
            dgc_ref[...] = jnp.zeros_like(dgc_ref)
            dcw_ref[...] = jnp.zeros_like(dcw_ref)

        dz, dgp = _rms_bwd(dh1_ref[...], z_ref[...], gp_ref[...])
        dgp_ref[...] += dgp
        dz = dz.astype(BF16)
        dz_ref[...] = dz
        dy = lax.dot_general(dz, w_ref[...].reshape(D_MODEL, D_MODEL), (((1,), (1,)), ((), ())),
                             preferred_element_type=F32)
        do, dga = _rms_bwd(dy[:, :ATTN_W], o_ref[...].astype(F32), ga_ref[...])
        do_ref[...] = do.astype(BF16)
        dga_ref[...] += dga

        cc, hh = c_ref[...].astype(F32), hc_ref[...].astype(F32)
        u = cc * hh
        first = i == nt - 1
        u_before = jnp.where(first, 0.0, (cp_ref[...].astype(F32) * hp_ref[...].astype(F32))[8:])
        u1 = _shift_rows(u, u_before, 1)
        u2 = _shift_rows(u, u_before, 2)
        cv = cw_ref[0:1, :] * u2 + cw_ref[1:2, :] * u1 + cw_ref[2:3, :] * u
        bb = b_ref[...].astype(F32)
        dyc, dgc = _rms_bwd(dy[:, ATTN_W:], bb * cv, gc_ref[...])
        dgc_ref[...] += dgc
        dcv = dyc * bb
        d1 = _advance_rows(dcv, halo[...], 1)
        d2 = _advance_rows(dcv, halo[...], 2)
        halo[...] = dcv[:8]
        du = cw_ref[2:3, :] * dcv + cw_ref[1:2, :] * d1 + cw_ref[0:1, :] * d2
        dbch_ref[...] = jnp.concatenate([dyc * cv, du * hh, du * cc], axis=1).astype(BF16)
        dcw_ref[...] += jnp.concatenate([jnp.sum(dcv * u2, axis=0, keepdims=True),
                                         jnp.sum(dcv * u1, axis=0, keepdims=True),
                                         jnp.sum(dcv * u, axis=0, keepdims=True)], axis=0)

    row = lambda n: pl.BlockSpec((tm, n), lambda i: (nt - 1 - i, 0))
    before = pl.BlockSpec((16, CONV_W), lambda i: (jnp.maximum((nt - 1 - i) * per16 - 1, 0), 0))
    full = lambda a: pl.BlockSpec(a.shape, lambda i: (0,) * a.ndim)
    vec = lambda n: pl.BlockSpec((1, n), lambda i: (0, 0))
    body, dep_specs = _behind(body, deps)
    return pl.pallas_call(
        body, name="mix_out_bwd", grid=(nt,),
        in_specs=dep_specs + [row(D_MODEL), row(D_MODEL), full(gp), full(w), row(ATTN_W), row(CONV_W), row(CONV_W),
                              row(CONV_W), before, before, full(cw), full(ga), full(gc)],
        out_specs=[row(D_MODEL), row(ATTN_W), row(3 * CONV_W), vec(D_MODEL), vec(ATTN_W), vec(CONV_W),
                   pl.BlockSpec((CONV_K, CONV_W), lambda i: (0, 0))],
        out_shape=[jax.ShapeDtypeStruct((t, D_MODEL), BF16), jax.ShapeDtypeStruct((t, ATTN_W), BF16),
                   jax.ShapeDtypeStruct((t, 3 * CONV_W), BF16), jax.ShapeDtypeStruct((1, D_MODEL), F32),
                   jax.ShapeDtypeStruct((1, ATTN_W), F32), jax.ShapeDtypeStruct((1, CONV_W), F32),
                   jax.ShapeDtypeStruct((CONV_K, CONV_W), F32)],
        scratch_shapes=[pltpu.VMEM((8, CONV_W), F32)],
        compiler_params=_params("arbitrary"),
    )(*deps, dh1, z, gp, w, o, b, c, hc, c, hc, cw, ga, gc)


def _attn_bwd(q, k, v, o, do, bias, sinks, tm, deps=()):
    t = q.shape[0]
    per_step = tm // BLOCK

    def body(s_ref, q_ref, k_ref, v_ref, o_ref, do_ref, bias_ref, dq_ref, dk_ref, dv_ref, ds_ref):
        step = pl.program_id(0)

        @pl.when(step == 0)
        def _():
            ds_ref[...] = jnp.zeros_like(ds_ref)

        heads = range(N_Q_HEADS)

        def first_matmuls(b):
            i = step * per_step + b
            rows = slice(b * BLOCK, (b + 1) * BLOCK)
            kc, vc = _two_blocks(k_ref, i), _two_blocks(v_ref, i)
            bias_i = bias_ref[jnp.minimum(i, 2)]
            kgs = [kc[:, _head(g)] for g in range(N_KV_HEADS)]
            vgs = [vc[:, _head(g)] for g in range(N_KV_HEADS)]
            qs = [q_ref[rows, _head(hh)] for hh in heads]
            dosb = [do_ref[rows, _head(hh)] for hh in heads]
            dos = [d.astype(F32) for d in dosb]
            scores = [_attn_scores(qs[hh], kgs[hh // GROUP], bias_i) for hh in heads]
            dps = [lax.dot_general(dosb[hh], vgs[hh // GROUP], (((1,), (1,)), ((), ())), preferred_element_type=F32)
                   for hh in heads]
            return kgs, qs, dos, dosb, scores, dps

        dsink = [jnp.zeros((BLOCK, 1), F32) for _ in range(N_Q_HEADS)]
        ahead = None
        for b in range(per_step):
            i = step * per_step + b
            rows = slice(b * BLOCK, (b + 1) * BLOCK)
            kgs, qs, dos, dosb, scores, dps = first_matmuls(b)
            ps, dss = [], []
            for hh in heads:
                p, share = _attn_probs(scores[hh], s_ref[hh])
                drow = jnp.sum(dos[hh] * o_ref[rows, _head(hh)].astype(F32), axis=-1, keepdims=True)
                dss.append((p * (dps[hh] - drow)).astype(BF16))
                ps.append(p.astype(BF16))
                dsink[hh] = dsink[hh] + share * drow
            for hh in heads:
                dq_ref[rows, _head(hh)] = (jnp.dot(dss[hh], kgs[hh // GROUP], preferred_element_type=F32)
                                           * SCALE).astype(BF16)
            groups = [slice(GROUP * g, GROUP * (g + 1)) for g in range(N_KV_HEADS)]
            dkg = [lax.dot_general(jnp.concatenate(dss[gr], axis=0), jnp.concatenate(qs[gr], axis=0),
                                   (((0,), (0,)), ((), ())), preferred_element_type=F32) for gr in groups]
            dvg = [lax.dot_general(jnp.concatenate(ps[gr], axis=0), jnp.concatenate(dosb[gr], axis=0),
                                   (((0,), (0,)), ((), ())), preferred_element_type=F32) for gr in groups]
            dkb, dvb = jnp.concatenate(dkg, axis=1), jnp.concatenate(dvg, axis=1)
            if b == 0:
                @pl.when(step > 0)
                def _():
                    before = pl.ds(pl.multiple_of((i - 1) * BLOCK, BLOCK), BLOCK)
                    dk_ref[before, :] += dkb[:BLOCK]
                    dv_ref[before, :] += dvb[:BLOCK]
            else:
                at = pl.ds(pl.multiple_of((i - 1) * BLOCK, BLOCK), BLOCK)
                dk_ref[at, :] = ahead[0] + dkb[:BLOCK]
                dv_ref[at, :] = ahead[1] + dvb[:BLOCK]
            ahead = (dkb[BLOCK:], dvb[BLOCK:])
        last = pl.ds(pl.multiple_of(((step + 1) * per_step - 1) * BLOCK, BLOCK), BLOCK)
        dk_ref[last, :] = ahead[0]
        dv_ref[last, :] = ahead[1]
        for hh in range(N_Q_HEADS):
            ds_ref[hh:hh + 1, :] -= jnp.sum(dsink[hh])

    whole = pl.BlockSpec((t, KV_W), lambda i: (0, 0))
    blk = pl.BlockSpec((tm, ATTN_W), lambda i: (i, 0))
    body, dep_specs = _behind(body, deps)
    return pl.pallas_call(
        body, name="attn_bwd", grid=(t // tm,),
        in_specs=dep_specs + [pl.BlockSpec(memory_space=pltpu.SMEM), blk, whole, whole, blk, blk,
                              pl.BlockSpec(bias.shape, lambda i: (0, 0, 0))],
        out_specs=[blk, whole, whole, pl.BlockSpec((N_Q_HEADS, 128), lambda i: (0, 0))],
        out_shape=[jax.ShapeDtypeStruct((t, ATTN_W), BF16), jax.ShapeDtypeStruct((t, KV_W), F32),
                   jax.ShapeDtypeStruct((t, KV_W), F32), jax.ShapeDtypeStruct((N_Q_HEADS, 128), F32)],
        compiler_params=_params("arbitrary"),
    )(*deps, sinks, q, k, v, o, do, bias)


def _in_proj_bwd(dq, dk, dv, dbch, w, dh1, h, g, tabs, tm, split_lead=False):
    t = h.shape[0]
    nt = t // tm

    def body(dq_ref, dk_ref, dv_ref, dbch_ref, w_ref, dh1_ref, h_ref, g_ref, c_ref, s_ref, *rest):
        dp_ref, dg_ref = rest[2:4] if split_lead else rest[1:3]
        i = pl.program_id(0)

        @pl.when(i == 0)
        def _():
            dg_ref[...] = jnp.zeros_like(dg_ref)

        cos, sa, sb = _rope_factors(c_ref[...], s_ref[...])
        rep = ATTN_W // (2 * HEAD_DIM)
        dqr = _rope_bwd(dq_ref[...].astype(F32), jnp.tile(cos, (1, rep)), jnp.tile(sa, (1, rep)),
                        jnp.tile(sb, (1, rep)))
        dkr = _rope_bwd(dk_ref[...], cos, sa, sb)
        dp = jnp.concatenate([dqr.astype(BF16), dkr.astype(BF16), dv_ref[...].astype(BF16), dbch_ref[...]], axis=1)
        dp_ref[...] = dp
        da = jnp.dot(dp, w_ref[...], preferred_element_type=F32)
        dx, dg = _rms_bwd(da, h_ref[...], g_ref[...])
        dg_ref[...] += dg
        dh = dh1_ref[...] + dx
        if not split_lead:
            rest[0][...] = dh
            return
        lead_ref, seq_ref, stage, sems = rest[0], rest[1], rest[4], rest[5]

        def copy(j, slot, first):
            if first:
                return pltpu.make_async_copy(stage.at[slot, pl.ds(BLOCK, tm - BLOCK)],
                                             seq_ref.at[pl.ds(0, tm - BLOCK)], sems.at[slot])
            return pltpu.make_async_copy(stage.at[slot], seq_ref.at[pl.ds(pl.multiple_of(j * tm - BLOCK, BLOCK), tm)],
                                         sems.at[slot])

        slot = i % 2
        pl.when(i == 2)(lambda: copy(0, slot, True).wait())
        pl.when(i > 2)(lambda: copy(i - 2, slot, False).wait())
        stage[slot] = dh

        @pl.when(i == 0)
        def _():
            lead_ref[...] = dh[:BLOCK]
            copy(0, slot, True).start()

        pl.when(i > 0)(lambda: copy(i, slot, False).start())

        @pl.when(i == nt - 1)
        def _():
            for j in range(max(nt - 2, 0), nt):
                copy(j, j % 2, j == 0).wait()

    row = lambda n: pl.BlockSpec((tm, n), lambda i: (i, 0))
    full = lambda a: pl.BlockSpec(a.shape, lambda i: (0, 0))
    dh_specs, dh_shapes, scratch = [row(D_MODEL)], [jax.ShapeDtypeStruct((t, D_MODEL), F32)], []
    if split_lead:
        dh_specs = [pl.BlockSpec((BLOCK, D_MODEL), lambda i: (0, 0)), pl.BlockSpec(memory_space=pl.ANY)]
        dh_shapes = [jax.ShapeDtypeStruct((BLOCK, D_MODEL), F32), jax.ShapeDtypeStruct((t - BLOCK, D_MODEL), F32)]
        scratch = [pltpu.VMEM((2, tm, D_MODEL), F32), pltpu.SemaphoreType.DMA((2,))]
    outs = pl.pallas_call(
        body, name="in_proj_bwd", grid=(nt,),
        in_specs=[row(ATTN_W), row(KV_W), row(KV_W), row(3 * CONV_W), full(w), row(D_MODEL), row(D_MODEL), full(g),
                  row(2 * HEAD_DIM), row(2 * HEAD_DIM)],
        out_specs=dh_specs + [row(IN_W), pl.BlockSpec((1, D_MODEL), lambda i: (0, 0))],
        out_shape=dh_shapes + [jax.ShapeDtypeStruct((t, IN_W), BF16), jax.ShapeDtypeStruct((1, D_MODEL), F32)],
        scratch_shapes=scratch,
        compiler_params=_params("arbitrary"),
    )(dq, dk, dv, dbch, w, dh1, h, g, *tabs)
    return (tuple(outs[:2]) if split_lead else outs[0],) + tuple(outs[-2:])


class _Tiles:
    def __init__(self, t):
        self.tm = _row_tile(t, 640)
        self.ts = self.tm
        self.tabs = _rope_tables(t)
        self.bias = _attn_bias()


def _mixer_fwd(h, p, tl, lead=None):
    if lead is None:
        a, q, k, v, b, c, hc = _in_proj(h, p["mix_pre_g"], p["w_in"], tl.tabs, tl.ts)
    else:
        h, a, q, k, v, b, c, hc = _in_proj(h, p["mix_pre_g"], p["w_in"], tl.tabs, tl.ts, lead)
    o = _attn_fwd(q, k, v, tl.bias, p["sinks"], tl.tm)
    return (h, a, q, k, v, b, c, hc, o)


def _out_fwd(mixed, p, tl, deps=()):
    h, a, q, k, v, b, c, hc, o = mixed
    h1, y, z = _mix_out(h, o, b, c, hc, p["conv_w"], p["attn_out_g"], p["conv_out_g"], p["w_out"], p["mix_post_g"],
                        tl.ts, deps)
    return h1, mixed + (h1, y, z)


def _mlp_fwd(h1, saved, p, tl, target=None):
    h2, a2, slope, f = _mlp(h1, p["mlp_pre_g"], p["w_up"], p["w_down"], p["mlp_post_g"], tl.tm, target)
    return h2, saved + (a2, slope, f)


def _mlp_part_bwd(dh, saved, p, tl, deps=()):
    h1, a2, slope, f = saved[9], saved[12], saved[13], saved[14]
    if isinstance(f, tuple):
        df, dg2 = f
        dup = _mlp_bwd_hidden(None, None, None, slope, p["w_down"], tl.tm, deps, df)
    else:
        df, dup, dg2 = _mlp_bwd_hidden(dh, f, p["mlp_post_g"], slope, p["w_down"], tl.tm, deps)
    dh1, dg1 = _mlp_bwd_input(dup, p["w_up"], h1, p["mlp_pre_g"], dh, tl.tm)
    g = {"w_down": [d.reshape(N_CHIPS, FF_CHUNK, D_MODEL)
                    for d in _weight_grad(slope, df, "grad_w_down", x_is_slope=True)],
         "w_up": [d.reshape(N_CHIPS, D_MODEL, FF_CHUNK) for d in _weight_grad(a2, dup, "grad_w_up")],
         "mlp_post_g": dg2, "mlp_pre_g": dg1}
    return dh1, g


def _mix_out_part_bwd(dh1, saved, p, tl, deps=()):
    b, c, hc, o, y, z = saved[5], saved[6], saved[7], saved[8], saved[10], saved[11]
    dz, do, dbch, dgp, dga, dgc, dcw = _mix_out_bwd(dh1, z, p["mix_post_g"], p["w_out"], o, b, c, hc, p["conv_w"],
                                                    p["attn_out_g"], p["conv_out_g"], tl.ts, deps)
    g = {"w_out": [d.reshape(N_CHIPS, D_MODEL // N_CHIPS, D_MODEL) for d in _weight_grad(y, dz, "grad_w_out")],
         "mix_post_g": dgp, "attn_out_g": dga, "conv_out_g": dgc, "conv_w": dcw}
    return (dh1, do, dbch), g


def _attn_in_part_bwd(carry, saved, p, tl, deps=(), split_lead=False):
    dh1, do, dbch = carry
    h_in, q, k, v, o = saved[0], saved[2], saved[3], saved[4], saved[8]
    dq, dk, dv, dsink = _attn_bwd(q, k, v, o, do, tl.bias, p["sinks"], tl.tm, deps)
    dh, dproj, dgi = _in_proj_bwd(dq, dk, dv, dbch, p["w_in"], dh1, h_in, p["mix_pre_g"], tl.tabs, tl.ts, split_lead)
    return dh, dproj, {"mix_pre_g": dgi, "sinks": dsink[:, 0]}


def _in_grad(dproj, saved, deps=()):
    return [d.reshape(N_CHIPS, IN_W // N_CHIPS, D_MODEL) for d in _weight_grad(dproj, saved[1], "grad_w_in", deps=deps)]


def _place():
    return lax.axis_index("x"), lax.axis_index("y"), lax.axis_index("c")


def _other_chips(x, y):
    return [(1 - x, y), (x, 1 - y), (1 - x, 1 - y)]


_HBM = pl.BlockSpec(memory_space=pltpu.HBM)
_SEM = pl.BlockSpec(memory_space=pltpu.SEMAPHORE)
_EFFECT = pltpu.SideEffectType.DATAFLOW_SIDE_EFFECTING


class _Exchange:
    def __init__(self, name, bufs, plan, n, after=()):
        self.name, self.plan, nb = name, plan, len(bufs)
        n_in = nb + len(after)

        def body(*refs):
            send, recv, token = refs[n_in], refs[n_in + 1], refs[-1]
            for k, (src, dst, target, _) in enumerate(plan(refs[:nb])):
                pltpu.make_async_remote_copy(src_ref=src, dst_ref=dst, send_sem=send.at[k], recv_sem=recv.at[k],
                                             device_id=target, device_id_type=MESH).start()
            token[...] = jnp.zeros_like(token)

        outs = pl.pallas_call(
            body, name=name + "_start",
            out_shape=(pltpu.SemaphoreType.DMA((n,)), pltpu.SemaphoreType.DMA((n,)),
                       *[pltpu.HBM(b.shape, b.dtype) for b in bufs], jax.ShapeDtypeStruct((8, 128), F32)),
            in_specs=[_HBM] * nb + [pl.BlockSpec(memory_space=pl.ANY)] * len(after),
            out_specs=(_SEM, _SEM, *[_HBM] * nb, pl.BlockSpec(memory_space=pltpu.VMEM)),
            input_output_aliases={i: 2 + i for i in range(nb)},
            compiler_params=pltpu.CompilerParams(has_side_effects=_EFFECT),
        )(*[pltpu.with_memory_space_constraint(b, pltpu.HBM) for b in bufs], *after)
        self.send, self.recv, self.bufs, self.token = outs[0], outs[1], list(outs[2:2 + nb]), outs[-1]

    def wait(self, *after):
        plan, nb = self.plan, len(self.bufs)

        def body(*refs):
            send, recv = refs[nb], refs[nb + 1]
            for k, (src, _, target, land) in enumerate(plan(refs[:nb])):
                cp = pltpu.make_async_remote_copy(src_ref=src, dst_ref=land, send_sem=send.at[k], recv_sem=recv.at[k],
                                                  device_id=target, device_id_type=MESH)
                cp.wait_send()
                cp.wait_recv()

        outs = pl.pallas_call(
            body, name=self.name + "_wait", out_shape=[pltpu.HBM(b.shape, b.dtype) for b in self.bufs],
            in_specs=[_HBM] * nb + [_SEM, _SEM] + [pl.BlockSpec(memory_space=pl.ANY)] * len(after),
            out_specs=[_HBM] * nb, input_output_aliases={i: i for i in range(nb)},
            compiler_params=pltpu.CompilerParams(has_side_effects=_EFFECT),
        )(*self.bufs, self.send, self.recv, *after)
        return list(outs)


def _gather_plan(n):
    def plan(refs):
        x, y, c = _place()
        me = 2 * x + y
        return [(refs[a].at[me], refs[a].at[me], (px, py, c), refs[a].at[2 * px + py])
                for a in range(n) for px, py in _other_chips(x, y)]

    return plan


def _gather_half_plan(n, half_rows):
    def plan(refs):
        x, y, c = _place()
        me = 2 * x + y
        out = []
        for a in range(n):
            rows = pl.ds(c * half_rows[a], half_rows[a])
            out += [(refs[a].at[me, rows], refs[a].at[me, rows], (px, py, c), refs[a].at[2 * px + py, rows])
                    for px, py in _other_chips(x, y)]
        return out

    return plan


def _hand_over_plan(n, half_rows):
    def plan(refs):
        x, y, c = _place()
        out = []
        for a in range(n):
            mine, theirs = pl.ds(c * half_rows[a], half_rows[a]), pl.ds((1 - c) * half_rows[a], half_rows[a])
            for px, py in _other_chips(x, y):
                held = refs[a].at[2 * px + py, mine]
                out.append((held, held, (x, y, 1 - c), refs[a].at[2 * px + py, theirs]))
        return out

    return plan


def _peers():
    x, y, c = _place()
    return [(k - 1, (x ^ (k >> 2), y ^ ((k >> 1) & 1), c ^ (k & 1))) for k in range(1, N_DEV)]


def _scatter_plan(n, half_rows):
    def plan(refs):
        out = []
        for a in range(n):
            hr = half_rows[a]
            for k, (px, py, pc) in _peers():
                out.append((refs[a].at[2 * px + py, pl.ds(pc * hr, hr)], refs[n + a].at[k], (px, py, pc),
                            refs[n + a].at[k]))
        return out

    return plan


def _join_plan(n):
    def plan(refs):
        x, y, c = _place()
        return [(refs[a].at[c], refs[a].at[c], (x, y, 1 - c), refs[a].at[1 - c]) for a in range(n)]

    return plan


def _sum_parts(gs, qs):
    n = len(gs)
    half_rows = [g.shape[1] // 2 for g in gs]
    tr = [_block_rows(hr) for hr in half_rows]
    per = [hr // t for hr, t in zip(half_rows, tr)]
    x, y, c = _place()
    where = jnp.stack([2 * x + y, c]).astype(jnp.int32)

    def body(where_ref, *refs):
        i = pl.program_id(0)
        for a in range(n):
            g_ref, q_ref, o_ref = refs[a], refs[n + a], refs[2 * n + a]

            @pl.when(i < per[a])
            def _():
                total = g_ref[...]
                for k in range(N_DEV - 1):
                    total = total + q_ref[k].astype(F32)
                o_ref[...] = total

    def at(a, i):
        return jnp.minimum(i, per[a] - 1)

    specs_g = [pl.BlockSpec((None, tr[a], gs[a].shape[2]),
                            lambda i, where_ref, a=a: (where_ref[0], where_ref[1] * per[a] + at(a, i), 0)) for a in range(n)]
    specs_q = [pl.BlockSpec((N_DEV - 1, tr[a], gs[a].shape[2]), lambda i, where_ref, a=a: (0, at(a, i), 0))
               for a in range(n)]
    specs_o = [pl.BlockSpec((None, tr[a], gs[a].shape[2]), lambda i, where_ref, a=a: (where_ref[1], at(a, i), 0))
               for a in range(n)]
    return pl.pallas_call(
        body, name="sum_parts",
        grid_spec=pltpu.PrefetchScalarGridSpec(num_scalar_prefetch=1, grid=(max(per),), in_specs=specs_g + specs_q,
                                               out_specs=specs_o),
        out_shape=[jax.ShapeDtypeStruct((2, hr, g.shape[2]), F32) for g, hr in zip(gs, half_rows)],
        compiler_params=_params("arbitrary"),
    )(where, *gs, *qs)


def _all_plan(refs):
    x, y, c = _place()
    mine = refs[0].at[4 * x + 2 * y + c]
    return [(mine, mine, (px, py, pc), refs[0].at[4 * px + 2 * py + pc]) for _, (px, py, pc) in _peers()]


def _sum_devices(parts):
    def body(p_ref, o_ref):
        total = p_ref[0]
        for d in range(1, N_DEV):
            total = total + p_ref[d]
        o_ref[...] = total

    vm = pl.BlockSpec(memory_space=pltpu.VMEM)
    return pl.pallas_call(body, name="sum_devices", in_specs=[vm], out_specs=vm,
                          out_shape=jax.ShapeDtypeStruct(parts.shape[1:], F32))(parts)


def _adamw_math(w, g, m, v):
    m = ADAM_B1 * m + (1.0 - ADAM_B1) * g
    v = ADAM_B2 * v + (1.0 - ADAM_B2) * jnp.square(g)
    m_hat = m / (1.0 - ADAM_B1 ** ADAM_STEP)
    v_hat = v / (1.0 - ADAM_B2 ** ADAM_STEP)
    delta = -ADAM_LR * (m_hat / (jnp.sqrt(v_hat) + ADAM_EPS) + ADAM_WD * w)
    return delta, m, v


def _adamw_large(layer, ws, halves, ms, vs, others):
    n = len(ws)
    tr = [_block_rows(w.shape[1] // 2) for w in ws]
    per = [w.shape[1] // 2 // t for w, t in zip(ws, tr)]
    kept = [] if others is None else [a for four in others for a in four]

    def body(*refs):
        i = pl.program_id(0)
        outs = refs[4 * n + len(kept):]
        for a in range(n):
            w_ref, g_ref, m_ref, v_ref = refs[a], refs[n + a], refs[2 * n + a], refs[3 * n + a]
            g_out, d_ref, nm_ref, nv_ref = outs[4 * a:4 * a + 4]

            @pl.when(i < 2 * per[a])
            def _():
                g = g_ref[...]
                g_out[...] = g
                d_ref[...], nm_ref[...], nv_ref[...] = _adamw_math(w_ref[...], g, m_ref[...], v_ref[...])

    def at(a, i):
        return jnp.minimum(i, 2 * per[a] - 1)

    blk = [pl.BlockSpec((None, tr[a], ws[a].shape[2]), lambda i, a=a: (layer, at(a, i), 0)) for a in range(n)]
    half = [pl.BlockSpec((None, tr[a], ws[a].shape[2]), lambda i, a=a: (at(a, i) // per[a], at(a, i) % per[a], 0))
            for a in range(n)]
    outs = pl.pallas_call(
        body, name="adamw_large", grid=(2 * max(per),),
        in_specs=blk + half + blk + blk + [pl.BlockSpec(memory_space=pl.ANY)] * len(kept),
        out_specs=[blk[a] for a in range(n) for _ in range(4)],
        out_shape=[jax.ShapeDtypeStruct(w.shape, F32) for w in ws for _ in range(4)],
        input_output_aliases={4 * n + k: k for k in range(len(kept))},
        compiler_params=_params("arbitrary"),
    )(*ws, *halves, *ms, *vs, *kept)
    return [outs[4 * a:4 * a + 4] for a in range(n)]


def _adamw_small(ws, gs, ms, vs):
    n = len(ws)

    def body(*refs):
        w_r, g_r, m_r, v_r = refs[:n], refs[n:2 * n], refs[2 * n:3 * n], refs[3 * n:4 * n]
        d_r, nm_r, nv_r = refs[4 * n:5 * n], refs[5 * n:6 * n], refs[6 * n:]
        for a in range(n):
            d_r[a][...], nm_r[a][...], nv_r[a][...] = _adamw_math(w_r[a][...], g_r[a][...], m_r[a][...], v_r[a][...])

    vm = pl.BlockSpec(memory_space=pltpu.VMEM)
    outs = pl.pallas_call(
        body, name="adamw_small", in_specs=[vm] * (4 * n), out_specs=[vm] * (3 * n),
        out_shape=[jax.ShapeDtypeStruct(w.shape, F32) for w in ws] * 3,
    )(*ws, *gs, *ms, *vs)
    return outs[:n], outs[n:2 * n], outs[2 * n:]


_LARGE = ("w_in", "w_out", "w_up", "w_down")
_SMALL = ("meta_tokens", "mix_pre_g", "conv_w", "sinks", "attn_out_g", "conv_out_g", "mix_post_g", "mlp_pre_g",
          "mlp_post_g")
_ORDER = ("meta_tokens", "mix_pre_g", "w_in", "conv_w", "sinks", "attn_out_g", "conv_out_g", "w_out", "mix_post_g",
          "mlp_pre_g", "w_up", "w_down", "mlp_post_g")


class _Reduce:
    def __init__(self, name, grads, after=()):
        self.name, self.n = name, len(grads)
        self.own = [g for g, _ in grads]
        half_rows = [g.shape[1] // 2 for g in self.own]
        zones = [lax.empty((N_DEV - 1, hr, g.shape[2]), BF16) for g, hr in zip(self.own, half_rows)]
        self.exchange = _Exchange(name + "_scatter", [b for _, b in grads] + zones, _scatter_plan(self.n, half_rows),
                                  (N_DEV - 1) * self.n, after)

    @property
    def token(self):
        return self.exchange.token

    def join(self, *after):
        bufs = self.exchange.wait(*after)
        halves = list(_sum_parts(self.own, bufs[self.n:]))
        self.exchange = _Exchange(self.name + "_join", halves, _join_plan(self.n), self.n)

    def done(self, *after):
        return self.exchange.wait(*after)


def _pad_cols(a, n=D_MODEL):
    return jnp.pad(a, ((0, 0), (0, n - a.shape[1])))


def kernel(x, meta_tokens, mix_pre_g, w_in, conv_w, sinks, attn_out_g, conv_out_g, w_out, mix_post_g, mlp_pre_g, w_up, w_down, mlp_post_g, loss_target, m_meta_tokens, m_mix_pre_g, m_w_in, m_conv_w, m_sinks, m_attn_out_g, m_conv_out_g, m_w_out, m_mix_post_g, m_mlp_pre_g, m_w_up, m_w_down, m_mlp_post_g, v_meta_tokens, v_mix_pre_g, v_w_in, v_conv_w, v_sinks, v_attn_out_g, v_conv_out_g, v_w_out, v_mix_post_g, v_mlp_pre_g, v_w_up, v_w_down, v_mlp_post_g):
    w = dict(meta_tokens=meta_tokens, mix_pre_g=mix_pre_g, w_in=w_in, conv_w=conv_w, sinks=sinks,
             attn_out_g=attn_out_g, conv_out_g=conv_out_g, w_out=w_out, mix_post_g=mix_post_g, mlp_pre_g=mlp_pre_g,
             w_up=w_up, w_down=w_down, mlp_post_g=mlp_post_g)
    m = dict(meta_tokens=m_meta_tokens, mix_pre_g=m_mix_pre_g, w_in=m_w_in, conv_w=m_conv_w, sinks=m_sinks,
             attn_out_g=m_attn_out_g, conv_out_g=m_conv_out_g, w_out=m_w_out, mix_post_g=m_mix_post_g,
             mlp_pre_g=m_mlp_pre_g, w_up=m_w_up, w_down=m_w_down, mlp_post_g=m_mlp_post_g)
    v = dict(meta_tokens=v_meta_tokens, mix_pre_g=v_mix_pre_g, w_in=v_w_in, conv_w=v_conv_w, sinks=v_sinks,
             attn_out_g=v_attn_out_g, conv_out_g=v_conv_out_g, w_out=v_w_out, mix_post_g=v_mix_post_g,
             mlp_pre_g=v_mlp_pre_g, w_up=v_w_up, w_down=v_w_down, mlp_post_g=v_mlp_post_g)
    chip = 2 * lax.axis_index("x") + lax.axis_index("y")
    tl = _Tiles(x.shape[1] + BLOCK)

    def zone(quarter):
        return lax.dynamic_update_slice(lax.empty((N_CHIPS,) + quarter.shape, quarter.dtype), quarter[None],
                                        (chip,) + (0,) * quarter.ndim)

    w, m, v = ({**d, "w_in": jnp.swapaxes(d["w_in"], 1, 2)} for d in (w, m, v))
    zones = {n: [zone(w[n][l].astype(BF16)) for l in range(DEPTH)] for n in _LARGE}
    first = _Exchange("gather_first", [zones["w_in"][0], zone(w["conv_w"]), zone(w["meta_tokens"])], _gather_plan(3), 9)
    out0 = _Exchange("gather_out", [zones["w_out"][0]], _gather_plan(1), 3, [first.token])
    mlp_halves = [D_MODEL // 2, FF_CHUNK // 2]
    rest = _Exchange("gather_rest", [zones[n][0] for n in ("w_up", "w_down")], _gather_half_plan(2, mlp_halves), 6,
                     [out0.token])

    def whole_in(quarters):
        return quarters.reshape(IN_W, D_MODEL)

    q_in, q_conv, q_meta = first.wait(rest.token, *tl.tabs, tl.bias)
    conv_whole = jnp.transpose(q_conv, (1, 2, 0, 3)).reshape(DEPTH, CONV_K, CONV_W)
    meta = jnp.transpose(q_meta, (1, 0, 2)).reshape(N_META, D_MODEL)
    p = [{"conv_w": conv_whole[l], "sinks": w["sinks"][l]} for l in range(DEPTH)]
    for l in range(DEPTH):
        for n in ("mix_pre_g", "attn_out_g", "conv_out_g", "mix_post_g", "mlp_pre_g", "mlp_post_g"):
            p[l][n] = w[n][l][None, :]

    lead = jnp.concatenate([jnp.zeros((LEAD_PAD, D_MODEL), F32), meta], axis=0)
    p[0]["w_in"] = whole_in(q_in)
    mixed = _mixer_fwd(x[0], p[0], tl, lead)
    second = _Exchange("gather_second", [zones["w_in"][1], zones["w_out"][1]], _gather_plan(2), 6, [mixed[-1]])
    second_mlp = _Exchange("gather_second_mlp", [zones["w_up"][1], zones["w_down"][1]], _gather_plan(2), 6,
                           [second.token])
    hand_over = _Exchange("hand_over_rest", rest.wait(second_mlp.token), _hand_over_plan(2, mlp_halves), 6)
    p[0]["w_out"], = out0.wait(hand_over.token)
    h1, saved0 = _out_fwd(mixed, p[0], tl)
    p[0]["w_up"], p[0]["w_down"] = hand_over.wait(h1)
    h, saved0 = _mlp_fwd(h1, saved0, p[0], tl)
    q_in, p[1]["w_out"] = second.wait(h)
    p[1]["w_in"] = whole_in(q_in)
    h1, saved1 = _out_fwd(_mixer_fwd(h, p[1], tl), p[1], tl)
    p[1]["w_up"], p[1]["w_down"] = second_mlp.wait(h1)
    (loss_tile, dh), saved1 = _mlp_fwd(h1, saved1, p[1], tl, loss_target[0])

    def adamw(layer, halves, other):
        names = list(halves)
        done = _adamw_large(layer, [w[n] for n in names], [halves[n] for n in names], [m[n] for n in names],
                            [v[n] for n in names], None if other is None else [other[n] for n in names])
        return dict(zip(names, done))

    dh1, g1 = _mlp_part_bwd(dh, saved1, p[1], tl)
    carry, gm = _mix_out_part_bwd(dh1, saved1, p[1], tl)
    dh, dproj, gi = _attn_in_part_bwd(carry, saved1, p[1], tl)
    g1.update(gm, w_in=_in_grad(dproj, saved1), **gi)
    red1 = _Reduce("reduce1", [g1[n] for n in _LARGE])
    dh1, g0 = _mlp_part_bwd(dh, saved0, p[0], tl, [red1.token])
    red1.join(g0["w_down"][0])
    carry, gm = _mix_out_part_bwd(dh1, saved0, p[0], tl, [red1.token])
    first0 = ("w_up", "w_down", "w_out")
    g0.update(gm)
    red0a = _Reduce("reduce0a", [g0[n] for n in first0])
    (dlead, dseq), dproj, gi = _attn_in_part_bwd(carry, saved0, p[0], tl, [red0a.token], split_lead=True)
    g0.update(gi)
    grad_x = dseq[None]
    grads = {n: [g0[n], g1[n]] for n in g0 if n not in _LARGE}

    rows = [dlead[LEAD_PAD:]]
    for n in ("mix_pre_g", "mix_post_g", "mlp_pre_g", "mlp_post_g"):
        rows += grads[n]
    rows += [jnp.concatenate([grads["attn_out_g"][l], grads["conv_out_g"][l]], axis=1) for l in range(DEPTH)]
    rows.append(jnp.concatenate(grads["conv_w"], axis=1))
    rows.append(_pad_cols(jnp.concatenate(grads["sinks"])[None, :]))
    rows.append(_pad_cols(loss_tile[:1]))
    packed = jnp.concatenate(rows, axis=0)
    packed = jnp.pad(packed, ((0, SMALL_ROWS - packed.shape[0]), (0, 0)))
    device = 2 * chip + lax.axis_index("c")
    small_parts = _Exchange("gather_small", [lax.dynamic_update_slice(lax.empty((N_DEV,) + packed.shape, F32),
                                                                      packed[None], (device, 0, 0))], _all_plan, N_DEV - 1)
    g0["w_in"] = _in_grad(dproj, saved0, [small_parts.token])
    red0b = _Reduce("reduce0b", [g0["w_in"]])
    done1 = adamw(1, dict(zip(_LARGE, red1.done(red0b.token))), None)
    total = _sum_devices(small_parts.wait(*[done1[n][0] for n in _LARGE])[0])
    r0 = N_META
    small = {
        "meta_tokens": lax.dynamic_slice(total[:N_META], (0, chip * (D_MODEL // N_CHIPS)), (N_META, D_MODEL // N_CHIPS)),
        "mix_pre_g": total[r0:r0 + 2], "mix_post_g": total[r0 + 2:r0 + 4], "mlp_pre_g": total[r0 + 4:r0 + 6],
        "mlp_post_g": total[r0 + 6:r0 + 8],
        "attn_out_g": total[r0 + 8:r0 + 10, :ATTN_W], "conv_out_g": total[r0 + 8:r0 + 10, ATTN_W:],
        "conv_w": lax.dynamic_slice(total[r0 + 10:r0 + 13].reshape(CONV_K, DEPTH, CONV_W).transpose(1, 0, 2),
                                    (0, 0, chip * (CONV_W // N_CHIPS)), (DEPTH, CONV_K, CONV_W // N_CHIPS)),
        "sinks": total[r0 + 13, :DEPTH * N_Q_HEADS].reshape(DEPTH, N_Q_HEADS),
    }
    loss = total[r0 + 14, 0]

    ds, nms, nvs = _adamw_small([w[n] for n in _SMALL], [small[n] for n in _SMALL], [m[n] for n in _SMALL],
                                [v[n] for n in _SMALL])
    red0a.join(ds[0], grad_x)
    red0b.join(red0a.token)
    done0 = adamw(0, dict(zip(first0, red0a.done(red0b.token))), done1)
    done0.update(adamw(0, {"w_in": red0b.done(done0["w_down"][0])[0]}, done1))
    grad, delta, new_m, new_v = {}, {}, {}, {}
    for n in _LARGE:
        grad[n], delta[n], new_m[n], new_v[n] = done0[n]
    for d in (grad, delta, new_m, new_v):
        d["w_in"] = jnp.swapaxes(d["w_in"], 1, 2)
    for i, n in enumerate(_SMALL):
        grad[n], delta[n], new_m[n], new_v[n] = small[n], ds[i], nms[i], nvs[i]
    return (loss, grad_x, *[grad[n] for n in _ORDER], *[delta[n] for n in _ORDER], *[new_m[n] for n in _ORDER],
            *[new_v[n] for n in _ORDER])
```

```python
import jax
import jax.numpy as jnp
from jax import lax
from jax.experimental import pallas as pl
from jax.experimental.pallas import tpu as pltpu

F32 = jnp.float32
BF16 = jnp.bfloat16

D_MODEL = 1024
DEPTH = 2
N_META = 16
ATTN_W = 512
CONV_W = 512
HEAD_DIM = 64
N_Q_HEADS = 8
N_KV_HEADS = 2
GROUP = N_Q_HEADS // N_KV_HEADS
KV_W = N_KV_HEADS * HEAD_DIM
CONV_K = 3
BLOCK = 128
LEAD_PAD = BLOCK - N_META
ROPE_THETA = 500000.0
ROT_DIM = HEAD_DIM // 4
ROT_HALF = ROT_DIM // 2
D_FF = 4 * D_MODEL
IN_W = ATTN_W + 2 * KV_W + 3 * CONV_W
QKV_W = ATTN_W + 2 * KV_W
EPS = 1e-6
SCALE = HEAD_DIM ** -0.5
FF_CHUNK = 1024
N_CHIPS = 4
N_DEV = 8

ADAM_LR = 0.001
ADAM_B1 = 0.9
ADAM_B2 = 0.999
ADAM_EPS = 1e-08
ADAM_WD = 0.01
ADAM_STEP = 10

V7X_VMEM_LIMIT = 60 * 1024 * 1024
SMALL_ROWS = 32

MESH = pl.DeviceIdType.MESH


def _params(*sem):
    return pltpu.CompilerParams(dimension_semantics=sem, vmem_limit_bytes=V7X_VMEM_LIMIT)


def _block_rows(n):
    return max(r for r in range(16, min(n, 128) + 1, 16) if n % r == 0)


def _row_tile(t, most):
    nb = t // BLOCK
    for b in range(most // BLOCK, 0, -1):
        if nb % b == 0:
            return b * BLOCK
    return BLOCK


def _behind(body, deps):
    n = len(deps)

    def wrapped(*refs):
        body(*refs[n:])

    return wrapped, [pl.BlockSpec(memory_space=pl.ANY)] * n


def _rms(x, g):
    r = lax.rsqrt(jnp.mean(x * x, axis=-1, keepdims=True) + EPS)
    return x * r * g


def _rms_bwd(dy, x, g):
    r = lax.rsqrt(jnp.mean(x * x, axis=-1, keepdims=True) + EPS)
    xh = x * r
    dg = jnp.sum(dy * xh, axis=0, keepdims=True)
    dxh = dy * g
    dx = r * (dxh - xh * jnp.mean(dxh * xh, axis=-1, keepdims=True))
    return dx, dg


def _rope(x, cos, sa, sb):
    n = x.shape[-1]
    return x * cos + pltpu.roll(x, n - ROT_HALF, 1) * sa + pltpu.roll(x, ROT_HALF, 1) * sb


def _rope_bwd(dy, cos, sa, sb):
    n = dy.shape[-1]
    return dy * cos + pltpu.roll(dy * sa, ROT_HALF, 1) + pltpu.roll(dy * sb, n - ROT_HALF, 1)


def _rope_tables(t):
    pos = lax.broadcasted_iota(jnp.int32, (t, ROT_HALF), 0).astype(F32) - LEAD_PAD
    pair = lax.broadcasted_iota(jnp.int32, (t, ROT_HALF), 1).astype(F32)
    inv_freq = jnp.power(jnp.float32(ROPE_THETA), -(2.0 * pair) / ROT_DIM)
    ang = pos * inv_freq
    cos, sin = lax.optimization_barrier((jnp.cos(ang), jnp.sin(ang)))
    spread = (1, 2 * HEAD_DIM // ROT_HALF)
    cos, sin = jnp.tile(cos, spread), jnp.tile(sin, spread)
    dim = lax.broadcasted_iota(jnp.int32, (t, 2 * HEAD_DIM), 1) % HEAD_DIM
    return jnp.where(dim < ROT_DIM, cos, 1.0), jnp.where(dim < ROT_DIM, sin, 0.0)


def _rope_factors(cos, sin):
    dim = lax.broadcasted_iota(jnp.int32, sin.shape, 1) % HEAD_DIM
    return cos, jnp.where(dim < ROT_HALF, -sin, 0.0), jnp.where(dim >= ROT_HALF, sin, 0.0)


def _in_proj(h, g, w, tabs, tm, lead=None):
    t = h.shape[0] + (0 if lead is None else BLOCK)
    per_step = 0 if lead is None else tm // BLOCK

    def body(*refs):
        if lead is None:
            x = refs[0][...]
            refs = refs[1:]
        else:
            blocks = [r[...] for r in refs[1:1 + per_step]]
            blocks[0] = jnp.where(pl.program_id(0) == 0, refs[0][...], blocks[0])
            x = jnp.concatenate(blocks, axis=0)
            first_out = 1 + per_step + 4
            refs[first_out][...] = x
            refs = refs[1 + per_step:first_out] + refs[first_out + 1:]
        g_ref, w_ref, c_ref, s_ref, a_ref, q_ref, k_ref, v_ref, b_ref, cg_ref, hc_ref = refs
        a = _rms(x, g_ref[...]).astype(BF16)
        a_ref[...] = a
        p = lax.dot_general(a, w_ref[...], (((1,), (1,)), ((), ())), preferred_element_type=F32)
        cos, sa, sb = _rope_factors(c_ref[...], s_ref[...])
        rep = ATTN_W // (2 * HEAD_DIM)
        q = _rope(p[:, :ATTN_W], jnp.tile(cos, (1, rep)), jnp.tile(sa, (1, rep)), jnp.tile(sb, (1, rep)))
        q_ref[...] = (q * SCALE).astype(BF16)
        k_ref[...] = _rope(p[:, ATTN_W:ATTN_W + KV_W], cos, sa, sb).astype(BF16)
        v_ref[...] = p[:, ATTN_W + KV_W:QKV_W].astype(BF16)
        b_ref[...] = p[:, QKV_W:QKV_W + CONV_W].astype(BF16)
        cg_ref[...] = p[:, QKV_W + CONV_W:QKV_W + 2 * CONV_W].astype(BF16)
        hc_ref[...] = p[:, QKV_W + 2 * CONV_W:].astype(BF16)

    row = lambda n: pl.BlockSpec((tm, n), lambda i: (i, 0))
    full = lambda a: pl.BlockSpec(a.shape, lambda i: (0, 0))

    def sequence_block(b):
        return pl.BlockSpec((BLOCK, D_MODEL), lambda i: (jnp.maximum(i * per_step + b - 1, 0), 0))

    if lead is None:
        first_in, first_args, first_out, first_shape = [row(D_MODEL)], [h], [], []
    else:
        first_in = [full(lead)] + [sequence_block(b) for b in range(per_step)]
        first_args = [lead] + [h] * per_step
        first_out, first_shape = [row(D_MODEL)], [jax.ShapeDtypeStruct((t, D_MODEL), F32)]
    return pl.pallas_call(
        body, name="in_proj", grid=(t // tm,),
        in_specs=first_in + [full(g), full(w), row(2 * HEAD_DIM), row(2 * HEAD_DIM)],
        out_specs=first_out + [row(D_MODEL), row(ATTN_W), row(KV_W), row(KV_W), row(CONV_W), row(CONV_W), row(CONV_W)],
        out_shape=first_shape + [jax.ShapeDtypeStruct((t, D_MODEL), BF16), jax.ShapeDtypeStruct((t, ATTN_W), BF16),
                                 jax.ShapeDtypeStruct((t, KV_W), BF16), jax.ShapeDtypeStruct((t, KV_W), BF16),
                                 jax.ShapeDtypeStruct((t, CONV_W), BF16), jax.ShapeDtypeStruct((t, CONV_W), BF16),
                                 jax.ShapeDtypeStruct((t, CONV_W), BF16)],
        compiler_params=_params("parallel"),
    )(*first_args, g, w, *tabs)


def _attn_bias():
    r = lax.broadcasted_iota(jnp.int32, (3, BLOCK, 2 * BLOCK), 1)
    c = lax.broadcasted_iota(jnp.int32, (3, BLOCK, 2 * BLOCK), 2)
    i = lax.broadcasted_iota(jnp.int32, (3, BLOCK, 2 * BLOCK), 0)
    ok = (c > r) & (c <= r + BLOCK) & (c + (i - 1) * BLOCK >= LEAD_PAD)
    return jnp.where(ok, 0.0, -jnp.inf).astype(F32)


def _attn_scores(qh, kg, bias):
    return lax.dot_general(qh, kg, (((1,), (1,)), ((), ())), preferred_element_type=F32) + bias


def _attn_probs(s, sk):
    m = jnp.maximum(jnp.max(s, axis=-1, keepdims=True), sk)
    e = jnp.exp(s - m)
    es = jnp.exp(sk - m)
    rden = 1.0 / (jnp.sum(e, axis=-1, keepdims=True) + es)
    return e * rden, es * rden


def _head(hh):
    return slice(hh * HEAD_DIM, (hh + 1) * HEAD_DIM)


def _two_blocks(ref, i):
    prev = jnp.maximum(i - 1, 0)
    return jnp.concatenate([ref[pl.ds(pl.multiple_of(prev * BLOCK, BLOCK), BLOCK), :],
                            ref[pl.ds(pl.multiple_of(i * BLOCK, BLOCK), BLOCK), :]], axis=0)


def _attn_fwd(q, k, v, bias, sinks, tm):
    t = q.shape[0]
    per_step = tm // BLOCK
    heads = range(N_Q_HEADS)

    def body(s_ref, q_ref, k_ref, v_ref, bias_ref, o_ref):
        for b in range(per_step):
            i = pl.program_id(0) * per_step + b
            rows = slice(b * BLOCK, (b + 1) * BLOCK)
            kc, vc = _two_blocks(k_ref, i), _two_blocks(v_ref, i)
            bias_i = bias_ref[jnp.minimum(i, 2)]
            scores = [_attn_scores(q_ref[rows, _head(hh)], kc[:, _head(hh // GROUP)], bias_i) for hh in heads]
            probs = [_attn_probs(scores[hh], s_ref[hh])[0].astype(BF16) for hh in heads]
            for hh in heads:
                o_ref[rows, _head(hh)] = jnp.dot(probs[hh], vc[:, _head(hh // GROUP)],
                                                 preferred_element_type=F32).astype(BF16)

    whole = pl.BlockSpec((t, KV_W), lambda i: (0, 0))
    return pl.pallas_call(
        body, name="attn_fwd", grid=(t // tm,),
        in_specs=[pl.BlockSpec(memory_space=pltpu.SMEM), pl.BlockSpec((tm, ATTN_W), lambda i: (i, 0)), whole, whole,
                  pl.BlockSpec(bias.shape, lambda i: (0, 0, 0))],
        out_specs=pl.BlockSpec((tm, ATTN_W), lambda i: (i, 0)),
        out_shape=jax.ShapeDtypeStruct((t, ATTN_W), BF16),
        compiler_params=_params("parallel"),
    )(sinks, q, k, v, bias)


def _shift_rows(u, halo, n):
    r = pltpu.roll(u, n, 0)
    hr = pltpu.roll(halo, n, 0)
    idx = lax.broadcasted_iota(jnp.int32, hr.shape, 0)
    return jnp.concatenate([jnp.where(idx < n, hr, r[:8]), r[8:]], axis=0)


def _advance_rows(u, halo, n):
    rows = u.shape[0]
    r = pltpu.roll(u, rows - n, 0)
    hr = pltpu.roll(halo, 8 - n, 0)
    idx = lax.broadcasted_iota(jnp.int32, hr.shape, 0)
    return jnp.concatenate([r[:rows - 8], jnp.where(idx >= 8 - n, hr, r[rows - 8:])], axis=0)


def _mix_out(h, o, b, c, hc, cw, ga, gc, w, gp, tm, deps=()):
    t = h.shape[0]

    def body(h_ref, o_ref, b_ref, c_ref, hc_ref, cw_ref, ga_ref, gc_ref, w_ref, gp_ref, h1_ref, y_ref, z_ref, halo):
        @pl.when(pl.program_id(0) == 0)
        def _():
            halo[...] = jnp.zeros_like(halo)

        u = c_ref[...].astype(F32) * hc_ref[...].astype(F32)
        cv = cw_ref[0:1, :] * _shift_rows(u, halo[...], 2) + cw_ref[1:2, :] * _shift_rows(u, halo[...], 1) \
            + cw_ref[2:3, :] * u
        halo[...] = u[tm - 8:]
        yc = b_ref[...].astype(F32) * cv
        y = jnp.concatenate([_rms(o_ref[...].astype(F32), ga_ref[...]), _rms(yc, gc_ref[...])], axis=1).astype(BF16)
        y_ref[...] = y
        z = jnp.dot(y, w_ref[...].reshape(D_MODEL, D_MODEL), preferred_element_type=F32)
        z_ref[...] = z
        h1_ref[...] = h_ref[...] + _rms(z, gp_ref[...])

    row = lambda n: pl.BlockSpec((tm, n), lambda i: (i, 0))
    full = lambda a: pl.BlockSpec(a.shape, lambda i: (0,) * a.ndim)
    body, dep_specs = _behind(body, deps)
    return pl.pallas_call(
        body, name="mix_out", grid=(t // tm,),
        in_specs=dep_specs + [row(D_MODEL), row(ATTN_W), row(CONV_W), row(CONV_W), row(CONV_W), full(cw), full(ga),
                              full(gc), full(w), full(gp)],
        out_specs=[row(D_MODEL), row(D_MODEL), row(D_MODEL)],
        out_shape=[jax.ShapeDtypeStruct((t, D_MODEL), F32), jax.ShapeDtypeStruct((t, D_MODEL), BF16),
                   jax.ShapeDtypeStruct((t, D_MODEL), F32)],
        scratch_shapes=[pltpu.VMEM((8, CONV_W), F32)],
        compiler_params=_params("arbitrary"),
    )(*deps, h, o, b, c, hc, cw, ga, gc, w, gp)


def _mlp(h1, g1, wu, wd, g2, tm, target=None):
    t = h1.shape[0]
    nj = D_FF // FF_CHUNK
    per_step = tm // BLOCK if target is not None else 0

    def body(h1_ref, g1_ref, wu_ref, wd_ref, g2_ref, *rest):
        t_refs, outs = rest[:per_step], rest[per_step:]
        a2_ref, slope_ref = (outs[1], outs[2]) if target is None else (outs[2], outs[3])
        a2 = _rms(h1_ref[...], g1_ref[...]).astype(BF16)
        a2_ref[...] = a2
        f = None
        for j in range(nj):
            up = jnp.dot(a2, wu_ref[j], preferred_element_type=F32)
            r = jnp.maximum(up, 0.0)
            slope_ref[:, j * FF_CHUNK:(j + 1) * FF_CHUNK] = (r + r).astype(BF16)
            part = jnp.dot((r * r).astype(BF16), wd_ref[j], preferred_element_type=F32)
            f = part if f is None else f + part
        h2 = h1_ref[...] + _rms(f, g2_ref[...])
        if target is None:
            outs[0][...] = h2
            outs[3][...] = f
            return
        loss_ref, dh_ref, df_ref, dg2_ref = outs[0], outs[1], outs[4], outs[5]
        i = pl.program_id(0)

        @pl.when(i == 0)
        def _():
            loss_ref[...] = jnp.zeros_like(loss_ref)
            dg2_ref[...] = jnp.zeros_like(dg2_ref)

        total = jnp.zeros((), F32)
        dh2 = []
        for b in range(per_step):
            err = h2[b * BLOCK:(b + 1) * BLOCK] - t_refs[b][...]
            if b == 0:
                err = jnp.where(i == 0, 0.0, err)
            dh2.append(err * (1.0 / D_MODEL))
            total = total + jnp.sum(err * err)
        loss_ref[...] += total * (0.5 / D_MODEL)
        dh2 = jnp.concatenate(dh2, axis=0)
        dh_ref[...] = dh2
        df, dg = _rms_bwd(dh2, f, g2_ref[...])
        df_ref[...] = df.astype(BF16)
        dg2_ref[...] += dg

    def target_block(b):
        return pl.BlockSpec((BLOCK, D_MODEL), lambda i: (jnp.maximum(i * per_step + b - 1, 0), 0))

    row = pl.BlockSpec((tm, D_MODEL), lambda i: (i, 0))
    vec = pl.BlockSpec((1, D_MODEL), lambda i: (0, 0))
    resident = pl.BlockSpec(memory_space=pltpu.VMEM)
    wide = pl.BlockSpec((tm, D_FF), lambda i: (i, 0))
    kept = [jax.ShapeDtypeStruct((t, D_MODEL), BF16), jax.ShapeDtypeStruct((t, D_FF), BF16)]
    if target is None:
        specs = [row, row, wide, row]
        shapes = [jax.ShapeDtypeStruct((t, D_MODEL), F32)] + kept + [jax.ShapeDtypeStruct((t, D_MODEL), F32)]
    else:
        specs = [pl.BlockSpec((8, 128), lambda i: (0, 0)), row, row, wide, row, vec]
        shapes = [jax.ShapeDtypeStruct((8, 128), F32), jax.ShapeDtypeStruct((t, D_MODEL), F32)] + kept \
            + [jax.ShapeDtypeStruct((t, D_MODEL), BF16), jax.ShapeDtypeStruct((1, D_MODEL), F32)]
    outs = pl.pallas_call(
        body, name="mlp", grid=(t // tm,),
        in_specs=[row, vec, resident, resident, vec] + [target_block(b) for b in range(per_step)],
        out_specs=specs, out_shape=shapes,
        compiler_params=_params("parallel" if target is None else "arbitrary"),
    )(h1, g1, wu, wd, g2, *([target] * per_step))
    if target is None:
        return tuple(outs)
    return (tuple(outs[:2]), outs[2], outs[3], tuple(outs[4:]))


def _mlp_bwd_hidden(dh2, f, g2, slope, wd, tm, deps=(), df=None):
    t = slope.shape[0]
    nj = D_FF // FF_CHUNK

    def hidden(df, slope_ref, wd_ref, dup_ref):
        for j in range(nj):
            cols = slice(j * FF_CHUNK, (j + 1) * FF_CHUNK)
            dact = lax.dot_general(df, wd_ref[j], (((1,), (1,)), ((), ())), preferred_element_type=F32)
            dup_ref[:, cols] = (dact * slope_ref[:, cols].astype(F32)).astype(BF16)

    def body(dh2_ref, f_ref, g2_ref, slope_ref, wd_ref, df_ref, dup_ref, dg2_ref):
        @pl.when(pl.program_id(0) == 0)
        def _():
            dg2_ref[...] = jnp.zeros_like(dg2_ref)

        df, dg = _rms_bwd(dh2_ref[...], f_ref[...], g2_ref[...])
        dg2_ref[...] += dg
        df = df.astype(BF16)
        df_ref[...] = df
        hidden(df, slope_ref, wd_ref, dup_ref)

    def body_from_df(df_ref, slope_ref, wd_ref, dup_ref):
        hidden(df_ref[...], slope_ref, wd_ref, dup_ref)

    row = pl.BlockSpec((tm, D_MODEL), lambda i: (i, 0))
    wide = pl.BlockSpec((tm, D_FF), lambda i: (i, 0))
    vec = pl.BlockSpec((1, D_MODEL), lambda i: (0, 0))
    resident = pl.BlockSpec(memory_space=pltpu.VMEM)
    if df is not None:
        body_from_df, dep_specs = _behind(body_from_df, deps)
        return pl.pallas_call(
            body_from_df, name="mlp_bwd_hidden", grid=(t // tm,), in_specs=dep_specs + [row, wide, resident],
            out_specs=wide, out_shape=jax.ShapeDtypeStruct((t, D_FF), BF16), compiler_params=_params("parallel"),
        )(*deps, df, slope, wd)
    body, dep_specs = _behind(body, deps)
    return pl.pallas_call(
        body, name="mlp_bwd_hidden", grid=(t // tm,),
        in_specs=dep_specs + [row, row, vec, wide, resident],
        out_specs=[row, wide, vec],
        out_shape=[jax.ShapeDtypeStruct((t, D_MODEL), BF16), jax.ShapeDtypeStruct((t, D_FF), BF16),
                   jax.ShapeDtypeStruct((1, D_MODEL), F32)],
        compiler_params=_params("arbitrary"),
    )(*deps, dh2, f, g2, slope, wd)


def _mlp_bwd_input(dup, wu, h1, g1, dh2, tm):
    t = dh2.shape[0]
    nj = D_FF // FF_CHUNK

    def body(dup_ref, wu_ref, h1_ref, g1_ref, dh2_ref, dh1_ref, dg1_ref):
        @pl.when(pl.program_id(0) == 0)
        def _():
            dg1_ref[...] = jnp.zeros_like(dg1_ref)

        da2 = None
        for j in range(nj):
            part = lax.dot_general(dup_ref[:, j * FF_CHUNK:(j + 1) * FF_CHUNK], wu_ref[j], (((1,), (1,)), ((), ())),
                                   preferred_element_type=F32)
            da2 = part if da2 is None else da2 + part
        dx, dg = _rms_bwd(da2, h1_ref[...], g1_ref[...])
        dh1_ref[...] = dh2_ref[...] + dx
        dg1_ref[...] += dg

    row = pl.BlockSpec((tm, D_MODEL), lambda i: (i, 0))
    vec = pl.BlockSpec((1, D_MODEL), lambda i: (0, 0))
    return pl.pallas_call(
        body, name="mlp_bwd_input", grid=(t // tm,),
        in_specs=[pl.BlockSpec((tm, D_FF), lambda i: (i, 0)), pl.BlockSpec(memory_space=pltpu.VMEM), row, vec, row],
        out_specs=[row, vec],
        out_shape=[jax.ShapeDtypeStruct((t, D_MODEL), F32), jax.ShapeDtypeStruct((1, D_MODEL), F32)],
        compiler_params=_params("arbitrary"),
    )(dup, wu, h1, g1, dh2)


def _row_split(t):
    tile = min(t, 1024)
    return tile, t // tile, t % tile


def _row_split_specs(t, cols):
    tile, whole, rest = _row_split(t)
    specs = [pl.BlockSpec((tile, cols), lambda r: (jnp.minimum(r, whole - 1), 0))]
    if rest:
        specs.append(pl.BlockSpec((rest, cols), lambda r: (whole * tile // rest, 0)))
    return specs


def _weight_grad(x, y, name, x_is_slope=False, deps=()):
    t, k = x.shape
    n = y.shape[1]
    tn = FF_CHUNK
    tk = FF_CHUNK if k % FF_CHUNK == 0 else k
    _, whole, rest = _row_split(t)
    steps = whole + bool(rest)

    def body(*refs):
        o_ref, ob_ref, r = refs[-2], refs[-1], pl.program_id(0)

        @pl.when(r == 0)
        def _():
            o_ref[...] = jnp.zeros_like(o_ref)

        def add(x_ref, y_ref):
            for a in range(k // tk):
                xv = x_ref[:, a * tk:(a + 1) * tk]
                if x_is_slope:
                    xv = xv.astype(F32)
                    xv = (xv * xv * 0.25).astype(BF16)
                for b in range(n // tn):
                    o_ref[a, b] += lax.dot_general(xv, y_ref[:, b * tn:(b + 1) * tn], (((0,), (0,)), ((), ())),
                                                   preferred_element_type=F32)

        if rest:
            pl.when(r < whole)(lambda: add(refs[0], refs[2]))
            pl.when(r == whole)(lambda: add(refs[1], refs[3]))
        else:
            add(refs[0], refs[1])

        @pl.when(r == steps - 1)
        def _():
            ob_ref[...] = o_ref[...].astype(BF16)

    vm = pl.BlockSpec(memory_space=pltpu.VMEM)
    body, dep_specs = _behind(body, deps)
    return pl.pallas_call(
        body, name=name, grid=(steps,),
        in_specs=dep_specs + _row_split_specs(t, k) + _row_split_specs(t, n), out_specs=[vm, vm],
        out_shape=[jax.ShapeDtypeStruct((k // tk, n // tn, tk, tn), F32),
                   jax.ShapeDtypeStruct((k // tk, n // tn, tk, tn), BF16)],
        compiler_params=_params("arbitrary"),
    )(*deps, *([x] * (1 + bool(rest))), *([y] * (1 + bool(rest))))


def _mix_out_bwd(dh1, z, gp, w, o, b, c, hc, cw, ga, gc, tm, deps=()):
    t = dh1.shape[0]
    nt = t // tm
    per16 = tm // 16

    def body(dh1_ref, z_ref, gp_ref, w_ref, o_ref, b_ref, c_ref, hc_ref, cp_ref, hp_ref, cw_ref, ga_ref, gc_ref,
             dz_ref, do_ref, dbch_ref, dgp_ref, dga_ref, dgc_ref, dcw_ref, halo):
        i = pl.program_id(0)

        @pl.when(i == 0)
        def _():
            halo[...] = jnp.zeros_like(halo)
            dgp_ref[...] = jnp.zeros_like(dgp_ref)
            dga_ref[...] = jnp.zeros_like(dga_ref)
            dgc_ref[...] = jnp.zeros_like(dgc_ref)
            dcw_ref[...] = jnp.zeros_like(dcw_ref)

        dz, dgp = _rms_bwd(dh1_ref[...], z_ref[...], gp_ref[...])
        dgp_ref[...] += dgp
        dz = dz.astype(BF16)
        dz_ref[...] = dz
        dy = lax.dot_general(dz, w_ref[...].reshape(D_MODEL, D_MODEL), (((1,), (1,)), ((), ())),
                             preferred_element_type=F32)
        do, dga = _rms_bwd(dy[:, :ATTN_W], o_ref[...].astype(F32), ga_ref[...])
        do_ref[...] = do.astype(BF16)
        dga_ref[...] += dga

        cc, hh = c_ref[...].astype(F32), hc_ref[...].astype(F32)
        u = cc * hh
        first = i == nt - 1
        u_before = jnp.where(first, 0.0, (cp_ref[...].astype(F32) * hp_ref[...].astype(F32))[8:])
        u1 = _shift_rows(u, u_before, 1)
        u2 = _shift_rows(u, u_before, 2)
        cv = cw_ref[0:1, :] * u2 + cw_ref[1:2, :] * u1 + cw_ref[2:3, :] * u
        bb = b_ref[...].astype(F32)
        dyc, dgc = _rms_bwd(dy[:, ATTN_W:], bb * cv, gc_ref[...])
        dgc_ref[...] += dgc
        dcv = dyc * bb
        d1 = _advance_rows(dcv, halo[...], 1)
        d2 = _advance_rows(dcv, halo[...], 2)
        halo[...] = dcv[:8]
        du = cw_ref[2:3, :] * dcv + cw_ref[1:2, :] * d1 + cw_ref[0:1, :] * d2
        dbch_ref[...] = jnp.concatenate([dyc * cv, du * hh, du * cc], axis=1).astype(BF16)
        dcw_ref[...] += jnp.concatenate([jnp.sum(dcv * u2, axis=0, keepdims=True),
                                         jnp.sum(dcv * u1, axis=0, keepdims=True),
                                         jnp.sum(dcv * u, axis=0, keepdims=True)], axis=0)

    row = lambda n: pl.BlockSpec((tm, n), lambda i: (nt - 1 - i, 0))
    before = pl.BlockSpec((16, CONV_W), lambda i: (jnp.maximum((nt - 1 - i) * per16 - 1, 0), 0))
    full = lambda a: pl.BlockSpec(a.shape, lambda i: (0,) * a.ndim)
    vec = lambda n: pl.BlockSpec((1, n), lambda i: (0, 0))
    body, dep_specs = _behind(body, deps)
    return pl.pallas_call(
        body, name="mix_out_bwd", grid=(nt,),
        in_specs=dep_specs + [row(D_MODEL), row(D_MODEL), full(gp), full(w), row(ATTN_W), row(CONV_W), row(CONV_W),
                              row(CONV_W), before, before, full(cw), full(ga), full(gc)],
        out_specs=[row(D_MODEL), row(ATTN_W), row(3 * CONV_W), vec(D_MODEL), vec(ATTN_W), vec(CONV_W),
                   pl.BlockSpec((CONV_K, CONV_W), lambda i: (0, 0))],
        out_shape=[jax.ShapeDtypeStruct((t, D_MODEL), BF16), jax.ShapeDtypeStruct((t, ATTN_W), BF16),
                   jax.ShapeDtypeStruct((t, 3 * CONV_W), BF16), jax.ShapeDtypeStruct((1, D_MODEL), F32),
                   jax.ShapeDtypeStruct((1, ATTN_W), F32), jax.ShapeDtypeStruct((1, CONV_W), F32),
                   jax.ShapeDtypeStruct((CONV_K, CONV_W), F32)],
        scratch_shapes=[pltpu.VMEM((8, CONV_W), F32)],
        compiler_params=_params("arbitrary"),
    )(*deps, dh1, z, gp, w, o, b, c, hc, c, hc, cw, ga, gc)


def _attn_bwd(q, k, v, o, do, bias, sinks, tm, deps=()):
    t = q.shape[0]
    per_step = tm // BLOCK

    def body(s_ref, q_ref, k_ref, v_ref, o_ref, do_ref, bias_ref, dq_ref, dk_ref, dv_ref, ds_ref):
        step = pl.program_id(0)

        @pl.when(step == 0)
        def _():
            ds_ref[...] = jnp.zeros_like(ds_ref)

        heads = range(N_Q_HEADS)

        def first_matmuls(b):
            i = step * per_step + b
            rows = slice(b * BLOCK, (b + 1) * BLOCK)
            kc, vc = _two_blocks(k_ref, i), _two_blocks(v_ref, i)
            bias_i = bias_ref[jnp.minimum(i, 2)]
            kgs = [kc[:, _head(g)] for g in range(N_KV_HEADS)]
            vgs = [vc[:, _head(g)] for g in range(N_KV_HEADS)]
            qs = [q_ref[rows, _head(hh)] for hh in heads]
            dosb = [do_ref[rows, _head(hh)] for hh in heads]
            dos = [d.astype(F32) for d in dosb]
            scores = [_attn_scores(qs[hh], kgs[hh // GROUP], bias_i) for hh in heads]
            dps = [lax.dot_general(dosb[hh], vgs[hh // GROUP], (((1,), (1,)), ((), ())), preferred_element_type=F32)
                   for hh in heads]
            return kgs, qs, dos, dosb, scores, dps

        dsink = [jnp.zeros((BLOCK, 1), F32) for _ in range(N_Q_HEADS)]
        ahead = None
        for b in range(per_step):
            i = step * per_step + b
            rows = slice(b * BLOCK, (b + 1) * BLOCK)
            kgs, qs, dos, dosb, scores, dps = first_matmuls(b)
            ps, dss = [], []
            for hh in heads:
                p, share = _attn_probs(scores[hh], s_ref[hh])
                drow = jnp.sum(dos[hh] * o_ref[rows, _head(hh)].astype(F32), axis=-1, keepdims=True)
                dss.append((p * (dps[hh] - drow)).astype(BF16))
                ps.append(p.astype(BF16))
                dsink[hh] = dsink[hh] + share * drow
            for hh in heads:
                dq_ref[rows, _head(hh)] = (jnp.dot(dss[hh], kgs[hh // GROUP], preferred_element_type=F32)
                                           * SCALE).astype(BF16)
            groups = [slice(GROUP * g, GROUP * (g + 1)) for g in range(N_KV_HEADS)]
            dkg = [lax.dot_general(jnp.concatenate(dss[gr], axis=0), jnp.concatenate(qs[gr], axis=0),
                                   (((0,), (0,)), ((), ())), preferred_element_type=F32) for gr in groups]
            dvg = [lax.dot_general(jnp.concatenate(ps[gr], axis=0), jnp.concatenate(dosb[gr], axis=0),
                                   (((0,), (0,)), ((), ())), preferred_element_type=F32) for gr in groups]
            dkb, dvb = jnp.concatenate(dkg, axis=1), jnp.concatenate(dvg, axis=1)
            if b == 0:
                @pl.when(step > 0)
                def _():
                    before = pl.ds(pl.multiple_of((i - 1) * BLOCK, BLOCK), BLOCK)
                    dk_ref[before, :] += dkb[:BLOCK]
                    dv_ref[before, :] += dvb[:BLOCK]
            else:
                at = pl.ds(pl.multiple_of((i - 1) * BLOCK, BLOCK), BLOCK)
                dk_ref[at, :] = ahead[0] + dkb[:BLOCK]
                dv_ref[at, :] = ahead[1] + dvb[:BLOCK]
            ahead = (dkb[BLOCK:], dvb[BLOCK:])
        last = pl.ds(pl.multiple_of(((step + 1) * per_step - 1) * BLOCK, BLOCK), BLOCK)
        dk_ref[last, :] = ahead[0]
        dv_ref[last, :] = ahead[1]
        for hh in range(N_Q_HEADS):
            ds_ref[hh:hh + 1, :] -= jnp.sum(dsink[hh])

    whole = pl.BlockSpec((t, KV_W), lambda i: (0, 0))
    blk = pl.BlockSpec((tm, ATTN_W), lambda i: (i, 0))
    body, dep_specs = _behind(body, deps)
    return pl.pallas_call(
        body, name="attn_bwd", grid=(t // tm,),
        in_specs=dep_specs + [pl.BlockSpec(memory_space=pltpu.SMEM), blk, whole, whole, blk, blk,
                              pl.BlockSpec(bias.shape, lambda i: (0, 0, 0))],
        out_specs=[blk, whole, whole, pl.BlockSpec((N_Q_HEADS, 128), lambda i: (0, 0))],
        out_shape=[jax.ShapeDtypeStruct((t, ATTN_W), BF16), jax.ShapeDtypeStruct((t, KV_W), F32),
                   jax.ShapeDtypeStruct((t, KV_W), F32), jax.ShapeDtypeStruct((N_Q_HEADS, 128), F32)],
        compiler_params=_params("arbitrary"),
    )(*deps, sinks, q, k, v, o, do, bias)


def _in_proj_bwd(dq, dk, dv, dbch, w, dh1, h, g, tabs, tm, split_lead=False):
    t = h.shape[0]
    nt = t // tm

    def body(dq_ref, dk_ref, dv_ref, dbch_ref, w_ref, dh1_ref, h_ref, g_ref, c_ref, s_ref, *rest):
        dp_ref, dg_ref = rest[2:4] if split_lead else rest[1:3]
        i = pl.program_id(0)

        @pl.when(i == 0)
        def _():
            dg_ref[...] = jnp.zeros_like(dg_ref)

        cos, sa, sb = _rope_factors(c_ref[...], s_ref[...])
        rep = ATTN_W // (2 * HEAD_DIM)
        dqr = _rope_bwd(dq_ref[...].astype(F32), jnp.tile(cos, (1, rep)), jnp.tile(sa, (1, rep)),
                        jnp.tile(sb, (1, rep)))
        dkr = _rope_bwd(dk_ref[...], cos, sa, sb)
        dp = jnp.concatenate([dqr.astype(BF16), dkr.astype(BF16), dv_ref[...].astype(BF16), dbch_ref[...]], axis=1)
        dp_ref[...] = dp
        da = jnp.dot(dp, w_ref[...], preferred_element_type=F32)
        dx, dg = _rms_bwd(da, h_ref[...], g_ref[...])
        dg_ref[...] += dg
        dh = dh1_ref[...] + dx
        if not split_lead:
            rest[0][...] = dh
            return
        lead_ref, seq_ref, stage, sems = rest[0], rest[1], rest[4], rest[5]

        def copy(j, slot, first):
            if first:
                return pltpu.make_async_copy(stage.at[slot, pl.ds(BLOCK, tm - BLOCK)],
                                             seq_ref.at[pl.ds(0, tm - BLOCK)], sems.at[slot])
            return pltpu.make_async_copy(stage.at[slot], seq_ref.at[pl.ds(pl.multiple_of(j * tm - BLOCK, BLOCK), tm)],
                                         sems.at[slot])

        slot = i % 2
        pl.when(i == 2)(lambda: copy(0, slot, True).wait())
        pl.when(i > 2)(lambda: copy(i - 2, slot, False).wait())
        stage[slot] = dh

        @pl.when(i == 0)
        def _():
            lead_ref[...] = dh[:BLOCK]
            copy(0, slot, True).start()

        pl.when(i > 0)(lambda: copy(i, slot, False).start())

        @pl.when(i == nt - 1)
        def _():
            for j in range(max(nt - 2, 0), nt):
                copy(j, j % 2, j == 0).wait()

    row = lambda n: pl.BlockSpec((tm, n), lambda i: (i, 0))
    full = lambda a: pl.BlockSpec(a.shape, lambda i: (0, 0))
    dh_specs, dh_shapes, scratch = [row(D_MODEL)], [jax.ShapeDtypeStruct((t, D_MODEL), F32)], []
    if split_lead:
        dh_specs = [pl.BlockSpec((BLOCK, D_MODEL), lambda i: (0, 0)), pl.BlockSpec(memory_space=pl.ANY)]
        dh_shapes = [jax.ShapeDtypeStruct((BLOCK, D_MODEL), F32), jax.ShapeDtypeStruct((t - BLOCK, D_MODEL), F32)]
        scratch = [pltpu.VMEM((2, tm, D_MODEL), F32), pltpu.SemaphoreType.DMA((2,))]
    outs = pl.pallas_call(
        body, name="in_proj_bwd", grid=(nt,),
        in_specs=[row(ATTN_W), row(KV_W), row(KV_W), row(3 * CONV_W), full(w), row(D_MODEL), row(D_MODEL), full(g),
                  row(2 * HEAD_DIM), row(2 * HEAD_DIM)],
        out_specs=dh_specs + [row(IN_W), pl.BlockSpec((1, D_MODEL), lambda i: (0, 0))],
        out_shape=dh_shapes + [jax.ShapeDtypeStruct((t, IN_W), BF16), jax.ShapeDtypeStruct((1, D_MODEL), F32)],
        scratch_shapes=scratch,
        compiler_params=_params("arbitrary"),
    )(dq, dk, dv, dbch, w, dh1, h, g, *tabs)
    return (tuple(outs[:2]) if split_lead else outs[0],) + tuple(outs[-2:])


class _Tiles:
    def __init__(self, t):
        self.tm = _row_tile(t, 640)
        self.ts = self.tm
        self.tabs = _rope_tables(t)
        self.bias = _attn_bias()


def _mixer_fwd(h, p, tl, lead=None):
    if lead is None:
        a, q, k, v, b, c, hc = _in_proj(h, p["mix_pre_g"], p["w_in"], tl.tabs, tl.ts)
    else:
        h, a, q, k, v, b, c, hc = _in_proj(h, p["mix_pre_g"], p["w_in"], tl.tabs, tl.ts, lead)
    o = _attn_fwd(q, k, v, tl.bias, p["sinks"], tl.tm)
    return (h, a, q, k, v, b, c, hc, o)


def _out_fwd(mixed, p, tl, deps=()):
    h, a, q, k, v, b, c, hc, o = mixed
    h1, y, z = _mix_out(h, o, b, c, hc, p["conv_w"], p["attn_out_g"], p["conv_out_g"], p["w_out"], p["mix_post_g"],
                        tl.ts, deps)
    return h1, mixed + (h1, y, z)


def _mlp_fwd(h1, saved, p, tl, target=None):
    h2, a2, slope, f = _mlp(h1, p["mlp_pre_g"], p["w_up"], p["w_down"], p["mlp_post_g"], tl.tm, target)
    return h2, saved + (a2, slope, f)


def _mlp_part_bwd(dh, saved, p, tl, deps=()):
    h1, a2, slope, f = saved[9], saved[12], saved[13], saved[14]
    if isinstance(f, tuple):
        df, dg2 = f
        dup = _mlp_bwd_hidden(None, None, None, slope, p["w_down"], tl.tm, deps, df)
    else:
        df, dup, dg2 = _mlp_bwd_hidden(dh, f, p["mlp_post_g"], slope, p["w_down"], tl.tm, deps)
    dh1, dg1 = _mlp_bwd_input(dup, p["w_up"], h1, p["mlp_pre_g"], dh, tl.tm)
    g = {"w_down": [d.reshape(N_CHIPS, FF_CHUNK, D_MODEL)
                    for d in _weight_grad(slope, df, "grad_w_down", x_is_slope=True)],
         "w_up": [d.reshape(N_CHIPS, D_MODEL, FF_CHUNK) for d in _weight_grad(a2, dup, "grad_w_up")],
         "mlp_post_g": dg2, "mlp_pre_g": dg1}
    return dh1, g


def _mix_out_part_bwd(dh1, saved, p, tl, deps=()):
    b, c, hc, o, y, z = saved[5], saved[6], saved[7], saved[8], saved[10], saved[11]
    dz, do, dbch, dgp, dga, dgc, dcw = _mix_out_bwd(dh1, z, p["mix_post_g"], p["w_out"], o, b, c, hc, p["conv_w"],
                                                    p["attn_out_g"], p["conv_out_g"], tl.ts, deps)
    g = {"w_out": [d.reshape(N_CHIPS, D_MODEL // N_CHIPS, D_MODEL) for d in _weight_grad(y, dz, "grad_w_out")],
         "mix_post_g": dgp, "attn_out_g": dga, "conv_out_g": dgc, "conv_w": dcw}
    return (dh1, do, dbch), g


def _attn_in_part_bwd(carry, saved, p, tl, deps=(), split_lead=False):
    dh1, do, dbch = carry
    h_in, q, k, v, o = saved[0], saved[2], saved[3], saved[4], saved[8]
    dq, dk, dv, dsink = _attn_bwd(q, k, v, o, do, tl.bias, p["sinks"], tl.tm, deps)
    dh, dproj, dgi = _in_proj_bwd(dq, dk, dv, dbch, p["w_in"], dh1, h_in, p["mix_pre_g"], tl.tabs, tl.ts, split_lead)
    return dh, dproj, {"mix_pre_g": dgi, "sinks": dsink[:, 0]}


def _in_grad(dproj, saved, deps=()):
    return [d.reshape(N_CHIPS, IN_W // N_CHIPS, D_MODEL) for d in _weight_grad(dproj, saved[1], "grad_w_in", deps=deps)]


def _place():
    return lax.axis_index("x"), lax.axis_index("y"), lax.axis_index("c")


def _other_chips(x, y):
    return [(1 - x, y), (x, 1 - y), (1 - x, 1 - y)]


_HBM = pl.BlockSpec(memory_space=pltpu.HBM)
_SEM = pl.BlockSpec(memory_space=pltpu.SEMAPHORE)
_EFFECT = pltpu.SideEffectType.DATAFLOW_SIDE_EFFECTING


class _Exchange:
    def __init__(self, name, bufs, plan, n, after=()):
        self.name, self.plan, nb = name, plan, len(bufs)
        n_in = nb + len(after)

        def body(*refs):
            send, recv, token = refs[n_in], refs[n_in + 1], refs[-1]
            for k, (src, dst, target, _) in enumerate(plan(refs[:nb])):
                pltpu.make_async_remote_copy(src_ref=src, dst_ref=dst, send_sem=send.at[k], recv_sem=recv.at[k],
                                             device_id=target, device_id_type=MESH).start()
            token[...] = jnp.zeros_like(token)

        outs = pl.pallas_call(
            body, name=name + "_start",
            out_shape=(pltpu.SemaphoreType.DMA((n,)), pltpu.SemaphoreType.DMA((n,)),
                       *[pltpu.HBM(b.shape, b.dtype) for b in bufs], jax.ShapeDtypeStruct((8, 128), F32)),
            in_specs=[_HBM] * nb + [pl.BlockSpec(memory_space=pl.ANY)] * len(after),
            out_specs=(_SEM, _SEM, *[_HBM] * nb, pl.BlockSpec(memory_space=pltpu.VMEM)),
            input_output_aliases={i: 2 + i for i in range(nb)},
            compiler_params=pltpu.CompilerParams(has_side_effects=_EFFECT),
        )(*[pltpu.with_memory_space_constraint(b, pltpu.HBM) for b in bufs], *after)
        self.send, self.recv, self.bufs, self.token = outs[0], outs[1], list(outs[2:2 + nb]), outs[-1]

    def wait(self, *after):
        plan, nb = self.plan, len(self.bufs)

        def body(*refs):
            send, recv = refs[nb], refs[nb + 1]
            for k, (src, _, target, land) in enumerate(plan(refs[:nb])):
                cp = pltpu.make_async_remote_copy(src_ref=src, dst_ref=land, send_sem=send.at[k], recv_sem=recv.at[k],
                                                  device_id=target, device_id_type=MESH)
                cp.wait_send()
                cp.wait_recv()

        outs = pl.pallas_call(
            body, name=self.name + "_wait", out_shape=[pltpu.HBM(b.shape, b.dtype) for b in self.bufs],
            in_specs=[_HBM] * nb + [_SEM, _SEM] + [pl.BlockSpec(memory_space=pl.ANY)] * len(after),
            out_specs=[_HBM] * nb, input_output_aliases={i: i for i in range(nb)},
            compiler_params=pltpu.CompilerParams(has_side_effects=_EFFECT),
        )(*self.bufs, self.send, self.recv, *after)
        return list(outs)


def _gather_plan(n):
    def plan(refs):
        x, y, c = _place()
        me = 2 * x + y
        return [(refs[a].at[me], refs[a].at[me], (px, py, c), refs[a].at[2 * px + py])
                for a in range(n) for px, py in _other_chips(x, y)]

    return plan


def _gather_half_plan(n, half_rows):
    def plan(refs):
        x, y, c = _place()
        me = 2 * x + y
        out = []
        for a in range(n):
            rows = pl.ds(c * half_rows[a], half_rows[a])
            out += [(refs[a].at[me, rows], refs[a].at[me, rows], (px, py, c), refs[a].at[2 * px + py, rows])
                    for px, py in _other_chips(x, y)]
        return out

    return plan


def _hand_over_plan(n, half_rows):
    def plan(refs):
        x, y, c = _place()
        out = []
        for a in range(n):
            mine, theirs = pl.ds(c * half_rows[a], half_rows[a]), pl.ds((1 - c) * half_rows[a], half_rows[a])
            for px, py in _other_chips(x, y):
                held = refs[a].at[2 * px + py, mine]
                out.append((held, held, (x, y, 1 - c), refs[a].at[2 * px + py, theirs]))
        return out

    return plan


def _peers():
    x, y, c = _place()
    return [(k - 1, (x ^ (k >> 2), y ^ ((k >> 1) & 1), c ^ (k & 1))) for k in range(1, N_DEV)]


def _scatter_plan(n, half_rows):
    def plan(refs):
        out = []
        for a in range(n):
            hr = half_rows[a]
            for k, (px, py, pc) in _peers():
                out.append((refs[a].at[2 * px + py, pl.ds(pc * hr, hr)], refs[n + a].at[k], (px, py, pc),
                            refs[n + a].at[k]))
        return out

    return plan


def _join_plan(n):
    def plan(refs):
        x, y, c = _place()
        return [(refs[a].at[c], refs[a].at[c], (x, y, 1 - c), refs[a].at[1 - c]) for a in range(n)]

    return plan


def _sum_parts(gs, qs):
    n = len(gs)
    half_rows = [g.shape[1] // 2 for g in gs]
    tr = [_block_rows(hr) for hr in half_rows]
    per = [hr // t for hr, t in zip(half_rows, tr)]
    x, y, c = _place()
    where = jnp.stack([2 * x + y, c]).astype(jnp.int32)

    def body(where_ref, *refs):
        i = pl.program_id(0)
        for a in range(n):
            g_ref, q_ref, o_ref = refs[a], refs[n + a], refs[2 * n + a]

            @pl.when(i < per[a])
            def _():
                total = g_ref[...]
                for k in range(N_DEV - 1):
                    total = total + q_ref[k].astype(F32)
                o_ref[...] = total

    def at(a, i):
        return jnp.minimum(i, per[a] - 1)

    specs_g = [pl.BlockSpec((None, tr[a], gs[a].shape[2]),
                            lambda i, where_ref, a=a: (where_ref[0], where_ref[1] * per[a] + at(a, i), 0)) for a in range(n)]
    specs_q = [pl.BlockSpec((N_DEV - 1, tr[a], gs[a].shape[2]), lambda i, where_ref, a=a: (0, at(a, i), 0))
               for a in range(n)]
    specs_o = [pl.BlockSpec((None, tr[a], gs[a].shape[2]), lambda i, where_ref, a=a: (where_ref[1], at(a, i), 0))
               for a in range(n)]
    return pl.pallas_call(
        body, name="sum_parts",
        grid_spec=pltpu.PrefetchScalarGridSpec(num_scalar_prefetch=1, grid=(max(per),), in_specs=specs_g + specs_q,
                                               out_specs=specs_o),
        out_shape=[jax.ShapeDtypeStruct((2, hr, g.shape[2]), F32) for g, hr in zip(gs, half_rows)],
        compiler_params=_params("arbitrary"),
    )(where, *gs, *qs)


def _all_plan(refs):
    x, y, c = _place()
    mine = refs[0].at[4 * x + 2 * y + c]
    return [(mine, mine, (px, py, pc), refs[0].at[4 * px + 2 * py + pc]) for _, (px, py, pc) in _peers()]


def _sum_devices(parts):
    def body(p_ref, o_ref):
        total = p_ref[0]
        for d in range(1, N_DEV):
            total = total + p_ref[d]
        o_ref[...] = total

    vm = pl.BlockSpec(memory_space=pltpu.VMEM)
    return pl.pallas_call(body, name="sum_devices", in_specs=[vm], out_specs=vm,
                          out_shape=jax.ShapeDtypeStruct(parts.shape[1:], F32))(parts)


def _adamw_math(w, g, m, v):
    m = ADAM_B1 * m + (1.0 - ADAM_B1) * g
    v = ADAM_B2 * v + (1.0 - ADAM_B2) * jnp.square(g)
    m_hat = m / (1.0 - ADAM_B1 ** ADAM_STEP)
    v_hat = v / (1.0 - ADAM_B2 ** ADAM_STEP)
    delta = -ADAM_LR * (m_hat / (jnp.sqrt(v_hat) + ADAM_EPS) + ADAM_WD * w)
    return delta, m, v


def _adamw_large(layer, ws, halves, ms, vs, others):
    n = len(ws)
    tr = [_block_rows(w.shape[1] // 2) for w in ws]
    per = [w.shape[1] // 2 // t for w, t in zip(ws, tr)]
    kept = [] if others is None else [a for four in others for a in four]

    def body(*refs):
        i = pl.program_id(0)
        outs = refs[4 * n + len(kept):]
        for a in range(n):
            w_ref, g_ref, m_ref, v_ref = refs[a], refs[n + a], refs[2 * n + a], refs[3 * n + a]
            g_out, d_ref, nm_ref, nv_ref = outs[4 * a:4 * a + 4]

            @pl.when(i < 2 * per[a])
            def _():
                g = g_ref[...]
                g_out[...] = g
                d_ref[...], nm_ref[...], nv_ref[...] = _adamw_math(w_ref[...], g, m_ref[...], v_ref[...])

    def at(a, i):
        return jnp.minimum(i, 2 * per[a] - 1)

    blk = [pl.BlockSpec((None, tr[a], ws[a].shape[2]), lambda i, a=a: (layer, at(a, i), 0)) for a in range(n)]
    half = [pl.BlockSpec((None, tr[a], ws[a].shape[2]), lambda i, a=a: (at(a, i) // per[a], at(a, i) % per[a], 0))
            for a in range(n)]
    outs = pl.pallas_call(
        body, name="adamw_large", grid=(2 * max(per),),
        in_specs=blk + half + blk + blk + [pl.BlockSpec(memory_space=pl.ANY)] * len(kept),
        out_specs=[blk[a] for a in range(n) for _ in range(4)],
        out_shape=[jax.ShapeDtypeStruct(w.shape, F32) for w in ws for _ in range(4)],
        input_output_aliases={4 * n + k: k for k in range(len(kept))},
        compiler_params=_params("arbitrary"),
    )(*ws, *halves, *ms, *vs, *kept)
    return [outs[4 * a:4 * a + 4] for a in range(n)]


def _adamw_small(ws, gs, ms, vs):
    n = len(ws)

    def body(*refs):
        w_r, g_r, m_r, v_r = refs[:n], refs[n:2 * n], refs[2 * n:3 * n], refs[3 * n:4 * n]
        d_r, nm_r, nv_r = refs[4 * n:5 * n], refs[5 * n:6 * n], refs[6 * n:]
        for a in range(n):
            d_r[a][...], nm_r[a][...], nv_r[a][...] = _adamw_math(w_r[a][...], g_r[a][...], m_r[a][...], v_r[a][...])

    vm = pl.BlockSpec(memory_space=pltpu.VMEM)
    outs = pl.pallas_call(
        body, name="adamw_small", in_specs=[vm] * (4 * n), out_specs=[vm] * (3 * n),
        out_shape=[jax.ShapeDtypeStruct(w.shape, F32) for w in ws] * 3,
    )(*ws, *gs, *ms, *vs)
    return outs[:n], outs[n:2 * n], outs[2 * n:]


_LARGE = ("w_in", "w_out", "w_up", "w_down")
_SMALL = ("meta_tokens", "mix_pre_g", "conv_w", "sinks", "attn_out_g", "conv_out_g", "mix_post_g", "mlp_pre_g",
          "mlp_post_g")
_ORDER = ("meta_tokens", "mix_pre_g", "w_in", "conv_w", "sinks", "attn_out_g", "conv_out_g", "w_out", "mix_post_g",
          "mlp_pre_g", "w_up", "w_down", "mlp_post_g")


class _Reduce:
    def __init__(self, name, grads, after=()):
        self.name, self.n = name, len(grads)
        self.own = [g for g, _ in grads]
        half_rows = [g.shape[1] // 2 for g in self.own]
        zones = [lax.empty((N_DEV - 1, hr, g.shape[2]), BF16) for g, hr in zip(self.own, half_rows)]
        self.exchange = _Exchange(name + "_scatter", [b for _, b in grads] + zones, _scatter_plan(self.n, half_rows),
                                  (N_DEV - 1) * self.n, after)

    @property
    def token(self):
        return self.exchange.token

    def join(self, *after):
        bufs = self.exchange.wait(*after)
        halves = list(_sum_parts(self.own, bufs[self.n:]))
        self.exchange = _Exchange(self.name + "_join", halves, _join_plan(self.n), self.n)

    def done(self, *after):
        return self.exchange.wait(*after)


def _pad_cols(a, n=D_MODEL):
    return jnp.pad(a, ((0, 0), (0, n - a.shape[1])))


def kernel(x, meta_tokens, mix_pre_g, w_in, conv_w, sinks, attn_out_g, conv_out_g, w_out, mix_post_g, mlp_pre_g, w_up, w_down, mlp_post_g, loss_target, m_meta_tokens, m_mix_pre_g, m_w_in, m_conv_w, m_sinks, m_attn_out_g, m_conv_out_g, m_w_out, m_mix_post_g, m_mlp_pre_g, m_w_up, m_w_down, m_mlp_post_g, v_meta_tokens, v_mix_pre_g, v_w_in, v_conv_w, v_sinks, v_attn_out_g, v_conv_out_g, v_w_out, v_mix_post_g, v_mlp_pre_g, v_w_up, v_w_down, v_mlp_post_g):
    w = dict(meta_tokens=meta_tokens, mix_pre_g=mix_pre_g, w_in=w_in, conv_w=conv_w, sinks=sinks,
             attn_out_g=attn_out_g, conv_out_g=conv_out_g, w_out=w_out, mix_post_g=mix_post_g, mlp_pre_g=mlp_pre_g,
             w_up=w_up, w_down=w_down, mlp_post_g=mlp_post_g)
    m = dict(meta_tokens=m_meta_tokens, mix_pre_g=m_mix_pre_g, w_in=m_w_in, conv_w=m_conv_w, sinks=m_sinks,
             attn_out_g=m_attn_out_g, conv_out_g=m_conv_out_g, w_out=m_w_out, mix_post_g=m_mix_post_g,
             mlp_pre_g=m_mlp_pre_g, w_up=m_w_up, w_down=m_w_down, mlp_post_g=m_mlp_post_g)
    v = dict(meta_tokens=v_meta_tokens, mix_pre_g=v_mix_pre_g, w_in=v_w_in, conv_w=v_conv_w, sinks=v_sinks,
             attn_out_g=v_attn_out_g, conv_out_g=v_conv_out_g, w_out=v_w_out, mix_post_g=v_mix_post_g,
             mlp_pre_g=v_mlp_pre_g, w_up=v_w_up, w_down=v_w_down, mlp_post_g=v_mlp_post_g)
    chip = 2 * lax.axis_index("x") + lax.axis_index("y")
    tl = _Tiles(x.shape[1] + BLOCK)

    def zone(quarter):
        return lax.dynamic_update_slice(lax.empty((N_CHIPS,) + quarter.shape, quarter.dtype), quarter[None],
                                        (chip,) + (0,) * quarter.ndim)

    w, m, v = ({**d, "w_in": jnp.swapaxes(d["w_in"], 1, 2)} for d in (w, m, v))
    zones = {n: [zone(w[n][l].astype(BF16)) for l in range(DEPTH)] for n in _LARGE}
    first = _Exchange("gather_first", [zones["w_in"][0], zone(w["conv_w"]), zone(w["meta_tokens"])], _gather_plan(3), 9)
    out0 = _Exchange("gather_out", [zones["w_out"][0]], _gather_plan(1), 3, [first.token])
    mlp_halves = [D_MODEL // 2, FF_CHUNK // 2]
    rest = _Exchange("gather_rest", [zones[n][0] for n in ("w_up", "w_down")], _gather_half_plan(2, mlp_halves), 6,
                     [out0.token])

    def whole_in(quarters):
        return quarters.reshape(IN_W, D_MODEL)

    q_in, q_conv, q_meta = first.wait(rest.token, *tl.tabs, tl.bias)
    conv_whole = jnp.transpose(q_conv, (1, 2, 0, 3)).reshape(DEPTH, CONV_K, CONV_W)
    meta = jnp.transpose(q_meta, (1, 0, 2)).reshape(N_META, D_MODEL)
    p = [{"conv_w": conv_whole[l], "sinks": w["sinks"][l]} for l in range(DEPTH)]
    for l in range(DEPTH):
        for n in ("mix_pre_g", "attn_out_g", "conv_out_g", "mix_post_g", "mlp_pre_g", "mlp_post_g"):
            p[l][n] = w[n][l][None, :]

    lead = jnp.concatenate([jnp.zeros((LEAD_PAD, D_MODEL), F32), meta], axis=0)
    p[0]["w_in"] = whole_in(q_in)
    mixed = _mixer_fwd(x[0], p[0], tl, lead)
    second = _Exchange("gather_second", [zones["w_in"][1], zones["w_out"][1]], _gather_plan(2), 6, [mixed[-1]])
    second_mlp = _Exchange("gather_second_mlp", [zones["w_up"][1], zones["w_down"][1]], _gather_plan(2), 6,
                           [second.token])
    hand_over = _Exchange("hand_over_rest", rest.wait(second_mlp.token), _hand_over_plan(2, mlp_halves), 6)
    p[0]["w_out"], = out0.wait(hand_over.token)
    h1, saved0 = _out_fwd(mixed, p[0], tl)
    p[0]["w_up"], p[0]["w_down"] = hand_over.wait(h1)
    h, saved0 = _mlp_fwd(h1, saved0, p[0], tl)
    q_in, p[1]["w_out"] = second.wait(h)
    p[1]["w_in"] = whole_in(q_in)
    h1, saved1 = _out_fwd(_mixer_fwd(h, p[1], tl), p[1], tl)
    p[1]["w_up"], p[1]["w_down"] = second_mlp.wait(h1)
    (loss_tile, dh), saved1 = _mlp_fwd(h1, saved1, p[1], tl, loss_target[0])

    def adamw(layer, halves, other):
        names = list(halves)
        done = _adamw_large(layer, [w[n] for n in names], [halves[n] for n in names], [m[n] for n in names],
                            [v[n] for n in names], None if other is None else [other[n] for n in names])
        return dict(zip(names, done))

    dh1, g1 = _mlp_part_bwd(dh, saved1, p[1], tl)
    carry, gm = _mix_out_part_bwd(dh1, saved1, p[1], tl)
    dh, dproj, gi = _attn_in_part_bwd(carry, saved1, p[1], tl)
    g1.update(gm, w_in=_in_grad(dproj, saved1), **gi)
    red1 = _Reduce("reduce1", [g1[n] for n in _LARGE])
    dh1, g0 = _mlp_part_bwd(dh, saved0, p[0], tl, [red1.token])
    red1.join(g0["w_down"][0])
    carry, gm = _mix_out_part_bwd(dh1, saved0, p[0], tl, [red1.token])
    first0 = ("w_up", "w_down", "w_out")
    g0.update(gm)
    red0a = _Reduce("reduce0a", [g0[n] for n in first0])
    (dlead, dseq), dproj, gi = _attn_in_part_bwd(carry, saved0, p[0], tl, [red0a.token], split_lead=True)
    g0.update(gi)
    grad_x = dseq[None]
    grads = {n: [g0[n], g1[n]] for n in g0 if n not in _LARGE}

    rows = [dlead[LEAD_PAD:]]
    for n in ("mix_pre_g", "mix_post_g", "mlp_pre_g", "mlp_post_g"):
        rows += grads[n]
    rows += [jnp.concatenate([grads["attn_out_g"][l], grads["conv_out_g"][l]], axis=1) for l in range(DEPTH)]
    rows.append(jnp.concatenate(grads["conv_w"], axis=1))
    rows.append(_pad_cols(jnp.concatenate(grads["sinks"])[None, :]))
    rows.append(_pad_cols(loss_tile[:1]))
    packed = jnp.concatenate(rows, axis=0)
    packed = jnp.pad(packed, ((0, SMALL_ROWS - packed.shape[0]), (0, 0)))
    device = 2 * chip + lax.axis_index("c")
    small_parts = _Exchange("gather_small", [lax.dynamic_update_slice(lax.empty((N_DEV,) + packed.shape, F32),
                                                                      packed[None], (device, 0, 0))], _all_plan, N_DEV - 1)
    g0["w_in"] = _in_grad(dproj, saved0, [small_parts.token])
    red0b = _Reduce("reduce0b", [g0["w_in"]])
    done1 = adamw(1, dict(zip(_LARGE, red1.done(red0b.token))), None)
    total = _sum_devices(small_parts.wait(*[done1[n][0] for n in _LARGE])[0])
    r0 = N_META
    small = {
        "meta_tokens": lax.dynamic_slice(total[:N_META], (0, chip * (D_MODEL // N_CHIPS)), (N_META, D_MODEL // N_CHIPS)),
        "mix_pre_g": total[r0:r0 + 2], "mix_post_g": total[r0 + 2:r0 + 4], "mlp_pre_g": total[r0 + 4:r0 + 6],
        "mlp_post_g": total[r0 + 6:r0 + 8],
        "attn_out_g": total[r0 + 8:r0 + 10, :ATTN_W], "conv_out_g": total[r0 + 8:r0 + 10, ATTN_W:],
        "conv_w": lax.dynamic_slice(total[r0 + 10:r0 + 13].reshape(CONV_K, DEPTH, CONV_W).transpose(1, 0, 2),
                                    (0, 0, chip * (CONV_W // N_CHIPS)), (DEPTH, CONV_K, CONV_W // N_CHIPS)),
        "sinks": total[r0 + 13, :DEPTH * N_Q_HEADS].reshape(DEPTH, N_Q_HEADS),
    }
    loss = total[r0 + 14, 0]

    ds, nms, nvs = _adamw_small([w[n] for n in _SMALL], [small[n] for n in _SMALL], [m[n] for n in _SMALL],
                                [v[n] for n in _SMALL])
    red0a.join(ds[0], grad_x)
    red0b.join(red0a.token)
    done0 = adamw(0, dict(zip(first0, red0a.done(red0b.token))), done1)
    done0.update(adamw(0, {"w_in": red0b.done(done0["w_down"][0])[0]}, done1))
    grad, delta, new_m, new_v = {}, {}, {}, {}
    for n in _LARGE:
        grad[n], delta[n], new_m[n], new_v[n] = done0[n]
    for d in (grad, delta, new_m, new_v):
        d["w_in"] = jnp.swapaxes(d["w_in"], 1, 2)
    for i, n in enumerate(_SMALL):
        grad[n], delta[n], new_m[n], new_v[n] = small[n], ds[i], nms[i], nvs[i]
    return (loss, grad_x, *[grad[n] for n in _ORDER], *[delta[n] for n in _ORDER], *[new_m[n] for n in _ORDER],
            *[new_v[n] for n in _ORDER])
```

```python
import jax
import jax.numpy as jnp
from jax import lax
from jax.experimental import pallas as pl
from jax.experimental.pallas import tpu as pltpu

F32 = jnp.float32
BF16 = jnp.bfloat16

D_MODEL = 1024
DEPTH = 2
N_META = 16
ATTN_W = 512
CONV_W = 512
HEAD_DIM = 64
N_Q_HEADS = 8
N_KV_HEADS = 2
GROUP = N_Q_HEADS // N_KV_HEADS
KV_W = N_KV_HEADS * HEAD_DIM
CONV_K = 3
BLOCK = 128
LEAD_PAD = BLOCK - N_META
ROPE_THETA = 500000.0
ROT_DIM = HEAD_DIM // 4
ROT_HALF = ROT_DIM // 2
D_FF = 4 * D_MODEL
IN_W = ATTN_W + 2 * KV_W + 3 * CONV_W
QKV_W = ATTN_W + 2 * KV_W
EPS = 1e-6
SCALE = HEAD_DIM ** -0.5
FF_CHUNK = 1024
N_CHIPS = 4
N_DEV = 8

ADAM_LR = 0.001
ADAM_B1 = 0.9
ADAM_B2 = 0.999
ADAM_EPS = 1e-08
ADAM_WD = 0.01
ADAM_STEP = 10

V7X_VMEM_LIMIT = 60 * 1024 * 1024
SMALL_ROWS = 32

MESH = pl.DeviceIdType.MESH


def _params(*sem):
    return pltpu.CompilerParams(dimension_semantics=sem, vmem_limit_bytes=V7X_VMEM_LIMIT)


def _block_rows(n):
    return max(r for r in range(16, min(n, 128) + 1, 16) if n % r == 0)


def _row_tile(t, most):
    nb = t // BLOCK
    for b in range(most // BLOCK, 0, -1):
        if nb % b == 0:
            return b * BLOCK
    return BLOCK


def _behind(body, deps):
    n = len(deps)

    def wrapped(*refs):
        body(*refs[n:])

    return wrapped, [pl.BlockSpec(memory_space=pl.ANY)] * n


def _rms(x, g):
    r = lax.rsqrt(jnp.mean(x * x, axis=-1, keepdims=True) + EPS)
    return x * r * g


def _rms_bwd(dy, x, g):
    r = lax.rsqrt(jnp.mean(x * x, axis=-1, keepdims=True) + EPS)
    xh = x * r
    dg = jnp.sum(dy * xh, axis=0, keepdims=True)
    dxh = dy * g
    dx = r * (dxh - xh * jnp.mean(dxh * xh, axis=-1, keepdims=True))
    return dx, dg


def _rope(x, cos, sa, sb):
    n = x.shape[-1]
    return x * cos + pltpu.roll(x, n - ROT_HALF, 1) * sa + pltpu.roll(x, ROT_HALF, 1) * sb


def _rope_bwd(dy, cos, sa, sb):
    n = dy.shape[-1]
    return dy * cos + pltpu.roll(dy * sa, ROT_HALF, 1) + pltpu.roll(dy * sb, n - ROT_HALF, 1)


def _rope_tables(t):
    pos = lax.broadcasted_iota(jnp.int32, (t, ROT_HALF), 0).astype(F32) - LEAD_PAD
    pair = lax.broadcasted_iota(jnp.int32, (t, ROT_HALF), 1).astype(F32)
    inv_freq = jnp.power(jnp.float32(ROPE_THETA), -(2.0 * pair) / ROT_DIM)
    ang = pos * inv_freq
    cos, sin = lax.optimization_barrier((jnp.cos(ang), jnp.sin(ang)))
    spread = (1, 2 * HEAD_DIM // ROT_HALF)
    cos, sin = jnp.tile(cos, spread), jnp.tile(sin, spread)
    dim = lax.broadcasted_iota(jnp.int32, (t, 2 * HEAD_DIM), 1) % HEAD_DIM
    return jnp.where(dim < ROT_DIM, cos, 1.0), jnp.where(dim < ROT_DIM, sin, 0.0)


def _rope_factors(cos, sin):
    dim = lax.broadcasted_iota(jnp.int32, sin.shape, 1) % HEAD_DIM
    return cos, jnp.where(dim < ROT_HALF, -sin, 0.0), jnp.where(dim >= ROT_HALF, sin, 0.0)


def _in_proj(h, g, w, tabs, tm, lead=None, deps=()):
    t = h.shape[0] + (0 if lead is None else BLOCK)
    per_step = 0 if lead is None else tm // BLOCK

    def body(*refs):
        if lead is None:
            x = refs[0][...]
            refs = refs[1:]
        else:
            blocks = [r[...] for r in refs[1:1 + per_step]]
            blocks[0] = jnp.where(pl.program_id(0) == 0, refs[0][...], blocks[0])
            x = jnp.concatenate(blocks, axis=0)
            first_out = 1 + per_step + 4
            refs[first_out][...] = x
            refs = refs[1 + per_step:first_out] + refs[first_out + 1:]
        g_ref, w_ref, c_ref, s_ref, a_ref, q_ref, k_ref, v_ref, b_ref, cg_ref, hc_ref = refs
        a = _rms(x, g_ref[...]).astype(BF16)
        a_ref[...] = a
        p = lax.dot_general(a, w_ref[...], (((1,), (1,)), ((), ())), preferred_element_type=F32)
        cos, sa, sb = _rope_factors(c_ref[...], s_ref[...])
        rep = ATTN_W // (2 * HEAD_DIM)
        q = _rope(p[:, :ATTN_W], jnp.tile(cos, (1, rep)), jnp.tile(sa, (1, rep)), jnp.tile(sb, (1, rep)))
        q_ref[...] = (q * SCALE).astype(BF16)
        k_ref[...] = _rope(p[:, ATTN_W:ATTN_W + KV_W], cos, sa, sb).astype(BF16)
        v_ref[...] = p[:, ATTN_W + KV_W:QKV_W].astype(BF16)
        b_ref[...] = p[:, QKV_W:QKV_W + CONV_W].astype(BF16)
        cg_ref[...] = p[:, QKV_W + CONV_W:QKV_W + 2 * CONV_W].astype(BF16)
        hc_ref[...] = p[:, QKV_W + 2 * CONV_W:].astype(BF16)

    row = lambda n: pl.BlockSpec((tm, n), lambda i: (i, 0))
    full = lambda a: pl.BlockSpec(a.shape, lambda i: (0, 0))

    def sequence_block(b):
        return pl.BlockSpec((BLOCK, D_MODEL), lambda i: (jnp.maximum(i * per_step + b - 1, 0), 0))

    if lead is None:
        first_in, first_args, first_out, first_shape = [row(D_MODEL)], [h], [], []
    else:
        first_in = [full(lead)] + [sequence_block(b) for b in range(per_step)]
        first_args = [lead] + [h] * per_step
        first_out, first_shape = [row(D_MODEL)], [jax.ShapeDtypeStruct((t, D_MODEL), F32)]
    body, dep_specs = _behind(body, deps)
    return pl.pallas_call(
        body, name="in_proj", grid=(t // tm,),
        in_specs=dep_specs + first_in + [full(g), full(w), row(2 * HEAD_DIM), row(2 * HEAD_DIM)],
        out_specs=first_out + [row(D_MODEL), row(ATTN_W), row(KV_W), row(KV_W), row(CONV_W), row(CONV_W), row(CONV_W)],
        out_shape=first_shape + [jax.ShapeDtypeStruct((t, D_MODEL), BF16), jax.ShapeDtypeStruct((t, ATTN_W), BF16),
                                 jax.ShapeDtypeStruct((t, KV_W), BF16), jax.ShapeDtypeStruct((t, KV_W), BF16),
                                 jax.ShapeDtypeStruct((t, CONV_W), BF16), jax.ShapeDtypeStruct((t, CONV_W), BF16),
                                 jax.ShapeDtypeStruct((t, CONV_W), BF16)],
        compiler_params=_params("parallel"),
    )(*deps, *first_args, g, w, *tabs)


def _attn_bias():
    r = lax.broadcasted_iota(jnp.int32, (3, BLOCK, 2 * BLOCK), 1)
    c = lax.broadcasted_iota(jnp.int32, (3, BLOCK, 2 * BLOCK), 2)
    i = lax.broadcasted_iota(jnp.int32, (3, BLOCK, 2 * BLOCK), 0)
    ok = (c > r) & (c <= r + BLOCK) & (c + (i - 1) * BLOCK >= LEAD_PAD)
    return jnp.where(ok, 0.0, -jnp.inf).astype(F32)


def _attn_scores(qh, kg, bias):
    return lax.dot_general(qh, kg, (((1,), (1,)), ((), ())), preferred_element_type=F32) + bias


def _attn_probs(s, sk):
    m = jnp.maximum(jnp.max(s, axis=-1, keepdims=True), sk)
    e = jnp.exp(s - m)
    es = jnp.exp(sk - m)
    rden = 1.0 / (jnp.sum(e, axis=-1, keepdims=True) + es)
    return e * rden, es * rden


def _head(hh):
    return slice(hh * HEAD_DIM, (hh + 1) * HEAD_DIM)


def _two_blocks(ref, i):
    prev = jnp.maximum(i - 1, 0)
    return jnp.concatenate([ref[pl.ds(pl.multiple_of(prev * BLOCK, BLOCK), BLOCK), :],
                            ref[pl.ds(pl.multiple_of(i * BLOCK, BLOCK), BLOCK), :]], axis=0)


def _attn_fwd(q, k, v, bias, sinks, tm):
    t = q.shape[0]
    per_step = tm // BLOCK
    heads = range(N_Q_HEADS)

    def body(s_ref, q_ref, k_ref, v_ref, bias_ref, o_ref):
        for b in range(per_step):
            i = pl.program_id(0) * per_step + b
            rows = slice(b * BLOCK, (b + 1) * BLOCK)
            kc, vc = _two_blocks(k_ref, i), _two_blocks(v_ref, i)
            bias_i = bias_ref[jnp.minimum(i, 2)]
            scores = [_attn_scores(q_ref[rows, _head(hh)], kc[:, _head(hh // GROUP)], bias_i) for hh in heads]
            probs = [_attn_probs(scores[hh], s_ref[hh])[0].astype(BF16) for hh in heads]
            for hh in heads:
                o_ref[rows, _head(hh)] = jnp.dot(probs[hh], vc[:, _head(hh // GROUP)],
                                                 preferred_element_type=F32).astype(BF16)

    whole = pl.BlockSpec((t, KV_W), lambda i: (0, 0))
    return pl.pallas_call(
        body, name="attn_fwd", grid=(t // tm,),
        in_specs=[pl.BlockSpec(memory_space=pltpu.SMEM), pl.BlockSpec((tm, ATTN_W), lambda i: (i, 0)), whole, whole,
                  pl.BlockSpec(bias.shape, lambda i: (0, 0, 0))],
        out_specs=pl.BlockSpec((tm, ATTN_W), lambda i: (i, 0)),
        out_shape=jax.ShapeDtypeStruct((t, ATTN_W), BF16),
        compiler_params=_params("parallel"),
    )(sinks, q, k, v, bias)


def _shift_rows(u, halo, n):
    r = pltpu.roll(u, n, 0)
    hr = pltpu.roll(halo, n, 0)
    idx = lax.broadcasted_iota(jnp.int32, hr.shape, 0)
    return jnp.concatenate([jnp.where(idx < n, hr, r[:8]), r[8:]], axis=0)


def _advance_rows(u, halo, n):
    rows = u.shape[0]
    r = pltpu.roll(u, rows - n, 0)
    hr = pltpu.roll(halo, 8 - n, 0)
    idx = lax.broadcasted_iota(jnp.int32, hr.shape, 0)
    return jnp.concatenate([r[:rows - 8], jnp.where(idx >= 8 - n, hr, r[rows - 8:])], axis=0)


def _mix_out(h, o, b, c, hc, cw, ga, gc, w, gp, tm, deps=()):
    t = h.shape[0]

    def body(h_ref, o_ref, b_ref, c_ref, hc_ref, cw_ref, ga_ref, gc_ref, w_ref, gp_ref, h1_ref, y_ref, z_ref, halo):
        @pl.when(pl.program_id(0) == 0)
        def _():
            halo[...] = jnp.zeros_like(halo)

        u = c_ref[...].astype(F32) * hc_ref[...].astype(F32)
        cv = cw_ref[0:1, :] * _shift_rows(u, halo[...], 2) + cw_ref[1:2, :] * _shift_rows(u, halo[...], 1) \
            + cw_ref[2:3, :] * u
        halo[...] = u[tm - 8:]
        yc = b_ref[...].astype(F32) * cv
        y = jnp.concatenate([_rms(o_ref[...].astype(F32), ga_ref[...]), _rms(yc, gc_ref[...])], axis=1).astype(BF16)
        y_ref[...] = y
        z = jnp.dot(y, w_ref[...].reshape(D_MODEL, D_MODEL), preferred_element_type=F32)
        z_ref[...] = z
        h1_ref[...] = h_ref[...] + _rms(z, gp_ref[...])

    row = lambda n: pl.BlockSpec((tm, n), lambda i: (i, 0))
    full = lambda a: pl.BlockSpec(a.shape, lambda i: (0,) * a.ndim)
    body, dep_specs = _behind(body, deps)
    return pl.pallas_call(
        body, name="mix_out", grid=(t // tm,),
        in_specs=dep_specs + [row(D_MODEL), row(ATTN_W), row(CONV_W), row(CONV_W), row(CONV_W), full(cw), full(ga),
                              full(gc), full(w), full(gp)],
        out_specs=[row(D_MODEL), row(D_MODEL), row(D_MODEL)],
        out_shape=[jax.ShapeDtypeStruct((t, D_MODEL), F32), jax.ShapeDtypeStruct((t, D_MODEL), BF16),
                   jax.ShapeDtypeStruct((t, D_MODEL), F32)],
        scratch_shapes=[pltpu.VMEM((8, CONV_W), F32)],
        compiler_params=_params("arbitrary"),
    )(*deps, h, o, b, c, hc, cw, ga, gc, w, gp)


def _mlp(h1, g1, wu, wd, g2, tm, target=None):
    t = h1.shape[0]
    nj = D_FF // FF_CHUNK
    per_step = tm // BLOCK if target is not None else 0

    def body(h1_ref, g1_ref, wu_ref, wd_ref, g2_ref, *rest):
        t_refs, outs = rest[:per_step], rest[per_step:]
        a2_ref, slope_ref = (outs[1], outs[2]) if target is None else (outs[2], outs[3])
        a2 = _rms(h1_ref[...], g1_ref[...]).astype(BF16)
        a2_ref[...] = a2
        f = None
        for j in range(nj):
            up = jnp.dot(a2, wu_ref[j], preferred_element_type=F32)
            r = jnp.maximum(up, 0.0)
            slope_ref[:, j * FF_CHUNK:(j + 1) * FF_CHUNK] = (r + r).astype(BF16)
            part = jnp.dot((r * r).astype(BF16), wd_ref[j], preferred_element_type=F32)
            f = part if f is None else f + part
        h2 = h1_ref[...] + _rms(f, g2_ref[...])
        if target is None:
            outs[0][...] = h2
            outs[3][...] = f
            return
        loss_ref, dh_ref, df_ref, dg2_ref = outs[0], outs[1], outs[4], outs[5]
        i = pl.program_id(0)

        @pl.when(i == 0)
        def _():
            loss_ref[...] = jnp.zeros_like(loss_ref)
            dg2_ref[...] = jnp.zeros_like(dg2_ref)

        total = jnp.zeros((), F32)
        dh2 = []
        for b in range(per_step):
            err = h2[b * BLOCK:(b + 1) * BLOCK] - t_refs[b][...]
            if b == 0:
                err = jnp.where(i == 0, 0.0, err)
            dh2.append(err * (1.0 / D_MODEL))
            total = total + jnp.sum(err * err)
        loss_ref[...] += total * (0.5 / D_MODEL)
        dh2 = jnp.concatenate(dh2, axis=0)
        dh_ref[...] = dh2
        df, dg = _rms_bwd(dh2, f, g2_ref[...])
        df_ref[...] = df.astype(BF16)
        dg2_ref[...] += dg

    def target_block(b):
        return pl.BlockSpec((BLOCK, D_MODEL), lambda i: (jnp.maximum(i * per_step + b - 1, 0), 0))

    row = pl.BlockSpec((tm, D_MODEL), lambda i: (i, 0))
    vec = pl.BlockSpec((1, D_MODEL), lambda i: (0, 0))
    resident = pl.BlockSpec(memory_space=pltpu.VMEM)
    wide = pl.BlockSpec((tm, D_FF), lambda i: (i, 0))
    kept = [jax.ShapeDtypeStruct((t, D_MODEL), BF16), jax.ShapeDtypeStruct((t, D_FF), BF16)]
    if target is None:
        specs = [row, row, wide, row]
        shapes = [jax.ShapeDtypeStruct((t, D_MODEL), F32)] + kept + [jax.ShapeDtypeStruct((t, D_MODEL), F32)]
    else:
        specs = [pl.BlockSpec((8, 128), lambda i: (0, 0)), row, row, wide, row, vec]
        shapes = [jax.ShapeDtypeStruct((8, 128), F32), jax.ShapeDtypeStruct((t, D_MODEL), F32)] + kept \
            + [jax.ShapeDtypeStruct((t, D_MODEL), BF16), jax.ShapeDtypeStruct((1, D_MODEL), F32)]
    outs = pl.pallas_call(
        body, name="mlp", grid=(t // tm,),
        in_specs=[row, vec, resident, resident, vec] + [target_block(b) for b in range(per_step)],
        out_specs=specs, out_shape=shapes,
        compiler_params=_params("parallel" if target is None else "arbitrary"),
    )(h1, g1, wu, wd, g2, *([target] * per_step))
    if target is None:
        return tuple(outs)
    return (tuple(outs[:2]), outs[2], outs[3], tuple(outs[4:]))


def _mlp_bwd_hidden(dh2, f, g2, slope, wd, tm, deps=(), df=None):
    t = slope.shape[0]
    nj = D_FF // FF_CHUNK

    def hidden(df, slope_ref, wd_ref, dup_ref):
        for j in range(nj):
            cols = slice(j * FF_CHUNK, (j + 1) * FF_CHUNK)
            dact = lax.dot_general(df, wd_ref[j], (((1,), (1,)), ((), ())), preferred_element_type=F32)
            dup_ref[:, cols] = (dact * slope_ref[:, cols].astype(F32)).astype(BF16)

    def body(dh2_ref, f_ref, g2_ref, slope_ref, wd_ref, df_ref, dup_ref, dg2_ref):
        @pl.when(pl.program_id(0) == 0)
        def _():
            dg2_ref[...] = jnp.zeros_like(dg2_ref)

        df, dg = _rms_bwd(dh2_ref[...], f_ref[...], g2_ref[...])
        dg2_ref[...] += dg
        df = df.astype(BF16)
        df_ref[...] = df
        hidden(df, slope_ref, wd_ref, dup_ref)

    def body_from_df(df_ref, slope_ref, wd_ref, dup_ref):
        hidden(df_ref[...], slope_ref, wd_ref, dup_ref)

    row = pl.BlockSpec((tm, D_MODEL), lambda i: (i, 0))
    wide = pl.BlockSpec((tm, D_FF), lambda i: (i, 0))
    vec = pl.BlockSpec((1, D_MODEL), lambda i: (0, 0))
    resident = pl.BlockSpec(memory_space=pltpu.VMEM)
    if df is not None:
        body_from_df, dep_specs = _behind(body_from_df, deps)
        return pl.pallas_call(
            body_from_df, name="mlp_bwd_hidden", grid=(t // tm,), in_specs=dep_specs + [row, wide, resident],
            out_specs=wide, out_shape=jax.ShapeDtypeStruct((t, D_FF), BF16), compiler_params=_params("parallel"),
        )(*deps, df, slope, wd)
    body, dep_specs = _behind(body, deps)
    return pl.pallas_call(
        body, name="mlp_bwd_hidden", grid=(t // tm,),
        in_specs=dep_specs + [row, row, vec, wide, resident],
        out_specs=[row, wide, vec],
        out_shape=[jax.ShapeDtypeStruct((t, D_MODEL), BF16), jax.ShapeDtypeStruct((t, D_FF), BF16),
                   jax.ShapeDtypeStruct((1, D_MODEL), F32)],
        compiler_params=_params("arbitrary"),
    )(*deps, dh2, f, g2, slope, wd)


def _mlp_bwd_input(dup, wu, h1, g1, dh2, tm):
    t = dh2.shape[0]
    nj = D_FF // FF_CHUNK

    def body(dup_ref, wu_ref, h1_ref, g1_ref, dh2_ref, dh1_ref, dg1_ref):
        @pl.when(pl.program_id(0) == 0)
        def _():
            dg1_ref[...] = jnp.zeros_like(dg1_ref)

        da2 = None
        for j in range(nj):
            part = lax.dot_general(dup_ref[:, j * FF_CHUNK:(j + 1) * FF_CHUNK], wu_ref[j], (((1,), (1,)), ((), ())),
                                   preferred_element_type=F32)
            da2 = part if da2 is None else da2 + part
        dx, dg = _rms_bwd(da2, h1_ref[...], g1_ref[...])
        dh1_ref[...] = dh2_ref[...] + dx
        dg1_ref[...] += dg

    row = pl.BlockSpec((tm, D_MODEL), lambda i: (i, 0))
    vec = pl.BlockSpec((1, D_MODEL), lambda i: (0, 0))
    return pl.pallas_call(
        body, name="mlp_bwd_input", grid=(t // tm,),
        in_specs=[pl.BlockSpec((tm, D_FF), lambda i: (i, 0)), pl.BlockSpec(memory_space=pltpu.VMEM), row, vec, row],
        out_specs=[row, vec],
        out_shape=[jax.ShapeDtypeStruct((t, D_MODEL), F32), jax.ShapeDtypeStruct((1, D_MODEL), F32)],
        compiler_params=_params("arbitrary"),
    )(dup, wu, h1, g1, dh2)


def _row_split(t):
    tile = min(t, 1024)
    return tile, t // tile, t % tile


def _row_split_specs(t, cols):
    tile, whole, rest = _row_split(t)
    specs = [pl.BlockSpec((tile, cols), lambda r: (jnp.minimum(r, whole - 1), 0))]
    if rest:
        specs.append(pl.BlockSpec((rest, cols), lambda r: (whole * tile // rest, 0)))
    return specs


def _weight_grad(x, y, name, x_is_slope=False, deps=()):
    t, k = x.shape
    n = y.shape[1]
    tn = FF_CHUNK
    tk = FF_CHUNK if k % FF_CHUNK == 0 else k
    _, whole, rest = _row_split(t)
    steps = whole + bool(rest)

    def body(*refs):
        o_ref, ob_ref, r = refs[-2], refs[-1], pl.program_id(0)

        @pl.when(r == 0)
        def _():
            o_ref[...] = jnp.zeros_like(o_ref)

        def add(x_ref, y_ref):
            for a in range(k // tk):
                xv = x_ref[:, a * tk:(a + 1) * tk]
                if x_is_slope:
                    xv = xv.astype(F32)
                    xv = (xv * xv * 0.25).astype(BF16)
                for b in range(n // tn):
                    o_ref[a, b] += lax.dot_general(xv, y_ref[:, b * tn:(b + 1) * tn], (((0,), (0,)), ((), ())),
                                                   preferred_element_type=F32)

        if rest:
            pl.when(r < whole)(lambda: add(refs[0], refs[2]))
            pl.when(r == whole)(lambda: add(refs[1], refs[3]))
        else:
            add(refs[0], refs[1])

        @pl.when(r == steps - 1)
        def _():
            ob_ref[...] = o_ref[...].astype(BF16)

    vm = pl.BlockSpec(memory_space=pltpu.VMEM)
    body, dep_specs = _behind(body, deps)
    return pl.pallas_call(
        body, name=name, grid=(steps,),
        in_specs=dep_specs + _row_split_specs(t, k) + _row_split_specs(t, n), out_specs=[vm, vm],
        out_shape=[jax.ShapeDtypeStruct((k // tk, n // tn, tk, tn), F32),
                   jax.ShapeDtypeStruct((k // tk, n // tn, tk, tn), BF16)],
        compiler_params=_params("arbitrary"),
    )(*deps, *([x] * (1 + bool(rest))), *([y] * (1 + bool(rest))))


def _mix_out_bwd(dh1, z, gp, w, o, b, c, hc, cw, ga, gc, tm, deps=()):
    t = dh1.shape[0]
    nt = t // tm
    per16 = tm // 16

    def body(dh1_ref, z_ref, gp_ref, w_ref, o_ref, b_ref, c_ref, hc_ref, cp_ref, hp_ref, cw_ref, ga_ref, gc_ref,
             dz_ref, do_ref, dbch_ref, dgp_ref, dga_ref, dgc_ref, dcw_ref, halo):
        i = pl.program_id(0)

        @pl.when(i == 0)
        def _():
            halo[...] = jnp.zeros_like(halo)
            dgp_ref[...] = jnp.zeros_like(dgp_ref)
            dga_ref[...] = jnp.zeros_like(dga_ref)
            dgc_ref[...] = jnp.zeros_like(dgc_ref)
            dcw_ref[...] = jnp.zeros_like(dcw_ref)

        dz, dgp = _rms_bwd(dh1_ref[...], z_ref[...], gp_ref[...])
        dgp_ref[...] += dgp
        dz = dz.astype(BF16)
        dz_ref[...] = dz
        dy = lax.dot_general(dz, w_ref[...].reshape(D_MODEL, D_MODEL), (((1,), (1,)), ((), ())),
                             preferred_element_type=F32)
        do, dga = _rms_bwd(dy[:, :ATTN_W], o_ref[...].astype(F32), ga_ref[...])
        do_ref[...] = do.astype(BF16)
        dga_ref[...] += dga

        cc, hh = c_ref[...].astype(F32), hc_ref[...].astype(F32)
        u = cc * hh
        first = i == nt - 1
        u_before = jnp.where(first, 0.0, (cp_ref[...].astype(F32) * hp_ref[...].astype(F32))[8:])
        u1 = _shift_rows(u, u_before, 1)
        u2 = _shift_rows(u, u_before, 2)
        cv = cw_ref[0:1, :] * u2 + cw_ref[1:2, :] * u1 + cw_ref[2:3, :] * u
        bb = b_ref[...].astype(F32)
        dyc, dgc = _rms_bwd(dy[:, ATTN_W:], bb * cv, gc_ref[...])
        dgc_ref[...] += dgc
        dcv = dyc * bb
        d1 = _advance_rows(dcv, halo[...], 1)
        d2 = _advance_rows(dcv, halo[...], 2)
        halo[...] = dcv[:8]
        du = cw_ref[2:3, :] * dcv + cw_ref[1:2, :] * d1 + cw_ref[0:1, :] * d2
        dbch_ref[...] = jnp.concatenate([dyc * cv, du * hh, du * cc], axis=1).astype(BF16)
        dcw_ref[...] += jnp.concatenate([jnp.sum(dcv * u2, axis=0, keepdims=True),
                                         jnp.sum(dcv * u1, axis=0, keepdims=True),
                                         jnp.sum(dcv * u, axis=0, keepdims=True)], axis=0)

    row = lambda n: pl.BlockSpec((tm, n), lambda i: (nt - 1 - i, 0))
    before = pl.BlockSpec((16, CONV_W), lambda i: (jnp.maximum((nt - 1 - i) * per16 - 1, 0), 0))
    full = lambda a: pl.BlockSpec(a.shape, lambda i: (0,) * a.ndim)
    vec = lambda n: pl.BlockSpec((1, n), lambda i: (0, 0))
    body, dep_specs = _behind(body, deps)
    return pl.pallas_call(
        body, name="mix_out_bwd", grid=(nt,),
        in_specs=dep_specs + [row(D_MODEL), row(D_MODEL), full(gp), full(w), row(ATTN_W), row(CONV_W), row(CONV_W),
                              row(CONV_W), before, before, full(cw), full(ga), full(gc)],
        out_specs=[row(D_MODEL), row(ATTN_W), row(3 * CONV_W), vec(D_MODEL), vec(ATTN_W), vec(CONV_W),
                   pl.BlockSpec((CONV_K, CONV_W), lambda i: (0, 0))],
        out_shape=[jax.ShapeDtypeStruct((t, D_MODEL), BF16), jax.ShapeDtypeStruct((t, ATTN_W), BF16),
                   jax.ShapeDtypeStruct((t, 3 * CONV_W), BF16), jax.ShapeDtypeStruct((1, D_MODEL), F32),
                   jax.ShapeDtypeStruct((1, ATTN_W), F32), jax.ShapeDtypeStruct((1, CONV_W), F32),
                   jax.ShapeDtypeStruct((CONV_K, CONV_W), F32)],
        scratch_shapes=[pltpu.VMEM((8, CONV_W), F32)],
        compiler_params=_params("arbitrary"),
    )(*deps, dh1, z, gp, w, o, b, c, hc, c, hc, cw, ga, gc)


def _attn_bwd(q, k, v, o, do, bias, sinks, tm, deps=()):
    t = q.shape[0]
    per_step = tm // BLOCK

    def body(s_ref, q_ref, k_ref, v_ref, o_ref, do_ref, bias_ref, dq_ref, dk_ref, dv_ref, ds_ref):
        step = pl.program_id(0)

        @pl.when(step == 0)
        def _():
            ds_ref[...] = jnp.zeros_like(ds_ref)

        heads = range(N_Q_HEADS)

        def first_matmuls(b):
            i = step * per_step + b
            rows = slice(b * BLOCK, (b + 1) * BLOCK)
            kc, vc = _two_blocks(k_ref, i), _two_blocks(v_ref, i)
            bias_i = bias_ref[jnp.minimum(i, 2)]
            kgs = [kc[:, _head(g)] for g in range(N_KV_HEADS)]
            vgs = [vc[:, _head(g)] for g in range(N_KV_HEADS)]
            qs = [q_ref[rows, _head(hh)] for hh in heads]
            dosb = [do_ref[rows, _head(hh)] for hh in heads]
            dos = [d.astype(F32) for d in dosb]
            scores = [_attn_scores(qs[hh], kgs[hh // GROUP], bias_i) for hh in heads]
            dps = [lax.dot_general(dosb[hh], vgs[hh // GROUP], (((1,), (1,)), ((), ())), preferred_element_type=F32)
                   for hh in heads]
            return kgs, qs, dos, dosb, scores, dps

        dsink = [jnp.zeros((BLOCK, 1), F32) for _ in range(N_Q_HEADS)]
        ahead = None
        for b in range(per_step):
            i = step * per_step + b
            rows = slice(b * BLOCK, (b + 1) * BLOCK)
            kgs, qs, dos, dosb, scores, dps = first_matmuls(b)
            ps, dss = [], []
            for hh in heads:
                p, share = _attn_probs(scores[hh], s_ref[hh])
                drow = jnp.sum(dos[hh] * o_ref[rows, _head(hh)].astype(F32), axis=-1, keepdims=True)
                dss.append((p * (dps[hh] - drow)).astype(BF16))
                ps.append(p.astype(BF16))
                dsink[hh] = dsink[hh] + share * drow
            for hh in heads:
                dq_ref[rows, _head(hh)] = (jnp.dot(dss[hh], kgs[hh // GROUP], preferred_element_type=F32)
                                           * SCALE).astype(BF16)
            groups = [slice(GROUP * g, GROUP * (g + 1)) for g in range(N_KV_HEADS)]
            dkg = [lax.dot_general(jnp.concatenate(dss[gr], axis=0), jnp.concatenate(qs[gr], axis=0),
                                   (((0,), (0,)), ((), ())), preferred_element_type=F32) for gr in groups]
            dvg = [lax.dot_general(jnp.concatenate(ps[gr], axis=0), jnp.concatenate(dosb[gr], axis=0),
                                   (((0,), (0,)), ((), ())), preferred_element_type=F32) for gr in groups]
            dkb, dvb = jnp.concatenate(dkg, axis=1), jnp.concatenate(dvg, axis=1)
            if b == 0:
                @pl.when(step > 0)
                def _():
                    before = pl.ds(pl.multiple_of((i - 1) * BLOCK, BLOCK), BLOCK)
                    dk_ref[before, :] += dkb[:BLOCK]
                    dv_ref[before, :] += dvb[:BLOCK]
            else:
                at = pl.ds(pl.multiple_of((i - 1) * BLOCK, BLOCK), BLOCK)
                dk_ref[at, :] = ahead[0] + dkb[:BLOCK]
                dv_ref[at, :] = ahead[1] + dvb[:BLOCK]
            ahead = (dkb[BLOCK:], dvb[BLOCK:])
        last = pl.ds(pl.multiple_of(((step + 1) * per_step - 1) * BLOCK, BLOCK), BLOCK)
        dk_ref[last, :] = ahead[0]
        dv_ref[last, :] = ahead[1]
        for hh in range(N_Q_HEADS):
            ds_ref[hh:hh + 1, :] -= jnp.sum(dsink[hh])

    whole = pl.BlockSpec((t, KV_W), lambda i: (0, 0))
    blk = pl.BlockSpec((tm, ATTN_W), lambda i: (i, 0))
    body, dep_specs = _behind(body, deps)
    return pl.pallas_call(
        body, name="attn_bwd", grid=(t // tm,),
        in_specs=dep_specs + [pl.BlockSpec(memory_space=pltpu.SMEM), blk, whole, whole, blk, blk,
                              pl.BlockSpec(bias.shape, lambda i: (0, 0, 0))],
        out_specs=[blk, whole, whole, pl.BlockSpec((N_Q_HEADS, 128), lambda i: (0, 0))],
        out_shape=[jax.ShapeDtypeStruct((t, ATTN_W), BF16), jax.ShapeDtypeStruct((t, KV_W), F32),
                   jax.ShapeDtypeStruct((t, KV_W), F32), jax.ShapeDtypeStruct((N_Q_HEADS, 128), F32)],
        compiler_params=_params("arbitrary"),
    )(*deps, sinks, q, k, v, o, do, bias)


def _in_proj_bwd(dq, dk, dv, dbch, w, dh1, h, g, tabs, tm, split_lead=False):
    t = h.shape[0]
    nt = t // tm

    def body(dq_ref, dk_ref, dv_ref, dbch_ref, w_ref, dh1_ref, h_ref, g_ref, c_ref, s_ref, *rest):
        dp_ref, dg_ref = rest[2:4] if split_lead else rest[1:3]
        i = pl.program_id(0)

        @pl.when(i == 0)
        def _():
            dg_ref[...] = jnp.zeros_like(dg_ref)

        cos, sa, sb = _rope_factors(c_ref[...], s_ref[...])
        rep = ATTN_W // (2 * HEAD_DIM)
        dqr = _rope_bwd(dq_ref[...].astype(F32), jnp.tile(cos, (1, rep)), jnp.tile(sa, (1, rep)),
                        jnp.tile(sb, (1, rep)))
        dkr = _rope_bwd(dk_ref[...], cos, sa, sb)
        dp = jnp.concatenate([dqr.astype(BF16), dkr.astype(BF16), dv_ref[...].astype(BF16), dbch_ref[...]], axis=1)
        dp_ref[...] = dp
        da = jnp.dot(dp, w_ref[...], preferred_element_type=F32)
        dx, dg = _rms_bwd(da, h_ref[...], g_ref[...])
        dg_ref[...] += dg
        dh = dh1_ref[...] + dx
        if not split_lead:
            rest[0][...] = dh
            return
        lead_ref, seq_ref, stage, sems = rest[0], rest[1], rest[4], rest[5]

        def copy(j, slot, first):
            if first:
                return pltpu.make_async_copy(stage.at[slot, pl.ds(BLOCK, tm - BLOCK)],
                                             seq_ref.at[pl.ds(0, tm - BLOCK)], sems.at[slot])
            return pltpu.make_async_copy(stage.at[slot], seq_ref.at[pl.ds(pl.multiple_of(j * tm - BLOCK, BLOCK), tm)],
                                         sems.at[slot])

        slot = i % 2
        pl.when(i == 2)(lambda: copy(0, slot, True).wait())
        pl.when(i > 2)(lambda: copy(i - 2, slot, False).wait())
        stage[slot] = dh

        @pl.when(i == 0)
        def _():
            lead_ref[...] = dh[:BLOCK]
            copy(0, slot, True).start()

        pl.when(i > 0)(lambda: copy(i, slot, False).start())

        @pl.when(i == nt - 1)
        def _():
            for j in range(max(nt - 2, 0), nt):
                copy(j, j % 2, j == 0).wait()

    row = lambda n: pl.BlockSpec((tm, n), lambda i: (i, 0))
    full = lambda a: pl.BlockSpec(a.shape, lambda i: (0, 0))
    dh_specs, dh_shapes, scratch = [row(D_MODEL)], [jax.ShapeDtypeStruct((t, D_MODEL), F32)], []
    if split_lead:
        dh_specs = [pl.BlockSpec((BLOCK, D_MODEL), lambda i: (0, 0)), pl.BlockSpec(memory_space=pl.ANY)]
        dh_shapes = [jax.ShapeDtypeStruct((BLOCK, D_MODEL), F32), jax.ShapeDtypeStruct((t - BLOCK, D_MODEL), F32)]
        scratch = [pltpu.VMEM((2, tm, D_MODEL), F32), pltpu.SemaphoreType.DMA((2,))]
    outs = pl.pallas_call(
        body, name="in_proj_bwd", grid=(nt,),
        in_specs=[row(ATTN_W), row(KV_W), row(KV_W), row(3 * CONV_W), full(w), row(D_MODEL), row(D_MODEL), full(g),
                  row(2 * HEAD_DIM), row(2 * HEAD_DIM)],
        out_specs=dh_specs + [row(IN_W), pl.BlockSpec((1, D_MODEL), lambda i: (0, 0))],
        out_shape=dh_shapes + [jax.ShapeDtypeStruct((t, IN_W), BF16), jax.ShapeDtypeStruct((1, D_MODEL), F32)],
        scratch_shapes=scratch,
        compiler_params=_params("arbitrary"),
    )(dq, dk, dv, dbch, w, dh1, h, g, *tabs)
    return (tuple(outs[:2]) if split_lead else outs[0],) + tuple(outs[-2:])


class _Tiles:
    def __init__(self, t):
        self.tm = _row_tile(t, 640)
        self.ts = self.tm
        self.tabs = _rope_tables(t)
        self.bias = _attn_bias()


def _mixer_fwd(h, p, tl, lead=None, deps=()):
    if lead is None:
        a, q, k, v, b, c, hc = _in_proj(h, p["mix_pre_g"], p["w_in"], tl.tabs, tl.ts, deps=deps)
    else:
        h, a, q, k, v, b, c, hc = _in_proj(h, p["mix_pre_g"], p["w_in"], tl.tabs, tl.ts, lead, deps)
    o = _attn_fwd(q, k, v, tl.bias, p["sinks"], tl.tm)
    return (h, a, q, k, v, b, c, hc, o)


def _out_fwd(mixed, p, tl, deps=()):
    h, a, q, k, v, b, c, hc, o = mixed
    h1, y, z = _mix_out(h, o, b, c, hc, p["conv_w"], p["attn_out_g"], p["conv_out_g"], p["w_out"], p["mix_post_g"],
                        tl.ts, deps)
    return h1, mixed + (h1, y, z)


def _mlp_fwd(h1, saved, p, tl, target=None):
    h2, a2, slope, f = _mlp(h1, p["mlp_pre_g"], p["w_up"], p["w_down"], p["mlp_post_g"], tl.tm, target)
    return h2, saved + (a2, slope, f)


def _mlp_part_bwd(dh, saved, p, tl, deps=()):
    h1, a2, slope, f = saved[9], saved[12], saved[13], saved[14]
    if isinstance(f, tuple):
        df, dg2 = f
        dup = _mlp_bwd_hidden(None, None, None, slope, p["w_down"], tl.tm, deps, df)
    else:
        df, dup, dg2 = _mlp_bwd_hidden(dh, f, p["mlp_post_g"], slope, p["w_down"], tl.tm, deps)
    dh1, dg1 = _mlp_bwd_input(dup, p["w_up"], h1, p["mlp_pre_g"], dh, tl.tm)
    g = {"w_down": [d.reshape(N_CHIPS, FF_CHUNK, D_MODEL)
                    for d in _weight_grad(slope, df, "grad_w_down", x_is_slope=True)],
         "w_up": [d.reshape(N_CHIPS, D_MODEL, FF_CHUNK) for d in _weight_grad(a2, dup, "grad_w_up")],
         "mlp_post_g": dg2, "mlp_pre_g": dg1}
    return dh1, g


def _mix_out_part_bwd(dh1, saved, p, tl, deps=()):
    b, c, hc, o, y, z = saved[5], saved[6], saved[7], saved[8], saved[10], saved[11]
    dz, do, dbch, dgp, dga, dgc, dcw = _mix_out_bwd(dh1, z, p["mix_post_g"], p["w_out"], o, b, c, hc, p["conv_w"],
                                                    p["attn_out_g"], p["conv_out_g"], tl.ts, deps)
    g = {"w_out": [d.reshape(N_CHIPS, D_MODEL // N_CHIPS, D_MODEL) for d in _weight_grad(y, dz, "grad_w_out")],
         "mix_post_g": dgp, "attn_out_g": dga, "conv_out_g": dgc, "conv_w": dcw}
    return (dh1, do, dbch), g


def _attn_in_part_bwd(carry, saved, p, tl, deps=(), split_lead=False):
    dh1, do, dbch = carry
    h_in, q, k, v, o = saved[0], saved[2], saved[3], saved[4], saved[8]
    dq, dk, dv, dsink = _attn_bwd(q, k, v, o, do, tl.bias, p["sinks"], tl.tm, deps)
    dh, dproj, dgi = _in_proj_bwd(dq, dk, dv, dbch, p["w_in"], dh1, h_in, p["mix_pre_g"], tl.tabs, tl.ts, split_lead)
    return dh, dproj, {"mix_pre_g": dgi, "sinks": dsink[:, 0]}


def _in_grad(dproj, saved, deps=()):
    return [d.reshape(N_CHIPS, IN_W // N_CHIPS, D_MODEL) for d in _weight_grad(dproj, saved[1], "grad_w_in", deps=deps)]


def _place():
    return lax.axis_index("x"), lax.axis_index("y"), lax.axis_index("c")


def _other_chips(x, y):
    return [(1 - x, y), (x, 1 - y), (1 - x, 1 - y)]


_HBM = pl.BlockSpec(memory_space=pltpu.HBM)
_SEM = pl.BlockSpec(memory_space=pltpu.SEMAPHORE)
_EFFECT = pltpu.SideEffectType.DATAFLOW_SIDE_EFFECTING


class _Exchange:
    def __init__(self, name, bufs, plan, n, after=()):
        self.name, self.plan, nb = name, plan, len(bufs)
        n_in = nb + len(after)

        def body(*refs):
            send, recv, token = refs[n_in], refs[n_in + 1], refs[-1]
            for k, (src, dst, target, _) in enumerate(plan(refs[:nb])):
                pltpu.make_async_remote_copy(src_ref=src, dst_ref=dst, send_sem=send.at[k], recv_sem=recv.at[k],
                                             device_id=target, device_id_type=MESH).start()
            token[...] = jnp.zeros_like(token)

        outs = pl.pallas_call(
            body, name=name + "_start",
            out_shape=(pltpu.SemaphoreType.DMA((n,)), pltpu.SemaphoreType.DMA((n,)),
                       *[pltpu.HBM(b.shape, b.dtype) for b in bufs], jax.ShapeDtypeStruct((8, 128), F32)),
            in_specs=[_HBM] * nb + [pl.BlockSpec(memory_space=pl.ANY)] * len(after),
            out_specs=(_SEM, _SEM, *[_HBM] * nb, pl.BlockSpec(memory_space=pltpu.VMEM)),
            input_output_aliases={i: 2 + i for i in range(nb)},
            compiler_params=pltpu.CompilerParams(has_side_effects=_EFFECT),
        )(*[pltpu.with_memory_space_constraint(b, pltpu.HBM) for b in bufs], *after)
        self.send, self.recv, self.bufs, self.token = outs[0], outs[1], list(outs[2:2 + nb]), outs[-1]

    def wait(self, *after):
        plan, nb = self.plan, len(self.bufs)

        def body(*refs):
            send, recv = refs[nb], refs[nb + 1]
            for k, (src, _, target, land) in enumerate(plan(refs[:nb])):
                cp = pltpu.make_async_remote_copy(src_ref=src, dst_ref=land, send_sem=send.at[k], recv_sem=recv.at[k],
                                                  device_id=target, device_id_type=MESH)
                cp.wait_send()
                cp.wait_recv()

        outs = pl.pallas_call(
            body, name=self.name + "_wait", out_shape=[pltpu.HBM(b.shape, b.dtype) for b in self.bufs],
            in_specs=[_HBM] * nb + [_SEM, _SEM] + [pl.BlockSpec(memory_space=pl.ANY)] * len(after),
            out_specs=[_HBM] * nb, input_output_aliases={i: i for i in range(nb)},
            compiler_params=pltpu.CompilerParams(has_side_effects=_EFFECT),
        )(*self.bufs, self.send, self.recv, *after)
        return list(outs)


def _gather_plan(n):
    def plan(refs):
        x, y, c = _place()
        me = 2 * x + y
        return [(refs[a].at[me], refs[a].at[me], (px, py, c), refs[a].at[2 * px + py])
                for a in range(n) for px, py in _other_chips(x, y)]

    return plan


def _gather_half_plan(n, half_rows):
    def plan(refs):
        x, y, c = _place()
        me = 2 * x + y
        out = []
        for a in range(n):
            rows = pl.ds(c * half_rows[a], half_rows[a])
            out += [(refs[a].at[me, rows], refs[a].at[me, rows], (px, py, c), refs[a].at[2 * px + py, rows])
                    for px, py in _other_chips(x, y)]
        return out

    return plan


def _hand_over_plan(n, half_rows):
    def plan(refs):
        x, y, c = _place()
        out = []
        for a in range(n):
            mine, theirs = pl.ds(c * half_rows[a], half_rows[a]), pl.ds((1 - c) * half_rows[a], half_rows[a])
            for px, py in _other_chips(x, y):
                held = refs[a].at[2 * px + py, mine]
                out.append((held, held, (x, y, 1 - c), refs[a].at[2 * px + py, theirs]))
        return out

    return plan


def _peers():
    x, y, c = _place()
    return [(k - 1, (x ^ (k >> 2), y ^ ((k >> 1) & 1), c ^ (k & 1))) for k in range(1, N_DEV)]


def _scatter_plan(n, half_rows):
    def plan(refs):
        out = []
        for a in range(n):
            hr = half_rows[a]
            for k, (px, py, pc) in _peers():
                out.append((refs[a].at[2 * px + py, pl.ds(pc * hr, hr)], refs[n + a].at[k], (px, py, pc),
                            refs[n + a].at[k]))
        return out

    return plan


def _join_plan(n):
    def plan(refs):
        x, y, c = _place()
        return [(refs[a].at[c], refs[a].at[c], (x, y, 1 - c), refs[a].at[1 - c]) for a in range(n)]

    return plan


def _sum_parts(gs, qs):
    n = len(gs)
    half_rows = [g.shape[1] // 2 for g in gs]
    tr = [_block_rows(hr) for hr in half_rows]
    per = [hr // t for hr, t in zip(half_rows, tr)]
    x, y, c = _place()
    where = jnp.stack([2 * x + y, c]).astype(jnp.int32)

    def body(where_ref, *refs):
        i = pl.program_id(0)
        for a in range(n):
            g_ref, q_ref, o_ref = refs[a], refs[n + a], refs[2 * n + a]

            @pl.when(i < per[a])
            def _():
                total = g_ref[...]
                for k in range(N_DEV - 1):
                    total = total + q_ref[k].astype(F32)
                o_ref[...] = total

    def at(a, i):
        return jnp.minimum(i, per[a] - 1)

    specs_g = [pl.BlockSpec((None, tr[a], gs[a].shape[2]),
                            lambda i, where_ref, a=a: (where_ref[0], where_ref[1] * per[a] + at(a, i), 0)) for a in range(n)]
    specs_q = [pl.BlockSpec((N_DEV - 1, tr[a], gs[a].shape[2]), lambda i, where_ref, a=a: (0, at(a, i), 0))
               for a in range(n)]
    specs_o = [pl.BlockSpec((None, tr[a], gs[a].shape[2]), lambda i, where_ref, a=a: (where_ref[1], at(a, i), 0))
               for a in range(n)]
    return pl.pallas_call(
        body, name="sum_parts",
        grid_spec=pltpu.PrefetchScalarGridSpec(num_scalar_prefetch=1, grid=(max(per),), in_specs=specs_g + specs_q,
                                               out_specs=specs_o),
        out_shape=[jax.ShapeDtypeStruct((2, hr, g.shape[2]), F32) for g, hr in zip(gs, half_rows)],
        compiler_params=_params("arbitrary"),
    )(where, *gs, *qs)


def _all_plan(refs):
    x, y, c = _place()
    mine = refs[0].at[4 * x + 2 * y + c]
    return [(mine, mine, (px, py, pc), refs[0].at[4 * px + 2 * py + pc]) for _, (px, py, pc) in _peers()]


def _sum_devices(parts):
    def body(p_ref, o_ref):
        total = p_ref[0]
        for d in range(1, N_DEV):
            total = total + p_ref[d]
        o_ref[...] = total

    vm = pl.BlockSpec(memory_space=pltpu.VMEM)
    return pl.pallas_call(body, name="sum_devices", in_specs=[vm], out_specs=vm,
                          out_shape=jax.ShapeDtypeStruct(parts.shape[1:], F32))(parts)


def _adamw_math(w, g, m, v):
    m = ADAM_B1 * m + (1.0 - ADAM_B1) * g
    v = ADAM_B2 * v + (1.0 - ADAM_B2) * jnp.square(g)
    m_hat = m / (1.0 - ADAM_B1 ** ADAM_STEP)
    v_hat = v / (1.0 - ADAM_B2 ** ADAM_STEP)
    delta = -ADAM_LR * (m_hat / (jnp.sqrt(v_hat) + ADAM_EPS) + ADAM_WD * w)
    return delta, m, v


def _adamw_large(layer, ws, halves, ms, vs, others):
    n = len(ws)
    tr = [_block_rows(w.shape[1] // 2) for w in ws]
    per = [w.shape[1] // 2 // t for w, t in zip(ws, tr)]
    kept = [] if others is None else [a for four in others for a in four]

    def body(*refs):
        i = pl.program_id(0)
        outs = refs[4 * n + len(kept):]
        for a in range(n):
            w_ref, g_ref, m_ref, v_ref = refs[a], refs[n + a], refs[2 * n + a], refs[3 * n + a]
            g_out, d_ref, nm_ref, nv_ref = outs[4 * a:4 * a + 4]

            @pl.when(i < 2 * per[a])
            def _():
                g = g_ref[...]
                g_out[...] = g
                d_ref[...], nm_ref[...], nv_ref[...] = _adamw_math(w_ref[...], g, m_ref[...], v_ref[...])

    def at(a, i):
        return jnp.minimum(i, 2 * per[a] - 1)

    blk = [pl.BlockSpec((None, tr[a], ws[a].shape[2]), lambda i, a=a: (layer, at(a, i), 0)) for a in range(n)]
    half = [pl.BlockSpec((None, tr[a], ws[a].shape[2]), lambda i, a=a: (at(a, i) // per[a], at(a, i) % per[a], 0))
            for a in range(n)]
    outs = pl.pallas_call(
        body, name="adamw_large", grid=(2 * max(per),),
        in_specs=blk + half + blk + blk + [pl.BlockSpec(memory_space=pl.ANY)] * len(kept),
        out_specs=[blk[a] for a in range(n) for _ in range(4)],
        out_shape=[jax.ShapeDtypeStruct(w.shape, F32) for w in ws for _ in range(4)],
        input_output_aliases={4 * n + k: k for k in range(len(kept))},
        compiler_params=_params("arbitrary"),
    )(*ws, *halves, *ms, *vs, *kept)
    return [outs[4 * a:4 * a + 4] for a in range(n)]


def _adamw_small(ws, gs, ms, vs):
    n = len(ws)

    def body(*refs):
        w_r, g_r, m_r, v_r = refs[:n], refs[n:2 * n], refs[2 * n:3 * n], refs[3 * n:4 * n]
        d_r, nm_r, nv_r = refs[4 * n:5 * n], refs[5 * n:6 * n], refs[6 * n:]
        for a in range(n):
            d_r[a][...], nm_r[a][...], nv_r[a][...] = _adamw_math(w_r[a][...], g_r[a][...], m_r[a][...], v_r[a][...])

    vm = pl.BlockSpec(memory_space=pltpu.VMEM)
    outs = pl.pallas_call(
        body, name="adamw_small", in_specs=[vm] * (4 * n), out_specs=[vm] * (3 * n),
        out_shape=[jax.ShapeDtypeStruct(w.shape, F32) for w in ws] * 3,
    )(*ws, *gs, *ms, *vs)
    return outs[:n], outs[n:2 * n], outs[2 * n:]


_LARGE = ("w_in", "w_out", "w_up", "w_down")
_SMALL = ("meta_tokens", "mix_pre_g", "conv_w", "sinks", "attn_out_g", "conv_out_g", "mix_post_g", "mlp_pre_g",
          "mlp_post_g")
_ORDER = ("meta_tokens", "mix_pre_g", "w_in", "conv_w", "sinks", "attn_out_g", "conv_out_g", "w_out", "mix_post_g",
          "mlp_pre_g", "w_up", "w_down", "mlp_post_g")


class _Reduce:
    def __init__(self, name, grads, after=()):
        self.name, self.n = name, len(grads)
        self.own = [g for g, _ in grads]
        half_rows = [g.shape[1] // 2 for g in self.own]
        zones = [lax.empty((N_DEV - 1, hr, g.shape[2]), BF16) for g, hr in zip(self.own, half_rows)]
        self.exchange = _Exchange(name + "_scatter", [b for _, b in grads] + zones, _scatter_plan(self.n, half_rows),
                                  (N_DEV - 1) * self.n, after)

    @property
    def token(self):
        return self.exchange.token

    def join(self, *after):
        bufs = self.exchange.wait(*after)
        halves = list(_sum_parts(self.own, bufs[self.n:]))
        self.exchange = _Exchange(self.name + "_join", halves, _join_plan(self.n), self.n)

    def done(self, *after):
        return self.exchange.wait(*after)


def _pad_cols(a, n=D_MODEL):
    return jnp.pad(a, ((0, 0), (0, n - a.shape[1])))


def kernel(x, meta_tokens, mix_pre_g, w_in, conv_w, sinks, attn_out_g, conv_out_g, w_out, mix_post_g, mlp_pre_g, w_up, w_down, mlp_post_g, loss_target, m_meta_tokens, m_mix_pre_g, m_w_in, m_conv_w, m_sinks, m_attn_out_g, m_conv_out_g, m_w_out, m_mix_post_g, m_mlp_pre_g, m_w_up, m_w_down, m_mlp_post_g, v_meta_tokens, v_mix_pre_g, v_w_in, v_conv_w, v_sinks, v_attn_out_g, v_conv_out_g, v_w_out, v_mix_post_g, v_mlp_pre_g, v_w_up, v_w_down, v_mlp_post_g):
    w = dict(meta_tokens=meta_tokens, mix_pre_g=mix_pre_g, w_in=w_in, conv_w=conv_w, sinks=sinks,
             attn_out_g=attn_out_g, conv_out_g=conv_out_g, w_out=w_out, mix_post_g=mix_post_g, mlp_pre_g=mlp_pre_g,
             w_up=w_up, w_down=w_down, mlp_post_g=mlp_post_g)
    m = dict(meta_tokens=m_meta_tokens, mix_pre_g=m_mix_pre_g, w_in=m_w_in, conv_w=m_conv_w, sinks=m_sinks,
             attn_out_g=m_attn_out_g, conv_out_g=m_conv_out_g, w_out=m_w_out, mix_post_g=m_mix_post_g,
             mlp_pre_g=m_mlp_pre_g, w_up=m_w_up, w_down=m_w_down, mlp_post_g=m_mlp_post_g)
    v = dict(meta_tokens=v_meta_tokens, mix_pre_g=v_mix_pre_g, w_in=v_w_in, conv_w=v_conv_w, sinks=v_sinks,
             attn_out_g=v_attn_out_g, conv_out_g=v_conv_out_g, w_out=v_w_out, mix_post_g=v_mix_post_g,
             mlp_pre_g=v_mlp_pre_g, w_up=v_w_up, w_down=v_w_down, mlp_post_g=v_mlp_post_g)
    chip = 2 * lax.axis_index("x") + lax.axis_index("y")
    tl = _Tiles(x.shape[1] + BLOCK)

    def zone(quarter):
        return lax.dynamic_update_slice(lax.empty((N_CHIPS,) + quarter.shape, quarter.dtype), quarter[None],
                                        (chip,) + (0,) * quarter.ndim)

    w, m, v = ({**d, "w_in": jnp.swapaxes(d["w_in"], 1, 2)} for d in (w, m, v))
    zones = {n: [zone(w[n][l].astype(BF16)) for l in range(DEPTH)] for n in _LARGE}
    first = _Exchange("gather_first", [zones["w_in"][0], zone(w["conv_w"]), zone(w["meta_tokens"])], _gather_plan(3), 9)

    def whole_in(quarters):
        return quarters.reshape(IN_W, D_MODEL)

    q_in, q_conv, q_meta = first.wait(*tl.tabs, tl.bias)
    out0 = _Exchange("gather_out", [zones["w_out"][0]], _gather_plan(1), 3, [q_in])
    mlp_halves = [D_MODEL // 2, FF_CHUNK // 2]
    rest = _Exchange("gather_rest", [zones[n][0] for n in ("w_up", "w_down")], _gather_half_plan(2, mlp_halves), 6,
                     [out0.token])
    conv_whole = jnp.transpose(q_conv, (1, 2, 0, 3)).reshape(DEPTH, CONV_K, CONV_W)
    meta = jnp.transpose(q_meta, (1, 0, 2)).reshape(N_META, D_MODEL)
    p = [{"conv_w": conv_whole[l], "sinks": w["sinks"][l]} for l in range(DEPTH)]
    for l in range(DEPTH):
        for n in ("mix_pre_g", "attn_out_g", "conv_out_g", "mix_post_g", "mlp_pre_g", "mlp_post_g"):
            p[l][n] = w[n][l][None, :]

    lead = jnp.concatenate([jnp.zeros((LEAD_PAD, D_MODEL), F32), meta], axis=0)
    p[0]["w_in"] = whole_in(q_in)
    mixed = _mixer_fwd(x[0], p[0], tl, lead, [rest.token])
    second = _Exchange("gather_second", [zones["w_in"][1], zones["w_out"][1]], _gather_plan(2), 6, [mixed[-1]])
    second_mlp = _Exchange("gather_second_mlp", [zones["w_up"][1], zones["w_down"][1]], _gather_plan(2), 6,
                           [second.token])
    hand_over = _Exchange("hand_over_rest", rest.wait(second_mlp.token), _hand_over_plan(2, mlp_halves), 6)
    p[0]["w_out"], = out0.wait(hand_over.token)
    h1, saved0 = _out_fwd(mixed, p[0], tl)
    p[0]["w_up"], p[0]["w_down"] = hand_over.wait(h1)
    h, saved0 = _mlp_fwd(h1, saved0, p[0], tl)
    q_in, p[1]["w_out"] = second.wait(h)
    p[1]["w_in"] = whole_in(q_in)
    h1, saved1 = _out_fwd(_mixer_fwd(h, p[1], tl), p[1], tl)
    p[1]["w_up"], p[1]["w_down"] = second_mlp.wait(h1)
    (loss_tile, dh), saved1 = _mlp_fwd(h1, saved1, p[1], tl, loss_target[0])

    def adamw(layer, halves, other):
        names = list(halves)
        done = _adamw_large(layer, [w[n] for n in names], [halves[n] for n in names], [m[n] for n in names],
                            [v[n] for n in names], None if other is None else [other[n] for n in names])
        return dict(zip(names, done))

    dh1, g1 = _mlp_part_bwd(dh, saved1, p[1], tl)
    carry, gm = _mix_out_part_bwd(dh1, saved1, p[1], tl)
    dh, dproj, gi = _attn_in_part_bwd(carry, saved1, p[1], tl)
    g1.update(gm, w_in=_in_grad(dproj, saved1), **gi)
    red1 = _Reduce("reduce1", [g1[n] for n in _LARGE])
    dh1, g0 = _mlp_part_bwd(dh, saved0, p[0], tl, [red1.token])
    red1.join(g0["w_down"][0])
    carry, gm = _mix_out_part_bwd(dh1, saved0, p[0], tl, [red1.token])
    first0 = ("w_up", "w_down", "w_out")
    g0.update(gm)
    red0a = _Reduce("reduce0a", [g0[n] for n in first0])
    (dlead, dseq), dproj, gi = _attn_in_part_bwd(carry, saved0, p[0], tl, [red0a.token], split_lead=True)
    g0.update(gi)
    grad_x = dseq[None]
    grads = {n: [g0[n], g1[n]] for n in g0 if n not in _LARGE}

    rows = [dlead[LEAD_PAD:]]
    for n in ("mix_pre_g", "mix_post_g", "mlp_pre_g", "mlp_post_g"):
        rows += grads[n]
    rows += [jnp.concatenate([grads["attn_out_g"][l], grads["conv_out_g"][l]], axis=1) for l in range(DEPTH)]
    rows.append(jnp.concatenate(grads["conv_w"], axis=1))
    rows.append(_pad_cols(jnp.concatenate(grads["sinks"])[None, :]))
    rows.append(_pad_cols(loss_tile[:1]))
    packed = jnp.concatenate(rows, axis=0)
    packed = jnp.pad(packed, ((0, SMALL_ROWS - packed.shape[0]), (0, 0)))
    device = 2 * chip + lax.axis_index("c")
    small_parts = _Exchange("gather_small", [lax.dynamic_update_slice(lax.empty((N_DEV,) + packed.shape, F32),
                                                                      packed[None], (device, 0, 0))], _all_plan, N_DEV - 1)
    g0["w_in"] = _in_grad(dproj, saved0, [small_parts.token])
    red0b = _Reduce("reduce0b", [g0["w_in"]])
    done1 = adamw(1, dict(zip(_LARGE, red1.done(red0b.token))), None)
    total = _sum_devices(small_parts.wait(*[done1[n][0] for n in _LARGE])[0])
    r0 = N_META
    small = {
        "meta_tokens": lax.dynamic_slice(total[:N_META], (0, chip * (D_MODEL // N_CHIPS)), (N_META, D_MODEL // N_CHIPS)),
        "mix_pre_g": total[r0:r0 + 2], "mix_post_g": total[r0 + 2:r0 + 4], "mlp_pre_g": total[r0 + 4:r0 + 6],
        "mlp_post_g": total[r0 + 6:r0 + 8],
        "attn_out_g": total[r0 + 8:r0 + 10, :ATTN_W], "conv_out_g": total[r0 + 8:r0 + 10, ATTN_W:],
        "conv_w": lax.dynamic_slice(total[r0 + 10:r0 + 13].reshape(CONV_K, DEPTH, CONV_W).transpose(1, 0, 2),
                                    (0, 0, chip * (CONV_W // N_CHIPS)), (DEPTH, CONV_K, CONV_W // N_CHIPS)),
        "sinks": total[r0 + 13, :DEPTH * N_Q_HEADS].reshape(DEPTH, N_Q_HEADS),
    }
    loss = total[r0 + 14, 0]

    ds, nms, nvs = _adamw_small([w[n] for n in _SMALL], [small[n] for n in _SMALL], [m[n] for n in _SMALL],
                                [v[n] for n in _SMALL])
    red0a.join(ds[0], grad_x)
    red0b.join(red0a.token)
    done0 = adamw(0, dict(zip(first0, red0a.done(red0b.token))), done1)
    done0.update(adamw(0, {"w_in": red0b.done(done0["w_down"][0])[0]}, done1))
    grad, delta, new_m, new_v = {}, {}, {}, {}
    for n in _LARGE:
        grad[n], delta[n], new_m[n], new_v[n] = done0[n]
    for d in (grad, delta, new_m, new_v):
        d["w_in"] = jnp.swapaxes(d["w_in"], 1, 2)
    for i, n in enumerate(_SMALL):
        grad[n], delta[n], new_m[n], new_v[n] = small[n], ds[i], nms[i], nvs[i]
    return (loss, grad_x, *[grad[n] for n in _ORDER], *[delta[n] for n in _ORDER], *[new_m[n] for n in _ORDER],
            *[new_v[n] for n in _ORDER])
```

```python
import jax
import jax.numpy as jnp
from jax import lax
from jax.experimental import pallas as pl
from jax.experimental.pallas import tpu as pltpu

F32 = jnp.float32
BF16 = jnp.bfloat16

D_MODEL = 1024
DEPTH = 2
N_META = 16
ATTN_W = 512
CONV_W = 512
HEAD_DIM = 64
N_Q_HEADS = 8
N_KV_HEADS = 2
GROUP = N_Q_HEADS // N_KV_HEADS
KV_W = N_KV_HEADS * HEAD_DIM
CONV_K = 3
BLOCK = 128
LEAD_PAD = BLOCK - N_META
ROPE_THETA = 500000.0
ROT_DIM = HEAD_DIM // 4
ROT_HALF = ROT_DIM // 2
D_FF = 4 * D_MODEL
IN_W = ATTN_W + 2 * KV_W + 3 * CONV_W
QKV_W = ATTN_W + 2 * KV_W
EPS = 1e-6
SCALE = HEAD_DIM ** -0.5
FF_CHUNK = 1024
N_CHIPS = 4
N_DEV = 8

ADAM_LR = 0.001
ADAM_B1 = 0.9
ADAM_B2 = 0.999
ADAM_EPS = 1e-08
ADAM_WD = 0.01
ADAM_STEP = 10

V7X_VMEM_LIMIT = 60 * 1024 * 1024
SMALL_ROWS = 32

MESH = pl.DeviceIdType.MESH


def _params(*sem):
    return pltpu.CompilerParams(dimension_semantics=sem, vmem_limit_bytes=V7X_VMEM_LIMIT)


def _block_rows(n):
    return max(r for r in range(16, min(n, 128) + 1, 16) if n % r == 0)


def _row_tile(t, most):
    nb = t // BLOCK
    for b in range(most // BLOCK, 0, -1):
        if nb % b == 0:
            return b * BLOCK
    return BLOCK


def _behind(body, deps):
    n = len(deps)

    def wrapped(*refs):
        body(*refs[n:])

    return wrapped, [pl.BlockSpec(memory_space=pl.ANY)] * n


def _rms(x, g):
    r = lax.rsqrt(jnp.mean(x * x, axis=-1, keepdims=True) + EPS)
    return x * r * g


def _rms_bwd(dy, x, g):
    r = lax.rsqrt(jnp.mean(x * x, axis=-1, keepdims=True) + EPS)
    xh = x * r
    dg = jnp.sum(dy * xh, axis=0, keepdims=True)
    dxh = dy * g
    dx = r * (dxh - xh * jnp.mean(dxh * xh, axis=-1, keepdims=True))
    return dx, dg


def _rope(x, cos, sa, sb):
    n = x.shape[-1]
    return x * cos + pltpu.roll(x, n - ROT_HALF, 1) * sa + pltpu.roll(x, ROT_HALF, 1) * sb


def _rope_bwd(dy, cos, sa, sb):
    n = dy.shape[-1]
    return dy * cos + pltpu.roll(dy * sa, ROT_HALF, 1) + pltpu.roll(dy * sb, n - ROT_HALF, 1)


def _rope_tables(t):
    pos = lax.broadcasted_iota(jnp.int32, (t, ROT_HALF), 0).astype(F32) - LEAD_PAD
    pair = lax.broadcasted_iota(jnp.int32, (t, ROT_HALF), 1).astype(F32)
    inv_freq = jnp.power(jnp.float32(ROPE_THETA), -(2.0 * pair) / ROT_DIM)
    ang = pos * inv_freq
    cos, sin = lax.optimization_barrier((jnp.cos(ang), jnp.sin(ang)))
    spread = (1, 2 * HEAD_DIM // ROT_HALF)
    cos, sin = jnp.tile(cos, spread), jnp.tile(sin, spread)
    dim = lax.broadcasted_iota(jnp.int32, (t, 2 * HEAD_DIM), 1) % HEAD_DIM
    return jnp.where(dim < ROT_DIM, cos, 1.0), jnp.where(dim < ROT_DIM, sin, 0.0)


def _rope_factors(cos, sin):
    dim = lax.broadcasted_iota(jnp.int32, sin.shape, 1) % HEAD_DIM
    return cos, jnp.where(dim < ROT_HALF, -sin, 0.0), jnp.where(dim >= ROT_HALF, sin, 0.0)


def _in_proj(h, g, w, tabs, tm, lead=None):
    t = h.shape[0] + (0 if lead is None else BLOCK)
    per_step = 0 if lead is None else tm // BLOCK

    def body(*refs):
        if lead is None:
            x = refs[0][...]
            refs = refs[1:]
        else:
            blocks = [r[...] for r in refs[1:1 + per_step]]
            blocks[0] = jnp.where(pl.program_id(0) == 0, refs[0][...], blocks[0])
            x = jnp.concatenate(blocks, axis=0)
            first_out = 1 + per_step + 4
            refs[first_out][...] = x
            refs = refs[1 + per_step:first_out] + refs[first_out + 1:]
        g_ref, w_ref, c_ref, s_ref, a_ref, q_ref, k_ref, v_ref, b_ref, cg_ref, hc_ref = refs
        a = _rms(x, g_ref[...]).astype(BF16)
        a_ref[...] = a
        p = lax.dot_general(a, w_ref[...], (((1,), (1,)), ((), ())), preferred_element_type=F32)
        cos, sa, sb = _rope_factors(c_ref[...], s_ref[...])
        rep = ATTN_W // (2 * HEAD_DIM)
        q = _rope(p[:, :ATTN_W], jnp.tile(cos, (1, rep)), jnp.tile(sa, (1, rep)), jnp.tile(sb, (1, rep)))
        q_ref[...] = (q * SCALE).astype(BF16)
        k_ref[...] = _rope(p[:, ATTN_W:ATTN_W + KV_W], cos, sa, sb).astype(BF16)
        v_ref[...] = p[:, ATTN_W + KV_W:QKV_W].astype(BF16)
        b_ref[...] = p[:, QKV_W:QKV_W + CONV_W].astype(BF16)
        cg_ref[...] = p[:, QKV_W + CONV_W:QKV_W + 2 * CONV_W].astype(BF16)
        hc_ref[...] = p[:, QKV_W + 2 * CONV_W:].astype(BF16)

    row = lambda n: pl.BlockSpec((tm, n), lambda i: (i, 0))
    full = lambda a: pl.BlockSpec(a.shape, lambda i: (0, 0))

    def sequence_block(b):
        return pl.BlockSpec((BLOCK, D_MODEL), lambda i: (jnp.maximum(i * per_step + b - 1, 0), 0))

    if lead is None:
        first_in, first_args, first_out, first_shape = [row(D_MODEL)], [h], [], []
    else:
        first_in = [full(lead)] + [sequence_block(b) for b in range(per_step)]
        first_args = [lead] + [h] * per_step
        first_out, first_shape = [row(D_MODEL)], [jax.ShapeDtypeStruct((t, D_MODEL), F32)]
    return pl.pallas_call(
        body, name="in_proj", grid=(t // tm,),
        in_specs=first_in + [full(g), full(w), row(2 * HEAD_DIM), row(2 * HEAD_DIM)],
        out_specs=first_out + [row(D_MODEL), row(ATTN_W), row(KV_W), row(KV_W), row(CONV_W), row(CONV_W), row(CONV_W)],
        out_shape=first_shape + [jax.ShapeDtypeStruct((t, D_MODEL), BF16), jax.ShapeDtypeStruct((t, ATTN_W), BF16),
                                 jax.ShapeDtypeStruct((t, KV_W), BF16), jax.ShapeDtypeStruct((t, KV_W), BF16),
                                 jax.ShapeDtypeStruct((t, CONV_W), BF16), jax.ShapeDtypeStruct((t, CONV_W), BF16),
                                 jax.ShapeDtypeStruct((t, CONV_W), BF16)],
        compiler_params=_params("parallel"),
    )(*first_args, g, w, *tabs)


def _attn_bias():
    r = lax.broadcasted_iota(jnp.int32, (3, BLOCK, 2 * BLOCK), 1)
    c = lax.broadcasted_iota(jnp.int32, (3, BLOCK, 2 * BLOCK), 2)
    i = lax.broadcasted_iota(jnp.int32, (3, BLOCK, 2 * BLOCK), 0)
    ok = (c > r) & (c <= r + BLOCK) & (c + (i - 1) * BLOCK >= LEAD_PAD)
    return jnp.where(ok, 0.0, -jnp.inf).astype(F32)


def _attn_scores(qh, kg, bias):
    return lax.dot_general(qh, kg, (((1,), (1,)), ((), ())), preferred_element_type=F32) + bias


def _attn_probs(s, sk):
    m = jnp.maximum(jnp.max(s, axis=-1, keepdims=True), sk)
    e = jnp.exp(s - m)
    es = jnp.exp(sk - m)
    rden = 1.0 / (jnp.sum(e, axis=-1, keepdims=True) + es)
    return e * rden, es * rden


def _head(hh):
    return slice(hh * HEAD_DIM, (hh + 1) * HEAD_DIM)


def _two_blocks(ref, i):
    prev = jnp.maximum(i - 1, 0)
    return jnp.concatenate([ref[pl.ds(pl.multiple_of(prev * BLOCK, BLOCK), BLOCK), :],
                            ref[pl.ds(pl.multiple_of(i * BLOCK, BLOCK), BLOCK), :]], axis=0)


def _attn_fwd(q, k, v, bias, sinks, tm):
    t = q.shape[0]
    per_step = tm // BLOCK
    heads = range(N_Q_HEADS)

    def body(s_ref, q_ref, k_ref, v_ref, bias_ref, o_ref):
        for b in range(per_step):
            i = pl.program_id(0) * per_step + b
            rows = slice(b * BLOCK, (b + 1) * BLOCK)
            kc, vc = _two_blocks(k_ref, i), _two_blocks(v_ref, i)
            bias_i = bias_ref[jnp.minimum(i, 2)]
            scores = [_attn_scores(q_ref[rows, _head(hh)], kc[:, _head(hh // GROUP)], bias_i) for hh in heads]
            probs = [_attn_probs(scores[hh], s_ref[hh])[0].astype(BF16) for hh in heads]
            for hh in heads:
                o_ref[rows, _head(hh)] = jnp.dot(probs[hh], vc[:, _head(hh // GROUP)],
                                                 preferred_element_type=F32).astype(BF16)

    whole = pl.BlockSpec((t, KV_W), lambda i: (0, 0))
    return pl.pallas_call(
        body, name="attn_fwd", grid=(t // tm,),
        in_specs=[pl.BlockSpec(memory_space=pltpu.SMEM), pl.BlockSpec((tm, ATTN_W), lambda i: (i, 0)), whole, whole,
                  pl.BlockSpec(bias.shape, lambda i: (0, 0, 0))],
        out_specs=pl.BlockSpec((tm, ATTN_W), lambda i: (i, 0)),
        out_shape=jax.ShapeDtypeStruct((t, ATTN_W), BF16),
        compiler_params=_params("parallel"),
    )(sinks, q, k, v, bias)


def _shift_rows(u, halo, n):
    r = pltpu.roll(u, n, 0)
    hr = pltpu.roll(halo, n, 0)
    idx = lax.broadcasted_iota(jnp.int32, hr.shape, 0)
    return jnp.concatenate([jnp.where(idx < n, hr, r[:8]), r[8:]], axis=0)


def _advance_rows(u, halo, n):
    rows = u.shape[0]
    r = pltpu.roll(u, rows - n, 0)
    hr = pltpu.roll(halo, 8 - n, 0)
    idx = lax.broadcasted_iota(jnp.int32, hr.shape, 0)
    return jnp.concatenate([r[:rows - 8], jnp.where(idx >= 8 - n, hr, r[rows - 8:])], axis=0)


def _mix_out(h, o, b, c, hc, cw, ga, gc, w, gp, tm, deps=()):
    t = h.shape[0]

    def body(h_ref, o_ref, b_ref, c_ref, hc_ref, cw_ref, ga_ref, gc_ref, w_ref, gp_ref, h1_ref, y_ref, z_ref, halo):
        @pl.when(pl.program_id(0) == 0)
        def _():
            halo[...] = jnp.zeros_like(halo)

        u = c_ref[...].astype(F32) * hc_ref[...].astype(F32)
        cv = cw_ref[0:1, :] * _shift_rows(u, halo[...], 2) + cw_ref[1:2, :] * _shift_rows(u, halo[...], 1) \
            + cw_ref[2:3, :] * u
        halo[...] = u[tm - 8:]
        yc = b_ref[...].astype(F32) * cv
        y = jnp.concatenate([_rms(o_ref[...].astype(F32), ga_ref[...]), _rms(yc, gc_ref[...])], axis=1).astype(BF16)
        y_ref[...] = y
        z = jnp.dot(y, w_ref[...].reshape(D_MODEL, D_MODEL), preferred_element_type=F32)
        z_ref[...] = z
        h1_ref[...] = h_ref[...] + _rms(z, gp_ref[...])

    row = lambda n: pl.BlockSpec((tm, n), lambda i: (i, 0))
    full = lambda a: pl.BlockSpec(a.shape, lambda i: (0,) * a.ndim)
    body, dep_specs = _behind(body, deps)
    return pl.pallas_call(
        body, name="mix_out", grid=(t // tm,),
        in_specs=dep_specs + [row(D_MODEL), row(ATTN_W), row(CONV_W), row(CONV_W), row(CONV_W), full(cw), full(ga),
                              full(gc), full(w), full(gp)],
        out_specs=[row(D_MODEL), row(D_MODEL), row(D_MODEL)],
        out_shape=[jax.ShapeDtypeStruct((t, D_MODEL), F32), jax.ShapeDtypeStruct((t, D_MODEL), BF16),
                   jax.ShapeDtypeStruct((t, D_MODEL), F32)],
        scratch_shapes=[pltpu.VMEM((8, CONV_W), F32)],
        compiler_params=_params("arbitrary"),
    )(*deps, h, o, b, c, hc, cw, ga, gc, w, gp)


def _mlp(h1, g1, wu, wd, g2, tm, target=None):
    t = h1.shape[0]
    nj = D_FF // FF_CHUNK
    per_step = tm // BLOCK if target is not None else 0

    def body(h1_ref, g1_ref, wu_ref, wd_ref, g2_ref, *rest):
        t_refs, outs = rest[:per_step], rest[per_step:]
        a2_ref, slope_ref = (outs[1], outs[2]) if target is None else (outs[2], outs[3])
        a2 = _rms(h1_ref[...], g1_ref[...]).astype(BF16)
        a2_ref[...] = a2
        f = None
        for j in range(nj):
            up = jnp.dot(a2, wu_ref[j], preferred_element_type=F32)
            r = jnp.maximum(up, 0.0)
            slope_ref[:, j * FF_CHUNK:(j + 1) * FF_CHUNK] = (r + r).astype(BF16)
            part = jnp.dot((r * r).astype(BF16), wd_ref[j], preferred_element_type=F32)
            f = part if f is None else f + part
        h2 = h1_ref[...] + _rms(f, g2_ref[...])
        if target is None:
            outs[0][...] = h2
            outs[3][...] = f
            return
        loss_ref, dh_ref, df_ref, dg2_ref = outs[0], outs[1], outs[4], outs[5]
        i = pl.program_id(0)

        @pl.when(i == 0)
        def _():
            loss_ref[...] = jnp.zeros_like(loss_ref)
            dg2_ref[...] = jnp.zeros_like(dg2_ref)

        total = jnp.zeros((), F32)
        dh2 = []
        for b in range(per_step):
            err = h2[b * BLOCK:(b + 1) * BLOCK] - t_refs[b][...]
            if b == 0:
                err = jnp.where(i == 0, 0.0, err)
            dh2.append(err * (1.0 / D_MODEL))
            total = total + jnp.sum(err * err)
        loss_ref[...] += total * (0.5 / D_MODEL)
        dh2 = jnp.concatenate(dh2, axis=0)
        dh_ref[...] = dh2
        df, dg = _rms_bwd(dh2, f, g2_ref[...])
        df_ref[...] = df.astype(BF16)
        dg2_ref[...] += dg

    def target_block(b):
        return pl.BlockSpec((BLOCK, D_MODEL), lambda i: (jnp.maximum(i * per_step + b - 1, 0), 0))

    row = pl.BlockSpec((tm, D_MODEL), lambda i: (i, 0))
    vec = pl.BlockSpec((1, D_MODEL), lambda i: (0, 0))
    resident = pl.BlockSpec(memory_space=pltpu.VMEM)
    wide = pl.BlockSpec((tm, D_FF), lambda i: (i, 0))
    kept = [jax.ShapeDtypeStruct((t, D_MODEL), BF16), jax.ShapeDtypeStruct((t, D_FF), BF16)]
    if target is None:
        specs = [row, row, wide, row]
        shapes = [jax.ShapeDtypeStruct((t, D_MODEL), F32)] + kept + [jax.ShapeDtypeStruct((t, D_MODEL), F32)]
    else:
        specs = [pl.BlockSpec((8, 128), lambda i: (0, 0)), row, row, wide, row, vec]
        shapes = [jax.ShapeDtypeStruct((8, 128), F32), jax.ShapeDtypeStruct((t, D_MODEL), F32)] + kept \
            + [jax.ShapeDtypeStruct((t, D_MODEL), BF16), jax.ShapeDtypeStruct((1, D_MODEL), F32)]
    outs = pl.pallas_call(
        body, name="mlp", grid=(t // tm,),
        in_specs=[row, vec, resident, resident, vec] + [target_block(b) for b in range(per_step)],
        out_specs=specs, out_shape=shapes,
        compiler_params=_params("parallel" if target is None else "arbitrary"),
    )(h1, g1, wu, wd, g2, *([target] * per_step))
    if target is None:
        return tuple(outs)
    return (tuple(outs[:2]), outs[2], outs[3], tuple(outs[4:]))


def _mlp_bwd_hidden(dh2, f, g2, slope, wd, tm, deps=(), df=None):
    t = slope.shape[0]
    nj = D_FF // FF_CHUNK

    def hidden(df, slope_ref, wd_ref, dup_ref):
        for j in range(nj):
            cols = slice(j * FF_CHUNK, (j + 1) * FF_CHUNK)
            dact = lax.dot_general(df, wd_ref[j], (((1,), (1,)), ((), ())), preferred_element_type=F32)
            dup_ref[:, cols] = (dact * slope_ref[:, cols].astype(F32)).astype(BF16)

    def body(dh2_ref, f_ref, g2_ref, slope_ref, wd_ref, df_ref, dup_ref, dg2_ref):
        @pl.when(pl.program_id(0) == 0)
        def _():
            dg2_ref[...] = jnp.zeros_like(dg2_ref)

        df, dg = _rms_bwd(dh2_ref[...], f_ref[...], g2_ref[...])
        dg2_ref[...] += dg
        df = df.astype(BF16)
        df_ref[...] = df
        hidden(df, slope_ref, wd_ref, dup_ref)

    def body_from_df(df_ref, slope_ref, wd_ref, dup_ref):
        hidden(df_ref[...], slope_ref, wd_ref, dup_ref)

    row = pl.BlockSpec((tm, D_MODEL), lambda i: (i, 0))
    wide = pl.BlockSpec((tm, D_FF), lambda i: (i, 0))
    vec = pl.BlockSpec((1, D_MODEL), lambda i: (0, 0))
    resident = pl.BlockSpec(memory_space=pltpu.VMEM)
    if df is not None:
        body_from_df, dep_specs = _behind(body_from_df, deps)
        return pl.pallas_call(
            body_from_df, name="mlp_bwd_hidden", grid=(t // tm,), in_specs=dep_specs + [row, wide, resident],
            out_specs=wide, out_shape=jax.ShapeDtypeStruct((t, D_FF), BF16), compiler_params=_params("parallel"),
        )(*deps, df, slope, wd)
    body, dep_specs = _behind(body, deps)
    return pl.pallas_call(
        body, name="mlp_bwd_hidden", grid=(t // tm,),
        in_specs=dep_specs + [row, row, vec, wide, resident],
        out_specs=[row, wide, vec],
        out_shape=[jax.ShapeDtypeStruct((t, D_MODEL), BF16), jax.ShapeDtypeStruct((t, D_FF), BF16),
                   jax.ShapeDtypeStruct((1, D_MODEL), F32)],
        compiler_params=_params("arbitrary"),
    )(*deps, dh2, f, g2, slope, wd)


def _mlp_bwd_input(dup, wu, h1, g1, dh2, tm):
    t = dh2.shape[0]
    nj = D_FF // FF_CHUNK

    def body(dup_ref, wu_ref, h1_ref, g1_ref, dh2_ref, dh1_ref, dg1_ref):
        @pl.when(pl.program_id(0) == 0)
        def _():
            dg1_ref[...] = jnp.zeros_like(dg1_ref)

        da2 = None
        for j in range(nj):
            part = lax.dot_general(dup_ref[:, j * FF_CHUNK:(j + 1) * FF_CHUNK], wu_ref[j], (((1,), (1,)), ((), ())),
                                   preferred_element_type=F32)
            da2 = part if da2 is None else da2 + part
        dx, dg = _rms_bwd(da2, h1_ref[...], g1_ref[...])
        dh1_ref[...] = dh2_ref[...] + dx
        dg1_ref[...] += dg

    row = pl.BlockSpec((tm, D_MODEL), lambda i: (i, 0))
    vec = pl.BlockSpec((1, D_MODEL), lambda i: (0, 0))
    return pl.pallas_call(
        body, name="mlp_bwd_input", grid=(t // tm,),
        in_specs=[pl.BlockSpec((tm, D_FF), lambda i: (i, 0)), pl.BlockSpec(memory_space=pltpu.VMEM), row, vec, row],
        out_specs=[row, vec],
        out_shape=[jax.ShapeDtypeStruct((t, D_MODEL), F32), jax.ShapeDtypeStruct((1, D_MODEL), F32)],
        compiler_params=_params("arbitrary"),
    )(dup, wu, h1, g1, dh2)


def _row_split(t):
    tile = min(t, 1024)
    return tile, t // tile, t % tile


def _row_split_specs(t, cols):
    tile, whole, rest = _row_split(t)
    specs = [pl.BlockSpec((tile, cols), lambda r: (jnp.maximum(r - (1 if rest else 0), 0), 0))]
    if rest:
        specs.append(pl.BlockSpec((rest, cols), lambda r: (whole * tile // rest, 0)))
    return specs


def _weight_grad(x, y, name, x_is_slope=False, deps=()):
    t, k = x.shape
    n = y.shape[1]
    tn = FF_CHUNK
    tk = FF_CHUNK if k % FF_CHUNK == 0 else k
    _, whole, rest = _row_split(t)
    steps = whole + bool(rest)
    tiles = [(a, b) for a in range(k // tk) for b in range(n // tn)]

    def body(*refs):
        o_ref, ob_ref, acc, accb, sem = refs[-5:]
        r = pl.program_id(0)

        def out_copies(a, b):
            return (pltpu.make_async_copy(acc.at[a, b], o_ref.at[a, b], sem.at[0, a, b]),
                    pltpu.make_async_copy(accb.at[a, b], ob_ref.at[a, b], sem.at[1, a, b]))

        def add(x_ref, y_ref, first, maybe_last):
            for a in range(k // tk):
                xv = x_ref[:, a * tk:(a + 1) * tk]
                if x_is_slope:
                    xv = xv.astype(F32)
                    xv = (xv * xv * 0.25).astype(BF16)
                for b in range(n // tn):
                    part = lax.dot_general(xv, y_ref[:, b * tn:(b + 1) * tn], (((0,), (0,)), ((), ())),
                                           preferred_element_type=F32)
                    if first:
                        acc[a, b] = part
                    else:
                        acc[a, b] += part
                    if maybe_last:
                        @pl.when(r == steps - 1)
                        def _():
                            accb[a, b] = acc[a, b].astype(BF16)
                            for copy in out_copies(a, b):
                                copy.start()

        if rest:
            pl.when(r == 0)(lambda: add(refs[1], refs[3], True, False))
            pl.when(r > 0)(lambda: add(refs[0], refs[2], False, True))
        elif steps == 1:
            add(refs[0], refs[1], True, True)
        else:
            pl.when(r == 0)(lambda: add(refs[0], refs[1], True, False))
            pl.when(r > 0)(lambda: add(refs[0], refs[1], False, True))

        @pl.when(r == steps - 1)
        def _():
            for a, b in tiles:
                for copy in out_copies(a, b):
                    copy.wait()

    shape = (k // tk, n // tn, tk, tn)
    body, dep_specs = _behind(body, deps)
    return pl.pallas_call(
        body, name=name, grid=(steps,),
        in_specs=dep_specs + _row_split_specs(t, k) + _row_split_specs(t, n),
        out_specs=[pl.BlockSpec(memory_space=pl.ANY)] * 2,
        out_shape=[jax.ShapeDtypeStruct(shape, F32), jax.ShapeDtypeStruct(shape, BF16)],
        scratch_shapes=[pltpu.VMEM(shape, F32), pltpu.VMEM(shape, BF16), pltpu.SemaphoreType.DMA((2,) + shape[:2])],
        compiler_params=_params("arbitrary"),
    )(*deps, *([x] * (1 + bool(rest))), *([y] * (1 + bool(rest))))


def _mix_out_bwd(dh1, z, gp, w, o, b, c, hc, cw, ga, gc, tm, deps=()):
    t = dh1.shape[0]
    nt = t // tm
    per16 = tm // 16

    def body(dh1_ref, z_ref, gp_ref, w_ref, o_ref, b_ref, c_ref, hc_ref, cp_ref, hp_ref, cw_ref, ga_ref, gc_ref,
             dz_ref, do_ref, dbch_ref, dgp_ref, dga_ref, dgc_ref, dcw_ref, halo):
        i = pl.program_id(0)

        @pl.when(i == 0)
        def _():
            halo[...] = jnp.zeros_like(halo)
            dgp_ref[...] = jnp.zeros_like(dgp_ref)
            dga_ref[...] = jnp.zeros_like(dga_ref)
            dgc_ref[...] = jnp.zeros_like(dgc_ref)
            dcw_ref[...] = jnp.zeros_like(dcw_ref)

        dz, dgp = _rms_bwd(dh1_ref[...], z_ref[...], gp_ref[...])
        dgp_ref[...] += dgp
        dz = dz.astype(BF16)
        dz_ref[...] = dz
        dy = lax.dot_general(dz, w_ref[...].reshape(D_MODEL, D_MODEL), (((1,), (1,)), ((), ())),
                             preferred_element_type=F32)
        do, dga = _rms_bwd(dy[:, :ATTN_W], o_ref[...].astype(F32), ga_ref[...])
        do_ref[...] = do.astype(BF16)
        dga_ref[...] += dga

        cc, hh = c_ref[...].astype(F32), hc_ref[...].astype(F32)
        u = cc * hh
        first = i == nt - 1
        u_before = jnp.where(first, 0.0, (cp_ref[...].astype(F32) * hp_ref[...].astype(F32))[8:])
        u1 = _shift_rows(u, u_before, 1)
        u2 = _shift_rows(u, u_before, 2)
        cv = cw_ref[0:1, :] * u2 + cw_ref[1:2, :] * u1 + cw_ref[2:3, :] * u
        bb = b_ref[...].astype(F32)
        dyc, dgc = _rms_bwd(dy[:, ATTN_W:], bb * cv, gc_ref[...])
        dgc_ref[...] += dgc
        dcv = dyc * bb
        d1 = _advance_rows(dcv, halo[...], 1)
        d2 = _advance_rows(dcv, halo[...], 2)
        halo[...] = dcv[:8]
        du = cw_ref[2:3, :] * dcv + cw_ref[1:2, :] * d1 + cw_ref[0:1, :] * d2
        dbch_ref[...] = jnp.concatenate([dyc * cv, du * hh, du * cc], axis=1).astype(BF16)
        dcw_ref[...] += jnp.concatenate([jnp.sum(dcv * u2, axis=0, keepdims=True),
                                         jnp.sum(dcv * u1, axis=0, keepdims=True),
                                         jnp.sum(dcv * u, axis=0, keepdims=True)], axis=0)

    row = lambda n: pl.BlockSpec((tm, n), lambda i: (nt - 1 - i, 0))
    before = pl.BlockSpec((16, CONV_W), lambda i: (jnp.maximum((nt - 1 - i) * per16 - 1, 0), 0))
    full = lambda a: pl.BlockSpec(a.shape, lambda i: (0,) * a.ndim)
    vec = lambda n: pl.BlockSpec((1, n), lambda i: (0, 0))
    body, dep_specs = _behind(body, deps)
    return pl.pallas_call(
        body, name="mix_out_bwd", grid=(nt,),
        in_specs=dep_specs + [row(D_MODEL), row(D_MODEL), full(gp), full(w), row(ATTN_W), row(CONV_W), row(CONV_W),
                              row(CONV_W), before, before, full(cw), full(ga), full(gc)],
        out_specs=[row(D_MODEL), row(ATTN_W), row(3 * CONV_W), vec(D_MODEL), vec(ATTN_W), vec(CONV_W),
                   pl.BlockSpec((CONV_K, CONV_W), lambda i: (0, 0))],
        out_shape=[jax.ShapeDtypeStruct((t, D_MODEL), BF16), jax.ShapeDtypeStruct((t, ATTN_W), BF16),
                   jax.ShapeDtypeStruct((t, 3 * CONV_W), BF16), jax.ShapeDtypeStruct((1, D_MODEL), F32),
                   jax.ShapeDtypeStruct((1, ATTN_W), F32), jax.ShapeDtypeStruct((1, CONV_W), F32),
                   jax.ShapeDtypeStruct((CONV_K, CONV_W), F32)],
        scratch_shapes=[pltpu.VMEM((8, CONV_W), F32)],
        compiler_params=_params("arbitrary"),
    )(*deps, dh1, z, gp, w, o, b, c, hc, c, hc, cw, ga, gc)


def _attn_bwd(q, k, v, o, do, bias, sinks, tm, deps=()):
    t = q.shape[0]
    per_step = tm // BLOCK

    def body(s_ref, q_ref, k_ref, v_ref, o_ref, do_ref, bias_ref, dq_ref, dk_ref, dv_ref, ds_ref):
        step = pl.program_id(0)

        @pl.when(step == 0)
        def _():
            ds_ref[...] = jnp.zeros_like(ds_ref)

        heads = range(N_Q_HEADS)

        def first_matmuls(b):
            i = step * per_step + b
            rows = slice(b * BLOCK, (b + 1) * BLOCK)
            kc, vc = _two_blocks(k_ref, i), _two_blocks(v_ref, i)
            bias_i = bias_ref[jnp.minimum(i, 2)]
            kgs = [kc[:, _head(g)] for g in range(N_KV_HEADS)]
            vgs = [vc[:, _head(g)] for g in range(N_KV_HEADS)]
            qs = [q_ref[rows, _head(hh)] for hh in heads]
            dosb = [do_ref[rows, _head(hh)] for hh in heads]
            dos = [d.astype(F32) for d in dosb]
            scores = [_attn_scores(qs[hh], kgs[hh // GROUP], bias_i) for hh in heads]
            dps = [lax.dot_general(dosb[hh], vgs[hh // GROUP], (((1,), (1,)), ((), ())), preferred_element_type=F32)
                   for hh in heads]
            return kgs, qs, dos, dosb, scores, dps

        dsink = [jnp.zeros((BLOCK, 1), F32) for _ in range(N_Q_HEADS)]
        ahead = None
        for b in range(per_step):
            i = step * per_step + b
            rows = slice(b * BLOCK, (b + 1) * BLOCK)
            kgs, qs, dos, dosb, scores, dps = first_matmuls(b)
            ps, dss = [], []
            for hh in heads:
                p, share = _attn_probs(scores[hh], s_ref[hh])
                drow = jnp.sum(dos[hh] * o_ref[rows, _head(hh)].astype(F32), axis=-1, keepdims=True)
                dss.append((p * (dps[hh] - drow)).astype(BF16))
                ps.append(p.astype(BF16))
                dsink[hh] = dsink[hh] + share * drow
            for hh in heads:
                dq_ref[rows, _head(hh)] = (jnp.dot(dss[hh], kgs[hh // GROUP], preferred_element_type=F32)
                                           * SCALE).astype(BF16)
            groups = [slice(GROUP * g, GROUP * (g + 1)) for g in range(N_KV_HEADS)]
            dkg = [lax.dot_general(jnp.concatenate(dss[gr], axis=0), jnp.concatenate(qs[gr], axis=0),
                                   (((0,), (0,)), ((), ())), preferred_element_type=F32) for gr in groups]
            dvg = [lax.dot_general(jnp.concatenate(ps[gr], axis=0), jnp.concatenate(dosb[gr], axis=0),
                                   (((0,), (0,)), ((), ())), preferred_element_type=F32) for gr in groups]
            dkb, dvb = jnp.concatenate(dkg, axis=1), jnp.concatenate(dvg, axis=1)
            if b == 0:
                @pl.when(step > 0)
                def _():
                    before = pl.ds(pl.multiple_of((i - 1) * BLOCK, BLOCK), BLOCK)
                    dk_ref[before, :] += dkb[:BLOCK]
                    dv_ref[before, :] += dvb[:BLOCK]
            else:
                at = pl.ds(pl.multiple_of((i - 1) * BLOCK, BLOCK), BLOCK)
                dk_ref[at, :] = ahead[0] + dkb[:BLOCK]
                dv_ref[at, :] = ahead[1] + dvb[:BLOCK]
            ahead = (dkb[BLOCK:], dvb[BLOCK:])
        last = pl.ds(pl.multiple_of(((step + 1) * per_step - 1) * BLOCK, BLOCK), BLOCK)
        dk_ref[last, :] = ahead[0]
        dv_ref[last, :] = ahead[1]
        for hh in range(N_Q_HEADS):
            ds_ref[hh:hh + 1, :] -= jnp.sum(dsink[hh])

    whole = pl.BlockSpec((t, KV_W), lambda i: (0, 0))
    blk = pl.BlockSpec((tm, ATTN_W), lambda i: (i, 0))
    body, dep_specs = _behind(body, deps)
    return pl.pallas_call(
        body, name="attn_bwd", grid=(t // tm,),
        in_specs=dep_specs + [pl.BlockSpec(memory_space=pltpu.SMEM), blk, whole, whole, blk, blk,
                              pl.BlockSpec(bias.shape, lambda i: (0, 0, 0))],
        out_specs=[blk, whole, whole, pl.BlockSpec((N_Q_HEADS, 128), lambda i: (0, 0))],
        out_shape=[jax.ShapeDtypeStruct((t, ATTN_W), BF16), jax.ShapeDtypeStruct((t, KV_W), F32),
                   jax.ShapeDtypeStruct((t, KV_W), F32), jax.ShapeDtypeStruct((N_Q_HEADS, 128), F32)],
        compiler_params=_params("arbitrary"),
    )(*deps, sinks, q, k, v, o, do, bias)


def _in_proj_bwd(dq, dk, dv, dbch, w, dh1, h, g, tabs, tm, split_lead=False):
    t = h.shape[0]
    nt = t // tm

    def body(dq_ref, dk_ref, dv_ref, dbch_ref, w_ref, dh1_ref, h_ref, g_ref, c_ref, s_ref, *rest):
        dp_ref, dg_ref = rest[2:4] if split_lead else rest[1:3]
        i = pl.program_id(0)

        @pl.when(i == 0)
        def _():
            dg_ref[...] = jnp.zeros_like(dg_ref)

        cos, sa, sb = _rope_factors(c_ref[...], s_ref[...])
        rep = ATTN_W // (2 * HEAD_DIM)
        dqr = _rope_bwd(dq_ref[...].astype(F32), jnp.tile(cos, (1, rep)), jnp.tile(sa, (1, rep)),
                        jnp.tile(sb, (1, rep)))
        dkr = _rope_bwd(dk_ref[...], cos, sa, sb)
        dp = jnp.concatenate([dqr.astype(BF16), dkr.astype(BF16), dv_ref[...].astype(BF16), dbch_ref[...]], axis=1)
        dp_ref[...] = dp
        da = jnp.dot(dp, w_ref[...], preferred_element_type=F32)
        dx, dg = _rms_bwd(da, h_ref[...], g_ref[...])
        dg_ref[...] += dg
        dh = dh1_ref[...] + dx
        if not split_lead:
            rest[0][...] = dh
            return
        lead_ref, seq_ref, stage, sems = rest[0], rest[1], rest[4], rest[5]

        def copy(j, slot, first):
            if first:
                return pltpu.make_async_copy(stage.at[slot, pl.ds(BLOCK, tm - BLOCK)],
                                             seq_ref.at[pl.ds(0, tm - BLOCK)], sems.at[slot])
            return pltpu.make_async_copy(stage.at[slot], seq_ref.at[pl.ds(pl.multiple_of(j * tm - BLOCK, BLOCK), tm)],
                                         sems.at[slot])

        slot = i % 2
        pl.when(i == 2)(lambda: copy(0, slot, True).wait())
        pl.when(i > 2)(lambda: copy(i - 2, slot, False).wait())
        stage[slot] = dh

        @pl.when(i == 0)
        def _():
            lead_ref[...] = dh[:BLOCK]
            copy(0, slot, True).start()

        pl.when(i > 0)(lambda: copy(i, slot, False).start())

        @pl.when(i == nt - 1)
        def _():
            for j in range(max(nt - 2, 0), nt):
                copy(j, j % 2, j == 0).wait()

    row = lambda n: pl.BlockSpec((tm, n), lambda i: (i, 0))
    full = lambda a: pl.BlockSpec(a.shape, lambda i: (0, 0))
    dh_specs, dh_shapes, scratch = [row(D_MODEL)], [jax.ShapeDtypeStruct((t, D_MODEL), F32)], []
    if split_lead:
        dh_specs = [pl.BlockSpec((BLOCK, D_MODEL), lambda i: (0, 0)), pl.BlockSpec(memory_space=pl.ANY)]
        dh_shapes = [jax.ShapeDtypeStruct((BLOCK, D_MODEL), F32), jax.ShapeDtypeStruct((t - BLOCK, D_MODEL), F32)]
        scratch = [pltpu.VMEM((2, tm, D_MODEL), F32), pltpu.SemaphoreType.DMA((2,))]
    outs = pl.pallas_call(
        body, name="in_proj_bwd", grid=(nt,),
        in_specs=[row(ATTN_W), row(KV_W), row(KV_W), row(3 * CONV_W), full(w), row(D_MODEL), row(D_MODEL), full(g),
                  row(2 * HEAD_DIM), row(2 * HEAD_DIM)],
        out_specs=dh_specs + [row(IN_W), pl.BlockSpec((1, D_MODEL), lambda i: (0, 0))],
        out_shape=dh_shapes + [jax.ShapeDtypeStruct((t, IN_W), BF16), jax.ShapeDtypeStruct((1, D_MODEL), F32)],
        scratch_shapes=scratch,
        compiler_params=_params("arbitrary"),
    )(dq, dk, dv, dbch, w, dh1, h, g, *tabs)
    return (tuple(outs[:2]) if split_lead else outs[0],) + tuple(outs[-2:])


class _Tiles:
    def __init__(self, t):
        self.tm = _row_tile(t, 640)
        self.ts = self.tm
        self.tabs = _rope_tables(t)
        self.bias = _attn_bias()


def _mixer_fwd(h, p, tl, lead=None):
    if lead is None:
        a, q, k, v, b, c, hc = _in_proj(h, p["mix_pre_g"], p["w_in"], tl.tabs, tl.ts)
    else:
        h, a, q, k, v, b, c, hc = _in_proj(h, p["mix_pre_g"], p["w_in"], tl.tabs, tl.ts, lead)
    o = _attn_fwd(q, k, v, tl.bias, p["sinks"], tl.tm)
    return (h, a, q, k, v, b, c, hc, o)


def _out_fwd(mixed, p, tl, deps=()):
    h, a, q, k, v, b, c, hc, o = mixed
    h1, y, z = _mix_out(h, o, b, c, hc, p["conv_w"], p["attn_out_g"], p["conv_out_g"], p["w_out"], p["mix_post_g"],
                        tl.ts, deps)
    return h1, mixed + (h1, y, z)


def _mlp_fwd(h1, saved, p, tl, target=None):
    h2, a2, slope, f = _mlp(h1, p["mlp_pre_g"], p["w_up"], p["w_down"], p["mlp_post_g"], tl.tm, target)
    return h2, saved + (a2, slope, f)


def _mlp_part_bwd(dh, saved, p, tl, deps=()):
    h1, a2, slope, f = saved[9], saved[12], saved[13], saved[14]
    if isinstance(f, tuple):
        df, dg2 = f
        dup = _mlp_bwd_hidden(None, None, None, slope, p["w_down"], tl.tm, deps, df)
    else:
        df, dup, dg2 = _mlp_bwd_hidden(dh, f, p["mlp_post_g"], slope, p["w_down"], tl.tm, deps)
    dh1, dg1 = _mlp_bwd_input(dup, p["w_up"], h1, p["mlp_pre_g"], dh, tl.tm)
    g = {"w_down": [d.reshape(N_CHIPS, FF_CHUNK, D_MODEL)
                    for d in _weight_grad(slope, df, "grad_w_down", x_is_slope=True)],
         "w_up": [d.reshape(N_CHIPS, D_MODEL, FF_CHUNK) for d in _weight_grad(a2, dup, "grad_w_up")],
         "mlp_post_g": dg2, "mlp_pre_g": dg1}
    return dh1, g


def _mix_out_part_bwd(dh1, saved, p, tl, deps=()):
    b, c, hc, o, y, z = saved[5], saved[6], saved[7], saved[8], saved[10], saved[11]
    dz, do, dbch, dgp, dga, dgc, dcw = _mix_out_bwd(dh1, z, p["mix_post_g"], p["w_out"], o, b, c, hc, p["conv_w"],
                                                    p["attn_out_g"], p["conv_out_g"], tl.ts, deps)
    g = {"w_out": [d.reshape(N_CHIPS, D_MODEL // N_CHIPS, D_MODEL) for d in _weight_grad(y, dz, "grad_w_out")],
         "mix_post_g": dgp, "attn_out_g": dga, "conv_out_g": dgc, "conv_w": dcw}
    return (dh1, do, dbch), g


def _attn_in_part_bwd(carry, saved, p, tl, deps=(), split_lead=False):
    dh1, do, dbch = carry
    h_in, q, k, v, o = saved[0], saved[2], saved[3], saved[4], saved[8]
    dq, dk, dv, dsink = _attn_bwd(q, k, v, o, do, tl.bias, p["sinks"], tl.tm, deps)
    dh, dproj, dgi = _in_proj_bwd(dq, dk, dv, dbch, p["w_in"], dh1, h_in, p["mix_pre_g"], tl.tabs, tl.ts, split_lead)
    return dh, dproj, {"mix_pre_g": dgi, "sinks": dsink[:, 0]}


def _in_grad(dproj, saved, deps=()):
    return [d.reshape(N_CHIPS, IN_W // N_CHIPS, D_MODEL) for d in _weight_grad(dproj, saved[1], "grad_w_in", deps=deps)]


def _place():
    return lax.axis_index("x"), lax.axis_index("y"), lax.axis_index("c")


def _other_chips(x, y):
    return [(1 - x, y), (x, 1 - y), (1 - x, 1 - y)]


_HBM = pl.BlockSpec(memory_space=pltpu.HBM)
_SEM = pl.BlockSpec(memory_space=pltpu.SEMAPHORE)
_EFFECT = pltpu.SideEffectType.DATAFLOW_SIDE_EFFECTING


class _Exchange:
    def __init__(self, name, bufs, plan, n, after=()):
        self.name, self.plan, nb = name, plan, len(bufs)
        n_in = nb + len(after)

        def body(*refs):
            send, recv, token = refs[n_in], refs[n_in + 1], refs[-1]
            for k, (src, dst, target, _) in enumerate(plan(refs[:nb])):
                pltpu.make_async_remote_copy(src_ref=src, dst_ref=dst, send_sem=send.at[k], recv_sem=recv.at[k],
                                             device_id=target, device_id_type=MESH).start()
            token[...] = jnp.zeros_like(token)

        outs = pl.pallas_call(
            body, name=name + "_start",
            out_shape=(pltpu.SemaphoreType.DMA((n,)), pltpu.SemaphoreType.DMA((n,)),
                       *[pltpu.HBM(b.shape, b.dtype) for b in bufs], jax.ShapeDtypeStruct((8, 128), F32)),
            in_specs=[_HBM] * nb + [pl.BlockSpec(memory_space=pl.ANY)] * len(after),
            out_specs=(_SEM, _SEM, *[_HBM] * nb, pl.BlockSpec(memory_space=pltpu.VMEM)),
            input_output_aliases={i: 2 + i for i in range(nb)},
            compiler_params=pltpu.CompilerParams(has_side_effects=_EFFECT),
        )(*[pltpu.with_memory_space_constraint(b, pltpu.HBM) for b in bufs], *after)
        self.send, self.recv, self.bufs, self.token = outs[0], outs[1], list(outs[2:2 + nb]), outs[-1]

    def wait(self, *after):
        plan, nb = self.plan, len(self.bufs)

        def body(*refs):
            send, recv = refs[nb], refs[nb + 1]
            for k, (src, _, target, land) in enumerate(plan(refs[:nb])):
                cp = pltpu.make_async_remote_copy(src_ref=src, dst_ref=land, send_sem=send.at[k], recv_sem=recv.at[k],
                                                  device_id=target, device_id_type=MESH)
                cp.wait_send()
                cp.wait_recv()

        outs = pl.pallas_call(
            body, name=self.name + "_wait", out_shape=[pltpu.HBM(b.shape, b.dtype) for b in self.bufs],
            in_specs=[_HBM] * nb + [_SEM, _SEM] + [pl.BlockSpec(memory_space=pl.ANY)] * len(after),
            out_specs=[_HBM] * nb, input_output_aliases={i: i for i in range(nb)},
            compiler_params=pltpu.CompilerParams(has_side_effects=_EFFECT),
        )(*self.bufs, self.send, self.recv, *after)
        return list(outs)


def _gather_plan(n):
    def plan(refs):
        x, y, c = _place()
        me = 2 * x + y
        return [(refs[a].at[me], refs[a].at[me], (px, py, c), refs[a].at[2 * px + py])
                for a in range(n) for px, py in _other_chips(x, y)]

    return plan


def _gather_half_plan(n, half_rows):
    def plan(refs):
        x, y, c = _place()
        me = 2 * x + y
        out = []
        for a in range(n):
            rows = pl.ds(c * half_rows[a], half_rows[a])
            out += [(refs[a].at[me, rows], refs[a].at[me, rows], (px, py, c), refs[a].at[2 * px + py, rows])
                    for px, py in _other_chips(x, y)]
        return out

    return plan


def _hand_over_plan(n, half_rows):
    def plan(refs):
        x, y, c = _place()
        out = []
        for a in range(n):
            mine, theirs = pl.ds(c * half_rows[a], half_rows[a]), pl.ds((1 - c) * half_rows[a], half_rows[a])
            for px, py in _other_chips(x, y):
                held = refs[a].at[2 * px + py, mine]
                out.append((held, held, (x, y, 1 - c), refs[a].at[2 * px + py, theirs]))
        return out

    return plan


def _peers():
    x, y, c = _place()
    return [(k - 1, (x ^ (k >> 2), y ^ ((k >> 1) & 1), c ^ (k & 1))) for k in range(1, N_DEV)]


def _scatter_plan(n, half_rows):
    def plan(refs):
        out = []
        for a in range(n):
            hr = half_rows[a]
            for k, (px, py, pc) in _peers():
                out.append((refs[a].at[2 * px + py, pl.ds(pc * hr, hr)], refs[n + a].at[k], (px, py, pc),
                            refs[n + a].at[k]))
        return out

    return plan


def _join_plan(n):
    def plan(refs):
        x, y, c = _place()
        return [(refs[a].at[c], refs[a].at[c], (x, y, 1 - c), refs[a].at[1 - c]) for a in range(n)]

    return plan


def _sum_parts(gs, qs):
    n = len(gs)
    half_rows = [g.shape[1] // 2 for g in gs]
    tr = [_block_rows(hr) for hr in half_rows]
    per = [hr // t for hr, t in zip(half_rows, tr)]
    x, y, c = _place()
    where = jnp.stack([2 * x + y, c]).astype(jnp.int32)

    def body(where_ref, *refs):
        i = pl.program_id(0)
        for a in range(n):
            g_ref, q_ref, o_ref = refs[a], refs[n + a], refs[2 * n + a]

            @pl.when(i < per[a])
            def _():
                total = g_ref[...]
                for k in range(N_DEV - 1):
                    total = total + q_ref[k].astype(F32)
                o_ref[...] = total

    def at(a, i):
        return jnp.minimum(i, per[a] - 1)

    specs_g = [pl.BlockSpec((None, tr[a], gs[a].shape[2]),
                            lambda i, where_ref, a=a: (where_ref[0], where_ref[1] * per[a] + at(a, i), 0)) for a in range(n)]
    specs_q = [pl.BlockSpec((N_DEV - 1, tr[a], gs[a].shape[2]), lambda i, where_ref, a=a: (0, at(a, i), 0))
               for a in range(n)]
    specs_o = [pl.BlockSpec((None, tr[a], gs[a].shape[2]), lambda i, where_ref, a=a: (where_ref[1], at(a, i), 0))
               for a in range(n)]
    return pl.pallas_call(
        body, name="sum_parts",
        grid_spec=pltpu.PrefetchScalarGridSpec(num_scalar_prefetch=1, grid=(max(per),), in_specs=specs_g + specs_q,
                                               out_specs=specs_o),
        out_shape=[jax.ShapeDtypeStruct((2, hr, g.shape[2]), F32) for g, hr in zip(gs, half_rows)],
        compiler_params=_params("arbitrary"),
    )(where, *gs, *qs)


def _all_plan(refs):
    x, y, c = _place()
    mine = refs[0].at[4 * x + 2 * y + c]
    return [(mine, mine, (px, py, pc), refs[0].at[4 * px + 2 * py + pc]) for _, (px, py, pc) in _peers()]


def _sum_devices(parts):
    def body(p_ref, o_ref):
        total = p_ref[0]
        for d in range(1, N_DEV):
            total = total + p_ref[d]
        o_ref[...] = total

    vm = pl.BlockSpec(memory_space=pltpu.VMEM)
    return pl.pallas_call(body, name="sum_devices", in_specs=[vm], out_specs=vm,
                          out_shape=jax.ShapeDtypeStruct(parts.shape[1:], F32))(parts)


def _adamw_math(w, g, m, v):
    m = ADAM_B1 * m + (1.0 - ADAM_B1) * g
    v = ADAM_B2 * v + (1.0 - ADAM_B2) * jnp.square(g)
    m_hat = m / (1.0 - ADAM_B1 ** ADAM_STEP)
    v_hat = v / (1.0 - ADAM_B2 ** ADAM_STEP)
    delta = -ADAM_LR * (m_hat / (jnp.sqrt(v_hat) + ADAM_EPS) + ADAM_WD * w)
    return delta, m, v


def _adamw_large(layer, ws, halves, ms, vs, others):
    n = len(ws)
    tr = [_block_rows(w.shape[1] // 2) for w in ws]
    per = [w.shape[1] // 2 // t for w, t in zip(ws, tr)]
    kept = [] if others is None else [a for four in others for a in four]

    def body(*refs):
        i = pl.program_id(0)
        outs = refs[4 * n + len(kept):]
        for a in range(n):
            w_ref, g_ref, m_ref, v_ref = refs[a], refs[n + a], refs[2 * n + a], refs[3 * n + a]
            g_out, d_ref, nm_ref, nv_ref = outs[4 * a:4 * a + 4]

            @pl.when(i < 2 * per[a])
            def _():
                g = g_ref[...]
                g_out[...] = g
                d_ref[...], nm_ref[...], nv_ref[...] = _adamw_math(w_ref[...], g, m_ref[...], v_ref[...])

    def at(a, i):
        return jnp.minimum(i, 2 * per[a] - 1)

    blk = [pl.BlockSpec((None, tr[a], ws[a].shape[2]), lambda i, a=a: (layer, at(a, i), 0)) for a in range(n)]
    half = [pl.BlockSpec((None, tr[a], ws[a].shape[2]), lambda i, a=a: (at(a, i) // per[a], at(a, i) % per[a], 0))
            for a in range(n)]
    outs = pl.pallas_call(
        body, name="adamw_large", grid=(2 * max(per),),
        in_specs=blk + half + blk + blk + [pl.BlockSpec(memory_space=pl.ANY)] * len(kept),
        out_specs=[blk[a] for a in range(n) for _ in range(4)],
        out_shape=[jax.ShapeDtypeStruct(w.shape, F32) for w in ws for _ in range(4)],
        input_output_aliases={4 * n + k: k for k in range(len(kept))},
        compiler_params=_params("arbitrary"),
    )(*ws, *halves, *ms, *vs, *kept)
    return [outs[4 * a:4 * a + 4] for a in range(n)]


def _adamw_small(ws, gs, ms, vs):
    n = len(ws)

    def body(*refs):
        w_r, g_r, m_r, v_r = refs[:n], refs[n:2 * n], refs[2 * n:3 * n], refs[3 * n:4 * n]
        d_r, nm_r, nv_r = refs[4 * n:5 * n], refs[5 * n:6 * n], refs[6 * n:]
        for a in range(n):
            d_r[a][...], nm_r[a][...], nv_r[a][...] = _adamw_math(w_r[a][...], g_r[a][...], m_r[a][...], v_r[a][...])

    vm = pl.BlockSpec(memory_space=pltpu.VMEM)
    outs = pl.pallas_call(
        body, name="adamw_small", in_specs=[vm] * (4 * n), out_specs=[vm] * (3 * n),
        out_shape=[jax.ShapeDtypeStruct(w.shape, F32) for w in ws] * 3,
    )(*ws, *gs, *ms, *vs)
    return outs[:n], outs[n:2 * n], outs[2 * n:]


_LARGE = ("w_in", "w_out", "w_up", "w_down")
_SMALL = ("meta_tokens", "mix_pre_g", "conv_w", "sinks", "attn_out_g", "conv_out_g", "mix_post_g", "mlp_pre_g",
          "mlp_post_g")
_ORDER = ("meta_tokens", "mix_pre_g", "w_in", "conv_w", "sinks", "attn_out_g", "conv_out_g", "w_out", "mix_post_g",
          "mlp_pre_g", "w_up", "w_down", "mlp_post_g")


class _Reduce:
    def __init__(self, name, grads, after=()):
        self.name, self.n = name, len(grads)
        self.own = [g for g, _ in grads]
        half_rows = [g.shape[1] // 2 for g in self.own]
        zones = [lax.empty((N_DEV - 1, hr, g.shape[2]), BF16) for g, hr in zip(self.own, half_rows)]
        self.exchange = _Exchange(name + "_scatter", [b for _, b in grads] + zones, _scatter_plan(self.n, half_rows),
                                  (N_DEV - 1) * self.n, after)

    @property
    def token(self):
        return self.exchange.token

    def join(self, *after):
        bufs = self.exchange.wait(*after)
        halves = list(_sum_parts(self.own, bufs[self.n:]))
        self.exchange = _Exchange(self.name + "_join", halves, _join_plan(self.n), self.n)

    def done(self, *after):
        return self.exchange.wait(*after)


def _pad_cols(a, n=D_MODEL):
    return jnp.pad(a, ((0, 0), (0, n - a.shape[1])))


def kernel(x, meta_tokens, mix_pre_g, w_in, conv_w, sinks, attn_out_g, conv_out_g, w_out, mix_post_g, mlp_pre_g, w_up, w_down, mlp_post_g, loss_target, m_meta_tokens, m_mix_pre_g, m_w_in, m_conv_w, m_sinks, m_attn_out_g, m_conv_out_g, m_w_out, m_mix_post_g, m_mlp_pre_g, m_w_up, m_w_down, m_mlp_post_g, v_meta_tokens, v_mix_pre_g, v_w_in, v_conv_w, v_sinks, v_attn_out_g, v_conv_out_g, v_w_out, v_mix_post_g, v_mlp_pre_g, v_w_up, v_w_down, v_mlp_post_g):
    w = dict(meta_tokens=meta_tokens, mix_pre_g=mix_pre_g, w_in=w_in, conv_w=conv_w, sinks=sinks,
             attn_out_g=attn_out_g, conv_out_g=conv_out_g, w_out=w_out, mix_post_g=mix_post_g, mlp_pre_g=mlp_pre_g,
             w_up=w_up, w_down=w_down, mlp_post_g=mlp_post_g)
    m = dict(meta_tokens=m_meta_tokens, mix_pre_g=m_mix_pre_g, w_in=m_w_in, conv_w=m_conv_w, sinks=m_sinks,
             attn_out_g=m_attn_out_g, conv_out_g=m_conv_out_g, w_out=m_w_out, mix_post_g=m_mix_post_g,
             mlp_pre_g=m_mlp_pre_g, w_up=m_w_up, w_down=m_w_down, mlp_post_g=m_mlp_post_g)
    v = dict(meta_tokens=v_meta_tokens, mix_pre_g=v_mix_pre_g, w_in=v_w_in, conv_w=v_conv_w, sinks=v_sinks,
             attn_out_g=v_attn_out_g, conv_out_g=v_conv_out_g, w_out=v_w_out, mix_post_g=v_mix_post_g,
             mlp_pre_g=v_mlp_pre_g, w_up=v_w_up, w_down=v_w_down, mlp_post_g=v_mlp_post_g)
    chip = 2 * lax.axis_index("x") + lax.axis_index("y")
    tl = _Tiles(x.shape[1] + BLOCK)

    def zone(quarter):
        return lax.dynamic_update_slice(lax.empty((N_CHIPS,) + quarter.shape, quarter.dtype), quarter[None],
                                        (chip,) + (0,) * quarter.ndim)

    w, m, v = ({**d, "w_in": jnp.swapaxes(d["w_in"], 1, 2)} for d in (w, m, v))
    zones = {n: [zone(w[n][l].astype(BF16)) for l in range(DEPTH)] for n in _LARGE}
    first = _Exchange("gather_first", [zones["w_in"][0], zone(w["conv_w"]), zone(w["meta_tokens"])], _gather_plan(3), 9)
    out0 = _Exchange("gather_out", [zones["w_out"][0]], _gather_plan(1), 3, [first.token])
    mlp_halves = [D_MODEL // 2, FF_CHUNK // 2]
    rest = _Exchange("gather_rest", [zones[n][0] for n in ("w_up", "w_down")], _gather_half_plan(2, mlp_halves), 6,
                     [out0.token])

    def whole_in(quarters):
        return quarters.reshape(IN_W, D_MODEL)

    q_in, q_conv, q_meta = first.wait(rest.token, *tl.tabs, tl.bias)
    conv_whole = jnp.transpose(q_conv, (1, 2, 0, 3)).reshape(DEPTH, CONV_K, CONV_W)
    meta = jnp.transpose(q_meta, (1, 0, 2)).reshape(N_META, D_MODEL)
    p = [{"conv_w": conv_whole[l], "sinks": w["sinks"][l]} for l in range(DEPTH)]
    for l in range(DEPTH):
        for n in ("mix_pre_g", "attn_out_g", "conv_out_g", "mix_post_g", "mlp_pre_g", "mlp_post_g"):
            p[l][n] = w[n][l][None, :]

    lead = jnp.concatenate([jnp.zeros((LEAD_PAD, D_MODEL), F32), meta], axis=0)
    p[0]["w_in"] = whole_in(q_in)
    mixed = _mixer_fwd(x[0], p[0], tl, lead)
    second = _Exchange("gather_second", [zones["w_in"][1], zones["w_out"][1]], _gather_plan(2), 6, [mixed[-1]])
    second_mlp = _Exchange("gather_second_mlp", [zones["w_up"][1], zones["w_down"][1]], _gather_plan(2), 6,
                           [second.token])
    hand_over = _Exchange("hand_over_rest", rest.wait(second_mlp.token), _hand_over_plan(2, mlp_halves), 6)
    p[0]["w_out"], = out0.wait(hand_over.token)
    h1, saved0 = _out_fwd(mixed, p[0], tl)
    p[0]["w_up"], p[0]["w_down"] = hand_over.wait(h1)
    h, saved0 = _mlp_fwd(h1, saved0, p[0], tl)
    q_in, p[1]["w_out"] = second.wait(h)
    p[1]["w_in"] = whole_in(q_in)
    h1, saved1 = _out_fwd(_mixer_fwd(h, p[1], tl), p[1], tl)
    p[1]["w_up"], p[1]["w_down"] = second_mlp.wait(h1)
    (loss_tile, dh), saved1 = _mlp_fwd(h1, saved1, p[1], tl, loss_target[0])

    def adamw(layer, halves, other):
        names = list(halves)
        done = _adamw_large(layer, [w[n] for n in names], [halves[n] for n in names], [m[n] for n in names],
                            [v[n] for n in names], None if other is None else [other[n] for n in names])
        return dict(zip(names, done))

    dh1, g1 = _mlp_part_bwd(dh, saved1, p[1], tl)
    carry, gm = _mix_out_part_bwd(dh1, saved1, p[1], tl)
    dh, dproj, gi = _attn_in_part_bwd(carry, saved1, p[1], tl)
    g1.update(gm, w_in=_in_grad(dproj, saved1), **gi)
    red1 = _Reduce("reduce1", [g1[n] for n in _LARGE])
    dh1, g0 = _mlp_part_bwd(dh, saved0, p[0], tl, [red1.token])
    red1.join(g0["w_down"][0])
    carry, gm = _mix_out_part_bwd(dh1, saved0, p[0], tl, [red1.token])
    first0 = ("w_up", "w_down", "w_out")
    g0.update(gm)
    red0a = _Reduce("reduce0a", [g0[n] for n in first0])
    (dlead, dseq), dproj, gi = _attn_in_part_bwd(carry, saved0, p[0], tl, [red0a.token], split_lead=True)
    g0.update(gi)
    grad_x = dseq[None]
    grads = {n: [g0[n], g1[n]] for n in g0 if n not in _LARGE}

    rows = [dlead[LEAD_PAD:]]
    for n in ("mix_pre_g", "mix_post_g", "mlp_pre_g", "mlp_post_g"):
        rows += grads[n]
    rows += [jnp.concatenate([grads["attn_out_g"][l], grads["conv_out_g"][l]], axis=1) for l in range(DEPTH)]
    rows.append(jnp.concatenate(grads["conv_w"], axis=1))
    rows.append(_pad_cols(jnp.concatenate(grads["sinks"])[None, :]))
    rows.append(_pad_cols(loss_tile[:1]))
    packed = jnp.concatenate(rows, axis=0)
    packed = jnp.pad(packed, ((0, SMALL_ROWS - packed.shape[0]), (0, 0)))
    device = 2 * chip + lax.axis_index("c")
    small_parts = _Exchange("gather_small", [lax.dynamic_update_slice(lax.empty((N_DEV,) + packed.shape, F32),
                                                                      packed[None], (device, 0, 0))], _all_plan, N_DEV - 1)
    g0["w_in"] = _in_grad(dproj, saved0, [small_parts.token])
    red0b = _Reduce("reduce0b", [g0["w_in"]])
    done1 = adamw(1, dict(zip(_LARGE, red1.done(red0b.token))), None)
    total = _sum_devices(small_parts.wait(*[done1[n][0] for n in _LARGE])[0])
    r0 = N_META
    small = {
        "meta_tokens": lax.dynamic_slice(total[:N_META], (0, chip * (D_MODEL // N_CHIPS)), (N_META, D_MODEL // N_CHIPS)),
        "mix_pre_g": total[r0:r0 + 2], "mix_post_g": total[r0 + 2:r0 + 4], "mlp_pre_g": total[r0 + 4:r0 + 6],
        "mlp_post_g": total[r0 + 6:r0 + 8],
        "attn_out_g": total[r0 + 8:r0 + 10, :ATTN_W], "conv_out_g": total[r0 + 8:r0 + 10, ATTN_W:],
        "conv_w": lax.dynamic_slice(total[r0 + 10:r0 + 13].reshape(CONV_K, DEPTH, CONV_W).transpose(1, 0, 2),
                                    (0, 0, chip * (CONV_W // N_CHIPS)), (DEPTH, CONV_K, CONV_W // N_CHIPS)),
        "sinks": total[r0 + 13, :DEPTH * N_Q_HEADS].reshape(DEPTH, N_Q_HEADS),
    }
    loss = total[r0 + 14, 0]

    ds, nms, nvs = _adamw_small([w[n] for n in _SMALL], [small[n] for n in _SMALL], [m[n] for n in _SMALL],
                                [v[n] for n in _SMALL])
    red0a.join(ds[0], grad_x)
    red0b.join(red0a.token)
    done0 = adamw(0, dict(zip(first0, red0a.done(red0b.token))), done1)
    done0.update(adamw(0, {"w_in": red0b.done(done0["w_down"][0])[0]}, done1))
    grad, delta, new_m, new_v = {}, {}, {}, {}
    for n in _LARGE:
        grad[n], delta[n], new_m[n], new_v[n] = done0[n]
    for d in (grad, delta, new_m, new_v):
        d["w_in"] = jnp.swapaxes(d["w_in"], 1, 2)
    for i, n in enumerate(_SMALL):
        grad[n], delta[n], new_m[n], new_v[n] = small[n], ds[i], nms[i], nvs[i]
    return (loss, grad_x, *[grad[n] for n in _ORDER], *[delta[n] for n in _ORDER], *[new_m[n] for n in _ORDER],
            *[new_v[n] for n in _ORDER])
```

```python
import jax
import jax.numpy as jnp
from jax import lax
from jax.experimental import pallas as pl
from jax.experimental.pallas import tpu as pltpu

F32 = jnp.float32
BF16 = jnp.bfloat16

D_MODEL = 1024
DEPTH = 2
N_META = 16
ATTN_W = 512
CONV_W = 512
HEAD_DIM = 64
N_Q_HEADS = 8
N_KV_HEADS = 2
GROUP = N_Q_HEADS // N_KV_HEADS
KV_W = N_KV_HEADS * HEAD_DIM
CONV_K = 3
BLOCK = 128
LEAD_PAD = BLOCK - N_META
ROPE_THETA = 500000.0
ROT_DIM = HEAD_DIM // 4
ROT_HALF = ROT_DIM // 2
D_FF = 4 * D_MODEL
IN_W = ATTN_W + 2 * KV_W + 3 * CONV_W
QKV_W = ATTN_W + 2 * KV_W
EPS = 1e-6
SCALE = HEAD_DIM ** -0.5
FF_CHUNK = 1024
N_CHIPS = 4
N_DEV = 8

ADAM_LR = 0.001
ADAM_B1 = 0.9
ADAM_B2 = 0.999
ADAM_EPS = 1e-08
ADAM_WD = 0.01
ADAM_STEP = 10

V7X_VMEM_LIMIT = 60 * 1024 * 1024
SMALL_ROWS = 32

MESH = pl.DeviceIdType.MESH


def _params(*sem):
    return pltpu.CompilerParams(dimension_semantics=sem, vmem_limit_bytes=V7X_VMEM_LIMIT)


def _block_rows(n):
    return max(r for r in range(16, min(n, 128) + 1, 16) if n % r == 0)


def _row_tile(t, most):
    nb = t // BLOCK
    for b in range(most // BLOCK, 0, -1):
        if nb % b == 0:
            return b * BLOCK
    return BLOCK


def _behind(body, deps):
    n = len(deps)

    def wrapped(*refs):
        body(*refs[n:])

    return wrapped, [pl.BlockSpec(memory_space=pl.ANY)] * n


def _rms(x, g):
    r = lax.rsqrt(jnp.mean(x * x, axis=-1, keepdims=True) + EPS)
    return x * r * g


def _rms_bwd(dy, x, g):
    r = lax.rsqrt(jnp.mean(x * x, axis=-1, keepdims=True) + EPS)
    xh = x * r
    dg = jnp.sum(dy * xh, axis=0, keepdims=True)
    dxh = dy * g
    dx = r * (dxh - xh * jnp.mean(dxh * xh, axis=-1, keepdims=True))
    return dx, dg


def _rope(x, cos, sa, sb):
    n = x.shape[-1]
    return x * cos + pltpu.roll(x, n - ROT_HALF, 1) * sa + pltpu.roll(x, ROT_HALF, 1) * sb


def _rope_bwd(dy, cos, sa, sb):
    n = dy.shape[-1]
    return dy * cos + pltpu.roll(dy * sa, ROT_HALF, 1) + pltpu.roll(dy * sb, n - ROT_HALF, 1)


def _rope_tables(t):
    pos = lax.broadcasted_iota(jnp.int32, (t, ROT_HALF), 0).astype(F32) - LEAD_PAD
    pair = lax.broadcasted_iota(jnp.int32, (t, ROT_HALF), 1).astype(F32)
    inv_freq = jnp.power(jnp.float32(ROPE_THETA), -(2.0 * pair) / ROT_DIM)
    ang = pos * inv_freq
    cos, sin = lax.optimization_barrier((jnp.cos(ang), jnp.sin(ang)))
    spread = (1, 2 * HEAD_DIM // ROT_HALF)
    cos, sin = jnp.tile(cos, spread), jnp.tile(sin, spread)
    dim = lax.broadcasted_iota(jnp.int32, (t, 2 * HEAD_DIM), 1) % HEAD_DIM
    return jnp.where(dim < ROT_DIM, cos, 1.0), jnp.where(dim < ROT_DIM, sin, 0.0)


def _rope_factors(cos, sin):
    dim = lax.broadcasted_iota(jnp.int32, sin.shape, 1) % HEAD_DIM
    return cos, jnp.where(dim < ROT_HALF, -sin, 0.0), jnp.where(dim >= ROT_HALF, sin, 0.0)


def _in_proj(h, g, w, tabs, tm, lead=None):
    t = h.shape[0] + (0 if lead is None else BLOCK)
    per_step = 0 if lead is None else tm // BLOCK

    def body(*refs):
        if lead is None:
            x = refs[0][...]
            refs = refs[1:]
        else:
            blocks = [r[...] for r in refs[1:1 + per_step]]
            blocks[0] = jnp.where(pl.program_id(0) == 0, refs[0][...], blocks[0])
            x = jnp.concatenate(blocks, axis=0)
            first_out = 1 + per_step + 4
            refs[first_out][...] = x
            refs = refs[1 + per_step:first_out] + refs[first_out + 1:]
        g_ref, w_ref, c_ref, s_ref, a_ref, q_ref, k_ref, v_ref, b_ref, cg_ref, hc_ref = refs
        a = _rms(x, g_ref[...]).astype(BF16)
        a_ref[...] = a
        p = lax.dot_general(a, w_ref[...], (((1,), (1,)), ((), ())), preferred_element_type=F32)
        cos, sa, sb = _rope_factors(c_ref[...], s_ref[...])
        rep = ATTN_W // (2 * HEAD_DIM)
        q = _rope(p[:, :ATTN_W], jnp.tile(cos, (1, rep)), jnp.tile(sa, (1, rep)), jnp.tile(sb, (1, rep)))
        q_ref[...] = (q * SCALE).astype(BF16)
        k_ref[...] = _rope(p[:, ATTN_W:ATTN_W + KV_W], cos, sa, sb).astype(BF16)
        v_ref[...] = p[:, ATTN_W + KV_W:QKV_W].astype(BF16)
        b_ref[...] = p[:, QKV_W:QKV_W + CONV_W].astype(BF16)
        cg_ref[...] = p[:, QKV_W + CONV_W:QKV_W + 2 * CONV_W].astype(BF16)
        hc_ref[...] = p[:, QKV_W + 2 * CONV_W:].astype(BF16)

    row = lambda n: pl.BlockSpec((tm, n), lambda i: (i, 0))
    full = lambda a: pl.BlockSpec(a.shape, lambda i: (0, 0))

    def sequence_block(b):
        return pl.BlockSpec((BLOCK, D_MODEL), lambda i: (jnp.maximum(i * per_step + b - 1, 0), 0))

    if lead is None:
        first_in, first_args, first_out, first_shape = [row(D_MODEL)], [h], [], []
    else:
        first_in = [full(lead)] + [sequence_block(b) for b in range(per_step)]
        first_args = [lead] + [h] * per_step
        first_out, first_shape = [row(D_MODEL)], [jax.ShapeDtypeStruct((t, D_MODEL), F32)]
    return pl.pallas_call(
        body, name="in_proj", grid=(t // tm,),
        in_specs=first_in + [full(g), full(w), row(2 * HEAD_DIM), row(2 * HEAD_DIM)],
        out_specs=first_out + [row(D_MODEL), row(ATTN_W), row(KV_W), row(KV_W), row(CONV_W), row(CONV_W), row(CONV_W)],
        out_shape=first_shape + [jax.ShapeDtypeStruct((t, D_MODEL), BF16), jax.ShapeDtypeStruct((t, ATTN_W), BF16),
                                 jax.ShapeDtypeStruct((t, KV_W), BF16), jax.ShapeDtypeStruct((t, KV_W), BF16),
                                 jax.ShapeDtypeStruct((t, CONV_W), BF16), jax.ShapeDtypeStruct((t, CONV_W), BF16),
                                 jax.ShapeDtypeStruct((t, CONV_W), BF16)],
        compiler_params=_params("parallel"),
    )(*first_args, g, w, *tabs)


def _attn_bias():
    r = lax.broadcasted_iota(jnp.int32, (3, BLOCK, 2 * BLOCK), 1)
    c = lax.broadcasted_iota(jnp.int32, (3, BLOCK, 2 * BLOCK), 2)
    i = lax.broadcasted_iota(jnp.int32, (3, BLOCK, 2 * BLOCK), 0)
    ok = (c > r) & (c <= r + BLOCK) & (c + (i - 1) * BLOCK >= LEAD_PAD)
    return jnp.where(ok, 0.0, -jnp.inf).astype(F32)


def _attn_scores(qh, kg, bias):
    return lax.dot_general(qh, kg, (((1,), (1,)), ((), ())), preferred_element_type=F32) + bias


def _attn_probs(s, sk):
    m = jnp.maximum(jnp.max(s, axis=-1, keepdims=True), sk)
    e = jnp.exp(s - m)
    es = jnp.exp(sk - m)
    rden = 1.0 / (jnp.sum(e, axis=-1, keepdims=True) + es)
    return e * rden, es * rden


def _head(hh):
    return slice(hh * HEAD_DIM, (hh + 1) * HEAD_DIM)


def _two_blocks(ref, i):
    prev = jnp.maximum(i - 1, 0)
    return jnp.concatenate([ref[pl.ds(pl.multiple_of(prev * BLOCK, BLOCK), BLOCK), :],
                            ref[pl.ds(pl.multiple_of(i * BLOCK, BLOCK), BLOCK), :]], axis=0)


def _attn_fwd(q, k, v, bias, sinks, tm):
    t = q.shape[0]
    per_step = tm // BLOCK
    heads = range(N_Q_HEADS)

    def body(s_ref, q_ref, k_ref, v_ref, bias_ref, o_ref):
        for b in range(per_step):
            i = pl.program_id(0) * per_step + b
            rows = slice(b * BLOCK, (b + 1) * BLOCK)
            kc, vc = _two_blocks(k_ref, i), _two_blocks(v_ref, i)
            bias_i = bias_ref[jnp.minimum(i, 2)]
            scores = [_attn_scores(q_ref[rows, _head(hh)], kc[:, _head(hh // GROUP)], bias_i) for hh in heads]
            probs = [_attn_probs(scores[hh], s_ref[hh])[0].astype(BF16) for hh in heads]
            for hh in heads:
                o_ref[rows, _head(hh)] = jnp.dot(probs[hh], vc[:, _head(hh // GROUP)],
                                                 preferred_element_type=F32).astype(BF16)

    whole = pl.BlockSpec((t, KV_W), lambda i: (0, 0))
    return pl.pallas_call(
        body, name="attn_fwd", grid=(t // tm,),
        in_specs=[pl.BlockSpec(memory_space=pltpu.SMEM), pl.BlockSpec((tm, ATTN_W), lambda i: (i, 0)), whole, whole,
                  pl.BlockSpec(bias.shape, lambda i: (0, 0, 0))],
        out_specs=pl.BlockSpec((tm, ATTN_W), lambda i: (i, 0)),
        out_shape=jax.ShapeDtypeStruct((t, ATTN_W), BF16),
        compiler_params=_params("parallel"),
    )(sinks, q, k, v, bias)


def _shift_rows(u, halo, n):
    r = pltpu.roll(u, n, 0)
    hr = pltpu.roll(halo, n, 0)
    idx = lax.broadcasted_iota(jnp.int32, hr.shape, 0)
    return jnp.concatenate([jnp.where(idx < n, hr, r[:8]), r[8:]], axis=0)


def _advance_rows(u, halo, n):
    rows = u.shape[0]
    r = pltpu.roll(u, rows - n, 0)
    hr = pltpu.roll(halo, 8 - n, 0)
    idx = lax.broadcasted_iota(jnp.int32, hr.shape, 0)
    return jnp.concatenate([r[:rows - 8], jnp.where(idx >= 8 - n, hr, r[rows - 8:])], axis=0)


def _mix_out(h, o, b, c, hc, cw, ga, gc, w, gp, tm, deps=()):
    t = h.shape[0]

    def body(h_ref, o_ref, b_ref, c_ref, hc_ref, cw_ref, ga_ref, gc_ref, w_ref, gp_ref, h1_ref, y_ref, z_ref, halo):
        @pl.when(pl.program_id(0) == 0)
        def _():
            halo[...] = jnp.zeros_like(halo)

        u = c_ref[...].astype(F32) * hc_ref[...].astype(F32)
        cv = cw_ref[0:1, :] * _shift_rows(u, halo[...], 2) + cw_ref[1:2, :] * _shift_rows(u, halo[...], 1) \
            + cw_ref[2:3, :] * u
        halo[...] = u[tm - 8:]
        yc = b_ref[...].astype(F32) * cv
        y = jnp.concatenate([_rms(o_ref[...].astype(F32), ga_ref[...]), _rms(yc, gc_ref[...])], axis=1).astype(BF16)
        y_ref[...] = y
        z = jnp.dot(y, w_ref[...].reshape(D_MODEL, D_MODEL), preferred_element_type=F32)
        z_ref[...] = z
        h1_ref[...] = h_ref[...] + _rms(z, gp_ref[...])

    row = lambda n: pl.BlockSpec((tm, n), lambda i: (i, 0))
    full = lambda a: pl.BlockSpec(a.shape, lambda i: (0,) * a.ndim)
    body, dep_specs = _behind(body, deps)
    return pl.pallas_call(
        body, name="mix_out", grid=(t // tm,),
        in_specs=dep_specs + [row(D_MODEL), row(ATTN_W), row(CONV_W), row(CONV_W), row(CONV_W), full(cw), full(ga),
                              full(gc), full(w), full(gp)],
        out_specs=[row(D_MODEL), row(D_MODEL), row(D_MODEL)],
        out_shape=[jax.ShapeDtypeStruct((t, D_MODEL), F32), jax.ShapeDtypeStruct((t, D_MODEL), BF16),
                   jax.ShapeDtypeStruct((t, D_MODEL), F32)],
        scratch_shapes=[pltpu.VMEM((8, CONV_W), F32)],
        compiler_params=_params("arbitrary"),
    )(*deps, h, o, b, c, hc, cw, ga, gc, w, gp)


def _mlp(h1, g1, wu, wd, g2, tm, target=None):
    t = h1.shape[0]
    nj = D_FF // FF_CHUNK
    per_step = tm // BLOCK if target is not None else 0

    def body(h1_ref, g1_ref, wu_ref, wd_ref, g2_ref, *rest):
        t_refs, outs = rest[:per_step], rest[per_step:]
        a2_ref, slope_ref = (outs[1], outs[2]) if target is None else (outs[2], outs[3])
        a2 = _rms(h1_ref[...], g1_ref[...]).astype(BF16)
        a2_ref[...] = a2
        f = None
        for j in range(nj):
            up = jnp.dot(a2, wu_ref[j], preferred_element_type=F32)
            r = jnp.maximum(up, 0.0)
            slope_ref[:, j * FF_CHUNK:(j + 1) * FF_CHUNK] = (r + r).astype(BF16)
            part = jnp.dot((r * r).astype(BF16), wd_ref[j], preferred_element_type=F32)
            f = part if f is None else f + part
        h2 = h1_ref[...] + _rms(f, g2_ref[...])
        if target is None:
            outs[0][...] = h2
            outs[3][...] = f
            return
        loss_ref, dh_ref, df_ref, dg2_ref = outs[0], outs[1], outs[4], outs[5]
        i = pl.program_id(0)

        @pl.when(i == 0)
        def _():
            loss_ref[...] = jnp.zeros_like(loss_ref)
            dg2_ref[...] = jnp.zeros_like(dg2_ref)

        total = jnp.zeros((), F32)
        dh2 = []
        for b in range(per_step):
            err = h2[b * BLOCK:(b + 1) * BLOCK] - t_refs[b][...]
            if b == 0:
                err = jnp.where(i == 0, 0.0, err)
            dh2.append(err * (1.0 / D_MODEL))
            total = total + jnp.sum(err * err)
        loss_ref[...] += total * (0.5 / D_MODEL)
        dh2 = jnp.concatenate(dh2, axis=0)
        dh_ref[...] = dh2
        df, dg = _rms_bwd(dh2, f, g2_ref[...])
        df_ref[...] = df.astype(BF16)
        dg2_ref[...] += dg

    def target_block(b):
        return pl.BlockSpec((BLOCK, D_MODEL), lambda i: (jnp.maximum(i * per_step + b - 1, 0), 0))

    row = pl.BlockSpec((tm, D_MODEL), lambda i: (i, 0))
    vec = pl.BlockSpec((1, D_MODEL), lambda i: (0, 0))
    resident = pl.BlockSpec(memory_space=pltpu.VMEM)
    wide = pl.BlockSpec((tm, D_FF), lambda i: (i, 0))
    kept = [jax.ShapeDtypeStruct((t, D_MODEL), BF16), jax.ShapeDtypeStruct((t, D_FF), BF16)]
    if target is None:
        specs = [row, row, wide, row]
        shapes = [jax.ShapeDtypeStruct((t, D_MODEL), F32)] + kept + [jax.ShapeDtypeStruct((t, D_MODEL), F32)]
    else:
        specs = [pl.BlockSpec((8, 128), lambda i: (0, 0)), row, row, wide, row, vec]
        shapes = [jax.ShapeDtypeStruct((8, 128), F32), jax.ShapeDtypeStruct((t, D_MODEL), F32)] + kept \
            + [jax.ShapeDtypeStruct((t, D_MODEL), BF16), jax.ShapeDtypeStruct((1, D_MODEL), F32)]
    outs = pl.pallas_call(
        body, name="mlp", grid=(t // tm,),
        in_specs=[row, vec, resident, resident, vec] + [target_block(b) for b in range(per_step)],
        out_specs=specs, out_shape=shapes,
        compiler_params=_params("parallel" if target is None else "arbitrary"),
    )(h1, g1, wu, wd, g2, *([target] * per_step))
    if target is None:
        return tuple(outs)
    return (tuple(outs[:2]), outs[2], outs[3], tuple(outs[4:]))


def _mlp_bwd_hidden(dh2, f, g2, slope, wd, tm, deps=(), df=None):
    t = slope.shape[0]
    nj = D_FF // FF_CHUNK

    def hidden(df, slope_ref, wd_ref, dup_ref):
        for j in range(nj):
            cols = slice(j * FF_CHUNK, (j + 1) * FF_CHUNK)
            dact = lax.dot_general(df, wd_ref[j], (((1,), (1,)), ((), ())), preferred_element_type=F32)
            dup_ref[:, cols] = (dact * slope_ref[:, cols].astype(F32)).astype(BF16)

    def body(dh2_ref, f_ref, g2_ref, slope_ref, wd_ref, df_ref, dup_ref, dg2_ref):
        @pl.when(pl.program_id(0) == 0)
        def _():
            dg2_ref[...] = jnp.zeros_like(dg2_ref)

        df, dg = _rms_bwd(dh2_ref[...], f_ref[...], g2_ref[...])
        dg2_ref[...] += dg
        df = df.astype(BF16)
        df_ref[...] = df
        hidden(df, slope_ref, wd_ref, dup_ref)

    def body_from_df(df_ref, slope_ref, wd_ref, dup_ref):
        hidden(df_ref[...], slope_ref, wd_ref, dup_ref)

    row = pl.BlockSpec((tm, D_MODEL), lambda i: (i, 0))
    wide = pl.BlockSpec((tm, D_FF), lambda i: (i, 0))
    vec = pl.BlockSpec((1, D_MODEL), lambda i: (0, 0))
    resident = pl.BlockSpec(memory_space=pltpu.VMEM)
    if df is not None:
        body_from_df, dep_specs = _behind(body_from_df, deps)
        return pl.pallas_call(
            body_from_df, name="mlp_bwd_hidden", grid=(t // tm,), in_specs=dep_specs + [row, wide, resident],
            out_specs=wide, out_shape=jax.ShapeDtypeStruct((t, D_FF), BF16), compiler_params=_params("parallel"),
        )(*deps, df, slope, wd)
    body, dep_specs = _behind(body, deps)
    return pl.pallas_call(
        body, name="mlp_bwd_hidden", grid=(t // tm,),
        in_specs=dep_specs + [row, row, vec, wide, resident],
        out_specs=[row, wide, vec],
        out_shape=[jax.ShapeDtypeStruct((t, D_MODEL), BF16), jax.ShapeDtypeStruct((t, D_FF), BF16),
                   jax.ShapeDtypeStruct((1, D_MODEL), F32)],
        compiler_params=_params("arbitrary"),
    )(*deps, dh2, f, g2, slope, wd)


def _mlp_bwd_input(dup, wu, h1, g1, dh2, tm):
    t = dh2.shape[0]
    nj = D_FF // FF_CHUNK

    def body(dup_ref, wu_ref, h1_ref, g1_ref, dh2_ref, dh1_ref, dg1_ref):
        @pl.when(pl.program_id(0) == 0)
        def _():
            dg1_ref[...] = jnp.zeros_like(dg1_ref)

        da2 = None
        for j in range(nj):
            part = lax.dot_general(dup_ref[:, j * FF_CHUNK:(j + 1) * FF_CHUNK], wu_ref[j], (((1,), (1,)), ((), ())),
                                   preferred_element_type=F32)
            da2 = part if da2 is None else da2 + part
        dx, dg = _rms_bwd(da2, h1_ref[...], g1_ref[...])
        dh1_ref[...] = dh2_ref[...] + dx
        dg1_ref[...] += dg

    row = pl.BlockSpec((tm, D_MODEL), lambda i: (i, 0))
    vec = pl.BlockSpec((1, D_MODEL), lambda i: (0, 0))
    return pl.pallas_call(
        body, name="mlp_bwd_input", grid=(t // tm,),
        in_specs=[pl.BlockSpec((tm, D_FF), lambda i: (i, 0)), pl.BlockSpec(memory_space=pltpu.VMEM), row, vec, row],
        out_specs=[row, vec],
        out_shape=[jax.ShapeDtypeStruct((t, D_MODEL), F32), jax.ShapeDtypeStruct((1, D_MODEL), F32)],
        compiler_params=_params("arbitrary"),
    )(dup, wu, h1, g1, dh2)


def _row_split(t):
    tile = min(t, 1024)
    return tile, t // tile, t % tile


def _row_split_specs(t, cols):
    tile, whole, rest = _row_split(t)
    specs = [pl.BlockSpec((tile, cols), lambda r: (jnp.maximum(r - (1 if rest else 0), 0), 0))]
    if rest:
        specs.append(pl.BlockSpec((rest, cols), lambda r: (whole * tile // rest, 0)))
    return specs


def _weight_grad(x, y, name, x_is_slope=False, deps=()):
    t, k = x.shape
    n = y.shape[1]
    tn = FF_CHUNK
    tk = FF_CHUNK if k % FF_CHUNK == 0 else k
    _, whole, rest = _row_split(t)
    steps = whole + bool(rest)
    tiles = [(a, b) for a in range(k // tk) for b in range(n // tn)]

    def body(*refs):
        o_ref, ob_ref, acc, accb, sem = refs[-5:]
        r = pl.program_id(0)

        def out_copies(a, b):
            return (pltpu.make_async_copy(acc.at[a, b], o_ref.at[a, b], sem.at[0, a, b]),
                    pltpu.make_async_copy(accb.at[a, b], ob_ref.at[a, b], sem.at[1, a, b]))

        def add(x_ref, y_ref, first, last):
            for a in range(k // tk):
                xv = x_ref[:, a * tk:(a + 1) * tk]
                if x_is_slope:
                    xv = xv.astype(F32)
                    xv = (xv * xv * 0.25).astype(BF16)
                for b in range(n // tn):
                    part = lax.dot_general(xv, y_ref[:, b * tn:(b + 1) * tn], (((0,), (0,)), ((), ())),
                                           preferred_element_type=F32)
                    if first:
                        acc[a, b] = part
                    else:
                        acc[a, b] += part
                    if last:
                        accb[a, b] = acc[a, b].astype(BF16)
                        for copy in out_copies(a, b):
                            copy.start()
            if last:
                for a, b in tiles:
                    for copy in out_copies(a, b):
                        copy.wait()

        mine = (1, 3) if rest else (0, 1)
        if steps == 1:
            add(refs[0], refs[1], True, True)
        else:
            pl.when(r == 0)(lambda: add(refs[mine[0]], refs[mine[1]], True, False))
            pl.when((r > 0) & (r < steps - 1))(lambda: add(refs[0], refs[2 if rest else 1], False, False))
            pl.when(r == steps - 1)(lambda: add(refs[0], refs[2 if rest else 1], False, True))

    shape = (k // tk, n // tn, tk, tn)
    body, dep_specs = _behind(body, deps)
    return pl.pallas_call(
        body, name=name, grid=(steps,),
        in_specs=dep_specs + _row_split_specs(t, k) + _row_split_specs(t, n),
        out_specs=[pl.BlockSpec(memory_space=pl.ANY)] * 2,
        out_shape=[jax.ShapeDtypeStruct(shape, F32), jax.ShapeDtypeStruct(shape, BF16)],
        scratch_shapes=[pltpu.VMEM(shape, F32), pltpu.VMEM(shape, BF16), pltpu.SemaphoreType.DMA((2,) + shape[:2])],
        compiler_params=_params("arbitrary"),
    )(*deps, *([x] * (1 + bool(rest))), *([y] * (1 + bool(rest))))


def _mix_out_bwd(dh1, z, gp, w, o, b, c, hc, cw, ga, gc, tm, deps=()):
    t = dh1.shape[0]
    nt = t // tm
    per16 = tm // 16

    def body(dh1_ref, z_ref, gp_ref, w_ref, o_ref, b_ref, c_ref, hc_ref, cp_ref, hp_ref, cw_ref, ga_ref, gc_ref,
             dz_ref, do_ref, dbch_ref, dgp_ref, dga_ref, dgc_ref, dcw_ref, halo):
        i = pl.program_id(0)

        @pl.when(i == 0)
        def _():
            halo[...] = jnp.zeros_like(halo)
            dgp_ref[...] = jnp.zeros_like(dgp_ref)
            dga_ref[...] = jnp.zeros_like(dga_ref)
            dgc_ref[...] = jnp.zeros_like(dgc_ref)
            dcw_ref[...] = jnp.zeros_like(dcw_ref)

        dz, dgp = _rms_bwd(dh1_ref[...], z_ref[...], gp_ref[...])
        dgp_ref[...] += dgp
        dz = dz.astype(BF16)
        dz_ref[...] = dz
        dy = lax.dot_general(dz, w_ref[...].reshape(D_MODEL, D_MODEL), (((1,), (1,)), ((), ())),
                             preferred_element_type=F32)
        do, dga = _rms_bwd(dy[:, :ATTN_W], o_ref[...].astype(F32), ga_ref[...])
        do_ref[...] = do.astype(BF16)
        dga_ref[...] += dga

        cc, hh = c_ref[...].astype(F32), hc_ref[...].astype(F32)
        u = cc * hh
        first = i == nt - 1
        u_before = jnp.where(first, 0.0, (cp_ref[...].astype(F32) * hp_ref[...].astype(F32))[8:])
        u1 = _shift_rows(u, u_before, 1)
        u2 = _shift_rows(u, u_before, 2)
        cv = cw_ref[0:1, :] * u2 + cw_ref[1:2, :] * u1 + cw_ref[2:3, :] * u
        bb = b_ref[...].astype(F32)
        dyc, dgc = _rms_bwd(dy[:, ATTN_W:], bb * cv, gc_ref[...])
        dgc_ref[...] += dgc
        dcv = dyc * bb
        d1 = _advance_rows(dcv, halo[...], 1)
        d2 = _advance_rows(dcv, halo[...], 2)
        halo[...] = dcv[:8]
        du = cw_ref[2:3, :] * dcv + cw_ref[1:2, :] * d1 + cw_ref[0:1, :] * d2
        dbch_ref[...] = jnp.concatenate([dyc * cv, du * hh, du * cc], axis=1).astype(BF16)
        dcw_ref[...] += jnp.concatenate([jnp.sum(dcv * u2, axis=0, keepdims=True),
                                         jnp.sum(dcv * u1, axis=0, keepdims=True),
                                         jnp.sum(dcv * u, axis=0, keepdims=True)], axis=0)

    row = lambda n: pl.BlockSpec((tm, n), lambda i: (nt - 1 - i, 0))
    before = pl.BlockSpec((16, CONV_W), lambda i: (jnp.maximum((nt - 1 - i) * per16 - 1, 0), 0))
    full = lambda a: pl.BlockSpec(a.shape, lambda i: (0,) * a.ndim)
    vec = lambda n: pl.BlockSpec((1, n), lambda i: (0, 0))
    body, dep_specs = _behind(body, deps)
    return pl.pallas_call(
        body, name="mix_out_bwd", grid=(nt,),
        in_specs=dep_specs + [row(D_MODEL), row(D_MODEL), full(gp), full(w), row(ATTN_W), row(CONV_W), row(CONV_W),
                              row(CONV_W), before, before, full(cw), full(ga), full(gc)],
        out_specs=[row(D_MODEL), row(ATTN_W), row(3 * CONV_W), vec(D_MODEL), vec(ATTN_W), vec(CONV_W),
                   pl.BlockSpec((CONV_K, CONV_W), lambda i: (0, 0))],
        out_shape=[jax.ShapeDtypeStruct((t, D_MODEL), BF16), jax.ShapeDtypeStruct((t, ATTN_W), BF16),
                   jax.ShapeDtypeStruct((t, 3 * CONV_W), BF16), jax.ShapeDtypeStruct((1, D_MODEL), F32),
                   jax.ShapeDtypeStruct((1, ATTN_W), F32), jax.ShapeDtypeStruct((1, CONV_W), F32),
                   jax.ShapeDtypeStruct((CONV_K, CONV_W), F32)],
        scratch_shapes=[pltpu.VMEM((8, CONV_W), F32)],
        compiler_params=_params("arbitrary"),
    )(*deps, dh1, z, gp, w, o, b, c, hc, c, hc, cw, ga, gc)


def _attn_bwd(q, k, v, o, do, bias, sinks, tm, deps=()):
    t = q.shape[0]
    per_step = tm // BLOCK

    def body(s_ref, q_ref, k_ref, v_ref, o_ref, do_ref, bias_ref, dq_ref, dk_ref, dv_ref, ds_ref):
        step = pl.program_id(0)

        @pl.when(step == 0)
        def _():
            ds_ref[...] = jnp.zeros_like(ds_ref)

        heads = range(N_Q_HEADS)

        def first_matmuls(b):
            i = step * per_step + b
            rows = slice(b * BLOCK, (b + 1) * BLOCK)
            kc, vc = _two_blocks(k_ref, i), _two_blocks(v_ref, i)
            bias_i = bias_ref[jnp.minimum(i, 2)]
            kgs = [kc[:, _head(g)] for g in range(N_KV_HEADS)]
            vgs = [vc[:, _head(g)] for g in range(N_KV_HEADS)]
            qs = [q_ref[rows, _head(hh)] for hh in heads]
            dosb = [do_ref[rows, _head(hh)] for hh in heads]
            dos = [d.astype(F32) for d in dosb]
            scores = [_attn_scores(qs[hh], kgs[hh // GROUP], bias_i) for hh in heads]
            dps = [lax.dot_general(dosb[hh], vgs[hh // GROUP], (((1,), (1,)), ((), ())), preferred_element_type=F32)
                   for hh in heads]
            return kgs, qs, dos, dosb, scores, dps

        dsink = [jnp.zeros((BLOCK, 1), F32) for _ in range(N_Q_HEADS)]
        ahead = None
        for b in range(per_step):
            i = step * per_step + b
            rows = slice(b * BLOCK, (b + 1) * BLOCK)
            kgs, qs, dos, dosb, scores, dps = first_matmuls(b)
            ps, dss = [], []
            for hh in heads:
                p, share = _attn_probs(scores[hh], s_ref[hh])
                drow = jnp.sum(dos[hh] * o_ref[rows, _head(hh)].astype(F32), axis=-1, keepdims=True)
                dss.append((p * (dps[hh] - drow)).astype(BF16))
                ps.append(p.astype(BF16))
                dsink[hh] = dsink[hh] + share * drow
            for hh in heads:
                dq_ref[rows, _head(hh)] = (jnp.dot(dss[hh], kgs[hh // GROUP], preferred_element_type=F32)
                                           * SCALE).astype(BF16)
            groups = [slice(GROUP * g, GROUP * (g + 1)) for g in range(N_KV_HEADS)]
            dkg = [lax.dot_general(jnp.concatenate(dss[gr], axis=0), jnp.concatenate(qs[gr], axis=0),
                                   (((0,), (0,)), ((), ())), preferred_element_type=F32) for gr in groups]
            dvg = [lax.dot_general(jnp.concatenate(ps[gr], axis=0), jnp.concatenate(dosb[gr], axis=0),
                                   (((0,), (0,)), ((), ())), preferred_element_type=F32) for gr in groups]
            dkb, dvb = jnp.concatenate(dkg, axis=1), jnp.concatenate(dvg, axis=1)
            if b == 0:
                @pl.when(step > 0)
                def _():
                    before = pl.ds(pl.multiple_of((i - 1) * BLOCK, BLOCK), BLOCK)
                    dk_ref[before, :] += dkb[:BLOCK]
                    dv_ref[before, :] += dvb[:BLOCK]
            else:
                at = pl.ds(pl.multiple_of((i - 1) * BLOCK, BLOCK), BLOCK)
                dk_ref[at, :] = ahead[0] + dkb[:BLOCK]
                dv_ref[at, :] = ahead[1] + dvb[:BLOCK]
            ahead = (dkb[BLOCK:], dvb[BLOCK:])
        last = pl.ds(pl.multiple_of(((step + 1) * per_step - 1) * BLOCK, BLOCK), BLOCK)
        dk_ref[last, :] = ahead[0]
        dv_ref[last, :] = ahead[1]
        for hh in range(N_Q_HEADS):
            ds_ref[hh:hh + 1, :] -= jnp.sum(dsink[hh])

    whole = pl.BlockSpec((t, KV_W), lambda i: (0, 0))
    blk = pl.BlockSpec((tm, ATTN_W), lambda i: (i, 0))
    body, dep_specs = _behind(body, deps)
    return pl.pallas_call(
        body, name="attn_bwd", grid=(t // tm,),
        in_specs=dep_specs + [pl.BlockSpec(memory_space=pltpu.SMEM), blk, whole, whole, blk, blk,
                              pl.BlockSpec(bias.shape, lambda i: (0, 0, 0))],
        out_specs=[blk, whole, whole, pl.BlockSpec((N_Q_HEADS, 128), lambda i: (0, 0))],
        out_shape=[jax.ShapeDtypeStruct((t, ATTN_W), BF16), jax.ShapeDtypeStruct((t, KV_W), F32),
                   jax.ShapeDtypeStruct((t, KV_W), F32), jax.ShapeDtypeStruct((N_Q_HEADS, 128), F32)],
        compiler_params=_params("arbitrary"),
    )(*deps, sinks, q, k, v, o, do, bias)


def _in_proj_bwd(dq, dk, dv, dbch, w, dh1, h, g, tabs, tm, split_lead=False):
    t = h.shape[0]
    nt = t // tm

    def body(dq_ref, dk_ref, dv_ref, dbch_ref, w_ref, dh1_ref, h_ref, g_ref, c_ref, s_ref, *rest):
        dp_ref, dg_ref = rest[2:4] if split_lead else rest[1:3]
        i = pl.program_id(0)

        @pl.when(i == 0)
        def _():
            dg_ref[...] = jnp.zeros_like(dg_ref)

        cos, sa, sb = _rope_factors(c_ref[...], s_ref[...])
        rep = ATTN_W // (2 * HEAD_DIM)
        dqr = _rope_bwd(dq_ref[...].astype(F32), jnp.tile(cos, (1, rep)), jnp.tile(sa, (1, rep)),
                        jnp.tile(sb, (1, rep)))
        dkr = _rope_bwd(dk_ref[...], cos, sa, sb)
        dp = jnp.concatenate([dqr.astype(BF16), dkr.astype(BF16), dv_ref[...].astype(BF16), dbch_ref[...]], axis=1)
        dp_ref[...] = dp
        da = jnp.dot(dp, w_ref[...], preferred_element_type=F32)
        dx, dg = _rms_bwd(da, h_ref[...], g_ref[...])
        dg_ref[...] += dg
        dh = dh1_ref[...] + dx
        if not split_lead:
            rest[0][...] = dh
            return
        lead_ref, seq_ref, stage, sems = rest[0], rest[1], rest[4], rest[5]

        def copy(j, slot, first):
            if first:
                return pltpu.make_async_copy(stage.at[slot, pl.ds(BLOCK, tm - BLOCK)],
                                             seq_ref.at[pl.ds(0, tm - BLOCK)], sems.at[slot])
            return pltpu.make_async_copy(stage.at[slot], seq_ref.at[pl.ds(pl.multiple_of(j * tm - BLOCK, BLOCK), tm)],
                                         sems.at[slot])

        slot = i % 2
        pl.when(i == 2)(lambda: copy(0, slot, True).wait())
        pl.when(i > 2)(lambda: copy(i - 2, slot, False).wait())
        stage[slot] = dh

        @pl.when(i == 0)
        def _():
            lead_ref[...] = dh[:BLOCK]
            copy(0, slot, True).start()

        pl.when(i > 0)(lambda: copy(i, slot, False).start())

        @pl.when(i == nt - 1)
        def _():
            for j in range(max(nt - 2, 0), nt):
                copy(j, j % 2, j == 0).wait()

    row = lambda n: pl.BlockSpec((tm, n), lambda i: (i, 0))
    full = lambda a: pl.BlockSpec(a.shape, lambda i: (0, 0))
    dh_specs, dh_shapes, scratch = [row(D_MODEL)], [jax.ShapeDtypeStruct((t, D_MODEL), F32)], []
    if split_lead:
        dh_specs = [pl.BlockSpec((BLOCK, D_MODEL), lambda i: (0, 0)), pl.BlockSpec(memory_space=pl.ANY)]
        dh_shapes = [jax.ShapeDtypeStruct((BLOCK, D_MODEL), F32), jax.ShapeDtypeStruct((t - BLOCK, D_MODEL), F32)]
        scratch = [pltpu.VMEM((2, tm, D_MODEL), F32), pltpu.SemaphoreType.DMA((2,))]
    outs = pl.pallas_call(
        body, name="in_proj_bwd", grid=(nt,),
        in_specs=[row(ATTN_W), row(KV_W), row(KV_W), row(3 * CONV_W), full(w), row(D_MODEL), row(D_MODEL), full(g),
                  row(2 * HEAD_DIM), row(2 * HEAD_DIM)],
        out_specs=dh_specs + [row(IN_W), pl.BlockSpec((1, D_MODEL), lambda i: (0, 0))],
        out_shape=dh_shapes + [jax.ShapeDtypeStruct((t, IN_W), BF16), jax.ShapeDtypeStruct((1, D_MODEL), F32)],
        scratch_shapes=scratch,
        compiler_params=_params("arbitrary"),
    )(dq, dk, dv, dbch, w, dh1, h, g, *tabs)
    return (tuple(outs[:2]) if split_lead else outs[0],) + tuple(outs[-2:])


class _Tiles:
    def __init__(self, t):
        self.tm = _row_tile(t, 640)
        self.ts = self.tm
        self.tabs = _rope_tables(t)
        self.bias = _attn_bias()


def _mixer_fwd(h, p, tl, lead=None):
    if lead is None:
        a, q, k, v, b, c, hc = _in_proj(h, p["mix_pre_g"], p["w_in"], tl.tabs, tl.ts)
    else:
        h, a, q, k, v, b, c, hc = _in_proj(h, p["mix_pre_g"], p["w_in"], tl.tabs, tl.ts, lead)
    o = _attn_fwd(q, k, v, tl.bias, p["sinks"], tl.tm)
    return (h, a, q, k, v, b, c, hc, o)


def _out_fwd(mixed, p, tl, deps=()):
    h, a, q, k, v, b, c, hc, o = mixed
    h1, y, z = _mix_out(h, o, b, c, hc, p["conv_w"], p["attn_out_g"], p["conv_out_g"], p["w_out"], p["mix_post_g"],
                        tl.ts, deps)
    return h1, mixed + (h1, y, z)


def _mlp_fwd(h1, saved, p, tl, target=None):
    h2, a2, slope, f = _mlp(h1, p["mlp_pre_g"], p["w_up"], p["w_down"], p["mlp_post_g"], tl.tm, target)
    return h2, saved + (a2, slope, f)


def _mlp_part_bwd(dh, saved, p, tl, deps=()):
    h1, a2, slope, f = saved[9], saved[12], saved[13], saved[14]
    if isinstance(f, tuple):
        df, dg2 = f
        dup = _mlp_bwd_hidden(None, None, None, slope, p["w_down"], tl.tm, deps, df)
    else:
        df, dup, dg2 = _mlp_bwd_hidden(dh, f, p["mlp_post_g"], slope, p["w_down"], tl.tm, deps)
    dh1, dg1 = _mlp_bwd_input(dup, p["w_up"], h1, p["mlp_pre_g"], dh, tl.tm)
    g = {"w_down": [d.reshape(N_CHIPS, FF_CHUNK, D_MODEL)
                    for d in _weight_grad(slope, df, "grad_w_down", x_is_slope=True)],
         "w_up": [d.reshape(N_CHIPS, D_MODEL, FF_CHUNK) for d in _weight_grad(a2, dup, "grad_w_up")],
         "mlp_post_g": dg2, "mlp_pre_g": dg1}
    return dh1, g


def _mix_out_part_bwd(dh1, saved, p, tl, deps=()):
    b, c, hc, o, y, z = saved[5], saved[6], saved[7], saved[8], saved[10], saved[11]
    dz, do, dbch, dgp, dga, dgc, dcw = _mix_out_bwd(dh1, z, p["mix_post_g"], p["w_out"], o, b, c, hc, p["conv_w"],
                                                    p["attn_out_g"], p["conv_out_g"], tl.ts, deps)
    g = {"w_out": [d.reshape(N_CHIPS, D_MODEL // N_CHIPS, D_MODEL) for d in _weight_grad(y, dz, "grad_w_out")],
         "mix_post_g": dgp, "attn_out_g": dga, "conv_out_g": dgc, "conv_w": dcw}
    return (dh1, do, dbch), g


def _attn_in_part_bwd(carry, saved, p, tl, deps=(), split_lead=False):
    dh1, do, dbch = carry
    h_in, q, k, v, o = saved[0], saved[2], saved[3], saved[4], saved[8]
    dq, dk, dv, dsink = _attn_bwd(q, k, v, o, do, tl.bias, p["sinks"], tl.tm, deps)
    dh, dproj, dgi = _in_proj_bwd(dq, dk, dv, dbch, p["w_in"], dh1, h_in, p["mix_pre_g"], tl.tabs, tl.ts, split_lead)
    return dh, dproj, {"mix_pre_g": dgi, "sinks": dsink[:, 0]}


def _in_grad(dproj, saved, deps=()):
    return [d.reshape(N_CHIPS, IN_W // N_CHIPS, D_MODEL) for d in _weight_grad(dproj, saved[1], "grad_w_in", deps=deps)]


def _place():
    return lax.axis_index("x"), lax.axis_index("y"), lax.axis_index("c")


def _other_chips(x, y):
    return [(1 - x, y), (x, 1 - y), (1 - x, 1 - y)]


_HBM = pl.BlockSpec(memory_space=pltpu.HBM)
_SEM = pl.BlockSpec(memory_space=pltpu.SEMAPHORE)
_EFFECT = pltpu.SideEffectType.DATAFLOW_SIDE_EFFECTING


class _Exchange:
    def __init__(self, name, bufs, plan, n, after=()):
        self.name, self.plan, nb = name, plan, len(bufs)
        n_in = nb + len(after)

        def body(*refs):
            send, recv, token = refs[n_in], refs[n_in + 1], refs[-1]
            for k, (src, dst, target, _) in enumerate(plan(refs[:nb])):
                pltpu.make_async_remote_copy(src_ref=src, dst_ref=dst, send_sem=send.at[k], recv_sem=recv.at[k],
                                             device_id=target, device_id_type=MESH).start()
            token[...] = jnp.zeros_like(token)

        outs = pl.pallas_call(
            body, name=name + "_start",
            out_shape=(pltpu.SemaphoreType.DMA((n,)), pltpu.SemaphoreType.DMA((n,)),
                       *[pltpu.HBM(b.shape, b.dtype) for b in bufs], jax.ShapeDtypeStruct((8, 128), F32)),
            in_specs=[_HBM] * nb + [pl.BlockSpec(memory_space=pl.ANY)] * len(after),
            out_specs=(_SEM, _SEM, *[_HBM] * nb, pl.BlockSpec(memory_space=pltpu.VMEM)),
            input_output_aliases={i: 2 + i for i in range(nb)},
            compiler_params=pltpu.CompilerParams(has_side_effects=_EFFECT),
        )(*[pltpu.with_memory_space_constraint(b, pltpu.HBM) for b in bufs], *after)
        self.send, self.recv, self.bufs, self.token = outs[0], outs[1], list(outs[2:2 + nb]), outs[-1]

    def wait(self, *after):
        plan, nb = self.plan, len(self.bufs)

        def body(*refs):
            send, recv = refs[nb], refs[nb + 1]
            for k, (src, _, target, land) in enumerate(plan(refs[:nb])):
                cp = pltpu.make_async_remote_copy(src_ref=src, dst_ref=land, send_sem=send.at[k], recv_sem=recv.at[k],
                                                  device_id=target, device_id_type=MESH)
                cp.wait_send()
                cp.wait_recv()

        outs = pl.pallas_call(
            body, name=self.name + "_wait", out_shape=[pltpu.HBM(b.shape, b.dtype) for b in self.bufs],
            in_specs=[_HBM] * nb + [_SEM, _SEM] + [pl.BlockSpec(memory_space=pl.ANY)] * len(after),
            out_specs=[_HBM] * nb, input_output_aliases={i: i for i in range(nb)},
            compiler_params=pltpu.CompilerParams(has_side_effects=_EFFECT),
        )(*self.bufs, self.send, self.recv, *after)
        return list(outs)


def _gather_plan(n):
    def plan(refs):
        x, y, c = _place()
        me = 2 * x + y
        return [(refs[a].at[me], refs[a].at[me], (px, py, c), refs[a].at[2 * px + py])
                for a in range(n) for px, py in _other_chips(x, y)]

    return plan


def _gather_half_plan(n, half_rows):
    def plan(refs):
        x, y, c = _place()
        me = 2 * x + y
        out = []
        for a in range(n):
            rows = pl.ds(c * half_rows[a], half_rows[a])
            out += [(refs[a].at[me, rows], refs[a].at[me, rows], (px, py, c), refs[a].at[2 * px + py, rows])
                    for px, py in _other_chips(x, y)]
        return out

    return plan


def _hand_over_plan(n, half_rows):
    def plan(refs):
        x, y, c = _place()
        out = []
        for a in range(n):
            mine, theirs = pl.ds(c * half_rows[a], half_rows[a]), pl.ds((1 - c) * half_rows[a], half_rows[a])
            for px, py in _other_chips(x, y):
                held = refs[a].at[2 * px + py, mine]
                out.append((held, held, (x, y, 1 - c), refs[a].at[2 * px + py, theirs]))
        return out

    return plan


def _peers():
    x, y, c = _place()
    return [(k - 1, (x ^ (k >> 2), y ^ ((k >> 1) & 1), c ^ (k & 1))) for k in range(1, N_DEV)]


def _scatter_plan(n, half_rows):
    def plan(refs):
        out = []
        for a in range(n):
            hr = half_rows[a]
            for k, (px, py, pc) in _peers():
                out.append((refs[a].at[2 * px + py, pl.ds(pc * hr, hr)], refs[n + a].at[k], (px, py, pc),
                            refs[n + a].at[k]))
        return out

    return plan


def _join_plan(n):
    def plan(refs):
        x, y, c = _place()
        return [(refs[a].at[c], refs[a].at[c], (x, y, 1 - c), refs[a].at[1 - c]) for a in range(n)]

    return plan


def _sum_parts(gs, qs):
    n = len(gs)
    half_rows = [g.shape[1] // 2 for g in gs]
    tr = [_block_rows(hr) for hr in half_rows]
    per = [hr // t for hr, t in zip(half_rows, tr)]
    x, y, c = _place()
    where = jnp.stack([2 * x + y, c]).astype(jnp.int32)

    def body(where_ref, *refs):
        i = pl.program_id(0)
        for a in range(n):
            g_ref, q_ref, o_ref = refs[a], refs[n + a], refs[2 * n + a]

            @pl.when(i < per[a])
            def _():
                total = g_ref[...]
                for k in range(N_DEV - 1):
                    total = total + q_ref[k].astype(F32)
                o_ref[...] = total

    def at(a, i):
        return jnp.minimum(i, per[a] - 1)

    specs_g = [pl.BlockSpec((None, tr[a], gs[a].shape[2]),
                            lambda i, where_ref, a=a: (where_ref[0], where_ref[1] * per[a] + at(a, i), 0)) for a in range(n)]
    specs_q = [pl.BlockSpec((N_DEV - 1, tr[a], gs[a].shape[2]), lambda i, where_ref, a=a: (0, at(a, i), 0))
               for a in range(n)]
    specs_o = [pl.BlockSpec((None, tr[a], gs[a].shape[2]), lambda i, where_ref, a=a: (where_ref[1], at(a, i), 0))
               for a in range(n)]
    return pl.pallas_call(
        body, name="sum_parts",
        grid_spec=pltpu.PrefetchScalarGridSpec(num_scalar_prefetch=1, grid=(max(per),), in_specs=specs_g + specs_q,
                                               out_specs=specs_o),
        out_shape=[jax.ShapeDtypeStruct((2, hr, g.shape[2]), F32) for g, hr in zip(gs, half_rows)],
        compiler_params=_params("arbitrary"),
    )(where, *gs, *qs)


def _all_plan(refs):
    x, y, c = _place()
    mine = refs[0].at[4 * x + 2 * y + c]
    return [(mine, mine, (px, py, pc), refs[0].at[4 * px + 2 * py + pc]) for _, (px, py, pc) in _peers()]


def _sum_devices(parts):
    def body(p_ref, o_ref):
        total = p_ref[0]
        for d in range(1, N_DEV):
            total = total + p_ref[d]
        o_ref[...] = total

    vm = pl.BlockSpec(memory_space=pltpu.VMEM)
    return pl.pallas_call(body, name="sum_devices", in_specs=[vm], out_specs=vm,
                          out_shape=jax.ShapeDtypeStruct(parts.shape[1:], F32))(parts)


def _adamw_math(w, g, m, v):
    m = ADAM_B1 * m + (1.0 - ADAM_B1) * g
    v = ADAM_B2 * v + (1.0 - ADAM_B2) * jnp.square(g)
    m_hat = m / (1.0 - ADAM_B1 ** ADAM_STEP)
    v_hat = v / (1.0 - ADAM_B2 ** ADAM_STEP)
    delta = -ADAM_LR * (m_hat / (jnp.sqrt(v_hat) + ADAM_EPS) + ADAM_WD * w)
    return delta, m, v


def _adamw_large(layer, ws, halves, ms, vs, others):
    n = len(ws)
    tr = [_block_rows(w.shape[1] // 2) for w in ws]
    per = [w.shape[1] // 2 // t for w, t in zip(ws, tr)]
    kept = [] if others is None else [a for four in others for a in four]

    def body(*refs):
        i = pl.program_id(0)
        outs = refs[4 * n + len(kept):]
        for a in range(n):
            w_ref, g_ref, m_ref, v_ref = refs[a], refs[n + a], refs[2 * n + a], refs[3 * n + a]
            g_out, d_ref, nm_ref, nv_ref = outs[4 * a:4 * a + 4]

            @pl.when(i < 2 * per[a])
            def _():
                g = g_ref[...]
                g_out[...] = g
                d_ref[...], nm_ref[...], nv_ref[...] = _adamw_math(w_ref[...], g, m_ref[...], v_ref[...])

    def at(a, i):
        return jnp.minimum(i, 2 * per[a] - 1)

    blk = [pl.BlockSpec((None, tr[a], ws[a].shape[2]), lambda i, a=a: (layer, at(a, i), 0)) for a in range(n)]
    half = [pl.BlockSpec((None, tr[a], ws[a].shape[2]), lambda i, a=a: (at(a, i) // per[a], at(a, i) % per[a], 0))
            for a in range(n)]
    outs = pl.pallas_call(
        body, name="adamw_large", grid=(2 * max(per),),
        in_specs=blk + half + blk + blk + [pl.BlockSpec(memory_space=pl.ANY)] * len(kept),
        out_specs=[blk[a] for a in range(n) for _ in range(4)],
        out_shape=[jax.ShapeDtypeStruct(w.shape, F32) for w in ws for _ in range(4)],
        input_output_aliases={4 * n + k: k for k in range(len(kept))},
        compiler_params=_params("arbitrary"),
    )(*ws, *halves, *ms, *vs, *kept)
    return [outs[4 * a:4 * a + 4] for a in range(n)]


def _adamw_small(ws, gs, ms, vs):
    n = len(ws)

    def body(*refs):
        w_r, g_r, m_r, v_r = refs[:n], refs[n:2 * n], refs[2 * n:3 * n], refs[3 * n:4 * n]
        d_r, nm_r, nv_r = refs[4 * n:5 * n], refs[5 * n:6 * n], refs[6 * n:]
        for a in range(n):
            d_r[a][...], nm_r[a][...], nv_r[a][...] = _adamw_math(w_r[a][...], g_r[a][...], m_r[a][...], v_r[a][...])

    vm = pl.BlockSpec(memory_space=pltpu.VMEM)
    outs = pl.pallas_call(
        body, name="adamw_small", in_specs=[vm] * (4 * n), out_specs=[vm] * (3 * n),
        out_shape=[jax.ShapeDtypeStruct(w.shape, F32) for w in ws] * 3,
    )(*ws, *gs, *ms, *vs)
    return outs[:n], outs[n:2 * n], outs[2 * n:]


_LARGE = ("w_in", "w_out", "w_up", "w_down")
_SMALL = ("meta_tokens", "mix_pre_g", "conv_w", "sinks", "attn_out_g", "conv_out_g", "mix_post_g", "mlp_pre_g",
          "mlp_post_g")
_ORDER = ("meta_tokens", "mix_pre_g", "w_in", "conv_w", "sinks", "attn_out_g", "conv_out_g", "w_out", "mix_post_g",
          "mlp_pre_g", "w_up", "w_down", "mlp_post_g")


class _Reduce:
    def __init__(self, name, grads, after=()):
        self.name, self.n = name, len(grads)
        self.own = [g for g, _ in grads]
        half_rows = [g.shape[1] // 2 for g in self.own]
        zones = [lax.empty((N_DEV - 1, hr, g.shape[2]), BF16) for g, hr in zip(self.own, half_rows)]
        self.exchange = _Exchange(name + "_scatter", [b for _, b in grads] + zones, _scatter_plan(self.n, half_rows),
                                  (N_DEV - 1) * self.n, after)

    @property
    def token(self):
        return self.exchange.token

    def join(self, *after):
        bufs = self.exchange.wait(*after)
        halves = list(_sum_parts(self.own, bufs[self.n:]))
        self.exchange = _Exchange(self.name + "_join", halves, _join_plan(self.n), self.n)

    def done(self, *after):
        return self.exchange.wait(*after)


def _pad_cols(a, n=D_MODEL):
    return jnp.pad(a, ((0, 0), (0, n - a.shape[1])))


def kernel(x, meta_tokens, mix_pre_g, w_in, conv_w, sinks, attn_out_g, conv_out_g, w_out, mix_post_g, mlp_pre_g, w_up, w_down, mlp_post_g, loss_target, m_meta_tokens, m_mix_pre_g, m_w_in, m_conv_w, m_sinks, m_attn_out_g, m_conv_out_g, m_w_out, m_mix_post_g, m_mlp_pre_g, m_w_up, m_w_down, m_mlp_post_g, v_meta_tokens, v_mix_pre_g, v_w_in, v_conv_w, v_sinks, v_attn_out_g, v_conv_out_g, v_w_out, v_mix_post_g, v_mlp_pre_g, v_w_up, v_w_down, v_mlp_post_g):
    w = dict(meta_tokens=meta_tokens, mix_pre_g=mix_pre_g, w_in=w_in, conv_w=conv_w, sinks=sinks,
             attn_out_g=attn_out_g, conv_out_g=conv_out_g, w_out=w_out, mix_post_g=mix_post_g, mlp_pre_g=mlp_pre_g,
             w_up=w_up, w_down=w_down, mlp_post_g=mlp_post_g)
    m = dict(meta_tokens=m_meta_tokens, mix_pre_g=m_mix_pre_g, w_in=m_w_in, conv_w=m_conv_w, sinks=m_sinks,
             attn_out_g=m_attn_out_g, conv_out_g=m_conv_out_g, w_out=m_w_out, mix_post_g=m_mix_post_g,
             mlp_pre_g=m_mlp_pre_g, w_up=m_w_up, w_down=m_w_down, mlp_post_g=m_mlp_post_g)
    v = dict(meta_tokens=v_meta_tokens, mix_pre_g=v_mix_pre_g, w_in=v_w_in, conv_w=v_conv_w, sinks=v_sinks,
             attn_out_g=v_attn_out_g, conv_out_g=v_conv_out_g, w_out=v_w_out, mix_post_g=v_mix_post_g,
             mlp_pre_g=v_mlp_pre_g, w_up=v_w_up, w_down=v_w_down, mlp_post_g=v_mlp_post_g)
    chip = 2 * lax.axis_index("x") + lax.axis_index("y")
    tl = _Tiles(x.shape[1] + BLOCK)

    def zone(quarter):
        return lax.dynamic_update_slice(lax.empty((N_CHIPS,) + quarter.shape, quarter.dtype), quarter[None],
                                        (chip,) + (0,) * quarter.ndim)

    w, m, v = ({**d, "w_in": jnp.swapaxes(d["w_in"], 1, 2)} for d in (w, m, v))
    zones = {n: [zone(w[n][l].astype(BF16)) for l in range(DEPTH)] for n in _LARGE}
    first = _Exchange("gather_first", [zones["w_in"][0], zone(w["conv_w"]), zone(w["meta_tokens"])], _gather_plan(3), 9)
    out0 = _Exchange("gather_out", [zones["w_out"][0]], _gather_plan(1), 3, [first.token])
    mlp_halves = [D_MODEL // 2, FF_CHUNK // 2]
    rest = _Exchange("gather_rest", [zones[n][0] for n in ("w_up", "w_down")], _gather_half_plan(2, mlp_halves), 6,
                     [out0.token])

    def whole_in(quarters):
        return quarters.reshape(IN_W, D_MODEL)

    q_in, q_conv, q_meta = first.wait(rest.token, *tl.tabs, tl.bias)
    conv_whole = jnp.transpose(q_conv, (1, 2, 0, 3)).reshape(DEPTH, CONV_K, CONV_W)
    meta = jnp.transpose(q_meta, (1, 0, 2)).reshape(N_META, D_MODEL)
    p = [{"conv_w": conv_whole[l], "sinks": w["sinks"][l]} for l in range(DEPTH)]
    for l in range(DEPTH):
        for n in ("mix_pre_g", "attn_out_g", "conv_out_g", "mix_post_g", "mlp_pre_g", "mlp_post_g"):
            p[l][n] = w[n][l][None, :]

    lead = jnp.concatenate([jnp.zeros((LEAD_PAD, D_MODEL), F32), meta], axis=0)
    p[0]["w_in"] = whole_in(q_in)
    mixed = _mixer_fwd(x[0], p[0], tl, lead)
    second = _Exchange("gather_second", [zones["w_in"][1], zones["w_out"][1]], _gather_plan(2), 6, [mixed[-1]])
    second_mlp = _Exchange("gather_second_mlp", [zones["w_up"][1], zones["w_down"][1]], _gather_plan(2), 6,
                           [second.token])
    hand_over = _Exchange("hand_over_rest", rest.wait(second_mlp.token), _hand_over_plan(2, mlp_halves), 6)
    p[0]["w_out"], = out0.wait(hand_over.token)
    h1, saved0 = _out_fwd(mixed, p[0], tl)
    p[0]["w_up"], p[0]["w_down"] = hand_over.wait(h1)
    h, saved0 = _mlp_fwd(h1, saved0, p[0], tl)
    q_in, p[1]["w_out"] = second.wait(h)
    p[1]["w_in"] = whole_in(q_in)
    h1, saved1 = _out_fwd(_mixer_fwd(h, p[1], tl), p[1], tl)
    p[1]["w_up"], p[1]["w_down"] = second_mlp.wait(h1)
    (loss_tile, dh), saved1 = _mlp_fwd(h1, saved1, p[1], tl, loss_target[0])

    def adamw(layer, halves, other):
        names = list(halves)
        done = _adamw_large(layer, [w[n] for n in names], [halves[n] for n in names], [m[n] for n in names],
                            [v[n] for n in names], None if other is None else [other[n] for n in names])
        return dict(zip(names, done))

    dh1, g1 = _mlp_part_bwd(dh, saved1, p[1], tl)
    carry, gm = _mix_out_part_bwd(dh1, saved1, p[1], tl)
    dh, dproj, gi = _attn_in_part_bwd(carry, saved1, p[1], tl)
    g1.update(gm, w_in=_in_grad(dproj, saved1), **gi)
    red1 = _Reduce("reduce1", [g1[n] for n in _LARGE])
    dh1, g0 = _mlp_part_bwd(dh, saved0, p[0], tl, [red1.token])
    red1.join(g0["w_down"][0])
    carry, gm = _mix_out_part_bwd(dh1, saved0, p[0], tl, [red1.token])
    first0 = ("w_up", "w_down", "w_out")
    g0.update(gm)
    red0a = _Reduce("reduce0a", [g0[n] for n in first0])
    (dlead, dseq), dproj, gi = _attn_in_part_bwd(carry, saved0, p[0], tl, [red0a.token], split_lead=True)
    g0.update(gi)
    grad_x = dseq[None]
    grads = {n: [g0[n], g1[n]] for n in g0 if n not in _LARGE}

    rows = [dlead[LEAD_PAD:]]
    for n in ("mix_pre_g", "mix_post_g", "mlp_pre_g", "mlp_post_g"):
        rows += grads[n]
    rows += [jnp.concatenate([grads["attn_out_g"][l], grads["conv_out_g"][l]], axis=1) for l in range(DEPTH)]
    rows.append(jnp.concatenate(grads["conv_w"], axis=1))
    rows.append(_pad_cols(jnp.concatenate(grads["sinks"])[None, :]))
    rows.append(_pad_cols(loss_tile[:1]))
    packed = jnp.concatenate(rows, axis=0)
    packed = jnp.pad(packed, ((0, SMALL_ROWS - packed.shape[0]), (0, 0)))
    device = 2 * chip + lax.axis_index("c")
    small_parts = _Exchange("gather_small", [lax.dynamic_update_slice(lax.empty((N_DEV,) + packed.shape, F32),
                                                                      packed[None], (device, 0, 0))], _all_plan, N_DEV - 1)
    g0["w_in"] = _in_grad(dproj, saved0, [small_parts.token])
    red0b = _Reduce("reduce0b", [g0["w_in"]])
    done1 = adamw(1, dict(zip(_LARGE, red1.done(red0b.token))), None)
    total = _sum_devices(small_parts.wait(*[done1[n][0] for n in _LARGE])[0])
    r0 = N_META
    small = {
        "meta_tokens": lax.dynamic_slice(total[:N_META], (0, chip * (D_MODEL // N_CHIPS)), (N_META, D_MODEL // N_CHIPS)),
        "mix_pre_g": total[r0:r0 + 2], "mix_post_g": total[r0 + 2:r0 + 4], "mlp_pre_g": total[r0 + 4:r0 + 6],
        "mlp_post_g": total[r0 + 6:r0 + 8],
        "attn_out_g": total[r0 + 8:r0 + 10, :ATTN_W], "conv_out_g": total[r0 + 8:r0 + 10, ATTN_W:],
        "conv_w": lax.dynamic_slice(total[r0 + 10:r0 + 13].reshape(CONV_K, DEPTH, CONV_W).transpose(1, 0, 2),
                                    (0, 0, chip * (CONV_W // N_CHIPS)), (DEPTH, CONV_K, CONV_W // N_CHIPS)),
        "sinks": total[r0 + 13, :DEPTH * N_Q_HEADS].reshape(DEPTH, N_Q_HEADS),
    }
    loss = total[r0 + 14, 0]

    ds, nms, nvs = _adamw_small([w[n] for n in _SMALL], [small[n] for n in _SMALL], [m[n] for n in _SMALL],
                                [v[n] for n in _SMALL])
    red0a.join(ds[0], grad_x)
    red0b.join(red0a.token)
    done0 = adamw(0, dict(zip(first0, red0a.done(red0b.token))), done1)
    done0.update(adamw(0, {"w_in": red0b.done(done0["w_down"][0])[0]}, done1))
    grad, delta, new_m, new_v = {}, {}, {}, {}
    for n in _LARGE:
        grad[n], delta[n], new_m[n], new_v[n] = done0[n]
    for d in (grad, delta, new_m, new_v):
        d["w_in"] = jnp.swapaxes(d["w_in"], 1, 2)
    for i, n in enumerate(_SMALL):
        grad[n], delta[n], new_m[n], new_v[n] = small[n], ds[i], nms[i], nvs[i]
    return (loss, grad_x, *[grad[n] for n in _ORDER], *[delta[n] for n in _ORDER], *[new_m[n] for n in _ORDER],
            *[new_v[n] for n in _ORDER])
```

```python
import jax
import jax.numpy as jnp
from jax import lax
from jax.experimental import pallas as pl
from jax.experimental.pallas import tpu as pltpu

F32 = jnp.float32
BF16 = jnp.bfloat16

D_MODEL = 1024
DEPTH = 2
N_META = 16
ATTN_W = 512
CONV_W = 512
HEAD_DIM = 64
N_Q_HEADS = 8
N_KV_HEADS = 2
GROUP = N_Q_HEADS // N_KV_HEADS
KV_W = N_KV_HEADS * HEAD_DIM
CONV_K = 3
BLOCK = 128
LEAD_PAD = BLOCK - N_META
ROPE_THETA = 500000.0
ROT_DIM = HEAD_DIM // 4
ROT_HALF = ROT_DIM // 2
D_FF = 4 * D_MODEL
IN_W = ATTN_W + 2 * KV_W + 3 * CONV_W
QKV_W = ATTN_W + 2 * KV_W
EPS = 1e-6
SCALE = HEAD_DIM ** -0.5
FF_CHUNK = 1024
N_CHIPS = 4
N_DEV = 8

ADAM_LR = 0.001
ADAM_B1 = 0.9
ADAM_B2 = 0.999
ADAM_EPS = 1e-08
ADAM_WD = 0.01
ADAM_STEP = 10

V7X_VMEM_LIMIT = 60 * 1024 * 1024
SMALL_ROWS = 32

MESH = pl.DeviceIdType.MESH


def _params(*sem):
    return pltpu.CompilerParams(dimension_semantics=sem, vmem_limit_bytes=V7X_VMEM_LIMIT)


def _block_rows(n):
    return max(r for r in range(16, min(n, 128) + 1, 16) if n % r == 0)


def _row_tile(t, most):
    nb = t // BLOCK
    for b in range(most // BLOCK, 0, -1):
        if nb % b == 0:
            return b * BLOCK
    return BLOCK


def _behind(body, deps):
    n = len(deps)

    def wrapped(*refs):
        body(*refs[n:])

    return wrapped, [pl.BlockSpec(memory_space=pl.ANY)] * n


def _with_chunked_weights(one_step, *weights):
    def run(first):
        copies = [[pltpu.make_async_copy(hbm.at[j], vm.at[j], sem.at[j]) for j in range(hbm.shape[0])]
                  for hbm, vm, sem in weights]
        if first:
            for same_chunk in zip(*copies):
                for copy in same_chunk:
                    copy.start()

        def getter(k):
            def chunk(j):
                if first:
                    copies[k][j].wait()
                return weights[k][1][j]
            return chunk

        one_step(*[getter(k) for k in range(len(weights))])

    first_step = pl.program_id(0) == 0
    pl.when(first_step)(lambda: run(True))
    pl.when(jnp.logical_not(first_step))(lambda: run(False))


def _chunked_weight_scratch(*weights):
    return [pltpu.VMEM(w.shape, w.dtype) for w in weights] + [pltpu.SemaphoreType.DMA((len(weights), weights[0].shape[0]))]


def _rms(x, g):
    r = lax.rsqrt(jnp.mean(x * x, axis=-1, keepdims=True) + EPS)
    return x * r * g


def _rms_bwd(dy, x, g):
    r = lax.rsqrt(jnp.mean(x * x, axis=-1, keepdims=True) + EPS)
    xh = x * r
    dg = jnp.sum(dy * xh, axis=0, keepdims=True)
    dxh = dy * g
    dx = r * (dxh - xh * jnp.mean(dxh * xh, axis=-1, keepdims=True))
    return dx, dg


def _rope(x, cos, sa, sb):
    n = x.shape[-1]
    return x * cos + pltpu.roll(x, n - ROT_HALF, 1) * sa + pltpu.roll(x, ROT_HALF, 1) * sb


def _rope_bwd(dy, cos, sa, sb):
    n = dy.shape[-1]
    return dy * cos + pltpu.roll(dy * sa, ROT_HALF, 1) + pltpu.roll(dy * sb, n - ROT_HALF, 1)


def _rope_tables(t):
    pos = lax.broadcasted_iota(jnp.int32, (t, ROT_HALF), 0).astype(F32) - LEAD_PAD
    pair = lax.broadcasted_iota(jnp.int32, (t, ROT_HALF), 1).astype(F32)
    inv_freq = jnp.power(jnp.float32(ROPE_THETA), -(2.0 * pair) / ROT_DIM)
    ang = pos * inv_freq
    cos, sin = lax.optimization_barrier((jnp.cos(ang), jnp.sin(ang)))
    spread = (1, 2 * HEAD_DIM // ROT_HALF)
    cos, sin = jnp.tile(cos, spread), jnp.tile(sin, spread)
    dim = lax.broadcasted_iota(jnp.int32, (t, 2 * HEAD_DIM), 1) % HEAD_DIM
    return jnp.where(dim < ROT_DIM, cos, 1.0), jnp.where(dim < ROT_DIM, sin, 0.0)


def _rope_factors(cos, sin):
    dim = lax.broadcasted_iota(jnp.int32, sin.shape, 1) % HEAD_DIM
    return cos, jnp.where(dim < ROT_HALF, -sin, 0.0), jnp.where(dim >= ROT_HALF, sin, 0.0)


def _in_proj(h, g, w, tabs, tm, lead=None):
    t = h.shape[0] + (0 if lead is None else BLOCK)
    per_step = 0 if lead is None else tm // BLOCK

    def body(*refs):
        if lead is None:
            x = refs[0][...]
            refs = refs[1:]
        else:
            blocks = [r[...] for r in refs[1:1 + per_step]]
            blocks[0] = jnp.where(pl.program_id(0) == 0, refs[0][...], blocks[0])
            x = jnp.concatenate(blocks, axis=0)
            first_out = 1 + per_step + 4
            refs[first_out][...] = x
            refs = refs[1 + per_step:first_out] + refs[first_out + 1:]
        g_ref, w_ref, c_ref, s_ref, a_ref, q_ref, k_ref, v_ref, b_ref, cg_ref, hc_ref = refs
        a = _rms(x, g_ref[...]).astype(BF16)
        a_ref[...] = a
        p = lax.dot_general(a, w_ref[...], (((1,), (1,)), ((), ())), preferred_element_type=F32)
        cos, sa, sb = _rope_factors(c_ref[...], s_ref[...])
        rep = ATTN_W // (2 * HEAD_DIM)
        q = _rope(p[:, :ATTN_W], jnp.tile(cos, (1, rep)), jnp.tile(sa, (1, rep)), jnp.tile(sb, (1, rep)))
        q_ref[...] = (q * SCALE).astype(BF16)
        k_ref[...] = _rope(p[:, ATTN_W:ATTN_W + KV_W], cos, sa, sb).astype(BF16)
        v_ref[...] = p[:, ATTN_W + KV_W:QKV_W].astype(BF16)
        b_ref[...] = p[:, QKV_W:QKV_W + CONV_W].astype(BF16)
        cg_ref[...] = p[:, QKV_W + CONV_W:QKV_W + 2 * CONV_W].astype(BF16)
        hc_ref[...] = p[:, QKV_W + 2 * CONV_W:].astype(BF16)

    row = lambda n: pl.BlockSpec((tm, n), lambda i: (i, 0))
    full = lambda a: pl.BlockSpec(a.shape, lambda i: (0, 0))

    def sequence_block(b):
        return pl.BlockSpec((BLOCK, D_MODEL), lambda i: (jnp.maximum(i * per_step + b - 1, 0), 0))

    if lead is None:
        first_in, first_args, first_out, first_shape = [row(D_MODEL)], [h], [], []
    else:
        first_in = [full(lead)] + [sequence_block(b) for b in range(per_step)]
        first_args = [lead] + [h] * per_step
        first_out, first_shape = [row(D_MODEL)], [jax.ShapeDtypeStruct((t, D_MODEL), F32)]
    return pl.pallas_call(
        body, name="in_proj", grid=(t // tm,),
        in_specs=first_in + [full(g), full(w), row(2 * HEAD_DIM), row(2 * HEAD_DIM)],
        out_specs=first_out + [row(D_MODEL), row(ATTN_W), row(KV_W), row(KV_W), row(CONV_W), row(CONV_W), row(CONV_W)],
        out_shape=first_shape + [jax.ShapeDtypeStruct((t, D_MODEL), BF16), jax.ShapeDtypeStruct((t, ATTN_W), BF16),
                                 jax.ShapeDtypeStruct((t, KV_W), BF16), jax.ShapeDtypeStruct((t, KV_W), BF16),
                                 jax.ShapeDtypeStruct((t, CONV_W), BF16), jax.ShapeDtypeStruct((t, CONV_W), BF16),
                                 jax.ShapeDtypeStruct((t, CONV_W), BF16)],
        compiler_params=_params("parallel"),
    )(*first_args, g, w, *tabs)


def _attn_bias():
    r = lax.broadcasted_iota(jnp.int32, (3, BLOCK, 2 * BLOCK), 1)
    c = lax.broadcasted_iota(jnp.int32, (3, BLOCK, 2 * BLOCK), 2)
    i = lax.broadcasted_iota(jnp.int32, (3, BLOCK, 2 * BLOCK), 0)
    ok = (c > r) & (c <= r + BLOCK) & (c + (i - 1) * BLOCK >= LEAD_PAD)
    return jnp.where(ok, 0.0, -jnp.inf).astype(F32)


def _attn_scores(qh, kg, bias):
    return lax.dot_general(qh, kg, (((1,), (1,)), ((), ())), preferred_element_type=F32) + bias


def _attn_probs(s, sk):
    m = jnp.maximum(jnp.max(s, axis=-1, keepdims=True), sk)
    e = jnp.exp(s - m)
    es = jnp.exp(sk - m)
    rden = 1.0 / (jnp.sum(e, axis=-1, keepdims=True) + es)
    return e * rden, es * rden


def _head(hh):
    return slice(hh * HEAD_DIM, (hh + 1) * HEAD_DIM)


def _two_blocks(ref, i):
    prev = jnp.maximum(i - 1, 0)
    return jnp.concatenate([ref[pl.ds(pl.multiple_of(prev * BLOCK, BLOCK), BLOCK), :],
                            ref[pl.ds(pl.multiple_of(i * BLOCK, BLOCK), BLOCK), :]], axis=0)


def _attn_fwd(q, k, v, bias, sinks, tm):
    t = q.shape[0]
    per_step = tm // BLOCK
    heads = range(N_Q_HEADS)

    def body(s_ref, q_ref, k_ref, v_ref, bias_ref, o_ref):
        for b in range(per_step):
            i = pl.program_id(0) * per_step + b
            rows = slice(b * BLOCK, (b + 1) * BLOCK)
            kc, vc = _two_blocks(k_ref, i), _two_blocks(v_ref, i)
            bias_i = bias_ref[jnp.minimum(i, 2)]
            scores = [_attn_scores(q_ref[rows, _head(hh)], kc[:, _head(hh // GROUP)], bias_i) for hh in heads]
            probs = [_attn_probs(scores[hh], s_ref[hh])[0].astype(BF16) for hh in heads]
            for hh in heads:
                o_ref[rows, _head(hh)] = jnp.dot(probs[hh], vc[:, _head(hh // GROUP)],
                                                 preferred_element_type=F32).astype(BF16)

    whole = pl.BlockSpec((t, KV_W), lambda i: (0, 0))
    return pl.pallas_call(
        body, name="attn_fwd", grid=(t // tm,),
        in_specs=[pl.BlockSpec(memory_space=pltpu.SMEM), pl.BlockSpec((tm, ATTN_W), lambda i: (i, 0)), whole, whole,
                  pl.BlockSpec(bias.shape, lambda i: (0, 0, 0))],
        out_specs=pl.BlockSpec((tm, ATTN_W), lambda i: (i, 0)),
        out_shape=jax.ShapeDtypeStruct((t, ATTN_W), BF16),
        compiler_params=_params("parallel"),
    )(sinks, q, k, v, bias)


def _shift_rows(u, halo, n):
    r = pltpu.roll(u, n, 0)
    hr = pltpu.roll(halo, n, 0)
    idx = lax.broadcasted_iota(jnp.int32, hr.shape, 0)
    return jnp.concatenate([jnp.where(idx < n, hr, r[:8]), r[8:]], axis=0)


def _advance_rows(u, halo, n):
    rows = u.shape[0]
    r = pltpu.roll(u, rows - n, 0)
    hr = pltpu.roll(halo, 8 - n, 0)
    idx = lax.broadcasted_iota(jnp.int32, hr.shape, 0)
    return jnp.concatenate([r[:rows - 8], jnp.where(idx >= 8 - n, hr, r[rows - 8:])], axis=0)


def _mix_out(h, o, b, c, hc, cw, ga, gc, w, gp, tm, deps=()):
    t = h.shape[0]

    def body(h_ref, o_ref, b_ref, c_ref, hc_ref, cw_ref, ga_ref, gc_ref, w_ref, gp_ref, h1_ref, y_ref, z_ref, halo):
        @pl.when(pl.program_id(0) == 0)
        def _():
            halo[...] = jnp.zeros_like(halo)

        u = c_ref[...].astype(F32) * hc_ref[...].astype(F32)
        cv = cw_ref[0:1, :] * _shift_rows(u, halo[...], 2) + cw_ref[1:2, :] * _shift_rows(u, halo[...], 1) \
            + cw_ref[2:3, :] * u
        halo[...] = u[tm - 8:]
        yc = b_ref[...].astype(F32) * cv
        y = jnp.concatenate([_rms(o_ref[...].astype(F32), ga_ref[...]), _rms(yc, gc_ref[...])], axis=1).astype(BF16)
        y_ref[...] = y
        z = jnp.dot(y, w_ref[...].reshape(D_MODEL, D_MODEL), preferred_element_type=F32)
        z_ref[...] = z
        h1_ref[...] = h_ref[...] + _rms(z, gp_ref[...])

    row = lambda n: pl.BlockSpec((tm, n), lambda i: (i, 0))
    full = lambda a: pl.BlockSpec(a.shape, lambda i: (0,) * a.ndim)
    body, dep_specs = _behind(body, deps)
    return pl.pallas_call(
        body, name="mix_out", grid=(t // tm,),
        in_specs=dep_specs + [row(D_MODEL), row(ATTN_W), row(CONV_W), row(CONV_W), row(CONV_W), full(cw), full(ga),
                              full(gc), full(w), full(gp)],
        out_specs=[row(D_MODEL), row(D_MODEL), row(D_MODEL)],
        out_shape=[jax.ShapeDtypeStruct((t, D_MODEL), F32), jax.ShapeDtypeStruct((t, D_MODEL), BF16),
                   jax.ShapeDtypeStruct((t, D_MODEL), F32)],
        scratch_shapes=[pltpu.VMEM((8, CONV_W), F32)],
        compiler_params=_params("arbitrary"),
    )(*deps, h, o, b, c, hc, cw, ga, gc, w, gp)


def _mlp(h1, g1, wu, wd, g2, tm, target=None):
    t = h1.shape[0]
    nj = D_FF // FF_CHUNK
    per_step = tm // BLOCK if target is not None else 0

    def body(h1_ref, g1_ref, wu_hbm, wd_hbm, g2_ref, *rest):
        wu_vm, wd_vm, sem = rest[-3:]
        _with_chunked_weights(lambda up_w, down_w: one_step(h1_ref, g1_ref, up_w, down_w, g2_ref, *rest[:-3]),
                              (wu_hbm, wu_vm, sem.at[0]), (wd_hbm, wd_vm, sem.at[1]))

    def one_step(h1_ref, g1_ref, up_w, down_w, g2_ref, *rest):
        t_refs, outs = rest[:per_step], rest[per_step:]
        a2_ref, slope_ref = (outs[1], outs[2]) if target is None else (outs[2], outs[3])
        a2 = _rms(h1_ref[...], g1_ref[...]).astype(BF16)
        a2_ref[...] = a2
        f = None
        for j in range(nj):
            up = jnp.dot(a2, up_w(j), preferred_element_type=F32)
            r = jnp.maximum(up, 0.0)
            slope_ref[:, j * FF_CHUNK:(j + 1) * FF_CHUNK] = (r + r).astype(BF16)
            part = jnp.dot((r * r).astype(BF16), down_w(j), preferred_element_type=F32)
            f = part if f is None else f + part
        h2 = h1_ref[...] + _rms(f, g2_ref[...])
        if target is None:
            outs[0][...] = h2
            outs[3][...] = f
            return
        loss_ref, dh_ref, df_ref, dg2_ref = outs[0], outs[1], outs[4], outs[5]
        i = pl.program_id(0)

        @pl.when(i == 0)
        def _():
            loss_ref[...] = jnp.zeros_like(loss_ref)
            dg2_ref[...] = jnp.zeros_like(dg2_ref)

        total = jnp.zeros((), F32)
        dh2 = []
        for b in range(per_step):
            err = h2[b * BLOCK:(b + 1) * BLOCK] - t_refs[b][...]
            if b == 0:
                err = jnp.where(i == 0, 0.0, err)
            dh2.append(err * (1.0 / D_MODEL))
            total = total + jnp.sum(err * err)
        loss_ref[...] += total * (0.5 / D_MODEL)
        dh2 = jnp.concatenate(dh2, axis=0)
        dh_ref[...] = dh2
        df, dg = _rms_bwd(dh2, f, g2_ref[...])
        df_ref[...] = df.astype(BF16)
        dg2_ref[...] += dg

    def target_block(b):
        return pl.BlockSpec((BLOCK, D_MODEL), lambda i: (jnp.maximum(i * per_step + b - 1, 0), 0))

    row = pl.BlockSpec((tm, D_MODEL), lambda i: (i, 0))
    vec = pl.BlockSpec((1, D_MODEL), lambda i: (0, 0))
    resident = pl.BlockSpec(memory_space=pl.ANY)
    wide = pl.BlockSpec((tm, D_FF), lambda i: (i, 0))
    kept =[jax.ShapeDtypeStruct((t, D_MODEL), BF16), jax.ShapeDtypeStruct((t, D_FF), BF16)]
    if target is None:
        specs = [row, row, wide, row]
        shapes = [jax.ShapeDtypeStruct((t, D_MODEL), F32)] + kept + [jax.ShapeDtypeStruct((t, D_MODEL), F32)]
    else:
        specs = [pl.BlockSpec((8, 128), lambda i: (0, 0)), row, row, wide, row, vec]
        shapes = [jax.ShapeDtypeStruct((8, 128), F32), jax.ShapeDtypeStruct((t, D_MODEL), F32)] + kept \
            + [jax.ShapeDtypeStruct((t, D_MODEL), BF16), jax.ShapeDtypeStruct((1, D_MODEL), F32)]
    outs = pl.pallas_call(
        body, name="mlp", grid=(t // tm,),
        in_specs=[row, vec, resident, resident, vec] + [target_block(b) for b in range(per_step)],
        out_specs=specs, out_shape=shapes, scratch_shapes=_chunked_weight_scratch(wu, wd),
        compiler_params=_params("arbitrary"),
    )(h1, g1, wu, wd, g2, *([target] * per_step))
    if target is None:
        return tuple(outs)
    return (tuple(outs[:2]), outs[2], outs[3], tuple(outs[4:]))


def _mlp_bwd_hidden(dh2, f, g2, slope, wd, tm, deps=(), df=None):
    t = slope.shape[0]
    nj = D_FF // FF_CHUNK

    def hidden(df, slope_ref, down_w, dup_ref):
        for j in range(nj):
            cols = slice(j * FF_CHUNK, (j + 1) * FF_CHUNK)
            dact = lax.dot_general(df, down_w(j), (((1,), (1,)), ((), ())), preferred_element_type=F32)
            dup_ref[:, cols] = (dact * slope_ref[:, cols].astype(F32)).astype(BF16)

    def body(dh2_ref, f_ref, g2_ref, slope_ref, wd_hbm, df_ref, dup_ref, dg2_ref, wd_vm, sem):
        def one_step(down_w):
            @pl.when(pl.program_id(0) == 0)
            def _():
                dg2_ref[...] = jnp.zeros_like(dg2_ref)

            df, dg = _rms_bwd(dh2_ref[...], f_ref[...], g2_ref[...])
            dg2_ref[...] += dg
            df = df.astype(BF16)
            df_ref[...] = df
            hidden(df, slope_ref, down_w, dup_ref)

        _with_chunked_weights(one_step, (wd_hbm, wd_vm, sem.at[0]))

    def body_from_df(df_ref, slope_ref, wd_hbm, dup_ref, wd_vm, sem):
        _with_chunked_weights(lambda down_w: hidden(df_ref[...], slope_ref, down_w, dup_ref),
                              (wd_hbm, wd_vm, sem.at[0]))

    row = pl.BlockSpec((tm, D_MODEL), lambda i: (i, 0))
    wide = pl.BlockSpec((tm, D_FF), lambda i: (i, 0))
    vec = pl.BlockSpec((1, D_MODEL), lambda i: (0, 0))
    resident = pl.BlockSpec(memory_space=pl.ANY)
    if df is not None:
        body_from_df, dep_specs = _behind(body_from_df, deps)
        return pl.pallas_call(
            body_from_df, name="mlp_bwd_hidden", grid=(t // tm,), in_specs=dep_specs + [row, wide, resident],
            out_specs=wide, out_shape=jax.ShapeDtypeStruct((t, D_FF), BF16),
            scratch_shapes=_chunked_weight_scratch(wd), compiler_params=_params("arbitrary"),
        )(*deps, df, slope, wd)
    body, dep_specs = _behind(body, deps)
    return pl.pallas_call(
        body, name="mlp_bwd_hidden", grid=(t // tm,),
        in_specs=dep_specs + [row, row, vec, wide, resident],
        out_specs=[row, wide, vec],
        out_shape=[jax.ShapeDtypeStruct((t, D_MODEL), BF16), jax.ShapeDtypeStruct((t, D_FF), BF16),
                   jax.ShapeDtypeStruct((1, D_MODEL), F32)],
        scratch_shapes=_chunked_weight_scratch(wd), compiler_params=_params("arbitrary"),
    )(*deps, dh2, f, g2, slope, wd)


def _mlp_bwd_input(dup, wu, h1, g1, dh2, tm):
    t = dh2.shape[0]
    nj = D_FF // FF_CHUNK

    def body(dup_ref, wu_hbm, h1_ref, g1_ref, dh2_ref, dh1_ref, dg1_ref, wu_vm, sem):
        def one_step(up_w):
            @pl.when(pl.program_id(0) == 0)
            def _():
                dg1_ref[...] = jnp.zeros_like(dg1_ref)

            da2 = None
            for j in range(nj):
                part = lax.dot_general(dup_ref[:, j * FF_CHUNK:(j + 1) * FF_CHUNK], up_w(j), (((1,), (1,)), ((), ())),
                                       preferred_element_type=F32)
                da2 = part if da2 is None else da2 + part
            dx, dg = _rms_bwd(da2, h1_ref[...], g1_ref[...])
            dh1_ref[...] = dh2_ref[...] + dx
            dg1_ref[...] += dg

        _with_chunked_weights(one_step, (wu_hbm, wu_vm, sem.at[0]))

    row = pl.BlockSpec((tm, D_MODEL), lambda i: (i, 0))
    vec = pl.BlockSpec((1, D_MODEL), lambda i: (0, 0))
    return pl.pallas_call(
        body, name="mlp_bwd_input", grid=(t // tm,),
        in_specs=[pl.BlockSpec((tm, D_FF), lambda i: (i, 0)), pl.BlockSpec(memory_space=pl.ANY), row, vec, row],
        out_specs=[row, vec],
        out_shape=[jax.ShapeDtypeStruct((t, D_MODEL), F32), jax.ShapeDtypeStruct((1, D_MODEL), F32)],
        scratch_shapes=_chunked_weight_scratch(wu), compiler_params=_params("arbitrary"),
    )(dup, wu, h1, g1, dh2)


def _row_split(t):
    tile = min(t, 1024)
    return tile, t // tile, t % tile


def _row_split_specs(t, cols):
    tile, whole, rest = _row_split(t)
    specs = [pl.BlockSpec((tile, cols), lambda r: (jnp.maximum(r - (1 if rest else 0), 0), 0))]
    if rest:
        specs.append(pl.BlockSpec((rest, cols), lambda r: (whole * tile // rest, 0)))
    return specs


def _weight_grad(x, y, name, x_is_slope=False, deps=()):
    t, k = x.shape
    n = y.shape[1]
    tn = FF_CHUNK
    tk = FF_CHUNK if k % FF_CHUNK == 0 else k
    _, whole, rest = _row_split(t)
    steps = whole + bool(rest)
    tiles = [(a, b) for a in range(k // tk) for b in range(n // tn)]

    def body(*refs):
        o_ref, ob_ref, acc, accb, sem = refs[-5:]
        r = pl.program_id(0)

        def out_copies(a, b):
            return (pltpu.make_async_copy(acc.at[a, b], o_ref.at[a, b], sem.at[0, a, b]),
                    pltpu.make_async_copy(accb.at[a, b], ob_ref.at[a, b], sem.at[1, a, b]))

        def add(x_ref, y_ref, first, last):
            for a in range(k // tk):
                xv = x_ref[:, a * tk:(a + 1) * tk]
                if x_is_slope:
                    xv = xv.astype(F32)
                    xv = (xv * xv * 0.25).astype(BF16)
                for b in range(n // tn):
                    part = lax.dot_general(xv, y_ref[:, b * tn:(b + 1) * tn], (((0,), (0,)), ((), ())),
                                           preferred_element_type=F32)
                    if first:
                        acc[a, b] = part
                    else:
                        acc[a, b] += part
                    if last:
                        accb[a, b] = acc[a, b].astype(BF16)
                        for copy in out_copies(a, b):
                            copy.start()
            if last:
                for a, b in tiles:
                    for copy in out_copies(a, b):
                        copy.wait()

        mine = (1, 3) if rest else (0, 1)
        if steps == 1:
            add(refs[0], refs[1], True, True)
        else:
            pl.when(r == 0)(lambda: add(refs[mine[0]], refs[mine[1]], True, False))
            pl.when((r > 0) & (r < steps - 1))(lambda: add(refs[0], refs[2 if rest else 1], False, False))
            pl.when(r == steps - 1)(lambda: add(refs[0], refs[2 if rest else 1], False, True))

    shape = (k // tk, n // tn, tk, tn)
    body, dep_specs = _behind(body, deps)
    return pl.pallas_call(
        body, name=name, grid=(steps,),
        in_specs=dep_specs + _row_split_specs(t, k) + _row_split_specs(t, n),
        out_specs=[pl.BlockSpec(memory_space=pl.ANY)] * 2,
        out_shape=[jax.ShapeDtypeStruct(shape, F32), jax.ShapeDtypeStruct(shape, BF16)],
        scratch_shapes=[pltpu.VMEM(shape, F32), pltpu.VMEM(shape, BF16), pltpu.SemaphoreType.DMA((2,) + shape[:2])],
        compiler_params=_params("arbitrary"),
    )(*deps, *([x] * (1 + bool(rest))), *([y] * (1 + bool(rest))))


def _mix_out_bwd(dh1, z, gp, w, o, b, c, hc, cw, ga, gc, tm, deps=()):
    t = dh1.shape[0]
    nt = t // tm
    per16 = tm // 16

    def body(dh1_ref, z_ref, gp_ref, w_ref, o_ref, b_ref, c_ref, hc_ref, cp_ref, hp_ref, cw_ref, ga_ref, gc_ref,
             dz_ref, do_ref, dbch_ref, dgp_ref, dga_ref, dgc_ref, dcw_ref, halo):
        i = pl.program_id(0)

        @pl.when(i == 0)
        def _():
            halo[...] = jnp.zeros_like(halo)
            dgp_ref[...] = jnp.zeros_like(dgp_ref)
            dga_ref[...] = jnp.zeros_like(dga_ref)
            dgc_ref[...] = jnp.zeros_like(dgc_ref)
            dcw_ref[...] = jnp.zeros_like(dcw_ref)

        dz, dgp = _rms_bwd(dh1_ref[...], z_ref[...], gp_ref[...])
        dgp_ref[...] += dgp
        dz = dz.astype(BF16)
        dz_ref[...] = dz
        dy = lax.dot_general(dz, w_ref[...].reshape(D_MODEL, D_MODEL), (((1,), (1,)), ((), ())),
                             preferred_element_type=F32)
        do, dga = _rms_bwd(dy[:, :ATTN_W], o_ref[...].astype(F32), ga_ref[...])
        do_ref[...] = do.astype(BF16)
        dga_ref[...] += dga

        cc, hh = c_ref[...].astype(F32), hc_ref[...].astype(F32)
        u = cc * hh
        first = i == nt - 1
        u_before = jnp.where(first, 0.0, (cp_ref[...].astype(F32) * hp_ref[...].astype(F32))[8:])
        u1 = _shift_rows(u, u_before, 1)
        u2 = _shift_rows(u, u_before, 2)
        cv = cw_ref[0:1, :] * u2 + cw_ref[1:2, :] * u1 + cw_ref[2:3, :] * u
        bb = b_ref[...].astype(F32)
        dyc, dgc = _rms_bwd(dy[:, ATTN_W:], bb * cv, gc_ref[...])
        dgc_ref[...] += dgc
        dcv = dyc * bb
        d1 = _advance_rows(dcv, halo[...], 1)
        d2 = _advance_rows(dcv, halo[...], 2)
        halo[...] = dcv[:8]
        du = cw_ref[2:3, :] * dcv + cw_ref[1:2, :] * d1 + cw_ref[0:1, :] * d2
        dbch_ref[...] = jnp.concatenate([dyc * cv, du * hh, du * cc], axis=1).astype(BF16)
        dcw_ref[...] += jnp.concatenate([jnp.sum(dcv * u2, axis=0, keepdims=True),
                                         jnp.sum(dcv * u1, axis=0, keepdims=True),
                                         jnp.sum(dcv * u, axis=0, keepdims=True)], axis=0)

    row = lambda n: pl.BlockSpec((tm, n), lambda i: (nt - 1 - i, 0))
    before = pl.BlockSpec((16, CONV_W), lambda i: (jnp.maximum((nt - 1 - i) * per16 - 1, 0), 0))
    full = lambda a: pl.BlockSpec(a.shape, lambda i: (0,) * a.ndim)
    vec = lambda n: pl.BlockSpec((1, n), lambda i: (0, 0))
    body, dep_specs = _behind(body, deps)
    return pl.pallas_call(
        body, name="mix_out_bwd", grid=(nt,),
        in_specs=dep_specs + [row(D_MODEL), row(D_MODEL), full(gp), full(w), row(ATTN_W), row(CONV_W), row(CONV_W),
                              row(CONV_W), before, before, full(cw), full(ga), full(gc)],
        out_specs=[row(D_MODEL), row(ATTN_W), row(3 * CONV_W), vec(D_MODEL), vec(ATTN_W), vec(CONV_W),
                   pl.BlockSpec((CONV_K, CONV_W), lambda i: (0, 0))],
        out_shape=[jax.ShapeDtypeStruct((t, D_MODEL), BF16), jax.ShapeDtypeStruct((t, ATTN_W), BF16),
                   jax.ShapeDtypeStruct((t, 3 * CONV_W), BF16), jax.ShapeDtypeStruct((1, D_MODEL), F32),
                   jax.ShapeDtypeStruct((1, ATTN_W), F32), jax.ShapeDtypeStruct((1, CONV_W), F32),
                   jax.ShapeDtypeStruct((CONV_K, CONV_W), F32)],
        scratch_shapes=[pltpu.VMEM((8, CONV_W), F32)],
        compiler_params=_params("arbitrary"),
    )(*deps, dh1, z, gp, w, o, b, c, hc, c, hc, cw, ga, gc)


def _attn_bwd(q, k, v, o, do, bias, sinks, tm, deps=()):
    t = q.shape[0]
    per_step = tm // BLOCK

    def body(s_ref, q_ref, k_ref, v_ref, o_ref, do_ref, bias_ref, dq_ref, dk_ref, dv_ref, ds_ref):
        step = pl.program_id(0)

        @pl.when(step == 0)
        def _():
            ds_ref[...] = jnp.zeros_like(ds_ref)

        heads = range(N_Q_HEADS)

        def first_matmuls(b):
            i = step * per_step + b
            rows = slice(b * BLOCK, (b + 1) * BLOCK)
            kc, vc = _two_blocks(k_ref, i), _two_blocks(v_ref, i)
            bias_i = bias_ref[jnp.minimum(i, 2)]
            kgs = [kc[:, _head(g)] for g in range(N_KV_HEADS)]
            vgs = [vc[:, _head(g)] for g in range(N_KV_HEADS)]
            qs = [q_ref[rows, _head(hh)] for hh in heads]
            dosb = [do_ref[rows, _head(hh)] for hh in heads]
            dos = [d.astype(F32) for d in dosb]
            scores = [_attn_scores(qs[hh], kgs[hh // GROUP], bias_i) for hh in heads]
            dps = [lax.dot_general(dosb[hh], vgs[hh // GROUP], (((1,), (1,)), ((), ())), preferred_element_type=F32)
                   for hh in heads]
            return kgs, qs, dos, dosb, scores, dps

        dsink = [jnp.zeros((BLOCK, 1), F32) for _ in range(N_Q_HEADS)]
        ahead = None
        for b in range(per_step):
            i = step * per_step + b
            rows = slice(b * BLOCK, (b + 1) * BLOCK)
            kgs, qs, dos, dosb, scores, dps = first_matmuls(b)
            ps, dss = [], []
            for hh in heads:
                p, share = _attn_probs(scores[hh], s_ref[hh])
                drow = jnp.sum(dos[hh] * o_ref[rows, _head(hh)].astype(F32), axis=-1, keepdims=True)
                dss.append((p * (dps[hh] - drow)).astype(BF16))
                ps.append(p.astype(BF16))
                dsink[hh] = dsink[hh] + share * drow
            for hh in heads:
                dq_ref[rows, _head(hh)] = (jnp.dot(dss[hh], kgs[hh // GROUP], preferred_element_type=F32)
                                           * SCALE).astype(BF16)
            groups = [slice(GROUP * g, GROUP * (g + 1)) for g in range(N_KV_HEADS)]
            dkg = [lax.dot_general(jnp.concatenate(dss[gr], axis=0), jnp.concatenate(qs[gr], axis=0),
                                   (((0,), (0,)), ((), ())), preferred_element_type=F32) for gr in groups]
            dvg = [lax.dot_general(jnp.concatenate(ps[gr], axis=0), jnp.concatenate(dosb[gr], axis=0),
                                   (((0,), (0,)), ((), ())), preferred_element_type=F32) for gr in groups]
            dkb, dvb = jnp.concatenate(dkg, axis=1), jnp.concatenate(dvg, axis=1)
            if b == 0:
                @pl.when(step > 0)
                def _():
                    before = pl.ds(pl.multiple_of((i - 1) * BLOCK, BLOCK), BLOCK)
                    dk_ref[before, :] += dkb[:BLOCK]
                    dv_ref[before, :] += dvb[:BLOCK]
            else:
                at = pl.ds(pl.multiple_of((i - 1) * BLOCK, BLOCK), BLOCK)
                dk_ref[at, :] = ahead[0] + dkb[:BLOCK]
                dv_ref[at, :] = ahead[1] + dvb[:BLOCK]
            ahead = (dkb[BLOCK:], dvb[BLOCK:])
        last = pl.ds(pl.multiple_of(((step + 1) * per_step - 1) * BLOCK, BLOCK), BLOCK)
        dk_ref[last, :] = ahead[0]
        dv_ref[last, :] = ahead[1]
        for hh in range(N_Q_HEADS):
            ds_ref[hh:hh + 1, :] -= jnp.sum(dsink[hh])

    whole = pl.BlockSpec((t, KV_W), lambda i: (0, 0))
    blk = pl.BlockSpec((tm, ATTN_W), lambda i: (i, 0))
    body, dep_specs = _behind(body, deps)
    return pl.pallas_call(
        body, name="attn_bwd", grid=(t // tm,),
        in_specs=dep_specs + [pl.BlockSpec(memory_space=pltpu.SMEM), blk, whole, whole, blk, blk,
                              pl.BlockSpec(bias.shape, lambda i: (0, 0, 0))],
        out_specs=[blk, whole, whole, pl.BlockSpec((N_Q_HEADS, 128), lambda i: (0, 0))],
        out_shape=[jax.ShapeDtypeStruct((t, ATTN_W), BF16), jax.ShapeDtypeStruct((t, KV_W), F32),
                   jax.ShapeDtypeStruct((t, KV_W), F32), jax.ShapeDtypeStruct((N_Q_HEADS, 128), F32)],
        compiler_params=_params("arbitrary"),
    )(*deps, sinks, q, k, v, o, do, bias)


def _in_proj_bwd(dq, dk, dv, dbch, w, dh1, h, g, tabs, tm, split_lead=False):
    t = h.shape[0]
    nt = t // tm

    def body(dq_ref, dk_ref, dv_ref, dbch_ref, w_ref, dh1_ref, h_ref, g_ref, c_ref, s_ref, *rest):
        dp_ref, dg_ref = rest[2:4] if split_lead else rest[1:3]
        i = pl.program_id(0)

        @pl.when(i == 0)
        def _():
            dg_ref[...] = jnp.zeros_like(dg_ref)

        cos, sa, sb = _rope_factors(c_ref[...], s_ref[...])
        rep = ATTN_W // (2 * HEAD_DIM)
        dqr = _rope_bwd(dq_ref[...].astype(F32), jnp.tile(cos, (1, rep)), jnp.tile(sa, (1, rep)),
                        jnp.tile(sb, (1, rep)))
        dkr = _rope_bwd(dk_ref[...], cos, sa, sb)
        dp = jnp.concatenate([dqr.astype(BF16), dkr.astype(BF16), dv_ref[...].astype(BF16), dbch_ref[...]], axis=1)
        dp_ref[...] = dp
        da = jnp.dot(dp, w_ref[...], preferred_element_type=F32)
        dx, dg = _rms_bwd(da, h_ref[...], g_ref[...])
        dg_ref[...] += dg
        dh = dh1_ref[...] + dx
        if not split_lead:
            rest[0][...] = dh
            return
        lead_ref, seq_ref, stage, sems = rest[0], rest[1], rest[4], rest[5]

        def copy(j, slot, first):
            if first:
                return pltpu.make_async_copy(stage.at[slot, pl.ds(BLOCK, tm - BLOCK)],
                                             seq_ref.at[pl.ds(0, tm - BLOCK)], sems.at[slot])
            return pltpu.make_async_copy(stage.at[slot], seq_ref.at[pl.ds(pl.multiple_of(j * tm - BLOCK, BLOCK), tm)],
                                         sems.at[slot])

        slot = i % 2
        pl.when(i == 2)(lambda: copy(0, slot, True).wait())
        pl.when(i > 2)(lambda: copy(i - 2, slot, False).wait())
        stage[slot] = dh

        @pl.when(i == 0)
        def _():
            lead_ref[...] = dh[:BLOCK]
            copy(0, slot, True).start()

        pl.when(i > 0)(lambda: copy(i, slot, False).start())

        @pl.when(i == nt - 1)
        def _():
            for j in range(max(nt - 2, 0), nt):
                copy(j, j % 2, j == 0).wait()

    row = lambda n: pl.BlockSpec((tm, n), lambda i: (i, 0))
    full = lambda a: pl.BlockSpec(a.shape, lambda i: (0, 0))
    dh_specs, dh_shapes, scratch = [row(D_MODEL)], [jax.ShapeDtypeStruct((t, D_MODEL), F32)], []
    if split_lead:
        dh_specs = [pl.BlockSpec((BLOCK, D_MODEL), lambda i: (0, 0)), pl.BlockSpec(memory_space=pl.ANY)]
        dh_shapes = [jax.ShapeDtypeStruct((BLOCK, D_MODEL), F32), jax.ShapeDtypeStruct((t - BLOCK, D_MODEL), F32)]
        scratch = [pltpu.VMEM((2, tm, D_MODEL), F32), pltpu.SemaphoreType.DMA((2,))]
    outs = pl.pallas_call(
        body, name="in_proj_bwd", grid=(nt,),
        in_specs=[row(ATTN_W), row(KV_W), row(KV_W), row(3 * CONV_W), full(w), row(D_MODEL), row(D_MODEL), full(g),
                  row(2 * HEAD_DIM), row(2 * HEAD_DIM)],
        out_specs=dh_specs + [row(IN_W), pl.BlockSpec((1, D_MODEL), lambda i: (0, 0))],
        out_shape=dh_shapes + [jax.ShapeDtypeStruct((t, IN_W), BF16), jax.ShapeDtypeStruct((1, D_MODEL), F32)],
        scratch_shapes=scratch,
        compiler_params=_params("arbitrary"),
    )(dq, dk, dv, dbch, w, dh1, h, g, *tabs)
    return (tuple(outs[:2]) if split_lead else outs[0],) + tuple(outs[-2:])


class _Tiles:
    def __init__(self, t):
        self.tm = _row_tile(t, 640)
        self.ts = self.tm
        self.tabs = _rope_tables(t)
        self.bias = _attn_bias()


def _mixer_fwd(h, p, tl, lead=None):
    if lead is None:
        a, q, k, v, b, c, hc = _in_proj(h, p["mix_pre_g"], p["w_in"], tl.tabs, tl.ts)
    else:
        h, a, q, k, v, b, c, hc = _in_proj(h, p["mix_pre_g"], p["w_in"], tl.tabs, tl.ts, lead)
    o = _attn_fwd(q, k, v, tl.bias, p["sinks"], tl.tm)
    return (h, a, q, k, v, b, c, hc, o)


def _out_fwd(mixed, p, tl, deps=()):
    h, a, q, k, v, b, c, hc, o = mixed
    h1, y, z = _mix_out(h, o, b, c, hc, p["conv_w"], p["attn_out_g"], p["conv_out_g"], p["w_out"], p["mix_post_g"],
                        tl.ts, deps)
    return h1, mixed + (h1, y, z)


def _mlp_fwd(h1, saved, p, tl, target=None):
    h2, a2, slope, f = _mlp(h1, p["mlp_pre_g"], p["w_up"], p["w_down"], p["mlp_post_g"], tl.tm, target)
    return h2, saved + (a2, slope, f)


def _mlp_part_bwd(dh, saved, p, tl, deps=()):
    h1, a2, slope, f = saved[9], saved[12], saved[13], saved[14]
    if isinstance(f, tuple):
        df, dg2 = f
        dup = _mlp_bwd_hidden(None, None, None, slope, p["w_down"], tl.tm, deps, df)
    else:
        df, dup, dg2 = _mlp_bwd_hidden(dh, f, p["mlp_post_g"], slope, p["w_down"], tl.tm, deps)
    dh1, dg1 = _mlp_bwd_input(dup, p["w_up"], h1, p["mlp_pre_g"], dh, tl.tm)
    g = {"w_down": [d.reshape(N_CHIPS, FF_CHUNK, D_MODEL)
                    for d in _weight_grad(slope, df, "grad_w_down", x_is_slope=True)],
         "w_up": [d.reshape(N_CHIPS, D_MODEL, FF_CHUNK) for d in _weight_grad(a2, dup, "grad_w_up")],
         "mlp_post_g": dg2, "mlp_pre_g": dg1}
    return dh1, g


def _mix_out_part_bwd(dh1, saved, p, tl, deps=()):
    b, c, hc, o, y, z = saved[5], saved[6], saved[7], saved[8], saved[10], saved[11]
    dz, do, dbch, dgp, dga, dgc, dcw = _mix_out_bwd(dh1, z, p["mix_post_g"], p["w_out"], o, b, c, hc, p["conv_w"],
                                                    p["attn_out_g"], p["conv_out_g"], tl.ts, deps)
    g = {"w_out": [d.reshape(N_CHIPS, D_MODEL // N_CHIPS, D_MODEL) for d in _weight_grad(y, dz, "grad_w_out")],
         "mix_post_g": dgp, "attn_out_g": dga, "conv_out_g": dgc, "conv_w": dcw}
    return (dh1, do, dbch), g


def _attn_in_part_bwd(carry, saved, p, tl, deps=(), split_lead=False):
    dh1, do, dbch = carry
    h_in, q, k, v, o = saved[0], saved[2], saved[3], saved[4], saved[8]
    dq, dk, dv, dsink = _attn_bwd(q, k, v, o, do, tl.bias, p["sinks"], tl.tm, deps)
    dh, dproj, dgi = _in_proj_bwd(dq, dk, dv, dbch, p["w_in"], dh1, h_in, p["mix_pre_g"], tl.tabs, tl.ts, split_lead)
    return dh, dproj, {"mix_pre_g": dgi, "sinks": dsink[:, 0]}


def _in_grad(dproj, saved, deps=()):
    return [d.reshape(N_CHIPS, IN_W // N_CHIPS, D_MODEL) for d in _weight_grad(dproj, saved[1], "grad_w_in", deps=deps)]


def _place():
    return lax.axis_index("x"), lax.axis_index("y"), lax.axis_index("c")


def _other_chips(x, y):
    return [(1 - x, y), (x, 1 - y), (1 - x, 1 - y)]


_HBM = pl.BlockSpec(memory_space=pltpu.HBM)
_SEM = pl.BlockSpec(memory_space=pltpu.SEMAPHORE)
_EFFECT = pltpu.SideEffectType.DATAFLOW_SIDE_EFFECTING


class _Exchange:
    def __init__(self, name, bufs, plan, n, after=()):
        self.name, self.plan, nb = name, plan, len(bufs)
        n_in = nb + len(after)

        def body(*refs):
            send, recv, token = refs[n_in], refs[n_in + 1], refs[-1]
            for k, (src, dst, target, _) in enumerate(plan(refs[:nb])):
                pltpu.make_async_remote_copy(src_ref=src, dst_ref=dst, send_sem=send.at[k], recv_sem=recv.at[k],
                                             device_id=target, device_id_type=MESH).start()
            token[...] = jnp.zeros_like(token)

        outs = pl.pallas_call(
            body, name=name + "_start",
            out_shape=(pltpu.SemaphoreType.DMA((n,)), pltpu.SemaphoreType.DMA((n,)),
                       *[pltpu.HBM(b.shape, b.dtype) for b in bufs], jax.ShapeDtypeStruct((8, 128), F32)),
            in_specs=[_HBM] * nb + [pl.BlockSpec(memory_space=pl.ANY)] * len(after),
            out_specs=(_SEM, _SEM, *[_HBM] * nb, pl.BlockSpec(memory_space=pltpu.VMEM)),
            input_output_aliases={i: 2 + i for i in range(nb)},
            compiler_params=pltpu.CompilerParams(has_side_effects=_EFFECT),
        )(*[pltpu.with_memory_space_constraint(b, pltpu.HBM) for b in bufs], *after)
        self.send, self.recv, self.bufs, self.token = outs[0], outs[1], list(outs[2:2 + nb]), outs[-1]

    def wait(self, *after):
        plan, nb = self.plan, len(self.bufs)

        def body(*refs):
            send, recv = refs[nb], refs[nb + 1]
            for k, (src, _, target, land) in enumerate(plan(refs[:nb])):
                cp = pltpu.make_async_remote_copy(src_ref=src, dst_ref=land, send_sem=send.at[k], recv_sem=recv.at[k],
                                                  device_id=target, device_id_type=MESH)
                cp.wait_send()
                cp.wait_recv()

        outs = pl.pallas_call(
            body, name=self.name + "_wait", out_shape=[pltpu.HBM(b.shape, b.dtype) for b in self.bufs],
            in_specs=[_HBM] * nb + [_SEM, _SEM] + [pl.BlockSpec(memory_space=pl.ANY)] * len(after),
            out_specs=[_HBM] * nb, input_output_aliases={i: i for i in range(nb)},
            compiler_params=pltpu.CompilerParams(has_side_effects=_EFFECT),
        )(*self.bufs, self.send, self.recv, *after)
        return list(outs)


def _gather_plan(n):
    def plan(refs):
        x, y, c = _place()
        me = 2 * x + y
        return [(refs[a].at[me], refs[a].at[me], (px, py, c), refs[a].at[2 * px + py])
                for a in range(n) for px, py in _other_chips(x, y)]

    return plan


def _gather_half_plan(n, half_rows):
    def plan(refs):
        x, y, c = _place()
        me = 2 * x + y
        out = []
        for a in range(n):
            rows = pl.ds(c * half_rows[a], half_rows[a])
            out += [(refs[a].at[me, rows], refs[a].at[me, rows], (px, py, c), refs[a].at[2 * px + py, rows])
                    for px, py in _other_chips(x, y)]
        return out

    return plan


def _hand_over_plan(n, half_rows):
    def plan(refs):
        x, y, c = _place()
        out = []
        for a in range(n):
            mine, theirs = pl.ds(c * half_rows[a], half_rows[a]), pl.ds((1 - c) * half_rows[a], half_rows[a])
            for px, py in _other_chips(x, y):
                held = refs[a].at[2 * px + py, mine]
                out.append((held, held, (x, y, 1 - c), refs[a].at[2 * px + py, theirs]))
        return out

    return plan


def _peers():
    x, y, c = _place()
    return [(k - 1, (x ^ (k >> 2), y ^ ((k >> 1) & 1), c ^ (k & 1))) for k in range(1, N_DEV)]


def _scatter_plan(n, half_rows):
    def plan(refs):
        out = []
        for a in range(n):
            hr = half_rows[a]
            for k, (px, py, pc) in _peers():
                out.append((refs[a].at[2 * px + py, pl.ds(pc * hr, hr)], refs[n + a].at[k], (px, py, pc),
                            refs[n + a].at[k]))
        return out

    return plan


def _join_plan(n):
    def plan(refs):
        x, y, c = _place()
        return [(refs[a].at[c], refs[a].at[c], (x, y, 1 - c), refs[a].at[1 - c]) for a in range(n)]

    return plan


def _sum_parts(gs, qs):
    n = len(gs)
    half_rows = [g.shape[1] // 2 for g in gs]
    tr = [_block_rows(hr) for hr in half_rows]
    per = [hr // t for hr, t in zip(half_rows, tr)]
    x, y, c = _place()
    where = jnp.stack([2 * x + y, c]).astype(jnp.int32)

    def body(where_ref, *refs):
        i = pl.program_id(0)
        for a in range(n):
            g_ref, q_ref, o_ref = refs[a], refs[n + a], refs[2 * n + a]

            @pl.when(i < per[a])
            def _():
                total = g_ref[...]
                for k in range(N_DEV - 1):
                    total = total + q_ref[k].astype(F32)
                o_ref[...] = total

    def at(a, i):
        return jnp.minimum(i, per[a] - 1)

    specs_g = [pl.BlockSpec((None, tr[a], gs[a].shape[2]),
                            lambda i, where_ref, a=a: (where_ref[0], where_ref[1] * per[a] + at(a, i), 0)) for a in range(n)]
    specs_q = [pl.BlockSpec((N_DEV - 1, tr[a], gs[a].shape[2]), lambda i, where_ref, a=a: (0, at(a, i), 0))
               for a in range(n)]
    specs_o = [pl.BlockSpec((None, tr[a], gs[a].shape[2]), lambda i, where_ref, a=a: (where_ref[1], at(a, i), 0))
               for a in range(n)]
    return pl.pallas_call(
        body, name="sum_parts",
        grid_spec=pltpu.PrefetchScalarGridSpec(num_scalar_prefetch=1, grid=(max(per),), in_specs=specs_g + specs_q,
                                               out_specs=specs_o),
        out_shape=[jax.ShapeDtypeStruct((2, hr, g.shape[2]), F32) for g, hr in zip(gs, half_rows)],
        compiler_params=_params("arbitrary"),
    )(where, *gs, *qs)


def _all_plan(refs):
    x, y, c = _place()
    mine = refs[0].at[4 * x + 2 * y + c]
    return [(mine, mine, (px, py, pc), refs[0].at[4 * px + 2 * py + pc]) for _, (px, py, pc) in _peers()]


def _sum_devices(parts):
    def body(p_ref, o_ref):
        total = p_ref[0]
        for d in range(1, N_DEV):
            total = total + p_ref[d]
        o_ref[...] = total

    vm = pl.BlockSpec(memory_space=pltpu.VMEM)
    return pl.pallas_call(body, name="sum_devices", in_specs=[vm], out_specs=vm,
                          out_shape=jax.ShapeDtypeStruct(parts.shape[1:], F32))(parts)


def _adamw_math(w, g, m, v):
    m = ADAM_B1 * m + (1.0 - ADAM_B1) * g
    v = ADAM_B2 * v + (1.0 - ADAM_B2) * jnp.square(g)
    m_hat = m / (1.0 - ADAM_B1 ** ADAM_STEP)
    v_hat = v / (1.0 - ADAM_B2 ** ADAM_STEP)
    delta = -ADAM_LR * (m_hat / (jnp.sqrt(v_hat) + ADAM_EPS) + ADAM_WD * w)
    return delta, m, v


def _adamw_large(layer, ws, halves, ms, vs, others):
    n = len(ws)
    tr = [_block_rows(w.shape[1] // 2) for w in ws]
    per = [w.shape[1] // 2 // t for w, t in zip(ws, tr)]
    kept = [] if others is None else [a for four in others for a in four]

    def body(*refs):
        i = pl.program_id(0)
        outs = refs[4 * n + len(kept):]
        for a in range(n):
            w_ref, g_ref, m_ref, v_ref = refs[a], refs[n + a], refs[2 * n + a], refs[3 * n + a]
            g_out, d_ref, nm_ref, nv_ref = outs[4 * a:4 * a + 4]

            @pl.when(i < 2 * per[a])
            def _():
                g = g_ref[...]
                g_out[...] = g
                d_ref[...], nm_ref[...], nv_ref[...] = _adamw_math(w_ref[...], g, m_ref[...], v_ref[...])

    def at(a, i):
        return jnp.minimum(i, 2 * per[a] - 1)

    blk = [pl.BlockSpec((None, tr[a], ws[a].shape[2]), lambda i, a=a: (layer, at(a, i), 0)) for a in range(n)]
    half = [pl.BlockSpec((None, tr[a], ws[a].shape[2]), lambda i, a=a: (at(a, i) // per[a], at(a, i) % per[a], 0))
            for a in range(n)]
    outs = pl.pallas_call(
        body, name="adamw_large", grid=(2 * max(per),),
        in_specs=blk + half + blk + blk + [pl.BlockSpec(memory_space=pl.ANY)] * len(kept),
        out_specs=[blk[a] for a in range(n) for _ in range(4)],
        out_shape=[jax.ShapeDtypeStruct(w.shape, F32) for w in ws for _ in range(4)],
        input_output_aliases={4 * n + k: k for k in range(len(kept))},
        compiler_params=_params("arbitrary"),
    )(*ws, *halves, *ms, *vs, *kept)
    return [outs[4 * a:4 * a + 4] for a in range(n)]


def _adamw_small(ws, gs, ms, vs):
    n = len(ws)

    def body(*refs):
        w_r, g_r, m_r, v_r = refs[:n], refs[n:2 * n], refs[2 * n:3 * n], refs[3 * n:4 * n]
        d_r, nm_r, nv_r = refs[4 * n:5 * n], refs[5 * n:6 * n], refs[6 * n:]
        for a in range(n):
            d_r[a][...], nm_r[a][...], nv_r[a][...] = _adamw_math(w_r[a][...], g_r[a][...], m_r[a][...], v_r[a][...])

    vm = pl.BlockSpec(memory_space=pltpu.VMEM)
    outs = pl.pallas_call(
        body, name="adamw_small", in_specs=[vm] * (4 * n), out_specs=[vm] * (3 * n),
        out_shape=[jax.ShapeDtypeStruct(w.shape, F32) for w in ws] * 3,
    )(*ws, *gs, *ms, *vs)
    return outs[:n], outs[n:2 * n], outs[2 * n:]


_LARGE = ("w_in", "w_out", "w_up", "w_down")
_SMALL = ("meta_tokens", "mix_pre_g", "conv_w", "sinks", "attn_out_g", "conv_out_g", "mix_post_g", "mlp_pre_g",
          "mlp_post_g")
_ORDER = ("meta_tokens", "mix_pre_g", "w_in", "conv_w", "sinks", "attn_out_g", "conv_out_g", "w_out", "mix_post_g",
          "mlp_pre_g", "w_up", "w_down", "mlp_post_g")


class _Reduce:
    def __init__(self, name, grads, after=()):
        self.name, self.n = name, len(grads)
        self.own = [g for g, _ in grads]
        half_rows = [g.shape[1] // 2 for g in self.own]
        zones = [lax.empty((N_DEV - 1, hr, g.shape[2]), BF16) for g, hr in zip(self.own, half_rows)]
        self.exchange = _Exchange(name + "_scatter", [b for _, b in grads] + zones, _scatter_plan(self.n, half_rows),
                                  (N_DEV - 1) * self.n, after)

    @property
    def token(self):
        return self.exchange.token

    def join(self, *after):
        bufs = self.exchange.wait(*after)
        halves = list(_sum_parts(self.own, bufs[self.n:]))
        self.exchange = _Exchange(self.name + "_join", halves, _join_plan(self.n), self.n)

    def done(self, *after):
        return self.exchange.wait(*after)


def _pad_cols(a, n=D_MODEL):
    return jnp.pad(a, ((0, 0), (0, n - a.shape[1])))


def kernel(x, meta_tokens, mix_pre_g, w_in, conv_w, sinks, attn_out_g, conv_out_g, w_out, mix_post_g, mlp_pre_g, w_up, w_down, mlp_post_g, loss_target, m_meta_tokens, m_mix_pre_g, m_w_in, m_conv_w, m_sinks, m_attn_out_g, m_conv_out_g, m_w_out, m_mix_post_g, m_mlp_pre_g, m_w_up, m_w_down, m_mlp_post_g, v_meta_tokens, v_mix_pre_g, v_w_in, v_conv_w, v_sinks, v_attn_out_g, v_conv_out_g, v_w_out, v_mix_post_g, v_mlp_pre_g, v_w_up, v_w_down, v_mlp_post_g):
    w = dict(meta_tokens=meta_tokens, mix_pre_g=mix_pre_g, w_in=w_in, conv_w=conv_w, sinks=sinks,
             attn_out_g=attn_out_g, conv_out_g=conv_out_g, w_out=w_out, mix_post_g=mix_post_g, mlp_pre_g=mlp_pre_g,
             w_up=w_up, w_down=w_down, mlp_post_g=mlp_post_g)
    m = dict(meta_tokens=m_meta_tokens, mix_pre_g=m_mix_pre_g, w_in=m_w_in, conv_w=m_conv_w, sinks=m_sinks,
             attn_out_g=m_attn_out_g, conv_out_g=m_conv_out_g, w_out=m_w_out, mix_post_g=m_mix_post_g,
             mlp_pre_g=m_mlp_pre_g, w_up=m_w_up, w_down=m_w_down, mlp_post_g=m_mlp_post_g)
    v = dict(meta_tokens=v_meta_tokens, mix_pre_g=v_mix_pre_g, w_in=v_w_in, conv_w=v_conv_w, sinks=v_sinks,
             attn_out_g=v_attn_out_g, conv_out_g=v_conv_out_g, w_out=v_w_out, mix_post_g=v_mix_post_g,
             mlp_pre_g=v_mlp_pre_g, w_up=v_w_up, w_down=v_w_down, mlp_post_g=v_mlp_post_g)
    chip = 2 * lax.axis_index("x") + lax.axis_index("y")
    tl = _Tiles(x.shape[1] + BLOCK)

    def zone(quarter):
        return lax.dynamic_update_slice(lax.empty((N_CHIPS,) + quarter.shape, quarter.dtype), quarter[None],
                                        (chip,) + (0,) * quarter.ndim)

    w, m, v = ({**d, "w_in": jnp.swapaxes(d["w_in"], 1, 2)} for d in (w, m, v))
    zones = {n: [zone(w[n][l].astype(BF16)) for l in range(DEPTH)] for n in _LARGE}
    first = _Exchange("gather_first", [zones["w_in"][0], zone(w["conv_w"]), zone(w["meta_tokens"])], _gather_plan(3), 9)
    out0 = _Exchange("gather_out", [zones["w_out"][0]], _gather_plan(1), 3, [first.token])
    mlp_halves = [D_MODEL // 2, FF_CHUNK // 2]
    rest = _Exchange("gather_rest", [zones[n][0] for n in ("w_up", "w_down")], _gather_half_plan(2, mlp_halves), 6,
                     [out0.token])

    def whole_in(quarters):
        return quarters.reshape(IN_W, D_MODEL)

    q_in, q_conv, q_meta = first.wait(rest.token, *tl.tabs, tl.bias)
    conv_whole = jnp.transpose(q_conv, (1, 2, 0, 3)).reshape(DEPTH, CONV_K, CONV_W)
    meta = jnp.transpose(q_meta, (1, 0, 2)).reshape(N_META, D_MODEL)
    p = [{"conv_w": conv_whole[l], "sinks": w["sinks"][l]} for l in range(DEPTH)]
    for l in range(DEPTH):
        for n in ("mix_pre_g", "attn_out_g", "conv_out_g", "mix_post_g", "mlp_pre_g", "mlp_post_g"):
            p[l][n] = w[n][l][None, :]

    lead = jnp.concatenate([jnp.zeros((LEAD_PAD, D_MODEL), F32), meta], axis=0)
    p[0]["w_in"] = whole_in(q_in)
    mixed = _mixer_fwd(x[0], p[0], tl, lead)
    second = _Exchange("gather_second", [zones["w_in"][1], zones["w_out"][1]], _gather_plan(2), 6, [mixed[-1]])
    second_mlp = _Exchange("gather_second_mlp", [zones["w_up"][1], zones["w_down"][1]], _gather_plan(2), 6,
                           [second.token])
    hand_over = _Exchange("hand_over_rest", rest.wait(second_mlp.token), _hand_over_plan(2, mlp_halves), 6)
    p[0]["w_out"], = out0.wait(hand_over.token)
    h1, saved0 = _out_fwd(mixed, p[0], tl)
    p[0]["w_up"], p[0]["w_down"] = hand_over.wait(h1)
    h, saved0 = _mlp_fwd(h1, saved0, p[0], tl)
    q_in, p[1]["w_out"] = second.wait(h)
    p[1]["w_in"] = whole_in(q_in)
    h1, saved1 = _out_fwd(_mixer_fwd(h, p[1], tl), p[1], tl)
    p[1]["w_up"], p[1]["w_down"] = second_mlp.wait(h1)
    (loss_tile, dh), saved1 = _mlp_fwd(h1, saved1, p[1], tl, loss_target[0])

    def adamw(layer, halves, other):
        names = list(halves)
        done = _adamw_large(layer, [w[n] for n in names], [halves[n] for n in names], [m[n] for n in names],
                            [v[n] for n in names], None if other is None else [other[n] for n in names])
        return dict(zip(names, done))

    dh1, g1 = _mlp_part_bwd(dh, saved1, p[1], tl)
    carry, gm = _mix_out_part_bwd(dh1, saved1, p[1], tl)
    dh, dproj, gi = _attn_in_part_bwd(carry, saved1, p[1], tl)
    g1.update(gm, w_in=_in_grad(dproj, saved1), **gi)
    red1 = _Reduce("reduce1", [g1[n] for n in _LARGE])
    dh1, g0 = _mlp_part_bwd(dh, saved0, p[0], tl, [red1.token])
    red1.join(g0["w_down"][0])
    carry, gm = _mix_out_part_bwd(dh1, saved0, p[0], tl, [red1.token])
    first0 = ("w_up", "w_down", "w_out")
    g0.update(gm)
    red0a = _Reduce("reduce0a", [g0[n] for n in first0])
    (dlead, dseq), dproj, gi = _attn_in_part_bwd(carry, saved0, p[0], tl, [red0a.token], split_lead=True)
    g0.update(gi)
    grad_x = dseq[None]
    grads = {n: [g0[n], g1[n]] for n in g0 if n not in _LARGE}

    rows = [dlead[LEAD_PAD:]]
    for n in ("mix_pre_g", "mix_post_g", "mlp_pre_g", "mlp_post_g"):
        rows += grads[n]
    rows += [jnp.concatenate([grads["attn_out_g"][l], grads["conv_out_g"][l]], axis=1) for l in range(DEPTH)]
    rows.append(jnp.concatenate(grads["conv_w"], axis=1))
    rows.append(_pad_cols(jnp.concatenate(grads["sinks"])[None, :]))
    rows.append(_pad_cols(loss_tile[:1]))
    packed = jnp.concatenate(rows, axis=0)
    packed = jnp.pad(packed, ((0, SMALL_ROWS - packed.shape[0]), (0, 0)))
    device = 2 * chip + lax.axis_index("c")
    small_parts = _Exchange("gather_small", [lax.dynamic_update_slice(lax.empty((N_DEV,) + packed.shape, F32),
                                                                      packed[None], (device, 0, 0))], _all_plan, N_DEV - 1)
    g0["w_in"] = _in_grad(dproj, saved0, [small_parts.token])
    red0b = _Reduce("reduce0b", [g0["w_in"]])
    done1 = adamw(1, dict(zip(_LARGE, red1.done(red0b.token))), None)
    total = _sum_devices(small_parts.wait(*[done1[n][0] for n in _LARGE])[0])
    r0 = N_META
    small = {
        "meta_tokens": lax.dynamic_slice(total[:N_META], (0, chip * (D_MODEL // N_CHIPS)), (N_META, D_MODEL // N_CHIPS)),
        "mix_pre_g": total[r0:r0 + 2], "mix_post_g": total[r0 + 2:r0 + 4], "mlp_pre_g": total[r0 + 4:r0 + 6],
        "mlp_post_g": total[r0 + 6:r0 + 8],
        "attn_out_g": total[r0 + 8:r0 + 10, :ATTN_W], "conv_out_g": total[r0 + 8:r0 + 10, ATTN_W:],
        "conv_w": lax.dynamic_slice(total[r0 + 10:r0 + 13].reshape(CONV_K, DEPTH, CONV_W).transpose(1, 0, 2),
                                    (0, 0, chip * (CONV_W // N_CHIPS)), (DEPTH, CONV_K, CONV_W // N_CHIPS)),
        "sinks": total[r0 + 13, :DEPTH * N_Q_HEADS].reshape(DEPTH, N_Q_HEADS),
    }
    loss = total[r0 + 14, 0]

    ds, nms, nvs = _adamw_small([w[n] for n in _SMALL], [small[n] for n in _SMALL], [m[n] for n in _SMALL],
                                [v[n] for n in _SMALL])
    red0a.join(ds[0], grad_x)
    red0b.join(red0a.token)
    done0 = adamw(0, dict(zip(first0, red0a.done(red0b.token))), done1)
    done0.update(adamw(0, {"w_in": red0b.done(done0["w_down"][0])[0]}, done1))
    grad, delta, new_m, new_v = {}, {}, {}, {}
    for n in _LARGE:
        grad[n], delta[n], new_m[n], new_v[n] = done0[n]
    for d in (grad, delta, new_m, new_v):
        d["w_in"] = jnp.swapaxes(d["w_in"], 1, 2)
    for i, n in enumerate(_SMALL):
        grad[n], delta[n], new_m[n], new_v[n] = small[n], ds[i], nms[i], nvs[i]
    return (loss, grad_x, *[grad[n] for n in _ORDER], *[delta[n] for n in _ORDER], *[new_m[n] for n in _ORDER],
            *[new_v[n] for n in _ORDER])
```
